```python
import math
import jax, jax.numpy as jnp
from jax import lax
import numpy as np

D_MODEL = 1024
BATCH = 16
SEQ = 2048
DEPTH = 1

HEAD_DIM = 64
D_MIX = D_MODEL
A_GROUPS = 4
A_WIDTH = A_GROUPS * HEAD_DIM
B_HEADS = 12
B_WIDTH = B_HEADS * HEAD_DIM
CHUNK = 128
DILATED_CONFIGS = ((128, 1), (512, 4), (2048, 16))
NUM_BUCKETS = 32
MAX_DISTANCE = 2048
D_FF = 2816
CONV_WIDTH = 3
IN_COLS = 2 * A_WIDTH + 3 * B_WIDTH
NORM_EPS = 1e-6
NEG_INF = -1e30

kernel_name = "hybrid_gmlp_dilated_attn_convffn"


def rms_norm(x, g):
    xf = x.astype(jnp.float32)
    y = xf * lax.rsqrt(jnp.mean(xf * xf, axis=-1, keepdims=True) + NORM_EPS)
    return (y * g.astype(jnp.float32)).astype(x.dtype)


def t5_bucket(dist):
    max_exact = NUM_BUCKETS // 2
    d = jnp.maximum(dist, 1).astype(jnp.float32)
    large = max_exact + (jnp.log(d / max_exact) / math.log(MAX_DISTANCE / max_exact)
                         * (NUM_BUCKETS - max_exact))
    large = jnp.minimum(large.astype(jnp.int32), NUM_BUCKETS - 1)
    return jnp.where(dist < max_exact, dist, large)


def spatial_gating(u, v, ln_g, ln_b, w_s, b_s):
    B, T, G, hd = u.shape
    u = jax.nn.gelu(u)
    vf = jax.nn.gelu(v).astype(jnp.float32)
    mu = jnp.mean(vf, axis=-1, keepdims=True)
    var = jnp.mean(jnp.square(vf - mu), axis=-1, keepdims=True)
    vn = (vf - mu) * lax.rsqrt(var + NORM_EPS) * ln_g.astype(jnp.float32) + ln_b.astype(jnp.float32)
    vc = vn.reshape(B, T // CHUNK, CHUNK, G, hd)
    tril = jnp.tril(jnp.ones((CHUNK, CHUNK), jnp.float32))
    w = w_s.astype(jnp.float32) * tril[None]
    z = jnp.einsum('gts,bcsgd->bctgd', w, vc) + b_s.astype(jnp.float32).T[None, None, :, :, None]
    return u * z.reshape(B, T, G, hd).astype(u.dtype)


def dilated_segment(q, k, v, rel_bias, window, dil):
    B, T, H, D = q.shape
    nw = window // dil
    seg = nw * dil
    nb = -(-T // seg)
    Tp = nb * seg

    def blocks(a):
        a = jnp.pad(a, ((0, 0), (0, Tp - T), (0, 0), (0, 0)))
        return a.reshape(B, nb, nw, dil, H, D)

    def with_prev(a):
        prev = jnp.pad(a, ((0, 0), (1, 0), (0, 0), (0, 0), (0, 0), (0, 0)))[:, :-1]
        return jnp.concatenate([prev, a], axis=2)

    qb = blocks(q)
    kc = with_prev(blocks(k))
    vc = with_prev(blocks(v))

    i = jnp.arange(nw)[:, None]
    j = jnp.arange(2 * nw)[None, :]
    rel = nw + i - j
    band = (rel >= 0) & (rel <= nw)
    key_ok = (jnp.arange(nb)[:, None] * nw + jnp.arange(2 * nw)[None, :] - nw) >= 0
    mask = band[None] & key_ok[:, None, :]
    bias = rel_bias.astype(jnp.float32)[t5_bucket(jnp.maximum(rel, 0) * dil)]
    bias = bias.transpose(2, 0, 1)

    scale = 1.0 / math.sqrt(D)
    logits = jnp.einsum('bnirhd,bnjrhd->bnrhij', qb, kc) * scale + bias[None, None, None]
    logits = jnp.where(mask[None, :, None, None], logits, NEG_INF)
    lse = jax.nn.logsumexp(logits, axis=-1)
    p = jnp.exp(logits - lse[..., None])
    o = jnp.einsum('bnrhij,bnjrhd->bnirhd', p, vc)
    o = o.reshape(B, Tp, H, D)[:, :T]
    lse = lse.transpose(0, 1, 4, 2, 3).reshape(B, Tp, H)[:, :T]
    return o, lse


def dilated_attention(q, k, v, rel_bias):
    outs, lses = [], []
    for window, dil in DILATED_CONFIGS:
        o, lse = dilated_segment(q, k, v, rel_bias, window, dil)
        outs.append(o)
        lses.append(lse)
    o = jnp.stack(outs, axis=0)
    w = jax.nn.softmax(jnp.stack(lses, axis=0), axis=0)
    return jnp.sum(w[..., None] * o, axis=0)


def causal_dwconv(h, w, b):
    T = h.shape[1]
    hp = jnp.pad(h, ((0, 0), (CONV_WIDTH - 1, 0), (0, 0)))
    out = b
    for kk in range(CONV_WIDTH):
        out = out + hp[:, kk:kk + T] * w[kk]
    return out


def _fwd_setup_inputs(seed: int = 0) -> dict:
    key = jax.random.key(seed)
    ks = jax.random.split(key, 20)
    f32 = jnp.float32

    def nrm(k, shape, s):
        return jax.random.normal(k, shape, f32) * s

    L = DEPTH
    return {
        "x": jax.random.normal(ks[0], (BATCH, SEQ, D_MODEL), f32),
        "norm_mix_pre": 1.0 + nrm(ks[1], (L, D_MODEL), 0.01),
        "norm_mix_post": 1.0 + nrm(ks[2], (L, D_MODEL), 0.01),
        "norm_ffn_pre": 1.0 + nrm(ks[3], (L, D_MODEL), 0.01),
        "norm_ffn_post": 1.0 + nrm(ks[4], (L, D_MODEL), 0.01),
        "w_in": nrm(ks[5], (L, D_MODEL, IN_COLS), D_MODEL ** -0.5),
        "ln_v_gain": 1.0 + nrm(ks[6], (L, A_GROUPS, HEAD_DIM), 0.01),
        "ln_v_bias": nrm(ks[7], (L, A_GROUPS, HEAD_DIM), 0.01),
        "spatial_w": nrm(ks[8], (L, A_GROUPS, CHUNK, CHUNK), CHUNK ** -0.5),
        "spatial_b": 1.0 + nrm(ks[9], (L, A_GROUPS, CHUNK), 0.01),
        "rel_bias": nrm(ks[10], (NUM_BUCKETS, B_HEADS), 0.5),
        "w_out": nrm(ks[11], (L, D_MIX, D_MODEL), D_MIX ** -0.5),
        "w_gate": nrm(ks[12], (L, D_MODEL, D_FF), D_MODEL ** -0.5),
        "w_up": nrm(ks[13], (L, D_MODEL, D_FF), D_MODEL ** -0.5),
        "conv_w": nrm(ks[14], (L, CONV_WIDTH, D_FF), CONV_WIDTH ** -0.5),
        "conv_b": nrm(ks[15], (L, D_FF), 0.01),
        "w_down": nrm(ks[16], (L, D_FF, D_MODEL), D_FF ** -0.5),
    }


def _fwd_reference(x, norm_mix_pre, norm_mix_post, norm_ffn_pre, norm_ffn_post, w_in,
              ln_v_gain, ln_v_bias, spatial_w, spatial_b, rel_bias, w_out,
              w_gate, w_up, conv_w, conv_b, w_down):
    B, T, _ = x.shape
    for l in range(DEPTH):
        h = rms_norm(x, norm_mix_pre[l])
        proj = h @ w_in[l]
        o0 = 0
        ua = proj[..., o0:o0 + A_WIDTH].reshape(B, T, A_GROUPS, HEAD_DIM); o0 += A_WIDTH
        va = proj[..., o0:o0 + A_WIDTH].reshape(B, T, A_GROUPS, HEAD_DIM); o0 += A_WIDTH
        qb = proj[..., o0:o0 + B_WIDTH].reshape(B, T, B_HEADS, HEAD_DIM); o0 += B_WIDTH
        kb = proj[..., o0:o0 + B_WIDTH].reshape(B, T, B_HEADS, HEAD_DIM); o0 += B_WIDTH
        vb = proj[..., o0:o0 + B_WIDTH].reshape(B, T, B_HEADS, HEAD_DIM)

        a_out = spatial_gating(ua, va, ln_v_gain[l], ln_v_bias[l], spatial_w[l], spatial_b[l])
        b_out = dilated_attention(qb.astype(jnp.float32), kb.astype(jnp.float32),
                                  vb.astype(jnp.float32), rel_bias)
        mix = jnp.concatenate([a_out.reshape(B, T, A_WIDTH),
                               b_out.reshape(B, T, B_WIDTH).astype(x.dtype)], axis=-1)
        x = x + rms_norm(mix @ w_out[l], norm_mix_post[l])

        h = rms_norm(x, norm_ffn_pre[l])
        g = jax.nn.gelu(causal_dwconv(h @ w_gate[l], conv_w[l], conv_b[l]))
        y = (g * (h @ w_up[l])) @ w_down[l]
        x = x + rms_norm(y, norm_ffn_post[l])
    return x


import jax as _jax
import jax.numpy as _jnp

TWIN_FORMAT = 'train_step'
FWD_PARAMS = ['x', 'norm_mix_pre', 'norm_mix_post', 'norm_ffn_pre', 'norm_ffn_post', 'w_in', 'ln_v_gain', 'ln_v_bias', 'spatial_w', 'spatial_b', 'rel_bias', 'w_out', 'w_gate', 'w_up', 'conv_w', 'conv_b', 'w_down']
TWIN_WEIGHTS = ['norm_mix_pre', 'norm_mix_post', 'norm_ffn_pre', 'norm_ffn_post', 'w_in', 'ln_v_gain', 'ln_v_bias', 'spatial_w', 'spatial_b', 'rel_bias', 'w_out', 'w_gate', 'w_up', 'conv_w', 'conv_b', 'w_down']
TWIN_DIFF_INPUT = 'x'
TWIN_INPUTS = ['x', 'norm_mix_pre', 'norm_mix_post', 'norm_ffn_pre', 'norm_ffn_post', 'w_in', 'ln_v_gain', 'ln_v_bias', 'spatial_w', 'spatial_b', 'rel_bias', 'w_out', 'w_gate', 'w_up', 'conv_w', 'conv_b', 'w_down', 'loss_target', 'm_norm_mix_pre', 'm_norm_mix_post', 'm_norm_ffn_pre', 'm_norm_ffn_post', 'm_w_in', 'm_ln_v_gain', 'm_ln_v_bias', 'm_spatial_w', 'm_spatial_b', 'm_rel_bias', 'm_w_out', 'm_w_gate', 'm_w_up', 'm_conv_w', 'm_conv_b', 'm_w_down', 'v_norm_mix_pre', 'v_norm_mix_post', 'v_norm_ffn_pre', 'v_norm_ffn_post', 'v_w_in', 'v_ln_v_gain', 'v_ln_v_bias', 'v_spatial_w', 'v_spatial_b', 'v_rel_bias', 'v_w_out', 'v_w_gate', 'v_w_up', 'v_conv_w', 'v_conv_b', 'v_w_down']
TWIN_OUTPUTS = ['loss', 'grad_x', 'grad_norm_mix_pre', 'grad_norm_mix_post', 'grad_norm_ffn_pre', 'grad_norm_ffn_post', 'grad_w_in', 'grad_ln_v_gain', 'grad_ln_v_bias', 'grad_spatial_w', 'grad_spatial_b', 'grad_rel_bias', 'grad_w_out', 'grad_w_gate', 'grad_w_up', 'grad_conv_w', 'grad_conv_b', 'grad_w_down', 'delta_norm_mix_pre', 'delta_norm_mix_post', 'delta_norm_ffn_pre', 'delta_norm_ffn_post', 'delta_w_in', 'delta_ln_v_gain', 'delta_ln_v_bias', 'delta_spatial_w', 'delta_spatial_b', 'delta_rel_bias', 'delta_w_out', 'delta_w_gate', 'delta_w_up', 'delta_conv_w', 'delta_conv_b', 'delta_w_down', 'new_m_norm_mix_pre', 'new_m_norm_mix_post', 'new_m_norm_ffn_pre', 'new_m_norm_ffn_post', 'new_m_w_in', 'new_m_ln_v_gain', 'new_m_ln_v_bias', 'new_m_spatial_w', 'new_m_spatial_b', 'new_m_rel_bias', 'new_m_w_out', 'new_m_w_gate', 'new_m_w_up', 'new_m_conv_w', 'new_m_conv_b', 'new_m_w_down', 'new_v_norm_mix_pre', 'new_v_norm_mix_post', 'new_v_norm_ffn_pre', 'new_v_norm_ffn_post', 'new_v_w_in', 'new_v_ln_v_gain', 'new_v_ln_v_bias', 'new_v_spatial_w', 'new_v_spatial_b', 'new_v_rel_bias', 'new_v_w_out', 'new_v_w_gate', 'new_v_w_up', 'new_v_conv_w', 'new_v_conv_b', 'new_v_w_down']
TWIN_LEAF_KINDS = {'loss': 'loss', 'grad_x': 'grad_x', 'grad_norm_mix_pre': 'grad_w', 'grad_norm_mix_post': 'grad_w', 'grad_norm_ffn_pre': 'grad_w', 'grad_norm_ffn_post': 'grad_w', 'grad_w_in': 'grad_w', 'grad_ln_v_gain': 'grad_w', 'grad_ln_v_bias': 'grad_w', 'grad_spatial_w': 'grad_w', 'grad_spatial_b': 'grad_w', 'grad_rel_bias': 'grad_w', 'grad_w_out': 'grad_w', 'grad_w_gate': 'grad_w', 'grad_w_up': 'grad_w', 'grad_conv_w': 'grad_w', 'grad_conv_b': 'grad_w', 'grad_w_down': 'grad_w', 'delta_norm_mix_pre': 'delta_w', 'delta_norm_mix_post': 'delta_w', 'delta_norm_ffn_pre': 'delta_w', 'delta_norm_ffn_post': 'delta_w', 'delta_w_in': 'delta_w', 'delta_ln_v_gain': 'delta_w', 'delta_ln_v_bias': 'delta_w', 'delta_spatial_w': 'delta_w', 'delta_spatial_b': 'delta_w', 'delta_rel_bias': 'delta_w', 'delta_w_out': 'delta_w', 'delta_w_gate': 'delta_w', 'delta_w_up': 'delta_w', 'delta_conv_w': 'delta_w', 'delta_conv_b': 'delta_w', 'delta_w_down': 'delta_w', 'new_m_norm_mix_pre': 'new_m', 'new_m_norm_mix_post': 'new_m', 'new_m_norm_ffn_pre': 'new_m', 'new_m_norm_ffn_post': 'new_m', 'new_m_w_in': 'new_m', 'new_m_ln_v_gain': 'new_m', 'new_m_ln_v_bias': 'new_m', 'new_m_spatial_w': 'new_m', 'new_m_spatial_b': 'new_m', 'new_m_rel_bias': 'new_m', 'new_m_w_out': 'new_m', 'new_m_w_gate': 'new_m', 'new_m_w_up': 'new_m', 'new_m_conv_w': 'new_m', 'new_m_conv_b': 'new_m', 'new_m_w_down': 'new_m', 'new_v_norm_mix_pre': 'new_v', 'new_v_norm_mix_post': 'new_v', 'new_v_norm_ffn_pre': 'new_v', 'new_v_norm_ffn_post': 'new_v', 'new_v_w_in': 'new_v', 'new_v_ln_v_gain': 'new_v', 'new_v_ln_v_bias': 'new_v', 'new_v_spatial_w': 'new_v', 'new_v_spatial_b': 'new_v', 'new_v_rel_bias': 'new_v', 'new_v_w_out': 'new_v', 'new_v_w_gate': 'new_v', 'new_v_w_up': 'new_v', 'new_v_conv_w': 'new_v', 'new_v_conv_b': 'new_v', 'new_v_w_down': 'new_v'}


def _forward(args):
    return _fwd_reference(*[args[k] for k in FWD_PARAMS])


def _output_shape():
    out = _jax.eval_shape(lambda: _forward(_fwd_setup_inputs(0)))
    return out.shape, out.dtype

N_MICROBATCH = 1
ADAM_LR = 0.001
ADAM_B1 = 0.9
ADAM_B2 = 0.999
ADAM_EPS = 1e-08
ADAM_WD = 0.01
ADAM_STEP = 10
PER_EXAMPLE_BATCH_AXIS = {'x': 0, 'loss_target': 0}
SHARED_INPUTS = []
_WEIGHT_DTYPES = {'norm_mix_pre': _jnp.float32, 'norm_mix_post': _jnp.float32, 'norm_ffn_pre': _jnp.float32, 'norm_ffn_post': _jnp.float32, 'w_in': _jnp.float32, 'ln_v_gain': _jnp.float32, 'ln_v_bias': _jnp.float32, 'spatial_w': _jnp.float32, 'spatial_b': _jnp.float32, 'rel_bias': _jnp.float32, 'w_out': _jnp.float32, 'w_gate': _jnp.float32, 'w_up': _jnp.float32, 'conv_w': _jnp.float32, 'conv_b': _jnp.float32, 'w_down': _jnp.float32}
MOMENT_SCALE = {'norm_mix_pre': 6.760916e-01, 'norm_mix_post': 3.242591e+01, 'norm_ffn_pre': 1.313264e+00, 'norm_ffn_post': 3.192304e+01, 'w_in': 4.049713e-01, 'ln_v_gain': 6.882329e-01, 'ln_v_bias': 5.504812e-01, 'spatial_w': 3.512888e-01, 'spatial_b': 5.406960e-01, 'rel_bias': 2.526240e-01, 'w_out': 1.713766e+00, 'w_gate': 2.991223e-01, 'w_up': 7.651509e-01, 'conv_w': 6.130878e-01, 'conv_b': 1.417195e+00, 'w_down': 1.255218e+00}


def _to_microbatches(a, axis):
    t = _jnp.moveaxis(a, axis, 0)
    t = t.reshape((N_MICROBATCH, t.shape[0] // N_MICROBATCH) + t.shape[1:])
    return _jnp.moveaxis(t, 1, axis + 1)


def setup_inputs(seed: int = 0) -> dict:
    inp = _fwd_setup_inputs(seed)
    key = _jax.random.fold_in(_jax.random.key(seed), 7919)
    shape, _ = _output_shape()
    out = dict(inp)
    out["loss_target"] = _jax.random.normal(_jax.random.fold_in(key, 0), shape, _jnp.float32)
    for i, name in enumerate(TWIN_WEIGHTS):
        w = inp[name].astype(_jnp.float32)
        if MOMENT_SCALE is None:
            s = _jnp.sqrt(_jnp.mean(_jnp.square(w)) + 1e-30)
        else:
            s = MOMENT_SCALE[name]
        km, kv = _jax.random.split(_jax.random.fold_in(key, i + 1))
        out[name] = w
        out["m_" + name] = s * _jax.random.normal(km, w.shape, _jnp.float32)
        out["v_" + name] = (s * s) * _jax.random.uniform(kv, w.shape, _jnp.float32, 0.5, 1.5)
    if N_MICROBATCH > 1:
        for name, axis in PER_EXAMPLE_BATCH_AXIS.items():
            out[name] = _to_microbatches(out[name], axis)
    return {'x': out['x'], 'norm_mix_pre': out['norm_mix_pre'], 'norm_mix_post': out['norm_mix_post'], 'norm_ffn_pre': out['norm_ffn_pre'], 'norm_ffn_post': out['norm_ffn_post'], 'w_in': out['w_in'], 'ln_v_gain': out['ln_v_gain'], 'ln_v_bias': out['ln_v_bias'], 'spatial_w': out['spatial_w'], 'spatial_b': out['spatial_b'], 'rel_bias': out['rel_bias'], 'w_out': out['w_out'], 'w_gate': out['w_gate'], 'w_up': out['w_up'], 'conv_w': out['conv_w'], 'conv_b': out['conv_b'], 'w_down': out['w_down'], 'loss_target': out['loss_target'], 'm_norm_mix_pre': out['m_norm_mix_pre'], 'm_norm_mix_post': out['m_norm_mix_post'], 'm_norm_ffn_pre': out['m_norm_ffn_pre'], 'm_norm_ffn_post': out['m_norm_ffn_post'], 'm_w_in': out['m_w_in'], 'm_ln_v_gain': out['m_ln_v_gain'], 'm_ln_v_bias': out['m_ln_v_bias'], 'm_spatial_w': out['m_spatial_w'], 'm_spatial_b': out['m_spatial_b'], 'm_rel_bias': out['m_rel_bias'], 'm_w_out': out['m_w_out'], 'm_w_gate': out['m_w_gate'], 'm_w_up': out['m_w_up'], 'm_conv_w': out['m_conv_w'], 'm_conv_b': out['m_conv_b'], 'm_w_down': out['m_w_down'], 'v_norm_mix_pre': out['v_norm_mix_pre'], 'v_norm_mix_post': out['v_norm_mix_post'], 'v_norm_ffn_pre': out['v_norm_ffn_pre'], 'v_norm_ffn_post': out['v_norm_ffn_post'], 'v_w_in': out['v_w_in'], 'v_ln_v_gain': out['v_ln_v_gain'], 'v_ln_v_bias': out['v_ln_v_bias'], 'v_spatial_w': out['v_spatial_w'], 'v_spatial_b': out['v_spatial_b'], 'v_rel_bias': out['v_rel_bias'], 'v_w_out': out['v_w_out'], 'v_w_gate': out['v_w_gate'], 'v_w_up': out['v_w_up'], 'v_conv_w': out['v_conv_w'], 'v_conv_b': out['v_conv_b'], 'v_w_down': out['v_w_down']}


def _loss(weights, diff, rest, loss_target):
    with _jax.named_scope("forward"):
        args = {**rest, TWIN_DIFF_INPUT: diff, **{k: w.astype(_WEIGHT_DTYPES[k]) for k, w in weights.items()}}
        y = _forward(args)
    with _jax.named_scope("loss_head"):
        err = _jnp.square(y.astype(_jnp.float32) - loss_target)
        return 0.5 * _jnp.sum(_jnp.mean(err, axis=-1)) if err.ndim else 0.5 * err


def _adamw(w, g, m, v):
    m = ADAM_B1 * m + (1.0 - ADAM_B1) * g
    v = ADAM_B2 * v + (1.0 - ADAM_B2) * _jnp.square(g)
    m_hat = m / (1.0 - ADAM_B1 ** ADAM_STEP)
    v_hat = v / (1.0 - ADAM_B2 ** ADAM_STEP)
    delta = -ADAM_LR * (m_hat / (_jnp.sqrt(v_hat) + ADAM_EPS) + ADAM_WD * w)
    return delta, m, v


def reference(x, norm_mix_pre, norm_mix_post, norm_ffn_pre, norm_ffn_post, w_in, ln_v_gain, ln_v_bias, spatial_w, spatial_b, rel_bias, w_out, w_gate, w_up, conv_w, conv_b, w_down, loss_target, m_norm_mix_pre, m_norm_mix_post, m_norm_ffn_pre, m_norm_ffn_post, m_w_in, m_ln_v_gain, m_ln_v_bias, m_spatial_w, m_spatial_b, m_rel_bias, m_w_out, m_w_gate, m_w_up, m_conv_w, m_conv_b, m_w_down, v_norm_mix_pre, v_norm_mix_post, v_norm_ffn_pre, v_norm_ffn_post, v_w_in, v_ln_v_gain, v_ln_v_bias, v_spatial_w, v_spatial_b, v_rel_bias, v_w_out, v_w_gate, v_w_up, v_conv_w, v_conv_b, v_w_down):
    given = dict(x=x, norm_mix_pre=norm_mix_pre, norm_mix_post=norm_mix_post, norm_ffn_pre=norm_ffn_pre, norm_ffn_post=norm_ffn_post, w_in=w_in, ln_v_gain=ln_v_gain, ln_v_bias=ln_v_bias, spatial_w=spatial_w, spatial_b=spatial_b, rel_bias=rel_bias, w_out=w_out, w_gate=w_gate, w_up=w_up, conv_w=conv_w, conv_b=conv_b, w_down=w_down, loss_target=loss_target, m_norm_mix_pre=m_norm_mix_pre, m_norm_mix_post=m_norm_mix_post, m_norm_ffn_pre=m_norm_ffn_pre, m_norm_ffn_post=m_norm_ffn_post, m_w_in=m_w_in, m_ln_v_gain=m_ln_v_gain, m_ln_v_bias=m_ln_v_bias, m_spatial_w=m_spatial_w, m_spatial_b=m_spatial_b, m_rel_bias=m_rel_bias, m_w_out=m_w_out, m_w_gate=m_w_gate, m_w_up=m_w_up, m_conv_w=m_conv_w, m_conv_b=m_conv_b, m_w_down=m_w_down, v_norm_mix_pre=v_norm_mix_pre, v_norm_mix_post=v_norm_mix_post, v_norm_ffn_pre=v_norm_ffn_pre, v_norm_ffn_post=v_norm_ffn_post, v_w_in=v_w_in, v_ln_v_gain=v_ln_v_gain, v_ln_v_bias=v_ln_v_bias, v_spatial_w=v_spatial_w, v_spatial_b=v_spatial_b, v_rel_bias=v_rel_bias, v_w_out=v_w_out, v_w_gate=v_w_gate, v_w_up=v_w_up, v_conv_w=v_conv_w, v_conv_b=v_conv_b, v_w_down=v_w_down)
    weights = {n: given[n] for n in TWIN_WEIGHTS}
    shared = {n: given[n] for n in SHARED_INPUTS}
    per_example = {n: given[n] for n in ['x']}
    grad_fn = _jax.value_and_grad(_loss, argnums=(0, 1))

    def one_microbatch(ex, loss_target):
        ex = dict(ex)
        diff = ex.pop(TWIN_DIFF_INPUT)
        return grad_fn(weights, diff, {**shared, **ex}, loss_target)

    if N_MICROBATCH == 1:
        loss, (grad_w, grad_x) = one_microbatch(per_example, given["loss_target"])
    else:
        def body(carry, xs):
            loss_sum, grad_sum = carry
            l_k, (gw_k, gx_k) = one_microbatch(xs[0], xs[1])
            with _jax.named_scope("update"):
                return (loss_sum + l_k, _jax.tree.map(_jnp.add, grad_sum, gw_k)), gx_k

        init = (_jnp.zeros((), _jnp.float32), _jax.tree.map(_jnp.zeros_like, weights))
        (loss, grad_w), grad_x = _jax.lax.scan(body, init, (per_example, given["loss_target"]))
    with _jax.named_scope("update"):
        delta_w, new_m, new_v = {}, {}, {}
        for n in TWIN_WEIGHTS:
            delta_w[n], new_m[n], new_v[n] = _adamw(weights[n], grad_w[n], given["m_" + n], given["v_" + n])
    return (loss, grad_x, *[grad_w[n] for n in TWIN_WEIGHTS], *[delta_w[n] for n in TWIN_WEIGHTS],
            *[new_m[n] for n in TWIN_WEIGHTS], *[new_v[n] for n in TWIN_WEIGHTS])
```

```python
import functools
import math

import numpy as np
import jax
import jax.numpy as jnp
from jax import lax
from jax.experimental import pallas as pl
from jax.experimental.pallas import tpu as pltpu

F32 = jnp.float32
BF16 = jnp.bfloat16
MESH = pl.DeviceIdType.MESH

D_MODEL = 1024
SEQ = 2048
HEAD_DIM = 64
A_GROUPS = 4
A_WIDTH = 256
B_HEADS = 12
B_WIDTH = 768
CHUNK = 128
DILATED = ((128, 1), (512, 4), (2048, 16))
NUM_BUCKETS = 32
MAX_DISTANCE = 2048
D_FF = 2816
IN_COLS = 2816
NORM_EPS = 1e-6
NEG_INF = -1e30
N_SHARD = 4
SHARD_FF = D_FF // N_SHARD
LANE_BLOCK = 256
VMEM_LIMIT = 56 * 1024 * 1024

ADAM_LR = 0.001
ADAM_B1 = 0.9
ADAM_B2 = 0.999
ADAM_EPS = 1e-08
ADAM_WD = 0.01
ADAM_STEP = 10

GELU_C = math.sqrt(2.0 / math.pi)
GELU_A = 0.044715


def _params(sem=None):
    return pltpu.CompilerParams(dimension_semantics=sem, vmem_limit_bytes=VMEM_LIMIT)


def _dot(a, b, precision=None):
    return jnp.dot(a, b, preferred_element_type=F32, precision=precision)


def _dot_nt(a, b):
    return lax.dot_general(a, b, (((1,), (1,)), ((), ())), preferred_element_type=F32)


def _dot_tn(a, b):
    return lax.dot_general(a, b, (((0,), (0,)), ((), ())), preferred_element_type=F32)


def _gelu(x):
    t = jnp.tanh(GELU_C * (x + GELU_A * (x * x * x)))
    return 0.5 * x * (1.0 + t)


def _gelu_and_grad(x):
    x2 = x * x
    t = jnp.tanh(GELU_C * (x + GELU_A * (x2 * x)))
    g = 0.5 * x * (1.0 + t)
    dg = 0.5 * (1.0 + t) + 0.5 * x * (1.0 - t * t) * (GELU_C * (1.0 + 3.0 * GELU_A * x2))
    return g, dg


def _mm(a, b, *, dims, tm, tn, tk, out_dtype, name, add=None):
    if dims == "nn":
        m, k = a.shape
        n = b.shape[1]
        a_spec = pl.BlockSpec((tm, tk), lambda i, j, kk: (i, kk))
        b_spec = pl.BlockSpec((tk, tn), lambda i, j, kk: (kk, j))
        dot = _dot
    elif dims == "nt":
        m, k = a.shape
        n = b.shape[0]
        a_spec = pl.BlockSpec((tm, tk), lambda i, j, kk: (i, kk))
        b_spec = pl.BlockSpec((tn, tk), lambda i, j, kk: (j, kk))
        dot = _dot_nt
    else:
        k, m = a.shape
        n = b.shape[1]
        a_spec = pl.BlockSpec((tk, tm), lambda i, j, kk: (kk, i))
        b_spec = pl.BlockSpec((tk, tn), lambda i, j, kk: (kk, j))
        dot = _dot_tn
    assert m % tm == 0 and n % tn == 0 and k % tk == 0, (name, m, n, k)
    nk = k // tk
    has_add = add is not None

    def body(*refs):
        if has_add:
            a_ref, b_ref, add_ref, o_ref, acc_ref = refs
        else:
            a_ref, b_ref, o_ref, acc_ref = refs
        kk = pl.program_id(2)

        @pl.when(kk == 0)
        def _():
            if has_add:
                acc_ref[...] = add_ref[...].astype(F32)
            else:
                acc_ref[...] = jnp.zeros_like(acc_ref)

        acc_ref[...] += dot(a_ref[...].astype(BF16), b_ref[...].astype(BF16))

        @pl.when(kk == nk - 1)
        def _():
            o_ref[...] = acc_ref[...].astype(out_dtype)

    in_specs = [a_spec, b_spec]
    args = [a, b]
    if has_add:
        in_specs.append(pl.BlockSpec((tm, tn), lambda i, j, kk: (i, j)))
        args.append(add)
    return pl.pallas_call(
        body,
        grid=(m // tm, n // tn, nk),
        in_specs=in_specs,
        out_specs=pl.BlockSpec((tm, tn), lambda i, j, kk: (i, j)),
        out_shape=jax.ShapeDtypeStruct((m, n), out_dtype),
        scratch_shapes=[pltpu.VMEM((tm, tn), F32)],
        compiler_params=_params(("parallel", "parallel", "arbitrary")),
        name=name,
    )(*args)


ROW_TILE = 512


def _row_spec(width=D_MODEL):
    return pl.BlockSpec((ROW_TILE, width), lambda i: (i, 0))


def _vec_spec(width=D_MODEL):
    return pl.BlockSpec((1, width), lambda i: (0, 0))


def _rstd(v):
    return lax.rsqrt(jnp.mean(v * v, axis=-1, keepdims=True) + NORM_EPS)


def _rms_fwd(x, g):
    m = x.shape[0]

    def body(x_ref, g_ref, h_ref):
        xv = x_ref[...]
        h_ref[...] = (xv * _rstd(xv) * g_ref[...]).astype(BF16)

    return pl.pallas_call(
        body, grid=(m // ROW_TILE,),
        in_specs=[_row_spec(), _vec_spec()],
        out_specs=_row_spec(),
        out_shape=jax.ShapeDtypeStruct((m, D_MODEL), BF16),
        compiler_params=_params(("parallel",)), name="rms_fwd",
    )(x, g)


def _mid_fwd(x0, y1, g2, g3):
    m = x0.shape[0]

    def body(x0_ref, y1_ref, g2_ref, g3_ref, x1_ref, h2_ref):
        y1v = y1_ref[...]
        x1 = x0_ref[...] + y1v * _rstd(y1v) * g2_ref[...]
        x1_ref[...] = x1
        h2_ref[...] = (x1 * _rstd(x1) * g3_ref[...]).astype(BF16)

    return pl.pallas_call(
        body, grid=(m // ROW_TILE,),
        in_specs=[_row_spec(), _row_spec(), _vec_spec(), _vec_spec()],
        out_specs=[_row_spec(), _row_spec()],
        out_shape=[jax.ShapeDtypeStruct((m, D_MODEL), F32), jax.ShapeDtypeStruct((m, D_MODEL), BF16)],
        compiler_params=_params(("parallel",)), name="mid_fwd",
    )(x0, y1, g2, g3)


def _rms_bwd_rows(dout, v, g):
    r = _rstd(v)
    n = v * r
    dn = dout * g
    dv = r * (dn - n * jnp.mean(dn * n, axis=-1, keepdims=True))
    dg = jnp.sum(dout * n, axis=0, keepdims=True)
    return dv, dg


def _loss_head(x1, y2, tgt, g4):
    m = x1.shape[0]

    def body(x1_ref, y2_ref, t_ref, g4_ref, dx2_ref, dy2_ref, dg4_ref, loss_ref):
        i = pl.program_id(0)

        @pl.when(i == 0)
        def _():
            dg4_ref[...] = jnp.zeros_like(dg4_ref)
            loss_ref[...] = jnp.zeros_like(loss_ref)

        y2v = y2_ref[...]
        g4 = g4_ref[...]
        x2 = x1_ref[...] + y2v * _rstd(y2v) * g4
        err = x2 - t_ref[...]
        loss_ref[...] += 0.5 * jnp.sum(jnp.mean(err * err, axis=-1, keepdims=True), axis=0, keepdims=True)
        dx2 = err * (1.0 / D_MODEL)
        dx2_ref[...] = dx2
        dy2, dg4 = _rms_bwd_rows(dx2, y2v, g4)
        dy2_ref[...] = dy2.astype(BF16)
        dg4_ref[...] += dg4

    return pl.pallas_call(
        body, grid=(m // ROW_TILE,),
        in_specs=[_row_spec(), _row_spec(), _row_spec(), _vec_spec()],
        out_specs=[_row_spec(), _row_spec(), _vec_spec(), pl.BlockSpec((1, 1), lambda i: (0, 0))],
        out_shape=[jax.ShapeDtypeStruct((m, D_MODEL), F32), jax.ShapeDtypeStruct((m, D_MODEL), BF16),
                   jax.ShapeDtypeStruct((1, D_MODEL), F32), jax.ShapeDtypeStruct((1, 1), F32)],
        compiler_params=_params(("arbitrary",)), name="loss_head",
    )(x1, y2, tgt, g4)


def _mid_bwd(x1, y1, dh2, dx2, g2, g3):
    m = x1.shape[0]

    def body(x1_ref, y1_ref, dh2_ref, dx2_ref, g2_ref, g3_ref, dx1_ref, dy1_ref, dg2_ref, dg3_ref):
        i = pl.program_id(0)

        @pl.when(i == 0)
        def _():
            dg2_ref[...] = jnp.zeros_like(dg2_ref)
            dg3_ref[...] = jnp.zeros_like(dg3_ref)

        d3, dg3 = _rms_bwd_rows(dh2_ref[...], x1_ref[...], g3_ref[...])
        dx1 = dx2_ref[...] + d3
        dx1_ref[...] = dx1
        dy1, dg2 = _rms_bwd_rows(dx1, y1_ref[...], g2_ref[...])
        dy1_ref[...] = dy1.astype(BF16)
        dg2_ref[...] += dg2
        dg3_ref[...] += dg3

    return pl.pallas_call(
        body, grid=(m // ROW_TILE,),
        in_specs=[_row_spec(), _row_spec(), _row_spec(), _row_spec(), _vec_spec(), _vec_spec()],
        out_specs=[_row_spec(), _row_spec(), _vec_spec(), _vec_spec()],
        out_shape=[jax.ShapeDtypeStruct((m, D_MODEL), F32), jax.ShapeDtypeStruct((m, D_MODEL), BF16),
                   jax.ShapeDtypeStruct((1, D_MODEL), F32), jax.ShapeDtypeStruct((1, D_MODEL), F32)],
        compiler_params=_params(("arbitrary",)), name="mid_bwd",
    )(x1, y1, dh2, dx2, g2, g3)


def _in_bwd(x0, dh1, dx1, g1):
    m = x0.shape[0]

    def body(x0_ref, dh1_ref, dx1_ref, g1_ref, dx0_ref, dg1_ref):
        i = pl.program_id(0)

        @pl.when(i == 0)
        def _():
            dg1_ref[...] = jnp.zeros_like(dg1_ref)

        d1, dg1 = _rms_bwd_rows(dh1_ref[...], x0_ref[...], g1_ref[...])
        dx0_ref[...] = dx1_ref[...] + d1
        dg1_ref[...] += dg1

    return pl.pallas_call(
        body, grid=(m // ROW_TILE,),
        in_specs=[_row_spec(), _row_spec(), _row_spec(), _vec_spec()],
        out_specs=[_row_spec(), _vec_spec()],
        out_shape=[jax.ShapeDtypeStruct((m, D_MODEL), F32), jax.ShapeDtypeStruct((1, D_MODEL), F32)],
        compiler_params=_params(("arbitrary",)), name="in_bwd",
    )(x0, dh1, dx1, g1)


GATE_ROWS = 512


def _group_mean_matrix():
    p = np.zeros((A_WIDTH, A_WIDTH), np.float32)
    for g in range(A_GROUPS):
        p[g * HEAD_DIM:(g + 1) * HEAD_DIM, g * HEAD_DIM:(g + 1) * HEAD_DIM] = 1.0 / HEAD_DIM
    return jnp.asarray(p)


def _group_masks(width=A_WIDTH):
    lane = lax.broadcasted_iota(jnp.int32, (1, width), 1)
    return [(lane >= g * HEAD_DIM) & (lane < (g + 1) * HEAD_DIM) for g in range(width // HEAD_DIM)]


def _layernorm_groups(vg, pavg):
    hi = lax.Precision.HIGHEST
    mu = _dot(vg, pavg, hi)
    xc = vg - mu
    var = _dot(xc * xc, pavg, hi)
    rstd = lax.rsqrt(var + NORM_EPS)
    return xc * rstd, rstd


def _spatial_mix(w_bf, vn_chunk_bf, masks, bz):
    z = bz
    for g in range(A_GROUPS):
        z = z + jnp.where(masks[g], _dot(w_bf[g], vn_chunk_bf), 0.0)
    return z


def _gate_fwd(proj, ln_g, ln_b, w_s, bz):
    m = proj.shape[0]
    pavg = _group_mean_matrix()

    def body(u_ref, v_ref, lg_ref, lb_ref, w_ref, bz_ref, p_ref, a_ref):
        masks = _group_masks()
        row = lax.broadcasted_iota(jnp.int32, (CHUNK, CHUNK), 0)
        col = lax.broadcasted_iota(jnp.int32, (CHUNK, CHUNK), 1)
        w_bf = [jnp.where(row >= col, w_ref[g], 0.0).astype(BF16) for g in range(A_GROUPS)]
        ug = _gelu(u_ref[...])
        vhat, _ = _layernorm_groups(_gelu(v_ref[...]), p_ref[...])
        vn = vhat * lg_ref[...] + lb_ref[...]
        bz = bz_ref[...]
        for c in range(GATE_ROWS // CHUNK):
            sl = slice(c * CHUNK, (c + 1) * CHUNK)
            z = _spatial_mix(w_bf, vn[sl].astype(BF16), masks, bz)
            a_ref[sl, :] = (ug[sl] * z).astype(BF16)

    full = lambda shape: pl.BlockSpec(shape, lambda i: tuple(0 for _ in shape))
    return pl.pallas_call(
        body, grid=(m // GATE_ROWS,),
        in_specs=[pl.BlockSpec((GATE_ROWS, A_WIDTH), lambda i: (i, 0)),
                  pl.BlockSpec((GATE_ROWS, A_WIDTH), lambda i: (i, 1)),
                  full((1, A_WIDTH)), full((1, A_WIDTH)), full((A_GROUPS, CHUNK, CHUNK)),
                  full((CHUNK, A_WIDTH)), full((A_WIDTH, A_WIDTH))],
        out_specs=pl.BlockSpec((GATE_ROWS, A_WIDTH), lambda i: (i, 0)),
        out_shape=jax.ShapeDtypeStruct((m, A_WIDTH), BF16),
        compiler_params=_params(("parallel",)), name="gate_fwd",
    )(proj, proj, ln_g, ln_b, w_s, bz, pavg)


def _gate_bwd(proj, dmix, ln_g, ln_b, w_s, w_st, bz):
    m = proj.shape[0]
    pavg = _group_mean_matrix()
    nsteps = m // GATE_ROWS

    def body(u_ref, v_ref, da_ref, lg_ref, lb_ref, w_ref, wt_ref, bz_ref, p_ref,
             duv_ref, dlg_ref, dlb_ref, dw_ref, dbz_ref):
        i = pl.program_id(0)

        @pl.when(i == 0)
        def _():
            dlg_ref[...] = jnp.zeros_like(dlg_ref)
            dlb_ref[...] = jnp.zeros_like(dlb_ref)
            dw_ref[...] = jnp.zeros_like(dw_ref)
            dbz_ref[...] = jnp.zeros_like(dbz_ref)

        hi = lax.Precision.HIGHEST
        masks = _group_masks()
        row = lax.broadcasted_iota(jnp.int32, (CHUNK, CHUNK), 0)
        col = lax.broadcasted_iota(jnp.int32, (CHUNK, CHUNK), 1)
        tril = row >= col
        w_bf = [jnp.where(tril, w_ref[g], 0.0).astype(BF16) for g in range(A_GROUPS)]
        wt_bf = [jnp.where(col >= row, wt_ref[g], 0.0).astype(BF16) for g in range(A_GROUPS)]
        pavg_v = p_ref[...]
        lg = lg_ref[...]
        ug, dug = _gelu_and_grad(u_ref[...])
        vg, dvg_dx = _gelu_and_grad(v_ref[...])
        vhat, rstd = _layernorm_groups(vg, pavg_v)
        vn = vhat * lg + lb_ref[...]
        da = da_ref[...]
        bz = bz_ref[...]
        for c in range(GATE_ROWS // CHUNK):
            sl = slice(c * CHUNK, (c + 1) * CHUNK)
            vn_bf = vn[sl].astype(BF16)
            z = _spatial_mix(w_bf, vn_bf, masks, bz)
            dz = da[sl] * ug[sl]
            duv_ref[sl, 0:A_WIDTH] = da[sl] * z * dug[sl]
            dbz_ref[...] += dz
            dz_bf = dz.astype(BF16)
            dvn = jnp.zeros((CHUNK, A_WIDTH), F32)
            for g in range(A_GROUPS):
                dz_g = jnp.where(masks[g], dz, 0.0).astype(BF16)
                dw_ref[g] += jnp.where(tril, _dot_nt(dz_g, vn_bf), 0.0)
                dvn = dvn + jnp.where(masks[g], _dot(wt_bf[g], dz_bf), 0.0)
            vh = vhat[sl]
            dlb_ref[...] += jnp.sum(dvn, axis=0, keepdims=True)
            dlg_ref[...] += jnp.sum(dvn * vh, axis=0, keepdims=True)
            dvh = dvn * lg
            m1 = _dot(dvh, pavg_v, hi)
            m2 = _dot(dvh * vh, pavg_v, hi)
            duv_ref[sl, A_WIDTH:2 * A_WIDTH] = rstd[sl] * (dvh - m1 - vh * m2) * dvg_dx[sl]

        @pl.when(i == nsteps - 1)
        def _():
            dbz_ref[...] = _dot(dbz_ref[...], pavg_v * float(HEAD_DIM), hi)

    full = lambda shape: pl.BlockSpec(shape, lambda i: tuple(0 for _ in shape))
    return pl.pallas_call(
        body, grid=(nsteps,),
        in_specs=[pl.BlockSpec((GATE_ROWS, A_WIDTH), lambda i: (i, 0)),
                  pl.BlockSpec((GATE_ROWS, A_WIDTH), lambda i: (i, 1)),
                  pl.BlockSpec((GATE_ROWS, A_WIDTH), lambda i: (i, 0)),
                  full((1, A_WIDTH)), full((1, A_WIDTH)), full((A_GROUPS, CHUNK, CHUNK)),
                  full((A_GROUPS, CHUNK, CHUNK)), full((CHUNK, A_WIDTH)), full((A_WIDTH, A_WIDTH))],
        out_specs=[pl.BlockSpec((GATE_ROWS, 2 * A_WIDTH), lambda i: (i, 0)),
                   full((1, A_WIDTH)), full((1, A_WIDTH)), full((A_GROUPS, CHUNK, CHUNK)),
                   full((CHUNK, A_WIDTH))],
        out_shape=[jax.ShapeDtypeStruct((m, 2 * A_WIDTH), F32),
                   jax.ShapeDtypeStruct((1, A_WIDTH), F32), jax.ShapeDtypeStruct((1, A_WIDTH), F32),
                   jax.ShapeDtypeStruct((A_GROUPS, CHUNK, CHUNK), F32),
                   jax.ShapeDtypeStruct((CHUNK, A_WIDTH), F32)],
        compiler_params=_params(("arbitrary",)), name="gate_bwd",
    )(proj, proj, dmix, ln_g, ln_b, w_s, w_st, bz, pavg)


Q_BLOCK = 128
Q_COL, K_COL, V_COL = 2, 5, 8
N_COLBLK = IN_COLS // LANE_BLOCK
HEAD_BLOCKS = B_WIDTH // LANE_BLOCK
HEADS_PER_BLOCK = LANE_BLOCK // HEAD_DIM


def _t5_bucket_np(dist, dtype):
    max_exact = NUM_BUCKETS // 2
    d = np.maximum(dist, 1).astype(dtype)
    large = max_exact + (np.log(d / dtype(max_exact)) / dtype(math.log(MAX_DISTANCE / max_exact))
                         * dtype(NUM_BUCKETS - max_exact))
    large = np.minimum(large.astype(np.int32), NUM_BUCKETS - 1)
    return np.where(dist < max_exact, dist, large)


def _bucket_tables():
    i = np.arange(Q_BLOCK)[:, None]
    j = np.arange(Q_BLOCK)[None, :]
    tables = []
    for _, dil in DILATED:
        rel_prev = Q_BLOCK + i - j
        rel_cur = i - j
        rel = np.concatenate([rel_prev, rel_cur], axis=1)
        valid = np.concatenate([rel_prev <= Q_BLOCK, rel_cur >= 0], axis=1)
        dist = np.maximum(rel, 0) * dil
        b32 = _t5_bucket_np(dist, np.float32)
        b64 = _t5_bucket_np(dist, np.float64)
        assert np.array_equal(b32, b64)
        tables.append(np.where(valid, b32, -1).astype(np.int32))
    return np.stack(tables)


def _bias_tables(rel_bias, buckets_np):
    present = [sorted(set(int(v) for v in np.unique(buckets_np[c]) if v >= 0)) for c in range(len(DILATED))]

    def body(rb_ref, bk_ref, o_ref):
        for c in range(len(DILATED)):
            bk = bk_ref[c]
            for h in range(B_HEADS):
                acc = jnp.full((Q_BLOCK, 2 * Q_BLOCK), NEG_INF, F32)
                for b in present[c]:
                    acc = jnp.where(bk == b, rb_ref[b, h], acc)
                o_ref[c, h] = acc

    return pl.pallas_call(
        body,
        in_specs=[pl.BlockSpec(memory_space=pltpu.SMEM), pl.BlockSpec(memory_space=pltpu.VMEM)],
        out_specs=pl.BlockSpec(memory_space=pltpu.VMEM),
        out_shape=jax.ShapeDtypeStruct((len(DILATED), B_HEADS, Q_BLOCK, 2 * Q_BLOCK), F32),
        compiler_params=_params(), name="bias_tables",
    )(rel_bias, jnp.asarray(buckets_np))


def _head_masks():
    lane = lax.broadcasted_iota(jnp.int32, (1, LANE_BLOCK), 1)
    return [(lane >= h * HEAD_DIM) & (lane < (h + 1) * HEAD_DIM) for h in range(HEADS_PER_BLOCK)]


def _attn_fwd(proj, bias, dil, batch):
    m = proj.shape[0]
    tr = SEQ // dil
    nb = tr // Q_BLOCK
    proj3 = proj.reshape(batch, tr, dil * IN_COLS)

    def body(q_ref, k_ref, v_ref, b_ref, o_ref, l_ref):
        masks = _head_masks()

        def block(n, carry):
            r0 = pl.multiple_of(n * Q_BLOCK, Q_BLOCK)
            rows = pl.ds(r0, Q_BLOCK)
            q = q_ref[rows, :] * 0.125
            kc = k_ref[rows, :].astype(BF16)
            vc = v_ref[rows, :].astype(BF16)
            if nb > 1:
                p0 = pl.multiple_of(jnp.maximum(n - 1, 0) * Q_BLOCK, Q_BLOCK)
                kp = k_ref[pl.ds(p0, Q_BLOCK), :].astype(BF16)
                vp = v_ref[pl.ds(p0, Q_BLOCK), :].astype(BF16)
            o_acc = jnp.zeros((Q_BLOCK, LANE_BLOCK), F32)
            l_acc = jnp.zeros((Q_BLOCK, LANE_BLOCK), F32)
            for h in range(HEADS_PER_BLOCK):
                qh = jnp.where(masks[h], q, 0.0).astype(BF16)
                sc = _dot_nt(qh, kc) + b_ref[h, :, Q_BLOCK:]
                mx = jnp.max(sc, axis=1, keepdims=True)
                if nb > 1:
                    sp = _dot_nt(qh, kp) + jnp.where(n == 0, NEG_INF, b_ref[h, :, :Q_BLOCK])
                    mx = jnp.maximum(mx, jnp.max(sp, axis=1, keepdims=True))
                pc = jnp.exp(sc - mx)
                den = jnp.sum(pc, axis=1, keepdims=True)
                oh = _dot(pc.astype(BF16), vc)
                if nb > 1:
                    pp = jnp.exp(sp - mx)
                    den = den + jnp.sum(pp, axis=1, keepdims=True)
                    oh = oh + _dot(pp.astype(BF16), vp)
                o_acc = jnp.where(masks[h], oh / den, o_acc)
                l_acc = jnp.where(masks[h], mx + jnp.log(den), l_acc)
            o_ref[rows, :] = o_acc
            l_ref[rows, :] = l_acc
            return carry

        if nb > 1:
            lax.fori_loop(0, nb, block, 0)
        else:
            block(0, 0)

    def in_spec(col0):
        return pl.BlockSpec((None, tr, LANE_BLOCK), lambda b, r, g: (b, 0, r * N_COLBLK + col0 + g))

    out_spec = pl.BlockSpec((None, tr, LANE_BLOCK), lambda b, r, g: (b, 0, r * HEAD_BLOCKS + g))
    out_sds = jax.ShapeDtypeStruct((batch, tr, dil * B_WIDTH), F32)
    o, lse = pl.pallas_call(
        body, grid=(batch, dil, HEAD_BLOCKS),
        in_specs=[in_spec(Q_COL), in_spec(K_COL), in_spec(V_COL),
                  pl.BlockSpec((HEADS_PER_BLOCK, Q_BLOCK, 2 * Q_BLOCK), lambda b, r, g: (g, 0, 0))],
        out_specs=[out_spec, out_spec],
        out_shape=[out_sds, out_sds],
        compiler_params=_params(("parallel", "parallel", "parallel")), name=f"attn_fwd_d{dil}",
    )(proj3, proj3, proj3, bias)
    return o.reshape(m, B_WIDTH), lse.reshape(m, B_WIDTH)


def _attn_combine(outs, lses):
    m = outs[0].shape[0]
    nc = len(outs)

    def body(*refs):
        o_refs, l_refs = refs[:nc], refs[nc:2 * nc]
        of_ref, ob_ref, lt_ref = refs[2 * nc:]
        ls = [r[...] for r in l_refs]
        mx = functools.reduce(jnp.maximum, ls)
        ws = [jnp.exp(l - mx) for l in ls]
        tot = functools.reduce(lambda a, b: a + b, ws)
        inv = 1.0 / tot
        o = functools.reduce(lambda a, b: a + b, [w * inv * r[...] for w, r in zip(ws, o_refs)])
        of_ref[...] = o
        ob_ref[...] = o.astype(BF16)
        lt_ref[...] = mx + jnp.log(tot)

    spec = pl.BlockSpec((ROW_TILE, B_WIDTH), lambda i: (i, 0))
    return pl.pallas_call(
        body, grid=(m // ROW_TILE,),
        in_specs=[spec] * (2 * nc), out_specs=[spec, spec, spec],
        out_shape=[jax.ShapeDtypeStruct((m, B_WIDTH), F32), jax.ShapeDtypeStruct((m, B_WIDTH), BF16),
                   jax.ShapeDtypeStruct((m, B_WIDTH), F32)],
        compiler_params=_params(("parallel",)), name="attn_combine",
    )(*outs, *lses)


def _attn_bwd(proj, dmix, o, lse, bias, dil, batch):
    m = proj.shape[0]
    tr = SEQ // dil
    nb = tr // Q_BLOCK
    proj3 = proj.reshape(batch, tr, dil * IN_COLS)
    dmix3 = dmix.reshape(batch, tr, dil * D_MODEL)
    o3 = o.reshape(batch, tr, dil * B_WIDTH)
    l3 = lse.reshape(batch, tr, dil * B_WIDTH)

    def body(q_ref, k_ref, v_ref, do_ref, o_ref, l_ref, b_ref, dq_ref, dk_ref, dv_ref, ds_ref):
        first = (pl.program_id(1) == 0) & (pl.program_id(2) == 0)

        @pl.when(first)
        def _():
            ds_ref[...] = jnp.zeros_like(ds_ref)

        dk_ref[...] = jnp.zeros_like(dk_ref)
        dv_ref[...] = jnp.zeros_like(dv_ref)
        masks = _head_masks()

        def block(n, carry):
            r0 = pl.multiple_of(n * Q_BLOCK, Q_BLOCK)
            rows = pl.ds(r0, Q_BLOCK)
            q = q_ref[rows, :] * 0.125
            kc = k_ref[rows, :].astype(BF16)
            vc = v_ref[rows, :].astype(BF16)
            do = do_ref[rows, :]
            ov = o_ref[rows, :]
            lv = l_ref[rows, :]
            if nb > 1:
                p0 = pl.multiple_of(jnp.maximum(n - 1, 0) * Q_BLOCK, Q_BLOCK)
                prow = pl.ds(p0, Q_BLOCK)
                kp = k_ref[prow, :].astype(BF16)
                vp = v_ref[prow, :].astype(BF16)
                dkp = jnp.zeros((Q_BLOCK, LANE_BLOCK), F32)
                dvp = jnp.zeros((Q_BLOCK, LANE_BLOCK), F32)
            dq = jnp.zeros((Q_BLOCK, LANE_BLOCK), F32)
            dkc = jnp.zeros((Q_BLOCK, LANE_BLOCK), F32)
            dvc = jnp.zeros((Q_BLOCK, LANE_BLOCK), F32)
            for h in range(HEADS_PER_BLOCK):
                qh = jnp.where(masks[h], q, 0.0).astype(BF16)
                doh = jnp.where(masks[h], do, 0.0)
                doh_bf = doh.astype(BF16)
                lrow = jnp.max(jnp.where(masks[h], lv, -3e38), axis=1, keepdims=True)
                drow = jnp.sum(doh * ov, axis=1, keepdims=True)
                pc = jnp.exp(_dot_nt(qh, kc) + b_ref[h, :, Q_BLOCK:] - lrow)
                dsc = pc * (_dot_nt(doh_bf, vc) - drow)
                ds_ref[h, :, Q_BLOCK:] += dsc
                dsc_bf = dsc.astype(BF16)
                dqh = _dot(dsc_bf, kc)
                dkc = dkc + _dot_tn(dsc_bf, qh)
                dvc = dvc + _dot_tn(pc.astype(BF16), doh_bf)
                if nb > 1:
                    bp = jnp.where(n == 0, NEG_INF, b_ref[h, :, :Q_BLOCK])
                    pp = jnp.exp(_dot_nt(qh, kp) + bp - lrow)
                    dsp = pp * (_dot_nt(doh_bf, vp) - drow)
                    ds_ref[h, :, :Q_BLOCK] += dsp
                    dsp_bf = dsp.astype(BF16)
                    dqh = dqh + _dot(dsp_bf, kp)
                    dkp = dkp + _dot_tn(dsp_bf, qh)
                    dvp = dvp + _dot_tn(pp.astype(BF16), doh_bf)
                dq = jnp.where(masks[h], dqh, dq)
            dq_ref[rows, :] = dq * 0.125
            dk_ref[rows, :] += dkc
            dv_ref[rows, :] += dvc
            if nb > 1:
                dk_ref[prow, :] += dkp
                dv_ref[prow, :] += dvp
            return carry

        if nb > 1:
            lax.fori_loop(0, nb, block, 0)
        else:
            block(0, 0)

    def in_spec(col0):
        return pl.BlockSpec((None, tr, LANE_BLOCK), lambda g, b, r: (b, 0, r * N_COLBLK + col0 + g))

    do_spec = pl.BlockSpec((None, tr, LANE_BLOCK), lambda g, b, r: (b, 0, r * (D_MODEL // LANE_BLOCK) + 1 + g))
    hd_spec = pl.BlockSpec((None, tr, LANE_BLOCK), lambda g, b, r: (b, 0, r * HEAD_BLOCKS + g))
    tbl_spec = pl.BlockSpec((HEADS_PER_BLOCK, Q_BLOCK, 2 * Q_BLOCK), lambda g, b, r: (g, 0, 0))
    out_sds = jax.ShapeDtypeStruct((batch, tr, dil * B_WIDTH), F32)
    dq, dk, dv, ds = pl.pallas_call(
        body, grid=(HEAD_BLOCKS, batch, dil),
        in_specs=[in_spec(Q_COL), in_spec(K_COL), in_spec(V_COL), do_spec, hd_spec, hd_spec, tbl_spec],
        out_specs=[hd_spec, hd_spec, hd_spec, tbl_spec],
        out_shape=[out_sds, out_sds, out_sds, jax.ShapeDtypeStruct((B_HEADS, Q_BLOCK, 2 * Q_BLOCK), F32)],
        compiler_params=_params(("parallel", "arbitrary", "arbitrary")), name=f"attn_bwd_d{dil}",
    )(proj3, proj3, proj3, dmix3, o3, l3, bias)
    return dq.reshape(m, B_WIDTH), dk.reshape(m, B_WIDTH), dv.reshape(m, B_WIDTH), ds


def _rel_bias_grad(ds_list, buckets_np):
    nc = len(ds_list)
    present = [sorted(set(int(v) for v in np.unique(buckets_np[c]) if v >= 0)) for c in range(nc)]

    def body(*refs):
        bk_ref = refs[0]
        ds_refs = refs[1:1 + nc]
        o_ref, acc_ref = refs[1 + nc:]
        acc_ref[...] = jnp.zeros_like(acc_ref)
        for c in range(nc):
            bk = bk_ref[c]
            for h in range(B_HEADS):
                dsv = ds_refs[c][h]
                for b in present[c]:
                    part = jnp.sum(jnp.where(bk == b, dsv, 0.0), axis=0, keepdims=True)
                    acc_ref[pl.ds(h * NUM_BUCKETS + b, 1), :] += part
        o_ref[...] = jnp.sum(acc_ref[...], axis=1, keepdims=True)

    vm = pl.BlockSpec(memory_space=pltpu.VMEM)
    return pl.pallas_call(
        body, in_specs=[vm] * (1 + nc), out_specs=vm,
        out_shape=jax.ShapeDtypeStruct((B_HEADS * NUM_BUCKETS, 1), F32),
        scratch_shapes=[pltpu.VMEM((B_HEADS * NUM_BUCKETS, 2 * Q_BLOCK), F32)],
        compiler_params=_params(), name="rel_bias_grad",
    )(jnp.asarray(buckets_np), *ds_list)


def _assemble_dproj(duv, dqs, dks, dvs):
    m = duv.shape[0]
    nc = len(dqs)

    def body(*refs):
        duv_ref = refs[0]
        groups = [refs[1:1 + nc], refs[1 + nc:1 + 2 * nc], refs[1 + 2 * nc:1 + 3 * nc]]
        o_ref = refs[-1]
        j = pl.program_id(1)

        @pl.when(j < 2)
        def _():
            o_ref[...] = duv_ref[...].astype(BF16)

        for gi, grp in enumerate(groups):
            lo = 2 + gi * HEAD_BLOCKS

            @pl.when((j >= lo) & (j < lo + HEAD_BLOCKS))
            def _(grp=grp):
                o_ref[...] = functools.reduce(lambda a, b: a + b, [r[...] for r in grp]).astype(BF16)

    def grp_spec(lo):
        return pl.BlockSpec((ROW_TILE, LANE_BLOCK),
                            lambda i, j: (i, jnp.clip(j - lo, 0, HEAD_BLOCKS - 1)))

    in_specs = [pl.BlockSpec((ROW_TILE, LANE_BLOCK), lambda i, j: (i, jnp.minimum(j, 1)))]
    for gi in range(3):
        in_specs += [grp_spec(2 + gi * HEAD_BLOCKS)] * nc
    return pl.pallas_call(
        body, grid=(m // ROW_TILE, N_COLBLK),
        in_specs=in_specs,
        out_specs=pl.BlockSpec((ROW_TILE, LANE_BLOCK), lambda i, j: (i, j)),
        out_shape=jax.ShapeDtypeStruct((m, IN_COLS), BF16),
        compiler_params=_params(("parallel", "arbitrary")), name="assemble_dproj",
    )(duv, *dqs, *dks, *dvs)


def _shift_down(x, k):
    row = lax.broadcasted_iota(jnp.int32, x.shape, 0)
    return jnp.where(row >= k, pltpu.roll(x, k, 0), 0.0)


def _shift_up(x, k):
    n = x.shape[0]
    row = lax.broadcasted_iota(jnp.int32, x.shape, 0)
    return jnp.where(row < n - k, pltpu.roll(x, n - k, 0), 0.0)


def _convgate_fwd(gate, up, conv_w, conv_b, batch):
    m = gate.shape[0]

    def body(g_ref, u_ref, w_ref, b_ref, a_ref):
        g = g_ref[...]
        w = w_ref[...]
        c = b_ref[...] + w[0:1] * _shift_down(g, 2) + w[1:2] * _shift_down(g, 1) + w[2:3] * g
        a_ref[...] = (_gelu(c) * u_ref[...]).astype(BF16)

    blk = pl.BlockSpec((SEQ, LANE_BLOCK), lambda b, j: (b, j))
    return pl.pallas_call(
        body, grid=(batch, D_FF // LANE_BLOCK),
        in_specs=[blk, blk, pl.BlockSpec((3, LANE_BLOCK), lambda b, j: (0, j)),
                  pl.BlockSpec((1, LANE_BLOCK), lambda b, j: (0, j))],
        out_specs=blk,
        out_shape=jax.ShapeDtypeStruct((m, D_FF), BF16),
        compiler_params=_params(("parallel", "parallel")), name="convgate_fwd",
    )(gate, up, conv_w, conv_b)


def _convgate_bwd(gate, up, dact, conv_w, conv_b, batch):
    m = gate.shape[0]

    def body(g_ref, u_ref, da_ref, w_ref, b_ref, dg_ref, du_ref, dw_ref, db_ref):
        @pl.when(pl.program_id(1) == 0)
        def _():
            dw_ref[...] = jnp.zeros_like(dw_ref)
            db_ref[...] = jnp.zeros_like(db_ref)

        g = g_ref[...]
        w = w_ref[...]
        g1 = _shift_down(g, 1)
        g2 = _shift_down(g, 2)
        c = b_ref[...] + w[0:1] * g2 + w[1:2] * g1 + w[2:3] * g
        gg, dgg = _gelu_and_grad(c)
        da = da_ref[...]
        du_ref[...] = (da * gg).astype(BF16)
        dc = da * u_ref[...] * dgg
        db_ref[...] += jnp.sum(dc, axis=0, keepdims=True)
        dw_ref[0:1, :] += jnp.sum(dc * g2, axis=0, keepdims=True)
        dw_ref[1:2, :] += jnp.sum(dc * g1, axis=0, keepdims=True)
        dw_ref[2:3, :] += jnp.sum(dc * g, axis=0, keepdims=True)
        dg_ref[...] = (w[2:3] * dc + w[1:2] * _shift_up(dc, 1) + w[0:1] * _shift_up(dc, 2)).astype(BF16)

    blk = pl.BlockSpec((SEQ, LANE_BLOCK), lambda j, b: (b, j))
    wspec = pl.BlockSpec((3, LANE_BLOCK), lambda j, b: (0, j))
    bspec = pl.BlockSpec((1, LANE_BLOCK), lambda j, b: (0, j))
    return pl.pallas_call(
        body, grid=(D_FF // LANE_BLOCK, batch),
        in_specs=[blk, blk, blk, wspec, bspec],
        out_specs=[blk, blk, wspec, bspec],
        out_shape=[jax.ShapeDtypeStruct((m, D_FF), BF16), jax.ShapeDtypeStruct((m, D_FF), BF16),
                   jax.ShapeDtypeStruct((3, D_FF), F32), jax.ShapeDtypeStruct((1, D_FF), F32)],
        compiler_params=_params(("parallel", "arbitrary")), name="convgate_bwd",
    )(gate, up, dact, conv_w, conv_b)


def _local_step(x, tgt, g1, g2, g3, g4, w_in, ln_g, ln_b, w_s, b_s, rel_bias, w_out, w_gate, w_up,
                conv_w, conv_b, w_down, batch):
    big = dict(tm=1024, out_dtype=F32)
    buckets = _bucket_tables()
    bias = _bias_tables(rel_bias, buckets)
    bz = jnp.repeat(b_s.T, HEAD_DIM, axis=1)
    w_st = jnp.swapaxes(w_s, 1, 2)

    h1 = _rms_fwd(x, g1)
    proj = _mm(h1, w_in, dims="nn", tn=1408, tk=1024, name="mm_proj", **big)
    a = _gate_fwd(proj, ln_g, ln_b, w_s, bz)
    outs, lses = [], []
    for ci, (_, dil) in enumerate(DILATED):
        o_c, l_c = _attn_fwd(proj, bias[ci], dil, batch)
        outs.append(o_c)
        lses.append(l_c)
    o_f32, o_bf, lse = _attn_combine(outs, lses)
    y1 = _mm(a, w_out[:A_WIDTH], dims="nn", tn=1024, tk=A_WIDTH, name="mm_out_a", **big)
    y1 = _mm(o_bf, w_out[A_WIDTH:], dims="nn", tn=1024, tk=B_WIDTH, name="mm_out_b", add=y1, **big)
    x1, h2 = _mid_fwd(x, y1, g2, g3)
    gate = _mm(h2, w_gate, dims="nn", tn=1408, tk=1024, name="mm_gate", **big)
    up = _mm(h2, w_up, dims="nn", tn=1408, tk=1024, name="mm_up", **big)
    act = _convgate_fwd(gate, up, conv_w, conv_b, batch)
    y2 = _mm(act, w_down, dims="nn", tn=1024, tk=1408, name="mm_down", **big)
    dx2, dy2, dg4, loss = _loss_head(x1, y2, tgt, g4)

    dact = _mm(dy2, w_down, dims="nt", tn=1408, tk=1024, name="mm_dact", **big)
    dw_down = _mm(act, dy2, dims="tn", tm=1408, tn=1024, tk=1024, out_dtype=F32, name="mm_dw_down")
    dgate, dup, dconv_w, dconv_b = _convgate_bwd(gate, up, dact, conv_w, conv_b, batch)
    dh2 = _mm(dgate, w_gate, dims="nt", tn=1024, tk=1408, name="mm_dh2_g", **big)
    dh2 = _mm(dup, w_up, dims="nt", tn=1024, tk=1408, name="mm_dh2_u", add=dh2, **big)
    dw_gate = _mm(h2, dgate, dims="tn", tm=1024, tn=1408, tk=1024, out_dtype=F32, name="mm_dw_gate")
    dw_up = _mm(h2, dup, dims="tn", tm=1024, tn=1408, tk=1024, out_dtype=F32, name="mm_dw_up")
    dx1, dy1, dg2, dg3 = _mid_bwd(x1, y1, dh2, dx2, g2, g3)
    dmix = _mm(dy1, w_out, dims="nt", tn=1024, tk=1024, name="mm_dmix", **big)
    dw_out_a = _mm(a, dy1, dims="tn", tm=A_WIDTH, tn=1024, tk=1024, out_dtype=F32, name="mm_dw_out_a")
    dw_out_b = _mm(o_bf, dy1, dims="tn", tm=B_WIDTH, tn=1024, tk=1024, out_dtype=F32, name="mm_dw_out_b")
    duv, dln_g, dln_b, dw_s, dbz = _gate_bwd(proj, dmix, ln_g, ln_b, w_s, w_st, bz)
    dqs, dks, dvs, dss = [], [], [], []
    for ci, (_, dil) in enumerate(DILATED):
        dq_c, dk_c, dv_c, ds_c = _attn_bwd(proj, dmix, o_f32, lse, bias[ci], dil, batch)
        dqs.append(dq_c)
        dks.append(dk_c)
        dvs.append(dv_c)
        dss.append(ds_c)
    drel = _rel_bias_grad(dss, buckets)
    dproj = _assemble_dproj(duv, dqs, dks, dvs)
    dh1 = _mm(dproj, w_in, dims="nt", tn=1024, tk=1408, name="mm_dh1", **big)
    dw_in = _mm(h1, dproj, dims="tn", tm=1024, tn=1408, tk=1024, out_dtype=F32, name="mm_dw_in")
    dx0, dg1 = _in_bwd(x, dh1, dx1, g1)

    grads = dict(
        norm_mix_pre=dg1, norm_mix_post=dg2, norm_ffn_pre=dg3, norm_ffn_post=dg4,
        w_in=dw_in, ln_v_gain=dln_g, ln_v_bias=dln_b, spatial_w=dw_s,
        spatial_b=dbz[:, ::HEAD_DIM].T,
        rel_bias=drel.reshape(B_HEADS, NUM_BUCKETS).T,
        w_out=jnp.concatenate([dw_out_a, dw_out_b], axis=0),
        w_gate=dw_gate, w_up=dw_up, conv_w=dconv_w, conv_b=dconv_b, w_down=dw_down,
    )
    return loss, dx0, grads


def _mesh_pos():
    x, y, c = lax.axis_index("x"), lax.axis_index("y"), lax.axis_index("c")
    chips = [(1 - x, y), (x, 1 - y), (1 - x, 1 - y)]
    return x, y, c, chips


ANY = pl.BlockSpec(memory_space=pl.ANY)


def _gather_weights(shards, conv_w_shard):
    nt = len(shards)

    def body(*refs):
        shard_refs = refs[:nt]
        cw_ref = refs[nt]
        out_refs = refs[nt + 1:2 * nt + 1]
        cw_out = refs[2 * nt + 1]
        send_sems, recv_sems, local_sems = refs[2 * nt + 2:]
        x, y, c, chips = _mesh_pos()
        s = 2 * x + y
        sib = (x, y, 1 - c)

        def half(ref, chip, which, t):
            rows = shards[t].shape[0] // 2
            return ref.at[2 * chip[0] + chip[1], pl.ds(which * rows, rows), :]

        def rcopy(k, src, dst, to):
            return pltpu.make_async_remote_copy(src_ref=src, dst_ref=dst, send_sem=send_sems.at[k],
                                                recv_sem=recv_sems.at[k], device_id=to, device_id_type=MESH)

        local = []
        for t in range(nt):
            local.append(pltpu.make_async_copy(shard_refs[t], out_refs[t].at[s], local_sems.at[t]))
        local.append(pltpu.make_async_copy(cw_ref, cw_out.at[s], local_sems.at[nt]))
        for cp in local:
            cp.start()
        sends = []
        for t in range(nt):
            rows = shards[t].shape[0] // 2
            for j, chip in enumerate(chips):
                sends.append(rcopy(7 * t + j, shard_refs[t].at[pl.ds(c * rows, rows), :],
                                   half(out_refs[t], (x, y), c, t), (*chip, c)))
        for j, chip in enumerate(chips):
            sends.append(rcopy(7 * nt + j, cw_ref, cw_out.at[s], (*chip, c)))
        for cp in sends:
            cp.start()
        fwd = []
        for t in range(nt):
            for j, chip in enumerate(chips):
                landed = half(out_refs[t], chip, c, t)
                rcopy(7 * t + j, landed, landed, (*chip, c)).wait_recv()
                f = rcopy(7 * t + 3 + j, landed, landed, sib)
                f.start()
                fwd.append(f)
        for j, chip in enumerate(chips):
            dst = cw_out.at[2 * chip[0] + chip[1]]
            rcopy(7 * nt + j, dst, dst, (*chip, c)).wait_recv()
        for t in range(nt):
            for j, chip in enumerate(chips):
                other = half(out_refs[t], chip, 1 - c, t)
                rcopy(7 * t + 3 + j, other, other, sib).wait_recv()
        for cp in sends + fwd:
            cp.wait_send()
        for cp in local:
            cp.wait()

    out_shape = [jax.ShapeDtypeStruct((N_SHARD,) + sh.shape, sh.dtype) for sh in shards]
    out_shape.append(jax.ShapeDtypeStruct((N_SHARD,) + conv_w_shard.shape, conv_w_shard.dtype))
    nsem = 7 * nt + 3
    return pl.pallas_call(
        body, in_specs=[ANY] * (nt + 1), out_specs=[ANY] * (nt + 1), out_shape=out_shape,
        scratch_shapes=[pltpu.SemaphoreType.DMA((nsem,)), pltpu.SemaphoreType.DMA((nsem,)),
                        pltpu.SemaphoreType.DMA((nt + 1,))],
        compiler_params=pltpu.CompilerParams(has_side_effects=True), name="gather_weights",
    )(*shards, conv_w_shard)


def _exchange_halves(grads):
    nt = len(grads)

    def body(*refs):
        g_refs = refs[:nt]
        out_refs = refs[nt:2 * nt]
        send_sems, recv_sems = refs[2 * nt:]
        x, y, c, _ = _mesh_pos()
        copies = []
        for t in range(nt):
            rows = grads[t].shape[1] // 2
            copies.append(pltpu.make_async_remote_copy(
                src_ref=g_refs[t].at[:, pl.ds((1 - c) * rows, rows), :], dst_ref=out_refs[t],
                send_sem=send_sems.at[t], recv_sem=recv_sems.at[t], device_id=(x, y, 1 - c), device_id_type=MESH))
        for cp in copies:
            cp.start()
        for cp in copies:
            cp.wait()

    out_shape = [jax.ShapeDtypeStruct((N_SHARD, g.shape[1] // 2, g.shape[2]), g.dtype) for g in grads]
    return pl.pallas_call(
        body, in_specs=[ANY] * nt, out_specs=[ANY] * nt, out_shape=out_shape,
        scratch_shapes=[pltpu.SemaphoreType.DMA((nt,)), pltpu.SemaphoreType.DMA((nt,))],
        compiler_params=pltpu.CompilerParams(has_side_effects=True), name="rs_sibling_exchange",
    )(*grads)


def _add_halves(g, recv, c_idx):
    _, rows2, cols = g.shape
    rows = rows2 // 2
    tr = rows // 2 if rows % 16 == 0 and rows >= 256 else rows
    nblk = rows // tr

    def body(c_ref, g_ref, r_ref, o_ref):
        o_ref[...] = (g_ref[...] + r_ref[...]).astype(BF16)

    return pl.pallas_call(
        body,
        grid_spec=pltpu.PrefetchScalarGridSpec(
            num_scalar_prefetch=1, grid=(N_SHARD, nblk),
            in_specs=[pl.BlockSpec((None, tr, cols), lambda s, i, c: (s, c[0] * nblk + i, 0)),
                      pl.BlockSpec((None, tr, cols), lambda s, i, c: (s, i, 0))],
            out_specs=pl.BlockSpec((None, tr, cols), lambda s, i, c: (s, i, 0))),
        out_shape=jax.ShapeDtypeStruct((N_SHARD, rows, cols), BF16),
        compiler_params=_params(("parallel", "parallel")), name="rs_add_halves",
    )(c_idx, g, recv)


def _exchange_chips(parts):
    nt = len(parts)

    def body(*refs):
        p_refs = refs[:nt]
        out_refs = refs[nt:2 * nt]
        send_sems, recv_sems = refs[2 * nt:]
        x, y, c, chips = _mesh_pos()
        copies = []
        for t in range(nt):
            for j, chip in enumerate(chips):
                copies.append(pltpu.make_async_remote_copy(
                    src_ref=p_refs[t].at[2 * chip[0] + chip[1]], dst_ref=out_refs[t].at[j],
                    send_sem=send_sems.at[3 * t + j], recv_sem=recv_sems.at[3 * t + j],
                    device_id=(*chip, c), device_id_type=MESH))
        for cp in copies:
            cp.start()
        for cp in copies:
            cp.wait()

    out_shape = [jax.ShapeDtypeStruct((3,) + p.shape[1:], p.dtype) for p in parts]
    return pl.pallas_call(
        body, in_specs=[ANY] * nt, out_specs=[ANY] * nt, out_shape=out_shape,
        scratch_shapes=[pltpu.SemaphoreType.DMA((3 * nt,)), pltpu.SemaphoreType.DMA((3 * nt,))],
        compiler_params=pltpu.CompilerParams(has_side_effects=True), name="rs_chip_exchange",
    )(*parts)


def _add_chips(part, recv, s_idx):
    _, rows, cols = part.shape
    tr = rows // 2 if rows % 32 == 0 and rows >= 256 else rows

    def body(s_ref, p_ref, r_ref, o_ref):
        acc = p_ref[...].astype(F32)
        for j in range(3):
            acc = acc + r_ref[j].astype(F32)
        o_ref[...] = acc

    return pl.pallas_call(
        body,
        grid_spec=pltpu.PrefetchScalarGridSpec(
            num_scalar_prefetch=1, grid=(rows // tr,),
            in_specs=[pl.BlockSpec((None, tr, cols), lambda i, s: (s[0], i, 0)),
                      pl.BlockSpec((3, tr, cols), lambda i, s: (0, i, 0))],
            out_specs=pl.BlockSpec((tr, cols), lambda i, s: (i, 0))),
        out_shape=jax.ShapeDtypeStruct((rows, cols), F32),
        compiler_params=_params(("parallel",)), name="rs_add_chips",
    )(s_idx, part, recv)


def _share_halves(halves):
    nt = len(halves)

    def body(*refs):
        h_refs = refs[:nt]
        out_refs = refs[nt:2 * nt]
        send_sems, recv_sems, local_sems = refs[2 * nt:]
        x, y, c, _ = _mesh_pos()
        copies, local = [], []
        for t in range(nt):
            rows = halves[t].shape[0]
            mine = out_refs[t].at[pl.ds(c * rows, rows), :]
            local.append(pltpu.make_async_copy(h_refs[t], mine, local_sems.at[t]))
            copies.append(pltpu.make_async_remote_copy(
                src_ref=h_refs[t], dst_ref=mine, send_sem=send_sems.at[t], recv_sem=recv_sems.at[t],
                device_id=(x, y, 1 - c), device_id_type=MESH))
        for cp in local + copies:
            cp.start()
        for t in range(nt):
            rows = halves[t].shape[0]
            theirs = out_refs[t].at[pl.ds((1 - c) * rows, rows), :]
            pltpu.make_async_remote_copy(
                src_ref=h_refs[t], dst_ref=theirs, send_sem=send_sems.at[t], recv_sem=recv_sems.at[t],
                device_id=(x, y, 1 - c), device_id_type=MESH).wait_recv()
        for cp in copies:
            cp.wait_send()
        for cp in local:
            cp.wait()

    out_shape = [jax.ShapeDtypeStruct((2 * h.shape[0], h.shape[1]), h.dtype) for h in halves]
    return pl.pallas_call(
        body, in_specs=[ANY] * nt, out_specs=[ANY] * nt, out_shape=out_shape,
        scratch_shapes=[pltpu.SemaphoreType.DMA((nt,)), pltpu.SemaphoreType.DMA((nt,)),
                        pltpu.SemaphoreType.DMA((nt,))],
        compiler_params=pltpu.CompilerParams(has_side_effects=True), name="rs_share_halves",
    )(*halves)


def _allreduce_small(packed):
    rows = packed.shape[0]

    def body(p_ref, o_ref, sib_ref, chip_ref, send_sems, recv_sems):
        x, y, c, chips = _mesh_pos()
        first = pltpu.make_async_remote_copy(src_ref=p_ref, dst_ref=sib_ref, send_sem=send_sems.at[0],
                                             recv_sem=recv_sems.at[0], device_id=(x, y, 1 - c), device_id_type=MESH)
        first.start()
        first.wait()
        o_ref[...] = p_ref[...] + sib_ref[...]
        copies = [pltpu.make_async_remote_copy(src_ref=o_ref, dst_ref=chip_ref.at[j], send_sem=send_sems.at[1 + j],
                                               recv_sem=recv_sems.at[1 + j], device_id=(*chip, c), device_id_type=MESH)
                  for j, chip in enumerate(chips)]
        for cp in copies:
            cp.start()
        for cp in copies:
            cp.wait()
        o_ref[...] = (o_ref[...] + chip_ref[0]) + (chip_ref[1] + chip_ref[2])

    vm = pl.BlockSpec(memory_space=pltpu.VMEM)
    return pl.pallas_call(
        body, in_specs=[vm], out_specs=vm, out_shape=jax.ShapeDtypeStruct(packed.shape, F32),
        scratch_shapes=[pltpu.VMEM((rows, 128), F32), pltpu.VMEM((3, rows, 128), F32),
                        pltpu.SemaphoreType.DMA((4,)), pltpu.SemaphoreType.DMA((4,))],
        compiler_params=pltpu.CompilerParams(has_side_effects=True, vmem_limit_bytes=VMEM_LIMIT),
        name="allreduce_small",
    )(packed)


def _adamw(w, g, m, v, name):
    rows, cols = w.shape
    tr = rows
    if rows * cols > 256 * 1024:
        tr = next(cand for cand in (256, 176, 128) if rows % cand == 0)

    def body(w_ref, g_ref, m_ref, v_ref, d_ref, nm_ref, nv_ref):
        gv = g_ref[...]
        nm = ADAM_B1 * m_ref[...] + (1.0 - ADAM_B1) * gv
        nv = ADAM_B2 * v_ref[...] + (1.0 - ADAM_B2) * (gv * gv)
        m_hat = nm / (1.0 - ADAM_B1 ** ADAM_STEP)
        v_hat = nv / (1.0 - ADAM_B2 ** ADAM_STEP)
        d_ref[...] = -ADAM_LR * (m_hat / (jnp.sqrt(v_hat) + ADAM_EPS) + ADAM_WD * w_ref[...])
        nm_ref[...] = nm
        nv_ref[...] = nv

    spec = pl.BlockSpec((tr, cols), lambda i: (i, 0))
    sds = jax.ShapeDtypeStruct((rows, cols), F32)
    return pl.pallas_call(
        body, grid=(rows // tr,), in_specs=[spec] * 4, out_specs=[spec] * 3, out_shape=[sds] * 3,
        compiler_params=_params(("parallel",)), name=name,
    )(w, g, m, v)


def _pack(arrays, rows):
    flat = jnp.concatenate([a.reshape(-1) for a in arrays])
    flat = jnp.pad(flat, (0, rows * 128 - flat.shape[0]))
    return flat.reshape(rows, 128)


def _unpack(packed, shapes):
    flat = packed.reshape(-1)
    out, off = [], 0
    for sh in shapes:
        n = int(np.prod(sh))
        out.append(flat[off:off + n].reshape(sh))
        off += n
    return out


SMALL = ["norm_mix_pre", "norm_mix_post", "norm_ffn_pre", "norm_ffn_post", "ln_v_gain", "ln_v_bias",
         "spatial_w", "spatial_b", "rel_bias", "conv_b"]
LARGE = ["w_in", "w_gate", "w_up", "w_down", "w_out"]
ORDER = ["norm_mix_pre", "norm_mix_post", "norm_ffn_pre", "norm_ffn_post", "w_in", "ln_v_gain", "ln_v_bias",
         "spatial_w", "spatial_b", "rel_bias", "w_out", "w_gate", "w_up", "conv_w", "conv_b", "w_down"]


def _col_shards(full):
    rows, cols4 = full.shape
    return full.reshape(rows, N_SHARD, cols4 // N_SHARD).transpose(1, 0, 2)


def _from_col_shards(g):
    n, rows, cols = g.shape
    return g.transpose(1, 0, 2).reshape(rows, n * cols)


def kernel(x, norm_mix_pre, norm_mix_post, norm_ffn_pre, norm_ffn_post, w_in, ln_v_gain, ln_v_bias, spatial_w, spatial_b, rel_bias, w_out, w_gate, w_up, conv_w, conv_b, w_down, loss_target, m_norm_mix_pre, m_norm_mix_post, m_norm_ffn_pre, m_norm_ffn_post, m_w_in, m_ln_v_gain, m_ln_v_bias, m_spatial_w, m_spatial_b, m_rel_bias, m_w_out, m_w_gate, m_w_up, m_conv_w, m_conv_b, m_w_down, v_norm_mix_pre, v_norm_mix_post, v_norm_ffn_pre, v_norm_ffn_post, v_w_in, v_ln_v_gain, v_ln_v_bias, v_spatial_w, v_spatial_b, v_rel_bias, v_w_out, v_w_gate, v_w_up, v_conv_w, v_conv_b, v_w_down):
    params = dict(norm_mix_pre=norm_mix_pre, norm_mix_post=norm_mix_post, norm_ffn_pre=norm_ffn_pre,
                  norm_ffn_post=norm_ffn_post, w_in=w_in, ln_v_gain=ln_v_gain, ln_v_bias=ln_v_bias,
                  spatial_w=spatial_w, spatial_b=spatial_b, rel_bias=rel_bias, w_out=w_out, w_gate=w_gate,
                  w_up=w_up, conv_w=conv_w, conv_b=conv_b, w_down=w_down)
    mom = dict(norm_mix_pre=m_norm_mix_pre, norm_mix_post=m_norm_mix_post, norm_ffn_pre=m_norm_ffn_pre,
               norm_ffn_post=m_norm_ffn_post, w_in=m_w_in, ln_v_gain=m_ln_v_gain, ln_v_bias=m_ln_v_bias,
               spatial_w=m_spatial_w, spatial_b=m_spatial_b, rel_bias=m_rel_bias, w_out=m_w_out, w_gate=m_w_gate,
               w_up=m_w_up, conv_w=m_conv_w, conv_b=m_conv_b, w_down=m_w_down)
    var = dict(norm_mix_pre=v_norm_mix_pre, norm_mix_post=v_norm_mix_post, norm_ffn_pre=v_norm_ffn_pre,
               norm_ffn_post=v_norm_ffn_post, w_in=v_w_in, ln_v_gain=v_ln_v_gain, ln_v_bias=v_ln_v_bias,
               spatial_w=v_spatial_w, spatial_b=v_spatial_b, rel_bias=v_rel_bias, w_out=v_w_out, w_gate=v_w_gate,
               w_up=v_w_up, conv_w=v_conv_w, conv_b=v_conv_b, w_down=v_w_down)

    batch = x.shape[0]
    xi, yi, ci = lax.axis_index("x"), lax.axis_index("y"), lax.axis_index("c")
    s_idx = (2 * xi + yi).astype(jnp.int32).reshape(1)
    c_idx = ci.astype(jnp.int32).reshape(1)

    shards = [params[n][0].astype(BF16) for n in LARGE]
    g_in, g_gate, g_up, g_down, g_out, g_convw = _gather_weights(shards, conv_w[0])
    w_in_f = _from_col_shards(g_in)
    w_gate_f = _from_col_shards(g_gate)
    w_up_f = _from_col_shards(g_up)
    w_down_f = g_down.reshape(D_FF, D_MODEL)
    w_out_f = g_out.reshape(D_MODEL, D_MODEL)
    conv_w_f = _from_col_shards(g_convw)

    loss_part, dx0, grads = _local_step(
        x.reshape(batch * SEQ, D_MODEL), loss_target.reshape(batch * SEQ, D_MODEL),
        norm_mix_pre, norm_mix_post, norm_ffn_pre, norm_ffn_post, w_in_f,
        ln_v_gain.reshape(1, A_WIDTH), ln_v_bias.reshape(1, A_WIDTH), spatial_w[0], spatial_b[0], rel_bias,
        w_out_f, w_gate_f, w_up_f, conv_w_f, conv_b, w_down_f, batch)
    loss = lax.psum(loss_part[0, 0], ("x", "y", "c"))
    grad_x = dx0.reshape(batch, SEQ, D_MODEL)

    big = [_col_shards(grads["w_in"]), _col_shards(grads["w_gate"]), _col_shards(grads["w_up"]),
           grads["w_down"].reshape(N_SHARD, SHARD_FF, D_MODEL),
           grads["w_out"].reshape(N_SHARD, D_MODEL // N_SHARD, D_MODEL)]
    recv_a = _exchange_halves(big)
    parts = [_add_halves(g, r, c_idx) for g, r in zip(big, recv_a)]
    recv_b = _exchange_chips(parts)
    halves = [_add_chips(p, r, s_idx) for p, r in zip(parts, recv_b)]
    reduced = dict(zip(LARGE, _share_halves(halves)))

    small_g = [grads[n].reshape(params[n].shape) for n in SMALL] + [grads["conv_w"]]
    n_small = sum(int(np.prod(g.shape)) for g in small_g)
    small_rows = -(-n_small // (8 * 128)) * 8
    summed = _unpack(_allreduce_small(_pack(small_g, small_rows)),
                     [params[n].shape for n in SMALL] + [(3, D_FF)])
    for n, g in zip(SMALL, summed[:-1]):
        reduced[n] = g
    reduced["conv_w"] = lax.dynamic_slice_in_dim(summed[-1], s_idx[0] * SHARD_FF, SHARD_FF, axis=1)[None]

    out_g, out_d, out_m, out_v = {}, {}, {}, {}
    for n in LARGE:
        shp = params[n].shape
        d, nm, nv = _adamw(params[n][0], reduced[n], mom[n][0], var[n][0], name=f"adamw_{n}")
        out_g[n], out_d[n], out_m[n], out_v[n] = (reduced[n].reshape(shp), d.reshape(shp), nm.reshape(shp),
                                                   nv.reshape(shp))
    small_names = SMALL + ["conv_w"]
    rows_s = -(-sum(int(np.prod(params[n].shape)) for n in small_names) // (8 * 128)) * 8
    d, nm, nv = _adamw(_pack([params[n] for n in small_names], rows_s), _pack([reduced[n] for n in small_names], rows_s),
                       _pack([mom[n] for n in small_names], rows_s), _pack([var[n] for n in small_names], rows_s),
                       name="adamw_small")
    shapes = [params[n].shape for n in small_names]
    for n, dd, mm, vv in zip(small_names, _unpack(d, shapes), _unpack(nm, shapes), _unpack(nv, shapes)):
        out_g[n], out_d[n], out_m[n], out_v[n] = reduced[n], dd, mm, vv

    return (loss, grad_x, *[out_g[n] for n in ORDER], *[out_d[n] for n in ORDER],
            *[out_m[n] for n in ORDER], *[out_v[n] for n in ORDER])
```

```python
import functools
import math

import numpy as np
import jax
import jax.numpy as jnp
from jax import lax
from jax.experimental import pallas as pl
from jax.experimental.pallas import tpu as pltpu

F32 = jnp.float32
BF16 = jnp.bfloat16
MESH = pl.DeviceIdType.MESH

D_MODEL = 1024
SEQ = 2048
HEAD_DIM = 64
A_GROUPS = 4
A_WIDTH = 256
B_HEADS = 12
B_WIDTH = 768
CHUNK = 128
DILATED = ((128, 1), (512, 4), (2048, 16))
NUM_BUCKETS = 32
MAX_DISTANCE = 2048
D_FF = 2816
IN_COLS = 2816
NORM_EPS = 1e-6
NEG_INF = -1e30
N_SHARD = 4
SHARD_FF = D_FF // N_SHARD
LANE_BLOCK = 256
VMEM_LIMIT = 56 * 1024 * 1024

ADAM_LR = 0.001
ADAM_B1 = 0.9
ADAM_B2 = 0.999
ADAM_EPS = 1e-08
ADAM_WD = 0.01
ADAM_STEP = 10

GELU_C = math.sqrt(2.0 / math.pi)
GELU_A = 0.044715


def _params(sem=None):
    return pltpu.CompilerParams(dimension_semantics=sem, vmem_limit_bytes=VMEM_LIMIT)


def _dot(a, b, precision=None):
    return jnp.dot(a, b, preferred_element_type=F32, precision=precision)


def _dot_nt(a, b):
    return lax.dot_general(a, b, (((1,), (1,)), ((), ())), preferred_element_type=F32)


def _dot_tn(a, b):
    return lax.dot_general(a, b, (((0,), (0,)), ((), ())), preferred_element_type=F32)


def _gelu(x):
    t = jnp.tanh(GELU_C * (x + GELU_A * (x * x * x)))
    return 0.5 * x * (1.0 + t)


def _gelu_and_grad(x):
    x2 = x * x
    t = jnp.tanh(GELU_C * (x + GELU_A * (x2 * x)))
    g = 0.5 * x * (1.0 + t)
    dg = 0.5 * (1.0 + t) + 0.5 * x * (1.0 - t * t) * (GELU_C * (1.0 + 3.0 * GELU_A * x2))
    return g, dg


def _mm(a, b, *, dims, tm, tn, tk, out_dtype, name, add=None):
    if dims == "nn":
        m, k = a.shape
        n = b.shape[1]
        a_spec = pl.BlockSpec((tm, tk), lambda i, j, kk: (i, kk))
        b_spec = pl.BlockSpec((tk, tn), lambda i, j, kk: (kk, j))
        dot = _dot
    elif dims == "nt":
        m, k = a.shape
        n = b.shape[0]
        a_spec = pl.BlockSpec((tm, tk), lambda i, j, kk: (i, kk))
        b_spec = pl.BlockSpec((tn, tk), lambda i, j, kk: (j, kk))
        dot = _dot_nt
    else:
        k, m = a.shape
        n = b.shape[1]
        a_spec = pl.BlockSpec((tk, tm), lambda i, j, kk: (kk, i))
        b_spec = pl.BlockSpec((tk, tn), lambda i, j, kk: (kk, j))
        dot = _dot_tn
    assert m % tm == 0 and n % tn == 0 and k % tk == 0, (name, m, n, k)
    nk = k // tk
    has_add = add is not None

    def body(*refs):
        if has_add:
            a_ref, b_ref, add_ref, o_ref, acc_ref = refs
        else:
            a_ref, b_ref, o_ref, acc_ref = refs
        kk = pl.program_id(2)

        @pl.when(kk == 0)
        def _():
            if has_add:
                acc_ref[...] = add_ref[...].astype(F32)
            else:
                acc_ref[...] = jnp.zeros_like(acc_ref)

        acc_ref[...] += dot(a_ref[...].astype(BF16), b_ref[...].astype(BF16))

        @pl.when(kk == nk - 1)
        def _():
            o_ref[...] = acc_ref[...].astype(out_dtype)

    in_specs = [a_spec, b_spec]
    args = [a, b]
    if has_add:
        in_specs.append(pl.BlockSpec((tm, tn), lambda i, j, kk: (i, j)))
        args.append(add)
    return pl.pallas_call(
        body,
        grid=(m // tm, n // tn, nk),
        in_specs=in_specs,
        out_specs=pl.BlockSpec((tm, tn), lambda i, j, kk: (i, j)),
        out_shape=jax.ShapeDtypeStruct((m, n), out_dtype),
        scratch_shapes=[pltpu.VMEM((tm, tn), F32)],
        compiler_params=_params(("parallel", "parallel", "arbitrary")),
        name=name,
    )(*args)


ROW_TILE = 512


def _row_spec(width=D_MODEL):
    return pl.BlockSpec((ROW_TILE, width), lambda i: (i, 0))


def _vec_spec(width=D_MODEL):
    return pl.BlockSpec((1, width), lambda i: (0, 0))


def _rstd(v):
    return lax.rsqrt(jnp.mean(v * v, axis=-1, keepdims=True) + NORM_EPS)


def _rms_fwd(x, g):
    m = x.shape[0]

    def body(x_ref, g_ref, h_ref):
        xv = x_ref[...]
        h_ref[...] = (xv * _rstd(xv) * g_ref[...]).astype(BF16)

    return pl.pallas_call(
        body, grid=(m // ROW_TILE,),
        in_specs=[_row_spec(), _vec_spec()],
        out_specs=_row_spec(),
        out_shape=jax.ShapeDtypeStruct((m, D_MODEL), BF16),
        compiler_params=_params(("parallel",)), name="rms_fwd",
    )(x, g)


def _mid_fwd(x0, y1, g2, g3):
    m = x0.shape[0]

    def body(x0_ref, y1_ref, g2_ref, g3_ref, x1_ref, h2_ref):
        y1v = y1_ref[...]
        x1 = x0_ref[...] + y1v * _rstd(y1v) * g2_ref[...]
        x1_ref[...] = x1
        h2_ref[...] = (x1 * _rstd(x1) * g3_ref[...]).astype(BF16)

    return pl.pallas_call(
        body, grid=(m // ROW_TILE,),
        in_specs=[_row_spec(), _row_spec(), _vec_spec(), _vec_spec()],
        out_specs=[_row_spec(), _row_spec()],
        out_shape=[jax.ShapeDtypeStruct((m, D_MODEL), F32), jax.ShapeDtypeStruct((m, D_MODEL), BF16)],
        compiler_params=_params(("parallel",)), name="mid_fwd",
    )(x0, y1, g2, g3)


def _rms_bwd_rows(dout, v, g):
    r = _rstd(v)
    n = v * r
    dn = dout * g
    dv = r * (dn - n * jnp.mean(dn * n, axis=-1, keepdims=True))
    dg = jnp.sum(dout * n, axis=0, keepdims=True)
    return dv, dg


def _loss_head(x1, y2, tgt, g4):
    m = x1.shape[0]

    def body(x1_ref, y2_ref, t_ref, g4_ref, dx2_ref, dy2_ref, dg4_ref, loss_ref):
        i = pl.program_id(0)

        @pl.when(i == 0)
        def _():
            dg4_ref[...] = jnp.zeros_like(dg4_ref)
            loss_ref[...] = jnp.zeros_like(loss_ref)

        y2v = y2_ref[...]
        g4 = g4_ref[...]
        x2 = x1_ref[...] + y2v * _rstd(y2v) * g4
        err = x2 - t_ref[...]
        loss_ref[...] += 0.5 * jnp.sum(jnp.mean(err * err, axis=-1, keepdims=True), axis=0, keepdims=True)
        dx2 = err * (1.0 / D_MODEL)
        dx2_ref[...] = dx2
        dy2, dg4 = _rms_bwd_rows(dx2, y2v, g4)
        dy2_ref[...] = dy2.astype(BF16)
        dg4_ref[...] += dg4

    return pl.pallas_call(
        body, grid=(m // ROW_TILE,),
        in_specs=[_row_spec(), _row_spec(), _row_spec(), _vec_spec()],
        out_specs=[_row_spec(), _row_spec(), _vec_spec(), pl.BlockSpec((1, 1), lambda i: (0, 0))],
        out_shape=[jax.ShapeDtypeStruct((m, D_MODEL), F32), jax.ShapeDtypeStruct((m, D_MODEL), BF16),
                   jax.ShapeDtypeStruct((1, D_MODEL), F32), jax.ShapeDtypeStruct((1, 1), F32)],
        compiler_params=_params(("arbitrary",)), name="loss_head",
    )(x1, y2, tgt, g4)


def _mid_bwd(x1, y1, dh2, dx2, g2, g3):
    m = x1.shape[0]

    def body(x1_ref, y1_ref, dh2_ref, dx2_ref, g2_ref, g3_ref, dx1_ref, dy1_ref, dg2_ref, dg3_ref):
        i = pl.program_id(0)

        @pl.when(i == 0)
        def _():
            dg2_ref[...] = jnp.zeros_like(dg2_ref)
            dg3_ref[...] = jnp.zeros_like(dg3_ref)

        d3, dg3 = _rms_bwd_rows(dh2_ref[...], x1_ref[...], g3_ref[...])
        dx1 = dx2_ref[...] + d3
        dx1_ref[...] = dx1
        dy1, dg2 = _rms_bwd_rows(dx1, y1_ref[...], g2_ref[...])
        dy1_ref[...] = dy1.astype(BF16)
        dg2_ref[...] += dg2
        dg3_ref[...] += dg3

    return pl.pallas_call(
        body, grid=(m // ROW_TILE,),
        in_specs=[_row_spec(), _row_spec(), _row_spec(), _row_spec(), _vec_spec(), _vec_spec()],
        out_specs=[_row_spec(), _row_spec(), _vec_spec(), _vec_spec()],
        out_shape=[jax.ShapeDtypeStruct((m, D_MODEL), F32), jax.ShapeDtypeStruct((m, D_MODEL), BF16),
                   jax.ShapeDtypeStruct((1, D_MODEL), F32), jax.ShapeDtypeStruct((1, D_MODEL), F32)],
        compiler_params=_params(("arbitrary",)), name="mid_bwd",
    )(x1, y1, dh2, dx2, g2, g3)


def _in_bwd(x0, dh1, dx1, g1):
    m = x0.shape[0]

    def body(x0_ref, dh1_ref, dx1_ref, g1_ref, dx0_ref, dg1_ref):
        i = pl.program_id(0)

        @pl.when(i == 0)
        def _():
            dg1_ref[...] = jnp.zeros_like(dg1_ref)

        d1, dg1 = _rms_bwd_rows(dh1_ref[...], x0_ref[...], g1_ref[...])
        dx0_ref[...] = dx1_ref[...] + d1
        dg1_ref[...] += dg1

    return pl.pallas_call(
        body, grid=(m // ROW_TILE,),
        in_specs=[_row_spec(), _row_spec(), _row_spec(), _vec_spec()],
        out_specs=[_row_spec(), _vec_spec()],
        out_shape=[jax.ShapeDtypeStruct((m, D_MODEL), F32), jax.ShapeDtypeStruct((1, D_MODEL), F32)],
        compiler_params=_params(("arbitrary",)), name="in_bwd",
    )(x0, dh1, dx1, g1)


GATE_ROWS = 512


def _group_mean_matrix():
    p = np.zeros((A_WIDTH, A_WIDTH), np.float32)
    for g in range(A_GROUPS):
        p[g * HEAD_DIM:(g + 1) * HEAD_DIM, g * HEAD_DIM:(g + 1) * HEAD_DIM] = 1.0 / HEAD_DIM
    return jnp.asarray(p)


def _group_masks(width=A_WIDTH):
    lane = lax.broadcasted_iota(jnp.int32, (1, width), 1)
    return [(lane >= g * HEAD_DIM) & (lane < (g + 1) * HEAD_DIM) for g in range(width // HEAD_DIM)]


def _layernorm_groups(vg, pavg):
    hi = lax.Precision.HIGHEST
    mu = _dot(vg, pavg, hi)
    xc = vg - mu
    var = _dot(xc * xc, pavg, hi)
    rstd = lax.rsqrt(var + NORM_EPS)
    return xc * rstd, rstd


def _spatial_mix(w_bf, vn_chunk_bf, masks, bz):
    z = bz
    for g in range(A_GROUPS):
        z = z + jnp.where(masks[g], _dot(w_bf[g], vn_chunk_bf), 0.0)
    return z


def _gate_fwd(proj, ln_g, ln_b, w_s, bz):
    m = proj.shape[0]
    pavg = _group_mean_matrix()

    def body(u_ref, v_ref, lg_ref, lb_ref, w_ref, bz_ref, p_ref, a_ref):
        masks = _group_masks()
        row = lax.broadcasted_iota(jnp.int32, (CHUNK, CHUNK), 0)
        col = lax.broadcasted_iota(jnp.int32, (CHUNK, CHUNK), 1)
        w_bf = [jnp.where(row >= col, w_ref[g], 0.0).astype(BF16) for g in range(A_GROUPS)]
        ug = _gelu(u_ref[...])
        vhat, _ = _layernorm_groups(_gelu(v_ref[...]), p_ref[...])
        vn = vhat * lg_ref[...] + lb_ref[...]
        bz = bz_ref[...]
        for c in range(GATE_ROWS // CHUNK):
            sl = slice(c * CHUNK, (c + 1) * CHUNK)
            z = _spatial_mix(w_bf, vn[sl].astype(BF16), masks, bz)
            a_ref[sl, :] = (ug[sl] * z).astype(BF16)

    full = lambda shape: pl.BlockSpec(shape, lambda i: tuple(0 for _ in shape))
    return pl.pallas_call(
        body, grid=(m // GATE_ROWS,),
        in_specs=[pl.BlockSpec((GATE_ROWS, A_WIDTH), lambda i: (i, 0)),
                  pl.BlockSpec((GATE_ROWS, A_WIDTH), lambda i: (i, 1)),
                  full((1, A_WIDTH)), full((1, A_WIDTH)), full((A_GROUPS, CHUNK, CHUNK)),
                  full((CHUNK, A_WIDTH)), full((A_WIDTH, A_WIDTH))],
        out_specs=pl.BlockSpec((GATE_ROWS, A_WIDTH), lambda i: (i, 0)),
        out_shape=jax.ShapeDtypeStruct((m, A_WIDTH), BF16),
        compiler_params=_params(("parallel",)), name="gate_fwd",
    )(proj, proj, ln_g, ln_b, w_s, bz, pavg)


def _gate_bwd(proj, dmix, ln_g, ln_b, w_s, w_st, bz):
    m = proj.shape[0]
    pavg = _group_mean_matrix()
    nsteps = m // GATE_ROWS

    def body(u_ref, v_ref, da_ref, lg_ref, lb_ref, w_ref, wt_ref, bz_ref, p_ref,
             duv_ref, dlg_ref, dlb_ref, dw_ref, dbz_ref):
        i = pl.program_id(0)

        @pl.when(i == 0)
        def _():
            dlg_ref[...] = jnp.zeros_like(dlg_ref)
            dlb_ref[...] = jnp.zeros_like(dlb_ref)
            dw_ref[...] = jnp.zeros_like(dw_ref)
            dbz_ref[...] = jnp.zeros_like(dbz_ref)

        hi = lax.Precision.HIGHEST
        masks = _group_masks()
        row = lax.broadcasted_iota(jnp.int32, (CHUNK, CHUNK), 0)
        col = lax.broadcasted_iota(jnp.int32, (CHUNK, CHUNK), 1)
        tril = row >= col
        w_bf = [jnp.where(tril, w_ref[g], 0.0).astype(BF16) for g in range(A_GROUPS)]
        wt_bf = [jnp.where(col >= row, wt_ref[g], 0.0).astype(BF16) for g in range(A_GROUPS)]
        pavg_v = p_ref[...]
        lg = lg_ref[...]
        ug, dug = _gelu_and_grad(u_ref[...])
        vg, dvg_dx = _gelu_and_grad(v_ref[...])
        vhat, rstd = _layernorm_groups(vg, pavg_v)
        vn = vhat * lg + lb_ref[...]
        da = da_ref[...]
        bz = bz_ref[...]
        for c in range(GATE_ROWS // CHUNK):
            sl = slice(c * CHUNK, (c + 1) * CHUNK)
            vn_bf = vn[sl].astype(BF16)
            z = _spatial_mix(w_bf, vn_bf, masks, bz)
            dz = da[sl] * ug[sl]
            duv_ref[sl, 0:A_WIDTH] = da[sl] * z * dug[sl]
            dbz_ref[...] += dz
            dz_bf = dz.astype(BF16)
            dvn = jnp.zeros((CHUNK, A_WIDTH), F32)
            for g in range(A_GROUPS):
                dz_g = jnp.where(masks[g], dz, 0.0).astype(BF16)
                dw_ref[g] += jnp.where(tril, _dot_nt(dz_g, vn_bf), 0.0)
                dvn = dvn + jnp.where(masks[g], _dot(wt_bf[g], dz_bf), 0.0)
            vh = vhat[sl]
            dlb_ref[...] += jnp.sum(dvn, axis=0, keepdims=True)
            dlg_ref[...] += jnp.sum(dvn * vh, axis=0, keepdims=True)
            dvh = dvn * lg
            m1 = _dot(dvh, pavg_v, hi)
            m2 = _dot(dvh * vh, pavg_v, hi)
            duv_ref[sl, A_WIDTH:2 * A_WIDTH] = rstd[sl] * (dvh - m1 - vh * m2) * dvg_dx[sl]

        @pl.when(i == nsteps - 1)
        def _():
            dbz_ref[...] = _dot(dbz_ref[...], pavg_v * float(HEAD_DIM), hi)

    full = lambda shape: pl.BlockSpec(shape, lambda i: tuple(0 for _ in shape))
    return pl.pallas_call(
        body, grid=(nsteps,),
        in_specs=[pl.BlockSpec((GATE_ROWS, A_WIDTH), lambda i: (i, 0)),
                  pl.BlockSpec((GATE_ROWS, A_WIDTH), lambda i: (i, 1)),
                  pl.BlockSpec((GATE_ROWS, A_WIDTH), lambda i: (i, 0)),
                  full((1, A_WIDTH)), full((1, A_WIDTH)), full((A_GROUPS, CHUNK, CHUNK)),
                  full((A_GROUPS, CHUNK, CHUNK)), full((CHUNK, A_WIDTH)), full((A_WIDTH, A_WIDTH))],
        out_specs=[pl.BlockSpec((GATE_ROWS, 2 * A_WIDTH), lambda i: (i, 0)),
                   full((1, A_WIDTH)), full((1, A_WIDTH)), full((A_GROUPS, CHUNK, CHUNK)),
                   full((CHUNK, A_WIDTH))],
        out_shape=[jax.ShapeDtypeStruct((m, 2 * A_WIDTH), F32),
                   jax.ShapeDtypeStruct((1, A_WIDTH), F32), jax.ShapeDtypeStruct((1, A_WIDTH), F32),
                   jax.ShapeDtypeStruct((A_GROUPS, CHUNK, CHUNK), F32),
                   jax.ShapeDtypeStruct((CHUNK, A_WIDTH), F32)],
        compiler_params=_params(("arbitrary",)), name="gate_bwd",
    )(proj, proj, dmix, ln_g, ln_b, w_s, w_st, bz, pavg)


Q_BLOCK = 128
Q_COL, K_COL, V_COL = 2, 5, 8
N_COLBLK = IN_COLS // LANE_BLOCK
HEAD_BLOCKS = B_WIDTH // LANE_BLOCK
HEADS_PER_BLOCK = LANE_BLOCK // HEAD_DIM


def _t5_bucket_np(dist, dtype):
    max_exact = NUM_BUCKETS // 2
    d = np.maximum(dist, 1).astype(dtype)
    large = max_exact + (np.log(d / dtype(max_exact)) / dtype(math.log(MAX_DISTANCE / max_exact))
                         * dtype(NUM_BUCKETS - max_exact))
    large = np.minimum(large.astype(np.int32), NUM_BUCKETS - 1)
    return np.where(dist < max_exact, dist, large)


def _bucket_tables():
    i = np.arange(Q_BLOCK)[:, None]
    j = np.arange(Q_BLOCK)[None, :]
    tables = []
    for _, dil in DILATED:
        rel_prev = Q_BLOCK + i - j
        rel_cur = i - j
        rel = np.concatenate([rel_prev, rel_cur], axis=1)
        valid = np.concatenate([rel_prev <= Q_BLOCK, rel_cur >= 0], axis=1)
        dist = np.maximum(rel, 0) * dil
        b32 = _t5_bucket_np(dist, np.float32)
        b64 = _t5_bucket_np(dist, np.float64)
        assert np.array_equal(b32, b64)
        tables.append(np.where(valid, b32, -1).astype(np.int32))
    return np.stack(tables)


def _bias_tables(rel_bias, buckets_np):
    present = [sorted(set(int(v) for v in np.unique(buckets_np[c]) if v >= 0)) for c in range(len(DILATED))]

    def body(rb_ref, bk_ref, o_ref):
        for c in range(len(DILATED)):
            bk = bk_ref[c]
            for h in range(B_HEADS):
                acc = jnp.full((Q_BLOCK, 2 * Q_BLOCK), NEG_INF, F32)
                for b in present[c]:
                    acc = jnp.where(bk == b, rb_ref[b, h], acc)
                o_ref[c, h] = acc

    return pl.pallas_call(
        body,
        in_specs=[pl.BlockSpec(memory_space=pltpu.SMEM), pl.BlockSpec(memory_space=pltpu.VMEM)],
        out_specs=pl.BlockSpec(memory_space=pltpu.VMEM),
        out_shape=jax.ShapeDtypeStruct((len(DILATED), B_HEADS, Q_BLOCK, 2 * Q_BLOCK), F32),
        compiler_params=_params(), name="bias_tables",
    )(rel_bias, jnp.asarray(buckets_np))


def _head_masks():
    lane = lax.broadcasted_iota(jnp.int32, (1, LANE_BLOCK), 1)
    return [(lane >= h * HEAD_DIM) & (lane < (h + 1) * HEAD_DIM) for h in range(HEADS_PER_BLOCK)]


def _attn_fwd(proj, bias, dil, batch):
    m = proj.shape[0]
    tr = SEQ // dil
    nb = tr // Q_BLOCK
    proj3 = proj.reshape(batch, tr, dil * IN_COLS)

    def body(q_ref, k_ref, v_ref, b_ref, o_ref, l_ref):
        masks = _head_masks()

        def block(n, carry):
            r0 = pl.multiple_of(n * Q_BLOCK, Q_BLOCK)
            rows = pl.ds(r0, Q_BLOCK)
            q = q_ref[rows, :] * 0.125
            kc = k_ref[rows, :].astype(BF16)
            vc = v_ref[rows, :].astype(BF16)
            if nb > 1:
                p0 = pl.multiple_of(jnp.maximum(n - 1, 0) * Q_BLOCK, Q_BLOCK)
                kp = k_ref[pl.ds(p0, Q_BLOCK), :].astype(BF16)
                vp = v_ref[pl.ds(p0, Q_BLOCK), :].astype(BF16)
            o_acc = jnp.zeros((Q_BLOCK, LANE_BLOCK), F32)
            l_acc = jnp.zeros((Q_BLOCK, LANE_BLOCK), F32)
            for h in range(HEADS_PER_BLOCK):
                qh = jnp.where(masks[h], q, 0.0).astype(BF16)
                sc = _dot_nt(qh, kc) + b_ref[h, :, Q_BLOCK:]
                mx = jnp.max(sc, axis=1, keepdims=True)
                if nb > 1:
                    sp = _dot_nt(qh, kp) + jnp.where(n == 0, NEG_INF, b_ref[h, :, :Q_BLOCK])
                    mx = jnp.maximum(mx, jnp.max(sp, axis=1, keepdims=True))
                pc = jnp.exp(sc - mx)
                den = jnp.sum(pc, axis=1, keepdims=True)
                oh = _dot(pc.astype(BF16), vc)
                if nb > 1:
                    pp = jnp.exp(sp - mx)
                    den = den + jnp.sum(pp, axis=1, keepdims=True)
                    oh = oh + _dot(pp.astype(BF16), vp)
                o_acc = jnp.where(masks[h], oh / den, o_acc)
                l_acc = jnp.where(masks[h], mx + jnp.log(den), l_acc)
            o_ref[rows, :] = o_acc
            l_ref[rows, :] = l_acc
            return carry

        if nb > 1:
            lax.fori_loop(0, nb, block, 0)
        else:
            block(0, 0)

    def in_spec(col0):
        return pl.BlockSpec((None, tr, LANE_BLOCK), lambda b, r, g: (b, 0, r * N_COLBLK + col0 + g))

    out_spec = pl.BlockSpec((None, tr, LANE_BLOCK), lambda b, r, g: (b, 0, r * HEAD_BLOCKS + g))
    out_sds = jax.ShapeDtypeStruct((batch, tr, dil * B_WIDTH), F32)
    o, lse = pl.pallas_call(
        body, grid=(batch, dil, HEAD_BLOCKS),
        in_specs=[in_spec(Q_COL), in_spec(K_COL), in_spec(V_COL),
                  pl.BlockSpec((HEADS_PER_BLOCK, Q_BLOCK, 2 * Q_BLOCK), lambda b, r, g: (g, 0, 0))],
        out_specs=[out_spec, out_spec],
        out_shape=[out_sds, out_sds],
        compiler_params=_params(("parallel", "parallel", "parallel")), name=f"attn_fwd_d{dil}",
    )(proj3, proj3, proj3, bias)
    return o.reshape(m, B_WIDTH), lse.reshape(m, B_WIDTH)


def _attn_combine(outs, lses):
    m = outs[0].shape[0]
    nc = len(outs)

    def body(*refs):
        o_refs, l_refs = refs[:nc], refs[nc:2 * nc]
        of_ref, ob_ref, lt_ref = refs[2 * nc:]
        ls = [r[...] for r in l_refs]
        mx = functools.reduce(jnp.maximum, ls)
        ws = [jnp.exp(l - mx) for l in ls]
        tot = functools.reduce(lambda a, b: a + b, ws)
        inv = 1.0 / tot
        o = functools.reduce(lambda a, b: a + b, [w * inv * r[...] for w, r in zip(ws, o_refs)])
        of_ref[...] = o
        ob_ref[...] = o.astype(BF16)
        lt_ref[...] = mx + jnp.log(tot)

    spec = pl.BlockSpec((ROW_TILE, B_WIDTH), lambda i: (i, 0))
    return pl.pallas_call(
        body, grid=(m // ROW_TILE,),
        in_specs=[spec] * (2 * nc), out_specs=[spec, spec, spec],
        out_shape=[jax.ShapeDtypeStruct((m, B_WIDTH), F32), jax.ShapeDtypeStruct((m, B_WIDTH), BF16),
                   jax.ShapeDtypeStruct((m, B_WIDTH), F32)],
        compiler_params=_params(("parallel",)), name="attn_combine",
    )(*outs, *lses)


def _attn_bwd(proj, dmix, o, lse, bias, dil, batch):
    m = proj.shape[0]
    tr = SEQ // dil
    nb = tr // Q_BLOCK
    proj3 = proj.reshape(batch, tr, dil * IN_COLS)
    dmix3 = dmix.reshape(batch, tr, dil * D_MODEL)
    o3 = o.reshape(batch, tr, dil * B_WIDTH)
    l3 = lse.reshape(batch, tr, dil * B_WIDTH)

    def body(q_ref, k_ref, v_ref, do_ref, o_ref, l_ref, b_ref, dq_ref, dk_ref, dv_ref, ds_ref):
        first = (pl.program_id(1) == 0) & (pl.program_id(2) == 0)

        @pl.when(first)
        def _():
            ds_ref[...] = jnp.zeros_like(ds_ref)

        dk_ref[...] = jnp.zeros_like(dk_ref)
        dv_ref[...] = jnp.zeros_like(dv_ref)
        masks = _head_masks()

        def block(n, carry):
            r0 = pl.multiple_of(n * Q_BLOCK, Q_BLOCK)
            rows = pl.ds(r0, Q_BLOCK)
            q = q_ref[rows, :] * 0.125
            kc = k_ref[rows, :].astype(BF16)
            vc = v_ref[rows, :].astype(BF16)
            do = do_ref[rows, :]
            ov = o_ref[rows, :]
            lv = l_ref[rows, :]
            if nb > 1:
                p0 = pl.multiple_of(jnp.maximum(n - 1, 0) * Q_BLOCK, Q_BLOCK)
                prow = pl.ds(p0, Q_BLOCK)
                kp = k_ref[prow, :].astype(BF16)
                vp = v_ref[prow, :].astype(BF16)
                dkp = jnp.zeros((Q_BLOCK, LANE_BLOCK), F32)
                dvp = jnp.zeros((Q_BLOCK, LANE_BLOCK), F32)
            dq = jnp.zeros((Q_BLOCK, LANE_BLOCK), F32)
            dkc = jnp.zeros((Q_BLOCK, LANE_BLOCK), F32)
            dvc = jnp.zeros((Q_BLOCK, LANE_BLOCK), F32)
            for h in range(HEADS_PER_BLOCK):
                qh = jnp.where(masks[h], q, 0.0).astype(BF16)
                doh = jnp.where(masks[h], do, 0.0)
                doh_bf = doh.astype(BF16)
                lrow = jnp.max(jnp.where(masks[h], lv, -3e38), axis=1, keepdims=True)
                drow = jnp.sum(doh * ov, axis=1, keepdims=True)
                pc = jnp.exp(_dot_nt(qh, kc) + b_ref[h, :, Q_BLOCK:] - lrow)
                dsc = pc * (_dot_nt(doh_bf, vc) - drow)
                ds_ref[h, :, Q_BLOCK:] += dsc
                dsc_bf = dsc.astype(BF16)
                dqh = _dot(dsc_bf, kc)
                dkc = dkc + _dot_tn(dsc_bf, qh)
                dvc = dvc + _dot_tn(pc.astype(BF16), doh_bf)
                if nb > 1:
                    bp = jnp.where(n == 0, NEG_INF, b_ref[h, :, :Q_BLOCK])
                    pp = jnp.exp(_dot_nt(qh, kp) + bp - lrow)
                    dsp = pp * (_dot_nt(doh_bf, vp) - drow)
                    ds_ref[h, :, :Q_BLOCK] += dsp
                    dsp_bf = dsp.astype(BF16)
                    dqh = dqh + _dot(dsp_bf, kp)
                    dkp = dkp + _dot_tn(dsp_bf, qh)
                    dvp = dvp + _dot_tn(pp.astype(BF16), doh_bf)
                dq = jnp.where(masks[h], dqh, dq)
            dq_ref[rows, :] = dq * 0.125
            dk_ref[rows, :] += dkc
            dv_ref[rows, :] += dvc
            if nb > 1:
                dk_ref[prow, :] += dkp
                dv_ref[prow, :] += dvp
            return carry

        if nb > 1:
            lax.fori_loop(0, nb, block, 0)
        else:
            block(0, 0)

    def in_spec(col0):
        return pl.BlockSpec((None, tr, LANE_BLOCK), lambda g, b, r: (b, 0, r * N_COLBLK + col0 + g))

    do_spec = pl.BlockSpec((None, tr, LANE_BLOCK), lambda g, b, r: (b, 0, r * (D_MODEL // LANE_BLOCK) + 1 + g))
    hd_spec = pl.BlockSpec((None, tr, LANE_BLOCK), lambda g, b, r: (b, 0, r * HEAD_BLOCKS + g))
    tbl_spec = pl.BlockSpec((HEADS_PER_BLOCK, Q_BLOCK, 2 * Q_BLOCK), lambda g, b, r: (g, 0, 0))
    out_sds = jax.ShapeDtypeStruct((batch, tr, dil * B_WIDTH), F32)
    dq, dk, dv, ds = pl.pallas_call(
        body, grid=(HEAD_BLOCKS, batch, dil),
        in_specs=[in_spec(Q_COL), in_spec(K_COL), in_spec(V_COL), do_spec, hd_spec, hd_spec, tbl_spec],
        out_specs=[hd_spec, hd_spec, hd_spec, tbl_spec],
        out_shape=[out_sds, out_sds, out_sds, jax.ShapeDtypeStruct((B_HEADS, Q_BLOCK, 2 * Q_BLOCK), F32)],
        compiler_params=_params(("parallel", "arbitrary", "arbitrary")), name=f"attn_bwd_d{dil}",
    )(proj3, proj3, proj3, dmix3, o3, l3, bias)
    return dq.reshape(m, B_WIDTH), dk.reshape(m, B_WIDTH), dv.reshape(m, B_WIDTH), ds


def _rel_bias_grad(ds_list, buckets_np):
    nc = len(ds_list)
    present = [sorted(set(int(v) for v in np.unique(buckets_np[c]) if v >= 0)) for c in range(nc)]

    def body(*refs):
        bk_ref = refs[0]
        ds_refs = refs[1:1 + nc]
        o_ref, acc_ref = refs[1 + nc:]
        acc_ref[...] = jnp.zeros_like(acc_ref)
        for c in range(nc):
            bk = bk_ref[c]
            for h in range(B_HEADS):
                dsv = ds_refs[c][h]
                for b in present[c]:
                    part = jnp.sum(jnp.where(bk == b, dsv, 0.0), axis=0, keepdims=True)
                    acc_ref[pl.ds(h * NUM_BUCKETS + b, 1), :] += part
        o_ref[...] = jnp.sum(acc_ref[...], axis=1, keepdims=True)

    vm = pl.BlockSpec(memory_space=pltpu.VMEM)
    return pl.pallas_call(
        body, in_specs=[vm] * (1 + nc), out_specs=vm,
        out_shape=jax.ShapeDtypeStruct((B_HEADS * NUM_BUCKETS, 1), F32),
        scratch_shapes=[pltpu.VMEM((B_HEADS * NUM_BUCKETS, 2 * Q_BLOCK), F32)],
        compiler_params=_params(), name="rel_bias_grad",
    )(jnp.asarray(buckets_np), *ds_list)


def _assemble_dproj(duv, dqs, dks, dvs):
    m = duv.shape[0]
    nc = len(dqs)

    def body(*refs):
        duv_ref = refs[0]
        groups = [refs[1:1 + nc], refs[1 + nc:1 + 2 * nc], refs[1 + 2 * nc:1 + 3 * nc]]
        o_ref = refs[-1]
        j = pl.program_id(1)

        @pl.when(j < 2)
        def _():
            o_ref[...] = duv_ref[...].astype(BF16)

        for gi, grp in enumerate(groups):
            lo = 2 + gi * HEAD_BLOCKS

            @pl.when((j >= lo) & (j < lo + HEAD_BLOCKS))
            def _(grp=grp):
                o_ref[...] = functools.reduce(lambda a, b: a + b, [r[...] for r in grp]).astype(BF16)

    def grp_spec(lo):
        return pl.BlockSpec((ROW_TILE, LANE_BLOCK),
                            lambda i, j: (i, jnp.clip(j - lo, 0, HEAD_BLOCKS - 1)))

    in_specs = [pl.BlockSpec((ROW_TILE, LANE_BLOCK), lambda i, j: (i, jnp.minimum(j, 1)))]
    for gi in range(3):
        in_specs += [grp_spec(2 + gi * HEAD_BLOCKS)] * nc
    return pl.pallas_call(
        body, grid=(m // ROW_TILE, N_COLBLK),
        in_specs=in_specs,
        out_specs=pl.BlockSpec((ROW_TILE, LANE_BLOCK), lambda i, j: (i, j)),
        out_shape=jax.ShapeDtypeStruct((m, IN_COLS), BF16),
        compiler_params=_params(("parallel", "arbitrary")), name="assemble_dproj",
    )(duv, *dqs, *dks, *dvs)


def _shift_down(x, k):
    row = lax.broadcasted_iota(jnp.int32, x.shape, 0)
    return jnp.where(row >= k, pltpu.roll(x, k, 0), 0.0)


def _shift_up(x, k):
    n = x.shape[0]
    row = lax.broadcasted_iota(jnp.int32, x.shape, 0)
    return jnp.where(row < n - k, pltpu.roll(x, n - k, 0), 0.0)


def _convgate_fwd(gate, up, conv_w, conv_b, batch):
    m = gate.shape[0]

    def body(g_ref, u_ref, w_ref, b_ref, a_ref):
        g = g_ref[...]
        w = w_ref[...]
        c = b_ref[...] + w[0:1] * _shift_down(g, 2) + w[1:2] * _shift_down(g, 1) + w[2:3] * g
        a_ref[...] = (_gelu(c) * u_ref[...]).astype(BF16)

    blk = pl.BlockSpec((SEQ, LANE_BLOCK), lambda b, j: (b, j))
    return pl.pallas_call(
        body, grid=(batch, D_FF // LANE_BLOCK),
        in_specs=[blk, blk, pl.BlockSpec((3, LANE_BLOCK), lambda b, j: (0, j)),
                  pl.BlockSpec((1, LANE_BLOCK), lambda b, j: (0, j))],
        out_specs=blk,
        out_shape=jax.ShapeDtypeStruct((m, D_FF), BF16),
        compiler_params=_params(("parallel", "parallel")), name="convgate_fwd",
    )(gate, up, conv_w, conv_b)


def _convgate_bwd(gate, up, dact, conv_w, conv_b, batch):
    m = gate.shape[0]

    def body(g_ref, u_ref, da_ref, w_ref, b_ref, dg_ref, du_ref, dw_ref, db_ref):
        @pl.when(pl.program_id(1) == 0)
        def _():
            dw_ref[...] = jnp.zeros_like(dw_ref)
            db_ref[...] = jnp.zeros_like(db_ref)

        g = g_ref[...]
        w = w_ref[...]
        g1 = _shift_down(g, 1)
        g2 = _shift_down(g, 2)
        c = b_ref[...] + w[0:1] * g2 + w[1:2] * g1 + w[2:3] * g
        gg, dgg = _gelu_and_grad(c)
        da = da_ref[...]
        du_ref[...] = (da * gg).astype(BF16)
        dc = da * u_ref[...] * dgg
        db_ref[...] += jnp.sum(dc, axis=0, keepdims=True)
        dw_ref[0:1, :] += jnp.sum(dc * g2, axis=0, keepdims=True)
        dw_ref[1:2, :] += jnp.sum(dc * g1, axis=0, keepdims=True)
        dw_ref[2:3, :] += jnp.sum(dc * g, axis=0, keepdims=True)
        dg_ref[...] = (w[2:3] * dc + w[1:2] * _shift_up(dc, 1) + w[0:1] * _shift_up(dc, 2)).astype(BF16)

    blk = pl.BlockSpec((SEQ, LANE_BLOCK), lambda j, b: (b, j))
    wspec = pl.BlockSpec((3, LANE_BLOCK), lambda j, b: (0, j))
    bspec = pl.BlockSpec((1, LANE_BLOCK), lambda j, b: (0, j))
    return pl.pallas_call(
        body, grid=(D_FF // LANE_BLOCK, batch),
        in_specs=[blk, blk, blk, wspec, bspec],
        out_specs=[blk, blk, wspec, bspec],
        out_shape=[jax.ShapeDtypeStruct((m, D_FF), BF16), jax.ShapeDtypeStruct((m, D_FF), BF16),
                   jax.ShapeDtypeStruct((3, D_FF), F32), jax.ShapeDtypeStruct((1, D_FF), F32)],
        compiler_params=_params(("parallel", "arbitrary")), name="convgate_bwd",
    )(gate, up, dact, conv_w, conv_b)


def _local_step(x, tgt, g1, g2, g3, g4, w_in, ln_g, ln_b, w_s, b_s, rel_bias, w_out, w_gate, w_up,
                conv_w, conv_b, w_down, batch):
    big = dict(tm=1024, out_dtype=F32)
    buckets = _bucket_tables()
    bias = _bias_tables(rel_bias, buckets)
    bz = jnp.repeat(b_s.T, HEAD_DIM, axis=1)
    w_st = jnp.swapaxes(w_s, 1, 2)

    h1 = _rms_fwd(x, g1)
    proj = _mm(h1, w_in, dims="nn", tn=1408, tk=1024, name="mm_proj", **big)
    a = _gate_fwd(proj, ln_g, ln_b, w_s, bz)
    outs, lses = [], []
    for ci, (_, dil) in enumerate(DILATED):
        o_c, l_c = _attn_fwd(proj, bias[ci], dil, batch)
        outs.append(o_c)
        lses.append(l_c)
    o_f32, o_bf, lse = _attn_combine(outs, lses)
    y1 = _mm(a, w_out[:A_WIDTH], dims="nn", tn=1024, tk=A_WIDTH, name="mm_out_a", **big)
    y1 = _mm(o_bf, w_out[A_WIDTH:], dims="nn", tn=1024, tk=B_WIDTH, name="mm_out_b", add=y1, **big)
    x1, h2 = _mid_fwd(x, y1, g2, g3)
    gate = _mm(h2, w_gate, dims="nn", tn=1408, tk=1024, name="mm_gate", **big)
    up = _mm(h2, w_up, dims="nn", tn=1408, tk=1024, name="mm_up", **big)
    act = _convgate_fwd(gate, up, conv_w, conv_b, batch)
    y2 = _mm(act, w_down, dims="nn", tn=1024, tk=1408, name="mm_down", **big)
    dx2, dy2, dg4, loss = _loss_head(x1, y2, tgt, g4)

    dact = _mm(dy2, w_down, dims="nt", tn=1408, tk=1024, name="mm_dact", **big)
    dw_down = _mm(act, dy2, dims="tn", tm=1408, tn=1024, tk=1024, out_dtype=F32, name="mm_dw_down")
    dgate, dup, dconv_w, dconv_b = _convgate_bwd(gate, up, dact, conv_w, conv_b, batch)
    dh2 = _mm(dgate, w_gate, dims="nt", tn=1024, tk=1408, name="mm_dh2_g", **big)
    dh2 = _mm(dup, w_up, dims="nt", tn=1024, tk=1408, name="mm_dh2_u", add=dh2, **big)
    dw_gate = _mm(h2, dgate, dims="tn", tm=1024, tn=1408, tk=1024, out_dtype=F32, name="mm_dw_gate")
    dw_up = _mm(h2, dup, dims="tn", tm=1024, tn=1408, tk=1024, out_dtype=F32, name="mm_dw_up")
    dx1, dy1, dg2, dg3 = _mid_bwd(x1, y1, dh2, dx2, g2, g3)
    dmix = _mm(dy1, w_out, dims="nt", tn=1024, tk=1024, name="mm_dmix", **big)
    dw_out_a = _mm(a, dy1, dims="tn", tm=A_WIDTH, tn=1024, tk=1024, out_dtype=F32, name="mm_dw_out_a")
    dw_out_b = _mm(o_bf, dy1, dims="tn", tm=B_WIDTH, tn=1024, tk=1024, out_dtype=F32, name="mm_dw_out_b")
    duv, dln_g, dln_b, dw_s, dbz = _gate_bwd(proj, dmix, ln_g, ln_b, w_s, w_st, bz)
    dqs, dks, dvs, dss = [], [], [], []
    for ci, (_, dil) in enumerate(DILATED):
        dq_c, dk_c, dv_c, ds_c = _attn_bwd(proj, dmix, o_f32, lse, bias[ci], dil, batch)
        dqs.append(dq_c)
        dks.append(dk_c)
        dvs.append(dv_c)
        dss.append(ds_c)
    drel = _rel_bias_grad(dss, buckets)
    dproj = _assemble_dproj(duv, dqs, dks, dvs)
    dh1 = _mm(dproj, w_in, dims="nt", tn=1024, tk=1408, name="mm_dh1", **big)
    dw_in = _mm(h1, dproj, dims="tn", tm=1024, tn=1408, tk=1024, out_dtype=F32, name="mm_dw_in")
    dx0, dg1 = _in_bwd(x, dh1, dx1, g1)

    grads = dict(
        norm_mix_pre=dg1, norm_mix_post=dg2, norm_ffn_pre=dg3, norm_ffn_post=dg4,
        w_in=dw_in, ln_v_gain=dln_g, ln_v_bias=dln_b, spatial_w=dw_s,
        spatial_b=dbz[:, ::HEAD_DIM].T,
        rel_bias=drel.reshape(B_HEADS, NUM_BUCKETS).T,
        w_out=jnp.concatenate([dw_out_a, dw_out_b], axis=0),
        w_gate=dw_gate, w_up=dw_up, conv_w=dconv_w, conv_b=dconv_b, w_down=dw_down,
    )
    return loss, dx0, grads


def _mesh_pos():
    x, y, c = lax.axis_index("x"), lax.axis_index("y"), lax.axis_index("c")
    chips = [(1 - x, y), (x, 1 - y), (1 - x, 1 - y)]
    return x, y, c, chips


ANY = pl.BlockSpec(memory_space=pl.ANY)


def _gather_weights(shards, conv_w_shard):
    nt = len(shards)

    def body(*refs):
        shard_refs = refs[:nt]
        cw_ref = refs[nt]
        out_refs = refs[nt + 1:2 * nt + 1]
        cw_out = refs[2 * nt + 1]
        send_sems, recv_sems = refs[2 * nt + 2:]
        x, y, c, chips = _mesh_pos()
        s = 2 * x + y
        sib = (x, y, 1 - c)

        def half(ref, chip, which, t):
            rows = shards[t].shape[0] // 2
            return ref.at[2 * chip[0] + chip[1], pl.ds(which * rows, rows), :]

        def rcopy(k, src, dst, to):
            return pltpu.make_async_remote_copy(src_ref=src, dst_ref=dst, send_sem=send_sems.at[k],
                                                recv_sem=recv_sems.at[k], device_id=to, device_id_type=MESH)

        sends = []
        for t in range(nt):
            rows = shards[t].shape[0] // 2
            for j, chip in enumerate(chips):
                sends.append(rcopy(7 * t + j, shard_refs[t].at[pl.ds(c * rows, rows), :],
                                   half(out_refs[t], (x, y), c, t), (*chip, c)))
        for j, chip in enumerate(chips):
            sends.append(rcopy(7 * nt + j, cw_ref, cw_out.at[s], (*chip, c)))
        for cp in sends:
            cp.start()
        fwd = []
        for t in range(nt):
            for j, chip in enumerate(chips):
                landed = half(out_refs[t], chip, c, t)
                rcopy(7 * t + j, landed, landed, (*chip, c)).wait_recv()
                f = rcopy(7 * t + 3 + j, landed, landed, sib)
                f.start()
                fwd.append(f)
        for j, chip in enumerate(chips):
            dst = cw_out.at[2 * chip[0] + chip[1]]
            rcopy(7 * nt + j, dst, dst, (*chip, c)).wait_recv()
        for t in range(nt):
            for j, chip in enumerate(chips):
                other = half(out_refs[t], chip, 1 - c, t)
                rcopy(7 * t + 3 + j, other, other, sib).wait_recv()
        for cp in sends + fwd:
            cp.wait_send()

    out_shape = [jax.ShapeDtypeStruct((N_SHARD,) + sh.shape, sh.dtype) for sh in shards]
    out_shape.append(jax.ShapeDtypeStruct((N_SHARD,) + conv_w_shard.shape, conv_w_shard.dtype))
    nsem = 7 * nt + 3
    return pl.pallas_call(
        body, in_specs=[ANY] * (nt + 1), out_specs=[ANY] * (nt + 1), out_shape=out_shape,
        scratch_shapes=[pltpu.SemaphoreType.DMA((nsem,)), pltpu.SemaphoreType.DMA((nsem,))],
        compiler_params=pltpu.CompilerParams(has_side_effects=True), name="gather_weights",
    )(*shards, conv_w_shard)


def _exchange_halves(grads):
    nt = len(grads)

    def body(*refs):
        g_refs = refs[:nt]
        out_refs = refs[nt:2 * nt]
        send_sems, recv_sems = refs[2 * nt:]
        x, y, c, _ = _mesh_pos()
        copies = []
        for t in range(nt):
            rows = grads[t].shape[1] // 2
            copies.append(pltpu.make_async_remote_copy(
                src_ref=g_refs[t].at[:, pl.ds((1 - c) * rows, rows), :], dst_ref=out_refs[t],
                send_sem=send_sems.at[t], recv_sem=recv_sems.at[t], device_id=(x, y, 1 - c), device_id_type=MESH))
        for cp in copies:
            cp.start()
        for cp in copies:
            cp.wait()

    out_shape = [jax.ShapeDtypeStruct((N_SHARD, g.shape[1] // 2, g.shape[2]), g.dtype) for g in grads]
    return pl.pallas_call(
        body, in_specs=[ANY] * nt, out_specs=[ANY] * nt, out_shape=out_shape,
        scratch_shapes=[pltpu.SemaphoreType.DMA((nt,)), pltpu.SemaphoreType.DMA((nt,))],
        compiler_params=pltpu.CompilerParams(has_side_effects=True), name="rs_sibling_exchange",
    )(*grads)


def _add_halves(g, recv, c_idx):
    _, rows2, cols = g.shape
    rows = rows2 // 2
    tr = rows // 2 if rows % 16 == 0 and rows >= 256 else rows
    nblk = rows // tr

    def body(c_ref, g_ref, r_ref, o_ref):
        o_ref[...] = (g_ref[...] + r_ref[...]).astype(BF16)

    return pl.pallas_call(
        body,
        grid_spec=pltpu.PrefetchScalarGridSpec(
            num_scalar_prefetch=1, grid=(N_SHARD, nblk),
            in_specs=[pl.BlockSpec((None, tr, cols), lambda s, i, c: (s, c[0] * nblk + i, 0)),
                      pl.BlockSpec((None, tr, cols), lambda s, i, c: (s, i, 0))],
            out_specs=pl.BlockSpec((None, tr, cols), lambda s, i, c: (s, i, 0))),
        out_shape=jax.ShapeDtypeStruct((N_SHARD, rows, cols), BF16),
        compiler_params=_params(("parallel", "parallel")), name="rs_add_halves",
    )(c_idx, g, recv)


def _exchange_chips(parts):
    nt = len(parts)

    def body(*refs):
        p_refs = refs[:nt]
        out_refs = refs[nt:2 * nt]
        send_sems, recv_sems = refs[2 * nt:]
        x, y, c, chips = _mesh_pos()
        copies = []
        for t in range(nt):
            for j, chip in enumerate(chips):
                copies.append(pltpu.make_async_remote_copy(
                    src_ref=p_refs[t].at[2 * chip[0] + chip[1]], dst_ref=out_refs[t].at[j],
                    send_sem=send_sems.at[3 * t + j], recv_sem=recv_sems.at[3 * t + j],
                    device_id=(*chip, c), device_id_type=MESH))
        for cp in copies:
            cp.start()
        for cp in copies:
            cp.wait()

    out_shape = [jax.ShapeDtypeStruct((3,) + p.shape[1:], p.dtype) for p in parts]
    return pl.pallas_call(
        body, in_specs=[ANY] * nt, out_specs=[ANY] * nt, out_shape=out_shape,
        scratch_shapes=[pltpu.SemaphoreType.DMA((3 * nt,)), pltpu.SemaphoreType.DMA((3 * nt,))],
        compiler_params=pltpu.CompilerParams(has_side_effects=True), name="rs_chip_exchange",
    )(*parts)


def _add_chips(part, recv, s_idx, c_idx):
    _, rows, cols = part.shape
    tr = rows // 2 if rows % 32 == 0 and rows >= 256 else rows
    nblk = rows // tr

    def body(idx_ref, p_ref, r_ref, o_ref):
        acc = p_ref[...].astype(F32)
        for j in range(3):
            acc = acc + r_ref[j].astype(F32)
        o_ref[...] = acc

    return pl.pallas_call(
        body,
        grid_spec=pltpu.PrefetchScalarGridSpec(
            num_scalar_prefetch=1, grid=(nblk,),
            in_specs=[pl.BlockSpec((None, tr, cols), lambda i, idx: (idx[0], i, 0)),
                      pl.BlockSpec((3, tr, cols), lambda i, idx: (0, i, 0))],
            out_specs=pl.BlockSpec((tr, cols), lambda i, idx: (idx[1] * nblk + i, 0))),
        out_shape=jax.ShapeDtypeStruct((2 * rows, cols), F32),
        compiler_params=_params(("parallel",)), name="rs_add_chips",
    )(jnp.concatenate([s_idx, c_idx]), part, recv)


def _share_halves(fulls):
    nt = len(fulls)

    def body(*refs):
        out_refs = refs[nt:2 * nt]
        send_sems, recv_sems = refs[2 * nt:]
        x, y, c, _ = _mesh_pos()
        copies = []
        for t in range(nt):
            rows = fulls[t].shape[0] // 2
            mine = out_refs[t].at[pl.ds(c * rows, rows), :]
            copies.append(pltpu.make_async_remote_copy(
                src_ref=mine, dst_ref=mine, send_sem=send_sems.at[t], recv_sem=recv_sems.at[t],
                device_id=(x, y, 1 - c), device_id_type=MESH))
        for cp in copies:
            cp.start()
        for t in range(nt):
            rows = fulls[t].shape[0] // 2
            theirs = out_refs[t].at[pl.ds((1 - c) * rows, rows), :]
            pltpu.make_async_remote_copy(
                src_ref=theirs, dst_ref=theirs, send_sem=send_sems.at[t], recv_sem=recv_sems.at[t],
                device_id=(x, y, 1 - c), device_id_type=MESH).wait_recv()
        for cp in copies:
            cp.wait_send()

    out_shape = [jax.ShapeDtypeStruct(f.shape, f.dtype) for f in fulls]
    return pl.pallas_call(
        body, in_specs=[ANY] * nt, out_specs=[ANY] * nt, out_shape=out_shape,
        input_output_aliases={t: t for t in range(nt)},
        scratch_shapes=[pltpu.SemaphoreType.DMA((nt,)), pltpu.SemaphoreType.DMA((nt,))],
        compiler_params=pltpu.CompilerParams(has_side_effects=True), name="rs_share_halves",
    )(*fulls)


def _allreduce_small(packed):
    rows = packed.shape[0]

    def body(p_ref, o_ref, sib_ref, chip_ref, send_sems, recv_sems):
        x, y, c, chips = _mesh_pos()
        first = pltpu.make_async_remote_copy(src_ref=p_ref, dst_ref=sib_ref, send_sem=send_sems.at[0],
                                             recv_sem=recv_sems.at[0], device_id=(x, y, 1 - c), device_id_type=MESH)
        first.start()
        first.wait()
        o_ref[...] = p_ref[...] + sib_ref[...]
        copies = [pltpu.make_async_remote_copy(src_ref=o_ref, dst_ref=chip_ref.at[j], send_sem=send_sems.at[1 + j],
                                               recv_sem=recv_sems.at[1 + j], device_id=(*chip, c), device_id_type=MESH)
                  for j, chip in enumerate(chips)]
        for cp in copies:
            cp.start()
        for cp in copies:
            cp.wait()
        o_ref[...] = (o_ref[...] + chip_ref[0]) + (chip_ref[1] + chip_ref[2])

    vm = pl.BlockSpec(memory_space=pltpu.VMEM)
    return pl.pallas_call(
        body, in_specs=[vm], out_specs=vm, out_shape=jax.ShapeDtypeStruct(packed.shape, F32),
        scratch_shapes=[pltpu.VMEM((rows, 128), F32), pltpu.VMEM((3, rows, 128), F32),
                        pltpu.SemaphoreType.DMA((4,)), pltpu.SemaphoreType.DMA((4,))],
        compiler_params=pltpu.CompilerParams(has_side_effects=True, vmem_limit_bytes=VMEM_LIMIT),
        name="allreduce_small",
    )(packed)


def _adamw(w, g, m, v, name):
    rows, cols = w.shape
    tr = rows
    if rows * cols > 256 * 1024:
        tr = next(cand for cand in (256, 176, 128) if rows % cand == 0)

    def body(w_ref, g_ref, m_ref, v_ref, go_ref, d_ref, nm_ref, nv_ref):
        gv = g_ref[...]
        go_ref[...] = gv
        nm = ADAM_B1 * m_ref[...] + (1.0 - ADAM_B1) * gv
        nv = ADAM_B2 * v_ref[...] + (1.0 - ADAM_B2) * (gv * gv)
        m_hat = nm / (1.0 - ADAM_B1 ** ADAM_STEP)
        v_hat = nv / (1.0 - ADAM_B2 ** ADAM_STEP)
        d_ref[...] = -ADAM_LR * (m_hat / (jnp.sqrt(v_hat) + ADAM_EPS) + ADAM_WD * w_ref[...])
        nm_ref[...] = nm
        nv_ref[...] = nv

    spec = pl.BlockSpec((tr, cols), lambda i: (i, 0))
    sds = jax.ShapeDtypeStruct((rows, cols), F32)
    return pl.pallas_call(
        body, grid=(rows // tr,), in_specs=[spec] * 4, out_specs=[spec] * 4, out_shape=[sds] * 4,
        compiler_params=_params(("parallel",)), name=name,
    )(w, g, m, v)


def _pack(arrays, rows):
    flat = jnp.concatenate([a.reshape(-1) for a in arrays])
    flat = jnp.pad(flat, (0, rows * 128 - flat.shape[0]))
    return flat.reshape(rows, 128)


def _unpack(packed, shapes):
    flat = packed.reshape(-1)
    out, off = [], 0
    for sh in shapes:
        n = int(np.prod(sh))
        out.append(flat[off:off + n].reshape(sh))
        off += n
    return out


SMALL = ["norm_mix_pre", "norm_mix_post", "norm_ffn_pre", "norm_ffn_post", "ln_v_gain", "ln_v_bias",
         "spatial_w", "spatial_b", "rel_bias", "conv_b"]
LARGE = ["w_in", "w_gate", "w_up", "w_down", "w_out"]
ORDER = ["norm_mix_pre", "norm_mix_post", "norm_ffn_pre", "norm_ffn_post", "w_in", "ln_v_gain", "ln_v_bias",
         "spatial_w", "spatial_b", "rel_bias", "w_out", "w_gate", "w_up", "conv_w", "conv_b", "w_down"]


def _col_shards(full):
    rows, cols4 = full.shape
    return full.reshape(rows, N_SHARD, cols4 // N_SHARD).transpose(1, 0, 2)


def _from_col_shards(g):
    n, rows, cols = g.shape
    return g.transpose(1, 0, 2).reshape(rows, n * cols)


def kernel(x, norm_mix_pre, norm_mix_post, norm_ffn_pre, norm_ffn_post, w_in, ln_v_gain, ln_v_bias, spatial_w, spatial_b, rel_bias, w_out, w_gate, w_up, conv_w, conv_b, w_down, loss_target, m_norm_mix_pre, m_norm_mix_post, m_norm_ffn_pre, m_norm_ffn_post, m_w_in, m_ln_v_gain, m_ln_v_bias, m_spatial_w, m_spatial_b, m_rel_bias, m_w_out, m_w_gate, m_w_up, m_conv_w, m_conv_b, m_w_down, v_norm_mix_pre, v_norm_mix_post, v_norm_ffn_pre, v_norm_ffn_post, v_w_in, v_ln_v_gain, v_ln_v_bias, v_spatial_w, v_spatial_b, v_rel_bias, v_w_out, v_w_gate, v_w_up, v_conv_w, v_conv_b, v_w_down):
    params = dict(norm_mix_pre=norm_mix_pre, norm_mix_post=norm_mix_post, norm_ffn_pre=norm_ffn_pre,
                  norm_ffn_post=norm_ffn_post, w_in=w_in, ln_v_gain=ln_v_gain, ln_v_bias=ln_v_bias,
                  spatial_w=spatial_w, spatial_b=spatial_b, rel_bias=rel_bias, w_out=w_out, w_gate=w_gate,
                  w_up=w_up, conv_w=conv_w, conv_b=conv_b, w_down=w_down)
    mom = dict(norm_mix_pre=m_norm_mix_pre, norm_mix_post=m_norm_mix_post, norm_ffn_pre=m_norm_ffn_pre,
               norm_ffn_post=m_norm_ffn_post, w_in=m_w_in, ln_v_gain=m_ln_v_gain, ln_v_bias=m_ln_v_bias,
               spatial_w=m_spatial_w, spatial_b=m_spatial_b, rel_bias=m_rel_bias, w_out=m_w_out, w_gate=m_w_gate,
               w_up=m_w_up, conv_w=m_conv_w, conv_b=m_conv_b, w_down=m_w_down)
    var = dict(norm_mix_pre=v_norm_mix_pre, norm_mix_post=v_norm_mix_post, norm_ffn_pre=v_norm_ffn_pre,
               norm_ffn_post=v_norm_ffn_post, w_in=v_w_in, ln_v_gain=v_ln_v_gain, ln_v_bias=v_ln_v_bias,
               spatial_w=v_spatial_w, spatial_b=v_spatial_b, rel_bias=v_rel_bias, w_out=v_w_out, w_gate=v_w_gate,
               w_up=v_w_up, conv_w=v_conv_w, conv_b=v_conv_b, w_down=v_w_down)

    batch = x.shape[0]
    xi, yi, ci = lax.axis_index("x"), lax.axis_index("y"), lax.axis_index("c")
    s_idx = (2 * xi + yi).astype(jnp.int32).reshape(1)
    c_idx = ci.astype(jnp.int32).reshape(1)

    shards = [params[n][0].astype(BF16) for n in LARGE]
    gathered = _gather_weights(shards, conv_w[0])
    g_in, g_gate, g_up, g_down, g_out, g_convw = [
        lax.dynamic_update_index_in_dim(g, own, s_idx[0], 0) for g, own in zip(gathered, shards + [conv_w[0]])]
    w_in_f = _from_col_shards(g_in)
    w_gate_f = _from_col_shards(g_gate)
    w_up_f = _from_col_shards(g_up)
    w_down_f = g_down.reshape(D_FF, D_MODEL)
    w_out_f = g_out.reshape(D_MODEL, D_MODEL)
    conv_w_f = _from_col_shards(g_convw)

    loss_part, dx0, grads = _local_step(
        x.reshape(batch * SEQ, D_MODEL), loss_target.reshape(batch * SEQ, D_MODEL),
        norm_mix_pre, norm_mix_post, norm_ffn_pre, norm_ffn_post, w_in_f,
        ln_v_gain.reshape(1, A_WIDTH), ln_v_bias.reshape(1, A_WIDTH), spatial_w[0], spatial_b[0], rel_bias,
        w_out_f, w_gate_f, w_up_f, conv_w_f, conv_b, w_down_f, batch)
    loss = lax.psum(loss_part[0, 0], ("x", "y", "c"))
    grad_x = dx0.reshape(batch, SEQ, D_MODEL)

    big = [_col_shards(grads["w_in"]), _col_shards(grads["w_gate"]), _col_shards(grads["w_up"]),
           grads["w_down"].reshape(N_SHARD, SHARD_FF, D_MODEL),
           grads["w_out"].reshape(N_SHARD, D_MODEL // N_SHARD, D_MODEL)]
    recv_a = _exchange_halves(big)
    parts = [_add_halves(g, r, c_idx) for g, r in zip(big, recv_a)]
    recv_b = _exchange_chips(parts)
    fulls = [_add_chips(p, r, s_idx, c_idx) for p, r in zip(parts, recv_b)]
    reduced = dict(zip(LARGE, _share_halves(fulls)))

    small_g = [grads[n].reshape(params[n].shape) for n in SMALL] + [grads["conv_w"]]
    n_small = sum(int(np.prod(g.shape)) for g in small_g)
    small_rows = -(-n_small // (8 * 128)) * 8
    summed = _unpack(_allreduce_small(_pack(small_g, small_rows)),
                     [params[n].shape for n in SMALL] + [(3, D_FF)])
    for n, g in zip(SMALL, summed[:-1]):
        reduced[n] = g
    reduced["conv_w"] = lax.dynamic_slice_in_dim(summed[-1], s_idx[0] * SHARD_FF, SHARD_FF, axis=1)[None]

    out_g, out_d, out_m, out_v = {}, {}, {}, {}
    for n in LARGE:
        shp = params[n].shape
        g, d, nm, nv = _adamw(params[n][0], reduced[n], mom[n][0], var[n][0], name=f"adamw_{n}")
        out_g[n], out_d[n], out_m[n], out_v[n] = g.reshape(shp), d.reshape(shp), nm.reshape(shp), nv.reshape(shp)
    small_names = SMALL + ["conv_w"]
    rows_s = -(-sum(int(np.prod(params[n].shape)) for n in small_names) // (8 * 128)) * 8
    _, d, nm, nv = _adamw(_pack([params[n] for n in small_names], rows_s),
                          _pack([reduced[n] for n in small_names], rows_s),
                          _pack([mom[n] for n in small_names], rows_s), _pack([var[n] for n in small_names], rows_s),
                          name="adamw_small")
    shapes = [params[n].shape for n in small_names]
    for n, dd, mm, vv in zip(small_names, _unpack(d, shapes), _unpack(nm, shapes), _unpack(nv, shapes)):
        out_g[n], out_d[n], out_m[n], out_v[n] = reduced[n], dd, mm, vv

    return (loss, grad_x, *[out_g[n] for n in ORDER], *[out_d[n] for n in ORDER],
            *[out_m[n] for n in ORDER], *[out_v[n] for n in ORDER])
```

```python
import functools
import math

import numpy as np
import jax
import jax.numpy as jnp
from jax import lax
from jax.experimental import pallas as pl
from jax.experimental.pallas import tpu as pltpu

F32 = jnp.float32
BF16 = jnp.bfloat16
MESH = pl.DeviceIdType.MESH

D_MODEL = 1024
SEQ = 2048
HEAD_DIM = 64
A_GROUPS = 4
A_WIDTH = 256
B_HEADS = 12
B_WIDTH = 768
CHUNK = 128
DILATED = ((128, 1), (512, 4), (2048, 16))
NUM_BUCKETS = 32
MAX_DISTANCE = 2048
D_FF = 2816
IN_COLS = 2816
NORM_EPS = 1e-6
NEG_INF = -1e30
N_SHARD = 4
SHARD_FF = D_FF // N_SHARD
LANE_BLOCK = 256
VMEM_LIMIT = 56 * 1024 * 1024

ADAM_LR = 0.001
ADAM_B1 = 0.9
ADAM_B2 = 0.999
ADAM_EPS = 1e-08
ADAM_WD = 0.01
ADAM_STEP = 10

GELU_C = math.sqrt(2.0 / math.pi)
GELU_A = 0.044715


def _params(sem=None):
    return pltpu.CompilerParams(dimension_semantics=sem, vmem_limit_bytes=VMEM_LIMIT)


def _dot(a, b, precision=None):
    return jnp.dot(a, b, preferred_element_type=F32, precision=precision)


def _dot_nt(a, b):
    return lax.dot_general(a, b, (((1,), (1,)), ((), ())), preferred_element_type=F32)


def _dot_tn(a, b):
    return lax.dot_general(a, b, (((0,), (0,)), ((), ())), preferred_element_type=F32)


def _gelu(x):
    t = jnp.tanh(GELU_C * (x + GELU_A * (x * x * x)))
    return 0.5 * x * (1.0 + t)


def _gelu_and_grad(x):
    x2 = x * x
    t = jnp.tanh(GELU_C * (x + GELU_A * (x2 * x)))
    g = 0.5 * x * (1.0 + t)
    dg = 0.5 * (1.0 + t) + 0.5 * x * (1.0 - t * t) * (GELU_C * (1.0 + 3.0 * GELU_A * x2))
    return g, dg


def _mm(a, b, *, dims, tm, tn, tk, out_dtype, name, add=None):
    if dims == "nn":
        m, k = a.shape
        n = b.shape[1]
        a_spec = pl.BlockSpec((tm, tk), lambda i, j, kk: (i, kk))
        b_spec = pl.BlockSpec((tk, tn), lambda i, j, kk: (kk, j))
        dot = _dot
    elif dims == "nt":
        m, k = a.shape
        n = b.shape[0]
        a_spec = pl.BlockSpec((tm, tk), lambda i, j, kk: (i, kk))
        b_spec = pl.BlockSpec((tn, tk), lambda i, j, kk: (j, kk))
        dot = _dot_nt
    else:
        k, m = a.shape
        n = b.shape[1]
        a_spec = pl.BlockSpec((tk, tm), lambda i, j, kk: (kk, i))
        b_spec = pl.BlockSpec((tk, tn), lambda i, j, kk: (kk, j))
        dot = _dot_tn
    assert m % tm == 0 and n % tn == 0 and k % tk == 0, (name, m, n, k)
    nk = k // tk
    has_add = add is not None

    def body(*refs):
        if has_add:
            a_ref, b_ref, add_ref, o_ref, acc_ref = refs
        else:
            a_ref, b_ref, o_ref, acc_ref = refs
        kk = pl.program_id(2)

        @pl.when(kk == 0)
        def _():
            if has_add:
                acc_ref[...] = add_ref[...].astype(F32)
            else:
                acc_ref[...] = jnp.zeros_like(acc_ref)

        acc_ref[...] += dot(a_ref[...].astype(BF16), b_ref[...].astype(BF16))

        @pl.when(kk == nk - 1)
        def _():
            o_ref[...] = acc_ref[...].astype(out_dtype)

    in_specs = [a_spec, b_spec]
    args = [a, b]
    if has_add:
        in_specs.append(pl.BlockSpec((tm, tn), lambda i, j, kk: (i, j)))
        args.append(add)
    return pl.pallas_call(
        body,
        grid=(m // tm, n // tn, nk),
        in_specs=in_specs,
        out_specs=pl.BlockSpec((tm, tn), lambda i, j, kk: (i, j)),
        out_shape=jax.ShapeDtypeStruct((m, n), out_dtype),
        scratch_shapes=[pltpu.VMEM((tm, tn), F32)],
        compiler_params=_params(("parallel", "parallel", "arbitrary")),
        name=name,
    )(*args)


ROW_TILE = 512


def _row_spec(width=D_MODEL):
    return pl.BlockSpec((ROW_TILE, width), lambda i: (i, 0))


def _vec_spec(width=D_MODEL):
    return pl.BlockSpec((1, width), lambda i: (0, 0))


def _rstd(v):
    return lax.rsqrt(jnp.mean(v * v, axis=-1, keepdims=True) + NORM_EPS)


def _rms_fwd(x, g):
    m = x.shape[0]

    def body(x_ref, g_ref, h_ref):
        xv = x_ref[...]
        h_ref[...] = (xv * _rstd(xv) * g_ref[...]).astype(BF16)

    return pl.pallas_call(
        body, grid=(m // ROW_TILE,),
        in_specs=[_row_spec(), _vec_spec()],
        out_specs=_row_spec(),
        out_shape=jax.ShapeDtypeStruct((m, D_MODEL), BF16),
        compiler_params=_params(("parallel",)), name="rms_fwd",
    )(x, g)


def _mid_fwd(x0, y1, g2, g3):
    m = x0.shape[0]

    def body(x0_ref, y1_ref, g2_ref, g3_ref, x1_ref, h2_ref):
        y1v = y1_ref[...]
        x1 = x0_ref[...] + y1v * _rstd(y1v) * g2_ref[...]
        x1_ref[...] = x1
        h2_ref[...] = (x1 * _rstd(x1) * g3_ref[...]).astype(BF16)

    return pl.pallas_call(
        body, grid=(m // ROW_TILE,),
        in_specs=[_row_spec(), _row_spec(), _vec_spec(), _vec_spec()],
        out_specs=[_row_spec(), _row_spec()],
        out_shape=[jax.ShapeDtypeStruct((m, D_MODEL), F32), jax.ShapeDtypeStruct((m, D_MODEL), BF16)],
        compiler_params=_params(("parallel",)), name="mid_fwd",
    )(x0, y1, g2, g3)


def _rms_bwd_rows(dout, v, g):
    r = _rstd(v)
    n = v * r
    dn = dout * g
    dv = r * (dn - n * jnp.mean(dn * n, axis=-1, keepdims=True))
    dg = jnp.sum(dout * n, axis=0, keepdims=True)
    return dv, dg


def _loss_head(x1, y2, tgt, g4):
    m = x1.shape[0]

    def body(x1_ref, y2_ref, t_ref, g4_ref, dx2_ref, dy2_ref, dg4_ref, loss_ref):
        i = pl.program_id(0)

        @pl.when(i == 0)
        def _():
            dg4_ref[...] = jnp.zeros_like(dg4_ref)
            loss_ref[...] = jnp.zeros_like(loss_ref)

        y2v = y2_ref[...]
        g4 = g4_ref[...]
        x2 = x1_ref[...] + y2v * _rstd(y2v) * g4
        err = x2 - t_ref[...]
        loss_ref[...] += 0.5 * jnp.sum(jnp.mean(err * err, axis=-1, keepdims=True), axis=0, keepdims=True)
        dx2 = err * (1.0 / D_MODEL)
        dx2_ref[...] = dx2
        dy2, dg4 = _rms_bwd_rows(dx2, y2v, g4)
        dy2_ref[...] = dy2.astype(BF16)
        dg4_ref[...] += dg4

    return pl.pallas_call(
        body, grid=(m // ROW_TILE,),
        in_specs=[_row_spec(), _row_spec(), _row_spec(), _vec_spec()],
        out_specs=[_row_spec(), _row_spec(), _vec_spec(), pl.BlockSpec((1, 1), lambda i: (0, 0))],
        out_shape=[jax.ShapeDtypeStruct((m, D_MODEL), F32), jax.ShapeDtypeStruct((m, D_MODEL), BF16),
                   jax.ShapeDtypeStruct((1, D_MODEL), F32), jax.ShapeDtypeStruct((1, 1), F32)],
        compiler_params=_params(("arbitrary",)), name="loss_head",
    )(x1, y2, tgt, g4)


def _mid_bwd(x1, y1, dh2, dx2, g2, g3):
    m = x1.shape[0]

    def body(x1_ref, y1_ref, dh2_ref, dx2_ref, g2_ref, g3_ref, dx1_ref, dy1_ref, dg2_ref, dg3_ref):
        i = pl.program_id(0)

        @pl.when(i == 0)
        def _():
            dg2_ref[...] = jnp.zeros_like(dg2_ref)
            dg3_ref[...] = jnp.zeros_like(dg3_ref)

        d3, dg3 = _rms_bwd_rows(dh2_ref[...], x1_ref[...], g3_ref[...])
        dx1 = dx2_ref[...] + d3
        dx1_ref[...] = dx1
        dy1, dg2 = _rms_bwd_rows(dx1, y1_ref[...], g2_ref[...])
        dy1_ref[...] = dy1.astype(BF16)
        dg2_ref[...] += dg2
        dg3_ref[...] += dg3

    return pl.pallas_call(
        body, grid=(m // ROW_TILE,),
        in_specs=[_row_spec(), _row_spec(), _row_spec(), _row_spec(), _vec_spec(), _vec_spec()],
        out_specs=[_row_spec(), _row_spec(), _vec_spec(), _vec_spec()],
        out_shape=[jax.ShapeDtypeStruct((m, D_MODEL), F32), jax.ShapeDtypeStruct((m, D_MODEL), BF16),
                   jax.ShapeDtypeStruct((1, D_MODEL), F32), jax.ShapeDtypeStruct((1, D_MODEL), F32)],
        compiler_params=_params(("arbitrary",)), name="mid_bwd",
    )(x1, y1, dh2, dx2, g2, g3)


def _in_bwd(x0, dh1, dx1, g1):
    m = x0.shape[0]

    def body(x0_ref, dh1_ref, dx1_ref, g1_ref, dx0_ref, dg1_ref):
        i = pl.program_id(0)

        @pl.when(i == 0)
        def _():
            dg1_ref[...] = jnp.zeros_like(dg1_ref)

        d1, dg1 = _rms_bwd_rows(dh1_ref[...], x0_ref[...], g1_ref[...])
        dx0_ref[...] = dx1_ref[...] + d1
        dg1_ref[...] += dg1

    return pl.pallas_call(
        body, grid=(m // ROW_TILE,),
        in_specs=[_row_spec(), _row_spec(), _row_spec(), _vec_spec()],
        out_specs=[_row_spec(), _vec_spec()],
        out_shape=[jax.ShapeDtypeStruct((m, D_MODEL), F32), jax.ShapeDtypeStruct((1, D_MODEL), F32)],
        compiler_params=_params(("arbitrary",)), name="in_bwd",
    )(x0, dh1, dx1, g1)


GATE_ROWS = 512


def _group_mean_matrix():
    p = np.zeros((A_WIDTH, A_WIDTH), np.float32)
    for g in range(A_GROUPS):
        p[g * HEAD_DIM:(g + 1) * HEAD_DIM, g * HEAD_DIM:(g + 1) * HEAD_DIM] = 1.0 / HEAD_DIM
    return jnp.asarray(p)


def _group_masks(width=A_WIDTH):
    lane = lax.broadcasted_iota(jnp.int32, (1, width), 1)
    return [(lane >= g * HEAD_DIM) & (lane < (g + 1) * HEAD_DIM) for g in range(width // HEAD_DIM)]


def _layernorm_groups(vg, pavg):
    hi = lax.Precision.HIGHEST
    mu = _dot(vg, pavg, hi)
    xc = vg - mu
    var = _dot(xc * xc, pavg, hi)
    rstd = lax.rsqrt(var + NORM_EPS)
    return xc * rstd, rstd


def _spatial_mix(w_bf, vn_chunk_bf, masks, bz):
    z = bz
    for g in range(A_GROUPS):
        z = z + jnp.where(masks[g], _dot(w_bf[g], vn_chunk_bf), 0.0)
    return z


def _gate_fwd(proj, ln_g, ln_b, w_s, bz):
    m = proj.shape[0]
    pavg = _group_mean_matrix()

    def body(u_ref, v_ref, lg_ref, lb_ref, w_ref, bz_ref, p_ref, a_ref):
        masks = _group_masks()
        row = lax.broadcasted_iota(jnp.int32, (CHUNK, CHUNK), 0)
        col = lax.broadcasted_iota(jnp.int32, (CHUNK, CHUNK), 1)
        w_bf = [jnp.where(row >= col, w_ref[g], 0.0).astype(BF16) for g in range(A_GROUPS)]
        ug = _gelu(u_ref[...])
        vhat, _ = _layernorm_groups(_gelu(v_ref[...]), p_ref[...])
        vn = vhat * lg_ref[...] + lb_ref[...]
        bz = bz_ref[...]
        for c in range(GATE_ROWS // CHUNK):
            sl = slice(c * CHUNK, (c + 1) * CHUNK)
            z = _spatial_mix(w_bf, vn[sl].astype(BF16), masks, bz)
            a_ref[sl, :] = (ug[sl] * z).astype(BF16)

    full = lambda shape: pl.BlockSpec(shape, lambda i: tuple(0 for _ in shape))
    return pl.pallas_call(
        body, grid=(m // GATE_ROWS,),
        in_specs=[pl.BlockSpec((GATE_ROWS, A_WIDTH), lambda i: (i, 0)),
                  pl.BlockSpec((GATE_ROWS, A_WIDTH), lambda i: (i, 1)),
                  full((1, A_WIDTH)), full((1, A_WIDTH)), full((A_GROUPS, CHUNK, CHUNK)),
                  full((CHUNK, A_WIDTH)), full((A_WIDTH, A_WIDTH))],
        out_specs=pl.BlockSpec((GATE_ROWS, A_WIDTH), lambda i: (i, 0)),
        out_shape=jax.ShapeDtypeStruct((m, A_WIDTH), BF16),
        compiler_params=_params(("parallel",)), name="gate_fwd",
    )(proj, proj, ln_g, ln_b, w_s, bz, pavg)


def _gate_bwd(proj, dmix, ln_g, ln_b, w_s, w_st, bz):
    m = proj.shape[0]
    pavg = _group_mean_matrix()
    nsteps = m // GATE_ROWS

    def body(u_ref, v_ref, da_ref, lg_ref, lb_ref, w_ref, wt_ref, bz_ref, p_ref,
             duv_ref, dlg_ref, dlb_ref, dw_ref, dbz_ref):
        i = pl.program_id(0)

        @pl.when(i == 0)
        def _():
            dlg_ref[...] = jnp.zeros_like(dlg_ref)
            dlb_ref[...] = jnp.zeros_like(dlb_ref)
            dw_ref[...] = jnp.zeros_like(dw_ref)
            dbz_ref[...] = jnp.zeros_like(dbz_ref)

        hi = lax.Precision.HIGHEST
        masks = _group_masks()
        row = lax.broadcasted_iota(jnp.int32, (CHUNK, CHUNK), 0)
        col = lax.broadcasted_iota(jnp.int32, (CHUNK, CHUNK), 1)
        tril = row >= col
        w_bf = [jnp.where(tril, w_ref[g], 0.0).astype(BF16) for g in range(A_GROUPS)]
        wt_bf = [jnp.where(col >= row, wt_ref[g], 0.0).astype(BF16) for g in range(A_GROUPS)]
        pavg_v = p_ref[...]
        lg = lg_ref[...]
        ug, dug = _gelu_and_grad(u_ref[...])
        vg, dvg_dx = _gelu_and_grad(v_ref[...])
        vhat, rstd = _layernorm_groups(vg, pavg_v)
        vn = vhat * lg + lb_ref[...]
        da = da_ref[...]
        bz = bz_ref[...]
        for c in range(GATE_ROWS // CHUNK):
            sl = slice(c * CHUNK, (c + 1) * CHUNK)
            vn_bf = vn[sl].astype(BF16)
            z = _spatial_mix(w_bf, vn_bf, masks, bz)
            dz = da[sl] * ug[sl]
            duv_ref[sl, 0:A_WIDTH] = da[sl] * z * dug[sl]
            dbz_ref[...] += dz
            dz_bf = dz.astype(BF16)
            dvn = jnp.zeros((CHUNK, A_WIDTH), F32)
            for g in range(A_GROUPS):
                dz_g = jnp.where(masks[g], dz, 0.0).astype(BF16)
                dw_ref[g] += jnp.where(tril, _dot_nt(dz_g, vn_bf), 0.0)
                dvn = dvn + jnp.where(masks[g], _dot(wt_bf[g], dz_bf), 0.0)
            vh = vhat[sl]
            dlb_ref[...] += jnp.sum(dvn, axis=0, keepdims=True)
            dlg_ref[...] += jnp.sum(dvn * vh, axis=0, keepdims=True)
            dvh = dvn * lg
            m1 = _dot(dvh, pavg_v, hi)
            m2 = _dot(dvh * vh, pavg_v, hi)
            duv_ref[sl, A_WIDTH:2 * A_WIDTH] = rstd[sl] * (dvh - m1 - vh * m2) * dvg_dx[sl]

        @pl.when(i == nsteps - 1)
        def _():
            dbz_ref[...] = _dot(dbz_ref[...], pavg_v * float(HEAD_DIM), hi)

    full = lambda shape: pl.BlockSpec(shape, lambda i: tuple(0 for _ in shape))
    return pl.pallas_call(
        body, grid=(nsteps,),
        in_specs=[pl.BlockSpec((GATE_ROWS, A_WIDTH), lambda i: (i, 0)),
                  pl.BlockSpec((GATE_ROWS, A_WIDTH), lambda i: (i, 1)),
                  pl.BlockSpec((GATE_ROWS, A_WIDTH), lambda i: (i, 0)),
                  full((1, A_WIDTH)), full((1, A_WIDTH)), full((A_GROUPS, CHUNK, CHUNK)),
                  full((A_GROUPS, CHUNK, CHUNK)), full((CHUNK, A_WIDTH)), full((A_WIDTH, A_WIDTH))],
        out_specs=[pl.BlockSpec((GATE_ROWS, 2 * A_WIDTH), lambda i: (i, 0)),
                   full((1, A_WIDTH)), full((1, A_WIDTH)), full((A_GROUPS, CHUNK, CHUNK)),
                   full((CHUNK, A_WIDTH))],
        out_shape=[jax.ShapeDtypeStruct((m, 2 * A_WIDTH), F32),
                   jax.ShapeDtypeStruct((1, A_WIDTH), F32), jax.ShapeDtypeStruct((1, A_WIDTH), F32),
                   jax.ShapeDtypeStruct((A_GROUPS, CHUNK, CHUNK), F32),
                   jax.ShapeDtypeStruct((CHUNK, A_WIDTH), F32)],
        compiler_params=_params(("arbitrary",)), name="gate_bwd",
    )(proj, proj, dmix, ln_g, ln_b, w_s, w_st, bz, pavg)


Q_BLOCK = 128
Q_COL, K_COL, V_COL = 2, 5, 8
N_COLBLK = IN_COLS // LANE_BLOCK
HEAD_BLOCKS = B_WIDTH // LANE_BLOCK
HEADS_PER_BLOCK = LANE_BLOCK // HEAD_DIM


def _t5_bucket_np(dist, dtype):
    max_exact = NUM_BUCKETS // 2
    d = np.maximum(dist, 1).astype(dtype)
    large = max_exact + (np.log(d / dtype(max_exact)) / dtype(math.log(MAX_DISTANCE / max_exact))
                         * dtype(NUM_BUCKETS - max_exact))
    large = np.minimum(large.astype(np.int32), NUM_BUCKETS - 1)
    return np.where(dist < max_exact, dist, large)


def _bucket_tables():
    i = np.arange(Q_BLOCK)[:, None]
    j = np.arange(Q_BLOCK)[None, :]
    tables = []
    for _, dil in DILATED:
        rel_prev = Q_BLOCK + i - j
        rel_cur = i - j
        rel = np.concatenate([rel_prev, rel_cur], axis=1)
        valid = np.concatenate([rel_prev <= Q_BLOCK, rel_cur >= 0], axis=1)
        dist = np.maximum(rel, 0) * dil
        b32 = _t5_bucket_np(dist, np.float32)
        b64 = _t5_bucket_np(dist, np.float64)
        assert np.array_equal(b32, b64)
        tables.append(np.where(valid, b32, -1).astype(np.int32))
    return np.stack(tables)


def _bias_tables(rel_bias, buckets_np):
    present = [sorted(set(int(v) for v in np.unique(buckets_np[c]) if v >= 0)) for c in range(len(DILATED))]

    def body(rb_ref, bk_ref, o_ref):
        for c in range(len(DILATED)):
            bk = bk_ref[c]
            for h in range(B_HEADS):
                acc = jnp.full((Q_BLOCK, 2 * Q_BLOCK), NEG_INF, F32)
                for b in present[c]:
                    acc = jnp.where(bk == b, rb_ref[b, h], acc)
                o_ref[c, h] = acc

    return pl.pallas_call(
        body,
        in_specs=[pl.BlockSpec(memory_space=pltpu.SMEM), pl.BlockSpec(memory_space=pltpu.VMEM)],
        out_specs=pl.BlockSpec(memory_space=pltpu.VMEM),
        out_shape=jax.ShapeDtypeStruct((len(DILATED), B_HEADS, Q_BLOCK, 2 * Q_BLOCK), F32),
        compiler_params=_params(), name="bias_tables",
    )(rel_bias, jnp.asarray(buckets_np))


def _head_masks():
    lane = lax.broadcasted_iota(jnp.int32, (1, LANE_BLOCK), 1)
    return [(lane >= h * HEAD_DIM) & (lane < (h + 1) * HEAD_DIM) for h in range(HEADS_PER_BLOCK)]


def _attn_fwd(proj, bias, dil, batch):
    m = proj.shape[0]
    tr = SEQ // dil
    nb = tr // Q_BLOCK
    proj3 = proj.reshape(batch, tr, dil * IN_COLS)

    def body(q_ref, k_ref, v_ref, b_ref, o_ref, l_ref):
        masks = _head_masks()

        def block(n, carry):
            r0 = pl.multiple_of(n * Q_BLOCK, Q_BLOCK)
            rows = pl.ds(r0, Q_BLOCK)
            q = q_ref[rows, :] * 0.125
            kc = k_ref[rows, :].astype(BF16)
            vc = v_ref[rows, :].astype(BF16)
            if nb > 1:
                p0 = pl.multiple_of(jnp.maximum(n - 1, 0) * Q_BLOCK, Q_BLOCK)
                kp = k_ref[pl.ds(p0, Q_BLOCK), :].astype(BF16)
                vp = v_ref[pl.ds(p0, Q_BLOCK), :].astype(BF16)
            o_acc = jnp.zeros((Q_BLOCK, LANE_BLOCK), F32)
            l_acc = jnp.zeros((Q_BLOCK, LANE_BLOCK), F32)
            for h in range(HEADS_PER_BLOCK):
                qh = jnp.where(masks[h], q, 0.0).astype(BF16)
                sc = _dot_nt(qh, kc) + b_ref[h, :, Q_BLOCK:]
                mx = jnp.max(sc, axis=1, keepdims=True)
                if nb > 1:
                    sp = _dot_nt(qh, kp) + jnp.where(n == 0, NEG_INF, b_ref[h, :, :Q_BLOCK])
                    mx = jnp.maximum(mx, jnp.max(sp, axis=1, keepdims=True))
                pc = jnp.exp(sc - mx)
                den = jnp.sum(pc, axis=1, keepdims=True)
                oh = _dot(pc.astype(BF16), vc)
                if nb > 1:
                    pp = jnp.exp(sp - mx)
                    den = den + jnp.sum(pp, axis=1, keepdims=True)
                    oh = oh + _dot(pp.astype(BF16), vp)
                o_acc = jnp.where(masks[h], oh / den, o_acc)
                l_acc = jnp.where(masks[h], mx + jnp.log(den), l_acc)
            o_ref[rows, :] = o_acc
            l_ref[rows, :] = l_acc
            return carry

        if nb > 1:
            lax.fori_loop(0, nb, block, 0)
        else:
            block(0, 0)

    def in_spec(col0):
        return pl.BlockSpec((None, tr, LANE_BLOCK), lambda b, r, g: (b, 0, r * N_COLBLK + col0 + g))

    out_spec = pl.BlockSpec((None, tr, LANE_BLOCK), lambda b, r, g: (b, 0, r * HEAD_BLOCKS + g))
    out_sds = jax.ShapeDtypeStruct((batch, tr, dil * B_WIDTH), F32)
    o, lse = pl.pallas_call(
        body, grid=(batch, dil, HEAD_BLOCKS),
        in_specs=[in_spec(Q_COL), in_spec(K_COL), in_spec(V_COL),
                  pl.BlockSpec((HEADS_PER_BLOCK, Q_BLOCK, 2 * Q_BLOCK), lambda b, r, g: (g, 0, 0))],
        out_specs=[out_spec, out_spec],
        out_shape=[out_sds, out_sds],
        compiler_params=_params(("parallel", "parallel", "parallel")), name=f"attn_fwd_d{dil}",
    )(proj3, proj3, proj3, bias)
    return o.reshape(m, B_WIDTH), lse.reshape(m, B_WIDTH)


def _attn_combine(outs, lses):
    m = outs[0].shape[0]
    nc = len(outs)

    def body(*refs):
        o_refs, l_refs = refs[:nc], refs[nc:2 * nc]
        of_ref, ob_ref, lt_ref = refs[2 * nc:]
        ls = [r[...] for r in l_refs]
        mx = functools.reduce(jnp.maximum, ls)
        ws = [jnp.exp(l - mx) for l in ls]
        tot = functools.reduce(lambda a, b: a + b, ws)
        inv = 1.0 / tot
        o = functools.reduce(lambda a, b: a + b, [w * inv * r[...] for w, r in zip(ws, o_refs)])
        of_ref[...] = o
        ob_ref[...] = o.astype(BF16)
        lt_ref[...] = mx + jnp.log(tot)

    spec = pl.BlockSpec((ROW_TILE, B_WIDTH), lambda i: (i, 0))
    return pl.pallas_call(
        body, grid=(m // ROW_TILE,),
        in_specs=[spec] * (2 * nc), out_specs=[spec, spec, spec],
        out_shape=[jax.ShapeDtypeStruct((m, B_WIDTH), F32), jax.ShapeDtypeStruct((m, B_WIDTH), BF16),
                   jax.ShapeDtypeStruct((m, B_WIDTH), F32)],
        compiler_params=_params(("parallel",)), name="attn_combine",
    )(*outs, *lses)


def _attn_bwd(proj, dmix, o, lse, bias, dil, batch):
    m = proj.shape[0]
    tr = SEQ // dil
    nb = tr // Q_BLOCK
    proj3 = proj.reshape(batch, tr, dil * IN_COLS)
    dmix3 = dmix.reshape(batch, tr, dil * D_MODEL)
    o3 = o.reshape(batch, tr, dil * B_WIDTH)
    l3 = lse.reshape(batch, tr, dil * B_WIDTH)

    def body(q_ref, k_ref, v_ref, do_ref, o_ref, l_ref, b_ref, dq_ref, dk_ref, dv_ref, ds_ref):
        first = (pl.program_id(1) == 0) & (pl.program_id(2) == 0)

        @pl.when(first)
        def _():
            ds_ref[...] = jnp.zeros_like(ds_ref)

        dk_ref[...] = jnp.zeros_like(dk_ref)
        dv_ref[...] = jnp.zeros_like(dv_ref)
        masks = _head_masks()

        def block(n, carry):
            r0 = pl.multiple_of(n * Q_BLOCK, Q_BLOCK)
            rows = pl.ds(r0, Q_BLOCK)
            q = q_ref[rows, :] * 0.125
            kc = k_ref[rows, :].astype(BF16)
            vc = v_ref[rows, :].astype(BF16)
            do = do_ref[rows, :]
            ov = o_ref[rows, :]
            lv = l_ref[rows, :]
            if nb > 1:
                p0 = pl.multiple_of(jnp.maximum(n - 1, 0) * Q_BLOCK, Q_BLOCK)
                prow = pl.ds(p0, Q_BLOCK)
                kp = k_ref[prow, :].astype(BF16)
                vp = v_ref[prow, :].astype(BF16)
                dkp = jnp.zeros((Q_BLOCK, LANE_BLOCK), F32)
                dvp = jnp.zeros((Q_BLOCK, LANE_BLOCK), F32)
            dq = jnp.zeros((Q_BLOCK, LANE_BLOCK), F32)
            dkc = jnp.zeros((Q_BLOCK, LANE_BLOCK), F32)
            dvc = jnp.zeros((Q_BLOCK, LANE_BLOCK), F32)
            for h in range(HEADS_PER_BLOCK):
                qh = jnp.where(masks[h], q, 0.0).astype(BF16)
                doh = jnp.where(masks[h], do, 0.0)
                doh_bf = doh.astype(BF16)
                lrow = jnp.max(jnp.where(masks[h], lv, -3e38), axis=1, keepdims=True)
                drow = jnp.sum(doh * ov, axis=1, keepdims=True)
                pc = jnp.exp(_dot_nt(qh, kc) + b_ref[h, :, Q_BLOCK:] - lrow)
                dsc = pc * (_dot_nt(doh_bf, vc) - drow)
                ds_ref[h, :, Q_BLOCK:] += dsc
                dsc_bf = dsc.astype(BF16)
                dqh = _dot(dsc_bf, kc)
                dkc = dkc + _dot_tn(dsc_bf, qh)
                dvc = dvc + _dot_tn(pc.astype(BF16), doh_bf)
                if nb > 1:
                    bp = jnp.where(n == 0, NEG_INF, b_ref[h, :, :Q_BLOCK])
                    pp = jnp.exp(_dot_nt(qh, kp) + bp - lrow)
                    dsp = pp * (_dot_nt(doh_bf, vp) - drow)
                    ds_ref[h, :, :Q_BLOCK] += dsp
                    dsp_bf = dsp.astype(BF16)
                    dqh = dqh + _dot(dsp_bf, kp)
                    dkp = dkp + _dot_tn(dsp_bf, qh)
                    dvp = dvp + _dot_tn(pp.astype(BF16), doh_bf)
                dq = jnp.where(masks[h], dqh, dq)
            dq_ref[rows, :] = dq * 0.125
            dk_ref[rows, :] += dkc
            dv_ref[rows, :] += dvc
            if nb > 1:
                dk_ref[prow, :] += dkp
                dv_ref[prow, :] += dvp
            return carry

        if nb > 1:
            lax.fori_loop(0, nb, block, 0)
        else:
            block(0, 0)

    def in_spec(col0):
        return pl.BlockSpec((None, tr, LANE_BLOCK), lambda g, b, r: (b, 0, r * N_COLBLK + col0 + g))

    do_spec = pl.BlockSpec((None, tr, LANE_BLOCK), lambda g, b, r: (b, 0, r * (D_MODEL // LANE_BLOCK) + 1 + g))
    hd_spec = pl.BlockSpec((None, tr, LANE_BLOCK), lambda g, b, r: (b, 0, r * HEAD_BLOCKS + g))
    tbl_spec = pl.BlockSpec((HEADS_PER_BLOCK, Q_BLOCK, 2 * Q_BLOCK), lambda g, b, r: (g, 0, 0))
    out_sds = jax.ShapeDtypeStruct((batch, tr, dil * B_WIDTH), F32)
    dq, dk, dv, ds = pl.pallas_call(
        body, grid=(HEAD_BLOCKS, batch, dil),
        in_specs=[in_spec(Q_COL), in_spec(K_COL), in_spec(V_COL), do_spec, hd_spec, hd_spec, tbl_spec],
        out_specs=[hd_spec, hd_spec, hd_spec, tbl_spec],
        out_shape=[out_sds, out_sds, out_sds, jax.ShapeDtypeStruct((B_HEADS, Q_BLOCK, 2 * Q_BLOCK), F32)],
        compiler_params=_params(("parallel", "arbitrary", "arbitrary")), name=f"attn_bwd_d{dil}",
    )(proj3, proj3, proj3, dmix3, o3, l3, bias)
    return dq.reshape(m, B_WIDTH), dk.reshape(m, B_WIDTH), dv.reshape(m, B_WIDTH), ds


PAIR = 2 * HEAD_DIM
N_PAIR = B_HEADS // 2
N_CFG = len(DILATED)
BLOCKS_PER_CFG = SEQ // Q_BLOCK
QKV_SLABS = 3 * N_PAIR


def _proj_fwd(x, g1, w_in):
    m = x.shape[0]
    tm = ROW_TILE

    def body(x_ref, g_ref, w_ref, h_ref, uv_ref, qkv_ref):
        xv = x_ref[...]
        h = (xv * _rstd(xv) * g_ref[...]).astype(BF16)
        h_ref[...] = h
        acc = _dot(h, w_ref[...])
        uv_ref[...] = acc[:, :2 * A_WIDTH]
        for s in range(QKV_SLABS):
            qkv_ref[s] = acc[:, 2 * A_WIDTH + s * PAIR:2 * A_WIDTH + (s + 1) * PAIR]

    return pl.pallas_call(
        body, grid=(m // tm,),
        in_specs=[pl.BlockSpec((tm, D_MODEL), lambda i: (i, 0)), _vec_spec(),
                  pl.BlockSpec((D_MODEL, IN_COLS), lambda i: (0, 0))],
        out_specs=[pl.BlockSpec((tm, D_MODEL), lambda i: (i, 0)),
                   pl.BlockSpec((tm, 2 * A_WIDTH), lambda i: (i, 0)),
                   pl.BlockSpec((QKV_SLABS, tm, PAIR), lambda i: (0, i, 0))],
        out_shape=[jax.ShapeDtypeStruct((m, D_MODEL), BF16), jax.ShapeDtypeStruct((m, 2 * A_WIDTH), F32),
                   jax.ShapeDtypeStruct((QKV_SLABS, m, PAIR), F32)],
        compiler_params=_params(("parallel",)), name="proj_fwd",
    )(x, g1, w_in)


def _pair_masks():
    lane = lax.broadcasted_iota(jnp.int32, (1, PAIR), 1)
    return [lane < HEAD_DIM, lane >= HEAD_DIM]


def _block_rows(idx, dil):
    if dil == 1:
        n = idx
        cur = pl.ds(pl.multiple_of(n * Q_BLOCK, Q_BLOCK), Q_BLOCK)
        prev = pl.ds(pl.multiple_of(jnp.maximum(n - 1, 0) * Q_BLOCK, Q_BLOCK), Q_BLOCK)
        return n, cur, prev
    r = idx % dil
    n = idx // dil
    cur = pl.ds(r + (dil * Q_BLOCK) * n, Q_BLOCK, stride=dil)
    prev = pl.ds(r + (dil * Q_BLOCK) * jnp.maximum(n - 1, 0), Q_BLOCK, stride=dil)
    return n, cur, prev


def _attn_fwd(qkv, bias, batch):
    m = qkv.shape[1]
    comb_rows = 256

    def body(q_ref, k_ref, v_ref, b_ref, o_ref, l_ref, *scratch):
        oc_refs, lc_refs = scratch[:N_CFG], scratch[N_CFG:]
        masks = _pair_masks()
        for ci, (_, dil) in enumerate(DILATED):
            nb = SEQ // dil // Q_BLOCK

            def block(idx, carry, ci=ci, dil=dil, nb=nb):
                n, rows, prow = _block_rows(idx, dil)
                q = q_ref[rows, :] * 0.125
                kc = k_ref[rows, :].astype(BF16)
                vc = v_ref[rows, :].astype(BF16)
                if nb > 1:
                    kp = k_ref[prow, :].astype(BF16)
                    vp = v_ref[prow, :].astype(BF16)
                o_blk = jnp.zeros((Q_BLOCK, PAIR), F32)
                l_blk = jnp.zeros((Q_BLOCK, PAIR), F32)
                for h in range(2):
                    qh = jnp.where(masks[h], q, 0.0).astype(BF16)
                    sc = _dot_nt(qh, kc) + b_ref[ci, h, :, Q_BLOCK:]
                    mx = jnp.max(sc, axis=1, keepdims=True)
                    if nb > 1:
                        sp = _dot_nt(qh, kp) + jnp.where(n == 0, NEG_INF, b_ref[ci, h, :, :Q_BLOCK])
                        mx = jnp.maximum(mx, jnp.max(sp, axis=1, keepdims=True))
                    pc = jnp.exp(sc - mx)
                    den = jnp.sum(pc, axis=1, keepdims=True)
                    oh = _dot(pc.astype(BF16), vc)
                    if nb > 1:
                        pp = jnp.exp(sp - mx)
                        den = den + jnp.sum(pp, axis=1, keepdims=True)
                        oh = oh + _dot(pp.astype(BF16), vp)
                    o_blk = jnp.where(masks[h], oh / den, o_blk)
                    l_blk = jnp.where(masks[h], mx + jnp.log(den), l_blk)
                oc_refs[ci][rows, :] = o_blk
                lc_refs[ci][rows, :] = l_blk
                return carry

            lax.fori_loop(0, BLOCKS_PER_CFG, block, 0)

        def combine(i, carry):
            rr = pl.ds(pl.multiple_of(i * comb_rows, comb_rows), comb_rows)
            ls = [lc_refs[c][rr, :] for c in range(N_CFG)]
            mx = functools.reduce(jnp.maximum, ls)
            ws = [jnp.exp(l - mx) for l in ls]
            tot = functools.reduce(lambda a, b: a + b, ws)
            o = functools.reduce(lambda a, b: a + b, [ws[c] * oc_refs[c][rr, :] for c in range(N_CFG)]) / tot
            o_ref[rr, :] = o.astype(BF16)
            l_ref[rr, :] = mx + jnp.log(tot)
            return carry

        lax.fori_loop(0, SEQ // comb_rows, combine, 0)

    def slab(first):
        return pl.BlockSpec((None, SEQ, PAIR), lambda b, p: (first + p, b, 0))

    nat = pl.BlockSpec((SEQ, PAIR), lambda b, p: (b, p))
    return pl.pallas_call(
        body, grid=(batch, N_PAIR),
        in_specs=[slab(0), slab(N_PAIR), slab(2 * N_PAIR),
                  pl.BlockSpec((N_CFG, 2, Q_BLOCK, 2 * Q_BLOCK), lambda b, p: (0, p, 0, 0))],
        out_specs=[nat, nat],
        out_shape=[jax.ShapeDtypeStruct((m, B_WIDTH), BF16), jax.ShapeDtypeStruct((m, B_WIDTH), F32)],
        scratch_shapes=[pltpu.VMEM((SEQ, PAIR), F32)] * (2 * N_CFG),
        compiler_params=_params(("parallel", "parallel")), name="attn_fwd",
    )(qkv, qkv, qkv, bias)


def _attn_bwd(qkv, dmix, o, lse, bias, batch):
    m = qkv.shape[1]

    def body(q_ref, k_ref, v_ref, do_ref, o_ref, l_ref, b_ref, dqkv_ref, ds_ref, dq_acc, dk_acc, dv_acc, d_scr):
        @pl.when(pl.program_id(1) == 0)
        def _():
            ds_ref[...] = jnp.zeros_like(ds_ref)

        dq_acc[...] = jnp.zeros_like(dq_acc)
        dk_acc[...] = jnp.zeros_like(dk_acc)
        dv_acc[...] = jnp.zeros_like(dv_acc)
        masks = _pair_masks()
        ri = lax.broadcasted_iota(jnp.int32, (PAIR, PAIR), 0)
        cj = lax.broadcasted_iota(jnp.int32, (PAIR, PAIR), 1)
        same_head = ((ri < HEAD_DIM) == (cj < HEAD_DIM)).astype(F32)
        d_scr[...] = _dot(do_ref[...] * o_ref[...].astype(F32), same_head, lax.Precision.HIGHEST)

        for ci, (_, dil) in enumerate(DILATED):
            nb = SEQ // dil // Q_BLOCK

            def block(idx, carry, ci=ci, dil=dil, nb=nb):
                n, rows, prow = _block_rows(idx, dil)
                q = q_ref[rows, :] * 0.125
                kc = k_ref[rows, :].astype(BF16)
                vc = v_ref[rows, :].astype(BF16)
                do = do_ref[rows, :]
                lv = l_ref[rows, :]
                dv_ = d_scr[rows, :]
                if nb > 1:
                    kp = k_ref[prow, :].astype(BF16)
                    vp = v_ref[prow, :].astype(BF16)
                    dkp = jnp.zeros((Q_BLOCK, PAIR), F32)
                    dvp = jnp.zeros((Q_BLOCK, PAIR), F32)
                dq = jnp.zeros((Q_BLOCK, PAIR), F32)
                dkc = jnp.zeros((Q_BLOCK, PAIR), F32)
                dvc = jnp.zeros((Q_BLOCK, PAIR), F32)
                for h in range(2):
                    c0 = h * HEAD_DIM
                    lrow = lv[:, c0:c0 + 1]
                    drow = dv_[:, c0:c0 + 1]
                    qh = jnp.where(masks[h], q, 0.0).astype(BF16)
                    doh = jnp.where(masks[h], do, 0.0).astype(BF16)
                    pc = jnp.exp(_dot_nt(qh, kc) + b_ref[ci, h, :, Q_BLOCK:] - lrow)
                    dsc = pc * (_dot_nt(doh, vc) - drow)
                    ds_ref[ci, h, :, Q_BLOCK:] += dsc
                    dsc_bf = dsc.astype(BF16)
                    dqh = _dot(dsc_bf, kc)
                    dkc = dkc + _dot_tn(dsc_bf, qh)
                    dvc = dvc + _dot_tn(pc.astype(BF16), doh)
                    if nb > 1:
                        bp = jnp.where(n == 0, NEG_INF, b_ref[ci, h, :, :Q_BLOCK])
                        pp = jnp.exp(_dot_nt(qh, kp) + bp - lrow)
                        dsp = pp * (_dot_nt(doh, vp) - drow)
                        ds_ref[ci, h, :, :Q_BLOCK] += dsp
                        dsp_bf = dsp.astype(BF16)
                        dqh = dqh + _dot(dsp_bf, kp)
                        dkp = dkp + _dot_tn(dsp_bf, qh)
                        dvp = dvp + _dot_tn(pp.astype(BF16), doh)
                    dq = jnp.where(masks[h], dqh, dq)
                dq_acc[rows, :] += dq * 0.125
                dk_acc[rows, :] += dkc
                dv_acc[rows, :] += dvc
                if nb > 1:
                    dk_acc[prow, :] += dkp
                    dv_acc[prow, :] += dvp
                return carry

            lax.fori_loop(0, BLOCKS_PER_CFG, block, 0)

        dqkv_ref[0] = dq_acc[...].astype(BF16)
        dqkv_ref[1] = dk_acc[...].astype(BF16)
        dqkv_ref[2] = dv_acc[...].astype(BF16)

    def slab(first):
        return pl.BlockSpec((None, SEQ, PAIR), lambda p, b: (first + p, b, 0))

    nat = pl.BlockSpec((SEQ, PAIR), lambda p, b: (b, p))
    tbl = pl.BlockSpec((N_CFG, 2, Q_BLOCK, 2 * Q_BLOCK), lambda p, b: (0, p, 0, 0))
    acc = pltpu.VMEM((SEQ, PAIR), F32)
    return pl.pallas_call(
        body, grid=(N_PAIR, batch),
        in_specs=[slab(0), slab(N_PAIR), slab(2 * N_PAIR),
                  pl.BlockSpec((SEQ, PAIR), lambda p, b: (b, A_WIDTH // PAIR + p)), nat, nat, tbl],
        out_specs=[pl.BlockSpec((3, SEQ, PAIR), lambda p, b: (0, b, p)), tbl],
        out_shape=[jax.ShapeDtypeStruct((3, m, B_WIDTH), BF16),
                   jax.ShapeDtypeStruct((N_CFG, B_HEADS, Q_BLOCK, 2 * Q_BLOCK), F32)],
        scratch_shapes=[acc, acc, acc, acc],
        compiler_params=_params(("parallel", "arbitrary")), name="attn_bwd",
    )(qkv, qkv, qkv, dmix, o, lse, bias)


def _rel_bias_grad(ds, buckets_np):
    present = [sorted(set(int(v) for v in np.unique(buckets_np[c]) if v >= 0)) for c in range(N_CFG)]

    def body(bk_ref, ds_ref, o_ref, acc_ref):
        acc_ref[...] = jnp.zeros_like(acc_ref)
        for c in range(N_CFG):
            bk = bk_ref[c]
            for h in range(B_HEADS):
                dsv = ds_ref[c, h]
                for b in present[c]:
                    part = jnp.sum(jnp.where(bk == b, dsv, 0.0), axis=0, keepdims=True)
                    acc_ref[pl.ds(h * NUM_BUCKETS + b, 1), :] += part
        o_ref[...] = jnp.sum(acc_ref[...], axis=1, keepdims=True)

    vm = pl.BlockSpec(memory_space=pltpu.VMEM)
    return pl.pallas_call(
        body, in_specs=[vm, vm], out_specs=vm,
        out_shape=jax.ShapeDtypeStruct((B_HEADS * NUM_BUCKETS, 1), F32),
        scratch_shapes=[pltpu.VMEM((B_HEADS * NUM_BUCKETS, 2 * Q_BLOCK), F32)],
        compiler_params=_params(), name="rel_bias_grad",
    )(jnp.asarray(buckets_np), ds)


def _assemble_dproj(duv, dqkv):
    m = duv.shape[0]

    def body(duv_ref, dqkv_ref, o_ref):
        j = pl.program_id(1)

        @pl.when(j < 2)
        def _():
            o_ref[...] = duv_ref[...].astype(BF16)

        @pl.when(j >= 2)
        def _():
            o_ref[...] = dqkv_ref[...]

    return pl.pallas_call(
        body, grid=(m // ROW_TILE, N_COLBLK),
        in_specs=[pl.BlockSpec((ROW_TILE, LANE_BLOCK), lambda i, j: (i, jnp.minimum(j, 1))),
                  pl.BlockSpec((None, ROW_TILE, LANE_BLOCK),
                               lambda i, j: (jnp.maximum(j - 2, 0) // HEAD_BLOCKS, i, jnp.maximum(j - 2, 0) % HEAD_BLOCKS))],
        out_specs=pl.BlockSpec((ROW_TILE, LANE_BLOCK), lambda i, j: (i, j)),
        out_shape=jax.ShapeDtypeStruct((m, IN_COLS), BF16),
        compiler_params=_params(("parallel", "arbitrary")), name="assemble_dproj",
    )(duv, dqkv)


def _shift_down(x, k):
    row = lax.broadcasted_iota(jnp.int32, x.shape, 0)
    return jnp.where(row >= k, pltpu.roll(x, k, 0), 0.0)


def _shift_up(x, k):
    n = x.shape[0]
    row = lax.broadcasted_iota(jnp.int32, x.shape, 0)
    return jnp.where(row < n - k, pltpu.roll(x, n - k, 0), 0.0)


def _convgate_fwd(gate, up, conv_w, conv_b, batch):
    m = gate.shape[0]

    def body(g_ref, u_ref, w_ref, b_ref, a_ref):
        g = g_ref[...]
        w = w_ref[...]
        c = b_ref[...] + w[0:1] * _shift_down(g, 2) + w[1:2] * _shift_down(g, 1) + w[2:3] * g
        a_ref[...] = (_gelu(c) * u_ref[...]).astype(BF16)

    blk = pl.BlockSpec((SEQ, LANE_BLOCK), lambda b, j: (b, j))
    return pl.pallas_call(
        body, grid=(batch, D_FF // LANE_BLOCK),
        in_specs=[blk, blk, pl.BlockSpec((3, LANE_BLOCK), lambda b, j: (0, j)),
                  pl.BlockSpec((1, LANE_BLOCK), lambda b, j: (0, j))],
        out_specs=blk,
        out_shape=jax.ShapeDtypeStruct((m, D_FF), BF16),
        compiler_params=_params(("parallel", "parallel")), name="convgate_fwd",
    )(gate, up, conv_w, conv_b)


def _convgate_bwd(gate, up, dact, conv_w, conv_b, batch):
    m = gate.shape[0]

    def body(g_ref, u_ref, da_ref, w_ref, b_ref, dg_ref, du_ref, dw_ref, db_ref):
        @pl.when(pl.program_id(1) == 0)
        def _():
            dw_ref[...] = jnp.zeros_like(dw_ref)
            db_ref[...] = jnp.zeros_like(db_ref)

        g = g_ref[...]
        w = w_ref[...]
        g1 = _shift_down(g, 1)
        g2 = _shift_down(g, 2)
        c = b_ref[...] + w[0:1] * g2 + w[1:2] * g1 + w[2:3] * g
        gg, dgg = _gelu_and_grad(c)
        da = da_ref[...]
        du_ref[...] = (da * gg).astype(BF16)
        dc = da * u_ref[...] * dgg
        db_ref[...] += jnp.sum(dc, axis=0, keepdims=True)
        dw_ref[0:1, :] += jnp.sum(dc * g2, axis=0, keepdims=True)
        dw_ref[1:2, :] += jnp.sum(dc * g1, axis=0, keepdims=True)
        dw_ref[2:3, :] += jnp.sum(dc * g, axis=0, keepdims=True)
        dg_ref[...] = (w[2:3] * dc + w[1:2] * _shift_up(dc, 1) + w[0:1] * _shift_up(dc, 2)).astype(BF16)

    blk = pl.BlockSpec((SEQ, LANE_BLOCK), lambda j, b: (b, j))
    wspec = pl.BlockSpec((3, LANE_BLOCK), lambda j, b: (0, j))
    bspec = pl.BlockSpec((1, LANE_BLOCK), lambda j, b: (0, j))
    return pl.pallas_call(
        body, grid=(D_FF // LANE_BLOCK, batch),
        in_specs=[blk, blk, blk, wspec, bspec],
        out_specs=[blk, blk, wspec, bspec],
        out_shape=[jax.ShapeDtypeStruct((m, D_FF), BF16), jax.ShapeDtypeStruct((m, D_FF), BF16),
                   jax.ShapeDtypeStruct((3, D_FF), F32), jax.ShapeDtypeStruct((1, D_FF), F32)],
        compiler_params=_params(("parallel", "arbitrary")), name="convgate_bwd",
    )(gate, up, dact, conv_w, conv_b)


def _local_step(x, tgt, g1, g2, g3, g4, w_in, ln_g, ln_b, w_s, b_s, rel_bias, w_out, w_gate, w_up,
                conv_w, conv_b, w_down, batch):
    big = dict(tm=1024, out_dtype=F32)
    buckets = _bucket_tables()
    bias = _bias_tables(rel_bias, buckets)
    bz = jnp.repeat(b_s.T, HEAD_DIM, axis=1)
    w_st = jnp.swapaxes(w_s, 1, 2)

    h1, uv, qkv = _proj_fwd(x, g1, w_in)
    a = _gate_fwd(uv, ln_g, ln_b, w_s, bz)
    o_bf, lse = _attn_fwd(qkv, bias, batch)
    y1 = _mm(a, w_out[:A_WIDTH], dims="nn", tn=1024, tk=A_WIDTH, name="mm_out_a", **big)
    y1 = _mm(o_bf, w_out[A_WIDTH:], dims="nn", tn=1024, tk=B_WIDTH, name="mm_out_b", add=y1, **big)
    x1, h2 = _mid_fwd(x, y1, g2, g3)
    gate = _mm(h2, w_gate, dims="nn", tn=1408, tk=1024, name="mm_gate", **big)
    up = _mm(h2, w_up, dims="nn", tn=1408, tk=1024, name="mm_up", **big)
    act = _convgate_fwd(gate, up, conv_w, conv_b, batch)
    y2 = _mm(act, w_down, dims="nn", tn=1024, tk=1408, name="mm_down", **big)
    dx2, dy2, dg4, loss = _loss_head(x1, y2, tgt, g4)

    dact = _mm(dy2, w_down, dims="nt", tn=1408, tk=1024, name="mm_dact", **big)
    dw_down = _mm(act, dy2, dims="tn", tm=1408, tn=1024, tk=1024, out_dtype=F32, name="mm_dw_down")
    dgate, dup, dconv_w, dconv_b = _convgate_bwd(gate, up, dact, conv_w, conv_b, batch)
    dh2 = _mm(dgate, w_gate, dims="nt", tn=1024, tk=1408, name="mm_dh2_g", **big)
    dh2 = _mm(dup, w_up, dims="nt", tn=1024, tk=1408, name="mm_dh2_u", add=dh2, **big)
    dw_gate = _mm(h2, dgate, dims="tn", tm=1024, tn=1408, tk=1024, out_dtype=F32, name="mm_dw_gate")
    dw_up = _mm(h2, dup, dims="tn", tm=1024, tn=1408, tk=1024, out_dtype=F32, name="mm_dw_up")
    dx1, dy1, dg2, dg3 = _mid_bwd(x1, y1, dh2, dx2, g2, g3)
    dmix = _mm(dy1, w_out, dims="nt", tn=1024, tk=1024, name="mm_dmix", **big)
    dw_out_a = _mm(a, dy1, dims="tn", tm=A_WIDTH, tn=1024, tk=1024, out_dtype=F32, name="mm_dw_out_a")
    dw_out_b = _mm(o_bf, dy1, dims="tn", tm=B_WIDTH, tn=1024, tk=1024, out_dtype=F32, name="mm_dw_out_b")
    duv, dln_g, dln_b, dw_s, dbz = _gate_bwd(uv, dmix, ln_g, ln_b, w_s, w_st, bz)
    dqkv, ds = _attn_bwd(qkv, dmix, o_bf, lse, bias, batch)
    drel = _rel_bias_grad(ds, buckets)
    dproj = _assemble_dproj(duv, dqkv)
    dh1 = _mm(dproj, w_in, dims="nt", tn=1024, tk=1408, name="mm_dh1", **big)
    dw_in = _mm(h1, dproj, dims="tn", tm=1024, tn=1408, tk=1024, out_dtype=F32, name="mm_dw_in")
    dx0, dg1 = _in_bwd(x, dh1, dx1, g1)

    grads = dict(
        norm_mix_pre=dg1, norm_mix_post=dg2, norm_ffn_pre=dg3, norm_ffn_post=dg4,
        w_in=dw_in, ln_v_gain=dln_g, ln_v_bias=dln_b, spatial_w=dw_s,
        spatial_b=dbz[:, ::HEAD_DIM].T,
        rel_bias=drel.reshape(B_HEADS, NUM_BUCKETS).T,
        w_out=jnp.concatenate([dw_out_a, dw_out_b], axis=0),
        w_gate=dw_gate, w_up=dw_up, conv_w=dconv_w, conv_b=dconv_b, w_down=dw_down,
    )
    return loss, dx0, grads


def _mesh_pos():
    x, y, c = lax.axis_index("x"), lax.axis_index("y"), lax.axis_index("c")
    chips = [(1 - x, y), (x, 1 - y), (1 - x, 1 - y)]
    return x, y, c, chips


ANY = pl.BlockSpec(memory_space=pl.ANY)


def _gather_weights(shards, conv_w_shard):
    nt = len(shards)

    def body(*refs):
        shard_refs = refs[:nt]
        cw_ref = refs[nt]
        out_refs = refs[nt + 1:2 * nt + 1]
        cw_out = refs[2 * nt + 1]
        send_sems, recv_sems = refs[2 * nt + 2:]
        x, y, c, chips = _mesh_pos()
        s = 2 * x + y
        sib = (x, y, 1 - c)

        def half(ref, chip, which, t):
            rows = shards[t].shape[0] // 2
            return ref.at[2 * chip[0] + chip[1], pl.ds(which * rows, rows), :]

        def rcopy(k, src, dst, to):
            return pltpu.make_async_remote_copy(src_ref=src, dst_ref=dst, send_sem=send_sems.at[k],
                                                recv_sem=recv_sems.at[k], device_id=to, device_id_type=MESH)

        sends = []
        for t in range(nt):
            rows = shards[t].shape[0] // 2
            for j, chip in enumerate(chips):
                sends.append(rcopy(7 * t + j, shard_refs[t].at[pl.ds(c * rows, rows), :],
                                   half(out_refs[t], (x, y), c, t), (*chip, c)))
        for j, chip in enumerate(chips):
            sends.append(rcopy(7 * nt + j, cw_ref, cw_out.at[s], (*chip, c)))
        for cp in sends:
            cp.start()
        fwd = []
        for t in range(nt):
            for j, chip in enumerate(chips):
                landed = half(out_refs[t], chip, c, t)
                rcopy(7 * t + j, landed, landed, (*chip, c)).wait_recv()
                f = rcopy(7 * t + 3 + j, landed, landed, sib)
                f.start()
                fwd.append(f)
        for j, chip in enumerate(chips):
            dst = cw_out.at[2 * chip[0] + chip[1]]
            rcopy(7 * nt + j, dst, dst, (*chip, c)).wait_recv()
        for t in range(nt):
            for j, chip in enumerate(chips):
                other = half(out_refs[t], chip, 1 - c, t)
                rcopy(7 * t + 3 + j, other, other, sib).wait_recv()
        for cp in sends + fwd:
            cp.wait_send()

    out_shape = [jax.ShapeDtypeStruct((N_SHARD,) + sh.shape, sh.dtype) for sh in shards]
    out_shape.append(jax.ShapeDtypeStruct((N_SHARD,) + conv_w_shard.shape, conv_w_shard.dtype))
    nsem = 7 * nt + 3
    return pl.pallas_call(
        body, in_specs=[ANY] * (nt + 1), out_specs=[ANY] * (nt + 1), out_shape=out_shape,
        scratch_shapes=[pltpu.SemaphoreType.DMA((nsem,)), pltpu.SemaphoreType.DMA((nsem,))],
        compiler_params=pltpu.CompilerParams(has_side_effects=True), name="gather_weights",
    )(*shards, conv_w_shard)


def _exchange_halves(grads):
    nt = len(grads)

    def body(*refs):
        g_refs = refs[:nt]
        out_refs = refs[nt:2 * nt]
        send_sems, recv_sems = refs[2 * nt:]
        x, y, c, _ = _mesh_pos()
        copies = []
        for t in range(nt):
            rows = grads[t].shape[1] // 2
            copies.append(pltpu.make_async_remote_copy(
                src_ref=g_refs[t].at[:, pl.ds((1 - c) * rows, rows), :], dst_ref=out_refs[t],
                send_sem=send_sems.at[t], recv_sem=recv_sems.at[t], device_id=(x, y, 1 - c), device_id_type=MESH))
        for cp in copies:
            cp.start()
        for cp in copies:
            cp.wait()

    out_shape = [jax.ShapeDtypeStruct((N_SHARD, g.shape[1] // 2, g.shape[2]), g.dtype) for g in grads]
    return pl.pallas_call(
        body, in_specs=[ANY] * nt, out_specs=[ANY] * nt, out_shape=out_shape,
        scratch_shapes=[pltpu.SemaphoreType.DMA((nt,)), pltpu.SemaphoreType.DMA((nt,))],
        compiler_params=pltpu.CompilerParams(has_side_effects=True), name="rs_sibling_exchange",
    )(*grads)


def _add_halves(g, recv, c_idx):
    _, rows2, cols = g.shape
    rows = rows2 // 2
    tr = rows // 2 if rows % 16 == 0 and rows >= 256 else rows
    nblk = rows // tr

    def body(c_ref, g_ref, r_ref, o_ref):
        o_ref[...] = (g_ref[...] + r_ref[...]).astype(BF16)

    return pl.pallas_call(
        body,
        grid_spec=pltpu.PrefetchScalarGridSpec(
            num_scalar_prefetch=1, grid=(N_SHARD, nblk),
            in_specs=[pl.BlockSpec((None, tr, cols), lambda s, i, c: (s, c[0] * nblk + i, 0)),
                      pl.BlockSpec((None, tr, cols), lambda s, i, c: (s, i, 0))],
            out_specs=pl.BlockSpec((None, tr, cols), lambda s, i, c: (s, i, 0))),
        out_shape=jax.ShapeDtypeStruct((N_SHARD, rows, cols), BF16),
        compiler_params=_params(("parallel", "parallel")), name="rs_add_halves",
    )(c_idx, g, recv)


def _exchange_chips(parts):
    nt = len(parts)

    def body(*refs):
        p_refs = refs[:nt]
        out_refs = refs[nt:2 * nt]
        send_sems, recv_sems = refs[2 * nt:]
        x, y, c, chips = _mesh_pos()
        copies = []
        for t in range(nt):
            for j, chip in enumerate(chips):
                copies.append(pltpu.make_async_remote_copy(
                    src_ref=p_refs[t].at[2 * chip[0] + chip[1]], dst_ref=out_refs[t].at[j],
                    send_sem=send_sems.at[3 * t + j], recv_sem=recv_sems.at[3 * t + j],
                    device_id=(*chip, c), device_id_type=MESH))
        for cp in copies:
            cp.start()
        for cp in copies:
            cp.wait()

    out_shape = [jax.ShapeDtypeStruct((3,) + p.shape[1:], p.dtype) for p in parts]
    return pl.pallas_call(
        body, in_specs=[ANY] * nt, out_specs=[ANY] * nt, out_shape=out_shape,
        scratch_shapes=[pltpu.SemaphoreType.DMA((3 * nt,)), pltpu.SemaphoreType.DMA((3 * nt,))],
        compiler_params=pltpu.CompilerParams(has_side_effects=True), name="rs_chip_exchange",
    )(*parts)


def _add_chips(part, recv, s_idx, c_idx):
    _, rows, cols = part.shape
    tr = rows // 2 if rows % 32 == 0 and rows >= 256 else rows
    nblk = rows // tr

    def body(idx_ref, p_ref, r_ref, o_ref):
        acc = p_ref[...].astype(F32)
        for j in range(3):
            acc = acc + r_ref[j].astype(F32)
        o_ref[...] = acc

    return pl.pallas_call(
        body,
        grid_spec=pltpu.PrefetchScalarGridSpec(
            num_scalar_prefetch=1, grid=(nblk,),
            in_specs=[pl.BlockSpec((None, tr, cols), lambda i, idx: (idx[0], i, 0)),
                      pl.BlockSpec((3, tr, cols), lambda i, idx: (0, i, 0))],
            out_specs=pl.BlockSpec((tr, cols), lambda i, idx: (idx[1] * nblk + i, 0))),
        out_shape=jax.ShapeDtypeStruct((2 * rows, cols), F32),
        compiler_params=_params(("parallel",)), name="rs_add_chips",
    )(jnp.concatenate([s_idx, c_idx]), part, recv)


def _share_halves(fulls):
    nt = len(fulls)

    def body(*refs):
        out_refs = refs[nt:2 * nt]
        send_sems, recv_sems = refs[2 * nt:]
        x, y, c, _ = _mesh_pos()
        copies = []
        for t in range(nt):
            rows = fulls[t].shape[0] // 2
            mine = out_refs[t].at[pl.ds(c * rows, rows), :]
            copies.append(pltpu.make_async_remote_copy(
                src_ref=mine, dst_ref=mine, send_sem=send_sems.at[t], recv_sem=recv_sems.at[t],
                device_id=(x, y, 1 - c), device_id_type=MESH))
        for cp in copies:
            cp.start()
        for t in range(nt):
            rows = fulls[t].shape[0] // 2
            theirs = out_refs[t].at[pl.ds((1 - c) * rows, rows), :]
            pltpu.make_async_remote_copy(
                src_ref=theirs, dst_ref=theirs, send_sem=send_sems.at[t], recv_sem=recv_sems.at[t],
                device_id=(x, y, 1 - c), device_id_type=MESH).wait_recv()
        for cp in copies:
            cp.wait_send()

    out_shape = [jax.ShapeDtypeStruct(f.shape, f.dtype) for f in fulls]
    return pl.pallas_call(
        body, in_specs=[ANY] * nt, out_specs=[ANY] * nt, out_shape=out_shape,
        input_output_aliases={t: t for t in range(nt)},
        scratch_shapes=[pltpu.SemaphoreType.DMA((nt,)), pltpu.SemaphoreType.DMA((nt,))],
        compiler_params=pltpu.CompilerParams(has_side_effects=True), name="rs_share_halves",
    )(*fulls)


def _allreduce_small(packed):
    rows = packed.shape[0]

    def body(p_ref, o_ref, sib_ref, chip_ref, send_sems, recv_sems):
        x, y, c, chips = _mesh_pos()
        first = pltpu.make_async_remote_copy(src_ref=p_ref, dst_ref=sib_ref, send_sem=send_sems.at[0],
                                             recv_sem=recv_sems.at[0], device_id=(x, y, 1 - c), device_id_type=MESH)
        first.start()
        first.wait()
        o_ref[...] = p_ref[...] + sib_ref[...]
        copies = [pltpu.make_async_remote_copy(src_ref=o_ref, dst_ref=chip_ref.at[j], send_sem=send_sems.at[1 + j],
                                               recv_sem=recv_sems.at[1 + j], device_id=(*chip, c), device_id_type=MESH)
                  for j, chip in enumerate(chips)]
        for cp in copies:
            cp.start()
        for cp in copies:
            cp.wait()
        o_ref[...] = (o_ref[...] + chip_ref[0]) + (chip_ref[1] + chip_ref[2])

    vm = pl.BlockSpec(memory_space=pltpu.VMEM)
    return pl.pallas_call(
        body, in_specs=[vm], out_specs=vm, out_shape=jax.ShapeDtypeStruct(packed.shape, F32),
        scratch_shapes=[pltpu.VMEM((rows, 128), F32), pltpu.VMEM((3, rows, 128), F32),
                        pltpu.SemaphoreType.DMA((4,)), pltpu.SemaphoreType.DMA((4,))],
        compiler_params=pltpu.CompilerParams(has_side_effects=True, vmem_limit_bytes=VMEM_LIMIT),
        name="allreduce_small",
    )(packed)


def _adamw(w, g, m, v, name):
    rows, cols = w.shape
    tr = rows
    if rows * cols > 256 * 1024:
        tr = next(cand for cand in (256, 176, 128) if rows % cand == 0)

    def body(w_ref, g_ref, m_ref, v_ref, go_ref, d_ref, nm_ref, nv_ref):
        gv = g_ref[...]
        go_ref[...] = gv
        nm = ADAM_B1 * m_ref[...] + (1.0 - ADAM_B1) * gv
        nv = ADAM_B2 * v_ref[...] + (1.0 - ADAM_B2) * (gv * gv)
        m_hat = nm / (1.0 - ADAM_B1 ** ADAM_STEP)
        v_hat = nv / (1.0 - ADAM_B2 ** ADAM_STEP)
        d_ref[...] = -ADAM_LR * (m_hat / (jnp.sqrt(v_hat) + ADAM_EPS) + ADAM_WD * w_ref[...])
        nm_ref[...] = nm
        nv_ref[...] = nv

    spec = pl.BlockSpec((tr, cols), lambda i: (i, 0))
    sds = jax.ShapeDtypeStruct((rows, cols), F32)
    return pl.pallas_call(
        body, grid=(rows // tr,), in_specs=[spec] * 4, out_specs=[spec] * 4, out_shape=[sds] * 4,
        compiler_params=_params(("parallel",)), name=name,
    )(w, g, m, v)


def _pack(arrays, rows):
    flat = jnp.concatenate([a.reshape(-1) for a in arrays])
    flat = jnp.pad(flat, (0, rows * 128 - flat.shape[0]))
    return flat.reshape(rows, 128)


def _unpack(packed, shapes):
    flat = packed.reshape(-1)
    out, off = [], 0
    for sh in shapes:
        n = int(np.prod(sh))
        out.append(flat[off:off + n].reshape(sh))
        off += n
    return out


SMALL = ["norm_mix_pre", "norm_mix_post", "norm_ffn_pre", "norm_ffn_post", "ln_v_gain", "ln_v_bias",
         "spatial_w", "spatial_b", "rel_bias", "conv_b"]
LARGE = ["w_in", "w_gate", "w_up", "w_down", "w_out"]
ORDER = ["norm_mix_pre", "norm_mix_post", "norm_ffn_pre", "norm_ffn_post", "w_in", "ln_v_gain", "ln_v_bias",
         "spatial_w", "spatial_b", "rel_bias", "w_out", "w_gate", "w_up", "conv_w", "conv_b", "w_down"]


def _col_shards(full):
    rows, cols4 = full.shape
    return full.reshape(rows, N_SHARD, cols4 // N_SHARD).transpose(1, 0, 2)


def _from_col_shards(g):
    n, rows, cols = g.shape
    return g.transpose(1, 0, 2).reshape(rows, n * cols)


def kernel(x, norm_mix_pre, norm_mix_post, norm_ffn_pre, norm_ffn_post, w_in, ln_v_gain, ln_v_bias, spatial_w, spatial_b, rel_bias, w_out, w_gate, w_up, conv_w, conv_b, w_down, loss_target, m_norm_mix_pre, m_norm_mix_post, m_norm_ffn_pre, m_norm_ffn_post, m_w_in, m_ln_v_gain, m_ln_v_bias, m_spatial_w, m_spatial_b, m_rel_bias, m_w_out, m_w_gate, m_w_up, m_conv_w, m_conv_b, m_w_down, v_norm_mix_pre, v_norm_mix_post, v_norm_ffn_pre, v_norm_ffn_post, v_w_in, v_ln_v_gain, v_ln_v_bias, v_spatial_w, v_spatial_b, v_rel_bias, v_w_out, v_w_gate, v_w_up, v_conv_w, v_conv_b, v_w_down):
    params = dict(norm_mix_pre=norm_mix_pre, norm_mix_post=norm_mix_post, norm_ffn_pre=norm_ffn_pre,
                  norm_ffn_post=norm_ffn_post, w_in=w_in, ln_v_gain=ln_v_gain, ln_v_bias=ln_v_bias,
                  spatial_w=spatial_w, spatial_b=spatial_b, rel_bias=rel_bias, w_out=w_out, w_gate=w_gate,
                  w_up=w_up, conv_w=conv_w, conv_b=conv_b, w_down=w_down)
    mom = dict(norm_mix_pre=m_norm_mix_pre, norm_mix_post=m_norm_mix_post, norm_ffn_pre=m_norm_ffn_pre,
               norm_ffn_post=m_norm_ffn_post, w_in=m_w_in, ln_v_gain=m_ln_v_gain, ln_v_bias=m_ln_v_bias,
               spatial_w=m_spatial_w, spatial_b=m_spatial_b, rel_bias=m_rel_bias, w_out=m_w_out, w_gate=m_w_gate,
               w_up=m_w_up, conv_w=m_conv_w, conv_b=m_conv_b, w_down=m_w_down)
    var = dict(norm_mix_pre=v_norm_mix_pre, norm_mix_post=v_norm_mix_post, norm_ffn_pre=v_norm_ffn_pre,
               norm_ffn_post=v_norm_ffn_post, w_in=v_w_in, ln_v_gain=v_ln_v_gain, ln_v_bias=v_ln_v_bias,
               spatial_w=v_spatial_w, spatial_b=v_spatial_b, rel_bias=v_rel_bias, w_out=v_w_out, w_gate=v_w_gate,
               w_up=v_w_up, conv_w=v_conv_w, conv_b=v_conv_b, w_down=v_w_down)

    batch = x.shape[0]
    xi, yi, ci = lax.axis_index("x"), lax.axis_index("y"), lax.axis_index("c")
    s_idx = (2 * xi + yi).astype(jnp.int32).reshape(1)
    c_idx = ci.astype(jnp.int32).reshape(1)

    shards = [params[n][0].astype(BF16) for n in LARGE]
    gathered = _gather_weights(shards, conv_w[0])
    g_in, g_gate, g_up, g_down, g_out, g_convw = [
        lax.dynamic_update_index_in_dim(g, own, s_idx[0], 0) for g, own in zip(gathered, shards + [conv_w[0]])]
    w_in_f = _from_col_shards(g_in)
    w_gate_f = _from_col_shards(g_gate)
    w_up_f = _from_col_shards(g_up)
    w_down_f = g_down.reshape(D_FF, D_MODEL)
    w_out_f = g_out.reshape(D_MODEL, D_MODEL)
    conv_w_f = _from_col_shards(g_convw)

    loss_part, dx0, grads = _local_step(
        x.reshape(batch * SEQ, D_MODEL), loss_target.reshape(batch * SEQ, D_MODEL),
        norm_mix_pre, norm_mix_post, norm_ffn_pre, norm_ffn_post, w_in_f,
        ln_v_gain.reshape(1, A_WIDTH), ln_v_bias.reshape(1, A_WIDTH), spatial_w[0], spatial_b[0], rel_bias,
        w_out_f, w_gate_f, w_up_f, conv_w_f, conv_b, w_down_f, batch)
    loss = lax.psum(loss_part[0, 0], ("x", "y", "c"))
    grad_x = dx0.reshape(batch, SEQ, D_MODEL)

    big = [_col_shards(grads["w_in"]), _col_shards(grads["w_gate"]), _col_shards(grads["w_up"]),
           grads["w_down"].reshape(N_SHARD, SHARD_FF, D_MODEL),
           grads["w_out"].reshape(N_SHARD, D_MODEL // N_SHARD, D_MODEL)]
    recv_a = _exchange_halves(big)
    parts = [_add_halves(g, r, c_idx) for g, r in zip(big, recv_a)]
    recv_b = _exchange_chips(parts)
    fulls = [_add_chips(p, r, s_idx, c_idx) for p, r in zip(parts, recv_b)]
    reduced = dict(zip(LARGE, _share_halves(fulls)))

    small_g = [grads[n].reshape(params[n].shape) for n in SMALL] + [grads["conv_w"]]
    n_small = sum(int(np.prod(g.shape)) for g in small_g)
    small_rows = -(-n_small // (8 * 128)) * 8
    summed = _unpack(_allreduce_small(_pack(small_g, small_rows)),
                     [params[n].shape for n in SMALL] + [(3, D_FF)])
    for n, g in zip(SMALL, summed[:-1]):
        reduced[n] = g
    reduced["conv_w"] = lax.dynamic_slice_in_dim(summed[-1], s_idx[0] * SHARD_FF, SHARD_FF, axis=1)[None]

    out_g, out_d, out_m, out_v = {}, {}, {}, {}
    for n in LARGE:
        shp = params[n].shape
        g, d, nm, nv = _adamw(params[n][0], reduced[n], mom[n][0], var[n][0], name=f"adamw_{n}")
        out_g[n], out_d[n], out_m[n], out_v[n] = g.reshape(shp), d.reshape(shp), nm.reshape(shp), nv.reshape(shp)
    small_names = SMALL + ["conv_w"]
    rows_s = -(-sum(int(np.prod(params[n].shape)) for n in small_names) // (8 * 128)) * 8
    _, d, nm, nv = _adamw(_pack([params[n] for n in small_names], rows_s),
                          _pack([reduced[n] for n in small_names], rows_s),
                          _pack([mom[n] for n in small_names], rows_s), _pack([var[n] for n in small_names], rows_s),
                          name="adamw_small")
    shapes = [params[n].shape for n in small_names]
    for n, dd, mm, vv in zip(small_names, _unpack(d, shapes), _unpack(nm, shapes), _unpack(nv, shapes)):
        out_g[n], out_d[n], out_m[n], out_v[n] = reduced[n], dd, mm, vv

    return (loss, grad_x, *[out_g[n] for n in ORDER], *[out_d[n] for n in ORDER],
            *[out_m[n] for n in ORDER], *[out_v[n] for n in ORDER])
```

```python
import functools
import math

import numpy as np
import jax
import jax.numpy as jnp
from jax import lax
from jax.experimental import pallas as pl
from jax.experimental.pallas import tpu as pltpu

F32 = jnp.float32
BF16 = jnp.bfloat16
MESH = pl.DeviceIdType.MESH

D_MODEL = 1024
SEQ = 2048
HEAD_DIM = 64
A_GROUPS = 4
A_WIDTH = 256
B_HEADS = 12
B_WIDTH = 768
CHUNK = 128
DILATED = ((128, 1), (512, 4), (2048, 16))
NUM_BUCKETS = 32
MAX_DISTANCE = 2048
D_FF = 2816
IN_COLS = 2816
NORM_EPS = 1e-6
NEG_INF = -1e30
N_SHARD = 4
SHARD_FF = D_FF // N_SHARD
LANE_BLOCK = 256
VMEM_LIMIT = 56 * 1024 * 1024

ADAM_LR = 0.001
ADAM_B1 = 0.9
ADAM_B2 = 0.999
ADAM_EPS = 1e-08
ADAM_WD = 0.01
ADAM_STEP = 10

GELU_C = math.sqrt(2.0 / math.pi)
GELU_A = 0.044715


def _params(sem=None):
    return pltpu.CompilerParams(dimension_semantics=sem, vmem_limit_bytes=VMEM_LIMIT)


def _dot(a, b, precision=None):
    return jnp.dot(a, b, preferred_element_type=F32, precision=precision)


def _dot_nt(a, b):
    return lax.dot_general(a, b, (((1,), (1,)), ((), ())), preferred_element_type=F32)


def _dot_tn(a, b):
    return lax.dot_general(a, b, (((0,), (0,)), ((), ())), preferred_element_type=F32)


def _gelu(x):
    t = jnp.tanh(GELU_C * (x + GELU_A * (x * x * x)))
    return 0.5 * x * (1.0 + t)


def _gelu_and_grad(x):
    x2 = x * x
    t = jnp.tanh(GELU_C * (x + GELU_A * (x2 * x)))
    g = 0.5 * x * (1.0 + t)
    dg = 0.5 * (1.0 + t) + 0.5 * x * (1.0 - t * t) * (GELU_C * (1.0 + 3.0 * GELU_A * x2))
    return g, dg


def _mm(a, b, *, dims, tm, tn, tk, out_dtype, name, add=None):
    if dims == "nn":
        m, k = a.shape
        n = b.shape[1]
        a_spec = pl.BlockSpec((tm, tk), lambda i, j, kk: (i, kk))
        b_spec = pl.BlockSpec((tk, tn), lambda i, j, kk: (kk, j))
        dot = _dot
    elif dims == "nt":
        m, k = a.shape
        n = b.shape[0]
        a_spec = pl.BlockSpec((tm, tk), lambda i, j, kk: (i, kk))
        b_spec = pl.BlockSpec((tn, tk), lambda i, j, kk: (j, kk))
        dot = _dot_nt
    else:
        k, m = a.shape
        n = b.shape[1]
        a_spec = pl.BlockSpec((tk, tm), lambda i, j, kk: (kk, i))
        b_spec = pl.BlockSpec((tk, tn), lambda i, j, kk: (kk, j))
        dot = _dot_tn
    assert m % tm == 0 and n % tn == 0 and k % tk == 0, (name, m, n, k)
    nk = k // tk
    has_add = add is not None

    def body(*refs):
        if has_add:
            a_ref, b_ref, add_ref, o_ref, acc_ref = refs
        else:
            a_ref, b_ref, o_ref, acc_ref = refs
        kk = pl.program_id(2)

        @pl.when(kk == 0)
        def _():
            if has_add:
                acc_ref[...] = add_ref[...].astype(F32)
            else:
                acc_ref[...] = jnp.zeros_like(acc_ref)

        acc_ref[...] += dot(a_ref[...].astype(BF16), b_ref[...].astype(BF16))

        @pl.when(kk == nk - 1)
        def _():
            o_ref[...] = acc_ref[...].astype(out_dtype)

    in_specs = [a_spec, b_spec]
    args = [a, b]
    if has_add:
        in_specs.append(pl.BlockSpec((tm, tn), lambda i, j, kk: (i, j)))
        args.append(add)
    return pl.pallas_call(
        body,
        grid=(m // tm, n // tn, nk),
        in_specs=in_specs,
        out_specs=pl.BlockSpec((tm, tn), lambda i, j, kk: (i, j)),
        out_shape=jax.ShapeDtypeStruct((m, n), out_dtype),
        scratch_shapes=[pltpu.VMEM((tm, tn), F32)],
        compiler_params=_params(("parallel", "parallel", "arbitrary")),
        name=name,
    )(*args)


ROW_TILE = 512


def _row_spec(width=D_MODEL):
    return pl.BlockSpec((ROW_TILE, width), lambda i: (i, 0))


def _vec_spec(width=D_MODEL):
    return pl.BlockSpec((1, width), lambda i: (0, 0))


def _rstd(v):
    return lax.rsqrt(jnp.mean(v * v, axis=-1, keepdims=True) + NORM_EPS)


def _rms_fwd(x, g):
    m = x.shape[0]

    def body(x_ref, g_ref, h_ref):
        xv = x_ref[...]
        h_ref[...] = (xv * _rstd(xv) * g_ref[...]).astype(BF16)

    return pl.pallas_call(
        body, grid=(m // ROW_TILE,),
        in_specs=[_row_spec(), _vec_spec()],
        out_specs=_row_spec(),
        out_shape=jax.ShapeDtypeStruct((m, D_MODEL), BF16),
        compiler_params=_params(("parallel",)), name="rms_fwd",
    )(x, g)


def _mid_fwd(x0, y1, g2, g3):
    m = x0.shape[0]

    def body(x0_ref, y1_ref, g2_ref, g3_ref, x1_ref, h2_ref):
        y1v = y1_ref[...]
        x1 = x0_ref[...] + y1v * _rstd(y1v) * g2_ref[...]
        x1_ref[...] = x1
        h2_ref[...] = (x1 * _rstd(x1) * g3_ref[...]).astype(BF16)

    return pl.pallas_call(
        body, grid=(m // ROW_TILE,),
        in_specs=[_row_spec(), _row_spec(), _vec_spec(), _vec_spec()],
        out_specs=[_row_spec(), _row_spec()],
        out_shape=[jax.ShapeDtypeStruct((m, D_MODEL), F32), jax.ShapeDtypeStruct((m, D_MODEL), BF16)],
        compiler_params=_params(("parallel",)), name="mid_fwd",
    )(x0, y1, g2, g3)


def _rms_bwd_rows(dout, v, g):
    r = _rstd(v)
    n = v * r
    dn = dout * g
    dv = r * (dn - n * jnp.mean(dn * n, axis=-1, keepdims=True))
    dg = jnp.sum(dout * n, axis=0, keepdims=True)
    return dv, dg


def _loss_head(x1, y2, tgt, g4):
    m = x1.shape[0]

    def body(x1_ref, y2_ref, t_ref, g4_ref, dx2_ref, dy2_ref, dg4_ref, loss_ref):
        i = pl.program_id(0)

        @pl.when(i == 0)
        def _():
            dg4_ref[...] = jnp.zeros_like(dg4_ref)
            loss_ref[...] = jnp.zeros_like(loss_ref)

        y2v = y2_ref[...]
        g4 = g4_ref[...]
        x2 = x1_ref[...] + y2v * _rstd(y2v) * g4
        err = x2 - t_ref[...]
        loss_ref[...] += 0.5 * jnp.sum(jnp.mean(err * err, axis=-1, keepdims=True), axis=0, keepdims=True)
        dx2 = err * (1.0 / D_MODEL)
        dx2_ref[...] = dx2
        dy2, dg4 = _rms_bwd_rows(dx2, y2v, g4)
        dy2_ref[...] = dy2.astype(BF16)
        dg4_ref[...] += dg4

    return pl.pallas_call(
        body, grid=(m // ROW_TILE,),
        in_specs=[_row_spec(), _row_spec(), _row_spec(), _vec_spec()],
        out_specs=[_row_spec(), _row_spec(), _vec_spec(), pl.BlockSpec((1, 1), lambda i: (0, 0))],
        out_shape=[jax.ShapeDtypeStruct((m, D_MODEL), F32), jax.ShapeDtypeStruct((m, D_MODEL), BF16),
                   jax.ShapeDtypeStruct((1, D_MODEL), F32), jax.ShapeDtypeStruct((1, 1), F32)],
        compiler_params=_params(("arbitrary",)), name="loss_head",
    )(x1, y2, tgt, g4)


def _mid_bwd(x1, y1, dh2, dx2, g2, g3):
    m = x1.shape[0]

    def body(x1_ref, y1_ref, dh2_ref, dx2_ref, g2_ref, g3_ref, dx1_ref, dy1_ref, dg2_ref, dg3_ref):
        i = pl.program_id(0)

        @pl.when(i == 0)
        def _():
            dg2_ref[...] = jnp.zeros_like(dg2_ref)
            dg3_ref[...] = jnp.zeros_like(dg3_ref)

        d3, dg3 = _rms_bwd_rows(dh2_ref[...], x1_ref[...], g3_ref[...])
        dx1 = dx2_ref[...] + d3
        dx1_ref[...] = dx1
        dy1, dg2 = _rms_bwd_rows(dx1, y1_ref[...], g2_ref[...])
        dy1_ref[...] = dy1.astype(BF16)
        dg2_ref[...] += dg2
        dg3_ref[...] += dg3

    return pl.pallas_call(
        body, grid=(m // ROW_TILE,),
        in_specs=[_row_spec(), _row_spec(), _row_spec(), _row_spec(), _vec_spec(), _vec_spec()],
        out_specs=[_row_spec(), _row_spec(), _vec_spec(), _vec_spec()],
        out_shape=[jax.ShapeDtypeStruct((m, D_MODEL), F32), jax.ShapeDtypeStruct((m, D_MODEL), BF16),
                   jax.ShapeDtypeStruct((1, D_MODEL), F32), jax.ShapeDtypeStruct((1, D_MODEL), F32)],
        compiler_params=_params(("arbitrary",)), name="mid_bwd",
    )(x1, y1, dh2, dx2, g2, g3)


def _in_bwd(x0, dh1, dx1, g1):
    m = x0.shape[0]

    def body(x0_ref, dh1_ref, dx1_ref, g1_ref, dx0_ref, dg1_ref):
        i = pl.program_id(0)

        @pl.when(i == 0)
        def _():
            dg1_ref[...] = jnp.zeros_like(dg1_ref)

        d1, dg1 = _rms_bwd_rows(dh1_ref[...], x0_ref[...], g1_ref[...])
        dx0_ref[...] = dx1_ref[...] + d1
        dg1_ref[...] += dg1

    return pl.pallas_call(
        body, grid=(m // ROW_TILE,),
        in_specs=[_row_spec(), _row_spec(), _row_spec(), _vec_spec()],
        out_specs=[_row_spec(), _vec_spec()],
        out_shape=[jax.ShapeDtypeStruct((m, D_MODEL), F32), jax.ShapeDtypeStruct((1, D_MODEL), F32)],
        compiler_params=_params(("arbitrary",)), name="in_bwd",
    )(x0, dh1, dx1, g1)


GATE_ROWS = 512


def _group_mean_matrix():
    p = np.zeros((A_WIDTH, A_WIDTH), np.float32)
    for g in range(A_GROUPS):
        p[g * HEAD_DIM:(g + 1) * HEAD_DIM, g * HEAD_DIM:(g + 1) * HEAD_DIM] = 1.0 / HEAD_DIM
    return jnp.asarray(p)


def _group_masks(width=A_WIDTH):
    lane = lax.broadcasted_iota(jnp.int32, (1, width), 1)
    return [(lane >= g * HEAD_DIM) & (lane < (g + 1) * HEAD_DIM) for g in range(width // HEAD_DIM)]


def _layernorm_groups(vg, pavg):
    hi = lax.Precision.HIGHEST
    mu = _dot(vg, pavg, hi)
    xc = vg - mu
    var = _dot(xc * xc, pavg, hi)
    rstd = lax.rsqrt(var + NORM_EPS)
    return xc * rstd, rstd


def _spatial_mix(w_bf, vn_chunk_bf, masks, bz):
    z = bz
    for g in range(A_GROUPS):
        z = z + jnp.where(masks[g], _dot(w_bf[g], vn_chunk_bf), 0.0)
    return z


def _gate_fwd(proj, ln_g, ln_b, w_s, bz):
    m = proj.shape[0]
    pavg = _group_mean_matrix()

    def body(u_ref, v_ref, lg_ref, lb_ref, w_ref, bz_ref, p_ref, a_ref):
        masks = _group_masks()
        row = lax.broadcasted_iota(jnp.int32, (CHUNK, CHUNK), 0)
        col = lax.broadcasted_iota(jnp.int32, (CHUNK, CHUNK), 1)
        w_bf = [jnp.where(row >= col, w_ref[g], 0.0).astype(BF16) for g in range(A_GROUPS)]
        ug = _gelu(u_ref[...])
        vhat, _ = _layernorm_groups(_gelu(v_ref[...]), p_ref[...])
        vn = vhat * lg_ref[...] + lb_ref[...]
        bz = bz_ref[...]
        for c in range(GATE_ROWS // CHUNK):
            sl = slice(c * CHUNK, (c + 1) * CHUNK)
            z = _spatial_mix(w_bf, vn[sl].astype(BF16), masks, bz)
            a_ref[sl, :] = (ug[sl] * z).astype(BF16)

    full = lambda shape: pl.BlockSpec(shape, lambda i: tuple(0 for _ in shape))
    return pl.pallas_call(
        body, grid=(m // GATE_ROWS,),
        in_specs=[pl.BlockSpec((GATE_ROWS, A_WIDTH), lambda i: (i, 0)),
                  pl.BlockSpec((GATE_ROWS, A_WIDTH), lambda i: (i, 1)),
                  full((1, A_WIDTH)), full((1, A_WIDTH)), full((A_GROUPS, CHUNK, CHUNK)),
                  full((CHUNK, A_WIDTH)), full((A_WIDTH, A_WIDTH))],
        out_specs=pl.BlockSpec((GATE_ROWS, A_WIDTH), lambda i: (i, 0)),
        out_shape=jax.ShapeDtypeStruct((m, A_WIDTH), BF16),
        compiler_params=_params(("parallel",)), name="gate_fwd",
    )(proj, proj, ln_g, ln_b, w_s, bz, pavg)


def _gate_bwd(proj, dmix, ln_g, ln_b, w_s, w_st, bz):
    m = proj.shape[0]
    pavg = _group_mean_matrix()
    nsteps = m // GATE_ROWS

    def body(u_ref, v_ref, da_ref, lg_ref, lb_ref, w_ref, wt_ref, bz_ref, p_ref,
             duv_ref, dlg_ref, dlb_ref, dw_ref, dbz_ref):
        i = pl.program_id(0)

        @pl.when(i == 0)
        def _():
            dlg_ref[...] = jnp.zeros_like(dlg_ref)
            dlb_ref[...] = jnp.zeros_like(dlb_ref)
            dw_ref[...] = jnp.zeros_like(dw_ref)
            dbz_ref[...] = jnp.zeros_like(dbz_ref)

        hi = lax.Precision.HIGHEST
        masks = _group_masks()
        row = lax.broadcasted_iota(jnp.int32, (CHUNK, CHUNK), 0)
        col = lax.broadcasted_iota(jnp.int32, (CHUNK, CHUNK), 1)
        tril = row >= col
        w_bf = [jnp.where(tril, w_ref[g], 0.0).astype(BF16) for g in range(A_GROUPS)]
        wt_bf = [jnp.where(col >= row, wt_ref[g], 0.0).astype(BF16) for g in range(A_GROUPS)]
        pavg_v = p_ref[...]
        lg = lg_ref[...]
        ug, dug = _gelu_and_grad(u_ref[...])
        vg, dvg_dx = _gelu_and_grad(v_ref[...])
        vhat, rstd = _layernorm_groups(vg, pavg_v)
        vn = vhat * lg + lb_ref[...]
        da = da_ref[...]
        bz = bz_ref[...]
        for c in range(GATE_ROWS // CHUNK):
            sl = slice(c * CHUNK, (c + 1) * CHUNK)
            vn_bf = vn[sl].astype(BF16)
            z = _spatial_mix(w_bf, vn_bf, masks, bz)
            dz = da[sl] * ug[sl]
            duv_ref[sl, 0:A_WIDTH] = da[sl] * z * dug[sl]
            dbz_ref[...] += dz
            dz_bf = dz.astype(BF16)
            dvn = jnp.zeros((CHUNK, A_WIDTH), F32)
            for g in range(A_GROUPS):
                dz_g = jnp.where(masks[g], dz, 0.0).astype(BF16)
                dw_ref[g] += jnp.where(tril, _dot_nt(dz_g, vn_bf), 0.0)
                dvn = dvn + jnp.where(masks[g], _dot(wt_bf[g], dz_bf), 0.0)
            vh = vhat[sl]
            dlb_ref[...] += jnp.sum(dvn, axis=0, keepdims=True)
            dlg_ref[...] += jnp.sum(dvn * vh, axis=0, keepdims=True)
            dvh = dvn * lg
            m1 = _dot(dvh, pavg_v, hi)
            m2 = _dot(dvh * vh, pavg_v, hi)
            duv_ref[sl, A_WIDTH:2 * A_WIDTH] = rstd[sl] * (dvh - m1 - vh * m2) * dvg_dx[sl]

        @pl.when(i == nsteps - 1)
        def _():
            dbz_ref[...] = _dot(dbz_ref[...], pavg_v * float(HEAD_DIM), hi)

    full = lambda shape: pl.BlockSpec(shape, lambda i: tuple(0 for _ in shape))
    return pl.pallas_call(
        body, grid=(nsteps,),
        in_specs=[pl.BlockSpec((GATE_ROWS, A_WIDTH), lambda i: (i, 0)),
                  pl.BlockSpec((GATE_ROWS, A_WIDTH), lambda i: (i, 1)),
                  pl.BlockSpec((GATE_ROWS, A_WIDTH), lambda i: (i, 0)),
                  full((1, A_WIDTH)), full((1, A_WIDTH)), full((A_GROUPS, CHUNK, CHUNK)),
                  full((A_GROUPS, CHUNK, CHUNK)), full((CHUNK, A_WIDTH)), full((A_WIDTH, A_WIDTH))],
        out_specs=[pl.BlockSpec((GATE_ROWS, 2 * A_WIDTH), lambda i: (i, 0)),
                   full((1, A_WIDTH)), full((1, A_WIDTH)), full((A_GROUPS, CHUNK, CHUNK)),
                   full((CHUNK, A_WIDTH))],
        out_shape=[jax.ShapeDtypeStruct((m, 2 * A_WIDTH), F32),
                   jax.ShapeDtypeStruct((1, A_WIDTH), F32), jax.ShapeDtypeStruct((1, A_WIDTH), F32),
                   jax.ShapeDtypeStruct((A_GROUPS, CHUNK, CHUNK), F32),
                   jax.ShapeDtypeStruct((CHUNK, A_WIDTH), F32)],
        compiler_params=_params(("arbitrary",)), name="gate_bwd",
    )(proj, proj, dmix, ln_g, ln_b, w_s, w_st, bz, pavg)


Q_BLOCK = 128
Q_COL, K_COL, V_COL = 2, 5, 8
N_COLBLK = IN_COLS // LANE_BLOCK
HEAD_BLOCKS = B_WIDTH // LANE_BLOCK
HEADS_PER_BLOCK = LANE_BLOCK // HEAD_DIM


def _t5_bucket_np(dist, dtype):
    max_exact = NUM_BUCKETS // 2
    d = np.maximum(dist, 1).astype(dtype)
    large = max_exact + (np.log(d / dtype(max_exact)) / dtype(math.log(MAX_DISTANCE / max_exact))
                         * dtype(NUM_BUCKETS - max_exact))
    large = np.minimum(large.astype(np.int32), NUM_BUCKETS - 1)
    return np.where(dist < max_exact, dist, large)


def _bucket_tables():
    i = np.arange(Q_BLOCK)[:, None]
    j = np.arange(Q_BLOCK)[None, :]
    tables = []
    for _, dil in DILATED:
        rel_prev = Q_BLOCK + i - j
        rel_cur = i - j
        rel = np.concatenate([rel_prev, rel_cur], axis=1)
        valid = np.concatenate([rel_prev <= Q_BLOCK, rel_cur >= 0], axis=1)
        dist = np.maximum(rel, 0) * dil
        b32 = _t5_bucket_np(dist, np.float32)
        b64 = _t5_bucket_np(dist, np.float64)
        assert np.array_equal(b32, b64)
        tables.append(np.where(valid, b32, -1).astype(np.int32))
    return np.stack(tables)


def _bias_tables(rel_bias, buckets_np):
    present = [sorted(set(int(v) for v in np.unique(buckets_np[c]) if v >= 0)) for c in range(len(DILATED))]

    def body(rb_ref, bk_ref, o_ref):
        for c in range(len(DILATED)):
            bk = bk_ref[c]
            for h in range(B_HEADS):
                acc = jnp.full((Q_BLOCK, 2 * Q_BLOCK), NEG_INF, F32)
                for b in present[c]:
                    acc = jnp.where(bk == b, rb_ref[b, h], acc)
                o_ref[c, h] = acc

    return pl.pallas_call(
        body,
        in_specs=[pl.BlockSpec(memory_space=pltpu.SMEM), pl.BlockSpec(memory_space=pltpu.VMEM)],
        out_specs=pl.BlockSpec(memory_space=pltpu.VMEM),
        out_shape=jax.ShapeDtypeStruct((len(DILATED), B_HEADS, Q_BLOCK, 2 * Q_BLOCK), F32),
        compiler_params=_params(), name="bias_tables",
    )(rel_bias, jnp.asarray(buckets_np))


def _head_masks():
    lane = lax.broadcasted_iota(jnp.int32, (1, LANE_BLOCK), 1)
    return [(lane >= h * HEAD_DIM) & (lane < (h + 1) * HEAD_DIM) for h in range(HEADS_PER_BLOCK)]


def _attn_fwd(proj, bias, dil, batch):
    m = proj.shape[0]
    tr = SEQ // dil
    nb = tr // Q_BLOCK
    proj3 = proj.reshape(batch, tr, dil * IN_COLS)

    def body(q_ref, k_ref, v_ref, b_ref, o_ref, l_ref):
        masks = _head_masks()

        def block(n, carry):
            r0 = pl.multiple_of(n * Q_BLOCK, Q_BLOCK)
            rows = pl.ds(r0, Q_BLOCK)
            q = q_ref[rows, :] * 0.125
            kc = k_ref[rows, :].astype(BF16)
            vc = v_ref[rows, :].astype(BF16)
            if nb > 1:
                p0 = pl.multiple_of(jnp.maximum(n - 1, 0) * Q_BLOCK, Q_BLOCK)
                kp = k_ref[pl.ds(p0, Q_BLOCK), :].astype(BF16)
                vp = v_ref[pl.ds(p0, Q_BLOCK), :].astype(BF16)
            o_acc = jnp.zeros((Q_BLOCK, LANE_BLOCK), F32)
            l_acc = jnp.zeros((Q_BLOCK, LANE_BLOCK), F32)
            for h in range(HEADS_PER_BLOCK):
                qh = jnp.where(masks[h], q, 0.0).astype(BF16)
                sc = _dot_nt(qh, kc) + b_ref[h, :, Q_BLOCK:]
                mx = jnp.max(sc, axis=1, keepdims=True)
                if nb > 1:
                    sp = _dot_nt(qh, kp) + jnp.where(n == 0, NEG_INF, b_ref[h, :, :Q_BLOCK])
                    mx = jnp.maximum(mx, jnp.max(sp, axis=1, keepdims=True))
                pc = jnp.exp(sc - mx)
                den = jnp.sum(pc, axis=1, keepdims=True)
                oh = _dot(pc.astype(BF16), vc)
                if nb > 1:
                    pp = jnp.exp(sp - mx)
                    den = den + jnp.sum(pp, axis=1, keepdims=True)
                    oh = oh + _dot(pp.astype(BF16), vp)
                o_acc = jnp.where(masks[h], oh / den, o_acc)
                l_acc = jnp.where(masks[h], mx + jnp.log(den), l_acc)
            o_ref[rows, :] = o_acc
            l_ref[rows, :] = l_acc
            return carry

        if nb > 1:
            lax.fori_loop(0, nb, block, 0)
        else:
            block(0, 0)

    def in_spec(col0):
        return pl.BlockSpec((None, tr, LANE_BLOCK), lambda b, r, g: (b, 0, r * N_COLBLK + col0 + g))

    out_spec = pl.BlockSpec((None, tr, LANE_BLOCK), lambda b, r, g: (b, 0, r * HEAD_BLOCKS + g))
    out_sds = jax.ShapeDtypeStruct((batch, tr, dil * B_WIDTH), F32)
    o, lse = pl.pallas_call(
        body, grid=(batch, dil, HEAD_BLOCKS),
        in_specs=[in_spec(Q_COL), in_spec(K_COL), in_spec(V_COL),
                  pl.BlockSpec((HEADS_PER_BLOCK, Q_BLOCK, 2 * Q_BLOCK), lambda b, r, g: (g, 0, 0))],
        out_specs=[out_spec, out_spec],
        out_shape=[out_sds, out_sds],
        compiler_params=_params(("parallel", "parallel", "parallel")), name=f"attn_fwd_d{dil}",
    )(proj3, proj3, proj3, bias)
    return o.reshape(m, B_WIDTH), lse.reshape(m, B_WIDTH)


def _attn_combine(outs, lses):
    m = outs[0].shape[0]
    nc = len(outs)

    def body(*refs):
        o_refs, l_refs = refs[:nc], refs[nc:2 * nc]
        of_ref, ob_ref, lt_ref = refs[2 * nc:]
        ls = [r[...] for r in l_refs]
        mx = functools.reduce(jnp.maximum, ls)
        ws = [jnp.exp(l - mx) for l in ls]
        tot = functools.reduce(lambda a, b: a + b, ws)
        inv = 1.0 / tot
        o = functools.reduce(lambda a, b: a + b, [w * inv * r[...] for w, r in zip(ws, o_refs)])
        of_ref[...] = o
        ob_ref[...] = o.astype(BF16)
        lt_ref[...] = mx + jnp.log(tot)

    spec = pl.BlockSpec((ROW_TILE, B_WIDTH), lambda i: (i, 0))
    return pl.pallas_call(
        body, grid=(m // ROW_TILE,),
        in_specs=[spec] * (2 * nc), out_specs=[spec, spec, spec],
        out_shape=[jax.ShapeDtypeStruct((m, B_WIDTH), F32), jax.ShapeDtypeStruct((m, B_WIDTH), BF16),
                   jax.ShapeDtypeStruct((m, B_WIDTH), F32)],
        compiler_params=_params(("parallel",)), name="attn_combine",
    )(*outs, *lses)


def _attn_bwd(proj, dmix, o, lse, bias, dil, batch):
    m = proj.shape[0]
    tr = SEQ // dil
    nb = tr // Q_BLOCK
    proj3 = proj.reshape(batch, tr, dil * IN_COLS)
    dmix3 = dmix.reshape(batch, tr, dil * D_MODEL)
    o3 = o.reshape(batch, tr, dil * B_WIDTH)
    l3 = lse.reshape(batch, tr, dil * B_WIDTH)

    def body(q_ref, k_ref, v_ref, do_ref, o_ref, l_ref, b_ref, dq_ref, dk_ref, dv_ref, ds_ref):
        first = (pl.program_id(1) == 0) & (pl.program_id(2) == 0)

        @pl.when(first)
        def _():
            ds_ref[...] = jnp.zeros_like(ds_ref)

        dk_ref[...] = jnp.zeros_like(dk_ref)
        dv_ref[...] = jnp.zeros_like(dv_ref)
        masks = _head_masks()

        def block(n, carry):
            r0 = pl.multiple_of(n * Q_BLOCK, Q_BLOCK)
            rows = pl.ds(r0, Q_BLOCK)
            q = q_ref[rows, :] * 0.125
            kc = k_ref[rows, :].astype(BF16)
            vc = v_ref[rows, :].astype(BF16)
            do = do_ref[rows, :]
            ov = o_ref[rows, :]
            lv = l_ref[rows, :]
            if nb > 1:
                p0 = pl.multiple_of(jnp.maximum(n - 1, 0) * Q_BLOCK, Q_BLOCK)
                prow = pl.ds(p0, Q_BLOCK)
                kp = k_ref[prow, :].astype(BF16)
                vp = v_ref[prow, :].astype(BF16)
                dkp = jnp.zeros((Q_BLOCK, LANE_BLOCK), F32)
                dvp = jnp.zeros((Q_BLOCK, LANE_BLOCK), F32)
            dq = jnp.zeros((Q_BLOCK, LANE_BLOCK), F32)
            dkc = jnp.zeros((Q_BLOCK, LANE_BLOCK), F32)
            dvc = jnp.zeros((Q_BLOCK, LANE_BLOCK), F32)
            for h in range(HEADS_PER_BLOCK):
                qh = jnp.where(masks[h], q, 0.0).astype(BF16)
                doh = jnp.where(masks[h], do, 0.0)
                doh_bf = doh.astype(BF16)
                lrow = jnp.max(jnp.where(masks[h], lv, -3e38), axis=1, keepdims=True)
                drow = jnp.sum(doh * ov, axis=1, keepdims=True)
                pc = jnp.exp(_dot_nt(qh, kc) + b_ref[h, :, Q_BLOCK:] - lrow)
                dsc = pc * (_dot_nt(doh_bf, vc) - drow)
                ds_ref[h, :, Q_BLOCK:] += dsc
                dsc_bf = dsc.astype(BF16)
                dqh = _dot(dsc_bf, kc)
                dkc = dkc + _dot_tn(dsc_bf, qh)
                dvc = dvc + _dot_tn(pc.astype(BF16), doh_bf)
                if nb > 1:
                    bp = jnp.where(n == 0, NEG_INF, b_ref[h, :, :Q_BLOCK])
                    pp = jnp.exp(_dot_nt(qh, kp) + bp - lrow)
                    dsp = pp * (_dot_nt(doh_bf, vp) - drow)
                    ds_ref[h, :, :Q_BLOCK] += dsp
                    dsp_bf = dsp.astype(BF16)
                    dqh = dqh + _dot(dsp_bf, kp)
                    dkp = dkp + _dot_tn(dsp_bf, qh)
                    dvp = dvp + _dot_tn(pp.astype(BF16), doh_bf)
                dq = jnp.where(masks[h], dqh, dq)
            dq_ref[rows, :] = dq * 0.125
            dk_ref[rows, :] += dkc
            dv_ref[rows, :] += dvc
            if nb > 1:
                dk_ref[prow, :] += dkp
                dv_ref[prow, :] += dvp
            return carry

        if nb > 1:
            lax.fori_loop(0, nb, block, 0)
        else:
            block(0, 0)

    def in_spec(col0):
        return pl.BlockSpec((None, tr, LANE_BLOCK), lambda g, b, r: (b, 0, r * N_COLBLK + col0 + g))

    do_spec = pl.BlockSpec((None, tr, LANE_BLOCK), lambda g, b, r: (b, 0, r * (D_MODEL // LANE_BLOCK) + 1 + g))
    hd_spec = pl.BlockSpec((None, tr, LANE_BLOCK), lambda g, b, r: (b, 0, r * HEAD_BLOCKS + g))
    tbl_spec = pl.BlockSpec((HEADS_PER_BLOCK, Q_BLOCK, 2 * Q_BLOCK), lambda g, b, r: (g, 0, 0))
    out_sds = jax.ShapeDtypeStruct((batch, tr, dil * B_WIDTH), F32)
    dq, dk, dv, ds = pl.pallas_call(
        body, grid=(HEAD_BLOCKS, batch, dil),
        in_specs=[in_spec(Q_COL), in_spec(K_COL), in_spec(V_COL), do_spec, hd_spec, hd_spec, tbl_spec],
        out_specs=[hd_spec, hd_spec, hd_spec, tbl_spec],
        out_shape=[out_sds, out_sds, out_sds, jax.ShapeDtypeStruct((B_HEADS, Q_BLOCK, 2 * Q_BLOCK), F32)],
        compiler_params=_params(("parallel", "arbitrary", "arbitrary")), name=f"attn_bwd_d{dil}",
    )(proj3, proj3, proj3, dmix3, o3, l3, bias)
    return dq.reshape(m, B_WIDTH), dk.reshape(m, B_WIDTH), dv.reshape(m, B_WIDTH), ds


PAIR = 2 * HEAD_DIM
N_PAIR = B_HEADS // 2
N_CFG = len(DILATED)
BLOCKS_PER_CFG = SEQ // Q_BLOCK
QKV_SLABS = 3 * N_PAIR
FWD_BLOCKS_PER_TRIP = 2
BWD_BLOCKS_PER_TRIP = 2


def _proj_fwd(x, g1, w_in):
    m = x.shape[0]
    tm = ROW_TILE

    def body(x_ref, g_ref, w_ref, h_ref, uv_ref, qkv_ref):
        xv = x_ref[...]
        h = (xv * _rstd(xv) * g_ref[...]).astype(BF16)
        h_ref[...] = h
        acc = _dot(h, w_ref[...])
        uv_ref[...] = acc[:, :2 * A_WIDTH]
        for s in range(QKV_SLABS):
            qkv_ref[s] = acc[:, 2 * A_WIDTH + s * PAIR:2 * A_WIDTH + (s + 1) * PAIR]

    return pl.pallas_call(
        body, grid=(m // tm,),
        in_specs=[pl.BlockSpec((tm, D_MODEL), lambda i: (i, 0)), _vec_spec(),
                  pl.BlockSpec((D_MODEL, IN_COLS), lambda i: (0, 0))],
        out_specs=[pl.BlockSpec((tm, D_MODEL), lambda i: (i, 0)),
                   pl.BlockSpec((tm, 2 * A_WIDTH), lambda i: (i, 0)),
                   pl.BlockSpec((QKV_SLABS, tm, PAIR), lambda i: (0, i, 0))],
        out_shape=[jax.ShapeDtypeStruct((m, D_MODEL), BF16), jax.ShapeDtypeStruct((m, 2 * A_WIDTH), F32),
                   jax.ShapeDtypeStruct((QKV_SLABS, m, PAIR), F32)],
        compiler_params=_params(("parallel",)), name="proj_fwd",
    )(x, g1, w_in)


def _pair_masks():
    lane = lax.broadcasted_iota(jnp.int32, (1, PAIR), 1)
    return [lane < HEAD_DIM, lane >= HEAD_DIM]


def _block_rows(idx, dil):
    if dil == 1:
        n = idx
        cur = pl.ds(pl.multiple_of(n * Q_BLOCK, Q_BLOCK), Q_BLOCK)
        prev = pl.ds(pl.multiple_of(jnp.maximum(n - 1, 0) * Q_BLOCK, Q_BLOCK), Q_BLOCK)
        return n, cur, prev
    r = idx % dil
    n = idx // dil
    cur = pl.ds(r + (dil * Q_BLOCK) * n, Q_BLOCK, stride=dil)
    prev = pl.ds(r + (dil * Q_BLOCK) * jnp.maximum(n - 1, 0), Q_BLOCK, stride=dil)
    return n, cur, prev


def _attn_fwd(qkv, bias, batch):
    m = qkv.shape[1]
    comb_rows = 256

    def body(q_ref, k_ref, v_ref, b_ref, o_ref, l_ref, *scratch):
        oc_refs, lc_refs = scratch[:N_CFG], scratch[N_CFG:]
        masks = _pair_masks()
        for ci, (_, dil) in enumerate(DILATED):
            nb = SEQ // dil // Q_BLOCK

            def block(trip, carry, ci=ci, dil=dil, nb=nb):
                work = []
                for u in range(FWD_BLOCKS_PER_TRIP):
                    n, rows, prow = _block_rows(trip * FWD_BLOCKS_PER_TRIP + u, dil)
                    q = q_ref[rows, :] * 0.125
                    kc = k_ref[rows, :].astype(BF16)
                    vc = v_ref[rows, :]
                    kp = k_ref[prow, :].astype(BF16) if nb > 1 else None
                    vp = v_ref[prow, :] if nb > 1 else None
                    tiles = []
                    for h in range(2):
                        qh = jnp.where(masks[h], q, 0.0).astype(BF16)
                        sc = _dot_nt(qh, kc) + b_ref[ci, h, :, Q_BLOCK:]
                        sp = None
                        if nb > 1:
                            sp = _dot_nt(qh, kp) + jnp.where(n == 0, NEG_INF, b_ref[ci, h, :, :Q_BLOCK])
                        tiles.append((sc, sp))
                    work.append((rows, vc, vp, tiles))
                probs = []
                for _, _, _, tiles in work:
                    ps = []
                    for sc, sp in tiles:
                        mx = jnp.max(sc if sp is None else jnp.maximum(sc, sp), axis=1, keepdims=True)
                        pc = jnp.exp(sc - mx).astype(BF16)
                        pp = None if sp is None else jnp.exp(sp - mx).astype(BF16)
                        ps.append((mx, pc, pp))
                    probs.append(ps)
                for (rows, vc, vp, _), ps in zip(work, probs):
                    res = []
                    for h, (_, pc, pp) in enumerate(ps):
                        r = _dot(pc, jnp.where(masks[h], vc, 1.0).astype(BF16))
                        if pp is not None:
                            r = r + _dot(pp, jnp.where(masks[h], vp, 1.0).astype(BF16))
                        res.append(r)
                    num = jnp.where(masks[0], res[0], res[1])
                    den = pltpu.roll(jnp.where(masks[0], res[1], res[0]), HEAD_DIM, 1)
                    oc_refs[ci][rows, :] = num / den
                    lc_refs[ci][rows, :] = jnp.where(masks[0], ps[0][0], ps[1][0]) + jnp.log(den)
                return carry

            lax.fori_loop(0, BLOCKS_PER_CFG // FWD_BLOCKS_PER_TRIP, block, 0)

        def combine(i, carry):
            rr = pl.ds(pl.multiple_of(i * comb_rows, comb_rows), comb_rows)
            ls = [lc_refs[c][rr, :] for c in range(N_CFG)]
            mx = functools.reduce(jnp.maximum, ls)
            ws = [jnp.exp(l - mx) for l in ls]
            tot = functools.reduce(lambda a, b: a + b, ws)
            o = functools.reduce(lambda a, b: a + b, [ws[c] * oc_refs[c][rr, :] for c in range(N_CFG)]) / tot
            o_ref[rr, :] = o.astype(BF16)
            l_ref[rr, :] = mx + jnp.log(tot)
            return carry

        lax.fori_loop(0, SEQ // comb_rows, combine, 0)

    def slab(first):
        return pl.BlockSpec((None, SEQ, PAIR), lambda b, p: (first + p, b, 0))

    nat = pl.BlockSpec((SEQ, PAIR), lambda b, p: (b, p))
    return pl.pallas_call(
        body, grid=(batch, N_PAIR),
        in_specs=[slab(0), slab(N_PAIR), slab(2 * N_PAIR),
                  pl.BlockSpec((N_CFG, 2, Q_BLOCK, 2 * Q_BLOCK), lambda b, p: (0, p, 0, 0))],
        out_specs=[nat, nat],
        out_shape=[jax.ShapeDtypeStruct((m, B_WIDTH), BF16), jax.ShapeDtypeStruct((m, B_WIDTH), F32)],
        scratch_shapes=[pltpu.VMEM((SEQ, PAIR), F32)] * (2 * N_CFG),
        compiler_params=_params(("parallel", "parallel")), name="attn_fwd",
    )(qkv, qkv, qkv, bias)


def _attn_bwd(qkv, dmix, o, lse, bias, batch):
    m = qkv.shape[1]

    def body(q_ref, k_ref, v_ref, do_ref, o_ref, l_ref, b_ref, dqkv_ref, ds_ref, dq_acc, dk_acc, dv_acc, d_scr):
        @pl.when(pl.program_id(1) == 0)
        def _():
            ds_ref[...] = jnp.zeros_like(ds_ref)

        dq_acc[...] = jnp.zeros_like(dq_acc)
        dk_acc[...] = jnp.zeros_like(dk_acc)
        dv_acc[...] = jnp.zeros_like(dv_acc)
        masks = _pair_masks()
        ri = lax.broadcasted_iota(jnp.int32, (PAIR, PAIR), 0)
        cj = lax.broadcasted_iota(jnp.int32, (PAIR, PAIR), 1)
        same_head = ((ri < HEAD_DIM) == (cj < HEAD_DIM)).astype(F32)
        d_scr[...] = _dot(do_ref[...] * o_ref[...].astype(F32), same_head, lax.Precision.HIGHEST)

        for ci, (_, dil) in enumerate(DILATED):
            nb = SEQ // dil // Q_BLOCK

            def block(trip, carry, ci=ci, dil=dil, nb=nb):
                first = []
                for u in range(BWD_BLOCKS_PER_TRIP):
                    n, rows, prow = _block_rows(trip * BWD_BLOCKS_PER_TRIP + u, dil)
                    q = q_ref[rows, :] * 0.125
                    kc = k_ref[rows, :].astype(BF16)
                    vc = v_ref[rows, :].astype(BF16)
                    do = do_ref[rows, :]
                    lv = l_ref[rows, :]
                    dv_ = d_scr[rows, :]
                    kp = k_ref[prow, :].astype(BF16) if nb > 1 else None
                    vp = v_ref[prow, :].astype(BF16) if nb > 1 else None
                    heads = []
                    for h in range(2):
                        c0 = h * HEAD_DIM
                        qh = jnp.where(masks[h], q, 0.0).astype(BF16)
                        doh = jnp.where(masks[h], do, 0.0).astype(BF16)
                        sc = _dot_nt(qh, kc)
                        dpc = _dot_nt(doh, vc)
                        sp = _dot_nt(qh, kp) if nb > 1 else None
                        dpp = _dot_nt(doh, vp) if nb > 1 else None
                        heads.append((qh, doh, lv[:, c0:c0 + 1], dv_[:, c0:c0 + 1], sc, dpc, sp, dpp))
                    first.append((n, rows, prow, kc, kp, heads))
                second = []
                for n, rows, prow, kc, kp, heads in first:
                    out = []
                    for h, (qh, doh, lrow, drow, sc, dpc, sp, dpp) in enumerate(heads):
                        pc = jnp.exp(sc + b_ref[ci, h, :, Q_BLOCK:] - lrow)
                        dsc = pc * (dpc - drow)
                        ds_ref[ci, h, :, Q_BLOCK:] += dsc
                        pp_bf = dsp_bf = None
                        if nb > 1:
                            pp = jnp.exp(sp + jnp.where(n == 0, NEG_INF, b_ref[ci, h, :, :Q_BLOCK]) - lrow)
                            dsp = pp * (dpp - drow)
                            ds_ref[ci, h, :, :Q_BLOCK] += dsp
                            pp_bf, dsp_bf = pp.astype(BF16), dsp.astype(BF16)
                        out.append((qh, doh, pc.astype(BF16), dsc.astype(BF16), pp_bf, dsp_bf))
                    second.append((rows, prow, kc, kp, out))
                for rows, prow, kc, kp, out in second:
                    dq = jnp.zeros((Q_BLOCK, PAIR), F32)
                    dkc = jnp.zeros((Q_BLOCK, PAIR), F32)
                    dvc = jnp.zeros((Q_BLOCK, PAIR), F32)
                    dkp = jnp.zeros((Q_BLOCK, PAIR), F32)
                    dvp = jnp.zeros((Q_BLOCK, PAIR), F32)
                    for h, (qh, doh, pc_bf, dsc_bf, pp_bf, dsp_bf) in enumerate(out):
                        dqh = _dot(dsc_bf, kc)
                        dkc = dkc + _dot_tn(dsc_bf, qh)
                        dvc = dvc + _dot_tn(pc_bf, doh)
                        if nb > 1:
                            dqh = dqh + _dot(dsp_bf, kp)
                            dkp = dkp + _dot_tn(dsp_bf, qh)
                            dvp = dvp + _dot_tn(pp_bf, doh)
                        dq = jnp.where(masks[h], dqh, dq)
                    dq_acc[rows, :] += dq * 0.125
                    dk_acc[rows, :] += dkc
                    dv_acc[rows, :] += dvc
                    if nb > 1:
                        dk_acc[prow, :] += dkp
                        dv_acc[prow, :] += dvp
                return carry

            lax.fori_loop(0, BLOCKS_PER_CFG // BWD_BLOCKS_PER_TRIP, block, 0)

        dqkv_ref[0] = dq_acc[...].astype(BF16)
        dqkv_ref[1] = dk_acc[...].astype(BF16)
        dqkv_ref[2] = dv_acc[...].astype(BF16)

    def slab(first):
        return pl.BlockSpec((None, SEQ, PAIR), lambda p, b: (first + p, b, 0))

    nat = pl.BlockSpec((SEQ, PAIR), lambda p, b: (b, p))
    tbl = pl.BlockSpec((N_CFG, 2, Q_BLOCK, 2 * Q_BLOCK), lambda p, b: (0, p, 0, 0))
    acc = pltpu.VMEM((SEQ, PAIR), F32)
    return pl.pallas_call(
        body, grid=(N_PAIR, batch),
        in_specs=[slab(0), slab(N_PAIR), slab(2 * N_PAIR),
                  pl.BlockSpec((SEQ, PAIR), lambda p, b: (b, A_WIDTH // PAIR + p)), nat, nat, tbl],
        out_specs=[pl.BlockSpec((3, SEQ, PAIR), lambda p, b: (0, b, p)), tbl],
        out_shape=[jax.ShapeDtypeStruct((3, m, B_WIDTH), BF16),
                   jax.ShapeDtypeStruct((N_CFG, B_HEADS, Q_BLOCK, 2 * Q_BLOCK), F32)],
        scratch_shapes=[acc, acc, acc, acc],
        compiler_params=_params(("parallel", "arbitrary")), name="attn_bwd",
    )(qkv, qkv, qkv, dmix, o, lse, bias)


def _rel_bias_grad(ds, buckets_np):
    present = [sorted(set(int(v) for v in np.unique(buckets_np[c]) if v >= 0)) for c in range(N_CFG)]

    def body(bk_ref, ds_ref, o_ref, acc_ref):
        acc_ref[...] = jnp.zeros_like(acc_ref)
        for c in range(N_CFG):
            bk = bk_ref[c]
            for h in range(B_HEADS):
                dsv = ds_ref[c, h]
                for b in present[c]:
                    part = jnp.sum(jnp.where(bk == b, dsv, 0.0), axis=0, keepdims=True)
                    acc_ref[pl.ds(h * NUM_BUCKETS + b, 1), :] += part
        o_ref[...] = jnp.sum(acc_ref[...], axis=1, keepdims=True)

    vm = pl.BlockSpec(memory_space=pltpu.VMEM)
    return pl.pallas_call(
        body, in_specs=[vm, vm], out_specs=vm,
        out_shape=jax.ShapeDtypeStruct((B_HEADS * NUM_BUCKETS, 1), F32),
        scratch_shapes=[pltpu.VMEM((B_HEADS * NUM_BUCKETS, 2 * Q_BLOCK), F32)],
        compiler_params=_params(), name="rel_bias_grad",
    )(jnp.asarray(buckets_np), ds)


def _assemble_dproj(duv, dqkv):
    m = duv.shape[0]

    def body(duv_ref, dqkv_ref, o_ref):
        j = pl.program_id(1)

        @pl.when(j < 2)
        def _():
            o_ref[...] = duv_ref[...].astype(BF16)

        @pl.when(j >= 2)
        def _():
            o_ref[...] = dqkv_ref[...]

    return pl.pallas_call(
        body, grid=(m // ROW_TILE, N_COLBLK),
        in_specs=[pl.BlockSpec((ROW_TILE, LANE_BLOCK), lambda i, j: (i, jnp.minimum(j, 1))),
                  pl.BlockSpec((None, ROW_TILE, LANE_BLOCK),
                               lambda i, j: (jnp.maximum(j - 2, 0) // HEAD_BLOCKS, i, jnp.maximum(j - 2, 0) % HEAD_BLOCKS))],
        out_specs=pl.BlockSpec((ROW_TILE, LANE_BLOCK), lambda i, j: (i, j)),
        out_shape=jax.ShapeDtypeStruct((m, IN_COLS), BF16),
        compiler_params=_params(("parallel", "arbitrary")), name="assemble_dproj",
    )(duv, dqkv)


def _shift_down(x, k):
    row = lax.broadcasted_iota(jnp.int32, x.shape, 0)
    return jnp.where(row >= k, pltpu.roll(x, k, 0), 0.0)


def _shift_up(x, k):
    n = x.shape[0]
    row = lax.broadcasted_iota(jnp.int32, x.shape, 0)
    return jnp.where(row < n - k, pltpu.roll(x, n - k, 0), 0.0)


def _convgate_fwd(gate, up, conv_w, conv_b, batch):
    m = gate.shape[0]

    def body(g_ref, u_ref, w_ref, b_ref, a_ref):
        g = g_ref[...]
        w = w_ref[...]
        c = b_ref[...] + w[0:1] * _shift_down(g, 2) + w[1:2] * _shift_down(g, 1) + w[2:3] * g
        a_ref[...] = (_gelu(c) * u_ref[...]).astype(BF16)

    blk = pl.BlockSpec((SEQ, LANE_BLOCK), lambda b, j: (b, j))
    return pl.pallas_call(
        body, grid=(batch, D_FF // LANE_BLOCK),
        in_specs=[blk, blk, pl.BlockSpec((3, LANE_BLOCK), lambda b, j: (0, j)),
                  pl.BlockSpec((1, LANE_BLOCK), lambda b, j: (0, j))],
        out_specs=blk,
        out_shape=jax.ShapeDtypeStruct((m, D_FF), BF16),
        compiler_params=_params(("parallel", "parallel")), name="convgate_fwd",
    )(gate, up, conv_w, conv_b)


def _convgate_bwd(gate, up, dact, conv_w, conv_b, batch):
    m = gate.shape[0]

    def body(g_ref, u_ref, da_ref, w_ref, b_ref, dg_ref, du_ref, dw_ref, db_ref):
        @pl.when(pl.program_id(1) == 0)
        def _():
            dw_ref[...] = jnp.zeros_like(dw_ref)
            db_ref[...] = jnp.zeros_like(db_ref)

        g = g_ref[...]
        w = w_ref[...]
        g1 = _shift_down(g, 1)
        g2 = _shift_down(g, 2)
        c = b_ref[...] + w[0:1] * g2 + w[1:2] * g1 + w[2:3] * g
        gg, dgg = _gelu_and_grad(c)
        da = da_ref[...]
        du_ref[...] = (da * gg).astype(BF16)
        dc = da * u_ref[...] * dgg
        db_ref[...] += jnp.sum(dc, axis=0, keepdims=True)
        dw_ref[0:1, :] += jnp.sum(dc * g2, axis=0, keepdims=True)
        dw_ref[1:2, :] += jnp.sum(dc * g1, axis=0, keepdims=True)
        dw_ref[2:3, :] += jnp.sum(dc * g, axis=0, keepdims=True)
        dg_ref[...] = (w[2:3] * dc + w[1:2] * _shift_up(dc, 1) + w[0:1] * _shift_up(dc, 2)).astype(BF16)

    blk = pl.BlockSpec((SEQ, LANE_BLOCK), lambda j, b: (b, j))
    wspec = pl.BlockSpec((3, LANE_BLOCK), lambda j, b: (0, j))
    bspec = pl.BlockSpec((1, LANE_BLOCK), lambda j, b: (0, j))
    return pl.pallas_call(
        body, grid=(D_FF // LANE_BLOCK, batch),
        in_specs=[blk, blk, blk, wspec, bspec],
        out_specs=[blk, blk, wspec, bspec],
        out_shape=[jax.ShapeDtypeStruct((m, D_FF), BF16), jax.ShapeDtypeStruct((m, D_FF), BF16),
                   jax.ShapeDtypeStruct((3, D_FF), F32), jax.ShapeDtypeStruct((1, D_FF), F32)],
        compiler_params=_params(("parallel", "arbitrary")), name="convgate_bwd",
    )(gate, up, dact, conv_w, conv_b)


def _local_step(x, tgt, g1, g2, g3, g4, w_in, ln_g, ln_b, w_s, b_s, rel_bias, w_out, w_gate, w_up,
                conv_w, conv_b, w_down, batch):
    big = dict(tm=1024, out_dtype=F32)
    buckets = _bucket_tables()
    bias = _bias_tables(rel_bias, buckets)
    bz = jnp.repeat(b_s.T, HEAD_DIM, axis=1)
    w_st = jnp.swapaxes(w_s, 1, 2)

    h1, uv, qkv = _proj_fwd(x, g1, w_in)
    a = _gate_fwd(uv, ln_g, ln_b, w_s, bz)
    o_bf, lse = _attn_fwd(qkv, bias, batch)
    y1 = _mm(a, w_out[:A_WIDTH], dims="nn", tn=1024, tk=A_WIDTH, name="mm_out_a", **big)
    y1 = _mm(o_bf, w_out[A_WIDTH:], dims="nn", tn=1024, tk=B_WIDTH, name="mm_out_b", add=y1, **big)
    x1, h2 = _mid_fwd(x, y1, g2, g3)
    gate = _mm(h2, w_gate, dims="nn", tn=1408, tk=1024, name="mm_gate", **big)
    up = _mm(h2, w_up, dims="nn", tn=1408, tk=1024, name="mm_up", **big)
    act = _convgate_fwd(gate, up, conv_w, conv_b, batch)
    y2 = _mm(act, w_down, dims="nn", tn=1024, tk=1408, name="mm_down", **big)
    dx2, dy2, dg4, loss = _loss_head(x1, y2, tgt, g4)

    dact = _mm(dy2, w_down, dims="nt", tn=1408, tk=1024, name="mm_dact", **big)
    dw_down = _mm(act, dy2, dims="tn", tm=1408, tn=1024, tk=1024, out_dtype=F32, name="mm_dw_down")
    dgate, dup, dconv_w, dconv_b = _convgate_bwd(gate, up, dact, conv_w, conv_b, batch)
    dh2 = _mm(dgate, w_gate, dims="nt", tn=1024, tk=1408, name="mm_dh2_g", **big)
    dh2 = _mm(dup, w_up, dims="nt", tn=1024, tk=1408, name="mm_dh2_u", add=dh2, **big)
    dw_gate = _mm(h2, dgate, dims="tn", tm=1024, tn=1408, tk=1024, out_dtype=F32, name="mm_dw_gate")
    dw_up = _mm(h2, dup, dims="tn", tm=1024, tn=1408, tk=1024, out_dtype=F32, name="mm_dw_up")
    dx1, dy1, dg2, dg3 = _mid_bwd(x1, y1, dh2, dx2, g2, g3)
    dmix = _mm(dy1, w_out, dims="nt", tn=1024, tk=1024, name="mm_dmix", **big)
    dw_out_a = _mm(a, dy1, dims="tn", tm=A_WIDTH, tn=1024, tk=1024, out_dtype=F32, name="mm_dw_out_a")
    dw_out_b = _mm(o_bf, dy1, dims="tn", tm=B_WIDTH, tn=1024, tk=1024, out_dtype=F32, name="mm_dw_out_b")
    duv, dln_g, dln_b, dw_s, dbz = _gate_bwd(uv, dmix, ln_g, ln_b, w_s, w_st, bz)
    dqkv, ds = _attn_bwd(qkv, dmix, o_bf, lse, bias, batch)
    drel = _rel_bias_grad(ds, buckets)
    dproj = _assemble_dproj(duv, dqkv)
    dh1 = _mm(dproj, w_in, dims="nt", tn=1024, tk=1408, name="mm_dh1", **big)
    dw_in = _mm(h1, dproj, dims="tn", tm=1024, tn=1408, tk=1024, out_dtype=F32, name="mm_dw_in")
    dx0, dg1 = _in_bwd(x, dh1, dx1, g1)

    grads = dict(
        norm_mix_pre=dg1, norm_mix_post=dg2, norm_ffn_pre=dg3, norm_ffn_post=dg4,
        w_in=dw_in, ln_v_gain=dln_g, ln_v_bias=dln_b, spatial_w=dw_s,
        spatial_b=dbz[:, ::HEAD_DIM].T,
        rel_bias=drel.reshape(B_HEADS, NUM_BUCKETS).T,
        w_out=jnp.concatenate([dw_out_a, dw_out_b], axis=0),
        w_gate=dw_gate, w_up=dw_up, conv_w=dconv_w, conv_b=dconv_b, w_down=dw_down,
    )
    return loss, dx0, grads


def _mesh_pos():
    x, y, c = lax.axis_index("x"), lax.axis_index("y"), lax.axis_index("c")
    chips = [(1 - x, y), (x, 1 - y), (1 - x, 1 - y)]
    return x, y, c, chips


ANY = pl.BlockSpec(memory_space=pl.ANY)


def _gather_weights(shards, conv_w_shard):
    nt = len(shards)

    def body(*refs):
        shard_refs = refs[:nt]
        cw_ref = refs[nt]
        out_refs = refs[nt + 1:2 * nt + 1]
        cw_out = refs[2 * nt + 1]
        send_sems, recv_sems = refs[2 * nt + 2:]
        x, y, c, chips = _mesh_pos()
        s = 2 * x + y
        sib = (x, y, 1 - c)

        def half(ref, chip, which, t):
            rows = shards[t].shape[0] // 2
            return ref.at[2 * chip[0] + chip[1], pl.ds(which * rows, rows), :]

        def rcopy(k, src, dst, to):
            return pltpu.make_async_remote_copy(src_ref=src, dst_ref=dst, send_sem=send_sems.at[k],
                                                recv_sem=recv_sems.at[k], device_id=to, device_id_type=MESH)

        sends = []
        for t in range(nt):
            rows = shards[t].shape[0] // 2
            for j, chip in enumerate(chips):
                sends.append(rcopy(7 * t + j, shard_refs[t].at[pl.ds(c * rows, rows), :],
                                   half(out_refs[t], (x, y), c, t), (*chip, c)))
        for j, chip in enumerate(chips):
            sends.append(rcopy(7 * nt + j, cw_ref, cw_out.at[s], (*chip, c)))
        for cp in sends:
            cp.start()
        fwd = []
        for t in range(nt):
            for j, chip in enumerate(chips):
                landed = half(out_refs[t], chip, c, t)
                rcopy(7 * t + j, landed, landed, (*chip, c)).wait_recv()
                f = rcopy(7 * t + 3 + j, landed, landed, sib)
                f.start()
                fwd.append(f)
        for j, chip in enumerate(chips):
            dst = cw_out.at[2 * chip[0] + chip[1]]
            rcopy(7 * nt + j, dst, dst, (*chip, c)).wait_recv()
        for t in range(nt):
            for j, chip in enumerate(chips):
                other = half(out_refs[t], chip, 1 - c, t)
                rcopy(7 * t + 3 + j, other, other, sib).wait_recv()
        for cp in sends + fwd:
            cp.wait_send()

    out_shape = [jax.ShapeDtypeStruct((N_SHARD,) + sh.shape, sh.dtype) for sh in shards]
    out_shape.append(jax.ShapeDtypeStruct((N_SHARD,) + conv_w_shard.shape, conv_w_shard.dtype))
    nsem = 7 * nt + 3
    return pl.pallas_call(
        body, in_specs=[ANY] * (nt + 1), out_specs=[ANY] * (nt + 1), out_shape=out_shape,
        scratch_shapes=[pltpu.SemaphoreType.DMA((nsem,)), pltpu.SemaphoreType.DMA((nsem,))],
        compiler_params=pltpu.CompilerParams(has_side_effects=True), name="gather_weights",
    )(*shards, conv_w_shard)


def _exchange_halves(grads):
    nt = len(grads)

    def body(*refs):
        g_refs = refs[:nt]
        out_refs = refs[nt:2 * nt]
        send_sems, recv_sems = refs[2 * nt:]
        x, y, c, _ = _mesh_pos()
        copies = []
        for t in range(nt):
            rows = grads[t].shape[1] // 2
            copies.append(pltpu.make_async_remote_copy(
                src_ref=g_refs[t].at[:, pl.ds((1 - c) * rows, rows), :], dst_ref=out_refs[t],
                send_sem=send_sems.at[t], recv_sem=recv_sems.at[t], device_id=(x, y, 1 - c), device_id_type=MESH))
        for cp in copies:
            cp.start()
        for cp in copies:
            cp.wait()

    out_shape = [jax.ShapeDtypeStruct((N_SHARD, g.shape[1] // 2, g.shape[2]), g.dtype) for g in grads]
    return pl.pallas_call(
        body, in_specs=[ANY] * nt, out_specs=[ANY] * nt, out_shape=out_shape,
        scratch_shapes=[pltpu.SemaphoreType.DMA((nt,)), pltpu.SemaphoreType.DMA((nt,))],
        compiler_params=pltpu.CompilerParams(has_side_effects=True), name="rs_sibling_exchange",
    )(*grads)


def _add_halves(g, recv, c_idx):
    _, rows2, cols = g.shape
    rows = rows2 // 2
    tr = rows // 2 if rows % 16 == 0 and rows >= 256 else rows
    nblk = rows // tr

    def body(c_ref, g_ref, r_ref, o_ref):
        o_ref[...] = (g_ref[...] + r_ref[...]).astype(BF16)

    return pl.pallas_call(
        body,
        grid_spec=pltpu.PrefetchScalarGridSpec(
            num_scalar_prefetch=1, grid=(N_SHARD, nblk),
            in_specs=[pl.BlockSpec((None, tr, cols), lambda s, i, c: (s, c[0] * nblk + i, 0)),
                      pl.BlockSpec((None, tr, cols), lambda s, i, c: (s, i, 0))],
            out_specs=pl.BlockSpec((None, tr, cols), lambda s, i, c: (s, i, 0))),
        out_shape=jax.ShapeDtypeStruct((N_SHARD, rows, cols), BF16),
        compiler_params=_params(("parallel", "parallel")), name="rs_add_halves",
    )(c_idx, g, recv)


def _exchange_chips(parts):
    nt = len(parts)

    def body(*refs):
        p_refs = refs[:nt]
        out_refs = refs[nt:2 * nt]
        send_sems, recv_sems = refs[2 * nt:]
        x, y, c, chips = _mesh_pos()
        copies = []
        for t in range(nt):
            for j, chip in enumerate(chips):
                copies.append(pltpu.make_async_remote_copy(
                    src_ref=p_refs[t].at[2 * chip[0] + chip[1]], dst_ref=out_refs[t].at[j],
                    send_sem=send_sems.at[3 * t + j], recv_sem=recv_sems.at[3 * t + j],
                    device_id=(*chip, c), device_id_type=MESH))
        for cp in copies:
            cp.start()
        for cp in copies:
            cp.wait()

    out_shape = [jax.ShapeDtypeStruct((3,) + p.shape[1:], p.dtype) for p in parts]
    return pl.pallas_call(
        body, in_specs=[ANY] * nt, out_specs=[ANY] * nt, out_shape=out_shape,
        scratch_shapes=[pltpu.SemaphoreType.DMA((3 * nt,)), pltpu.SemaphoreType.DMA((3 * nt,))],
        compiler_params=pltpu.CompilerParams(has_side_effects=True), name="rs_chip_exchange",
    )(*parts)


def _add_chips(part, recv, s_idx, c_idx):
    _, rows, cols = part.shape
    tr = rows // 2 if rows % 32 == 0 and rows >= 256 else rows
    nblk = rows // tr

    def body(idx_ref, p_ref, r_ref, o_ref):
        acc = p_ref[...].astype(F32)
        for j in range(3):
            acc = acc + r_ref[j].astype(F32)
        o_ref[...] = acc

    return pl.pallas_call(
        body,
        grid_spec=pltpu.PrefetchScalarGridSpec(
            num_scalar_prefetch=1, grid=(nblk,),
            in_specs=[pl.BlockSpec((None, tr, cols), lambda i, idx: (idx[0], i, 0)),
                      pl.BlockSpec((3, tr, cols), lambda i, idx: (0, i, 0))],
            out_specs=pl.BlockSpec((tr, cols), lambda i, idx: (idx[1] * nblk + i, 0))),
        out_shape=jax.ShapeDtypeStruct((2 * rows, cols), F32),
        compiler_params=_params(("parallel",)), name="rs_add_chips",
    )(jnp.concatenate([s_idx, c_idx]), part, recv)


def _share_halves(fulls):
    nt = len(fulls)

    def body(*refs):
        out_refs = refs[nt:2 * nt]
        send_sems, recv_sems = refs[2 * nt:]
        x, y, c, _ = _mesh_pos()
        copies = []
        for t in range(nt):
            rows = fulls[t].shape[0] // 2
            mine = out_refs[t].at[pl.ds(c * rows, rows), :]
            copies.append(pltpu.make_async_remote_copy(
                src_ref=mine, dst_ref=mine, send_sem=send_sems.at[t], recv_sem=recv_sems.at[t],
                device_id=(x, y, 1 - c), device_id_type=MESH))
        for cp in copies:
            cp.start()
        for t in range(nt):
            rows = fulls[t].shape[0] // 2
            theirs = out_refs[t].at[pl.ds((1 - c) * rows, rows), :]
            pltpu.make_async_remote_copy(
                src_ref=theirs, dst_ref=theirs, send_sem=send_sems.at[t], recv_sem=recv_sems.at[t],
                device_id=(x, y, 1 - c), device_id_type=MESH).wait_recv()
        for cp in copies:
            cp.wait_send()

    out_shape = [jax.ShapeDtypeStruct(f.shape, f.dtype) for f in fulls]
    return pl.pallas_call(
        body, in_specs=[ANY] * nt, out_specs=[ANY] * nt, out_shape=out_shape,
        input_output_aliases={t: t for t in range(nt)},
        scratch_shapes=[pltpu.SemaphoreType.DMA((nt,)), pltpu.SemaphoreType.DMA((nt,))],
        compiler_params=pltpu.CompilerParams(has_side_effects=True), name="rs_share_halves",
    )(*fulls)


def _allreduce_small(packed):
    rows = packed.shape[0]

    def body(p_ref, o_ref, sib_ref, chip_ref, send_sems, recv_sems):
        x, y, c, chips = _mesh_pos()
        first = pltpu.make_async_remote_copy(src_ref=p_ref, dst_ref=sib_ref, send_sem=send_sems.at[0],
                                             recv_sem=recv_sems.at[0], device_id=(x, y, 1 - c), device_id_type=MESH)
        first.start()
        first.wait()
        o_ref[...] = p_ref[...] + sib_ref[...]
        copies = [pltpu.make_async_remote_copy(src_ref=o_ref, dst_ref=chip_ref.at[j], send_sem=send_sems.at[1 + j],
                                               recv_sem=recv_sems.at[1 + j], device_id=(*chip, c), device_id_type=MESH)
                  for j, chip in enumerate(chips)]
        for cp in copies:
            cp.start()
        for cp in copies:
            cp.wait()
        o_ref[...] = (o_ref[...] + chip_ref[0]) + (chip_ref[1] + chip_ref[2])

    vm = pl.BlockSpec(memory_space=pltpu.VMEM)
    return pl.pallas_call(
        body, in_specs=[vm], out_specs=vm, out_shape=jax.ShapeDtypeStruct(packed.shape, F32),
        scratch_shapes=[pltpu.VMEM((rows, 128), F32), pltpu.VMEM((3, rows, 128), F32),
                        pltpu.SemaphoreType.DMA((4,)), pltpu.SemaphoreType.DMA((4,))],
        compiler_params=pltpu.CompilerParams(has_side_effects=True, vmem_limit_bytes=VMEM_LIMIT),
        name="allreduce_small",
    )(packed)


def _adamw(w, g, m, v, name):
    rows, cols = w.shape
    tr = rows
    if rows * cols > 256 * 1024:
        tr = next(cand for cand in (256, 176, 128) if rows % cand == 0)

    def body(w_ref, g_ref, m_ref, v_ref, go_ref, d_ref, nm_ref, nv_ref):
        gv = g_ref[...]
        go_ref[...] = gv
        nm = ADAM_B1 * m_ref[...] + (1.0 - ADAM_B1) * gv
        nv = ADAM_B2 * v_ref[...] + (1.0 - ADAM_B2) * (gv * gv)
        m_hat = nm / (1.0 - ADAM_B1 ** ADAM_STEP)
        v_hat = nv / (1.0 - ADAM_B2 ** ADAM_STEP)
        d_ref[...] = -ADAM_LR * (m_hat / (jnp.sqrt(v_hat) + ADAM_EPS) + ADAM_WD * w_ref[...])
        nm_ref[...] = nm
        nv_ref[...] = nv

    spec = pl.BlockSpec((tr, cols), lambda i: (i, 0))
    sds = jax.ShapeDtypeStruct((rows, cols), F32)
    return pl.pallas_call(
        body, grid=(rows // tr,), in_specs=[spec] * 4, out_specs=[spec] * 4, out_shape=[sds] * 4,
        compiler_params=_params(("parallel",)), name=name,
    )(w, g, m, v)


def _pack(arrays, rows):
    flat = jnp.concatenate([a.reshape(-1) for a in arrays])
    flat = jnp.pad(flat, (0, rows * 128 - flat.shape[0]))
    return flat.reshape(rows, 128)


def _unpack(packed, shapes):
    flat = packed.reshape(-1)
    out, off = [], 0
    for sh in shapes:
        n = int(np.prod(sh))
        out.append(flat[off:off + n].reshape(sh))
        off += n
    return out


SMALL = ["norm_mix_pre", "norm_mix_post", "norm_ffn_pre", "norm_ffn_post", "ln_v_gain", "ln_v_bias",
         "spatial_w", "spatial_b", "rel_bias", "conv_b"]
LARGE = ["w_in", "w_gate", "w_up", "w_down", "w_out"]
ORDER = ["norm_mix_pre", "norm_mix_post", "norm_ffn_pre", "norm_ffn_post", "w_in", "ln_v_gain", "ln_v_bias",
         "spatial_w", "spatial_b", "rel_bias", "w_out", "w_gate", "w_up", "conv_w", "conv_b", "w_down"]


def _col_shards(full):
    rows, cols4 = full.shape
    return full.reshape(rows, N_SHARD, cols4 // N_SHARD).transpose(1, 0, 2)


def _from_col_shards(g):
    n, rows, cols = g.shape
    return g.transpose(1, 0, 2).reshape(rows, n * cols)


def kernel(x, norm_mix_pre, norm_mix_post, norm_ffn_pre, norm_ffn_post, w_in, ln_v_gain, ln_v_bias, spatial_w, spatial_b, rel_bias, w_out, w_gate, w_up, conv_w, conv_b, w_down, loss_target, m_norm_mix_pre, m_norm_mix_post, m_norm_ffn_pre, m_norm_ffn_post, m_w_in, m_ln_v_gain, m_ln_v_bias, m_spatial_w, m_spatial_b, m_rel_bias, m_w_out, m_w_gate, m_w_up, m_conv_w, m_conv_b, m_w_down, v_norm_mix_pre, v_norm_mix_post, v_norm_ffn_pre, v_norm_ffn_post, v_w_in, v_ln_v_gain, v_ln_v_bias, v_spatial_w, v_spatial_b, v_rel_bias, v_w_out, v_w_gate, v_w_up, v_conv_w, v_conv_b, v_w_down):
    params = dict(norm_mix_pre=norm_mix_pre, norm_mix_post=norm_mix_post, norm_ffn_pre=norm_ffn_pre,
                  norm_ffn_post=norm_ffn_post, w_in=w_in, ln_v_gain=ln_v_gain, ln_v_bias=ln_v_bias,
                  spatial_w=spatial_w, spatial_b=spatial_b, rel_bias=rel_bias, w_out=w_out, w_gate=w_gate,
                  w_up=w_up, conv_w=conv_w, conv_b=conv_b, w_down=w_down)
    mom = dict(norm_mix_pre=m_norm_mix_pre, norm_mix_post=m_norm_mix_post, norm_ffn_pre=m_norm_ffn_pre,
               norm_ffn_post=m_norm_ffn_post, w_in=m_w_in, ln_v_gain=m_ln_v_gain, ln_v_bias=m_ln_v_bias,
               spatial_w=m_spatial_w, spatial_b=m_spatial_b, rel_bias=m_rel_bias, w_out=m_w_out, w_gate=m_w_gate,
               w_up=m_w_up, conv_w=m_conv_w, conv_b=m_conv_b, w_down=m_w_down)
    var = dict(norm_mix_pre=v_norm_mix_pre, norm_mix_post=v_norm_mix_post, norm_ffn_pre=v_norm_ffn_pre,
               norm_ffn_post=v_norm_ffn_post, w_in=v_w_in, ln_v_gain=v_ln_v_gain, ln_v_bias=v_ln_v_bias,
               spatial_w=v_spatial_w, spatial_b=v_spatial_b, rel_bias=v_rel_bias, w_out=v_w_out, w_gate=v_w_gate,
               w_up=v_w_up, conv_w=v_conv_w, conv_b=v_conv_b, w_down=v_w_down)

    batch = x.shape[0]
    xi, yi, ci = lax.axis_index("x"), lax.axis_index("y"), lax.axis_index("c")
    s_idx = (2 * xi + yi).astype(jnp.int32).reshape(1)
    c_idx = ci.astype(jnp.int32).reshape(1)

    shards = [params[n][0].astype(BF16) for n in LARGE]
    gathered = _gather_weights(shards, conv_w[0])
    g_in, g_gate, g_up, g_down, g_out, g_convw = [
        lax.dynamic_update_index_in_dim(g, own, s_idx[0], 0) for g, own in zip(gathered, shards + [conv_w[0]])]
    w_in_f = _from_col_shards(g_in)
    w_gate_f = _from_col_shards(g_gate)
    w_up_f = _from_col_shards(g_up)
    w_down_f = g_down.reshape(D_FF, D_MODEL)
    w_out_f = g_out.reshape(D_MODEL, D_MODEL)
    conv_w_f = _from_col_shards(g_convw)

    loss_part, dx0, grads = _local_step(
        x.reshape(batch * SEQ, D_MODEL), loss_target.reshape(batch * SEQ, D_MODEL),
        norm_mix_pre, norm_mix_post, norm_ffn_pre, norm_ffn_post, w_in_f,
        ln_v_gain.reshape(1, A_WIDTH), ln_v_bias.reshape(1, A_WIDTH), spatial_w[0], spatial_b[0], rel_bias,
        w_out_f, w_gate_f, w_up_f, conv_w_f, conv_b, w_down_f, batch)
    loss = lax.psum(loss_part[0, 0], ("x", "y", "c"))
    grad_x = dx0.reshape(batch, SEQ, D_MODEL)

    big = [_col_shards(grads["w_in"]), _col_shards(grads["w_gate"]), _col_shards(grads["w_up"]),
           grads["w_down"].reshape(N_SHARD, SHARD_FF, D_MODEL),
           grads["w_out"].reshape(N_SHARD, D_MODEL // N_SHARD, D_MODEL)]
    recv_a = _exchange_halves(big)
    parts = [_add_halves(g, r, c_idx) for g, r in zip(big, recv_a)]
    recv_b = _exchange_chips(parts)
    fulls = [_add_chips(p, r, s_idx, c_idx) for p, r in zip(parts, recv_b)]
    reduced = dict(zip(LARGE, _share_halves(fulls)))

    small_g = [grads[n].reshape(params[n].shape) for n in SMALL] + [grads["conv_w"]]
    n_small = sum(int(np.prod(g.shape)) for g in small_g)
    small_rows = -(-n_small // (8 * 128)) * 8
    summed = _unpack(_allreduce_small(_pack(small_g, small_rows)),
                     [params[n].shape for n in SMALL] + [(3, D_FF)])
    for n, g in zip(SMALL, summed[:-1]):
        reduced[n] = g
    reduced["conv_w"] = lax.dynamic_slice_in_dim(summed[-1], s_idx[0] * SHARD_FF, SHARD_FF, axis=1)[None]

    out_g, out_d, out_m, out_v = {}, {}, {}, {}
    for n in LARGE:
        shp = params[n].shape
        g, d, nm, nv = _adamw(params[n][0], reduced[n], mom[n][0], var[n][0], name=f"adamw_{n}")
        out_g[n], out_d[n], out_m[n], out_v[n] = g.reshape(shp), d.reshape(shp), nm.reshape(shp), nv.reshape(shp)
    small_names = SMALL + ["conv_w"]
    rows_s = -(-sum(int(np.prod(params[n].shape)) for n in small_names) // (8 * 128)) * 8
    _, d, nm, nv = _adamw(_pack([params[n] for n in small_names], rows_s),
                          _pack([reduced[n] for n in small_names], rows_s),
                          _pack([mom[n] for n in small_names], rows_s), _pack([var[n] for n in small_names], rows_s),
                          name="adamw_small")
    shapes = [params[n].shape for n in small_names]
    for n, dd, mm, vv in zip(small_names, _unpack(d, shapes), _unpack(nm, shapes), _unpack(nv, shapes)):
        out_g[n], out_d[n], out_m[n], out_v[n] = reduced[n], dd, mm, vv

    return (loss, grad_x, *[out_g[n] for n in ORDER], *[out_d[n] for n in ORDER],
            *[out_m[n] for n in ORDER], *[out_v[n] for n in ORDER])
```

```python
import functools
import math

import numpy as np
import jax
import jax.numpy as jnp
from jax import lax
from jax.experimental import pallas as pl
from jax.experimental.pallas import tpu as pltpu

F32 = jnp.float32
BF16 = jnp.bfloat16
MESH = pl.DeviceIdType.MESH

D_MODEL = 1024
SEQ = 2048
HEAD_DIM = 64
A_GROUPS = 4
A_WIDTH = 256
B_HEADS = 12
B_WIDTH = 768
CHUNK = 128
DILATED = ((128, 1), (512, 4), (2048, 16))
NUM_BUCKETS = 32
MAX_DISTANCE = 2048
D_FF = 2816
IN_COLS = 2816
NORM_EPS = 1e-6
NEG_INF = -1e30
N_SHARD = 4
SHARD_FF = D_FF // N_SHARD
LANE_BLOCK = 256
VMEM_LIMIT = 56 * 1024 * 1024

ADAM_LR = 0.001
ADAM_B1 = 0.9
ADAM_B2 = 0.999
ADAM_EPS = 1e-08
ADAM_WD = 0.01
ADAM_STEP = 10

GELU_C = math.sqrt(2.0 / math.pi)
GELU_A = 0.044715


def _params(sem=None):
    return pltpu.CompilerParams(dimension_semantics=sem, vmem_limit_bytes=VMEM_LIMIT)


def _dot(a, b, precision=None):
    return jnp.dot(a, b, preferred_element_type=F32, precision=precision)


def _dot_nt(a, b):
    return lax.dot_general(a, b, (((1,), (1,)), ((), ())), preferred_element_type=F32)


def _dot_tn(a, b):
    return lax.dot_general(a, b, (((0,), (0,)), ((), ())), preferred_element_type=F32)


def _gelu(x):
    t = jnp.tanh(GELU_C * (x + GELU_A * (x * x * x)))
    return 0.5 * x * (1.0 + t)


def _gelu_and_grad(x):
    x2 = x * x
    t = jnp.tanh(GELU_C * (x + GELU_A * (x2 * x)))
    g = 0.5 * x * (1.0 + t)
    dg = 0.5 * (1.0 + t) + 0.5 * x * (1.0 - t * t) * (GELU_C * (1.0 + 3.0 * GELU_A * x2))
    return g, dg


def _mm(a, b, *, dims, tm, tn, tk, out_dtype, name, add=None):
    if dims == "nn":
        m, k = a.shape
        n = b.shape[1]
        a_spec = pl.BlockSpec((tm, tk), lambda i, j, kk: (i, kk))
        b_spec = pl.BlockSpec((tk, tn), lambda i, j, kk: (kk, j))
        dot = _dot
    elif dims == "nt":
        m, k = a.shape
        n = b.shape[0]
        a_spec = pl.BlockSpec((tm, tk), lambda i, j, kk: (i, kk))
        b_spec = pl.BlockSpec((tn, tk), lambda i, j, kk: (j, kk))
        dot = _dot_nt
    else:
        k, m = a.shape
        n = b.shape[1]
        a_spec = pl.BlockSpec((tk, tm), lambda i, j, kk: (kk, i))
        b_spec = pl.BlockSpec((tk, tn), lambda i, j, kk: (kk, j))
        dot = _dot_tn
    assert m % tm == 0 and n % tn == 0 and k % tk == 0, (name, m, n, k)
    nk = k // tk
    has_add = add is not None

    def body(*refs):
        if has_add:
            a_ref, b_ref, add_ref, o_ref, acc_ref = refs
        else:
            a_ref, b_ref, o_ref, acc_ref = refs
        kk = pl.program_id(2)

        @pl.when(kk == 0)
        def _():
            if has_add:
                acc_ref[...] = add_ref[...].astype(F32)
            else:
                acc_ref[...] = jnp.zeros_like(acc_ref)

        acc_ref[...] += dot(a_ref[...].astype(BF16), b_ref[...].astype(BF16))

        @pl.when(kk == nk - 1)
        def _():
            o_ref[...] = acc_ref[...].astype(out_dtype)

    in_specs = [a_spec, b_spec]
    args = [a, b]
    if has_add:
        in_specs.append(pl.BlockSpec((tm, tn), lambda i, j, kk: (i, j)))
        args.append(add)
    return pl.pallas_call(
        body,
        grid=(m // tm, n // tn, nk),
        in_specs=in_specs,
        out_specs=pl.BlockSpec((tm, tn), lambda i, j, kk: (i, j)),
        out_shape=jax.ShapeDtypeStruct((m, n), out_dtype),
        scratch_shapes=[pltpu.VMEM((tm, tn), F32)],
        compiler_params=_params(("parallel", "parallel", "arbitrary")),
        name=name,
    )(*args)


ROW_TILE = 512


def _row_spec(width=D_MODEL):
    return pl.BlockSpec((ROW_TILE, width), lambda i: (i, 0))


def _vec_spec(width=D_MODEL):
    return pl.BlockSpec((1, width), lambda i: (0, 0))


def _rstd(v):
    return lax.rsqrt(jnp.mean(v * v, axis=-1, keepdims=True) + NORM_EPS)


def _rms_fwd(x, g):
    m = x.shape[0]

    def body(x_ref, g_ref, h_ref):
        xv = x_ref[...]
        h_ref[...] = (xv * _rstd(xv) * g_ref[...]).astype(BF16)

    return pl.pallas_call(
        body, grid=(m // ROW_TILE,),
        in_specs=[_row_spec(), _vec_spec()],
        out_specs=_row_spec(),
        out_shape=jax.ShapeDtypeStruct((m, D_MODEL), BF16),
        compiler_params=_params(("parallel",)), name="rms_fwd",
    )(x, g)


def _mid_fwd(x0, y1, g2, g3):
    m = x0.shape[0]

    def body(x0_ref, y1_ref, g2_ref, g3_ref, x1_ref, h2_ref):
        y1v = y1_ref[...]
        x1 = x0_ref[...] + y1v * _rstd(y1v) * g2_ref[...]
        x1_ref[...] = x1
        h2_ref[...] = (x1 * _rstd(x1) * g3_ref[...]).astype(BF16)

    return pl.pallas_call(
        body, grid=(m // ROW_TILE,),
        in_specs=[_row_spec(), _row_spec(), _vec_spec(), _vec_spec()],
        out_specs=[_row_spec(), _row_spec()],
        out_shape=[jax.ShapeDtypeStruct((m, D_MODEL), F32), jax.ShapeDtypeStruct((m, D_MODEL), BF16)],
        compiler_params=_params(("parallel",)), name="mid_fwd",
    )(x0, y1, g2, g3)


def _rms_bwd_rows(dout, v, g):
    r = _rstd(v)
    n = v * r
    dn = dout * g
    dv = r * (dn - n * jnp.mean(dn * n, axis=-1, keepdims=True))
    dg = jnp.sum(dout * n, axis=0, keepdims=True)
    return dv, dg


def _loss_head(x1, y2, tgt, g4):
    m = x1.shape[0]

    def body(x1_ref, y2_ref, t_ref, g4_ref, dx2_ref, dy2_ref, dg4_ref, loss_ref):
        i = pl.program_id(0)

        @pl.when(i == 0)
        def _():
            dg4_ref[...] = jnp.zeros_like(dg4_ref)
            loss_ref[...] = jnp.zeros_like(loss_ref)

        y2v = y2_ref[...]
        g4 = g4_ref[...]
        x2 = x1_ref[...] + y2v * _rstd(y2v) * g4
        err = x2 - t_ref[...]
        loss_ref[...] += 0.5 * jnp.sum(jnp.mean(err * err, axis=-1, keepdims=True), axis=0, keepdims=True)
        dx2 = err * (1.0 / D_MODEL)
        dx2_ref[...] = dx2
        dy2, dg4 = _rms_bwd_rows(dx2, y2v, g4)
        dy2_ref[...] = dy2.astype(BF16)
        dg4_ref[...] += dg4

    return pl.pallas_call(
        body, grid=(m // ROW_TILE,),
        in_specs=[_row_spec(), _row_spec(), _row_spec(), _vec_spec()],
        out_specs=[_row_spec(), _row_spec(), _vec_spec(), pl.BlockSpec((1, 1), lambda i: (0, 0))],
        out_shape=[jax.ShapeDtypeStruct((m, D_MODEL), F32), jax.ShapeDtypeStruct((m, D_MODEL), BF16),
                   jax.ShapeDtypeStruct((1, D_MODEL), F32), jax.ShapeDtypeStruct((1, 1), F32)],
        compiler_params=_params(("arbitrary",)), name="loss_head",
    )(x1, y2, tgt, g4)


def _mid_bwd(x1, y1, dh2, dx2, g2, g3):
    m = x1.shape[0]

    def body(x1_ref, y1_ref, dh2_ref, dx2_ref, g2_ref, g3_ref, dx1_ref, dy1_ref, dg2_ref, dg3_ref):
        i = pl.program_id(0)

        @pl.when(i == 0)
        def _():
            dg2_ref[...] = jnp.zeros_like(dg2_ref)
            dg3_ref[...] = jnp.zeros_like(dg3_ref)

        d3, dg3 = _rms_bwd_rows(dh2_ref[...], x1_ref[...], g3_ref[...])
        dx1 = dx2_ref[...] + d3
        dx1_ref[...] = dx1
        dy1, dg2 = _rms_bwd_rows(dx1, y1_ref[...], g2_ref[...])
        dy1_ref[...] = dy1.astype(BF16)
        dg2_ref[...] += dg2
        dg3_ref[...] += dg3

    return pl.pallas_call(
        body, grid=(m // ROW_TILE,),
        in_specs=[_row_spec(), _row_spec(), _row_spec(), _row_spec(), _vec_spec(), _vec_spec()],
        out_specs=[_row_spec(), _row_spec(), _vec_spec(), _vec_spec()],
        out_shape=[jax.ShapeDtypeStruct((m, D_MODEL), F32), jax.ShapeDtypeStruct((m, D_MODEL), BF16),
                   jax.ShapeDtypeStruct((1, D_MODEL), F32), jax.ShapeDtypeStruct((1, D_MODEL), F32)],
        compiler_params=_params(("arbitrary",)), name="mid_bwd",
    )(x1, y1, dh2, dx2, g2, g3)


def _in_bwd(x0, dh1, dx1, g1):
    m = x0.shape[0]

    def body(x0_ref, dh1_ref, dx1_ref, g1_ref, dx0_ref, dg1_ref):
        i = pl.program_id(0)

        @pl.when(i == 0)
        def _():
            dg1_ref[...] = jnp.zeros_like(dg1_ref)

        d1, dg1 = _rms_bwd_rows(dh1_ref[...], x0_ref[...], g1_ref[...])
        dx0_ref[...] = dx1_ref[...] + d1
        dg1_ref[...] += dg1

    return pl.pallas_call(
        body, grid=(m // ROW_TILE,),
        in_specs=[_row_spec(), _row_spec(), _row_spec(), _vec_spec()],
        out_specs=[_row_spec(), _vec_spec()],
        out_shape=[jax.ShapeDtypeStruct((m, D_MODEL), F32), jax.ShapeDtypeStruct((1, D_MODEL), F32)],
        compiler_params=_params(("arbitrary",)), name="in_bwd",
    )(x0, dh1, dx1, g1)


GATE_ROWS = 512


def _group_mean_matrix():
    p = np.zeros((A_WIDTH, A_WIDTH), np.float32)
    for g in range(A_GROUPS):
        p[g * HEAD_DIM:(g + 1) * HEAD_DIM, g * HEAD_DIM:(g + 1) * HEAD_DIM] = 1.0 / HEAD_DIM
    return jnp.asarray(p)


def _group_masks(width=A_WIDTH):
    lane = lax.broadcasted_iota(jnp.int32, (1, width), 1)
    return [(lane >= g * HEAD_DIM) & (lane < (g + 1) * HEAD_DIM) for g in range(width // HEAD_DIM)]


def _layernorm_groups(vg, pavg):
    hi = lax.Precision.HIGHEST
    mu = _dot(vg, pavg, hi)
    xc = vg - mu
    var = _dot(xc * xc, pavg, hi)
    rstd = lax.rsqrt(var + NORM_EPS)
    return xc * rstd, rstd


def _spatial_mix(w_bf, vn_chunk_bf, masks, bz):
    z = bz
    for g in range(A_GROUPS):
        z = z + jnp.where(masks[g], _dot(w_bf[g], vn_chunk_bf), 0.0)
    return z


def _gate_fwd(proj, ln_g, ln_b, w_s, bz):
    m = proj.shape[0]
    pavg = _group_mean_matrix()

    def body(u_ref, v_ref, lg_ref, lb_ref, w_ref, bz_ref, p_ref, a_ref):
        masks = _group_masks()
        row = lax.broadcasted_iota(jnp.int32, (CHUNK, CHUNK), 0)
        col = lax.broadcasted_iota(jnp.int32, (CHUNK, CHUNK), 1)
        w_bf = [jnp.where(row >= col, w_ref[g], 0.0).astype(BF16) for g in range(A_GROUPS)]
        ug = _gelu(u_ref[...])
        vhat, _ = _layernorm_groups(_gelu(v_ref[...]), p_ref[...])
        vn = vhat * lg_ref[...] + lb_ref[...]
        bz = bz_ref[...]
        for c in range(GATE_ROWS // CHUNK):
            sl = slice(c * CHUNK, (c + 1) * CHUNK)
            z = _spatial_mix(w_bf, vn[sl].astype(BF16), masks, bz)
            a_ref[sl, :] = (ug[sl] * z).astype(BF16)

    full = lambda shape: pl.BlockSpec(shape, lambda i: tuple(0 for _ in shape))
    return pl.pallas_call(
        body, grid=(m // GATE_ROWS,),
        in_specs=[pl.BlockSpec((GATE_ROWS, A_WIDTH), lambda i: (i, 0)),
                  pl.BlockSpec((GATE_ROWS, A_WIDTH), lambda i: (i, 1)),
                  full((1, A_WIDTH)), full((1, A_WIDTH)), full((A_GROUPS, CHUNK, CHUNK)),
                  full((CHUNK, A_WIDTH)), full((A_WIDTH, A_WIDTH))],
        out_specs=pl.BlockSpec((GATE_ROWS, A_WIDTH), lambda i: (i, 0)),
        out_shape=jax.ShapeDtypeStruct((m, A_WIDTH), BF16),
        compiler_params=_params(("parallel",)), name="gate_fwd",
    )(proj, proj, ln_g, ln_b, w_s, bz, pavg)


def _gate_bwd(proj, dmix, ln_g, ln_b, w_s, w_st, bz):
    m = proj.shape[0]
    pavg = _group_mean_matrix()
    nsteps = m // GATE_ROWS

    def body(u_ref, v_ref, da_ref, lg_ref, lb_ref, w_ref, wt_ref, bz_ref, p_ref,
             duv_ref, dlg_ref, dlb_ref, dw_ref, dbz_ref):
        i = pl.program_id(0)

        @pl.when(i == 0)
        def _():
            dlg_ref[...] = jnp.zeros_like(dlg_ref)
            dlb_ref[...] = jnp.zeros_like(dlb_ref)
            dw_ref[...] = jnp.zeros_like(dw_ref)
            dbz_ref[...] = jnp.zeros_like(dbz_ref)

        hi = lax.Precision.HIGHEST
        masks = _group_masks()
        row = lax.broadcasted_iota(jnp.int32, (CHUNK, CHUNK), 0)
        col = lax.broadcasted_iota(jnp.int32, (CHUNK, CHUNK), 1)
        tril = row >= col
        w_bf = [jnp.where(tril, w_ref[g], 0.0).astype(BF16) for g in range(A_GROUPS)]
        wt_bf = [jnp.where(col >= row, wt_ref[g], 0.0).astype(BF16) for g in range(A_GROUPS)]
        pavg_v = p_ref[...]
        lg = lg_ref[...]
        ug, dug = _gelu_and_grad(u_ref[...])
        vg, dvg_dx = _gelu_and_grad(v_ref[...])
        vhat, rstd = _layernorm_groups(vg, pavg_v)
        vn = vhat * lg + lb_ref[...]
        da = da_ref[...]
        bz = bz_ref[...]
        for c in range(GATE_ROWS // CHUNK):
            sl = slice(c * CHUNK, (c + 1) * CHUNK)
            vn_bf = vn[sl].astype(BF16)
            z = _spatial_mix(w_bf, vn_bf, masks, bz)
            dz = da[sl] * ug[sl]
            duv_ref[sl, 0:A_WIDTH] = da[sl] * z * dug[sl]
            dbz_ref[...] += dz
            dz_bf = dz.astype(BF16)
            dvn = jnp.zeros((CHUNK, A_WIDTH), F32)
            for g in range(A_GROUPS):
                dz_g = jnp.where(masks[g], dz, 0.0).astype(BF16)
                dw_ref[g] += jnp.where(tril, _dot_nt(dz_g, vn_bf), 0.0)
                dvn = dvn + jnp.where(masks[g], _dot(wt_bf[g], dz_bf), 0.0)
            vh = vhat[sl]
            dlb_ref[...] += jnp.sum(dvn, axis=0, keepdims=True)
            dlg_ref[...] += jnp.sum(dvn * vh, axis=0, keepdims=True)
            dvh = dvn * lg
            m1 = _dot(dvh, pavg_v, hi)
            m2 = _dot(dvh * vh, pavg_v, hi)
            duv_ref[sl, A_WIDTH:2 * A_WIDTH] = rstd[sl] * (dvh - m1 - vh * m2) * dvg_dx[sl]

        @pl.when(i == nsteps - 1)
        def _():
            dbz_ref[...] = _dot(dbz_ref[...], pavg_v * float(HEAD_DIM), hi)

    full = lambda shape: pl.BlockSpec(shape, lambda i: tuple(0 for _ in shape))
    return pl.pallas_call(
        body, grid=(nsteps,),
        in_specs=[pl.BlockSpec((GATE_ROWS, A_WIDTH), lambda i: (i, 0)),
                  pl.BlockSpec((GATE_ROWS, A_WIDTH), lambda i: (i, 1)),
                  pl.BlockSpec((GATE_ROWS, A_WIDTH), lambda i: (i, 0)),
                  full((1, A_WIDTH)), full((1, A_WIDTH)), full((A_GROUPS, CHUNK, CHUNK)),
                  full((A_GROUPS, CHUNK, CHUNK)), full((CHUNK, A_WIDTH)), full((A_WIDTH, A_WIDTH))],
        out_specs=[pl.BlockSpec((GATE_ROWS, 2 * A_WIDTH), lambda i: (i, 0)),
                   full((1, A_WIDTH)), full((1, A_WIDTH)), full((A_GROUPS, CHUNK, CHUNK)),
                   full((CHUNK, A_WIDTH))],
        out_shape=[jax.ShapeDtypeStruct((m, 2 * A_WIDTH), F32),
                   jax.ShapeDtypeStruct((1, A_WIDTH), F32), jax.ShapeDtypeStruct((1, A_WIDTH), F32),
                   jax.ShapeDtypeStruct((A_GROUPS, CHUNK, CHUNK), F32),
                   jax.ShapeDtypeStruct((CHUNK, A_WIDTH), F32)],
        compiler_params=_params(("arbitrary",)), name="gate_bwd",
    )(proj, proj, dmix, ln_g, ln_b, w_s, w_st, bz, pavg)


Q_BLOCK = 128
Q_COL, K_COL, V_COL = 2, 5, 8
N_COLBLK = IN_COLS // LANE_BLOCK
HEAD_BLOCKS = B_WIDTH // LANE_BLOCK
HEADS_PER_BLOCK = LANE_BLOCK // HEAD_DIM


def _t5_bucket_np(dist, dtype):
    max_exact = NUM_BUCKETS // 2
    d = np.maximum(dist, 1).astype(dtype)
    large = max_exact + (np.log(d / dtype(max_exact)) / dtype(math.log(MAX_DISTANCE / max_exact))
                         * dtype(NUM_BUCKETS - max_exact))
    large = np.minimum(large.astype(np.int32), NUM_BUCKETS - 1)
    return np.where(dist < max_exact, dist, large)


def _bucket_tables():
    i = np.arange(Q_BLOCK)[:, None]
    j = np.arange(Q_BLOCK)[None, :]
    tables = []
    for _, dil in DILATED:
        rel_prev = Q_BLOCK + i - j
        rel_cur = i - j
        rel = np.concatenate([rel_prev, rel_cur], axis=1)
        valid = np.concatenate([rel_prev <= Q_BLOCK, rel_cur >= 0], axis=1)
        dist = np.maximum(rel, 0) * dil
        b32 = _t5_bucket_np(dist, np.float32)
        b64 = _t5_bucket_np(dist, np.float64)
        assert np.array_equal(b32, b64)
        tables.append(np.where(valid, b32, -1).astype(np.int32))
    return np.stack(tables)


def _bias_tables(rel_bias, buckets_np):
    present = [sorted(set(int(v) for v in np.unique(buckets_np[c]) if v >= 0)) for c in range(len(DILATED))]

    def body(rb_ref, bk_ref, o_ref):
        for c in range(len(DILATED)):
            bk = bk_ref[c]
            for h in range(B_HEADS):
                acc = jnp.full((Q_BLOCK, 2 * Q_BLOCK), NEG_INF, F32)
                for b in present[c]:
                    acc = jnp.where(bk == b, rb_ref[b, h], acc)
                o_ref[c, h] = acc

    return pl.pallas_call(
        body,
        in_specs=[pl.BlockSpec(memory_space=pltpu.SMEM), pl.BlockSpec(memory_space=pltpu.VMEM)],
        out_specs=pl.BlockSpec(memory_space=pltpu.VMEM),
        out_shape=jax.ShapeDtypeStruct((len(DILATED), B_HEADS, Q_BLOCK, 2 * Q_BLOCK), F32),
        compiler_params=_params(), name="bias_tables",
    )(rel_bias, jnp.asarray(buckets_np))


def _head_masks():
    lane = lax.broadcasted_iota(jnp.int32, (1, LANE_BLOCK), 1)
    return [(lane >= h * HEAD_DIM) & (lane < (h + 1) * HEAD_DIM) for h in range(HEADS_PER_BLOCK)]


def _attn_fwd(proj, bias, dil, batch):
    m = proj.shape[0]
    tr = SEQ // dil
    nb = tr // Q_BLOCK
    proj3 = proj.reshape(batch, tr, dil * IN_COLS)

    def body(q_ref, k_ref, v_ref, b_ref, o_ref, l_ref):
        masks = _head_masks()

        def block(n, carry):
            r0 = pl.multiple_of(n * Q_BLOCK, Q_BLOCK)
            rows = pl.ds(r0, Q_BLOCK)
            q = q_ref[rows, :] * 0.125
            kc = k_ref[rows, :].astype(BF16)
            vc = v_ref[rows, :].astype(BF16)
            if nb > 1:
                p0 = pl.multiple_of(jnp.maximum(n - 1, 0) * Q_BLOCK, Q_BLOCK)
                kp = k_ref[pl.ds(p0, Q_BLOCK), :].astype(BF16)
                vp = v_ref[pl.ds(p0, Q_BLOCK), :].astype(BF16)
            o_acc = jnp.zeros((Q_BLOCK, LANE_BLOCK), F32)
            l_acc = jnp.zeros((Q_BLOCK, LANE_BLOCK), F32)
            for h in range(HEADS_PER_BLOCK):
                qh = jnp.where(masks[h], q, 0.0).astype(BF16)
                sc = _dot_nt(qh, kc) + b_ref[h, :, Q_BLOCK:]
                mx = jnp.max(sc, axis=1, keepdims=True)
                if nb > 1:
                    sp = _dot_nt(qh, kp) + jnp.where(n == 0, NEG_INF, b_ref[h, :, :Q_BLOCK])
                    mx = jnp.maximum(mx, jnp.max(sp, axis=1, keepdims=True))
                pc = jnp.exp(sc - mx)
                den = jnp.sum(pc, axis=1, keepdims=True)
                oh = _dot(pc.astype(BF16), vc)
                if nb > 1:
                    pp = jnp.exp(sp - mx)
                    den = den + jnp.sum(pp, axis=1, keepdims=True)
                    oh = oh + _dot(pp.astype(BF16), vp)
                o_acc = jnp.where(masks[h], oh / den, o_acc)
                l_acc = jnp.where(masks[h], mx + jnp.log(den), l_acc)
            o_ref[rows, :] = o_acc
            l_ref[rows, :] = l_acc
            return carry

        if nb > 1:
            lax.fori_loop(0, nb, block, 0)
        else:
            block(0, 0)

    def in_spec(col0):
        return pl.BlockSpec((None, tr, LANE_BLOCK), lambda b, r, g: (b, 0, r * N_COLBLK + col0 + g))

    out_spec = pl.BlockSpec((None, tr, LANE_BLOCK), lambda b, r, g: (b, 0, r * HEAD_BLOCKS + g))
    out_sds = jax.ShapeDtypeStruct((batch, tr, dil * B_WIDTH), F32)
    o, lse = pl.pallas_call(
        body, grid=(batch, dil, HEAD_BLOCKS),
        in_specs=[in_spec(Q_COL), in_spec(K_COL), in_spec(V_COL),
                  pl.BlockSpec((HEADS_PER_BLOCK, Q_BLOCK, 2 * Q_BLOCK), lambda b, r, g: (g, 0, 0))],
        out_specs=[out_spec, out_spec],
        out_shape=[out_sds, out_sds],
        compiler_params=_params(("parallel", "parallel", "parallel")), name=f"attn_fwd_d{dil}",
    )(proj3, proj3, proj3, bias)
    return o.reshape(m, B_WIDTH), lse.reshape(m, B_WIDTH)


def _attn_combine(outs, lses):
    m = outs[0].shape[0]
    nc = len(outs)

    def body(*refs):
        o_refs, l_refs = refs[:nc], refs[nc:2 * nc]
        of_ref, ob_ref, lt_ref = refs[2 * nc:]
        ls = [r[...] for r in l_refs]
        mx = functools.reduce(jnp.maximum, ls)
        ws = [jnp.exp(l - mx) for l in ls]
        tot = functools.reduce(lambda a, b: a + b, ws)
        inv = 1.0 / tot
        o = functools.reduce(lambda a, b: a + b, [w * inv * r[...] for w, r in zip(ws, o_refs)])
        of_ref[...] = o
        ob_ref[...] = o.astype(BF16)
        lt_ref[...] = mx + jnp.log(tot)

    spec = pl.BlockSpec((ROW_TILE, B_WIDTH), lambda i: (i, 0))
    return pl.pallas_call(
        body, grid=(m // ROW_TILE,),
        in_specs=[spec] * (2 * nc), out_specs=[spec, spec, spec],
        out_shape=[jax.ShapeDtypeStruct((m, B_WIDTH), F32), jax.ShapeDtypeStruct((m, B_WIDTH), BF16),
                   jax.ShapeDtypeStruct((m, B_WIDTH), F32)],
        compiler_params=_params(("parallel",)), name="attn_combine",
    )(*outs, *lses)


def _attn_bwd(proj, dmix, o, lse, bias, dil, batch):
    m = proj.shape[0]
    tr = SEQ // dil
    nb = tr // Q_BLOCK
    proj3 = proj.reshape(batch, tr, dil * IN_COLS)
    dmix3 = dmix.reshape(batch, tr, dil * D_MODEL)
    o3 = o.reshape(batch, tr, dil * B_WIDTH)
    l3 = lse.reshape(batch, tr, dil * B_WIDTH)

    def body(q_ref, k_ref, v_ref, do_ref, o_ref, l_ref, b_ref, dq_ref, dk_ref, dv_ref, ds_ref):
        first = (pl.program_id(1) == 0) & (pl.program_id(2) == 0)

        @pl.when(first)
        def _():
            ds_ref[...] = jnp.zeros_like(ds_ref)

        dk_ref[...] = jnp.zeros_like(dk_ref)
        dv_ref[...] = jnp.zeros_like(dv_ref)
        masks = _head_masks()

        def block(n, carry):
            r0 = pl.multiple_of(n * Q_BLOCK, Q_BLOCK)
            rows = pl.ds(r0, Q_BLOCK)
            q = q_ref[rows, :] * 0.125
            kc = k_ref[rows, :].astype(BF16)
            vc = v_ref[rows, :].astype(BF16)
            do = do_ref[rows, :]
            ov = o_ref[rows, :]
            lv = l_ref[rows, :]
            if nb > 1:
                p0 = pl.multiple_of(jnp.maximum(n - 1, 0) * Q_BLOCK, Q_BLOCK)
                prow = pl.ds(p0, Q_BLOCK)
                kp = k_ref[prow, :].astype(BF16)
                vp = v_ref[prow, :].astype(BF16)
                dkp = jnp.zeros((Q_BLOCK, LANE_BLOCK), F32)
                dvp = jnp.zeros((Q_BLOCK, LANE_BLOCK), F32)
            dq = jnp.zeros((Q_BLOCK, LANE_BLOCK), F32)
            dkc = jnp.zeros((Q_BLOCK, LANE_BLOCK), F32)
            dvc = jnp.zeros((Q_BLOCK, LANE_BLOCK), F32)
            for h in range(HEADS_PER_BLOCK):
                qh = jnp.where(masks[h], q, 0.0).astype(BF16)
                doh = jnp.where(masks[h], do, 0.0)
                doh_bf = doh.astype(BF16)
                lrow = jnp.max(jnp.where(masks[h], lv, -3e38), axis=1, keepdims=True)
                drow = jnp.sum(doh * ov, axis=1, keepdims=True)
                pc = jnp.exp(_dot_nt(qh, kc) + b_ref[h, :, Q_BLOCK:] - lrow)
                dsc = pc * (_dot_nt(doh_bf, vc) - drow)
                ds_ref[h, :, Q_BLOCK:] += dsc
                dsc_bf = dsc.astype(BF16)
                dqh = _dot(dsc_bf, kc)
                dkc = dkc + _dot_tn(dsc_bf, qh)
                dvc = dvc + _dot_tn(pc.astype(BF16), doh_bf)
                if nb > 1:
                    bp = jnp.where(n == 0, NEG_INF, b_ref[h, :, :Q_BLOCK])
                    pp = jnp.exp(_dot_nt(qh, kp) + bp - lrow)
                    dsp = pp * (_dot_nt(doh_bf, vp) - drow)
                    ds_ref[h, :, :Q_BLOCK] += dsp
                    dsp_bf = dsp.astype(BF16)
                    dqh = dqh + _dot(dsp_bf, kp)
                    dkp = dkp + _dot_tn(dsp_bf, qh)
                    dvp = dvp + _dot_tn(pp.astype(BF16), doh_bf)
                dq = jnp.where(masks[h], dqh, dq)
            dq_ref[rows, :] = dq * 0.125
            dk_ref[rows, :] += dkc
            dv_ref[rows, :] += dvc
            if nb > 1:
                dk_ref[prow, :] += dkp
                dv_ref[prow, :] += dvp
            return carry

        if nb > 1:
            lax.fori_loop(0, nb, block, 0)
        else:
            block(0, 0)

    def in_spec(col0):
        return pl.BlockSpec((None, tr, LANE_BLOCK), lambda g, b, r: (b, 0, r * N_COLBLK + col0 + g))

    do_spec = pl.BlockSpec((None, tr, LANE_BLOCK), lambda g, b, r: (b, 0, r * (D_MODEL // LANE_BLOCK) + 1 + g))
    hd_spec = pl.BlockSpec((None, tr, LANE_BLOCK), lambda g, b, r: (b, 0, r * HEAD_BLOCKS + g))
    tbl_spec = pl.BlockSpec((HEADS_PER_BLOCK, Q_BLOCK, 2 * Q_BLOCK), lambda g, b, r: (g, 0, 0))
    out_sds = jax.ShapeDtypeStruct((batch, tr, dil * B_WIDTH), F32)
    dq, dk, dv, ds = pl.pallas_call(
        body, grid=(HEAD_BLOCKS, batch, dil),
        in_specs=[in_spec(Q_COL), in_spec(K_COL), in_spec(V_COL), do_spec, hd_spec, hd_spec, tbl_spec],
        out_specs=[hd_spec, hd_spec, hd_spec, tbl_spec],
        out_shape=[out_sds, out_sds, out_sds, jax.ShapeDtypeStruct((B_HEADS, Q_BLOCK, 2 * Q_BLOCK), F32)],
        compiler_params=_params(("parallel", "arbitrary", "arbitrary")), name=f"attn_bwd_d{dil}",
    )(proj3, proj3, proj3, dmix3, o3, l3, bias)
    return dq.reshape(m, B_WIDTH), dk.reshape(m, B_WIDTH), dv.reshape(m, B_WIDTH), ds


PAIR = 2 * HEAD_DIM
N_PAIR = B_HEADS // 2
N_CFG = len(DILATED)
BLOCKS_PER_CFG = SEQ // Q_BLOCK
QKV_SLABS = 3 * N_PAIR
FWD_BLOCKS_PER_TRIP = 8
BWD_BLOCKS_PER_TRIP = 4


def _proj_fwd(x, g1, w_in):
    m = x.shape[0]
    tm = ROW_TILE

    def body(x_ref, g_ref, w_ref, h_ref, uv_ref, qkv_ref):
        xv = x_ref[...]
        h = (xv * _rstd(xv) * g_ref[...]).astype(BF16)
        h_ref[...] = h
        acc = _dot(h, w_ref[...])
        uv_ref[...] = acc[:, :2 * A_WIDTH]
        for s in range(QKV_SLABS):
            qkv_ref[s] = acc[:, 2 * A_WIDTH + s * PAIR:2 * A_WIDTH + (s + 1) * PAIR]

    return pl.pallas_call(
        body, grid=(m // tm,),
        in_specs=[pl.BlockSpec((tm, D_MODEL), lambda i: (i, 0)), _vec_spec(),
                  pl.BlockSpec((D_MODEL, IN_COLS), lambda i: (0, 0))],
        out_specs=[pl.BlockSpec((tm, D_MODEL), lambda i: (i, 0)),
                   pl.BlockSpec((tm, 2 * A_WIDTH), lambda i: (i, 0)),
                   pl.BlockSpec((QKV_SLABS, tm, PAIR), lambda i: (0, i, 0))],
        out_shape=[jax.ShapeDtypeStruct((m, D_MODEL), BF16), jax.ShapeDtypeStruct((m, 2 * A_WIDTH), F32),
                   jax.ShapeDtypeStruct((QKV_SLABS, m, PAIR), F32)],
        compiler_params=_params(("parallel",)), name="proj_fwd",
    )(x, g1, w_in)


def _pair_masks():
    lane = lax.broadcasted_iota(jnp.int32, (1, PAIR), 1)
    return [lane < HEAD_DIM, lane >= HEAD_DIM]


def _block_rows(idx, dil):
    if dil == 1:
        n = idx
        cur = pl.ds(pl.multiple_of(n * Q_BLOCK, Q_BLOCK), Q_BLOCK)
        prev = pl.ds(pl.multiple_of(jnp.maximum(n - 1, 0) * Q_BLOCK, Q_BLOCK), Q_BLOCK)
        return n, cur, prev
    r = idx % dil
    n = idx // dil
    cur = pl.ds(r + (dil * Q_BLOCK) * n, Q_BLOCK, stride=dil)
    prev = pl.ds(r + (dil * Q_BLOCK) * jnp.maximum(n - 1, 0), Q_BLOCK, stride=dil)
    return n, cur, prev


def _attn_fwd(qkv, bias, batch):
    m = qkv.shape[1]
    comb_rows = 256

    def body(q_ref, k_ref, v_ref, b_ref, o_ref, l_ref, *scratch):
        oc_refs, lc_refs = scratch[:N_CFG], scratch[N_CFG:]
        masks = _pair_masks()
        for ci, (_, dil) in enumerate(DILATED):
            nb = SEQ // dil // Q_BLOCK

            def block(trip, carry, ci=ci, dil=dil, nb=nb):
                work = []
                for u in range(FWD_BLOCKS_PER_TRIP):
                    n, rows, prow = _block_rows(trip * FWD_BLOCKS_PER_TRIP + u, dil)
                    q = q_ref[rows, :] * 0.125
                    kc = k_ref[rows, :].astype(BF16)
                    vc = v_ref[rows, :]
                    kp = k_ref[prow, :].astype(BF16) if nb > 1 else None
                    vp = v_ref[prow, :] if nb > 1 else None
                    tiles = []
                    for h in range(2):
                        qh = jnp.where(masks[h], q, 0.0).astype(BF16)
                        sc = _dot_nt(qh, kc) + b_ref[ci, h, :, Q_BLOCK:]
                        sp = None
                        if nb > 1:
                            sp = _dot_nt(qh, kp) + jnp.where(n == 0, NEG_INF, b_ref[ci, h, :, :Q_BLOCK])
                        tiles.append((sc, sp))
                    work.append((rows, vc, vp, tiles))
                probs = []
                for _, _, _, tiles in work:
                    ps = []
                    for sc, sp in tiles:
                        mx = jnp.max(sc if sp is None else jnp.maximum(sc, sp), axis=1, keepdims=True)
                        pc = jnp.exp(sc - mx).astype(BF16)
                        pp = None if sp is None else jnp.exp(sp - mx).astype(BF16)
                        ps.append((mx, pc, pp))
                    probs.append(ps)
                for (rows, vc, vp, _), ps in zip(work, probs):
                    res = []
                    for h, (_, pc, pp) in enumerate(ps):
                        r = _dot(pc, jnp.where(masks[h], vc, 1.0).astype(BF16))
                        if pp is not None:
                            r = r + _dot(pp, jnp.where(masks[h], vp, 1.0).astype(BF16))
                        res.append(r)
                    num = jnp.where(masks[0], res[0], res[1])
                    den = pltpu.roll(jnp.where(masks[0], res[1], res[0]), HEAD_DIM, 1)
                    oc_refs[ci][rows, :] = num / den
                    lc_refs[ci][rows, :] = jnp.where(masks[0], ps[0][0], ps[1][0]) + jnp.log(den)
                return carry

            lax.fori_loop(0, BLOCKS_PER_CFG // FWD_BLOCKS_PER_TRIP, block, 0)

        def combine(i, carry):
            rr = pl.ds(pl.multiple_of(i * comb_rows, comb_rows), comb_rows)
            ls = [lc_refs[c][rr, :] for c in range(N_CFG)]
            mx = functools.reduce(jnp.maximum, ls)
            ws = [jnp.exp(l - mx) for l in ls]
            tot = functools.reduce(lambda a, b: a + b, ws)
            o = functools.reduce(lambda a, b: a + b, [ws[c] * oc_refs[c][rr, :] for c in range(N_CFG)]) / tot
            o_ref[rr, :] = o.astype(BF16)
            l_ref[rr, :] = mx + jnp.log(tot)
            return carry

        lax.fori_loop(0, SEQ // comb_rows, combine, 0)

    def slab(first):
        return pl.BlockSpec((None, SEQ, PAIR), lambda b, p: (first + p, b, 0))

    nat = pl.BlockSpec((SEQ, PAIR), lambda b, p: (b, p))
    return pl.pallas_call(
        body, grid=(batch, N_PAIR),
        in_specs=[slab(0), slab(N_PAIR), slab(2 * N_PAIR),
                  pl.BlockSpec((N_CFG, 2, Q_BLOCK, 2 * Q_BLOCK), lambda b, p: (0, p, 0, 0))],
        out_specs=[nat, nat],
        out_shape=[jax.ShapeDtypeStruct((m, B_WIDTH), BF16), jax.ShapeDtypeStruct((m, B_WIDTH), F32)],
        scratch_shapes=[pltpu.VMEM((SEQ, PAIR), F32)] * (2 * N_CFG),
        compiler_params=_params(("parallel", "parallel")), name="attn_fwd",
    )(qkv, qkv, qkv, bias)


def _attn_bwd(qkv, dmix, o, lse, bias, batch):
    m = qkv.shape[1]

    def body(q_ref, k_ref, v_ref, do_ref, o_ref, l_ref, b_ref, dqkv_ref, ds_ref, dq_acc, dk_acc, dv_acc, d_scr):
        @pl.when(pl.program_id(1) == 0)
        def _():
            ds_ref[...] = jnp.zeros_like(ds_ref)

        dq_acc[...] = jnp.zeros_like(dq_acc)
        dk_acc[...] = jnp.zeros_like(dk_acc)
        dv_acc[...] = jnp.zeros_like(dv_acc)
        masks = _pair_masks()
        ri = lax.broadcasted_iota(jnp.int32, (PAIR, PAIR), 0)
        cj = lax.broadcasted_iota(jnp.int32, (PAIR, PAIR), 1)
        same_head = ((ri < HEAD_DIM) == (cj < HEAD_DIM)).astype(F32)
        d_scr[...] = _dot(do_ref[...] * o_ref[...].astype(F32), same_head, lax.Precision.HIGHEST)

        for ci, (_, dil) in enumerate(DILATED):
            nb = SEQ // dil // Q_BLOCK

            def block(trip, carry, ci=ci, dil=dil, nb=nb):
                first = []
                for u in range(BWD_BLOCKS_PER_TRIP):
                    n, rows, prow = _block_rows(trip * BWD_BLOCKS_PER_TRIP + u, dil)
                    q = q_ref[rows, :] * 0.125
                    kc = k_ref[rows, :].astype(BF16)
                    vc = v_ref[rows, :].astype(BF16)
                    do = do_ref[rows, :]
                    lv = l_ref[rows, :]
                    dv_ = d_scr[rows, :]
                    kp = k_ref[prow, :].astype(BF16) if nb > 1 else None
                    vp = v_ref[prow, :].astype(BF16) if nb > 1 else None
                    heads = []
                    for h in range(2):
                        c0 = h * HEAD_DIM
                        qh = jnp.where(masks[h], q, 0.0).astype(BF16)
                        doh = jnp.where(masks[h], do, 0.0).astype(BF16)
                        sc = _dot_nt(qh, kc)
                        dpc = _dot_nt(doh, vc)
                        sp = _dot_nt(qh, kp) if nb > 1 else None
                        dpp = _dot_nt(doh, vp) if nb > 1 else None
                        heads.append((qh, doh, lv[:, c0:c0 + 1], dv_[:, c0:c0 + 1], sc, dpc, sp, dpp))
                    first.append((n, rows, prow, kc, kp, heads))
                second = []
                for n, rows, prow, kc, kp, heads in first:
                    out = []
                    for h, (qh, doh, lrow, drow, sc, dpc, sp, dpp) in enumerate(heads):
                        pc = jnp.exp(sc + b_ref[ci, h, :, Q_BLOCK:] - lrow)
                        dsc = pc * (dpc - drow)
                        ds_ref[ci, h, :, Q_BLOCK:] += dsc
                        pp_bf = dsp_bf = None
                        if nb > 1:
                            pp = jnp.exp(sp + jnp.where(n == 0, NEG_INF, b_ref[ci, h, :, :Q_BLOCK]) - lrow)
                            dsp = pp * (dpp - drow)
                            ds_ref[ci, h, :, :Q_BLOCK] += dsp
                            pp_bf, dsp_bf = pp.astype(BF16), dsp.astype(BF16)
                        out.append((qh, doh, pc.astype(BF16), dsc.astype(BF16), pp_bf, dsp_bf))
                    second.append((rows, prow, kc, kp, out))
                for rows, prow, kc, kp, out in second:
                    dq = jnp.zeros((Q_BLOCK, PAIR), F32)
                    dkc = jnp.zeros((Q_BLOCK, PAIR), F32)
                    dvc = jnp.zeros((Q_BLOCK, PAIR), F32)
                    dkp = jnp.zeros((Q_BLOCK, PAIR), F32)
                    dvp = jnp.zeros((Q_BLOCK, PAIR), F32)
                    for h, (qh, doh, pc_bf, dsc_bf, pp_bf, dsp_bf) in enumerate(out):
                        dqh = _dot(dsc_bf, kc)
                        dkc = dkc + _dot_tn(dsc_bf, qh)
                        dvc = dvc + _dot_tn(pc_bf, doh)
                        if nb > 1:
                            dqh = dqh + _dot(dsp_bf, kp)
                            dkp = dkp + _dot_tn(dsp_bf, qh)
                            dvp = dvp + _dot_tn(pp_bf, doh)
                        dq = jnp.where(masks[h], dqh, dq)
                    dq_acc[rows, :] += dq * 0.125
                    dk_acc[rows, :] += dkc
                    dv_acc[rows, :] += dvc
                    if nb > 1:
                        dk_acc[prow, :] += dkp
                        dv_acc[prow, :] += dvp
                return carry

            lax.fori_loop(0, BLOCKS_PER_CFG // BWD_BLOCKS_PER_TRIP, block, 0)

        dqkv_ref[0] = dq_acc[...].astype(BF16)
        dqkv_ref[1] = dk_acc[...].astype(BF16)
        dqkv_ref[2] = dv_acc[...].astype(BF16)

    def slab(first):
        return pl.BlockSpec((None, SEQ, PAIR), lambda p, b: (first + p, b, 0))

    nat = pl.BlockSpec((SEQ, PAIR), lambda p, b: (b, p))
    tbl = pl.BlockSpec((N_CFG, 2, Q_BLOCK, 2 * Q_BLOCK), lambda p, b: (0, p, 0, 0))
    acc = pltpu.VMEM((SEQ, PAIR), F32)
    return pl.pallas_call(
        body, grid=(N_PAIR, batch),
        in_specs=[slab(0), slab(N_PAIR), slab(2 * N_PAIR),
                  pl.BlockSpec((SEQ, PAIR), lambda p, b: (b, A_WIDTH // PAIR + p)), nat, nat, tbl],
        out_specs=[pl.BlockSpec((3, SEQ, PAIR), lambda p, b: (0, b, p)), tbl],
        out_shape=[jax.ShapeDtypeStruct((3, m, B_WIDTH), BF16),
                   jax.ShapeDtypeStruct((N_CFG, B_HEADS, Q_BLOCK, 2 * Q_BLOCK), F32)],
        scratch_shapes=[acc, acc, acc, acc],
        compiler_params=_params(("parallel", "arbitrary")), name="attn_bwd",
    )(qkv, qkv, qkv, dmix, o, lse, bias)


def _rel_bias_grad(ds, buckets_np):
    present = [sorted(set(int(v) for v in np.unique(buckets_np[c]) if v >= 0)) for c in range(N_CFG)]

    def body(bk_ref, ds_ref, o_ref, acc_ref):
        acc_ref[...] = jnp.zeros_like(acc_ref)
        for c in range(N_CFG):
            bk = bk_ref[c]
            for h in range(B_HEADS):
                dsv = ds_ref[c, h]
                for b in present[c]:
                    part = jnp.sum(jnp.where(bk == b, dsv, 0.0), axis=0, keepdims=True)
                    acc_ref[pl.ds(h * NUM_BUCKETS + b, 1), :] += part
        o_ref[...] = jnp.sum(acc_ref[...], axis=1, keepdims=True)

    vm = pl.BlockSpec(memory_space=pltpu.VMEM)
    return pl.pallas_call(
        body, in_specs=[vm, vm], out_specs=vm,
        out_shape=jax.ShapeDtypeStruct((B_HEADS * NUM_BUCKETS, 1), F32),
        scratch_shapes=[pltpu.VMEM((B_HEADS * NUM_BUCKETS, 2 * Q_BLOCK), F32)],
        compiler_params=_params(), name="rel_bias_grad",
    )(jnp.asarray(buckets_np), ds)


def _assemble_dproj(duv, dqkv):
    m = duv.shape[0]

    rows = 1024

    def body(duv_ref, dqkv_ref, o_ref):
        o_ref[:, :2 * A_WIDTH] = duv_ref[...].astype(BF16)
        for k in range(3):
            o_ref[:, 2 * A_WIDTH + k * B_WIDTH:2 * A_WIDTH + (k + 1) * B_WIDTH] = dqkv_ref[k]

    return pl.pallas_call(
        body, grid=(m // rows,),
        in_specs=[pl.BlockSpec((rows, 2 * A_WIDTH), lambda i: (i, 0)),
                  pl.BlockSpec((3, rows, B_WIDTH), lambda i: (0, i, 0))],
        out_specs=pl.BlockSpec((rows, IN_COLS), lambda i: (i, 0)),
        out_shape=jax.ShapeDtypeStruct((m, IN_COLS), BF16),
        compiler_params=_params(("parallel",)), name="assemble_dproj",
    )(duv, dqkv)


def _shift_down(x, k):
    row = lax.broadcasted_iota(jnp.int32, x.shape, 0)
    return jnp.where(row >= k, pltpu.roll(x, k, 0), 0.0)


def _shift_up(x, k):
    n = x.shape[0]
    row = lax.broadcasted_iota(jnp.int32, x.shape, 0)
    return jnp.where(row < n - k, pltpu.roll(x, n - k, 0), 0.0)


def _convgate_fwd(gate, up, conv_w, conv_b, batch):
    m = gate.shape[0]

    def body(g_ref, u_ref, w_ref, b_ref, a_ref):
        g = g_ref[...].astype(F32)
        w = w_ref[...]
        c = b_ref[...] + w[0:1] * _shift_down(g, 2) + w[1:2] * _shift_down(g, 1) + w[2:3] * g
        a_ref[...] = (_gelu(c) * u_ref[...].astype(F32)).astype(BF16)

    blk = pl.BlockSpec((SEQ, LANE_BLOCK), lambda b, j: (b, j))
    return pl.pallas_call(
        body, grid=(batch, D_FF // LANE_BLOCK),
        in_specs=[blk, blk, pl.BlockSpec((3, LANE_BLOCK), lambda b, j: (0, j)),
                  pl.BlockSpec((1, LANE_BLOCK), lambda b, j: (0, j))],
        out_specs=blk,
        out_shape=jax.ShapeDtypeStruct((m, D_FF), BF16),
        compiler_params=_params(("parallel", "parallel")), name="convgate_fwd",
    )(gate, up, conv_w, conv_b)


def _convgate_bwd(gate, up, dact, conv_w, conv_b, batch):
    m = gate.shape[0]

    def body(g_ref, u_ref, da_ref, w_ref, b_ref, dg_ref, du_ref, dw_ref, db_ref):
        @pl.when(pl.program_id(1) == 0)
        def _():
            dw_ref[...] = jnp.zeros_like(dw_ref)
            db_ref[...] = jnp.zeros_like(db_ref)

        g = g_ref[...].astype(F32)
        w = w_ref[...]
        g1 = _shift_down(g, 1)
        g2 = _shift_down(g, 2)
        c = b_ref[...] + w[0:1] * g2 + w[1:2] * g1 + w[2:3] * g
        gg, dgg = _gelu_and_grad(c)
        da = da_ref[...].astype(F32)
        du_ref[...] = (da * gg).astype(BF16)
        dc = da * u_ref[...].astype(F32) * dgg
        db_ref[...] += jnp.sum(dc, axis=0, keepdims=True)
        dw_ref[0:1, :] += jnp.sum(dc * g2, axis=0, keepdims=True)
        dw_ref[1:2, :] += jnp.sum(dc * g1, axis=0, keepdims=True)
        dw_ref[2:3, :] += jnp.sum(dc * g, axis=0, keepdims=True)
        dg_ref[...] = (w[2:3] * dc + w[1:2] * _shift_up(dc, 1) + w[0:1] * _shift_up(dc, 2)).astype(BF16)

    blk = pl.BlockSpec((SEQ, LANE_BLOCK), lambda j, b: (b, j))
    wspec = pl.BlockSpec((3, LANE_BLOCK), lambda j, b: (0, j))
    bspec = pl.BlockSpec((1, LANE_BLOCK), lambda j, b: (0, j))
    return pl.pallas_call(
        body, grid=(D_FF // LANE_BLOCK, batch),
        in_specs=[blk, blk, blk, wspec, bspec],
        out_specs=[blk, blk, wspec, bspec],
        out_shape=[jax.ShapeDtypeStruct((m, D_FF), BF16), jax.ShapeDtypeStruct((m, D_FF), BF16),
                   jax.ShapeDtypeStruct((3, D_FF), F32), jax.ShapeDtypeStruct((1, D_FF), F32)],
        compiler_params=_params(("parallel", "arbitrary")), name="convgate_bwd",
    )(gate, up, dact, conv_w, conv_b)


def _local_step(x, tgt, g1, g2, g3, g4, w_in, ln_g, ln_b, w_s, b_s, rel_bias, w_out, w_gate, w_up,
                conv_w, conv_b, w_down, batch):
    big = dict(tm=1024, out_dtype=F32)
    buckets = _bucket_tables()
    bias = _bias_tables(rel_bias, buckets)
    bz = jnp.repeat(b_s.T, HEAD_DIM, axis=1)
    w_st = jnp.swapaxes(w_s, 1, 2)

    h1, uv, qkv = _proj_fwd(x, g1, w_in)
    a = _gate_fwd(uv, ln_g, ln_b, w_s, bz)
    o_bf, lse = _attn_fwd(qkv, bias, batch)
    y1 = _mm(a, w_out[:A_WIDTH], dims="nn", tn=1024, tk=A_WIDTH, name="mm_out_a", **big)
    y1 = _mm(o_bf, w_out[A_WIDTH:], dims="nn", tn=1024, tk=B_WIDTH, name="mm_out_b", add=y1, **big)
    x1, h2 = _mid_fwd(x, y1, g2, g3)
    gate = _mm(h2, w_gate, dims="nn", tm=1024, tn=1408, tk=1024, out_dtype=BF16, name="mm_gate")
    up = _mm(h2, w_up, dims="nn", tm=1024, tn=1408, tk=1024, out_dtype=BF16, name="mm_up")
    act = _convgate_fwd(gate, up, conv_w, conv_b, batch)
    y2 = _mm(act, w_down, dims="nn", tn=1024, tk=1408, name="mm_down", **big)
    dx2, dy2, dg4, loss = _loss_head(x1, y2, tgt, g4)

    dact = _mm(dy2, w_down, dims="nt", tm=1024, tn=1408, tk=1024, out_dtype=BF16, name="mm_dact")
    dw_down = _mm(act, dy2, dims="tn", tm=1408, tn=1024, tk=1024, out_dtype=F32, name="mm_dw_down")
    dgate, dup, dconv_w, dconv_b = _convgate_bwd(gate, up, dact, conv_w, conv_b, batch)
    dh2 = _mm(dgate, w_gate, dims="nt", tn=1024, tk=1408, name="mm_dh2_g", **big)
    dh2 = _mm(dup, w_up, dims="nt", tn=1024, tk=1408, name="mm_dh2_u", add=dh2, **big)
    dw_gate = _mm(h2, dgate, dims="tn", tm=1024, tn=1408, tk=1024, out_dtype=F32, name="mm_dw_gate")
    dw_up = _mm(h2, dup, dims="tn", tm=1024, tn=1408, tk=1024, out_dtype=F32, name="mm_dw_up")
    dx1, dy1, dg2, dg3 = _mid_bwd(x1, y1, dh2, dx2, g2, g3)
    dmix = _mm(dy1, w_out, dims="nt", tn=1024, tk=1024, name="mm_dmix", **big)
    dw_out_a = _mm(a, dy1, dims="tn", tm=A_WIDTH, tn=1024, tk=1024, out_dtype=F32, name="mm_dw_out_a")
    dw_out_b = _mm(o_bf, dy1, dims="tn", tm=B_WIDTH, tn=1024, tk=1024, out_dtype=F32, name="mm_dw_out_b")
    duv, dln_g, dln_b, dw_s, dbz = _gate_bwd(uv, dmix, ln_g, ln_b, w_s, w_st, bz)
    dqkv, ds = _attn_bwd(qkv, dmix, o_bf, lse, bias, batch)
    drel = _rel_bias_grad(ds, buckets)
    dproj = _assemble_dproj(duv, dqkv)
    dh1 = _mm(dproj, w_in, dims="nt", tn=1024, tk=1408, name="mm_dh1", **big)
    dw_in = _mm(h1, dproj, dims="tn", tm=1024, tn=1408, tk=1024, out_dtype=F32, name="mm_dw_in")
    dx0, dg1 = _in_bwd(x, dh1, dx1, g1)

    grads = dict(
        norm_mix_pre=dg1, norm_mix_post=dg2, norm_ffn_pre=dg3, norm_ffn_post=dg4,
        w_in=dw_in, ln_v_gain=dln_g, ln_v_bias=dln_b, spatial_w=dw_s,
        spatial_b=dbz[:, ::HEAD_DIM].T,
        rel_bias=drel.reshape(B_HEADS, NUM_BUCKETS).T,
        w_out=jnp.concatenate([dw_out_a, dw_out_b], axis=0),
        w_gate=dw_gate, w_up=dw_up, conv_w=dconv_w, conv_b=dconv_b, w_down=dw_down,
    )
    return loss, dx0, grads


def _mesh_pos():
    x, y, c = lax.axis_index("x"), lax.axis_index("y"), lax.axis_index("c")
    chips = [(1 - x, y), (x, 1 - y), (1 - x, 1 - y)]
    return x, y, c, chips


ANY = pl.BlockSpec(memory_space=pl.ANY)


def _gather_weights(shards, conv_w_shard):
    nt = len(shards)

    def body(*refs):
        shard_refs = refs[:nt]
        cw_ref = refs[nt]
        out_refs = refs[nt + 1:2 * nt + 1]
        cw_out = refs[2 * nt + 1]
        send_sems, recv_sems = refs[2 * nt + 2:]
        x, y, c, chips = _mesh_pos()
        s = 2 * x + y
        sib = (x, y, 1 - c)

        def half(ref, chip, which, t):
            rows = shards[t].shape[0] // 2
            return ref.at[2 * chip[0] + chip[1], pl.ds(which * rows, rows), :]

        def rcopy(k, src, dst, to):
            return pltpu.make_async_remote_copy(src_ref=src, dst_ref=dst, send_sem=send_sems.at[k],
                                                recv_sem=recv_sems.at[k], device_id=to, device_id_type=MESH)

        sends = []
        for t in range(nt):
            rows = shards[t].shape[0] // 2
            for j, chip in enumerate(chips):
                sends.append(rcopy(7 * t + j, shard_refs[t].at[pl.ds(c * rows, rows), :],
                                   half(out_refs[t], (x, y), c, t), (*chip, c)))
        for j, chip in enumerate(chips):
            sends.append(rcopy(7 * nt + j, cw_ref, cw_out.at[s], (*chip, c)))
        for cp in sends:
            cp.start()
        fwd = []
        for t in range(nt):
            for j, chip in enumerate(chips):
                landed = half(out_refs[t], chip, c, t)
                rcopy(7 * t + j, landed, landed, (*chip, c)).wait_recv()
                f = rcopy(7 * t + 3 + j, landed, landed, sib)
                f.start()
                fwd.append(f)
        for j, chip in enumerate(chips):
            dst = cw_out.at[2 * chip[0] + chip[1]]
            rcopy(7 * nt + j, dst, dst, (*chip, c)).wait_recv()
        for t in range(nt):
            for j, chip in enumerate(chips):
                other = half(out_refs[t], chip, 1 - c, t)
                rcopy(7 * t + 3 + j, other, other, sib).wait_recv()
        for cp in sends + fwd:
            cp.wait_send()

    out_shape = [jax.ShapeDtypeStruct((N_SHARD,) + sh.shape, sh.dtype) for sh in shards]
    out_shape.append(jax.ShapeDtypeStruct((N_SHARD,) + conv_w_shard.shape, conv_w_shard.dtype))
    nsem = 7 * nt + 3
    return pl.pallas_call(
        body, in_specs=[ANY] * (nt + 1), out_specs=[ANY] * (nt + 1), out_shape=out_shape,
        scratch_shapes=[pltpu.SemaphoreType.DMA((nsem,)), pltpu.SemaphoreType.DMA((nsem,))],
        compiler_params=pltpu.CompilerParams(has_side_effects=True), name="gather_weights",
    )(*shards, conv_w_shard)


def _exchange_halves(grads):
    nt = len(grads)

    def body(*refs):
        g_refs = refs[:nt]
        out_refs = refs[nt:2 * nt]
        send_sems, recv_sems = refs[2 * nt:]
        x, y, c, _ = _mesh_pos()
        copies = []
        for t in range(nt):
            rows = grads[t].shape[1] // 2
            copies.append(pltpu.make_async_remote_copy(
                src_ref=g_refs[t].at[:, pl.ds((1 - c) * rows, rows), :], dst_ref=out_refs[t],
                send_sem=send_sems.at[t], recv_sem=recv_sems.at[t], device_id=(x, y, 1 - c), device_id_type=MESH))
        for cp in copies:
            cp.start()
        for cp in copies:
            cp.wait()

    out_shape = [jax.ShapeDtypeStruct((N_SHARD, g.shape[1] // 2, g.shape[2]), g.dtype) for g in grads]
    return pl.pallas_call(
        body, in_specs=[ANY] * nt, out_specs=[ANY] * nt, out_shape=out_shape,
        scratch_shapes=[pltpu.SemaphoreType.DMA((nt,)), pltpu.SemaphoreType.DMA((nt,))],
        compiler_params=pltpu.CompilerParams(has_side_effects=True), name="rs_sibling_exchange",
    )(*grads)


def _add_halves(g, recv, c_idx):
    _, rows2, cols = g.shape
    rows = rows2 // 2
    tr = rows // 2 if rows % 16 == 0 and rows >= 256 else rows
    nblk = rows // tr

    def body(c_ref, g_ref, r_ref, o_ref):
        o_ref[...] = (g_ref[...] + r_ref[...]).astype(BF16)

    return pl.pallas_call(
        body,
        grid_spec=pltpu.PrefetchScalarGridSpec(
            num_scalar_prefetch=1, grid=(N_SHARD, nblk),
            in_specs=[pl.BlockSpec((None, tr, cols), lambda s, i, c: (s, c[0] * nblk + i, 0)),
                      pl.BlockSpec((None, tr, cols), lambda s, i, c: (s, i, 0))],
            out_specs=pl.BlockSpec((None, tr, cols), lambda s, i, c: (s, i, 0))),
        out_shape=jax.ShapeDtypeStruct((N_SHARD, rows, cols), BF16),
        compiler_params=_params(("parallel", "parallel")), name="rs_add_halves",
    )(c_idx, g, recv)


def _exchange_chips(parts):
    nt = len(parts)

    def body(*refs):
        p_refs = refs[:nt]
        out_refs = refs[nt:2 * nt]
        send_sems, recv_sems = refs[2 * nt:]
        x, y, c, chips = _mesh_pos()
        copies = []
        for t in range(nt):
            for j, chip in enumerate(chips):
                copies.append(pltpu.make_async_remote_copy(
                    src_ref=p_refs[t].at[2 * chip[0] + chip[1]], dst_ref=out_refs[t].at[j],
                    send_sem=send_sems.at[3 * t + j], recv_sem=recv_sems.at[3 * t + j],
                    device_id=(*chip, c), device_id_type=MESH))
        for cp in copies:
            cp.start()
        for cp in copies:
            cp.wait()

    out_shape = [jax.ShapeDtypeStruct((3,) + p.shape[1:], p.dtype) for p in parts]
    return pl.pallas_call(
        body, in_specs=[ANY] * nt, out_specs=[ANY] * nt, out_shape=out_shape,
        scratch_shapes=[pltpu.SemaphoreType.DMA((3 * nt,)), pltpu.SemaphoreType.DMA((3 * nt,))],
        compiler_params=pltpu.CompilerParams(has_side_effects=True), name="rs_chip_exchange",
    )(*parts)


def _add_chips(part, recv, s_idx, c_idx):
    _, rows, cols = part.shape
    tr = rows // 2 if rows % 32 == 0 and rows >= 256 else rows
    nblk = rows // tr

    def body(idx_ref, p_ref, r_ref, o_ref):
        acc = p_ref[...].astype(F32)
        for j in range(3):
            acc = acc + r_ref[j].astype(F32)
        o_ref[...] = acc

    return pl.pallas_call(
        body,
        grid_spec=pltpu.PrefetchScalarGridSpec(
            num_scalar_prefetch=1, grid=(nblk,),
            in_specs=[pl.BlockSpec((None, tr, cols), lambda i, idx: (idx[0], i, 0)),
                      pl.BlockSpec((3, tr, cols), lambda i, idx: (0, i, 0))],
            out_specs=pl.BlockSpec((tr, cols), lambda i, idx: (idx[1] * nblk + i, 0))),
        out_shape=jax.ShapeDtypeStruct((2 * rows, cols), F32),
        compiler_params=_params(("parallel",)), name="rs_add_chips",
    )(jnp.concatenate([s_idx, c_idx]), part, recv)


def _share_halves(fulls):
    nt = len(fulls)

    def body(*refs):
        out_refs = refs[nt:2 * nt]
        send_sems, recv_sems = refs[2 * nt:]
        x, y, c, _ = _mesh_pos()
        copies = []
        for t in range(nt):
            rows = fulls[t].shape[0] // 2
            mine = out_refs[t].at[pl.ds(c * rows, rows), :]
            copies.append(pltpu.make_async_remote_copy(
                src_ref=mine, dst_ref=mine, send_sem=send_sems.at[t], recv_sem=recv_sems.at[t],
                device_id=(x, y, 1 - c), device_id_type=MESH))
        for cp in copies:
            cp.start()
        for t in range(nt):
            rows = fulls[t].shape[0] // 2
            theirs = out_refs[t].at[pl.ds((1 - c) * rows, rows), :]
            pltpu.make_async_remote_copy(
                src_ref=theirs, dst_ref=theirs, send_sem=send_sems.at[t], recv_sem=recv_sems.at[t],
                device_id=(x, y, 1 - c), device_id_type=MESH).wait_recv()
        for cp in copies:
            cp.wait_send()

    out_shape = [jax.ShapeDtypeStruct(f.shape, f.dtype) for f in fulls]
    return pl.pallas_call(
        body, in_specs=[ANY] * nt, out_specs=[ANY] * nt, out_shape=out_shape,
        input_output_aliases={t: t for t in range(nt)},
        scratch_shapes=[pltpu.SemaphoreType.DMA((nt,)), pltpu.SemaphoreType.DMA((nt,))],
        compiler_params=pltpu.CompilerParams(has_side_effects=True), name="rs_share_halves",
    )(*fulls)


def _allreduce_small(packed):
    rows = packed.shape[0]

    def body(p_ref, o_ref, sib_ref, chip_ref, send_sems, recv_sems):
        x, y, c, chips = _mesh_pos()
        first = pltpu.make_async_remote_copy(src_ref=p_ref, dst_ref=sib_ref, send_sem=send_sems.at[0],
                                             recv_sem=recv_sems.at[0], device_id=(x, y, 1 - c), device_id_type=MESH)
        first.start()
        first.wait()
        o_ref[...] = p_ref[...] + sib_ref[...]
        copies = [pltpu.make_async_remote_copy(src_ref=o_ref, dst_ref=chip_ref.at[j], send_sem=send_sems.at[1 + j],
                                               recv_sem=recv_sems.at[1 + j], device_id=(*chip, c), device_id_type=MESH)
                  for j, chip in enumerate(chips)]
        for cp in copies:
            cp.start()
        for cp in copies:
            cp.wait()
        o_ref[...] = (o_ref[...] + chip_ref[0]) + (chip_ref[1] + chip_ref[2])

    vm = pl.BlockSpec(memory_space=pltpu.VMEM)
    return pl.pallas_call(
        body, in_specs=[vm], out_specs=vm, out_shape=jax.ShapeDtypeStruct(packed.shape, F32),
        scratch_shapes=[pltpu.VMEM((rows, 128), F32), pltpu.VMEM((3, rows, 128), F32),
                        pltpu.SemaphoreType.DMA((4,)), pltpu.SemaphoreType.DMA((4,))],
        compiler_params=pltpu.CompilerParams(has_side_effects=True, vmem_limit_bytes=VMEM_LIMIT),
        name="allreduce_small",
    )(packed)


def _adamw(w, g, m, v, name):
    rows, cols = w.shape
    tr = rows
    if rows * cols > 256 * 1024:
        tr = next(cand for cand in (256, 176, 128) if rows % cand == 0)

    def body(w_ref, g_ref, m_ref, v_ref, go_ref, d_ref, nm_ref, nv_ref):
        gv = g_ref[...]
        go_ref[...] = gv
        nm = ADAM_B1 * m_ref[...] + (1.0 - ADAM_B1) * gv
        nv = ADAM_B2 * v_ref[...] + (1.0 - ADAM_B2) * (gv * gv)
        m_hat = nm / (1.0 - ADAM_B1 ** ADAM_STEP)
        v_hat = nv / (1.0 - ADAM_B2 ** ADAM_STEP)
        d_ref[...] = -ADAM_LR * (m_hat / (jnp.sqrt(v_hat) + ADAM_EPS) + ADAM_WD * w_ref[...])
        nm_ref[...] = nm
        nv_ref[...] = nv

    spec = pl.BlockSpec((tr, cols), lambda i: (i, 0))
    sds = jax.ShapeDtypeStruct((rows, cols), F32)
    return pl.pallas_call(
        body, grid=(rows // tr,), in_specs=[spec] * 4, out_specs=[spec] * 4, out_shape=[sds] * 4,
        compiler_params=_params(("parallel",)), name=name,
    )(w, g, m, v)


def _pack(arrays, rows):
    flat = jnp.concatenate([a.reshape(-1) for a in arrays])
    flat = jnp.pad(flat, (0, rows * 128 - flat.shape[0]))
    return flat.reshape(rows, 128)


def _unpack(packed, shapes):
    flat = packed.reshape(-1)
    out, off = [], 0
    for sh in shapes:
        n = int(np.prod(sh))
        out.append(flat[off:off + n].reshape(sh))
        off += n
    return out


SMALL = ["norm_mix_pre", "norm_mix_post", "norm_ffn_pre", "norm_ffn_post", "ln_v_gain", "ln_v_bias",
         "spatial_w", "spatial_b", "rel_bias", "conv_b"]
LARGE = ["w_in", "w_gate", "w_up", "w_down", "w_out"]
ORDER = ["norm_mix_pre", "norm_mix_post", "norm_ffn_pre", "norm_ffn_post", "w_in", "ln_v_gain", "ln_v_bias",
         "spatial_w", "spatial_b", "rel_bias", "w_out", "w_gate", "w_up", "conv_w", "conv_b", "w_down"]


def _col_shards(full):
    rows, cols4 = full.shape
    return full.reshape(rows, N_SHARD, cols4 // N_SHARD).transpose(1, 0, 2)


def _from_col_shards(g):
    n, rows, cols = g.shape
    return g.transpose(1, 0, 2).reshape(rows, n * cols)


def kernel(x, norm_mix_pre, norm_mix_post, norm_ffn_pre, norm_ffn_post, w_in, ln_v_gain, ln_v_bias, spatial_w, spatial_b, rel_bias, w_out, w_gate, w_up, conv_w, conv_b, w_down, loss_target, m_norm_mix_pre, m_norm_mix_post, m_norm_ffn_pre, m_norm_ffn_post, m_w_in, m_ln_v_gain, m_ln_v_bias, m_spatial_w, m_spatial_b, m_rel_bias, m_w_out, m_w_gate, m_w_up, m_conv_w, m_conv_b, m_w_down, v_norm_mix_pre, v_norm_mix_post, v_norm_ffn_pre, v_norm_ffn_post, v_w_in, v_ln_v_gain, v_ln_v_bias, v_spatial_w, v_spatial_b, v_rel_bias, v_w_out, v_w_gate, v_w_up, v_conv_w, v_conv_b, v_w_down):
    params = dict(norm_mix_pre=norm_mix_pre, norm_mix_post=norm_mix_post, norm_ffn_pre=norm_ffn_pre,
                  norm_ffn_post=norm_ffn_post, w_in=w_in, ln_v_gain=ln_v_gain, ln_v_bias=ln_v_bias,
                  spatial_w=spatial_w, spatial_b=spatial_b, rel_bias=rel_bias, w_out=w_out, w_gate=w_gate,
                  w_up=w_up, conv_w=conv_w, conv_b=conv_b, w_down=w_down)
    mom = dict(norm_mix_pre=m_norm_mix_pre, norm_mix_post=m_norm_mix_post, norm_ffn_pre=m_norm_ffn_pre,
               norm_ffn_post=m_norm_ffn_post, w_in=m_w_in, ln_v_gain=m_ln_v_gain, ln_v_bias=m_ln_v_bias,
               spatial_w=m_spatial_w, spatial_b=m_spatial_b, rel_bias=m_rel_bias, w_out=m_w_out, w_gate=m_w_gate,
               w_up=m_w_up, conv_w=m_conv_w, conv_b=m_conv_b, w_down=m_w_down)
    var = dict(norm_mix_pre=v_norm_mix_pre, norm_mix_post=v_norm_mix_post, norm_ffn_pre=v_norm_ffn_pre,
               norm_ffn_post=v_norm_ffn_post, w_in=v_w_in, ln_v_gain=v_ln_v_gain, ln_v_bias=v_ln_v_bias,
               spatial_w=v_spatial_w, spatial_b=v_spatial_b, rel_bias=v_rel_bias, w_out=v_w_out, w_gate=v_w_gate,
               w_up=v_w_up, conv_w=v_conv_w, conv_b=v_conv_b, w_down=v_w_down)

    batch = x.shape[0]
    xi, yi, ci = lax.axis_index("x"), lax.axis_index("y"), lax.axis_index("c")
    s_idx = (2 * xi + yi).astype(jnp.int32).reshape(1)
    c_idx = ci.astype(jnp.int32).reshape(1)

    shards = [params[n][0].astype(BF16) for n in LARGE]
    gathered = _gather_weights(shards, conv_w[0])
    g_in, g_gate, g_up, g_down, g_out, g_convw = [
        lax.dynamic_update_index_in_dim(g, own, s_idx[0], 0) for g, own in zip(gathered, shards + [conv_w[0]])]
    w_in_f = _from_col_shards(g_in)
    w_gate_f = _from_col_shards(g_gate)
    w_up_f = _from_col_shards(g_up)
    w_down_f = g_down.reshape(D_FF, D_MODEL)
    w_out_f = g_out.reshape(D_MODEL, D_MODEL)
    conv_w_f = _from_col_shards(g_convw)

    loss_part, dx0, grads = _local_step(
        x.reshape(batch * SEQ, D_MODEL), loss_target.reshape(batch * SEQ, D_MODEL),
        norm_mix_pre, norm_mix_post, norm_ffn_pre, norm_ffn_post, w_in_f,
        ln_v_gain.reshape(1, A_WIDTH), ln_v_bias.reshape(1, A_WIDTH), spatial_w[0], spatial_b[0], rel_bias,
        w_out_f, w_gate_f, w_up_f, conv_w_f, conv_b, w_down_f, batch)
    loss = lax.psum(loss_part[0, 0], ("x", "y", "c"))
    grad_x = dx0.reshape(batch, SEQ, D_MODEL)

    big = [_col_shards(grads["w_in"]), _col_shards(grads["w_gate"]), _col_shards(grads["w_up"]),
           grads["w_down"].reshape(N_SHARD, SHARD_FF, D_MODEL),
           grads["w_out"].reshape(N_SHARD, D_MODEL // N_SHARD, D_MODEL)]
    recv_a = _exchange_halves(big)
    parts = [_add_halves(g, r, c_idx) for g, r in zip(big, recv_a)]
    recv_b = _exchange_chips(parts)
    fulls = [_add_chips(p, r, s_idx, c_idx) for p, r in zip(parts, recv_b)]
    reduced = dict(zip(LARGE, _share_halves(fulls)))

    small_g = [grads[n].reshape(params[n].shape) for n in SMALL] + [grads["conv_w"]]
    n_small = sum(int(np.prod(g.shape)) for g in small_g)
    small_rows = -(-n_small // (8 * 128)) * 8
    summed = _unpack(_allreduce_small(_pack(small_g, small_rows)),
                     [params[n].shape for n in SMALL] + [(3, D_FF)])
    for n, g in zip(SMALL, summed[:-1]):
        reduced[n] = g
    reduced["conv_w"] = lax.dynamic_slice_in_dim(summed[-1], s_idx[0] * SHARD_FF, SHARD_FF, axis=1)[None]

    out_g, out_d, out_m, out_v = {}, {}, {}, {}
    for n in LARGE:
        shp = params[n].shape
        g, d, nm, nv = _adamw(params[n][0], reduced[n], mom[n][0], var[n][0], name=f"adamw_{n}")
        out_g[n], out_d[n], out_m[n], out_v[n] = g.reshape(shp), d.reshape(shp), nm.reshape(shp), nv.reshape(shp)
    small_names = SMALL + ["conv_w"]
    rows_s = -(-sum(int(np.prod(params[n].shape)) for n in small_names) // (8 * 128)) * 8
    _, d, nm, nv = _adamw(_pack([params[n] for n in small_names], rows_s),
                          _pack([reduced[n] for n in small_names], rows_s),
                          _pack([mom[n] for n in small_names], rows_s), _pack([var[n] for n in small_names], rows_s),
                          name="adamw_small")
    shapes = [params[n].shape for n in small_names]
    for n, dd, mm, vv in zip(small_names, _unpack(d, shapes), _unpack(nm, shapes), _unpack(nv, shapes)):
        out_g[n], out_d[n], out_m[n], out_v[n] = reduced[n], dd, mm, vv

    return (loss, grad_x, *[out_g[n] for n in ORDER], *[out_d[n] for n in ORDER],
            *[out_m[n] for n in ORDER], *[out_v[n] for n in ORDER])
```

```python
import functools
import math

import numpy as np
import jax
import jax.numpy as jnp
from jax import lax
from jax.experimental import pallas as pl
from jax.experimental.pallas import tpu as pltpu

F32 = jnp.float32
BF16 = jnp.bfloat16
MESH = pl.DeviceIdType.MESH

D_MODEL = 1024
SEQ = 2048
HEAD_DIM = 64
A_GROUPS = 4
A_WIDTH = 256
B_HEADS = 12
B_WIDTH = 768
CHUNK = 128
DILATED = ((128, 1), (512, 4), (2048, 16))
NUM_BUCKETS = 32
MAX_DISTANCE = 2048
D_FF = 2816
IN_COLS = 2816
NORM_EPS = 1e-6
NEG_INF = -1e30
N_SHARD = 4
SHARD_FF = D_FF // N_SHARD
LANE_BLOCK = 256
VMEM_LIMIT = 56 * 1024 * 1024

ADAM_LR = 0.001
ADAM_B1 = 0.9
ADAM_B2 = 0.999
ADAM_EPS = 1e-08
ADAM_WD = 0.01
ADAM_STEP = 10

GELU_C = math.sqrt(2.0 / math.pi)
GELU_A = 0.044715


def _params(sem=None):
    return pltpu.CompilerParams(dimension_semantics=sem, vmem_limit_bytes=VMEM_LIMIT)


def _dot(a, b, precision=None):
    return jnp.dot(a, b, preferred_element_type=F32, precision=precision)


def _dot_nt(a, b):
    return lax.dot_general(a, b, (((1,), (1,)), ((), ())), preferred_element_type=F32)


def _dot_tn(a, b):
    return lax.dot_general(a, b, (((0,), (0,)), ((), ())), preferred_element_type=F32)


def _gelu(x):
    t = jnp.tanh(GELU_C * (x + GELU_A * (x * x * x)))
    return 0.5 * x * (1.0 + t)


def _gelu_and_grad(x):
    x2 = x * x
    t = jnp.tanh(GELU_C * (x + GELU_A * (x2 * x)))
    g = 0.5 * x * (1.0 + t)
    dg = 0.5 * (1.0 + t) + 0.5 * x * (1.0 - t * t) * (GELU_C * (1.0 + 3.0 * GELU_A * x2))
    return g, dg


def _mm(a, b, *, dims, tm, tn, tk, out_dtype, name, add=None):
    if dims == "nn":
        m, k = a.shape
        n = b.shape[1]
        a_spec = pl.BlockSpec((tm, tk), lambda i, j, kk: (i, kk))
        b_spec = pl.BlockSpec((tk, tn), lambda i, j, kk: (kk, j))
        dot = _dot
    elif dims == "nt":
        m, k = a.shape
        n = b.shape[0]
        a_spec = pl.BlockSpec((tm, tk), lambda i, j, kk: (i, kk))
        b_spec = pl.BlockSpec((tn, tk), lambda i, j, kk: (j, kk))
        dot = _dot_nt
    else:
        k, m = a.shape
        n = b.shape[1]
        a_spec = pl.BlockSpec((tk, tm), lambda i, j, kk: (kk, i))
        b_spec = pl.BlockSpec((tk, tn), lambda i, j, kk: (kk, j))
        dot = _dot_tn
    assert m % tm == 0 and n % tn == 0 and k % tk == 0, (name, m, n, k)
    nk = k // tk
    has_add = add is not None

    def body(*refs):
        if has_add:
            a_ref, b_ref, add_ref, o_ref, acc_ref = refs
        else:
            a_ref, b_ref, o_ref, acc_ref = refs
        kk = pl.program_id(2)

        @pl.when(kk == 0)
        def _():
            if has_add:
                acc_ref[...] = add_ref[...].astype(F32)
            else:
                acc_ref[...] = jnp.zeros_like(acc_ref)

        acc_ref[...] += dot(a_ref[...].astype(BF16), b_ref[...].astype(BF16))

        @pl.when(kk == nk - 1)
        def _():
            o_ref[...] = acc_ref[...].astype(out_dtype)

    in_specs = [a_spec, b_spec]
    args = [a, b]
    if has_add:
        in_specs.append(pl.BlockSpec((tm, tn), lambda i, j, kk: (i, j)))
        args.append(add)
    return pl.pallas_call(
        body,
        grid=(m // tm, n // tn, nk),
        in_specs=in_specs,
        out_specs=pl.BlockSpec((tm, tn), lambda i, j, kk: (i, j)),
        out_shape=jax.ShapeDtypeStruct((m, n), out_dtype),
        scratch_shapes=[pltpu.VMEM((tm, tn), F32)],
        compiler_params=_params(("parallel", "parallel", "arbitrary")),
        name=name,
    )(*args)


ROW_TILE = 512


def _row_spec(width=D_MODEL):
    return pl.BlockSpec((ROW_TILE, width), lambda i: (i, 0))


def _vec_spec(width=D_MODEL):
    return pl.BlockSpec((1, width), lambda i: (0, 0))


def _rstd(v):
    return lax.rsqrt(jnp.mean(v * v, axis=-1, keepdims=True) + NORM_EPS)


def _rms_fwd(x, g):
    m = x.shape[0]

    def body(x_ref, g_ref, h_ref):
        xv = x_ref[...]
        h_ref[...] = (xv * _rstd(xv) * g_ref[...]).astype(BF16)

    return pl.pallas_call(
        body, grid=(m // ROW_TILE,),
        in_specs=[_row_spec(), _vec_spec()],
        out_specs=_row_spec(),
        out_shape=jax.ShapeDtypeStruct((m, D_MODEL), BF16),
        compiler_params=_params(("parallel",)), name="rms_fwd",
    )(x, g)


def _mid_fwd(x0, y1, g2, g3):
    m = x0.shape[0]

    def body(x0_ref, y1_ref, g2_ref, g3_ref, x1_ref, h2_ref):
        y1v = y1_ref[...]
        x1 = x0_ref[...] + y1v * _rstd(y1v) * g2_ref[...]
        x1_ref[...] = x1
        h2_ref[...] = (x1 * _rstd(x1) * g3_ref[...]).astype(BF16)

    return pl.pallas_call(
        body, grid=(m // ROW_TILE,),
        in_specs=[_row_spec(), _row_spec(), _vec_spec(), _vec_spec()],
        out_specs=[_row_spec(), _row_spec()],
        out_shape=[jax.ShapeDtypeStruct((m, D_MODEL), F32), jax.ShapeDtypeStruct((m, D_MODEL), BF16)],
        compiler_params=_params(("parallel",)), name="mid_fwd",
    )(x0, y1, g2, g3)


def _rms_bwd_rows(dout, v, g):
    r = _rstd(v)
    n = v * r
    dn = dout * g
    dv = r * (dn - n * jnp.mean(dn * n, axis=-1, keepdims=True))
    dg = jnp.sum(dout * n, axis=0, keepdims=True)
    return dv, dg


def _loss_head(x1, y2, tgt, g4):
    m = x1.shape[0]

    def body(x1_ref, y2_ref, t_ref, g4_ref, dx2_ref, dy2_ref, dg4_ref, loss_ref):
        i = pl.program_id(0)

        @pl.when(i == 0)
        def _():
            dg4_ref[...] = jnp.zeros_like(dg4_ref)
            loss_ref[...] = jnp.zeros_like(loss_ref)

        y2v = y2_ref[...]
        g4 = g4_ref[...]
        x2 = x1_ref[...] + y2v * _rstd(y2v) * g4
        err = x2 - t_ref[...]
        loss_ref[...] += 0.5 * jnp.sum(jnp.mean(err * err, axis=-1, keepdims=True), axis=0, keepdims=True)
        dx2 = err * (1.0 / D_MODEL)
        dx2_ref[...] = dx2
        dy2, dg4 = _rms_bwd_rows(dx2, y2v, g4)
        dy2_ref[...] = dy2.astype(BF16)
        dg4_ref[...] += dg4

    return pl.pallas_call(
        body, grid=(m // ROW_TILE,),
        in_specs=[_row_spec(), _row_spec(), _row_spec(), _vec_spec()],
        out_specs=[_row_spec(), _row_spec(), _vec_spec(), pl.BlockSpec((1, 1), lambda i: (0, 0))],
        out_shape=[jax.ShapeDtypeStruct((m, D_MODEL), F32), jax.ShapeDtypeStruct((m, D_MODEL), BF16),
                   jax.ShapeDtypeStruct((1, D_MODEL), F32), jax.ShapeDtypeStruct((1, 1), F32)],
        compiler_params=_params(("arbitrary",)), name="loss_head",
    )(x1, y2, tgt, g4)


def _mid_bwd(x1, y1, dh2, dx2, g2, g3):
    m = x1.shape[0]

    def body(x1_ref, y1_ref, dh2_ref, dx2_ref, g2_ref, g3_ref, dx1_ref, dy1_ref, dg2_ref, dg3_ref):
        i = pl.program_id(0)

        @pl.when(i == 0)
        def _():
            dg2_ref[...] = jnp.zeros_like(dg2_ref)
            dg3_ref[...] = jnp.zeros_like(dg3_ref)

        d3, dg3 = _rms_bwd_rows(dh2_ref[...], x1_ref[...], g3_ref[...])
        dx1 = dx2_ref[...] + d3
        dx1_ref[...] = dx1
        dy1, dg2 = _rms_bwd_rows(dx1, y1_ref[...], g2_ref[...])
        dy1_ref[...] = dy1.astype(BF16)
        dg2_ref[...] += dg2
        dg3_ref[...] += dg3

    return pl.pallas_call(
        body, grid=(m // ROW_TILE,),
        in_specs=[_row_spec(), _row_spec(), _row_spec(), _row_spec(), _vec_spec(), _vec_spec()],
        out_specs=[_row_spec(), _row_spec(), _vec_spec(), _vec_spec()],
        out_shape=[jax.ShapeDtypeStruct((m, D_MODEL), F32), jax.ShapeDtypeStruct((m, D_MODEL), BF16),
                   jax.ShapeDtypeStruct((1, D_MODEL), F32), jax.ShapeDtypeStruct((1, D_MODEL), F32)],
        compiler_params=_params(("arbitrary",)), name="mid_bwd",
    )(x1, y1, dh2, dx2, g2, g3)


def _in_bwd(x0, dh1, dx1, g1):
    m = x0.shape[0]

    def body(x0_ref, dh1_ref, dx1_ref, g1_ref, dx0_ref, dg1_ref):
        i = pl.program_id(0)

        @pl.when(i == 0)
        def _():
            dg1_ref[...] = jnp.zeros_like(dg1_ref)

        d1, dg1 = _rms_bwd_rows(dh1_ref[...], x0_ref[...], g1_ref[...])
        dx0_ref[...] = dx1_ref[...] + d1
        dg1_ref[...] += dg1

    return pl.pallas_call(
        body, grid=(m // ROW_TILE,),
        in_specs=[_row_spec(), _row_spec(), _row_spec(), _vec_spec()],
        out_specs=[_row_spec(), _vec_spec()],
        out_shape=[jax.ShapeDtypeStruct((m, D_MODEL), F32), jax.ShapeDtypeStruct((1, D_MODEL), F32)],
        compiler_params=_params(("arbitrary",)), name="in_bwd",
    )(x0, dh1, dx1, g1)


GATE_ROWS = 512


def _group_mean_matrix():
    p = np.zeros((A_WIDTH, A_WIDTH), np.float32)
    for g in range(A_GROUPS):
        p[g * HEAD_DIM:(g + 1) * HEAD_DIM, g * HEAD_DIM:(g + 1) * HEAD_DIM] = 1.0 / HEAD_DIM
    return jnp.asarray(p)


def _group_masks(width=A_WIDTH):
    lane = lax.broadcasted_iota(jnp.int32, (1, width), 1)
    return [(lane >= g * HEAD_DIM) & (lane < (g + 1) * HEAD_DIM) for g in range(width // HEAD_DIM)]


def _layernorm_groups(vg, pavg):
    hi = lax.Precision.HIGHEST
    mu = _dot(vg, pavg, hi)
    xc = vg - mu
    var = _dot(xc * xc, pavg, hi)
    rstd = lax.rsqrt(var + NORM_EPS)
    return xc * rstd, rstd


def _spatial_mix(w_bf, vn_chunk_bf, masks, bz):
    z = bz
    for g in range(A_GROUPS):
        z = z + jnp.where(masks[g], _dot(w_bf[g], vn_chunk_bf), 0.0)
    return z


def _gate_fwd(proj, ln_g, ln_b, w_s, bz):
    m = proj.shape[0]
    pavg = _group_mean_matrix()

    def body(u_ref, v_ref, lg_ref, lb_ref, w_ref, bz_ref, p_ref, a_ref):
        masks = _group_masks()
        row = lax.broadcasted_iota(jnp.int32, (CHUNK, CHUNK), 0)
        col = lax.broadcasted_iota(jnp.int32, (CHUNK, CHUNK), 1)
        w_bf = [jnp.where(row >= col, w_ref[g], 0.0).astype(BF16) for g in range(A_GROUPS)]
        ug = _gelu(u_ref[...])
        vhat, _ = _layernorm_groups(_gelu(v_ref[...]), p_ref[...])
        vn = vhat * lg_ref[...] + lb_ref[...]
        bz = bz_ref[...]
        for c in range(GATE_ROWS // CHUNK):
            sl = slice(c * CHUNK, (c + 1) * CHUNK)
            z = _spatial_mix(w_bf, vn[sl].astype(BF16), masks, bz)
            a_ref[sl, :] = (ug[sl] * z).astype(BF16)

    full = lambda shape: pl.BlockSpec(shape, lambda i: tuple(0 for _ in shape))
    return pl.pallas_call(
        body, grid=(m // GATE_ROWS,),
        in_specs=[pl.BlockSpec((GATE_ROWS, A_WIDTH), lambda i: (i, 0)),
                  pl.BlockSpec((GATE_ROWS, A_WIDTH), lambda i: (i, 1)),
                  full((1, A_WIDTH)), full((1, A_WIDTH)), full((A_GROUPS, CHUNK, CHUNK)),
                  full((CHUNK, A_WIDTH)), full((A_WIDTH, A_WIDTH))],
        out_specs=pl.BlockSpec((GATE_ROWS, A_WIDTH), lambda i: (i, 0)),
        out_shape=jax.ShapeDtypeStruct((m, A_WIDTH), BF16),
        compiler_params=_params(("parallel",)), name="gate_fwd",
    )(proj, proj, ln_g, ln_b, w_s, bz, pavg)


def _gate_bwd(proj, dmix, ln_g, ln_b, w_s, w_st, bz):
    m = proj.shape[0]
    pavg = _group_mean_matrix()
    nsteps = m // GATE_ROWS

    def body(u_ref, v_ref, da_ref, lg_ref, lb_ref, w_ref, wt_ref, bz_ref, p_ref,
             duv_ref, dlg_ref, dlb_ref, dw_ref, dbz_ref):
        i = pl.program_id(0)

        @pl.when(i == 0)
        def _():
            dlg_ref[...] = jnp.zeros_like(dlg_ref)
            dlb_ref[...] = jnp.zeros_like(dlb_ref)
            dw_ref[...] = jnp.zeros_like(dw_ref)
            dbz_ref[...] = jnp.zeros_like(dbz_ref)

        hi = lax.Precision.HIGHEST
        masks = _group_masks()
        row = lax.broadcasted_iota(jnp.int32, (CHUNK, CHUNK), 0)
        col = lax.broadcasted_iota(jnp.int32, (CHUNK, CHUNK), 1)
        tril = row >= col
        w_bf = [jnp.where(tril, w_ref[g], 0.0).astype(BF16) for g in range(A_GROUPS)]
        wt_bf = [jnp.where(col >= row, wt_ref[g], 0.0).astype(BF16) for g in range(A_GROUPS)]
        pavg_v = p_ref[...]
        lg = lg_ref[...]
        ug, dug = _gelu_and_grad(u_ref[...])
        vg, dvg_dx = _gelu_and_grad(v_ref[...])
        vhat, rstd = _layernorm_groups(vg, pavg_v)
        vn = vhat * lg + lb_ref[...]
        da = da_ref[...]
        bz = bz_ref[...]
        for c in range(GATE_ROWS // CHUNK):
            sl = slice(c * CHUNK, (c + 1) * CHUNK)
            vn_bf = vn[sl].astype(BF16)
            z = _spatial_mix(w_bf, vn_bf, masks, bz)
            dz = da[sl] * ug[sl]
            duv_ref[sl, 0:A_WIDTH] = da[sl] * z * dug[sl]
            dbz_ref[...] += dz
            dz_bf = dz.astype(BF16)
            dvn = jnp.zeros((CHUNK, A_WIDTH), F32)
            for g in range(A_GROUPS):
                dz_g = jnp.where(masks[g], dz, 0.0).astype(BF16)
                dw_ref[g] += jnp.where(tril, _dot_nt(dz_g, vn_bf), 0.0)
                dvn = dvn + jnp.where(masks[g], _dot(wt_bf[g], dz_bf), 0.0)
            vh = vhat[sl]
            dlb_ref[...] += jnp.sum(dvn, axis=0, keepdims=True)
            dlg_ref[...] += jnp.sum(dvn * vh, axis=0, keepdims=True)
            dvh = dvn * lg
            m1 = _dot(dvh, pavg_v, hi)
            m2 = _dot(dvh * vh, pavg_v, hi)
            duv_ref[sl, A_WIDTH:2 * A_WIDTH] = rstd[sl] * (dvh - m1 - vh * m2) * dvg_dx[sl]

        @pl.when(i == nsteps - 1)
        def _():
            dbz_ref[...] = _dot(dbz_ref[...], pavg_v * float(HEAD_DIM), hi)

    full = lambda shape: pl.BlockSpec(shape, lambda i: tuple(0 for _ in shape))
    return pl.pallas_call(
        body, grid=(nsteps,),
        in_specs=[pl.BlockSpec((GATE_ROWS, A_WIDTH), lambda i: (i, 0)),
                  pl.BlockSpec((GATE_ROWS, A_WIDTH), lambda i: (i, 1)),
                  pl.BlockSpec((GATE_ROWS, A_WIDTH), lambda i: (i, 0)),
                  full((1, A_WIDTH)), full((1, A_WIDTH)), full((A_GROUPS, CHUNK, CHUNK)),
                  full((A_GROUPS, CHUNK, CHUNK)), full((CHUNK, A_WIDTH)), full((A_WIDTH, A_WIDTH))],
        out_specs=[pl.BlockSpec((GATE_ROWS, 2 * A_WIDTH), lambda i: (i, 0)),
                   full((1, A_WIDTH)), full((1, A_WIDTH)), full((A_GROUPS, CHUNK, CHUNK)),
                   full((CHUNK, A_WIDTH))],
        out_shape=[jax.ShapeDtypeStruct((m, 2 * A_WIDTH), F32),
                   jax.ShapeDtypeStruct((1, A_WIDTH), F32), jax.ShapeDtypeStruct((1, A_WIDTH), F32),
                   jax.ShapeDtypeStruct((A_GROUPS, CHUNK, CHUNK), F32),
                   jax.ShapeDtypeStruct((CHUNK, A_WIDTH), F32)],
        compiler_params=_params(("arbitrary",)), name="gate_bwd",
    )(proj, proj, dmix, ln_g, ln_b, w_s, w_st, bz, pavg)


Q_BLOCK = 128
Q_COL, K_COL, V_COL = 2, 5, 8
N_COLBLK = IN_COLS // LANE_BLOCK
HEAD_BLOCKS = B_WIDTH // LANE_BLOCK
HEADS_PER_BLOCK = LANE_BLOCK // HEAD_DIM


def _t5_bucket_np(dist, dtype):
    max_exact = NUM_BUCKETS // 2
    d = np.maximum(dist, 1).astype(dtype)
    large = max_exact + (np.log(d / dtype(max_exact)) / dtype(math.log(MAX_DISTANCE / max_exact))
                         * dtype(NUM_BUCKETS - max_exact))
    large = np.minimum(large.astype(np.int32), NUM_BUCKETS - 1)
    return np.where(dist < max_exact, dist, large)


def _bucket_tables():
    i = np.arange(Q_BLOCK)[:, None]
    j = np.arange(Q_BLOCK)[None, :]
    tables = []
    for _, dil in DILATED:
        rel_prev = Q_BLOCK + i - j
        rel_cur = i - j
        rel = np.concatenate([rel_prev, rel_cur], axis=1)
        valid = np.concatenate([rel_prev <= Q_BLOCK, rel_cur >= 0], axis=1)
        dist = np.maximum(rel, 0) * dil
        b32 = _t5_bucket_np(dist, np.float32)
        b64 = _t5_bucket_np(dist, np.float64)
        assert np.array_equal(b32, b64)
        tables.append(np.where(valid, b32, -1).astype(np.int32))
    return np.stack(tables)


def _bias_tables(rel_bias, buckets_np):
    present = [sorted(set(int(v) for v in np.unique(buckets_np[c]) if v >= 0)) for c in range(len(DILATED))]

    def body(rb_ref, bk_ref, o_ref):
        for c in range(len(DILATED)):
            bk = bk_ref[c]
            for h in range(B_HEADS):
                acc = jnp.full((Q_BLOCK, 2 * Q_BLOCK), NEG_INF, F32)
                for b in present[c]:
                    acc = jnp.where(bk == b, rb_ref[b, h], acc)
                o_ref[c, h] = acc

    return pl.pallas_call(
        body,
        in_specs=[pl.BlockSpec(memory_space=pltpu.SMEM), pl.BlockSpec(memory_space=pltpu.VMEM)],
        out_specs=pl.BlockSpec(memory_space=pltpu.VMEM),
        out_shape=jax.ShapeDtypeStruct((len(DILATED), B_HEADS, Q_BLOCK, 2 * Q_BLOCK), F32),
        compiler_params=_params(), name="bias_tables",
    )(rel_bias, jnp.asarray(buckets_np))


def _head_masks():
    lane = lax.broadcasted_iota(jnp.int32, (1, LANE_BLOCK), 1)
    return [(lane >= h * HEAD_DIM) & (lane < (h + 1) * HEAD_DIM) for h in range(HEADS_PER_BLOCK)]


def _attn_fwd(proj, bias, dil, batch):
    m = proj.shape[0]
    tr = SEQ // dil
    nb = tr // Q_BLOCK
    proj3 = proj.reshape(batch, tr, dil * IN_COLS)

    def body(q_ref, k_ref, v_ref, b_ref, o_ref, l_ref):
        masks = _head_masks()

        def block(n, carry):
            r0 = pl.multiple_of(n * Q_BLOCK, Q_BLOCK)
            rows = pl.ds(r0, Q_BLOCK)
            q = q_ref[rows, :] * 0.125
            kc = k_ref[rows, :].astype(BF16)
            vc = v_ref[rows, :].astype(BF16)
            if nb > 1:
                p0 = pl.multiple_of(jnp.maximum(n - 1, 0) * Q_BLOCK, Q_BLOCK)
                kp = k_ref[pl.ds(p0, Q_BLOCK), :].astype(BF16)
                vp = v_ref[pl.ds(p0, Q_BLOCK), :].astype(BF16)
            o_acc = jnp.zeros((Q_BLOCK, LANE_BLOCK), F32)
            l_acc = jnp.zeros((Q_BLOCK, LANE_BLOCK), F32)
            for h in range(HEADS_PER_BLOCK):
                qh = jnp.where(masks[h], q, 0.0).astype(BF16)
                sc = _dot_nt(qh, kc) + b_ref[h, :, Q_BLOCK:]
                mx = jnp.max(sc, axis=1, keepdims=True)
                if nb > 1:
                    sp = _dot_nt(qh, kp) + jnp.where(n == 0, NEG_INF, b_ref[h, :, :Q_BLOCK])
                    mx = jnp.maximum(mx, jnp.max(sp, axis=1, keepdims=True))
                pc = jnp.exp(sc - mx)
                den = jnp.sum(pc, axis=1, keepdims=True)
                oh = _dot(pc.astype(BF16), vc)
                if nb > 1:
                    pp = jnp.exp(sp - mx)
                    den = den + jnp.sum(pp, axis=1, keepdims=True)
                    oh = oh + _dot(pp.astype(BF16), vp)
                o_acc = jnp.where(masks[h], oh / den, o_acc)
                l_acc = jnp.where(masks[h], mx + jnp.log(den), l_acc)
            o_ref[rows, :] = o_acc
            l_ref[rows, :] = l_acc
            return carry

        if nb > 1:
            lax.fori_loop(0, nb, block, 0)
        else:
            block(0, 0)

    def in_spec(col0):
        return pl.BlockSpec((None, tr, LANE_BLOCK), lambda b, r, g: (b, 0, r * N_COLBLK + col0 + g))

    out_spec = pl.BlockSpec((None, tr, LANE_BLOCK), lambda b, r, g: (b, 0, r * HEAD_BLOCKS + g))
    out_sds = jax.ShapeDtypeStruct((batch, tr, dil * B_WIDTH), F32)
    o, lse = pl.pallas_call(
        body, grid=(batch, dil, HEAD_BLOCKS),
        in_specs=[in_spec(Q_COL), in_spec(K_COL), in_spec(V_COL),
                  pl.BlockSpec((HEADS_PER_BLOCK, Q_BLOCK, 2 * Q_BLOCK), lambda b, r, g: (g, 0, 0))],
        out_specs=[out_spec, out_spec],
        out_shape=[out_sds, out_sds],
        compiler_params=_params(("parallel", "parallel", "parallel")), name=f"attn_fwd_d{dil}",
    )(proj3, proj3, proj3, bias)
    return o.reshape(m, B_WIDTH), lse.reshape(m, B_WIDTH)


def _attn_combine(outs, lses):
    m = outs[0].shape[0]
    nc = len(outs)

    def body(*refs):
        o_refs, l_refs = refs[:nc], refs[nc:2 * nc]
        of_ref, ob_ref, lt_ref = refs[2 * nc:]
        ls = [r[...] for r in l_refs]
        mx = functools.reduce(jnp.maximum, ls)
        ws = [jnp.exp(l - mx) for l in ls]
        tot = functools.reduce(lambda a, b: a + b, ws)
        inv = 1.0 / tot
        o = functools.reduce(lambda a, b: a + b, [w * inv * r[...] for w, r in zip(ws, o_refs)])
        of_ref[...] = o
        ob_ref[...] = o.astype(BF16)
        lt_ref[...] = mx + jnp.log(tot)

    spec = pl.BlockSpec((ROW_TILE, B_WIDTH), lambda i: (i, 0))
    return pl.pallas_call(
        body, grid=(m // ROW_TILE,),
        in_specs=[spec] * (2 * nc), out_specs=[spec, spec, spec],
        out_shape=[jax.ShapeDtypeStruct((m, B_WIDTH), F32), jax.ShapeDtypeStruct((m, B_WIDTH), BF16),
                   jax.ShapeDtypeStruct((m, B_WIDTH), F32)],
        compiler_params=_params(("parallel",)), name="attn_combine",
    )(*outs, *lses)


def _attn_bwd(proj, dmix, o, lse, bias, dil, batch):
    m = proj.shape[0]
    tr = SEQ // dil
    nb = tr // Q_BLOCK
    proj3 = proj.reshape(batch, tr, dil * IN_COLS)
    dmix3 = dmix.reshape(batch, tr, dil * D_MODEL)
    o3 = o.reshape(batch, tr, dil * B_WIDTH)
    l3 = lse.reshape(batch, tr, dil * B_WIDTH)

    def body(q_ref, k_ref, v_ref, do_ref, o_ref, l_ref, b_ref, dq_ref, dk_ref, dv_ref, ds_ref):
        first = (pl.program_id(1) == 0) & (pl.program_id(2) == 0)

        @pl.when(first)
        def _():
            ds_ref[...] = jnp.zeros_like(ds_ref)

        dk_ref[...] = jnp.zeros_like(dk_ref)
        dv_ref[...] = jnp.zeros_like(dv_ref)
        masks = _head_masks()

        def block(n, carry):
            r0 = pl.multiple_of(n * Q_BLOCK, Q_BLOCK)
            rows = pl.ds(r0, Q_BLOCK)
            q = q_ref[rows, :] * 0.125
            kc = k_ref[rows, :].astype(BF16)
            vc = v_ref[rows, :].astype(BF16)
            do = do_ref[rows, :]
            ov = o_ref[rows, :]
            lv = l_ref[rows, :]
            if nb > 1:
                p0 = pl.multiple_of(jnp.maximum(n - 1, 0) * Q_BLOCK, Q_BLOCK)
                prow = pl.ds(p0, Q_BLOCK)
                kp = k_ref[prow, :].astype(BF16)
                vp = v_ref[prow, :].astype(BF16)
                dkp = jnp.zeros((Q_BLOCK, LANE_BLOCK), F32)
                dvp = jnp.zeros((Q_BLOCK, LANE_BLOCK), F32)
            dq = jnp.zeros((Q_BLOCK, LANE_BLOCK), F32)
            dkc = jnp.zeros((Q_BLOCK, LANE_BLOCK), F32)
            dvc = jnp.zeros((Q_BLOCK, LANE_BLOCK), F32)
            for h in range(HEADS_PER_BLOCK):
                qh = jnp.where(masks[h], q, 0.0).astype(BF16)
                doh = jnp.where(masks[h], do, 0.0)
                doh_bf = doh.astype(BF16)
                lrow = jnp.max(jnp.where(masks[h], lv, -3e38), axis=1, keepdims=True)
                drow = jnp.sum(doh * ov, axis=1, keepdims=True)
                pc = jnp.exp(_dot_nt(qh, kc) + b_ref[h, :, Q_BLOCK:] - lrow)
                dsc = pc * (_dot_nt(doh_bf, vc) - drow)
                ds_ref[h, :, Q_BLOCK:] += dsc
                dsc_bf = dsc.astype(BF16)
                dqh = _dot(dsc_bf, kc)
                dkc = dkc + _dot_tn(dsc_bf, qh)
                dvc = dvc + _dot_tn(pc.astype(BF16), doh_bf)
                if nb > 1:
                    bp = jnp.where(n == 0, NEG_INF, b_ref[h, :, :Q_BLOCK])
                    pp = jnp.exp(_dot_nt(qh, kp) + bp - lrow)
                    dsp = pp * (_dot_nt(doh_bf, vp) - drow)
                    ds_ref[h, :, :Q_BLOCK] += dsp
                    dsp_bf = dsp.astype(BF16)
                    dqh = dqh + _dot(dsp_bf, kp)
                    dkp = dkp + _dot_tn(dsp_bf, qh)
                    dvp = dvp + _dot_tn(pp.astype(BF16), doh_bf)
                dq = jnp.where(masks[h], dqh, dq)
            dq_ref[rows, :] = dq * 0.125
            dk_ref[rows, :] += dkc
            dv_ref[rows, :] += dvc
            if nb > 1:
                dk_ref[prow, :] += dkp
                dv_ref[prow, :] += dvp
            return carry

        if nb > 1:
            lax.fori_loop(0, nb, block, 0)
        else:
            block(0, 0)

    def in_spec(col0):
        return pl.BlockSpec((None, tr, LANE_BLOCK), lambda g, b, r: (b, 0, r * N_COLBLK + col0 + g))

    do_spec = pl.BlockSpec((None, tr, LANE_BLOCK), lambda g, b, r: (b, 0, r * (D_MODEL // LANE_BLOCK) + 1 + g))
    hd_spec = pl.BlockSpec((None, tr, LANE_BLOCK), lambda g, b, r: (b, 0, r * HEAD_BLOCKS + g))
    tbl_spec = pl.BlockSpec((HEADS_PER_BLOCK, Q_BLOCK, 2 * Q_BLOCK), lambda g, b, r: (g, 0, 0))
    out_sds = jax.ShapeDtypeStruct((batch, tr, dil * B_WIDTH), F32)
    dq, dk, dv, ds = pl.pallas_call(
        body, grid=(HEAD_BLOCKS, batch, dil),
        in_specs=[in_spec(Q_COL), in_spec(K_COL), in_spec(V_COL), do_spec, hd_spec, hd_spec, tbl_spec],
        out_specs=[hd_spec, hd_spec, hd_spec, tbl_spec],
        out_shape=[out_sds, out_sds, out_sds, jax.ShapeDtypeStruct((B_HEADS, Q_BLOCK, 2 * Q_BLOCK), F32)],
        compiler_params=_params(("parallel", "arbitrary", "arbitrary")), name=f"attn_bwd_d{dil}",
    )(proj3, proj3, proj3, dmix3, o3, l3, bias)
    return dq.reshape(m, B_WIDTH), dk.reshape(m, B_WIDTH), dv.reshape(m, B_WIDTH), ds


PAIR = 2 * HEAD_DIM
N_PAIR = B_HEADS // 2
N_CFG = len(DILATED)
BLOCKS_PER_CFG = SEQ // Q_BLOCK
QKV_SLABS = 3 * N_PAIR
FWD_BLOCKS_PER_TRIP = 8
BWD_BLOCKS_PER_TRIP = 4


def _proj_fwd(x, g1, w_in):
    m = x.shape[0]
    tm = ROW_TILE

    def body(x_ref, g_ref, w_ref, h_ref, uv_ref, qkv_ref):
        xv = x_ref[...]
        h = (xv * _rstd(xv) * g_ref[...]).astype(BF16)
        h_ref[...] = h
        acc = _dot(h, w_ref[...])
        uv_ref[...] = acc[:, :2 * A_WIDTH]
        for s in range(QKV_SLABS):
            qkv_ref[s] = acc[:, 2 * A_WIDTH + s * PAIR:2 * A_WIDTH + (s + 1) * PAIR]

    return pl.pallas_call(
        body, grid=(m // tm,),
        in_specs=[pl.BlockSpec((tm, D_MODEL), lambda i: (i, 0)), _vec_spec(),
                  pl.BlockSpec((D_MODEL, IN_COLS), lambda i: (0, 0))],
        out_specs=[pl.BlockSpec((tm, D_MODEL), lambda i: (i, 0)),
                   pl.BlockSpec((tm, 2 * A_WIDTH), lambda i: (i, 0)),
                   pl.BlockSpec((QKV_SLABS, tm, PAIR), lambda i: (0, i, 0))],
        out_shape=[jax.ShapeDtypeStruct((m, D_MODEL), BF16), jax.ShapeDtypeStruct((m, 2 * A_WIDTH), F32),
                   jax.ShapeDtypeStruct((QKV_SLABS, m, PAIR), F32)],
        compiler_params=_params(("parallel",)), name="proj_fwd",
    )(x, g1, w_in)


def _pair_masks():
    lane = lax.broadcasted_iota(jnp.int32, (1, PAIR), 1)
    return [lane < HEAD_DIM, lane >= HEAD_DIM]


def _block_rows(idx, dil):
    if dil == 1:
        n = idx
        cur = pl.ds(pl.multiple_of(n * Q_BLOCK, Q_BLOCK), Q_BLOCK)
        prev = pl.ds(pl.multiple_of(jnp.maximum(n - 1, 0) * Q_BLOCK, Q_BLOCK), Q_BLOCK)
        return n, cur, prev
    r = idx % dil
    n = idx // dil
    cur = pl.ds(r + (dil * Q_BLOCK) * n, Q_BLOCK, stride=dil)
    prev = pl.ds(r + (dil * Q_BLOCK) * jnp.maximum(n - 1, 0), Q_BLOCK, stride=dil)
    return n, cur, prev


def _attn_fwd(qkv, bias, batch, shards):
    m = qkv.shape[1]
    comb_rows = 256
    nt = len(shards)
    shapes = [sh.shape for sh in shards]
    n_steps = batch * N_PAIR
    early, late = list(range(nt // 2)), list(range(nt // 2, nt))

    def body(q_ref, k_ref, v_ref, b_ref, *rest):
        shard_refs = rest[:nt]
        o_ref, l_ref = rest[nt:nt + 2]
        gat_refs = rest[nt + 2:2 * nt + 2]
        scratch = rest[2 * nt + 2:]
        oc_refs, lc_refs = scratch[:N_CFG], scratch[N_CFG:2 * N_CFG]
        send_sems, recv_sems = scratch[2 * N_CFG:]
        step = pl.program_id(0) * N_PAIR + pl.program_id(1)
        gather = _GatherPlan(shapes, shard_refs, gat_refs, send_sems, recv_sems)

        @pl.when(step == 0)
        def _():
            gather.start(early + late)

        @pl.when(step == n_steps // 2)
        def _():
            gather.forward(early)

        @pl.when(step == n_steps - 2)
        def _():
            gather.forward(late)

        masks = _pair_masks()
        for ci, (_, dil) in enumerate(DILATED):
            nb = SEQ // dil // Q_BLOCK

            def block(trip, carry, ci=ci, dil=dil, nb=nb):
                work = []
                for u in range(FWD_BLOCKS_PER_TRIP):
                    n, rows, prow = _block_rows(trip * FWD_BLOCKS_PER_TRIP + u, dil)
                    q = q_ref[rows, :] * 0.125
                    kc = k_ref[rows, :].astype(BF16)
                    vc = v_ref[rows, :]
                    kp = k_ref[prow, :].astype(BF16) if nb > 1 else None
                    vp = v_ref[prow, :] if nb > 1 else None
                    tiles = []
                    for h in range(2):
                        qh = jnp.where(masks[h], q, 0.0).astype(BF16)
                        sc = _dot_nt(qh, kc) + b_ref[ci, h, :, Q_BLOCK:]
                        sp = None
                        if nb > 1:
                            sp = _dot_nt(qh, kp) + jnp.where(n == 0, NEG_INF, b_ref[ci, h, :, :Q_BLOCK])
                        tiles.append((sc, sp))
                    work.append((rows, vc, vp, tiles))
                probs = []
                for _, _, _, tiles in work:
                    ps = []
                    for sc, sp in tiles:
                        mx = jnp.max(sc if sp is None else jnp.maximum(sc, sp), axis=1, keepdims=True)
                        pc = jnp.exp(sc - mx).astype(BF16)
                        pp = None if sp is None else jnp.exp(sp - mx).astype(BF16)
                        ps.append((mx, pc, pp))
                    probs.append(ps)
                for (rows, vc, vp, _), ps in zip(work, probs):
                    res = []
                    for h, (_, pc, pp) in enumerate(ps):
                        r = _dot(pc, jnp.where(masks[h], vc, 1.0).astype(BF16))
                        if pp is not None:
                            r = r + _dot(pp, jnp.where(masks[h], vp, 1.0).astype(BF16))
                        res.append(r)
                    num = jnp.where(masks[0], res[0], res[1])
                    den = pltpu.roll(jnp.where(masks[0], res[1], res[0]), HEAD_DIM, 1)
                    oc_refs[ci][rows, :] = num / den
                    lc_refs[ci][rows, :] = jnp.where(masks[0], ps[0][0], ps[1][0]) + jnp.log(den)
                return carry

            lax.fori_loop(0, BLOCKS_PER_CFG // FWD_BLOCKS_PER_TRIP, block, 0)

        def combine(i, carry):
            rr = pl.ds(pl.multiple_of(i * comb_rows, comb_rows), comb_rows)
            ls = [lc_refs[c][rr, :] for c in range(N_CFG)]
            mx = functools.reduce(jnp.maximum, ls)
            ws = [jnp.exp(l - mx) for l in ls]
            tot = functools.reduce(lambda a, b: a + b, ws)
            o = functools.reduce(lambda a, b: a + b, [ws[c] * oc_refs[c][rr, :] for c in range(N_CFG)]) / tot
            o_ref[rr, :] = o.astype(BF16)
            l_ref[rr, :] = mx + jnp.log(tot)
            return carry

        lax.fori_loop(0, SEQ // comb_rows, combine, 0)

        @pl.when(step == n_steps - 1)
        def _():
            gather.finish(early + late)

    def slab(first):
        return pl.BlockSpec((None, SEQ, PAIR), lambda b, p: (first + p, b, 0))

    nat = pl.BlockSpec((SEQ, PAIR), lambda b, p: (b, p))
    res = pl.pallas_call(
        body, grid=(batch, N_PAIR),
        in_specs=[slab(0), slab(N_PAIR), slab(2 * N_PAIR),
                  pl.BlockSpec((N_CFG, 2, Q_BLOCK, 2 * Q_BLOCK), lambda b, p: (0, p, 0, 0))] + [ANY] * nt,
        out_specs=[nat, nat] + [ANY] * nt,
        out_shape=[jax.ShapeDtypeStruct((m, B_WIDTH), BF16), jax.ShapeDtypeStruct((m, B_WIDTH), F32)]
        + [jax.ShapeDtypeStruct((N_SHARD,) + sh.shape, sh.dtype) for sh in shards],
        scratch_shapes=[pltpu.VMEM((SEQ, PAIR), F32)] * (2 * N_CFG)
        + [pltpu.SemaphoreType.DMA((6 * nt,)), pltpu.SemaphoreType.DMA((6 * nt,))],
        compiler_params=_params(("arbitrary", "arbitrary")), name="attn_fwd",
    )(qkv, qkv, qkv, bias, *shards)
    return res[0], res[1], list(res[2:])


def _attn_bwd(qkv, dmix, o, lse, bias, batch, parts):
    m = qkv.shape[1]
    nt = len(parts)
    n_steps = N_PAIR * batch

    def body(q_ref, k_ref, v_ref, do_ref, o_ref, l_ref, b_ref, *rest):
        part_refs = rest[:nt]
        dqkv_ref, ds_ref = rest[nt:nt + 2]
        recv_refs = rest[nt + 2:2 * nt + 2]
        dq_acc, dk_acc, dv_acc, d_scr, send_sems, recv_sems = rest[2 * nt + 2:]
        step = pl.program_id(0) * batch + pl.program_id(1)
        exchange = _ChipExchangePlan(part_refs, recv_refs, send_sems, recv_sems)

        @pl.when(step == 0)
        def _():
            exchange.start()

        @pl.when(pl.program_id(1) == 0)
        def _():
            ds_ref[...] = jnp.zeros_like(ds_ref)

        dq_acc[...] = jnp.zeros_like(dq_acc)
        dk_acc[...] = jnp.zeros_like(dk_acc)
        dv_acc[...] = jnp.zeros_like(dv_acc)
        masks = _pair_masks()
        ri = lax.broadcasted_iota(jnp.int32, (PAIR, PAIR), 0)
        cj = lax.broadcasted_iota(jnp.int32, (PAIR, PAIR), 1)
        same_head = ((ri < HEAD_DIM) == (cj < HEAD_DIM)).astype(F32)
        d_scr[...] = _dot(do_ref[...] * o_ref[...].astype(F32), same_head, lax.Precision.HIGHEST)

        for ci, (_, dil) in enumerate(DILATED):
            nb = SEQ // dil // Q_BLOCK

            def block(trip, carry, ci=ci, dil=dil, nb=nb):
                first = []
                for u in range(BWD_BLOCKS_PER_TRIP):
                    n, rows, prow = _block_rows(trip * BWD_BLOCKS_PER_TRIP + u, dil)
                    q = q_ref[rows, :] * 0.125
                    kc = k_ref[rows, :].astype(BF16)
                    vc = v_ref[rows, :].astype(BF16)
                    do = do_ref[rows, :]
                    lv = l_ref[rows, :]
                    dv_ = d_scr[rows, :]
                    kp = k_ref[prow, :].astype(BF16) if nb > 1 else None
                    vp = v_ref[prow, :].astype(BF16) if nb > 1 else None
                    heads = []
                    for h in range(2):
                        c0 = h * HEAD_DIM
                        qh = jnp.where(masks[h], q, 0.0).astype(BF16)
                        doh = jnp.where(masks[h], do, 0.0).astype(BF16)
                        sc = _dot_nt(qh, kc)
                        dpc = _dot_nt(doh, vc)
                        sp = _dot_nt(qh, kp) if nb > 1 else None
                        dpp = _dot_nt(doh, vp) if nb > 1 else None
                        heads.append((qh, doh, lv[:, c0:c0 + 1], dv_[:, c0:c0 + 1], sc, dpc, sp, dpp))
                    first.append((n, rows, prow, kc, kp, heads))
                second = []
                for n, rows, prow, kc, kp, heads in first:
                    out = []
                    for h, (qh, doh, lrow, drow, sc, dpc, sp, dpp) in enumerate(heads):
                        pc = jnp.exp(sc + b_ref[ci, h, :, Q_BLOCK:] - lrow)
                        dsc = pc * (dpc - drow)
                        ds_ref[ci, h, :, Q_BLOCK:] += dsc
                        pp_bf = dsp_bf = None
                        if nb > 1:
                            pp = jnp.exp(sp + jnp.where(n == 0, NEG_INF, b_ref[ci, h, :, :Q_BLOCK]) - lrow)
                            dsp = pp * (dpp - drow)
                            ds_ref[ci, h, :, :Q_BLOCK] += dsp
                            pp_bf, dsp_bf = pp.astype(BF16), dsp.astype(BF16)
                        out.append((qh, doh, pc.astype(BF16), dsc.astype(BF16), pp_bf, dsp_bf))
                    second.append((rows, prow, kc, kp, out))
                for rows, prow, kc, kp, out in second:
                    dq = jnp.zeros((Q_BLOCK, PAIR), F32)
                    dkc = jnp.zeros((Q_BLOCK, PAIR), F32)
                    dvc = jnp.zeros((Q_BLOCK, PAIR), F32)
                    dkp = jnp.zeros((Q_BLOCK, PAIR), F32)
                    dvp = jnp.zeros((Q_BLOCK, PAIR), F32)
                    for h, (qh, doh, pc_bf, dsc_bf, pp_bf, dsp_bf) in enumerate(out):
                        dqh = _dot(dsc_bf, kc)
                        dkc = dkc + _dot_tn(dsc_bf, qh)
                        dvc = dvc + _dot_tn(pc_bf, doh)
                        if nb > 1:
                            dqh = dqh + _dot(dsp_bf, kp)
                            dkp = dkp + _dot_tn(dsp_bf, qh)
                            dvp = dvp + _dot_tn(pp_bf, doh)
                        dq = jnp.where(masks[h], dqh, dq)
                    dq_acc[rows, :] += dq * 0.125
                    dk_acc[rows, :] += dkc
                    dv_acc[rows, :] += dvc
                    if nb > 1:
                        dk_acc[prow, :] += dkp
                        dv_acc[prow, :] += dvp
                return carry

            lax.fori_loop(0, BLOCKS_PER_CFG // BWD_BLOCKS_PER_TRIP, block, 0)

        dqkv_ref[0] = dq_acc[...].astype(BF16)
        dqkv_ref[1] = dk_acc[...].astype(BF16)
        dqkv_ref[2] = dv_acc[...].astype(BF16)

        @pl.when(step == n_steps - 1)
        def _():
            exchange.finish()

    def slab(first):
        return pl.BlockSpec((None, SEQ, PAIR), lambda p, b: (first + p, b, 0))

    nat = pl.BlockSpec((SEQ, PAIR), lambda p, b: (b, p))
    tbl = pl.BlockSpec((N_CFG, 2, Q_BLOCK, 2 * Q_BLOCK), lambda p, b: (0, p, 0, 0))
    acc = pltpu.VMEM((SEQ, PAIR), F32)
    res = pl.pallas_call(
        body, grid=(N_PAIR, batch),
        in_specs=[slab(0), slab(N_PAIR), slab(2 * N_PAIR),
                  pl.BlockSpec((SEQ, PAIR), lambda p, b: (b, A_WIDTH // PAIR + p)), nat, nat, tbl] + [ANY] * nt,
        out_specs=[pl.BlockSpec((3, SEQ, PAIR), lambda p, b: (0, b, p)), tbl] + [ANY] * nt,
        out_shape=[jax.ShapeDtypeStruct((3, m, B_WIDTH), BF16),
                   jax.ShapeDtypeStruct((N_CFG, B_HEADS, Q_BLOCK, 2 * Q_BLOCK), F32)]
        + [jax.ShapeDtypeStruct((3,) + p.shape[1:], p.dtype) for p in parts],
        scratch_shapes=[acc, acc, acc, acc, pltpu.SemaphoreType.DMA((3 * nt,)), pltpu.SemaphoreType.DMA((3 * nt,))],
        compiler_params=_params(("arbitrary", "arbitrary")), name="attn_bwd",
    )(qkv, qkv, qkv, dmix, o, lse, bias, *parts)
    return res[0], res[1], list(res[2:])


def _rel_bias_grad(ds, buckets_np):
    present = [sorted(set(int(v) for v in np.unique(buckets_np[c]) if v >= 0)) for c in range(N_CFG)]

    def body(bk_ref, ds_ref, o_ref, acc_ref):
        acc_ref[...] = jnp.zeros_like(acc_ref)
        for c in range(N_CFG):
            bk = bk_ref[c]
            for h in range(B_HEADS):
                dsv = ds_ref[c, h]
                for b in present[c]:
                    part = jnp.sum(jnp.where(bk == b, dsv, 0.0), axis=0, keepdims=True)
                    acc_ref[pl.ds(h * NUM_BUCKETS + b, 1), :] += part
        o_ref[...] = jnp.sum(acc_ref[...], axis=1, keepdims=True)

    vm = pl.BlockSpec(memory_space=pltpu.VMEM)
    return pl.pallas_call(
        body, in_specs=[vm, vm], out_specs=vm,
        out_shape=jax.ShapeDtypeStruct((B_HEADS * NUM_BUCKETS, 1), F32),
        scratch_shapes=[pltpu.VMEM((B_HEADS * NUM_BUCKETS, 2 * Q_BLOCK), F32)],
        compiler_params=_params(), name="rel_bias_grad",
    )(jnp.asarray(buckets_np), ds)


def _assemble_dproj(duv, dqkv):
    m = duv.shape[0]

    rows = 1024

    def body(duv_ref, dqkv_ref, o_ref):
        o_ref[:, :2 * A_WIDTH] = duv_ref[...].astype(BF16)
        for k in range(3):
            o_ref[:, 2 * A_WIDTH + k * B_WIDTH:2 * A_WIDTH + (k + 1) * B_WIDTH] = dqkv_ref[k]

    return pl.pallas_call(
        body, grid=(m // rows,),
        in_specs=[pl.BlockSpec((rows, 2 * A_WIDTH), lambda i: (i, 0)),
                  pl.BlockSpec((3, rows, B_WIDTH), lambda i: (0, i, 0))],
        out_specs=pl.BlockSpec((rows, IN_COLS), lambda i: (i, 0)),
        out_shape=jax.ShapeDtypeStruct((m, IN_COLS), BF16),
        compiler_params=_params(("parallel",)), name="assemble_dproj",
    )(duv, dqkv)


def _shift_down(x, k):
    row = lax.broadcasted_iota(jnp.int32, x.shape, 0)
    return jnp.where(row >= k, pltpu.roll(x, k, 0), 0.0)


def _shift_up(x, k):
    n = x.shape[0]
    row = lax.broadcasted_iota(jnp.int32, x.shape, 0)
    return jnp.where(row < n - k, pltpu.roll(x, n - k, 0), 0.0)


def _convgate_fwd(gate, up, conv_w, conv_b, batch):
    m = gate.shape[0]

    def body(g_ref, u_ref, w_ref, b_ref, a_ref):
        g = g_ref[...].astype(F32)
        w = w_ref[...]
        c = b_ref[...] + w[0:1] * _shift_down(g, 2) + w[1:2] * _shift_down(g, 1) + w[2:3] * g
        a_ref[...] = (_gelu(c) * u_ref[...].astype(F32)).astype(BF16)

    blk = pl.BlockSpec((SEQ, LANE_BLOCK), lambda b, j: (b, j))
    return pl.pallas_call(
        body, grid=(batch, D_FF // LANE_BLOCK),
        in_specs=[blk, blk, pl.BlockSpec((3, LANE_BLOCK), lambda b, j: (0, j)),
                  pl.BlockSpec((1, LANE_BLOCK), lambda b, j: (0, j))],
        out_specs=blk,
        out_shape=jax.ShapeDtypeStruct((m, D_FF), BF16),
        compiler_params=_params(("parallel", "parallel")), name="convgate_fwd",
    )(gate, up, conv_w, conv_b)


def _convgate_bwd(gate, up, dact, conv_w, conv_b, batch):
    m = gate.shape[0]

    def body(g_ref, u_ref, da_ref, w_ref, b_ref, dg_ref, du_ref, dw_ref, db_ref):
        @pl.when(pl.program_id(1) == 0)
        def _():
            dw_ref[...] = jnp.zeros_like(dw_ref)
            db_ref[...] = jnp.zeros_like(db_ref)

        g = g_ref[...].astype(F32)
        w = w_ref[...]
        g1 = _shift_down(g, 1)
        g2 = _shift_down(g, 2)
        c = b_ref[...] + w[0:1] * g2 + w[1:2] * g1 + w[2:3] * g
        gg, dgg = _gelu_and_grad(c)
        da = da_ref[...].astype(F32)
        du_ref[...] = (da * gg).astype(BF16)
        dc = da * u_ref[...].astype(F32) * dgg
        db_ref[...] += jnp.sum(dc, axis=0, keepdims=True)
        dw_ref[0:1, :] += jnp.sum(dc * g2, axis=0, keepdims=True)
        dw_ref[1:2, :] += jnp.sum(dc * g1, axis=0, keepdims=True)
        dw_ref[2:3, :] += jnp.sum(dc * g, axis=0, keepdims=True)
        dg_ref[...] = (w[2:3] * dc + w[1:2] * _shift_up(dc, 1) + w[0:1] * _shift_up(dc, 2)).astype(BF16)

    blk = pl.BlockSpec((SEQ, LANE_BLOCK), lambda j, b: (b, j))
    wspec = pl.BlockSpec((3, LANE_BLOCK), lambda j, b: (0, j))
    bspec = pl.BlockSpec((1, LANE_BLOCK), lambda j, b: (0, j))
    return pl.pallas_call(
        body, grid=(D_FF // LANE_BLOCK, batch),
        in_specs=[blk, blk, blk, wspec, bspec],
        out_specs=[blk, blk, wspec, bspec],
        out_shape=[jax.ShapeDtypeStruct((m, D_FF), BF16), jax.ShapeDtypeStruct((m, D_FF), BF16),
                   jax.ShapeDtypeStruct((3, D_FF), F32), jax.ShapeDtypeStruct((1, D_FF), F32)],
        compiler_params=_params(("parallel", "arbitrary")), name="convgate_bwd",
    )(gate, up, dact, conv_w, conv_b)


def _train_step(x, tgt, g1, g2, g3, g4, shards, ln_g, ln_b, w_s, b_s, rel_bias, conv_w_shard, conv_b, batch,
                s_idx, c_idx):
    big = dict(tm=1024, out_dtype=F32)
    buckets = _bucket_tables()
    bias = _bias_tables(rel_bias, buckets)
    bz = jnp.repeat(b_s.T, HEAD_DIM, axis=1)
    w_st = jnp.swapaxes(w_s, 1, 2)

    def with_own(gathered, own):
        return lax.dynamic_update_index_in_dim(gathered, own, s_idx[0], 0)

    g_in, g_convw = _gather_weights([shards["w_in"]], conv_w_shard)
    w_in = _from_col_shards(with_own(g_in, shards["w_in"]))
    conv_w = _from_col_shards(with_own(g_convw, conv_w_shard))

    h1, uv, qkv = _proj_fwd(x, g1, w_in)
    a = _gate_fwd(uv, ln_g, ln_b, w_s, bz)
    later = ["w_out", "w_gate", "w_up", "w_down"]
    o_bf, lse, gathered = _attn_fwd(qkv, bias, batch, [shards[n] for n in later])
    g_out, g_gate, g_up, g_down = [with_own(g, shards[n]) for g, n in zip(gathered, later)]
    w_out = g_out.reshape(D_MODEL, D_MODEL)
    w_gate = _from_col_shards(g_gate)
    w_up = _from_col_shards(g_up)
    w_down = g_down.reshape(D_FF, D_MODEL)
    y1 = _mm(a, w_out[:A_WIDTH], dims="nn", tn=1024, tk=A_WIDTH, name="mm_out_a", **big)
    y1 = _mm(o_bf, w_out[A_WIDTH:], dims="nn", tn=1024, tk=B_WIDTH, name="mm_out_b", add=y1, **big)
    x1, h2 = _mid_fwd(x, y1, g2, g3)
    gate = _mm(h2, w_gate, dims="nn", tm=1024, tn=1408, tk=1024, out_dtype=BF16, name="mm_gate")
    up = _mm(h2, w_up, dims="nn", tm=1024, tn=1408, tk=1024, out_dtype=BF16, name="mm_up")
    act = _convgate_fwd(gate, up, conv_w, conv_b, batch)
    y2 = _mm(act, w_down, dims="nn", tn=1024, tk=1408, name="mm_down", **big)
    dx2, dy2, dg4, loss = _loss_head(x1, y2, tgt, g4)

    dact = _mm(dy2, w_down, dims="nt", tm=1024, tn=1408, tk=1024, out_dtype=BF16, name="mm_dact")
    dw_down = _mm(act, dy2, dims="tn", tm=1408, tn=1024, tk=1024, out_dtype=F32, name="mm_dw_down")
    dgate, dup, dconv_w, dconv_b = _convgate_bwd(gate, up, dact, conv_w, conv_b, batch)
    dh2 = _mm(dgate, w_gate, dims="nt", tn=1024, tk=1408, name="mm_dh2_g", **big)
    dh2 = _mm(dup, w_up, dims="nt", tn=1024, tk=1408, name="mm_dh2_u", add=dh2, **big)
    dw_gate = _mm(h2, dgate, dims="tn", tm=1024, tn=1408, tk=1024, out_dtype=F32, name="mm_dw_gate")
    dw_up = _mm(h2, dup, dims="tn", tm=1024, tn=1408, tk=1024, out_dtype=F32, name="mm_dw_up")
    dx1, dy1, dg2, dg3 = _mid_bwd(x1, y1, dh2, dx2, g2, g3)
    dmix = _mm(dy1, w_out, dims="nt", tn=1024, tk=1024, name="mm_dmix", **big)
    dw_out_a = _mm(a, dy1, dims="tn", tm=A_WIDTH, tn=1024, tk=1024, out_dtype=F32, name="mm_dw_out_a")
    dw_out_b = _mm(o_bf, dy1, dims="tn", tm=B_WIDTH, tn=1024, tk=1024, out_dtype=F32, name="mm_dw_out_b")
    duv, dln_g, dln_b, dw_s, dbz = _gate_bwd(uv, dmix, ln_g, ln_b, w_s, w_st, bz)

    dw_out = jnp.concatenate([dw_out_a, dw_out_b], axis=0)
    done = [dw_down.reshape(N_SHARD, SHARD_FF, D_MODEL), _col_shards(dw_gate), _col_shards(dw_up),
            dw_out.reshape(N_SHARD, D_MODEL // N_SHARD, D_MODEL)]
    parts = [_add_halves(g, r, c_idx) for g, r in zip(done, _exchange_halves(done, "rs_sibling_exchange_ffn"))]
    dqkv, ds, recv = _attn_bwd(qkv, dmix, o_bf, lse, bias, batch, parts)
    fulls = [_add_chips(p, r, s_idx, c_idx) for p, r in zip(parts, recv)]
    drel = _rel_bias_grad(ds, buckets)
    dproj = _assemble_dproj(duv, dqkv)
    dw_in = _mm(h1, dproj, dims="tn", tm=1024, tn=1408, tk=1024, out_dtype=F32, name="mm_dw_in")
    last = [_col_shards(dw_in)]
    part_in = [_add_halves(g, r, c_idx) for g, r in zip(last, _exchange_halves(last, "rs_sibling_exchange_in"))]
    fulls += [_add_chips(p, r, s_idx, c_idx) for p, r in zip(part_in, _exchange_chips(part_in))]
    dh1 = _mm(dproj, w_in, dims="nt", tn=1024, tk=1408, name="mm_dh1", **big)
    dx0, dg1 = _in_bwd(x, dh1, dx1, g1)
    reduced = dict(zip(["w_down", "w_gate", "w_up", "w_out", "w_in"], _share_halves(fulls)))

    small = dict(
        norm_mix_pre=dg1, norm_mix_post=dg2, norm_ffn_pre=dg3, norm_ffn_post=dg4,
        ln_v_gain=dln_g, ln_v_bias=dln_b, spatial_w=dw_s,
        spatial_b=dbz[:, ::HEAD_DIM].T,
        rel_bias=drel.reshape(B_HEADS, NUM_BUCKETS).T,
        conv_w=dconv_w, conv_b=dconv_b,
    )
    return loss, dx0, small, reduced


def _mesh_pos():
    x, y, c = lax.axis_index("x"), lax.axis_index("y"), lax.axis_index("c")
    chips = [(1 - x, y), (x, 1 - y), (1 - x, 1 - y)]
    return x, y, c, chips


ANY = pl.BlockSpec(memory_space=pl.ANY)


class _GatherPlan:
    def __init__(self, shapes, shard_refs, out_refs, send_sems, recv_sems):
        self.shapes, self.shard_refs, self.out_refs = shapes, shard_refs, out_refs
        self.send_sems, self.recv_sems = send_sems, recv_sems
        self.x, self.y, self.c, self.chips = _mesh_pos()
        self.sib = (self.x, self.y, 1 - self.c)

    def _half(self, t, chip, which):
        rows = self.shapes[t][0] // 2
        return self.out_refs[t].at[2 * chip[0] + chip[1], pl.ds(which * rows, rows), :]

    def _copy(self, k, src, dst, to):
        return pltpu.make_async_remote_copy(src_ref=src, dst_ref=dst, send_sem=self.send_sems.at[k],
                                            recv_sem=self.recv_sems.at[k], device_id=to, device_id_type=MESH)

    def _sends(self, t):
        rows = self.shapes[t][0] // 2
        src = self.shard_refs[t].at[pl.ds(self.c * rows, rows), :]
        return [self._copy(6 * t + j, src, self._half(t, (self.x, self.y), self.c), (*chip, self.c))
                for j, chip in enumerate(self.chips)]

    def _forwards(self, t):
        return [self._copy(6 * t + 3 + j, self._half(t, chip, self.c), self._half(t, chip, self.c), self.sib)
                for j, chip in enumerate(self.chips)]

    def start(self, ts):
        for t in ts:
            for cp in self._sends(t):
                cp.start()

    def forward(self, ts):
        for t in ts:
            for j, chip in enumerate(self.chips):
                landed = self._half(t, chip, self.c)
                self._copy(6 * t + j, landed, landed, (*chip, self.c)).wait_recv()
            for cp in self._forwards(t):
                cp.start()

    def finish(self, ts):
        for t in ts:
            for j, chip in enumerate(self.chips):
                other = self._half(t, chip, 1 - self.c)
                self._copy(6 * t + 3 + j, other, other, self.sib).wait_recv()
        for t in ts:
            for cp in self._sends(t) + self._forwards(t):
                cp.wait_send()


class _ChipExchangePlan:
    def __init__(self, part_refs, out_refs, send_sems, recv_sems):
        self.part_refs, self.out_refs, self.send_sems, self.recv_sems = part_refs, out_refs, send_sems, recv_sems
        _, _, self.c, self.chips = _mesh_pos()

    def _copies(self):
        return [pltpu.make_async_remote_copy(
            src_ref=p.at[2 * chip[0] + chip[1]], dst_ref=o.at[j], send_sem=self.send_sems.at[3 * t + j],
            recv_sem=self.recv_sems.at[3 * t + j], device_id=(*chip, self.c), device_id_type=MESH)
            for t, (p, o) in enumerate(zip(self.part_refs, self.out_refs)) for j, chip in enumerate(self.chips)]

    def start(self):
        for cp in self._copies():
            cp.start()

    def finish(self):
        for cp in self._copies():
            cp.wait()


def _gather_weights(shards, conv_w_shard):
    nt = len(shards)

    def body(*refs):
        shard_refs = refs[:nt]
        cw_ref = refs[nt]
        out_refs = refs[nt + 1:2 * nt + 1]
        cw_out = refs[2 * nt + 1]
        send_sems, recv_sems = refs[2 * nt + 2:]
        x, y, c, chips = _mesh_pos()
        s = 2 * x + y
        sib = (x, y, 1 - c)

        def half(ref, chip, which, t):
            rows = shards[t].shape[0] // 2
            return ref.at[2 * chip[0] + chip[1], pl.ds(which * rows, rows), :]

        def rcopy(k, src, dst, to):
            return pltpu.make_async_remote_copy(src_ref=src, dst_ref=dst, send_sem=send_sems.at[k],
                                                recv_sem=recv_sems.at[k], device_id=to, device_id_type=MESH)

        sends = []
        for t in range(nt):
            rows = shards[t].shape[0] // 2
            for j, chip in enumerate(chips):
                sends.append(rcopy(7 * t + j, shard_refs[t].at[pl.ds(c * rows, rows), :],
                                   half(out_refs[t], (x, y), c, t), (*chip, c)))
        for j, chip in enumerate(chips):
            sends.append(rcopy(7 * nt + j, cw_ref, cw_out.at[s], (*chip, c)))
        for cp in sends:
            cp.start()
        fwd = []
        for t in range(nt):
            for j, chip in enumerate(chips):
                landed = half(out_refs[t], chip, c, t)
                rcopy(7 * t + j, landed, landed, (*chip, c)).wait_recv()
                f = rcopy(7 * t + 3 + j, landed, landed, sib)
                f.start()
                fwd.append(f)
        for j, chip in enumerate(chips):
            dst = cw_out.at[2 * chip[0] + chip[1]]
            rcopy(7 * nt + j, dst, dst, (*chip, c)).wait_recv()
        for t in range(nt):
            for j, chip in enumerate(chips):
                other = half(out_refs[t], chip, 1 - c, t)
                rcopy(7 * t + 3 + j, other, other, sib).wait_recv()
        for cp in sends + fwd:
            cp.wait_send()

    out_shape = [jax.ShapeDtypeStruct((N_SHARD,) + sh.shape, sh.dtype) for sh in shards]
    out_shape.append(jax.ShapeDtypeStruct((N_SHARD,) + conv_w_shard.shape, conv_w_shard.dtype))
    nsem = 7 * nt + 3
    return pl.pallas_call(
        body, in_specs=[ANY] * (nt + 1), out_specs=[ANY] * (nt + 1), out_shape=out_shape,
        scratch_shapes=[pltpu.SemaphoreType.DMA((nsem,)), pltpu.SemaphoreType.DMA((nsem,))],
        compiler_params=pltpu.CompilerParams(has_side_effects=True), name="gather_weights",
    )(*shards, conv_w_shard)


def _exchange_halves(grads, name):
    nt = len(grads)

    def body(*refs):
        g_refs = refs[:nt]
        out_refs = refs[nt:2 * nt]
        send_sems, recv_sems = refs[2 * nt:]
        x, y, c, _ = _mesh_pos()
        copies = []
        for t in range(nt):
            rows = grads[t].shape[1] // 2
            copies.append(pltpu.make_async_remote_copy(
                src_ref=g_refs[t].at[:, pl.ds((1 - c) * rows, rows), :], dst_ref=out_refs[t],
                send_sem=send_sems.at[t], recv_sem=recv_sems.at[t], device_id=(x, y, 1 - c), device_id_type=MESH))
        for cp in copies:
            cp.start()
        for cp in copies:
            cp.wait()

    out_shape = [jax.ShapeDtypeStruct((N_SHARD, g.shape[1] // 2, g.shape[2]), g.dtype) for g in grads]
    return pl.pallas_call(
        body, in_specs=[ANY] * nt, out_specs=[ANY] * nt, out_shape=out_shape,
        scratch_shapes=[pltpu.SemaphoreType.DMA((nt,)), pltpu.SemaphoreType.DMA((nt,))],
        compiler_params=pltpu.CompilerParams(has_side_effects=True), name=name,
    )(*grads)


def _add_halves(g, recv, c_idx):
    _, rows2, cols = g.shape
    rows = rows2 // 2
    tr = rows // 2 if rows % 16 == 0 and rows >= 256 else rows
    nblk = rows // tr

    def body(c_ref, g_ref, r_ref, o_ref):
        o_ref[...] = (g_ref[...] + r_ref[...]).astype(BF16)

    return pl.pallas_call(
        body,
        grid_spec=pltpu.PrefetchScalarGridSpec(
            num_scalar_prefetch=1, grid=(N_SHARD, nblk),
            in_specs=[pl.BlockSpec((None, tr, cols), lambda s, i, c: (s, c[0] * nblk + i, 0)),
                      pl.BlockSpec((None, tr, cols), lambda s, i, c: (s, i, 0))],
            out_specs=pl.BlockSpec((None, tr, cols), lambda s, i, c: (s, i, 0))),
        out_shape=jax.ShapeDtypeStruct((N_SHARD, rows, cols), BF16),
        compiler_params=_params(("parallel", "parallel")), name="rs_add_halves",
    )(c_idx, g, recv)


def _exchange_chips(parts):
    nt = len(parts)

    def body(*refs):
        p_refs = refs[:nt]
        out_refs = refs[nt:2 * nt]
        send_sems, recv_sems = refs[2 * nt:]
        x, y, c, chips = _mesh_pos()
        copies = []
        for t in range(nt):
            for j, chip in enumerate(chips):
                copies.append(pltpu.make_async_remote_copy(
                    src_ref=p_refs[t].at[2 * chip[0] + chip[1]], dst_ref=out_refs[t].at[j],
                    send_sem=send_sems.at[3 * t + j], recv_sem=recv_sems.at[3 * t + j],
                    device_id=(*chip, c), device_id_type=MESH))
        for cp in copies:
            cp.start()
        for cp in copies:
            cp.wait()

    out_shape = [jax.ShapeDtypeStruct((3,) + p.shape[1:], p.dtype) for p in parts]
    return pl.pallas_call(
        body, in_specs=[ANY] * nt, out_specs=[ANY] * nt, out_shape=out_shape,
        scratch_shapes=[pltpu.SemaphoreType.DMA((3 * nt,)), pltpu.SemaphoreType.DMA((3 * nt,))],
        compiler_params=pltpu.CompilerParams(has_side_effects=True), name="rs_chip_exchange",
    )(*parts)


def _add_chips(part, recv, s_idx, c_idx):
    _, rows, cols = part.shape
    tr = rows // 2 if rows % 32 == 0 and rows >= 256 else rows
    nblk = rows // tr

    def body(idx_ref, p_ref, r_ref, o_ref):
        acc = p_ref[...].astype(F32)
        for j in range(3):
            acc = acc + r_ref[j].astype(F32)
        o_ref[...] = acc

    return pl.pallas_call(
        body,
        grid_spec=pltpu.PrefetchScalarGridSpec(
            num_scalar_prefetch=1, grid=(nblk,),
            in_specs=[pl.BlockSpec((None, tr, cols), lambda i, idx: (idx[0], i, 0)),
                      pl.BlockSpec((3, tr, cols), lambda i, idx: (0, i, 0))],
            out_specs=pl.BlockSpec((tr, cols), lambda i, idx: (idx[1] * nblk + i, 0))),
        out_shape=jax.ShapeDtypeStruct((2 * rows, cols), F32),
        compiler_params=_params(("parallel",)), name="rs_add_chips",
    )(jnp.concatenate([s_idx, c_idx]), part, recv)


def _share_halves(fulls):
    nt = len(fulls)

    def body(*refs):
        out_refs = refs[nt:2 * nt]
        send_sems, recv_sems = refs[2 * nt:]
        x, y, c, _ = _mesh_pos()
        copies = []
        for t in range(nt):
            rows = fulls[t].shape[0] // 2
            mine = out_refs[t].at[pl.ds(c * rows, rows), :]
            copies.append(pltpu.make_async_remote_copy(
                src_ref=mine, dst_ref=mine, send_sem=send_sems.at[t], recv_sem=recv_sems.at[t],
                device_id=(x, y, 1 - c), device_id_type=MESH))
        for cp in copies:
            cp.start()
        for t in range(nt):
            rows = fulls[t].shape[0] // 2
            theirs = out_refs[t].at[pl.ds((1 - c) * rows, rows), :]
            pltpu.make_async_remote_copy(
                src_ref=theirs, dst_ref=theirs, send_sem=send_sems.at[t], recv_sem=recv_sems.at[t],
                device_id=(x, y, 1 - c), device_id_type=MESH).wait_recv()
        for cp in copies:
            cp.wait_send()

    out_shape = [jax.ShapeDtypeStruct(f.shape, f.dtype) for f in fulls]
    return pl.pallas_call(
        body, in_specs=[ANY] * nt, out_specs=[ANY] * nt, out_shape=out_shape,
        input_output_aliases={t: t for t in range(nt)},
        scratch_shapes=[pltpu.SemaphoreType.DMA((nt,)), pltpu.SemaphoreType.DMA((nt,))],
        compiler_params=pltpu.CompilerParams(has_side_effects=True), name="rs_share_halves",
    )(*fulls)


def _allreduce_small(packed):
    rows = packed.shape[0]

    def body(p_ref, o_ref, sib_ref, chip_ref, send_sems, recv_sems):
        x, y, c, chips = _mesh_pos()
        first = pltpu.make_async_remote_copy(src_ref=p_ref, dst_ref=sib_ref, send_sem=send_sems.at[0],
                                             recv_sem=recv_sems.at[0], device_id=(x, y, 1 - c), device_id_type=MESH)
        first.start()
        first.wait()
        o_ref[...] = p_ref[...] + sib_ref[...]
        copies = [pltpu.make_async_remote_copy(src_ref=o_ref, dst_ref=chip_ref.at[j], send_sem=send_sems.at[1 + j],
                                               recv_sem=recv_sems.at[1 + j], device_id=(*chip, c), device_id_type=MESH)
                  for j, chip in enumerate(chips)]
        for cp in copies:
            cp.start()
        for cp in copies:
            cp.wait()
        o_ref[...] = (o_ref[...] + chip_ref[0]) + (chip_ref[1] + chip_ref[2])

    vm = pl.BlockSpec(memory_space=pltpu.VMEM)
    return pl.pallas_call(
        body, in_specs=[vm], out_specs=vm, out_shape=jax.ShapeDtypeStruct(packed.shape, F32),
        scratch_shapes=[pltpu.VMEM((rows, 128), F32), pltpu.VMEM((3, rows, 128), F32),
                        pltpu.SemaphoreType.DMA((4,)), pltpu.SemaphoreType.DMA((4,))],
        compiler_params=pltpu.CompilerParams(has_side_effects=True, vmem_limit_bytes=VMEM_LIMIT),
        name="allreduce_small",
    )(packed)


def _adamw(w, g, m, v, name):
    rows, cols = w.shape
    tr = rows
    if rows * cols > 256 * 1024:
        tr = next(cand for cand in (256, 176, 128) if rows % cand == 0)

    def body(w_ref, g_ref, m_ref, v_ref, go_ref, d_ref, nm_ref, nv_ref):
        gv = g_ref[...]
        go_ref[...] = gv
        nm = ADAM_B1 * m_ref[...] + (1.0 - ADAM_B1) * gv
        nv = ADAM_B2 * v_ref[...] + (1.0 - ADAM_B2) * (gv * gv)
        m_hat = nm / (1.0 - ADAM_B1 ** ADAM_STEP)
        v_hat = nv / (1.0 - ADAM_B2 ** ADAM_STEP)
        d_ref[...] = -ADAM_LR * (m_hat / (jnp.sqrt(v_hat) + ADAM_EPS) + ADAM_WD * w_ref[...])
        nm_ref[...] = nm
        nv_ref[...] = nv

    spec = pl.BlockSpec((tr, cols), lambda i: (i, 0))
    sds = jax.ShapeDtypeStruct((rows, cols), F32)
    return pl.pallas_call(
        body, grid=(rows // tr,), in_specs=[spec] * 4, out_specs=[spec] * 4, out_shape=[sds] * 4,
        compiler_params=_params(("parallel",)), name=name,
    )(w, g, m, v)


def _pack(arrays, rows):
    flat = jnp.concatenate([a.reshape(-1) for a in arrays])
    flat = jnp.pad(flat, (0, rows * 128 - flat.shape[0]))
    return flat.reshape(rows, 128)


def _unpack(packed, shapes):
    flat = packed.reshape(-1)
    out, off = [], 0
    for sh in shapes:
        n = int(np.prod(sh))
        out.append(flat[off:off + n].reshape(sh))
        off += n
    return out


SMALL = ["norm_mix_pre", "norm_mix_post", "norm_ffn_pre", "norm_ffn_post", "ln_v_gain", "ln_v_bias",
         "spatial_w", "spatial_b", "rel_bias", "conv_b"]
LARGE = ["w_in", "w_gate", "w_up", "w_down", "w_out"]
ORDER = ["norm_mix_pre", "norm_mix_post", "norm_ffn_pre", "norm_ffn_post", "w_in", "ln_v_gain", "ln_v_bias",
         "spatial_w", "spatial_b", "rel_bias", "w_out", "w_gate", "w_up", "conv_w", "conv_b", "w_down"]


def _col_shards(full):
    rows, cols4 = full.shape
    return full.reshape(rows, N_SHARD, cols4 // N_SHARD).transpose(1, 0, 2)


def _from_col_shards(g):
    n, rows, cols = g.shape
    return g.transpose(1, 0, 2).reshape(rows, n * cols)


def kernel(x, norm_mix_pre, norm_mix_post, norm_ffn_pre, norm_ffn_post, w_in, ln_v_gain, ln_v_bias, spatial_w, spatial_b, rel_bias, w_out, w_gate, w_up, conv_w, conv_b, w_down, loss_target, m_norm_mix_pre, m_norm_mix_post, m_norm_ffn_pre, m_norm_ffn_post, m_w_in, m_ln_v_gain, m_ln_v_bias, m_spatial_w, m_spatial_b, m_rel_bias, m_w_out, m_w_gate, m_w_up, m_conv_w, m_conv_b, m_w_down, v_norm_mix_pre, v_norm_mix_post, v_norm_ffn_pre, v_norm_ffn_post, v_w_in, v_ln_v_gain, v_ln_v_bias, v_spatial_w, v_spatial_b, v_rel_bias, v_w_out, v_w_gate, v_w_up, v_conv_w, v_conv_b, v_w_down):
    params = dict(norm_mix_pre=norm_mix_pre, norm_mix_post=norm_mix_post, norm_ffn_pre=norm_ffn_pre,
                  norm_ffn_post=norm_ffn_post, w_in=w_in, ln_v_gain=ln_v_gain, ln_v_bias=ln_v_bias,
                  spatial_w=spatial_w, spatial_b=spatial_b, rel_bias=rel_bias, w_out=w_out, w_gate=w_gate,
                  w_up=w_up, conv_w=conv_w, conv_b=conv_b, w_down=w_down)
    mom = dict(norm_mix_pre=m_norm_mix_pre, norm_mix_post=m_norm_mix_post, norm_ffn_pre=m_norm_ffn_pre,
               norm_ffn_post=m_norm_ffn_post, w_in=m_w_in, ln_v_gain=m_ln_v_gain, ln_v_bias=m_ln_v_bias,
               spatial_w=m_spatial_w, spatial_b=m_spatial_b, rel_bias=m_rel_bias, w_out=m_w_out, w_gate=m_w_gate,
               w_up=m_w_up, conv_w=m_conv_w, conv_b=m_conv_b, w_down=m_w_down)
    var = dict(norm_mix_pre=v_norm_mix_pre, norm_mix_post=v_norm_mix_post, norm_ffn_pre=v_norm_ffn_pre,
               norm_ffn_post=v_norm_ffn_post, w_in=v_w_in, ln_v_gain=v_ln_v_gain, ln_v_bias=v_ln_v_bias,
               spatial_w=v_spatial_w, spatial_b=v_spatial_b, rel_bias=v_rel_bias, w_out=v_w_out, w_gate=v_w_gate,
               w_up=v_w_up, conv_w=v_conv_w, conv_b=v_conv_b, w_down=v_w_down)

    batch = x.shape[0]
    xi, yi, ci = lax.axis_index("x"), lax.axis_index("y"), lax.axis_index("c")
    s_idx = (2 * xi + yi).astype(jnp.int32).reshape(1)
    c_idx = ci.astype(jnp.int32).reshape(1)

    shards = {n: params[n][0].astype(BF16) for n in LARGE}
    loss_part, dx0, grads, reduced = _train_step(
        x.reshape(batch * SEQ, D_MODEL), loss_target.reshape(batch * SEQ, D_MODEL),
        norm_mix_pre, norm_mix_post, norm_ffn_pre, norm_ffn_post, shards,
        ln_v_gain.reshape(1, A_WIDTH), ln_v_bias.reshape(1, A_WIDTH), spatial_w[0], spatial_b[0], rel_bias,
        conv_w[0], conv_b, batch, s_idx, c_idx)
    loss = lax.psum(loss_part[0, 0], ("x", "y", "c"))
    grad_x = dx0.reshape(batch, SEQ, D_MODEL)

    small_g = [grads[n].reshape(params[n].shape) for n in SMALL] + [grads["conv_w"]]
    n_small = sum(int(np.prod(g.shape)) for g in small_g)
    small_rows = -(-n_small // (8 * 128)) * 8
    summed = _unpack(_allreduce_small(_pack(small_g, small_rows)),
                     [params[n].shape for n in SMALL] + [(3, D_FF)])
    for n, g in zip(SMALL, summed[:-1]):
        reduced[n] = g
    reduced["conv_w"] = lax.dynamic_slice_in_dim(summed[-1], s_idx[0] * SHARD_FF, SHARD_FF, axis=1)[None]

    out_g, out_d, out_m, out_v = {}, {}, {}, {}
    for n in LARGE:
        shp = params[n].shape
        g, d, nm, nv = _adamw(params[n][0], reduced[n], mom[n][0], var[n][0], name=f"adamw_{n}")
        out_g[n], out_d[n], out_m[n], out_v[n] = g.reshape(shp), d.reshape(shp), nm.reshape(shp), nv.reshape(shp)
    small_names = SMALL + ["conv_w"]
    rows_s = -(-sum(int(np.prod(params[n].shape)) for n in small_names) // (8 * 128)) * 8
    _, d, nm, nv = _adamw(_pack([params[n] for n in small_names], rows_s),
                          _pack([reduced[n] for n in small_names], rows_s),
                          _pack([mom[n] for n in small_names], rows_s), _pack([var[n] for n in small_names], rows_s),
                          name="adamw_small")
    shapes = [params[n].shape for n in small_names]
    for n, dd, mm, vv in zip(small_names, _unpack(d, shapes), _unpack(nm, shapes), _unpack(nv, shapes)):
        out_g[n], out_d[n], out_m[n], out_v[n] = reduced[n], dd, mm, vv

    return (loss, grad_x, *[out_g[n] for n in ORDER], *[out_d[n] for n in ORDER],
            *[out_m[n] for n in ORDER], *[out_v[n] for n in ORDER])
```

```python
import functools
import math

import numpy as np
import jax
import jax.numpy as jnp
from jax import lax
from jax.experimental import pallas as pl
from jax.experimental.pallas import tpu as pltpu

F32 = jnp.float32
BF16 = jnp.bfloat16
MESH = pl.DeviceIdType.MESH

D_MODEL = 1024
SEQ = 2048
HEAD_DIM = 64
A_GROUPS = 4
A_WIDTH = 256
B_HEADS = 12
B_WIDTH = 768
CHUNK = 128
DILATED = ((128, 1), (512, 4), (2048, 16))
NUM_BUCKETS = 32
MAX_DISTANCE = 2048
D_FF = 2816
IN_COLS = 2816
NORM_EPS = 1e-6
NEG_INF = -1e30
N_SHARD = 4
SHARD_FF = D_FF // N_SHARD
LANE_BLOCK = 256
VMEM_LIMIT = 56 * 1024 * 1024

ADAM_LR = 0.001
ADAM_B1 = 0.9
ADAM_B2 = 0.999
ADAM_EPS = 1e-08
ADAM_WD = 0.01
ADAM_STEP = 10

GELU_C = math.sqrt(2.0 / math.pi)
GELU_A = 0.044715


def _params(sem=None):
    return pltpu.CompilerParams(dimension_semantics=sem, vmem_limit_bytes=VMEM_LIMIT)


def _dot(a, b, precision=None):
    return jnp.dot(a, b, preferred_element_type=F32, precision=precision)


def _dot_nt(a, b):
    return lax.dot_general(a, b, (((1,), (1,)), ((), ())), preferred_element_type=F32)


def _dot_tn(a, b):
    return lax.dot_general(a, b, (((0,), (0,)), ((), ())), preferred_element_type=F32)


def _gelu(x):
    t = jnp.tanh(GELU_C * (x + GELU_A * (x * x * x)))
    return 0.5 * x * (1.0 + t)


def _gelu_and_grad(x):
    x2 = x * x
    t = jnp.tanh(GELU_C * (x + GELU_A * (x2 * x)))
    g = 0.5 * x * (1.0 + t)
    dg = 0.5 * (1.0 + t) + 0.5 * x * (1.0 - t * t) * (GELU_C * (1.0 + 3.0 * GELU_A * x2))
    return g, dg


def _mm(a, b, *, dims, tm, tn, tk, out_dtype, name, add=None):
    if dims == "nn":
        m, k = a.shape
        n = b.shape[1]
        a_spec = pl.BlockSpec((tm, tk), lambda i, j, kk: (i, kk))
        b_spec = pl.BlockSpec((tk, tn), lambda i, j, kk: (kk, j))
        dot = _dot
    elif dims == "nt":
        m, k = a.shape
        n = b.shape[0]
        a_spec = pl.BlockSpec((tm, tk), lambda i, j, kk: (i, kk))
        b_spec = pl.BlockSpec((tn, tk), lambda i, j, kk: (j, kk))
        dot = _dot_nt
    else:
        k, m = a.shape
        n = b.shape[1]
        a_spec = pl.BlockSpec((tk, tm), lambda i, j, kk: (kk, i))
        b_spec = pl.BlockSpec((tk, tn), lambda i, j, kk: (kk, j))
        dot = _dot_tn
    assert m % tm == 0 and n % tn == 0 and k % tk == 0, (name, m, n, k)
    nk = k // tk
    has_add = add is not None

    def body(*refs):
        if has_add:
            a_ref, b_ref, add_ref, o_ref, acc_ref = refs
        else:
            a_ref, b_ref, o_ref, acc_ref = refs
        kk = pl.program_id(2)

        @pl.when(kk == 0)
        def _():
            if has_add:
                acc_ref[...] = add_ref[...].astype(F32)
            else:
                acc_ref[...] = jnp.zeros_like(acc_ref)

        acc_ref[...] += dot(a_ref[...].astype(BF16), b_ref[...].astype(BF16))

        @pl.when(kk == nk - 1)
        def _():
            o_ref[...] = acc_ref[...].astype(out_dtype)

    in_specs = [a_spec, b_spec]
    args = [a, b]
    if has_add:
        in_specs.append(pl.BlockSpec((tm, tn), lambda i, j, kk: (i, j)))
        args.append(add)
    return pl.pallas_call(
        body,
        grid=(m // tm, n // tn, nk),
        in_specs=in_specs,
        out_specs=pl.BlockSpec((tm, tn), lambda i, j, kk: (i, j)),
        out_shape=jax.ShapeDtypeStruct((m, n), out_dtype),
        scratch_shapes=[pltpu.VMEM((tm, tn), F32)],
        compiler_params=_params(("parallel", "parallel", "arbitrary")),
        name=name,
    )(*args)


ROW_TILE = 512


def _row_spec(width=D_MODEL):
    return pl.BlockSpec((ROW_TILE, width), lambda i: (i, 0))


def _vec_spec(width=D_MODEL):
    return pl.BlockSpec((1, width), lambda i: (0, 0))


def _rstd(v):
    return lax.rsqrt(jnp.mean(v * v, axis=-1, keepdims=True) + NORM_EPS)


def _rms_fwd(x, g):
    m = x.shape[0]

    def body(x_ref, g_ref, h_ref):
        xv = x_ref[...]
        h_ref[...] = (xv * _rstd(xv) * g_ref[...]).astype(BF16)

    return pl.pallas_call(
        body, grid=(m // ROW_TILE,),
        in_specs=[_row_spec(), _vec_spec()],
        out_specs=_row_spec(),
        out_shape=jax.ShapeDtypeStruct((m, D_MODEL), BF16),
        compiler_params=_params(("parallel",)), name="rms_fwd",
    )(x, g)


def _mid_fwd(x0, y1, g2, g3):
    m = x0.shape[0]

    def body(x0_ref, y1_ref, g2_ref, g3_ref, x1_ref, h2_ref):
        y1v = y1_ref[...]
        x1 = x0_ref[...] + y1v * _rstd(y1v) * g2_ref[...]
        x1_ref[...] = x1
        h2_ref[...] = (x1 * _rstd(x1) * g3_ref[...]).astype(BF16)

    return pl.pallas_call(
        body, grid=(m // ROW_TILE,),
        in_specs=[_row_spec(), _row_spec(), _vec_spec(), _vec_spec()],
        out_specs=[_row_spec(), _row_spec()],
        out_shape=[jax.ShapeDtypeStruct((m, D_MODEL), F32), jax.ShapeDtypeStruct((m, D_MODEL), BF16)],
        compiler_params=_params(("parallel",)), name="mid_fwd",
    )(x0, y1, g2, g3)


def _rms_bwd_rows(dout, v, g):
    r = _rstd(v)
    n = v * r
    dn = dout * g
    dv = r * (dn - n * jnp.mean(dn * n, axis=-1, keepdims=True))
    dg = jnp.sum(dout * n, axis=0, keepdims=True)
    return dv, dg


def _loss_head(x1, y2, tgt, g4):
    m = x1.shape[0]

    def body(x1_ref, y2_ref, t_ref, g4_ref, dx2_ref, dy2_ref, dg4_ref, loss_ref):
        i = pl.program_id(0)

        @pl.when(i == 0)
        def _():
            dg4_ref[...] = jnp.zeros_like(dg4_ref)
            loss_ref[...] = jnp.zeros_like(loss_ref)

        y2v = y2_ref[...]
        g4 = g4_ref[...]
        x2 = x1_ref[...] + y2v * _rstd(y2v) * g4
        err = x2 - t_ref[...]
        loss_ref[...] += 0.5 * jnp.sum(jnp.mean(err * err, axis=-1, keepdims=True), axis=0, keepdims=True)
        dx2 = err * (1.0 / D_MODEL)
        dx2_ref[...] = dx2
        dy2, dg4 = _rms_bwd_rows(dx2, y2v, g4)
        dy2_ref[...] = dy2.astype(BF16)
        dg4_ref[...] += dg4

    return pl.pallas_call(
        body, grid=(m // ROW_TILE,),
        in_specs=[_row_spec(), _row_spec(), _row_spec(), _vec_spec()],
        out_specs=[_row_spec(), _row_spec(), _vec_spec(), pl.BlockSpec((1, 1), lambda i: (0, 0))],
        out_shape=[jax.ShapeDtypeStruct((m, D_MODEL), F32), jax.ShapeDtypeStruct((m, D_MODEL), BF16),
                   jax.ShapeDtypeStruct((1, D_MODEL), F32), jax.ShapeDtypeStruct((1, 1), F32)],
        compiler_params=_params(("arbitrary",)), name="loss_head",
    )(x1, y2, tgt, g4)


def _mid_bwd(x1, y1, dh2, dx2, g2, g3):
    m = x1.shape[0]

    def body(x1_ref, y1_ref, dh2_ref, dx2_ref, g2_ref, g3_ref, dx1_ref, dy1_ref, dg2_ref, dg3_ref):
        i = pl.program_id(0)

        @pl.when(i == 0)
        def _():
            dg2_ref[...] = jnp.zeros_like(dg2_ref)
            dg3_ref[...] = jnp.zeros_like(dg3_ref)

        d3, dg3 = _rms_bwd_rows(dh2_ref[...], x1_ref[...], g3_ref[...])
        dx1 = dx2_ref[...] + d3
        dx1_ref[...] = dx1
        dy1, dg2 = _rms_bwd_rows(dx1, y1_ref[...], g2_ref[...])
        dy1_ref[...] = dy1.astype(BF16)
        dg2_ref[...] += dg2
        dg3_ref[...] += dg3

    return pl.pallas_call(
        body, grid=(m // ROW_TILE,),
        in_specs=[_row_spec(), _row_spec(), _row_spec(), _row_spec(), _vec_spec(), _vec_spec()],
        out_specs=[_row_spec(), _row_spec(), _vec_spec(), _vec_spec()],
        out_shape=[jax.ShapeDtypeStruct((m, D_MODEL), F32), jax.ShapeDtypeStruct((m, D_MODEL), BF16),
                   jax.ShapeDtypeStruct((1, D_MODEL), F32), jax.ShapeDtypeStruct((1, D_MODEL), F32)],
        compiler_params=_params(("arbitrary",)), name="mid_bwd",
    )(x1, y1, dh2, dx2, g2, g3)


def _in_bwd(x0, dh1, dx1, g1):
    m = x0.shape[0]

    def body(x0_ref, dh1_ref, dx1_ref, g1_ref, dx0_ref, dg1_ref):
        i = pl.program_id(0)

        @pl.when(i == 0)
        def _():
            dg1_ref[...] = jnp.zeros_like(dg1_ref)

        d1, dg1 = _rms_bwd_rows(dh1_ref[...], x0_ref[...], g1_ref[...])
        dx0_ref[...] = dx1_ref[...] + d1
        dg1_ref[...] += dg1

    return pl.pallas_call(
        body, grid=(m // ROW_TILE,),
        in_specs=[_row_spec(), _row_spec(), _row_spec(), _vec_spec()],
        out_specs=[_row_spec(), _vec_spec()],
        out_shape=[jax.ShapeDtypeStruct((m, D_MODEL), F32), jax.ShapeDtypeStruct((1, D_MODEL), F32)],
        compiler_params=_params(("arbitrary",)), name="in_bwd",
    )(x0, dh1, dx1, g1)


GATE_ROWS = 512


def _group_mean_matrix():
    p = np.zeros((A_WIDTH, A_WIDTH), np.float32)
    for g in range(A_GROUPS):
        p[g * HEAD_DIM:(g + 1) * HEAD_DIM, g * HEAD_DIM:(g + 1) * HEAD_DIM] = 1.0 / HEAD_DIM
    return jnp.asarray(p)


def _group_masks(width=A_WIDTH):
    lane = lax.broadcasted_iota(jnp.int32, (1, width), 1)
    return [(lane >= g * HEAD_DIM) & (lane < (g + 1) * HEAD_DIM) for g in range(width // HEAD_DIM)]


def _layernorm_groups(vg, pavg):
    hi = lax.Precision.HIGHEST
    mu = _dot(vg, pavg, hi)
    xc = vg - mu
    var = _dot(xc * xc, pavg, hi)
    rstd = lax.rsqrt(var + NORM_EPS)
    return xc * rstd, rstd


def _spatial_mix(w_bf, vn_chunk_bf, masks, bz):
    z = bz
    for g in range(A_GROUPS):
        z = z + jnp.where(masks[g], _dot(w_bf[g], vn_chunk_bf), 0.0)
    return z


def _gate_fwd(proj, ln_g, ln_b, w_s, bz):
    m = proj.shape[0]
    pavg = _group_mean_matrix()

    def body(u_ref, v_ref, lg_ref, lb_ref, w_ref, bz_ref, p_ref, a_ref):
        masks = _group_masks()
        row = lax.broadcasted_iota(jnp.int32, (CHUNK, CHUNK), 0)
        col = lax.broadcasted_iota(jnp.int32, (CHUNK, CHUNK), 1)
        w_bf = [jnp.where(row >= col, w_ref[g], 0.0).astype(BF16) for g in range(A_GROUPS)]
        ug = _gelu(u_ref[...])
        vhat, _ = _layernorm_groups(_gelu(v_ref[...]), p_ref[...])
        vn = vhat * lg_ref[...] + lb_ref[...]
        bz = bz_ref[...]
        for c in range(GATE_ROWS // CHUNK):
            sl = slice(c * CHUNK, (c + 1) * CHUNK)
            z = _spatial_mix(w_bf, vn[sl].astype(BF16), masks, bz)
            a_ref[sl, :] = (ug[sl] * z).astype(BF16)

    full = lambda shape: pl.BlockSpec(shape, lambda i: tuple(0 for _ in shape))
    return pl.pallas_call(
        body, grid=(m // GATE_ROWS,),
        in_specs=[pl.BlockSpec((GATE_ROWS, A_WIDTH), lambda i: (i, 0)),
                  pl.BlockSpec((GATE_ROWS, A_WIDTH), lambda i: (i, 1)),
                  full((1, A_WIDTH)), full((1, A_WIDTH)), full((A_GROUPS, CHUNK, CHUNK)),
                  full((CHUNK, A_WIDTH)), full((A_WIDTH, A_WIDTH))],
        out_specs=pl.BlockSpec((GATE_ROWS, A_WIDTH), lambda i: (i, 0)),
        out_shape=jax.ShapeDtypeStruct((m, A_WIDTH), BF16),
        compiler_params=_params(("parallel",)), name="gate_fwd",
    )(proj, proj, ln_g, ln_b, w_s, bz, pavg)


def _gate_bwd(proj, dmix, ln_g, ln_b, w_s, w_st, bz):
    m = proj.shape[0]
    pavg = _group_mean_matrix()
    nsteps = m // GATE_ROWS

    def body(u_ref, v_ref, da_ref, lg_ref, lb_ref, w_ref, wt_ref, bz_ref, p_ref,
             duv_ref, dlg_ref, dlb_ref, dw_ref, dbz_ref):
        i = pl.program_id(0)

        @pl.when(i == 0)
        def _():
            dlg_ref[...] = jnp.zeros_like(dlg_ref)
            dlb_ref[...] = jnp.zeros_like(dlb_ref)
            dw_ref[...] = jnp.zeros_like(dw_ref)
            dbz_ref[...] = jnp.zeros_like(dbz_ref)

        hi = lax.Precision.HIGHEST
        masks = _group_masks()
        row = lax.broadcasted_iota(jnp.int32, (CHUNK, CHUNK), 0)
        col = lax.broadcasted_iota(jnp.int32, (CHUNK, CHUNK), 1)
        tril = row >= col
        w_bf = [jnp.where(tril, w_ref[g], 0.0).astype(BF16) for g in range(A_GROUPS)]
        wt_bf = [jnp.where(col >= row, wt_ref[g], 0.0).astype(BF16) for g in range(A_GROUPS)]
        pavg_v = p_ref[...]
        lg = lg_ref[...]
        ug, dug = _gelu_and_grad(u_ref[...])
        vg, dvg_dx = _gelu_and_grad(v_ref[...])
        vhat, rstd = _layernorm_groups(vg, pavg_v)
        vn = vhat * lg + lb_ref[...]
        da = da_ref[...]
        bz = bz_ref[...]
        for c in range(GATE_ROWS // CHUNK):
            sl = slice(c * CHUNK, (c + 1) * CHUNK)
            vn_bf = vn[sl].astype(BF16)
            z = _spatial_mix(w_bf, vn_bf, masks, bz)
            dz = da[sl] * ug[sl]
            duv_ref[sl, 0:A_WIDTH] = da[sl] * z * dug[sl]
            dbz_ref[...] += dz
            dz_bf = dz.astype(BF16)
            dvn = jnp.zeros((CHUNK, A_WIDTH), F32)
            for g in range(A_GROUPS):
                dz_g = jnp.where(masks[g], dz, 0.0).astype(BF16)
                dw_ref[g] += jnp.where(tril, _dot_nt(dz_g, vn_bf), 0.0)
                dvn = dvn + jnp.where(masks[g], _dot(wt_bf[g], dz_bf), 0.0)
            vh = vhat[sl]
            dlb_ref[...] += jnp.sum(dvn, axis=0, keepdims=True)
            dlg_ref[...] += jnp.sum(dvn * vh, axis=0, keepdims=True)
            dvh = dvn * lg
            m1 = _dot(dvh, pavg_v, hi)
            m2 = _dot(dvh * vh, pavg_v, hi)
            duv_ref[sl, A_WIDTH:2 * A_WIDTH] = rstd[sl] * (dvh - m1 - vh * m2) * dvg_dx[sl]

        @pl.when(i == nsteps - 1)
        def _():
            dbz_ref[...] = _dot(dbz_ref[...], pavg_v * float(HEAD_DIM), hi)

    full = lambda shape: pl.BlockSpec(shape, lambda i: tuple(0 for _ in shape))
    return pl.pallas_call(
        body, grid=(nsteps,),
        in_specs=[pl.BlockSpec((GATE_ROWS, A_WIDTH), lambda i: (i, 0)),
                  pl.BlockSpec((GATE_ROWS, A_WIDTH), lambda i: (i, 1)),
                  pl.BlockSpec((GATE_ROWS, A_WIDTH), lambda i: (i, 0)),
                  full((1, A_WIDTH)), full((1, A_WIDTH)), full((A_GROUPS, CHUNK, CHUNK)),
                  full((A_GROUPS, CHUNK, CHUNK)), full((CHUNK, A_WIDTH)), full((A_WIDTH, A_WIDTH))],
        out_specs=[pl.BlockSpec((GATE_ROWS, 2 * A_WIDTH), lambda i: (i, 0)),
                   full((1, A_WIDTH)), full((1, A_WIDTH)), full((A_GROUPS, CHUNK, CHUNK)),
                   full((CHUNK, A_WIDTH))],
        out_shape=[jax.ShapeDtypeStruct((m, 2 * A_WIDTH), F32),
                   jax.ShapeDtypeStruct((1, A_WIDTH), F32), jax.ShapeDtypeStruct((1, A_WIDTH), F32),
                   jax.ShapeDtypeStruct((A_GROUPS, CHUNK, CHUNK), F32),
                   jax.ShapeDtypeStruct((CHUNK, A_WIDTH), F32)],
        compiler_params=_params(("arbitrary",)), name="gate_bwd",
    )(proj, proj, dmix, ln_g, ln_b, w_s, w_st, bz, pavg)


Q_BLOCK = 128
Q_COL, K_COL, V_COL = 2, 5, 8
N_COLBLK = IN_COLS // LANE_BLOCK
HEAD_BLOCKS = B_WIDTH // LANE_BLOCK
HEADS_PER_BLOCK = LANE_BLOCK // HEAD_DIM


def _t5_bucket_np(dist, dtype):
    max_exact = NUM_BUCKETS // 2
    d = np.maximum(dist, 1).astype(dtype)
    large = max_exact + (np.log(d / dtype(max_exact)) / dtype(math.log(MAX_DISTANCE / max_exact))
                         * dtype(NUM_BUCKETS - max_exact))
    large = np.minimum(large.astype(np.int32), NUM_BUCKETS - 1)
    return np.where(dist < max_exact, dist, large)


def _bucket_tables():
    i = np.arange(Q_BLOCK)[:, None]
    j = np.arange(Q_BLOCK)[None, :]
    tables = []
    for _, dil in DILATED:
        rel_prev = Q_BLOCK + i - j
        rel_cur = i - j
        rel = np.concatenate([rel_prev, rel_cur], axis=1)
        valid = np.concatenate([rel_prev <= Q_BLOCK, rel_cur >= 0], axis=1)
        dist = np.maximum(rel, 0) * dil
        b32 = _t5_bucket_np(dist, np.float32)
        b64 = _t5_bucket_np(dist, np.float64)
        assert np.array_equal(b32, b64)
        tables.append(np.where(valid, b32, -1).astype(np.int32))
    return np.stack(tables)


def _bias_tables(rel_bias, buckets_np):
    present = [sorted(set(int(v) for v in np.unique(buckets_np[c]) if v >= 0)) for c in range(len(DILATED))]

    def body(rb_ref, bk_ref, o_ref):
        for c in range(len(DILATED)):
            bk = bk_ref[c]
            for h in range(B_HEADS):
                acc = jnp.full((Q_BLOCK, 2 * Q_BLOCK), NEG_INF, F32)
                for b in present[c]:
                    acc = jnp.where(bk == b, rb_ref[b, h], acc)
                o_ref[c, h] = acc

    return pl.pallas_call(
        body,
        in_specs=[pl.BlockSpec(memory_space=pltpu.SMEM), pl.BlockSpec(memory_space=pltpu.VMEM)],
        out_specs=pl.BlockSpec(memory_space=pltpu.VMEM),
        out_shape=jax.ShapeDtypeStruct((len(DILATED), B_HEADS, Q_BLOCK, 2 * Q_BLOCK), F32),
        compiler_params=_params(), name="bias_tables",
    )(rel_bias, jnp.asarray(buckets_np))


def _head_masks():
    lane = lax.broadcasted_iota(jnp.int32, (1, LANE_BLOCK), 1)
    return [(lane >= h * HEAD_DIM) & (lane < (h + 1) * HEAD_DIM) for h in range(HEADS_PER_BLOCK)]


def _attn_fwd(proj, bias, dil, batch):
    m = proj.shape[0]
    tr = SEQ // dil
    nb = tr // Q_BLOCK
    proj3 = proj.reshape(batch, tr, dil * IN_COLS)

    def body(q_ref, k_ref, v_ref, b_ref, o_ref, l_ref):
        masks = _head_masks()

        def block(n, carry):
            r0 = pl.multiple_of(n * Q_BLOCK, Q_BLOCK)
            rows = pl.ds(r0, Q_BLOCK)
            q = q_ref[rows, :] * 0.125
            kc = k_ref[rows, :].astype(BF16)
            vc = v_ref[rows, :].astype(BF16)
            if nb > 1:
                p0 = pl.multiple_of(jnp.maximum(n - 1, 0) * Q_BLOCK, Q_BLOCK)
                kp = k_ref[pl.ds(p0, Q_BLOCK), :].astype(BF16)
                vp = v_ref[pl.ds(p0, Q_BLOCK), :].astype(BF16)
            o_acc = jnp.zeros((Q_BLOCK, LANE_BLOCK), F32)
            l_acc = jnp.zeros((Q_BLOCK, LANE_BLOCK), F32)
            for h in range(HEADS_PER_BLOCK):
                qh = jnp.where(masks[h], q, 0.0).astype(BF16)
                sc = _dot_nt(qh, kc) + b_ref[h, :, Q_BLOCK:]
                mx = jnp.max(sc, axis=1, keepdims=True)
                if nb > 1:
                    sp = _dot_nt(qh, kp) + jnp.where(n == 0, NEG_INF, b_ref[h, :, :Q_BLOCK])
                    mx = jnp.maximum(mx, jnp.max(sp, axis=1, keepdims=True))
                pc = jnp.exp(sc - mx)
                den = jnp.sum(pc, axis=1, keepdims=True)
                oh = _dot(pc.astype(BF16), vc)
                if nb > 1:
                    pp = jnp.exp(sp - mx)
                    den = den + jnp.sum(pp, axis=1, keepdims=True)
                    oh = oh + _dot(pp.astype(BF16), vp)
                o_acc = jnp.where(masks[h], oh / den, o_acc)
                l_acc = jnp.where(masks[h], mx + jnp.log(den), l_acc)
            o_ref[rows, :] = o_acc
            l_ref[rows, :] = l_acc
            return carry

        if nb > 1:
            lax.fori_loop(0, nb, block, 0)
        else:
            block(0, 0)

    def in_spec(col0):
        return pl.BlockSpec((None, tr, LANE_BLOCK), lambda b, r, g: (b, 0, r * N_COLBLK + col0 + g))

    out_spec = pl.BlockSpec((None, tr, LANE_BLOCK), lambda b, r, g: (b, 0, r * HEAD_BLOCKS + g))
    out_sds = jax.ShapeDtypeStruct((batch, tr, dil * B_WIDTH), F32)
    o, lse = pl.pallas_call(
        body, grid=(batch, dil, HEAD_BLOCKS),
        in_specs=[in_spec(Q_COL), in_spec(K_COL), in_spec(V_COL),
                  pl.BlockSpec((HEADS_PER_BLOCK, Q_BLOCK, 2 * Q_BLOCK), lambda b, r, g: (g, 0, 0))],
        out_specs=[out_spec, out_spec],
        out_shape=[out_sds, out_sds],
        compiler_params=_params(("parallel", "parallel", "parallel")), name=f"attn_fwd_d{dil}",
    )(proj3, proj3, proj3, bias)
    return o.reshape(m, B_WIDTH), lse.reshape(m, B_WIDTH)


def _attn_combine(outs, lses):
    m = outs[0].shape[0]
    nc = len(outs)

    def body(*refs):
        o_refs, l_refs = refs[:nc], refs[nc:2 * nc]
        of_ref, ob_ref, lt_ref = refs[2 * nc:]
        ls = [r[...] for r in l_refs]
        mx = functools.reduce(jnp.maximum, ls)
        ws = [jnp.exp(l - mx) for l in ls]
        tot = functools.reduce(lambda a, b: a + b, ws)
        inv = 1.0 / tot
        o = functools.reduce(lambda a, b: a + b, [w * inv * r[...] for w, r in zip(ws, o_refs)])
        of_ref[...] = o
        ob_ref[...] = o.astype(BF16)
        lt_ref[...] = mx + jnp.log(tot)

    spec = pl.BlockSpec((ROW_TILE, B_WIDTH), lambda i: (i, 0))
    return pl.pallas_call(
        body, grid=(m // ROW_TILE,),
        in_specs=[spec] * (2 * nc), out_specs=[spec, spec, spec],
        out_shape=[jax.ShapeDtypeStruct((m, B_WIDTH), F32), jax.ShapeDtypeStruct((m, B_WIDTH), BF16),
                   jax.ShapeDtypeStruct((m, B_WIDTH), F32)],
        compiler_params=_params(("parallel",)), name="attn_combine",
    )(*outs, *lses)


def _attn_bwd(proj, dmix, o, lse, bias, dil, batch):
    m = proj.shape[0]
    tr = SEQ // dil
    nb = tr // Q_BLOCK
    proj3 = proj.reshape(batch, tr, dil * IN_COLS)
    dmix3 = dmix.reshape(batch, tr, dil * D_MODEL)
    o3 = o.reshape(batch, tr, dil * B_WIDTH)
    l3 = lse.reshape(batch, tr, dil * B_WIDTH)

    def body(q_ref, k_ref, v_ref, do_ref, o_ref, l_ref, b_ref, dq_ref, dk_ref, dv_ref, ds_ref):
        first = (pl.program_id(1) == 0) & (pl.program_id(2) == 0)

        @pl.when(first)
        def _():
            ds_ref[...] = jnp.zeros_like(ds_ref)

        dk_ref[...] = jnp.zeros_like(dk_ref)
        dv_ref[...] = jnp.zeros_like(dv_ref)
        masks = _head_masks()

        def block(n, carry):
            r0 = pl.multiple_of(n * Q_BLOCK, Q_BLOCK)
            rows = pl.ds(r0, Q_BLOCK)
            q = q_ref[rows, :] * 0.125
            kc = k_ref[rows, :].astype(BF16)
            vc = v_ref[rows, :].astype(BF16)
            do = do_ref[rows, :]
            ov = o_ref[rows, :]
            lv = l_ref[rows, :]
            if nb > 1:
                p0 = pl.multiple_of(jnp.maximum(n - 1, 0) * Q_BLOCK, Q_BLOCK)
                prow = pl.ds(p0, Q_BLOCK)
                kp = k_ref[prow, :].astype(BF16)
                vp = v_ref[prow, :].astype(BF16)
                dkp = jnp.zeros((Q_BLOCK, LANE_BLOCK), F32)
                dvp = jnp.zeros((Q_BLOCK, LANE_BLOCK), F32)
            dq = jnp.zeros((Q_BLOCK, LANE_BLOCK), F32)
            dkc = jnp.zeros((Q_BLOCK, LANE_BLOCK), F32)
            dvc = jnp.zeros((Q_BLOCK, LANE_BLOCK), F32)
            for h in range(HEADS_PER_BLOCK):
                qh = jnp.where(masks[h], q, 0.0).astype(BF16)
                doh = jnp.where(masks[h], do, 0.0)
                doh_bf = doh.astype(BF16)
                lrow = jnp.max(jnp.where(masks[h], lv, -3e38), axis=1, keepdims=True)
                drow = jnp.sum(doh * ov, axis=1, keepdims=True)
                pc = jnp.exp(_dot_nt(qh, kc) + b_ref[h, :, Q_BLOCK:] - lrow)
                dsc = pc * (_dot_nt(doh_bf, vc) - drow)
                ds_ref[h, :, Q_BLOCK:] += dsc
                dsc_bf = dsc.astype(BF16)
                dqh = _dot(dsc_bf, kc)
                dkc = dkc + _dot_tn(dsc_bf, qh)
                dvc = dvc + _dot_tn(pc.astype(BF16), doh_bf)
                if nb > 1:
                    bp = jnp.where(n == 0, NEG_INF, b_ref[h, :, :Q_BLOCK])
                    pp = jnp.exp(_dot_nt(qh, kp) + bp - lrow)
                    dsp = pp * (_dot_nt(doh_bf, vp) - drow)
                    ds_ref[h, :, :Q_BLOCK] += dsp
                    dsp_bf = dsp.astype(BF16)
                    dqh = dqh + _dot(dsp_bf, kp)
                    dkp = dkp + _dot_tn(dsp_bf, qh)
                    dvp = dvp + _dot_tn(pp.astype(BF16), doh_bf)
                dq = jnp.where(masks[h], dqh, dq)
            dq_ref[rows, :] = dq * 0.125
            dk_ref[rows, :] += dkc
            dv_ref[rows, :] += dvc
            if nb > 1:
                dk_ref[prow, :] += dkp
                dv_ref[prow, :] += dvp
            return carry

        if nb > 1:
            lax.fori_loop(0, nb, block, 0)
        else:
            block(0, 0)

    def in_spec(col0):
        return pl.BlockSpec((None, tr, LANE_BLOCK), lambda g, b, r: (b, 0, r * N_COLBLK + col0 + g))

    do_spec = pl.BlockSpec((None, tr, LANE_BLOCK), lambda g, b, r: (b, 0, r * (D_MODEL // LANE_BLOCK) + 1 + g))
    hd_spec = pl.BlockSpec((None, tr, LANE_BLOCK), lambda g, b, r: (b, 0, r * HEAD_BLOCKS + g))
    tbl_spec = pl.BlockSpec((HEADS_PER_BLOCK, Q_BLOCK, 2 * Q_BLOCK), lambda g, b, r: (g, 0, 0))
    out_sds = jax.ShapeDtypeStruct((batch, tr, dil * B_WIDTH), F32)
    dq, dk, dv, ds = pl.pallas_call(
        body, grid=(HEAD_BLOCKS, batch, dil),
        in_specs=[in_spec(Q_COL), in_spec(K_COL), in_spec(V_COL), do_spec, hd_spec, hd_spec, tbl_spec],
        out_specs=[hd_spec, hd_spec, hd_spec, tbl_spec],
        out_shape=[out_sds, out_sds, out_sds, jax.ShapeDtypeStruct((B_HEADS, Q_BLOCK, 2 * Q_BLOCK), F32)],
        compiler_params=_params(("parallel", "arbitrary", "arbitrary")), name=f"attn_bwd_d{dil}",
    )(proj3, proj3, proj3, dmix3, o3, l3, bias)
    return dq.reshape(m, B_WIDTH), dk.reshape(m, B_WIDTH), dv.reshape(m, B_WIDTH), ds


PAIR = 2 * HEAD_DIM
N_PAIR = B_HEADS // 2
N_CFG = len(DILATED)
BLOCKS_PER_CFG = SEQ // Q_BLOCK
QKV_SLABS = 3 * N_PAIR
FWD_BLOCKS_PER_TRIP = 8
BWD_BLOCKS_PER_TRIP = 4


def _proj_fwd(x, g1, w_in_t):
    m = x.shape[0]
    tm = ROW_TILE

    def body(x_ref, g_ref, w_ref, h_ref, uv_ref, qkv_ref):
        xv = x_ref[...]
        h = (xv * _rstd(xv) * g_ref[...]).astype(BF16)
        h_ref[...] = h
        acc = _dot_nt(h, w_ref[...])
        uv_ref[...] = acc[:, :2 * A_WIDTH]
        for s in range(QKV_SLABS):
            qkv_ref[s] = acc[:, 2 * A_WIDTH + s * PAIR:2 * A_WIDTH + (s + 1) * PAIR]

    return pl.pallas_call(
        body, grid=(m // tm,),
        in_specs=[pl.BlockSpec((tm, D_MODEL), lambda i: (i, 0)), _vec_spec(),
                  pl.BlockSpec((IN_COLS, D_MODEL), lambda i: (0, 0))],
        out_specs=[pl.BlockSpec((tm, D_MODEL), lambda i: (i, 0)),
                   pl.BlockSpec((tm, 2 * A_WIDTH), lambda i: (i, 0)),
                   pl.BlockSpec((QKV_SLABS, tm, PAIR), lambda i: (0, i, 0))],
        out_shape=[jax.ShapeDtypeStruct((m, D_MODEL), BF16), jax.ShapeDtypeStruct((m, 2 * A_WIDTH), F32),
                   jax.ShapeDtypeStruct((QKV_SLABS, m, PAIR), F32)],
        compiler_params=_params(("parallel",)), name="proj_fwd",
    )(x, g1, w_in_t)


def _pair_masks():
    lane = lax.broadcasted_iota(jnp.int32, (1, PAIR), 1)
    return [lane < HEAD_DIM, lane >= HEAD_DIM]


def _block_rows(idx, dil):
    if dil == 1:
        n = idx
        cur = pl.ds(pl.multiple_of(n * Q_BLOCK, Q_BLOCK), Q_BLOCK)
        prev = pl.ds(pl.multiple_of(jnp.maximum(n - 1, 0) * Q_BLOCK, Q_BLOCK), Q_BLOCK)
        return n, cur, prev
    r = idx % dil
    n = idx // dil
    cur = pl.ds(r + (dil * Q_BLOCK) * n, Q_BLOCK, stride=dil)
    prev = pl.ds(r + (dil * Q_BLOCK) * jnp.maximum(n - 1, 0), Q_BLOCK, stride=dil)
    return n, cur, prev


def _attn_fwd(qkv, bias, batch, shards):
    m = qkv.shape[1]
    comb_rows = 256
    nt = len(shards)
    shapes = [sh.shape for sh in shards]
    n_steps = batch * N_PAIR
    early, late = list(range(nt // 2)), list(range(nt // 2, nt))

    def body(q_ref, k_ref, v_ref, b_ref, *rest):
        shard_refs = rest[:nt]
        o_ref, l_ref = rest[nt:nt + 2]
        gat_refs = rest[nt + 2:2 * nt + 2]
        scratch = rest[2 * nt + 2:]
        oc_refs, lc_refs = scratch[:N_CFG], scratch[N_CFG:2 * N_CFG]
        send_sems, recv_sems = scratch[2 * N_CFG:]
        step = pl.program_id(0) * N_PAIR + pl.program_id(1)
        gather = _GatherPlan(shapes, shard_refs, gat_refs, send_sems, recv_sems)

        @pl.when(step == 0)
        def _():
            gather.start(early + late)

        @pl.when(step == n_steps // 2)
        def _():
            gather.forward(early)

        @pl.when(step == n_steps - 2)
        def _():
            gather.forward(late)

        masks = _pair_masks()
        for ci, (_, dil) in enumerate(DILATED):
            nb = SEQ // dil // Q_BLOCK

            def block(trip, carry, ci=ci, dil=dil, nb=nb):
                work = []
                for u in range(FWD_BLOCKS_PER_TRIP):
                    n, rows, prow = _block_rows(trip * FWD_BLOCKS_PER_TRIP + u, dil)
                    q = q_ref[rows, :] * 0.125
                    kc = k_ref[rows, :].astype(BF16)
                    vc = v_ref[rows, :]
                    kp = k_ref[prow, :].astype(BF16) if nb > 1 else None
                    vp = v_ref[prow, :] if nb > 1 else None
                    tiles = []
                    for h in range(2):
                        qh = jnp.where(masks[h], q, 0.0).astype(BF16)
                        sc = _dot_nt(qh, kc) + b_ref[ci, h, :, Q_BLOCK:]
                        sp = None
                        if nb > 1:
                            sp = _dot_nt(qh, kp) + jnp.where(n == 0, NEG_INF, b_ref[ci, h, :, :Q_BLOCK])
                        tiles.append((sc, sp))
                    work.append((rows, vc, vp, tiles))
                probs = []
                for _, _, _, tiles in work:
                    ps = []
                    for sc, sp in tiles:
                        mx = jnp.max(sc if sp is None else jnp.maximum(sc, sp), axis=1, keepdims=True)
                        pc = jnp.exp(sc - mx).astype(BF16)
                        pp = None if sp is None else jnp.exp(sp - mx).astype(BF16)
                        ps.append((mx, pc, pp))
                    probs.append(ps)
                for (rows, vc, vp, _), ps in zip(work, probs):
                    res = []
                    for h, (_, pc, pp) in enumerate(ps):
                        r = _dot(pc, jnp.where(masks[h], vc, 1.0).astype(BF16))
                        if pp is not None:
                            r = r + _dot(pp, jnp.where(masks[h], vp, 1.0).astype(BF16))
                        res.append(r)
                    num = jnp.where(masks[0], res[0], res[1])
                    den = pltpu.roll(jnp.where(masks[0], res[1], res[0]), HEAD_DIM, 1)
                    oc_refs[ci][rows, :] = num / den
                    lc_refs[ci][rows, :] = jnp.where(masks[0], ps[0][0], ps[1][0]) + jnp.log(den)
                return carry

            lax.fori_loop(0, BLOCKS_PER_CFG // FWD_BLOCKS_PER_TRIP, block, 0)

        def combine(i, carry):
            rr = pl.ds(pl.multiple_of(i * comb_rows, comb_rows), comb_rows)
            ls = [lc_refs[c][rr, :] for c in range(N_CFG)]
            mx = functools.reduce(jnp.maximum, ls)
            ws = [jnp.exp(l - mx) for l in ls]
            tot = functools.reduce(lambda a, b: a + b, ws)
            o = functools.reduce(lambda a, b: a + b, [ws[c] * oc_refs[c][rr, :] for c in range(N_CFG)]) / tot
            o_ref[rr, :] = o.astype(BF16)
            l_ref[rr, :] = mx + jnp.log(tot)
            return carry

        lax.fori_loop(0, SEQ // comb_rows, combine, 0)

        @pl.when(step == n_steps - 1)
        def _():
            gather.finish(early + late)

    def slab(first):
        return pl.BlockSpec((None, SEQ, PAIR), lambda b, p: (first + p, b, 0))

    nat = pl.BlockSpec((SEQ, PAIR), lambda b, p: (b, p))
    res = pl.pallas_call(
        body, grid=(batch, N_PAIR),
        in_specs=[slab(0), slab(N_PAIR), slab(2 * N_PAIR),
                  pl.BlockSpec((N_CFG, 2, Q_BLOCK, 2 * Q_BLOCK), lambda b, p: (0, p, 0, 0))] + [ANY] * nt,
        out_specs=[nat, nat] + [ANY] * nt,
        out_shape=[jax.ShapeDtypeStruct((m, B_WIDTH), BF16), jax.ShapeDtypeStruct((m, B_WIDTH), F32)]
        + [jax.ShapeDtypeStruct((N_SHARD,) + sh.shape, sh.dtype) for sh in shards],
        scratch_shapes=[pltpu.VMEM((SEQ, PAIR), F32)] * (2 * N_CFG)
        + [pltpu.SemaphoreType.DMA((6 * nt,)), pltpu.SemaphoreType.DMA((6 * nt,))],
        compiler_params=_params(("arbitrary", "arbitrary")), name="attn_fwd",
    )(qkv, qkv, qkv, bias, *shards)
    return res[0], res[1], list(res[2:])


def _attn_bwd(qkv, dmix, o, lse, bias, batch, parts):
    m = qkv.shape[1]
    nt = len(parts)
    n_steps = N_PAIR * batch

    def body(q_ref, k_ref, v_ref, do_ref, o_ref, l_ref, b_ref, *rest):
        part_refs = rest[:nt]
        dqkv_ref, ds_ref = rest[nt:nt + 2]
        recv_refs = rest[nt + 2:2 * nt + 2]
        dq_acc, dk_acc, dv_acc, d_scr, send_sems, recv_sems = rest[2 * nt + 2:]
        step = pl.program_id(0) * batch + pl.program_id(1)
        exchange = _ChipExchangePlan(part_refs, recv_refs, send_sems, recv_sems)

        @pl.when(step == 0)
        def _():
            exchange.start()

        @pl.when(pl.program_id(1) == 0)
        def _():
            ds_ref[...] = jnp.zeros_like(ds_ref)

        dq_acc[...] = jnp.zeros_like(dq_acc)
        dk_acc[...] = jnp.zeros_like(dk_acc)
        dv_acc[...] = jnp.zeros_like(dv_acc)
        masks = _pair_masks()
        ri = lax.broadcasted_iota(jnp.int32, (PAIR, PAIR), 0)
        cj = lax.broadcasted_iota(jnp.int32, (PAIR, PAIR), 1)
        same_head = ((ri < HEAD_DIM) == (cj < HEAD_DIM)).astype(F32)
        d_scr[...] = _dot(do_ref[...] * o_ref[...].astype(F32), same_head, lax.Precision.HIGHEST)

        for ci, (_, dil) in enumerate(DILATED):
            nb = SEQ // dil // Q_BLOCK

            def block(trip, carry, ci=ci, dil=dil, nb=nb):
                first = []
                for u in range(BWD_BLOCKS_PER_TRIP):
                    n, rows, prow = _block_rows(trip * BWD_BLOCKS_PER_TRIP + u, dil)
                    q = q_ref[rows, :] * 0.125
                    kc = k_ref[rows, :].astype(BF16)
                    vc = v_ref[rows, :].astype(BF16)
                    do = do_ref[rows, :]
                    lv = l_ref[rows, :]
                    dv_ = d_scr[rows, :]
                    kp = k_ref[prow, :].astype(BF16) if nb > 1 else None
                    vp = v_ref[prow, :].astype(BF16) if nb > 1 else None
                    heads = []
                    for h in range(2):
                        c0 = h * HEAD_DIM
                        qh = jnp.where(masks[h], q, 0.0).astype(BF16)
                        doh = jnp.where(masks[h], do, 0.0).astype(BF16)
                        sc = _dot_nt(qh, kc)
                        dpc = _dot_nt(doh, vc)
                        sp = _dot_nt(qh, kp) if nb > 1 else None
                        dpp = _dot_nt(doh, vp) if nb > 1 else None
                        heads.append((qh, doh, lv[:, c0:c0 + 1], dv_[:, c0:c0 + 1], sc, dpc, sp, dpp))
                    first.append((n, rows, prow, kc, kp, heads))
                second = []
                for n, rows, prow, kc, kp, heads in first:
                    out = []
                    for h, (qh, doh, lrow, drow, sc, dpc, sp, dpp) in enumerate(heads):
                        pc = jnp.exp(sc + b_ref[ci, h, :, Q_BLOCK:] - lrow)
                        dsc = pc * (dpc - drow)
                        ds_ref[ci, h, :, Q_BLOCK:] += dsc
                        pp_bf = dsp_bf = None
                        if nb > 1:
                            pp = jnp.exp(sp + jnp.where(n == 0, NEG_INF, b_ref[ci, h, :, :Q_BLOCK]) - lrow)
                            dsp = pp * (dpp - drow)
                            ds_ref[ci, h, :, :Q_BLOCK] += dsp
                            pp_bf, dsp_bf = pp.astype(BF16), dsp.astype(BF16)
                        out.append((qh, doh, pc.astype(BF16), dsc.astype(BF16), pp_bf, dsp_bf))
                    second.append((rows, prow, kc, kp, out))
                for rows, prow, kc, kp, out in second:
                    dq = jnp.zeros((Q_BLOCK, PAIR), F32)
                    dkc = jnp.zeros((Q_BLOCK, PAIR), F32)
                    dvc = jnp.zeros((Q_BLOCK, PAIR), F32)
                    dkp = jnp.zeros((Q_BLOCK, PAIR), F32)
                    dvp = jnp.zeros((Q_BLOCK, PAIR), F32)
                    for h, (qh, doh, pc_bf, dsc_bf, pp_bf, dsp_bf) in enumerate(out):
                        dqh = _dot(dsc_bf, kc)
                        dkc = dkc + _dot_tn(dsc_bf, qh)
                        dvc = dvc + _dot_tn(pc_bf, doh)
                        if nb > 1:
                            dqh = dqh + _dot(dsp_bf, kp)
                            dkp = dkp + _dot_tn(dsp_bf, qh)
                            dvp = dvp + _dot_tn(pp_bf, doh)
                        dq = jnp.where(masks[h], dqh, dq)
                    dq_acc[rows, :] += dq * 0.125
                    dk_acc[rows, :] += dkc
                    dv_acc[rows, :] += dvc
                    if nb > 1:
                        dk_acc[prow, :] += dkp
                        dv_acc[prow, :] += dvp
                return carry

            lax.fori_loop(0, BLOCKS_PER_CFG // BWD_BLOCKS_PER_TRIP, block, 0)

        dqkv_ref[0] = dq_acc[...].astype(BF16)
        dqkv_ref[1] = dk_acc[...].astype(BF16)
        dqkv_ref[2] = dv_acc[...].astype(BF16)

        @pl.when(step == n_steps - 1)
        def _():
            exchange.finish()

    def slab(first):
        return pl.BlockSpec((None, SEQ, PAIR), lambda p, b: (first + p, b, 0))

    nat = pl.BlockSpec((SEQ, PAIR), lambda p, b: (b, p))
    tbl = pl.BlockSpec((N_CFG, 2, Q_BLOCK, 2 * Q_BLOCK), lambda p, b: (0, p, 0, 0))
    acc = pltpu.VMEM((SEQ, PAIR), F32)
    res = pl.pallas_call(
        body, grid=(N_PAIR, batch),
        in_specs=[slab(0), slab(N_PAIR), slab(2 * N_PAIR),
                  pl.BlockSpec((SEQ, PAIR), lambda p, b: (b, A_WIDTH // PAIR + p)), nat, nat, tbl] + [ANY] * nt,
        out_specs=[pl.BlockSpec((3, SEQ, PAIR), lambda p, b: (0, b, p)), tbl] + [ANY] * nt,
        out_shape=[jax.ShapeDtypeStruct((3, m, B_WIDTH), BF16),
                   jax.ShapeDtypeStruct((N_CFG, B_HEADS, Q_BLOCK, 2 * Q_BLOCK), F32)]
        + [jax.ShapeDtypeStruct((3,) + p.shape[1:], p.dtype) for p in parts],
        scratch_shapes=[acc, acc, acc, acc, pltpu.SemaphoreType.DMA((3 * nt,)), pltpu.SemaphoreType.DMA((3 * nt,))],
        compiler_params=_params(("arbitrary", "arbitrary")), name="attn_bwd",
    )(qkv, qkv, qkv, dmix, o, lse, bias, *parts)
    return res[0], res[1], list(res[2:])


def _rel_bias_grad(ds, buckets_np):
    present = [sorted(set(int(v) for v in np.unique(buckets_np[c]) if v >= 0)) for c in range(N_CFG)]

    def body(bk_ref, ds_ref, o_ref, acc_ref):
        acc_ref[...] = jnp.zeros_like(acc_ref)
        for c in range(N_CFG):
            bk = bk_ref[c]
            for h in range(B_HEADS):
                dsv = ds_ref[c, h]
                for b in present[c]:
                    part = jnp.sum(jnp.where(bk == b, dsv, 0.0), axis=0, keepdims=True)
                    acc_ref[pl.ds(h * NUM_BUCKETS + b, 1), :] += part
        o_ref[...] = jnp.sum(acc_ref[...], axis=1, keepdims=True)

    vm = pl.BlockSpec(memory_space=pltpu.VMEM)
    return pl.pallas_call(
        body, in_specs=[vm, vm], out_specs=vm,
        out_shape=jax.ShapeDtypeStruct((B_HEADS * NUM_BUCKETS, 1), F32),
        scratch_shapes=[pltpu.VMEM((B_HEADS * NUM_BUCKETS, 2 * Q_BLOCK), F32)],
        compiler_params=_params(), name="rel_bias_grad",
    )(jnp.asarray(buckets_np), ds)


def _assemble_dproj(duv, dqkv):
    m = duv.shape[0]

    rows = 1024

    def body(duv_ref, dqkv_ref, o_ref):
        o_ref[:, :2 * A_WIDTH] = duv_ref[...].astype(BF16)
        for k in range(3):
            o_ref[:, 2 * A_WIDTH + k * B_WIDTH:2 * A_WIDTH + (k + 1) * B_WIDTH] = dqkv_ref[k]

    return pl.pallas_call(
        body, grid=(m // rows,),
        in_specs=[pl.BlockSpec((rows, 2 * A_WIDTH), lambda i: (i, 0)),
                  pl.BlockSpec((3, rows, B_WIDTH), lambda i: (0, i, 0))],
        out_specs=pl.BlockSpec((rows, IN_COLS), lambda i: (i, 0)),
        out_shape=jax.ShapeDtypeStruct((m, IN_COLS), BF16),
        compiler_params=_params(("parallel",)), name="assemble_dproj",
    )(duv, dqkv)


def _shift_down(x, k):
    row = lax.broadcasted_iota(jnp.int32, x.shape, 0)
    return jnp.where(row >= k, pltpu.roll(x, k, 0), 0.0)


def _shift_up(x, k):
    n = x.shape[0]
    row = lax.broadcasted_iota(jnp.int32, x.shape, 0)
    return jnp.where(row < n - k, pltpu.roll(x, n - k, 0), 0.0)


def _convgate_fwd(gate, up, conv_w, conv_b, batch):
    m = gate.shape[0]

    def body(g_ref, u_ref, w_ref, b_ref, a_ref):
        g = g_ref[...].astype(F32)
        w = w_ref[...]
        c = b_ref[...] + w[0:1] * _shift_down(g, 2) + w[1:2] * _shift_down(g, 1) + w[2:3] * g
        a_ref[...] = (_gelu(c) * u_ref[...].astype(F32)).astype(BF16)

    blk = pl.BlockSpec((SEQ, LANE_BLOCK), lambda b, j: (b, j))
    return pl.pallas_call(
        body, grid=(batch, D_FF // LANE_BLOCK),
        in_specs=[blk, blk, pl.BlockSpec((3, LANE_BLOCK), lambda b, j: (0, j)),
                  pl.BlockSpec((1, LANE_BLOCK), lambda b, j: (0, j))],
        out_specs=blk,
        out_shape=jax.ShapeDtypeStruct((m, D_FF), BF16),
        compiler_params=_params(("parallel", "parallel")), name="convgate_fwd",
    )(gate, up, conv_w, conv_b)


def _convgate_bwd(gate, up, dact, conv_w, conv_b, batch):
    m = gate.shape[0]

    def body(g_ref, u_ref, da_ref, w_ref, b_ref, dg_ref, du_ref, dw_ref, db_ref):
        @pl.when(pl.program_id(1) == 0)
        def _():
            dw_ref[...] = jnp.zeros_like(dw_ref)
            db_ref[...] = jnp.zeros_like(db_ref)

        g = g_ref[...].astype(F32)
        w = w_ref[...]
        g1 = _shift_down(g, 1)
        g2 = _shift_down(g, 2)
        c = b_ref[...] + w[0:1] * g2 + w[1:2] * g1 + w[2:3] * g
        gg, dgg = _gelu_and_grad(c)
        da = da_ref[...].astype(F32)
        du_ref[...] = (da * gg).astype(BF16)
        dc = da * u_ref[...].astype(F32) * dgg
        db_ref[...] += jnp.sum(dc, axis=0, keepdims=True)
        dw_ref[0:1, :] += jnp.sum(dc * g2, axis=0, keepdims=True)
        dw_ref[1:2, :] += jnp.sum(dc * g1, axis=0, keepdims=True)
        dw_ref[2:3, :] += jnp.sum(dc * g, axis=0, keepdims=True)
        dg_ref[...] = (w[2:3] * dc + w[1:2] * _shift_up(dc, 1) + w[0:1] * _shift_up(dc, 2)).astype(BF16)

    blk = pl.BlockSpec((SEQ, LANE_BLOCK), lambda j, b: (b, j))
    wspec = pl.BlockSpec((3, LANE_BLOCK), lambda j, b: (0, j))
    bspec = pl.BlockSpec((1, LANE_BLOCK), lambda j, b: (0, j))
    return pl.pallas_call(
        body, grid=(D_FF // LANE_BLOCK, batch),
        in_specs=[blk, blk, blk, wspec, bspec],
        out_specs=[blk, blk, wspec, bspec],
        out_shape=[jax.ShapeDtypeStruct((m, D_FF), BF16), jax.ShapeDtypeStruct((m, D_FF), BF16),
                   jax.ShapeDtypeStruct((3, D_FF), F32), jax.ShapeDtypeStruct((1, D_FF), F32)],
        compiler_params=_params(("parallel", "arbitrary")), name="convgate_bwd",
    )(gate, up, dact, conv_w, conv_b)


def _train_step(x, tgt, g1, g2, g3, g4, shards, ln_g, ln_b, w_s, b_s, rel_bias, conv_w_shard, conv_b, batch,
                s_idx, c_idx):
    big = dict(tm=1024, out_dtype=F32)
    buckets = _bucket_tables()
    bias = _bias_tables(rel_bias, buckets)
    bz = jnp.repeat(b_s.T, HEAD_DIM, axis=1)
    w_st = jnp.swapaxes(w_s, 1, 2)

    def with_own(gathered, own):
        return lax.dynamic_update_index_in_dim(gathered, own, s_idx[0], 0)

    g_in, g_convw = _gather_weights([shards["w_in"]], conv_w_shard)
    w_in_t = with_own(g_in, shards["w_in"]).reshape(IN_COLS, D_MODEL)
    conv_w = _from_col_shards(with_own(g_convw, conv_w_shard))

    h1, uv, qkv = _proj_fwd(x, g1, w_in_t)
    a = _gate_fwd(uv, ln_g, ln_b, w_s, bz)
    later = ["w_out", "w_gate", "w_up", "w_down"]
    o_bf, lse, gathered = _attn_fwd(qkv, bias, batch, [shards[n] for n in later])
    g_out, g_gate, g_up, g_down = [with_own(g, shards[n]) for g, n in zip(gathered, later)]
    w_out = g_out.reshape(D_MODEL, D_MODEL)
    w_gate_t = g_gate.reshape(D_FF, D_MODEL)
    w_up_t = g_up.reshape(D_FF, D_MODEL)
    w_down = g_down.reshape(D_FF, D_MODEL)
    y1 = _mm(a, w_out[:A_WIDTH], dims="nn", tn=1024, tk=A_WIDTH, name="mm_out_a", **big)
    y1 = _mm(o_bf, w_out[A_WIDTH:], dims="nn", tn=1024, tk=B_WIDTH, name="mm_out_b", add=y1, **big)
    x1, h2 = _mid_fwd(x, y1, g2, g3)
    gate = _mm(h2, w_gate_t, dims="nt", tm=1024, tn=1408, tk=1024, out_dtype=BF16, name="mm_gate")
    up = _mm(h2, w_up_t, dims="nt", tm=1024, tn=1408, tk=1024, out_dtype=BF16, name="mm_up")
    act = _convgate_fwd(gate, up, conv_w, conv_b, batch)
    y2 = _mm(act, w_down, dims="nn", tn=1024, tk=1408, name="mm_down", **big)
    dx2, dy2, dg4, loss = _loss_head(x1, y2, tgt, g4)

    dact = _mm(dy2, w_down, dims="nt", tm=1024, tn=1408, tk=1024, out_dtype=BF16, name="mm_dact")
    dw_down = _mm(act, dy2, dims="tn", tm=1408, tn=1024, tk=1024, out_dtype=F32, name="mm_dw_down")
    dgate, dup, dconv_w, dconv_b = _convgate_bwd(gate, up, dact, conv_w, conv_b, batch)
    dh2 = _mm(dgate, w_gate_t, dims="nn", tn=1024, tk=1408, name="mm_dh2_g", **big)
    dh2 = _mm(dup, w_up_t, dims="nn", tn=1024, tk=1408, name="mm_dh2_u", add=dh2, **big)
    dw_gate_t = _mm(dgate, h2, dims="tn", tm=1408, tn=1024, tk=1024, out_dtype=F32, name="mm_dw_gate")
    dw_up_t = _mm(dup, h2, dims="tn", tm=1408, tn=1024, tk=1024, out_dtype=F32, name="mm_dw_up")
    dx1, dy1, dg2, dg3 = _mid_bwd(x1, y1, dh2, dx2, g2, g3)
    dmix = _mm(dy1, w_out, dims="nt", tn=1024, tk=1024, name="mm_dmix", **big)
    dw_out_a = _mm(a, dy1, dims="tn", tm=A_WIDTH, tn=1024, tk=1024, out_dtype=F32, name="mm_dw_out_a")
    dw_out_b = _mm(o_bf, dy1, dims="tn", tm=B_WIDTH, tn=1024, tk=1024, out_dtype=F32, name="mm_dw_out_b")
    duv, dln_g, dln_b, dw_s, dbz = _gate_bwd(uv, dmix, ln_g, ln_b, w_s, w_st, bz)

    dw_out = jnp.concatenate([dw_out_a, dw_out_b], axis=0)
    done = [g.reshape(N_SHARD, g.shape[0] // N_SHARD, D_MODEL) for g in (dw_down, dw_gate_t, dw_up_t, dw_out)]
    parts = [_add_halves(g, r, c_idx) for g, r in zip(done, _exchange_halves(done, "rs_sibling_exchange_ffn"))]
    dqkv, ds, recv = _attn_bwd(qkv, dmix, o_bf, lse, bias, batch, parts)
    fulls = [_add_chips(p, r, s_idx, c_idx) for p, r in zip(parts, recv)]
    drel = _rel_bias_grad(ds, buckets)
    dproj = _assemble_dproj(duv, dqkv)
    dw_in_t = _mm(dproj, h1, dims="tn", tm=1408, tn=1024, tk=1024, out_dtype=F32, name="mm_dw_in")
    last = [dw_in_t.reshape(N_SHARD, SHARD_FF, D_MODEL)]
    part_in = [_add_halves(g, r, c_idx) for g, r in zip(last, _exchange_halves(last, "rs_sibling_exchange_in"))]
    fulls += [_add_chips(p, r, s_idx, c_idx) for p, r in zip(part_in, _exchange_chips(part_in))]
    dh1 = _mm(dproj, w_in_t, dims="nn", tn=1024, tk=1408, name="mm_dh1", **big)
    dx0, dg1 = _in_bwd(x, dh1, dx1, g1)
    reduced = dict(zip(["w_down", "w_gate", "w_up", "w_out", "w_in"], _share_halves(fulls)))

    small = dict(
        norm_mix_pre=dg1, norm_mix_post=dg2, norm_ffn_pre=dg3, norm_ffn_post=dg4,
        ln_v_gain=dln_g, ln_v_bias=dln_b, spatial_w=dw_s,
        spatial_b=dbz[:, ::HEAD_DIM].T,
        rel_bias=drel.reshape(B_HEADS, NUM_BUCKETS).T,
        conv_w=dconv_w, conv_b=dconv_b,
    )
    return loss, dx0, small, reduced


def _mesh_pos():
    x, y, c = lax.axis_index("x"), lax.axis_index("y"), lax.axis_index("c")
    chips = [(1 - x, y), (x, 1 - y), (1 - x, 1 - y)]
    return x, y, c, chips


ANY = pl.BlockSpec(memory_space=pl.ANY)


class _GatherPlan:
    def __init__(self, shapes, shard_refs, out_refs, send_sems, recv_sems):
        self.shapes, self.shard_refs, self.out_refs = shapes, shard_refs, out_refs
        self.send_sems, self.recv_sems = send_sems, recv_sems
        self.x, self.y, self.c, self.chips = _mesh_pos()
        self.sib = (self.x, self.y, 1 - self.c)

    def _half(self, t, chip, which):
        rows = self.shapes[t][0] // 2
        return self.out_refs[t].at[2 * chip[0] + chip[1], pl.ds(which * rows, rows), :]

    def _copy(self, k, src, dst, to):
        return pltpu.make_async_remote_copy(src_ref=src, dst_ref=dst, send_sem=self.send_sems.at[k],
                                            recv_sem=self.recv_sems.at[k], device_id=to, device_id_type=MESH)

    def _sends(self, t):
        rows = self.shapes[t][0] // 2
        src = self.shard_refs[t].at[pl.ds(self.c * rows, rows), :]
        return [self._copy(6 * t + j, src, self._half(t, (self.x, self.y), self.c), (*chip, self.c))
                for j, chip in enumerate(self.chips)]

    def _forwards(self, t):
        return [self._copy(6 * t + 3 + j, self._half(t, chip, self.c), self._half(t, chip, self.c), self.sib)
                for j, chip in enumerate(self.chips)]

    def start(self, ts):
        for t in ts:
            for cp in self._sends(t):
                cp.start()

    def forward(self, ts):
        for t in ts:
            for j, chip in enumerate(self.chips):
                landed = self._half(t, chip, self.c)
                self._copy(6 * t + j, landed, landed, (*chip, self.c)).wait_recv()
            for cp in self._forwards(t):
                cp.start()

    def finish(self, ts):
        for t in ts:
            for j, chip in enumerate(self.chips):
                other = self._half(t, chip, 1 - self.c)
                self._copy(6 * t + 3 + j, other, other, self.sib).wait_recv()
        for t in ts:
            for cp in self._sends(t) + self._forwards(t):
                cp.wait_send()


class _ChipExchangePlan:
    def __init__(self, part_refs, out_refs, send_sems, recv_sems):
        self.part_refs, self.out_refs, self.send_sems, self.recv_sems = part_refs, out_refs, send_sems, recv_sems
        _, _, self.c, self.chips = _mesh_pos()

    def _copies(self):
        return [pltpu.make_async_remote_copy(
            src_ref=p.at[2 * chip[0] + chip[1]], dst_ref=o.at[j], send_sem=self.send_sems.at[3 * t + j],
            recv_sem=self.recv_sems.at[3 * t + j], device_id=(*chip, self.c), device_id_type=MESH)
            for t, (p, o) in enumerate(zip(self.part_refs, self.out_refs)) for j, chip in enumerate(self.chips)]

    def start(self):
        for cp in self._copies():
            cp.start()

    def finish(self):
        for cp in self._copies():
            cp.wait()


def _gather_weights(shards, conv_w_shard):
    nt = len(shards)

    def body(*refs):
        shard_refs = refs[:nt]
        cw_ref = refs[nt]
        out_refs = refs[nt + 1:2 * nt + 1]
        cw_out = refs[2 * nt + 1]
        send_sems, recv_sems = refs[2 * nt + 2:]
        x, y, c, chips = _mesh_pos()
        s = 2 * x + y
        sib = (x, y, 1 - c)

        def half(ref, chip, which, t):
            rows = shards[t].shape[0] // 2
            return ref.at[2 * chip[0] + chip[1], pl.ds(which * rows, rows), :]

        def rcopy(k, src, dst, to):
            return pltpu.make_async_remote_copy(src_ref=src, dst_ref=dst, send_sem=send_sems.at[k],
                                                recv_sem=recv_sems.at[k], device_id=to, device_id_type=MESH)

        sends = []
        for t in range(nt):
            rows = shards[t].shape[0] // 2
            for j, chip in enumerate(chips):
                sends.append(rcopy(7 * t + j, shard_refs[t].at[pl.ds(c * rows, rows), :],
                                   half(out_refs[t], (x, y), c, t), (*chip, c)))
        for j, chip in enumerate(chips):
            sends.append(rcopy(7 * nt + j, cw_ref, cw_out.at[s], (*chip, c)))
        for cp in sends:
            cp.start()
        fwd = []
        for t in range(nt):
            for j, chip in enumerate(chips):
                landed = half(out_refs[t], chip, c, t)
                rcopy(7 * t + j, landed, landed, (*chip, c)).wait_recv()
                f = rcopy(7 * t + 3 + j, landed, landed, sib)
                f.start()
                fwd.append(f)
        for j, chip in enumerate(chips):
            dst = cw_out.at[2 * chip[0] + chip[1]]
            rcopy(7 * nt + j, dst, dst, (*chip, c)).wait_recv()
        for t in range(nt):
            for j, chip in enumerate(chips):
                other = half(out_refs[t], chip, 1 - c, t)
                rcopy(7 * t + 3 + j, other, other, sib).wait_recv()
        for cp in sends + fwd:
            cp.wait_send()

    out_shape = [jax.ShapeDtypeStruct((N_SHARD,) + sh.shape, sh.dtype) for sh in shards]
    out_shape.append(jax.ShapeDtypeStruct((N_SHARD,) + conv_w_shard.shape, conv_w_shard.dtype))
    nsem = 7 * nt + 3
    return pl.pallas_call(
        body, in_specs=[ANY] * (nt + 1), out_specs=[ANY] * (nt + 1), out_shape=out_shape,
        scratch_shapes=[pltpu.SemaphoreType.DMA((nsem,)), pltpu.SemaphoreType.DMA((nsem,))],
        compiler_params=pltpu.CompilerParams(has_side_effects=True), name="gather_weights",
    )(*shards, conv_w_shard)


def _exchange_halves(grads, name):
    nt = len(grads)

    def body(*refs):
        g_refs = refs[:nt]
        out_refs = refs[nt:2 * nt]
        send_sems, recv_sems = refs[2 * nt:]
        x, y, c, _ = _mesh_pos()
        copies = []
        for t in range(nt):
            rows = grads[t].shape[1] // 2
            copies.append(pltpu.make_async_remote_copy(
                src_ref=g_refs[t].at[:, pl.ds((1 - c) * rows, rows), :], dst_ref=out_refs[t],
                send_sem=send_sems.at[t], recv_sem=recv_sems.at[t], device_id=(x, y, 1 - c), device_id_type=MESH))
        for cp in copies:
            cp.start()
        for cp in copies:
            cp.wait()

    out_shape = [jax.ShapeDtypeStruct((N_SHARD, g.shape[1] // 2, g.shape[2]), g.dtype) for g in grads]
    return pl.pallas_call(
        body, in_specs=[ANY] * nt, out_specs=[ANY] * nt, out_shape=out_shape,
        scratch_shapes=[pltpu.SemaphoreType.DMA((nt,)), pltpu.SemaphoreType.DMA((nt,))],
        compiler_params=pltpu.CompilerParams(has_side_effects=True), name=name,
    )(*grads)


def _add_halves(g, recv, c_idx):
    _, rows2, cols = g.shape
    rows = rows2 // 2
    tr = rows // 2 if rows % 16 == 0 and rows >= 256 else rows
    nblk = rows // tr

    def body(c_ref, g_ref, r_ref, o_ref):
        o_ref[...] = (g_ref[...] + r_ref[...]).astype(BF16)

    return pl.pallas_call(
        body,
        grid_spec=pltpu.PrefetchScalarGridSpec(
            num_scalar_prefetch=1, grid=(N_SHARD, nblk),
            in_specs=[pl.BlockSpec((None, tr, cols), lambda s, i, c: (s, c[0] * nblk + i, 0)),
                      pl.BlockSpec((None, tr, cols), lambda s, i, c: (s, i, 0))],
            out_specs=pl.BlockSpec((None, tr, cols), lambda s, i, c: (s, i, 0))),
        out_shape=jax.ShapeDtypeStruct((N_SHARD, rows, cols), BF16),
        compiler_params=_params(("parallel", "parallel")), name="rs_add_halves",
    )(c_idx, g, recv)


def _exchange_chips(parts):
    nt = len(parts)

    def body(*refs):
        p_refs = refs[:nt]
        out_refs = refs[nt:2 * nt]
        send_sems, recv_sems = refs[2 * nt:]
        x, y, c, chips = _mesh_pos()
        copies = []
        for t in range(nt):
            for j, chip in enumerate(chips):
                copies.append(pltpu.make_async_remote_copy(
                    src_ref=p_refs[t].at[2 * chip[0] + chip[1]], dst_ref=out_refs[t].at[j],
                    send_sem=send_sems.at[3 * t + j], recv_sem=recv_sems.at[3 * t + j],
                    device_id=(*chip, c), device_id_type=MESH))
        for cp in copies:
            cp.start()
        for cp in copies:
            cp.wait()

    out_shape = [jax.ShapeDtypeStruct((3,) + p.shape[1:], p.dtype) for p in parts]
    return pl.pallas_call(
        body, in_specs=[ANY] * nt, out_specs=[ANY] * nt, out_shape=out_shape,
        scratch_shapes=[pltpu.SemaphoreType.DMA((3 * nt,)), pltpu.SemaphoreType.DMA((3 * nt,))],
        compiler_params=pltpu.CompilerParams(has_side_effects=True), name="rs_chip_exchange",
    )(*parts)


def _add_chips(part, recv, s_idx, c_idx):
    _, rows, cols = part.shape
    tr = rows // 2 if rows % 32 == 0 and rows >= 256 else rows
    nblk = rows // tr

    def body(idx_ref, p_ref, r_ref, o_ref):
        acc = p_ref[...].astype(F32)
        for j in range(3):
            acc = acc + r_ref[j].astype(F32)
        o_ref[...] = acc

    return pl.pallas_call(
        body,
        grid_spec=pltpu.PrefetchScalarGridSpec(
            num_scalar_prefetch=1, grid=(nblk,),
            in_specs=[pl.BlockSpec((None, tr, cols), lambda i, idx: (idx[0], i, 0)),
                      pl.BlockSpec((3, tr, cols), lambda i, idx: (0, i, 0))],
            out_specs=pl.BlockSpec((tr, cols), lambda i, idx: (idx[1] * nblk + i, 0))),
        out_shape=jax.ShapeDtypeStruct((2 * rows, cols), F32),
        compiler_params=_params(("parallel",)), name="rs_add_chips",
    )(jnp.concatenate([s_idx, c_idx]), part, recv)


def _share_halves(fulls):
    nt = len(fulls)

    def body(*refs):
        out_refs = refs[nt:2 * nt]
        send_sems, recv_sems = refs[2 * nt:]
        x, y, c, _ = _mesh_pos()
        copies = []
        for t in range(nt):
            rows = fulls[t].shape[0] // 2
            mine = out_refs[t].at[pl.ds(c * rows, rows), :]
            copies.append(pltpu.make_async_remote_copy(
                src_ref=mine, dst_ref=mine, send_sem=send_sems.at[t], recv_sem=recv_sems.at[t],
                device_id=(x, y, 1 - c), device_id_type=MESH))
        for cp in copies:
            cp.start()
        for t in range(nt):
            rows = fulls[t].shape[0] // 2
            theirs = out_refs[t].at[pl.ds((1 - c) * rows, rows), :]
            pltpu.make_async_remote_copy(
                src_ref=theirs, dst_ref=theirs, send_sem=send_sems.at[t], recv_sem=recv_sems.at[t],
                device_id=(x, y, 1 - c), device_id_type=MESH).wait_recv()
        for cp in copies:
            cp.wait_send()

    out_shape = [jax.ShapeDtypeStruct(f.shape, f.dtype) for f in fulls]
    return pl.pallas_call(
        body, in_specs=[ANY] * nt, out_specs=[ANY] * nt, out_shape=out_shape,
        input_output_aliases={t: t for t in range(nt)},
        scratch_shapes=[pltpu.SemaphoreType.DMA((nt,)), pltpu.SemaphoreType.DMA((nt,))],
        compiler_params=pltpu.CompilerParams(has_side_effects=True), name="rs_share_halves",
    )(*fulls)


def _allreduce_small(packed):
    rows = packed.shape[0]

    def body(p_ref, o_ref, sib_ref, chip_ref, send_sems, recv_sems):
        x, y, c, chips = _mesh_pos()
        first = pltpu.make_async_remote_copy(src_ref=p_ref, dst_ref=sib_ref, send_sem=send_sems.at[0],
                                             recv_sem=recv_sems.at[0], device_id=(x, y, 1 - c), device_id_type=MESH)
        first.start()
        first.wait()
        o_ref[...] = p_ref[...] + sib_ref[...]
        copies = [pltpu.make_async_remote_copy(src_ref=o_ref, dst_ref=chip_ref.at[j], send_sem=send_sems.at[1 + j],
                                               recv_sem=recv_sems.at[1 + j], device_id=(*chip, c), device_id_type=MESH)
                  for j, chip in enumerate(chips)]
        for cp in copies:
            cp.start()
        for cp in copies:
            cp.wait()
        o_ref[...] = (o_ref[...] + chip_ref[0]) + (chip_ref[1] + chip_ref[2])

    vm = pl.BlockSpec(memory_space=pltpu.VMEM)
    return pl.pallas_call(
        body, in_specs=[vm], out_specs=vm, out_shape=jax.ShapeDtypeStruct(packed.shape, F32),
        scratch_shapes=[pltpu.VMEM((rows, 128), F32), pltpu.VMEM((3, rows, 128), F32),
                        pltpu.SemaphoreType.DMA((4,)), pltpu.SemaphoreType.DMA((4,))],
        compiler_params=pltpu.CompilerParams(has_side_effects=True, vmem_limit_bytes=VMEM_LIMIT),
        name="allreduce_small",
    )(packed)


def _adamw(w, g, m, v, name):
    rows, cols = w.shape
    tr = rows
    if rows * cols > 256 * 1024:
        tr = next(cand for cand in (256, 176, 128) if rows % cand == 0)

    def body(w_ref, g_ref, m_ref, v_ref, go_ref, d_ref, nm_ref, nv_ref):
        gv = g_ref[...]
        go_ref[...] = gv
        nm = ADAM_B1 * m_ref[...] + (1.0 - ADAM_B1) * gv
        nv = ADAM_B2 * v_ref[...] + (1.0 - ADAM_B2) * (gv * gv)
        m_hat = nm / (1.0 - ADAM_B1 ** ADAM_STEP)
        v_hat = nv / (1.0 - ADAM_B2 ** ADAM_STEP)
        d_ref[...] = -ADAM_LR * (m_hat / (jnp.sqrt(v_hat) + ADAM_EPS) + ADAM_WD * w_ref[...])
        nm_ref[...] = nm
        nv_ref[...] = nv

    spec = pl.BlockSpec((tr, cols), lambda i: (i, 0))
    sds = jax.ShapeDtypeStruct((rows, cols), F32)
    return pl.pallas_call(
        body, grid=(rows // tr,), in_specs=[spec] * 4, out_specs=[spec] * 4, out_shape=[sds] * 4,
        compiler_params=_params(("parallel",)), name=name,
    )(w, g, m, v)


def _pack(arrays, rows):
    flat = jnp.concatenate([a.reshape(-1) for a in arrays])
    flat = jnp.pad(flat, (0, rows * 128 - flat.shape[0]))
    return flat.reshape(rows, 128)


def _unpack(packed, shapes):
    flat = packed.reshape(-1)
    out, off = [], 0
    for sh in shapes:
        n = int(np.prod(sh))
        out.append(flat[off:off + n].reshape(sh))
        off += n
    return out


SMALL = ["norm_mix_pre", "norm_mix_post", "norm_ffn_pre", "norm_ffn_post", "ln_v_gain", "ln_v_bias",
         "spatial_w", "spatial_b", "rel_bias", "conv_b"]
LARGE = ["w_in", "w_gate", "w_up", "w_down", "w_out"]
TRANSPOSED = ("w_in", "w_gate", "w_up")
ORDER = ["norm_mix_pre", "norm_mix_post", "norm_ffn_pre", "norm_ffn_post", "w_in", "ln_v_gain", "ln_v_bias",
         "spatial_w", "spatial_b", "rel_bias", "w_out", "w_gate", "w_up", "conv_w", "conv_b", "w_down"]


def _col_shards(full):
    rows, cols4 = full.shape
    return full.reshape(rows, N_SHARD, cols4 // N_SHARD).transpose(1, 0, 2)


def _from_col_shards(g):
    n, rows, cols = g.shape
    return g.transpose(1, 0, 2).reshape(rows, n * cols)


def kernel(x, norm_mix_pre, norm_mix_post, norm_ffn_pre, norm_ffn_post, w_in, ln_v_gain, ln_v_bias, spatial_w, spatial_b, rel_bias, w_out, w_gate, w_up, conv_w, conv_b, w_down, loss_target, m_norm_mix_pre, m_norm_mix_post, m_norm_ffn_pre, m_norm_ffn_post, m_w_in, m_ln_v_gain, m_ln_v_bias, m_spatial_w, m_spatial_b, m_rel_bias, m_w_out, m_w_gate, m_w_up, m_conv_w, m_conv_b, m_w_down, v_norm_mix_pre, v_norm_mix_post, v_norm_ffn_pre, v_norm_ffn_post, v_w_in, v_ln_v_gain, v_ln_v_bias, v_spatial_w, v_spatial_b, v_rel_bias, v_w_out, v_w_gate, v_w_up, v_conv_w, v_conv_b, v_w_down):
    params = dict(norm_mix_pre=norm_mix_pre, norm_mix_post=norm_mix_post, norm_ffn_pre=norm_ffn_pre,
                  norm_ffn_post=norm_ffn_post, w_in=w_in, ln_v_gain=ln_v_gain, ln_v_bias=ln_v_bias,
                  spatial_w=spatial_w, spatial_b=spatial_b, rel_bias=rel_bias, w_out=w_out, w_gate=w_gate,
                  w_up=w_up, conv_w=conv_w, conv_b=conv_b, w_down=w_down)
    mom = dict(norm_mix_pre=m_norm_mix_pre, norm_mix_post=m_norm_mix_post, norm_ffn_pre=m_norm_ffn_pre,
               norm_ffn_post=m_norm_ffn_post, w_in=m_w_in, ln_v_gain=m_ln_v_gain, ln_v_bias=m_ln_v_bias,
               spatial_w=m_spatial_w, spatial_b=m_spatial_b, rel_bias=m_rel_bias, w_out=m_w_out, w_gate=m_w_gate,
               w_up=m_w_up, conv_w=m_conv_w, conv_b=m_conv_b, w_down=m_w_down)
    var = dict(norm_mix_pre=v_norm_mix_pre, norm_mix_post=v_norm_mix_post, norm_ffn_pre=v_norm_ffn_pre,
               norm_ffn_post=v_norm_ffn_post, w_in=v_w_in, ln_v_gain=v_ln_v_gain, ln_v_bias=v_ln_v_bias,
               spatial_w=v_spatial_w, spatial_b=v_spatial_b, rel_bias=v_rel_bias, w_out=v_w_out, w_gate=v_w_gate,
               w_up=v_w_up, conv_w=v_conv_w, conv_b=v_conv_b, w_down=v_w_down)

    batch = x.shape[0]
    xi, yi, ci = lax.axis_index("x"), lax.axis_index("y"), lax.axis_index("c")
    s_idx = (2 * xi + yi).astype(jnp.int32).reshape(1)
    c_idx = ci.astype(jnp.int32).reshape(1)

    def local(a, n):
        return jnp.swapaxes(a[0], 0, 1) if n in TRANSPOSED else a[0]

    shards = {n: local(params[n], n).astype(BF16) for n in LARGE}
    loss_part, dx0, grads, reduced = _train_step(
        x.reshape(batch * SEQ, D_MODEL), loss_target.reshape(batch * SEQ, D_MODEL),
        norm_mix_pre, norm_mix_post, norm_ffn_pre, norm_ffn_post, shards,
        ln_v_gain.reshape(1, A_WIDTH), ln_v_bias.reshape(1, A_WIDTH), spatial_w[0], spatial_b[0], rel_bias,
        conv_w[0], conv_b, batch, s_idx, c_idx)
    loss = lax.psum(loss_part[0, 0], ("x", "y", "c"))
    grad_x = dx0.reshape(batch, SEQ, D_MODEL)

    small_g = [grads[n].reshape(params[n].shape) for n in SMALL] + [grads["conv_w"]]
    n_small = sum(int(np.prod(g.shape)) for g in small_g)
    small_rows = -(-n_small // (8 * 128)) * 8
    summed = _unpack(_allreduce_small(_pack(small_g, small_rows)),
                     [params[n].shape for n in SMALL] + [(3, D_FF)])
    for n, g in zip(SMALL, summed[:-1]):
        reduced[n] = g
    reduced["conv_w"] = lax.dynamic_slice_in_dim(summed[-1], s_idx[0] * SHARD_FF, SHARD_FF, axis=1)[None]

    out_g, out_d, out_m, out_v = {}, {}, {}, {}
    for n in LARGE:
        res = _adamw(local(params[n], n), reduced[n], local(mom[n], n), local(var[n], n), name=f"adamw_{n}")
        if n in TRANSPOSED:
            res = [jnp.swapaxes(r, 0, 1) for r in res]
        out_g[n], out_d[n], out_m[n], out_v[n] = [r[None] for r in res]
    small_names = SMALL + ["conv_w"]
    rows_s = -(-sum(int(np.prod(params[n].shape)) for n in small_names) // (8 * 128)) * 8
    _, d, nm, nv = _adamw(_pack([params[n] for n in small_names], rows_s),
                          _pack([reduced[n] for n in small_names], rows_s),
                          _pack([mom[n] for n in small_names], rows_s), _pack([var[n] for n in small_names], rows_s),
                          name="adamw_small")
    shapes = [params[n].shape for n in small_names]
    for n, dd, mm, vv in zip(small_names, _unpack(d, shapes), _unpack(nm, shapes), _unpack(nv, shapes)):
        out_g[n], out_d[n], out_m[n], out_v[n] = reduced[n], dd, mm, vv

    return (loss, grad_x, *[out_g[n] for n in ORDER], *[out_d[n] for n in ORDER],
            *[out_m[n] for n in ORDER], *[out_v[n] for n in ORDER])
```

```python
import functools
import math

import numpy as np
import jax
import jax.numpy as jnp
from jax import lax
from jax.experimental import pallas as pl
from jax.experimental.pallas import tpu as pltpu

F32 = jnp.float32
BF16 = jnp.bfloat16
MESH = pl.DeviceIdType.MESH

D_MODEL = 1024
SEQ = 2048
HEAD_DIM = 64
A_GROUPS = 4
A_WIDTH = 256
B_HEADS = 12
B_WIDTH = 768
CHUNK = 128
DILATED = ((128, 1), (512, 4), (2048, 16))
NUM_BUCKETS = 32
MAX_DISTANCE = 2048
D_FF = 2816
IN_COLS = 2816
NORM_EPS = 1e-6
NEG_INF = -1e30
N_SHARD = 4
SHARD_FF = D_FF // N_SHARD
LANE_BLOCK = 256
VMEM_LIMIT = 56 * 1024 * 1024

ADAM_LR = 0.001
ADAM_B1 = 0.9
ADAM_B2 = 0.999
ADAM_EPS = 1e-08
ADAM_WD = 0.01
ADAM_STEP = 10

GELU_C = math.sqrt(2.0 / math.pi)
GELU_A = 0.044715

ANY = pl.BlockSpec(memory_space=pl.ANY)


def _params(sem=None):
    return pltpu.CompilerParams(dimension_semantics=sem, vmem_limit_bytes=VMEM_LIMIT)


def _dot(a, b, precision=None):
    return jnp.dot(a, b, preferred_element_type=F32, precision=precision)


def _dot_nt(a, b):
    return lax.dot_general(a, b, (((1,), (1,)), ((), ())), preferred_element_type=F32)


def _dot_tn(a, b):
    return lax.dot_general(a, b, (((0,), (0,)), ((), ())), preferred_element_type=F32)


def _gelu(x):
    t = jnp.tanh(GELU_C * (x + GELU_A * (x * x * x)))
    return 0.5 * x * (1.0 + t)


def _gelu_and_grad(x):
    x2 = x * x
    t = jnp.tanh(GELU_C * (x + GELU_A * (x2 * x)))
    g = 0.5 * x * (1.0 + t)
    dg = 0.5 * (1.0 + t) + 0.5 * x * (1.0 - t * t) * (GELU_C * (1.0 + 3.0 * GELU_A * x2))
    return g, dg


def _mesh_pos():
    x, y, c = lax.axis_index("x"), lax.axis_index("y"), lax.axis_index("c")
    chips = [(1 - x, y), (x, 1 - y), (1 - x, 1 - y)]
    return x, y, c, chips


class _GatherPlan:
    def __init__(self, shapes, shard_refs, out_refs, send_sems, recv_sems):
        self.shapes, self.shard_refs, self.out_refs = shapes, shard_refs, out_refs
        self.send_sems, self.recv_sems = send_sems, recv_sems
        self.x, self.y, self.c, self.chips = _mesh_pos()
        self.sib = (self.x, self.y, 1 - self.c)

    def _half(self, t, chip, which):
        rows = self.shapes[t][0] // 2
        return self.out_refs[t].at[2 * chip[0] + chip[1], pl.ds(which * rows, rows), :]

    def _copy(self, k, src, dst, to):
        return pltpu.make_async_remote_copy(src_ref=src, dst_ref=dst, send_sem=self.send_sems.at[k],
                                            recv_sem=self.recv_sems.at[k], device_id=to, device_id_type=MESH)

    def _sends(self, t):
        rows = self.shapes[t][0] // 2
        src = self.shard_refs[t].at[pl.ds(self.c * rows, rows), :]
        return [self._copy(6 * t + j, src, self._half(t, (self.x, self.y), self.c), (*chip, self.c))
                for j, chip in enumerate(self.chips)]

    def _forwards(self, t):
        return [self._copy(6 * t + 3 + j, self._half(t, chip, self.c), self._half(t, chip, self.c), self.sib)
                for j, chip in enumerate(self.chips)]

    def start(self, ts):
        for t in ts:
            for cp in self._sends(t):
                cp.start()

    def forward(self, ts):
        for t in ts:
            for j, chip in enumerate(self.chips):
                landed = self._half(t, chip, self.c)
                self._copy(6 * t + j, landed, landed, (*chip, self.c)).wait_recv()
            for cp in self._forwards(t):
                cp.start()

    def finish(self, ts):
        for t in ts:
            for j, chip in enumerate(self.chips):
                other = self._half(t, chip, 1 - self.c)
                self._copy(6 * t + 3 + j, other, other, self.sib).wait_recv()
        for t in ts:
            for cp in self._sends(t) + self._forwards(t):
                cp.wait_send()


class _SiblingExchangePlan:
    def __init__(self, shapes, grad_refs, out_refs, send_sems, recv_sems):
        self.shapes, self.grad_refs, self.out_refs = shapes, grad_refs, out_refs
        self.send_sems, self.recv_sems = send_sems, recv_sems
        self.x, self.y, self.c, _ = _mesh_pos()

    def _copies(self):
        out = []
        for t, (g, o) in enumerate(zip(self.grad_refs, self.out_refs)):
            rows = self.shapes[t][1] // 2
            out.append(pltpu.make_async_remote_copy(
                src_ref=g.at[:, pl.ds((1 - self.c) * rows, rows), :], dst_ref=o, send_sem=self.send_sems.at[t],
                recv_sem=self.recv_sems.at[t], device_id=(self.x, self.y, 1 - self.c), device_id_type=MESH))
        return out

    def start(self):
        for cp in self._copies():
            cp.start()

    def finish(self):
        for cp in self._copies():
            cp.wait()


class _ChipExchangePlan:
    def __init__(self, part_refs, out_refs, send_sems, recv_sems):
        self.part_refs, self.out_refs, self.send_sems, self.recv_sems = part_refs, out_refs, send_sems, recv_sems
        _, _, self.c, self.chips = _mesh_pos()

    def _copies(self):
        return [pltpu.make_async_remote_copy(
            src_ref=p.at[2 * chip[0] + chip[1]], dst_ref=o.at[j], send_sem=self.send_sems.at[3 * t + j],
            recv_sem=self.recv_sems.at[3 * t + j], device_id=(*chip, self.c), device_id_type=MESH)
            for t, (p, o) in enumerate(zip(self.part_refs, self.out_refs)) for j, chip in enumerate(self.chips)]

    def start(self):
        for cp in self._copies():
            cp.start()

    def finish(self):
        for cp in self._copies():
            cp.wait()


def _sem_pair(n):
    return [pltpu.SemaphoreType.DMA((n,)), pltpu.SemaphoreType.DMA((n,))]


def _mm(a, b, *, dims, tm, tn, tk, out_dtype, name, add=None, exchange=()):
    if dims == "nn":
        m, k = a.shape
        n = b.shape[1]
        a_spec = pl.BlockSpec((tm, tk), lambda i, j, kk: (i, kk))
        b_spec = pl.BlockSpec((tk, tn), lambda i, j, kk: (kk, j))
        dot = _dot
    elif dims == "nt":
        m, k = a.shape
        n = b.shape[0]
        a_spec = pl.BlockSpec((tm, tk), lambda i, j, kk: (i, kk))
        b_spec = pl.BlockSpec((tn, tk), lambda i, j, kk: (j, kk))
        dot = _dot_nt
    else:
        k, m = a.shape
        n = b.shape[1]
        a_spec = pl.BlockSpec((tk, tm), lambda i, j, kk: (kk, i))
        b_spec = pl.BlockSpec((tk, tn), lambda i, j, kk: (kk, j))
        dot = _dot_tn
    assert m % tm == 0 and n % tn == 0 and k % tk == 0, (name, m, n, k)
    grid = (m // tm, n // tn, k // tk)
    nk = grid[2]
    has_add = add is not None
    assert nk == 1 or out_dtype == F32, name
    nx = len(exchange)

    def body(*refs):
        a_ref, b_ref = refs[:2]
        add_ref = refs[2] if has_add else None
        part_refs = refs[2 + has_add:2 + has_add + nx]
        o_ref = refs[2 + has_add + nx]
        recv_refs = refs[3 + has_add + nx:3 + has_add + 2 * nx]
        sems = refs[3 + has_add + 2 * nx:]
        ids = [pl.program_id(d) for d in range(3)]
        if nx:
            plan = _ChipExchangePlan(part_refs, recv_refs, *sems)

            @pl.when((ids[0] == 0) & (ids[1] == 0) & (ids[2] == 0))
            def _():
                plan.start()

        prod = dot(a_ref[...].astype(BF16), b_ref[...].astype(BF16))
        if nk == 1:
            if has_add:
                prod = prod + add_ref[...].astype(F32)
            o_ref[...] = prod.astype(out_dtype)
        else:
            @pl.when(ids[2] == 0)
            def _():
                o_ref[...] = prod + add_ref[...].astype(F32) if has_add else prod

            @pl.when(ids[2] > 0)
            def _():
                o_ref[...] += prod

        if nx:
            @pl.when((ids[0] == grid[0] - 1) & (ids[1] == grid[1] - 1) & (ids[2] == nk - 1))
            def _():
                plan.finish()

    in_specs = [a_spec, b_spec]
    args = [a, b]
    if has_add:
        in_specs.append(pl.BlockSpec((tm, tn), lambda i, j, kk: (i, j)))
        args.append(add)
    res = pl.pallas_call(
        body,
        grid=grid,
        in_specs=in_specs + [ANY] * nx,
        out_specs=[pl.BlockSpec((tm, tn), lambda i, j, kk: (i, j))] + [ANY] * nx,
        out_shape=[jax.ShapeDtypeStruct((m, n), out_dtype)]
        + [jax.ShapeDtypeStruct((3,) + p.shape[1:], p.dtype) for p in exchange],
        scratch_shapes=_sem_pair(3 * nx) if nx else [],
        compiler_params=_params(("arbitrary",) * 3 if nx else ("parallel", "parallel", "arbitrary")),
        name=name,
    )(*args, *exchange)
    return (res[0], list(res[1:])) if nx else res[0]


ROW_TILE = 512


def _row_spec(width=D_MODEL):
    return pl.BlockSpec((ROW_TILE, width), lambda i: (i, 0))


def _vec_spec(width=D_MODEL):
    return pl.BlockSpec((1, width), lambda i: (0, 0))


def _rstd(v):
    return lax.rsqrt(jnp.mean(v * v, axis=-1, keepdims=True) + NORM_EPS)


def _mid_fwd(x0, y1, g2, g3):
    m = x0.shape[0]

    def body(x0_ref, y1_ref, g2_ref, g3_ref, x1_ref, h2_ref):
        y1v = y1_ref[...]
        x1 = x0_ref[...] + y1v * _rstd(y1v) * g2_ref[...]
        x1_ref[...] = x1
        h2_ref[...] = (x1 * _rstd(x1) * g3_ref[...]).astype(BF16)

    return pl.pallas_call(
        body, grid=(m // ROW_TILE,),
        in_specs=[_row_spec(), _row_spec(), _vec_spec(), _vec_spec()],
        out_specs=[_row_spec(), _row_spec()],
        out_shape=[jax.ShapeDtypeStruct((m, D_MODEL), F32), jax.ShapeDtypeStruct((m, D_MODEL), BF16)],
        compiler_params=_params(("parallel",)), name="mid_fwd",
    )(x0, y1, g2, g3)


def _rms_bwd_rows(dout, v, g):
    r = _rstd(v)
    n = v * r
    dn = dout * g
    dv = r * (dn - n * jnp.mean(dn * n, axis=-1, keepdims=True))
    dg = jnp.sum(dout * n, axis=0, keepdims=True)
    return dv, dg


def _loss_head(x1, y2, tgt, g4):
    m = x1.shape[0]

    def body(x1_ref, y2_ref, t_ref, g4_ref, dx2_ref, dy2_ref, dg4_ref, loss_ref):
        i = pl.program_id(0)

        @pl.when(i == 0)
        def _():
            dg4_ref[...] = jnp.zeros_like(dg4_ref)
            loss_ref[...] = jnp.zeros_like(loss_ref)

        y2v = y2_ref[...]
        g4 = g4_ref[...]
        x2 = x1_ref[...] + y2v * _rstd(y2v) * g4
        err = x2 - t_ref[...]
        loss_ref[...] += 0.5 * jnp.sum(jnp.mean(err * err, axis=-1, keepdims=True), axis=0, keepdims=True)
        dx2 = err * (1.0 / D_MODEL)
        dx2_ref[...] = dx2
        dy2, dg4 = _rms_bwd_rows(dx2, y2v, g4)
        dy2_ref[...] = dy2.astype(BF16)
        dg4_ref[...] += dg4

    return pl.pallas_call(
        body, grid=(m // ROW_TILE,),
        in_specs=[_row_spec(), _row_spec(), _row_spec(), _vec_spec()],
        out_specs=[_row_spec(), _row_spec(), _vec_spec(), pl.BlockSpec((1, 1), lambda i: (0, 0))],
        out_shape=[jax.ShapeDtypeStruct((m, D_MODEL), F32), jax.ShapeDtypeStruct((m, D_MODEL), BF16),
                   jax.ShapeDtypeStruct((1, D_MODEL), F32), jax.ShapeDtypeStruct((1, 1), F32)],
        compiler_params=_params(("arbitrary",)), name="loss_head",
    )(x1, y2, tgt, g4)


def _mid_bwd(x1, y1, dh2, dx2, g2, g3):
    m = x1.shape[0]

    def body(x1_ref, y1_ref, dh2_ref, dx2_ref, g2_ref, g3_ref, dx1_ref, dy1_ref, dg2_ref, dg3_ref):
        i = pl.program_id(0)

        @pl.when(i == 0)
        def _():
            dg2_ref[...] = jnp.zeros_like(dg2_ref)
            dg3_ref[...] = jnp.zeros_like(dg3_ref)

        d3, dg3 = _rms_bwd_rows(dh2_ref[...], x1_ref[...], g3_ref[...])
        dx1 = dx2_ref[...] + d3
        dx1_ref[...] = dx1
        dy1, dg2 = _rms_bwd_rows(dx1, y1_ref[...], g2_ref[...])
        dy1_ref[...] = dy1.astype(BF16)
        dg2_ref[...] += dg2
        dg3_ref[...] += dg3

    return pl.pallas_call(
        body, grid=(m // ROW_TILE,),
        in_specs=[_row_spec(), _row_spec(), _row_spec(), _row_spec(), _vec_spec(), _vec_spec()],
        out_specs=[_row_spec(), _row_spec(), _vec_spec(), _vec_spec()],
        out_shape=[jax.ShapeDtypeStruct((m, D_MODEL), F32), jax.ShapeDtypeStruct((m, D_MODEL), BF16),
                   jax.ShapeDtypeStruct((1, D_MODEL), F32), jax.ShapeDtypeStruct((1, D_MODEL), F32)],
        compiler_params=_params(("arbitrary",)), name="mid_bwd",
    )(x1, y1, dh2, dx2, g2, g3)


def _in_bwd(x0, dh1, dx1, g1):
    m = x0.shape[0]

    def body(x0_ref, dh1_ref, dx1_ref, g1_ref, dx0_ref, dg1_ref):
        i = pl.program_id(0)

        @pl.when(i == 0)
        def _():
            dg1_ref[...] = jnp.zeros_like(dg1_ref)

        d1, dg1 = _rms_bwd_rows(dh1_ref[...], x0_ref[...], g1_ref[...])
        dx0_ref[...] = dx1_ref[...] + d1
        dg1_ref[...] += dg1

    return pl.pallas_call(
        body, grid=(m // ROW_TILE,),
        in_specs=[_row_spec(), _row_spec(), _row_spec(), _vec_spec()],
        out_specs=[_row_spec(), _vec_spec()],
        out_shape=[jax.ShapeDtypeStruct((m, D_MODEL), F32), jax.ShapeDtypeStruct((1, D_MODEL), F32)],
        compiler_params=_params(("arbitrary",)), name="in_bwd",
    )(x0, dh1, dx1, g1)


GATE_ROWS = 512


def _group_mean_matrix():
    p = np.zeros((A_WIDTH, A_WIDTH), np.float32)
    for g in range(A_GROUPS):
        p[g * HEAD_DIM:(g + 1) * HEAD_DIM, g * HEAD_DIM:(g + 1) * HEAD_DIM] = 1.0 / HEAD_DIM
    return jnp.asarray(p)


def _group_masks(width=A_WIDTH):
    lane = lax.broadcasted_iota(jnp.int32, (1, width), 1)
    return [(lane >= g * HEAD_DIM) & (lane < (g + 1) * HEAD_DIM) for g in range(width // HEAD_DIM)]


def _layernorm_groups(vg, pavg):
    hi = lax.Precision.HIGHEST
    mu = _dot(vg, pavg, hi)
    xc = vg - mu
    var = _dot(xc * xc, pavg, hi)
    rstd = lax.rsqrt(var + NORM_EPS)
    return xc * rstd, rstd


def _spatial_mix(w_bf, vn_chunk_bf, masks, bz):
    z = bz
    for g in range(A_GROUPS):
        z = z + jnp.where(masks[g], _dot(w_bf[g], vn_chunk_bf), 0.0)
    return z


def _full_spec(shape):
    return pl.BlockSpec(shape, lambda i: tuple(0 for _ in shape))


def _gate_fwd(uv, ln_g, ln_b, w_s, bz):
    m = uv.shape[0]
    pavg = _group_mean_matrix()

    def body(u_ref, v_ref, lg_ref, lb_ref, w_ref, bz_ref, p_ref, a_ref):
        masks = _group_masks()
        row = lax.broadcasted_iota(jnp.int32, (CHUNK, CHUNK), 0)
        col = lax.broadcasted_iota(jnp.int32, (CHUNK, CHUNK), 1)
        w_bf = [jnp.where(row >= col, w_ref[g], 0.0).astype(BF16) for g in range(A_GROUPS)]
        ug = _gelu(u_ref[...])
        vhat, _ = _layernorm_groups(_gelu(v_ref[...]), p_ref[...])
        vn = vhat * lg_ref[...] + lb_ref[...]
        bz = bz_ref[...]
        for c in range(GATE_ROWS // CHUNK):
            sl = slice(c * CHUNK, (c + 1) * CHUNK)
            z = _spatial_mix(w_bf, vn[sl].astype(BF16), masks, bz)
            a_ref[sl, :] = (ug[sl] * z).astype(BF16)

    return pl.pallas_call(
        body, grid=(m // GATE_ROWS,),
        in_specs=[pl.BlockSpec((GATE_ROWS, A_WIDTH), lambda i: (i, 0)),
                  pl.BlockSpec((GATE_ROWS, A_WIDTH), lambda i: (i, 1)),
                  _full_spec((1, A_WIDTH)), _full_spec((1, A_WIDTH)), _full_spec((A_GROUPS, CHUNK, CHUNK)),
                  _full_spec((CHUNK, A_WIDTH)), _full_spec((A_WIDTH, A_WIDTH))],
        out_specs=pl.BlockSpec((GATE_ROWS, A_WIDTH), lambda i: (i, 0)),
        out_shape=jax.ShapeDtypeStruct((m, A_WIDTH), BF16),
        compiler_params=_params(("parallel",)), name="gate_fwd",
    )(uv, uv, ln_g, ln_b, w_s, bz, pavg)


def _gate_bwd(uv, dmix, ln_g, ln_b, w_s, w_st, bz, grads):
    m = uv.shape[0]
    pavg = _group_mean_matrix()
    nsteps = m // GATE_ROWS
    nx = len(grads)
    shapes = [g.shape for g in grads]

    def body(u_ref, v_ref, da_ref, lg_ref, lb_ref, w_ref, wt_ref, bz_ref, p_ref, *rest):
        grad_refs = rest[:nx]
        duv_ref, dlg_ref, dlb_ref, dw_ref, dbz_ref = rest[nx:nx + 5]
        recv_refs = rest[nx + 5:2 * nx + 5]
        exchange = _SiblingExchangePlan(shapes, grad_refs, recv_refs, *rest[2 * nx + 5:])
        i = pl.program_id(0)

        @pl.when(i == 0)
        def _():
            exchange.start()
            dlg_ref[...] = jnp.zeros_like(dlg_ref)
            dlb_ref[...] = jnp.zeros_like(dlb_ref)
            dw_ref[...] = jnp.zeros_like(dw_ref)
            dbz_ref[...] = jnp.zeros_like(dbz_ref)

        hi = lax.Precision.HIGHEST
        masks = _group_masks()
        row = lax.broadcasted_iota(jnp.int32, (CHUNK, CHUNK), 0)
        col = lax.broadcasted_iota(jnp.int32, (CHUNK, CHUNK), 1)
        tril = row >= col
        w_bf = [jnp.where(tril, w_ref[g], 0.0).astype(BF16) for g in range(A_GROUPS)]
        wt_bf = [jnp.where(col >= row, wt_ref[g], 0.0).astype(BF16) for g in range(A_GROUPS)]
        pavg_v = p_ref[...]
        lg = lg_ref[...]
        ug, dug = _gelu_and_grad(u_ref[...])
        vg, dvg_dx = _gelu_and_grad(v_ref[...])
        vhat, rstd = _layernorm_groups(vg, pavg_v)
        vn = vhat * lg + lb_ref[...]
        da = da_ref[...]
        bz = bz_ref[...]
        for c in range(GATE_ROWS // CHUNK):
            sl = slice(c * CHUNK, (c + 1) * CHUNK)
            vn_bf = vn[sl].astype(BF16)
            z = _spatial_mix(w_bf, vn_bf, masks, bz)
            dz = da[sl] * ug[sl]
            duv_ref[sl, 0:A_WIDTH] = da[sl] * z * dug[sl]
            dbz_ref[...] += dz
            dz_bf = dz.astype(BF16)
            dvn = jnp.zeros((CHUNK, A_WIDTH), F32)
            for g in range(A_GROUPS):
                dz_g = jnp.where(masks[g], dz, 0.0).astype(BF16)
                dw_ref[g] += jnp.where(tril, _dot_nt(dz_g, vn_bf), 0.0)
                dvn = dvn + jnp.where(masks[g], _dot(wt_bf[g], dz_bf), 0.0)
            vh = vhat[sl]
            dlb_ref[...] += jnp.sum(dvn, axis=0, keepdims=True)
            dlg_ref[...] += jnp.sum(dvn * vh, axis=0, keepdims=True)
            dvh = dvn * lg
            m1 = _dot(dvh, pavg_v, hi)
            m2 = _dot(dvh * vh, pavg_v, hi)
            duv_ref[sl, A_WIDTH:2 * A_WIDTH] = rstd[sl] * (dvh - m1 - vh * m2) * dvg_dx[sl]

        @pl.when(i == nsteps - 1)
        def _():
            dbz_ref[...] = _dot(dbz_ref[...], pavg_v * float(HEAD_DIM), hi)
            exchange.finish()

    res = pl.pallas_call(
        body, grid=(nsteps,),
        in_specs=[pl.BlockSpec((GATE_ROWS, A_WIDTH), lambda i: (i, 0)),
                  pl.BlockSpec((GATE_ROWS, A_WIDTH), lambda i: (i, 1)),
                  pl.BlockSpec((GATE_ROWS, A_WIDTH), lambda i: (i, 0)),
                  _full_spec((1, A_WIDTH)), _full_spec((1, A_WIDTH)), _full_spec((A_GROUPS, CHUNK, CHUNK)),
                  _full_spec((A_GROUPS, CHUNK, CHUNK)), _full_spec((CHUNK, A_WIDTH)),
                  _full_spec((A_WIDTH, A_WIDTH))] + [ANY] * nx,
        out_specs=[pl.BlockSpec((GATE_ROWS, 2 * A_WIDTH), lambda i: (i, 0)),
                   _full_spec((1, A_WIDTH)), _full_spec((1, A_WIDTH)), _full_spec((A_GROUPS, CHUNK, CHUNK)),
                   _full_spec((CHUNK, A_WIDTH))] + [ANY] * nx,
        out_shape=[jax.ShapeDtypeStruct((m, 2 * A_WIDTH), F32),
                   jax.ShapeDtypeStruct((1, A_WIDTH), F32), jax.ShapeDtypeStruct((1, A_WIDTH), F32),
                   jax.ShapeDtypeStruct((A_GROUPS, CHUNK, CHUNK), F32),
                   jax.ShapeDtypeStruct((CHUNK, A_WIDTH), F32)]
        + [jax.ShapeDtypeStruct((N_SHARD, s[1] // 2, s[2]), F32) for s in shapes],
        scratch_shapes=_sem_pair(nx),
        compiler_params=_params(("arbitrary",)), name="gate_bwd",
    )(uv, uv, dmix, ln_g, ln_b, w_s, w_st, bz, pavg, *grads)
    return res[:5], list(res[5:])


Q_BLOCK = 128
PAIR = 2 * HEAD_DIM
N_PAIR = B_HEADS // 2
N_CFG = len(DILATED)
BLOCKS_PER_CFG = SEQ // Q_BLOCK
QKV_SLABS = 3 * N_PAIR
FWD_BLOCKS_PER_TRIP = 8
BWD_BLOCKS_PER_TRIP = 4


def _t5_bucket_np(dist, dtype):
    max_exact = NUM_BUCKETS // 2
    d = np.maximum(dist, 1).astype(dtype)
    large = max_exact + (np.log(d / dtype(max_exact)) / dtype(math.log(MAX_DISTANCE / max_exact))
                         * dtype(NUM_BUCKETS - max_exact))
    large = np.minimum(large.astype(np.int32), NUM_BUCKETS - 1)
    return np.where(dist < max_exact, dist, large)


def _bucket_tables():
    i = np.arange(Q_BLOCK)[:, None]
    j = np.arange(Q_BLOCK)[None, :]
    tables = []
    for _, dil in DILATED:
        rel_prev = Q_BLOCK + i - j
        rel_cur = i - j
        rel = np.concatenate([rel_prev, rel_cur], axis=1)
        valid = np.concatenate([rel_prev <= Q_BLOCK, rel_cur >= 0], axis=1)
        dist = np.maximum(rel, 0) * dil
        b32 = _t5_bucket_np(dist, np.float32)
        b64 = _t5_bucket_np(dist, np.float64)
        assert np.array_equal(b32, b64)
        tables.append(np.where(valid, b32, -1).astype(np.int32))
    return np.stack(tables)


def _present_buckets(buckets_np):
    return [sorted(set(int(v) for v in np.unique(buckets_np[c]) if v >= 0)) for c in range(N_CFG)]


def _bias_tables(rel_bias, buckets_np):
    present = _present_buckets(buckets_np)

    def body(rb_ref, bk_ref, o_ref):
        for c in range(N_CFG):
            bk = bk_ref[c]
            for h in range(B_HEADS):
                acc = jnp.full((Q_BLOCK, 2 * Q_BLOCK), NEG_INF, F32)
                for b in present[c]:
                    acc = jnp.where(bk == b, rb_ref[b, h], acc)
                o_ref[c, h] = acc

    return pl.pallas_call(
        body,
        in_specs=[pl.BlockSpec(memory_space=pltpu.SMEM), pl.BlockSpec(memory_space=pltpu.VMEM)],
        out_specs=pl.BlockSpec(memory_space=pltpu.VMEM),
        out_shape=jax.ShapeDtypeStruct((N_CFG, B_HEADS, Q_BLOCK, 2 * Q_BLOCK), F32),
        compiler_params=_params(), name="bias_tables",
    )(rel_bias, jnp.asarray(buckets_np))


def _proj_fwd(x, g1, w_in_t):
    m = x.shape[0]
    tm = ROW_TILE

    def body(x_ref, g_ref, w_ref, h_ref, uv_ref, qkv_ref):
        xv = x_ref[...]
        h = (xv * _rstd(xv) * g_ref[...]).astype(BF16)
        h_ref[...] = h
        acc = _dot_nt(h, w_ref[...])
        uv_ref[...] = acc[:, :2 * A_WIDTH]
        for s in range(QKV_SLABS):
            qkv_ref[s] = acc[:, 2 * A_WIDTH + s * PAIR:2 * A_WIDTH + (s + 1) * PAIR]

    return pl.pallas_call(
        body, grid=(m // tm,),
        in_specs=[pl.BlockSpec((tm, D_MODEL), lambda i: (i, 0)), _vec_spec(),
                  pl.BlockSpec((IN_COLS, D_MODEL), lambda i: (0, 0))],
        out_specs=[pl.BlockSpec((tm, D_MODEL), lambda i: (i, 0)),
                   pl.BlockSpec((tm, 2 * A_WIDTH), lambda i: (i, 0)),
                   pl.BlockSpec((QKV_SLABS, tm, PAIR), lambda i: (0, i, 0))],
        out_shape=[jax.ShapeDtypeStruct((m, D_MODEL), BF16), jax.ShapeDtypeStruct((m, 2 * A_WIDTH), F32),
                   jax.ShapeDtypeStruct((QKV_SLABS, m, PAIR), F32)],
        compiler_params=_params(("parallel",)), name="proj_fwd",
    )(x, g1, w_in_t)


def _pair_masks():
    lane = lax.broadcasted_iota(jnp.int32, (1, PAIR), 1)
    return [lane < HEAD_DIM, lane >= HEAD_DIM]


def _block_rows(idx, dil):
    if dil == 1:
        n = idx
        cur = pl.ds(pl.multiple_of(n * Q_BLOCK, Q_BLOCK), Q_BLOCK)
        prev = pl.ds(pl.multiple_of(jnp.maximum(n - 1, 0) * Q_BLOCK, Q_BLOCK), Q_BLOCK)
        return n, cur, prev
    r = idx % dil
    n = idx // dil
    cur = pl.ds(r + (dil * Q_BLOCK) * n, Q_BLOCK, stride=dil)
    prev = pl.ds(r + (dil * Q_BLOCK) * jnp.maximum(n - 1, 0), Q_BLOCK, stride=dil)
    return n, cur, prev


def _attn_fwd(qkv, bias, batch, shards):
    m = qkv.shape[1]
    comb_rows = 256
    nt = len(shards)
    shapes = [sh.shape for sh in shards]
    n_steps = batch * N_PAIR
    early, late = list(range(nt // 2)), list(range(nt // 2, nt))

    def body(q_ref, k_ref, v_ref, b_ref, *rest):
        shard_refs = rest[:nt]
        o_ref, l_ref = rest[nt:nt + 2]
        gat_refs = rest[nt + 2:2 * nt + 2]
        scratch = rest[2 * nt + 2:]
        oc_refs, lc_refs = scratch[:N_CFG], scratch[N_CFG:2 * N_CFG]
        step = pl.program_id(0) * N_PAIR + pl.program_id(1)
        gather = _GatherPlan(shapes, shard_refs, gat_refs, *scratch[2 * N_CFG:])

        @pl.when(step == 0)
        def _():
            gather.start(early + late)

        @pl.when(step == n_steps // 2)
        def _():
            gather.forward(early)

        @pl.when(step == n_steps - 2)
        def _():
            gather.forward(late)

        masks = _pair_masks()
        for ci, (_, dil) in enumerate(DILATED):
            nb = SEQ // dil // Q_BLOCK

            def block(trip, carry, ci=ci, dil=dil, nb=nb):
                work = []
                for u in range(FWD_BLOCKS_PER_TRIP):
                    n, rows, prow = _block_rows(trip * FWD_BLOCKS_PER_TRIP + u, dil)
                    q = q_ref[rows, :] * 0.125
                    kc = k_ref[rows, :].astype(BF16)
                    vc = v_ref[rows, :]
                    kp = k_ref[prow, :].astype(BF16) if nb > 1 else None
                    vp = v_ref[prow, :] if nb > 1 else None
                    tiles = []
                    for h in range(2):
                        qh = jnp.where(masks[h], q, 0.0).astype(BF16)
                        sc = _dot_nt(qh, kc) + b_ref[ci, h, :, Q_BLOCK:]
                        sp = None
                        if nb > 1:
                            sp = _dot_nt(qh, kp) + jnp.where(n == 0, NEG_INF, b_ref[ci, h, :, :Q_BLOCK])
                        tiles.append((sc, sp))
                    work.append((rows, vc, vp, tiles))
                probs = []
                for _, _, _, tiles in work:
                    ps = []
                    for sc, sp in tiles:
                        mx = jnp.max(sc if sp is None else jnp.maximum(sc, sp), axis=1, keepdims=True)
                        pc = jnp.exp(sc - mx).astype(BF16)
                        pp = None if sp is None else jnp.exp(sp - mx).astype(BF16)
                        ps.append((mx, pc, pp))
                    probs.append(ps)
                for (rows, vc, vp, _), ps in zip(work, probs):
                    res = []
                    for h, (_, pc, pp) in enumerate(ps):
                        r = _dot(pc, jnp.where(masks[h], vc, 1.0).astype(BF16))
                        if pp is not None:
                            r = r + _dot(pp, jnp.where(masks[h], vp, 1.0).astype(BF16))
                        res.append(r)
                    num = jnp.where(masks[0], res[0], res[1])
                    den = pltpu.roll(jnp.where(masks[0], res[1], res[0]), HEAD_DIM, 1)
                    oc_refs[ci][rows, :] = num / den
                    lc_refs[ci][rows, :] = jnp.where(masks[0], ps[0][0], ps[1][0]) + jnp.log(den)
                return carry

            lax.fori_loop(0, BLOCKS_PER_CFG // FWD_BLOCKS_PER_TRIP, block, 0)

        def combine(i, carry):
            rr = pl.ds(pl.multiple_of(i * comb_rows, comb_rows), comb_rows)
            ls = [lc_refs[c][rr, :] for c in range(N_CFG)]
            mx = functools.reduce(jnp.maximum, ls)
            ws = [jnp.exp(l - mx) for l in ls]
            tot = functools.reduce(lambda a, b: a + b, ws)
            o = functools.reduce(lambda a, b: a + b, [ws[c] * oc_refs[c][rr, :] for c in range(N_CFG)]) / tot
            o_ref[rr, :] = o.astype(BF16)
            l_ref[rr, :] = mx + jnp.log(tot)
            return carry

        lax.fori_loop(0, SEQ // comb_rows, combine, 0)

        @pl.when(step == n_steps - 1)
        def _():
            gather.finish(early + late)

    def slab(first):
        return pl.BlockSpec((None, SEQ, PAIR), lambda b, p: (first + p, b, 0))

    nat = pl.BlockSpec((SEQ, PAIR), lambda b, p: (b, p))
    res = pl.pallas_call(
        body, grid=(batch, N_PAIR),
        in_specs=[slab(0), slab(N_PAIR), slab(2 * N_PAIR),
                  pl.BlockSpec((N_CFG, 2, Q_BLOCK, 2 * Q_BLOCK), lambda b, p: (0, p, 0, 0))] + [ANY] * nt,
        out_specs=[nat, nat] + [ANY] * nt,
        out_shape=[jax.ShapeDtypeStruct((m, B_WIDTH), BF16), jax.ShapeDtypeStruct((m, B_WIDTH), F32)]
        + [jax.ShapeDtypeStruct((N_SHARD,) + sh.shape, sh.dtype) for sh in shards],
        scratch_shapes=[pltpu.VMEM((SEQ, PAIR), F32)] * (2 * N_CFG) + _sem_pair(6 * nt),
        compiler_params=_params(("arbitrary", "arbitrary")), name="attn_fwd",
    )(qkv, qkv, qkv, bias, *shards)
    return res[0], res[1], list(res[2:])


def _attn_bwd(qkv, dmix, o, lse, bias, batch, parts):
    m = qkv.shape[1]
    nt = len(parts)
    n_steps = N_PAIR * batch

    def body(q_ref, k_ref, v_ref, do_ref, o_ref, l_ref, b_ref, *rest):
        part_refs = rest[:nt]
        dqkv_ref, ds_ref = rest[nt:nt + 2]
        recv_refs = rest[nt + 2:2 * nt + 2]
        dq_acc, dk_acc, dv_acc, d_scr, send_sems, recv_sems = rest[2 * nt + 2:]
        step = pl.program_id(0) * batch + pl.program_id(1)
        exchange = _ChipExchangePlan(part_refs, recv_refs, send_sems, recv_sems)

        @pl.when(step == 0)
        def _():
            exchange.start()

        @pl.when(pl.program_id(1) == 0)
        def _():
            ds_ref[...] = jnp.zeros_like(ds_ref)

        dq_acc[...] = jnp.zeros_like(dq_acc)
        dk_acc[...] = jnp.zeros_like(dk_acc)
        dv_acc[...] = jnp.zeros_like(dv_acc)
        masks = _pair_masks()
        ri = lax.broadcasted_iota(jnp.int32, (PAIR, PAIR), 0)
        cj = lax.broadcasted_iota(jnp.int32, (PAIR, PAIR), 1)
        same_head = ((ri < HEAD_DIM) == (cj < HEAD_DIM)).astype(F32)
        d_scr[...] = _dot(do_ref[...] * o_ref[...].astype(F32), same_head, lax.Precision.HIGHEST)

        for ci, (_, dil) in enumerate(DILATED):
            nb = SEQ // dil // Q_BLOCK

            def block(trip, carry, ci=ci, dil=dil, nb=nb):
                first = []
                for u in range(BWD_BLOCKS_PER_TRIP):
                    n, rows, prow = _block_rows(trip * BWD_BLOCKS_PER_TRIP + u, dil)
                    q = q_ref[rows, :] * 0.125
                    kc = k_ref[rows, :].astype(BF16)
                    vc = v_ref[rows, :].astype(BF16)
                    do = do_ref[rows, :]
                    lv = l_ref[rows, :]
                    dv_ = d_scr[rows, :]
                    kp = k_ref[prow, :].astype(BF16) if nb > 1 else None
                    vp = v_ref[prow, :].astype(BF16) if nb > 1 else None
                    heads = []
                    for h in range(2):
                        c0 = h * HEAD_DIM
                        qh = jnp.where(masks[h], q, 0.0).astype(BF16)
                        doh = jnp.where(masks[h], do, 0.0).astype(BF16)
                        sc = _dot_nt(qh, kc)
                        dpc = _dot_nt(doh, vc)
                        sp = _dot_nt(qh, kp) if nb > 1 else None
                        dpp = _dot_nt(doh, vp) if nb > 1 else None
                        heads.append((qh, doh, lv[:, c0:c0 + 1], dv_[:, c0:c0 + 1], sc, dpc, sp, dpp))
                    first.append((n, rows, prow, kc, kp, heads))
                second = []
                for n, rows, prow, kc, kp, heads in first:
                    out = []
                    for h, (qh, doh, lrow, drow, sc, dpc, sp, dpp) in enumerate(heads):
                        pc = jnp.exp(sc + b_ref[ci, h, :, Q_BLOCK:] - lrow)
                        dsc = pc * (dpc - drow)
                        ds_ref[ci, h, :, Q_BLOCK:] += dsc
                        pp_bf = dsp_bf = None
                        if nb > 1:
                            pp = jnp.exp(sp + jnp.where(n == 0, NEG_INF, b_ref[ci, h, :, :Q_BLOCK]) - lrow)
                            dsp = pp * (dpp - drow)
                            ds_ref[ci, h, :, :Q_BLOCK] += dsp
                            pp_bf, dsp_bf = pp.astype(BF16), dsp.astype(BF16)
                        out.append((qh, doh, pc.astype(BF16), dsc.astype(BF16), pp_bf, dsp_bf))
                    second.append((rows, prow, kc, kp, out))
                for rows, prow, kc, kp, out in second:
                    dq = jnp.zeros((Q_BLOCK, PAIR), F32)
                    dkc = jnp.zeros((Q_BLOCK, PAIR), F32)
                    dvc = jnp.zeros((Q_BLOCK, PAIR), F32)
                    dkp = jnp.zeros((Q_BLOCK, PAIR), F32)
                    dvp = jnp.zeros((Q_BLOCK, PAIR), F32)
                    for h, (qh, doh, pc_bf, dsc_bf, pp_bf, dsp_bf) in enumerate(out):
                        dqh = _dot(dsc_bf, kc)
                        dkc = dkc + _dot_tn(dsc_bf, qh)
                        dvc = dvc + _dot_tn(pc_bf, doh)
                        if nb > 1:
                            dqh = dqh + _dot(dsp_bf, kp)
                            dkp = dkp + _dot_tn(dsp_bf, qh)
                            dvp = dvp + _dot_tn(pp_bf, doh)
                        dq = jnp.where(masks[h], dqh, dq)
                    dq_acc[rows, :] += dq * 0.125
                    dk_acc[rows, :] += dkc
                    dv_acc[rows, :] += dvc
                    if nb > 1:
                        dk_acc[prow, :] += dkp
                        dv_acc[prow, :] += dvp
                return carry

            lax.fori_loop(0, BLOCKS_PER_CFG // BWD_BLOCKS_PER_TRIP, block, 0)

        dqkv_ref[0] = dq_acc[...].astype(BF16)
        dqkv_ref[1] = dk_acc[...].astype(BF16)
        dqkv_ref[2] = dv_acc[...].astype(BF16)

        @pl.when(step == n_steps - 1)
        def _():
            exchange.finish()

    def slab(first):
        return pl.BlockSpec((None, SEQ, PAIR), lambda p, b: (first + p, b, 0))

    nat = pl.BlockSpec((SEQ, PAIR), lambda p, b: (b, p))
    tbl = pl.BlockSpec((N_CFG, 2, Q_BLOCK, 2 * Q_BLOCK), lambda p, b: (0, p, 0, 0))
    acc = pltpu.VMEM((SEQ, PAIR), F32)
    res = pl.pallas_call(
        body, grid=(N_PAIR, batch),
        in_specs=[slab(0), slab(N_PAIR), slab(2 * N_PAIR),
                  pl.BlockSpec((SEQ, PAIR), lambda p, b: (b, A_WIDTH // PAIR + p)), nat, nat, tbl] + [ANY] * nt,
        out_specs=[pl.BlockSpec((3, SEQ, PAIR), lambda p, b: (0, b, p)), tbl] + [ANY] * nt,
        out_shape=[jax.ShapeDtypeStruct((3, m, B_WIDTH), BF16),
                   jax.ShapeDtypeStruct((N_CFG, B_HEADS, Q_BLOCK, 2 * Q_BLOCK), F32)]
        + [jax.ShapeDtypeStruct((3,) + p.shape[1:], p.dtype) for p in parts],
        scratch_shapes=[acc, acc, acc, acc] + _sem_pair(3 * nt),
        compiler_params=_params(("arbitrary", "arbitrary")), name="attn_bwd",
    )(qkv, qkv, qkv, dmix, o, lse, bias, *parts)
    return res[0], res[1], list(res[2:])


def _rel_bias_grad(ds, buckets_np):
    present = _present_buckets(buckets_np)

    def body(bk_ref, ds_ref, o_ref, acc_ref):
        acc_ref[...] = jnp.zeros_like(acc_ref)
        for c in range(N_CFG):
            bk = bk_ref[c]
            for h in range(B_HEADS):
                dsv = ds_ref[c, h]
                for b in present[c]:
                    part = jnp.sum(jnp.where(bk == b, dsv, 0.0), axis=0, keepdims=True)
                    acc_ref[pl.ds(h * NUM_BUCKETS + b, 1), :] += part
        o_ref[...] = jnp.sum(acc_ref[...], axis=1, keepdims=True)

    vm = pl.BlockSpec(memory_space=pltpu.VMEM)
    return pl.pallas_call(
        body, in_specs=[vm, vm], out_specs=vm,
        out_shape=jax.ShapeDtypeStruct((B_HEADS * NUM_BUCKETS, 1), F32),
        scratch_shapes=[pltpu.VMEM((B_HEADS * NUM_BUCKETS, 2 * Q_BLOCK), F32)],
        compiler_params=_params(), name="rel_bias_grad",
    )(jnp.asarray(buckets_np), ds)


def _assemble_dproj(duv, dqkv):
    m = duv.shape[0]
    rows = 1024

    def body(duv_ref, dqkv_ref, o_ref):
        o_ref[:, :2 * A_WIDTH] = duv_ref[...].astype(BF16)
        for k in range(3):
            o_ref[:, 2 * A_WIDTH + k * B_WIDTH:2 * A_WIDTH + (k + 1) * B_WIDTH] = dqkv_ref[k]

    return pl.pallas_call(
        body, grid=(m // rows,),
        in_specs=[pl.BlockSpec((rows, 2 * A_WIDTH), lambda i: (i, 0)),
                  pl.BlockSpec((3, rows, B_WIDTH), lambda i: (0, i, 0))],
        out_specs=pl.BlockSpec((rows, IN_COLS), lambda i: (i, 0)),
        out_shape=jax.ShapeDtypeStruct((m, IN_COLS), BF16),
        compiler_params=_params(("parallel",)), name="assemble_dproj",
    )(duv, dqkv)


def _shift_down(x, k):
    row = lax.broadcasted_iota(jnp.int32, x.shape, 0)
    return jnp.where(row >= k, pltpu.roll(x, k, 0), 0.0)


def _shift_up(x, k):
    n = x.shape[0]
    row = lax.broadcasted_iota(jnp.int32, x.shape, 0)
    return jnp.where(row < n - k, pltpu.roll(x, n - k, 0), 0.0)


def _convgate_fwd(gate, up, conv_w, conv_b, batch):
    m = gate.shape[0]

    def body(g_ref, u_ref, w_ref, b_ref, a_ref):
        g = g_ref[...].astype(F32)
        w = w_ref[...]
        c = b_ref[...] + w[0:1] * _shift_down(g, 2) + w[1:2] * _shift_down(g, 1) + w[2:3] * g
        a_ref[...] = (_gelu(c) * u_ref[...].astype(F32)).astype(BF16)

    blk = pl.BlockSpec((SEQ, LANE_BLOCK), lambda b, j: (b, j))
    return pl.pallas_call(
        body, grid=(batch, D_FF // LANE_BLOCK),
        in_specs=[blk, blk, pl.BlockSpec((3, LANE_BLOCK), lambda b, j: (0, j)),
                  pl.BlockSpec((1, LANE_BLOCK), lambda b, j: (0, j))],
        out_specs=blk,
        out_shape=jax.ShapeDtypeStruct((m, D_FF), BF16),
        compiler_params=_params(("parallel", "parallel")), name="convgate_fwd",
    )(gate, up, conv_w, conv_b)


def _convgate_bwd(gate, up, dact, conv_w, conv_b, batch):
    m = gate.shape[0]

    def body(g_ref, u_ref, da_ref, w_ref, b_ref, dg_ref, du_ref, dw_ref, db_ref):
        @pl.when(pl.program_id(1) == 0)
        def _():
            dw_ref[...] = jnp.zeros_like(dw_ref)
            db_ref[...] = jnp.zeros_like(db_ref)

        g = g_ref[...].astype(F32)
        w = w_ref[...]
        g1 = _shift_down(g, 1)
        g2 = _shift_down(g, 2)
        c = b_ref[...] + w[0:1] * g2 + w[1:2] * g1 + w[2:3] * g
        gg, dgg = _gelu_and_grad(c)
        da = da_ref[...].astype(F32)
        du_ref[...] = (da * gg).astype(BF16)
        dc = da * u_ref[...].astype(F32) * dgg
        db_ref[...] += jnp.sum(dc, axis=0, keepdims=True)
        dw_ref[0:1, :] += jnp.sum(dc * g2, axis=0, keepdims=True)
        dw_ref[1:2, :] += jnp.sum(dc * g1, axis=0, keepdims=True)
        dw_ref[2:3, :] += jnp.sum(dc * g, axis=0, keepdims=True)
        dg_ref[...] = (w[2:3] * dc + w[1:2] * _shift_up(dc, 1) + w[0:1] * _shift_up(dc, 2)).astype(BF16)

    blk = pl.BlockSpec((SEQ, LANE_BLOCK), lambda j, b: (b, j))
    wspec = pl.BlockSpec((3, LANE_BLOCK), lambda j, b: (0, j))
    bspec = pl.BlockSpec((1, LANE_BLOCK), lambda j, b: (0, j))
    return pl.pallas_call(
        body, grid=(D_FF // LANE_BLOCK, batch),
        in_specs=[blk, blk, blk, wspec, bspec],
        out_specs=[blk, blk, wspec, bspec],
        out_shape=[jax.ShapeDtypeStruct((m, D_FF), BF16), jax.ShapeDtypeStruct((m, D_FF), BF16),
                   jax.ShapeDtypeStruct((3, D_FF), F32), jax.ShapeDtypeStruct((1, D_FF), F32)],
        compiler_params=_params(("parallel", "arbitrary")), name="convgate_bwd",
    )(gate, up, dact, conv_w, conv_b)


def _gather_weights(shards, conv_w_shard):
    nt = len(shards)
    shapes = [sh.shape for sh in shards]
    ts = list(range(nt))

    def body(*refs):
        shard_refs = refs[:nt]
        cw_ref = refs[nt]
        out_refs = refs[nt + 1:2 * nt + 1]
        cw_out = refs[2 * nt + 1]
        send_sems, recv_sems, cw_send, cw_recv = refs[2 * nt + 2:]
        plan = _GatherPlan(shapes, shard_refs, out_refs, send_sems, recv_sems)
        x, y, c, chips = _mesh_pos()

        def cw_copy(j, src, dst, chip):
            return pltpu.make_async_remote_copy(src_ref=src, dst_ref=dst, send_sem=cw_send.at[j],
                                                recv_sem=cw_recv.at[j], device_id=(*chip, c), device_id_type=MESH)

        plan.start(ts)
        cw_sends = [cw_copy(j, cw_ref, cw_out.at[2 * x + y], chip) for j, chip in enumerate(chips)]
        for cp in cw_sends:
            cp.start()
        plan.forward(ts)
        for j, chip in enumerate(chips):
            dst = cw_out.at[2 * chip[0] + chip[1]]
            cw_copy(j, dst, dst, chip).wait_recv()
        plan.finish(ts)
        for cp in cw_sends:
            cp.wait_send()

    out_shape = [jax.ShapeDtypeStruct((N_SHARD,) + sh.shape, sh.dtype) for sh in shards]
    out_shape.append(jax.ShapeDtypeStruct((N_SHARD,) + conv_w_shard.shape, conv_w_shard.dtype))
    return pl.pallas_call(
        body, in_specs=[ANY] * (nt + 1), out_specs=[ANY] * (nt + 1), out_shape=out_shape,
        scratch_shapes=_sem_pair(6 * nt) + _sem_pair(3),
        compiler_params=pltpu.CompilerParams(has_side_effects=True), name="gather_weights",
    )(*shards, conv_w_shard)


def _exchange_halves(grads, name):
    nt = len(grads)
    shapes = [g.shape for g in grads]

    def body(*refs):
        plan = _SiblingExchangePlan(shapes, refs[:nt], refs[nt:2 * nt], *refs[2 * nt:])
        plan.start()
        plan.finish()

    out_shape = [jax.ShapeDtypeStruct((N_SHARD, g.shape[1] // 2, g.shape[2]), g.dtype) for g in grads]
    return pl.pallas_call(
        body, in_specs=[ANY] * nt, out_specs=[ANY] * nt, out_shape=out_shape,
        scratch_shapes=_sem_pair(nt),
        compiler_params=pltpu.CompilerParams(has_side_effects=True), name=name,
    )(*grads)


def _add_halves(g, recv, c_idx):
    _, rows2, cols = g.shape
    rows = rows2 // 2
    tr = rows // 2 if rows % 16 == 0 and rows >= 256 else rows
    nblk = rows // tr

    def body(c_ref, g_ref, r_ref, o_ref):
        o_ref[...] = (g_ref[...] + r_ref[...]).astype(BF16)

    return pl.pallas_call(
        body,
        grid_spec=pltpu.PrefetchScalarGridSpec(
            num_scalar_prefetch=1, grid=(N_SHARD, nblk),
            in_specs=[pl.BlockSpec((None, tr, cols), lambda s, i, c: (s, c[0] * nblk + i, 0)),
                      pl.BlockSpec((None, tr, cols), lambda s, i, c: (s, i, 0))],
            out_specs=pl.BlockSpec((None, tr, cols), lambda s, i, c: (s, i, 0))),
        out_shape=jax.ShapeDtypeStruct((N_SHARD, rows, cols), BF16),
        compiler_params=_params(("parallel", "parallel")), name="rs_add_halves",
    )(c_idx, g, recv)


def _add_chips(part, recv, s_idx, c_idx):
    _, rows, cols = part.shape
    tr = rows // 2 if rows % 32 == 0 and rows >= 256 else rows
    nblk = rows // tr

    def body(idx_ref, p_ref, r_ref, o_ref):
        acc = p_ref[...].astype(F32)
        for j in range(3):
            acc = acc + r_ref[j].astype(F32)
        o_ref[...] = acc

    return pl.pallas_call(
        body,
        grid_spec=pltpu.PrefetchScalarGridSpec(
            num_scalar_prefetch=1, grid=(nblk,),
            in_specs=[pl.BlockSpec((None, tr, cols), lambda i, idx: (idx[0], i, 0)),
                      pl.BlockSpec((3, tr, cols), lambda i, idx: (0, i, 0))],
            out_specs=pl.BlockSpec((tr, cols), lambda i, idx: (idx[1] * nblk + i, 0))),
        out_shape=jax.ShapeDtypeStruct((2 * rows, cols), F32),
        compiler_params=_params(("parallel",)), name="rs_add_chips",
    )(jnp.concatenate([s_idx, c_idx]), part, recv)


def _share_halves(fulls):
    nt = len(fulls)

    def body(*refs):
        out_refs = refs[nt:2 * nt]
        send_sems, recv_sems = refs[2 * nt:]
        x, y, c, _ = _mesh_pos()
        copies = []
        for t in range(nt):
            rows = fulls[t].shape[0] // 2
            mine = out_refs[t].at[pl.ds(c * rows, rows), :]
            copies.append(pltpu.make_async_remote_copy(
                src_ref=mine, dst_ref=mine, send_sem=send_sems.at[t], recv_sem=recv_sems.at[t],
                device_id=(x, y, 1 - c), device_id_type=MESH))
        for cp in copies:
            cp.start()
        for t in range(nt):
            rows = fulls[t].shape[0] // 2
            theirs = out_refs[t].at[pl.ds((1 - c) * rows, rows), :]
            pltpu.make_async_remote_copy(
                src_ref=theirs, dst_ref=theirs, send_sem=send_sems.at[t], recv_sem=recv_sems.at[t],
                device_id=(x, y, 1 - c), device_id_type=MESH).wait_recv()
        for cp in copies:
            cp.wait_send()

    out_shape = [jax.ShapeDtypeStruct(f.shape, f.dtype) for f in fulls]
    return pl.pallas_call(
        body, in_specs=[ANY] * nt, out_specs=[ANY] * nt, out_shape=out_shape,
        input_output_aliases={t: t for t in range(nt)},
        scratch_shapes=_sem_pair(nt),
        compiler_params=pltpu.CompilerParams(has_side_effects=True), name="rs_share_halves",
    )(*fulls)


def _allreduce_small(packed):
    rows = packed.shape[0]

    def body(p_ref, o_ref, sib_ref, chip_ref, send_sems, recv_sems):
        x, y, c, chips = _mesh_pos()
        first = pltpu.make_async_remote_copy(src_ref=p_ref, dst_ref=sib_ref, send_sem=send_sems.at[0],
                                             recv_sem=recv_sems.at[0], device_id=(x, y, 1 - c), device_id_type=MESH)
        first.start()
        first.wait()
        o_ref[...] = p_ref[...] + sib_ref[...]
        copies = [pltpu.make_async_remote_copy(src_ref=o_ref, dst_ref=chip_ref.at[j], send_sem=send_sems.at[1 + j],
                                               recv_sem=recv_sems.at[1 + j], device_id=(*chip, c), device_id_type=MESH)
                  for j, chip in enumerate(chips)]
        for cp in copies:
            cp.start()
        for cp in copies:
            cp.wait()
        o_ref[...] = (o_ref[...] + chip_ref[0]) + (chip_ref[1] + chip_ref[2])

    vm = pl.BlockSpec(memory_space=pltpu.VMEM)
    return pl.pallas_call(
        body, in_specs=[vm], out_specs=vm, out_shape=jax.ShapeDtypeStruct(packed.shape, F32),
        scratch_shapes=[pltpu.VMEM((rows, 128), F32), pltpu.VMEM((3, rows, 128), F32)] + _sem_pair(4),
        compiler_params=pltpu.CompilerParams(has_side_effects=True, vmem_limit_bytes=VMEM_LIMIT),
        name="allreduce_small",
    )(packed)


def _from_col_shards(g):
    n, rows, cols = g.shape
    return g.transpose(1, 0, 2).reshape(rows, n * cols)


def _train_step(x, tgt, g1, g2, g3, g4, shards, ln_g, ln_b, w_s, b_s, rel_bias, conv_w_shard, conv_b, batch,
                s_idx, c_idx):
    big = dict(tm=1024, out_dtype=F32)
    buckets = _bucket_tables()
    bias = _bias_tables(rel_bias, buckets)
    bz = jnp.repeat(b_s.T, HEAD_DIM, axis=1)
    w_st = jnp.swapaxes(w_s, 1, 2)

    def with_own(gathered, own):
        return lax.dynamic_update_index_in_dim(gathered, own, s_idx[0], 0)

    def shard_major(g):
        return g.reshape(N_SHARD, g.shape[0] // N_SHARD, D_MODEL)

    g_in, g_convw = _gather_weights([shards["w_in"]], conv_w_shard)
    w_in_t = with_own(g_in, shards["w_in"]).reshape(IN_COLS, D_MODEL)
    conv_w = _from_col_shards(with_own(g_convw, conv_w_shard))

    h1, uv, qkv = _proj_fwd(x, g1, w_in_t)
    a = _gate_fwd(uv, ln_g, ln_b, w_s, bz)
    later = ["w_out", "w_gate", "w_up", "w_down"]
    o_bf, lse, gathered = _attn_fwd(qkv, bias, batch, [shards[n] for n in later])
    g_out, g_gate, g_up, g_down = [with_own(g, shards[n]) for g, n in zip(gathered, later)]
    w_out = g_out.reshape(D_MODEL, D_MODEL)
    w_gate_t = g_gate.reshape(D_FF, D_MODEL)
    w_up_t = g_up.reshape(D_FF, D_MODEL)
    w_down = g_down.reshape(D_FF, D_MODEL)
    y1 = _mm(a, w_out[:A_WIDTH], dims="nn", tn=1024, tk=A_WIDTH, name="mm_out_a", **big)
    y1 = _mm(o_bf, w_out[A_WIDTH:], dims="nn", tn=1024, tk=B_WIDTH, name="mm_out_b", add=y1, **big)
    x1, h2 = _mid_fwd(x, y1, g2, g3)
    gate = _mm(h2, w_gate_t, dims="nt", tm=1024, tn=1408, tk=1024, out_dtype=BF16, name="mm_gate")
    up = _mm(h2, w_up_t, dims="nt", tm=1024, tn=1408, tk=1024, out_dtype=BF16, name="mm_up")
    act = _convgate_fwd(gate, up, conv_w, conv_b, batch)
    y2 = _mm(act, w_down, dims="nn", tn=1024, tk=D_FF, name="mm_down", **big)
    dx2, dy2, dg4, loss = _loss_head(x1, y2, tgt, g4)

    dact = _mm(dy2, w_down, dims="nt", tm=1024, tn=1408, tk=1024, out_dtype=BF16, name="mm_dact")
    dw_down = _mm(act, dy2, dims="tn", tm=1408, tn=1024, tk=1024, out_dtype=F32, name="mm_dw_down")
    dgate, dup, dconv_w, dconv_b = _convgate_bwd(gate, up, dact, conv_w, conv_b, batch)
    dh2 = _mm(dgate, w_gate_t, dims="nn", tn=1024, tk=D_FF, name="mm_dh2_g", **big)
    dh2 = _mm(dup, w_up_t, dims="nn", tn=1024, tk=D_FF, name="mm_dh2_u", add=dh2, **big)
    dw_gate_t = _mm(dgate, h2, dims="tn", tm=1408, tn=1024, tk=1024, out_dtype=F32, name="mm_dw_gate")
    dw_up_t = _mm(dup, h2, dims="tn", tm=1408, tn=1024, tk=1024, out_dtype=F32, name="mm_dw_up")
    dx1, dy1, dg2, dg3 = _mid_bwd(x1, y1, dh2, dx2, g2, g3)
    dmix = _mm(dy1, w_out, dims="nt", tn=1024, tk=1024, name="mm_dmix", **big)
    dw_out_a = _mm(a, dy1, dims="tn", tm=A_WIDTH, tn=1024, tk=1024, out_dtype=F32, name="mm_dw_out_a")
    dw_out_b = _mm(o_bf, dy1, dims="tn", tm=B_WIDTH, tn=1024, tk=1024, out_dtype=F32, name="mm_dw_out_b")

    dw_out = jnp.concatenate([dw_out_a, dw_out_b], axis=0)
    done = [shard_major(g) for g in (dw_down, dw_gate_t, dw_up_t, dw_out)]
    (duv, dln_g, dln_b, dw_s, dbz), recv_a = _gate_bwd(uv, dmix, ln_g, ln_b, w_s, w_st, bz, done)
    parts = [_add_halves(g, r, c_idx) for g, r in zip(done, recv_a)]
    dqkv, ds, recv = _attn_bwd(qkv, dmix, o_bf, lse, bias, batch, parts)
    fulls = [_add_chips(p, r, s_idx, c_idx) for p, r in zip(parts, recv)]
    drel = _rel_bias_grad(ds, buckets)
    dproj = _assemble_dproj(duv, dqkv)
    dw_in_t = _mm(dproj, h1, dims="tn", tm=1408, tn=1024, tk=1024, out_dtype=F32, name="mm_dw_in")
    last = [shard_major(dw_in_t)]
    part_in = [_add_halves(g, r, c_idx) for g, r in zip(last, _exchange_halves(last, "rs_sibling_exchange_in"))]
    dh1, recv_in = _mm(dproj, w_in_t, dims="nn", tn=1024, tk=D_FF, name="mm_dh1", exchange=part_in, **big)
    fulls += [_add_chips(p, r, s_idx, c_idx) for p, r in zip(part_in, recv_in)]
    dx0, dg1 = _in_bwd(x, dh1, dx1, g1)
    reduced = dict(zip(["w_down", "w_gate", "w_up", "w_out", "w_in"], _share_halves(fulls)))

    small = dict(
        norm_mix_pre=dg1, norm_mix_post=dg2, norm_ffn_pre=dg3, norm_ffn_post=dg4,
        ln_v_gain=dln_g, ln_v_bias=dln_b, spatial_w=dw_s,
        spatial_b=dbz[:, ::HEAD_DIM].T,
        rel_bias=drel.reshape(B_HEADS, NUM_BUCKETS).T,
        conv_w=dconv_w, conv_b=dconv_b,
    )
    return loss, dx0, small, reduced


def _adamw(w, g, m, v, name):
    rows, cols = w.shape
    tr = rows
    if rows * cols > 256 * 1024:
        tr = next(cand for cand in (256, 176, 128) if rows % cand == 0)

    def body(w_ref, g_ref, m_ref, v_ref, go_ref, d_ref, nm_ref, nv_ref):
        gv = g_ref[...]
        go_ref[...] = gv
        nm = ADAM_B1 * m_ref[...] + (1.0 - ADAM_B1) * gv
        nv = ADAM_B2 * v_ref[...] + (1.0 - ADAM_B2) * (gv * gv)
        m_hat = nm / (1.0 - ADAM_B1 ** ADAM_STEP)
        v_hat = nv / (1.0 - ADAM_B2 ** ADAM_STEP)
        d_ref[...] = -ADAM_LR * (m_hat / (jnp.sqrt(v_hat) + ADAM_EPS) + ADAM_WD * w_ref[...])
        nm_ref[...] = nm
        nv_ref[...] = nv

    spec = pl.BlockSpec((tr, cols), lambda i: (i, 0))
    sds = jax.ShapeDtypeStruct((rows, cols), F32)
    return pl.pallas_call(
        body, grid=(rows // tr,), in_specs=[spec] * 4, out_specs=[spec] * 4, out_shape=[sds] * 4,
        compiler_params=_params(("parallel",)), name=name,
    )(w, g, m, v)


def _pack(arrays, rows):
    flat = jnp.concatenate([a.reshape(-1) for a in arrays])
    flat = jnp.pad(flat, (0, rows * 128 - flat.shape[0]))
    return flat.reshape(rows, 128)


def _unpack(packed, shapes):
    flat = packed.reshape(-1)
    out, off = [], 0
    for sh in shapes:
        n = int(np.prod(sh))
        out.append(flat[off:off + n].reshape(sh))
        off += n
    return out


SMALL = ["norm_mix_pre", "norm_mix_post", "norm_ffn_pre", "norm_ffn_post", "ln_v_gain", "ln_v_bias",
         "spatial_w", "spatial_b", "rel_bias", "conv_b"]
LARGE = ["w_in", "w_gate", "w_up", "w_down", "w_out"]
TRANSPOSED = ("w_in", "w_gate", "w_up")
ORDER = ["norm_mix_pre", "norm_mix_post", "norm_ffn_pre", "norm_ffn_post", "w_in", "ln_v_gain", "ln_v_bias",
         "spatial_w", "spatial_b", "rel_bias", "w_out", "w_gate", "w_up", "conv_w", "conv_b", "w_down"]


def kernel(x, norm_mix_pre, norm_mix_post, norm_ffn_pre, norm_ffn_post, w_in, ln_v_gain, ln_v_bias, spatial_w, spatial_b, rel_bias, w_out, w_gate, w_up, conv_w, conv_b, w_down, loss_target, m_norm_mix_pre, m_norm_mix_post, m_norm_ffn_pre, m_norm_ffn_post, m_w_in, m_ln_v_gain, m_ln_v_bias, m_spatial_w, m_spatial_b, m_rel_bias, m_w_out, m_w_gate, m_w_up, m_conv_w, m_conv_b, m_w_down, v_norm_mix_pre, v_norm_mix_post, v_norm_ffn_pre, v_norm_ffn_post, v_w_in, v_ln_v_gain, v_ln_v_bias, v_spatial_w, v_spatial_b, v_rel_bias, v_w_out, v_w_gate, v_w_up, v_conv_w, v_conv_b, v_w_down):
    params = dict(norm_mix_pre=norm_mix_pre, norm_mix_post=norm_mix_post, norm_ffn_pre=norm_ffn_pre,
                  norm_ffn_post=norm_ffn_post, w_in=w_in, ln_v_gain=ln_v_gain, ln_v_bias=ln_v_bias,
                  spatial_w=spatial_w, spatial_b=spatial_b, rel_bias=rel_bias, w_out=w_out, w_gate=w_gate,
                  w_up=w_up, conv_w=conv_w, conv_b=conv_b, w_down=w_down)
    mom = dict(norm_mix_pre=m_norm_mix_pre, norm_mix_post=m_norm_mix_post, norm_ffn_pre=m_norm_ffn_pre,
               norm_ffn_post=m_norm_ffn_post, w_in=m_w_in, ln_v_gain=m_ln_v_gain, ln_v_bias=m_ln_v_bias,
               spatial_w=m_spatial_w, spatial_b=m_spatial_b, rel_bias=m_rel_bias, w_out=m_w_out, w_gate=m_w_gate,
               w_up=m_w_up, conv_w=m_conv_w, conv_b=m_conv_b, w_down=m_w_down)
    var = dict(norm_mix_pre=v_norm_mix_pre, norm_mix_post=v_norm_mix_post, norm_ffn_pre=v_norm_ffn_pre,
               norm_ffn_post=v_norm_ffn_post, w_in=v_w_in, ln_v_gain=v_ln_v_gain, ln_v_bias=v_ln_v_bias,
               spatial_w=v_spatial_w, spatial_b=v_spatial_b, rel_bias=v_rel_bias, w_out=v_w_out, w_gate=v_w_gate,
               w_up=v_w_up, conv_w=v_conv_w, conv_b=v_conv_b, w_down=v_w_down)

    batch = x.shape[0]
    xi, yi, ci = lax.axis_index("x"), lax.axis_index("y"), lax.axis_index("c")
    s_idx = (2 * xi + yi).astype(jnp.int32).reshape(1)
    c_idx = ci.astype(jnp.int32).reshape(1)

    def local(a, n):
        return jnp.swapaxes(a[0], 0, 1) if n in TRANSPOSED else a[0]

    shards = {n: local(params[n], n).astype(BF16) for n in LARGE}
    loss_part, dx0, grads, reduced = _train_step(
        x.reshape(batch * SEQ, D_MODEL), loss_target.reshape(batch * SEQ, D_MODEL),
        norm_mix_pre, norm_mix_post, norm_ffn_pre, norm_ffn_post, shards,
        ln_v_gain.reshape(1, A_WIDTH), ln_v_bias.reshape(1, A_WIDTH), spatial_w[0], spatial_b[0], rel_bias,
        conv_w[0], conv_b, batch, s_idx, c_idx)
    loss = lax.psum(loss_part[0, 0], ("x", "y", "c"))
    grad_x = dx0.reshape(batch, SEQ, D_MODEL)

    small_g = [grads[n].reshape(params[n].shape) for n in SMALL] + [grads["conv_w"]]
    n_small = sum(int(np.prod(g.shape)) for g in small_g)
    small_rows = -(-n_small // (8 * 128)) * 8
    summed = _unpack(_allreduce_small(_pack(small_g, small_rows)),
                     [params[n].shape for n in SMALL] + [(3, D_FF)])
    for n, g in zip(SMALL, summed[:-1]):
        reduced[n] = g
    reduced["conv_w"] = lax.dynamic_slice_in_dim(summed[-1], s_idx[0] * SHARD_FF, SHARD_FF, axis=1)[None]

    out_g, out_d, out_m, out_v = {}, {}, {}, {}
    for n in LARGE:
        res = _adamw(local(params[n], n), reduced[n], local(mom[n], n), local(var[n], n), name=f"adamw_{n}")
        if n in TRANSPOSED:
            res = [jnp.swapaxes(r, 0, 1) for r in res]
        out_g[n], out_d[n], out_m[n], out_v[n] = [r[None] for r in res]
    small_names = SMALL + ["conv_w"]
    rows_s = -(-sum(int(np.prod(params[n].shape)) for n in small_names) // (8 * 128)) * 8
    _, d, nm, nv = _adamw(_pack([params[n] for n in small_names], rows_s),
                          _pack([reduced[n] for n in small_names], rows_s),
                          _pack([mom[n] for n in small_names], rows_s), _pack([var[n] for n in small_names], rows_s),
                          name="adamw_small")
    shapes = [params[n].shape for n in small_names]
    for n, dd, mm, vv in zip(small_names, _unpack(d, shapes), _unpack(nm, shapes), _unpack(nv, shapes)):
        out_g[n], out_d[n], out_m[n], out_v[n] = reduced[n], dd, mm, vv

    return (loss, grad_x, *[out_g[n] for n in ORDER], *[out_d[n] for n in ORDER],
            *[out_m[n] for n in ORDER], *[out_v[n] for n in ORDER])
```

```python
import functools
import math

import numpy as np
import jax
import jax.numpy as jnp
from jax import lax
from jax.experimental import pallas as pl
from jax.experimental.pallas import tpu as pltpu

F32 = jnp.float32
BF16 = jnp.bfloat16
MESH = pl.DeviceIdType.MESH

D_MODEL = 1024
SEQ = 2048
HEAD_DIM = 64
A_GROUPS = 4
A_WIDTH = 256
B_HEADS = 12
B_WIDTH = 768
CHUNK = 128
DILATED = ((128, 1), (512, 4), (2048, 16))
NUM_BUCKETS = 32
MAX_DISTANCE = 2048
D_FF = 2816
IN_COLS = 2816
NORM_EPS = 1e-6
NEG_INF = -1e30
N_SHARD = 4
SHARD_FF = D_FF // N_SHARD
LANE_BLOCK = 256
VMEM_LIMIT = 56 * 1024 * 1024

ADAM_LR = 0.001
ADAM_B1 = 0.9
ADAM_B2 = 0.999
ADAM_EPS = 1e-08
ADAM_WD = 0.01
ADAM_STEP = 10

GELU_C = math.sqrt(2.0 / math.pi)
GELU_A = 0.044715

ANY = pl.BlockSpec(memory_space=pl.ANY)


def _params(sem=None):
    return pltpu.CompilerParams(dimension_semantics=sem, vmem_limit_bytes=VMEM_LIMIT)


def _dot(a, b, precision=None):
    return jnp.dot(a, b, preferred_element_type=F32, precision=precision)


def _dot_nt(a, b, precision=None):
    return lax.dot_general(a, b, (((1,), (1,)), ((), ())), preferred_element_type=F32, precision=precision)


def _dot_tn(a, b):
    return lax.dot_general(a, b, (((0,), (0,)), ((), ())), preferred_element_type=F32)


def _gelu(x):
    t = jnp.tanh(GELU_C * (x + GELU_A * (x * x * x)))
    return 0.5 * x * (1.0 + t)


def _gelu_and_grad(x):
    x2 = x * x
    t = jnp.tanh(GELU_C * (x + GELU_A * (x2 * x)))
    g = 0.5 * x * (1.0 + t)
    dg = 0.5 * (1.0 + t) + 0.5 * x * (1.0 - t * t) * (GELU_C * (1.0 + 3.0 * GELU_A * x2))
    return g, dg


def _mesh_pos():
    x, y, c = lax.axis_index("x"), lax.axis_index("y"), lax.axis_index("c")
    chips = [(1 - x, y), (x, 1 - y), (1 - x, 1 - y)]
    return x, y, c, chips


class _GatherPlan:
    def __init__(self, shapes, shard_refs, out_refs, send_sems, recv_sems):
        self.shapes, self.shard_refs, self.out_refs = shapes, shard_refs, out_refs
        self.send_sems, self.recv_sems = send_sems, recv_sems
        self.x, self.y, self.c, self.chips = _mesh_pos()
        self.sib = (self.x, self.y, 1 - self.c)

    def _half(self, t, chip, which):
        rows = self.shapes[t][0] // 2
        return self.out_refs[t].at[2 * chip[0] + chip[1], pl.ds(which * rows, rows), :]

    def _copy(self, k, src, dst, to):
        return pltpu.make_async_remote_copy(src_ref=src, dst_ref=dst, send_sem=self.send_sems.at[k],
                                            recv_sem=self.recv_sems.at[k], device_id=to, device_id_type=MESH)

    def _sends(self, t):
        rows = self.shapes[t][0] // 2
        src = self.shard_refs[t].at[pl.ds(self.c * rows, rows), :]
        return [self._copy(6 * t + j, src, self._half(t, (self.x, self.y), self.c), (*chip, self.c))
                for j, chip in enumerate(self.chips)]

    def _forwards(self, t):
        return [self._copy(6 * t + 3 + j, self._half(t, chip, self.c), self._half(t, chip, self.c), self.sib)
                for j, chip in enumerate(self.chips)]

    def start(self, ts):
        for t in ts:
            for cp in self._sends(t):
                cp.start()

    def forward(self, ts):
        for t in ts:
            for j, chip in enumerate(self.chips):
                landed = self._half(t, chip, self.c)
                self._copy(6 * t + j, landed, landed, (*chip, self.c)).wait_recv()
            for cp in self._forwards(t):
                cp.start()

    def finish(self, ts):
        for t in ts:
            for j, chip in enumerate(self.chips):
                other = self._half(t, chip, 1 - self.c)
                self._copy(6 * t + 3 + j, other, other, self.sib).wait_recv()
        for t in ts:
            for cp in self._sends(t) + self._forwards(t):
                cp.wait_send()


class _SiblingExchangePlan:
    def __init__(self, shapes, grad_refs, out_refs, send_sems, recv_sems):
        self.shapes, self.grad_refs, self.out_refs = shapes, grad_refs, out_refs
        self.send_sems, self.recv_sems = send_sems, recv_sems
        self.x, self.y, self.c, _ = _mesh_pos()

    def _copies(self):
        out = []
        for t, (g, o) in enumerate(zip(self.grad_refs, self.out_refs)):
            rows = self.shapes[t][1] // 2
            out.append(pltpu.make_async_remote_copy(
                src_ref=g.at[:, pl.ds((1 - self.c) * rows, rows), :], dst_ref=o, send_sem=self.send_sems.at[t],
                recv_sem=self.recv_sems.at[t], device_id=(self.x, self.y, 1 - self.c), device_id_type=MESH))
        return out

    def start(self):
        for cp in self._copies():
            cp.start()

    def finish(self):
        for cp in self._copies():
            cp.wait()


class _ChipExchangePlan:
    def __init__(self, part_refs, out_refs, send_sems, recv_sems):
        self.part_refs, self.out_refs, self.send_sems, self.recv_sems = part_refs, out_refs, send_sems, recv_sems
        _, _, self.c, self.chips = _mesh_pos()

    def _copies(self):
        return [pltpu.make_async_remote_copy(
            src_ref=p.at[2 * chip[0] + chip[1]], dst_ref=o.at[j], send_sem=self.send_sems.at[3 * t + j],
            recv_sem=self.recv_sems.at[3 * t + j], device_id=(*chip, self.c), device_id_type=MESH)
            for t, (p, o) in enumerate(zip(self.part_refs, self.out_refs)) for j, chip in enumerate(self.chips)]

    def start(self):
        for cp in self._copies():
            cp.start()

    def finish(self):
        for cp in self._copies():
            cp.wait()


def _sem_pair(n):
    return [pltpu.SemaphoreType.DMA((n,)), pltpu.SemaphoreType.DMA((n,))]


def _mm(a, b, *, dims, tm, tn, tk, out_dtype, name):
    if dims == "nn":
        m, k = a.shape
        n = b.shape[1]
        a_spec = pl.BlockSpec((tm, tk), lambda i, j, kk: (i, kk))
        b_spec = pl.BlockSpec((tk, tn), lambda i, j, kk: (kk, j))
        dot = _dot
    elif dims == "nt":
        m, k = a.shape
        n = b.shape[0]
        a_spec = pl.BlockSpec((tm, tk), lambda i, j, kk: (i, kk))
        b_spec = pl.BlockSpec((tn, tk), lambda i, j, kk: (j, kk))
        dot = _dot_nt
    else:
        k, m = a.shape
        n = b.shape[1]
        a_spec = pl.BlockSpec((tk, tm), lambda i, j, kk: (kk, i))
        b_spec = pl.BlockSpec((tk, tn), lambda i, j, kk: (kk, j))
        dot = _dot_tn
    assert m % tm == 0 and n % tn == 0 and k % tk == 0, (name, m, n, k)
    grid = (m // tm, n // tn, k // tk)
    nk = grid[2]
    assert nk == 1 or out_dtype == F32, name

    def body(a_ref, b_ref, o_ref):
        prod = dot(a_ref[...].astype(BF16), b_ref[...].astype(BF16))
        if nk == 1:
            o_ref[...] = prod.astype(out_dtype)
        else:
            kk = pl.program_id(2)

            @pl.when(kk == 0)
            def _():
                o_ref[...] = prod

            @pl.when(kk > 0)
            def _():
                o_ref[...] += prod

    return pl.pallas_call(
        body, grid=grid, in_specs=[a_spec, b_spec],
        out_specs=pl.BlockSpec((tm, tn), lambda i, j, kk: (i, j)),
        out_shape=jax.ShapeDtypeStruct((m, n), out_dtype),
        compiler_params=_params(("parallel", "parallel", "arbitrary")), name=name,
    )(a, b)


def _fused_rows(name, tm, mats, rows, vecs, fn, row_outs, acc_outs, exchange=()):
    m = mats[0][0].shape[0]
    nm, nr, nv, nro, nao, nx = len(mats), len(rows), len(vecs), len(row_outs), len(acc_outs), len(exchange)
    n_steps = m // tm

    def body(*refs):
        a_refs, w_refs = refs[:nm], refs[nm:2 * nm]
        pos = 2 * nm
        row_refs, vec_refs, part_refs = refs[pos:pos + nr], refs[pos + nr:pos + nr + nv], refs[pos + nr + nv:pos + nr + nv + nx]
        pos += nr + nv + nx
        out_refs, acc_refs, recv_refs = refs[pos:pos + nro], refs[pos + nro:pos + nro + nao], refs[pos + nro + nao:pos + nro + nao + nx]
        sems = refs[pos + nro + nao + nx:]
        i = pl.program_id(0)
        if nx:
            plan = _ChipExchangePlan(part_refs, recv_refs, *sems)

            @pl.when(i == 0)
            def _():
                plan.start()

        @pl.when(i == 0)
        def _():
            for r in acc_refs:
                r[...] = jnp.zeros_like(r)

        y = None
        for a_ref, w_ref, (_, _, dims, sl) in zip(a_refs, w_refs, mats):
            w = w_ref[...] if sl is None else w_ref[sl, :]
            part = (_dot if dims == "nn" else _dot_nt)(a_ref[...], w)
            y = part if y is None else y + part
        res = fn(y, *[r[...] for r in row_refs], *[v[...] for v in vec_refs])
        for r, val in zip(out_refs, res[:nro]):
            r[...] = val.astype(r.dtype)
        for r, val in zip(acc_refs, res[nro:]):
            r[...] += val

        if nx:
            @pl.when(i == n_steps - 1)
            def _():
                plan.finish()

    tile = lambda width: pl.BlockSpec((tm, width), lambda i: (i, 0))
    res = pl.pallas_call(
        body, grid=(n_steps,),
        in_specs=[tile(a.shape[1]) for a, _, _, _ in mats] + [_full_spec(w.shape) for _, w, _, _ in mats]
        + [tile(D_MODEL)] * nr + [_full_spec((1, D_MODEL))] * nv + [ANY] * nx,
        out_specs=[tile(D_MODEL)] * nro + [_full_spec(s) for s in acc_outs] + [ANY] * nx,
        out_shape=[jax.ShapeDtypeStruct((m, D_MODEL), dt) for dt in row_outs]
        + [jax.ShapeDtypeStruct(s, F32) for s in acc_outs]
        + [jax.ShapeDtypeStruct((3,) + p.shape[1:], p.dtype) for p in exchange],
        scratch_shapes=_sem_pair(3 * nx) if nx else [],
        compiler_params=_params(("arbitrary",)), name=name,
    )(*[a for a, _, _, _ in mats], *[w for _, w, _, _ in mats], *rows, *vecs, *exchange)
    return list(res[:nro + nao]), list(res[nro + nao:])


ROW_TILE = 512


def _row_spec(width=D_MODEL):
    return pl.BlockSpec((ROW_TILE, width), lambda i: (i, 0))


def _vec_spec(width=D_MODEL):
    return pl.BlockSpec((1, width), lambda i: (0, 0))


def _rstd(v):
    return lax.rsqrt(jnp.mean(v * v, axis=-1, keepdims=True) + NORM_EPS)


def _mid_fwd_rows(y1, x0, g2, g3):
    x1 = x0 + y1 * _rstd(y1) * g2
    return y1, x1, x1 * _rstd(x1) * g3


def _rms_bwd_rows(dout, v, g):
    r = _rstd(v)
    n = v * r
    dn = dout * g
    dv = r * (dn - n * jnp.mean(dn * n, axis=-1, keepdims=True))
    dg = jnp.sum(dout * n, axis=0, keepdims=True)
    return dv, dg


def _loss_head_rows(y2, x1, tgt, g4):
    x2 = x1 + y2 * _rstd(y2) * g4
    err = x2 - tgt
    loss = 0.5 * jnp.sum(jnp.mean(err * err, axis=-1, keepdims=True), axis=0, keepdims=True)
    dx2 = err * (1.0 / D_MODEL)
    dy2, dg4 = _rms_bwd_rows(dx2, y2, g4)
    return dx2, dy2, dg4, loss


def _mid_bwd_rows(dh2, x1, y1, dx2, g2, g3):
    d3, dg3 = _rms_bwd_rows(dh2, x1, g3)
    dx1 = dx2 + d3
    dy1, dg2 = _rms_bwd_rows(dx1, y1, g2)
    return dx1, dy1, dg2, dg3


def _in_bwd_rows(dh1, x0, dx1, g1):
    d1, dg1 = _rms_bwd_rows(dh1, x0, g1)
    return dx1 + d1, dg1


GATE_ROWS = 512


def _group_mean_matrix():
    p = np.zeros((A_WIDTH, A_WIDTH), np.float32)
    for g in range(A_GROUPS):
        p[g * HEAD_DIM:(g + 1) * HEAD_DIM, g * HEAD_DIM:(g + 1) * HEAD_DIM] = 1.0 / HEAD_DIM
    return jnp.asarray(p)


def _group_masks(width=A_WIDTH):
    lane = lax.broadcasted_iota(jnp.int32, (1, width), 1)
    return [(lane >= g * HEAD_DIM) & (lane < (g + 1) * HEAD_DIM) for g in range(width // HEAD_DIM)]


def _layernorm_groups(vg, pavg):
    hi = lax.Precision.HIGHEST
    mu = _dot(vg, pavg, hi)
    xc = vg - mu
    var = _dot(xc * xc, pavg, hi)
    rstd = lax.rsqrt(var + NORM_EPS)
    return xc * rstd, rstd


def _spatial_mix(w_bf, vn_chunk_bf, masks, bz):
    z = bz
    for g in range(A_GROUPS):
        z = z + jnp.where(masks[g], _dot(w_bf[g], vn_chunk_bf), 0.0)
    return z


def _full_spec(shape):
    return pl.BlockSpec(shape, lambda i: tuple(0 for _ in shape))


def _gate_fwd(uv, ln_g, ln_b, w_s, bz):
    m = uv.shape[0]
    pavg = _group_mean_matrix()

    def body(u_ref, v_ref, lg_ref, lb_ref, w_ref, bz_ref, p_ref, a_ref):
        masks = _group_masks()
        row = lax.broadcasted_iota(jnp.int32, (CHUNK, CHUNK), 0)
        col = lax.broadcasted_iota(jnp.int32, (CHUNK, CHUNK), 1)
        w_bf = [jnp.where(row >= col, w_ref[g], 0.0).astype(BF16) for g in range(A_GROUPS)]
        ug = _gelu(u_ref[...])
        vhat, _ = _layernorm_groups(_gelu(v_ref[...]), p_ref[...])
        vn = vhat * lg_ref[...] + lb_ref[...]
        bz = bz_ref[...]
        for c in range(GATE_ROWS // CHUNK):
            sl = slice(c * CHUNK, (c + 1) * CHUNK)
            z = _spatial_mix(w_bf, vn[sl].astype(BF16), masks, bz)
            a_ref[sl, :] = (ug[sl] * z).astype(BF16)

    return pl.pallas_call(
        body, grid=(m // GATE_ROWS,),
        in_specs=[pl.BlockSpec((GATE_ROWS, A_WIDTH), lambda i: (i, 0)),
                  pl.BlockSpec((GATE_ROWS, A_WIDTH), lambda i: (i, 1)),
                  _full_spec((1, A_WIDTH)), _full_spec((1, A_WIDTH)), _full_spec((A_GROUPS, CHUNK, CHUNK)),
                  _full_spec((CHUNK, A_WIDTH)), _full_spec((A_WIDTH, A_WIDTH))],
        out_specs=pl.BlockSpec((GATE_ROWS, A_WIDTH), lambda i: (i, 0)),
        out_shape=jax.ShapeDtypeStruct((m, A_WIDTH), BF16),
        compiler_params=_params(("parallel",)), name="gate_fwd",
    )(uv, uv, ln_g, ln_b, w_s, bz, pavg)


def _gate_bwd(uv, dmix, ln_g, ln_b, w_s, w_st, bz, grads):
    m = uv.shape[0]
    pavg = _group_mean_matrix()
    nsteps = m // GATE_ROWS
    nx = len(grads)
    shapes = [g.shape for g in grads]

    def body(u_ref, v_ref, da_ref, lg_ref, lb_ref, w_ref, wt_ref, bz_ref, p_ref, *rest):
        grad_refs = rest[:nx]
        duv_ref, dlg_ref, dlb_ref, dw_ref, dbz_ref = rest[nx:nx + 5]
        recv_refs = rest[nx + 5:2 * nx + 5]
        exchange = _SiblingExchangePlan(shapes, grad_refs, recv_refs, *rest[2 * nx + 5:])
        i = pl.program_id(0)

        @pl.when(i == 0)
        def _():
            exchange.start()
            dlg_ref[...] = jnp.zeros_like(dlg_ref)
            dlb_ref[...] = jnp.zeros_like(dlb_ref)
            dw_ref[...] = jnp.zeros_like(dw_ref)
            dbz_ref[...] = jnp.zeros_like(dbz_ref)

        hi = lax.Precision.HIGHEST
        masks = _group_masks()
        row = lax.broadcasted_iota(jnp.int32, (CHUNK, CHUNK), 0)
        col = lax.broadcasted_iota(jnp.int32, (CHUNK, CHUNK), 1)
        tril = row >= col
        w_bf = [jnp.where(tril, w_ref[g], 0.0).astype(BF16) for g in range(A_GROUPS)]
        wt_bf = [jnp.where(col >= row, wt_ref[g], 0.0).astype(BF16) for g in range(A_GROUPS)]
        pavg_v = p_ref[...]
        lg = lg_ref[...]
        ug, dug = _gelu_and_grad(u_ref[...])
        vg, dvg_dx = _gelu_and_grad(v_ref[...])
        vhat, rstd = _layernorm_groups(vg, pavg_v)
        vn = vhat * lg + lb_ref[...]
        da = da_ref[...]
        bz = bz_ref[...]
        for c in range(GATE_ROWS // CHUNK):
            sl = slice(c * CHUNK, (c + 1) * CHUNK)
            vn_bf = vn[sl].astype(BF16)
            z = _spatial_mix(w_bf, vn_bf, masks, bz)
            dz = da[sl] * ug[sl]
            duv_ref[sl, 0:A_WIDTH] = da[sl] * z * dug[sl]
            dbz_ref[...] += dz
            dz_bf = dz.astype(BF16)
            dvn = jnp.zeros((CHUNK, A_WIDTH), F32)
            for g in range(A_GROUPS):
                dz_g = jnp.where(masks[g], dz, 0.0).astype(BF16)
                dw_ref[g] += jnp.where(tril, _dot_nt(dz_g, vn_bf), 0.0)
                dvn = dvn + jnp.where(masks[g], _dot(wt_bf[g], dz_bf), 0.0)
            vh = vhat[sl]
            dlb_ref[...] += jnp.sum(dvn, axis=0, keepdims=True)
            dlg_ref[...] += jnp.sum(dvn * vh, axis=0, keepdims=True)
            dvh = dvn * lg
            m1 = _dot(dvh, pavg_v, hi)
            m2 = _dot(dvh * vh, pavg_v, hi)
            duv_ref[sl, A_WIDTH:2 * A_WIDTH] = rstd[sl] * (dvh - m1 - vh * m2) * dvg_dx[sl]

        @pl.when(i == nsteps - 1)
        def _():
            dbz_ref[...] = _dot(dbz_ref[...], pavg_v * float(HEAD_DIM), hi)
            exchange.finish()

    res = pl.pallas_call(
        body, grid=(nsteps,),
        in_specs=[pl.BlockSpec((GATE_ROWS, A_WIDTH), lambda i: (i, 0)),
                  pl.BlockSpec((GATE_ROWS, A_WIDTH), lambda i: (i, 1)),
                  pl.BlockSpec((GATE_ROWS, A_WIDTH), lambda i: (i, 0)),
                  _full_spec((1, A_WIDTH)), _full_spec((1, A_WIDTH)), _full_spec((A_GROUPS, CHUNK, CHUNK)),
                  _full_spec((A_GROUPS, CHUNK, CHUNK)), _full_spec((CHUNK, A_WIDTH)),
                  _full_spec((A_WIDTH, A_WIDTH))] + [ANY] * nx,
        out_specs=[pl.BlockSpec((GATE_ROWS, 2 * A_WIDTH), lambda i: (i, 0)),
                   _full_spec((1, A_WIDTH)), _full_spec((1, A_WIDTH)), _full_spec((A_GROUPS, CHUNK, CHUNK)),
                   _full_spec((CHUNK, A_WIDTH))] + [ANY] * nx,
        out_shape=[jax.ShapeDtypeStruct((m, 2 * A_WIDTH), F32),
                   jax.ShapeDtypeStruct((1, A_WIDTH), F32), jax.ShapeDtypeStruct((1, A_WIDTH), F32),
                   jax.ShapeDtypeStruct((A_GROUPS, CHUNK, CHUNK), F32),
                   jax.ShapeDtypeStruct((CHUNK, A_WIDTH), F32)]
        + [jax.ShapeDtypeStruct((N_SHARD, s[1] // 2, s[2]), F32) for s in shapes],
        scratch_shapes=_sem_pair(nx),
        compiler_params=_params(("arbitrary",)), name="gate_bwd",
    )(uv, uv, dmix, ln_g, ln_b, w_s, w_st, bz, pavg, *grads)
    return res[:5], list(res[5:])


Q_BLOCK = 128
PAIR = 2 * HEAD_DIM
N_PAIR = B_HEADS // 2
N_CFG = len(DILATED)
BLOCKS_PER_CFG = SEQ // Q_BLOCK
QKV_SLABS = 3 * N_PAIR
FWD_BLOCKS_PER_TRIP = 8
BWD_BLOCKS_PER_TRIP = 4


def _t5_bucket_np(dist, dtype):
    max_exact = NUM_BUCKETS // 2
    d = np.maximum(dist, 1).astype(dtype)
    large = max_exact + (np.log(d / dtype(max_exact)) / dtype(math.log(MAX_DISTANCE / max_exact))
                         * dtype(NUM_BUCKETS - max_exact))
    large = np.minimum(large.astype(np.int32), NUM_BUCKETS - 1)
    return np.where(dist < max_exact, dist, large)


def _bucket_tables():
    i = np.arange(Q_BLOCK)[:, None]
    j = np.arange(Q_BLOCK)[None, :]
    tables = []
    for _, dil in DILATED:
        rel_prev = Q_BLOCK + i - j
        rel_cur = i - j
        rel = np.concatenate([rel_prev, rel_cur], axis=1)
        valid = np.concatenate([rel_prev <= Q_BLOCK, rel_cur >= 0], axis=1)
        dist = np.maximum(rel, 0) * dil
        b32 = _t5_bucket_np(dist, np.float32)
        b64 = _t5_bucket_np(dist, np.float64)
        assert np.array_equal(b32, b64)
        tables.append(np.where(valid, b32, -1).astype(np.int32))
    return np.stack(tables)


def _present_buckets(buckets_np):
    return [sorted(set(int(v) for v in np.unique(buckets_np[c]) if v >= 0)) for c in range(N_CFG)]


def _bias_tables(rel_bias, buckets_np):
    present = _present_buckets(buckets_np)

    def body(rb_ref, bk_ref, o_ref):
        for c in range(N_CFG):
            bk = bk_ref[c]
            for h in range(B_HEADS):
                acc = jnp.full((Q_BLOCK, 2 * Q_BLOCK), NEG_INF, F32)
                for b in present[c]:
                    acc = jnp.where(bk == b, rb_ref[b, h], acc)
                o_ref[c, h] = acc

    return pl.pallas_call(
        body,
        in_specs=[pl.BlockSpec(memory_space=pltpu.SMEM), pl.BlockSpec(memory_space=pltpu.VMEM)],
        out_specs=pl.BlockSpec(memory_space=pltpu.VMEM),
        out_shape=jax.ShapeDtypeStruct((N_CFG, B_HEADS, Q_BLOCK, 2 * Q_BLOCK), F32),
        compiler_params=_params(), name="bias_tables",
    )(rel_bias, jnp.asarray(buckets_np))


def _proj_fwd(x, g1, w_in_t):
    m = x.shape[0]
    tm = ROW_TILE

    def body(x_ref, g_ref, w_ref, h_ref, uv_ref, qkv_ref):
        xv = x_ref[...]
        h = (xv * _rstd(xv) * g_ref[...]).astype(BF16)
        h_ref[...] = h
        acc = _dot_nt(h, w_ref[...])
        uv_ref[...] = acc[:, :2 * A_WIDTH]
        for s in range(QKV_SLABS):
            qkv_ref[s] = acc[:, 2 * A_WIDTH + s * PAIR:2 * A_WIDTH + (s + 1) * PAIR]

    return pl.pallas_call(
        body, grid=(m // tm,),
        in_specs=[pl.BlockSpec((tm, D_MODEL), lambda i: (i, 0)), _vec_spec(),
                  pl.BlockSpec((IN_COLS, D_MODEL), lambda i: (0, 0))],
        out_specs=[pl.BlockSpec((tm, D_MODEL), lambda i: (i, 0)),
                   pl.BlockSpec((tm, 2 * A_WIDTH), lambda i: (i, 0)),
                   pl.BlockSpec((QKV_SLABS, tm, PAIR), lambda i: (0, i, 0))],
        out_shape=[jax.ShapeDtypeStruct((m, D_MODEL), BF16), jax.ShapeDtypeStruct((m, 2 * A_WIDTH), F32),
                   jax.ShapeDtypeStruct((QKV_SLABS, m, PAIR), F32)],
        compiler_params=_params(("parallel",)), name="proj_fwd",
    )(x, g1, w_in_t)


def _pair_masks():
    lane = lax.broadcasted_iota(jnp.int32, (1, PAIR), 1)
    return [lane < HEAD_DIM, lane >= HEAD_DIM]


def _block_rows(idx, dil):
    if dil == 1:
        n = idx
        cur = pl.ds(pl.multiple_of(n * Q_BLOCK, Q_BLOCK), Q_BLOCK)
        prev = pl.ds(pl.multiple_of(jnp.maximum(n - 1, 0) * Q_BLOCK, Q_BLOCK), Q_BLOCK)
        return n, cur, prev
    r = idx % dil
    n = idx // dil
    cur = pl.ds(r + (dil * Q_BLOCK) * n, Q_BLOCK, stride=dil)
    prev = pl.ds(r + (dil * Q_BLOCK) * jnp.maximum(n - 1, 0), Q_BLOCK, stride=dil)
    return n, cur, prev


def _attn_fwd(qkv, bias, batch, shards):
    m = qkv.shape[1]
    comb_rows = 256
    nt = len(shards)
    shapes = [sh.shape for sh in shards]
    n_steps = batch * N_PAIR
    early, late = list(range(nt // 2)), list(range(nt // 2, nt))

    def body(q_ref, k_ref, v_ref, b_ref, *rest):
        shard_refs = rest[:nt]
        o_ref, l_ref = rest[nt:nt + 2]
        gat_refs = rest[nt + 2:2 * nt + 2]
        scratch = rest[2 * nt + 2:]
        oc_refs, lc_refs = scratch[:N_CFG], scratch[N_CFG:2 * N_CFG]
        step = pl.program_id(0) * N_PAIR + pl.program_id(1)
        gather = _GatherPlan(shapes, shard_refs, gat_refs, *scratch[2 * N_CFG:])

        @pl.when(step == 0)
        def _():
            gather.start(early + late)

        @pl.when(step == n_steps // 2)
        def _():
            gather.forward(early)

        @pl.when(step == n_steps - 2)
        def _():
            gather.forward(late)

        masks = _pair_masks()
        for ci, (_, dil) in enumerate(DILATED):
            nb = SEQ // dil // Q_BLOCK

            def block(trip, carry, ci=ci, dil=dil, nb=nb):
                work = []
                for u in range(FWD_BLOCKS_PER_TRIP):
                    n, rows, prow = _block_rows(trip * FWD_BLOCKS_PER_TRIP + u, dil)
                    q = q_ref[rows, :] * 0.125
                    kc = k_ref[rows, :].astype(BF16)
                    vc = v_ref[rows, :]
                    kp = k_ref[prow, :].astype(BF16) if nb > 1 else None
                    vp = v_ref[prow, :] if nb > 1 else None
                    tiles = []
                    for h in range(2):
                        qh = jnp.where(masks[h], q, 0.0).astype(BF16)
                        sc = _dot_nt(qh, kc) + b_ref[ci, h, :, Q_BLOCK:]
                        sp = None
                        if nb > 1:
                            sp = _dot_nt(qh, kp) + jnp.where(n == 0, NEG_INF, b_ref[ci, h, :, :Q_BLOCK])
                        tiles.append((sc, sp))
                    work.append((rows, vc, vp, tiles))
                probs = []
                for _, _, _, tiles in work:
                    ps = []
                    for sc, sp in tiles:
                        mx = jnp.max(sc if sp is None else jnp.maximum(sc, sp), axis=1, keepdims=True)
                        pc = jnp.exp(sc - mx).astype(BF16)
                        pp = None if sp is None else jnp.exp(sp - mx).astype(BF16)
                        ps.append((mx, pc, pp))
                    probs.append(ps)
                for (rows, vc, vp, _), ps in zip(work, probs):
                    res = []
                    for h, (_, pc, pp) in enumerate(ps):
                        r = _dot(pc, jnp.where(masks[h], vc, 1.0).astype(BF16))
                        if pp is not None:
                            r = r + _dot(pp, jnp.where(masks[h], vp, 1.0).astype(BF16))
                        res.append(r)
                    num = jnp.where(masks[0], res[0], res[1])
                    den = pltpu.roll(jnp.where(masks[0], res[1], res[0]), HEAD_DIM, 1)
                    oc_refs[ci][rows, :] = num / den
                    lc_refs[ci][rows, :] = jnp.where(masks[0], ps[0][0], ps[1][0]) + jnp.log(den)
                return carry

            lax.fori_loop(0, BLOCKS_PER_CFG // FWD_BLOCKS_PER_TRIP, block, 0)

        def combine(i, carry):
            rr = pl.ds(pl.multiple_of(i * comb_rows, comb_rows), comb_rows)
            ls = [lc_refs[c][rr, :] for c in range(N_CFG)]
            mx = functools.reduce(jnp.maximum, ls)
            ws = [jnp.exp(l - mx) for l in ls]
            tot = functools.reduce(lambda a, b: a + b, ws)
            o = functools.reduce(lambda a, b: a + b, [ws[c] * oc_refs[c][rr, :] for c in range(N_CFG)]) / tot
            o_ref[rr, :] = o.astype(BF16)
            l_ref[rr, :] = mx + jnp.log(tot)
            return carry

        lax.fori_loop(0, SEQ // comb_rows, combine, 0)

        @pl.when(step == n_steps - 1)
        def _():
            gather.finish(early + late)

    def slab(first):
        return pl.BlockSpec((None, SEQ, PAIR), lambda b, p: (first + p, b, 0))

    nat = pl.BlockSpec((SEQ, PAIR), lambda b, p: (b, p))
    res = pl.pallas_call(
        body, grid=(batch, N_PAIR),
        in_specs=[slab(0), slab(N_PAIR), slab(2 * N_PAIR),
                  pl.BlockSpec((N_CFG, 2, Q_BLOCK, 2 * Q_BLOCK), lambda b, p: (0, p, 0, 0))] + [ANY] * nt,
        out_specs=[nat, nat] + [ANY] * nt,
        out_shape=[jax.ShapeDtypeStruct((m, B_WIDTH), BF16), jax.ShapeDtypeStruct((m, B_WIDTH), F32)]
        + [jax.ShapeDtypeStruct((N_SHARD,) + sh.shape, sh.dtype) for sh in shards],
        scratch_shapes=[pltpu.VMEM((SEQ, PAIR), F32)] * (2 * N_CFG) + _sem_pair(6 * nt),
        compiler_params=_params(("arbitrary", "arbitrary")), name="attn_fwd",
    )(qkv, qkv, qkv, bias, *shards)
    return res[0], res[1], list(res[2:])


def _attn_bwd(qkv, dmix, o, lse, bias_t, batch, parts):
    m = qkv.shape[1]
    nt = len(parts)
    n_steps = N_PAIR * batch

    def body(q_ref, k_ref, v_ref, do_ref, o_ref, l_ref, b_ref, *rest):
        part_refs = rest[:nt]
        dqkv_ref, ds_ref = rest[nt:nt + 2]
        recv_refs = rest[nt + 2:2 * nt + 2]
        dq_acc, dk_acc, dv_acc, d_scr, send_sems, recv_sems = rest[2 * nt + 2:]
        step = pl.program_id(0) * batch + pl.program_id(1)
        exchange = _ChipExchangePlan(part_refs, recv_refs, send_sems, recv_sems)

        @pl.when(step == 0)
        def _():
            exchange.start()

        @pl.when(pl.program_id(1) == 0)
        def _():
            ds_ref[...] = jnp.zeros_like(ds_ref)

        dq_acc[...] = jnp.zeros_like(dq_acc)
        dk_acc[...] = jnp.zeros_like(dk_acc)
        dv_acc[...] = jnp.zeros_like(dv_acc)
        d_scr[...] = do_ref[...] * o_ref[...].astype(F32)
        masks = _pair_masks()
        key_row = lax.broadcasted_iota(jnp.int32, (2 * Q_BLOCK, 2 * Q_BLOCK), 0)

        def stack_heads(t):
            return jnp.concatenate([jnp.where(masks[0], t, 0.0), jnp.where(masks[1], t, 0.0)], axis=0).astype(BF16)

        for ci, (_, dil) in enumerate(DILATED):
            nb = SEQ // dil // Q_BLOCK

            def block(trip, carry, ci=ci, dil=dil, nb=nb):
                first = []
                for u in range(BWD_BLOCKS_PER_TRIP):
                    n, rows, prow = _block_rows(trip * BWD_BLOCKS_PER_TRIP + u, dil)
                    if nb > 1:
                        kcat = jnp.concatenate([k_ref[prow, :], k_ref[rows, :]], axis=0).astype(BF16)
                        vcat = jnp.concatenate([v_ref[prow, :], v_ref[rows, :]], axis=0).astype(BF16)
                    else:
                        kcat = k_ref[rows, :].astype(BF16)
                        vcat = v_ref[rows, :].astype(BF16)
                    qst = stack_heads(q_ref[rows, :] * 0.125)
                    dost = stack_heads(do_ref[rows, :])
                    lt = l_ref[rows, :].T
                    dt = d_scr[rows, :].T
                    lrow = jnp.concatenate([lt[0:1], lt[HEAD_DIM:HEAD_DIM + 1]], axis=1)
                    drow = jnp.concatenate([jnp.sum(dt[:HEAD_DIM], axis=0, keepdims=True),
                                            jnp.sum(dt[HEAD_DIM:], axis=0, keepdims=True)], axis=1)
                    first.append((n, rows, prow, kcat, qst, dost, lrow, drow, _dot_nt(kcat, qst), _dot_nt(vcat, dost)))
                second = []
                for n, rows, prow, kcat, qst, dost, lrow, drow, st, dpt in first:
                    if nb > 1:
                        bt = jnp.concatenate([b_ref[ci, 0], b_ref[ci, 1]], axis=1)
                        bt = jnp.where((n == 0) & (key_row < Q_BLOCK), NEG_INF, bt)
                    else:
                        bt = jnp.concatenate([b_ref[ci, 0, Q_BLOCK:, :], b_ref[ci, 1, Q_BLOCK:, :]], axis=1)
                    pt = jnp.exp(st + bt - lrow)
                    dst = pt * (dpt - drow)
                    keys = slice(0, 2 * Q_BLOCK) if nb > 1 else slice(Q_BLOCK, 2 * Q_BLOCK)
                    ds_ref[ci, 0, keys, :] += dst[:, :Q_BLOCK]
                    ds_ref[ci, 1, keys, :] += dst[:, Q_BLOCK:]
                    second.append((rows, prow, kcat, qst, dost, pt.astype(BF16), dst.astype(BF16)))
                for rows, prow, kcat, qst, dost, pt_bf, dst_bf in second:
                    dk = _dot(dst_bf, qst)
                    dv = _dot(pt_bf, dost)
                    dq2 = _dot_tn(dst_bf, kcat)
                    dq_acc[rows, :] += jnp.where(masks[0], dq2[:Q_BLOCK], dq2[Q_BLOCK:]) * 0.125
                    if nb > 1:
                        dk_acc[prow, :] += dk[:Q_BLOCK]
                        dv_acc[prow, :] += dv[:Q_BLOCK]
                        dk_acc[rows, :] += dk[Q_BLOCK:]
                        dv_acc[rows, :] += dv[Q_BLOCK:]
                    else:
                        dk_acc[rows, :] += dk
                        dv_acc[rows, :] += dv
                return carry

            lax.fori_loop(0, BLOCKS_PER_CFG // BWD_BLOCKS_PER_TRIP, block, 0)

        dqkv_ref[0] = dq_acc[...].astype(BF16)
        dqkv_ref[1] = dk_acc[...].astype(BF16)
        dqkv_ref[2] = dv_acc[...].astype(BF16)

        @pl.when(step == n_steps - 1)
        def _():
            exchange.finish()

    def slab(first):
        return pl.BlockSpec((None, SEQ, PAIR), lambda p, b: (first + p, b, 0))

    nat = pl.BlockSpec((SEQ, PAIR), lambda p, b: (b, p))
    tbl = pl.BlockSpec((N_CFG, 2, 2 * Q_BLOCK, Q_BLOCK), lambda p, b: (0, p, 0, 0))
    acc = pltpu.VMEM((SEQ, PAIR), F32)
    res = pl.pallas_call(
        body, grid=(N_PAIR, batch),
        in_specs=[slab(0), slab(N_PAIR), slab(2 * N_PAIR),
                  pl.BlockSpec((SEQ, PAIR), lambda p, b: (b, A_WIDTH // PAIR + p)), nat, nat, tbl] + [ANY] * nt,
        out_specs=[pl.BlockSpec((3, SEQ, PAIR), lambda p, b: (0, b, p)), tbl] + [ANY] * nt,
        out_shape=[jax.ShapeDtypeStruct((3, m, B_WIDTH), BF16),
                   jax.ShapeDtypeStruct((N_CFG, B_HEADS, 2 * Q_BLOCK, Q_BLOCK), F32)]
        + [jax.ShapeDtypeStruct((3,) + p.shape[1:], p.dtype) for p in parts],
        scratch_shapes=[acc, acc, acc, acc] + _sem_pair(3 * nt),
        compiler_params=_params(("arbitrary", "arbitrary")), name="attn_bwd",
    )(qkv, qkv, qkv, dmix, o, lse, bias_t, *parts)
    return res[0], res[1], list(res[2:])


def _rel_bias_grad(ds, buckets_np):
    present = _present_buckets(buckets_np)

    def body(bk_ref, ds_ref, o_ref, acc_ref):
        acc_ref[...] = jnp.zeros_like(acc_ref)
        for c in range(N_CFG):
            bk = bk_ref[c]
            for h in range(B_HEADS):
                dsv = ds_ref[c, h]
                for b in present[c]:
                    part = jnp.sum(jnp.where(bk == b, dsv, 0.0), axis=0, keepdims=True)
                    acc_ref[pl.ds(h * NUM_BUCKETS + b, 1), :] += part
        o_ref[...] = jnp.sum(acc_ref[...], axis=1, keepdims=True)

    vm = pl.BlockSpec(memory_space=pltpu.VMEM)
    return pl.pallas_call(
        body, in_specs=[vm, vm], out_specs=vm,
        out_shape=jax.ShapeDtypeStruct((B_HEADS * NUM_BUCKETS, 1), F32),
        scratch_shapes=[pltpu.VMEM((B_HEADS * NUM_BUCKETS, buckets_np.shape[-1]), F32)],
        compiler_params=_params(), name="rel_bias_grad",
    )(jnp.asarray(buckets_np), ds)


def _assemble_dproj(duv, dqkv):
    m = duv.shape[0]
    rows = 1024

    def body(duv_ref, dqkv_ref, o_ref):
        o_ref[:, :2 * A_WIDTH] = duv_ref[...].astype(BF16)
        for k in range(3):
            o_ref[:, 2 * A_WIDTH + k * B_WIDTH:2 * A_WIDTH + (k + 1) * B_WIDTH] = dqkv_ref[k]

    return pl.pallas_call(
        body, grid=(m // rows,),
        in_specs=[pl.BlockSpec((rows, 2 * A_WIDTH), lambda i: (i, 0)),
                  pl.BlockSpec((3, rows, B_WIDTH), lambda i: (0, i, 0))],
        out_specs=pl.BlockSpec((rows, IN_COLS), lambda i: (i, 0)),
        out_shape=jax.ShapeDtypeStruct((m, IN_COLS), BF16),
        compiler_params=_params(("parallel",)), name="assemble_dproj",
    )(duv, dqkv)


def _shift_down(x, k):
    row = lax.broadcasted_iota(jnp.int32, x.shape, 0)
    return jnp.where(row >= k, pltpu.roll(x, k, 0), 0.0)


def _shift_up(x, k):
    n = x.shape[0]
    row = lax.broadcasted_iota(jnp.int32, x.shape, 0)
    return jnp.where(row < n - k, pltpu.roll(x, n - k, 0), 0.0)


def _convgate_fwd(gate, up, conv_w, conv_b, batch):
    m = gate.shape[0]

    def body(g_ref, u_ref, w_ref, b_ref, a_ref):
        g = g_ref[...].astype(F32)
        w = w_ref[...]
        c = b_ref[...] + w[0:1] * _shift_down(g, 2) + w[1:2] * _shift_down(g, 1) + w[2:3] * g
        a_ref[...] = (_gelu(c) * u_ref[...].astype(F32)).astype(BF16)

    blk = pl.BlockSpec((SEQ, LANE_BLOCK), lambda b, j: (b, j))
    return pl.pallas_call(
        body, grid=(batch, D_FF // LANE_BLOCK),
        in_specs=[blk, blk, pl.BlockSpec((3, LANE_BLOCK), lambda b, j: (0, j)),
                  pl.BlockSpec((1, LANE_BLOCK), lambda b, j: (0, j))],
        out_specs=blk,
        out_shape=jax.ShapeDtypeStruct((m, D_FF), BF16),
        compiler_params=_params(("parallel", "parallel")), name="convgate_fwd",
    )(gate, up, conv_w, conv_b)


def _convgate_bwd(gate, up, dact, conv_w, conv_b, batch):
    m = gate.shape[0]

    def body(g_ref, u_ref, da_ref, w_ref, b_ref, dg_ref, du_ref, dw_ref, db_ref):
        @pl.when(pl.program_id(1) == 0)
        def _():
            dw_ref[...] = jnp.zeros_like(dw_ref)
            db_ref[...] = jnp.zeros_like(db_ref)

        g = g_ref[...].astype(F32)
        w = w_ref[...]
        g1 = _shift_down(g, 1)
        g2 = _shift_down(g, 2)
        c = b_ref[...] + w[0:1] * g2 + w[1:2] * g1 + w[2:3] * g
        gg, dgg = _gelu_and_grad(c)
        da = da_ref[...].astype(F32)
        du_ref[...] = (da * gg).astype(BF16)
        dc = da * u_ref[...].astype(F32) * dgg
        db_ref[...] += jnp.sum(dc, axis=0, keepdims=True)
        dw_ref[0:1, :] += jnp.sum(dc * g2, axis=0, keepdims=True)
        dw_ref[1:2, :] += jnp.sum(dc * g1, axis=0, keepdims=True)
        dw_ref[2:3, :] += jnp.sum(dc * g, axis=0, keepdims=True)
        dg_ref[...] = (w[2:3] * dc + w[1:2] * _shift_up(dc, 1) + w[0:1] * _shift_up(dc, 2)).astype(BF16)

    blk = pl.BlockSpec((SEQ, LANE_BLOCK), lambda j, b: (b, j))
    wspec = pl.BlockSpec((3, LANE_BLOCK), lambda j, b: (0, j))
    bspec = pl.BlockSpec((1, LANE_BLOCK), lambda j, b: (0, j))
    return pl.pallas_call(
        body, grid=(D_FF // LANE_BLOCK, batch),
        in_specs=[blk, blk, blk, wspec, bspec],
        out_specs=[blk, blk, wspec, bspec],
        out_shape=[jax.ShapeDtypeStruct((m, D_FF), BF16), jax.ShapeDtypeStruct((m, D_FF), BF16),
                   jax.ShapeDtypeStruct((3, D_FF), F32), jax.ShapeDtypeStruct((1, D_FF), F32)],
        compiler_params=_params(("parallel", "arbitrary")), name="convgate_bwd",
    )(gate, up, dact, conv_w, conv_b)


def _gather_weights(shards, conv_w_shard):
    nt = len(shards)
    shapes = [sh.shape for sh in shards]
    ts = list(range(nt))

    def body(*refs):
        shard_refs = refs[:nt]
        cw_ref = refs[nt]
        out_refs = refs[nt + 1:2 * nt + 1]
        cw_out = refs[2 * nt + 1]
        send_sems, recv_sems, cw_send, cw_recv = refs[2 * nt + 2:]
        plan = _GatherPlan(shapes, shard_refs, out_refs, send_sems, recv_sems)
        x, y, c, chips = _mesh_pos()

        def cw_copy(j, src, dst, chip):
            return pltpu.make_async_remote_copy(src_ref=src, dst_ref=dst, send_sem=cw_send.at[j],
                                                recv_sem=cw_recv.at[j], device_id=(*chip, c), device_id_type=MESH)

        plan.start(ts)
        cw_sends = [cw_copy(j, cw_ref, cw_out.at[2 * x + y], chip) for j, chip in enumerate(chips)]
        for cp in cw_sends:
            cp.start()
        plan.forward(ts)
        for j, chip in enumerate(chips):
            dst = cw_out.at[2 * chip[0] + chip[1]]
            cw_copy(j, dst, dst, chip).wait_recv()
        plan.finish(ts)
        for cp in cw_sends:
            cp.wait_send()

    out_shape = [jax.ShapeDtypeStruct((N_SHARD,) + sh.shape, sh.dtype) for sh in shards]
    out_shape.append(jax.ShapeDtypeStruct((N_SHARD,) + conv_w_shard.shape, conv_w_shard.dtype))
    return pl.pallas_call(
        body, in_specs=[ANY] * (nt + 1), out_specs=[ANY] * (nt + 1), out_shape=out_shape,
        scratch_shapes=_sem_pair(6 * nt) + _sem_pair(3),
        compiler_params=pltpu.CompilerParams(has_side_effects=True), name="gather_weights",
    )(*shards, conv_w_shard)


def _exchange_halves(grads, name):
    nt = len(grads)
    shapes = [g.shape for g in grads]

    def body(*refs):
        plan = _SiblingExchangePlan(shapes, refs[:nt], refs[nt:2 * nt], *refs[2 * nt:])
        plan.start()
        plan.finish()

    out_shape = [jax.ShapeDtypeStruct((N_SHARD, g.shape[1] // 2, g.shape[2]), g.dtype) for g in grads]
    return pl.pallas_call(
        body, in_specs=[ANY] * nt, out_specs=[ANY] * nt, out_shape=out_shape,
        scratch_shapes=_sem_pair(nt),
        compiler_params=pltpu.CompilerParams(has_side_effects=True), name=name,
    )(*grads)


def _add_halves(g, recv, c_idx):
    _, rows2, cols = g.shape
    rows = rows2 // 2
    tr = rows // 2 if rows % 16 == 0 and rows >= 256 else rows
    nblk = rows // tr

    def body(c_ref, g_ref, r_ref, o_ref):
        o_ref[...] = (g_ref[...] + r_ref[...]).astype(BF16)

    return pl.pallas_call(
        body,
        grid_spec=pltpu.PrefetchScalarGridSpec(
            num_scalar_prefetch=1, grid=(N_SHARD, nblk),
            in_specs=[pl.BlockSpec((None, tr, cols), lambda s, i, c: (s, c[0] * nblk + i, 0)),
                      pl.BlockSpec((None, tr, cols), lambda s, i, c: (s, i, 0))],
            out_specs=pl.BlockSpec((None, tr, cols), lambda s, i, c: (s, i, 0))),
        out_shape=jax.ShapeDtypeStruct((N_SHARD, rows, cols), BF16),
        compiler_params=_params(("parallel", "parallel")), name="rs_add_halves",
    )(c_idx, g, recv)


def _add_chips(part, recv, s_idx, c_idx):
    _, rows, cols = part.shape
    tr = rows // 2 if rows % 32 == 0 and rows >= 256 else rows
    nblk = rows // tr

    def body(idx_ref, p_ref, r_ref, o_ref):
        acc = p_ref[...].astype(F32)
        for j in range(3):
            acc = acc + r_ref[j].astype(F32)
        o_ref[...] = acc

    return pl.pallas_call(
        body,
        grid_spec=pltpu.PrefetchScalarGridSpec(
            num_scalar_prefetch=1, grid=(nblk,),
            in_specs=[pl.BlockSpec((None, tr, cols), lambda i, idx: (idx[0], i, 0)),
                      pl.BlockSpec((3, tr, cols), lambda i, idx: (0, i, 0))],
            out_specs=pl.BlockSpec((tr, cols), lambda i, idx: (idx[1] * nblk + i, 0))),
        out_shape=jax.ShapeDtypeStruct((2 * rows, cols), F32),
        compiler_params=_params(("parallel",)), name="rs_add_chips",
    )(jnp.concatenate([s_idx, c_idx]), part, recv)


def _share_halves(fulls):
    nt = len(fulls)

    def body(*refs):
        out_refs = refs[nt:2 * nt]
        send_sems, recv_sems = refs[2 * nt:]
        x, y, c, _ = _mesh_pos()
        copies = []
        for t in range(nt):
            rows = fulls[t].shape[0] // 2
            mine = out_refs[t].at[pl.ds(c * rows, rows), :]
            copies.append(pltpu.make_async_remote_copy(
                src_ref=mine, dst_ref=mine, send_sem=send_sems.at[t], recv_sem=recv_sems.at[t],
                device_id=(x, y, 1 - c), device_id_type=MESH))
        for cp in copies:
            cp.start()
        for t in range(nt):
            rows = fulls[t].shape[0] // 2
            theirs = out_refs[t].at[pl.ds((1 - c) * rows, rows), :]
            pltpu.make_async_remote_copy(
                src_ref=theirs, dst_ref=theirs, send_sem=send_sems.at[t], recv_sem=recv_sems.at[t],
                device_id=(x, y, 1 - c), device_id_type=MESH).wait_recv()
        for cp in copies:
            cp.wait_send()

    out_shape = [jax.ShapeDtypeStruct(f.shape, f.dtype) for f in fulls]
    return pl.pallas_call(
        body, in_specs=[ANY] * nt, out_specs=[ANY] * nt, out_shape=out_shape,
        input_output_aliases={t: t for t in range(nt)},
        scratch_shapes=_sem_pair(nt),
        compiler_params=pltpu.CompilerParams(has_side_effects=True), name="rs_share_halves",
    )(*fulls)


def _allreduce_small(packed):
    rows = packed.shape[0]

    def body(p_ref, o_ref, sib_ref, chip_ref, send_sems, recv_sems):
        x, y, c, chips = _mesh_pos()
        first = pltpu.make_async_remote_copy(src_ref=p_ref, dst_ref=sib_ref, send_sem=send_sems.at[0],
                                             recv_sem=recv_sems.at[0], device_id=(x, y, 1 - c), device_id_type=MESH)
        first.start()
        first.wait()
        o_ref[...] = p_ref[...] + sib_ref[...]
        copies = [pltpu.make_async_remote_copy(src_ref=o_ref, dst_ref=chip_ref.at[j], send_sem=send_sems.at[1 + j],
                                               recv_sem=recv_sems.at[1 + j], device_id=(*chip, c), device_id_type=MESH)
                  for j, chip in enumerate(chips)]
        for cp in copies:
            cp.start()
        for cp in copies:
            cp.wait()
        o_ref[...] = (o_ref[...] + chip_ref[0]) + (chip_ref[1] + chip_ref[2])

    vm = pl.BlockSpec(memory_space=pltpu.VMEM)
    return pl.pallas_call(
        body, in_specs=[vm], out_specs=vm, out_shape=jax.ShapeDtypeStruct(packed.shape, F32),
        scratch_shapes=[pltpu.VMEM((rows, 128), F32), pltpu.VMEM((3, rows, 128), F32)] + _sem_pair(4),
        compiler_params=pltpu.CompilerParams(has_side_effects=True, vmem_limit_bytes=VMEM_LIMIT),
        name="allreduce_small",
    )(packed)


def _from_col_shards(g):
    n, rows, cols = g.shape
    return g.transpose(1, 0, 2).reshape(rows, n * cols)


def _train_step(x, tgt, g1, g2, g3, g4, shards, ln_g, ln_b, w_s, b_s, rel_bias, conv_w_shard, conv_b, batch,
                s_idx, c_idx):
    big = dict(tm=1024, out_dtype=F32)
    buckets = _bucket_tables()
    bias = _bias_tables(rel_bias, buckets)
    bz = jnp.repeat(b_s.T, HEAD_DIM, axis=1)
    w_st = jnp.swapaxes(w_s, 1, 2)

    def with_own(gathered, own):
        return lax.dynamic_update_index_in_dim(gathered, own, s_idx[0], 0)

    def shard_major(g):
        return g.reshape(N_SHARD, g.shape[0] // N_SHARD, D_MODEL)

    g_in, g_convw = _gather_weights([shards["w_in"]], conv_w_shard)
    w_in_t = with_own(g_in, shards["w_in"]).reshape(IN_COLS, D_MODEL)
    conv_w = _from_col_shards(with_own(g_convw, conv_w_shard))

    h1, uv, qkv = _proj_fwd(x, g1, w_in_t)
    a = _gate_fwd(uv, ln_g, ln_b, w_s, bz)
    later = ["w_out", "w_gate", "w_up", "w_down"]
    o_bf, lse, gathered = _attn_fwd(qkv, bias, batch, [shards[n] for n in later])
    g_out, g_gate, g_up, g_down = [with_own(g, shards[n]) for g, n in zip(gathered, later)]
    w_out = g_out.reshape(D_MODEL, D_MODEL)
    w_gate_t = g_gate.reshape(D_FF, D_MODEL)
    w_up_t = g_up.reshape(D_FF, D_MODEL)
    w_down = g_down.reshape(D_FF, D_MODEL)
    (y1, x1, h2), _ = _fused_rows(
        "out_proj_mid_fwd", 512,
        [(a, w_out, "nn", slice(0, A_WIDTH)), (o_bf, w_out, "nn", slice(A_WIDTH, D_MODEL))],
        [x], [g2, g3], _mid_fwd_rows, [F32, F32, BF16], [])
    gate = _mm(h2, w_gate_t, dims="nt", tm=1024, tn=1408, tk=1024, out_dtype=BF16, name="mm_gate")
    up = _mm(h2, w_up_t, dims="nt", tm=1024, tn=1408, tk=1024, out_dtype=BF16, name="mm_up")
    act = _convgate_fwd(gate, up, conv_w, conv_b, batch)
    (dx2, dy2, dg4, loss), _ = _fused_rows(
        "down_proj_loss_head", 512, [(act, w_down, "nn", None)], [x1, tgt], [g4], _loss_head_rows,
        [F32, BF16], [(1, D_MODEL), (1, 1)])

    dact = _mm(dy2, w_down, dims="nt", tm=1024, tn=1408, tk=1024, out_dtype=BF16, name="mm_dact")
    dw_down = _mm(act, dy2, dims="tn", tm=1408, tn=1024, tk=1024, out_dtype=F32, name="mm_dw_down")
    dgate, dup, dconv_w, dconv_b = _convgate_bwd(gate, up, dact, conv_w, conv_b, batch)
    (dx1, dy1, dg2, dg3), _ = _fused_rows(
        "dh2_mid_bwd", 256, [(dgate, w_gate_t, "nn", None), (dup, w_up_t, "nn", None)],
        [x1, y1, dx2], [g2, g3], _mid_bwd_rows, [F32, BF16], [(1, D_MODEL), (1, D_MODEL)])
    dw_gate_t = _mm(dgate, h2, dims="tn", tm=1408, tn=1024, tk=1024, out_dtype=F32, name="mm_dw_gate")
    dw_up_t = _mm(dup, h2, dims="tn", tm=1408, tn=1024, tk=1024, out_dtype=F32, name="mm_dw_up")
    dmix = _mm(dy1, w_out, dims="nt", tn=1024, tk=1024, name="mm_dmix", **big)
    dw_out_a = _mm(a, dy1, dims="tn", tm=A_WIDTH, tn=1024, tk=1024, out_dtype=F32, name="mm_dw_out_a")
    dw_out_b = _mm(o_bf, dy1, dims="tn", tm=B_WIDTH, tn=1024, tk=1024, out_dtype=F32, name="mm_dw_out_b")

    dw_out = jnp.concatenate([dw_out_a, dw_out_b], axis=0)
    done = [shard_major(g) for g in (dw_down, dw_gate_t, dw_up_t, dw_out)]
    (duv, dln_g, dln_b, dw_s, dbz), recv_a = _gate_bwd(uv, dmix, ln_g, ln_b, w_s, w_st, bz, done)
    parts = [_add_halves(g, r, c_idx) for g, r in zip(done, recv_a)]
    dqkv, ds, recv = _attn_bwd(qkv, dmix, o_bf, lse, jnp.swapaxes(bias, 2, 3), batch, parts)
    fulls = [_add_chips(p, r, s_idx, c_idx) for p, r in zip(parts, recv)]
    drel = _rel_bias_grad(ds, np.ascontiguousarray(np.swapaxes(buckets, 1, 2)))
    dproj = _assemble_dproj(duv, dqkv)
    dw_in_t = _mm(dproj, h1, dims="tn", tm=1408, tn=1024, tk=1024, out_dtype=F32, name="mm_dw_in")
    last = [shard_major(dw_in_t)]
    part_in = [_add_halves(g, r, c_idx) for g, r in zip(last, _exchange_halves(last, "rs_sibling_exchange_in"))]
    (dx0, dg1), recv_in = _fused_rows(
        "dh1_in_bwd", 512, [(dproj, w_in_t, "nn", None)], [x, dx1], [g1], _in_bwd_rows,
        [F32], [(1, D_MODEL)], exchange=part_in)
    fulls += [_add_chips(p, r, s_idx, c_idx) for p, r in zip(part_in, recv_in)]
    reduced = dict(zip(["w_down", "w_gate", "w_up", "w_out", "w_in"], _share_halves(fulls)))

    small = dict(
        norm_mix_pre=dg1, norm_mix_post=dg2, norm_ffn_pre=dg3, norm_ffn_post=dg4,
        ln_v_gain=dln_g, ln_v_bias=dln_b, spatial_w=dw_s,
        spatial_b=dbz[:, ::HEAD_DIM].T,
        rel_bias=drel.reshape(B_HEADS, NUM_BUCKETS).T,
        conv_w=dconv_w, conv_b=dconv_b,
    )
    return loss, dx0, small, reduced


def _adamw(w, g, m, v, name):
    rows, cols = w.shape
    tr = rows
    if rows * cols > 256 * 1024:
        tr = next(cand for cand in (256, 176, 128) if rows % cand == 0)

    def body(w_ref, g_ref, m_ref, v_ref, go_ref, d_ref, nm_ref, nv_ref):
        gv = g_ref[...]
        go_ref[...] = gv
        nm = ADAM_B1 * m_ref[...] + (1.0 - ADAM_B1) * gv
        nv = ADAM_B2 * v_ref[...] + (1.0 - ADAM_B2) * (gv * gv)
        m_hat = nm / (1.0 - ADAM_B1 ** ADAM_STEP)
        v_hat = nv / (1.0 - ADAM_B2 ** ADAM_STEP)
        d_ref[...] = -ADAM_LR * (m_hat / (jnp.sqrt(v_hat) + ADAM_EPS) + ADAM_WD * w_ref[...])
        nm_ref[...] = nm
        nv_ref[...] = nv

    spec = pl.BlockSpec((tr, cols), lambda i: (i, 0))
    sds = jax.ShapeDtypeStruct((rows, cols), F32)
    return pl.pallas_call(
        body, grid=(rows // tr,), in_specs=[spec] * 4, out_specs=[spec] * 4, out_shape=[sds] * 4,
        compiler_params=_params(("parallel",)), name=name,
    )(w, g, m, v)


def _pack(arrays, rows):
    flat = jnp.concatenate([a.reshape(-1) for a in arrays])
    flat = jnp.pad(flat, (0, rows * 128 - flat.shape[0]))
    return flat.reshape(rows, 128)


def _unpack(packed, shapes):
    flat = packed.reshape(-1)
    out, off = [], 0
    for sh in shapes:
        n = int(np.prod(sh))
        out.append(flat[off:off + n].reshape(sh))
        off += n
    return out


SMALL = ["norm_mix_pre", "norm_mix_post", "norm_ffn_pre", "norm_ffn_post", "ln_v_gain", "ln_v_bias",
         "spatial_w", "spatial_b", "rel_bias", "conv_b"]
LARGE = ["w_in", "w_gate", "w_up", "w_down", "w_out"]
TRANSPOSED = ("w_in", "w_gate", "w_up")
ORDER = ["norm_mix_pre", "norm_mix_post", "norm_ffn_pre", "norm_ffn_post", "w_in", "ln_v_gain", "ln_v_bias",
         "spatial_w", "spatial_b", "rel_bias", "w_out", "w_gate", "w_up", "conv_w", "conv_b", "w_down"]


def kernel(x, norm_mix_pre, norm_mix_post, norm_ffn_pre, norm_ffn_post, w_in, ln_v_gain, ln_v_bias, spatial_w, spatial_b, rel_bias, w_out, w_gate, w_up, conv_w, conv_b, w_down, loss_target, m_norm_mix_pre, m_norm_mix_post, m_norm_ffn_pre, m_norm_ffn_post, m_w_in, m_ln_v_gain, m_ln_v_bias, m_spatial_w, m_spatial_b, m_rel_bias, m_w_out, m_w_gate, m_w_up, m_conv_w, m_conv_b, m_w_down, v_norm_mix_pre, v_norm_mix_post, v_norm_ffn_pre, v_norm_ffn_post, v_w_in, v_ln_v_gain, v_ln_v_bias, v_spatial_w, v_spatial_b, v_rel_bias, v_w_out, v_w_gate, v_w_up, v_conv_w, v_conv_b, v_w_down):
    params = dict(norm_mix_pre=norm_mix_pre, norm_mix_post=norm_mix_post, norm_ffn_pre=norm_ffn_pre,
                  norm_ffn_post=norm_ffn_post, w_in=w_in, ln_v_gain=ln_v_gain, ln_v_bias=ln_v_bias,
                  spatial_w=spatial_w, spatial_b=spatial_b, rel_bias=rel_bias, w_out=w_out, w_gate=w_gate,
                  w_up=w_up, conv_w=conv_w, conv_b=conv_b, w_down=w_down)
    mom = dict(norm_mix_pre=m_norm_mix_pre, norm_mix_post=m_norm_mix_post, norm_ffn_pre=m_norm_ffn_pre,
               norm_ffn_post=m_norm_ffn_post, w_in=m_w_in, ln_v_gain=m_ln_v_gain, ln_v_bias=m_ln_v_bias,
               spatial_w=m_spatial_w, spatial_b=m_spatial_b, rel_bias=m_rel_bias, w_out=m_w_out, w_gate=m_w_gate,
               w_up=m_w_up, conv_w=m_conv_w, conv_b=m_conv_b, w_down=m_w_down)
    var = dict(norm_mix_pre=v_norm_mix_pre, norm_mix_post=v_norm_mix_post, norm_ffn_pre=v_norm_ffn_pre,
               norm_ffn_post=v_norm_ffn_post, w_in=v_w_in, ln_v_gain=v_ln_v_gain, ln_v_bias=v_ln_v_bias,
               spatial_w=v_spatial_w, spatial_b=v_spatial_b, rel_bias=v_rel_bias, w_out=v_w_out, w_gate=v_w_gate,
               w_up=v_w_up, conv_w=v_conv_w, conv_b=v_conv_b, w_down=v_w_down)

    batch = x.shape[0]
    xi, yi, ci = lax.axis_index("x"), lax.axis_index("y"), lax.axis_index("c")
    s_idx = (2 * xi + yi).astype(jnp.int32).reshape(1)
    c_idx = ci.astype(jnp.int32).reshape(1)

    def local(a, n):
        return jnp.swapaxes(a[0], 0, 1) if n in TRANSPOSED else a[0]

    shards = {n: local(params[n], n).astype(BF16) for n in LARGE}
    loss_part, dx0, grads, reduced = _train_step(
        x.reshape(batch * SEQ, D_MODEL), loss_target.reshape(batch * SEQ, D_MODEL),
        norm_mix_pre, norm_mix_post, norm_ffn_pre, norm_ffn_post, shards,
        ln_v_gain.reshape(1, A_WIDTH), ln_v_bias.reshape(1, A_WIDTH), spatial_w[0], spatial_b[0], rel_bias,
        conv_w[0], conv_b, batch, s_idx, c_idx)
    loss = lax.psum(loss_part[0, 0], ("x", "y", "c"))
    grad_x = dx0.reshape(batch, SEQ, D_MODEL)

    small_g = [grads[n].reshape(params[n].shape) for n in SMALL] + [grads["conv_w"]]
    n_small = sum(int(np.prod(g.shape)) for g in small_g)
    small_rows = -(-n_small // (8 * 128)) * 8
    summed = _unpack(_allreduce_small(_pack(small_g, small_rows)),
                     [params[n].shape for n in SMALL] + [(3, D_FF)])
    for n, g in zip(SMALL, summed[:-1]):
        reduced[n] = g
    reduced["conv_w"] = lax.dynamic_slice_in_dim(summed[-1], s_idx[0] * SHARD_FF, SHARD_FF, axis=1)[None]

    out_g, out_d, out_m, out_v = {}, {}, {}, {}
    for n in LARGE:
        res = _adamw(local(params[n], n), reduced[n], local(mom[n], n), local(var[n], n), name=f"adamw_{n}")
        if n in TRANSPOSED:
            res = [jnp.swapaxes(r, 0, 1) for r in res]
        out_g[n], out_d[n], out_m[n], out_v[n] = [r[None] for r in res]
    small_names = SMALL + ["conv_w"]
    rows_s = -(-sum(int(np.prod(params[n].shape)) for n in small_names) // (8 * 128)) * 8
    _, d, nm, nv = _adamw(_pack([params[n] for n in small_names], rows_s),
                          _pack([reduced[n] for n in small_names], rows_s),
                          _pack([mom[n] for n in small_names], rows_s), _pack([var[n] for n in small_names], rows_s),
                          name="adamw_small")
    shapes = [params[n].shape for n in small_names]
    for n, dd, mm, vv in zip(small_names, _unpack(d, shapes), _unpack(nm, shapes), _unpack(nv, shapes)):
        out_g[n], out_d[n], out_m[n], out_v[n] = reduced[n], dd, mm, vv

    return (loss, grad_x, *[out_g[n] for n in ORDER], *[out_d[n] for n in ORDER],
            *[out_m[n] for n in ORDER], *[out_v[n] for n in ORDER])
```

```python
import functools
import math

import numpy as np
import jax
import jax.numpy as jnp
from jax import lax
from jax.experimental import pallas as pl
from jax.experimental.pallas import tpu as pltpu

F32 = jnp.float32
BF16 = jnp.bfloat16
MESH = pl.DeviceIdType.MESH

D_MODEL = 1024
SEQ = 2048
HEAD_DIM = 64
A_GROUPS = 4
A_WIDTH = 256
B_HEADS = 12
B_WIDTH = 768
CHUNK = 128
DILATED = ((128, 1), (512, 4), (2048, 16))
NUM_BUCKETS = 32
MAX_DISTANCE = 2048
D_FF = 2816
IN_COLS = 2816
NORM_EPS = 1e-6
NEG_INF = -1e30
N_SHARD = 4
SHARD_FF = D_FF // N_SHARD
LANE_BLOCK = 256
VMEM_LIMIT = 56 * 1024 * 1024

ADAM_LR = 0.001
ADAM_B1 = 0.9
ADAM_B2 = 0.999
ADAM_EPS = 1e-08
ADAM_WD = 0.01
ADAM_STEP = 10

GELU_C = math.sqrt(2.0 / math.pi)
GELU_A = 0.044715

ANY = pl.BlockSpec(memory_space=pl.ANY)


def _params(sem=None):
    return pltpu.CompilerParams(dimension_semantics=sem, vmem_limit_bytes=VMEM_LIMIT)


def _dot(a, b, precision=None):
    return jnp.dot(a, b, preferred_element_type=F32, precision=precision)


def _dot_nt(a, b, precision=None):
    return lax.dot_general(a, b, (((1,), (1,)), ((), ())), preferred_element_type=F32, precision=precision)


def _dot_tn(a, b):
    return lax.dot_general(a, b, (((0,), (0,)), ((), ())), preferred_element_type=F32)


def _gelu(x):
    t = jnp.tanh(GELU_C * (x + GELU_A * (x * x * x)))
    return 0.5 * x * (1.0 + t)


def _gelu_and_grad(x):
    x2 = x * x
    t = jnp.tanh(GELU_C * (x + GELU_A * (x2 * x)))
    g = 0.5 * x * (1.0 + t)
    dg = 0.5 * (1.0 + t) + 0.5 * x * (1.0 - t * t) * (GELU_C * (1.0 + 3.0 * GELU_A * x2))
    return g, dg


def _mesh_pos():
    x, y, c = lax.axis_index("x"), lax.axis_index("y"), lax.axis_index("c")
    chips = [(1 - x, y), (x, 1 - y), (1 - x, 1 - y)]
    return x, y, c, chips


class _GatherPlan:
    def __init__(self, shapes, shard_refs, out_refs, send_sems, recv_sems):
        self.shapes, self.shard_refs, self.out_refs = shapes, shard_refs, out_refs
        self.send_sems, self.recv_sems = send_sems, recv_sems
        self.x, self.y, self.c, self.chips = _mesh_pos()
        self.sib = (self.x, self.y, 1 - self.c)

    def _half(self, t, chip, which):
        rows = self.shapes[t][0] // 2
        return self.out_refs[t].at[2 * chip[0] + chip[1], pl.ds(which * rows, rows), :]

    def _copy(self, k, src, dst, to):
        return pltpu.make_async_remote_copy(src_ref=src, dst_ref=dst, send_sem=self.send_sems.at[k],
                                            recv_sem=self.recv_sems.at[k], device_id=to, device_id_type=MESH)

    def _sends(self, t):
        rows = self.shapes[t][0] // 2
        src = self.shard_refs[t].at[pl.ds(self.c * rows, rows), :]
        return [self._copy(6 * t + j, src, self._half(t, (self.x, self.y), self.c), (*chip, self.c))
                for j, chip in enumerate(self.chips)]

    def _forwards(self, t):
        return [self._copy(6 * t + 3 + j, self._half(t, chip, self.c), self._half(t, chip, self.c), self.sib)
                for j, chip in enumerate(self.chips)]

    def start(self, ts):
        for t in ts:
            for cp in self._sends(t):
                cp.start()

    def forward(self, ts):
        for t in ts:
            for j, chip in enumerate(self.chips):
                landed = self._half(t, chip, self.c)
                self._copy(6 * t + j, landed, landed, (*chip, self.c)).wait_recv()
            for cp in self._forwards(t):
                cp.start()

    def finish(self, ts):
        for t in ts:
            for j, chip in enumerate(self.chips):
                other = self._half(t, chip, 1 - self.c)
                self._copy(6 * t + 3 + j, other, other, self.sib).wait_recv()
        for t in ts:
            for cp in self._sends(t) + self._forwards(t):
                cp.wait_send()


class _SiblingExchangePlan:
    def __init__(self, shapes, grad_refs, out_refs, send_sems, recv_sems):
        self.shapes, self.grad_refs, self.out_refs = shapes, grad_refs, out_refs
        self.send_sems, self.recv_sems = send_sems, recv_sems
        self.x, self.y, self.c, _ = _mesh_pos()

    def _copies(self):
        out = []
        for t, (g, o) in enumerate(zip(self.grad_refs, self.out_refs)):
            rows = self.shapes[t][1] // 2
            out.append(pltpu.make_async_remote_copy(
                src_ref=g.at[:, pl.ds((1 - self.c) * rows, rows), :], dst_ref=o, send_sem=self.send_sems.at[t],
                recv_sem=self.recv_sems.at[t], device_id=(self.x, self.y, 1 - self.c), device_id_type=MESH))
        return out

    def start(self):
        for cp in self._copies():
            cp.start()

    def finish(self):
        for cp in self._copies():
            cp.wait()


class _ChipExchangePlan:
    def __init__(self, part_refs, out_refs, send_sems, recv_sems):
        self.part_refs, self.out_refs, self.send_sems, self.recv_sems = part_refs, out_refs, send_sems, recv_sems
        _, _, self.c, self.chips = _mesh_pos()

    def _copies(self):
        return [pltpu.make_async_remote_copy(
            src_ref=p.at[2 * chip[0] + chip[1]], dst_ref=o.at[j], send_sem=self.send_sems.at[3 * t + j],
            recv_sem=self.recv_sems.at[3 * t + j], device_id=(*chip, self.c), device_id_type=MESH)
            for t, (p, o) in enumerate(zip(self.part_refs, self.out_refs)) for j, chip in enumerate(self.chips)]

    def start(self):
        for cp in self._copies():
            cp.start()

    def finish(self):
        for cp in self._copies():
            cp.wait()


def _sem_pair(n):
    return [pltpu.SemaphoreType.DMA((n,)), pltpu.SemaphoreType.DMA((n,))]


def _mm(a, b, *, dims, tm, tn, tk, out_dtype, name):
    if dims == "nn":
        m, k = a.shape
        n = b.shape[1]
        a_spec = pl.BlockSpec((tm, tk), lambda i, j, kk: (i, kk))
        b_spec = pl.BlockSpec((tk, tn), lambda i, j, kk: (kk, j))
        dot = _dot
    elif dims == "nt":
        m, k = a.shape
        n = b.shape[0]
        a_spec = pl.BlockSpec((tm, tk), lambda i, j, kk: (i, kk))
        b_spec = pl.BlockSpec((tn, tk), lambda i, j, kk: (j, kk))
        dot = _dot_nt
    else:
        k, m = a.shape
        n = b.shape[1]
        a_spec = pl.BlockSpec((tk, tm), lambda i, j, kk: (kk, i))
        b_spec = pl.BlockSpec((tk, tn), lambda i, j, kk: (kk, j))
        dot = _dot_tn
    assert m % tm == 0 and n % tn == 0 and k % tk == 0, (name, m, n, k)
    grid = (m // tm, n // tn, k // tk)
    nk = grid[2]
    assert nk == 1 or out_dtype == F32, name

    def body(a_ref, b_ref, o_ref):
        prod = dot(a_ref[...].astype(BF16), b_ref[...].astype(BF16))
        if nk == 1:
            o_ref[...] = prod.astype(out_dtype)
        else:
            kk = pl.program_id(2)

            @pl.when(kk == 0)
            def _():
                o_ref[...] = prod

            @pl.when(kk > 0)
            def _():
                o_ref[...] += prod

    return pl.pallas_call(
        body, grid=grid, in_specs=[a_spec, b_spec],
        out_specs=pl.BlockSpec((tm, tn), lambda i, j, kk: (i, j)),
        out_shape=jax.ShapeDtypeStruct((m, n), out_dtype),
        compiler_params=_params(("parallel", "parallel", "arbitrary")), name=name,
    )(a, b)


def _fused_rows(name, tm, mats, rows, vecs, fn, row_outs, acc_outs, exchange=()):
    m = mats[0][0].shape[0]
    nm, nr, nv, nro, nao, nx = len(mats), len(rows), len(vecs), len(row_outs), len(acc_outs), len(exchange)
    n_steps = m // tm

    def body(*refs):
        a_refs, w_refs = refs[:nm], refs[nm:2 * nm]
        pos = 2 * nm
        row_refs, vec_refs, part_refs = refs[pos:pos + nr], refs[pos + nr:pos + nr + nv], refs[pos + nr + nv:pos + nr + nv + nx]
        pos += nr + nv + nx
        out_refs, acc_refs, recv_refs = refs[pos:pos + nro], refs[pos + nro:pos + nro + nao], refs[pos + nro + nao:pos + nro + nao + nx]
        sems = refs[pos + nro + nao + nx:]
        i = pl.program_id(0)
        if nx:
            plan = _ChipExchangePlan(part_refs, recv_refs, *sems)

            @pl.when(i == 0)
            def _():
                plan.start()

        @pl.when(i == 0)
        def _():
            for r in acc_refs:
                r[...] = jnp.zeros_like(r)

        y = None
        for a_ref, w_ref, (_, _, dims, sl) in zip(a_refs, w_refs, mats):
            w = w_ref[...] if sl is None else w_ref[sl, :]
            part = (_dot if dims == "nn" else _dot_nt)(a_ref[...], w)
            y = part if y is None else y + part
        res = fn(y, *[r[...] for r in row_refs], *[v[...] for v in vec_refs])
        for r, val in zip(out_refs, res[:nro]):
            r[...] = val.astype(r.dtype)
        for r, val in zip(acc_refs, res[nro:]):
            r[...] += val

        if nx:
            @pl.when(i == n_steps - 1)
            def _():
                plan.finish()

    tile = lambda width: pl.BlockSpec((tm, width), lambda i: (i, 0))
    res = pl.pallas_call(
        body, grid=(n_steps,),
        in_specs=[tile(a.shape[1]) for a, _, _, _ in mats] + [_full_spec(w.shape) for _, w, _, _ in mats]
        + [tile(D_MODEL)] * nr + [_full_spec((1, D_MODEL))] * nv + [ANY] * nx,
        out_specs=[tile(D_MODEL)] * nro + [_full_spec(s) for s in acc_outs] + [ANY] * nx,
        out_shape=[jax.ShapeDtypeStruct((m, D_MODEL), dt) for dt in row_outs]
        + [jax.ShapeDtypeStruct(s, F32) for s in acc_outs]
        + [jax.ShapeDtypeStruct((3,) + p.shape[1:], p.dtype) for p in exchange],
        scratch_shapes=_sem_pair(3 * nx) if nx else [],
        compiler_params=_params(("arbitrary",)), name=name,
    )(*[a for a, _, _, _ in mats], *[w for _, w, _, _ in mats], *rows, *vecs, *exchange)
    return list(res[:nro + nao]), list(res[nro + nao:])


ROW_TILE = 512


def _vec_spec(width=D_MODEL):
    return pl.BlockSpec((1, width), lambda i: (0, 0))


def _rstd(v):
    return lax.rsqrt(jnp.mean(v * v, axis=-1, keepdims=True) + NORM_EPS)


def _mid_fwd_rows(y1, x0, g2, g3):
    x1 = x0 + y1 * _rstd(y1) * g2
    return y1, x1, x1 * _rstd(x1) * g3


def _rms_bwd_rows(dout, v, g):
    r = _rstd(v)
    n = v * r
    dn = dout * g
    dv = r * (dn - n * jnp.mean(dn * n, axis=-1, keepdims=True))
    dg = jnp.sum(dout * n, axis=0, keepdims=True)
    return dv, dg


def _loss_head_rows(y2, x1, tgt, g4):
    x2 = x1 + y2 * _rstd(y2) * g4
    err = x2 - tgt
    loss = 0.5 * jnp.sum(jnp.mean(err * err, axis=-1, keepdims=True), axis=0, keepdims=True)
    dx2 = err * (1.0 / D_MODEL)
    dy2, dg4 = _rms_bwd_rows(dx2, y2, g4)
    return dx2, dy2, dg4, loss


def _mid_bwd_rows(dh2, x1, y1, dx2, g2, g3):
    d3, dg3 = _rms_bwd_rows(dh2, x1, g3)
    dx1 = dx2 + d3
    dy1, dg2 = _rms_bwd_rows(dx1, y1, g2)
    return dx1, dy1, dg2, dg3


def _in_bwd_rows(dh1, x0, dx1, g1):
    d1, dg1 = _rms_bwd_rows(dh1, x0, g1)
    return dx1 + d1, dg1


GATE_ROWS = 512


def _group_mean_matrix():
    p = np.zeros((A_WIDTH, A_WIDTH), np.float32)
    for g in range(A_GROUPS):
        p[g * HEAD_DIM:(g + 1) * HEAD_DIM, g * HEAD_DIM:(g + 1) * HEAD_DIM] = 1.0 / HEAD_DIM
    return jnp.asarray(p)


def _group_masks(width=A_WIDTH):
    lane = lax.broadcasted_iota(jnp.int32, (1, width), 1)
    return [(lane >= g * HEAD_DIM) & (lane < (g + 1) * HEAD_DIM) for g in range(width // HEAD_DIM)]


def _layernorm_groups(vg, pavg):
    hi = lax.Precision.HIGHEST
    mu = _dot(vg, pavg, hi)
    xc = vg - mu
    var = _dot(xc * xc, pavg, hi)
    rstd = lax.rsqrt(var + NORM_EPS)
    return xc * rstd, rstd


def _spatial_mix(w_bf, vn_chunk_bf, masks, bz):
    z = bz
    for g in range(A_GROUPS):
        z = z + jnp.where(masks[g], _dot(w_bf[g], vn_chunk_bf), 0.0)
    return z


def _full_spec(shape):
    return pl.BlockSpec(shape, lambda i: tuple(0 for _ in shape))


def _gate_fwd(uv, ln_g, ln_b, w_s, bz):
    m = uv.shape[0]
    pavg = _group_mean_matrix()

    def body(u_ref, v_ref, lg_ref, lb_ref, w_ref, bz_ref, p_ref, a_ref):
        masks = _group_masks()
        row = lax.broadcasted_iota(jnp.int32, (CHUNK, CHUNK), 0)
        col = lax.broadcasted_iota(jnp.int32, (CHUNK, CHUNK), 1)
        w_bf = [jnp.where(row >= col, w_ref[g], 0.0).astype(BF16) for g in range(A_GROUPS)]
        ug = _gelu(u_ref[...])
        vhat, _ = _layernorm_groups(_gelu(v_ref[...]), p_ref[...])
        vn = vhat * lg_ref[...] + lb_ref[...]
        bz = bz_ref[...]
        for c in range(GATE_ROWS // CHUNK):
            sl = slice(c * CHUNK, (c + 1) * CHUNK)
            z = _spatial_mix(w_bf, vn[sl].astype(BF16), masks, bz)
            a_ref[sl, :] = (ug[sl] * z).astype(BF16)

    return pl.pallas_call(
        body, grid=(m // GATE_ROWS,),
        in_specs=[pl.BlockSpec((GATE_ROWS, A_WIDTH), lambda i: (i, 0)),
                  pl.BlockSpec((GATE_ROWS, A_WIDTH), lambda i: (i, 1)),
                  _full_spec((1, A_WIDTH)), _full_spec((1, A_WIDTH)), _full_spec((A_GROUPS, CHUNK, CHUNK)),
                  _full_spec((CHUNK, A_WIDTH)), _full_spec((A_WIDTH, A_WIDTH))],
        out_specs=pl.BlockSpec((GATE_ROWS, A_WIDTH), lambda i: (i, 0)),
        out_shape=jax.ShapeDtypeStruct((m, A_WIDTH), BF16),
        compiler_params=_params(("parallel",)), name="gate_fwd",
    )(uv, uv, ln_g, ln_b, w_s, bz, pavg)


def _gate_bwd(uv, dmix, ln_g, ln_b, w_s, w_st, bz, grads):
    m = uv.shape[0]
    pavg = _group_mean_matrix()
    nsteps = m // GATE_ROWS
    nx = len(grads)
    shapes = [g.shape for g in grads]

    def body(u_ref, v_ref, da_ref, lg_ref, lb_ref, w_ref, wt_ref, bz_ref, p_ref, *rest):
        grad_refs = rest[:nx]
        duv_ref, dlg_ref, dlb_ref, dw_ref, dbz_ref = rest[nx:nx + 5]
        recv_refs = rest[nx + 5:2 * nx + 5]
        exchange = _SiblingExchangePlan(shapes, grad_refs, recv_refs, *rest[2 * nx + 5:])
        i = pl.program_id(0)

        @pl.when(i == 0)
        def _():
            exchange.start()
            dlg_ref[...] = jnp.zeros_like(dlg_ref)
            dlb_ref[...] = jnp.zeros_like(dlb_ref)
            dw_ref[...] = jnp.zeros_like(dw_ref)
            dbz_ref[...] = jnp.zeros_like(dbz_ref)

        hi = lax.Precision.HIGHEST
        masks = _group_masks()
        row = lax.broadcasted_iota(jnp.int32, (CHUNK, CHUNK), 0)
        col = lax.broadcasted_iota(jnp.int32, (CHUNK, CHUNK), 1)
        tril = row >= col
        w_bf = [jnp.where(tril, w_ref[g], 0.0).astype(BF16) for g in range(A_GROUPS)]
        wt_bf = [jnp.where(col >= row, wt_ref[g], 0.0).astype(BF16) for g in range(A_GROUPS)]
        pavg_v = p_ref[...]
        lg = lg_ref[...]
        ug, dug = _gelu_and_grad(u_ref[...])
        vg, dvg_dx = _gelu_and_grad(v_ref[...])
        vhat, rstd = _layernorm_groups(vg, pavg_v)
        vn = vhat * lg + lb_ref[...]
        da = da_ref[...]
        bz = bz_ref[...]
        for c in range(GATE_ROWS // CHUNK):
            sl = slice(c * CHUNK, (c + 1) * CHUNK)
            vn_bf = vn[sl].astype(BF16)
            z = _spatial_mix(w_bf, vn_bf, masks, bz)
            dz = da[sl] * ug[sl]
            duv_ref[sl, 0:A_WIDTH] = da[sl] * z * dug[sl]
            dbz_ref[...] += dz
            dz_bf = dz.astype(BF16)
            dvn = jnp.zeros((CHUNK, A_WIDTH), F32)
            for g in range(A_GROUPS):
                dz_g = jnp.where(masks[g], dz, 0.0).astype(BF16)
                dw_ref[g] += jnp.where(tril, _dot_nt(dz_g, vn_bf), 0.0)
                dvn = dvn + jnp.where(masks[g], _dot(wt_bf[g], dz_bf), 0.0)
            vh = vhat[sl]
            dlb_ref[...] += jnp.sum(dvn, axis=0, keepdims=True)
            dlg_ref[...] += jnp.sum(dvn * vh, axis=0, keepdims=True)
            dvh = dvn * lg
            m1 = _dot(dvh, pavg_v, hi)
            m2 = _dot(dvh * vh, pavg_v, hi)
            duv_ref[sl, A_WIDTH:2 * A_WIDTH] = rstd[sl] * (dvh - m1 - vh * m2) * dvg_dx[sl]

        @pl.when(i == nsteps - 1)
        def _():
            dbz_ref[...] = _dot(dbz_ref[...], pavg_v * float(HEAD_DIM), hi)
            exchange.finish()

    res = pl.pallas_call(
        body, grid=(nsteps,),
        in_specs=[pl.BlockSpec((GATE_ROWS, A_WIDTH), lambda i: (i, 0)),
                  pl.BlockSpec((GATE_ROWS, A_WIDTH), lambda i: (i, 1)),
                  pl.BlockSpec((GATE_ROWS, A_WIDTH), lambda i: (i, 0)),
                  _full_spec((1, A_WIDTH)), _full_spec((1, A_WIDTH)), _full_spec((A_GROUPS, CHUNK, CHUNK)),
                  _full_spec((A_GROUPS, CHUNK, CHUNK)), _full_spec((CHUNK, A_WIDTH)),
                  _full_spec((A_WIDTH, A_WIDTH))] + [ANY] * nx,
        out_specs=[pl.BlockSpec((GATE_ROWS, 2 * A_WIDTH), lambda i: (i, 0)),
                   _full_spec((1, A_WIDTH)), _full_spec((1, A_WIDTH)), _full_spec((A_GROUPS, CHUNK, CHUNK)),
                   _full_spec((CHUNK, A_WIDTH))] + [ANY] * nx,
        out_shape=[jax.ShapeDtypeStruct((m, 2 * A_WIDTH), F32),
                   jax.ShapeDtypeStruct((1, A_WIDTH), F32), jax.ShapeDtypeStruct((1, A_WIDTH), F32),
                   jax.ShapeDtypeStruct((A_GROUPS, CHUNK, CHUNK), F32),
                   jax.ShapeDtypeStruct((CHUNK, A_WIDTH), F32)]
        + [jax.ShapeDtypeStruct((N_SHARD, s[1] // 2, s[2]), F32) for s in shapes],
        scratch_shapes=_sem_pair(nx),
        compiler_params=_params(("arbitrary",)), name="gate_bwd",
    )(uv, uv, dmix, ln_g, ln_b, w_s, w_st, bz, pavg, *grads)
    return res[:5], list(res[5:])


Q_BLOCK = 128
PAIR = 2 * HEAD_DIM
N_PAIR = B_HEADS // 2
N_CFG = len(DILATED)
BLOCKS_PER_CFG = SEQ // Q_BLOCK
QKV_SLABS = 3 * N_PAIR
FWD_BLOCKS_PER_TRIP = 8
BWD_BLOCKS_PER_TRIP = 4


def _t5_bucket_np(dist, dtype):
    max_exact = NUM_BUCKETS // 2
    d = np.maximum(dist, 1).astype(dtype)
    large = max_exact + (np.log(d / dtype(max_exact)) / dtype(math.log(MAX_DISTANCE / max_exact))
                         * dtype(NUM_BUCKETS - max_exact))
    large = np.minimum(large.astype(np.int32), NUM_BUCKETS - 1)
    return np.where(dist < max_exact, dist, large)


def _bucket_tables():
    i = np.arange(Q_BLOCK)[:, None]
    j = np.arange(Q_BLOCK)[None, :]
    tables = []
    for _, dil in DILATED:
        rel_prev = Q_BLOCK + i - j
        rel_cur = i - j
        rel = np.concatenate([rel_prev, rel_cur], axis=1)
        valid = np.concatenate([rel_prev <= Q_BLOCK, rel_cur >= 0], axis=1)
        dist = np.maximum(rel, 0) * dil
        b32 = _t5_bucket_np(dist, np.float32)
        b64 = _t5_bucket_np(dist, np.float64)
        assert np.array_equal(b32, b64)
        tables.append(np.where(valid, b32, -1).astype(np.int32))
    return np.stack(tables)


def _present_buckets(buckets_np):
    return [sorted(set(int(v) for v in np.unique(buckets_np[c]) if v >= 0)) for c in range(N_CFG)]


def _bias_tables(rel_bias, buckets_np):
    present = _present_buckets(buckets_np)

    def body(rb_ref, bk_ref, o_ref, ot_ref):
        for c in range(N_CFG):
            bk = bk_ref[c]
            for h in range(B_HEADS):
                acc = jnp.full((Q_BLOCK, 2 * Q_BLOCK), NEG_INF, F32)
                for b in present[c]:
                    acc = jnp.where(bk == b, rb_ref[b, h], acc)
                o_ref[c, h] = acc
                ot_ref[c, h] = acc.T

    vm = pl.BlockSpec(memory_space=pltpu.VMEM)
    return pl.pallas_call(
        body,
        in_specs=[pl.BlockSpec(memory_space=pltpu.SMEM), vm],
        out_specs=[vm, vm],
        out_shape=[jax.ShapeDtypeStruct((N_CFG, B_HEADS, Q_BLOCK, 2 * Q_BLOCK), F32),
                   jax.ShapeDtypeStruct((N_CFG, B_HEADS, 2 * Q_BLOCK, Q_BLOCK), F32)],
        compiler_params=_params(), name="bias_tables",
    )(rel_bias, jnp.asarray(buckets_np))


def _proj_fwd(x, g1, w_in_t):
    m = x.shape[0]
    tm = ROW_TILE

    def body(x_ref, g_ref, w_ref, h_ref, uv_ref, qkv_ref):
        xv = x_ref[...]
        h = (xv * _rstd(xv) * g_ref[...]).astype(BF16)
        h_ref[...] = h
        acc = _dot_nt(h, w_ref[...])
        uv_ref[...] = acc[:, :2 * A_WIDTH]
        for s in range(QKV_SLABS):
            qkv_ref[s] = acc[:, 2 * A_WIDTH + s * PAIR:2 * A_WIDTH + (s + 1) * PAIR]

    return pl.pallas_call(
        body, grid=(m // tm,),
        in_specs=[pl.BlockSpec((tm, D_MODEL), lambda i: (i, 0)), _vec_spec(),
                  pl.BlockSpec((IN_COLS, D_MODEL), lambda i: (0, 0))],
        out_specs=[pl.BlockSpec((tm, D_MODEL), lambda i: (i, 0)),
                   pl.BlockSpec((tm, 2 * A_WIDTH), lambda i: (i, 0)),
                   pl.BlockSpec((QKV_SLABS, tm, PAIR), lambda i: (0, i, 0))],
        out_shape=[jax.ShapeDtypeStruct((m, D_MODEL), BF16), jax.ShapeDtypeStruct((m, 2 * A_WIDTH), F32),
                   jax.ShapeDtypeStruct((QKV_SLABS, m, PAIR), F32)],
        compiler_params=_params(("parallel",)), name="proj_fwd",
    )(x, g1, w_in_t)


def _pair_masks():
    lane = lax.broadcasted_iota(jnp.int32, (1, PAIR), 1)
    return [lane < HEAD_DIM, lane >= HEAD_DIM]


def _block_rows(idx, dil):
    if dil == 1:
        n = idx
        cur = pl.ds(pl.multiple_of(n * Q_BLOCK, Q_BLOCK), Q_BLOCK)
        prev = pl.ds(pl.multiple_of(jnp.maximum(n - 1, 0) * Q_BLOCK, Q_BLOCK), Q_BLOCK)
        return n, cur, prev
    r = idx % dil
    n = idx // dil
    cur = pl.ds(r + (dil * Q_BLOCK) * n, Q_BLOCK, stride=dil)
    prev = pl.ds(r + (dil * Q_BLOCK) * jnp.maximum(n - 1, 0), Q_BLOCK, stride=dil)
    return n, cur, prev


def _attn_fwd(qkv, bias, batch, shards):
    m = qkv.shape[1]
    comb_rows = 256
    nt = len(shards)
    shapes = [sh.shape for sh in shards]
    n_steps = batch * N_PAIR
    early, late = list(range(nt // 2)), list(range(nt // 2, nt))

    def body(q_ref, k_ref, v_ref, b_ref, *rest):
        shard_refs = rest[:nt]
        o_ref, l_ref = rest[nt:nt + 2]
        gat_refs = rest[nt + 2:2 * nt + 2]
        scratch = rest[2 * nt + 2:]
        oc_refs, lc_refs = scratch[:N_CFG], scratch[N_CFG:2 * N_CFG]
        step = pl.program_id(0) * N_PAIR + pl.program_id(1)
        gather = _GatherPlan(shapes, shard_refs, gat_refs, *scratch[2 * N_CFG:])

        @pl.when(step == 0)
        def _():
            gather.start(early + late)

        @pl.when(step == n_steps // 2)
        def _():
            gather.forward(early)

        @pl.when(step == n_steps - 2)
        def _():
            gather.forward(late)

        masks = _pair_masks()
        for ci, (_, dil) in enumerate(DILATED):
            nb = SEQ // dil // Q_BLOCK

            def block(trip, carry, ci=ci, dil=dil, nb=nb):
                work = []
                for u in range(FWD_BLOCKS_PER_TRIP):
                    n, rows, prow = _block_rows(trip * FWD_BLOCKS_PER_TRIP + u, dil)
                    q = q_ref[rows, :] * 0.125
                    kc = k_ref[rows, :].astype(BF16)
                    vc = v_ref[rows, :]
                    kp = k_ref[prow, :].astype(BF16) if nb > 1 else None
                    vp = v_ref[prow, :] if nb > 1 else None
                    tiles = []
                    for h in range(2):
                        qh = jnp.where(masks[h], q, 0.0).astype(BF16)
                        sc = _dot_nt(qh, kc) + b_ref[ci, h, :, Q_BLOCK:]
                        sp = None
                        if nb > 1:
                            sp = _dot_nt(qh, kp) + jnp.where(n == 0, NEG_INF, b_ref[ci, h, :, :Q_BLOCK])
                        tiles.append((sc, sp))
                    work.append((rows, vc, vp, tiles))
                probs = []
                for _, _, _, tiles in work:
                    ps = []
                    for sc, sp in tiles:
                        mx = jnp.max(sc if sp is None else jnp.maximum(sc, sp), axis=1, keepdims=True)
                        pc = jnp.exp(sc - mx).astype(BF16)
                        pp = None if sp is None else jnp.exp(sp - mx).astype(BF16)
                        ps.append((mx, pc, pp))
                    probs.append(ps)
                for (rows, vc, vp, _), ps in zip(work, probs):
                    res = []
                    for h, (_, pc, pp) in enumerate(ps):
                        r = _dot(pc, jnp.where(masks[h], vc, 1.0).astype(BF16))
                        if pp is not None:
                            r = r + _dot(pp, jnp.where(masks[h], vp, 1.0).astype(BF16))
                        res.append(r)
                    num = jnp.where(masks[0], res[0], res[1])
                    den = pltpu.roll(jnp.where(masks[0], res[1], res[0]), HEAD_DIM, 1)
                    oc_refs[ci][rows, :] = num / den
                    lc_refs[ci][rows, :] = jnp.where(masks[0], ps[0][0], ps[1][0]) + jnp.log(den)
                return carry

            lax.fori_loop(0, BLOCKS_PER_CFG // FWD_BLOCKS_PER_TRIP, block, 0)

        def combine(i, carry):
            rr = pl.ds(pl.multiple_of(i * comb_rows, comb_rows), comb_rows)
            ls = [lc_refs[c][rr, :] for c in range(N_CFG)]
            mx = functools.reduce(jnp.maximum, ls)
            ws = [jnp.exp(l - mx) for l in ls]
            tot = functools.reduce(lambda a, b: a + b, ws)
            o = functools.reduce(lambda a, b: a + b, [ws[c] * oc_refs[c][rr, :] for c in range(N_CFG)]) / tot
            o_ref[rr, :] = o.astype(BF16)
            l_ref[rr, :] = mx + jnp.log(tot)
            return carry

        lax.fori_loop(0, SEQ // comb_rows, combine, 0)

        @pl.when(step == n_steps - 1)
        def _():
            gather.finish(early + late)

    def slab(first):
        return pl.BlockSpec((None, SEQ, PAIR), lambda b, p: (first + p, b, 0))

    nat = pl.BlockSpec((SEQ, PAIR), lambda b, p: (b, p))
    res = pl.pallas_call(
        body, grid=(batch, N_PAIR),
        in_specs=[slab(0), slab(N_PAIR), slab(2 * N_PAIR),
                  pl.BlockSpec((N_CFG, 2, Q_BLOCK, 2 * Q_BLOCK), lambda b, p: (0, p, 0, 0))] + [ANY] * nt,
        out_specs=[nat, nat] + [ANY] * nt,
        out_shape=[jax.ShapeDtypeStruct((m, B_WIDTH), BF16), jax.ShapeDtypeStruct((m, B_WIDTH), F32)]
        + [jax.ShapeDtypeStruct((N_SHARD,) + sh.shape, sh.dtype) for sh in shards],
        scratch_shapes=[pltpu.VMEM((SEQ, PAIR), F32)] * (2 * N_CFG) + _sem_pair(6 * nt),
        compiler_params=_params(("arbitrary", "arbitrary")), name="attn_fwd",
    )(qkv, qkv, qkv, bias, *shards)
    return res[0], res[1], list(res[2:])


def _attn_bwd(qkv, dmix, o, lse, bias_t, batch, parts):
    m = qkv.shape[1]
    nt = len(parts)
    n_steps = N_PAIR * batch

    def body(q_ref, k_ref, v_ref, do_ref, o_ref, l_ref, b_ref, *rest):
        part_refs = rest[:nt]
        dqkv_ref, ds_ref = rest[nt:nt + 2]
        recv_refs = rest[nt + 2:2 * nt + 2]
        dq_acc, dk_acc, dv_acc, d_scr, send_sems, recv_sems = rest[2 * nt + 2:]
        step = pl.program_id(0) * batch + pl.program_id(1)
        exchange = _ChipExchangePlan(part_refs, recv_refs, send_sems, recv_sems)

        @pl.when(step == 0)
        def _():
            exchange.start()

        @pl.when(pl.program_id(1) == 0)
        def _():
            ds_ref[...] = jnp.zeros_like(ds_ref)

        dq_acc[...] = jnp.zeros_like(dq_acc)
        dk_acc[...] = jnp.zeros_like(dk_acc)
        dv_acc[...] = jnp.zeros_like(dv_acc)
        d_scr[...] = do_ref[...] * o_ref[...].astype(F32)
        masks = _pair_masks()
        key_row = lax.broadcasted_iota(jnp.int32, (2 * Q_BLOCK, 2 * Q_BLOCK), 0)

        def stack_heads(t):
            return jnp.concatenate([jnp.where(masks[0], t, 0.0), jnp.where(masks[1], t, 0.0)], axis=0).astype(BF16)

        for ci, (_, dil) in enumerate(DILATED):
            nb = SEQ // dil // Q_BLOCK

            def block(trip, carry, ci=ci, dil=dil, nb=nb):
                first = []
                for u in range(BWD_BLOCKS_PER_TRIP):
                    n, rows, prow = _block_rows(trip * BWD_BLOCKS_PER_TRIP + u, dil)
                    if nb > 1:
                        kcat = jnp.concatenate([k_ref[prow, :], k_ref[rows, :]], axis=0).astype(BF16)
                        vcat = jnp.concatenate([v_ref[prow, :], v_ref[rows, :]], axis=0).astype(BF16)
                    else:
                        kcat = k_ref[rows, :].astype(BF16)
                        vcat = v_ref[rows, :].astype(BF16)
                    qst = stack_heads(q_ref[rows, :] * 0.125)
                    dost = stack_heads(do_ref[rows, :])
                    lt = l_ref[rows, :].T
                    dt = d_scr[rows, :].T
                    lrow = jnp.concatenate([lt[0:1], lt[HEAD_DIM:HEAD_DIM + 1]], axis=1)
                    drow = jnp.concatenate([jnp.sum(dt[:HEAD_DIM], axis=0, keepdims=True),
                                            jnp.sum(dt[HEAD_DIM:], axis=0, keepdims=True)], axis=1)
                    first.append((n, rows, prow, kcat, qst, dost, lrow, drow, _dot_nt(kcat, qst), _dot_nt(vcat, dost)))
                second = []
                for n, rows, prow, kcat, qst, dost, lrow, drow, st, dpt in first:
                    if nb > 1:
                        bt = jnp.concatenate([b_ref[ci, 0], b_ref[ci, 1]], axis=1)
                        bt = jnp.where((n == 0) & (key_row < Q_BLOCK), NEG_INF, bt)
                    else:
                        bt = jnp.concatenate([b_ref[ci, 0, Q_BLOCK:, :], b_ref[ci, 1, Q_BLOCK:, :]], axis=1)
                    pt = jnp.exp(st + bt - lrow)
                    dst = pt * (dpt - drow)
                    keys = slice(0, 2 * Q_BLOCK) if nb > 1 else slice(Q_BLOCK, 2 * Q_BLOCK)
                    ds_ref[ci, 0, keys, :] += dst[:, :Q_BLOCK]
                    ds_ref[ci, 1, keys, :] += dst[:, Q_BLOCK:]
                    second.append((rows, prow, kcat, qst, dost, pt.astype(BF16), dst.astype(BF16)))
                for rows, prow, kcat, qst, dost, pt_bf, dst_bf in second:
                    dk = _dot(dst_bf, qst)
                    dv = _dot(pt_bf, dost)
                    dq2 = _dot_tn(dst_bf, kcat)
                    dq_acc[rows, :] += jnp.where(masks[0], dq2[:Q_BLOCK], dq2[Q_BLOCK:]) * 0.125
                    if nb > 1:
                        dk_acc[prow, :] += dk[:Q_BLOCK]
                        dv_acc[prow, :] += dv[:Q_BLOCK]
                        dk_acc[rows, :] += dk[Q_BLOCK:]
                        dv_acc[rows, :] += dv[Q_BLOCK:]
                    else:
                        dk_acc[rows, :] += dk
                        dv_acc[rows, :] += dv
                return carry

            lax.fori_loop(0, BLOCKS_PER_CFG // BWD_BLOCKS_PER_TRIP, block, 0)

        dqkv_ref[0] = dq_acc[...].astype(BF16)
        dqkv_ref[1] = dk_acc[...].astype(BF16)
        dqkv_ref[2] = dv_acc[...].astype(BF16)

        @pl.when(step == n_steps - 1)
        def _():
            exchange.finish()

    def slab(first):
        return pl.BlockSpec((None, SEQ, PAIR), lambda p, b: (first + p, b, 0))

    nat = pl.BlockSpec((SEQ, PAIR), lambda p, b: (b, p))
    tbl = pl.BlockSpec((N_CFG, 2, 2 * Q_BLOCK, Q_BLOCK), lambda p, b: (0, p, 0, 0))
    acc = pltpu.VMEM((SEQ, PAIR), F32)
    res = pl.pallas_call(
        body, grid=(N_PAIR, batch),
        in_specs=[slab(0), slab(N_PAIR), slab(2 * N_PAIR),
                  pl.BlockSpec((SEQ, PAIR), lambda p, b: (b, A_WIDTH // PAIR + p)), nat, nat, tbl] + [ANY] * nt,
        out_specs=[pl.BlockSpec((3, SEQ, PAIR), lambda p, b: (0, b, p)), tbl] + [ANY] * nt,
        out_shape=[jax.ShapeDtypeStruct((3, m, B_WIDTH), BF16),
                   jax.ShapeDtypeStruct((N_CFG, B_HEADS, 2 * Q_BLOCK, Q_BLOCK), F32)]
        + [jax.ShapeDtypeStruct((3,) + p.shape[1:], p.dtype) for p in parts],
        scratch_shapes=[acc, acc, acc, acc] + _sem_pair(3 * nt),
        compiler_params=_params(("arbitrary", "arbitrary")), name="attn_bwd",
    )(qkv, qkv, qkv, dmix, o, lse, bias_t, *parts)
    return res[0], res[1], list(res[2:])


def _rel_bias_grad(ds, buckets_np):
    present = _present_buckets(buckets_np)

    def body(bk_ref, ds_ref, o_ref, acc_ref):
        acc_ref[...] = jnp.zeros_like(acc_ref)
        for c in range(N_CFG):
            bk = bk_ref[c]
            for h in range(B_HEADS):
                dsv = ds_ref[c, h]
                for b in present[c]:
                    part = jnp.sum(jnp.where(bk == b, dsv, 0.0), axis=0, keepdims=True)
                    acc_ref[pl.ds(h * NUM_BUCKETS + b, 1), :] += part
        o_ref[...] = jnp.sum(acc_ref[...], axis=1, keepdims=True)

    vm = pl.BlockSpec(memory_space=pltpu.VMEM)
    return pl.pallas_call(
        body, in_specs=[vm, vm], out_specs=vm,
        out_shape=jax.ShapeDtypeStruct((B_HEADS * NUM_BUCKETS, 1), F32),
        scratch_shapes=[pltpu.VMEM((B_HEADS * NUM_BUCKETS, buckets_np.shape[-1]), F32)],
        compiler_params=_params(), name="rel_bias_grad",
    )(jnp.asarray(buckets_np), ds)


def _assemble_dproj(duv, dqkv):
    m = duv.shape[0]
    rows = 1024

    def body(duv_ref, dqkv_ref, o_ref):
        o_ref[:, :2 * A_WIDTH] = duv_ref[...].astype(BF16)
        for k in range(3):
            o_ref[:, 2 * A_WIDTH + k * B_WIDTH:2 * A_WIDTH + (k + 1) * B_WIDTH] = dqkv_ref[k]

    return pl.pallas_call(
        body, grid=(m // rows,),
        in_specs=[pl.BlockSpec((rows, 2 * A_WIDTH), lambda i: (i, 0)),
                  pl.BlockSpec((3, rows, B_WIDTH), lambda i: (0, i, 0))],
        out_specs=pl.BlockSpec((rows, IN_COLS), lambda i: (i, 0)),
        out_shape=jax.ShapeDtypeStruct((m, IN_COLS), BF16),
        compiler_params=_params(("parallel",)), name="assemble_dproj",
    )(duv, dqkv)


def _shift_down(x, k):
    row = lax.broadcasted_iota(jnp.int32, x.shape, 0)
    return jnp.where(row >= k, pltpu.roll(x, k, 0), 0.0)


def _shift_up(x, k):
    n = x.shape[0]
    row = lax.broadcasted_iota(jnp.int32, x.shape, 0)
    return jnp.where(row < n - k, pltpu.roll(x, n - k, 0), 0.0)


def _convgate_fwd(gate, up, conv_w, conv_b, batch):
    m = gate.shape[0]

    def body(g_ref, u_ref, w_ref, b_ref, a_ref):
        g = g_ref[...].astype(F32)
        w = w_ref[...]
        c = b_ref[...] + w[0:1] * _shift_down(g, 2) + w[1:2] * _shift_down(g, 1) + w[2:3] * g
        a_ref[...] = (_gelu(c) * u_ref[...].astype(F32)).astype(BF16)

    blk = pl.BlockSpec((SEQ, LANE_BLOCK), lambda b, j: (b, j))
    return pl.pallas_call(
        body, grid=(batch, D_FF // LANE_BLOCK),
        in_specs=[blk, blk, pl.BlockSpec((3, LANE_BLOCK), lambda b, j: (0, j)),
                  pl.BlockSpec((1, LANE_BLOCK), lambda b, j: (0, j))],
        out_specs=blk,
        out_shape=jax.ShapeDtypeStruct((m, D_FF), BF16),
        compiler_params=_params(("parallel", "parallel")), name="convgate_fwd",
    )(gate, up, conv_w, conv_b)


def _convgate_bwd(gate, up, dact, conv_w, conv_b, batch):
    m = gate.shape[0]

    def body(g_ref, u_ref, da_ref, w_ref, b_ref, dg_ref, du_ref, dw_ref, db_ref):
        @pl.when(pl.program_id(1) == 0)
        def _():
            dw_ref[...] = jnp.zeros_like(dw_ref)
            db_ref[...] = jnp.zeros_like(db_ref)

        g = g_ref[...].astype(F32)
        w = w_ref[...]
        g1 = _shift_down(g, 1)
        g2 = _shift_down(g, 2)
        c = b_ref[...] + w[0:1] * g2 + w[1:2] * g1 + w[2:3] * g
        gg, dgg = _gelu_and_grad(c)
        da = da_ref[...].astype(F32)
        du_ref[...] = (da * gg).astype(BF16)
        dc = da * u_ref[...].astype(F32) * dgg
        db_ref[...] += jnp.sum(dc, axis=0, keepdims=True)
        dw_ref[0:1, :] += jnp.sum(dc * g2, axis=0, keepdims=True)
        dw_ref[1:2, :] += jnp.sum(dc * g1, axis=0, keepdims=True)
        dw_ref[2:3, :] += jnp.sum(dc * g, axis=0, keepdims=True)
        dg_ref[...] = (w[2:3] * dc + w[1:2] * _shift_up(dc, 1) + w[0:1] * _shift_up(dc, 2)).astype(BF16)

    blk = pl.BlockSpec((SEQ, LANE_BLOCK), lambda j, b: (b, j))
    wspec = pl.BlockSpec((3, LANE_BLOCK), lambda j, b: (0, j))
    bspec = pl.BlockSpec((1, LANE_BLOCK), lambda j, b: (0, j))
    return pl.pallas_call(
        body, grid=(D_FF // LANE_BLOCK, batch),
        in_specs=[blk, blk, blk, wspec, bspec],
        out_specs=[blk, blk, wspec, bspec],
        out_shape=[jax.ShapeDtypeStruct((m, D_FF), BF16), jax.ShapeDtypeStruct((m, D_FF), BF16),
                   jax.ShapeDtypeStruct((3, D_FF), F32), jax.ShapeDtypeStruct((1, D_FF), F32)],
        compiler_params=_params(("parallel", "arbitrary")), name="convgate_bwd",
    )(gate, up, dact, conv_w, conv_b)


def _gather_weights(shards, conv_w_shard):
    nt = len(shards)
    shapes = [sh.shape for sh in shards]
    ts = list(range(nt))

    def body(*refs):
        shard_refs = refs[:nt]
        cw_ref = refs[nt]
        out_refs = refs[nt + 1:2 * nt + 1]
        cw_out = refs[2 * nt + 1]
        send_sems, recv_sems, cw_send, cw_recv = refs[2 * nt + 2:]
        plan = _GatherPlan(shapes, shard_refs, out_refs, send_sems, recv_sems)
        x, y, c, chips = _mesh_pos()

        def cw_copy(j, src, dst, chip):
            return pltpu.make_async_remote_copy(src_ref=src, dst_ref=dst, send_sem=cw_send.at[j],
                                                recv_sem=cw_recv.at[j], device_id=(*chip, c), device_id_type=MESH)

        plan.start(ts)
        cw_sends = [cw_copy(j, cw_ref, cw_out.at[2 * x + y], chip) for j, chip in enumerate(chips)]
        for cp in cw_sends:
            cp.start()
        plan.forward(ts)
        for j, chip in enumerate(chips):
            dst = cw_out.at[2 * chip[0] + chip[1]]
            cw_copy(j, dst, dst, chip).wait_recv()
        plan.finish(ts)
        for cp in cw_sends:
            cp.wait_send()

    out_shape = [jax.ShapeDtypeStruct((N_SHARD,) + sh.shape, sh.dtype) for sh in shards]
    out_shape.append(jax.ShapeDtypeStruct((N_SHARD,) + conv_w_shard.shape, conv_w_shard.dtype))
    return pl.pallas_call(
        body, in_specs=[ANY] * (nt + 1), out_specs=[ANY] * (nt + 1), out_shape=out_shape,
        scratch_shapes=_sem_pair(6 * nt) + _sem_pair(3),
        compiler_params=pltpu.CompilerParams(has_side_effects=True), name="gather_weights",
    )(*shards, conv_w_shard)


def _exchange_halves(grads, name):
    nt = len(grads)
    shapes = [g.shape for g in grads]

    def body(*refs):
        plan = _SiblingExchangePlan(shapes, refs[:nt], refs[nt:2 * nt], *refs[2 * nt:])
        plan.start()
        plan.finish()

    out_shape = [jax.ShapeDtypeStruct((N_SHARD, g.shape[1] // 2, g.shape[2]), g.dtype) for g in grads]
    return pl.pallas_call(
        body, in_specs=[ANY] * nt, out_specs=[ANY] * nt, out_shape=out_shape,
        scratch_shapes=_sem_pair(nt),
        compiler_params=pltpu.CompilerParams(has_side_effects=True), name=name,
    )(*grads)


def _add_halves(g, recv, c_idx):
    _, rows2, cols = g.shape
    rows = rows2 // 2
    tr = rows // 2 if rows % 16 == 0 and rows >= 256 else rows
    nblk = rows // tr

    def body(c_ref, g_ref, r_ref, o_ref):
        o_ref[...] = (g_ref[...] + r_ref[...]).astype(BF16)

    return pl.pallas_call(
        body,
        grid_spec=pltpu.PrefetchScalarGridSpec(
            num_scalar_prefetch=1, grid=(N_SHARD, nblk),
            in_specs=[pl.BlockSpec((None, tr, cols), lambda s, i, c: (s, c[0] * nblk + i, 0)),
                      pl.BlockSpec((None, tr, cols), lambda s, i, c: (s, i, 0))],
            out_specs=pl.BlockSpec((None, tr, cols), lambda s, i, c: (s, i, 0))),
        out_shape=jax.ShapeDtypeStruct((N_SHARD, rows, cols), BF16),
        compiler_params=_params(("parallel", "parallel")), name="rs_add_halves",
    )(c_idx, g, recv)


def _add_chips(part, recv, s_idx, c_idx):
    _, rows, cols = part.shape
    tr = rows // 2 if rows % 32 == 0 and rows >= 256 else rows
    nblk = rows // tr

    def body(idx_ref, p_ref, r_ref, o_ref):
        acc = p_ref[...].astype(F32)
        for j in range(3):
            acc = acc + r_ref[j].astype(F32)
        o_ref[...] = acc

    return pl.pallas_call(
        body,
        grid_spec=pltpu.PrefetchScalarGridSpec(
            num_scalar_prefetch=1, grid=(nblk,),
            in_specs=[pl.BlockSpec((None, tr, cols), lambda i, idx: (idx[0], i, 0)),
                      pl.BlockSpec((3, tr, cols), lambda i, idx: (0, i, 0))],
            out_specs=pl.BlockSpec((tr, cols), lambda i, idx: (idx[1] * nblk + i, 0))),
        out_shape=jax.ShapeDtypeStruct((2 * rows, cols), F32),
        compiler_params=_params(("parallel",)), name="rs_add_chips",
    )(jnp.concatenate([s_idx, c_idx]), part, recv)


def _share_halves(fulls):
    nt = len(fulls)

    def body(*refs):
        out_refs = refs[nt:2 * nt]
        send_sems, recv_sems = refs[2 * nt:]
        x, y, c, _ = _mesh_pos()
        copies = []
        for t in range(nt):
            rows = fulls[t].shape[0] // 2
            mine = out_refs[t].at[pl.ds(c * rows, rows), :]
            copies.append(pltpu.make_async_remote_copy(
                src_ref=mine, dst_ref=mine, send_sem=send_sems.at[t], recv_sem=recv_sems.at[t],
                device_id=(x, y, 1 - c), device_id_type=MESH))
        for cp in copies:
            cp.start()
        for t in range(nt):
            rows = fulls[t].shape[0] // 2
            theirs = out_refs[t].at[pl.ds((1 - c) * rows, rows), :]
            pltpu.make_async_remote_copy(
                src_ref=theirs, dst_ref=theirs, send_sem=send_sems.at[t], recv_sem=recv_sems.at[t],
                device_id=(x, y, 1 - c), device_id_type=MESH).wait_recv()
        for cp in copies:
            cp.wait_send()

    out_shape = [jax.ShapeDtypeStruct(f.shape, f.dtype) for f in fulls]
    return pl.pallas_call(
        body, in_specs=[ANY] * nt, out_specs=[ANY] * nt, out_shape=out_shape,
        input_output_aliases={t: t for t in range(nt)},
        scratch_shapes=_sem_pair(nt),
        compiler_params=pltpu.CompilerParams(has_side_effects=True), name="rs_share_halves",
    )(*fulls)


def _allreduce_small(arrays):
    n = len(arrays)

    def body(*refs):
        in_refs, out_refs = refs[:n], refs[n:2 * n]
        sib_refs, chip_refs = refs[2 * n:3 * n], refs[3 * n:4 * n]
        send_sems, recv_sems = refs[4 * n:]
        x, y, c, chips = _mesh_pos()

        def copy(k, src, dst, to):
            return pltpu.make_async_remote_copy(src_ref=src, dst_ref=dst, send_sem=send_sems.at[k],
                                                recv_sem=recv_sems.at[k], device_id=to, device_id_type=MESH)

        first = [copy(t, in_refs[t], sib_refs[t], (x, y, 1 - c)) for t in range(n)]
        for cp in first:
            cp.start()
        for cp in first:
            cp.wait()
        for t in range(n):
            out_refs[t][...] = in_refs[t][...] + sib_refs[t][...]
        second = [copy(n + 3 * t + j, out_refs[t], chip_refs[t].at[j], (*chip, c))
                  for t in range(n) for j, chip in enumerate(chips)]
        for cp in second:
            cp.start()
        for cp in second:
            cp.wait()
        for t in range(n):
            out_refs[t][...] = (out_refs[t][...] + chip_refs[t][0]) + (chip_refs[t][1] + chip_refs[t][2])

    vm = pl.BlockSpec(memory_space=pltpu.VMEM)
    return pl.pallas_call(
        body, in_specs=[vm] * n, out_specs=[vm] * n,
        out_shape=[jax.ShapeDtypeStruct(a.shape, F32) for a in arrays],
        scratch_shapes=[pltpu.VMEM(a.shape, F32) for a in arrays] + [pltpu.VMEM((3,) + a.shape, F32) for a in arrays]
        + _sem_pair(4 * n),
        compiler_params=pltpu.CompilerParams(has_side_effects=True, vmem_limit_bytes=VMEM_LIMIT),
        name="allreduce_small",
    )(*arrays)


def _from_col_shards(g):
    n, rows, cols = g.shape
    return g.transpose(1, 0, 2).reshape(rows, n * cols)


def _train_step(x, tgt, g1, g2, g3, g4, shards, ln_g, ln_b, w_s, b_s, rel_bias, conv_w_shard, conv_b, batch,
                s_idx, c_idx):
    big = dict(tm=1024, out_dtype=F32)
    buckets = _bucket_tables()
    bias, bias_t = _bias_tables(rel_bias, buckets)
    bz = jnp.repeat(b_s.T, HEAD_DIM, axis=1)
    w_st = jnp.swapaxes(w_s, 1, 2)

    def with_own(gathered, own):
        return lax.dynamic_update_index_in_dim(gathered, own, s_idx[0], 0)

    def shard_major(g):
        return g.reshape(N_SHARD, g.shape[0] // N_SHARD, D_MODEL)

    g_in, g_convw = _gather_weights([shards["w_in"]], conv_w_shard)
    w_in_t = with_own(g_in, shards["w_in"]).reshape(IN_COLS, D_MODEL)
    conv_w = _from_col_shards(with_own(g_convw, conv_w_shard))

    h1, uv, qkv = _proj_fwd(x, g1, w_in_t)
    a = _gate_fwd(uv, ln_g, ln_b, w_s, bz)
    later = ["w_out", "w_gate", "w_up", "w_down"]
    o_bf, lse, gathered = _attn_fwd(qkv, bias, batch, [shards[n] for n in later])
    g_out, g_gate, g_up, g_down = [with_own(g, shards[n]) for g, n in zip(gathered, later)]
    w_out = g_out.reshape(D_MODEL, D_MODEL)
    w_gate_t = g_gate.reshape(D_FF, D_MODEL)
    w_up_t = g_up.reshape(D_FF, D_MODEL)
    w_down = g_down.reshape(D_FF, D_MODEL)
    (y1, x1, h2), _ = _fused_rows(
        "out_proj_mid_fwd", 512,
        [(a, w_out, "nn", slice(0, A_WIDTH)), (o_bf, w_out, "nn", slice(A_WIDTH, D_MODEL))],
        [x], [g2, g3], _mid_fwd_rows, [F32, F32, BF16], [])
    gate = _mm(h2, w_gate_t, dims="nt", tm=1024, tn=1408, tk=1024, out_dtype=BF16, name="mm_gate")
    up = _mm(h2, w_up_t, dims="nt", tm=1024, tn=1408, tk=1024, out_dtype=BF16, name="mm_up")
    act = _convgate_fwd(gate, up, conv_w, conv_b, batch)
    (dx2, dy2, dg4, loss), _ = _fused_rows(
        "down_proj_loss_head", 512, [(act, w_down, "nn", None)], [x1, tgt], [g4], _loss_head_rows,
        [F32, BF16], [(1, D_MODEL), (1, 128)])

    dact = _mm(dy2, w_down, dims="nt", tm=1024, tn=1408, tk=1024, out_dtype=BF16, name="mm_dact")
    dw_down = _mm(act, dy2, dims="tn", tm=1408, tn=1024, tk=1024, out_dtype=F32, name="mm_dw_down")
    dgate, dup, dconv_w, dconv_b = _convgate_bwd(gate, up, dact, conv_w, conv_b, batch)
    (dx1, dy1, dg2, dg3), _ = _fused_rows(
        "dh2_mid_bwd", 256, [(dgate, w_gate_t, "nn", None), (dup, w_up_t, "nn", None)],
        [x1, y1, dx2], [g2, g3], _mid_bwd_rows, [F32, BF16], [(1, D_MODEL), (1, D_MODEL)])
    dw_gate_t = _mm(dgate, h2, dims="tn", tm=1408, tn=1024, tk=1024, out_dtype=F32, name="mm_dw_gate")
    dw_up_t = _mm(dup, h2, dims="tn", tm=1408, tn=1024, tk=1024, out_dtype=F32, name="mm_dw_up")
    dmix = _mm(dy1, w_out, dims="nt", tn=1024, tk=1024, name="mm_dmix", **big)
    dw_out_a = _mm(a, dy1, dims="tn", tm=A_WIDTH, tn=1024, tk=1024, out_dtype=F32, name="mm_dw_out_a")
    dw_out_b = _mm(o_bf, dy1, dims="tn", tm=B_WIDTH, tn=1024, tk=1024, out_dtype=F32, name="mm_dw_out_b")

    dw_out = jnp.concatenate([dw_out_a, dw_out_b], axis=0)
    done = [shard_major(g) for g in (dw_down, dw_gate_t, dw_up_t, dw_out)]
    (duv, dln_g, dln_b, dw_s, dbz), recv_a = _gate_bwd(uv, dmix, ln_g, ln_b, w_s, w_st, bz, done)
    parts = [_add_halves(g, r, c_idx) for g, r in zip(done, recv_a)]
    dqkv, ds, recv = _attn_bwd(qkv, dmix, o_bf, lse, bias_t, batch, parts)
    fulls = [_add_chips(p, r, s_idx, c_idx) for p, r in zip(parts, recv)]
    drel = _rel_bias_grad(ds, np.ascontiguousarray(np.swapaxes(buckets, 1, 2)))
    dproj = _assemble_dproj(duv, dqkv)
    dw_in_t = _mm(dproj, h1, dims="tn", tm=1408, tn=1024, tk=1024, out_dtype=F32, name="mm_dw_in")
    last = [shard_major(dw_in_t)]
    part_in = [_add_halves(g, r, c_idx) for g, r in zip(last, _exchange_halves(last, "rs_sibling_exchange_in"))]
    (dx0, dg1), recv_in = _fused_rows(
        "dh1_in_bwd", 512, [(dproj, w_in_t, "nn", None)], [x, dx1], [g1], _in_bwd_rows,
        [F32], [(1, D_MODEL)], exchange=part_in)
    fulls += [_add_chips(p, r, s_idx, c_idx) for p, r in zip(part_in, recv_in)]
    reduced = dict(zip(["w_down", "w_gate", "w_up", "w_out", "w_in"], _share_halves(fulls)))

    small = dict(
        loss=loss, norm_mix_pre=dg1, norm_mix_post=dg2, norm_ffn_pre=dg3, norm_ffn_post=dg4,
        ln_v_gain=dln_g, ln_v_bias=dln_b, spatial_w=dw_s, spatial_b=dbz, rel_bias=drel,
        conv_w=dconv_w, conv_b=dconv_b,
    )
    return dx0, small, reduced


def _adamw_update(w, g, m, v):
    nm = ADAM_B1 * m + (1.0 - ADAM_B1) * g
    nv = ADAM_B2 * v + (1.0 - ADAM_B2) * (g * g)
    m_hat = nm / (1.0 - ADAM_B1 ** ADAM_STEP)
    v_hat = nv / (1.0 - ADAM_B2 ** ADAM_STEP)
    return -ADAM_LR * (m_hat / (jnp.sqrt(v_hat) + ADAM_EPS) + ADAM_WD * w), nm, nv


def _adamw(w, g, m, v, name):
    rows, cols = w.shape
    tr = next(cand for cand in (256, 176, 128) if rows % cand == 0)

    def body(w_ref, g_ref, m_ref, v_ref, go_ref, d_ref, nm_ref, nv_ref):
        gv = g_ref[...]
        go_ref[...] = gv
        d_ref[...], nm_ref[...], nv_ref[...] = _adamw_update(w_ref[...], gv, m_ref[...], v_ref[...])

    spec = pl.BlockSpec((tr, cols), lambda i: (i, 0))
    sds = jax.ShapeDtypeStruct((rows, cols), F32)
    return pl.pallas_call(
        body, grid=(rows // tr,), in_specs=[spec] * 4, out_specs=[spec] * 4, out_shape=[sds] * 4,
        compiler_params=_params(("parallel",)), name=name,
    )(w, g, m, v)


def _adamw_small(ws, gs, ms, vs):
    n = len(ws)

    def body(*refs):
        w_refs, g_refs, m_refs, v_refs = refs[:n], refs[n:2 * n], refs[2 * n:3 * n], refs[3 * n:4 * n]
        d_refs, nm_refs, nv_refs = refs[4 * n:5 * n], refs[5 * n:6 * n], refs[6 * n:7 * n]
        for t in range(n):
            d_refs[t][...], nm_refs[t][...], nv_refs[t][...] = _adamw_update(
                w_refs[t][...], g_refs[t][...], m_refs[t][...], v_refs[t][...])

    vm = pl.BlockSpec(memory_space=pltpu.VMEM)
    sds = [jax.ShapeDtypeStruct(w.shape, F32) for w in ws]
    res = pl.pallas_call(
        body, in_specs=[vm] * (4 * n), out_specs=[vm] * (3 * n), out_shape=sds * 3,
        compiler_params=_params(), name="adamw_small",
    )(*ws, *gs, *ms, *vs)
    return res[:n], res[n:2 * n], res[2 * n:]


SMALL = ["norm_mix_pre", "norm_mix_post", "norm_ffn_pre", "norm_ffn_post", "ln_v_gain", "ln_v_bias",
         "spatial_w", "spatial_b", "rel_bias", "conv_b"]
LARGE = ["w_in", "w_gate", "w_up", "w_down", "w_out"]
TRANSPOSED = ("w_in", "w_gate", "w_up")
ORDER = ["norm_mix_pre", "norm_mix_post", "norm_ffn_pre", "norm_ffn_post", "w_in", "ln_v_gain", "ln_v_bias",
         "spatial_w", "spatial_b", "rel_bias", "w_out", "w_gate", "w_up", "conv_w", "conv_b", "w_down"]


def kernel(x, norm_mix_pre, norm_mix_post, norm_ffn_pre, norm_ffn_post, w_in, ln_v_gain, ln_v_bias, spatial_w, spatial_b, rel_bias, w_out, w_gate, w_up, conv_w, conv_b, w_down, loss_target, m_norm_mix_pre, m_norm_mix_post, m_norm_ffn_pre, m_norm_ffn_post, m_w_in, m_ln_v_gain, m_ln_v_bias, m_spatial_w, m_spatial_b, m_rel_bias, m_w_out, m_w_gate, m_w_up, m_conv_w, m_conv_b, m_w_down, v_norm_mix_pre, v_norm_mix_post, v_norm_ffn_pre, v_norm_ffn_post, v_w_in, v_ln_v_gain, v_ln_v_bias, v_spatial_w, v_spatial_b, v_rel_bias, v_w_out, v_w_gate, v_w_up, v_conv_w, v_conv_b, v_w_down):
    params = dict(norm_mix_pre=norm_mix_pre, norm_mix_post=norm_mix_post, norm_ffn_pre=norm_ffn_pre,
                  norm_ffn_post=norm_ffn_post, w_in=w_in, ln_v_gain=ln_v_gain, ln_v_bias=ln_v_bias,
                  spatial_w=spatial_w, spatial_b=spatial_b, rel_bias=rel_bias, w_out=w_out, w_gate=w_gate,
                  w_up=w_up, conv_w=conv_w, conv_b=conv_b, w_down=w_down)
    mom = dict(norm_mix_pre=m_norm_mix_pre, norm_mix_post=m_norm_mix_post, norm_ffn_pre=m_norm_ffn_pre,
               norm_ffn_post=m_norm_ffn_post, w_in=m_w_in, ln_v_gain=m_ln_v_gain, ln_v_bias=m_ln_v_bias,
               spatial_w=m_spatial_w, spatial_b=m_spatial_b, rel_bias=m_rel_bias, w_out=m_w_out, w_gate=m_w_gate,
               w_up=m_w_up, conv_w=m_conv_w, conv_b=m_conv_b, w_down=m_w_down)
    var = dict(norm_mix_pre=v_norm_mix_pre, norm_mix_post=v_norm_mix_post, norm_ffn_pre=v_norm_ffn_pre,
               norm_ffn_post=v_norm_ffn_post, w_in=v_w_in, ln_v_gain=v_ln_v_gain, ln_v_bias=v_ln_v_bias,
               spatial_w=v_spatial_w, spatial_b=v_spatial_b, rel_bias=v_rel_bias, w_out=v_w_out, w_gate=v_w_gate,
               w_up=v_w_up, conv_w=v_conv_w, conv_b=v_conv_b, w_down=v_w_down)

    batch = x.shape[0]
    xi, yi, ci = lax.axis_index("x"), lax.axis_index("y"), lax.axis_index("c")
    s_idx = (2 * xi + yi).astype(jnp.int32).reshape(1)
    c_idx = ci.astype(jnp.int32).reshape(1)

    def local(a, n):
        return jnp.swapaxes(a[0], 0, 1) if n in TRANSPOSED else a[0]

    shards = {n: local(params[n], n).astype(BF16) for n in LARGE}
    dx0, partial, reduced = _train_step(
        x.reshape(batch * SEQ, D_MODEL), loss_target.reshape(batch * SEQ, D_MODEL),
        norm_mix_pre, norm_mix_post, norm_ffn_pre, norm_ffn_post, shards,
        ln_v_gain.reshape(1, A_WIDTH), ln_v_bias.reshape(1, A_WIDTH), spatial_w[0], spatial_b[0], rel_bias,
        conv_w[0], conv_b, batch, s_idx, c_idx)
    grad_x = dx0.reshape(batch, SEQ, D_MODEL)

    names = list(partial)
    total = dict(zip(names, _allreduce_small([partial[n] for n in names])))
    loss = total["loss"][0, 0]
    total["spatial_b"] = total["spatial_b"][:, ::HEAD_DIM].T
    total["rel_bias"] = total["rel_bias"].reshape(B_HEADS, NUM_BUCKETS).T
    total["conv_w"] = lax.dynamic_slice_in_dim(total["conv_w"], s_idx[0] * SHARD_FF, SHARD_FF, axis=1)
    small_names = SMALL + ["conv_w"]
    for n in small_names:
        reduced[n] = total[n].reshape(params[n].shape)

    out_g, out_d, out_m, out_v = {}, {}, {}, {}
    for n in LARGE:
        res = _adamw(local(params[n], n), reduced[n], local(mom[n], n), local(var[n], n), name=f"adamw_{n}")
        if n in TRANSPOSED:
            res = [jnp.swapaxes(r, 0, 1) for r in res]
        out_g[n], out_d[n], out_m[n], out_v[n] = [r[None] for r in res]
    d, nm, nv = _adamw_small([params[n] for n in small_names], [reduced[n] for n in small_names],
                             [mom[n] for n in small_names], [var[n] for n in small_names])
    for n, dd, mm, vv in zip(small_names, d, nm, nv):
        out_g[n], out_d[n], out_m[n], out_v[n] = reduced[n], dd, mm, vv

    return (loss, grad_x, *[out_g[n] for n in ORDER], *[out_d[n] for n in ORDER],
            *[out_m[n] for n in ORDER], *[out_v[n] for n in ORDER])
```

```python
import functools
import math

import numpy as np
import jax
import jax.numpy as jnp
from jax import lax
from jax.experimental import pallas as pl
from jax.experimental.pallas import tpu as pltpu

F32 = jnp.float32
BF16 = jnp.bfloat16
MESH = pl.DeviceIdType.MESH

D_MODEL = 1024
SEQ = 2048
HEAD_DIM = 64
A_GROUPS = 4
A_WIDTH = 256
B_HEADS = 12
B_WIDTH = 768
CHUNK = 128
DILATED = ((128, 1), (512, 4), (2048, 16))
NUM_BUCKETS = 32
MAX_DISTANCE = 2048
D_FF = 2816
IN_COLS = 2816
NORM_EPS = 1e-6
NEG_INF = -1e30
N_SHARD = 4
SHARD_FF = D_FF // N_SHARD
LANE_BLOCK = 256
VMEM_LIMIT = 56 * 1024 * 1024

ADAM_LR = 0.001
ADAM_B1 = 0.9
ADAM_B2 = 0.999
ADAM_EPS = 1e-08
ADAM_WD = 0.01
ADAM_STEP = 10

GELU_C = math.sqrt(2.0 / math.pi)
GELU_A = 0.044715

ANY = pl.BlockSpec(memory_space=pl.ANY)


def _params(sem=None):
    return pltpu.CompilerParams(dimension_semantics=sem, vmem_limit_bytes=VMEM_LIMIT)


def _dot(a, b, precision=None):
    return jnp.dot(a, b, preferred_element_type=F32, precision=precision)


def _dot_nt(a, b, precision=None):
    return lax.dot_general(a, b, (((1,), (1,)), ((), ())), preferred_element_type=F32, precision=precision)


def _dot_tn(a, b):
    return lax.dot_general(a, b, (((0,), (0,)), ((), ())), preferred_element_type=F32)


def _gelu(x):
    t = jnp.tanh(x * (GELU_C + (GELU_C * GELU_A) * (x * x)))
    return (0.5 * x) * (1.0 + t)


def _gelu_and_grad(x):
    x2 = x * x
    u = 1.0 + jnp.tanh(x * (GELU_C + (GELU_C * GELU_A) * x2))
    hx = 0.5 * x
    dg = u * (0.5 + hx * (2.0 - u) * (GELU_C + (3.0 * GELU_C * GELU_A) * x2))
    return hx * u, dg


def _mesh_pos():
    x, y, c = lax.axis_index("x"), lax.axis_index("y"), lax.axis_index("c")
    chips = [(1 - x, y), (x, 1 - y), (1 - x, 1 - y)]
    return x, y, c, chips


class _GatherPlan:
    def __init__(self, shapes, shard_refs, out_refs, send_sems, recv_sems):
        self.shapes, self.shard_refs, self.out_refs = shapes, shard_refs, out_refs
        self.send_sems, self.recv_sems = send_sems, recv_sems
        self.x, self.y, self.c, self.chips = _mesh_pos()
        self.sib = (self.x, self.y, 1 - self.c)

    def _half(self, t, chip, which):
        rows = self.shapes[t][0] // 2
        return self.out_refs[t].at[2 * chip[0] + chip[1], pl.ds(which * rows, rows), :]

    def _copy(self, k, src, dst, to):
        return pltpu.make_async_remote_copy(src_ref=src, dst_ref=dst, send_sem=self.send_sems.at[k],
                                            recv_sem=self.recv_sems.at[k], device_id=to, device_id_type=MESH)

    def _sends(self, t):
        rows = self.shapes[t][0] // 2
        src = self.shard_refs[t].at[pl.ds(self.c * rows, rows), :]
        return [self._copy(6 * t + j, src, self._half(t, (self.x, self.y), self.c), (*chip, self.c))
                for j, chip in enumerate(self.chips)]

    def _forwards(self, t):
        return [self._copy(6 * t + 3 + j, self._half(t, chip, self.c), self._half(t, chip, self.c), self.sib)
                for j, chip in enumerate(self.chips)]

    def start(self, ts):
        for t in ts:
            for cp in self._sends(t):
                cp.start()

    def forward(self, ts):
        for t in ts:
            for j, chip in enumerate(self.chips):
                landed = self._half(t, chip, self.c)
                self._copy(6 * t + j, landed, landed, (*chip, self.c)).wait_recv()
            for cp in self._forwards(t):
                cp.start()

    def finish(self, ts):
        for t in ts:
            for j, chip in enumerate(self.chips):
                other = self._half(t, chip, 1 - self.c)
                self._copy(6 * t + 3 + j, other, other, self.sib).wait_recv()
        for t in ts:
            for cp in self._sends(t) + self._forwards(t):
                cp.wait_send()


class _SiblingExchangePlan:
    def __init__(self, shapes, grad_refs, out_refs, send_sems, recv_sems):
        self.shapes, self.grad_refs, self.out_refs = shapes, grad_refs, out_refs
        self.send_sems, self.recv_sems = send_sems, recv_sems
        self.x, self.y, self.c, _ = _mesh_pos()

    def _copies(self):
        out = []
        for t, (g, o) in enumerate(zip(self.grad_refs, self.out_refs)):
            rows = self.shapes[t][1] // 2
            out.append(pltpu.make_async_remote_copy(
                src_ref=g.at[:, pl.ds((1 - self.c) * rows, rows), :], dst_ref=o, send_sem=self.send_sems.at[t],
                recv_sem=self.recv_sems.at[t], device_id=(self.x, self.y, 1 - self.c), device_id_type=MESH))
        return out

    def start(self):
        for cp in self._copies():
            cp.start()

    def finish(self):
        for cp in self._copies():
            cp.wait()


class _ChipExchangePlan:
    def __init__(self, part_refs, out_refs, send_sems, recv_sems):
        self.part_refs, self.out_refs, self.send_sems, self.recv_sems = part_refs, out_refs, send_sems, recv_sems
        _, _, self.c, self.chips = _mesh_pos()

    def _copies(self):
        return [pltpu.make_async_remote_copy(
            src_ref=p.at[2 * chip[0] + chip[1]], dst_ref=o.at[j], send_sem=self.send_sems.at[3 * t + j],
            recv_sem=self.recv_sems.at[3 * t + j], device_id=(*chip, self.c), device_id_type=MESH)
            for t, (p, o) in enumerate(zip(self.part_refs, self.out_refs)) for j, chip in enumerate(self.chips)]

    def start(self):
        for cp in self._copies():
            cp.start()

    def finish(self):
        for cp in self._copies():
            cp.wait()


def _sem_pair(n):
    return [pltpu.SemaphoreType.DMA((n,)), pltpu.SemaphoreType.DMA((n,))]


def _mm(a, b, *, dims, tm, tn, tk, out_dtype, name):
    if dims == "nn":
        m, k = a.shape
        n = b.shape[1]
        a_spec = pl.BlockSpec((tm, tk), lambda i, j, kk: (i, kk))
        b_spec = pl.BlockSpec((tk, tn), lambda i, j, kk: (kk, j))
        dot = _dot
    elif dims == "nt":
        m, k = a.shape
        n = b.shape[0]
        a_spec = pl.BlockSpec((tm, tk), lambda i, j, kk: (i, kk))
        b_spec = pl.BlockSpec((tn, tk), lambda i, j, kk: (j, kk))
        dot = _dot_nt
    else:
        k, m = a.shape
        n = b.shape[1]
        a_spec = pl.BlockSpec((tk, tm), lambda i, j, kk: (kk, i))
        b_spec = pl.BlockSpec((tk, tn), lambda i, j, kk: (kk, j))
        dot = _dot_tn
    assert m % tm == 0 and n % tn == 0 and k % tk == 0, (name, m, n, k)
    grid = (m // tm, n // tn, k // tk)
    nk = grid[2]
    assert nk == 1 or out_dtype == F32, name

    def body(a_ref, b_ref, o_ref):
        prod = dot(a_ref[...].astype(BF16), b_ref[...].astype(BF16))
        if nk == 1:
            o_ref[...] = prod.astype(out_dtype)
        else:
            kk = pl.program_id(2)

            @pl.when(kk == 0)
            def _():
                o_ref[...] = prod

            @pl.when(kk > 0)
            def _():
                o_ref[...] += prod

    return pl.pallas_call(
        body, grid=grid, in_specs=[a_spec, b_spec],
        out_specs=pl.BlockSpec((tm, tn), lambda i, j, kk: (i, j)),
        out_shape=jax.ShapeDtypeStruct((m, n), out_dtype),
        compiler_params=_params(("parallel", "parallel", "arbitrary")), name=name,
    )(a, b)


def _fused_rows(name, tm, mats, rows, vecs, fn, row_outs, acc_outs, exchange=()):
    m = mats[0][0].shape[0]
    nm, nr, nv, nro, nao, nx = len(mats), len(rows), len(vecs), len(row_outs), len(acc_outs), len(exchange)
    n_steps = m // tm

    def body(*refs):
        a_refs, w_refs = refs[:nm], refs[nm:2 * nm]
        pos = 2 * nm
        row_refs, vec_refs, part_refs = refs[pos:pos + nr], refs[pos + nr:pos + nr + nv], refs[pos + nr + nv:pos + nr + nv + nx]
        pos += nr + nv + nx
        out_refs, acc_refs, recv_refs = refs[pos:pos + nro], refs[pos + nro:pos + nro + nao], refs[pos + nro + nao:pos + nro + nao + nx]
        sems = refs[pos + nro + nao + nx:]
        i = pl.program_id(0)
        if nx:
            plan = _ChipExchangePlan(part_refs, recv_refs, *sems)

            @pl.when(i == 0)
            def _():
                plan.start()

        @pl.when(i == 0)
        def _():
            for r in acc_refs:
                r[...] = jnp.zeros_like(r)

        y = None
        for a_ref, w_ref, (_, _, dims, sl) in zip(a_refs, w_refs, mats):
            w = w_ref[...] if sl is None else w_ref[sl, :]
            part = (_dot if dims == "nn" else _dot_nt)(a_ref[...], w)
            y = part if y is None else y + part
        res = fn(y, *[r[...] for r in row_refs], *[v[...] for v in vec_refs])
        for r, val in zip(out_refs, res[:nro]):
            r[...] = val.astype(r.dtype)
        for r, val in zip(acc_refs, res[nro:]):
            r[...] += val

        if nx:
            @pl.when(i == n_steps - 1)
            def _():
                plan.finish()

    tile = lambda width: pl.BlockSpec((tm, width), lambda i: (i, 0))
    res = pl.pallas_call(
        body, grid=(n_steps,),
        in_specs=[tile(a.shape[1]) for a, _, _, _ in mats] + [_full_spec(w.shape) for _, w, _, _ in mats]
        + [tile(D_MODEL)] * nr + [_full_spec((1, D_MODEL))] * nv + [ANY] * nx,
        out_specs=[tile(D_MODEL)] * nro + [_full_spec(s) for s in acc_outs] + [ANY] * nx,
        out_shape=[jax.ShapeDtypeStruct((m, D_MODEL), dt) for dt in row_outs]
        + [jax.ShapeDtypeStruct(s, F32) for s in acc_outs]
        + [jax.ShapeDtypeStruct((3,) + p.shape[1:], p.dtype) for p in exchange],
        scratch_shapes=_sem_pair(3 * nx) if nx else [],
        compiler_params=_params(("arbitrary",)), name=name,
    )(*[a for a, _, _, _ in mats], *[w for _, w, _, _ in mats], *rows, *vecs, *exchange)
    return list(res[:nro + nao]), list(res[nro + nao:])


ROW_TILE = 512


def _vec_spec(width=D_MODEL):
    return pl.BlockSpec((1, width), lambda i: (0, 0))


def _rstd(v):
    return lax.rsqrt(jnp.mean(v * v, axis=-1, keepdims=True) + NORM_EPS)


def _mid_fwd_rows(y1, x0, g2, g3):
    x1 = x0 + y1 * _rstd(y1) * g2
    return y1, x1, x1 * _rstd(x1) * g3


def _rms_bwd_rows(dout, v, g):
    r = _rstd(v)
    n = v * r
    dn = dout * g
    dv = r * (dn - n * jnp.mean(dn * n, axis=-1, keepdims=True))
    dg = jnp.sum(dout * n, axis=0, keepdims=True)
    return dv, dg


def _loss_head_rows(y2, x1, tgt, g4):
    x2 = x1 + y2 * _rstd(y2) * g4
    err = x2 - tgt
    loss = 0.5 * jnp.sum(jnp.mean(err * err, axis=-1, keepdims=True), axis=0, keepdims=True)
    dx2 = err * (1.0 / D_MODEL)
    dy2, dg4 = _rms_bwd_rows(dx2, y2, g4)
    return dx2, dy2, dg4, loss


def _mid_bwd_rows(dh2, x1, y1, dx2, g2, g3):
    d3, dg3 = _rms_bwd_rows(dh2, x1, g3)
    dx1 = dx2 + d3
    dy1, dg2 = _rms_bwd_rows(dx1, y1, g2)
    return dx1, dy1, dg2, dg3


def _in_bwd_rows(dh1, x0, dx1, g1):
    d1, dg1 = _rms_bwd_rows(dh1, x0, g1)
    return dx1 + d1, dg1


GATE_ROWS = 512


def _group_mean_matrix():
    p = np.zeros((A_WIDTH, A_WIDTH), np.float32)
    for g in range(A_GROUPS):
        p[g * HEAD_DIM:(g + 1) * HEAD_DIM, g * HEAD_DIM:(g + 1) * HEAD_DIM] = 1.0 / HEAD_DIM
    return jnp.asarray(p)


def _group_masks(width=A_WIDTH):
    lane = lax.broadcasted_iota(jnp.int32, (1, width), 1)
    return [(lane >= g * HEAD_DIM) & (lane < (g + 1) * HEAD_DIM) for g in range(width // HEAD_DIM)]


def _layernorm_groups(vg, pavg):
    hi = lax.Precision.HIGHEST
    mu = _dot(vg, pavg, hi)
    xc = vg - mu
    var = _dot(xc * xc, pavg, hi)
    rstd = lax.rsqrt(var + NORM_EPS)
    return xc * rstd, rstd


def _spatial_mix(w_bf, vn_chunk_bf, masks, bz):
    z = bz
    for g in range(A_GROUPS):
        z = z + jnp.where(masks[g], _dot(w_bf[g], vn_chunk_bf), 0.0)
    return z


def _full_spec(shape):
    return pl.BlockSpec(shape, lambda i: tuple(0 for _ in shape))


def _gate_fwd(uv, ln_g, ln_b, w_s, bz):
    m = uv.shape[0]
    pavg = _group_mean_matrix()

    def body(u_ref, v_ref, lg_ref, lb_ref, w_ref, bz_ref, p_ref, a_ref):
        masks = _group_masks()
        row = lax.broadcasted_iota(jnp.int32, (CHUNK, CHUNK), 0)
        col = lax.broadcasted_iota(jnp.int32, (CHUNK, CHUNK), 1)
        w_bf = [jnp.where(row >= col, w_ref[g], 0.0).astype(BF16) for g in range(A_GROUPS)]
        ug = _gelu(u_ref[...])
        vhat, _ = _layernorm_groups(_gelu(v_ref[...]), p_ref[...])
        vn = vhat * lg_ref[...] + lb_ref[...]
        bz = bz_ref[...]
        for c in range(GATE_ROWS // CHUNK):
            sl = slice(c * CHUNK, (c + 1) * CHUNK)
            z = _spatial_mix(w_bf, vn[sl].astype(BF16), masks, bz)
            a_ref[sl, :] = (ug[sl] * z).astype(BF16)

    return pl.pallas_call(
        body, grid=(m // GATE_ROWS,),
        in_specs=[pl.BlockSpec((GATE_ROWS, A_WIDTH), lambda i: (i, 0)),
                  pl.BlockSpec((GATE_ROWS, A_WIDTH), lambda i: (i, 1)),
                  _full_spec((1, A_WIDTH)), _full_spec((1, A_WIDTH)), _full_spec((A_GROUPS, CHUNK, CHUNK)),
                  _full_spec((CHUNK, A_WIDTH)), _full_spec((A_WIDTH, A_WIDTH))],
        out_specs=pl.BlockSpec((GATE_ROWS, A_WIDTH), lambda i: (i, 0)),
        out_shape=jax.ShapeDtypeStruct((m, A_WIDTH), BF16),
        compiler_params=_params(("parallel",)), name="gate_fwd",
    )(uv, uv, ln_g, ln_b, w_s, bz, pavg)


def _gate_bwd(uv, dmix, ln_g, ln_b, w_s, w_st, bz, grads):
    m = uv.shape[0]
    pavg = _group_mean_matrix()
    nsteps = m // GATE_ROWS
    nx = len(grads)
    shapes = [g.shape for g in grads]

    def body(u_ref, v_ref, da_ref, lg_ref, lb_ref, w_ref, wt_ref, bz_ref, p_ref, *rest):
        grad_refs = rest[:nx]
        duv_ref, dlg_ref, dlb_ref, dw_ref, dbz_ref = rest[nx:nx + 5]
        recv_refs = rest[nx + 5:2 * nx + 5]
        exchange = _SiblingExchangePlan(shapes, grad_refs, recv_refs, *rest[2 * nx + 5:])
        i = pl.program_id(0)

        @pl.when(i == 0)
        def _():
            exchange.start()
            dlg_ref[...] = jnp.zeros_like(dlg_ref)
            dlb_ref[...] = jnp.zeros_like(dlb_ref)
            dw_ref[...] = jnp.zeros_like(dw_ref)
            dbz_ref[...] = jnp.zeros_like(dbz_ref)

        hi = lax.Precision.HIGHEST
        masks = _group_masks()
        row = lax.broadcasted_iota(jnp.int32, (CHUNK, CHUNK), 0)
        col = lax.broadcasted_iota(jnp.int32, (CHUNK, CHUNK), 1)
        tril = row >= col
        w_bf = [jnp.where(tril, w_ref[g], 0.0).astype(BF16) for g in range(A_GROUPS)]
        wt_bf = [jnp.where(col >= row, wt_ref[g], 0.0).astype(BF16) for g in range(A_GROUPS)]
        pavg_v = p_ref[...]
        lg = lg_ref[...]
        ug, dug = _gelu_and_grad(u_ref[...])
        vg, dvg_dx = _gelu_and_grad(v_ref[...])
        vhat, rstd = _layernorm_groups(vg, pavg_v)
        vn = vhat * lg + lb_ref[...]
        da = da_ref[...]
        bz = bz_ref[...]
        for c in range(GATE_ROWS // CHUNK):
            sl = slice(c * CHUNK, (c + 1) * CHUNK)
            vn_bf = vn[sl].astype(BF16)
            z = _spatial_mix(w_bf, vn_bf, masks, bz)
            dz = da[sl] * ug[sl]
            duv_ref[sl, 0:A_WIDTH] = da[sl] * z * dug[sl]
            dbz_ref[...] += dz
            dz_bf = dz.astype(BF16)
            dvn = jnp.zeros((CHUNK, A_WIDTH), F32)
            for g in range(A_GROUPS):
                dz_g = jnp.where(masks[g], dz, 0.0).astype(BF16)
                dw_ref[g] += jnp.where(tril, _dot_nt(dz_g, vn_bf), 0.0)
                dvn = dvn + jnp.where(masks[g], _dot(wt_bf[g], dz_bf), 0.0)
            vh = vhat[sl]
            dlb_ref[...] += jnp.sum(dvn, axis=0, keepdims=True)
            dlg_ref[...] += jnp.sum(dvn * vh, axis=0, keepdims=True)
            dvh = dvn * lg
            m1 = _dot(dvh, pavg_v, hi)
            m2 = _dot(dvh * vh, pavg_v, hi)
            duv_ref[sl, A_WIDTH:2 * A_WIDTH] = rstd[sl] * (dvh - m1 - vh * m2) * dvg_dx[sl]

        @pl.when(i == nsteps - 1)
        def _():
            dbz_ref[...] = _dot(dbz_ref[...], pavg_v * float(HEAD_DIM), hi)
            exchange.finish()

    res = pl.pallas_call(
        body, grid=(nsteps,),
        in_specs=[pl.BlockSpec((GATE_ROWS, A_WIDTH), lambda i: (i, 0)),
                  pl.BlockSpec((GATE_ROWS, A_WIDTH), lambda i: (i, 1)),
                  pl.BlockSpec((GATE_ROWS, A_WIDTH), lambda i: (i, 0)),
                  _full_spec((1, A_WIDTH)), _full_spec((1, A_WIDTH)), _full_spec((A_GROUPS, CHUNK, CHUNK)),
                  _full_spec((A_GROUPS, CHUNK, CHUNK)), _full_spec((CHUNK, A_WIDTH)),
                  _full_spec((A_WIDTH, A_WIDTH))] + [ANY] * nx,
        out_specs=[pl.BlockSpec((GATE_ROWS, 2 * A_WIDTH), lambda i: (i, 0)),
                   _full_spec((1, A_WIDTH)), _full_spec((1, A_WIDTH)), _full_spec((A_GROUPS, CHUNK, CHUNK)),
                   _full_spec((CHUNK, A_WIDTH))] + [ANY] * nx,
        out_shape=[jax.ShapeDtypeStruct((m, 2 * A_WIDTH), F32),
                   jax.ShapeDtypeStruct((1, A_WIDTH), F32), jax.ShapeDtypeStruct((1, A_WIDTH), F32),
                   jax.ShapeDtypeStruct((A_GROUPS, CHUNK, CHUNK), F32),
                   jax.ShapeDtypeStruct((CHUNK, A_WIDTH), F32)]
        + [jax.ShapeDtypeStruct((N_SHARD, s[1] // 2, s[2]), F32) for s in shapes],
        scratch_shapes=_sem_pair(nx),
        compiler_params=_params(("arbitrary",)), name="gate_bwd",
    )(uv, uv, dmix, ln_g, ln_b, w_s, w_st, bz, pavg, *grads)
    return res[:5], list(res[5:])


Q_BLOCK = 128
PAIR = 2 * HEAD_DIM
N_PAIR = B_HEADS // 2
N_CFG = len(DILATED)
BLOCKS_PER_CFG = SEQ // Q_BLOCK
QKV_SLABS = 3 * N_PAIR
FWD_BLOCKS_PER_TRIP = 8
BWD_BLOCKS_PER_TRIP = 4


def _t5_bucket_np(dist, dtype):
    max_exact = NUM_BUCKETS // 2
    d = np.maximum(dist, 1).astype(dtype)
    large = max_exact + (np.log(d / dtype(max_exact)) / dtype(math.log(MAX_DISTANCE / max_exact))
                         * dtype(NUM_BUCKETS - max_exact))
    large = np.minimum(large.astype(np.int32), NUM_BUCKETS - 1)
    return np.where(dist < max_exact, dist, large)


def _bucket_tables():
    i = np.arange(Q_BLOCK)[:, None]
    j = np.arange(Q_BLOCK)[None, :]
    tables = []
    for _, dil in DILATED:
        rel_prev = Q_BLOCK + i - j
        rel_cur = i - j
        rel = np.concatenate([rel_prev, rel_cur], axis=1)
        valid = np.concatenate([rel_prev <= Q_BLOCK, rel_cur >= 0], axis=1)
        dist = np.maximum(rel, 0) * dil
        b32 = _t5_bucket_np(dist, np.float32)
        b64 = _t5_bucket_np(dist, np.float64)
        assert np.array_equal(b32, b64)
        tables.append(np.where(valid, b32, -1).astype(np.int32))
    return np.stack(tables)


def _present_buckets(buckets_np):
    return [sorted(set(int(v) for v in np.unique(buckets_np[c]) if v >= 0)) for c in range(N_CFG)]


def _bias_tables(rel_bias, buckets_np):
    present = _present_buckets(buckets_np)

    def body(rb_ref, bk_ref, o_ref, ot_ref):
        for c in range(N_CFG):
            bk = bk_ref[c]
            for h in range(B_HEADS):
                acc = jnp.full((Q_BLOCK, 2 * Q_BLOCK), NEG_INF, F32)
                for b in present[c]:
                    acc = jnp.where(bk == b, rb_ref[b, h], acc)
                o_ref[c, h] = acc
                ot_ref[c, h] = acc.T

    vm = pl.BlockSpec(memory_space=pltpu.VMEM)
    return pl.pallas_call(
        body,
        in_specs=[pl.BlockSpec(memory_space=pltpu.SMEM), vm],
        out_specs=[vm, vm],
        out_shape=[jax.ShapeDtypeStruct((N_CFG, B_HEADS, Q_BLOCK, 2 * Q_BLOCK), F32),
                   jax.ShapeDtypeStruct((N_CFG, B_HEADS, 2 * Q_BLOCK, Q_BLOCK), F32)],
        compiler_params=_params(), name="bias_tables",
    )(rel_bias, jnp.asarray(buckets_np))


def _proj_fwd(x, g1, w_in_t):
    m = x.shape[0]
    tm = ROW_TILE

    def body(x_ref, g_ref, w_ref, h_ref, uv_ref, qkv_ref):
        xv = x_ref[...]
        h = (xv * _rstd(xv) * g_ref[...]).astype(BF16)
        h_ref[...] = h
        acc = _dot_nt(h, w_ref[...])
        uv_ref[...] = acc[:, :2 * A_WIDTH]
        for s in range(QKV_SLABS):
            qkv_ref[s] = acc[:, 2 * A_WIDTH + s * PAIR:2 * A_WIDTH + (s + 1) * PAIR]

    return pl.pallas_call(
        body, grid=(m // tm,),
        in_specs=[pl.BlockSpec((tm, D_MODEL), lambda i: (i, 0)), _vec_spec(),
                  pl.BlockSpec((IN_COLS, D_MODEL), lambda i: (0, 0))],
        out_specs=[pl.BlockSpec((tm, D_MODEL), lambda i: (i, 0)),
                   pl.BlockSpec((tm, 2 * A_WIDTH), lambda i: (i, 0)),
                   pl.BlockSpec((QKV_SLABS, tm, PAIR), lambda i: (0, i, 0))],
        out_shape=[jax.ShapeDtypeStruct((m, D_MODEL), BF16), jax.ShapeDtypeStruct((m, 2 * A_WIDTH), F32),
                   jax.ShapeDtypeStruct((QKV_SLABS, m, PAIR), F32)],
        compiler_params=_params(("parallel",)), name="proj_fwd",
    )(x, g1, w_in_t)


def _pair_masks():
    lane = lax.broadcasted_iota(jnp.int32, (1, PAIR), 1)
    return [lane < HEAD_DIM, lane >= HEAD_DIM]


def _block_rows(idx, dil):
    static = isinstance(idx, int)
    r, n = idx % dil, idx // dil

    def rows_of(block):
        start = r + (dil * Q_BLOCK) * block
        if dil == 1:
            return pl.ds(start if static else pl.multiple_of(start, Q_BLOCK), Q_BLOCK)
        return pl.ds(start, Q_BLOCK, stride=dil)

    prev = rows_of(n - 1) if not static or n > 0 else None
    return rows_of(n), prev


def _attn_fwd(qkv, bias, batch, shards):
    m = qkv.shape[1]
    comb_rows = 256
    nt = len(shards)
    shapes = [sh.shape for sh in shards]
    n_steps = batch * N_PAIR
    early, late = list(range(nt // 2)), list(range(nt // 2, nt))

    def body(q_ref, k_ref, v_ref, b_ref, *rest):
        shard_refs = rest[:nt]
        o_ref, l_ref = rest[nt:nt + 2]
        gat_refs = rest[nt + 2:2 * nt + 2]
        scratch = rest[2 * nt + 2:]
        oc_refs, lc_refs = scratch[:N_CFG], scratch[N_CFG:2 * N_CFG]
        step = pl.program_id(0) * N_PAIR + pl.program_id(1)
        gather = _GatherPlan(shapes, shard_refs, gat_refs, *scratch[2 * N_CFG:])

        @pl.when(step == 0)
        def _():
            gather.start(early + late)

        @pl.when(step == n_steps // 2)
        def _():
            gather.forward(early)

        @pl.when(step == n_steps - 2)
        def _():
            gather.forward(late)

        masks = _pair_masks()
        for ci, (_, dil) in enumerate(DILATED):
            nb = SEQ // dil // Q_BLOCK

            def block(trip, ci=ci, dil=dil, nb=nb):
                work = []
                for u in range(FWD_BLOCKS_PER_TRIP):
                    rows, prow = _block_rows(trip * FWD_BLOCKS_PER_TRIP + u, dil)
                    has_prev = nb > 1 and prow is not None
                    q = q_ref[rows, :] * 0.125
                    kc = k_ref[rows, :].astype(BF16)
                    vc = v_ref[rows, :]
                    kp = k_ref[prow, :].astype(BF16) if has_prev else None
                    vp = v_ref[prow, :] if has_prev else None
                    tiles = []
                    for h in range(2):
                        qh = jnp.where(masks[h], q, 0.0).astype(BF16)
                        sc = _dot_nt(qh, kc) + b_ref[ci, h, :, Q_BLOCK:]
                        sp = _dot_nt(qh, kp) + b_ref[ci, h, :, :Q_BLOCK] if has_prev else None
                        tiles.append((sc, sp))
                    work.append((rows, vc, vp, tiles))
                probs = []
                for _, _, _, tiles in work:
                    ps = []
                    for sc, sp in tiles:
                        mx = jnp.max(sc if sp is None else jnp.maximum(sc, sp), axis=1, keepdims=True)
                        pc = jnp.exp(sc - mx).astype(BF16)
                        pp = None if sp is None else jnp.exp(sp - mx).astype(BF16)
                        ps.append((mx, pc, pp))
                    probs.append(ps)
                for (rows, vc, vp, _), ps in zip(work, probs):
                    res = []
                    for h, (_, pc, pp) in enumerate(ps):
                        r = _dot(pc, jnp.where(masks[h], vc, 1.0).astype(BF16))
                        if pp is not None:
                            r = r + _dot(pp, jnp.where(masks[h], vp, 1.0).astype(BF16))
                        res.append(r)
                    num = jnp.where(masks[0], res[0], res[1])
                    den = pltpu.roll(jnp.where(masks[0], res[1], res[0]), HEAD_DIM, 1)
                    oc_refs[ci][rows, :] = num / den
                    lc_refs[ci][rows, :] = jnp.where(masks[0], ps[0][0], ps[1][0]) + jnp.log(den)

            for trip in range(BLOCKS_PER_CFG // FWD_BLOCKS_PER_TRIP):
                block(trip)

        def combine(i, carry):
            rr = pl.ds(pl.multiple_of(i * comb_rows, comb_rows), comb_rows)
            ls = [lc_refs[c][rr, :] for c in range(N_CFG)]
            mx = functools.reduce(jnp.maximum, ls)
            ws = [jnp.exp(l - mx) for l in ls]
            tot = functools.reduce(lambda a, b: a + b, ws)
            o = functools.reduce(lambda a, b: a + b, [ws[c] * oc_refs[c][rr, :] for c in range(N_CFG)]) / tot
            o_ref[rr, :] = o.astype(BF16)
            l_ref[rr, :] = mx + jnp.log(tot)
            return carry

        lax.fori_loop(0, SEQ // comb_rows, combine, 0)

        @pl.when(step == n_steps - 1)
        def _():
            gather.finish(early + late)

    def slab(first):
        return pl.BlockSpec((None, SEQ, PAIR), lambda b, p: (first + p, b, 0))

    nat = pl.BlockSpec((SEQ, PAIR), lambda b, p: (b, p))
    res = pl.pallas_call(
        body, grid=(batch, N_PAIR),
        in_specs=[slab(0), slab(N_PAIR), slab(2 * N_PAIR),
                  pl.BlockSpec((N_CFG, 2, Q_BLOCK, 2 * Q_BLOCK), lambda b, p: (0, p, 0, 0))] + [ANY] * nt,
        out_specs=[nat, nat] + [ANY] * nt,
        out_shape=[jax.ShapeDtypeStruct((m, B_WIDTH), BF16), jax.ShapeDtypeStruct((m, B_WIDTH), F32)]
        + [jax.ShapeDtypeStruct((N_SHARD,) + sh.shape, sh.dtype) for sh in shards],
        scratch_shapes=[pltpu.VMEM((SEQ, PAIR), F32)] * (2 * N_CFG) + _sem_pair(6 * nt),
        compiler_params=_params(("arbitrary", "arbitrary")), name="attn_fwd",
    )(qkv, qkv, qkv, bias, *shards)
    return res[0], res[1], list(res[2:])


def _attn_bwd(qkv, dmix, o, lse, bias_t, batch, parts):
    m = qkv.shape[1]
    nt = len(parts)
    n_steps = N_PAIR * batch

    def body(q_ref, k_ref, v_ref, do_ref, o_ref, l_ref, b_ref, *rest):
        part_refs = rest[:nt]
        dqkv_ref, ds_ref = rest[nt:nt + 2]
        recv_refs = rest[nt + 2:2 * nt + 2]
        dq_acc, dk_acc, dv_acc, d_scr, send_sems, recv_sems = rest[2 * nt + 2:]
        step = pl.program_id(0) * batch + pl.program_id(1)
        exchange = _ChipExchangePlan(part_refs, recv_refs, send_sems, recv_sems)

        @pl.when(step == 0)
        def _():
            exchange.start()

        @pl.when(pl.program_id(1) == 0)
        def _():
            ds_ref[...] = jnp.zeros_like(ds_ref)

        dq_acc[...] = jnp.zeros_like(dq_acc)
        dk_acc[...] = jnp.zeros_like(dk_acc)
        dv_acc[...] = jnp.zeros_like(dv_acc)
        d_scr[...] = do_ref[...] * o_ref[...].astype(F32)
        masks = _pair_masks()

        def stack_heads(t):
            return jnp.concatenate([jnp.where(masks[0], t, 0.0), jnp.where(masks[1], t, 0.0)], axis=0).astype(BF16)

        for ci, (_, dil) in enumerate(DILATED):
            nb = SEQ // dil // Q_BLOCK

            def block(trip, carry, ci=ci, dil=dil, nb=nb):
                first = []
                for u in range(BWD_BLOCKS_PER_TRIP):
                    rows, prow = _block_rows(trip * BWD_BLOCKS_PER_TRIP + u, dil)
                    has_prev = nb > 1 and prow is not None
                    if has_prev:
                        kcat = jnp.concatenate([k_ref[prow, :], k_ref[rows, :]], axis=0).astype(BF16)
                        vcat = jnp.concatenate([v_ref[prow, :], v_ref[rows, :]], axis=0).astype(BF16)
                    else:
                        kcat = k_ref[rows, :].astype(BF16)
                        vcat = v_ref[rows, :].astype(BF16)
                    qst = stack_heads(q_ref[rows, :] * 0.125)
                    dost = stack_heads(do_ref[rows, :])
                    lt = l_ref[rows, :].T
                    dt = d_scr[rows, :].T
                    lrow = jnp.concatenate([lt[0:1], lt[HEAD_DIM:HEAD_DIM + 1]], axis=1)
                    drow = jnp.concatenate([jnp.sum(dt[:HEAD_DIM], axis=0, keepdims=True),
                                            jnp.sum(dt[HEAD_DIM:], axis=0, keepdims=True)], axis=1)
                    first.append((has_prev, rows, prow, kcat, qst, dost, lrow, drow,
                                  _dot_nt(kcat, qst), _dot_nt(vcat, dost)))
                second = []
                for has_prev, rows, prow, kcat, qst, dost, lrow, drow, st, dpt in first:
                    keys = slice(0, 2 * Q_BLOCK) if has_prev else slice(Q_BLOCK, 2 * Q_BLOCK)
                    bt = jnp.concatenate([b_ref[ci, 0, keys, :], b_ref[ci, 1, keys, :]], axis=1)
                    pt = jnp.exp(st + bt - lrow)
                    dst = pt * (dpt - drow)
                    ds_ref[ci, 0, keys, :] += dst[:, :Q_BLOCK]
                    ds_ref[ci, 1, keys, :] += dst[:, Q_BLOCK:]
                    second.append((has_prev, rows, prow, kcat, qst, dost, pt.astype(BF16), dst.astype(BF16)))
                for has_prev, rows, prow, kcat, qst, dost, pt_bf, dst_bf in second:
                    dk = _dot(dst_bf, qst)
                    dv = _dot(pt_bf, dost)
                    dq2 = _dot_tn(dst_bf, kcat)
                    dq_acc[rows, :] += jnp.where(masks[0], dq2[:Q_BLOCK], dq2[Q_BLOCK:]) * 0.125
                    if has_prev:
                        dk_acc[prow, :] += dk[:Q_BLOCK]
                        dv_acc[prow, :] += dv[:Q_BLOCK]
                        dk_acc[rows, :] += dk[Q_BLOCK:]
                        dv_acc[rows, :] += dv[Q_BLOCK:]
                    else:
                        dk_acc[rows, :] += dk
                        dv_acc[rows, :] += dv
                return carry

            block(0, 0)
            lax.fori_loop(1, BLOCKS_PER_CFG // BWD_BLOCKS_PER_TRIP, block, 0)

        dqkv_ref[0] = dq_acc[...].astype(BF16)
        dqkv_ref[1] = dk_acc[...].astype(BF16)
        dqkv_ref[2] = dv_acc[...].astype(BF16)

        @pl.when(step == n_steps - 1)
        def _():
            exchange.finish()

    def slab(first):
        return pl.BlockSpec((None, SEQ, PAIR), lambda p, b: (first + p, b, 0))

    nat = pl.BlockSpec((SEQ, PAIR), lambda p, b: (b, p))
    tbl = pl.BlockSpec((N_CFG, 2, 2 * Q_BLOCK, Q_BLOCK), lambda p, b: (0, p, 0, 0))
    acc = pltpu.VMEM((SEQ, PAIR), F32)
    res = pl.pallas_call(
        body, grid=(N_PAIR, batch),
        in_specs=[slab(0), slab(N_PAIR), slab(2 * N_PAIR),
                  pl.BlockSpec((SEQ, PAIR), lambda p, b: (b, A_WIDTH // PAIR + p)), nat, nat, tbl] + [ANY] * nt,
        out_specs=[pl.BlockSpec((3, SEQ, PAIR), lambda p, b: (0, b, p)), tbl] + [ANY] * nt,
        out_shape=[jax.ShapeDtypeStruct((3, m, B_WIDTH), BF16),
                   jax.ShapeDtypeStruct((N_CFG, B_HEADS, 2 * Q_BLOCK, Q_BLOCK), F32)]
        + [jax.ShapeDtypeStruct((3,) + p.shape[1:], p.dtype) for p in parts],
        scratch_shapes=[acc, acc, acc, acc] + _sem_pair(3 * nt),
        compiler_params=_params(("arbitrary", "arbitrary")), name="attn_bwd",
    )(qkv, qkv, qkv, dmix, o, lse, bias_t, *parts)
    return res[0], res[1], list(res[2:])


def _rel_bias_grad(ds, buckets_np):
    present = _present_buckets(buckets_np)

    def body(bk_ref, ds_ref, o_ref, acc_ref):
        acc_ref[...] = jnp.zeros_like(acc_ref)
        for c in range(N_CFG):
            bk = bk_ref[c]
            for h in range(B_HEADS):
                dsv = ds_ref[c, h]
                for b in present[c]:
                    part = jnp.sum(jnp.where(bk == b, dsv, 0.0), axis=0, keepdims=True)
                    acc_ref[pl.ds(h * NUM_BUCKETS + b, 1), :] += part
        o_ref[...] = jnp.sum(acc_ref[...], axis=1, keepdims=True)

    vm = pl.BlockSpec(memory_space=pltpu.VMEM)
    return pl.pallas_call(
        body, in_specs=[vm, vm], out_specs=vm,
        out_shape=jax.ShapeDtypeStruct((B_HEADS * NUM_BUCKETS, 1), F32),
        scratch_shapes=[pltpu.VMEM((B_HEADS * NUM_BUCKETS, buckets_np.shape[-1]), F32)],
        compiler_params=_params(), name="rel_bias_grad",
    )(jnp.asarray(buckets_np), ds)


def _assemble_dproj(duv, dqkv):
    m = duv.shape[0]
    rows = 1024

    def body(duv_ref, dqkv_ref, o_ref):
        o_ref[:, :2 * A_WIDTH] = duv_ref[...].astype(BF16)
        for k in range(3):
            o_ref[:, 2 * A_WIDTH + k * B_WIDTH:2 * A_WIDTH + (k + 1) * B_WIDTH] = dqkv_ref[k]

    return pl.pallas_call(
        body, grid=(m // rows,),
        in_specs=[pl.BlockSpec((rows, 2 * A_WIDTH), lambda i: (i, 0)),
                  pl.BlockSpec((3, rows, B_WIDTH), lambda i: (0, i, 0))],
        out_specs=pl.BlockSpec((rows, IN_COLS), lambda i: (i, 0)),
        out_shape=jax.ShapeDtypeStruct((m, IN_COLS), BF16),
        compiler_params=_params(("parallel",)), name="assemble_dproj",
    )(duv, dqkv)


def _row_index():
    return lax.broadcasted_iota(jnp.int32, (SEQ, LANE_BLOCK), 0)


def _shift_down(x, k, row):
    return jnp.where(row >= k, pltpu.roll(x, k, 0), 0.0)


def _shift_up(x, k, row):
    return jnp.where(row < SEQ - k, pltpu.roll(x, SEQ - k, 0), 0.0)


def _convgate_fwd(gate, up, conv_w, conv_b, batch):
    m = gate.shape[0]

    def body(g_ref, u_ref, w_ref, b_ref, a_ref):
        g = g_ref[...].astype(F32)
        w = w_ref[...]
        row = _row_index()
        c = b_ref[...] + w[0:1] * _shift_down(g, 2, row) + w[1:2] * _shift_down(g, 1, row) + w[2:3] * g
        a_ref[...] = (_gelu(c) * u_ref[...].astype(F32)).astype(BF16)

    blk = pl.BlockSpec((SEQ, LANE_BLOCK), lambda b, j: (b, j))
    return pl.pallas_call(
        body, grid=(batch, D_FF // LANE_BLOCK),
        in_specs=[blk, blk, pl.BlockSpec((3, LANE_BLOCK), lambda b, j: (0, j)),
                  pl.BlockSpec((1, LANE_BLOCK), lambda b, j: (0, j))],
        out_specs=blk,
        out_shape=jax.ShapeDtypeStruct((m, D_FF), BF16),
        compiler_params=_params(("parallel", "parallel")), name="convgate_fwd",
    )(gate, up, conv_w, conv_b)


def _convgate_bwd(gate, up, dact, conv_w, conv_b, batch):
    m = gate.shape[0]

    def body(g_ref, u_ref, da_ref, w_ref, b_ref, dg_ref, du_ref, dw_ref, db_ref):
        @pl.when(pl.program_id(1) == 0)
        def _():
            dw_ref[...] = jnp.zeros_like(dw_ref)
            db_ref[...] = jnp.zeros_like(db_ref)

        g = g_ref[...].astype(F32)
        w = w_ref[...]
        row = _row_index()
        g1 = _shift_down(g, 1, row)
        g2 = _shift_down(g, 2, row)
        c = b_ref[...] + w[0:1] * g2 + w[1:2] * g1 + w[2:3] * g
        gg, dgg = _gelu_and_grad(c)
        da = da_ref[...].astype(F32)
        du_ref[...] = (da * gg).astype(BF16)
        dc = da * u_ref[...].astype(F32) * dgg
        db_ref[...] += jnp.sum(dc, axis=0, keepdims=True)
        dw_ref[0:1, :] += jnp.sum(dc * g2, axis=0, keepdims=True)
        dw_ref[1:2, :] += jnp.sum(dc * g1, axis=0, keepdims=True)
        dw_ref[2:3, :] += jnp.sum(dc * g, axis=0, keepdims=True)
        dg_ref[...] = (w[2:3] * dc + w[1:2] * _shift_up(dc, 1, row) + w[0:1] * _shift_up(dc, 2, row)).astype(BF16)

    blk = pl.BlockSpec((SEQ, LANE_BLOCK), lambda j, b: (b, j))
    wspec = pl.BlockSpec((3, LANE_BLOCK), lambda j, b: (0, j))
    bspec = pl.BlockSpec((1, LANE_BLOCK), lambda j, b: (0, j))
    return pl.pallas_call(
        body, grid=(D_FF // LANE_BLOCK, batch),
        in_specs=[blk, blk, blk, wspec, bspec],
        out_specs=[blk, blk, wspec, bspec],
        out_shape=[jax.ShapeDtypeStruct((m, D_FF), BF16), jax.ShapeDtypeStruct((m, D_FF), BF16),
                   jax.ShapeDtypeStruct((3, D_FF), F32), jax.ShapeDtypeStruct((1, D_FF), F32)],
        compiler_params=_params(("parallel", "arbitrary")), name="convgate_bwd",
    )(gate, up, dact, conv_w, conv_b)


def _gather_weights(shards, conv_w_shard):
    nt = len(shards)
    shapes = [sh.shape for sh in shards]
    ts = list(range(nt))

    def body(*refs):
        shard_refs = refs[:nt]
        cw_ref = refs[nt]
        out_refs = refs[nt + 1:2 * nt + 1]
        cw_out = refs[2 * nt + 1]
        send_sems, recv_sems, cw_send, cw_recv = refs[2 * nt + 2:]
        plan = _GatherPlan(shapes, shard_refs, out_refs, send_sems, recv_sems)
        x, y, c, chips = _mesh_pos()

        def cw_copy(j, src, dst, chip):
            return pltpu.make_async_remote_copy(src_ref=src, dst_ref=dst, send_sem=cw_send.at[j],
                                                recv_sem=cw_recv.at[j], device_id=(*chip, c), device_id_type=MESH)

        plan.start(ts)
        cw_sends = [cw_copy(j, cw_ref, cw_out.at[2 * x + y], chip) for j, chip in enumerate(chips)]
        for cp in cw_sends:
            cp.start()
        plan.forward(ts)
        for j, chip in enumerate(chips):
            dst = cw_out.at[2 * chip[0] + chip[1]]
            cw_copy(j, dst, dst, chip).wait_recv()
        plan.finish(ts)
        for cp in cw_sends:
            cp.wait_send()

    out_shape = [jax.ShapeDtypeStruct((N_SHARD,) + sh.shape, sh.dtype) for sh in shards]
    out_shape.append(jax.ShapeDtypeStruct((N_SHARD,) + conv_w_shard.shape, conv_w_shard.dtype))
    return pl.pallas_call(
        body, in_specs=[ANY] * (nt + 1), out_specs=[ANY] * (nt + 1), out_shape=out_shape,
        scratch_shapes=_sem_pair(6 * nt) + _sem_pair(3),
        compiler_params=pltpu.CompilerParams(has_side_effects=True), name="gather_weights",
    )(*shards, conv_w_shard)


def _exchange_halves(grads, name):
    nt = len(grads)
    shapes = [g.shape for g in grads]

    def body(*refs):
        plan = _SiblingExchangePlan(shapes, refs[:nt], refs[nt:2 * nt], *refs[2 * nt:])
        plan.start()
        plan.finish()

    out_shape = [jax.ShapeDtypeStruct((N_SHARD, g.shape[1] // 2, g.shape[2]), g.dtype) for g in grads]
    return pl.pallas_call(
        body, in_specs=[ANY] * nt, out_specs=[ANY] * nt, out_shape=out_shape,
        scratch_shapes=_sem_pair(nt),
        compiler_params=pltpu.CompilerParams(has_side_effects=True), name=name,
    )(*grads)


def _add_halves(g, recv, c_idx):
    _, rows2, cols = g.shape
    rows = rows2 // 2
    tr = rows // 2 if rows % 16 == 0 and rows >= 256 else rows
    nblk = rows // tr

    def body(c_ref, g_ref, r_ref, o_ref):
        o_ref[...] = (g_ref[...] + r_ref[...]).astype(BF16)

    return pl.pallas_call(
        body,
        grid_spec=pltpu.PrefetchScalarGridSpec(
            num_scalar_prefetch=1, grid=(N_SHARD, nblk),
            in_specs=[pl.BlockSpec((None, tr, cols), lambda s, i, c: (s, c[0] * nblk + i, 0)),
                      pl.BlockSpec((None, tr, cols), lambda s, i, c: (s, i, 0))],
            out_specs=pl.BlockSpec((None, tr, cols), lambda s, i, c: (s, i, 0))),
        out_shape=jax.ShapeDtypeStruct((N_SHARD, rows, cols), BF16),
        compiler_params=_params(("parallel", "parallel")), name="rs_add_halves",
    )(c_idx, g, recv)


def _add_chips(part, recv, s_idx, c_idx):
    _, rows, cols = part.shape
    tr = rows // 2 if rows % 32 == 0 and rows >= 256 else rows
    nblk = rows // tr

    def body(idx_ref, p_ref, r_ref, o_ref):
        acc = p_ref[...].astype(F32)
        for j in range(3):
            acc = acc + r_ref[j].astype(F32)
        o_ref[...] = acc

    return pl.pallas_call(
        body,
        grid_spec=pltpu.PrefetchScalarGridSpec(
            num_scalar_prefetch=1, grid=(nblk,),
            in_specs=[pl.BlockSpec((None, tr, cols), lambda i, idx: (idx[0], i, 0)),
                      pl.BlockSpec((3, tr, cols), lambda i, idx: (0, i, 0))],
            out_specs=pl.BlockSpec((tr, cols), lambda i, idx: (idx[1] * nblk + i, 0))),
        out_shape=jax.ShapeDtypeStruct((2 * rows, cols), F32),
        compiler_params=_params(("parallel",)), name="rs_add_chips",
    )(jnp.concatenate([s_idx, c_idx]), part, recv)


def _share_halves(fulls):
    nt = len(fulls)

    def body(*refs):
        out_refs = refs[nt:2 * nt]
        send_sems, recv_sems = refs[2 * nt:]
        x, y, c, _ = _mesh_pos()
        copies = []
        for t in range(nt):
            rows = fulls[t].shape[0] // 2
            mine = out_refs[t].at[pl.ds(c * rows, rows), :]
            copies.append(pltpu.make_async_remote_copy(
                src_ref=mine, dst_ref=mine, send_sem=send_sems.at[t], recv_sem=recv_sems.at[t],
                device_id=(x, y, 1 - c), device_id_type=MESH))
        for cp in copies:
            cp.start()
        for t in range(nt):
            rows = fulls[t].shape[0] // 2
            theirs = out_refs[t].at[pl.ds((1 - c) * rows, rows), :]
            pltpu.make_async_remote_copy(
                src_ref=theirs, dst_ref=theirs, send_sem=send_sems.at[t], recv_sem=recv_sems.at[t],
                device_id=(x, y, 1 - c), device_id_type=MESH).wait_recv()
        for cp in copies:
            cp.wait_send()

    out_shape = [jax.ShapeDtypeStruct(f.shape, f.dtype) for f in fulls]
    return pl.pallas_call(
        body, in_specs=[ANY] * nt, out_specs=[ANY] * nt, out_shape=out_shape,
        input_output_aliases={t: t for t in range(nt)},
        scratch_shapes=_sem_pair(nt),
        compiler_params=pltpu.CompilerParams(has_side_effects=True), name="rs_share_halves",
    )(*fulls)


def _allreduce_small(arrays):
    n = len(arrays)

    def body(*refs):
        in_refs, out_refs = refs[:n], refs[n:2 * n]
        sib_refs, chip_refs = refs[2 * n:3 * n], refs[3 * n:4 * n]
        send_sems, recv_sems = refs[4 * n:]
        x, y, c, chips = _mesh_pos()

        def copy(k, src, dst, to):
            return pltpu.make_async_remote_copy(src_ref=src, dst_ref=dst, send_sem=send_sems.at[k],
                                                recv_sem=recv_sems.at[k], device_id=to, device_id_type=MESH)

        first = [copy(t, in_refs[t], sib_refs[t], (x, y, 1 - c)) for t in range(n)]
        for cp in first:
            cp.start()
        for cp in first:
            cp.wait()
        for t in range(n):
            out_refs[t][...] = in_refs[t][...] + sib_refs[t][...]
        second = [copy(n + 3 * t + j, out_refs[t], chip_refs[t].at[j], (*chip, c))
                  for t in range(n) for j, chip in enumerate(chips)]
        for cp in second:
            cp.start()
        for cp in second:
            cp.wait()
        for t in range(n):
            out_refs[t][...] = (out_refs[t][...] + chip_refs[t][0]) + (chip_refs[t][1] + chip_refs[t][2])

    vm = pl.BlockSpec(memory_space=pltpu.VMEM)
    return pl.pallas_call(
        body, in_specs=[vm] * n, out_specs=[vm] * n,
        out_shape=[jax.ShapeDtypeStruct(a.shape, F32) for a in arrays],
        scratch_shapes=[pltpu.VMEM(a.shape, F32) for a in arrays] + [pltpu.VMEM((3,) + a.shape, F32) for a in arrays]
        + _sem_pair(4 * n),
        compiler_params=pltpu.CompilerParams(has_side_effects=True, vmem_limit_bytes=VMEM_LIMIT),
        name="allreduce_small",
    )(*arrays)


def _from_col_shards(g):
    n, rows, cols = g.shape
    return g.transpose(1, 0, 2).reshape(rows, n * cols)


def _train_step(x, tgt, g1, g2, g3, g4, shards, ln_g, ln_b, w_s, b_s, rel_bias, conv_w_shard, conv_b, batch,
                s_idx, c_idx):
    big = dict(tm=1024, out_dtype=F32)
    buckets = _bucket_tables()
    bias, bias_t = _bias_tables(rel_bias, buckets)
    bz = jnp.repeat(b_s.T, HEAD_DIM, axis=1)
    w_st = jnp.swapaxes(w_s, 1, 2)

    def with_own(gathered, own):
        return lax.dynamic_update_index_in_dim(gathered, own, s_idx[0], 0)

    def shard_major(g):
        return g.reshape(N_SHARD, g.shape[0] // N_SHARD, D_MODEL)

    g_in, g_convw = _gather_weights([shards["w_in"]], conv_w_shard)
    w_in_t = with_own(g_in, shards["w_in"]).reshape(IN_COLS, D_MODEL)
    conv_w = _from_col_shards(with_own(g_convw, conv_w_shard))

    h1, uv, qkv = _proj_fwd(x, g1, w_in_t)
    a = _gate_fwd(uv, ln_g, ln_b, w_s, bz)
    later = ["w_out", "w_gate", "w_up", "w_down"]
    o_bf, lse, gathered = _attn_fwd(qkv, bias, batch, [shards[n] for n in later])
    g_out, g_gate, g_up, g_down = [with_own(g, shards[n]) for g, n in zip(gathered, later)]
    w_out = g_out.reshape(D_MODEL, D_MODEL)
    w_gate_t = g_gate.reshape(D_FF, D_MODEL)
    w_up_t = g_up.reshape(D_FF, D_MODEL)
    w_down = g_down.reshape(D_FF, D_MODEL)
    (y1, x1, h2), _ = _fused_rows(
        "out_proj_mid_fwd", 512,
        [(a, w_out, "nn", slice(0, A_WIDTH)), (o_bf, w_out, "nn", slice(A_WIDTH, D_MODEL))],
        [x], [g2, g3], _mid_fwd_rows, [F32, F32, BF16], [])
    gate = _mm(h2, w_gate_t, dims="nt", tm=1024, tn=1408, tk=1024, out_dtype=BF16, name="mm_gate")
    up = _mm(h2, w_up_t, dims="nt", tm=1024, tn=1408, tk=1024, out_dtype=BF16, name="mm_up")
    act = _convgate_fwd(gate, up, conv_w, conv_b, batch)
    (dx2, dy2, dg4, loss), _ = _fused_rows(
        "down_proj_loss_head", 512, [(act, w_down, "nn", None)], [x1, tgt], [g4], _loss_head_rows,
        [F32, BF16], [(1, D_MODEL), (1, 128)])

    dact = _mm(dy2, w_down, dims="nt", tm=1024, tn=1408, tk=1024, out_dtype=BF16, name="mm_dact")
    dw_down = _mm(act, dy2, dims="tn", tm=1408, tn=1024, tk=1024, out_dtype=F32, name="mm_dw_down")
    dgate, dup, dconv_w, dconv_b = _convgate_bwd(gate, up, dact, conv_w, conv_b, batch)
    (dx1, dy1, dg2, dg3), _ = _fused_rows(
        "dh2_mid_bwd", 256, [(dgate, w_gate_t, "nn", None), (dup, w_up_t, "nn", None)],
        [x1, y1, dx2], [g2, g3], _mid_bwd_rows, [F32, BF16], [(1, D_MODEL), (1, D_MODEL)])
    dw_gate_t = _mm(dgate, h2, dims="tn", tm=1408, tn=1024, tk=1024, out_dtype=F32, name="mm_dw_gate")
    dw_up_t = _mm(dup, h2, dims="tn", tm=1408, tn=1024, tk=1024, out_dtype=F32, name="mm_dw_up")
    dmix = _mm(dy1, w_out, dims="nt", tn=1024, tk=1024, name="mm_dmix", **big)
    dw_out_a = _mm(a, dy1, dims="tn", tm=A_WIDTH, tn=1024, tk=1024, out_dtype=F32, name="mm_dw_out_a")
    dw_out_b = _mm(o_bf, dy1, dims="tn", tm=B_WIDTH, tn=1024, tk=1024, out_dtype=F32, name="mm_dw_out_b")

    dw_out = jnp.concatenate([dw_out_a, dw_out_b], axis=0)
    done = [shard_major(g) for g in (dw_down, dw_gate_t, dw_up_t, dw_out)]
    (duv, dln_g, dln_b, dw_s, dbz), recv_a = _gate_bwd(uv, dmix, ln_g, ln_b, w_s, w_st, bz, done)
    parts = [_add_halves(g, r, c_idx) for g, r in zip(done, recv_a)]
    dqkv, ds, recv = _attn_bwd(qkv, dmix, o_bf, lse, bias_t, batch, parts)
    fulls = [_add_chips(p, r, s_idx, c_idx) for p, r in zip(parts, recv)]
    drel = _rel_bias_grad(ds, np.ascontiguousarray(np.swapaxes(buckets, 1, 2)))
    dproj = _assemble_dproj(duv, dqkv)
    dw_in_t = _mm(dproj, h1, dims="tn", tm=1408, tn=1024, tk=1024, out_dtype=F32, name="mm_dw_in")
    last = [shard_major(dw_in_t)]
    part_in = [_add_halves(g, r, c_idx) for g, r in zip(last, _exchange_halves(last, "rs_sibling_exchange_in"))]
    (dx0, dg1), recv_in = _fused_rows(
        "dh1_in_bwd", 512, [(dproj, w_in_t, "nn", None)], [x, dx1], [g1], _in_bwd_rows,
        [F32], [(1, D_MODEL)], exchange=part_in)
    fulls += [_add_chips(p, r, s_idx, c_idx) for p, r in zip(part_in, recv_in)]
    reduced = dict(zip(["w_down", "w_gate", "w_up", "w_out", "w_in"], _share_halves(fulls)))

    small = dict(
        loss=loss, norm_mix_pre=dg1, norm_mix_post=dg2, norm_ffn_pre=dg3, norm_ffn_post=dg4,
        ln_v_gain=dln_g, ln_v_bias=dln_b, spatial_w=dw_s, spatial_b=dbz, rel_bias=drel,
        conv_w=dconv_w, conv_b=dconv_b,
    )
    return dx0, small, reduced


def _adamw_update(w, g, m, v):
    nm = ADAM_B1 * m + (1.0 - ADAM_B1) * g
    nv = ADAM_B2 * v + (1.0 - ADAM_B2) * (g * g)
    m_hat = nm / (1.0 - ADAM_B1 ** ADAM_STEP)
    v_hat = nv / (1.0 - ADAM_B2 ** ADAM_STEP)
    return -ADAM_LR * (m_hat / (jnp.sqrt(v_hat) + ADAM_EPS) + ADAM_WD * w), nm, nv


def _adamw(w, g, m, v, name):
    rows, cols = w.shape
    tr = next(cand for cand in (256, 176, 128) if rows % cand == 0)

    def body(w_ref, g_ref, m_ref, v_ref, go_ref, d_ref, nm_ref, nv_ref):
        gv = g_ref[...]
        go_ref[...] = gv
        d_ref[...], nm_ref[...], nv_ref[...] = _adamw_update(w_ref[...], gv, m_ref[...], v_ref[...])

    spec = pl.BlockSpec((tr, cols), lambda i: (i, 0))
    sds = jax.ShapeDtypeStruct((rows, cols), F32)
    return pl.pallas_call(
        body, grid=(rows // tr,), in_specs=[spec] * 4, out_specs=[spec] * 4, out_shape=[sds] * 4,
        compiler_params=_params(("parallel",)), name=name,
    )(w, g, m, v)


def _adamw_small(ws, gs, ms, vs):
    n = len(ws)

    def body(*refs):
        w_refs, g_refs, m_refs, v_refs = refs[:n], refs[n:2 * n], refs[2 * n:3 * n], refs[3 * n:4 * n]
        d_refs, nm_refs, nv_refs = refs[4 * n:5 * n], refs[5 * n:6 * n], refs[6 * n:7 * n]
        for t in range(n):
            d_refs[t][...], nm_refs[t][...], nv_refs[t][...] = _adamw_update(
                w_refs[t][...], g_refs[t][...], m_refs[t][...], v_refs[t][...])

    vm = pl.BlockSpec(memory_space=pltpu.VMEM)
    sds = [jax.ShapeDtypeStruct(w.shape, F32) for w in ws]
    res = pl.pallas_call(
        body, in_specs=[vm] * (4 * n), out_specs=[vm] * (3 * n), out_shape=sds * 3,
        compiler_params=_params(), name="adamw_small",
    )(*ws, *gs, *ms, *vs)
    return res[:n], res[n:2 * n], res[2 * n:]


SMALL = ["norm_mix_pre", "norm_mix_post", "norm_ffn_pre", "norm_ffn_post", "ln_v_gain", "ln_v_bias",
         "spatial_w", "spatial_b", "rel_bias", "conv_b"]
LARGE = ["w_in", "w_gate", "w_up", "w_down", "w_out"]
TRANSPOSED = ("w_in", "w_gate", "w_up")
ORDER = ["norm_mix_pre", "norm_mix_post", "norm_ffn_pre", "norm_ffn_post", "w_in", "ln_v_gain", "ln_v_bias",
         "spatial_w", "spatial_b", "rel_bias", "w_out", "w_gate", "w_up", "conv_w", "conv_b", "w_down"]


def kernel(x, norm_mix_pre, norm_mix_post, norm_ffn_pre, norm_ffn_post, w_in, ln_v_gain, ln_v_bias, spatial_w, spatial_b, rel_bias, w_out, w_gate, w_up, conv_w, conv_b, w_down, loss_target, m_norm_mix_pre, m_norm_mix_post, m_norm_ffn_pre, m_norm_ffn_post, m_w_in, m_ln_v_gain, m_ln_v_bias, m_spatial_w, m_spatial_b, m_rel_bias, m_w_out, m_w_gate, m_w_up, m_conv_w, m_conv_b, m_w_down, v_norm_mix_pre, v_norm_mix_post, v_norm_ffn_pre, v_norm_ffn_post, v_w_in, v_ln_v_gain, v_ln_v_bias, v_spatial_w, v_spatial_b, v_rel_bias, v_w_out, v_w_gate, v_w_up, v_conv_w, v_conv_b, v_w_down):
    params = dict(norm_mix_pre=norm_mix_pre, norm_mix_post=norm_mix_post, norm_ffn_pre=norm_ffn_pre,
                  norm_ffn_post=norm_ffn_post, w_in=w_in, ln_v_gain=ln_v_gain, ln_v_bias=ln_v_bias,
                  spatial_w=spatial_w, spatial_b=spatial_b, rel_bias=rel_bias, w_out=w_out, w_gate=w_gate,
                  w_up=w_up, conv_w=conv_w, conv_b=conv_b, w_down=w_down)
    mom = dict(norm_mix_pre=m_norm_mix_pre, norm_mix_post=m_norm_mix_post, norm_ffn_pre=m_norm_ffn_pre,
               norm_ffn_post=m_norm_ffn_post, w_in=m_w_in, ln_v_gain=m_ln_v_gain, ln_v_bias=m_ln_v_bias,
               spatial_w=m_spatial_w, spatial_b=m_spatial_b, rel_bias=m_rel_bias, w_out=m_w_out, w_gate=m_w_gate,
               w_up=m_w_up, conv_w=m_conv_w, conv_b=m_conv_b, w_down=m_w_down)
    var = dict(norm_mix_pre=v_norm_mix_pre, norm_mix_post=v_norm_mix_post, norm_ffn_pre=v_norm_ffn_pre,
               norm_ffn_post=v_norm_ffn_post, w_in=v_w_in, ln_v_gain=v_ln_v_gain, ln_v_bias=v_ln_v_bias,
               spatial_w=v_spatial_w, spatial_b=v_spatial_b, rel_bias=v_rel_bias, w_out=v_w_out, w_gate=v_w_gate,
               w_up=v_w_up, conv_w=v_conv_w, conv_b=v_conv_b, w_down=v_w_down)

    batch = x.shape[0]
    xi, yi, ci = lax.axis_index("x"), lax.axis_index("y"), lax.axis_index("c")
    s_idx = (2 * xi + yi).astype(jnp.int32).reshape(1)
    c_idx = ci.astype(jnp.int32).reshape(1)

    def local(a, n):
        return jnp.swapaxes(a[0], 0, 1) if n in TRANSPOSED else a[0]

    shards = {n: local(params[n], n).astype(BF16) for n in LARGE}
    dx0, partial, reduced = _train_step(
        x.reshape(batch * SEQ, D_MODEL), loss_target.reshape(batch * SEQ, D_MODEL),
        norm_mix_pre, norm_mix_post, norm_ffn_pre, norm_ffn_post, shards,
        ln_v_gain.reshape(1, A_WIDTH), ln_v_bias.reshape(1, A_WIDTH), spatial_w[0], spatial_b[0], rel_bias,
        conv_w[0], conv_b, batch, s_idx, c_idx)
    grad_x = dx0.reshape(batch, SEQ, D_MODEL)

    names = list(partial)
    total = dict(zip(names, _allreduce_small([partial[n] for n in names])))
    loss = total["loss"][0, 0]
    total["spatial_b"] = total["spatial_b"][:, ::HEAD_DIM].T
    total["rel_bias"] = total["rel_bias"].reshape(B_HEADS, NUM_BUCKETS).T
    total["conv_w"] = lax.dynamic_slice_in_dim(total["conv_w"], s_idx[0] * SHARD_FF, SHARD_FF, axis=1)
    small_names = SMALL + ["conv_w"]
    for n in small_names:
        reduced[n] = total[n].reshape(params[n].shape)

    out_g, out_d, out_m, out_v = {}, {}, {}, {}
    for n in LARGE:
        res = _adamw(local(params[n], n), reduced[n], local(mom[n], n), local(var[n], n), name=f"adamw_{n}")
        if n in TRANSPOSED:
            res = [jnp.swapaxes(r, 0, 1) for r in res]
        out_g[n], out_d[n], out_m[n], out_v[n] = [r[None] for r in res]
    d, nm, nv = _adamw_small([params[n] for n in small_names], [reduced[n] for n in small_names],
                             [mom[n] for n in small_names], [var[n] for n in small_names])
    for n, dd, mm, vv in zip(small_names, d, nm, nv):
        out_g[n], out_d[n], out_m[n], out_v[n] = reduced[n], dd, mm, vv

    return (loss, grad_x, *[out_g[n] for n in ORDER], *[out_d[n] for n in ORDER],
            *[out_m[n] for n in ORDER], *[out_v[n] for n in ORDER])
```

```python
import functools
import math

import numpy as np
import jax
import jax.numpy as jnp
from jax import lax
from jax.experimental import pallas as pl
from jax.experimental.pallas import tpu as pltpu

F32 = jnp.float32
BF16 = jnp.bfloat16
MESH = pl.DeviceIdType.MESH

D_MODEL = 1024
SEQ = 2048
HEAD_DIM = 64
A_GROUPS = 4
A_WIDTH = 256
B_HEADS = 12
B_WIDTH = 768
CHUNK = 128
DILATED = ((128, 1), (512, 4), (2048, 16))
NUM_BUCKETS = 32
MAX_DISTANCE = 2048
D_FF = 2816
IN_COLS = 2816
NORM_EPS = 1e-6
NEG_INF = -1e30
N_SHARD = 4
SHARD_FF = D_FF // N_SHARD
LANE_BLOCK = 256
VMEM_LIMIT = 56 * 1024 * 1024

ADAM_LR = 0.001
ADAM_B1 = 0.9
ADAM_B2 = 0.999
ADAM_EPS = 1e-08
ADAM_WD = 0.01
ADAM_STEP = 10

GELU_C = math.sqrt(2.0 / math.pi)
GELU_A = 0.044715

ANY = pl.BlockSpec(memory_space=pl.ANY)


def _params(sem=None):
    return pltpu.CompilerParams(dimension_semantics=sem, vmem_limit_bytes=VMEM_LIMIT)


def _dot(a, b, precision=None):
    return jnp.dot(a, b, preferred_element_type=F32, precision=precision)


def _dot_nt(a, b, precision=None):
    return lax.dot_general(a, b, (((1,), (1,)), ((), ())), preferred_element_type=F32, precision=precision)


def _dot_tn(a, b):
    return lax.dot_general(a, b, (((0,), (0,)), ((), ())), preferred_element_type=F32)


def _gelu(x):
    t = jnp.tanh(x * (GELU_C + (GELU_C * GELU_A) * (x * x)))
    return (0.5 * x) * (1.0 + t)


def _gelu_and_grad(x):
    x2 = x * x
    u = 1.0 + jnp.tanh(x * (GELU_C + (GELU_C * GELU_A) * x2))
    hx = 0.5 * x
    dg = u * (0.5 + hx * (2.0 - u) * (GELU_C + (3.0 * GELU_C * GELU_A) * x2))
    return hx * u, dg


def _mesh_pos():
    x, y, c = lax.axis_index("x"), lax.axis_index("y"), lax.axis_index("c")
    chips = [(1 - x, y), (x, 1 - y), (1 - x, 1 - y)]
    return x, y, c, chips


class _GatherPlan:
    def __init__(self, shapes, shard_refs, out_refs, send_sems, recv_sems):
        self.shapes, self.shard_refs, self.out_refs = shapes, shard_refs, out_refs
        self.send_sems, self.recv_sems = send_sems, recv_sems
        self.x, self.y, self.c, self.chips = _mesh_pos()
        self.sib = (self.x, self.y, 1 - self.c)

    def _half(self, t, chip, which):
        rows = self.shapes[t][0] // 2
        return self.out_refs[t].at[2 * chip[0] + chip[1], pl.ds(which * rows, rows), :]

    def _copy(self, k, src, dst, to):
        return pltpu.make_async_remote_copy(src_ref=src, dst_ref=dst, send_sem=self.send_sems.at[k],
                                            recv_sem=self.recv_sems.at[k], device_id=to, device_id_type=MESH)

    def _sends(self, t):
        rows = self.shapes[t][0] // 2
        src = self.shard_refs[t].at[pl.ds(self.c * rows, rows), :]
        return [self._copy(6 * t + j, src, self._half(t, (self.x, self.y), self.c), (*chip, self.c))
                for j, chip in enumerate(self.chips)]

    def _forwards(self, t):
        return [self._copy(6 * t + 3 + j, self._half(t, chip, self.c), self._half(t, chip, self.c), self.sib)
                for j, chip in enumerate(self.chips)]

    def start(self, ts):
        for t in ts:
            for cp in self._sends(t):
                cp.start()

    def forward(self, ts):
        for t in ts:
            for j, chip in enumerate(self.chips):
                landed = self._half(t, chip, self.c)
                self._copy(6 * t + j, landed, landed, (*chip, self.c)).wait_recv()
            for cp in self._forwards(t):
                cp.start()

    def finish(self, ts):
        for t in ts:
            for j, chip in enumerate(self.chips):
                other = self._half(t, chip, 1 - self.c)
                self._copy(6 * t + 3 + j, other, other, self.sib).wait_recv()
        for t in ts:
            for cp in self._sends(t) + self._forwards(t):
                cp.wait_send()


class _SiblingExchangePlan:
    def __init__(self, shapes, grad_refs, out_refs, send_sems, recv_sems):
        self.shapes, self.grad_refs, self.out_refs = shapes, grad_refs, out_refs
        self.send_sems, self.recv_sems = send_sems, recv_sems
        self.x, self.y, self.c, _ = _mesh_pos()

    def _copies(self):
        out = []
        for t, (g, o) in enumerate(zip(self.grad_refs, self.out_refs)):
            rows = self.shapes[t][1] // 2
            out.append(pltpu.make_async_remote_copy(
                src_ref=g.at[:, pl.ds((1 - self.c) * rows, rows), :], dst_ref=o, send_sem=self.send_sems.at[t],
                recv_sem=self.recv_sems.at[t], device_id=(self.x, self.y, 1 - self.c), device_id_type=MESH))
        return out

    def start(self):
        for cp in self._copies():
            cp.start()

    def finish(self):
        for cp in self._copies():
            cp.wait()


class _ChipExchangePlan:
    def __init__(self, part_refs, out_refs, send_sems, recv_sems):
        self.part_refs, self.out_refs, self.send_sems, self.recv_sems = part_refs, out_refs, send_sems, recv_sems
        _, _, self.c, self.chips = _mesh_pos()

    def _copies(self):
        return [pltpu.make_async_remote_copy(
            src_ref=p.at[2 * chip[0] + chip[1]], dst_ref=o.at[j], send_sem=self.send_sems.at[3 * t + j],
            recv_sem=self.recv_sems.at[3 * t + j], device_id=(*chip, self.c), device_id_type=MESH)
            for t, (p, o) in enumerate(zip(self.part_refs, self.out_refs)) for j, chip in enumerate(self.chips)]

    def start(self):
        for cp in self._copies():
            cp.start()

    def finish(self):
        for cp in self._copies():
            cp.wait()


def _sem_pair(n):
    return [pltpu.SemaphoreType.DMA((n,)), pltpu.SemaphoreType.DMA((n,))]


def _mm(a, b, *, dims, tm, tn, tk, out_dtype, name):
    if dims == "nn":
        m, k = a.shape
        n = b.shape[1]
        a_spec = pl.BlockSpec((tm, tk), lambda i, j, kk: (i, kk))
        b_spec = pl.BlockSpec((tk, tn), lambda i, j, kk: (kk, j))
        dot = _dot
    elif dims == "nt":
        m, k = a.shape
        n = b.shape[0]
        a_spec = pl.BlockSpec((tm, tk), lambda i, j, kk: (i, kk))
        b_spec = pl.BlockSpec((tn, tk), lambda i, j, kk: (j, kk))
        dot = _dot_nt
    else:
        k, m = a.shape
        n = b.shape[1]
        a_spec = pl.BlockSpec((tk, tm), lambda i, j, kk: (kk, i))
        b_spec = pl.BlockSpec((tk, tn), lambda i, j, kk: (kk, j))
        dot = _dot_tn
    assert m % tm == 0 and n % tn == 0 and k % tk == 0, (name, m, n, k)
    grid = (m // tm, n // tn, k // tk)
    nk = grid[2]
    assert nk == 1 or out_dtype == F32, name

    def body(a_ref, b_ref, o_ref):
        prod = dot(a_ref[...].astype(BF16), b_ref[...].astype(BF16))
        if nk == 1:
            o_ref[...] = prod.astype(out_dtype)
        else:
            kk = pl.program_id(2)

            @pl.when(kk == 0)
            def _():
                o_ref[...] = prod

            @pl.when(kk > 0)
            def _():
                o_ref[...] += prod

    return pl.pallas_call(
        body, grid=grid, in_specs=[a_spec, b_spec],
        out_specs=pl.BlockSpec((tm, tn), lambda i, j, kk: (i, j)),
        out_shape=jax.ShapeDtypeStruct((m, n), out_dtype),
        compiler_params=_params(("parallel", "parallel", "arbitrary")), name=name,
    )(a, b)


def _fused_rows(name, tm, mats, rows, vecs, fn, row_outs, acc_outs, exchange=()):
    m = mats[0][0].shape[0]
    nm, nr, nv, nro, nao, nx = len(mats), len(rows), len(vecs), len(row_outs), len(acc_outs), len(exchange)
    n_steps = m // tm

    def body(*refs):
        a_refs, w_refs = refs[:nm], refs[nm:2 * nm]
        pos = 2 * nm
        row_refs, vec_refs, part_refs = refs[pos:pos + nr], refs[pos + nr:pos + nr + nv], refs[pos + nr + nv:pos + nr + nv + nx]
        pos += nr + nv + nx
        out_refs, acc_refs, recv_refs = refs[pos:pos + nro], refs[pos + nro:pos + nro + nao], refs[pos + nro + nao:pos + nro + nao + nx]
        sems = refs[pos + nro + nao + nx:]
        i = pl.program_id(0)
        if nx:
            plan = _ChipExchangePlan(part_refs, recv_refs, *sems)

            @pl.when(i == 0)
            def _():
                plan.start()

        @pl.when(i == 0)
        def _():
            for r in acc_refs:
                r[...] = jnp.zeros_like(r)

        y = None
        for a_ref, w_ref, (_, _, dims, sl) in zip(a_refs, w_refs, mats):
            w = w_ref[...] if sl is None else w_ref[sl, :]
            part = (_dot if dims == "nn" else _dot_nt)(a_ref[...], w)
            y = part if y is None else y + part
        res = fn(y, *[r[...] for r in row_refs], *[v[...] for v in vec_refs])
        for r, val in zip(out_refs, res[:nro]):
            r[...] = val.astype(r.dtype)
        for r, val in zip(acc_refs, res[nro:]):
            r[...] += val

        if nx:
            @pl.when(i == n_steps - 1)
            def _():
                plan.finish()

    tile = lambda width: pl.BlockSpec((tm, width), lambda i: (i, 0))
    res = pl.pallas_call(
        body, grid=(n_steps,),
        in_specs=[tile(a.shape[1]) for a, _, _, _ in mats] + [_full_spec(w.shape) for _, w, _, _ in mats]
        + [tile(D_MODEL)] * nr + [_full_spec((1, D_MODEL))] * nv + [ANY] * nx,
        out_specs=[tile(D_MODEL)] * nro + [_full_spec(s) for s in acc_outs] + [ANY] * nx,
        out_shape=[jax.ShapeDtypeStruct((m, D_MODEL), dt) for dt in row_outs]
        + [jax.ShapeDtypeStruct(s, F32) for s in acc_outs]
        + [jax.ShapeDtypeStruct((3,) + p.shape[1:], p.dtype) for p in exchange],
        scratch_shapes=_sem_pair(3 * nx) if nx else [],
        compiler_params=_params(("arbitrary",)), name=name,
    )(*[a for a, _, _, _ in mats], *[w for _, w, _, _ in mats], *rows, *vecs, *exchange)
    return list(res[:nro + nao]), list(res[nro + nao:])


ROW_TILE = 512


def _vec_spec(width=D_MODEL):
    return pl.BlockSpec((1, width), lambda i: (0, 0))


def _rstd(v):
    return lax.rsqrt(jnp.mean(v * v, axis=-1, keepdims=True) + NORM_EPS)


def _mid_fwd_rows(y1, x0, g2, g3):
    x1 = x0 + y1 * _rstd(y1) * g2
    return y1, x1, x1 * _rstd(x1) * g3


def _rms_bwd_rows(dout, v, g):
    r = _rstd(v)
    n = v * r
    dn = dout * g
    dv = r * (dn - n * jnp.mean(dn * n, axis=-1, keepdims=True))
    dg = jnp.sum(dout * n, axis=0, keepdims=True)
    return dv, dg


def _loss_head_rows(y2, x1, tgt, g4):
    x2 = x1 + y2 * _rstd(y2) * g4
    err = x2 - tgt
    loss = 0.5 * jnp.sum(jnp.mean(err * err, axis=-1, keepdims=True), axis=0, keepdims=True)
    dx2 = err * (1.0 / D_MODEL)
    dy2, dg4 = _rms_bwd_rows(dx2, y2, g4)
    return dx2, dy2, dg4, loss


def _mid_bwd_rows(dh2, x1, y1, dx2, g2, g3):
    d3, dg3 = _rms_bwd_rows(dh2, x1, g3)
    dx1 = dx2 + d3
    dy1, dg2 = _rms_bwd_rows(dx1, y1, g2)
    return dx1, dy1, dg2, dg3


def _in_bwd_rows(dh1, x0, dx1, g1):
    d1, dg1 = _rms_bwd_rows(dh1, x0, g1)
    return dx1 + d1, dg1


GATE_ROWS = 512


def _group_mean_matrix():
    p = np.zeros((A_WIDTH, A_WIDTH), np.float32)
    for g in range(A_GROUPS):
        p[g * HEAD_DIM:(g + 1) * HEAD_DIM, g * HEAD_DIM:(g + 1) * HEAD_DIM] = 1.0 / HEAD_DIM
    return jnp.asarray(p)


def _group_masks(width=A_WIDTH):
    lane = lax.broadcasted_iota(jnp.int32, (1, width), 1)
    return [(lane >= g * HEAD_DIM) & (lane < (g + 1) * HEAD_DIM) for g in range(width // HEAD_DIM)]


def _layernorm_groups(vg, pavg):
    hi = lax.Precision.HIGHEST
    mu = _dot(vg, pavg, hi)
    xc = vg - mu
    var = _dot(xc * xc, pavg, hi)
    rstd = lax.rsqrt(var + NORM_EPS)
    return xc * rstd, rstd


def _spatial_mix(w_bf, vn_chunk_bf, masks, bz):
    z = bz
    for g in range(A_GROUPS):
        z = z + jnp.where(masks[g], _dot(w_bf[g], vn_chunk_bf), 0.0)
    return z


def _full_spec(shape):
    return pl.BlockSpec(shape, lambda i: tuple(0 for _ in shape))


def _gate_fwd(uv, ln_g, ln_b, w_s, bz):
    m = uv.shape[0]
    pavg = _group_mean_matrix()

    def body(u_ref, v_ref, lg_ref, lb_ref, w_ref, bz_ref, p_ref, a_ref):
        masks = _group_masks()
        row = lax.broadcasted_iota(jnp.int32, (CHUNK, CHUNK), 0)
        col = lax.broadcasted_iota(jnp.int32, (CHUNK, CHUNK), 1)
        w_bf = [jnp.where(row >= col, w_ref[g], 0.0).astype(BF16) for g in range(A_GROUPS)]
        ug = _gelu(u_ref[...])
        vhat, _ = _layernorm_groups(_gelu(v_ref[...]), p_ref[...])
        vn = vhat * lg_ref[...] + lb_ref[...]
        bz = bz_ref[...]
        for c in range(GATE_ROWS // CHUNK):
            sl = slice(c * CHUNK, (c + 1) * CHUNK)
            z = _spatial_mix(w_bf, vn[sl].astype(BF16), masks, bz)
            a_ref[sl, :] = (ug[sl] * z).astype(BF16)

    return pl.pallas_call(
        body, grid=(m // GATE_ROWS,),
        in_specs=[pl.BlockSpec((GATE_ROWS, A_WIDTH), lambda i: (i, 0)),
                  pl.BlockSpec((GATE_ROWS, A_WIDTH), lambda i: (i, 1)),
                  _full_spec((1, A_WIDTH)), _full_spec((1, A_WIDTH)), _full_spec((A_GROUPS, CHUNK, CHUNK)),
                  _full_spec((CHUNK, A_WIDTH)), _full_spec((A_WIDTH, A_WIDTH))],
        out_specs=pl.BlockSpec((GATE_ROWS, A_WIDTH), lambda i: (i, 0)),
        out_shape=jax.ShapeDtypeStruct((m, A_WIDTH), BF16),
        compiler_params=_params(("parallel",)), name="gate_fwd",
    )(uv, uv, ln_g, ln_b, w_s, bz, pavg)


def _gate_bwd(uv, dmix, ln_g, ln_b, w_s, w_st, bz, grads):
    m = uv.shape[0]
    pavg = _group_mean_matrix()
    nsteps = m // GATE_ROWS
    nx = len(grads)
    shapes = [g.shape for g in grads]

    def body(u_ref, v_ref, da_ref, lg_ref, lb_ref, w_ref, wt_ref, bz_ref, p_ref, *rest):
        grad_refs = rest[:nx]
        duv_ref, dlg_ref, dlb_ref, dw_ref, dbz_ref = rest[nx:nx + 5]
        recv_refs = rest[nx + 5:2 * nx + 5]
        exchange = _SiblingExchangePlan(shapes, grad_refs, recv_refs, *rest[2 * nx + 5:])
        i = pl.program_id(0)

        @pl.when(i == 0)
        def _():
            exchange.start()
            dlg_ref[...] = jnp.zeros_like(dlg_ref)
            dlb_ref[...] = jnp.zeros_like(dlb_ref)
            dw_ref[...] = jnp.zeros_like(dw_ref)
            dbz_ref[...] = jnp.zeros_like(dbz_ref)

        hi = lax.Precision.HIGHEST
        masks = _group_masks()
        row = lax.broadcasted_iota(jnp.int32, (CHUNK, CHUNK), 0)
        col = lax.broadcasted_iota(jnp.int32, (CHUNK, CHUNK), 1)
        tril = row >= col
        w_bf = [jnp.where(tril, w_ref[g], 0.0).astype(BF16) for g in range(A_GROUPS)]
        wt_bf = [jnp.where(col >= row, wt_ref[g], 0.0).astype(BF16) for g in range(A_GROUPS)]
        pavg_v = p_ref[...]
        lg = lg_ref[...]
        ug, dug = _gelu_and_grad(u_ref[...])
        vg, dvg_dx = _gelu_and_grad(v_ref[...])
        vhat, rstd = _layernorm_groups(vg, pavg_v)
        vn = vhat * lg + lb_ref[...]
        da = da_ref[...]
        bz = bz_ref[...]
        for c in range(GATE_ROWS // CHUNK):
            sl = slice(c * CHUNK, (c + 1) * CHUNK)
            vn_bf = vn[sl].astype(BF16)
            z = _spatial_mix(w_bf, vn_bf, masks, bz)
            dz = da[sl] * ug[sl]
            duv_ref[sl, 0:A_WIDTH] = da[sl] * z * dug[sl]
            dbz_ref[...] += dz
            dz_bf = dz.astype(BF16)
            dvn = jnp.zeros((CHUNK, A_WIDTH), F32)
            for g in range(A_GROUPS):
                dz_g = jnp.where(masks[g], dz, 0.0).astype(BF16)
                dw_ref[g] += jnp.where(tril, _dot_nt(dz_g, vn_bf), 0.0)
                dvn = dvn + jnp.where(masks[g], _dot(wt_bf[g], dz_bf), 0.0)
            vh = vhat[sl]
            dlb_ref[...] += jnp.sum(dvn, axis=0, keepdims=True)
            dlg_ref[...] += jnp.sum(dvn * vh, axis=0, keepdims=True)
            dvh = dvn * lg
            m1 = _dot(dvh, pavg_v, hi)
            m2 = _dot(dvh * vh, pavg_v, hi)
            duv_ref[sl, A_WIDTH:2 * A_WIDTH] = rstd[sl] * (dvh - m1 - vh * m2) * dvg_dx[sl]

        @pl.when(i == nsteps - 1)
        def _():
            dbz_ref[...] = _dot(dbz_ref[...], pavg_v * float(HEAD_DIM), hi)
            exchange.finish()

    res = pl.pallas_call(
        body, grid=(nsteps,),
        in_specs=[pl.BlockSpec((GATE_ROWS, A_WIDTH), lambda i: (i, 0)),
                  pl.BlockSpec((GATE_ROWS, A_WIDTH), lambda i: (i, 1)),
                  pl.BlockSpec((GATE_ROWS, A_WIDTH), lambda i: (i, 0)),
                  _full_spec((1, A_WIDTH)), _full_spec((1, A_WIDTH)), _full_spec((A_GROUPS, CHUNK, CHUNK)),
                  _full_spec((A_GROUPS, CHUNK, CHUNK)), _full_spec((CHUNK, A_WIDTH)),
                  _full_spec((A_WIDTH, A_WIDTH))] + [ANY] * nx,
        out_specs=[pl.BlockSpec((GATE_ROWS, 2 * A_WIDTH), lambda i: (i, 0)),
                   _full_spec((1, A_WIDTH)), _full_spec((1, A_WIDTH)), _full_spec((A_GROUPS, CHUNK, CHUNK)),
                   _full_spec((CHUNK, A_WIDTH))] + [ANY] * nx,
        out_shape=[jax.ShapeDtypeStruct((m, 2 * A_WIDTH), F32),
                   jax.ShapeDtypeStruct((1, A_WIDTH), F32), jax.ShapeDtypeStruct((1, A_WIDTH), F32),
                   jax.ShapeDtypeStruct((A_GROUPS, CHUNK, CHUNK), F32),
                   jax.ShapeDtypeStruct((CHUNK, A_WIDTH), F32)]
        + [jax.ShapeDtypeStruct((N_SHARD, s[1] // 2, s[2]), F32) for s in shapes],
        scratch_shapes=_sem_pair(nx),
        compiler_params=_params(("arbitrary",)), name="gate_bwd",
    )(uv, uv, dmix, ln_g, ln_b, w_s, w_st, bz, pavg, *grads)
    return res[:5], list(res[5:])


Q_BLOCK = 128
PAIR = 2 * HEAD_DIM
N_PAIR = B_HEADS // 2
N_CFG = len(DILATED)
BLOCKS_PER_CFG = SEQ // Q_BLOCK
QKV_SLABS = 3 * N_PAIR
FWD_BLOCKS_PER_TRIP = 8
BWD_BLOCKS_PER_TRIP = 4


def _t5_bucket_np(dist, dtype):
    max_exact = NUM_BUCKETS // 2
    d = np.maximum(dist, 1).astype(dtype)
    large = max_exact + (np.log(d / dtype(max_exact)) / dtype(math.log(MAX_DISTANCE / max_exact))
                         * dtype(NUM_BUCKETS - max_exact))
    large = np.minimum(large.astype(np.int32), NUM_BUCKETS - 1)
    return np.where(dist < max_exact, dist, large)


def _bucket_tables():
    i = np.arange(Q_BLOCK)[:, None]
    j = np.arange(Q_BLOCK)[None, :]
    tables = []
    for _, dil in DILATED:
        rel_prev = Q_BLOCK + i - j
        rel_cur = i - j
        rel = np.concatenate([rel_prev, rel_cur], axis=1)
        valid = np.concatenate([rel_prev <= Q_BLOCK, rel_cur >= 0], axis=1)
        dist = np.maximum(rel, 0) * dil
        b32 = _t5_bucket_np(dist, np.float32)
        b64 = _t5_bucket_np(dist, np.float64)
        assert np.array_equal(b32, b64)
        tables.append(np.where(valid, b32, -1).astype(np.int32))
    return np.stack(tables)


def _present_buckets(buckets_np):
    return [sorted(set(int(v) for v in np.unique(buckets_np[c]) if v >= 0)) for c in range(N_CFG)]


def _bias_tables_body(buckets_np):
    present = _present_buckets(buckets_np)

    def tables(rb_ref, bk_ref, o_ref, ot_ref):
        for c in range(N_CFG):
            bk = bk_ref[c]
            for h in range(B_HEADS):
                acc = jnp.full((Q_BLOCK, 2 * Q_BLOCK), NEG_INF, F32)
                for b in present[c]:
                    acc = jnp.where(bk == b, rb_ref[b, h], acc)
                o_ref[c, h] = acc
                ot_ref[c, h] = acc.T

    return tables


def _proj_fwd(x, g1, w_in_t):
    m = x.shape[0]
    tm = ROW_TILE

    def body(x_ref, g_ref, w_ref, h_ref, uv_ref, qkv_ref):
        xv = x_ref[...]
        h = (xv * _rstd(xv) * g_ref[...]).astype(BF16)
        h_ref[...] = h
        acc = _dot_nt(h, w_ref[...])
        uv_ref[...] = acc[:, :2 * A_WIDTH]
        for s in range(QKV_SLABS):
            qkv_ref[s] = acc[:, 2 * A_WIDTH + s * PAIR:2 * A_WIDTH + (s + 1) * PAIR]

    return pl.pallas_call(
        body, grid=(m // tm,),
        in_specs=[pl.BlockSpec((tm, D_MODEL), lambda i: (i, 0)), _vec_spec(),
                  pl.BlockSpec((IN_COLS, D_MODEL), lambda i: (0, 0))],
        out_specs=[pl.BlockSpec((tm, D_MODEL), lambda i: (i, 0)),
                   pl.BlockSpec((tm, 2 * A_WIDTH), lambda i: (i, 0)),
                   pl.BlockSpec((QKV_SLABS, tm, PAIR), lambda i: (0, i, 0))],
        out_shape=[jax.ShapeDtypeStruct((m, D_MODEL), BF16), jax.ShapeDtypeStruct((m, 2 * A_WIDTH), F32),
                   jax.ShapeDtypeStruct((QKV_SLABS, m, PAIR), F32)],
        compiler_params=_params(("parallel",)), name="proj_fwd",
    )(x, g1, w_in_t)


def _pair_masks():
    lane = lax.broadcasted_iota(jnp.int32, (1, PAIR), 1)
    return [lane < HEAD_DIM, lane >= HEAD_DIM]


def _block_rows(idx, dil):
    static = isinstance(idx, int)
    r, n = idx % dil, idx // dil

    def rows_of(block):
        start = r + (dil * Q_BLOCK) * block
        if dil == 1:
            return pl.ds(start if static else pl.multiple_of(start, Q_BLOCK), Q_BLOCK)
        return pl.ds(start, Q_BLOCK, stride=dil)

    prev = rows_of(n - 1) if not static or n > 0 else None
    return rows_of(n), prev


def _attn_fwd(qkv, bias, batch, shards):
    m = qkv.shape[1]
    comb_rows = 256
    nt = len(shards)
    shapes = [sh.shape for sh in shards]
    n_steps = batch * N_PAIR
    early, late = list(range(nt // 2)), list(range(nt // 2, nt))

    def body(q_ref, k_ref, v_ref, b_ref, *rest):
        shard_refs = rest[:nt]
        o_ref, l_ref = rest[nt:nt + 2]
        gat_refs = rest[nt + 2:2 * nt + 2]
        scratch = rest[2 * nt + 2:]
        oc_refs, lc_refs = scratch[:N_CFG], scratch[N_CFG:2 * N_CFG]
        step = pl.program_id(0) * N_PAIR + pl.program_id(1)
        gather = _GatherPlan(shapes, shard_refs, gat_refs, *scratch[2 * N_CFG:])

        @pl.when(step == 0)
        def _():
            gather.start(early + late)

        @pl.when(step == n_steps // 2)
        def _():
            gather.forward(early)

        @pl.when(step == n_steps - 2)
        def _():
            gather.forward(late)

        masks = _pair_masks()
        for ci, (_, dil) in enumerate(DILATED):
            nb = SEQ // dil // Q_BLOCK

            def block(trip, ci=ci, dil=dil, nb=nb):
                work = []
                for u in range(FWD_BLOCKS_PER_TRIP):
                    rows, prow = _block_rows(trip * FWD_BLOCKS_PER_TRIP + u, dil)
                    has_prev = nb > 1 and prow is not None
                    q = q_ref[rows, :] * 0.125
                    kc = k_ref[rows, :].astype(BF16)
                    vc = v_ref[rows, :]
                    kp = k_ref[prow, :].astype(BF16) if has_prev else None
                    vp = v_ref[prow, :] if has_prev else None
                    tiles = []
                    for h in range(2):
                        qh = jnp.where(masks[h], q, 0.0).astype(BF16)
                        sc = _dot_nt(qh, kc) + b_ref[ci, h, :, Q_BLOCK:]
                        sp = _dot_nt(qh, kp) + b_ref[ci, h, :, :Q_BLOCK] if has_prev else None
                        tiles.append((sc, sp))
                    work.append((rows, vc, vp, tiles))
                probs = []
                for _, _, _, tiles in work:
                    ps = []
                    for sc, sp in tiles:
                        mx = jnp.max(sc if sp is None else jnp.maximum(sc, sp), axis=1, keepdims=True)
                        pc = jnp.exp(sc - mx).astype(BF16)
                        pp = None if sp is None else jnp.exp(sp - mx).astype(BF16)
                        ps.append((mx, pc, pp))
                    probs.append(ps)
                for (rows, vc, vp, _), ps in zip(work, probs):
                    res = []
                    for h, (_, pc, pp) in enumerate(ps):
                        r = _dot(pc, jnp.where(masks[h], vc, 1.0).astype(BF16))
                        if pp is not None:
                            r = r + _dot(pp, jnp.where(masks[h], vp, 1.0).astype(BF16))
                        res.append(r)
                    num = jnp.where(masks[0], res[0], res[1])
                    den = pltpu.roll(jnp.where(masks[0], res[1], res[0]), HEAD_DIM, 1)
                    oc_refs[ci][rows, :] = num / den
                    lc_refs[ci][rows, :] = jnp.where(masks[0], ps[0][0], ps[1][0]) + jnp.log(den)

            for trip in range(BLOCKS_PER_CFG // FWD_BLOCKS_PER_TRIP):
                block(trip)

        def combine(i, carry):
            rr = pl.ds(pl.multiple_of(i * comb_rows, comb_rows), comb_rows)
            ls = [lc_refs[c][rr, :] for c in range(N_CFG)]
            mx = functools.reduce(jnp.maximum, ls)
            ws = [jnp.exp(l - mx) for l in ls]
            tot = functools.reduce(lambda a, b: a + b, ws)
            o = functools.reduce(lambda a, b: a + b, [ws[c] * oc_refs[c][rr, :] for c in range(N_CFG)]) / tot
            o_ref[rr, :] = o.astype(BF16)
            l_ref[rr, :] = mx + jnp.log(tot)
            return carry

        lax.fori_loop(0, SEQ // comb_rows, combine, 0)

        @pl.when(step == n_steps - 1)
        def _():
            gather.finish(early + late)

    def slab(first):
        return pl.BlockSpec((None, SEQ, PAIR), lambda b, p: (first + p, b, 0))

    nat = pl.BlockSpec((SEQ, PAIR), lambda b, p: (b, p))
    res = pl.pallas_call(
        body, grid=(batch, N_PAIR),
        in_specs=[slab(0), slab(N_PAIR), slab(2 * N_PAIR),
                  pl.BlockSpec((N_CFG, 2, Q_BLOCK, 2 * Q_BLOCK), lambda b, p: (0, p, 0, 0))] + [ANY] * nt,
        out_specs=[nat, nat] + [ANY] * nt,
        out_shape=[jax.ShapeDtypeStruct((m, B_WIDTH), BF16), jax.ShapeDtypeStruct((m, B_WIDTH), F32)]
        + [jax.ShapeDtypeStruct((N_SHARD,) + sh.shape, sh.dtype) for sh in shards],
        scratch_shapes=[pltpu.VMEM((SEQ, PAIR), F32)] * (2 * N_CFG) + _sem_pair(6 * nt),
        compiler_params=_params(("arbitrary", "arbitrary")), name="attn_fwd",
    )(qkv, qkv, qkv, bias, *shards)
    return res[0], res[1], list(res[2:])


def _attn_bwd(qkv, dmix, o, lse, bias_t, batch, parts):
    m = qkv.shape[1]
    nt = len(parts)
    n_steps = N_PAIR * batch

    def body(q_ref, k_ref, v_ref, do_ref, o_ref, l_ref, b_ref, *rest):
        part_refs = rest[:nt]
        dqkv_ref, ds_ref = rest[nt:nt + 2]
        recv_refs = rest[nt + 2:2 * nt + 2]
        dq_acc, dk_acc, dv_acc, d_scr, send_sems, recv_sems = rest[2 * nt + 2:]
        step = pl.program_id(0) * batch + pl.program_id(1)
        exchange = _ChipExchangePlan(part_refs, recv_refs, send_sems, recv_sems)

        @pl.when(step == 0)
        def _():
            exchange.start()

        @pl.when(pl.program_id(1) == 0)
        def _():
            ds_ref[...] = jnp.zeros_like(ds_ref)

        dq_acc[...] = jnp.zeros_like(dq_acc)
        dk_acc[...] = jnp.zeros_like(dk_acc)
        dv_acc[...] = jnp.zeros_like(dv_acc)
        d_scr[...] = do_ref[...] * o_ref[...].astype(F32)
        masks = _pair_masks()

        def stack_heads(t):
            return jnp.concatenate([jnp.where(masks[0], t, 0.0), jnp.where(masks[1], t, 0.0)], axis=0).astype(BF16)

        for ci, (_, dil) in enumerate(DILATED):
            nb = SEQ // dil // Q_BLOCK

            def block(trip, carry, ci=ci, dil=dil, nb=nb):
                first = []
                for u in range(BWD_BLOCKS_PER_TRIP):
                    rows, prow = _block_rows(trip * BWD_BLOCKS_PER_TRIP + u, dil)
                    has_prev = nb > 1 and prow is not None
                    if has_prev:
                        kcat = jnp.concatenate([k_ref[prow, :], k_ref[rows, :]], axis=0).astype(BF16)
                        vcat = jnp.concatenate([v_ref[prow, :], v_ref[rows, :]], axis=0).astype(BF16)
                    else:
                        kcat = k_ref[rows, :].astype(BF16)
                        vcat = v_ref[rows, :].astype(BF16)
                    qst = stack_heads(q_ref[rows, :] * 0.125)
                    dost = stack_heads(do_ref[rows, :])
                    lt = l_ref[rows, :].T
                    dt = d_scr[rows, :].T
                    lrow = jnp.concatenate([lt[0:1], lt[HEAD_DIM:HEAD_DIM + 1]], axis=1)
                    drow = jnp.concatenate([jnp.sum(dt[:HEAD_DIM], axis=0, keepdims=True),
                                            jnp.sum(dt[HEAD_DIM:], axis=0, keepdims=True)], axis=1)
                    first.append((has_prev, rows, prow, kcat, qst, dost, lrow, drow,
                                  _dot_nt(kcat, qst), _dot_nt(vcat, dost)))
                second = []
                for has_prev, rows, prow, kcat, qst, dost, lrow, drow, st, dpt in first:
                    keys = slice(0, 2 * Q_BLOCK) if has_prev else slice(Q_BLOCK, 2 * Q_BLOCK)
                    bt = jnp.concatenate([b_ref[ci, 0, keys, :], b_ref[ci, 1, keys, :]], axis=1)
                    pt = jnp.exp(st + bt - lrow)
                    dst = pt * (dpt - drow)
                    ds_ref[ci, 0, keys, :] += dst[:, :Q_BLOCK]
                    ds_ref[ci, 1, keys, :] += dst[:, Q_BLOCK:]
                    second.append((has_prev, rows, prow, kcat, qst, dost, pt.astype(BF16), dst.astype(BF16)))
                for has_prev, rows, prow, kcat, qst, dost, pt_bf, dst_bf in second:
                    dk = _dot(dst_bf, qst)
                    dv = _dot(pt_bf, dost)
                    dq2 = _dot_tn(dst_bf, kcat)
                    dq_acc[rows, :] += jnp.where(masks[0], dq2[:Q_BLOCK], dq2[Q_BLOCK:]) * 0.125
                    if has_prev:
                        dk_acc[prow, :] += dk[:Q_BLOCK]
                        dv_acc[prow, :] += dv[:Q_BLOCK]
                        dk_acc[rows, :] += dk[Q_BLOCK:]
                        dv_acc[rows, :] += dv[Q_BLOCK:]
                    else:
                        dk_acc[rows, :] += dk
                        dv_acc[rows, :] += dv
                return carry

            block(0, 0)
            lax.fori_loop(1, BLOCKS_PER_CFG // BWD_BLOCKS_PER_TRIP, block, 0)

        dqkv_ref[0] = dq_acc[...].astype(BF16)
        dqkv_ref[1] = dk_acc[...].astype(BF16)
        dqkv_ref[2] = dv_acc[...].astype(BF16)

        @pl.when(step == n_steps - 1)
        def _():
            exchange.finish()

    def slab(first):
        return pl.BlockSpec((None, SEQ, PAIR), lambda p, b: (first + p, b, 0))

    nat = pl.BlockSpec((SEQ, PAIR), lambda p, b: (b, p))
    tbl = pl.BlockSpec((N_CFG, 2, 2 * Q_BLOCK, Q_BLOCK), lambda p, b: (0, p, 0, 0))
    acc = pltpu.VMEM((SEQ, PAIR), F32)
    res = pl.pallas_call(
        body, grid=(N_PAIR, batch),
        in_specs=[slab(0), slab(N_PAIR), slab(2 * N_PAIR),
                  pl.BlockSpec((SEQ, PAIR), lambda p, b: (b, A_WIDTH // PAIR + p)), nat, nat, tbl] + [ANY] * nt,
        out_specs=[pl.BlockSpec((3, SEQ, PAIR), lambda p, b: (0, b, p)), tbl] + [ANY] * nt,
        out_shape=[jax.ShapeDtypeStruct((3, m, B_WIDTH), BF16),
                   jax.ShapeDtypeStruct((N_CFG, B_HEADS, 2 * Q_BLOCK, Q_BLOCK), F32)]
        + [jax.ShapeDtypeStruct((3,) + p.shape[1:], p.dtype) for p in parts],
        scratch_shapes=[acc, acc, acc, acc] + _sem_pair(3 * nt),
        compiler_params=_params(("arbitrary", "arbitrary")), name="attn_bwd",
    )(qkv, qkv, qkv, dmix, o, lse, bias_t, *parts)
    return res[0], res[1], list(res[2:])


def _rel_bias_grad(ds, buckets_np):
    present = _present_buckets(buckets_np)

    def body(bk_ref, ds_ref, o_ref, acc_ref):
        acc_ref[...] = jnp.zeros_like(acc_ref)
        for c in range(N_CFG):
            bk = bk_ref[c]
            for h in range(B_HEADS):
                dsv = ds_ref[c, h]
                for b in present[c]:
                    part = jnp.sum(jnp.where(bk == b, dsv, 0.0), axis=0, keepdims=True)
                    acc_ref[pl.ds(h * NUM_BUCKETS + b, 1), :] += part
        o_ref[...] = jnp.sum(acc_ref[...], axis=1, keepdims=True)

    vm = pl.BlockSpec(memory_space=pltpu.VMEM)
    return pl.pallas_call(
        body, in_specs=[vm, vm], out_specs=vm,
        out_shape=jax.ShapeDtypeStruct((B_HEADS * NUM_BUCKETS, 1), F32),
        scratch_shapes=[pltpu.VMEM((B_HEADS * NUM_BUCKETS, buckets_np.shape[-1]), F32)],
        compiler_params=_params(), name="rel_bias_grad",
    )(jnp.asarray(buckets_np), ds)


def _assemble_dproj(duv, dqkv):
    m = duv.shape[0]
    rows = 1024

    def body(duv_ref, dqkv_ref, o_ref):
        o_ref[:, :2 * A_WIDTH] = duv_ref[...].astype(BF16)
        for k in range(3):
            o_ref[:, 2 * A_WIDTH + k * B_WIDTH:2 * A_WIDTH + (k + 1) * B_WIDTH] = dqkv_ref[k]

    return pl.pallas_call(
        body, grid=(m // rows,),
        in_specs=[pl.BlockSpec((rows, 2 * A_WIDTH), lambda i: (i, 0)),
                  pl.BlockSpec((3, rows, B_WIDTH), lambda i: (0, i, 0))],
        out_specs=pl.BlockSpec((rows, IN_COLS), lambda i: (i, 0)),
        out_shape=jax.ShapeDtypeStruct((m, IN_COLS), BF16),
        compiler_params=_params(("parallel",)), name="assemble_dproj",
    )(duv, dqkv)


def _row_index():
    return lax.broadcasted_iota(jnp.int32, (SEQ, LANE_BLOCK), 0)


def _shift_down(x, k, row):
    return jnp.where(row >= k, pltpu.roll(x, k, 0), 0.0)


def _shift_up(x, k, row):
    return jnp.where(row < SEQ - k, pltpu.roll(x, SEQ - k, 0), 0.0)


def _convgate_fwd(gate, up, conv_w, conv_b, batch, shard):
    m = gate.shape[0]
    n_col = D_FF // LANE_BLOCK
    n_steps = batch * n_col

    def body(g_ref, u_ref, w_ref, b_ref, shard_ref, a_ref, gat_ref, send_sems, recv_sems):
        step = pl.program_id(0) * n_col + pl.program_id(1)
        gather = _GatherPlan([shard.shape], [shard_ref], [gat_ref], send_sems, recv_sems)

        @pl.when(step == 0)
        def _():
            gather.start([0])

        @pl.when(step == (2 * n_steps) // 3)
        def _():
            gather.forward([0])

        g = g_ref[...].astype(F32)
        w = w_ref[...]
        row = _row_index()
        c = b_ref[...] + w[0:1] * _shift_down(g, 2, row) + w[1:2] * _shift_down(g, 1, row) + w[2:3] * g
        a_ref[...] = (_gelu(c) * u_ref[...].astype(F32)).astype(BF16)

        @pl.when(step == n_steps - 1)
        def _():
            gather.finish([0])

    blk = pl.BlockSpec((SEQ, LANE_BLOCK), lambda b, j: (b, j))
    return pl.pallas_call(
        body, grid=(batch, n_col),
        in_specs=[blk, blk, pl.BlockSpec((3, LANE_BLOCK), lambda b, j: (0, j)),
                  pl.BlockSpec((1, LANE_BLOCK), lambda b, j: (0, j)), ANY],
        out_specs=[blk, ANY],
        out_shape=[jax.ShapeDtypeStruct((m, D_FF), BF16),
                   jax.ShapeDtypeStruct((N_SHARD,) + shard.shape, shard.dtype)],
        scratch_shapes=_sem_pair(6),
        compiler_params=_params(("arbitrary", "arbitrary")), name="convgate_fwd",
    )(gate, up, conv_w, conv_b, shard)


def _convgate_bwd(gate, up, dact, conv_w, conv_b, batch):
    m = gate.shape[0]

    def body(g_ref, u_ref, da_ref, w_ref, b_ref, dg_ref, du_ref, dw_ref, db_ref):
        @pl.when(pl.program_id(1) == 0)
        def _():
            dw_ref[...] = jnp.zeros_like(dw_ref)
            db_ref[...] = jnp.zeros_like(db_ref)

        g = g_ref[...].astype(F32)
        w = w_ref[...]
        row = _row_index()
        g1 = _shift_down(g, 1, row)
        g2 = _shift_down(g, 2, row)
        c = b_ref[...] + w[0:1] * g2 + w[1:2] * g1 + w[2:3] * g
        gg, dgg = _gelu_and_grad(c)
        da = da_ref[...].astype(F32)
        du_ref[...] = (da * gg).astype(BF16)
        dc = da * u_ref[...].astype(F32) * dgg
        db_ref[...] += jnp.sum(dc, axis=0, keepdims=True)
        dw_ref[0:1, :] += jnp.sum(dc * g2, axis=0, keepdims=True)
        dw_ref[1:2, :] += jnp.sum(dc * g1, axis=0, keepdims=True)
        dw_ref[2:3, :] += jnp.sum(dc * g, axis=0, keepdims=True)
        dg_ref[...] = (w[2:3] * dc + w[1:2] * _shift_up(dc, 1, row) + w[0:1] * _shift_up(dc, 2, row)).astype(BF16)

    blk = pl.BlockSpec((SEQ, LANE_BLOCK), lambda j, b: (b, j))
    wspec = pl.BlockSpec((3, LANE_BLOCK), lambda j, b: (0, j))
    bspec = pl.BlockSpec((1, LANE_BLOCK), lambda j, b: (0, j))
    return pl.pallas_call(
        body, grid=(D_FF // LANE_BLOCK, batch),
        in_specs=[blk, blk, blk, wspec, bspec],
        out_specs=[blk, blk, wspec, bspec],
        out_shape=[jax.ShapeDtypeStruct((m, D_FF), BF16), jax.ShapeDtypeStruct((m, D_FF), BF16),
                   jax.ShapeDtypeStruct((3, D_FF), F32), jax.ShapeDtypeStruct((1, D_FF), F32)],
        compiler_params=_params(("parallel", "arbitrary")), name="convgate_bwd",
    )(gate, up, dact, conv_w, conv_b)


def _gather_weights(shards, conv_w_shard, rel_bias, buckets_np):
    nt = len(shards)
    shapes = [sh.shape for sh in shards]
    ts = list(range(nt))
    tables = _bias_tables_body(buckets_np)

    def body(*refs):
        shard_refs = refs[:nt]
        cw_ref, rb_ref, bk_ref = refs[nt:nt + 3]
        out_refs = refs[nt + 3:2 * nt + 3]
        cw_out, bias_ref, bias_t_ref = refs[2 * nt + 3:2 * nt + 6]
        send_sems, recv_sems, cw_send, cw_recv = refs[2 * nt + 6:]
        plan = _GatherPlan(shapes, shard_refs, out_refs, send_sems, recv_sems)
        x, y, c, chips = _mesh_pos()

        def cw_copy(j, src, dst, chip):
            return pltpu.make_async_remote_copy(src_ref=src, dst_ref=dst, send_sem=cw_send.at[j],
                                                recv_sem=cw_recv.at[j], device_id=(*chip, c), device_id_type=MESH)

        plan.start(ts)
        cw_sends = [cw_copy(j, cw_ref, cw_out.at[2 * x + y], chip) for j, chip in enumerate(chips)]
        for cp in cw_sends:
            cp.start()
        tables(rb_ref, bk_ref, bias_ref, bias_t_ref)
        plan.forward(ts)
        for j, chip in enumerate(chips):
            dst = cw_out.at[2 * chip[0] + chip[1]]
            cw_copy(j, dst, dst, chip).wait_recv()
        plan.finish(ts)
        for cp in cw_sends:
            cp.wait_send()

    out_shape = [jax.ShapeDtypeStruct((N_SHARD,) + sh.shape, sh.dtype) for sh in shards]
    out_shape.append(jax.ShapeDtypeStruct((N_SHARD,) + conv_w_shard.shape, conv_w_shard.dtype))
    out_shape += [jax.ShapeDtypeStruct((N_CFG, B_HEADS, Q_BLOCK, 2 * Q_BLOCK), F32),
                  jax.ShapeDtypeStruct((N_CFG, B_HEADS, 2 * Q_BLOCK, Q_BLOCK), F32)]
    vm = pl.BlockSpec(memory_space=pltpu.VMEM)
    res = pl.pallas_call(
        body, in_specs=[ANY] * (nt + 1) + [pl.BlockSpec(memory_space=pltpu.SMEM), vm],
        out_specs=[ANY] * (nt + 1) + [vm, vm], out_shape=out_shape,
        scratch_shapes=_sem_pair(6 * nt) + _sem_pair(3),
        compiler_params=pltpu.CompilerParams(has_side_effects=True, vmem_limit_bytes=VMEM_LIMIT),
        name="gather_weights",
    )(*shards, conv_w_shard, rel_bias, jnp.asarray(buckets_np))
    return list(res[:nt + 1]), res[nt + 1], res[nt + 2]


def _exchange_halves(grads, name):
    nt = len(grads)
    shapes = [g.shape for g in grads]

    def body(*refs):
        plan = _SiblingExchangePlan(shapes, refs[:nt], refs[nt:2 * nt], *refs[2 * nt:])
        plan.start()
        plan.finish()

    out_shape = [jax.ShapeDtypeStruct((N_SHARD, g.shape[1] // 2, g.shape[2]), g.dtype) for g in grads]
    return pl.pallas_call(
        body, in_specs=[ANY] * nt, out_specs=[ANY] * nt, out_shape=out_shape,
        scratch_shapes=_sem_pair(nt),
        compiler_params=pltpu.CompilerParams(has_side_effects=True), name=name,
    )(*grads)


def _add_halves(g, recv, c_idx):
    _, rows2, cols = g.shape
    rows = rows2 // 2
    tr = rows // 2 if rows % 16 == 0 and rows >= 256 else rows
    nblk = rows // tr

    def body(c_ref, g_ref, r_ref, o_ref):
        o_ref[...] = (g_ref[...] + r_ref[...]).astype(BF16)

    return pl.pallas_call(
        body,
        grid_spec=pltpu.PrefetchScalarGridSpec(
            num_scalar_prefetch=1, grid=(N_SHARD, nblk),
            in_specs=[pl.BlockSpec((None, tr, cols), lambda s, i, c: (s, c[0] * nblk + i, 0)),
                      pl.BlockSpec((None, tr, cols), lambda s, i, c: (s, i, 0))],
            out_specs=pl.BlockSpec((None, tr, cols), lambda s, i, c: (s, i, 0))),
        out_shape=jax.ShapeDtypeStruct((N_SHARD, rows, cols), BF16),
        compiler_params=_params(("parallel", "parallel")), name="rs_add_halves",
    )(c_idx, g, recv)


def _add_chips(part, recv, s_idx, c_idx):
    _, rows, cols = part.shape
    tr = rows // 2 if rows % 32 == 0 and rows >= 256 else rows
    nblk = rows // tr

    def body(idx_ref, p_ref, r_ref, o_ref):
        acc = p_ref[...].astype(F32)
        for j in range(3):
            acc = acc + r_ref[j].astype(F32)
        o_ref[...] = acc

    return pl.pallas_call(
        body,
        grid_spec=pltpu.PrefetchScalarGridSpec(
            num_scalar_prefetch=1, grid=(nblk,),
            in_specs=[pl.BlockSpec((None, tr, cols), lambda i, idx: (idx[0], i, 0)),
                      pl.BlockSpec((3, tr, cols), lambda i, idx: (0, i, 0))],
            out_specs=pl.BlockSpec((tr, cols), lambda i, idx: (idx[1] * nblk + i, 0))),
        out_shape=jax.ShapeDtypeStruct((2 * rows, cols), F32),
        compiler_params=_params(("parallel",)), name="rs_add_chips",
    )(jnp.concatenate([s_idx, c_idx]), part, recv)


def _share_halves(fulls):
    nt = len(fulls)

    def body(*refs):
        out_refs = refs[nt:2 * nt]
        send_sems, recv_sems = refs[2 * nt:]
        x, y, c, _ = _mesh_pos()
        copies = []
        for t in range(nt):
            rows = fulls[t].shape[0] // 2
            mine = out_refs[t].at[pl.ds(c * rows, rows), :]
            copies.append(pltpu.make_async_remote_copy(
                src_ref=mine, dst_ref=mine, send_sem=send_sems.at[t], recv_sem=recv_sems.at[t],
                device_id=(x, y, 1 - c), device_id_type=MESH))
        for cp in copies:
            cp.start()
        for t in range(nt):
            rows = fulls[t].shape[0] // 2
            theirs = out_refs[t].at[pl.ds((1 - c) * rows, rows), :]
            pltpu.make_async_remote_copy(
                src_ref=theirs, dst_ref=theirs, send_sem=send_sems.at[t], recv_sem=recv_sems.at[t],
                device_id=(x, y, 1 - c), device_id_type=MESH).wait_recv()
        for cp in copies:
            cp.wait_send()

    out_shape = [jax.ShapeDtypeStruct(f.shape, f.dtype) for f in fulls]
    return pl.pallas_call(
        body, in_specs=[ANY] * nt, out_specs=[ANY] * nt, out_shape=out_shape,
        input_output_aliases={t: t for t in range(nt)},
        scratch_shapes=_sem_pair(nt),
        compiler_params=pltpu.CompilerParams(has_side_effects=True), name="rs_share_halves",
    )(*fulls)


def _allreduce_small(arrays):
    n = len(arrays)

    def body(*refs):
        in_refs, out_refs = refs[:n], refs[n:2 * n]
        sib_refs, chip_refs = refs[2 * n:3 * n], refs[3 * n:4 * n]
        send_sems, recv_sems = refs[4 * n:]
        x, y, c, chips = _mesh_pos()

        def copy(k, src, dst, to):
            return pltpu.make_async_remote_copy(src_ref=src, dst_ref=dst, send_sem=send_sems.at[k],
                                                recv_sem=recv_sems.at[k], device_id=to, device_id_type=MESH)

        first = [copy(t, in_refs[t], sib_refs[t], (x, y, 1 - c)) for t in range(n)]
        for cp in first:
            cp.start()
        for cp in first:
            cp.wait()
        for t in range(n):
            out_refs[t][...] = in_refs[t][...] + sib_refs[t][...]
        second = [copy(n + 3 * t + j, out_refs[t], chip_refs[t].at[j], (*chip, c))
                  for t in range(n) for j, chip in enumerate(chips)]
        for cp in second:
            cp.start()
        for cp in second:
            cp.wait()
        for t in range(n):
            out_refs[t][...] = (out_refs[t][...] + chip_refs[t][0]) + (chip_refs[t][1] + chip_refs[t][2])

    vm = pl.BlockSpec(memory_space=pltpu.VMEM)
    return pl.pallas_call(
        body, in_specs=[vm] * n, out_specs=[vm] * n,
        out_shape=[jax.ShapeDtypeStruct(a.shape, F32) for a in arrays],
        scratch_shapes=[pltpu.VMEM(a.shape, F32) for a in arrays] + [pltpu.VMEM((3,) + a.shape, F32) for a in arrays]
        + _sem_pair(4 * n),
        compiler_params=pltpu.CompilerParams(has_side_effects=True, vmem_limit_bytes=VMEM_LIMIT),
        name="allreduce_small",
    )(*arrays)


def _from_col_shards(g):
    n, rows, cols = g.shape
    return g.transpose(1, 0, 2).reshape(rows, n * cols)


def _train_step(x, tgt, g1, g2, g3, g4, shards, ln_g, ln_b, w_s, b_s, rel_bias, conv_w_shard, conv_b, batch,
                s_idx, c_idx):
    big = dict(tm=1024, out_dtype=F32)
    buckets = _bucket_tables()
    bz = jnp.repeat(b_s.T, HEAD_DIM, axis=1)
    w_st = jnp.swapaxes(w_s, 1, 2)

    def with_own(gathered, own):
        return lax.dynamic_update_index_in_dim(gathered, own, s_idx[0], 0)

    def shard_major(g):
        return g.reshape(N_SHARD, g.shape[0] // N_SHARD, D_MODEL)

    (g_in, g_convw), bias, bias_t = _gather_weights([shards["w_in"]], conv_w_shard, rel_bias, buckets)
    w_in_t = with_own(g_in, shards["w_in"]).reshape(IN_COLS, D_MODEL)
    conv_w = _from_col_shards(with_own(g_convw, conv_w_shard))

    h1, uv, qkv = _proj_fwd(x, g1, w_in_t)
    a = _gate_fwd(uv, ln_g, ln_b, w_s, bz)
    later = ["w_out", "w_gate", "w_up"]
    o_bf, lse, gathered = _attn_fwd(qkv, bias, batch, [shards[n] for n in later])
    g_out, g_gate, g_up = [with_own(g, shards[n]) for g, n in zip(gathered, later)]
    w_out = g_out.reshape(D_MODEL, D_MODEL)
    w_gate_t = g_gate.reshape(D_FF, D_MODEL)
    w_up_t = g_up.reshape(D_FF, D_MODEL)
    (y1, x1, h2), _ = _fused_rows(
        "out_proj_mid_fwd", 512,
        [(a, w_out, "nn", slice(0, A_WIDTH)), (o_bf, w_out, "nn", slice(A_WIDTH, D_MODEL))],
        [x], [g2, g3], _mid_fwd_rows, [F32, F32, BF16], [])
    gate = _mm(h2, w_gate_t, dims="nt", tm=1024, tn=1408, tk=1024, out_dtype=BF16, name="mm_gate")
    up = _mm(h2, w_up_t, dims="nt", tm=1024, tn=1408, tk=1024, out_dtype=BF16, name="mm_up")
    act, g_down = _convgate_fwd(gate, up, conv_w, conv_b, batch, shards["w_down"])
    w_down = with_own(g_down, shards["w_down"]).reshape(D_FF, D_MODEL)
    (dx2, dy2, dg4, loss), _ = _fused_rows(
        "down_proj_loss_head", 512, [(act, w_down, "nn", None)], [x1, tgt], [g4], _loss_head_rows,
        [F32, BF16], [(1, D_MODEL), (1, 128)])

    dact = _mm(dy2, w_down, dims="nt", tm=1024, tn=1408, tk=1024, out_dtype=BF16, name="mm_dact")
    dw_down = _mm(act, dy2, dims="tn", tm=1408, tn=1024, tk=1024, out_dtype=F32, name="mm_dw_down")
    dgate, dup, dconv_w, dconv_b = _convgate_bwd(gate, up, dact, conv_w, conv_b, batch)
    (dx1, dy1, dg2, dg3), _ = _fused_rows(
        "dh2_mid_bwd", 256, [(dgate, w_gate_t, "nn", None), (dup, w_up_t, "nn", None)],
        [x1, y1, dx2], [g2, g3], _mid_bwd_rows, [F32, BF16], [(1, D_MODEL), (1, D_MODEL)])
    dw_gate_t = _mm(dgate, h2, dims="tn", tm=1408, tn=1024, tk=1024, out_dtype=F32, name="mm_dw_gate")
    dw_up_t = _mm(dup, h2, dims="tn", tm=1408, tn=1024, tk=1024, out_dtype=F32, name="mm_dw_up")
    dmix = _mm(dy1, w_out, dims="nt", tn=1024, tk=1024, name="mm_dmix", **big)
    dw_out_a = _mm(a, dy1, dims="tn", tm=A_WIDTH, tn=1024, tk=1024, out_dtype=F32, name="mm_dw_out_a")
    dw_out_b = _mm(o_bf, dy1, dims="tn", tm=B_WIDTH, tn=1024, tk=1024, out_dtype=F32, name="mm_dw_out_b")

    dw_out = jnp.concatenate([dw_out_a, dw_out_b], axis=0)
    done = [shard_major(g) for g in (dw_down, dw_gate_t, dw_up_t, dw_out)]
    (duv, dln_g, dln_b, dw_s, dbz), recv_a = _gate_bwd(uv, dmix, ln_g, ln_b, w_s, w_st, bz, done)
    parts = [_add_halves(g, r, c_idx) for g, r in zip(done, recv_a)]
    dqkv, ds, recv = _attn_bwd(qkv, dmix, o_bf, lse, bias_t, batch, parts)
    fulls = [_add_chips(p, r, s_idx, c_idx) for p, r in zip(parts, recv)]
    drel = _rel_bias_grad(ds, np.ascontiguousarray(np.swapaxes(buckets, 1, 2)))
    dproj = _assemble_dproj(duv, dqkv)
    dw_in_t = _mm(dproj, h1, dims="tn", tm=1408, tn=1024, tk=1024, out_dtype=F32, name="mm_dw_in")
    last = [shard_major(dw_in_t)]
    part_in = [_add_halves(g, r, c_idx) for g, r in zip(last, _exchange_halves(last, "rs_sibling_exchange_in"))]
    (dx0, dg1), recv_in = _fused_rows(
        "dh1_in_bwd", 512, [(dproj, w_in_t, "nn", None)], [x, dx1], [g1], _in_bwd_rows,
        [F32], [(1, D_MODEL)], exchange=part_in)
    fulls += [_add_chips(p, r, s_idx, c_idx) for p, r in zip(part_in, recv_in)]
    reduced = dict(zip(["w_down", "w_gate", "w_up", "w_out", "w_in"], _share_halves(fulls)))

    small = dict(
        loss=loss, norm_mix_pre=dg1, norm_mix_post=dg2, norm_ffn_pre=dg3, norm_ffn_post=dg4,
        ln_v_gain=dln_g, ln_v_bias=dln_b, spatial_w=dw_s, spatial_b=dbz, rel_bias=drel,
        conv_w=dconv_w, conv_b=dconv_b,
    )
    return dx0, small, reduced


def _adamw_update(w, g, m, v):
    nm = ADAM_B1 * m + (1.0 - ADAM_B1) * g
    nv = ADAM_B2 * v + (1.0 - ADAM_B2) * (g * g)
    m_hat = nm / (1.0 - ADAM_B1 ** ADAM_STEP)
    v_hat = nv / (1.0 - ADAM_B2 ** ADAM_STEP)
    return -ADAM_LR * (m_hat / (jnp.sqrt(v_hat) + ADAM_EPS) + ADAM_WD * w), nm, nv


def _adamw(w, g, m, v, name):
    rows, cols = w.shape
    tr = next(cand for cand in (256, 176, 128) if rows % cand == 0)

    def body(w_ref, g_ref, m_ref, v_ref, go_ref, d_ref, nm_ref, nv_ref):
        gv = g_ref[...]
        go_ref[...] = gv
        d_ref[...], nm_ref[...], nv_ref[...] = _adamw_update(w_ref[...], gv, m_ref[...], v_ref[...])

    spec = pl.BlockSpec((tr, cols), lambda i: (i, 0))
    sds = jax.ShapeDtypeStruct((rows, cols), F32)
    return pl.pallas_call(
        body, grid=(rows // tr,), in_specs=[spec] * 4, out_specs=[spec] * 4, out_shape=[sds] * 4,
        compiler_params=_params(("parallel",)), name=name,
    )(w, g, m, v)


def _adamw_small(ws, gs, ms, vs):
    n = len(ws)

    def body(*refs):
        w_refs, g_refs, m_refs, v_refs = refs[:n], refs[n:2 * n], refs[2 * n:3 * n], refs[3 * n:4 * n]
        d_refs, nm_refs, nv_refs = refs[4 * n:5 * n], refs[5 * n:6 * n], refs[6 * n:7 * n]
        for t in range(n):
            d_refs[t][...], nm_refs[t][...], nv_refs[t][...] = _adamw_update(
                w_refs[t][...], g_refs[t][...], m_refs[t][...], v_refs[t][...])

    vm = pl.BlockSpec(memory_space=pltpu.VMEM)
    sds = [jax.ShapeDtypeStruct(w.shape, F32) for w in ws]
    res = pl.pallas_call(
        body, in_specs=[vm] * (4 * n), out_specs=[vm] * (3 * n), out_shape=sds * 3,
        compiler_params=_params(), name="adamw_small",
    )(*ws, *gs, *ms, *vs)
    return res[:n], res[n:2 * n], res[2 * n:]


SMALL = ["norm_mix_pre", "norm_mix_post", "norm_ffn_pre", "norm_ffn_post", "ln_v_gain", "ln_v_bias",
         "spatial_w", "spatial_b", "rel_bias", "conv_b"]
LARGE = ["w_in", "w_gate", "w_up", "w_down", "w_out"]
TRANSPOSED = ("w_in", "w_gate", "w_up")
ORDER = ["norm_mix_pre", "norm_mix_post", "norm_ffn_pre", "norm_ffn_post", "w_in", "ln_v_gain", "ln_v_bias",
         "spatial_w", "spatial_b", "rel_bias", "w_out", "w_gate", "w_up", "conv_w", "conv_b", "w_down"]


def kernel(x, norm_mix_pre, norm_mix_post, norm_ffn_pre, norm_ffn_post, w_in, ln_v_gain, ln_v_bias, spatial_w, spatial_b, rel_bias, w_out, w_gate, w_up, conv_w, conv_b, w_down, loss_target, m_norm_mix_pre, m_norm_mix_post, m_norm_ffn_pre, m_norm_ffn_post, m_w_in, m_ln_v_gain, m_ln_v_bias, m_spatial_w, m_spatial_b, m_rel_bias, m_w_out, m_w_gate, m_w_up, m_conv_w, m_conv_b, m_w_down, v_norm_mix_pre, v_norm_mix_post, v_norm_ffn_pre, v_norm_ffn_post, v_w_in, v_ln_v_gain, v_ln_v_bias, v_spatial_w, v_spatial_b, v_rel_bias, v_w_out, v_w_gate, v_w_up, v_conv_w, v_conv_b, v_w_down):
    params = dict(norm_mix_pre=norm_mix_pre, norm_mix_post=norm_mix_post, norm_ffn_pre=norm_ffn_pre,
                  norm_ffn_post=norm_ffn_post, w_in=w_in, ln_v_gain=ln_v_gain, ln_v_bias=ln_v_bias,
                  spatial_w=spatial_w, spatial_b=spatial_b, rel_bias=rel_bias, w_out=w_out, w_gate=w_gate,
                  w_up=w_up, conv_w=conv_w, conv_b=conv_b, w_down=w_down)
    mom = dict(norm_mix_pre=m_norm_mix_pre, norm_mix_post=m_norm_mix_post, norm_ffn_pre=m_norm_ffn_pre,
               norm_ffn_post=m_norm_ffn_post, w_in=m_w_in, ln_v_gain=m_ln_v_gain, ln_v_bias=m_ln_v_bias,
               spatial_w=m_spatial_w, spatial_b=m_spatial_b, rel_bias=m_rel_bias, w_out=m_w_out, w_gate=m_w_gate,
               w_up=m_w_up, conv_w=m_conv_w, conv_b=m_conv_b, w_down=m_w_down)
    var = dict(norm_mix_pre=v_norm_mix_pre, norm_mix_post=v_norm_mix_post, norm_ffn_pre=v_norm_ffn_pre,
               norm_ffn_post=v_norm_ffn_post, w_in=v_w_in, ln_v_gain=v_ln_v_gain, ln_v_bias=v_ln_v_bias,
               spatial_w=v_spatial_w, spatial_b=v_spatial_b, rel_bias=v_rel_bias, w_out=v_w_out, w_gate=v_w_gate,
               w_up=v_w_up, conv_w=v_conv_w, conv_b=v_conv_b, w_down=v_w_down)

    batch = x.shape[0]
    xi, yi, ci = lax.axis_index("x"), lax.axis_index("y"), lax.axis_index("c")
    s_idx = (2 * xi + yi).astype(jnp.int32).reshape(1)
    c_idx = ci.astype(jnp.int32).reshape(1)

    def local(a, n):
        return jnp.swapaxes(a[0], 0, 1) if n in TRANSPOSED else a[0]

    shards = {n: local(params[n], n).astype(BF16) for n in LARGE}
    dx0, partial, reduced = _train_step(
        x.reshape(batch * SEQ, D_MODEL), loss_target.reshape(batch * SEQ, D_MODEL),
        norm_mix_pre, norm_mix_post, norm_ffn_pre, norm_ffn_post, shards,
        ln_v_gain.reshape(1, A_WIDTH), ln_v_bias.reshape(1, A_WIDTH), spatial_w[0], spatial_b[0], rel_bias,
        conv_w[0], conv_b, batch, s_idx, c_idx)
    grad_x = dx0.reshape(batch, SEQ, D_MODEL)

    names = list(partial)
    total = dict(zip(names, _allreduce_small([partial[n] for n in names])))
    loss = total["loss"][0, 0]
    total["spatial_b"] = total["spatial_b"][:, ::HEAD_DIM].T
    total["rel_bias"] = total["rel_bias"].reshape(B_HEADS, NUM_BUCKETS).T
    total["conv_w"] = lax.dynamic_slice_in_dim(total["conv_w"], s_idx[0] * SHARD_FF, SHARD_FF, axis=1)
    small_names = SMALL + ["conv_w"]
    for n in small_names:
        reduced[n] = total[n].reshape(params[n].shape)

    out_g, out_d, out_m, out_v = {}, {}, {}, {}
    for n in LARGE:
        res = _adamw(local(params[n], n), reduced[n], local(mom[n], n), local(var[n], n), name=f"adamw_{n}")
        if n in TRANSPOSED:
            res = [jnp.swapaxes(r, 0, 1) for r in res]
        out_g[n], out_d[n], out_m[n], out_v[n] = [r[None] for r in res]
    d, nm, nv = _adamw_small([params[n] for n in small_names], [reduced[n] for n in small_names],
                             [mom[n] for n in small_names], [var[n] for n in small_names])
    for n, dd, mm, vv in zip(small_names, d, nm, nv):
        out_g[n], out_d[n], out_m[n], out_v[n] = reduced[n], dd, mm, vv

    return (loss, grad_x, *[out_g[n] for n in ORDER], *[out_d[n] for n in ORDER],
            *[out_m[n] for n in ORDER], *[out_v[n] for n in ORDER])
```

```python
import functools
import math

import numpy as np
import jax
import jax.numpy as jnp
from jax import lax
from jax.experimental import pallas as pl
from jax.experimental.pallas import tpu as pltpu

F32 = jnp.float32
BF16 = jnp.bfloat16
MESH = pl.DeviceIdType.MESH

D_MODEL = 1024
SEQ = 2048
HEAD_DIM = 64
A_GROUPS = 4
A_WIDTH = 256
B_HEADS = 12
B_WIDTH = 768
CHUNK = 128
DILATED = ((128, 1), (512, 4), (2048, 16))
NUM_BUCKETS = 32
MAX_DISTANCE = 2048
D_FF = 2816
IN_COLS = 2816
NORM_EPS = 1e-6
NEG_INF = -1e30
N_SHARD = 4
SHARD_FF = D_FF // N_SHARD
LANE_BLOCK = 256
VMEM_LIMIT = 56 * 1024 * 1024

ADAM_LR = 0.001
ADAM_B1 = 0.9
ADAM_B2 = 0.999
ADAM_EPS = 1e-08
ADAM_WD = 0.01
ADAM_STEP = 10

GELU_C = math.sqrt(2.0 / math.pi)
GELU_A = 0.044715

ANY = pl.BlockSpec(memory_space=pl.ANY)


def _params(sem=None):
    return pltpu.CompilerParams(dimension_semantics=sem, vmem_limit_bytes=VMEM_LIMIT)


def _dot(a, b, precision=None):
    return jnp.dot(a, b, preferred_element_type=F32, precision=precision)


def _dot_nt(a, b, precision=None):
    return lax.dot_general(a, b, (((1,), (1,)), ((), ())), preferred_element_type=F32, precision=precision)


def _dot_tn(a, b):
    return lax.dot_general(a, b, (((0,), (0,)), ((), ())), preferred_element_type=F32)


def _gelu(x):
    t = jnp.tanh(x * (GELU_C + (GELU_C * GELU_A) * (x * x)))
    return (0.5 * x) * (1.0 + t)


def _gelu_and_grad(x):
    x2 = x * x
    u = 1.0 + jnp.tanh(x * (GELU_C + (GELU_C * GELU_A) * x2))
    hx = 0.5 * x
    dg = u * (0.5 + hx * (2.0 - u) * (GELU_C + (3.0 * GELU_C * GELU_A) * x2))
    return hx * u, dg


def _mesh_pos():
    x, y, c = lax.axis_index("x"), lax.axis_index("y"), lax.axis_index("c")
    chips = [(1 - x, y), (x, 1 - y), (1 - x, 1 - y)]
    return x, y, c, chips


class _GatherPlan:
    def __init__(self, shapes, shard_refs, out_refs, send_sems, recv_sems):
        self.shapes, self.shard_refs, self.out_refs = shapes, shard_refs, out_refs
        self.send_sems, self.recv_sems = send_sems, recv_sems
        self.x, self.y, self.c, self.chips = _mesh_pos()
        self.sib = (self.x, self.y, 1 - self.c)

    def _half(self, t, chip, which):
        rows = self.shapes[t][0] // 2
        return self.out_refs[t].at[2 * chip[0] + chip[1], pl.ds(which * rows, rows), :]

    def _copy(self, k, src, dst, to):
        return pltpu.make_async_remote_copy(src_ref=src, dst_ref=dst, send_sem=self.send_sems.at[k],
                                            recv_sem=self.recv_sems.at[k], device_id=to, device_id_type=MESH)

    def _sends(self, t):
        rows = self.shapes[t][0] // 2
        src = self.shard_refs[t].at[pl.ds(self.c * rows, rows), :]
        return [self._copy(6 * t + j, src, self._half(t, (self.x, self.y), self.c), (*chip, self.c))
                for j, chip in enumerate(self.chips)]

    def _forwards(self, t):
        return [self._copy(6 * t + 3 + j, self._half(t, chip, self.c), self._half(t, chip, self.c), self.sib)
                for j, chip in enumerate(self.chips)]

    def start(self, ts):
        for t in ts:
            for cp in self._sends(t):
                cp.start()

    def forward(self, ts):
        for t in ts:
            for j, chip in enumerate(self.chips):
                landed = self._half(t, chip, self.c)
                self._copy(6 * t + j, landed, landed, (*chip, self.c)).wait_recv()
            for cp in self._forwards(t):
                cp.start()

    def finish(self, ts):
        for t in ts:
            for j, chip in enumerate(self.chips):
                other = self._half(t, chip, 1 - self.c)
                self._copy(6 * t + 3 + j, other, other, self.sib).wait_recv()
        for t in ts:
            for cp in self._sends(t) + self._forwards(t):
                cp.wait_send()


class _SiblingExchangePlan:
    def __init__(self, shapes, grad_refs, out_refs, send_sems, recv_sems):
        self.shapes, self.grad_refs, self.out_refs = shapes, grad_refs, out_refs
        self.send_sems, self.recv_sems = send_sems, recv_sems
        self.x, self.y, self.c, _ = _mesh_pos()

    def _copies(self):
        out = []
        for t, (g, o) in enumerate(zip(self.grad_refs, self.out_refs)):
            rows = self.shapes[t][1] // 2
            out.append(pltpu.make_async_remote_copy(
                src_ref=g.at[:, pl.ds((1 - self.c) * rows, rows), :], dst_ref=o, send_sem=self.send_sems.at[t],
                recv_sem=self.recv_sems.at[t], device_id=(self.x, self.y, 1 - self.c), device_id_type=MESH))
        return out

    def start(self):
        for cp in self._copies():
            cp.start()

    def finish(self):
        for cp in self._copies():
            cp.wait()


class _ChipExchangePlan:
    def __init__(self, part_refs, out_refs, send_sems, recv_sems):
        self.part_refs, self.out_refs, self.send_sems, self.recv_sems = part_refs, out_refs, send_sems, recv_sems
        _, _, self.c, self.chips = _mesh_pos()

    def _copies(self):
        return [pltpu.make_async_remote_copy(
            src_ref=p.at[2 * chip[0] + chip[1]], dst_ref=o.at[j], send_sem=self.send_sems.at[3 * t + j],
            recv_sem=self.recv_sems.at[3 * t + j], device_id=(*chip, self.c), device_id_type=MESH)
            for t, (p, o) in enumerate(zip(self.part_refs, self.out_refs)) for j, chip in enumerate(self.chips)]

    def start(self):
        for cp in self._copies():
            cp.start()

    def finish(self):
        for cp in self._copies():
            cp.wait()


def _sem_pair(n):
    return [pltpu.SemaphoreType.DMA((n,)), pltpu.SemaphoreType.DMA((n,))]


def _mm(a, b, *, dims, tm, tn, tk, out_dtype, name):
    if dims == "nn":
        m, k = a.shape
        n = b.shape[1]
        a_spec = pl.BlockSpec((tm, tk), lambda i, j, kk: (i, kk))
        b_spec = pl.BlockSpec((tk, tn), lambda i, j, kk: (kk, j))
        dot = _dot
    elif dims == "nt":
        m, k = a.shape
        n = b.shape[0]
        a_spec = pl.BlockSpec((tm, tk), lambda i, j, kk: (i, kk))
        b_spec = pl.BlockSpec((tn, tk), lambda i, j, kk: (j, kk))
        dot = _dot_nt
    else:
        k, m = a.shape
        n = b.shape[1]
        a_spec = pl.BlockSpec((tk, tm), lambda i, j, kk: (kk, i))
        b_spec = pl.BlockSpec((tk, tn), lambda i, j, kk: (kk, j))
        dot = _dot_tn
    assert m % tm == 0 and n % tn == 0 and k % tk == 0, (name, m, n, k)
    grid = (m // tm, n // tn, k // tk)
    nk = grid[2]
    assert nk == 1 or out_dtype == F32, name

    def body(a_ref, b_ref, o_ref):
        prod = dot(a_ref[...].astype(BF16), b_ref[...].astype(BF16))
        if nk == 1:
            o_ref[...] = prod.astype(out_dtype)
        else:
            kk = pl.program_id(2)

            @pl.when(kk == 0)
            def _():
                o_ref[...] = prod

            @pl.when(kk > 0)
            def _():
                o_ref[...] += prod

    return pl.pallas_call(
        body, grid=grid, in_specs=[a_spec, b_spec],
        out_specs=pl.BlockSpec((tm, tn), lambda i, j, kk: (i, j)),
        out_shape=jax.ShapeDtypeStruct((m, n), out_dtype),
        compiler_params=_params(("parallel", "parallel", "arbitrary")), name=name,
    )(a, b)


def _fused_rows(name, tm, mats, rows, vecs, fn, row_outs, acc_outs, exchange=()):
    m = mats[0][0].shape[0]
    nm, nr, nv, nro, nao, nx = len(mats), len(rows), len(vecs), len(row_outs), len(acc_outs), len(exchange)
    n_steps = m // tm

    def body(*refs):
        a_refs, w_refs = refs[:nm], refs[nm:2 * nm]
        pos = 2 * nm
        row_refs, vec_refs, part_refs = refs[pos:pos + nr], refs[pos + nr:pos + nr + nv], refs[pos + nr + nv:pos + nr + nv + nx]
        pos += nr + nv + nx
        out_refs, acc_refs, recv_refs = refs[pos:pos + nro], refs[pos + nro:pos + nro + nao], refs[pos + nro + nao:pos + nro + nao + nx]
        sems = refs[pos + nro + nao + nx:]
        i = pl.program_id(0)
        if nx:
            plan = _ChipExchangePlan(part_refs, recv_refs, *sems)

            @pl.when(i == 0)
            def _():
                plan.start()

        @pl.when(i == 0)
        def _():
            for r in acc_refs:
                r[...] = jnp.zeros_like(r)

        y = None
        for a_ref, w_ref, (_, _, dims, sl) in zip(a_refs, w_refs, mats):
            w = w_ref[...] if sl is None else w_ref[sl, :]
            part = (_dot if dims == "nn" else _dot_nt)(a_ref[...], w)
            y = part if y is None else y + part
        res = fn(y, *[r[...] for r in row_refs], *[v[...] for v in vec_refs])
        for r, val in zip(out_refs, res[:nro]):
            r[...] = val.astype(r.dtype)
        for r, val in zip(acc_refs, res[nro:]):
            r[...] += val

        if nx:
            @pl.when(i == n_steps - 1)
            def _():
                plan.finish()

    tile = lambda width: pl.BlockSpec((tm, width), lambda i: (i, 0))
    res = pl.pallas_call(
        body, grid=(n_steps,),
        in_specs=[tile(a.shape[1]) for a, _, _, _ in mats] + [_full_spec(w.shape) for _, w, _, _ in mats]
        + [tile(D_MODEL)] * nr + [_full_spec((1, D_MODEL))] * nv + [ANY] * nx,
        out_specs=[tile(D_MODEL)] * nro + [_full_spec(s) for s in acc_outs] + [ANY] * nx,
        out_shape=[jax.ShapeDtypeStruct((m, D_MODEL), dt) for dt in row_outs]
        + [jax.ShapeDtypeStruct(s, F32) for s in acc_outs]
        + [jax.ShapeDtypeStruct((3,) + p.shape[1:], p.dtype) for p in exchange],
        scratch_shapes=_sem_pair(3 * nx) if nx else [],
        compiler_params=_params(("arbitrary",)), name=name,
    )(*[a for a, _, _, _ in mats], *[w for _, w, _, _ in mats], *rows, *vecs, *exchange)
    return list(res[:nro + nao]), list(res[nro + nao:])


ROW_TILE = 512


def _vec_spec(width=D_MODEL):
    return pl.BlockSpec((1, width), lambda i: (0, 0))


def _rstd(v):
    return lax.rsqrt(jnp.mean(v * v, axis=-1, keepdims=True) + NORM_EPS)


def _mid_fwd_rows(y1, x0, g2, g3):
    x1 = x0 + y1 * _rstd(y1) * g2
    return y1, x1, x1 * _rstd(x1) * g3


def _rms_bwd_rows(dout, v, g):
    r = _rstd(v)
    n = v * r
    dn = dout * g
    dv = r * (dn - n * jnp.mean(dn * n, axis=-1, keepdims=True))
    dg = jnp.sum(dout * n, axis=0, keepdims=True)
    return dv, dg


def _loss_head_rows(y2, x1, tgt, g4):
    x2 = x1 + y2 * _rstd(y2) * g4
    err = x2 - tgt
    loss = 0.5 * jnp.sum(jnp.mean(err * err, axis=-1, keepdims=True), axis=0, keepdims=True)
    dx2 = err * (1.0 / D_MODEL)
    dy2, dg4 = _rms_bwd_rows(dx2, y2, g4)
    return dx2, dy2, dg4, loss


def _mid_bwd_rows(dh2, x1, y1, dx2, g2, g3):
    d3, dg3 = _rms_bwd_rows(dh2, x1, g3)
    dx1 = dx2 + d3
    dy1, dg2 = _rms_bwd_rows(dx1, y1, g2)
    return dx1, dy1, dg2, dg3


def _in_bwd_rows(dh1, x0, dx1, g1):
    d1, dg1 = _rms_bwd_rows(dh1, x0, g1)
    return dx1 + d1, dg1


GATE_ROWS = 512


def _group_mean_matrix():
    p = np.zeros((A_WIDTH, A_WIDTH), np.float32)
    for g in range(A_GROUPS):
        p[g * HEAD_DIM:(g + 1) * HEAD_DIM, g * HEAD_DIM:(g + 1) * HEAD_DIM] = 1.0 / HEAD_DIM
    return jnp.asarray(p)


def _group_masks(width=A_WIDTH):
    lane = lax.broadcasted_iota(jnp.int32, (1, width), 1)
    return [(lane >= g * HEAD_DIM) & (lane < (g + 1) * HEAD_DIM) for g in range(width // HEAD_DIM)]


def _layernorm_groups(vg, pavg):
    hi = lax.Precision.HIGHEST
    mu = _dot(vg, pavg, hi)
    xc = vg - mu
    var = _dot(xc * xc, pavg, hi)
    rstd = lax.rsqrt(var + NORM_EPS)
    return xc * rstd, rstd


def _spatial_mix(w_bf, vn_chunk_bf, masks, bz):
    z = bz
    for g in range(A_GROUPS):
        z = z + jnp.where(masks[g], _dot(w_bf[g], vn_chunk_bf), 0.0)
    return z


def _full_spec(shape):
    return pl.BlockSpec(shape, lambda i: tuple(0 for _ in shape))


def _gate_fwd(uv, ln_g, ln_b, w_s, bz):
    m = uv.shape[0]
    pavg = _group_mean_matrix()

    def body(u_ref, v_ref, lg_ref, lb_ref, w_ref, bz_ref, p_ref, a_ref):
        masks = _group_masks()
        row = lax.broadcasted_iota(jnp.int32, (CHUNK, CHUNK), 0)
        col = lax.broadcasted_iota(jnp.int32, (CHUNK, CHUNK), 1)
        w_bf = [jnp.where(row >= col, w_ref[g], 0.0).astype(BF16) for g in range(A_GROUPS)]
        ug = _gelu(u_ref[...])
        vhat, _ = _layernorm_groups(_gelu(v_ref[...]), p_ref[...])
        vn = vhat * lg_ref[...] + lb_ref[...]
        bz = bz_ref[...]
        for c in range(GATE_ROWS // CHUNK):
            sl = slice(c * CHUNK, (c + 1) * CHUNK)
            z = _spatial_mix(w_bf, vn[sl].astype(BF16), masks, bz)
            a_ref[sl, :] = (ug[sl] * z).astype(BF16)

    return pl.pallas_call(
        body, grid=(m // GATE_ROWS,),
        in_specs=[pl.BlockSpec((GATE_ROWS, A_WIDTH), lambda i: (i, 0)),
                  pl.BlockSpec((GATE_ROWS, A_WIDTH), lambda i: (i, 1)),
                  _full_spec((1, A_WIDTH)), _full_spec((1, A_WIDTH)), _full_spec((A_GROUPS, CHUNK, CHUNK)),
                  _full_spec((CHUNK, A_WIDTH)), _full_spec((A_WIDTH, A_WIDTH))],
        out_specs=pl.BlockSpec((GATE_ROWS, A_WIDTH), lambda i: (i, 0)),
        out_shape=jax.ShapeDtypeStruct((m, A_WIDTH), BF16),
        compiler_params=_params(("parallel",)), name="gate_fwd",
    )(uv, uv, ln_g, ln_b, w_s, bz, pavg)


def _gate_bwd(uv, dmix, ln_g, ln_b, w_s, w_st, bz, grads):
    m = uv.shape[0]
    pavg = _group_mean_matrix()
    nsteps = m // GATE_ROWS
    nx = len(grads)
    shapes = [g.shape for g in grads]

    def body(u_ref, v_ref, da_ref, lg_ref, lb_ref, w_ref, wt_ref, bz_ref, p_ref, *rest):
        grad_refs = rest[:nx]
        duv_ref, dlg_ref, dlb_ref, dw_ref, dbz_ref = rest[nx:nx + 5]
        recv_refs = rest[nx + 5:2 * nx + 5]
        exchange = _SiblingExchangePlan(shapes, grad_refs, recv_refs, *rest[2 * nx + 5:])
        i = pl.program_id(0)

        @pl.when(i == 0)
        def _():
            exchange.start()
            dlg_ref[...] = jnp.zeros_like(dlg_ref)
            dlb_ref[...] = jnp.zeros_like(dlb_ref)
            dw_ref[...] = jnp.zeros_like(dw_ref)
            dbz_ref[...] = jnp.zeros_like(dbz_ref)

        hi = lax.Precision.HIGHEST
        masks = _group_masks()
        row = lax.broadcasted_iota(jnp.int32, (CHUNK, CHUNK), 0)
        col = lax.broadcasted_iota(jnp.int32, (CHUNK, CHUNK), 1)
        tril = row >= col
        w_bf = [jnp.where(tril, w_ref[g], 0.0).astype(BF16) for g in range(A_GROUPS)]
        wt_bf = [jnp.where(col >= row, wt_ref[g], 0.0).astype(BF16) for g in range(A_GROUPS)]
        pavg_v = p_ref[...]
        lg = lg_ref[...]
        ug, dug = _gelu_and_grad(u_ref[...])
        vg, dvg_dx = _gelu_and_grad(v_ref[...])
        vhat, rstd = _layernorm_groups(vg, pavg_v)
        vn = vhat * lg + lb_ref[...]
        da = da_ref[...]
        bz = bz_ref[...]
        for c in range(GATE_ROWS // CHUNK):
            sl = slice(c * CHUNK, (c + 1) * CHUNK)
            vn_bf = vn[sl].astype(BF16)
            z = _spatial_mix(w_bf, vn_bf, masks, bz)
            dz = da[sl] * ug[sl]
            duv_ref[sl, 0:A_WIDTH] = da[sl] * z * dug[sl]
            dbz_ref[...] += dz
            dz_bf = dz.astype(BF16)
            dvn = jnp.zeros((CHUNK, A_WIDTH), F32)
            for g in range(A_GROUPS):
                dz_g = jnp.where(masks[g], dz, 0.0).astype(BF16)
                dw_ref[g] += jnp.where(tril, _dot_nt(dz_g, vn_bf), 0.0)
                dvn = dvn + jnp.where(masks[g], _dot(wt_bf[g], dz_bf), 0.0)
            vh = vhat[sl]
            dlb_ref[...] += jnp.sum(dvn, axis=0, keepdims=True)
            dlg_ref[...] += jnp.sum(dvn * vh, axis=0, keepdims=True)
            dvh = dvn * lg
            m1 = _dot(dvh, pavg_v, hi)
            m2 = _dot(dvh * vh, pavg_v, hi)
            duv_ref[sl, A_WIDTH:2 * A_WIDTH] = rstd[sl] * (dvh - m1 - vh * m2) * dvg_dx[sl]

        @pl.when(i == nsteps - 1)
        def _():
            dbz_ref[...] = _dot(dbz_ref[...], pavg_v * float(HEAD_DIM), hi)
            exchange.finish()

    res = pl.pallas_call(
        body, grid=(nsteps,),
        in_specs=[pl.BlockSpec((GATE_ROWS, A_WIDTH), lambda i: (i, 0)),
                  pl.BlockSpec((GATE_ROWS, A_WIDTH), lambda i: (i, 1)),
                  pl.BlockSpec((GATE_ROWS, A_WIDTH), lambda i: (i, 0)),
                  _full_spec((1, A_WIDTH)), _full_spec((1, A_WIDTH)), _full_spec((A_GROUPS, CHUNK, CHUNK)),
                  _full_spec((A_GROUPS, CHUNK, CHUNK)), _full_spec((CHUNK, A_WIDTH)),
                  _full_spec((A_WIDTH, A_WIDTH))] + [ANY] * nx,
        out_specs=[pl.BlockSpec((GATE_ROWS, 2 * A_WIDTH), lambda i: (i, 0)),
                   _full_spec((1, A_WIDTH)), _full_spec((1, A_WIDTH)), _full_spec((A_GROUPS, CHUNK, CHUNK)),
                   _full_spec((CHUNK, A_WIDTH))] + [ANY] * nx,
        out_shape=[jax.ShapeDtypeStruct((m, 2 * A_WIDTH), F32),
                   jax.ShapeDtypeStruct((1, A_WIDTH), F32), jax.ShapeDtypeStruct((1, A_WIDTH), F32),
                   jax.ShapeDtypeStruct((A_GROUPS, CHUNK, CHUNK), F32),
                   jax.ShapeDtypeStruct((CHUNK, A_WIDTH), F32)]
        + [jax.ShapeDtypeStruct((N_SHARD, s[1] // 2, s[2]), F32) for s in shapes],
        scratch_shapes=_sem_pair(nx),
        compiler_params=_params(("arbitrary",)), name="gate_bwd",
    )(uv, uv, dmix, ln_g, ln_b, w_s, w_st, bz, pavg, *grads)
    return res[:5], list(res[5:])


Q_BLOCK = 128
PAIR = 2 * HEAD_DIM
N_PAIR = B_HEADS // 2
N_CFG = len(DILATED)
BLOCKS_PER_CFG = SEQ // Q_BLOCK
QKV_SLABS = 3 * N_PAIR
FWD_BLOCKS_PER_TRIP = 8
BWD_BLOCKS_PER_TRIP = 4


def _t5_bucket_np(dist, dtype):
    max_exact = NUM_BUCKETS // 2
    d = np.maximum(dist, 1).astype(dtype)
    large = max_exact + (np.log(d / dtype(max_exact)) / dtype(math.log(MAX_DISTANCE / max_exact))
                         * dtype(NUM_BUCKETS - max_exact))
    large = np.minimum(large.astype(np.int32), NUM_BUCKETS - 1)
    return np.where(dist < max_exact, dist, large)


def _bucket_tables():
    i = np.arange(Q_BLOCK)[:, None]
    j = np.arange(Q_BLOCK)[None, :]
    tables = []
    for _, dil in DILATED:
        rel_prev = Q_BLOCK + i - j
        rel_cur = i - j
        rel = np.concatenate([rel_prev, rel_cur], axis=1)
        valid = np.concatenate([rel_prev <= Q_BLOCK, rel_cur >= 0], axis=1)
        dist = np.maximum(rel, 0) * dil
        b32 = _t5_bucket_np(dist, np.float32)
        b64 = _t5_bucket_np(dist, np.float64)
        assert np.array_equal(b32, b64)
        tables.append(np.where(valid, b32, -1).astype(np.int32))
    return np.stack(tables)


def _present_buckets(buckets_np):
    return [sorted(set(int(v) for v in np.unique(buckets_np[c]) if v >= 0)) for c in range(N_CFG)]


def _bias_tables_body(buckets_np):
    present = _present_buckets(buckets_np)

    def tables(rb_ref, bk_ref, o_ref, ot_ref):
        for c in range(N_CFG):
            bk = bk_ref[c]
            for h in range(B_HEADS):
                acc = jnp.full((Q_BLOCK, 2 * Q_BLOCK), NEG_INF, F32)
                for b in present[c]:
                    acc = jnp.where(bk == b, rb_ref[b, h], acc)
                o_ref[c, h] = acc
                ot_ref[c, h] = acc.T

    return tables


def _proj_fwd(x, g1, w_in_t):
    m = x.shape[0]
    tm = ROW_TILE

    def body(x_ref, g_ref, w_ref, h_ref, uv_ref, qkv_ref):
        xv = x_ref[...]
        h = (xv * _rstd(xv) * g_ref[...]).astype(BF16)
        h_ref[...] = h
        acc = _dot_nt(h, w_ref[...])
        uv_ref[...] = acc[:, :2 * A_WIDTH]
        for s in range(QKV_SLABS):
            qkv_ref[s] = acc[:, 2 * A_WIDTH + s * PAIR:2 * A_WIDTH + (s + 1) * PAIR]

    return pl.pallas_call(
        body, grid=(m // tm,),
        in_specs=[pl.BlockSpec((tm, D_MODEL), lambda i: (i, 0)), _vec_spec(),
                  pl.BlockSpec((IN_COLS, D_MODEL), lambda i: (0, 0))],
        out_specs=[pl.BlockSpec((tm, D_MODEL), lambda i: (i, 0)),
                   pl.BlockSpec((tm, 2 * A_WIDTH), lambda i: (i, 0)),
                   pl.BlockSpec((QKV_SLABS, tm, PAIR), lambda i: (0, i, 0))],
        out_shape=[jax.ShapeDtypeStruct((m, D_MODEL), BF16), jax.ShapeDtypeStruct((m, 2 * A_WIDTH), F32),
                   jax.ShapeDtypeStruct((QKV_SLABS, m, PAIR), F32)],
        compiler_params=_params(("parallel",)), name="proj_fwd",
    )(x, g1, w_in_t)


def _pair_masks():
    lane = lax.broadcasted_iota(jnp.int32, (1, PAIR), 1)
    return [lane < HEAD_DIM, lane >= HEAD_DIM]


def _block_rows(idx, dil):
    static = isinstance(idx, int)
    r, n = idx % dil, idx // dil

    def rows_of(block):
        start = r + (dil * Q_BLOCK) * block
        if dil == 1:
            return pl.ds(start if static else pl.multiple_of(start, Q_BLOCK), Q_BLOCK)
        return pl.ds(start, Q_BLOCK, stride=dil)

    prev = rows_of(n - 1) if not static or n > 0 else None
    return rows_of(n), prev


def _attn_fwd(qkv, bias, batch, shards):
    m = qkv.shape[1]
    comb_rows = 256
    nt = len(shards)
    shapes = [sh.shape for sh in shards]
    n_steps = batch * N_PAIR
    early, late = list(range(nt // 2)), list(range(nt // 2, nt))

    def body(q_ref, k_ref, v_ref, b_ref, *rest):
        shard_refs = rest[:nt]
        o_ref, l_ref = rest[nt:nt + 2]
        gat_refs = rest[nt + 2:2 * nt + 2]
        scratch = rest[2 * nt + 2:]
        oc_refs, lc_refs = scratch[:N_CFG], scratch[N_CFG:2 * N_CFG]
        step = pl.program_id(0) * N_PAIR + pl.program_id(1)
        gather = _GatherPlan(shapes, shard_refs, gat_refs, *scratch[2 * N_CFG:])

        @pl.when(step == 0)
        def _():
            gather.start(early + late)

        @pl.when(step == n_steps // 2)
        def _():
            gather.forward(early)

        @pl.when(step == n_steps - 2)
        def _():
            gather.forward(late)

        masks = _pair_masks()
        for ci, (_, dil) in enumerate(DILATED):
            nb = SEQ // dil // Q_BLOCK

            def block(trip, ci=ci, dil=dil, nb=nb):
                work = []
                for u in range(FWD_BLOCKS_PER_TRIP):
                    rows, prow = _block_rows(trip * FWD_BLOCKS_PER_TRIP + u, dil)
                    has_prev = nb > 1 and prow is not None
                    q = q_ref[rows, :] * 0.125
                    kc = k_ref[rows, :].astype(BF16)
                    vc = v_ref[rows, :]
                    kp = k_ref[prow, :].astype(BF16) if has_prev else None
                    vp = v_ref[prow, :] if has_prev else None
                    tiles = []
                    for h in range(2):
                        qh = jnp.where(masks[h], q, 0.0).astype(BF16)
                        sc = _dot_nt(qh, kc) + b_ref[ci, h, :, Q_BLOCK:]
                        sp = _dot_nt(qh, kp) + b_ref[ci, h, :, :Q_BLOCK] if has_prev else None
                        tiles.append((sc, sp))
                    work.append((rows, vc, vp, tiles))
                probs = []
                for _, _, _, tiles in work:
                    ps = []
                    for sc, sp in tiles:
                        mx = jnp.max(sc if sp is None else jnp.maximum(sc, sp), axis=1, keepdims=True)
                        pc = jnp.exp(sc - mx).astype(BF16)
                        pp = None if sp is None else jnp.exp(sp - mx).astype(BF16)
                        ps.append((mx, pc, pp))
                    probs.append(ps)
                for (rows, vc, vp, _), ps in zip(work, probs):
                    res = []
                    for h, (_, pc, pp) in enumerate(ps):
                        r = _dot(pc, jnp.where(masks[h], vc, 1.0).astype(BF16))
                        if pp is not None:
                            r = r + _dot(pp, jnp.where(masks[h], vp, 1.0).astype(BF16))
                        res.append(r)
                    num = jnp.where(masks[0], res[0], res[1])
                    den = pltpu.roll(jnp.where(masks[0], res[1], res[0]), HEAD_DIM, 1)
                    oc_refs[ci][rows, :] = num / den
                    lc_refs[ci][rows, :] = jnp.where(masks[0], ps[0][0], ps[1][0]) + jnp.log(den)

            for trip in range(BLOCKS_PER_CFG // FWD_BLOCKS_PER_TRIP):
                block(trip)

        def combine(i, carry):
            rr = pl.ds(pl.multiple_of(i * comb_rows, comb_rows), comb_rows)
            ls = [lc_refs[c][rr, :] for c in range(N_CFG)]
            mx = functools.reduce(jnp.maximum, ls)
            ws = [jnp.exp(l - mx) for l in ls]
            tot = functools.reduce(lambda a, b: a + b, ws)
            o = functools.reduce(lambda a, b: a + b, [ws[c] * oc_refs[c][rr, :] for c in range(N_CFG)]) / tot
            o_ref[rr, :] = o.astype(BF16)
            l_ref[rr, :] = mx + jnp.log(tot)
            return carry

        lax.fori_loop(0, SEQ // comb_rows, combine, 0)

        @pl.when(step == n_steps - 1)
        def _():
            gather.finish(early + late)

    def slab(first):
        return pl.BlockSpec((None, SEQ, PAIR), lambda b, p: (first + p, b, 0))

    nat = pl.BlockSpec((SEQ, PAIR), lambda b, p: (b, p))
    res = pl.pallas_call(
        body, grid=(batch, N_PAIR),
        in_specs=[slab(0), slab(N_PAIR), slab(2 * N_PAIR),
                  pl.BlockSpec((N_CFG, 2, Q_BLOCK, 2 * Q_BLOCK), lambda b, p: (0, p, 0, 0))] + [ANY] * nt,
        out_specs=[nat, nat] + [ANY] * nt,
        out_shape=[jax.ShapeDtypeStruct((m, B_WIDTH), BF16), jax.ShapeDtypeStruct((m, B_WIDTH), F32)]
        + [jax.ShapeDtypeStruct((N_SHARD,) + sh.shape, sh.dtype) for sh in shards],
        scratch_shapes=[pltpu.VMEM((SEQ, PAIR), F32)] * (2 * N_CFG) + _sem_pair(6 * nt),
        compiler_params=_params(("arbitrary", "arbitrary")), name="attn_fwd",
    )(qkv, qkv, qkv, bias, *shards)
    return res[0], res[1], list(res[2:])


def _attn_bwd(qkv, dmix, o, lse, bias_t, batch, parts):
    m = qkv.shape[1]
    nt = len(parts)
    n_steps = N_PAIR * batch

    def body(q_ref, k_ref, v_ref, do_ref, o_ref, l_ref, b_ref, *rest):
        part_refs = rest[:nt]
        dqkv_ref, ds_ref = rest[nt:nt + 2]
        recv_refs = rest[nt + 2:2 * nt + 2]
        dq_acc, dk_acc, dv_acc, d_scr, send_sems, recv_sems = rest[2 * nt + 2:]
        step = pl.program_id(0) * batch + pl.program_id(1)
        exchange = _ChipExchangePlan(part_refs, recv_refs, send_sems, recv_sems)

        @pl.when(step == 0)
        def _():
            exchange.start()

        @pl.when(pl.program_id(1) == 0)
        def _():
            ds_ref[...] = jnp.zeros_like(ds_ref)

        dq_acc[...] = jnp.zeros_like(dq_acc)
        dk_acc[...] = jnp.zeros_like(dk_acc)
        dv_acc[...] = jnp.zeros_like(dv_acc)
        d_scr[...] = do_ref[...] * o_ref[...].astype(F32)
        masks = _pair_masks()

        def stack_heads(t):
            return jnp.concatenate([jnp.where(masks[0], t, 0.0), jnp.where(masks[1], t, 0.0)], axis=0).astype(BF16)

        for ci, (_, dil) in enumerate(DILATED):
            nb = SEQ // dil // Q_BLOCK

            def block(trip, carry, ci=ci, dil=dil, nb=nb):
                first = []
                for u in range(BWD_BLOCKS_PER_TRIP):
                    rows, prow = _block_rows(trip * BWD_BLOCKS_PER_TRIP + u, dil)
                    has_prev = nb > 1 and prow is not None
                    if has_prev:
                        kcat = jnp.concatenate([k_ref[prow, :], k_ref[rows, :]], axis=0).astype(BF16)
                        vcat = jnp.concatenate([v_ref[prow, :], v_ref[rows, :]], axis=0).astype(BF16)
                    else:
                        kcat = k_ref[rows, :].astype(BF16)
                        vcat = v_ref[rows, :].astype(BF16)
                    qst = stack_heads(q_ref[rows, :] * 0.125)
                    dost = stack_heads(do_ref[rows, :])
                    lt = l_ref[rows, :].T
                    dt = d_scr[rows, :].T
                    lrow = jnp.concatenate([lt[0:1], lt[HEAD_DIM:HEAD_DIM + 1]], axis=1)
                    drow = jnp.concatenate([jnp.sum(dt[:HEAD_DIM], axis=0, keepdims=True),
                                            jnp.sum(dt[HEAD_DIM:], axis=0, keepdims=True)], axis=1)
                    first.append((has_prev, rows, prow, kcat, qst, dost, lrow, drow,
                                  _dot_nt(kcat, qst), _dot_nt(vcat, dost)))
                second = []
                for has_prev, rows, prow, kcat, qst, dost, lrow, drow, st, dpt in first:
                    keys = slice(0, 2 * Q_BLOCK) if has_prev else slice(Q_BLOCK, 2 * Q_BLOCK)
                    bt = jnp.concatenate([b_ref[ci, 0, keys, :], b_ref[ci, 1, keys, :]], axis=1)
                    pt = jnp.exp(st + bt - lrow)
                    dst = pt * (dpt - drow)
                    ds_ref[ci, 0, keys, :] += dst[:, :Q_BLOCK]
                    ds_ref[ci, 1, keys, :] += dst[:, Q_BLOCK:]
                    second.append((has_prev, rows, prow, kcat, qst, dost, pt.astype(BF16), dst.astype(BF16)))
                for has_prev, rows, prow, kcat, qst, dost, pt_bf, dst_bf in second:
                    dk = _dot(dst_bf, qst)
                    dv = _dot(pt_bf, dost)
                    dq2 = _dot_tn(dst_bf, kcat)
                    dq_acc[rows, :] += jnp.where(masks[0], dq2[:Q_BLOCK], dq2[Q_BLOCK:]) * 0.125
                    if has_prev:
                        dk_acc[prow, :] += dk[:Q_BLOCK]
                        dv_acc[prow, :] += dv[:Q_BLOCK]
                        dk_acc[rows, :] += dk[Q_BLOCK:]
                        dv_acc[rows, :] += dv[Q_BLOCK:]
                    else:
                        dk_acc[rows, :] += dk
                        dv_acc[rows, :] += dv
                return carry

            block(0, 0)
            lax.fori_loop(1, BLOCKS_PER_CFG // BWD_BLOCKS_PER_TRIP, block, 0)

        dqkv_ref[0] = dq_acc[...].astype(BF16)
        dqkv_ref[1] = dk_acc[...].astype(BF16)
        dqkv_ref[2] = dv_acc[...].astype(BF16)

        @pl.when(step == n_steps - 1)
        def _():
            exchange.finish()

    def slab(first):
        return pl.BlockSpec((None, SEQ, PAIR), lambda p, b: (first + p, b, 0))

    nat = pl.BlockSpec((SEQ, PAIR), lambda p, b: (b, p))
    tbl = pl.BlockSpec((N_CFG, 2, 2 * Q_BLOCK, Q_BLOCK), lambda p, b: (0, p, 0, 0))
    acc = pltpu.VMEM((SEQ, PAIR), F32)
    res = pl.pallas_call(
        body, grid=(N_PAIR, batch),
        in_specs=[slab(0), slab(N_PAIR), slab(2 * N_PAIR),
                  pl.BlockSpec((SEQ, PAIR), lambda p, b: (b, A_WIDTH // PAIR + p)), nat, nat, tbl] + [ANY] * nt,
        out_specs=[pl.BlockSpec((3, SEQ, PAIR), lambda p, b: (0, b, p)), tbl] + [ANY] * nt,
        out_shape=[jax.ShapeDtypeStruct((3, m, B_WIDTH), BF16),
                   jax.ShapeDtypeStruct((N_CFG, B_HEADS, 2 * Q_BLOCK, Q_BLOCK), F32)]
        + [jax.ShapeDtypeStruct((3,) + p.shape[1:], p.dtype) for p in parts],
        scratch_shapes=[acc, acc, acc, acc] + _sem_pair(3 * nt),
        compiler_params=_params(("arbitrary", "arbitrary")), name="attn_bwd",
    )(qkv, qkv, qkv, dmix, o, lse, bias_t, *parts)
    return res[0], res[1], list(res[2:])


def _rel_bias_grad(ds, buckets_np, grads):
    present = _present_buckets(buckets_np)
    nx = len(grads)
    shapes = [g.shape for g in grads]

    def body(bk_ref, ds_ref, *rest):
        o_ref = rest[nx]
        acc_ref = rest[2 * nx + 1]
        exchange = _SiblingExchangePlan(shapes, rest[:nx], rest[nx + 1:2 * nx + 1], *rest[2 * nx + 2:])
        exchange.start()
        acc_ref[...] = jnp.zeros_like(acc_ref)
        for c in range(N_CFG):
            bk = bk_ref[c]
            for h in range(B_HEADS):
                dsv = ds_ref[c, h]
                for b in present[c]:
                    part = jnp.sum(jnp.where(bk == b, dsv, 0.0), axis=0, keepdims=True)
                    acc_ref[pl.ds(h * NUM_BUCKETS + b, 1), :] += part
        o_ref[...] = jnp.sum(acc_ref[...], axis=1, keepdims=True)
        exchange.finish()

    vm = pl.BlockSpec(memory_space=pltpu.VMEM)
    res = pl.pallas_call(
        body, in_specs=[vm, vm] + [ANY] * nx, out_specs=[vm] + [ANY] * nx,
        out_shape=[jax.ShapeDtypeStruct((B_HEADS * NUM_BUCKETS, 1), F32)]
        + [jax.ShapeDtypeStruct((N_SHARD, s[1] // 2, s[2]), F32) for s in shapes],
        scratch_shapes=[pltpu.VMEM((B_HEADS * NUM_BUCKETS, buckets_np.shape[-1]), F32)] + _sem_pair(nx),
        compiler_params=_params(), name="rel_bias_grad",
    )(jnp.asarray(buckets_np), ds, *grads)
    return res[0], list(res[1:])


def _assemble_dproj(duv, dqkv):
    m = duv.shape[0]
    rows = 1024

    def body(duv_ref, dqkv_ref, o_ref):
        o_ref[:, :2 * A_WIDTH] = duv_ref[...].astype(BF16)
        for k in range(3):
            o_ref[:, 2 * A_WIDTH + k * B_WIDTH:2 * A_WIDTH + (k + 1) * B_WIDTH] = dqkv_ref[k]

    return pl.pallas_call(
        body, grid=(m // rows,),
        in_specs=[pl.BlockSpec((rows, 2 * A_WIDTH), lambda i: (i, 0)),
                  pl.BlockSpec((3, rows, B_WIDTH), lambda i: (0, i, 0))],
        out_specs=pl.BlockSpec((rows, IN_COLS), lambda i: (i, 0)),
        out_shape=jax.ShapeDtypeStruct((m, IN_COLS), BF16),
        compiler_params=_params(("parallel",)), name="assemble_dproj",
    )(duv, dqkv)


def _row_index():
    return lax.broadcasted_iota(jnp.int32, (SEQ, LANE_BLOCK), 0)


def _shift_down(x, k, row):
    return jnp.where(row >= k, pltpu.roll(x, k, 0), 0.0)


def _shift_up(x, k, row):
    return jnp.where(row < SEQ - k, pltpu.roll(x, SEQ - k, 0), 0.0)


def _convgate_fwd(gate, up, conv_w, conv_b, batch, shard):
    m = gate.shape[0]
    n_col = D_FF // LANE_BLOCK
    n_steps = batch * n_col

    def body(g_ref, u_ref, w_ref, b_ref, shard_ref, a_ref, gat_ref, send_sems, recv_sems):
        step = pl.program_id(0) * n_col + pl.program_id(1)
        gather = _GatherPlan([shard.shape], [shard_ref], [gat_ref], send_sems, recv_sems)

        @pl.when(step == 0)
        def _():
            gather.start([0])

        @pl.when(step == (2 * n_steps) // 3)
        def _():
            gather.forward([0])

        g = g_ref[...].astype(F32)
        w = w_ref[...]
        row = _row_index()
        c = b_ref[...] + w[0:1] * _shift_down(g, 2, row) + w[1:2] * _shift_down(g, 1, row) + w[2:3] * g
        a_ref[...] = (_gelu(c) * u_ref[...].astype(F32)).astype(BF16)

        @pl.when(step == n_steps - 1)
        def _():
            gather.finish([0])

    blk = pl.BlockSpec((SEQ, LANE_BLOCK), lambda b, j: (b, j))
    return pl.pallas_call(
        body, grid=(batch, n_col),
        in_specs=[blk, blk, pl.BlockSpec((3, LANE_BLOCK), lambda b, j: (0, j)),
                  pl.BlockSpec((1, LANE_BLOCK), lambda b, j: (0, j)), ANY],
        out_specs=[blk, ANY],
        out_shape=[jax.ShapeDtypeStruct((m, D_FF), BF16),
                   jax.ShapeDtypeStruct((N_SHARD,) + shard.shape, shard.dtype)],
        scratch_shapes=_sem_pair(6),
        compiler_params=_params(("arbitrary", "arbitrary")), name="convgate_fwd",
    )(gate, up, conv_w, conv_b, shard)


def _convgate_bwd(gate, up, dact, conv_w, conv_b, batch):
    m = gate.shape[0]

    def body(g_ref, u_ref, da_ref, w_ref, b_ref, dg_ref, du_ref, dw_ref, db_ref):
        @pl.when(pl.program_id(1) == 0)
        def _():
            dw_ref[...] = jnp.zeros_like(dw_ref)
            db_ref[...] = jnp.zeros_like(db_ref)

        g = g_ref[...].astype(F32)
        w = w_ref[...]
        row = _row_index()
        g1 = _shift_down(g, 1, row)
        g2 = _shift_down(g, 2, row)
        c = b_ref[...] + w[0:1] * g2 + w[1:2] * g1 + w[2:3] * g
        gg, dgg = _gelu_and_grad(c)
        da = da_ref[...].astype(F32)
        du_ref[...] = (da * gg).astype(BF16)
        dc = da * u_ref[...].astype(F32) * dgg
        db_ref[...] += jnp.sum(dc, axis=0, keepdims=True)
        dw_ref[0:1, :] += jnp.sum(dc * g2, axis=0, keepdims=True)
        dw_ref[1:2, :] += jnp.sum(dc * g1, axis=0, keepdims=True)
        dw_ref[2:3, :] += jnp.sum(dc * g, axis=0, keepdims=True)
        dg_ref[...] = (w[2:3] * dc + w[1:2] * _shift_up(dc, 1, row) + w[0:1] * _shift_up(dc, 2, row)).astype(BF16)

    blk = pl.BlockSpec((SEQ, LANE_BLOCK), lambda j, b: (b, j))
    wspec = pl.BlockSpec((3, LANE_BLOCK), lambda j, b: (0, j))
    bspec = pl.BlockSpec((1, LANE_BLOCK), lambda j, b: (0, j))
    return pl.pallas_call(
        body, grid=(D_FF // LANE_BLOCK, batch),
        in_specs=[blk, blk, blk, wspec, bspec],
        out_specs=[blk, blk, wspec, bspec],
        out_shape=[jax.ShapeDtypeStruct((m, D_FF), BF16), jax.ShapeDtypeStruct((m, D_FF), BF16),
                   jax.ShapeDtypeStruct((3, D_FF), F32), jax.ShapeDtypeStruct((1, D_FF), F32)],
        compiler_params=_params(("parallel", "arbitrary")), name="convgate_bwd",
    )(gate, up, dact, conv_w, conv_b)


def _gather_weights(shards, conv_w_shard, rel_bias, buckets_np):
    nt = len(shards)
    shapes = [sh.shape for sh in shards]
    ts = list(range(nt))
    tables = _bias_tables_body(buckets_np)

    def body(*refs):
        shard_refs = refs[:nt]
        cw_ref, rb_ref, bk_ref = refs[nt:nt + 3]
        out_refs = refs[nt + 3:2 * nt + 3]
        cw_out, bias_ref, bias_t_ref = refs[2 * nt + 3:2 * nt + 6]
        send_sems, recv_sems, cw_send, cw_recv = refs[2 * nt + 6:]
        plan = _GatherPlan(shapes, shard_refs, out_refs, send_sems, recv_sems)
        x, y, c, chips = _mesh_pos()

        def cw_copy(j, src, dst, chip):
            return pltpu.make_async_remote_copy(src_ref=src, dst_ref=dst, send_sem=cw_send.at[j],
                                                recv_sem=cw_recv.at[j], device_id=(*chip, c), device_id_type=MESH)

        plan.start(ts)
        cw_sends = [cw_copy(j, cw_ref, cw_out.at[2 * x + y], chip) for j, chip in enumerate(chips)]
        for cp in cw_sends:
            cp.start()
        tables(rb_ref, bk_ref, bias_ref, bias_t_ref)
        plan.forward(ts)
        for j, chip in enumerate(chips):
            dst = cw_out.at[2 * chip[0] + chip[1]]
            cw_copy(j, dst, dst, chip).wait_recv()
        plan.finish(ts)
        for cp in cw_sends:
            cp.wait_send()

    out_shape = [jax.ShapeDtypeStruct((N_SHARD,) + sh.shape, sh.dtype) for sh in shards]
    out_shape.append(jax.ShapeDtypeStruct((N_SHARD,) + conv_w_shard.shape, conv_w_shard.dtype))
    out_shape += [jax.ShapeDtypeStruct((N_CFG, B_HEADS, Q_BLOCK, 2 * Q_BLOCK), F32),
                  jax.ShapeDtypeStruct((N_CFG, B_HEADS, 2 * Q_BLOCK, Q_BLOCK), F32)]
    vm = pl.BlockSpec(memory_space=pltpu.VMEM)
    res = pl.pallas_call(
        body, in_specs=[ANY] * (nt + 1) + [pl.BlockSpec(memory_space=pltpu.SMEM), vm],
        out_specs=[ANY] * (nt + 1) + [vm, vm], out_shape=out_shape,
        scratch_shapes=_sem_pair(6 * nt) + _sem_pair(3),
        compiler_params=pltpu.CompilerParams(has_side_effects=True, vmem_limit_bytes=VMEM_LIMIT),
        name="gather_weights",
    )(*shards, conv_w_shard, rel_bias, jnp.asarray(buckets_np))
    return list(res[:nt + 1]), res[nt + 1], res[nt + 2]


def _add_halves(g, recv, c_idx):
    _, rows2, cols = g.shape
    rows = rows2 // 2
    tr = rows // 2 if rows % 16 == 0 and rows >= 256 else rows
    nblk = rows // tr

    def body(c_ref, g_ref, r_ref, o_ref):
        o_ref[...] = (g_ref[...] + r_ref[...]).astype(BF16)

    return pl.pallas_call(
        body,
        grid_spec=pltpu.PrefetchScalarGridSpec(
            num_scalar_prefetch=1, grid=(N_SHARD, nblk),
            in_specs=[pl.BlockSpec((None, tr, cols), lambda s, i, c: (s, c[0] * nblk + i, 0)),
                      pl.BlockSpec((None, tr, cols), lambda s, i, c: (s, i, 0))],
            out_specs=pl.BlockSpec((None, tr, cols), lambda s, i, c: (s, i, 0))),
        out_shape=jax.ShapeDtypeStruct((N_SHARD, rows, cols), BF16),
        compiler_params=_params(("parallel", "parallel")), name="rs_add_halves",
    )(c_idx, g, recv)


def _add_chips(part, recv, s_idx, c_idx):
    _, rows, cols = part.shape
    tr = rows // 2 if rows % 32 == 0 and rows >= 256 else rows
    nblk = rows // tr

    def body(idx_ref, p_ref, r_ref, o_ref):
        acc = p_ref[...].astype(F32)
        for j in range(3):
            acc = acc + r_ref[j].astype(F32)
        o_ref[...] = acc

    return pl.pallas_call(
        body,
        grid_spec=pltpu.PrefetchScalarGridSpec(
            num_scalar_prefetch=1, grid=(nblk,),
            in_specs=[pl.BlockSpec((None, tr, cols), lambda i, idx: (idx[0], i, 0)),
                      pl.BlockSpec((3, tr, cols), lambda i, idx: (0, i, 0))],
            out_specs=pl.BlockSpec((tr, cols), lambda i, idx: (idx[1] * nblk + i, 0))),
        out_shape=jax.ShapeDtypeStruct((2 * rows, cols), F32),
        compiler_params=_params(("parallel",)), name="rs_add_chips",
    )(jnp.concatenate([s_idx, c_idx]), part, recv)


def _finish_reductions(fulls, arrays):
    nt, n = len(fulls), len(arrays)

    def body(*refs):
        in_refs = refs[nt:nt + n]
        full_refs, out_refs = refs[nt + n:2 * nt + n], refs[2 * nt + n:2 * nt + 2 * n]
        pos = 2 * nt + 2 * n
        sib_refs, chip_refs = refs[pos:pos + n], refs[pos + n:pos + 2 * n]
        share_send, share_recv, send_sems, recv_sems = refs[pos + 2 * n:]
        x, y, c, chips = _mesh_pos()

        def half(t, which):
            rows = fulls[t].shape[0] // 2
            return full_refs[t].at[pl.ds(which * rows, rows), :]

        def share(t, which):
            return pltpu.make_async_remote_copy(
                src_ref=half(t, which), dst_ref=half(t, which), send_sem=share_send.at[t],
                recv_sem=share_recv.at[t], device_id=(x, y, 1 - c), device_id_type=MESH)

        for t in range(nt):
            share(t, c).start()

        def copy(k, src, dst, to):
            return pltpu.make_async_remote_copy(src_ref=src, dst_ref=dst, send_sem=send_sems.at[k],
                                                recv_sem=recv_sems.at[k], device_id=to, device_id_type=MESH)

        first = [copy(t, in_refs[t], sib_refs[t], (x, y, 1 - c)) for t in range(n)]
        for cp in first:
            cp.start()
        for cp in first:
            cp.wait()
        for t in range(n):
            out_refs[t][...] = in_refs[t][...] + sib_refs[t][...]
        second = [copy(n + 3 * t + j, out_refs[t], chip_refs[t].at[j], (*chip, c))
                  for t in range(n) for j, chip in enumerate(chips)]
        for cp in second:
            cp.start()
        for cp in second:
            cp.wait()
        for t in range(n):
            out_refs[t][...] = (out_refs[t][...] + chip_refs[t][0]) + (chip_refs[t][1] + chip_refs[t][2])
        for t in range(nt):
            share(t, 1 - c).wait_recv()
        for t in range(nt):
            share(t, c).wait_send()

    vm = pl.BlockSpec(memory_space=pltpu.VMEM)
    res = pl.pallas_call(
        body, in_specs=[ANY] * nt + [vm] * n, out_specs=[ANY] * nt + [vm] * n,
        out_shape=[jax.ShapeDtypeStruct(f.shape, f.dtype) for f in fulls]
        + [jax.ShapeDtypeStruct(a.shape, F32) for a in arrays],
        input_output_aliases={t: t for t in range(nt)},
        scratch_shapes=[pltpu.VMEM(a.shape, F32) for a in arrays] + [pltpu.VMEM((3,) + a.shape, F32) for a in arrays]
        + _sem_pair(nt) + _sem_pair(4 * n),
        compiler_params=pltpu.CompilerParams(has_side_effects=True, vmem_limit_bytes=VMEM_LIMIT),
        name="finish_reductions",
    )(*fulls, *arrays)
    return list(res[:nt]), list(res[nt:])


def _from_col_shards(g):
    n, rows, cols = g.shape
    return g.transpose(1, 0, 2).reshape(rows, n * cols)


def _train_step(x, tgt, g1, g2, g3, g4, shards, ln_g, ln_b, w_s, b_s, rel_bias, conv_w_shard, conv_b, batch,
                s_idx, c_idx):
    big = dict(tm=1024, out_dtype=F32)
    buckets = _bucket_tables()
    bz = jnp.repeat(b_s.T, HEAD_DIM, axis=1)
    w_st = jnp.swapaxes(w_s, 1, 2)

    def with_own(gathered, own):
        return lax.dynamic_update_index_in_dim(gathered, own, s_idx[0], 0)

    def shard_major(g):
        return g.reshape(N_SHARD, g.shape[0] // N_SHARD, D_MODEL)

    (g_in, g_convw), bias, bias_t = _gather_weights([shards["w_in"]], conv_w_shard, rel_bias, buckets)
    w_in_t = with_own(g_in, shards["w_in"]).reshape(IN_COLS, D_MODEL)
    conv_w = _from_col_shards(with_own(g_convw, conv_w_shard))

    h1, uv, qkv = _proj_fwd(x, g1, w_in_t)
    a = _gate_fwd(uv, ln_g, ln_b, w_s, bz)
    later = ["w_out", "w_gate", "w_up"]
    o_bf, lse, gathered = _attn_fwd(qkv, bias, batch, [shards[n] for n in later])
    g_out, g_gate, g_up = [with_own(g, shards[n]) for g, n in zip(gathered, later)]
    w_out = g_out.reshape(D_MODEL, D_MODEL)
    w_gate_t = g_gate.reshape(D_FF, D_MODEL)
    w_up_t = g_up.reshape(D_FF, D_MODEL)
    (y1, x1, h2), _ = _fused_rows(
        "out_proj_mid_fwd", 512,
        [(a, w_out, "nn", slice(0, A_WIDTH)), (o_bf, w_out, "nn", slice(A_WIDTH, D_MODEL))],
        [x], [g2, g3], _mid_fwd_rows, [F32, F32, BF16], [])
    gate = _mm(h2, w_gate_t, dims="nt", tm=1024, tn=1408, tk=1024, out_dtype=BF16, name="mm_gate")
    up = _mm(h2, w_up_t, dims="nt", tm=1024, tn=1408, tk=1024, out_dtype=BF16, name="mm_up")
    act, g_down = _convgate_fwd(gate, up, conv_w, conv_b, batch, shards["w_down"])
    w_down = with_own(g_down, shards["w_down"]).reshape(D_FF, D_MODEL)
    (dx2, dy2, dg4, loss), _ = _fused_rows(
        "down_proj_loss_head", 512, [(act, w_down, "nn", None)], [x1, tgt], [g4], _loss_head_rows,
        [F32, BF16], [(1, D_MODEL), (1, 128)])

    dact = _mm(dy2, w_down, dims="nt", tm=1024, tn=1408, tk=1024, out_dtype=BF16, name="mm_dact")
    dw_down = _mm(act, dy2, dims="tn", tm=1408, tn=1024, tk=1024, out_dtype=F32, name="mm_dw_down")
    dgate, dup, dconv_w, dconv_b = _convgate_bwd(gate, up, dact, conv_w, conv_b, batch)
    (dx1, dy1, dg2, dg3), _ = _fused_rows(
        "dh2_mid_bwd", 256, [(dgate, w_gate_t, "nn", None), (dup, w_up_t, "nn", None)],
        [x1, y1, dx2], [g2, g3], _mid_bwd_rows, [F32, BF16], [(1, D_MODEL), (1, D_MODEL)])
    dw_gate_t = _mm(dgate, h2, dims="tn", tm=1408, tn=1024, tk=1024, out_dtype=F32, name="mm_dw_gate")
    dw_up_t = _mm(dup, h2, dims="tn", tm=1408, tn=1024, tk=1024, out_dtype=F32, name="mm_dw_up")
    dmix = _mm(dy1, w_out, dims="nt", tn=1024, tk=1024, name="mm_dmix", **big)
    dw_out_a = _mm(a, dy1, dims="tn", tm=A_WIDTH, tn=1024, tk=1024, out_dtype=F32, name="mm_dw_out_a")
    dw_out_b = _mm(o_bf, dy1, dims="tn", tm=B_WIDTH, tn=1024, tk=1024, out_dtype=F32, name="mm_dw_out_b")

    dw_out = jnp.concatenate([dw_out_a, dw_out_b], axis=0)
    done = [shard_major(g) for g in (dw_down, dw_gate_t, dw_up_t, dw_out)]
    (duv, dln_g, dln_b, dw_s, dbz), recv_a = _gate_bwd(uv, dmix, ln_g, ln_b, w_s, w_st, bz, done)
    parts = [_add_halves(g, r, c_idx) for g, r in zip(done, recv_a)]
    dqkv, ds, recv = _attn_bwd(qkv, dmix, o_bf, lse, bias_t, batch, parts)
    fulls = [_add_chips(p, r, s_idx, c_idx) for p, r in zip(parts, recv)]
    dproj = _assemble_dproj(duv, dqkv)
    dw_in_t = _mm(dproj, h1, dims="tn", tm=1408, tn=1024, tk=1024, out_dtype=F32, name="mm_dw_in")
    last = [shard_major(dw_in_t)]
    drel, recv_in_a = _rel_bias_grad(ds, np.ascontiguousarray(np.swapaxes(buckets, 1, 2)), last)
    part_in = [_add_halves(g, r, c_idx) for g, r in zip(last, recv_in_a)]
    (dx0, dg1), recv_in = _fused_rows(
        "dh1_in_bwd", 512, [(dproj, w_in_t, "nn", None)], [x, dx1], [g1], _in_bwd_rows,
        [F32], [(1, D_MODEL)], exchange=part_in)
    fulls += [_add_chips(p, r, s_idx, c_idx) for p, r in zip(part_in, recv_in)]
    half_reduced = dict(zip(["w_down", "w_gate", "w_up", "w_out", "w_in"], fulls))

    small = dict(
        loss=loss, norm_mix_pre=dg1, norm_mix_post=dg2, norm_ffn_pre=dg3, norm_ffn_post=dg4,
        ln_v_gain=dln_g, ln_v_bias=dln_b, spatial_w=dw_s, spatial_b=dbz, rel_bias=drel,
        conv_w=dconv_w, conv_b=dconv_b,
    )
    return dx0, small, half_reduced


def _adamw_update(w, g, m, v):
    nm = ADAM_B1 * m + (1.0 - ADAM_B1) * g
    nv = ADAM_B2 * v + (1.0 - ADAM_B2) * (g * g)
    m_hat = nm / (1.0 - ADAM_B1 ** ADAM_STEP)
    v_hat = nv / (1.0 - ADAM_B2 ** ADAM_STEP)
    return -ADAM_LR * (m_hat / (jnp.sqrt(v_hat) + ADAM_EPS) + ADAM_WD * w), nm, nv


def _adamw(w, g, m, v, name):
    rows, cols = w.shape
    tr = next(cand for cand in (256, 176, 128) if rows % cand == 0)

    def body(w_ref, g_ref, m_ref, v_ref, go_ref, d_ref, nm_ref, nv_ref):
        gv = g_ref[...]
        go_ref[...] = gv
        d_ref[...], nm_ref[...], nv_ref[...] = _adamw_update(w_ref[...], gv, m_ref[...], v_ref[...])

    spec = pl.BlockSpec((tr, cols), lambda i: (i, 0))
    sds = jax.ShapeDtypeStruct((rows, cols), F32)
    return pl.pallas_call(
        body, grid=(rows // tr,), in_specs=[spec] * 4, out_specs=[spec] * 4, out_shape=[sds] * 4,
        compiler_params=_params(("parallel",)), name=name,
    )(w, g, m, v)


def _adamw_small(ws, gs, ms, vs):
    n = len(ws)

    def body(*refs):
        w_refs, g_refs, m_refs, v_refs = refs[:n], refs[n:2 * n], refs[2 * n:3 * n], refs[3 * n:4 * n]
        d_refs, nm_refs, nv_refs = refs[4 * n:5 * n], refs[5 * n:6 * n], refs[6 * n:7 * n]
        for t in range(n):
            d_refs[t][...], nm_refs[t][...], nv_refs[t][...] = _adamw_update(
                w_refs[t][...], g_refs[t][...], m_refs[t][...], v_refs[t][...])

    vm = pl.BlockSpec(memory_space=pltpu.VMEM)
    sds = [jax.ShapeDtypeStruct(w.shape, F32) for w in ws]
    res = pl.pallas_call(
        body, in_specs=[vm] * (4 * n), out_specs=[vm] * (3 * n), out_shape=sds * 3,
        compiler_params=_params(), name="adamw_small",
    )(*ws, *gs, *ms, *vs)
    return res[:n], res[n:2 * n], res[2 * n:]


SMALL = ["norm_mix_pre", "norm_mix_post", "norm_ffn_pre", "norm_ffn_post", "ln_v_gain", "ln_v_bias",
         "spatial_w", "spatial_b", "rel_bias", "conv_b"]
LARGE = ["w_in", "w_gate", "w_up", "w_down", "w_out"]
TRANSPOSED = ("w_in", "w_gate", "w_up")
ORDER = ["norm_mix_pre", "norm_mix_post", "norm_ffn_pre", "norm_ffn_post", "w_in", "ln_v_gain", "ln_v_bias",
         "spatial_w", "spatial_b", "rel_bias", "w_out", "w_gate", "w_up", "conv_w", "conv_b", "w_down"]


def kernel(x, norm_mix_pre, norm_mix_post, norm_ffn_pre, norm_ffn_post, w_in, ln_v_gain, ln_v_bias, spatial_w, spatial_b, rel_bias, w_out, w_gate, w_up, conv_w, conv_b, w_down, loss_target, m_norm_mix_pre, m_norm_mix_post, m_norm_ffn_pre, m_norm_ffn_post, m_w_in, m_ln_v_gain, m_ln_v_bias, m_spatial_w, m_spatial_b, m_rel_bias, m_w_out, m_w_gate, m_w_up, m_conv_w, m_conv_b, m_w_down, v_norm_mix_pre, v_norm_mix_post, v_norm_ffn_pre, v_norm_ffn_post, v_w_in, v_ln_v_gain, v_ln_v_bias, v_spatial_w, v_spatial_b, v_rel_bias, v_w_out, v_w_gate, v_w_up, v_conv_w, v_conv_b, v_w_down):
    params = dict(norm_mix_pre=norm_mix_pre, norm_mix_post=norm_mix_post, norm_ffn_pre=norm_ffn_pre,
                  norm_ffn_post=norm_ffn_post, w_in=w_in, ln_v_gain=ln_v_gain, ln_v_bias=ln_v_bias,
                  spatial_w=spatial_w, spatial_b=spatial_b, rel_bias=rel_bias, w_out=w_out, w_gate=w_gate,
                  w_up=w_up, conv_w=conv_w, conv_b=conv_b, w_down=w_down)
    mom = dict(norm_mix_pre=m_norm_mix_pre, norm_mix_post=m_norm_mix_post, norm_ffn_pre=m_norm_ffn_pre,
               norm_ffn_post=m_norm_ffn_post, w_in=m_w_in, ln_v_gain=m_ln_v_gain, ln_v_bias=m_ln_v_bias,
               spatial_w=m_spatial_w, spatial_b=m_spatial_b, rel_bias=m_rel_bias, w_out=m_w_out, w_gate=m_w_gate,
               w_up=m_w_up, conv_w=m_conv_w, conv_b=m_conv_b, w_down=m_w_down)
    var = dict(norm_mix_pre=v_norm_mix_pre, norm_mix_post=v_norm_mix_post, norm_ffn_pre=v_norm_ffn_pre,
               norm_ffn_post=v_norm_ffn_post, w_in=v_w_in, ln_v_gain=v_ln_v_gain, ln_v_bias=v_ln_v_bias,
               spatial_w=v_spatial_w, spatial_b=v_spatial_b, rel_bias=v_rel_bias, w_out=v_w_out, w_gate=v_w_gate,
               w_up=v_w_up, conv_w=v_conv_w, conv_b=v_conv_b, w_down=v_w_down)

    batch = x.shape[0]
    xi, yi, ci = lax.axis_index("x"), lax.axis_index("y"), lax.axis_index("c")
    s_idx = (2 * xi + yi).astype(jnp.int32).reshape(1)
    c_idx = ci.astype(jnp.int32).reshape(1)

    def local(a, n):
        return jnp.swapaxes(a[0], 0, 1) if n in TRANSPOSED else a[0]

    shards = {n: local(params[n], n).astype(BF16) for n in LARGE}
    dx0, partial, half_reduced = _train_step(
        x.reshape(batch * SEQ, D_MODEL), loss_target.reshape(batch * SEQ, D_MODEL),
        norm_mix_pre, norm_mix_post, norm_ffn_pre, norm_ffn_post, shards,
        ln_v_gain.reshape(1, A_WIDTH), ln_v_bias.reshape(1, A_WIDTH), spatial_w[0], spatial_b[0], rel_bias,
        conv_w[0], conv_b, batch, s_idx, c_idx)
    grad_x = dx0.reshape(batch, SEQ, D_MODEL)

    names = list(partial)
    fulls, sums = _finish_reductions([half_reduced[n] for n in LARGE], [partial[n] for n in names])
    reduced = dict(zip(LARGE, fulls))
    total = dict(zip(names, sums))
    loss = total["loss"][0, 0]
    total["spatial_b"] = total["spatial_b"][:, ::HEAD_DIM].T
    total["rel_bias"] = total["rel_bias"].reshape(B_HEADS, NUM_BUCKETS).T
    total["conv_w"] = lax.dynamic_slice_in_dim(total["conv_w"], s_idx[0] * SHARD_FF, SHARD_FF, axis=1)
    small_names = SMALL + ["conv_w"]
    for n in small_names:
        reduced[n] = total[n].reshape(params[n].shape)

    out_g, out_d, out_m, out_v = {}, {}, {}, {}
    for n in LARGE:
        res = _adamw(local(params[n], n), reduced[n], local(mom[n], n), local(var[n], n), name=f"adamw_{n}")
        if n in TRANSPOSED:
            res = [jnp.swapaxes(r, 0, 1) for r in res]
        out_g[n], out_d[n], out_m[n], out_v[n] = [r[None] for r in res]
    d, nm, nv = _adamw_small([params[n] for n in small_names], [reduced[n] for n in small_names],
                             [mom[n] for n in small_names], [var[n] for n in small_names])
    for n, dd, mm, vv in zip(small_names, d, nm, nv):
        out_g[n], out_d[n], out_m[n], out_v[n] = reduced[n], dd, mm, vv

    return (loss, grad_x, *[out_g[n] for n in ORDER], *[out_d[n] for n in ORDER],
            *[out_m[n] for n in ORDER], *[out_v[n] for n in ORDER])
```

```python
import functools
import math

import numpy as np
import jax
import jax.numpy as jnp
from jax import lax
from jax.experimental import pallas as pl
from jax.experimental.pallas import tpu as pltpu

F32 = jnp.float32
BF16 = jnp.bfloat16
MESH = pl.DeviceIdType.MESH

D_MODEL = 1024
SEQ = 2048
HEAD_DIM = 64
A_GROUPS = 4
A_WIDTH = 256
B_HEADS = 12
B_WIDTH = 768
CHUNK = 128
DILATED = ((128, 1), (512, 4), (2048, 16))
NUM_BUCKETS = 32
MAX_DISTANCE = 2048
D_FF = 2816
IN_COLS = 2816
NORM_EPS = 1e-6
NEG_INF = -1e30
N_SHARD = 4
SHARD_FF = D_FF // N_SHARD
LANE_BLOCK = 256
VMEM_LIMIT = 56 * 1024 * 1024

ADAM_LR = 0.001
ADAM_B1 = 0.9
ADAM_B2 = 0.999
ADAM_EPS = 1e-08
ADAM_WD = 0.01
ADAM_STEP = 10

GELU_C = math.sqrt(2.0 / math.pi)
GELU_A = 0.044715

ANY = pl.BlockSpec(memory_space=pl.ANY)


def _params(sem=None):
    return pltpu.CompilerParams(dimension_semantics=sem, vmem_limit_bytes=VMEM_LIMIT)


def _dot(a, b, precision=None):
    return jnp.dot(a, b, preferred_element_type=F32, precision=precision)


def _dot_nt(a, b, precision=None):
    return lax.dot_general(a, b, (((1,), (1,)), ((), ())), preferred_element_type=F32, precision=precision)


def _dot_tn(a, b):
    return lax.dot_general(a, b, (((0,), (0,)), ((), ())), preferred_element_type=F32)


def _gelu(x):
    t = jnp.tanh(x * (GELU_C + (GELU_C * GELU_A) * (x * x)))
    return (0.5 * x) * (1.0 + t)


def _gelu_and_grad(x):
    x2 = x * x
    u = 1.0 + jnp.tanh(x * (GELU_C + (GELU_C * GELU_A) * x2))
    hx = 0.5 * x
    dg = u * (0.5 + hx * (2.0 - u) * (GELU_C + (3.0 * GELU_C * GELU_A) * x2))
    return hx * u, dg


def _mesh_pos():
    x, y, c = lax.axis_index("x"), lax.axis_index("y"), lax.axis_index("c")
    chips = [(1 - x, y), (x, 1 - y), (1 - x, 1 - y)]
    return x, y, c, chips


class _GatherPlan:
    def __init__(self, shapes, shard_refs, out_refs, send_sems, recv_sems):
        self.shapes, self.shard_refs, self.out_refs = shapes, shard_refs, out_refs
        self.send_sems, self.recv_sems = send_sems, recv_sems
        self.x, self.y, self.c, self.chips = _mesh_pos()
        self.sib = (self.x, self.y, 1 - self.c)

    def _half(self, t, chip, which):
        rows = self.shapes[t][0] // 2
        return self.out_refs[t].at[2 * chip[0] + chip[1], pl.ds(which * rows, rows), :]

    def _copy(self, k, src, dst, to):
        return pltpu.make_async_remote_copy(src_ref=src, dst_ref=dst, send_sem=self.send_sems.at[k],
                                            recv_sem=self.recv_sems.at[k], device_id=to, device_id_type=MESH)

    def _sends(self, t):
        rows = self.shapes[t][0] // 2
        src = self.shard_refs[t].at[pl.ds(self.c * rows, rows), :]
        return [self._copy(6 * t + j, src, self._half(t, (self.x, self.y), self.c), (*chip, self.c))
                for j, chip in enumerate(self.chips)]

    def _forwards(self, t):
        return [self._copy(6 * t + 3 + j, self._half(t, chip, self.c), self._half(t, chip, self.c), self.sib)
                for j, chip in enumerate(self.chips)]

    def start(self, ts):
        for t in ts:
            for cp in self._sends(t):
                cp.start()

    def forward(self, ts):
        for t in ts:
            for j, chip in enumerate(self.chips):
                landed = self._half(t, chip, self.c)
                self._copy(6 * t + j, landed, landed, (*chip, self.c)).wait_recv()
            for cp in self._forwards(t):
                cp.start()

    def finish(self, ts):
        for t in ts:
            for j, chip in enumerate(self.chips):
                other = self._half(t, chip, 1 - self.c)
                self._copy(6 * t + 3 + j, other, other, self.sib).wait_recv()
        for t in ts:
            for cp in self._sends(t) + self._forwards(t):
                cp.wait_send()


class _SiblingExchangePlan:
    def __init__(self, shapes, grad_refs, out_refs, send_sems, recv_sems):
        self.shapes, self.grad_refs, self.out_refs = shapes, grad_refs, out_refs
        self.send_sems, self.recv_sems = send_sems, recv_sems
        self.x, self.y, self.c, _ = _mesh_pos()

    def _copies(self):
        out = []
        for t, (g, o) in enumerate(zip(self.grad_refs, self.out_refs)):
            rows = self.shapes[t][1] // 2
            out.append(pltpu.make_async_remote_copy(
                src_ref=g.at[:, pl.ds((1 - self.c) * rows, rows), :], dst_ref=o, send_sem=self.send_sems.at[t],
                recv_sem=self.recv_sems.at[t], device_id=(self.x, self.y, 1 - self.c), device_id_type=MESH))
        return out

    def start(self):
        for cp in self._copies():
            cp.start()

    def finish(self):
        for cp in self._copies():
            cp.wait()


class _ChipExchangePlan:
    def __init__(self, part_refs, out_refs, send_sems, recv_sems):
        self.part_refs, self.out_refs, self.send_sems, self.recv_sems = part_refs, out_refs, send_sems, recv_sems
        _, _, self.c, self.chips = _mesh_pos()

    def _copies(self):
        return [pltpu.make_async_remote_copy(
            src_ref=p.at[2 * chip[0] + chip[1]], dst_ref=o.at[j], send_sem=self.send_sems.at[3 * t + j],
            recv_sem=self.recv_sems.at[3 * t + j], device_id=(*chip, self.c), device_id_type=MESH)
            for t, (p, o) in enumerate(zip(self.part_refs, self.out_refs)) for j, chip in enumerate(self.chips)]

    def start(self):
        for cp in self._copies():
            cp.start()

    def finish(self):
        for cp in self._copies():
            cp.wait()


def _sem_pair(n):
    return [pltpu.SemaphoreType.DMA((n,)), pltpu.SemaphoreType.DMA((n,))]


def _mm(a, b, *, dims, tm, tn, tk, out_dtype, name):
    if dims == "nn":
        m, k = a.shape
        n = b.shape[1]
        a_spec = pl.BlockSpec((tm, tk), lambda i, j, kk: (i, kk))
        b_spec = pl.BlockSpec((tk, tn), lambda i, j, kk: (kk, j))
        dot = _dot
    elif dims == "nt":
        m, k = a.shape
        n = b.shape[0]
        a_spec = pl.BlockSpec((tm, tk), lambda i, j, kk: (i, kk))
        b_spec = pl.BlockSpec((tn, tk), lambda i, j, kk: (j, kk))
        dot = _dot_nt
    else:
        k, m = a.shape
        n = b.shape[1]
        a_spec = pl.BlockSpec((tk, tm), lambda i, j, kk: (kk, i))
        b_spec = pl.BlockSpec((tk, tn), lambda i, j, kk: (kk, j))
        dot = _dot_tn
    assert m % tm == 0 and n % tn == 0 and k % tk == 0, (name, m, n, k)
    grid = (m // tm, n // tn, k // tk)
    nk = grid[2]
    assert nk == 1 or out_dtype == F32, name

    def body(a_ref, b_ref, o_ref):
        prod = dot(a_ref[...].astype(BF16), b_ref[...].astype(BF16))
        if nk == 1:
            o_ref[...] = prod.astype(out_dtype)
        else:
            kk = pl.program_id(2)

            @pl.when(kk == 0)
            def _():
                o_ref[...] = prod

            @pl.when(kk > 0)
            def _():
                o_ref[...] += prod

    return pl.pallas_call(
        body, grid=grid, in_specs=[a_spec, b_spec],
        out_specs=pl.BlockSpec((tm, tn), lambda i, j, kk: (i, j)),
        out_shape=jax.ShapeDtypeStruct((m, n), out_dtype),
        compiler_params=_params(("parallel", "parallel", "arbitrary")), name=name,
    )(a, b)


def _fused_rows(name, tm, mats, rows, vecs, fn, row_outs, acc_outs, exchange=()):
    m = mats[0][0].shape[0]
    nm, nr, nv, nro, nao, nx = len(mats), len(rows), len(vecs), len(row_outs), len(acc_outs), len(exchange)
    n_steps = m // tm

    def body(*refs):
        a_refs, w_refs = refs[:nm], refs[nm:2 * nm]
        pos = 2 * nm
        row_refs, vec_refs, part_refs = refs[pos:pos + nr], refs[pos + nr:pos + nr + nv], refs[pos + nr + nv:pos + nr + nv + nx]
        pos += nr + nv + nx
        out_refs, acc_refs, recv_refs = refs[pos:pos + nro], refs[pos + nro:pos + nro + nao], refs[pos + nro + nao:pos + nro + nao + nx]
        sems = refs[pos + nro + nao + nx:]
        i = pl.program_id(0)
        if nx:
            plan = _ChipExchangePlan(part_refs, recv_refs, *sems)

            @pl.when(i == 0)
            def _():
                plan.start()

        @pl.when(i == 0)
        def _():
            for r in acc_refs:
                r[...] = jnp.zeros_like(r)

        y = None
        for a_ref, w_ref, (_, _, dims, sl) in zip(a_refs, w_refs, mats):
            w = w_ref[...] if sl is None else w_ref[sl, :]
            part = (_dot if dims == "nn" else _dot_nt)(a_ref[...], w)
            y = part if y is None else y + part
        res = fn(y, *[r[...] for r in row_refs], *[v[...] for v in vec_refs])
        for r, val in zip(out_refs, res[:nro]):
            r[...] = val.astype(r.dtype)
        for r, val in zip(acc_refs, res[nro:]):
            r[...] += val

        if nx:
            @pl.when(i == n_steps - 1)
            def _():
                plan.finish()

    tile = lambda width: pl.BlockSpec((tm, width), lambda i: (i, 0))
    res = pl.pallas_call(
        body, grid=(n_steps,),
        in_specs=[tile(a.shape[1]) for a, _, _, _ in mats] + [_full_spec(w.shape) for _, w, _, _ in mats]
        + [tile(D_MODEL)] * nr + [_full_spec((1, D_MODEL))] * nv + [ANY] * nx,
        out_specs=[tile(D_MODEL)] * nro + [_full_spec(s) for s in acc_outs] + [ANY] * nx,
        out_shape=[jax.ShapeDtypeStruct((m, D_MODEL), dt) for dt in row_outs]
        + [jax.ShapeDtypeStruct(s, F32) for s in acc_outs]
        + [jax.ShapeDtypeStruct((3,) + p.shape[1:], p.dtype) for p in exchange],
        scratch_shapes=_sem_pair(3 * nx) if nx else [],
        compiler_params=_params(("arbitrary",)), name=name,
    )(*[a for a, _, _, _ in mats], *[w for _, w, _, _ in mats], *rows, *vecs, *exchange)
    return list(res[:nro + nao]), list(res[nro + nao:])


ROW_TILE = 512


def _vec_spec(width=D_MODEL):
    return pl.BlockSpec((1, width), lambda i: (0, 0))


def _rstd(v):
    return lax.rsqrt(jnp.mean(v * v, axis=-1, keepdims=True) + NORM_EPS)


def _mid_fwd_rows(y1, x0, g2, g3):
    x1 = x0 + y1 * _rstd(y1) * g2
    return y1, x1, x1 * _rstd(x1) * g3


def _rms_bwd_rows(dout, v, g):
    r = _rstd(v)
    n = v * r
    dn = dout * g
    dv = r * (dn - n * jnp.mean(dn * n, axis=-1, keepdims=True))
    dg = jnp.sum(dout * n, axis=0, keepdims=True)
    return dv, dg


def _loss_head_rows(y2, x1, tgt, g4):
    x2 = x1 + y2 * _rstd(y2) * g4
    err = x2 - tgt
    loss = 0.5 * jnp.sum(jnp.mean(err * err, axis=-1, keepdims=True), axis=0, keepdims=True)
    dx2 = err * (1.0 / D_MODEL)
    dy2, dg4 = _rms_bwd_rows(dx2, y2, g4)
    return dx2, dy2, dg4, loss


def _mid_bwd_rows(dh2, x1, y1, dx2, g2, g3):
    d3, dg3 = _rms_bwd_rows(dh2, x1, g3)
    dx1 = dx2 + d3
    dy1, dg2 = _rms_bwd_rows(dx1, y1, g2)
    return dx1, dy1, dg2, dg3


def _in_bwd_rows(dh1, x0, dx1, g1):
    d1, dg1 = _rms_bwd_rows(dh1, x0, g1)
    return dx1 + d1, dg1


GATE_ROWS = 512


def _group_mean_matrix():
    p = np.zeros((A_WIDTH, A_WIDTH), np.float32)
    for g in range(A_GROUPS):
        p[g * HEAD_DIM:(g + 1) * HEAD_DIM, g * HEAD_DIM:(g + 1) * HEAD_DIM] = 1.0 / HEAD_DIM
    return jnp.asarray(p)


def _group_masks(width=A_WIDTH):
    lane = lax.broadcasted_iota(jnp.int32, (1, width), 1)
    return [(lane >= g * HEAD_DIM) & (lane < (g + 1) * HEAD_DIM) for g in range(width // HEAD_DIM)]


GROUP_SUM_PRECISION = lax.Precision.HIGH


def _layernorm_groups(vg, pavg):
    hi = GROUP_SUM_PRECISION
    mu = _dot(vg, pavg, hi)
    xc = vg - mu
    var = _dot(xc * xc, pavg, hi)
    rstd = lax.rsqrt(var + NORM_EPS)
    return xc * rstd, rstd


def _spatial_mix(w_bf, vn_chunk_bf, masks, bz):
    z = bz
    for g in range(A_GROUPS):
        z = z + jnp.where(masks[g], _dot(w_bf[g], vn_chunk_bf), 0.0)
    return z


def _full_spec(shape):
    return pl.BlockSpec(shape, lambda i: tuple(0 for _ in shape))


def _gate_fwd(uv, ln_g, ln_b, w_s, bz):
    m = uv.shape[0]
    pavg = _group_mean_matrix()

    def body(u_ref, v_ref, lg_ref, lb_ref, w_ref, bz_ref, p_ref, a_ref):
        masks = _group_masks()
        row = lax.broadcasted_iota(jnp.int32, (CHUNK, CHUNK), 0)
        col = lax.broadcasted_iota(jnp.int32, (CHUNK, CHUNK), 1)
        w_bf = [jnp.where(row >= col, w_ref[g], 0.0).astype(BF16) for g in range(A_GROUPS)]
        ug = _gelu(u_ref[...])
        vhat, _ = _layernorm_groups(_gelu(v_ref[...]), p_ref[...])
        vn = vhat * lg_ref[...] + lb_ref[...]
        bz = bz_ref[...]
        for c in range(GATE_ROWS // CHUNK):
            sl = slice(c * CHUNK, (c + 1) * CHUNK)
            z = _spatial_mix(w_bf, vn[sl].astype(BF16), masks, bz)
            a_ref[sl, :] = (ug[sl] * z).astype(BF16)

    return pl.pallas_call(
        body, grid=(m // GATE_ROWS,),
        in_specs=[pl.BlockSpec((GATE_ROWS, A_WIDTH), lambda i: (i, 0)),
                  pl.BlockSpec((GATE_ROWS, A_WIDTH), lambda i: (i, 1)),
                  _full_spec((1, A_WIDTH)), _full_spec((1, A_WIDTH)), _full_spec((A_GROUPS, CHUNK, CHUNK)),
                  _full_spec((CHUNK, A_WIDTH)), _full_spec((A_WIDTH, A_WIDTH))],
        out_specs=pl.BlockSpec((GATE_ROWS, A_WIDTH), lambda i: (i, 0)),
        out_shape=jax.ShapeDtypeStruct((m, A_WIDTH), BF16),
        compiler_params=_params(("parallel",)), name="gate_fwd",
    )(uv, uv, ln_g, ln_b, w_s, bz, pavg)


def _gate_bwd(uv, dmix, ln_g, ln_b, w_s, w_st, bz, grads):
    m = uv.shape[0]
    pavg = _group_mean_matrix()
    nsteps = m // GATE_ROWS
    nx = len(grads)
    shapes = [g.shape for g in grads]

    def body(u_ref, v_ref, da_ref, lg_ref, lb_ref, w_ref, wt_ref, bz_ref, p_ref, *rest):
        grad_refs = rest[:nx]
        duv_ref, dlg_ref, dlb_ref, dw_ref, dbz_ref = rest[nx:nx + 5]
        recv_refs = rest[nx + 5:2 * nx + 5]
        exchange = _SiblingExchangePlan(shapes, grad_refs, recv_refs, *rest[2 * nx + 5:])
        i = pl.program_id(0)

        @pl.when(i == 0)
        def _():
            exchange.start()
            dlg_ref[...] = jnp.zeros_like(dlg_ref)
            dlb_ref[...] = jnp.zeros_like(dlb_ref)
            dw_ref[...] = jnp.zeros_like(dw_ref)
            dbz_ref[...] = jnp.zeros_like(dbz_ref)

        hi = GROUP_SUM_PRECISION
        masks = _group_masks()
        row = lax.broadcasted_iota(jnp.int32, (CHUNK, CHUNK), 0)
        col = lax.broadcasted_iota(jnp.int32, (CHUNK, CHUNK), 1)
        tril = row >= col
        w_bf = [jnp.where(tril, w_ref[g], 0.0).astype(BF16) for g in range(A_GROUPS)]
        wt_bf = [jnp.where(col >= row, wt_ref[g], 0.0).astype(BF16) for g in range(A_GROUPS)]
        pavg_v = p_ref[...]
        lg = lg_ref[...]
        ug, dug = _gelu_and_grad(u_ref[...])
        vg, dvg_dx = _gelu_and_grad(v_ref[...])
        vhat, rstd = _layernorm_groups(vg, pavg_v)
        vn = vhat * lg + lb_ref[...]
        da = da_ref[...]
        bz = bz_ref[...]
        for c in range(GATE_ROWS // CHUNK):
            sl = slice(c * CHUNK, (c + 1) * CHUNK)
            vn_bf = vn[sl].astype(BF16)
            z = _spatial_mix(w_bf, vn_bf, masks, bz)
            dz = da[sl] * ug[sl]
            duv_ref[sl, 0:A_WIDTH] = (da[sl] * z * dug[sl]).astype(BF16)
            dbz_ref[...] += dz
            dz_bf = dz.astype(BF16)
            dvn = jnp.zeros((CHUNK, A_WIDTH), F32)
            for g in range(A_GROUPS):
                dz_g = jnp.where(masks[g], dz, 0.0).astype(BF16)
                dw_ref[g] += jnp.where(tril, _dot_nt(dz_g, vn_bf), 0.0)
                dvn = dvn + jnp.where(masks[g], _dot(wt_bf[g], dz_bf), 0.0)
            vh = vhat[sl]
            dlb_ref[...] += jnp.sum(dvn, axis=0, keepdims=True)
            dlg_ref[...] += jnp.sum(dvn * vh, axis=0, keepdims=True)
            dvh = dvn * lg
            m1 = _dot(dvh, pavg_v, hi)
            m2 = _dot(dvh * vh, pavg_v, hi)
            duv_ref[sl, A_WIDTH:2 * A_WIDTH] = (rstd[sl] * (dvh - m1 - vh * m2) * dvg_dx[sl]).astype(BF16)

        @pl.when(i == nsteps - 1)
        def _():
            dbz_ref[...] = _dot(dbz_ref[...], pavg_v * float(HEAD_DIM), hi)
            exchange.finish()

    res = pl.pallas_call(
        body, grid=(nsteps,),
        in_specs=[pl.BlockSpec((GATE_ROWS, A_WIDTH), lambda i: (i, 0)),
                  pl.BlockSpec((GATE_ROWS, A_WIDTH), lambda i: (i, 1)),
                  pl.BlockSpec((GATE_ROWS, A_WIDTH), lambda i: (i, 0)),
                  _full_spec((1, A_WIDTH)), _full_spec((1, A_WIDTH)), _full_spec((A_GROUPS, CHUNK, CHUNK)),
                  _full_spec((A_GROUPS, CHUNK, CHUNK)), _full_spec((CHUNK, A_WIDTH)),
                  _full_spec((A_WIDTH, A_WIDTH))] + [ANY] * nx,
        out_specs=[pl.BlockSpec((GATE_ROWS, 2 * A_WIDTH), lambda i: (i, 0)),
                   _full_spec((1, A_WIDTH)), _full_spec((1, A_WIDTH)), _full_spec((A_GROUPS, CHUNK, CHUNK)),
                   _full_spec((CHUNK, A_WIDTH))] + [ANY] * nx,
        out_shape=[jax.ShapeDtypeStruct((m, IN_COLS), BF16),
                   jax.ShapeDtypeStruct((1, A_WIDTH), F32), jax.ShapeDtypeStruct((1, A_WIDTH), F32),
                   jax.ShapeDtypeStruct((A_GROUPS, CHUNK, CHUNK), F32),
                   jax.ShapeDtypeStruct((CHUNK, A_WIDTH), F32)]
        + [jax.ShapeDtypeStruct((N_SHARD, s[1] // 2, s[2]), F32) for s in shapes],
        scratch_shapes=_sem_pair(nx),
        compiler_params=_params(("arbitrary",)), name="gate_bwd",
    )(uv, uv, dmix, ln_g, ln_b, w_s, w_st, bz, pavg, *grads)
    return res[:5], list(res[5:])


Q_BLOCK = 128
PAIR = 2 * HEAD_DIM
N_PAIR = B_HEADS // 2
N_CFG = len(DILATED)
BLOCKS_PER_CFG = SEQ // Q_BLOCK
QKV_SLABS = 3 * N_PAIR
FWD_BLOCKS_PER_TRIP = 8
BWD_BLOCKS_PER_TRIP = 4


def _t5_bucket_np(dist, dtype):
    max_exact = NUM_BUCKETS // 2
    d = np.maximum(dist, 1).astype(dtype)
    large = max_exact + (np.log(d / dtype(max_exact)) / dtype(math.log(MAX_DISTANCE / max_exact))
                         * dtype(NUM_BUCKETS - max_exact))
    large = np.minimum(large.astype(np.int32), NUM_BUCKETS - 1)
    return np.where(dist < max_exact, dist, large)


def _bucket_tables():
    i = np.arange(Q_BLOCK)[:, None]
    j = np.arange(Q_BLOCK)[None, :]
    tables = []
    for _, dil in DILATED:
        rel_prev = Q_BLOCK + i - j
        rel_cur = i - j
        rel = np.concatenate([rel_prev, rel_cur], axis=1)
        valid = np.concatenate([rel_prev <= Q_BLOCK, rel_cur >= 0], axis=1)
        dist = np.maximum(rel, 0) * dil
        b32 = _t5_bucket_np(dist, np.float32)
        b64 = _t5_bucket_np(dist, np.float64)
        assert np.array_equal(b32, b64)
        tables.append(np.where(valid, b32, -1).astype(np.int32))
    return np.stack(tables)


def _present_buckets(buckets_np):
    return [sorted(set(int(v) for v in np.unique(buckets_np[c]) if v >= 0)) for c in range(N_CFG)]


def _bias_tables_body(buckets_np):
    present = _present_buckets(buckets_np)

    def tables(rb_ref, bk_ref, o_ref, ot_ref):
        for c in range(N_CFG):
            bk = bk_ref[c]
            for h in range(B_HEADS):
                acc = jnp.full((Q_BLOCK, 2 * Q_BLOCK), NEG_INF, F32)
                for b in present[c]:
                    acc = jnp.where(bk == b, rb_ref[b, h], acc)
                o_ref[c, h] = acc
                ot_ref[c, h] = acc.T

    return tables


def _proj_fwd(x, g1, w_in_t):
    m = x.shape[0]
    tm = ROW_TILE

    def body(x_ref, g_ref, w_ref, h_ref, uv_ref, qkv_ref):
        xv = x_ref[...]
        h = (xv * _rstd(xv) * g_ref[...]).astype(BF16)
        h_ref[...] = h
        acc = _dot_nt(h, w_ref[...])
        uv_ref[...] = acc[:, :2 * A_WIDTH]
        for s in range(QKV_SLABS):
            qkv_ref[s] = acc[:, 2 * A_WIDTH + s * PAIR:2 * A_WIDTH + (s + 1) * PAIR]

    return pl.pallas_call(
        body, grid=(m // tm,),
        in_specs=[pl.BlockSpec((tm, D_MODEL), lambda i: (i, 0)), _vec_spec(),
                  pl.BlockSpec((IN_COLS, D_MODEL), lambda i: (0, 0))],
        out_specs=[pl.BlockSpec((tm, D_MODEL), lambda i: (i, 0)),
                   pl.BlockSpec((tm, 2 * A_WIDTH), lambda i: (i, 0)),
                   pl.BlockSpec((QKV_SLABS, tm, PAIR), lambda i: (0, i, 0))],
        out_shape=[jax.ShapeDtypeStruct((m, D_MODEL), BF16), jax.ShapeDtypeStruct((m, 2 * A_WIDTH), F32),
                   jax.ShapeDtypeStruct((QKV_SLABS, m, PAIR), F32)],
        compiler_params=_params(("parallel",)), name="proj_fwd",
    )(x, g1, w_in_t)


def _pair_masks():
    lane = lax.broadcasted_iota(jnp.int32, (1, PAIR), 1)
    return [lane < HEAD_DIM, lane >= HEAD_DIM]


def _block_rows(idx, dil):
    static = isinstance(idx, int)
    r, n = idx % dil, idx // dil

    def rows_of(block):
        start = r + (dil * Q_BLOCK) * block
        if dil == 1:
            return pl.ds(start if static else pl.multiple_of(start, Q_BLOCK), Q_BLOCK)
        return pl.ds(start, Q_BLOCK, stride=dil)

    prev = rows_of(n - 1) if not static or n > 0 else None
    return rows_of(n), prev


def _attn_fwd(qkv, bias, batch, shards):
    m = qkv.shape[1]
    comb_rows = 256
    nt = len(shards)
    shapes = [sh.shape for sh in shards]
    n_steps = batch * N_PAIR
    early, late = list(range(nt // 2)), list(range(nt // 2, nt))

    def body(q_ref, k_ref, v_ref, b_ref, *rest):
        shard_refs = rest[:nt]
        o_ref, l_ref = rest[nt:nt + 2]
        gat_refs = rest[nt + 2:2 * nt + 2]
        scratch = rest[2 * nt + 2:]
        oc_refs, lc_refs = scratch[:N_CFG], scratch[N_CFG:2 * N_CFG]
        step = pl.program_id(0) * N_PAIR + pl.program_id(1)
        gather = _GatherPlan(shapes, shard_refs, gat_refs, *scratch[2 * N_CFG:])

        @pl.when(step == 0)
        def _():
            gather.start(early + late)

        @pl.when(step == n_steps // 2)
        def _():
            gather.forward(early)

        @pl.when(step == n_steps - 2)
        def _():
            gather.forward(late)

        masks = _pair_masks()
        for ci, (_, dil) in enumerate(DILATED):
            nb = SEQ // dil // Q_BLOCK

            def block(trip, ci=ci, dil=dil, nb=nb):
                work = []
                for u in range(FWD_BLOCKS_PER_TRIP):
                    rows, prow = _block_rows(trip * FWD_BLOCKS_PER_TRIP + u, dil)
                    has_prev = nb > 1 and prow is not None
                    q = q_ref[rows, :] * 0.125
                    kc = k_ref[rows, :].astype(BF16)
                    vc = v_ref[rows, :]
                    kp = k_ref[prow, :].astype(BF16) if has_prev else None
                    vp = v_ref[prow, :] if has_prev else None
                    tiles = []
                    for h in range(2):
                        qh = jnp.where(masks[h], q, 0.0).astype(BF16)
                        sc = _dot_nt(qh, kc) + b_ref[ci, h, :, Q_BLOCK:]
                        sp = _dot_nt(qh, kp) + b_ref[ci, h, :, :Q_BLOCK] if has_prev else None
                        tiles.append((sc, sp))
                    work.append((rows, vc, vp, tiles))
                probs = []
                for _, _, _, tiles in work:
                    ps = []
                    for sc, sp in tiles:
                        mx = jnp.max(sc if sp is None else jnp.maximum(sc, sp), axis=1, keepdims=True)
                        pc = jnp.exp(sc - mx).astype(BF16)
                        pp = None if sp is None else jnp.exp(sp - mx).astype(BF16)
                        ps.append((mx, pc, pp))
                    probs.append(ps)
                for (rows, vc, vp, _), ps in zip(work, probs):
                    res = []
                    for h, (_, pc, pp) in enumerate(ps):
                        r = _dot(pc, jnp.where(masks[h], vc, 1.0).astype(BF16))
                        if pp is not None:
                            r = r + _dot(pp, jnp.where(masks[h], vp, 1.0).astype(BF16))
                        res.append(r)
                    num = jnp.where(masks[0], res[0], res[1])
                    den = pltpu.roll(jnp.where(masks[0], res[1], res[0]), HEAD_DIM, 1)
                    oc_refs[ci][rows, :] = num / den
                    lc_refs[ci][rows, :] = jnp.where(masks[0], ps[0][0], ps[1][0]) + jnp.log(den)

            for trip in range(BLOCKS_PER_CFG // FWD_BLOCKS_PER_TRIP):
                block(trip)

        def combine(i, carry):
            rr = pl.ds(pl.multiple_of(i * comb_rows, comb_rows), comb_rows)
            ls = [lc_refs[c][rr, :] for c in range(N_CFG)]
            mx = functools.reduce(jnp.maximum, ls)
            ws = [jnp.exp(l - mx) for l in ls]
            tot = functools.reduce(lambda a, b: a + b, ws)
            o = functools.reduce(lambda a, b: a + b, [ws[c] * oc_refs[c][rr, :] for c in range(N_CFG)]) / tot
            o_ref[rr, :] = o.astype(BF16)
            l_ref[rr, :] = mx + jnp.log(tot)
            return carry

        lax.fori_loop(0, SEQ // comb_rows, combine, 0)

        @pl.when(step == n_steps - 1)
        def _():
            gather.finish(early + late)

    def slab(first):
        return pl.BlockSpec((None, SEQ, PAIR), lambda b, p: (first + p, b, 0))

    nat = pl.BlockSpec((SEQ, PAIR), lambda b, p: (b, p))
    res = pl.pallas_call(
        body, grid=(batch, N_PAIR),
        in_specs=[slab(0), slab(N_PAIR), slab(2 * N_PAIR),
                  pl.BlockSpec((N_CFG, 2, Q_BLOCK, 2 * Q_BLOCK), lambda b, p: (0, p, 0, 0))] + [ANY] * nt,
        out_specs=[nat, nat] + [ANY] * nt,
        out_shape=[jax.ShapeDtypeStruct((m, B_WIDTH), BF16), jax.ShapeDtypeStruct((m, B_WIDTH), F32)]
        + [jax.ShapeDtypeStruct((N_SHARD,) + sh.shape, sh.dtype) for sh in shards],
        scratch_shapes=[pltpu.VMEM((SEQ, PAIR), F32)] * (2 * N_CFG) + _sem_pair(6 * nt),
        compiler_params=_params(("arbitrary", "arbitrary")), name="attn_fwd",
    )(qkv, qkv, qkv, bias, *shards)
    return res[0], res[1], list(res[2:])


def _attn_bwd(qkv, dmix, o, lse, bias_t, dproj, batch, parts):
    m = qkv.shape[1]
    nt = len(parts)
    n_steps = N_PAIR * batch

    def body(q_ref, k_ref, v_ref, do_ref, o_ref, l_ref, b_ref, *rest):
        part_refs = rest[1:nt + 1]
        dproj_ref, ds_ref = rest[nt + 1:nt + 3]
        recv_refs = rest[nt + 3:2 * nt + 3]
        dq_acc, dk_acc, dv_acc, d_scr, stage, stage_sems, send_sems, recv_sems = rest[2 * nt + 3:]
        pair, seq = pl.program_id(0), pl.program_id(1)
        step = pair * batch + seq
        exchange = _ChipExchangePlan(part_refs, recv_refs, send_sems, recv_sems)

        def stage_copies():
            rows = pl.ds(pl.multiple_of(seq * SEQ, SEQ), SEQ)
            return [pltpu.make_async_copy(
                stage.at[k],
                dproj_ref.at[rows, pl.ds(pl.multiple_of(2 * A_WIDTH + k * B_WIDTH + pair * PAIR, PAIR), PAIR)],
                stage_sems.at[k]) for k in range(3)]

        @pl.when(step == 0)
        def _():
            exchange.start()

        @pl.when(pl.program_id(1) == 0)
        def _():
            ds_ref[...] = jnp.zeros_like(ds_ref)

        dq_acc[...] = jnp.zeros_like(dq_acc)
        dk_acc[...] = jnp.zeros_like(dk_acc)
        dv_acc[...] = jnp.zeros_like(dv_acc)
        d_scr[...] = do_ref[...] * o_ref[...].astype(F32)
        masks = _pair_masks()

        def stack_heads(t):
            return jnp.concatenate([jnp.where(masks[0], t, 0.0), jnp.where(masks[1], t, 0.0)], axis=0).astype(BF16)

        for ci, (_, dil) in enumerate(DILATED):
            nb = SEQ // dil // Q_BLOCK

            def block(trip, carry, ci=ci, dil=dil, nb=nb):
                first = []
                for u in range(BWD_BLOCKS_PER_TRIP):
                    rows, prow = _block_rows(trip * BWD_BLOCKS_PER_TRIP + u, dil)
                    has_prev = nb > 1 and prow is not None
                    if has_prev:
                        kcat = jnp.concatenate([k_ref[prow, :], k_ref[rows, :]], axis=0).astype(BF16)
                        vcat = jnp.concatenate([v_ref[prow, :], v_ref[rows, :]], axis=0).astype(BF16)
                    else:
                        kcat = k_ref[rows, :].astype(BF16)
                        vcat = v_ref[rows, :].astype(BF16)
                    qst = stack_heads(q_ref[rows, :] * 0.125)
                    dost = stack_heads(do_ref[rows, :])
                    lt = l_ref[rows, :].T
                    dt = d_scr[rows, :].T
                    lrow = jnp.concatenate([lt[0:1], lt[HEAD_DIM:HEAD_DIM + 1]], axis=1)
                    drow = jnp.concatenate([jnp.sum(dt[:HEAD_DIM], axis=0, keepdims=True),
                                            jnp.sum(dt[HEAD_DIM:], axis=0, keepdims=True)], axis=1)
                    first.append((has_prev, rows, prow, kcat, qst, dost, lrow, drow,
                                  _dot_nt(kcat, qst), _dot_nt(vcat, dost)))
                second = []
                for has_prev, rows, prow, kcat, qst, dost, lrow, drow, st, dpt in first:
                    keys = slice(0, 2 * Q_BLOCK) if has_prev else slice(Q_BLOCK, 2 * Q_BLOCK)
                    bt = jnp.concatenate([b_ref[ci, 0, keys, :], b_ref[ci, 1, keys, :]], axis=1)
                    pt = jnp.exp(st + bt - lrow)
                    dst = pt * (dpt - drow)
                    ds_ref[ci, 0, keys, :] += dst[:, :Q_BLOCK]
                    ds_ref[ci, 1, keys, :] += dst[:, Q_BLOCK:]
                    second.append((has_prev, rows, prow, kcat, qst, dost, pt.astype(BF16), dst.astype(BF16)))
                for has_prev, rows, prow, kcat, qst, dost, pt_bf, dst_bf in second:
                    dk = _dot(dst_bf, qst)
                    dv = _dot(pt_bf, dost)
                    dq2 = _dot_tn(dst_bf, kcat)
                    dq_acc[rows, :] += jnp.where(masks[0], dq2[:Q_BLOCK], dq2[Q_BLOCK:]) * 0.125
                    if has_prev:
                        dk_acc[prow, :] += dk[:Q_BLOCK]
                        dv_acc[prow, :] += dv[:Q_BLOCK]
                        dk_acc[rows, :] += dk[Q_BLOCK:]
                        dv_acc[rows, :] += dv[Q_BLOCK:]
                    else:
                        dk_acc[rows, :] += dk
                        dv_acc[rows, :] += dv
                return carry

            block(0, 0)
            lax.fori_loop(1, BLOCKS_PER_CFG // BWD_BLOCKS_PER_TRIP, block, 0)

        @pl.when(step > 0)
        def _():
            for cp in stage_copies():
                cp.wait()

        stage[0] = dq_acc[...].astype(BF16)
        stage[1] = dk_acc[...].astype(BF16)
        stage[2] = dv_acc[...].astype(BF16)
        for cp in stage_copies():
            cp.start()

        @pl.when(step == n_steps - 1)
        def _():
            for cp in stage_copies():
                cp.wait()
            exchange.finish()

    def slab(first):
        return pl.BlockSpec((None, SEQ, PAIR), lambda p, b: (first + p, b, 0))

    nat = pl.BlockSpec((SEQ, PAIR), lambda p, b: (b, p))
    tbl = pl.BlockSpec((N_CFG, 2, 2 * Q_BLOCK, Q_BLOCK), lambda p, b: (0, p, 0, 0))
    acc = pltpu.VMEM((SEQ, PAIR), F32)
    res = pl.pallas_call(
        body, grid=(N_PAIR, batch),
        in_specs=[slab(0), slab(N_PAIR), slab(2 * N_PAIR),
                  pl.BlockSpec((SEQ, PAIR), lambda p, b: (b, A_WIDTH // PAIR + p)), nat, nat, tbl] + [ANY] * (nt + 1),
        out_specs=[ANY, tbl] + [ANY] * nt,
        out_shape=[jax.ShapeDtypeStruct(dproj.shape, dproj.dtype),
                   jax.ShapeDtypeStruct((N_CFG, B_HEADS, 2 * Q_BLOCK, Q_BLOCK), F32)]
        + [jax.ShapeDtypeStruct((3,) + p.shape[1:], p.dtype) for p in parts],
        input_output_aliases={7: 0},
        scratch_shapes=[acc, acc, acc, acc, pltpu.VMEM((3, SEQ, PAIR), BF16), pltpu.SemaphoreType.DMA((3,))]
        + _sem_pair(3 * nt),
        compiler_params=_params(("arbitrary", "arbitrary")), name="attn_bwd",
    )(qkv, qkv, qkv, dmix, o, lse, bias_t, dproj, *parts)
    return res[0], res[1], list(res[2:])


def _rel_bias_grad(ds, buckets_np, grads):
    present = _present_buckets(buckets_np)
    nx = len(grads)
    shapes = [g.shape for g in grads]

    def body(bk_ref, ds_ref, *rest):
        o_ref = rest[nx]
        acc_ref = rest[2 * nx + 1]
        exchange = _SiblingExchangePlan(shapes, rest[:nx], rest[nx + 1:2 * nx + 1], *rest[2 * nx + 2:])
        exchange.start()
        acc_ref[...] = jnp.zeros_like(acc_ref)
        for c in range(N_CFG):
            bk = bk_ref[c]
            for h in range(B_HEADS):
                dsv = ds_ref[c, h]
                for b in present[c]:
                    part = jnp.sum(jnp.where(bk == b, dsv, 0.0), axis=0, keepdims=True)
                    acc_ref[pl.ds(h * NUM_BUCKETS + b, 1), :] += part
        o_ref[...] = jnp.sum(acc_ref[...], axis=1, keepdims=True)
        exchange.finish()

    vm = pl.BlockSpec(memory_space=pltpu.VMEM)
    res = pl.pallas_call(
        body, in_specs=[vm, vm] + [ANY] * nx, out_specs=[vm] + [ANY] * nx,
        out_shape=[jax.ShapeDtypeStruct((B_HEADS * NUM_BUCKETS, 1), F32)]
        + [jax.ShapeDtypeStruct((N_SHARD, s[1] // 2, s[2]), F32) for s in shapes],
        scratch_shapes=[pltpu.VMEM((B_HEADS * NUM_BUCKETS, buckets_np.shape[-1]), F32)] + _sem_pair(nx),
        compiler_params=_params(), name="rel_bias_grad",
    )(jnp.asarray(buckets_np), ds, *grads)
    return res[0], list(res[1:])


def _row_index():
    return lax.broadcasted_iota(jnp.int32, (SEQ, LANE_BLOCK), 0)


def _shift_down(x, k, row):
    return jnp.where(row >= k, pltpu.roll(x, k, 0), 0.0)


def _shift_up(x, k, row):
    return jnp.where(row < SEQ - k, pltpu.roll(x, SEQ - k, 0), 0.0)


def _convgate_fwd(gate, up, conv_w, conv_b, batch, shard):
    m = gate.shape[0]
    n_col = D_FF // LANE_BLOCK
    n_steps = batch * n_col

    def body(g_ref, u_ref, w_ref, b_ref, shard_ref, a_ref, gat_ref, send_sems, recv_sems):
        step = pl.program_id(0) * n_col + pl.program_id(1)
        gather = _GatherPlan([shard.shape], [shard_ref], [gat_ref], send_sems, recv_sems)

        @pl.when(step == 0)
        def _():
            gather.start([0])

        @pl.when(step == (2 * n_steps) // 3)
        def _():
            gather.forward([0])

        g = g_ref[...].astype(F32)
        w = w_ref[...]
        row = _row_index()
        c = b_ref[...] + w[0:1] * _shift_down(g, 2, row) + w[1:2] * _shift_down(g, 1, row) + w[2:3] * g
        a_ref[...] = (_gelu(c) * u_ref[...].astype(F32)).astype(BF16)

        @pl.when(step == n_steps - 1)
        def _():
            gather.finish([0])

    blk = pl.BlockSpec((SEQ, LANE_BLOCK), lambda b, j: (b, j))
    return pl.pallas_call(
        body, grid=(batch, n_col),
        in_specs=[blk, blk, pl.BlockSpec((3, LANE_BLOCK), lambda b, j: (0, j)),
                  pl.BlockSpec((1, LANE_BLOCK), lambda b, j: (0, j)), ANY],
        out_specs=[blk, ANY],
        out_shape=[jax.ShapeDtypeStruct((m, D_FF), BF16),
                   jax.ShapeDtypeStruct((N_SHARD,) + shard.shape, shard.dtype)],
        scratch_shapes=_sem_pair(6),
        compiler_params=_params(("arbitrary", "arbitrary")), name="convgate_fwd",
    )(gate, up, conv_w, conv_b, shard)


def _convgate_bwd(gate, up, dact, conv_w, conv_b, batch):
    m = gate.shape[0]

    def body(g_ref, u_ref, da_ref, w_ref, b_ref, dg_ref, du_ref, dw_ref, db_ref):
        @pl.when(pl.program_id(1) == 0)
        def _():
            dw_ref[...] = jnp.zeros_like(dw_ref)
            db_ref[...] = jnp.zeros_like(db_ref)

        g = g_ref[...].astype(F32)
        w = w_ref[...]
        row = _row_index()
        g1 = _shift_down(g, 1, row)
        g2 = _shift_down(g, 2, row)
        c = b_ref[...] + w[0:1] * g2 + w[1:2] * g1 + w[2:3] * g
        gg, dgg = _gelu_and_grad(c)
        da = da_ref[...].astype(F32)
        du_ref[...] = (da * gg).astype(BF16)
        dc = da * u_ref[...].astype(F32) * dgg
        db_ref[...] += jnp.sum(dc, axis=0, keepdims=True)
        dw_ref[0:1, :] += jnp.sum(dc * g2, axis=0, keepdims=True)
        dw_ref[1:2, :] += jnp.sum(dc * g1, axis=0, keepdims=True)
        dw_ref[2:3, :] += jnp.sum(dc * g, axis=0, keepdims=True)
        dg_ref[...] = (w[2:3] * dc + w[1:2] * _shift_up(dc, 1, row) + w[0:1] * _shift_up(dc, 2, row)).astype(BF16)

    blk = pl.BlockSpec((SEQ, LANE_BLOCK), lambda j, b: (b, j))
    wspec = pl.BlockSpec((3, LANE_BLOCK), lambda j, b: (0, j))
    bspec = pl.BlockSpec((1, LANE_BLOCK), lambda j, b: (0, j))
    return pl.pallas_call(
        body, grid=(D_FF // LANE_BLOCK, batch),
        in_specs=[blk, blk, blk, wspec, bspec],
        out_specs=[blk, blk, wspec, bspec],
        out_shape=[jax.ShapeDtypeStruct((m, D_FF), BF16), jax.ShapeDtypeStruct((m, D_FF), BF16),
                   jax.ShapeDtypeStruct((3, D_FF), F32), jax.ShapeDtypeStruct((1, D_FF), F32)],
        compiler_params=_params(("parallel", "arbitrary")), name="convgate_bwd",
    )(gate, up, dact, conv_w, conv_b)


def _gather_weights(shards, conv_w_shard, rel_bias, buckets_np):
    nt = len(shards)
    shapes = [sh.shape for sh in shards]
    ts = list(range(nt))
    tables = _bias_tables_body(buckets_np)

    def body(*refs):
        shard_refs = refs[:nt]
        cw_ref, rb_ref, bk_ref = refs[nt:nt + 3]
        out_refs = refs[nt + 3:2 * nt + 3]
        cw_out, bias_ref, bias_t_ref = refs[2 * nt + 3:2 * nt + 6]
        send_sems, recv_sems, cw_send, cw_recv = refs[2 * nt + 6:]
        plan = _GatherPlan(shapes, shard_refs, out_refs, send_sems, recv_sems)
        x, y, c, chips = _mesh_pos()

        def cw_copy(j, src, dst, chip):
            return pltpu.make_async_remote_copy(src_ref=src, dst_ref=dst, send_sem=cw_send.at[j],
                                                recv_sem=cw_recv.at[j], device_id=(*chip, c), device_id_type=MESH)

        plan.start(ts)
        cw_sends = [cw_copy(j, cw_ref, cw_out.at[2 * x + y], chip) for j, chip in enumerate(chips)]
        for cp in cw_sends:
            cp.start()
        tables(rb_ref, bk_ref, bias_ref, bias_t_ref)
        plan.forward(ts)
        for j, chip in enumerate(chips):
            dst = cw_out.at[2 * chip[0] + chip[1]]
            cw_copy(j, dst, dst, chip).wait_recv()
        plan.finish(ts)
        for cp in cw_sends:
            cp.wait_send()

    out_shape = [jax.ShapeDtypeStruct((N_SHARD,) + sh.shape, sh.dtype) for sh in shards]
    out_shape.append(jax.ShapeDtypeStruct((N_SHARD,) + conv_w_shard.shape, conv_w_shard.dtype))
    out_shape += [jax.ShapeDtypeStruct((N_CFG, B_HEADS, Q_BLOCK, 2 * Q_BLOCK), F32),
                  jax.ShapeDtypeStruct((N_CFG, B_HEADS, 2 * Q_BLOCK, Q_BLOCK), F32)]
    vm = pl.BlockSpec(memory_space=pltpu.VMEM)
    res = pl.pallas_call(
        body, in_specs=[ANY] * (nt + 1) + [pl.BlockSpec(memory_space=pltpu.SMEM), vm],
        out_specs=[ANY] * (nt + 1) + [vm, vm], out_shape=out_shape,
        scratch_shapes=_sem_pair(6 * nt) + _sem_pair(3),
        compiler_params=pltpu.CompilerParams(has_side_effects=True, vmem_limit_bytes=VMEM_LIMIT),
        name="gather_weights",
    )(*shards, conv_w_shard, rel_bias, jnp.asarray(buckets_np))
    return list(res[:nt + 1]), res[nt + 1], res[nt + 2]


def _add_halves(g, recv, c_idx):
    _, rows2, cols = g.shape
    rows = rows2 // 2
    tr = rows // 2 if rows % 16 == 0 and rows >= 256 else rows
    nblk = rows // tr

    def body(c_ref, g_ref, r_ref, o_ref):
        o_ref[...] = (g_ref[...] + r_ref[...]).astype(BF16)

    return pl.pallas_call(
        body,
        grid_spec=pltpu.PrefetchScalarGridSpec(
            num_scalar_prefetch=1, grid=(N_SHARD, nblk),
            in_specs=[pl.BlockSpec((None, tr, cols), lambda s, i, c: (s, c[0] * nblk + i, 0)),
                      pl.BlockSpec((None, tr, cols), lambda s, i, c: (s, i, 0))],
            out_specs=pl.BlockSpec((None, tr, cols), lambda s, i, c: (s, i, 0))),
        out_shape=jax.ShapeDtypeStruct((N_SHARD, rows, cols), BF16),
        compiler_params=_params(("parallel", "parallel")), name="rs_add_halves",
    )(c_idx, g, recv)


def _add_chips(part, recv, s_idx, c_idx):
    _, rows, cols = part.shape
    tr = rows // 2 if rows % 32 == 0 and rows >= 256 else rows
    nblk = rows // tr

    def body(idx_ref, p_ref, r_ref, o_ref):
        acc = p_ref[...].astype(F32)
        for j in range(3):
            acc = acc + r_ref[j].astype(F32)
        o_ref[...] = acc

    return pl.pallas_call(
        body,
        grid_spec=pltpu.PrefetchScalarGridSpec(
            num_scalar_prefetch=1, grid=(nblk,),
            in_specs=[pl.BlockSpec((None, tr, cols), lambda i, idx: (idx[0], i, 0)),
                      pl.BlockSpec((3, tr, cols), lambda i, idx: (0, i, 0))],
            out_specs=pl.BlockSpec((tr, cols), lambda i, idx: (idx[1] * nblk + i, 0))),
        out_shape=jax.ShapeDtypeStruct((2 * rows, cols), F32),
        compiler_params=_params(("parallel",)), name="rs_add_chips",
    )(jnp.concatenate([s_idx, c_idx]), part, recv)


def _finish_reductions(fulls, arrays):
    nt, n = len(fulls), len(arrays)

    def body(*refs):
        in_refs = refs[nt:nt + n]
        full_refs, out_refs = refs[nt + n:2 * nt + n], refs[2 * nt + n:2 * nt + 2 * n]
        pos = 2 * nt + 2 * n
        sib_refs, chip_refs = refs[pos:pos + n], refs[pos + n:pos + 2 * n]
        share_send, share_recv, send_sems, recv_sems = refs[pos + 2 * n:]
        x, y, c, chips = _mesh_pos()

        def half(t, which):
            rows = fulls[t].shape[0] // 2
            return full_refs[t].at[pl.ds(which * rows, rows), :]

        def share(t, which):
            return pltpu.make_async_remote_copy(
                src_ref=half(t, which), dst_ref=half(t, which), send_sem=share_send.at[t],
                recv_sem=share_recv.at[t], device_id=(x, y, 1 - c), device_id_type=MESH)

        for t in range(nt):
            share(t, c).start()

        def copy(k, src, dst, to):
            return pltpu.make_async_remote_copy(src_ref=src, dst_ref=dst, send_sem=send_sems.at[k],
                                                recv_sem=recv_sems.at[k], device_id=to, device_id_type=MESH)

        first = [copy(t, in_refs[t], sib_refs[t], (x, y, 1 - c)) for t in range(n)]
        for cp in first:
            cp.start()
        for cp in first:
            cp.wait()
        for t in range(n):
            out_refs[t][...] = in_refs[t][...] + sib_refs[t][...]
        second = [copy(n + 3 * t + j, out_refs[t], chip_refs[t].at[j], (*chip, c))
                  for t in range(n) for j, chip in enumerate(chips)]
        for cp in second:
            cp.start()
        for cp in second:
            cp.wait()
        for t in range(n):
            out_refs[t][...] = (out_refs[t][...] + chip_refs[t][0]) + (chip_refs[t][1] + chip_refs[t][2])
        for t in range(nt):
            share(t, 1 - c).wait_recv()
        for t in range(nt):
            share(t, c).wait_send()

    vm = pl.BlockSpec(memory_space=pltpu.VMEM)
    res = pl.pallas_call(
        body, in_specs=[ANY] * nt + [vm] * n, out_specs=[ANY] * nt + [vm] * n,
        out_shape=[jax.ShapeDtypeStruct(f.shape, f.dtype) for f in fulls]
        + [jax.ShapeDtypeStruct(a.shape, F32) for a in arrays],
        input_output_aliases={t: t for t in range(nt)},
        scratch_shapes=[pltpu.VMEM(a.shape, F32) for a in arrays] + [pltpu.VMEM((3,) + a.shape, F32) for a in arrays]
        + _sem_pair(nt) + _sem_pair(4 * n),
        compiler_params=pltpu.CompilerParams(has_side_effects=True, vmem_limit_bytes=VMEM_LIMIT),
        name="finish_reductions",
    )(*fulls, *arrays)
    return list(res[:nt]), list(res[nt:])


def _from_col_shards(g):
    n, rows, cols = g.shape
    return g.transpose(1, 0, 2).reshape(rows, n * cols)


def _train_step(x, tgt, g1, g2, g3, g4, shards, ln_g, ln_b, w_s, b_s, rel_bias, conv_w_shard, conv_b, batch,
                s_idx, c_idx):
    big = dict(tm=1024, out_dtype=F32)
    buckets = _bucket_tables()
    bz = jnp.repeat(b_s.T, HEAD_DIM, axis=1)
    w_st = jnp.swapaxes(w_s, 1, 2)

    def with_own(gathered, own):
        return lax.dynamic_update_index_in_dim(gathered, own, s_idx[0], 0)

    def shard_major(g):
        return g.reshape(N_SHARD, g.shape[0] // N_SHARD, D_MODEL)

    (g_in, g_convw), bias, bias_t = _gather_weights([shards["w_in"]], conv_w_shard, rel_bias, buckets)
    w_in_t = with_own(g_in, shards["w_in"]).reshape(IN_COLS, D_MODEL)
    conv_w = _from_col_shards(with_own(g_convw, conv_w_shard))

    h1, uv, qkv = _proj_fwd(x, g1, w_in_t)
    a = _gate_fwd(uv, ln_g, ln_b, w_s, bz)
    later = ["w_out", "w_gate", "w_up"]
    o_bf, lse, gathered = _attn_fwd(qkv, bias, batch, [shards[n] for n in later])
    g_out, g_gate, g_up = [with_own(g, shards[n]) for g, n in zip(gathered, later)]
    w_out = g_out.reshape(D_MODEL, D_MODEL)
    w_gate_t = g_gate.reshape(D_FF, D_MODEL)
    w_up_t = g_up.reshape(D_FF, D_MODEL)
    (y1, x1, h2), _ = _fused_rows(
        "out_proj_mid_fwd", 512,
        [(a, w_out, "nn", slice(0, A_WIDTH)), (o_bf, w_out, "nn", slice(A_WIDTH, D_MODEL))],
        [x], [g2, g3], _mid_fwd_rows, [F32, F32, BF16], [])
    gate = _mm(h2, w_gate_t, dims="nt", tm=1024, tn=1408, tk=1024, out_dtype=BF16, name="mm_gate")
    up = _mm(h2, w_up_t, dims="nt", tm=1024, tn=1408, tk=1024, out_dtype=BF16, name="mm_up")
    act, g_down = _convgate_fwd(gate, up, conv_w, conv_b, batch, shards["w_down"])
    w_down = with_own(g_down, shards["w_down"]).reshape(D_FF, D_MODEL)
    (dx2, dy2, dg4, loss), _ = _fused_rows(
        "down_proj_loss_head", 512, [(act, w_down, "nn", None)], [x1, tgt], [g4], _loss_head_rows,
        [F32, BF16], [(1, D_MODEL), (1, 128)])

    dact = _mm(dy2, w_down, dims="nt", tm=1024, tn=1408, tk=1024, out_dtype=BF16, name="mm_dact")
    dw_down = _mm(act, dy2, dims="tn", tm=1408, tn=1024, tk=1024, out_dtype=F32, name="mm_dw_down")
    dgate, dup, dconv_w, dconv_b = _convgate_bwd(gate, up, dact, conv_w, conv_b, batch)
    (dx1, dy1, dg2, dg3), _ = _fused_rows(
        "dh2_mid_bwd", 256, [(dgate, w_gate_t, "nn", None), (dup, w_up_t, "nn", None)],
        [x1, y1, dx2], [g2, g3], _mid_bwd_rows, [F32, BF16], [(1, D_MODEL), (1, D_MODEL)])
    dw_gate_t = _mm(dgate, h2, dims="tn", tm=1408, tn=1024, tk=1024, out_dtype=F32, name="mm_dw_gate")
    dw_up_t = _mm(dup, h2, dims="tn", tm=1408, tn=1024, tk=1024, out_dtype=F32, name="mm_dw_up")
    dmix = _mm(dy1, w_out, dims="nt", tn=1024, tk=1024, name="mm_dmix", **big)
    dw_out_a = _mm(a, dy1, dims="tn", tm=A_WIDTH, tn=1024, tk=1024, out_dtype=F32, name="mm_dw_out_a")
    dw_out_b = _mm(o_bf, dy1, dims="tn", tm=B_WIDTH, tn=1024, tk=1024, out_dtype=F32, name="mm_dw_out_b")

    dw_out = jnp.concatenate([dw_out_a, dw_out_b], axis=0)
    done = [shard_major(g) for g in (dw_down, dw_gate_t, dw_up_t, dw_out)]
    (dproj, dln_g, dln_b, dw_s, dbz), recv_a = _gate_bwd(uv, dmix, ln_g, ln_b, w_s, w_st, bz, done)
    parts = [_add_halves(g, r, c_idx) for g, r in zip(done, recv_a)]
    dproj, ds, recv = _attn_bwd(qkv, dmix, o_bf, lse, bias_t, dproj, batch, parts)
    fulls = [_add_chips(p, r, s_idx, c_idx) for p, r in zip(parts, recv)]
    dw_in_t = _mm(dproj, h1, dims="tn", tm=1408, tn=1024, tk=1024, out_dtype=F32, name="mm_dw_in")
    last = [shard_major(dw_in_t)]
    drel, recv_in_a = _rel_bias_grad(ds, np.ascontiguousarray(np.swapaxes(buckets, 1, 2)), last)
    part_in = [_add_halves(g, r, c_idx) for g, r in zip(last, recv_in_a)]
    (dx0, dg1), recv_in = _fused_rows(
        "dh1_in_bwd", 512, [(dproj, w_in_t, "nn", None)], [x, dx1], [g1], _in_bwd_rows,
        [F32], [(1, D_MODEL)], exchange=part_in)
    fulls += [_add_chips(p, r, s_idx, c_idx) for p, r in zip(part_in, recv_in)]
    half_reduced = dict(zip(["w_down", "w_gate", "w_up", "w_out", "w_in"], fulls))

    small = dict(
        loss=loss, norm_mix_pre=dg1, norm_mix_post=dg2, norm_ffn_pre=dg3, norm_ffn_post=dg4,
        ln_v_gain=dln_g, ln_v_bias=dln_b, spatial_w=dw_s, spatial_b=dbz, rel_bias=drel,
        conv_w=dconv_w, conv_b=dconv_b,
    )
    return dx0, small, half_reduced


def _adamw_update(w, g, m, v):
    nm = ADAM_B1 * m + (1.0 - ADAM_B1) * g
    nv = ADAM_B2 * v + (1.0 - ADAM_B2) * (g * g)
    m_hat = nm / (1.0 - ADAM_B1 ** ADAM_STEP)
    v_hat = nv / (1.0 - ADAM_B2 ** ADAM_STEP)
    return -ADAM_LR * (m_hat / (jnp.sqrt(v_hat) + ADAM_EPS) + ADAM_WD * w), nm, nv


def _adamw(w, g, m, v, name):
    rows, cols = w.shape
    tr = next(cand for cand in (256, 176, 128) if rows % cand == 0)

    def body(w_ref, g_ref, m_ref, v_ref, go_ref, d_ref, nm_ref, nv_ref):
        gv = g_ref[...]
        go_ref[...] = gv
        d_ref[...], nm_ref[...], nv_ref[...] = _adamw_update(w_ref[...], gv, m_ref[...], v_ref[...])

    spec = pl.BlockSpec((tr, cols), lambda i: (i, 0))
    sds = jax.ShapeDtypeStruct((rows, cols), F32)
    return pl.pallas_call(
        body, grid=(rows // tr,), in_specs=[spec] * 4, out_specs=[spec] * 4, out_shape=[sds] * 4,
        compiler_params=_params(("parallel",)), name=name,
    )(w, g, m, v)


def _adamw_small(ws, gs, ms, vs):
    n = len(ws)

    def body(*refs):
        w_refs, g_refs, m_refs, v_refs = refs[:n], refs[n:2 * n], refs[2 * n:3 * n], refs[3 * n:4 * n]
        d_refs, nm_refs, nv_refs = refs[4 * n:5 * n], refs[5 * n:6 * n], refs[6 * n:7 * n]
        for t in range(n):
            d_refs[t][...], nm_refs[t][...], nv_refs[t][...] = _adamw_update(
                w_refs[t][...], g_refs[t][...], m_refs[t][...], v_refs[t][...])

    vm = pl.BlockSpec(memory_space=pltpu.VMEM)
    sds = [jax.ShapeDtypeStruct(w.shape, F32) for w in ws]
    res = pl.pallas_call(
        body, in_specs=[vm] * (4 * n), out_specs=[vm] * (3 * n), out_shape=sds * 3,
        compiler_params=_params(), name="adamw_small",
    )(*ws, *gs, *ms, *vs)
    return res[:n], res[n:2 * n], res[2 * n:]


SMALL = ["norm_mix_pre", "norm_mix_post", "norm_ffn_pre", "norm_ffn_post", "ln_v_gain", "ln_v_bias",
         "spatial_w", "spatial_b", "rel_bias", "conv_b"]
LARGE = ["w_in", "w_gate", "w_up", "w_down", "w_out"]
TRANSPOSED = ("w_in", "w_gate", "w_up")
ORDER = ["norm_mix_pre", "norm_mix_post", "norm_ffn_pre", "norm_ffn_post", "w_in", "ln_v_gain", "ln_v_bias",
         "spatial_w", "spatial_b", "rel_bias", "w_out", "w_gate", "w_up", "conv_w", "conv_b", "w_down"]


def kernel(x, norm_mix_pre, norm_mix_post, norm_ffn_pre, norm_ffn_post, w_in, ln_v_gain, ln_v_bias, spatial_w, spatial_b, rel_bias, w_out, w_gate, w_up, conv_w, conv_b, w_down, loss_target, m_norm_mix_pre, m_norm_mix_post, m_norm_ffn_pre, m_norm_ffn_post, m_w_in, m_ln_v_gain, m_ln_v_bias, m_spatial_w, m_spatial_b, m_rel_bias, m_w_out, m_w_gate, m_w_up, m_conv_w, m_conv_b, m_w_down, v_norm_mix_pre, v_norm_mix_post, v_norm_ffn_pre, v_norm_ffn_post, v_w_in, v_ln_v_gain, v_ln_v_bias, v_spatial_w, v_spatial_b, v_rel_bias, v_w_out, v_w_gate, v_w_up, v_conv_w, v_conv_b, v_w_down):
    params = dict(norm_mix_pre=norm_mix_pre, norm_mix_post=norm_mix_post, norm_ffn_pre=norm_ffn_pre,
                  norm_ffn_post=norm_ffn_post, w_in=w_in, ln_v_gain=ln_v_gain, ln_v_bias=ln_v_bias,
                  spatial_w=spatial_w, spatial_b=spatial_b, rel_bias=rel_bias, w_out=w_out, w_gate=w_gate,
                  w_up=w_up, conv_w=conv_w, conv_b=conv_b, w_down=w_down)
    mom = dict(norm_mix_pre=m_norm_mix_pre, norm_mix_post=m_norm_mix_post, norm_ffn_pre=m_norm_ffn_pre,
               norm_ffn_post=m_norm_ffn_post, w_in=m_w_in, ln_v_gain=m_ln_v_gain, ln_v_bias=m_ln_v_bias,
               spatial_w=m_spatial_w, spatial_b=m_spatial_b, rel_bias=m_rel_bias, w_out=m_w_out, w_gate=m_w_gate,
               w_up=m_w_up, conv_w=m_conv_w, conv_b=m_conv_b, w_down=m_w_down)
    var = dict(norm_mix_pre=v_norm_mix_pre, norm_mix_post=v_norm_mix_post, norm_ffn_pre=v_norm_ffn_pre,
               norm_ffn_post=v_norm_ffn_post, w_in=v_w_in, ln_v_gain=v_ln_v_gain, ln_v_bias=v_ln_v_bias,
               spatial_w=v_spatial_w, spatial_b=v_spatial_b, rel_bias=v_rel_bias, w_out=v_w_out, w_gate=v_w_gate,
               w_up=v_w_up, conv_w=v_conv_w, conv_b=v_conv_b, w_down=v_w_down)

    batch = x.shape[0]
    xi, yi, ci = lax.axis_index("x"), lax.axis_index("y"), lax.axis_index("c")
    s_idx = (2 * xi + yi).astype(jnp.int32).reshape(1)
    c_idx = ci.astype(jnp.int32).reshape(1)

    def local(a, n):
        return jnp.swapaxes(a[0], 0, 1) if n in TRANSPOSED else a[0]

    shards = {n: local(params[n], n).astype(BF16) for n in LARGE}
    dx0, partial, half_reduced = _train_step(
        x.reshape(batch * SEQ, D_MODEL), loss_target.reshape(batch * SEQ, D_MODEL),
        norm_mix_pre, norm_mix_post, norm_ffn_pre, norm_ffn_post, shards,
        ln_v_gain.reshape(1, A_WIDTH), ln_v_bias.reshape(1, A_WIDTH), spatial_w[0], spatial_b[0], rel_bias,
        conv_w[0], conv_b, batch, s_idx, c_idx)
    grad_x = dx0.reshape(batch, SEQ, D_MODEL)

    names = list(partial)
    fulls, sums = _finish_reductions([half_reduced[n] for n in LARGE], [partial[n] for n in names])
    reduced = dict(zip(LARGE, fulls))
    total = dict(zip(names, sums))
    loss = total["loss"][0, 0]
    total["spatial_b"] = total["spatial_b"][:, ::HEAD_DIM].T
    total["rel_bias"] = total["rel_bias"].reshape(B_HEADS, NUM_BUCKETS).T
    total["conv_w"] = lax.dynamic_slice_in_dim(total["conv_w"], s_idx[0] * SHARD_FF, SHARD_FF, axis=1)
    small_names = SMALL + ["conv_w"]
    for n in small_names:
        reduced[n] = total[n].reshape(params[n].shape)

    out_g, out_d, out_m, out_v = {}, {}, {}, {}
    for n in LARGE:
        res = _adamw(local(params[n], n), reduced[n], local(mom[n], n), local(var[n], n), name=f"adamw_{n}")
        if n in TRANSPOSED:
            res = [jnp.swapaxes(r, 0, 1) for r in res]
        out_g[n], out_d[n], out_m[n], out_v[n] = [r[None] for r in res]
    d, nm, nv = _adamw_small([params[n] for n in small_names], [reduced[n] for n in small_names],
                             [mom[n] for n in small_names], [var[n] for n in small_names])
    for n, dd, mm, vv in zip(small_names, d, nm, nv):
        out_g[n], out_d[n], out_m[n], out_v[n] = reduced[n], dd, mm, vv

    return (loss, grad_x, *[out_g[n] for n in ORDER], *[out_d[n] for n in ORDER],
            *[out_m[n] for n in ORDER], *[out_v[n] for n in ORDER])
```

```python
import functools
import math

import numpy as np
import jax
import jax.numpy as jnp
from jax import lax
from jax.experimental import pallas as pl
from jax.experimental.pallas import tpu as pltpu

F32 = jnp.float32
BF16 = jnp.bfloat16
MESH = pl.DeviceIdType.MESH

D_MODEL = 1024
SEQ = 2048
HEAD_DIM = 64
A_GROUPS = 4
A_WIDTH = 256
B_HEADS = 12
B_WIDTH = 768
CHUNK = 128
DILATED = ((128, 1), (512, 4), (2048, 16))
NUM_BUCKETS = 32
MAX_DISTANCE = 2048
D_FF = 2816
IN_COLS = 2816
NORM_EPS = 1e-6
NEG_INF = -1e30
N_SHARD = 4
SHARD_FF = D_FF // N_SHARD
LANE_BLOCK = 256
VMEM_LIMIT = 56 * 1024 * 1024

ADAM_LR = 0.001
ADAM_B1 = 0.9
ADAM_B2 = 0.999
ADAM_EPS = 1e-08
ADAM_WD = 0.01
ADAM_STEP = 10

GELU_C = math.sqrt(2.0 / math.pi)
GELU_A = 0.044715

ANY = pl.BlockSpec(memory_space=pl.ANY)


def _params(sem=None):
    return pltpu.CompilerParams(dimension_semantics=sem, vmem_limit_bytes=VMEM_LIMIT)


def _dot(a, b, precision=None):
    return jnp.dot(a, b, preferred_element_type=F32, precision=precision)


def _dot_nt(a, b, precision=None):
    return lax.dot_general(a, b, (((1,), (1,)), ((), ())), preferred_element_type=F32, precision=precision)


def _dot_tn(a, b):
    return lax.dot_general(a, b, (((0,), (0,)), ((), ())), preferred_element_type=F32)


def _gelu(x):
    t = jnp.tanh(x * (GELU_C + (GELU_C * GELU_A) * (x * x)))
    return (0.5 * x) * (1.0 + t)


def _gelu_and_grad(x):
    x2 = x * x
    u = 1.0 + jnp.tanh(x * (GELU_C + (GELU_C * GELU_A) * x2))
    hx = 0.5 * x
    dg = u * (0.5 + hx * (2.0 - u) * (GELU_C + (3.0 * GELU_C * GELU_A) * x2))
    return hx * u, dg


def _mesh_pos():
    x, y, c = lax.axis_index("x"), lax.axis_index("y"), lax.axis_index("c")
    chips = [(1 - x, y), (x, 1 - y), (1 - x, 1 - y)]
    return x, y, c, chips


class _GatherPlan:
    def __init__(self, shapes, shard_refs, out_refs, send_sems, recv_sems):
        self.shapes, self.shard_refs, self.out_refs = shapes, shard_refs, out_refs
        self.send_sems, self.recv_sems = send_sems, recv_sems
        self.x, self.y, self.c, self.chips = _mesh_pos()
        self.sib = (self.x, self.y, 1 - self.c)

    def _half(self, t, chip, which):
        rows = self.shapes[t][0] // 2
        return self.out_refs[t].at[2 * chip[0] + chip[1], pl.ds(which * rows, rows), :]

    def _copy(self, k, src, dst, to):
        return pltpu.make_async_remote_copy(src_ref=src, dst_ref=dst, send_sem=self.send_sems.at[k],
                                            recv_sem=self.recv_sems.at[k], device_id=to, device_id_type=MESH)

    def _sends(self, t):
        rows = self.shapes[t][0] // 2
        src = self.shard_refs[t].at[pl.ds(self.c * rows, rows), :]
        return [self._copy(6 * t + j, src, self._half(t, (self.x, self.y), self.c), (*chip, self.c))
                for j, chip in enumerate(self.chips)]

    def _forwards(self, t):
        return [self._copy(6 * t + 3 + j, self._half(t, chip, self.c), self._half(t, chip, self.c), self.sib)
                for j, chip in enumerate(self.chips)]

    def start(self, ts):
        for t in ts:
            for cp in self._sends(t):
                cp.start()

    def forward(self, ts):
        for t in ts:
            for j, chip in enumerate(self.chips):
                landed = self._half(t, chip, self.c)
                self._copy(6 * t + j, landed, landed, (*chip, self.c)).wait_recv()
            for cp in self._forwards(t):
                cp.start()

    def finish(self, ts):
        for t in ts:
            for j, chip in enumerate(self.chips):
                other = self._half(t, chip, 1 - self.c)
                self._copy(6 * t + 3 + j, other, other, self.sib).wait_recv()
        for t in ts:
            for cp in self._sends(t) + self._forwards(t):
                cp.wait_send()


class _SiblingExchangePlan:
    def __init__(self, shapes, grad_refs, out_refs, send_sems, recv_sems):
        self.shapes, self.grad_refs, self.out_refs = shapes, grad_refs, out_refs
        self.send_sems, self.recv_sems = send_sems, recv_sems
        self.x, self.y, self.c, _ = _mesh_pos()

    def _copies(self):
        out = []
        for t, (g, o) in enumerate(zip(self.grad_refs, self.out_refs)):
            rows = self.shapes[t][1] // 2
            out.append(pltpu.make_async_remote_copy(
                src_ref=g.at[:, pl.ds((1 - self.c) * rows, rows), :], dst_ref=o, send_sem=self.send_sems.at[t],
                recv_sem=self.recv_sems.at[t], device_id=(self.x, self.y, 1 - self.c), device_id_type=MESH))
        return out

    def start(self):
        for cp in self._copies():
            cp.start()

    def finish(self):
        for cp in self._copies():
            cp.wait()


class _ChipExchangePlan:
    def __init__(self, part_refs, out_refs, send_sems, recv_sems):
        self.part_refs, self.out_refs, self.send_sems, self.recv_sems = part_refs, out_refs, send_sems, recv_sems
        _, _, self.c, self.chips = _mesh_pos()

    def _copies(self):
        return [pltpu.make_async_remote_copy(
            src_ref=p.at[2 * chip[0] + chip[1]], dst_ref=o.at[j], send_sem=self.send_sems.at[3 * t + j],
            recv_sem=self.recv_sems.at[3 * t + j], device_id=(*chip, self.c), device_id_type=MESH)
            for t, (p, o) in enumerate(zip(self.part_refs, self.out_refs)) for j, chip in enumerate(self.chips)]

    def start(self):
        for cp in self._copies():
            cp.start()

    def finish(self):
        for cp in self._copies():
            cp.wait()


class _SmallAllReducePlan:
    def __init__(self, in_refs, out_refs, sib_refs, chip_refs, send_sems, recv_sems):
        self.in_refs, self.out_refs, self.sib_refs, self.chip_refs = in_refs, out_refs, sib_refs, chip_refs
        self.send_sems, self.recv_sems = send_sems, recv_sems
        self.n = len(in_refs)
        self.x, self.y, self.c, self.chips = _mesh_pos()

    def _copy(self, k, src, dst, to):
        return pltpu.make_async_remote_copy(src_ref=src, dst_ref=dst, send_sem=self.send_sems.at[k],
                                            recv_sem=self.recv_sems.at[k], device_id=to, device_id_type=MESH)

    def _first(self):
        return [self._copy(t, self.in_refs[t], self.sib_refs[t], (self.x, self.y, 1 - self.c)) for t in range(self.n)]

    def _second(self):
        return [self._copy(self.n + 3 * t + j, self.out_refs[t], self.chip_refs[t].at[j], (*chip, self.c))
                for t in range(self.n) for j, chip in enumerate(self.chips)]

    def start_sibling(self):
        for cp in self._first():
            cp.start()

    def sum_sibling_and_start_chips(self):
        for cp in self._first():
            cp.wait()
        for t in range(self.n):
            self.out_refs[t][...] = self.in_refs[t][...] + self.sib_refs[t][...]
        for cp in self._second():
            cp.start()

    def finish(self):
        for cp in self._second():
            cp.wait()
        for t in range(self.n):
            self.out_refs[t][...] = ((self.out_refs[t][...] + self.chip_refs[t][0])
                                     + (self.chip_refs[t][1] + self.chip_refs[t][2]))

    @staticmethod
    def scratch(arrays):
        return ([pltpu.VMEM(a.shape, F32) for a in arrays] + [pltpu.VMEM((3,) + a.shape, F32) for a in arrays]
                + _sem_pair(4 * len(arrays)))


def _sem_pair(n):
    return [pltpu.SemaphoreType.DMA((n,)), pltpu.SemaphoreType.DMA((n,))]


def _mm(a, b, *, dims, tm, tn, tk, out_dtype, name):
    if dims == "nn":
        m, k = a.shape
        n = b.shape[1]
        a_spec = pl.BlockSpec((tm, tk), lambda i, j, kk: (i, kk))
        b_spec = pl.BlockSpec((tk, tn), lambda i, j, kk: (kk, j))
        dot = _dot
    elif dims == "nt":
        m, k = a.shape
        n = b.shape[0]
        a_spec = pl.BlockSpec((tm, tk), lambda i, j, kk: (i, kk))
        b_spec = pl.BlockSpec((tn, tk), lambda i, j, kk: (j, kk))
        dot = _dot_nt
    else:
        k, m = a.shape
        n = b.shape[1]
        a_spec = pl.BlockSpec((tk, tm), lambda i, j, kk: (kk, i))
        b_spec = pl.BlockSpec((tk, tn), lambda i, j, kk: (kk, j))
        dot = _dot_tn
    assert m % tm == 0 and n % tn == 0 and k % tk == 0, (name, m, n, k)
    grid = (m // tm, n // tn, k // tk)
    nk = grid[2]
    assert nk == 1 or out_dtype == F32, name

    def body(a_ref, b_ref, o_ref):
        prod = dot(a_ref[...].astype(BF16), b_ref[...].astype(BF16))
        if nk == 1:
            o_ref[...] = prod.astype(out_dtype)
        else:
            kk = pl.program_id(2)

            @pl.when(kk == 0)
            def _():
                o_ref[...] = prod

            @pl.when(kk > 0)
            def _():
                o_ref[...] += prod

    return pl.pallas_call(
        body, grid=grid, in_specs=[a_spec, b_spec],
        out_specs=pl.BlockSpec((tm, tn), lambda i, j, kk: (i, j)),
        out_shape=jax.ShapeDtypeStruct((m, n), out_dtype),
        compiler_params=_params(("parallel", "parallel", "arbitrary")), name=name,
    )(a, b)


def _fused_rows(name, tm, mats, rows, vecs, fn, row_outs, acc_outs, exchange=()):
    m = mats[0][0].shape[0]
    nm, nr, nv, nro, nao, nx = len(mats), len(rows), len(vecs), len(row_outs), len(acc_outs), len(exchange)
    n_steps = m // tm

    def body(*refs):
        a_refs, w_refs = refs[:nm], refs[nm:2 * nm]
        pos = 2 * nm
        row_refs, vec_refs, part_refs = refs[pos:pos + nr], refs[pos + nr:pos + nr + nv], refs[pos + nr + nv:pos + nr + nv + nx]
        pos += nr + nv + nx
        out_refs, acc_refs, recv_refs = refs[pos:pos + nro], refs[pos + nro:pos + nro + nao], refs[pos + nro + nao:pos + nro + nao + nx]
        sems = refs[pos + nro + nao + nx:]
        i = pl.program_id(0)
        if nx:
            plan = _ChipExchangePlan(part_refs, recv_refs, *sems)

            @pl.when(i == 0)
            def _():
                plan.start()

        @pl.when(i == 0)
        def _():
            for r in acc_refs:
                r[...] = jnp.zeros_like(r)

        y = None
        for a_ref, w_ref, (_, _, dims, sl) in zip(a_refs, w_refs, mats):
            w = w_ref[...] if sl is None else w_ref[sl, :]
            part = (_dot if dims == "nn" else _dot_nt)(a_ref[...], w)
            y = part if y is None else y + part
        res = fn(y, *[r[...] for r in row_refs], *[v[...] for v in vec_refs])
        for r, val in zip(out_refs, res[:nro]):
            r[...] = val.astype(r.dtype)
        for r, val in zip(acc_refs, res[nro:]):
            r[...] += val

        if nx:
            @pl.when(i == n_steps - 1)
            def _():
                plan.finish()

    tile = lambda width: pl.BlockSpec((tm, width), lambda i: (i, 0))
    res = pl.pallas_call(
        body, grid=(n_steps,),
        in_specs=[tile(a.shape[1]) for a, _, _, _ in mats] + [_full_spec(w.shape) for _, w, _, _ in mats]
        + [tile(D_MODEL)] * nr + [_full_spec((1, D_MODEL))] * nv + [ANY] * nx,
        out_specs=[tile(D_MODEL)] * nro + [_full_spec(s) for s in acc_outs] + [ANY] * nx,
        out_shape=[jax.ShapeDtypeStruct((m, D_MODEL), dt) for dt in row_outs]
        + [jax.ShapeDtypeStruct(s, F32) for s in acc_outs]
        + [jax.ShapeDtypeStruct((3,) + p.shape[1:], p.dtype) for p in exchange],
        scratch_shapes=_sem_pair(3 * nx) if nx else [],
        compiler_params=_params(("arbitrary",)), name=name,
    )(*[a for a, _, _, _ in mats], *[w for _, w, _, _ in mats], *rows, *vecs, *exchange)
    return list(res[:nro + nao]), list(res[nro + nao:])


ROW_TILE = 512


def _vec_spec(width=D_MODEL):
    return pl.BlockSpec((1, width), lambda i: (0, 0))


def _rstd(v):
    return lax.rsqrt(jnp.mean(v * v, axis=-1, keepdims=True) + NORM_EPS)


def _mid_fwd_rows(y1, x0, g2, g3):
    x1 = x0 + y1 * _rstd(y1) * g2
    return y1, x1, x1 * _rstd(x1) * g3


def _rms_bwd_rows(dout, v, g):
    r = _rstd(v)
    n = v * r
    dn = dout * g
    dv = r * (dn - n * jnp.mean(dn * n, axis=-1, keepdims=True))
    dg = jnp.sum(dout * n, axis=0, keepdims=True)
    return dv, dg


def _loss_head_rows(y2, x1, tgt, g4):
    x2 = x1 + y2 * _rstd(y2) * g4
    err = x2 - tgt
    loss = 0.5 * jnp.sum(jnp.mean(err * err, axis=-1, keepdims=True), axis=0, keepdims=True)
    dx2 = err * (1.0 / D_MODEL)
    dy2, dg4 = _rms_bwd_rows(dx2, y2, g4)
    return dx2, dy2, dg4, loss


def _mid_bwd_rows(dh2, x1, y1, dx2, g2, g3):
    d3, dg3 = _rms_bwd_rows(dh2, x1, g3)
    dx1 = dx2 + d3
    dy1, dg2 = _rms_bwd_rows(dx1, y1, g2)
    return dx1, dy1, dg2, dg3


def _in_bwd_rows(dh1, x0, dx1, g1):
    d1, dg1 = _rms_bwd_rows(dh1, x0, g1)
    return dx1 + d1, dg1


GATE_ROWS = 512


def _group_mean_matrix():
    p = np.zeros((A_WIDTH, A_WIDTH), np.float32)
    for g in range(A_GROUPS):
        p[g * HEAD_DIM:(g + 1) * HEAD_DIM, g * HEAD_DIM:(g + 1) * HEAD_DIM] = 1.0 / HEAD_DIM
    return jnp.asarray(p)


def _group_masks(width=A_WIDTH):
    lane = lax.broadcasted_iota(jnp.int32, (1, width), 1)
    return [(lane >= g * HEAD_DIM) & (lane < (g + 1) * HEAD_DIM) for g in range(width // HEAD_DIM)]


GROUP_SUM_PRECISION = lax.Precision.HIGH


def _layernorm_groups(vg, pavg):
    hi = GROUP_SUM_PRECISION
    mu = _dot(vg, pavg, hi)
    xc = vg - mu
    var = _dot(xc * xc, pavg, hi)
    rstd = lax.rsqrt(var + NORM_EPS)
    return xc * rstd, rstd


def _spatial_mix(w_bf, vn_chunk_bf, masks, bz):
    z = bz
    for g in range(A_GROUPS):
        z = z + jnp.where(masks[g], _dot(w_bf[g], vn_chunk_bf), 0.0)
    return z


def _full_spec(shape):
    return pl.BlockSpec(shape, lambda i: tuple(0 for _ in shape))


def _gate_fwd(uv, ln_g, ln_b, w_s, bz):
    m = uv.shape[0]
    pavg = _group_mean_matrix()

    def body(u_ref, v_ref, lg_ref, lb_ref, w_ref, bz_ref, p_ref, a_ref):
        masks = _group_masks()
        row = lax.broadcasted_iota(jnp.int32, (CHUNK, CHUNK), 0)
        col = lax.broadcasted_iota(jnp.int32, (CHUNK, CHUNK), 1)
        w_bf = [jnp.where(row >= col, w_ref[g], 0.0).astype(BF16) for g in range(A_GROUPS)]
        ug = _gelu(u_ref[...])
        vhat, _ = _layernorm_groups(_gelu(v_ref[...]), p_ref[...])
        vn = vhat * lg_ref[...] + lb_ref[...]
        bz = bz_ref[...]
        for c in range(GATE_ROWS // CHUNK):
            sl = slice(c * CHUNK, (c + 1) * CHUNK)
            z = _spatial_mix(w_bf, vn[sl].astype(BF16), masks, bz)
            a_ref[sl, :] = (ug[sl] * z).astype(BF16)

    return pl.pallas_call(
        body, grid=(m // GATE_ROWS,),
        in_specs=[pl.BlockSpec((GATE_ROWS, A_WIDTH), lambda i: (i, 0)),
                  pl.BlockSpec((GATE_ROWS, A_WIDTH), lambda i: (i, 1)),
                  _full_spec((1, A_WIDTH)), _full_spec((1, A_WIDTH)), _full_spec((A_GROUPS, CHUNK, CHUNK)),
                  _full_spec((CHUNK, A_WIDTH)), _full_spec((A_WIDTH, A_WIDTH))],
        out_specs=pl.BlockSpec((GATE_ROWS, A_WIDTH), lambda i: (i, 0)),
        out_shape=jax.ShapeDtypeStruct((m, A_WIDTH), BF16),
        compiler_params=_params(("parallel",)), name="gate_fwd",
    )(uv, uv, ln_g, ln_b, w_s, bz, pavg)


def _gate_bwd(uv, dmix, ln_g, ln_b, w_s, w_st, bz, grads):
    m = uv.shape[0]
    pavg = _group_mean_matrix()
    nsteps = m // GATE_ROWS
    nx = len(grads)
    shapes = [g.shape for g in grads]

    def body(u_ref, v_ref, da_ref, lg_ref, lb_ref, w_ref, wt_ref, bz_ref, p_ref, *rest):
        grad_refs = rest[:nx]
        duv_ref, dlg_ref, dlb_ref, dw_ref, dbz_ref = rest[nx:nx + 5]
        recv_refs = rest[nx + 5:2 * nx + 5]
        exchange = _SiblingExchangePlan(shapes, grad_refs, recv_refs, *rest[2 * nx + 5:])
        i = pl.program_id(0)

        @pl.when(i == 0)
        def _():
            exchange.start()
            dlg_ref[...] = jnp.zeros_like(dlg_ref)
            dlb_ref[...] = jnp.zeros_like(dlb_ref)
            dw_ref[...] = jnp.zeros_like(dw_ref)
            dbz_ref[...] = jnp.zeros_like(dbz_ref)

        hi = GROUP_SUM_PRECISION
        masks = _group_masks()
        row = lax.broadcasted_iota(jnp.int32, (CHUNK, CHUNK), 0)
        col = lax.broadcasted_iota(jnp.int32, (CHUNK, CHUNK), 1)
        tril = row >= col
        w_bf = [jnp.where(tril, w_ref[g], 0.0).astype(BF16) for g in range(A_GROUPS)]
        wt_bf = [jnp.where(col >= row, wt_ref[g], 0.0).astype(BF16) for g in range(A_GROUPS)]
        pavg_v = p_ref[...]
        lg = lg_ref[...]
        ug, dug = _gelu_and_grad(u_ref[...])
        vg, dvg_dx = _gelu_and_grad(v_ref[...])
        vhat, rstd = _layernorm_groups(vg, pavg_v)
        vn = vhat * lg + lb_ref[...]
        da = da_ref[...]
        bz = bz_ref[...]
        for c in range(GATE_ROWS // CHUNK):
            sl = slice(c * CHUNK, (c + 1) * CHUNK)
            vn_bf = vn[sl].astype(BF16)
            z = _spatial_mix(w_bf, vn_bf, masks, bz)
            dz = da[sl] * ug[sl]
            duv_ref[sl, 0:A_WIDTH] = (da[sl] * z * dug[sl]).astype(BF16)
            dbz_ref[...] += dz
            dz_bf = dz.astype(BF16)
            dvn = jnp.zeros((CHUNK, A_WIDTH), F32)
            for g in range(A_GROUPS):
                dz_g = jnp.where(masks[g], dz, 0.0).astype(BF16)
                dw_ref[g] += jnp.where(tril, _dot_nt(dz_g, vn_bf), 0.0)
                dvn = dvn + jnp.where(masks[g], _dot(wt_bf[g], dz_bf), 0.0)
            vh = vhat[sl]
            dlb_ref[...] += jnp.sum(dvn, axis=0, keepdims=True)
            dlg_ref[...] += jnp.sum(dvn * vh, axis=0, keepdims=True)
            dvh = dvn * lg
            m1 = _dot(dvh, pavg_v, hi)
            m2 = _dot(dvh * vh, pavg_v, hi)
            duv_ref[sl, A_WIDTH:2 * A_WIDTH] = (rstd[sl] * (dvh - m1 - vh * m2) * dvg_dx[sl]).astype(BF16)

        @pl.when(i == nsteps - 1)
        def _():
            dbz_ref[...] = _dot(dbz_ref[...], pavg_v * float(HEAD_DIM), hi)
            exchange.finish()

    res = pl.pallas_call(
        body, grid=(nsteps,),
        in_specs=[pl.BlockSpec((GATE_ROWS, A_WIDTH), lambda i: (i, 0)),
                  pl.BlockSpec((GATE_ROWS, A_WIDTH), lambda i: (i, 1)),
                  pl.BlockSpec((GATE_ROWS, A_WIDTH), lambda i: (i, 0)),
                  _full_spec((1, A_WIDTH)), _full_spec((1, A_WIDTH)), _full_spec((A_GROUPS, CHUNK, CHUNK)),
                  _full_spec((A_GROUPS, CHUNK, CHUNK)), _full_spec((CHUNK, A_WIDTH)),
                  _full_spec((A_WIDTH, A_WIDTH))] + [ANY] * nx,
        out_specs=[pl.BlockSpec((GATE_ROWS, 2 * A_WIDTH), lambda i: (i, 0)),
                   _full_spec((1, A_WIDTH)), _full_spec((1, A_WIDTH)), _full_spec((A_GROUPS, CHUNK, CHUNK)),
                   _full_spec((CHUNK, A_WIDTH))] + [ANY] * nx,
        out_shape=[jax.ShapeDtypeStruct((m, IN_COLS), BF16),
                   jax.ShapeDtypeStruct((1, A_WIDTH), F32), jax.ShapeDtypeStruct((1, A_WIDTH), F32),
                   jax.ShapeDtypeStruct((A_GROUPS, CHUNK, CHUNK), F32),
                   jax.ShapeDtypeStruct((CHUNK, A_WIDTH), F32)]
        + [jax.ShapeDtypeStruct((N_SHARD, s[1] // 2, s[2]), F32) for s in shapes],
        scratch_shapes=_sem_pair(nx),
        compiler_params=_params(("arbitrary",)), name="gate_bwd",
    )(uv, uv, dmix, ln_g, ln_b, w_s, w_st, bz, pavg, *grads)
    return res[:5], list(res[5:])


Q_BLOCK = 128
PAIR = 2 * HEAD_DIM
N_PAIR = B_HEADS // 2
N_CFG = len(DILATED)
BLOCKS_PER_CFG = SEQ // Q_BLOCK
QKV_SLABS = 3 * N_PAIR
FWD_BLOCKS_PER_TRIP = 8
BWD_BLOCKS_PER_TRIP = 4


def _t5_bucket_np(dist, dtype):
    max_exact = NUM_BUCKETS // 2
    d = np.maximum(dist, 1).astype(dtype)
    large = max_exact + (np.log(d / dtype(max_exact)) / dtype(math.log(MAX_DISTANCE / max_exact))
                         * dtype(NUM_BUCKETS - max_exact))
    large = np.minimum(large.astype(np.int32), NUM_BUCKETS - 1)
    return np.where(dist < max_exact, dist, large)


def _bucket_tables():
    i = np.arange(Q_BLOCK)[:, None]
    j = np.arange(Q_BLOCK)[None, :]
    tables = []
    for _, dil in DILATED:
        rel_prev = Q_BLOCK + i - j
        rel_cur = i - j
        rel = np.concatenate([rel_prev, rel_cur], axis=1)
        valid = np.concatenate([rel_prev <= Q_BLOCK, rel_cur >= 0], axis=1)
        dist = np.maximum(rel, 0) * dil
        b32 = _t5_bucket_np(dist, np.float32)
        b64 = _t5_bucket_np(dist, np.float64)
        assert np.array_equal(b32, b64)
        tables.append(np.where(valid, b32, -1).astype(np.int32))
    return np.stack(tables)


def _present_buckets(buckets_np):
    return [sorted(set(int(v) for v in np.unique(buckets_np[c]) if v >= 0)) for c in range(N_CFG)]


def _bias_tables_body(buckets_np):
    present = _present_buckets(buckets_np)

    def tables(rb_ref, bk_ref, o_ref, ot_ref):
        for c in range(N_CFG):
            bk = bk_ref[c]
            for h in range(B_HEADS):
                acc = jnp.full((Q_BLOCK, 2 * Q_BLOCK), NEG_INF, F32)
                for b in present[c]:
                    acc = jnp.where(bk == b, rb_ref[b, h], acc)
                o_ref[c, h] = acc
                ot_ref[c, h] = acc.T

    return tables


def _proj_fwd(x, g1, w_in_t):
    m = x.shape[0]
    tm = ROW_TILE

    def body(x_ref, g_ref, w_ref, h_ref, uv_ref, qkv_ref):
        xv = x_ref[...]
        h = (xv * _rstd(xv) * g_ref[...]).astype(BF16)
        h_ref[...] = h
        acc = _dot_nt(h, w_ref[...])
        uv_ref[...] = acc[:, :2 * A_WIDTH]
        for s in range(QKV_SLABS):
            qkv_ref[s] = acc[:, 2 * A_WIDTH + s * PAIR:2 * A_WIDTH + (s + 1) * PAIR]

    return pl.pallas_call(
        body, grid=(m // tm,),
        in_specs=[pl.BlockSpec((tm, D_MODEL), lambda i: (i, 0)), _vec_spec(),
                  pl.BlockSpec((IN_COLS, D_MODEL), lambda i: (0, 0))],
        out_specs=[pl.BlockSpec((tm, D_MODEL), lambda i: (i, 0)),
                   pl.BlockSpec((tm, 2 * A_WIDTH), lambda i: (i, 0)),
                   pl.BlockSpec((QKV_SLABS, tm, PAIR), lambda i: (0, i, 0))],
        out_shape=[jax.ShapeDtypeStruct((m, D_MODEL), BF16), jax.ShapeDtypeStruct((m, 2 * A_WIDTH), F32),
                   jax.ShapeDtypeStruct((QKV_SLABS, m, PAIR), F32)],
        compiler_params=_params(("parallel",)), name="proj_fwd",
    )(x, g1, w_in_t)


def _pair_masks():
    lane = lax.broadcasted_iota(jnp.int32, (1, PAIR), 1)
    return [lane < HEAD_DIM, lane >= HEAD_DIM]


def _block_rows(idx, dil):
    static = isinstance(idx, int)
    r, n = idx % dil, idx // dil

    def rows_of(block):
        start = r + (dil * Q_BLOCK) * block
        if dil == 1:
            return pl.ds(start if static else pl.multiple_of(start, Q_BLOCK), Q_BLOCK)
        return pl.ds(start, Q_BLOCK, stride=dil)

    prev = rows_of(n - 1) if not static or n > 0 else None
    return rows_of(n), prev


def _attn_fwd(qkv, bias, batch, shards):
    m = qkv.shape[1]
    comb_rows = 256
    nt = len(shards)
    shapes = [sh.shape for sh in shards]
    n_steps = batch * N_PAIR
    early, late = list(range(nt // 2)), list(range(nt // 2, nt))

    def body(q_ref, k_ref, v_ref, b_ref, *rest):
        shard_refs = rest[:nt]
        o_ref, l_ref = rest[nt:nt + 2]
        gat_refs = rest[nt + 2:2 * nt + 2]
        scratch = rest[2 * nt + 2:]
        oc_refs, lc_refs = scratch[:N_CFG], scratch[N_CFG:2 * N_CFG]
        step = pl.program_id(0) * N_PAIR + pl.program_id(1)
        gather = _GatherPlan(shapes, shard_refs, gat_refs, *scratch[2 * N_CFG:])

        @pl.when(step == 0)
        def _():
            gather.start(early + late)

        @pl.when(step == n_steps // 2)
        def _():
            gather.forward(early)

        @pl.when(step == n_steps - 2)
        def _():
            gather.forward(late)

        masks = _pair_masks()
        for ci, (_, dil) in enumerate(DILATED):
            nb = SEQ // dil // Q_BLOCK

            def block(trip, ci=ci, dil=dil, nb=nb):
                work = []
                for u in range(FWD_BLOCKS_PER_TRIP):
                    rows, prow = _block_rows(trip * FWD_BLOCKS_PER_TRIP + u, dil)
                    has_prev = nb > 1 and prow is not None
                    q = q_ref[rows, :] * 0.125
                    kc = k_ref[rows, :].astype(BF16)
                    vc = v_ref[rows, :]
                    kp = k_ref[prow, :].astype(BF16) if has_prev else None
                    vp = v_ref[prow, :] if has_prev else None
                    tiles = []
                    for h in range(2):
                        qh = jnp.where(masks[h], q, 0.0).astype(BF16)
                        sc = _dot_nt(qh, kc) + b_ref[ci, h, :, Q_BLOCK:]
                        sp = _dot_nt(qh, kp) + b_ref[ci, h, :, :Q_BLOCK] if has_prev else None
                        tiles.append((sc, sp))
                    work.append((rows, vc, vp, tiles))
                probs = []
                for _, _, _, tiles in work:
                    ps = []
                    for sc, sp in tiles:
                        mx = jnp.max(sc if sp is None else jnp.maximum(sc, sp), axis=1, keepdims=True)
                        pc = jnp.exp(sc - mx).astype(BF16)
                        pp = None if sp is None else jnp.exp(sp - mx).astype(BF16)
                        ps.append((mx, pc, pp))
                    probs.append(ps)
                for (rows, vc, vp, _), ps in zip(work, probs):
                    res = []
                    for h, (_, pc, pp) in enumerate(ps):
                        r = _dot(pc, jnp.where(masks[h], vc, 1.0).astype(BF16))
                        if pp is not None:
                            r = r + _dot(pp, jnp.where(masks[h], vp, 1.0).astype(BF16))
                        res.append(r)
                    num = jnp.where(masks[0], res[0], res[1])
                    den = pltpu.roll(jnp.where(masks[0], res[1], res[0]), HEAD_DIM, 1)
                    oc_refs[ci][rows, :] = num / den
                    lc_refs[ci][rows, :] = jnp.where(masks[0], ps[0][0], ps[1][0]) + jnp.log(den)

            for trip in range(BLOCKS_PER_CFG // FWD_BLOCKS_PER_TRIP):
                block(trip)

        def combine(i, carry):
            rr = pl.ds(pl.multiple_of(i * comb_rows, comb_rows), comb_rows)
            ls = [lc_refs[c][rr, :] for c in range(N_CFG)]
            mx = functools.reduce(jnp.maximum, ls)
            ws = [jnp.exp(l - mx) for l in ls]
            tot = functools.reduce(lambda a, b: a + b, ws)
            o = functools.reduce(lambda a, b: a + b, [ws[c] * oc_refs[c][rr, :] for c in range(N_CFG)]) / tot
            o_ref[rr, :] = o.astype(BF16)
            l_ref[rr, :] = mx + jnp.log(tot)
            return carry

        lax.fori_loop(0, SEQ // comb_rows, combine, 0)

        @pl.when(step == n_steps - 1)
        def _():
            gather.finish(early + late)

    def slab(first):
        return pl.BlockSpec((None, SEQ, PAIR), lambda b, p: (first + p, b, 0))

    nat = pl.BlockSpec((SEQ, PAIR), lambda b, p: (b, p))
    res = pl.pallas_call(
        body, grid=(batch, N_PAIR),
        in_specs=[slab(0), slab(N_PAIR), slab(2 * N_PAIR),
                  pl.BlockSpec((N_CFG, 2, Q_BLOCK, 2 * Q_BLOCK), lambda b, p: (0, p, 0, 0))] + [ANY] * nt,
        out_specs=[nat, nat] + [ANY] * nt,
        out_shape=[jax.ShapeDtypeStruct((m, B_WIDTH), BF16), jax.ShapeDtypeStruct((m, B_WIDTH), F32)]
        + [jax.ShapeDtypeStruct((N_SHARD,) + sh.shape, sh.dtype) for sh in shards],
        scratch_shapes=[pltpu.VMEM((SEQ, PAIR), F32)] * (2 * N_CFG) + _sem_pair(6 * nt),
        compiler_params=_params(("arbitrary", "arbitrary")), name="attn_fwd",
    )(qkv, qkv, qkv, bias, *shards)
    return res[0], res[1], list(res[2:])


def _attn_bwd(qkv, dmix, o, lse, bias_t, dproj, batch, parts, smalls):
    m = qkv.shape[1]
    nt, ns = len(parts), len(smalls)
    n_steps = N_PAIR * batch

    def body(q_ref, k_ref, v_ref, do_ref, o_ref, l_ref, b_ref, *rest):
        part_refs = rest[1:nt + 1]
        small_refs = rest[nt + 1:nt + 1 + ns]
        pos = nt + 1 + ns
        dproj_ref, ds_ref = rest[pos:pos + 2]
        recv_refs = rest[pos + 2:pos + 2 + nt]
        sum_refs = rest[pos + 2 + nt:pos + 2 + nt + ns]
        pos += 2 + nt + ns
        dq_acc, dk_acc, dv_acc, d_scr, stage, stage_sems, send_sems, recv_sems = rest[pos:pos + 8]
        allreduce = _SmallAllReducePlan(small_refs, sum_refs, rest[pos + 8:pos + 8 + ns],
                                        rest[pos + 8 + ns:pos + 8 + 2 * ns], *rest[pos + 8 + 2 * ns:])
        pair, seq = pl.program_id(0), pl.program_id(1)
        step = pair * batch + seq
        exchange = _ChipExchangePlan(part_refs, recv_refs, send_sems, recv_sems)

        @pl.when(step == 0)
        def _():
            allreduce.start_sibling()

        @pl.when(step == 3)
        def _():
            allreduce.sum_sibling_and_start_chips()

        def stage_copies():
            rows = pl.ds(pl.multiple_of(seq * SEQ, SEQ), SEQ)
            return [pltpu.make_async_copy(
                stage.at[k],
                dproj_ref.at[rows, pl.ds(pl.multiple_of(2 * A_WIDTH + k * B_WIDTH + pair * PAIR, PAIR), PAIR)],
                stage_sems.at[k]) for k in range(3)]

        @pl.when(step == 0)
        def _():
            exchange.start()

        @pl.when(pl.program_id(1) == 0)
        def _():
            ds_ref[...] = jnp.zeros_like(ds_ref)

        dq_acc[...] = jnp.zeros_like(dq_acc)
        dk_acc[...] = jnp.zeros_like(dk_acc)
        dv_acc[...] = jnp.zeros_like(dv_acc)
        d_scr[...] = do_ref[...] * o_ref[...].astype(F32)
        masks = _pair_masks()

        def stack_heads(t):
            return jnp.concatenate([jnp.where(masks[0], t, 0.0), jnp.where(masks[1], t, 0.0)], axis=0).astype(BF16)

        for ci, (_, dil) in enumerate(DILATED):
            nb = SEQ // dil // Q_BLOCK

            def block(trip, carry, ci=ci, dil=dil, nb=nb):
                first = []
                for u in range(BWD_BLOCKS_PER_TRIP):
                    rows, prow = _block_rows(trip * BWD_BLOCKS_PER_TRIP + u, dil)
                    has_prev = nb > 1 and prow is not None
                    if has_prev:
                        kcat = jnp.concatenate([k_ref[prow, :], k_ref[rows, :]], axis=0).astype(BF16)
                        vcat = jnp.concatenate([v_ref[prow, :], v_ref[rows, :]], axis=0).astype(BF16)
                    else:
                        kcat = k_ref[rows, :].astype(BF16)
                        vcat = v_ref[rows, :].astype(BF16)
                    qst = stack_heads(q_ref[rows, :] * 0.125)
                    dost = stack_heads(do_ref[rows, :])
                    lt = l_ref[rows, :].T
                    dt = d_scr[rows, :].T
                    lrow = jnp.concatenate([lt[0:1], lt[HEAD_DIM:HEAD_DIM + 1]], axis=1)
                    drow = jnp.concatenate([jnp.sum(dt[:HEAD_DIM], axis=0, keepdims=True),
                                            jnp.sum(dt[HEAD_DIM:], axis=0, keepdims=True)], axis=1)
                    first.append((has_prev, rows, prow, kcat, qst, dost, lrow, drow,
                                  _dot_nt(kcat, qst), _dot_nt(vcat, dost)))
                second = []
                for has_prev, rows, prow, kcat, qst, dost, lrow, drow, st, dpt in first:
                    keys = slice(0, 2 * Q_BLOCK) if has_prev else slice(Q_BLOCK, 2 * Q_BLOCK)
                    bt = jnp.concatenate([b_ref[ci, 0, keys, :], b_ref[ci, 1, keys, :]], axis=1)
                    pt = jnp.exp(st + bt - lrow)
                    dst = pt * (dpt - drow)
                    ds_ref[ci, 0, keys, :] += dst[:, :Q_BLOCK]
                    ds_ref[ci, 1, keys, :] += dst[:, Q_BLOCK:]
                    second.append((has_prev, rows, prow, kcat, qst, dost, pt.astype(BF16), dst.astype(BF16)))
                for has_prev, rows, prow, kcat, qst, dost, pt_bf, dst_bf in second:
                    dk = _dot(dst_bf, qst)
                    dv = _dot(pt_bf, dost)
                    dq2 = _dot_tn(dst_bf, kcat)
                    dq_acc[rows, :] += jnp.where(masks[0], dq2[:Q_BLOCK], dq2[Q_BLOCK:]) * 0.125
                    if has_prev:
                        dk_acc[prow, :] += dk[:Q_BLOCK]
                        dv_acc[prow, :] += dv[:Q_BLOCK]
                        dk_acc[rows, :] += dk[Q_BLOCK:]
                        dv_acc[rows, :] += dv[Q_BLOCK:]
                    else:
                        dk_acc[rows, :] += dk
                        dv_acc[rows, :] += dv
                return carry

            block(0, 0)
            lax.fori_loop(1, BLOCKS_PER_CFG // BWD_BLOCKS_PER_TRIP, block, 0)

        @pl.when(step > 0)
        def _():
            for cp in stage_copies():
                cp.wait()

        stage[0] = dq_acc[...].astype(BF16)
        stage[1] = dk_acc[...].astype(BF16)
        stage[2] = dv_acc[...].astype(BF16)
        for cp in stage_copies():
            cp.start()

        @pl.when(step == n_steps - 1)
        def _():
            for cp in stage_copies():
                cp.wait()
            exchange.finish()
            allreduce.finish()

    def slab(first):
        return pl.BlockSpec((None, SEQ, PAIR), lambda p, b: (first + p, b, 0))

    nat = pl.BlockSpec((SEQ, PAIR), lambda p, b: (b, p))
    tbl = pl.BlockSpec((N_CFG, 2, 2 * Q_BLOCK, Q_BLOCK), lambda p, b: (0, p, 0, 0))
    acc = pltpu.VMEM((SEQ, PAIR), F32)
    vm = pl.BlockSpec(memory_space=pltpu.VMEM)
    res = pl.pallas_call(
        body, grid=(N_PAIR, batch),
        in_specs=[slab(0), slab(N_PAIR), slab(2 * N_PAIR),
                  pl.BlockSpec((SEQ, PAIR), lambda p, b: (b, A_WIDTH // PAIR + p)), nat, nat, tbl]
        + [ANY] * (nt + 1) + [vm] * ns,
        out_specs=[ANY, tbl] + [ANY] * nt + [vm] * ns,
        out_shape=[jax.ShapeDtypeStruct(dproj.shape, dproj.dtype),
                   jax.ShapeDtypeStruct((N_CFG, B_HEADS, 2 * Q_BLOCK, Q_BLOCK), F32)]
        + [jax.ShapeDtypeStruct((3,) + p.shape[1:], p.dtype) for p in parts]
        + [jax.ShapeDtypeStruct(a.shape, F32) for a in smalls],
        input_output_aliases={7: 0},
        scratch_shapes=[acc, acc, acc, acc, pltpu.VMEM((3, SEQ, PAIR), BF16), pltpu.SemaphoreType.DMA((3,))]
        + _sem_pair(3 * nt) + _SmallAllReducePlan.scratch(smalls),
        compiler_params=_params(("arbitrary", "arbitrary")), name="attn_bwd",
    )(qkv, qkv, qkv, dmix, o, lse, bias_t, dproj, *parts, *smalls)
    return res[0], res[1], list(res[2:2 + nt]), list(res[2 + nt:])


def _rel_bias_grad(ds, buckets_np, grads):
    present = _present_buckets(buckets_np)
    nx = len(grads)
    shapes = [g.shape for g in grads]

    def body(bk_ref, ds_ref, *rest):
        o_ref = rest[nx]
        acc_ref = rest[2 * nx + 1]
        exchange = _SiblingExchangePlan(shapes, rest[:nx], rest[nx + 1:2 * nx + 1], *rest[2 * nx + 2:])
        exchange.start()
        acc_ref[...] = jnp.zeros_like(acc_ref)
        for c in range(N_CFG):
            bk = bk_ref[c]
            for h in range(B_HEADS):
                dsv = ds_ref[c, h]
                for b in present[c]:
                    part = jnp.sum(jnp.where(bk == b, dsv, 0.0), axis=0, keepdims=True)
                    acc_ref[pl.ds(h * NUM_BUCKETS + b, 1), :] += part
        o_ref[...] = jnp.sum(acc_ref[...], axis=1, keepdims=True)
        exchange.finish()

    vm = pl.BlockSpec(memory_space=pltpu.VMEM)
    res = pl.pallas_call(
        body, in_specs=[vm, vm] + [ANY] * nx, out_specs=[vm] + [ANY] * nx,
        out_shape=[jax.ShapeDtypeStruct((B_HEADS * NUM_BUCKETS, 1), F32)]
        + [jax.ShapeDtypeStruct((N_SHARD, s[1] // 2, s[2]), F32) for s in shapes],
        scratch_shapes=[pltpu.VMEM((B_HEADS * NUM_BUCKETS, buckets_np.shape[-1]), F32)] + _sem_pair(nx),
        compiler_params=_params(), name="rel_bias_grad",
    )(jnp.asarray(buckets_np), ds, *grads)
    return res[0], list(res[1:])


def _row_index():
    return lax.broadcasted_iota(jnp.int32, (SEQ, LANE_BLOCK), 0)


def _shift_down(x, k, row):
    return jnp.where(row >= k, pltpu.roll(x, k, 0), 0.0)


def _shift_up(x, k, row):
    return jnp.where(row < SEQ - k, pltpu.roll(x, SEQ - k, 0), 0.0)


def _convgate_fwd(gate, up, conv_w, conv_b, batch, shard):
    m = gate.shape[0]
    n_col = D_FF // LANE_BLOCK
    n_steps = batch * n_col

    def body(g_ref, u_ref, w_ref, b_ref, shard_ref, a_ref, gat_ref, send_sems, recv_sems):
        step = pl.program_id(0) * n_col + pl.program_id(1)
        gather = _GatherPlan([shard.shape], [shard_ref], [gat_ref], send_sems, recv_sems)

        @pl.when(step == 0)
        def _():
            gather.start([0])

        @pl.when(step == (2 * n_steps) // 3)
        def _():
            gather.forward([0])

        g = g_ref[...].astype(F32)
        w = w_ref[...]
        row = _row_index()
        c = b_ref[...] + w[0:1] * _shift_down(g, 2, row) + w[1:2] * _shift_down(g, 1, row) + w[2:3] * g
        a_ref[...] = (_gelu(c) * u_ref[...].astype(F32)).astype(BF16)

        @pl.when(step == n_steps - 1)
        def _():
            gather.finish([0])

    blk = pl.BlockSpec((SEQ, LANE_BLOCK), lambda b, j: (b, j))
    return pl.pallas_call(
        body, grid=(batch, n_col),
        in_specs=[blk, blk, pl.BlockSpec((3, LANE_BLOCK), lambda b, j: (0, j)),
                  pl.BlockSpec((1, LANE_BLOCK), lambda b, j: (0, j)), ANY],
        out_specs=[blk, ANY],
        out_shape=[jax.ShapeDtypeStruct((m, D_FF), BF16),
                   jax.ShapeDtypeStruct((N_SHARD,) + shard.shape, shard.dtype)],
        scratch_shapes=_sem_pair(6),
        compiler_params=_params(("arbitrary", "arbitrary")), name="convgate_fwd",
    )(gate, up, conv_w, conv_b, shard)


def _convgate_bwd(gate, up, dact, conv_w, conv_b, batch):
    m = gate.shape[0]

    def body(g_ref, u_ref, da_ref, w_ref, b_ref, dg_ref, du_ref, dw_ref, db_ref):
        @pl.when(pl.program_id(1) == 0)
        def _():
            dw_ref[...] = jnp.zeros_like(dw_ref)
            db_ref[...] = jnp.zeros_like(db_ref)

        g = g_ref[...].astype(F32)
        w = w_ref[...]
        row = _row_index()
        g1 = _shift_down(g, 1, row)
        g2 = _shift_down(g, 2, row)
        c = b_ref[...] + w[0:1] * g2 + w[1:2] * g1 + w[2:3] * g
        gg, dgg = _gelu_and_grad(c)
        da = da_ref[...].astype(F32)
        du_ref[...] = (da * gg).astype(BF16)
        dc = da * u_ref[...].astype(F32) * dgg
        db_ref[...] += jnp.sum(dc, axis=0, keepdims=True)
        dw_ref[0:1, :] += jnp.sum(dc * g2, axis=0, keepdims=True)
        dw_ref[1:2, :] += jnp.sum(dc * g1, axis=0, keepdims=True)
        dw_ref[2:3, :] += jnp.sum(dc * g, axis=0, keepdims=True)
        dg_ref[...] = (w[2:3] * dc + w[1:2] * _shift_up(dc, 1, row) + w[0:1] * _shift_up(dc, 2, row)).astype(BF16)

    blk = pl.BlockSpec((SEQ, LANE_BLOCK), lambda j, b: (b, j))
    wspec = pl.BlockSpec((3, LANE_BLOCK), lambda j, b: (0, j))
    bspec = pl.BlockSpec((1, LANE_BLOCK), lambda j, b: (0, j))
    return pl.pallas_call(
        body, grid=(D_FF // LANE_BLOCK, batch),
        in_specs=[blk, blk, blk, wspec, bspec],
        out_specs=[blk, blk, wspec, bspec],
        out_shape=[jax.ShapeDtypeStruct((m, D_FF), BF16), jax.ShapeDtypeStruct((m, D_FF), BF16),
                   jax.ShapeDtypeStruct((3, D_FF), F32), jax.ShapeDtypeStruct((1, D_FF), F32)],
        compiler_params=_params(("parallel", "arbitrary")), name="convgate_bwd",
    )(gate, up, dact, conv_w, conv_b)


def _gather_weights(shards, conv_w_shard, rel_bias, buckets_np):
    nt = len(shards)
    shapes = [sh.shape for sh in shards]
    ts = list(range(nt))
    tables = _bias_tables_body(buckets_np)

    def body(*refs):
        shard_refs = refs[:nt]
        cw_ref, rb_ref, bk_ref = refs[nt:nt + 3]
        out_refs = refs[nt + 3:2 * nt + 3]
        cw_out, bias_ref, bias_t_ref = refs[2 * nt + 3:2 * nt + 6]
        send_sems, recv_sems, cw_send, cw_recv = refs[2 * nt + 6:]
        plan = _GatherPlan(shapes, shard_refs, out_refs, send_sems, recv_sems)
        x, y, c, chips = _mesh_pos()

        def cw_copy(j, src, dst, chip):
            return pltpu.make_async_remote_copy(src_ref=src, dst_ref=dst, send_sem=cw_send.at[j],
                                                recv_sem=cw_recv.at[j], device_id=(*chip, c), device_id_type=MESH)

        plan.start(ts)
        cw_sends = [cw_copy(j, cw_ref, cw_out.at[2 * x + y], chip) for j, chip in enumerate(chips)]
        for cp in cw_sends:
            cp.start()
        tables(rb_ref, bk_ref, bias_ref, bias_t_ref)
        plan.forward(ts)
        for j, chip in enumerate(chips):
            dst = cw_out.at[2 * chip[0] + chip[1]]
            cw_copy(j, dst, dst, chip).wait_recv()
        plan.finish(ts)
        for cp in cw_sends:
            cp.wait_send()

    out_shape = [jax.ShapeDtypeStruct((N_SHARD,) + sh.shape, sh.dtype) for sh in shards]
    out_shape.append(jax.ShapeDtypeStruct((N_SHARD,) + conv_w_shard.shape, conv_w_shard.dtype))
    out_shape += [jax.ShapeDtypeStruct((N_CFG, B_HEADS, Q_BLOCK, 2 * Q_BLOCK), F32),
                  jax.ShapeDtypeStruct((N_CFG, B_HEADS, 2 * Q_BLOCK, Q_BLOCK), F32)]
    vm = pl.BlockSpec(memory_space=pltpu.VMEM)
    res = pl.pallas_call(
        body, in_specs=[ANY] * (nt + 1) + [pl.BlockSpec(memory_space=pltpu.SMEM), vm],
        out_specs=[ANY] * (nt + 1) + [vm, vm], out_shape=out_shape,
        scratch_shapes=_sem_pair(6 * nt) + _sem_pair(3),
        compiler_params=pltpu.CompilerParams(has_side_effects=True, vmem_limit_bytes=VMEM_LIMIT),
        name="gather_weights",
    )(*shards, conv_w_shard, rel_bias, jnp.asarray(buckets_np))
    return list(res[:nt + 1]), res[nt + 1], res[nt + 2]


def _add_halves(g, recv, c_idx):
    _, rows2, cols = g.shape
    rows = rows2 // 2
    tr = rows
    nblk = rows // tr

    def body(c_ref, g_ref, r_ref, o_ref):
        o_ref[...] = (g_ref[...] + r_ref[...]).astype(BF16)

    return pl.pallas_call(
        body,
        grid_spec=pltpu.PrefetchScalarGridSpec(
            num_scalar_prefetch=1, grid=(N_SHARD, nblk),
            in_specs=[pl.BlockSpec((None, tr, cols), lambda s, i, c: (s, c[0] * nblk + i, 0)),
                      pl.BlockSpec((None, tr, cols), lambda s, i, c: (s, i, 0))],
            out_specs=pl.BlockSpec((None, tr, cols), lambda s, i, c: (s, i, 0))),
        out_shape=jax.ShapeDtypeStruct((N_SHARD, rows, cols), BF16),
        compiler_params=_params(("parallel", "parallel")), name="rs_add_halves",
    )(c_idx, g, recv)


def _add_chips(part, recv, s_idx, c_idx):
    _, rows, cols = part.shape
    tr = rows
    nblk = rows // tr

    def body(idx_ref, p_ref, r_ref, o_ref):
        acc = p_ref[...].astype(F32)
        for j in range(3):
            acc = acc + r_ref[j].astype(F32)
        o_ref[...] = acc

    return pl.pallas_call(
        body,
        grid_spec=pltpu.PrefetchScalarGridSpec(
            num_scalar_prefetch=1, grid=(nblk,),
            in_specs=[pl.BlockSpec((None, tr, cols), lambda i, idx: (idx[0], i, 0)),
                      pl.BlockSpec((3, tr, cols), lambda i, idx: (0, i, 0))],
            out_specs=pl.BlockSpec((tr, cols), lambda i, idx: (idx[1] * nblk + i, 0))),
        out_shape=jax.ShapeDtypeStruct((2 * rows, cols), F32),
        compiler_params=_params(("parallel",)), name="rs_add_chips",
    )(jnp.concatenate([s_idx, c_idx]), part, recv)


def _finish_reductions(fulls, arrays):
    nt, n = len(fulls), len(arrays)

    def body(*refs):
        in_refs = refs[nt:nt + n]
        full_refs, out_refs = refs[nt + n:2 * nt + n], refs[2 * nt + n:2 * nt + 2 * n]
        pos = 2 * nt + 2 * n
        share_send, share_recv = refs[pos + 2 * n:pos + 2 * n + 2]
        allreduce = _SmallAllReducePlan(in_refs, out_refs, refs[pos:pos + n], refs[pos + n:pos + 2 * n],
                                        *refs[pos + 2 * n + 2:])
        x, y, c, _ = _mesh_pos()

        def half(t, which):
            rows = fulls[t].shape[0] // 2
            return full_refs[t].at[pl.ds(which * rows, rows), :]

        def share(t, which):
            return pltpu.make_async_remote_copy(
                src_ref=half(t, which), dst_ref=half(t, which), send_sem=share_send.at[t],
                recv_sem=share_recv.at[t], device_id=(x, y, 1 - c), device_id_type=MESH)

        for t in range(nt):
            share(t, c).start()
        allreduce.start_sibling()
        allreduce.sum_sibling_and_start_chips()
        allreduce.finish()
        for t in range(nt):
            share(t, 1 - c).wait_recv()
        for t in range(nt):
            share(t, c).wait_send()

    vm = pl.BlockSpec(memory_space=pltpu.VMEM)
    res = pl.pallas_call(
        body, in_specs=[ANY] * nt + [vm] * n, out_specs=[ANY] * nt + [vm] * n,
        out_shape=[jax.ShapeDtypeStruct(f.shape, f.dtype) for f in fulls]
        + [jax.ShapeDtypeStruct(a.shape, F32) for a in arrays],
        input_output_aliases={t: t for t in range(nt)},
        scratch_shapes=[pltpu.VMEM(a.shape, F32) for a in arrays] + [pltpu.VMEM((3,) + a.shape, F32) for a in arrays]
        + _sem_pair(nt) + _sem_pair(4 * n),
        compiler_params=pltpu.CompilerParams(has_side_effects=True),
        name="finish_reductions",
    )(*fulls, *arrays)
    return list(res[:nt]), list(res[nt:])


def _from_col_shards(g):
    n, rows, cols = g.shape
    return g.transpose(1, 0, 2).reshape(rows, n * cols)


def _train_step(x, tgt, g1, g2, g3, g4, shards, ln_g, ln_b, w_s, b_s, rel_bias, conv_w_shard, conv_b, batch,
                s_idx, c_idx):
    big = dict(tm=1024, out_dtype=F32)
    buckets = _bucket_tables()
    bz = jnp.repeat(b_s.T, HEAD_DIM, axis=1)
    w_st = jnp.swapaxes(w_s, 1, 2)

    def with_own(gathered, own):
        return lax.dynamic_update_index_in_dim(gathered, own, s_idx[0], 0)

    def shard_major(g):
        return g.reshape(N_SHARD, g.shape[0] // N_SHARD, D_MODEL)

    (g_in, g_convw), bias, bias_t = _gather_weights([shards["w_in"]], conv_w_shard, rel_bias, buckets)
    w_in_t = with_own(g_in, shards["w_in"]).reshape(IN_COLS, D_MODEL)
    conv_w = _from_col_shards(with_own(g_convw, conv_w_shard))

    h1, uv, qkv = _proj_fwd(x, g1, w_in_t)
    a = _gate_fwd(uv, ln_g, ln_b, w_s, bz)
    later = ["w_out", "w_gate", "w_up"]
    o_bf, lse, gathered = _attn_fwd(qkv, bias, batch, [shards[n] for n in later])
    g_out, g_gate, g_up = [with_own(g, shards[n]) for g, n in zip(gathered, later)]
    w_out = g_out.reshape(D_MODEL, D_MODEL)
    w_gate_t = g_gate.reshape(D_FF, D_MODEL)
    w_up_t = g_up.reshape(D_FF, D_MODEL)
    (y1, x1, h2), _ = _fused_rows(
        "out_proj_mid_fwd", 512,
        [(a, w_out, "nn", slice(0, A_WIDTH)), (o_bf, w_out, "nn", slice(A_WIDTH, D_MODEL))],
        [x], [g2, g3], _mid_fwd_rows, [F32, F32, BF16], [])
    gate = _mm(h2, w_gate_t, dims="nt", tm=1024, tn=1408, tk=1024, out_dtype=BF16, name="mm_gate")
    up = _mm(h2, w_up_t, dims="nt", tm=1024, tn=1408, tk=1024, out_dtype=BF16, name="mm_up")
    act, g_down = _convgate_fwd(gate, up, conv_w, conv_b, batch, shards["w_down"])
    w_down = with_own(g_down, shards["w_down"]).reshape(D_FF, D_MODEL)
    (dx2, dy2, dg4, loss), _ = _fused_rows(
        "down_proj_loss_head", 512, [(act, w_down, "nn", None)], [x1, tgt], [g4], _loss_head_rows,
        [F32, BF16], [(1, D_MODEL), (1, 128)])

    dact = _mm(dy2, w_down, dims="nt", tm=1024, tn=1408, tk=1024, out_dtype=BF16, name="mm_dact")
    dw_down = _mm(act, dy2, dims="tn", tm=1408, tn=1024, tk=1024, out_dtype=F32, name="mm_dw_down")
    dgate, dup, dconv_w, dconv_b = _convgate_bwd(gate, up, dact, conv_w, conv_b, batch)
    (dx1, dy1, dg2, dg3), _ = _fused_rows(
        "dh2_mid_bwd", 256, [(dgate, w_gate_t, "nn", None), (dup, w_up_t, "nn", None)],
        [x1, y1, dx2], [g2, g3], _mid_bwd_rows, [F32, BF16], [(1, D_MODEL), (1, D_MODEL)])
    dw_gate_t = _mm(dgate, h2, dims="tn", tm=1408, tn=1024, tk=1024, out_dtype=F32, name="mm_dw_gate")
    dw_up_t = _mm(dup, h2, dims="tn", tm=1408, tn=1024, tk=1024, out_dtype=F32, name="mm_dw_up")
    dmix = _mm(dy1, w_out, dims="nt", tn=1024, tk=1024, name="mm_dmix", **big)
    dw_out_a = _mm(a, dy1, dims="tn", tm=A_WIDTH, tn=1024, tk=1024, out_dtype=F32, name="mm_dw_out_a")
    dw_out_b = _mm(o_bf, dy1, dims="tn", tm=B_WIDTH, tn=1024, tk=1024, out_dtype=F32, name="mm_dw_out_b")

    dw_out = jnp.concatenate([dw_out_a, dw_out_b], axis=0)
    done = [shard_major(g) for g in (dw_down, dw_gate_t, dw_up_t, dw_out)]
    (dproj, dln_g, dln_b, dw_s, dbz), recv_a = _gate_bwd(uv, dmix, ln_g, ln_b, w_s, w_st, bz, done)
    parts = [_add_halves(g, r, c_idx) for g, r in zip(done, recv_a)]
    early = dict(loss=loss, norm_mix_post=dg2, norm_ffn_pre=dg3, norm_ffn_post=dg4, ln_v_gain=dln_g,
                 ln_v_bias=dln_b, spatial_w=dw_s, spatial_b=dbz, conv_w=dconv_w, conv_b=dconv_b)
    dproj, ds, recv, early_sums = _attn_bwd(qkv, dmix, o_bf, lse, bias_t, dproj, batch, parts, list(early.values()))
    fulls = [_add_chips(p, r, s_idx, c_idx) for p, r in zip(parts, recv)]
    dw_in_t = _mm(dproj, h1, dims="tn", tm=1408, tn=1024, tk=1024, out_dtype=F32, name="mm_dw_in")
    last = [shard_major(dw_in_t)]
    drel, recv_in_a = _rel_bias_grad(ds, np.ascontiguousarray(np.swapaxes(buckets, 1, 2)), last)
    part_in = [_add_halves(g, r, c_idx) for g, r in zip(last, recv_in_a)]
    (dx0, dg1), recv_in = _fused_rows(
        "dh1_in_bwd", 512, [(dproj, w_in_t, "nn", None)], [x, dx1], [g1], _in_bwd_rows,
        [F32], [(1, D_MODEL)], exchange=part_in)
    fulls += [_add_chips(p, r, s_idx, c_idx) for p, r in zip(part_in, recv_in)]
    half_reduced = dict(zip(["w_down", "w_gate", "w_up", "w_out", "w_in"], fulls))

    return dx0, dict(zip(early, early_sums)), dict(norm_mix_pre=dg1, rel_bias=drel), half_reduced


def _adamw_update(w, g, m, v):
    nm = ADAM_B1 * m + (1.0 - ADAM_B1) * g
    nv = ADAM_B2 * v + (1.0 - ADAM_B2) * (g * g)
    m_hat = nm / (1.0 - ADAM_B1 ** ADAM_STEP)
    v_hat = nv / (1.0 - ADAM_B2 ** ADAM_STEP)
    return -ADAM_LR * (m_hat / (jnp.sqrt(v_hat) + ADAM_EPS) + ADAM_WD * w), nm, nv


def _adamw(w, g, m, v, name):
    rows, cols = w.shape
    tr = next(cand for cand in (352, 256, 128) if rows % cand == 0)

    def body(w_ref, g_ref, m_ref, v_ref, go_ref, d_ref, nm_ref, nv_ref):
        gv = g_ref[...]
        go_ref[...] = gv
        d_ref[...], nm_ref[...], nv_ref[...] = _adamw_update(w_ref[...], gv, m_ref[...], v_ref[...])

    spec = pl.BlockSpec((tr, cols), lambda i: (i, 0))
    sds = jax.ShapeDtypeStruct((rows, cols), F32)
    return pl.pallas_call(
        body, grid=(rows // tr,), in_specs=[spec] * 4, out_specs=[spec] * 4, out_shape=[sds] * 4,
        compiler_params=_params(("parallel",)), name=name,
    )(w, g, m, v)


def _adamw_small(ws, gs, ms, vs):
    n = len(ws)

    def body(*refs):
        w_refs, g_refs, m_refs, v_refs = refs[:n], refs[n:2 * n], refs[2 * n:3 * n], refs[3 * n:4 * n]
        d_refs, nm_refs, nv_refs = refs[4 * n:5 * n], refs[5 * n:6 * n], refs[6 * n:7 * n]
        for t in range(n):
            d_refs[t][...], nm_refs[t][...], nv_refs[t][...] = _adamw_update(
                w_refs[t][...], g_refs[t][...], m_refs[t][...], v_refs[t][...])

    vm = pl.BlockSpec(memory_space=pltpu.VMEM)
    sds = [jax.ShapeDtypeStruct(w.shape, F32) for w in ws]
    res = pl.pallas_call(
        body, in_specs=[vm] * (4 * n), out_specs=[vm] * (3 * n), out_shape=sds * 3,
        compiler_params=_params(), name="adamw_small",
    )(*ws, *gs, *ms, *vs)
    return res[:n], res[n:2 * n], res[2 * n:]


SMALL = ["norm_mix_pre", "norm_mix_post", "norm_ffn_pre", "norm_ffn_post", "ln_v_gain", "ln_v_bias",
         "spatial_w", "spatial_b", "rel_bias", "conv_b"]
LARGE = ["w_in", "w_gate", "w_up", "w_down", "w_out"]
TRANSPOSED = ("w_in", "w_gate", "w_up")
ORDER = ["norm_mix_pre", "norm_mix_post", "norm_ffn_pre", "norm_ffn_post", "w_in", "ln_v_gain", "ln_v_bias",
         "spatial_w", "spatial_b", "rel_bias", "w_out", "w_gate", "w_up", "conv_w", "conv_b", "w_down"]


def kernel(x, norm_mix_pre, norm_mix_post, norm_ffn_pre, norm_ffn_post, w_in, ln_v_gain, ln_v_bias, spatial_w, spatial_b, rel_bias, w_out, w_gate, w_up, conv_w, conv_b, w_down, loss_target, m_norm_mix_pre, m_norm_mix_post, m_norm_ffn_pre, m_norm_ffn_post, m_w_in, m_ln_v_gain, m_ln_v_bias, m_spatial_w, m_spatial_b, m_rel_bias, m_w_out, m_w_gate, m_w_up, m_conv_w, m_conv_b, m_w_down, v_norm_mix_pre, v_norm_mix_post, v_norm_ffn_pre, v_norm_ffn_post, v_w_in, v_ln_v_gain, v_ln_v_bias, v_spatial_w, v_spatial_b, v_rel_bias, v_w_out, v_w_gate, v_w_up, v_conv_w, v_conv_b, v_w_down):
    params = dict(norm_mix_pre=norm_mix_pre, norm_mix_post=norm_mix_post, norm_ffn_pre=norm_ffn_pre,
                  norm_ffn_post=norm_ffn_post, w_in=w_in, ln_v_gain=ln_v_gain, ln_v_bias=ln_v_bias,
                  spatial_w=spatial_w, spatial_b=spatial_b, rel_bias=rel_bias, w_out=w_out, w_gate=w_gate,
                  w_up=w_up, conv_w=conv_w, conv_b=conv_b, w_down=w_down)
    mom = dict(norm_mix_pre=m_norm_mix_pre, norm_mix_post=m_norm_mix_post, norm_ffn_pre=m_norm_ffn_pre,
               norm_ffn_post=m_norm_ffn_post, w_in=m_w_in, ln_v_gain=m_ln_v_gain, ln_v_bias=m_ln_v_bias,
               spatial_w=m_spatial_w, spatial_b=m_spatial_b, rel_bias=m_rel_bias, w_out=m_w_out, w_gate=m_w_gate,
               w_up=m_w_up, conv_w=m_conv_w, conv_b=m_conv_b, w_down=m_w_down)
    var = dict(norm_mix_pre=v_norm_mix_pre, norm_mix_post=v_norm_mix_post, norm_ffn_pre=v_norm_ffn_pre,
               norm_ffn_post=v_norm_ffn_post, w_in=v_w_in, ln_v_gain=v_ln_v_gain, ln_v_bias=v_ln_v_bias,
               spatial_w=v_spatial_w, spatial_b=v_spatial_b, rel_bias=v_rel_bias, w_out=v_w_out, w_gate=v_w_gate,
               w_up=v_w_up, conv_w=v_conv_w, conv_b=v_conv_b, w_down=v_w_down)

    batch = x.shape[0]
    xi, yi, ci = lax.axis_index("x"), lax.axis_index("y"), lax.axis_index("c")
    s_idx = (2 * xi + yi).astype(jnp.int32).reshape(1)
    c_idx = ci.astype(jnp.int32).reshape(1)

    def local(a, n):
        return jnp.swapaxes(a[0], 0, 1) if n in TRANSPOSED else a[0]

    shards = {n: local(params[n], n).astype(BF16) for n in LARGE}
    dx0, total, partial, half_reduced = _train_step(
        x.reshape(batch * SEQ, D_MODEL), loss_target.reshape(batch * SEQ, D_MODEL),
        norm_mix_pre, norm_mix_post, norm_ffn_pre, norm_ffn_post, shards,
        ln_v_gain.reshape(1, A_WIDTH), ln_v_bias.reshape(1, A_WIDTH), spatial_w[0], spatial_b[0], rel_bias,
        conv_w[0], conv_b, batch, s_idx, c_idx)
    grad_x = dx0.reshape(batch, SEQ, D_MODEL)

    names = list(partial)
    fulls, sums = _finish_reductions([half_reduced[n] for n in LARGE], [partial[n] for n in names])
    reduced = dict(zip(LARGE, fulls))
    total.update(zip(names, sums))
    loss = total["loss"][0, 0]
    total["spatial_b"] = total["spatial_b"][:, ::HEAD_DIM].T
    total["rel_bias"] = total["rel_bias"].reshape(B_HEADS, NUM_BUCKETS).T
    total["conv_w"] = lax.dynamic_slice_in_dim(total["conv_w"], s_idx[0] * SHARD_FF, SHARD_FF, axis=1)
    small_names = SMALL + ["conv_w"]
    for n in small_names:
        reduced[n] = total[n].reshape(params[n].shape)

    out_g, out_d, out_m, out_v = {}, {}, {}, {}
    for n in LARGE:
        res = _adamw(local(params[n], n), reduced[n], local(mom[n], n), local(var[n], n), name=f"adamw_{n}")
        if n in TRANSPOSED:
            res = [jnp.swapaxes(r, 0, 1) for r in res]
        out_g[n], out_d[n], out_m[n], out_v[n] = [r[None] for r in res]
    d, nm, nv = _adamw_small([params[n] for n in small_names], [reduced[n] for n in small_names],
                             [mom[n] for n in small_names], [var[n] for n in small_names])
    for n, dd, mm, vv in zip(small_names, d, nm, nv):
        out_g[n], out_d[n], out_m[n], out_v[n] = reduced[n], dd, mm, vv

    return (loss, grad_x, *[out_g[n] for n in ORDER], *[out_d[n] for n in ORDER],
            *[out_m[n] for n in ORDER], *[out_v[n] for n in ORDER])
```

```python
import functools
import math

import numpy as np
import jax
import jax.numpy as jnp
from jax import lax
from jax.experimental import pallas as pl
from jax.experimental.pallas import tpu as pltpu

F32 = jnp.float32
BF16 = jnp.bfloat16
MESH = pl.DeviceIdType.MESH

D_MODEL = 1024
SEQ = 2048
HEAD_DIM = 64
A_GROUPS = 4
A_WIDTH = 256
B_HEADS = 12
B_WIDTH = 768
CHUNK = 128
DILATED = ((128, 1), (512, 4), (2048, 16))
NUM_BUCKETS = 32
MAX_DISTANCE = 2048
D_FF = 2816
IN_COLS = 2816
NORM_EPS = 1e-6
NEG_INF = -1e30
N_SHARD = 4
SHARD_FF = D_FF // N_SHARD
LANE_BLOCK = 256
VMEM_LIMIT = 56 * 1024 * 1024

ADAM_LR = 0.001
ADAM_B1 = 0.9
ADAM_B2 = 0.999
ADAM_EPS = 1e-08
ADAM_WD = 0.01
ADAM_STEP = 10

GELU_C = math.sqrt(2.0 / math.pi)
GELU_A = 0.044715

ANY = pl.BlockSpec(memory_space=pl.ANY)


def _params(sem=None):
    return pltpu.CompilerParams(dimension_semantics=sem, vmem_limit_bytes=VMEM_LIMIT)


def _dot(a, b, precision=None):
    return jnp.dot(a, b, preferred_element_type=F32, precision=precision)


def _dot_nt(a, b, precision=None):
    return lax.dot_general(a, b, (((1,), (1,)), ((), ())), preferred_element_type=F32, precision=precision)


def _dot_tn(a, b):
    return lax.dot_general(a, b, (((0,), (0,)), ((), ())), preferred_element_type=F32)


def _gelu(x):
    t = jnp.tanh(x * (GELU_C + (GELU_C * GELU_A) * (x * x)))
    return (0.5 * x) * (1.0 + t)


def _gelu_and_grad(x):
    x2 = x * x
    u = 1.0 + jnp.tanh(x * (GELU_C + (GELU_C * GELU_A) * x2))
    hx = 0.5 * x
    dg = u * (0.5 + hx * (2.0 - u) * (GELU_C + (3.0 * GELU_C * GELU_A) * x2))
    return hx * u, dg


def _mesh_pos():
    x, y, c = lax.axis_index("x"), lax.axis_index("y"), lax.axis_index("c")
    chips = [(1 - x, y), (x, 1 - y), (1 - x, 1 - y)]
    return x, y, c, chips


class _GatherPlan:
    def __init__(self, shapes, shard_refs, out_refs, send_sems, recv_sems):
        self.shapes, self.shard_refs, self.out_refs = shapes, shard_refs, out_refs
        self.send_sems, self.recv_sems = send_sems, recv_sems
        self.x, self.y, self.c, self.chips = _mesh_pos()
        self.sib = (self.x, self.y, 1 - self.c)

    def _half(self, t, chip, which):
        rows = self.shapes[t][0] // 2
        return self.out_refs[t].at[2 * chip[0] + chip[1], pl.ds(which * rows, rows), :]

    def _copy(self, k, src, dst, to):
        return pltpu.make_async_remote_copy(src_ref=src, dst_ref=dst, send_sem=self.send_sems.at[k],
                                            recv_sem=self.recv_sems.at[k], device_id=to, device_id_type=MESH)

    def _sends(self, t):
        rows = self.shapes[t][0] // 2
        src = self.shard_refs[t].at[pl.ds(self.c * rows, rows), :]
        return [self._copy(6 * t + j, src, self._half(t, (self.x, self.y), self.c), (*chip, self.c))
                for j, chip in enumerate(self.chips)]

    def _forwards(self, t):
        return [self._copy(6 * t + 3 + j, self._half(t, chip, self.c), self._half(t, chip, self.c), self.sib)
                for j, chip in enumerate(self.chips)]

    def start(self, ts):
        for t in ts:
            for cp in self._sends(t):
                cp.start()

    def forward(self, ts):
        for t in ts:
            for j, chip in enumerate(self.chips):
                landed = self._half(t, chip, self.c)
                self._copy(6 * t + j, landed, landed, (*chip, self.c)).wait_recv()
            for cp in self._forwards(t):
                cp.start()

    def finish(self, ts):
        for t in ts:
            for j, chip in enumerate(self.chips):
                other = self._half(t, chip, 1 - self.c)
                self._copy(6 * t + 3 + j, other, other, self.sib).wait_recv()
        for t in ts:
            for cp in self._sends(t) + self._forwards(t):
                cp.wait_send()


class _SiblingExchangePlan:
    def __init__(self, shapes, grad_refs, out_refs, send_sems, recv_sems):
        self.shapes, self.grad_refs, self.out_refs = shapes, grad_refs, out_refs
        self.send_sems, self.recv_sems = send_sems, recv_sems
        self.x, self.y, self.c, _ = _mesh_pos()

    def _copies(self):
        out = []
        for t, (g, o) in enumerate(zip(self.grad_refs, self.out_refs)):
            rows = self.shapes[t][1] // 2
            out.append(pltpu.make_async_remote_copy(
                src_ref=g.at[:, pl.ds((1 - self.c) * rows, rows), :], dst_ref=o, send_sem=self.send_sems.at[t],
                recv_sem=self.recv_sems.at[t], device_id=(self.x, self.y, 1 - self.c), device_id_type=MESH))
        return out

    def start(self):
        for cp in self._copies():
            cp.start()

    def finish(self):
        for cp in self._copies():
            cp.wait()


class _ChipExchangePlan:
    def __init__(self, part_refs, out_refs, send_sems, recv_sems):
        self.part_refs, self.out_refs, self.send_sems, self.recv_sems = part_refs, out_refs, send_sems, recv_sems
        _, _, self.c, self.chips = _mesh_pos()

    def _copies(self):
        return [pltpu.make_async_remote_copy(
            src_ref=p.at[2 * chip[0] + chip[1]], dst_ref=o.at[j], send_sem=self.send_sems.at[3 * t + j],
            recv_sem=self.recv_sems.at[3 * t + j], device_id=(*chip, self.c), device_id_type=MESH)
            for t, (p, o) in enumerate(zip(self.part_refs, self.out_refs)) for j, chip in enumerate(self.chips)]

    def start(self):
        for cp in self._copies():
            cp.start()

    def finish(self):
        for cp in self._copies():
            cp.wait()


class _SmallAllReducePlan:
    def __init__(self, in_refs, out_refs, sib_refs, chip_refs, send_sems, recv_sems):
        self.in_refs, self.out_refs, self.sib_refs, self.chip_refs = in_refs, out_refs, sib_refs, chip_refs
        self.send_sems, self.recv_sems = send_sems, recv_sems
        self.n = len(in_refs)
        self.x, self.y, self.c, self.chips = _mesh_pos()

    def _copy(self, k, src, dst, to):
        return pltpu.make_async_remote_copy(src_ref=src, dst_ref=dst, send_sem=self.send_sems.at[k],
                                            recv_sem=self.recv_sems.at[k], device_id=to, device_id_type=MESH)

    def _first(self):
        return [self._copy(t, self.in_refs[t], self.sib_refs[t], (self.x, self.y, 1 - self.c)) for t in range(self.n)]

    def _second(self):
        return [self._copy(self.n + 3 * t + j, self.out_refs[t], self.chip_refs[t].at[j], (*chip, self.c))
                for t in range(self.n) for j, chip in enumerate(self.chips)]

    def start_sibling(self):
        for cp in self._first():
            cp.start()

    def sum_sibling_and_start_chips(self):
        for cp in self._first():
            cp.wait()
        for t in range(self.n):
            self.out_refs[t][...] = self.in_refs[t][...] + self.sib_refs[t][...]
        for cp in self._second():
            cp.start()

    def finish(self):
        for cp in self._second():
            cp.wait()
        for t in range(self.n):
            self.out_refs[t][...] = ((self.out_refs[t][...] + self.chip_refs[t][0])
                                     + (self.chip_refs[t][1] + self.chip_refs[t][2]))

    @staticmethod
    def scratch(arrays):
        return ([pltpu.VMEM(a.shape, F32) for a in arrays] + [pltpu.VMEM((3,) + a.shape, F32) for a in arrays]
                + _sem_pair(4 * len(arrays)))


def _sem_pair(n):
    return [pltpu.SemaphoreType.DMA((n,)), pltpu.SemaphoreType.DMA((n,))]


def _mm(a, b, *, dims, tm, tn, tk, out_dtype, name):
    if dims == "nn":
        m, k = a.shape
        n = b.shape[1]
        a_spec = pl.BlockSpec((tm, tk), lambda i, j, kk: (i, kk))
        b_spec = pl.BlockSpec((tk, tn), lambda i, j, kk: (kk, j))
        dot = _dot
    elif dims == "nt":
        m, k = a.shape
        n = b.shape[0]
        a_spec = pl.BlockSpec((tm, tk), lambda i, j, kk: (i, kk))
        b_spec = pl.BlockSpec((tn, tk), lambda i, j, kk: (j, kk))
        dot = _dot_nt
    else:
        k, m = a.shape
        n = b.shape[1]
        a_spec = pl.BlockSpec((tk, tm), lambda i, j, kk: (kk, i))
        b_spec = pl.BlockSpec((tk, tn), lambda i, j, kk: (kk, j))
        dot = _dot_tn
    assert m % tm == 0 and n % tn == 0 and k % tk == 0, (name, m, n, k)
    grid = (m // tm, n // tn, k // tk)
    nk = grid[2]
    assert nk == 1 or out_dtype == F32, name

    def body(a_ref, b_ref, o_ref):
        prod = dot(a_ref[...].astype(BF16), b_ref[...].astype(BF16))
        if nk == 1:
            o_ref[...] = prod.astype(out_dtype)
        else:
            kk = pl.program_id(2)

            @pl.when(kk == 0)
            def _():
                o_ref[...] = prod

            @pl.when(kk > 0)
            def _():
                o_ref[...] += prod

    return pl.pallas_call(
        body, grid=grid, in_specs=[a_spec, b_spec],
        out_specs=pl.BlockSpec((tm, tn), lambda i, j, kk: (i, j)),
        out_shape=jax.ShapeDtypeStruct((m, n), out_dtype),
        compiler_params=_params(("parallel", "parallel", "arbitrary")), name=name,
    )(a, b)


def _mm_pair_nt(a, w1_t, w2_t, shard, *, tm, tn, out_dtype, name):
    m, k = a.shape
    n = w1_t.shape[0]
    assert m % tm == 0 and n % tn == 0 and w2_t.shape == w1_t.shape, name
    grid = (m // tm, n // tn)
    n_steps = grid[0] * grid[1]

    def body(a_ref, w1_ref, w2_ref, shard_ref, o1_ref, o2_ref, gat_ref, send_sems, recv_sems):
        step = pl.program_id(0) * grid[1] + pl.program_id(1)
        gather = _GatherPlan([shard.shape], [shard_ref], [gat_ref], send_sems, recv_sems)

        @pl.when(step == 0)
        def _():
            gather.start([0])

        @pl.when(step == (2 * n_steps) // 3)
        def _():
            gather.forward([0])

        av = a_ref[...]
        o1_ref[...] = _dot_nt(av, w1_ref[...]).astype(out_dtype)
        o2_ref[...] = _dot_nt(av, w2_ref[...]).astype(out_dtype)

        @pl.when(step == n_steps - 1)
        def _():
            gather.finish([0])

    w_spec = pl.BlockSpec((tn, k), lambda i, j: (j, 0))
    o_spec = pl.BlockSpec((tm, tn), lambda i, j: (i, j))
    return pl.pallas_call(
        body, grid=grid,
        in_specs=[pl.BlockSpec((tm, k), lambda i, j: (i, 0)), w_spec, w_spec, ANY],
        out_specs=[o_spec, o_spec, ANY],
        out_shape=[jax.ShapeDtypeStruct((m, n), out_dtype)] * 2
        + [jax.ShapeDtypeStruct((N_SHARD,) + shard.shape, shard.dtype)],
        scratch_shapes=_sem_pair(6),
        compiler_params=_params(("arbitrary", "arbitrary")), name=name,
    )(a, w1_t, w2_t, shard)


def _fused_rows(name, tm, mats, rows, vecs, fn, row_outs, acc_outs, exchange=()):
    m = mats[0][0].shape[0]
    nm, nr, nv, nro, nao, nx = len(mats), len(rows), len(vecs), len(row_outs), len(acc_outs), len(exchange)
    n_steps = m // tm

    def body(*refs):
        a_refs, w_refs = refs[:nm], refs[nm:2 * nm]
        pos = 2 * nm
        row_refs, vec_refs, part_refs = refs[pos:pos + nr], refs[pos + nr:pos + nr + nv], refs[pos + nr + nv:pos + nr + nv + nx]
        pos += nr + nv + nx
        out_refs, acc_refs, recv_refs = refs[pos:pos + nro], refs[pos + nro:pos + nro + nao], refs[pos + nro + nao:pos + nro + nao + nx]
        sems = refs[pos + nro + nao + nx:]
        i = pl.program_id(0)
        if nx:
            plan = _ChipExchangePlan(part_refs, recv_refs, *sems)

            @pl.when(i == 0)
            def _():
                plan.start()

        @pl.when(i == 0)
        def _():
            for r in acc_refs:
                r[...] = jnp.zeros_like(r)

        y = None
        for a_ref, w_ref, (_, _, dims, sl) in zip(a_refs, w_refs, mats):
            w = w_ref[...] if sl is None else w_ref[sl, :]
            part = (_dot if dims == "nn" else _dot_nt)(a_ref[...], w)
            y = part if y is None else y + part
        res = fn(y, *[r[...] for r in row_refs], *[v[...] for v in vec_refs])
        for r, val in zip(out_refs, res[:nro]):
            r[...] = val.astype(r.dtype)
        for r, val in zip(acc_refs, res[nro:]):
            r[...] += val

        if nx:
            @pl.when(i == n_steps - 1)
            def _():
                plan.finish()

    tile = lambda width: pl.BlockSpec((tm, width), lambda i: (i, 0))
    res = pl.pallas_call(
        body, grid=(n_steps,),
        in_specs=[tile(a.shape[1]) for a, _, _, _ in mats] + [_full_spec(w.shape) for _, w, _, _ in mats]
        + [tile(D_MODEL)] * nr + [_full_spec((1, D_MODEL))] * nv + [ANY] * nx,
        out_specs=[tile(D_MODEL)] * nro + [_full_spec(s) for s in acc_outs] + [ANY] * nx,
        out_shape=[jax.ShapeDtypeStruct((m, D_MODEL), dt) for dt in row_outs]
        + [jax.ShapeDtypeStruct(s, F32) for s in acc_outs]
        + [jax.ShapeDtypeStruct((3,) + p.shape[1:], p.dtype) for p in exchange],
        scratch_shapes=_sem_pair(3 * nx) if nx else [],
        compiler_params=_params(("arbitrary",)), name=name,
    )(*[a for a, _, _, _ in mats], *[w for _, w, _, _ in mats], *rows, *vecs, *exchange)
    return list(res[:nro + nao]), list(res[nro + nao:])


ROW_TILE = 512


def _vec_spec(width=D_MODEL):
    return pl.BlockSpec((1, width), lambda i: (0, 0))


def _rstd(v):
    return lax.rsqrt(jnp.mean(v * v, axis=-1, keepdims=True) + NORM_EPS)


def _mid_fwd_rows(y1, x0, g2, g3):
    x1 = x0 + y1 * _rstd(y1) * g2
    return y1, x1, x1 * _rstd(x1) * g3


def _rms_bwd_rows(dout, v, g):
    r = _rstd(v)
    n = v * r
    dn = dout * g
    dv = r * (dn - n * jnp.mean(dn * n, axis=-1, keepdims=True))
    dg = jnp.sum(dout * n, axis=0, keepdims=True)
    return dv, dg


def _loss_head_rows(y2, x1, tgt, g4):
    x2 = x1 + y2 * _rstd(y2) * g4
    err = x2 - tgt
    loss = 0.5 * jnp.sum(jnp.mean(err * err, axis=-1, keepdims=True), axis=0, keepdims=True)
    dx2 = err * (1.0 / D_MODEL)
    dy2, dg4 = _rms_bwd_rows(dx2, y2, g4)
    return dx2, dy2, dg4, loss


def _mid_bwd_rows(dh2, x1, y1, dx2, g2, g3):
    d3, dg3 = _rms_bwd_rows(dh2, x1, g3)
    dx1 = dx2 + d3
    dy1, dg2 = _rms_bwd_rows(dx1, y1, g2)
    return dx1, dy1, dg2, dg3


def _in_bwd_rows(dh1, x0, dx1, g1):
    d1, dg1 = _rms_bwd_rows(dh1, x0, g1)
    return dx1 + d1, dg1


GATE_ROWS = 512


def _group_mean_matrix():
    p = np.zeros((A_WIDTH, A_WIDTH), np.float32)
    for g in range(A_GROUPS):
        p[g * HEAD_DIM:(g + 1) * HEAD_DIM, g * HEAD_DIM:(g + 1) * HEAD_DIM] = 1.0 / HEAD_DIM
    return jnp.asarray(p)


def _group_masks(width=A_WIDTH):
    lane = lax.broadcasted_iota(jnp.int32, (1, width), 1)
    return [(lane >= g * HEAD_DIM) & (lane < (g + 1) * HEAD_DIM) for g in range(width // HEAD_DIM)]


GROUP_SUM_PRECISION = lax.Precision.HIGH


def _layernorm_groups(vg, pavg):
    hi = GROUP_SUM_PRECISION
    mu = _dot(vg, pavg, hi)
    xc = vg - mu
    var = _dot(xc * xc, pavg, hi)
    rstd = lax.rsqrt(var + NORM_EPS)
    return xc * rstd, rstd


def _spatial_mix(w_bf, vn_chunk_bf, masks, bz):
    z = bz
    for g in range(A_GROUPS):
        z = z + jnp.where(masks[g], _dot(w_bf[g], vn_chunk_bf), 0.0)
    return z


def _full_spec(shape):
    return pl.BlockSpec(shape, lambda i: tuple(0 for _ in shape))


def _gate_fwd(uv, ln_g, ln_b, w_s, bz):
    m = uv.shape[0]
    pavg = _group_mean_matrix()

    def body(u_ref, v_ref, lg_ref, lb_ref, w_ref, bz_ref, p_ref, a_ref):
        masks = _group_masks()
        row = lax.broadcasted_iota(jnp.int32, (CHUNK, CHUNK), 0)
        col = lax.broadcasted_iota(jnp.int32, (CHUNK, CHUNK), 1)
        w_bf = [jnp.where(row >= col, w_ref[g], 0.0).astype(BF16) for g in range(A_GROUPS)]
        ug = _gelu(u_ref[...])
        vhat, _ = _layernorm_groups(_gelu(v_ref[...]), p_ref[...])
        vn = vhat * lg_ref[...] + lb_ref[...]
        bz = bz_ref[...]
        for c in range(GATE_ROWS // CHUNK):
            sl = slice(c * CHUNK, (c + 1) * CHUNK)
            z = _spatial_mix(w_bf, vn[sl].astype(BF16), masks, bz)
            a_ref[sl, :] = (ug[sl] * z).astype(BF16)

    return pl.pallas_call(
        body, grid=(m // GATE_ROWS,),
        in_specs=[pl.BlockSpec((GATE_ROWS, A_WIDTH), lambda i: (i, 0)),
                  pl.BlockSpec((GATE_ROWS, A_WIDTH), lambda i: (i, 1)),
                  _full_spec((1, A_WIDTH)), _full_spec((1, A_WIDTH)), _full_spec((A_GROUPS, CHUNK, CHUNK)),
                  _full_spec((CHUNK, A_WIDTH)), _full_spec((A_WIDTH, A_WIDTH))],
        out_specs=pl.BlockSpec((GATE_ROWS, A_WIDTH), lambda i: (i, 0)),
        out_shape=jax.ShapeDtypeStruct((m, A_WIDTH), BF16),
        compiler_params=_params(("parallel",)), name="gate_fwd",
    )(uv, uv, ln_g, ln_b, w_s, bz, pavg)


def _gate_bwd(uv, dmix, ln_g, ln_b, w_s, w_st, bz, grads):
    m = uv.shape[0]
    pavg = _group_mean_matrix()
    nsteps = m // GATE_ROWS
    nx = len(grads)
    shapes = [g.shape for g in grads]

    def body(u_ref, v_ref, da_ref, lg_ref, lb_ref, w_ref, wt_ref, bz_ref, p_ref, *rest):
        grad_refs = rest[:nx]
        duv_ref, dlg_ref, dlb_ref, dw_ref, dbz_ref = rest[nx:nx + 5]
        recv_refs = rest[nx + 5:2 * nx + 5]
        exchange = _SiblingExchangePlan(shapes, grad_refs, recv_refs, *rest[2 * nx + 5:])
        i = pl.program_id(0)

        @pl.when(i == 0)
        def _():
            exchange.start()
            dlg_ref[...] = jnp.zeros_like(dlg_ref)
            dlb_ref[...] = jnp.zeros_like(dlb_ref)
            dw_ref[...] = jnp.zeros_like(dw_ref)
            dbz_ref[...] = jnp.zeros_like(dbz_ref)

        hi = GROUP_SUM_PRECISION
        masks = _group_masks()
        row = lax.broadcasted_iota(jnp.int32, (CHUNK, CHUNK), 0)
        col = lax.broadcasted_iota(jnp.int32, (CHUNK, CHUNK), 1)
        tril = row >= col
        w_bf = [jnp.where(tril, w_ref[g], 0.0).astype(BF16) for g in range(A_GROUPS)]
        wt_bf = [jnp.where(col >= row, wt_ref[g], 0.0).astype(BF16) for g in range(A_GROUPS)]
        pavg_v = p_ref[...]
        lg = lg_ref[...]
        ug, dug = _gelu_and_grad(u_ref[...])
        vg, dvg_dx = _gelu_and_grad(v_ref[...])
        vhat, rstd = _layernorm_groups(vg, pavg_v)
        vn = vhat * lg + lb_ref[...]
        da = da_ref[...]
        bz = bz_ref[...]
        for c in range(GATE_ROWS // CHUNK):
            sl = slice(c * CHUNK, (c + 1) * CHUNK)
            vn_bf = vn[sl].astype(BF16)
            z = _spatial_mix(w_bf, vn_bf, masks, bz)
            dz = da[sl] * ug[sl]
            duv_ref[sl, 0:A_WIDTH] = (da[sl] * z * dug[sl]).astype(BF16)
            dbz_ref[...] += dz
            dz_bf = dz.astype(BF16)
            dvn = jnp.zeros((CHUNK, A_WIDTH), F32)
            for g in range(A_GROUPS):
                dz_g = jnp.where(masks[g], dz, 0.0).astype(BF16)
                dw_ref[g] += jnp.where(tril, _dot_nt(dz_g, vn_bf), 0.0)
                dvn = dvn + jnp.where(masks[g], _dot(wt_bf[g], dz_bf), 0.0)
            vh = vhat[sl]
            dlb_ref[...] += jnp.sum(dvn, axis=0, keepdims=True)
            dlg_ref[...] += jnp.sum(dvn * vh, axis=0, keepdims=True)
            dvh = dvn * lg
            m1 = _dot(dvh, pavg_v, hi)
            m2 = _dot(dvh * vh, pavg_v, hi)
            duv_ref[sl, A_WIDTH:2 * A_WIDTH] = (rstd[sl] * (dvh - m1 - vh * m2) * dvg_dx[sl]).astype(BF16)

        @pl.when(i == nsteps - 1)
        def _():
            dbz_ref[...] = _dot(dbz_ref[...], pavg_v * float(HEAD_DIM), hi)
            exchange.finish()

    res = pl.pallas_call(
        body, grid=(nsteps,),
        in_specs=[pl.BlockSpec((GATE_ROWS, A_WIDTH), lambda i: (i, 0)),
                  pl.BlockSpec((GATE_ROWS, A_WIDTH), lambda i: (i, 1)),
                  pl.BlockSpec((GATE_ROWS, A_WIDTH), lambda i: (i, 0)),
                  _full_spec((1, A_WIDTH)), _full_spec((1, A_WIDTH)), _full_spec((A_GROUPS, CHUNK, CHUNK)),
                  _full_spec((A_GROUPS, CHUNK, CHUNK)), _full_spec((CHUNK, A_WIDTH)),
                  _full_spec((A_WIDTH, A_WIDTH))] + [ANY] * nx,
        out_specs=[pl.BlockSpec((GATE_ROWS, 2 * A_WIDTH), lambda i: (i, 0)),
                   _full_spec((1, A_WIDTH)), _full_spec((1, A_WIDTH)), _full_spec((A_GROUPS, CHUNK, CHUNK)),
                   _full_spec((CHUNK, A_WIDTH))] + [ANY] * nx,
        out_shape=[jax.ShapeDtypeStruct((m, IN_COLS), BF16),
                   jax.ShapeDtypeStruct((1, A_WIDTH), F32), jax.ShapeDtypeStruct((1, A_WIDTH), F32),
                   jax.ShapeDtypeStruct((A_GROUPS, CHUNK, CHUNK), F32),
                   jax.ShapeDtypeStruct((CHUNK, A_WIDTH), F32)]
        + [jax.ShapeDtypeStruct((N_SHARD, s[1] // 2, s[2]), F32) for s in shapes],
        scratch_shapes=_sem_pair(nx),
        compiler_params=_params(("arbitrary",)), name="gate_bwd",
    )(uv, uv, dmix, ln_g, ln_b, w_s, w_st, bz, pavg, *grads)
    return res[:5], list(res[5:])


Q_BLOCK = 128
PAIR = 2 * HEAD_DIM
N_PAIR = B_HEADS // 2
N_CFG = len(DILATED)
BLOCKS_PER_CFG = SEQ // Q_BLOCK
QKV_SLABS = 3 * N_PAIR
FWD_BLOCKS_PER_TRIP = 8
BWD_BLOCKS_PER_TRIP = 4


def _t5_bucket_np(dist, dtype):
    max_exact = NUM_BUCKETS // 2
    d = np.maximum(dist, 1).astype(dtype)
    large = max_exact + (np.log(d / dtype(max_exact)) / dtype(math.log(MAX_DISTANCE / max_exact))
                         * dtype(NUM_BUCKETS - max_exact))
    large = np.minimum(large.astype(np.int32), NUM_BUCKETS - 1)
    return np.where(dist < max_exact, dist, large)


def _bucket_tables():
    i = np.arange(Q_BLOCK)[:, None]
    j = np.arange(Q_BLOCK)[None, :]
    tables = []
    for _, dil in DILATED:
        rel_prev = Q_BLOCK + i - j
        rel_cur = i - j
        rel = np.concatenate([rel_prev, rel_cur], axis=1)
        valid = np.concatenate([rel_prev <= Q_BLOCK, rel_cur >= 0], axis=1)
        dist = np.maximum(rel, 0) * dil
        b32 = _t5_bucket_np(dist, np.float32)
        b64 = _t5_bucket_np(dist, np.float64)
        assert np.array_equal(b32, b64)
        tables.append(np.where(valid, b32, -1).astype(np.int32))
    return np.stack(tables)


def _present_buckets(buckets_np):
    return [sorted(set(int(v) for v in np.unique(buckets_np[c]) if v >= 0)) for c in range(N_CFG)]


def _bias_tables_body(buckets_np):
    present = _present_buckets(buckets_np)

    def tables(rb_ref, bk_ref, o_ref, ot_ref):
        for c in range(N_CFG):
            bk = bk_ref[c]
            for h in range(B_HEADS):
                acc = jnp.full((Q_BLOCK, 2 * Q_BLOCK), NEG_INF, F32)
                for b in present[c]:
                    acc = jnp.where(bk == b, rb_ref[b, h], acc)
                o_ref[c, h] = acc
                ot_ref[c, h] = acc.T

    return tables


def _proj_fwd(x, g1, w_in_t):
    m = x.shape[0]
    tm = ROW_TILE

    def body(x_ref, g_ref, w_ref, h_ref, uv_ref, qkv_ref):
        xv = x_ref[...]
        h = (xv * _rstd(xv) * g_ref[...]).astype(BF16)
        h_ref[...] = h
        acc = _dot_nt(h, w_ref[...])
        uv_ref[...] = acc[:, :2 * A_WIDTH]
        for s in range(QKV_SLABS):
            qkv_ref[s] = acc[:, 2 * A_WIDTH + s * PAIR:2 * A_WIDTH + (s + 1) * PAIR]

    return pl.pallas_call(
        body, grid=(m // tm,),
        in_specs=[pl.BlockSpec((tm, D_MODEL), lambda i: (i, 0)), _vec_spec(),
                  pl.BlockSpec((IN_COLS, D_MODEL), lambda i: (0, 0))],
        out_specs=[pl.BlockSpec((tm, D_MODEL), lambda i: (i, 0)),
                   pl.BlockSpec((tm, 2 * A_WIDTH), lambda i: (i, 0)),
                   pl.BlockSpec((QKV_SLABS, tm, PAIR), lambda i: (0, i, 0))],
        out_shape=[jax.ShapeDtypeStruct((m, D_MODEL), BF16), jax.ShapeDtypeStruct((m, 2 * A_WIDTH), F32),
                   jax.ShapeDtypeStruct((QKV_SLABS, m, PAIR), F32)],
        compiler_params=_params(("parallel",)), name="proj_fwd",
    )(x, g1, w_in_t)


def _pair_masks():
    lane = lax.broadcasted_iota(jnp.int32, (1, PAIR), 1)
    return [lane < HEAD_DIM, lane >= HEAD_DIM]


def _block_rows(idx, dil):
    static = isinstance(idx, int)
    r, n = idx % dil, idx // dil

    def rows_of(block):
        start = r + (dil * Q_BLOCK) * block
        if dil == 1:
            return pl.ds(start if static else pl.multiple_of(start, Q_BLOCK), Q_BLOCK)
        return pl.ds(start, Q_BLOCK, stride=dil)

    prev = rows_of(n - 1) if not static or n > 0 else None
    return rows_of(n), prev


def _attn_fwd(qkv, bias, batch, shards):
    m = qkv.shape[1]
    comb_rows = 256
    nt = len(shards)
    shapes = [sh.shape for sh in shards]
    n_steps = batch * N_PAIR
    early, late = list(range(nt // 2)), list(range(nt // 2, nt))

    def body(q_ref, k_ref, v_ref, b_ref, *rest):
        shard_refs = rest[:nt]
        o_ref, l_ref = rest[nt:nt + 2]
        gat_refs = rest[nt + 2:2 * nt + 2]
        scratch = rest[2 * nt + 2:]
        oc_refs, lc_refs = scratch[:N_CFG], scratch[N_CFG:2 * N_CFG]
        step = pl.program_id(0) * N_PAIR + pl.program_id(1)
        gather = _GatherPlan(shapes, shard_refs, gat_refs, *scratch[2 * N_CFG:])

        @pl.when(step == 0)
        def _():
            gather.start(early + late)

        @pl.when(step == n_steps // 2)
        def _():
            gather.forward(early)

        @pl.when(step == n_steps - 2)
        def _():
            gather.forward(late)

        masks = _pair_masks()
        for ci, (_, dil) in enumerate(DILATED):
            nb = SEQ // dil // Q_BLOCK

            def block(trip, ci=ci, dil=dil, nb=nb):
                work = []
                for u in range(FWD_BLOCKS_PER_TRIP):
                    rows, prow = _block_rows(trip * FWD_BLOCKS_PER_TRIP + u, dil)
                    has_prev = nb > 1 and prow is not None
                    q = q_ref[rows, :] * 0.125
                    kc = k_ref[rows, :].astype(BF16)
                    vc = v_ref[rows, :]
                    kp = k_ref[prow, :].astype(BF16) if has_prev else None
                    vp = v_ref[prow, :] if has_prev else None
                    tiles = []
                    for h in range(2):
                        qh = jnp.where(masks[h], q, 0.0).astype(BF16)
                        sc = _dot_nt(qh, kc) + b_ref[ci, h, :, Q_BLOCK:]
                        sp = _dot_nt(qh, kp) + b_ref[ci, h, :, :Q_BLOCK] if has_prev else None
                        tiles.append((sc, sp))
                    work.append((rows, vc, vp, tiles))
                probs = []
                for _, _, _, tiles in work:
                    ps = []
                    for sc, sp in tiles:
                        mx = jnp.max(sc if sp is None else jnp.maximum(sc, sp), axis=1, keepdims=True)
                        pc = jnp.exp(sc - mx).astype(BF16)
                        pp = None if sp is None else jnp.exp(sp - mx).astype(BF16)
                        ps.append((mx, pc, pp))
                    probs.append(ps)
                for (rows, vc, vp, _), ps in zip(work, probs):
                    res = []
                    for h, (_, pc, pp) in enumerate(ps):
                        r = _dot(pc, jnp.where(masks[h], vc, 1.0).astype(BF16))
                        if pp is not None:
                            r = r + _dot(pp, jnp.where(masks[h], vp, 1.0).astype(BF16))
                        res.append(r)
                    num = jnp.where(masks[0], res[0], res[1])
                    den = pltpu.roll(jnp.where(masks[0], res[1], res[0]), HEAD_DIM, 1)
                    oc_refs[ci][rows, :] = num / den
                    lc_refs[ci][rows, :] = jnp.where(masks[0], ps[0][0], ps[1][0]) + jnp.log(den)

            for trip in range(BLOCKS_PER_CFG // FWD_BLOCKS_PER_TRIP):
                block(trip)

        def combine(i, carry):
            rr = pl.ds(pl.multiple_of(i * comb_rows, comb_rows), comb_rows)
            ls = [lc_refs[c][rr, :] for c in range(N_CFG)]
            mx = functools.reduce(jnp.maximum, ls)
            ws = [jnp.exp(l - mx) for l in ls]
            tot = functools.reduce(lambda a, b: a + b, ws)
            o = functools.reduce(lambda a, b: a + b, [ws[c] * oc_refs[c][rr, :] for c in range(N_CFG)]) / tot
            o_ref[rr, :] = o.astype(BF16)
            l_ref[rr, :] = mx + jnp.log(tot)
            return carry

        lax.fori_loop(0, SEQ // comb_rows, combine, 0)

        @pl.when(step == n_steps - 1)
        def _():
            gather.finish(early + late)

    def slab(first):
        return pl.BlockSpec((None, SEQ, PAIR), lambda b, p: (first + p, b, 0))

    nat = pl.BlockSpec((SEQ, PAIR), lambda b, p: (b, p))
    res = pl.pallas_call(
        body, grid=(batch, N_PAIR),
        in_specs=[slab(0), slab(N_PAIR), slab(2 * N_PAIR),
                  pl.BlockSpec((N_CFG, 2, Q_BLOCK, 2 * Q_BLOCK), lambda b, p: (0, p, 0, 0))] + [ANY] * nt,
        out_specs=[nat, nat] + [ANY] * nt,
        out_shape=[jax.ShapeDtypeStruct((m, B_WIDTH), BF16), jax.ShapeDtypeStruct((m, B_WIDTH), F32)]
        + [jax.ShapeDtypeStruct((N_SHARD,) + sh.shape, sh.dtype) for sh in shards],
        scratch_shapes=[pltpu.VMEM((SEQ, PAIR), F32)] * (2 * N_CFG) + _sem_pair(6 * nt),
        compiler_params=_params(("arbitrary", "arbitrary")), name="attn_fwd",
    )(qkv, qkv, qkv, bias, *shards)
    return res[0], res[1], list(res[2:])


def _attn_bwd(qkv, dmix, o, lse, bias_t, dproj, batch, parts, smalls):
    m = qkv.shape[1]
    nt, ns = len(parts), len(smalls)
    n_steps = N_PAIR * batch

    def body(q_ref, k_ref, v_ref, do_ref, o_ref, l_ref, b_ref, *rest):
        part_refs = rest[1:nt + 1]
        small_refs = rest[nt + 1:nt + 1 + ns]
        pos = nt + 1 + ns
        dproj_ref, ds_ref = rest[pos:pos + 2]
        recv_refs = rest[pos + 2:pos + 2 + nt]
        sum_refs = rest[pos + 2 + nt:pos + 2 + nt + ns]
        pos += 2 + nt + ns
        dq_acc, dk_acc, dv_acc, d_scr, stage, stage_sems, send_sems, recv_sems = rest[pos:pos + 8]
        allreduce = _SmallAllReducePlan(small_refs, sum_refs, rest[pos + 8:pos + 8 + ns],
                                        rest[pos + 8 + ns:pos + 8 + 2 * ns], *rest[pos + 8 + 2 * ns:])
        pair, seq = pl.program_id(0), pl.program_id(1)
        step = pair * batch + seq
        exchange = _ChipExchangePlan(part_refs, recv_refs, send_sems, recv_sems)

        @pl.when(step == 0)
        def _():
            allreduce.start_sibling()

        @pl.when(step == 3)
        def _():
            allreduce.sum_sibling_and_start_chips()

        def stage_copies():
            rows = pl.ds(pl.multiple_of(seq * SEQ, SEQ), SEQ)
            return [pltpu.make_async_copy(
                stage.at[k],
                dproj_ref.at[rows, pl.ds(pl.multiple_of(2 * A_WIDTH + k * B_WIDTH + pair * PAIR, PAIR), PAIR)],
                stage_sems.at[k]) for k in range(3)]

        @pl.when(step == 0)
        def _():
            exchange.start()

        @pl.when(pl.program_id(1) == 0)
        def _():
            ds_ref[...] = jnp.zeros_like(ds_ref)

        dq_acc[...] = jnp.zeros_like(dq_acc)
        dk_acc[...] = jnp.zeros_like(dk_acc)
        dv_acc[...] = jnp.zeros_like(dv_acc)
        d_scr[...] = do_ref[...] * o_ref[...].astype(F32)
        masks = _pair_masks()

        def stack_heads(t):
            return jnp.concatenate([jnp.where(masks[0], t, 0.0), jnp.where(masks[1], t, 0.0)], axis=0).astype(BF16)

        for ci, (_, dil) in enumerate(DILATED):
            nb = SEQ // dil // Q_BLOCK

            def block(trip, carry, ci=ci, dil=dil, nb=nb):
                first = []
                for u in range(BWD_BLOCKS_PER_TRIP):
                    rows, prow = _block_rows(trip * BWD_BLOCKS_PER_TRIP + u, dil)
                    has_prev = nb > 1 and prow is not None
                    if has_prev:
                        kcat = jnp.concatenate([k_ref[prow, :], k_ref[rows, :]], axis=0).astype(BF16)
                        vcat = jnp.concatenate([v_ref[prow, :], v_ref[rows, :]], axis=0).astype(BF16)
                    else:
                        kcat = k_ref[rows, :].astype(BF16)
                        vcat = v_ref[rows, :].astype(BF16)
                    qst = stack_heads(q_ref[rows, :] * 0.125)
                    dost = stack_heads(do_ref[rows, :])
                    lt = l_ref[rows, :].T
                    dt = d_scr[rows, :].T
                    lrow = jnp.concatenate([lt[0:1], lt[HEAD_DIM:HEAD_DIM + 1]], axis=1)
                    drow = jnp.concatenate([jnp.sum(dt[:HEAD_DIM], axis=0, keepdims=True),
                                            jnp.sum(dt[HEAD_DIM:], axis=0, keepdims=True)], axis=1)
                    first.append((has_prev, rows, prow, kcat, qst, dost, lrow, drow,
                                  _dot_nt(kcat, qst), _dot_nt(vcat, dost)))
                second = []
                for has_prev, rows, prow, kcat, qst, dost, lrow, drow, st, dpt in first:
                    keys = slice(0, 2 * Q_BLOCK) if has_prev else slice(Q_BLOCK, 2 * Q_BLOCK)
                    bt = jnp.concatenate([b_ref[ci, 0, keys, :], b_ref[ci, 1, keys, :]], axis=1)
                    pt = jnp.exp(st + bt - lrow)
                    dst = pt * (dpt - drow)
                    ds_ref[ci, 0, keys, :] += dst[:, :Q_BLOCK]
                    ds_ref[ci, 1, keys, :] += dst[:, Q_BLOCK:]
                    second.append((has_prev, rows, prow, kcat, qst, dost, pt.astype(BF16), dst.astype(BF16)))
                for has_prev, rows, prow, kcat, qst, dost, pt_bf, dst_bf in second:
                    dk = _dot(dst_bf, qst)
                    dv = _dot(pt_bf, dost)
                    dq2 = _dot_tn(dst_bf, kcat)
                    dq_acc[rows, :] += jnp.where(masks[0], dq2[:Q_BLOCK], dq2[Q_BLOCK:]) * 0.125
                    if has_prev:
                        dk_acc[prow, :] += dk[:Q_BLOCK]
                        dv_acc[prow, :] += dv[:Q_BLOCK]
                        dk_acc[rows, :] += dk[Q_BLOCK:]
                        dv_acc[rows, :] += dv[Q_BLOCK:]
                    else:
                        dk_acc[rows, :] += dk
                        dv_acc[rows, :] += dv
                return carry

            block(0, 0)
            lax.fori_loop(1, BLOCKS_PER_CFG // BWD_BLOCKS_PER_TRIP, block, 0)

        @pl.when(step > 0)
        def _():
            for cp in stage_copies():
                cp.wait()

        stage[0] = dq_acc[...].astype(BF16)
        stage[1] = dk_acc[...].astype(BF16)
        stage[2] = dv_acc[...].astype(BF16)
        for cp in stage_copies():
            cp.start()

        @pl.when(step == n_steps - 1)
        def _():
            for cp in stage_copies():
                cp.wait()
            exchange.finish()
            allreduce.finish()

    def slab(first):
        return pl.BlockSpec((None, SEQ, PAIR), lambda p, b: (first + p, b, 0))

    nat = pl.BlockSpec((SEQ, PAIR), lambda p, b: (b, p))
    tbl = pl.BlockSpec((N_CFG, 2, 2 * Q_BLOCK, Q_BLOCK), lambda p, b: (0, p, 0, 0))
    acc = pltpu.VMEM((SEQ, PAIR), F32)
    vm = pl.BlockSpec(memory_space=pltpu.VMEM)
    res = pl.pallas_call(
        body, grid=(N_PAIR, batch),
        in_specs=[slab(0), slab(N_PAIR), slab(2 * N_PAIR),
                  pl.BlockSpec((SEQ, PAIR), lambda p, b: (b, A_WIDTH // PAIR + p)), nat, nat, tbl]
        + [ANY] * (nt + 1) + [vm] * ns,
        out_specs=[ANY, tbl] + [ANY] * nt + [vm] * ns,
        out_shape=[jax.ShapeDtypeStruct(dproj.shape, dproj.dtype),
                   jax.ShapeDtypeStruct((N_CFG, B_HEADS, 2 * Q_BLOCK, Q_BLOCK), F32)]
        + [jax.ShapeDtypeStruct((3,) + p.shape[1:], p.dtype) for p in parts]
        + [jax.ShapeDtypeStruct(a.shape, F32) for a in smalls],
        input_output_aliases={7: 0},
        scratch_shapes=[acc, acc, acc, acc, pltpu.VMEM((3, SEQ, PAIR), BF16), pltpu.SemaphoreType.DMA((3,))]
        + _sem_pair(3 * nt) + _SmallAllReducePlan.scratch(smalls),
        compiler_params=_params(("arbitrary", "arbitrary")), name="attn_bwd",
    )(qkv, qkv, qkv, dmix, o, lse, bias_t, dproj, *parts, *smalls)
    return res[0], res[1], list(res[2:2 + nt]), list(res[2 + nt:])


def _rel_bias_grad(ds, buckets_np, grads):
    present = _present_buckets(buckets_np)
    nx = len(grads)
    shapes = [g.shape for g in grads]

    def body(bk_ref, ds_ref, *rest):
        o_ref = rest[nx]
        acc_ref = rest[2 * nx + 1]
        exchange = _SiblingExchangePlan(shapes, rest[:nx], rest[nx + 1:2 * nx + 1], *rest[2 * nx + 2:])
        exchange.start()
        acc_ref[...] = jnp.zeros_like(acc_ref)
        for c in range(N_CFG):
            bk = bk_ref[c]
            for h in range(B_HEADS):
                dsv = ds_ref[c, h]
                for b in present[c]:
                    part = jnp.sum(jnp.where(bk == b, dsv, 0.0), axis=0, keepdims=True)
                    acc_ref[pl.ds(h * NUM_BUCKETS + b, 1), :] += part
        o_ref[...] = jnp.sum(acc_ref[...], axis=1, keepdims=True)
        exchange.finish()

    vm = pl.BlockSpec(memory_space=pltpu.VMEM)
    res = pl.pallas_call(
        body, in_specs=[vm, vm] + [ANY] * nx, out_specs=[vm] + [ANY] * nx,
        out_shape=[jax.ShapeDtypeStruct((B_HEADS * NUM_BUCKETS, 1), F32)]
        + [jax.ShapeDtypeStruct((N_SHARD, s[1] // 2, s[2]), F32) for s in shapes],
        scratch_shapes=[pltpu.VMEM((B_HEADS * NUM_BUCKETS, buckets_np.shape[-1]), F32)] + _sem_pair(nx),
        compiler_params=_params(), name="rel_bias_grad",
    )(jnp.asarray(buckets_np), ds, *grads)
    return res[0], list(res[1:])


def _row_index():
    return lax.broadcasted_iota(jnp.int32, (SEQ, LANE_BLOCK), 0)


def _shift_down(x, k, row):
    return jnp.where(row >= k, pltpu.roll(x, k, 0), 0.0)


def _shift_up(x, k, row):
    return jnp.where(row < SEQ - k, pltpu.roll(x, SEQ - k, 0), 0.0)


def _convgate_fwd(gate, up, conv_w, conv_b, batch):
    m = gate.shape[0]

    def body(g_ref, u_ref, w_ref, b_ref, a_ref):
        g = g_ref[...].astype(F32)
        w = w_ref[...]
        row = _row_index()
        c = b_ref[...] + w[0:1] * _shift_down(g, 2, row) + w[1:2] * _shift_down(g, 1, row) + w[2:3] * g
        a_ref[...] = (_gelu(c) * u_ref[...].astype(F32)).astype(BF16)

    blk = pl.BlockSpec((SEQ, LANE_BLOCK), lambda b, j: (b, j))
    return pl.pallas_call(
        body, grid=(batch, D_FF // LANE_BLOCK),
        in_specs=[blk, blk, pl.BlockSpec((3, LANE_BLOCK), lambda b, j: (0, j)),
                  pl.BlockSpec((1, LANE_BLOCK), lambda b, j: (0, j))],
        out_specs=blk,
        out_shape=jax.ShapeDtypeStruct((m, D_FF), BF16),
        compiler_params=_params(("parallel", "parallel")), name="convgate_fwd",
    )(gate, up, conv_w, conv_b)


def _convgate_bwd(gate, up, dact, conv_w, conv_b, batch):
    m = gate.shape[0]

    def body(g_ref, u_ref, da_ref, w_ref, b_ref, dg_ref, du_ref, dw_ref, db_ref):
        @pl.when(pl.program_id(1) == 0)
        def _():
            dw_ref[...] = jnp.zeros_like(dw_ref)
            db_ref[...] = jnp.zeros_like(db_ref)

        g = g_ref[...].astype(F32)
        w = w_ref[...]
        row = _row_index()
        g1 = _shift_down(g, 1, row)
        g2 = _shift_down(g, 2, row)
        c = b_ref[...] + w[0:1] * g2 + w[1:2] * g1 + w[2:3] * g
        gg, dgg = _gelu_and_grad(c)
        da = da_ref[...].astype(F32)
        du_ref[...] = (da * gg).astype(BF16)
        dc = da * u_ref[...].astype(F32) * dgg
        db_ref[...] += jnp.sum(dc, axis=0, keepdims=True)
        dw_ref[0:1, :] += jnp.sum(dc * g2, axis=0, keepdims=True)
        dw_ref[1:2, :] += jnp.sum(dc * g1, axis=0, keepdims=True)
        dw_ref[2:3, :] += jnp.sum(dc * g, axis=0, keepdims=True)
        dg_ref[...] = (w[2:3] * dc + w[1:2] * _shift_up(dc, 1, row) + w[0:1] * _shift_up(dc, 2, row)).astype(BF16)

    blk = pl.BlockSpec((SEQ, LANE_BLOCK), lambda j, b: (b, j))
    wspec = pl.BlockSpec((3, LANE_BLOCK), lambda j, b: (0, j))
    bspec = pl.BlockSpec((1, LANE_BLOCK), lambda j, b: (0, j))
    return pl.pallas_call(
        body, grid=(D_FF // LANE_BLOCK, batch),
        in_specs=[blk, blk, blk, wspec, bspec],
        out_specs=[blk, blk, wspec, bspec],
        out_shape=[jax.ShapeDtypeStruct((m, D_FF), BF16), jax.ShapeDtypeStruct((m, D_FF), BF16),
                   jax.ShapeDtypeStruct((3, D_FF), F32), jax.ShapeDtypeStruct((1, D_FF), F32)],
        compiler_params=_params(("parallel", "arbitrary")), name="convgate_bwd",
    )(gate, up, dact, conv_w, conv_b)


def _gather_weights(shards, conv_w_shard, rel_bias, buckets_np):
    nt = len(shards)
    shapes = [sh.shape for sh in shards]
    ts = list(range(nt))
    tables = _bias_tables_body(buckets_np)

    def body(*refs):
        shard_refs = refs[:nt]
        cw_ref, rb_ref, bk_ref = refs[nt:nt + 3]
        out_refs = refs[nt + 3:2 * nt + 3]
        cw_out, bias_ref, bias_t_ref = refs[2 * nt + 3:2 * nt + 6]
        send_sems, recv_sems, cw_send, cw_recv = refs[2 * nt + 6:]
        plan = _GatherPlan(shapes, shard_refs, out_refs, send_sems, recv_sems)
        x, y, c, chips = _mesh_pos()

        def cw_copy(j, src, dst, chip):
            return pltpu.make_async_remote_copy(src_ref=src, dst_ref=dst, send_sem=cw_send.at[j],
                                                recv_sem=cw_recv.at[j], device_id=(*chip, c), device_id_type=MESH)

        plan.start(ts)
        cw_sends = [cw_copy(j, cw_ref, cw_out.at[2 * x + y], chip) for j, chip in enumerate(chips)]
        for cp in cw_sends:
            cp.start()
        tables(rb_ref, bk_ref, bias_ref, bias_t_ref)
        plan.forward(ts)
        for j, chip in enumerate(chips):
            dst = cw_out.at[2 * chip[0] + chip[1]]
            cw_copy(j, dst, dst, chip).wait_recv()
        plan.finish(ts)
        for cp in cw_sends:
            cp.wait_send()

    out_shape = [jax.ShapeDtypeStruct((N_SHARD,) + sh.shape, sh.dtype) for sh in shards]
    out_shape.append(jax.ShapeDtypeStruct((N_SHARD,) + conv_w_shard.shape, conv_w_shard.dtype))
    out_shape += [jax.ShapeDtypeStruct((N_CFG, B_HEADS, Q_BLOCK, 2 * Q_BLOCK), F32),
                  jax.ShapeDtypeStruct((N_CFG, B_HEADS, 2 * Q_BLOCK, Q_BLOCK), F32)]
    vm = pl.BlockSpec(memory_space=pltpu.VMEM)
    res = pl.pallas_call(
        body, in_specs=[ANY] * (nt + 1) + [pl.BlockSpec(memory_space=pltpu.SMEM), vm],
        out_specs=[ANY] * (nt + 1) + [vm, vm], out_shape=out_shape,
        scratch_shapes=_sem_pair(6 * nt) + _sem_pair(3),
        compiler_params=pltpu.CompilerParams(has_side_effects=True, vmem_limit_bytes=VMEM_LIMIT),
        name="gather_weights",
    )(*shards, conv_w_shard, rel_bias, jnp.asarray(buckets_np))
    return list(res[:nt + 1]), res[nt + 1], res[nt + 2]


def _add_halves(g, recv, c_idx):
    _, rows2, cols = g.shape
    rows = rows2 // 2
    tr = rows
    nblk = rows // tr

    def body(c_ref, g_ref, r_ref, o_ref):
        o_ref[...] = (g_ref[...] + r_ref[...]).astype(BF16)

    return pl.pallas_call(
        body,
        grid_spec=pltpu.PrefetchScalarGridSpec(
            num_scalar_prefetch=1, grid=(N_SHARD, nblk),
            in_specs=[pl.BlockSpec((None, tr, cols), lambda s, i, c: (s, c[0] * nblk + i, 0)),
                      pl.BlockSpec((None, tr, cols), lambda s, i, c: (s, i, 0))],
            out_specs=pl.BlockSpec((None, tr, cols), lambda s, i, c: (s, i, 0))),
        out_shape=jax.ShapeDtypeStruct((N_SHARD, rows, cols), BF16),
        compiler_params=_params(("parallel", "parallel")), name="rs_add_halves",
    )(c_idx, g, recv)


def _add_chips(part, recv, s_idx, c_idx):
    _, rows, cols = part.shape
    tr = rows
    nblk = rows // tr

    def body(idx_ref, p_ref, r_ref, o_ref):
        acc = p_ref[...].astype(F32)
        for j in range(3):
            acc = acc + r_ref[j].astype(F32)
        o_ref[...] = acc

    return pl.pallas_call(
        body,
        grid_spec=pltpu.PrefetchScalarGridSpec(
            num_scalar_prefetch=1, grid=(nblk,),
            in_specs=[pl.BlockSpec((None, tr, cols), lambda i, idx: (idx[0], i, 0)),
                      pl.BlockSpec((3, tr, cols), lambda i, idx: (0, i, 0))],
            out_specs=pl.BlockSpec((tr, cols), lambda i, idx: (idx[1] * nblk + i, 0))),
        out_shape=jax.ShapeDtypeStruct((2 * rows, cols), F32),
        compiler_params=_params(("parallel",)), name="rs_add_chips",
    )(jnp.concatenate([s_idx, c_idx]), part, recv)


def _finish_reductions(fulls, arrays):
    nt, n = len(fulls), len(arrays)

    def body(*refs):
        in_refs = refs[nt:nt + n]
        full_refs, out_refs = refs[nt + n:2 * nt + n], refs[2 * nt + n:2 * nt + 2 * n]
        pos = 2 * nt + 2 * n
        share_send, share_recv = refs[pos + 2 * n:pos + 2 * n + 2]
        allreduce = _SmallAllReducePlan(in_refs, out_refs, refs[pos:pos + n], refs[pos + n:pos + 2 * n],
                                        *refs[pos + 2 * n + 2:])
        x, y, c, _ = _mesh_pos()

        def half(t, which):
            rows = fulls[t].shape[0] // 2
            return full_refs[t].at[pl.ds(which * rows, rows), :]

        def share(t, which):
            return pltpu.make_async_remote_copy(
                src_ref=half(t, which), dst_ref=half(t, which), send_sem=share_send.at[t],
                recv_sem=share_recv.at[t], device_id=(x, y, 1 - c), device_id_type=MESH)

        for t in range(nt):
            share(t, c).start()
        allreduce.start_sibling()
        allreduce.sum_sibling_and_start_chips()
        allreduce.finish()
        for t in range(nt):
            share(t, 1 - c).wait_recv()
        for t in range(nt):
            share(t, c).wait_send()

    vm = pl.BlockSpec(memory_space=pltpu.VMEM)
    res = pl.pallas_call(
        body, in_specs=[ANY] * nt + [vm] * n, out_specs=[ANY] * nt + [vm] * n,
        out_shape=[jax.ShapeDtypeStruct(f.shape, f.dtype) for f in fulls]
        + [jax.ShapeDtypeStruct(a.shape, F32) for a in arrays],
        input_output_aliases={t: t for t in range(nt)},
        scratch_shapes=[pltpu.VMEM(a.shape, F32) for a in arrays] + [pltpu.VMEM((3,) + a.shape, F32) for a in arrays]
        + _sem_pair(nt) + _sem_pair(4 * n),
        compiler_params=pltpu.CompilerParams(has_side_effects=True),
        name="finish_reductions",
    )(*fulls, *arrays)
    return list(res[:nt]), list(res[nt:])


def _from_col_shards(g):
    n, rows, cols = g.shape
    return g.transpose(1, 0, 2).reshape(rows, n * cols)


def _train_step(x, tgt, g1, g2, g3, g4, shards, ln_g, ln_b, w_s, b_s, rel_bias, conv_w_shard, conv_b, batch,
                s_idx, c_idx):
    big = dict(tm=1024, out_dtype=F32)
    buckets = _bucket_tables()
    bz = jnp.repeat(b_s.T, HEAD_DIM, axis=1)
    w_st = jnp.swapaxes(w_s, 1, 2)

    def with_own(gathered, own):
        return lax.dynamic_update_index_in_dim(gathered, own, s_idx[0], 0)

    def shard_major(g):
        return g.reshape(N_SHARD, g.shape[0] // N_SHARD, D_MODEL)

    (g_in, g_convw), bias, bias_t = _gather_weights([shards["w_in"]], conv_w_shard, rel_bias, buckets)
    w_in_t = with_own(g_in, shards["w_in"]).reshape(IN_COLS, D_MODEL)
    conv_w = _from_col_shards(with_own(g_convw, conv_w_shard))

    h1, uv, qkv = _proj_fwd(x, g1, w_in_t)
    a = _gate_fwd(uv, ln_g, ln_b, w_s, bz)
    later = ["w_out", "w_gate", "w_up"]
    o_bf, lse, gathered = _attn_fwd(qkv, bias, batch, [shards[n] for n in later])
    g_out, g_gate, g_up = [with_own(g, shards[n]) for g, n in zip(gathered, later)]
    w_out = g_out.reshape(D_MODEL, D_MODEL)
    w_gate_t = g_gate.reshape(D_FF, D_MODEL)
    w_up_t = g_up.reshape(D_FF, D_MODEL)
    (y1, x1, h2), _ = _fused_rows(
        "out_proj_mid_fwd", 512,
        [(a, w_out, "nn", slice(0, A_WIDTH)), (o_bf, w_out, "nn", slice(A_WIDTH, D_MODEL))],
        [x], [g2, g3], _mid_fwd_rows, [F32, F32, BF16], [])
    gate, up, g_down = _mm_pair_nt(h2, w_gate_t, w_up_t, shards["w_down"], tm=1024, tn=1408, out_dtype=BF16,
                                   name="mm_gate_up")
    w_down = with_own(g_down, shards["w_down"]).reshape(D_FF, D_MODEL)
    act = _convgate_fwd(gate, up, conv_w, conv_b, batch)
    (dx2, dy2, dg4, loss), _ = _fused_rows(
        "down_proj_loss_head", 512, [(act, w_down, "nn", None)], [x1, tgt], [g4], _loss_head_rows,
        [F32, BF16], [(1, D_MODEL), (1, 128)])

    dact = _mm(dy2, w_down, dims="nt", tm=1024, tn=1408, tk=1024, out_dtype=BF16, name="mm_dact")
    dw_down = _mm(act, dy2, dims="tn", tm=1408, tn=1024, tk=1024, out_dtype=F32, name="mm_dw_down")
    dgate, dup, dconv_w, dconv_b = _convgate_bwd(gate, up, dact, conv_w, conv_b, batch)
    (dx1, dy1, dg2, dg3), _ = _fused_rows(
        "dh2_mid_bwd", 256, [(dgate, w_gate_t, "nn", None), (dup, w_up_t, "nn", None)],
        [x1, y1, dx2], [g2, g3], _mid_bwd_rows, [F32, BF16], [(1, D_MODEL), (1, D_MODEL)])
    dw_gate_t = _mm(dgate, h2, dims="tn", tm=1408, tn=1024, tk=1024, out_dtype=F32, name="mm_dw_gate")
    dw_up_t = _mm(dup, h2, dims="tn", tm=1408, tn=1024, tk=1024, out_dtype=F32, name="mm_dw_up")
    dmix = _mm(dy1, w_out, dims="nt", tn=1024, tk=1024, name="mm_dmix", **big)
    dw_out_a = _mm(a, dy1, dims="tn", tm=A_WIDTH, tn=1024, tk=1024, out_dtype=F32, name="mm_dw_out_a")
    dw_out_b = _mm(o_bf, dy1, dims="tn", tm=B_WIDTH, tn=1024, tk=1024, out_dtype=F32, name="mm_dw_out_b")

    dw_out = jnp.concatenate([dw_out_a, dw_out_b], axis=0)
    done = [shard_major(g) for g in (dw_down, dw_gate_t, dw_up_t, dw_out)]
    (dproj, dln_g, dln_b, dw_s, dbz), recv_a = _gate_bwd(uv, dmix, ln_g, ln_b, w_s, w_st, bz, done)
    parts = [_add_halves(g, r, c_idx) for g, r in zip(done, recv_a)]
    early = dict(loss=loss, norm_mix_post=dg2, norm_ffn_pre=dg3, norm_ffn_post=dg4, ln_v_gain=dln_g,
                 ln_v_bias=dln_b, spatial_w=dw_s, spatial_b=dbz, conv_w=dconv_w, conv_b=dconv_b)
    dproj, ds, recv, early_sums = _attn_bwd(qkv, dmix, o_bf, lse, bias_t, dproj, batch, parts, list(early.values()))
    fulls = [_add_chips(p, r, s_idx, c_idx) for p, r in zip(parts, recv)]
    dw_in_t = _mm(dproj, h1, dims="tn", tm=1408, tn=1024, tk=1024, out_dtype=F32, name="mm_dw_in")
    last = [shard_major(dw_in_t)]
    drel, recv_in_a = _rel_bias_grad(ds, np.ascontiguousarray(np.swapaxes(buckets, 1, 2)), last)
    part_in = [_add_halves(g, r, c_idx) for g, r in zip(last, recv_in_a)]
    (dx0, dg1), recv_in = _fused_rows(
        "dh1_in_bwd", 512, [(dproj, w_in_t, "nn", None)], [x, dx1], [g1], _in_bwd_rows,
        [F32], [(1, D_MODEL)], exchange=part_in)
    fulls += [_add_chips(p, r, s_idx, c_idx) for p, r in zip(part_in, recv_in)]
    half_reduced = dict(zip(["w_down", "w_gate", "w_up", "w_out", "w_in"], fulls))

    return dx0, dict(zip(early, early_sums)), dict(norm_mix_pre=dg1, rel_bias=drel), half_reduced


def _adamw_update(w, g, m, v):
    nm = ADAM_B1 * m + (1.0 - ADAM_B1) * g
    nv = ADAM_B2 * v + (1.0 - ADAM_B2) * (g * g)
    m_hat = nm / (1.0 - ADAM_B1 ** ADAM_STEP)
    v_hat = nv / (1.0 - ADAM_B2 ** ADAM_STEP)
    return -ADAM_LR * (m_hat / (jnp.sqrt(v_hat) + ADAM_EPS) + ADAM_WD * w), nm, nv


def _adamw(w, g, m, v, name):
    rows, cols = w.shape
    tr = next(cand for cand in (352, 256, 128) if rows % cand == 0)

    def body(w_ref, g_ref, m_ref, v_ref, go_ref, d_ref, nm_ref, nv_ref):
        gv = g_ref[...]
        go_ref[...] = gv
        d_ref[...], nm_ref[...], nv_ref[...] = _adamw_update(w_ref[...], gv, m_ref[...], v_ref[...])

    spec = pl.BlockSpec((tr, cols), lambda i: (i, 0))
    sds = jax.ShapeDtypeStruct((rows, cols), F32)
    return pl.pallas_call(
        body, grid=(rows // tr,), in_specs=[spec] * 4, out_specs=[spec] * 4, out_shape=[sds] * 4,
        compiler_params=_params(("parallel",)), name=name,
    )(w, g, m, v)


def _adamw_small(ws, gs, ms, vs):
    n = len(ws)

    def body(*refs):
        w_refs, g_refs, m_refs, v_refs = refs[:n], refs[n:2 * n], refs[2 * n:3 * n], refs[3 * n:4 * n]
        d_refs, nm_refs, nv_refs = refs[4 * n:5 * n], refs[5 * n:6 * n], refs[6 * n:7 * n]
        for t in range(n):
            d_refs[t][...], nm_refs[t][...], nv_refs[t][...] = _adamw_update(
                w_refs[t][...], g_refs[t][...], m_refs[t][...], v_refs[t][...])

    vm = pl.BlockSpec(memory_space=pltpu.VMEM)
    sds = [jax.ShapeDtypeStruct(w.shape, F32) for w in ws]
    res = pl.pallas_call(
        body, in_specs=[vm] * (4 * n), out_specs=[vm] * (3 * n), out_shape=sds * 3,
        compiler_params=_params(), name="adamw_small",
    )(*ws, *gs, *ms, *vs)
    return res[:n], res[n:2 * n], res[2 * n:]


SMALL = ["norm_mix_pre", "norm_mix_post", "norm_ffn_pre", "norm_ffn_post", "ln_v_gain", "ln_v_bias",
         "spatial_w", "spatial_b", "rel_bias", "conv_b"]
LARGE = ["w_in", "w_gate", "w_up", "w_down", "w_out"]
TRANSPOSED = ("w_in", "w_gate", "w_up")
ORDER = ["norm_mix_pre", "norm_mix_post", "norm_ffn_pre", "norm_ffn_post", "w_in", "ln_v_gain", "ln_v_bias",
         "spatial_w", "spatial_b", "rel_bias", "w_out", "w_gate", "w_up", "conv_w", "conv_b", "w_down"]


def kernel(x, norm_mix_pre, norm_mix_post, norm_ffn_pre, norm_ffn_post, w_in, ln_v_gain, ln_v_bias, spatial_w, spatial_b, rel_bias, w_out, w_gate, w_up, conv_w, conv_b, w_down, loss_target, m_norm_mix_pre, m_norm_mix_post, m_norm_ffn_pre, m_norm_ffn_post, m_w_in, m_ln_v_gain, m_ln_v_bias, m_spatial_w, m_spatial_b, m_rel_bias, m_w_out, m_w_gate, m_w_up, m_conv_w, m_conv_b, m_w_down, v_norm_mix_pre, v_norm_mix_post, v_norm_ffn_pre, v_norm_ffn_post, v_w_in, v_ln_v_gain, v_ln_v_bias, v_spatial_w, v_spatial_b, v_rel_bias, v_w_out, v_w_gate, v_w_up, v_conv_w, v_conv_b, v_w_down):
    params = dict(norm_mix_pre=norm_mix_pre, norm_mix_post=norm_mix_post, norm_ffn_pre=norm_ffn_pre,
                  norm_ffn_post=norm_ffn_post, w_in=w_in, ln_v_gain=ln_v_gain, ln_v_bias=ln_v_bias,
                  spatial_w=spatial_w, spatial_b=spatial_b, rel_bias=rel_bias, w_out=w_out, w_gate=w_gate,
                  w_up=w_up, conv_w=conv_w, conv_b=conv_b, w_down=w_down)
    mom = dict(norm_mix_pre=m_norm_mix_pre, norm_mix_post=m_norm_mix_post, norm_ffn_pre=m_norm_ffn_pre,
               norm_ffn_post=m_norm_ffn_post, w_in=m_w_in, ln_v_gain=m_ln_v_gain, ln_v_bias=m_ln_v_bias,
               spatial_w=m_spatial_w, spatial_b=m_spatial_b, rel_bias=m_rel_bias, w_out=m_w_out, w_gate=m_w_gate,
               w_up=m_w_up, conv_w=m_conv_w, conv_b=m_conv_b, w_down=m_w_down)
    var = dict(norm_mix_pre=v_norm_mix_pre, norm_mix_post=v_norm_mix_post, norm_ffn_pre=v_norm_ffn_pre,
               norm_ffn_post=v_norm_ffn_post, w_in=v_w_in, ln_v_gain=v_ln_v_gain, ln_v_bias=v_ln_v_bias,
               spatial_w=v_spatial_w, spatial_b=v_spatial_b, rel_bias=v_rel_bias, w_out=v_w_out, w_gate=v_w_gate,
               w_up=v_w_up, conv_w=v_conv_w, conv_b=v_conv_b, w_down=v_w_down)

    batch = x.shape[0]
    xi, yi, ci = lax.axis_index("x"), lax.axis_index("y"), lax.axis_index("c")
    s_idx = (2 * xi + yi).astype(jnp.int32).reshape(1)
    c_idx = ci.astype(jnp.int32).reshape(1)

    def local(a, n):
        return jnp.swapaxes(a[0], 0, 1) if n in TRANSPOSED else a[0]

    shards = {n: local(params[n], n).astype(BF16) for n in LARGE}
    dx0, total, partial, half_reduced = _train_step(
        x.reshape(batch * SEQ, D_MODEL), loss_target.reshape(batch * SEQ, D_MODEL),
        norm_mix_pre, norm_mix_post, norm_ffn_pre, norm_ffn_post, shards,
        ln_v_gain.reshape(1, A_WIDTH), ln_v_bias.reshape(1, A_WIDTH), spatial_w[0], spatial_b[0], rel_bias,
        conv_w[0], conv_b, batch, s_idx, c_idx)
    grad_x = dx0.reshape(batch, SEQ, D_MODEL)

    names = list(partial)
    fulls, sums = _finish_reductions([half_reduced[n] for n in LARGE], [partial[n] for n in names])
    reduced = dict(zip(LARGE, fulls))
    total.update(zip(names, sums))
    loss = total["loss"][0, 0]
    total["spatial_b"] = total["spatial_b"][:, ::HEAD_DIM].T
    total["rel_bias"] = total["rel_bias"].reshape(B_HEADS, NUM_BUCKETS).T
    total["conv_w"] = lax.dynamic_slice_in_dim(total["conv_w"], s_idx[0] * SHARD_FF, SHARD_FF, axis=1)
    small_names = SMALL + ["conv_w"]
    for n in small_names:
        reduced[n] = total[n].reshape(params[n].shape)

    out_g, out_d, out_m, out_v = {}, {}, {}, {}
    for n in LARGE:
        res = _adamw(local(params[n], n), reduced[n], local(mom[n], n), local(var[n], n), name=f"adamw_{n}")
        if n in TRANSPOSED:
            res = [jnp.swapaxes(r, 0, 1) for r in res]
        out_g[n], out_d[n], out_m[n], out_v[n] = [r[None] for r in res]
    d, nm, nv = _adamw_small([params[n] for n in small_names], [reduced[n] for n in small_names],
                             [mom[n] for n in small_names], [var[n] for n in small_names])
    for n, dd, mm, vv in zip(small_names, d, nm, nv):
        out_g[n], out_d[n], out_m[n], out_v[n] = reduced[n], dd, mm, vv

    return (loss, grad_x, *[out_g[n] for n in ORDER], *[out_d[n] for n in ORDER],
            *[out_m[n] for n in ORDER], *[out_v[n] for n in ORDER])
```

```python
import functools
import math

import numpy as np
import jax
import jax.numpy as jnp
from jax import lax
from jax.experimental import pallas as pl
from jax.experimental.pallas import tpu as pltpu

F32 = jnp.float32
BF16 = jnp.bfloat16
MESH = pl.DeviceIdType.MESH

D_MODEL = 1024
SEQ = 2048
HEAD_DIM = 64
A_GROUPS = 4
A_WIDTH = 256
B_HEADS = 12
B_WIDTH = 768
CHUNK = 128
DILATED = ((128, 1), (512, 4), (2048, 16))
NUM_BUCKETS = 32
MAX_DISTANCE = 2048
D_FF = 2816
IN_COLS = 2816
NORM_EPS = 1e-6
NEG_INF = -1e30
N_SHARD = 4
SHARD_FF = D_FF // N_SHARD
LANE_BLOCK = 256
VMEM_LIMIT = 56 * 1024 * 1024

ADAM_LR = 0.001
ADAM_B1 = 0.9
ADAM_B2 = 0.999
ADAM_EPS = 1e-08
ADAM_WD = 0.01
ADAM_STEP = 10

GELU_C = math.sqrt(2.0 / math.pi)
GELU_A = 0.044715

ANY = pl.BlockSpec(memory_space=pl.ANY)


def _params(sem=None):
    return pltpu.CompilerParams(dimension_semantics=sem, vmem_limit_bytes=VMEM_LIMIT)


def _dot(a, b, precision=None):
    return jnp.dot(a, b, preferred_element_type=F32, precision=precision)


def _dot_nt(a, b, precision=None):
    return lax.dot_general(a, b, (((1,), (1,)), ((), ())), preferred_element_type=F32, precision=precision)


def _dot_tn(a, b):
    return lax.dot_general(a, b, (((0,), (0,)), ((), ())), preferred_element_type=F32)


def _gelu(x):
    t = jnp.tanh(x * (GELU_C + (GELU_C * GELU_A) * (x * x)))
    return (0.5 * x) * (1.0 + t)


def _gelu_and_grad(x):
    x2 = x * x
    u = 1.0 + jnp.tanh(x * (GELU_C + (GELU_C * GELU_A) * x2))
    hx = 0.5 * x
    dg = u * (0.5 + hx * (2.0 - u) * (GELU_C + (3.0 * GELU_C * GELU_A) * x2))
    return hx * u, dg


def _mesh_pos():
    x, y, c = lax.axis_index("x"), lax.axis_index("y"), lax.axis_index("c")
    chips = [(1 - x, y), (x, 1 - y), (1 - x, 1 - y)]
    return x, y, c, chips


class _GatherPlan:
    def __init__(self, shapes, shard_refs, out_refs, send_sems, recv_sems):
        self.shapes, self.shard_refs, self.out_refs = shapes, shard_refs, out_refs
        self.send_sems, self.recv_sems = send_sems, recv_sems
        self.x, self.y, self.c, self.chips = _mesh_pos()
        self.sib = (self.x, self.y, 1 - self.c)

    def _half(self, t, chip, which):
        rows = self.shapes[t][0] // 2
        return self.out_refs[t].at[2 * chip[0] + chip[1], pl.ds(which * rows, rows), :]

    def _copy(self, k, src, dst, to):
        return pltpu.make_async_remote_copy(src_ref=src, dst_ref=dst, send_sem=self.send_sems.at[k],
                                            recv_sem=self.recv_sems.at[k], device_id=to, device_id_type=MESH)

    def _sends(self, t):
        rows = self.shapes[t][0] // 2
        src = self.shard_refs[t].at[pl.ds(self.c * rows, rows), :]
        return [self._copy(6 * t + j, src, self._half(t, (self.x, self.y), self.c), (*chip, self.c))
                for j, chip in enumerate(self.chips)]

    def _forwards(self, t):
        return [self._copy(6 * t + 3 + j, self._half(t, chip, self.c), self._half(t, chip, self.c), self.sib)
                for j, chip in enumerate(self.chips)]

    def start(self, ts):
        for t in ts:
            for cp in self._sends(t):
                cp.start()

    def forward(self, ts):
        for t in ts:
            for j, chip in enumerate(self.chips):
                landed = self._half(t, chip, self.c)
                self._copy(6 * t + j, landed, landed, (*chip, self.c)).wait_recv()
            for cp in self._forwards(t):
                cp.start()

    def finish(self, ts):
        for t in ts:
            for j, chip in enumerate(self.chips):
                other = self._half(t, chip, 1 - self.c)
                self._copy(6 * t + 3 + j, other, other, self.sib).wait_recv()
        for t in ts:
            for cp in self._sends(t) + self._forwards(t):
                cp.wait_send()


class _SiblingExchangePlan:
    def __init__(self, shapes, grad_refs, out_refs, send_sems, recv_sems):
        self.shapes, self.grad_refs, self.out_refs = shapes, grad_refs, out_refs
        self.send_sems, self.recv_sems = send_sems, recv_sems
        self.x, self.y, self.c, _ = _mesh_pos()

    def _copies(self):
        out = []
        for t, (g, o) in enumerate(zip(self.grad_refs, self.out_refs)):
            rows = self.shapes[t][1] // 2
            out.append(pltpu.make_async_remote_copy(
                src_ref=g.at[:, pl.ds((1 - self.c) * rows, rows), :], dst_ref=o, send_sem=self.send_sems.at[t],
                recv_sem=self.recv_sems.at[t], device_id=(self.x, self.y, 1 - self.c), device_id_type=MESH))
        return out

    def start(self):
        for cp in self._copies():
            cp.start()

    def finish(self):
        for cp in self._copies():
            cp.wait()


class _ChipExchangePlan:
    def __init__(self, part_refs, out_refs, send_sems, recv_sems):
        self.part_refs, self.out_refs, self.send_sems, self.recv_sems = part_refs, out_refs, send_sems, recv_sems
        _, _, self.c, self.chips = _mesh_pos()

    def _copies(self):
        return [pltpu.make_async_remote_copy(
            src_ref=p.at[2 * chip[0] + chip[1]], dst_ref=o.at[j], send_sem=self.send_sems.at[3 * t + j],
            recv_sem=self.recv_sems.at[3 * t + j], device_id=(*chip, self.c), device_id_type=MESH)
            for t, (p, o) in enumerate(zip(self.part_refs, self.out_refs)) for j, chip in enumerate(self.chips)]

    def start(self):
        for cp in self._copies():
            cp.start()

    def finish(self):
        for cp in self._copies():
            cp.wait()


class _SmallAllReducePlan:
    def __init__(self, in_refs, out_refs, sib_refs, chip_refs, send_sems, recv_sems):
        self.in_refs, self.out_refs, self.sib_refs, self.chip_refs = in_refs, out_refs, sib_refs, chip_refs
        self.send_sems, self.recv_sems = send_sems, recv_sems
        self.n = len(in_refs)
        self.x, self.y, self.c, self.chips = _mesh_pos()

    def _copy(self, k, src, dst, to):
        return pltpu.make_async_remote_copy(src_ref=src, dst_ref=dst, send_sem=self.send_sems.at[k],
                                            recv_sem=self.recv_sems.at[k], device_id=to, device_id_type=MESH)

    def _first(self):
        return [self._copy(t, self.in_refs[t], self.sib_refs[t], (self.x, self.y, 1 - self.c)) for t in range(self.n)]

    def _second(self):
        return [self._copy(self.n + 3 * t + j, self.out_refs[t], self.chip_refs[t].at[j], (*chip, self.c))
                for t in range(self.n) for j, chip in enumerate(self.chips)]

    def start_sibling(self):
        for cp in self._first():
            cp.start()

    def sum_sibling_and_start_chips(self):
        for cp in self._first():
            cp.wait()
        for t in range(self.n):
            self.out_refs[t][...] = self.in_refs[t][...] + self.sib_refs[t][...]
        for cp in self._second():
            cp.start()

    def finish(self):
        for cp in self._second():
            cp.wait()
        for t in range(self.n):
            self.out_refs[t][...] = ((self.out_refs[t][...] + self.chip_refs[t][0])
                                     + (self.chip_refs[t][1] + self.chip_refs[t][2]))

    @staticmethod
    def scratch(arrays):
        return ([pltpu.VMEM(a.shape, F32) for a in arrays] + [pltpu.VMEM((3,) + a.shape, F32) for a in arrays]
                + _sem_pair(4 * len(arrays)))


def _sem_pair(n):
    return [pltpu.SemaphoreType.DMA((n,)), pltpu.SemaphoreType.DMA((n,))]


def _mm(a, b, *, dims, tm, tn, tk, out_dtype, name):
    if dims == "nn":
        m, k = a.shape
        n = b.shape[1]
        a_spec = pl.BlockSpec((tm, tk), lambda i, j, kk: (i, kk))
        b_spec = pl.BlockSpec((tk, tn), lambda i, j, kk: (kk, j))
        dot = _dot
    elif dims == "nt":
        m, k = a.shape
        n = b.shape[0]
        a_spec = pl.BlockSpec((tm, tk), lambda i, j, kk: (i, kk))
        b_spec = pl.BlockSpec((tn, tk), lambda i, j, kk: (j, kk))
        dot = _dot_nt
    else:
        k, m = a.shape
        n = b.shape[1]
        a_spec = pl.BlockSpec((tk, tm), lambda i, j, kk: (kk, i))
        b_spec = pl.BlockSpec((tk, tn), lambda i, j, kk: (kk, j))
        dot = _dot_tn
    assert m % tm == 0 and n % tn == 0 and k % tk == 0, (name, m, n, k)
    grid = (m // tm, n // tn, k // tk)
    nk = grid[2]
    assert nk == 1 or out_dtype == F32, name

    def body(a_ref, b_ref, o_ref):
        prod = dot(a_ref[...].astype(BF16), b_ref[...].astype(BF16))
        if nk == 1:
            o_ref[...] = prod.astype(out_dtype)
        else:
            kk = pl.program_id(2)

            @pl.when(kk == 0)
            def _():
                o_ref[...] = prod

            @pl.when(kk > 0)
            def _():
                o_ref[...] += prod

    return pl.pallas_call(
        body, grid=grid, in_specs=[a_spec, b_spec],
        out_specs=pl.BlockSpec((tm, tn), lambda i, j, kk: (i, j)),
        out_shape=jax.ShapeDtypeStruct((m, n), out_dtype),
        compiler_params=_params(("parallel", "parallel", "arbitrary")), name=name,
    )(a, b)


def _mm_pair_nt(a, w1_t, w2_t, shard, *, tm, tn, out_dtype, name):
    m, k = a.shape
    n = w1_t.shape[0]
    assert m % tm == 0 and n % tn == 0 and w2_t.shape == w1_t.shape, name
    grid = (m // tm, n // tn)
    n_steps = grid[0] * grid[1]

    def body(a_ref, w1_ref, w2_ref, shard_ref, o1_ref, o2_ref, gat_ref, send_sems, recv_sems):
        step = pl.program_id(0) * grid[1] + pl.program_id(1)
        gather = _GatherPlan([shard.shape], [shard_ref], [gat_ref], send_sems, recv_sems)

        @pl.when(step == 0)
        def _():
            gather.start([0])

        @pl.when(step == (2 * n_steps) // 3)
        def _():
            gather.forward([0])

        av = a_ref[...]
        o1_ref[...] = _dot_nt(av, w1_ref[...]).astype(out_dtype)
        o2_ref[...] = _dot_nt(av, w2_ref[...]).astype(out_dtype)

        @pl.when(step == n_steps - 1)
        def _():
            gather.finish([0])

    w_spec = pl.BlockSpec((tn, k), lambda i, j: (j, 0))
    o_spec = pl.BlockSpec((tm, tn), lambda i, j: (i, j))
    return pl.pallas_call(
        body, grid=grid,
        in_specs=[pl.BlockSpec((tm, k), lambda i, j: (i, 0)), w_spec, w_spec, ANY],
        out_specs=[o_spec, o_spec, ANY],
        out_shape=[jax.ShapeDtypeStruct((m, n), out_dtype)] * 2
        + [jax.ShapeDtypeStruct((N_SHARD,) + shard.shape, shard.dtype)],
        scratch_shapes=_sem_pair(6),
        compiler_params=_params(("arbitrary", "arbitrary")), name=name,
    )(a, w1_t, w2_t, shard)


def _fused_rows(name, tm, mats, rows, vecs, fn, row_outs, acc_outs, exchange=()):
    m = mats[0][0].shape[0]
    nm, nr, nv, nro, nao, nx = len(mats), len(rows), len(vecs), len(row_outs), len(acc_outs), len(exchange)
    n_steps = m // tm

    def body(*refs):
        a_refs, w_refs = refs[:nm], refs[nm:2 * nm]
        pos = 2 * nm
        row_refs, vec_refs, part_refs = refs[pos:pos + nr], refs[pos + nr:pos + nr + nv], refs[pos + nr + nv:pos + nr + nv + nx]
        pos += nr + nv + nx
        out_refs, acc_refs, recv_refs = refs[pos:pos + nro], refs[pos + nro:pos + nro + nao], refs[pos + nro + nao:pos + nro + nao + nx]
        sems = refs[pos + nro + nao + nx:]
        i = pl.program_id(0)
        if nx:
            plan = _ChipExchangePlan(part_refs, recv_refs, *sems)

            @pl.when(i == 0)
            def _():
                plan.start()

        @pl.when(i == 0)
        def _():
            for r in acc_refs:
                r[...] = jnp.zeros_like(r)

        y = None
        for a_ref, w_ref, (_, _, dims, sl) in zip(a_refs, w_refs, mats):
            w = w_ref[...] if sl is None else w_ref[sl, :]
            part = (_dot if dims == "nn" else _dot_nt)(a_ref[...], w)
            y = part if y is None else y + part
        res = fn(y, *[r[...] for r in row_refs], *[v[...] for v in vec_refs])
        for r, val in zip(out_refs, res[:nro]):
            r[...] = val.astype(r.dtype)
        for r, val in zip(acc_refs, res[nro:]):
            r[...] += val

        if nx:
            @pl.when(i == n_steps - 1)
            def _():
                plan.finish()

    tile = lambda width: pl.BlockSpec((tm, width), lambda i: (i, 0))
    res = pl.pallas_call(
        body, grid=(n_steps,),
        in_specs=[tile(a.shape[1]) for a, _, _, _ in mats] + [_full_spec(w.shape) for _, w, _, _ in mats]
        + [tile(D_MODEL)] * nr + [_full_spec((1, D_MODEL))] * nv + [ANY] * nx,
        out_specs=[tile(D_MODEL)] * nro + [_full_spec(s) for s in acc_outs] + [ANY] * nx,
        out_shape=[jax.ShapeDtypeStruct((m, D_MODEL), dt) for dt in row_outs]
        + [jax.ShapeDtypeStruct(s, F32) for s in acc_outs]
        + [jax.ShapeDtypeStruct((3,) + p.shape[1:], p.dtype) for p in exchange],
        scratch_shapes=_sem_pair(3 * nx) if nx else [],
        compiler_params=_params(("arbitrary",)), name=name,
    )(*[a for a, _, _, _ in mats], *[w for _, w, _, _ in mats], *rows, *vecs, *exchange)
    return list(res[:nro + nao]), list(res[nro + nao:])


ROW_TILE = 512


def _vec_spec(width=D_MODEL):
    return pl.BlockSpec((1, width), lambda i: (0, 0))


def _rstd(v):
    return lax.rsqrt(jnp.mean(v * v, axis=-1, keepdims=True) + NORM_EPS)


def _mid_fwd_rows(y1, x0, g2, g3):
    x1 = x0 + y1 * _rstd(y1) * g2
    return y1, x1, x1 * _rstd(x1) * g3


def _rms_bwd_rows(dout, v, g):
    r = _rstd(v)
    n = v * r
    dn = dout * g
    dv = r * (dn - n * jnp.mean(dn * n, axis=-1, keepdims=True))
    dg = jnp.sum(dout * n, axis=0, keepdims=True)
    return dv, dg


def _loss_head_rows(y2, x1, tgt, g4):
    x2 = x1 + y2 * _rstd(y2) * g4
    err = x2 - tgt
    loss = 0.5 * jnp.sum(jnp.mean(err * err, axis=-1, keepdims=True), axis=0, keepdims=True)
    dx2 = err * (1.0 / D_MODEL)
    dy2, dg4 = _rms_bwd_rows(dx2, y2, g4)
    return dx2, dy2, dg4, loss


def _mid_bwd_rows(dh2, x1, y1, dx2, g2, g3):
    d3, dg3 = _rms_bwd_rows(dh2, x1, g3)
    dx1 = dx2 + d3
    dy1, dg2 = _rms_bwd_rows(dx1, y1, g2)
    return dx1, dy1, dg2, dg3


def _in_bwd_rows(dh1, x0, dx1, g1):
    d1, dg1 = _rms_bwd_rows(dh1, x0, g1)
    return dx1 + d1, dg1


GATE_ROWS = 512


def _group_mean_matrix():
    p = np.zeros((A_WIDTH, A_WIDTH), np.float32)
    for g in range(A_GROUPS):
        p[g * HEAD_DIM:(g + 1) * HEAD_DIM, g * HEAD_DIM:(g + 1) * HEAD_DIM] = 1.0 / HEAD_DIM
    return jnp.asarray(p)


def _group_masks(width=A_WIDTH):
    lane = lax.broadcasted_iota(jnp.int32, (1, width), 1)
    return [(lane >= g * HEAD_DIM) & (lane < (g + 1) * HEAD_DIM) for g in range(width // HEAD_DIM)]


GROUP_SUM_PRECISION = lax.Precision.HIGH


def _layernorm_groups(vg, pavg):
    hi = GROUP_SUM_PRECISION
    mu = _dot(vg, pavg, hi)
    xc = vg - mu
    var = _dot(xc * xc, pavg, hi)
    rstd = lax.rsqrt(var + NORM_EPS)
    return xc * rstd, rstd


def _spatial_mix(w_bf, vn_chunk_bf, masks, bz):
    z = bz
    for g in range(A_GROUPS):
        z = z + jnp.where(masks[g], _dot(w_bf[g], vn_chunk_bf), 0.0)
    return z


def _full_spec(shape):
    return pl.BlockSpec(shape, lambda i: tuple(0 for _ in shape))


def _gate_fwd(uv, ln_g, ln_b, w_s, bz):
    m = uv.shape[0]
    pavg = _group_mean_matrix()

    def body(u_ref, v_ref, lg_ref, lb_ref, w_ref, bz_ref, p_ref, a_ref):
        masks = _group_masks()
        row = lax.broadcasted_iota(jnp.int32, (CHUNK, CHUNK), 0)
        col = lax.broadcasted_iota(jnp.int32, (CHUNK, CHUNK), 1)
        w_bf = [jnp.where(row >= col, w_ref[g], 0.0).astype(BF16) for g in range(A_GROUPS)]
        ug = _gelu(u_ref[...])
        vhat, _ = _layernorm_groups(_gelu(v_ref[...]), p_ref[...])
        vn = vhat * lg_ref[...] + lb_ref[...]
        bz = bz_ref[...]
        for c in range(GATE_ROWS // CHUNK):
            sl = slice(c * CHUNK, (c + 1) * CHUNK)
            z = _spatial_mix(w_bf, vn[sl].astype(BF16), masks, bz)
            a_ref[sl, :] = (ug[sl] * z).astype(BF16)

    return pl.pallas_call(
        body, grid=(m // GATE_ROWS,),
        in_specs=[pl.BlockSpec((GATE_ROWS, A_WIDTH), lambda i: (i, 0)),
                  pl.BlockSpec((GATE_ROWS, A_WIDTH), lambda i: (i, 1)),
                  _full_spec((1, A_WIDTH)), _full_spec((1, A_WIDTH)), _full_spec((A_GROUPS, CHUNK, CHUNK)),
                  _full_spec((CHUNK, A_WIDTH)), _full_spec((A_WIDTH, A_WIDTH))],
        out_specs=pl.BlockSpec((GATE_ROWS, A_WIDTH), lambda i: (i, 0)),
        out_shape=jax.ShapeDtypeStruct((m, A_WIDTH), BF16),
        compiler_params=_params(("parallel",)), name="gate_fwd",
    )(uv, uv, ln_g, ln_b, w_s, bz, pavg)


def _gate_bwd(uv, dmix, ln_g, ln_b, w_s, w_st, bz, grads):
    m = uv.shape[0]
    pavg = _group_mean_matrix()
    nsteps = m // GATE_ROWS
    nx = len(grads)
    shapes = [g.shape for g in grads]

    def body(u_ref, v_ref, da_ref, lg_ref, lb_ref, w_ref, wt_ref, bz_ref, p_ref, *rest):
        grad_refs = rest[:nx]
        duv_ref, dlg_ref, dlb_ref, dw_ref, dbz_ref = rest[nx:nx + 5]
        recv_refs = rest[nx + 5:2 * nx + 5]
        exchange = _SiblingExchangePlan(shapes, grad_refs, recv_refs, *rest[2 * nx + 5:])
        i = pl.program_id(0)

        @pl.when(i == 0)
        def _():
            exchange.start()
            dlg_ref[...] = jnp.zeros_like(dlg_ref)
            dlb_ref[...] = jnp.zeros_like(dlb_ref)
            dw_ref[...] = jnp.zeros_like(dw_ref)
            dbz_ref[...] = jnp.zeros_like(dbz_ref)

        hi = GROUP_SUM_PRECISION
        masks = _group_masks()
        row = lax.broadcasted_iota(jnp.int32, (CHUNK, CHUNK), 0)
        col = lax.broadcasted_iota(jnp.int32, (CHUNK, CHUNK), 1)
        tril = row >= col
        w_bf = [jnp.where(tril, w_ref[g], 0.0).astype(BF16) for g in range(A_GROUPS)]
        wt_bf = [jnp.where(col >= row, wt_ref[g], 0.0).astype(BF16) for g in range(A_GROUPS)]
        pavg_v = p_ref[...]
        lg = lg_ref[...]
        ug, dug = _gelu_and_grad(u_ref[...])
        vg, dvg_dx = _gelu_and_grad(v_ref[...])
        vhat, rstd = _layernorm_groups(vg, pavg_v)
        vn = vhat * lg + lb_ref[...]
        da = da_ref[...]
        bz = bz_ref[...]
        for c in range(GATE_ROWS // CHUNK):
            sl = slice(c * CHUNK, (c + 1) * CHUNK)
            vn_bf = vn[sl].astype(BF16)
            z = _spatial_mix(w_bf, vn_bf, masks, bz)
            dz = da[sl] * ug[sl]
            duv_ref[sl, 0:A_WIDTH] = (da[sl] * z * dug[sl]).astype(BF16)
            dbz_ref[...] += dz
            dz_bf = dz.astype(BF16)
            dvn = jnp.zeros((CHUNK, A_WIDTH), F32)
            for g in range(A_GROUPS):
                dz_g = jnp.where(masks[g], dz, 0.0).astype(BF16)
                dw_ref[g] += jnp.where(tril, _dot_nt(dz_g, vn_bf), 0.0)
                dvn = dvn + jnp.where(masks[g], _dot(wt_bf[g], dz_bf), 0.0)
            vh = vhat[sl]
            dlb_ref[...] += jnp.sum(dvn, axis=0, keepdims=True)
            dlg_ref[...] += jnp.sum(dvn * vh, axis=0, keepdims=True)
            dvh = dvn * lg
            m1 = _dot(dvh, pavg_v, hi)
            m2 = _dot(dvh * vh, pavg_v, hi)
            duv_ref[sl, A_WIDTH:2 * A_WIDTH] = (rstd[sl] * (dvh - m1 - vh * m2) * dvg_dx[sl]).astype(BF16)

        @pl.when(i == nsteps - 1)
        def _():
            dbz_ref[...] = _dot(dbz_ref[...], pavg_v * float(HEAD_DIM), hi)
            exchange.finish()

    res = pl.pallas_call(
        body, grid=(nsteps,),
        in_specs=[pl.BlockSpec((GATE_ROWS, A_WIDTH), lambda i: (i, 0)),
                  pl.BlockSpec((GATE_ROWS, A_WIDTH), lambda i: (i, 1)),
                  pl.BlockSpec((GATE_ROWS, A_WIDTH), lambda i: (i, 0)),
                  _full_spec((1, A_WIDTH)), _full_spec((1, A_WIDTH)), _full_spec((A_GROUPS, CHUNK, CHUNK)),
                  _full_spec((A_GROUPS, CHUNK, CHUNK)), _full_spec((CHUNK, A_WIDTH)),
                  _full_spec((A_WIDTH, A_WIDTH))] + [ANY] * nx,
        out_specs=[pl.BlockSpec((GATE_ROWS, 2 * A_WIDTH), lambda i: (i, 0)),
                   _full_spec((1, A_WIDTH)), _full_spec((1, A_WIDTH)), _full_spec((A_GROUPS, CHUNK, CHUNK)),
                   _full_spec((CHUNK, A_WIDTH))] + [ANY] * nx,
        out_shape=[jax.ShapeDtypeStruct((m, IN_COLS), BF16),
                   jax.ShapeDtypeStruct((1, A_WIDTH), F32), jax.ShapeDtypeStruct((1, A_WIDTH), F32),
                   jax.ShapeDtypeStruct((A_GROUPS, CHUNK, CHUNK), F32),
                   jax.ShapeDtypeStruct((CHUNK, A_WIDTH), F32)]
        + [jax.ShapeDtypeStruct((N_SHARD, s[1] // 2, s[2]), F32) for s in shapes],
        scratch_shapes=_sem_pair(nx),
        compiler_params=_params(("arbitrary",)), name="gate_bwd",
    )(uv, uv, dmix, ln_g, ln_b, w_s, w_st, bz, pavg, *grads)
    return res[:5], list(res[5:])


Q_BLOCK = 128
PAIR = 2 * HEAD_DIM
N_PAIR = B_HEADS // 2
N_CFG = len(DILATED)
BLOCKS_PER_CFG = SEQ // Q_BLOCK
QKV_SLABS = 3 * N_PAIR
FWD_BLOCKS_PER_TRIP = 8
BWD_BLOCKS_PER_TRIP = 4


def _t5_bucket_np(dist, dtype):
    max_exact = NUM_BUCKETS // 2
    d = np.maximum(dist, 1).astype(dtype)
    large = max_exact + (np.log(d / dtype(max_exact)) / dtype(math.log(MAX_DISTANCE / max_exact))
                         * dtype(NUM_BUCKETS - max_exact))
    large = np.minimum(large.astype(np.int32), NUM_BUCKETS - 1)
    return np.where(dist < max_exact, dist, large)


def _bucket_tables():
    i = np.arange(Q_BLOCK)[:, None]
    j = np.arange(Q_BLOCK)[None, :]
    tables = []
    for _, dil in DILATED:
        rel_prev = Q_BLOCK + i - j
        rel_cur = i - j
        rel = np.concatenate([rel_prev, rel_cur], axis=1)
        valid = np.concatenate([rel_prev <= Q_BLOCK, rel_cur >= 0], axis=1)
        dist = np.maximum(rel, 0) * dil
        b32 = _t5_bucket_np(dist, np.float32)
        b64 = _t5_bucket_np(dist, np.float64)
        assert np.array_equal(b32, b64)
        tables.append(np.where(valid, b32, -1).astype(np.int32))
    return np.stack(tables)


def _present_buckets(buckets_np):
    return [sorted(set(int(v) for v in np.unique(buckets_np[c]) if v >= 0)) for c in range(N_CFG)]


def _bias_tables_body(buckets_np):
    present = _present_buckets(buckets_np)

    def tables(rb_ref, bk_ref, o_ref, ot_ref):
        for c in range(N_CFG):
            bk = bk_ref[c]
            for h in range(B_HEADS):
                acc = jnp.full((Q_BLOCK, 2 * Q_BLOCK), NEG_INF, F32)
                for b in present[c]:
                    acc = jnp.where(bk == b, rb_ref[b, h], acc)
                o_ref[c, h] = acc
                ot_ref[c, h] = acc.T

    return tables


def _proj_fwd(x, g1, w_in_t):
    m = x.shape[0]
    tm = ROW_TILE

    def body(x_ref, g_ref, w_ref, h_ref, uv_ref, qkv_ref):
        xv = x_ref[...]
        h = (xv * _rstd(xv) * g_ref[...]).astype(BF16)
        h_ref[...] = h
        acc = _dot_nt(h, w_ref[...])
        uv_ref[...] = acc[:, :2 * A_WIDTH]
        for s in range(QKV_SLABS):
            qkv_ref[s] = acc[:, 2 * A_WIDTH + s * PAIR:2 * A_WIDTH + (s + 1) * PAIR]

    return pl.pallas_call(
        body, grid=(m // tm,),
        in_specs=[pl.BlockSpec((tm, D_MODEL), lambda i: (i, 0)), _vec_spec(),
                  pl.BlockSpec((IN_COLS, D_MODEL), lambda i: (0, 0))],
        out_specs=[pl.BlockSpec((tm, D_MODEL), lambda i: (i, 0)),
                   pl.BlockSpec((tm, 2 * A_WIDTH), lambda i: (i, 0)),
                   pl.BlockSpec((QKV_SLABS, tm, PAIR), lambda i: (0, i, 0))],
        out_shape=[jax.ShapeDtypeStruct((m, D_MODEL), BF16), jax.ShapeDtypeStruct((m, 2 * A_WIDTH), F32),
                   jax.ShapeDtypeStruct((QKV_SLABS, m, PAIR), F32)],
        compiler_params=_params(("parallel",)), name="proj_fwd",
    )(x, g1, w_in_t)


def _pair_masks():
    lane = lax.broadcasted_iota(jnp.int32, (1, PAIR), 1)
    return [lane < HEAD_DIM, lane >= HEAD_DIM]


def _block_rows(idx, dil):
    static = isinstance(idx, int)
    r, n = idx % dil, idx // dil

    def rows_of(block):
        start = r + (dil * Q_BLOCK) * block
        if dil == 1:
            return pl.ds(start if static else pl.multiple_of(start, Q_BLOCK), Q_BLOCK)
        return pl.ds(start, Q_BLOCK, stride=dil)

    prev = rows_of(n - 1) if not static or n > 0 else None
    return rows_of(n), prev


def _attn_fwd(qkv, bias, batch, shards):
    m = qkv.shape[1]
    comb_rows = 256
    nt = len(shards)
    shapes = [sh.shape for sh in shards]
    n_steps = batch * N_PAIR
    early, late = list(range(nt // 2)), list(range(nt // 2, nt))

    def body(q_ref, k_ref, v_ref, b_ref, *rest):
        shard_refs = rest[:nt]
        o_ref, l_ref = rest[nt:nt + 2]
        gat_refs = rest[nt + 2:2 * nt + 2]
        scratch = rest[2 * nt + 2:]
        oc_refs, lc_refs = scratch[:N_CFG], scratch[N_CFG:2 * N_CFG]
        step = pl.program_id(0) * N_PAIR + pl.program_id(1)
        gather = _GatherPlan(shapes, shard_refs, gat_refs, *scratch[2 * N_CFG:])

        @pl.when(step == 0)
        def _():
            gather.start(early + late)

        @pl.when(step == n_steps // 2)
        def _():
            gather.forward(early)

        @pl.when(step == n_steps - 2)
        def _():
            gather.forward(late)

        masks = _pair_masks()
        for ci, (_, dil) in enumerate(DILATED):
            nb = SEQ // dil // Q_BLOCK

            def block(trip, ci=ci, dil=dil, nb=nb):
                work = []
                for u in range(FWD_BLOCKS_PER_TRIP):
                    rows, prow = _block_rows(trip * FWD_BLOCKS_PER_TRIP + u, dil)
                    has_prev = nb > 1 and prow is not None
                    q = q_ref[rows, :] * 0.125
                    kc = k_ref[rows, :].astype(BF16)
                    vc = v_ref[rows, :]
                    kp = k_ref[prow, :].astype(BF16) if has_prev else None
                    vp = v_ref[prow, :] if has_prev else None
                    tiles = []
                    for h in range(2):
                        qh = jnp.where(masks[h], q, 0.0).astype(BF16)
                        sc = _dot_nt(qh, kc) + b_ref[ci, h, :, Q_BLOCK:]
                        sp = _dot_nt(qh, kp) + b_ref[ci, h, :, :Q_BLOCK] if has_prev else None
                        tiles.append((sc, sp))
                    work.append((rows, vc, vp, tiles))
                probs = []
                for _, _, _, tiles in work:
                    ps = []
                    for sc, sp in tiles:
                        mx = jnp.max(sc if sp is None else jnp.maximum(sc, sp), axis=1, keepdims=True)
                        pc = jnp.exp(sc - mx).astype(BF16)
                        pp = None if sp is None else jnp.exp(sp - mx).astype(BF16)
                        ps.append((mx, pc, pp))
                    probs.append(ps)
                for (rows, vc, vp, _), ps in zip(work, probs):
                    res = []
                    for h, (_, pc, pp) in enumerate(ps):
                        r = _dot(pc, jnp.where(masks[h], vc, 1.0).astype(BF16))
                        if pp is not None:
                            r = r + _dot(pp, jnp.where(masks[h], vp, 1.0).astype(BF16))
                        res.append(r)
                    num = jnp.where(masks[0], res[0], res[1])
                    den = pltpu.roll(jnp.where(masks[0], res[1], res[0]), HEAD_DIM, 1)
                    oc_refs[ci][rows, :] = num / den
                    lc_refs[ci][rows, :] = jnp.where(masks[0], ps[0][0], ps[1][0]) + jnp.log(den)

            for trip in range(BLOCKS_PER_CFG // FWD_BLOCKS_PER_TRIP):
                block(trip)

        def combine(i, carry):
            rr = pl.ds(pl.multiple_of(i * comb_rows, comb_rows), comb_rows)
            ls = [lc_refs[c][rr, :] for c in range(N_CFG)]
            mx = functools.reduce(jnp.maximum, ls)
            ws = [jnp.exp(l - mx) for l in ls]
            tot = functools.reduce(lambda a, b: a + b, ws)
            o = functools.reduce(lambda a, b: a + b, [ws[c] * oc_refs[c][rr, :] for c in range(N_CFG)]) / tot
            o_ref[rr, :] = o.astype(BF16)
            l_ref[rr, :] = mx + jnp.log(tot)
            return carry

        lax.fori_loop(0, SEQ // comb_rows, combine, 0)

        @pl.when(step == n_steps - 1)
        def _():
            gather.finish(early + late)

    def slab(first):
        return pl.BlockSpec((None, SEQ, PAIR), lambda b, p: (first + p, b, 0))

    nat = pl.BlockSpec((SEQ, PAIR), lambda b, p: (b, p))
    res = pl.pallas_call(
        body, grid=(batch, N_PAIR),
        in_specs=[slab(0), slab(N_PAIR), slab(2 * N_PAIR),
                  pl.BlockSpec((N_CFG, 2, Q_BLOCK, 2 * Q_BLOCK), lambda b, p: (0, p, 0, 0))] + [ANY] * nt,
        out_specs=[nat, nat] + [ANY] * nt,
        out_shape=[jax.ShapeDtypeStruct((m, B_WIDTH), BF16), jax.ShapeDtypeStruct((m, B_WIDTH), F32)]
        + [jax.ShapeDtypeStruct((N_SHARD,) + sh.shape, sh.dtype) for sh in shards],
        scratch_shapes=[pltpu.VMEM((SEQ, PAIR), F32)] * (2 * N_CFG) + _sem_pair(6 * nt),
        compiler_params=_params(("arbitrary", "arbitrary")), name="attn_fwd",
    )(qkv, qkv, qkv, bias, *shards)
    return res[0], res[1], list(res[2:])


def _attn_bwd(qkv, dmix, o, lse, bias_t, dproj, batch, parts, smalls):
    m = qkv.shape[1]
    nt, ns = len(parts), len(smalls)
    n_steps = N_PAIR * batch

    def body(q_ref, k_ref, v_ref, do_ref, o_ref, l_ref, b_ref, *rest):
        part_refs = rest[1:nt + 1]
        small_refs = rest[nt + 1:nt + 1 + ns]
        pos = nt + 1 + ns
        dproj_ref, ds_ref = rest[pos:pos + 2]
        recv_refs = rest[pos + 2:pos + 2 + nt]
        sum_refs = rest[pos + 2 + nt:pos + 2 + nt + ns]
        pos += 2 + nt + ns
        dq_acc, dk_acc, dv_acc, d_scr, stage, stage_sems, send_sems, recv_sems = rest[pos:pos + 8]
        allreduce = _SmallAllReducePlan(small_refs, sum_refs, rest[pos + 8:pos + 8 + ns],
                                        rest[pos + 8 + ns:pos + 8 + 2 * ns], *rest[pos + 8 + 2 * ns:])
        pair, seq = pl.program_id(0), pl.program_id(1)
        step = pair * batch + seq
        exchange = _ChipExchangePlan(part_refs, recv_refs, send_sems, recv_sems)

        @pl.when(step == 0)
        def _():
            allreduce.start_sibling()

        @pl.when(step == n_steps // 2)
        def _():
            allreduce.sum_sibling_and_start_chips()

        def stage_copies():
            rows = pl.ds(pl.multiple_of(seq * SEQ, SEQ), SEQ)
            return [pltpu.make_async_copy(
                stage.at[k],
                dproj_ref.at[rows, pl.ds(pl.multiple_of(2 * A_WIDTH + k * B_WIDTH + pair * PAIR, PAIR), PAIR)],
                stage_sems.at[k]) for k in range(3)]

        @pl.when(step == 0)
        def _():
            exchange.start()

        @pl.when(pl.program_id(1) == 0)
        def _():
            ds_ref[...] = jnp.zeros_like(ds_ref)

        dq_acc[...] = jnp.zeros_like(dq_acc)
        dk_acc[...] = jnp.zeros_like(dk_acc)
        dv_acc[...] = jnp.zeros_like(dv_acc)
        d_scr[...] = do_ref[...] * o_ref[...].astype(F32)
        masks = _pair_masks()

        def stack_heads(t):
            return jnp.concatenate([jnp.where(masks[0], t, 0.0), jnp.where(masks[1], t, 0.0)], axis=0).astype(BF16)

        for ci, (_, dil) in enumerate(DILATED):
            nb = SEQ // dil // Q_BLOCK

            def block(trip, carry, ci=ci, dil=dil, nb=nb):
                first = []
                for u in range(BWD_BLOCKS_PER_TRIP):
                    rows, prow = _block_rows(trip * BWD_BLOCKS_PER_TRIP + u, dil)
                    has_prev = nb > 1 and prow is not None
                    if has_prev:
                        kcat = jnp.concatenate([k_ref[prow, :], k_ref[rows, :]], axis=0).astype(BF16)
                        vcat = jnp.concatenate([v_ref[prow, :], v_ref[rows, :]], axis=0).astype(BF16)
                    else:
                        kcat = k_ref[rows, :].astype(BF16)
                        vcat = v_ref[rows, :].astype(BF16)
                    qst = stack_heads(q_ref[rows, :] * 0.125)
                    dost = stack_heads(do_ref[rows, :])
                    lt = l_ref[rows, :].T
                    dt = d_scr[rows, :].T
                    lrow = jnp.concatenate([lt[0:1], lt[HEAD_DIM:HEAD_DIM + 1]], axis=1)
                    drow = jnp.concatenate([jnp.sum(dt[:HEAD_DIM], axis=0, keepdims=True),
                                            jnp.sum(dt[HEAD_DIM:], axis=0, keepdims=True)], axis=1)
                    first.append((has_prev, rows, prow, kcat, qst, dost, lrow, drow,
                                  _dot_nt(kcat, qst), _dot_nt(vcat, dost)))
                second = []
                for has_prev, rows, prow, kcat, qst, dost, lrow, drow, st, dpt in first:
                    keys = slice(0, 2 * Q_BLOCK) if has_prev else slice(Q_BLOCK, 2 * Q_BLOCK)
                    bt = jnp.concatenate([b_ref[ci, 0, keys, :], b_ref[ci, 1, keys, :]], axis=1)
                    pt = jnp.exp(st + bt - lrow)
                    dst = pt * (dpt - drow)
                    ds_ref[ci, 0, keys, :] += dst[:, :Q_BLOCK]
                    ds_ref[ci, 1, keys, :] += dst[:, Q_BLOCK:]
                    second.append((has_prev, rows, prow, kcat, qst, dost, pt.astype(BF16), dst.astype(BF16)))
                for has_prev, rows, prow, kcat, qst, dost, pt_bf, dst_bf in second:
                    dk = _dot(dst_bf, qst)
                    dv = _dot(pt_bf, dost)
                    dq2 = _dot_tn(dst_bf, kcat)
                    dq_acc[rows, :] += jnp.where(masks[0], dq2[:Q_BLOCK], dq2[Q_BLOCK:]) * 0.125
                    if has_prev:
                        dk_acc[prow, :] += dk[:Q_BLOCK]
                        dv_acc[prow, :] += dv[:Q_BLOCK]
                        dk_acc[rows, :] += dk[Q_BLOCK:]
                        dv_acc[rows, :] += dv[Q_BLOCK:]
                    else:
                        dk_acc[rows, :] += dk
                        dv_acc[rows, :] += dv
                return carry

            for trip in range(BLOCKS_PER_CFG // BWD_BLOCKS_PER_TRIP):
                block(trip, 0)

        @pl.when(step > 0)
        def _():
            for cp in stage_copies():
                cp.wait()

        stage[0] = dq_acc[...].astype(BF16)
        stage[1] = dk_acc[...].astype(BF16)
        stage[2] = dv_acc[...].astype(BF16)
        for cp in stage_copies():
            cp.start()

        @pl.when(step == n_steps - 1)
        def _():
            for cp in stage_copies():
                cp.wait()
            exchange.finish()
            allreduce.finish()

    def slab(first):
        return pl.BlockSpec((None, SEQ, PAIR), lambda p, b: (first + p, b, 0))

    nat = pl.BlockSpec((SEQ, PAIR), lambda p, b: (b, p))
    tbl = pl.BlockSpec((N_CFG, 2, 2 * Q_BLOCK, Q_BLOCK), lambda p, b: (0, p, 0, 0))
    acc = pltpu.VMEM((SEQ, PAIR), F32)
    vm = pl.BlockSpec(memory_space=pltpu.VMEM)
    res = pl.pallas_call(
        body, grid=(N_PAIR, batch),
        in_specs=[slab(0), slab(N_PAIR), slab(2 * N_PAIR),
                  pl.BlockSpec((SEQ, PAIR), lambda p, b: (b, A_WIDTH // PAIR + p)), nat, nat, tbl]
        + [ANY] * (nt + 1) + [vm] * ns,
        out_specs=[ANY, tbl] + [ANY] * nt + [vm] * ns,
        out_shape=[jax.ShapeDtypeStruct(dproj.shape, dproj.dtype),
                   jax.ShapeDtypeStruct((N_CFG, B_HEADS, 2 * Q_BLOCK, Q_BLOCK), F32)]
        + [jax.ShapeDtypeStruct((3,) + p.shape[1:], p.dtype) for p in parts]
        + [jax.ShapeDtypeStruct(a.shape, F32) for a in smalls],
        input_output_aliases={7: 0},
        scratch_shapes=[acc, acc, acc, acc, pltpu.VMEM((3, SEQ, PAIR), BF16), pltpu.SemaphoreType.DMA((3,))]
        + _sem_pair(3 * nt) + _SmallAllReducePlan.scratch(smalls),
        compiler_params=_params(("arbitrary", "arbitrary")), name="attn_bwd",
    )(qkv, qkv, qkv, dmix, o, lse, bias_t, dproj, *parts, *smalls)
    return res[0], res[1], list(res[2:2 + nt]), list(res[2 + nt:])


def _rel_bias_grad(ds, buckets_np, grads):
    present = _present_buckets(buckets_np)
    nx = len(grads)
    shapes = [g.shape for g in grads]

    def body(bk_ref, ds_ref, *rest):
        o_ref = rest[nx]
        acc_ref = rest[2 * nx + 1]
        exchange = _SiblingExchangePlan(shapes, rest[:nx], rest[nx + 1:2 * nx + 1], *rest[2 * nx + 2:])
        exchange.start()
        acc_ref[...] = jnp.zeros_like(acc_ref)
        for c in range(N_CFG):
            bk = bk_ref[c]
            for h in range(B_HEADS):
                dsv = ds_ref[c, h]
                for b in present[c]:
                    part = jnp.sum(jnp.where(bk == b, dsv, 0.0), axis=0, keepdims=True)
                    acc_ref[pl.ds(h * NUM_BUCKETS + b, 1), :] += part
        o_ref[...] = jnp.sum(acc_ref[...], axis=1, keepdims=True)
        exchange.finish()

    vm = pl.BlockSpec(memory_space=pltpu.VMEM)
    res = pl.pallas_call(
        body, in_specs=[vm, vm] + [ANY] * nx, out_specs=[vm] + [ANY] * nx,
        out_shape=[jax.ShapeDtypeStruct((B_HEADS * NUM_BUCKETS, 1), F32)]
        + [jax.ShapeDtypeStruct((N_SHARD, s[1] // 2, s[2]), F32) for s in shapes],
        scratch_shapes=[pltpu.VMEM((B_HEADS * NUM_BUCKETS, buckets_np.shape[-1]), F32)] + _sem_pair(nx),
        compiler_params=_params(), name="rel_bias_grad",
    )(jnp.asarray(buckets_np), ds, *grads)
    return res[0], list(res[1:])


def _row_index():
    return lax.broadcasted_iota(jnp.int32, (SEQ, LANE_BLOCK), 0)


def _shift_down(x, k, row):
    return jnp.where(row >= k, pltpu.roll(x, k, 0), 0.0)


def _shift_up(x, k, row):
    return jnp.where(row < SEQ - k, pltpu.roll(x, SEQ - k, 0), 0.0)


def _convgate_fwd(gate, up, conv_w, conv_b, batch):
    m = gate.shape[0]

    def body(g_ref, u_ref, w_ref, b_ref, a_ref):
        g = g_ref[...].astype(F32)
        w = w_ref[...]
        row = _row_index()
        c = b_ref[...] + w[0:1] * _shift_down(g, 2, row) + w[1:2] * _shift_down(g, 1, row) + w[2:3] * g
        a_ref[...] = (_gelu(c) * u_ref[...].astype(F32)).astype(BF16)

    blk = pl.BlockSpec((SEQ, LANE_BLOCK), lambda b, j: (b, j))
    return pl.pallas_call(
        body, grid=(batch, D_FF // LANE_BLOCK),
        in_specs=[blk, blk, pl.BlockSpec((3, LANE_BLOCK), lambda b, j: (0, j)),
                  pl.BlockSpec((1, LANE_BLOCK), lambda b, j: (0, j))],
        out_specs=blk,
        out_shape=jax.ShapeDtypeStruct((m, D_FF), BF16),
        compiler_params=_params(("parallel", "parallel")), name="convgate_fwd",
    )(gate, up, conv_w, conv_b)


def _convgate_bwd(gate, up, dact, conv_w, conv_b, batch):
    m = gate.shape[0]

    def body(g_ref, u_ref, da_ref, w_ref, b_ref, dg_ref, du_ref, dw_ref, db_ref):
        @pl.when(pl.program_id(1) == 0)
        def _():
            dw_ref[...] = jnp.zeros_like(dw_ref)
            db_ref[...] = jnp.zeros_like(db_ref)

        g = g_ref[...].astype(F32)
        w = w_ref[...]
        row = _row_index()
        g1 = _shift_down(g, 1, row)
        g2 = _shift_down(g, 2, row)
        c = b_ref[...] + w[0:1] * g2 + w[1:2] * g1 + w[2:3] * g
        gg, dgg = _gelu_and_grad(c)
        da = da_ref[...].astype(F32)
        du_ref[...] = (da * gg).astype(BF16)
        dc = da * u_ref[...].astype(F32) * dgg
        db_ref[...] += jnp.sum(dc, axis=0, keepdims=True)
        dw_ref[0:1, :] += jnp.sum(dc * g2, axis=0, keepdims=True)
        dw_ref[1:2, :] += jnp.sum(dc * g1, axis=0, keepdims=True)
        dw_ref[2:3, :] += jnp.sum(dc * g, axis=0, keepdims=True)
        dg_ref[...] = (w[2:3] * dc + w[1:2] * _shift_up(dc, 1, row) + w[0:1] * _shift_up(dc, 2, row)).astype(BF16)

    blk = pl.BlockSpec((SEQ, LANE_BLOCK), lambda j, b: (b, j))
    wspec = pl.BlockSpec((3, LANE_BLOCK), lambda j, b: (0, j))
    bspec = pl.BlockSpec((1, LANE_BLOCK), lambda j, b: (0, j))
    return pl.pallas_call(
        body, grid=(D_FF // LANE_BLOCK, batch),
        in_specs=[blk, blk, blk, wspec, bspec],
        out_specs=[blk, blk, wspec, bspec],
        out_shape=[jax.ShapeDtypeStruct((m, D_FF), BF16), jax.ShapeDtypeStruct((m, D_FF), BF16),
                   jax.ShapeDtypeStruct((3, D_FF), F32), jax.ShapeDtypeStruct((1, D_FF), F32)],
        compiler_params=_params(("parallel", "arbitrary")), name="convgate_bwd",
    )(gate, up, dact, conv_w, conv_b)


def _gather_weights(shards, conv_w_shard, rel_bias, buckets_np):
    nt = len(shards)
    shapes = [sh.shape for sh in shards]
    ts = list(range(nt))
    tables = _bias_tables_body(buckets_np)

    def body(*refs):
        shard_refs = refs[:nt]
        cw_ref, rb_ref, bk_ref = refs[nt:nt + 3]
        out_refs = refs[nt + 3:2 * nt + 3]
        cw_out, bias_ref, bias_t_ref = refs[2 * nt + 3:2 * nt + 6]
        send_sems, recv_sems, cw_send, cw_recv = refs[2 * nt + 6:]
        plan = _GatherPlan(shapes, shard_refs, out_refs, send_sems, recv_sems)
        x, y, c, chips = _mesh_pos()

        def cw_copy(j, src, dst, chip):
            return pltpu.make_async_remote_copy(src_ref=src, dst_ref=dst, send_sem=cw_send.at[j],
                                                recv_sem=cw_recv.at[j], device_id=(*chip, c), device_id_type=MESH)

        plan.start(ts)
        cw_sends = [cw_copy(j, cw_ref, cw_out.at[2 * x + y], chip) for j, chip in enumerate(chips)]
        for cp in cw_sends:
            cp.start()
        tables(rb_ref, bk_ref, bias_ref, bias_t_ref)
        plan.forward(ts)
        for j, chip in enumerate(chips):
            dst = cw_out.at[2 * chip[0] + chip[1]]
            cw_copy(j, dst, dst, chip).wait_recv()
        plan.finish(ts)
        for cp in cw_sends:
            cp.wait_send()

    out_shape = [jax.ShapeDtypeStruct((N_SHARD,) + sh.shape, sh.dtype) for sh in shards]
    out_shape.append(jax.ShapeDtypeStruct((N_SHARD,) + conv_w_shard.shape, conv_w_shard.dtype))
    out_shape += [jax.ShapeDtypeStruct((N_CFG, B_HEADS, Q_BLOCK, 2 * Q_BLOCK), F32),
                  jax.ShapeDtypeStruct((N_CFG, B_HEADS, 2 * Q_BLOCK, Q_BLOCK), F32)]
    vm = pl.BlockSpec(memory_space=pltpu.VMEM)
    res = pl.pallas_call(
        body, in_specs=[ANY] * (nt + 1) + [pl.BlockSpec(memory_space=pltpu.SMEM), vm],
        out_specs=[ANY] * (nt + 1) + [vm, vm], out_shape=out_shape,
        scratch_shapes=_sem_pair(6 * nt) + _sem_pair(3),
        compiler_params=pltpu.CompilerParams(has_side_effects=True, vmem_limit_bytes=VMEM_LIMIT),
        name="gather_weights",
    )(*shards, conv_w_shard, rel_bias, jnp.asarray(buckets_np))
    return list(res[:nt + 1]), res[nt + 1], res[nt + 2]


def _add_halves(g, recv, c_idx):
    _, rows2, cols = g.shape
    rows = rows2 // 2
    tr = rows
    nblk = rows // tr

    def body(c_ref, g_ref, r_ref, o_ref):
        o_ref[...] = (g_ref[...] + r_ref[...]).astype(BF16)

    return pl.pallas_call(
        body,
        grid_spec=pltpu.PrefetchScalarGridSpec(
            num_scalar_prefetch=1, grid=(N_SHARD, nblk),
            in_specs=[pl.BlockSpec((None, tr, cols), lambda s, i, c: (s, c[0] * nblk + i, 0)),
                      pl.BlockSpec((None, tr, cols), lambda s, i, c: (s, i, 0))],
            out_specs=pl.BlockSpec((None, tr, cols), lambda s, i, c: (s, i, 0))),
        out_shape=jax.ShapeDtypeStruct((N_SHARD, rows, cols), BF16),
        compiler_params=_params(("parallel", "parallel")), name="rs_add_halves",
    )(c_idx, g, recv)


def _add_chips(part, recv, s_idx, c_idx):
    _, rows, cols = part.shape
    tr = rows
    nblk = rows // tr

    def body(idx_ref, p_ref, r_ref, o_ref):
        acc = p_ref[...].astype(F32)
        for j in range(3):
            acc = acc + r_ref[j].astype(F32)
        o_ref[...] = acc

    return pl.pallas_call(
        body,
        grid_spec=pltpu.PrefetchScalarGridSpec(
            num_scalar_prefetch=1, grid=(nblk,),
            in_specs=[pl.BlockSpec((None, tr, cols), lambda i, idx: (idx[0], i, 0)),
                      pl.BlockSpec((3, tr, cols), lambda i, idx: (0, i, 0))],
            out_specs=pl.BlockSpec((tr, cols), lambda i, idx: (idx[1] * nblk + i, 0))),
        out_shape=jax.ShapeDtypeStruct((2 * rows, cols), F32),
        compiler_params=_params(("parallel",)), name="rs_add_chips",
    )(jnp.concatenate([s_idx, c_idx]), part, recv)


def _finish_reductions(fulls, arrays):
    nt, n = len(fulls), len(arrays)

    def body(*refs):
        in_refs = refs[nt:nt + n]
        full_refs, out_refs = refs[nt + n:2 * nt + n], refs[2 * nt + n:2 * nt + 2 * n]
        pos = 2 * nt + 2 * n
        share_send, share_recv = refs[pos + 2 * n:pos + 2 * n + 2]
        allreduce = _SmallAllReducePlan(in_refs, out_refs, refs[pos:pos + n], refs[pos + n:pos + 2 * n],
                                        *refs[pos + 2 * n + 2:])
        x, y, c, _ = _mesh_pos()

        def half(t, which):
            rows = fulls[t].shape[0] // 2
            return full_refs[t].at[pl.ds(which * rows, rows), :]

        def share(t, which):
            return pltpu.make_async_remote_copy(
                src_ref=half(t, which), dst_ref=half(t, which), send_sem=share_send.at[t],
                recv_sem=share_recv.at[t], device_id=(x, y, 1 - c), device_id_type=MESH)

        for t in range(nt):
            share(t, c).start()
        allreduce.start_sibling()
        allreduce.sum_sibling_and_start_chips()
        allreduce.finish()
        for t in range(nt):
            share(t, 1 - c).wait_recv()
        for t in range(nt):
            share(t, c).wait_send()

    vm = pl.BlockSpec(memory_space=pltpu.VMEM)
    res = pl.pallas_call(
        body, in_specs=[ANY] * nt + [vm] * n, out_specs=[ANY] * nt + [vm] * n,
        out_shape=[jax.ShapeDtypeStruct(f.shape, f.dtype) for f in fulls]
        + [jax.ShapeDtypeStruct(a.shape, F32) for a in arrays],
        input_output_aliases={t: t for t in range(nt)},
        scratch_shapes=[pltpu.VMEM(a.shape, F32) for a in arrays] + [pltpu.VMEM((3,) + a.shape, F32) for a in arrays]
        + _sem_pair(nt) + _sem_pair(4 * n),
        compiler_params=pltpu.CompilerParams(has_side_effects=True),
        name="finish_reductions",
    )(*fulls, *arrays)
    return list(res[:nt]), list(res[nt:])


def _from_col_shards(g):
    n, rows, cols = g.shape
    return g.transpose(1, 0, 2).reshape(rows, n * cols)


def _train_step(x, tgt, g1, g2, g3, g4, shards, ln_g, ln_b, w_s, b_s, rel_bias, conv_w_shard, conv_b, batch,
                s_idx, c_idx):
    big = dict(tm=1024, out_dtype=F32)
    buckets = _bucket_tables()
    bz = jnp.repeat(b_s.T, HEAD_DIM, axis=1)
    w_st = jnp.swapaxes(w_s, 1, 2)

    def with_own(gathered, own):
        return lax.dynamic_update_index_in_dim(gathered, own, s_idx[0], 0)

    def shard_major(g):
        return g.reshape(N_SHARD, g.shape[0] // N_SHARD, D_MODEL)

    (g_in, g_convw), bias, bias_t = _gather_weights([shards["w_in"]], conv_w_shard, rel_bias, buckets)
    w_in_t = with_own(g_in, shards["w_in"]).reshape(IN_COLS, D_MODEL)
    conv_w = _from_col_shards(with_own(g_convw, conv_w_shard))

    h1, uv, qkv = _proj_fwd(x, g1, w_in_t)
    a = _gate_fwd(uv, ln_g, ln_b, w_s, bz)
    later = ["w_out", "w_gate", "w_up"]
    o_bf, lse, gathered = _attn_fwd(qkv, bias, batch, [shards[n] for n in later])
    g_out, g_gate, g_up = [with_own(g, shards[n]) for g, n in zip(gathered, later)]
    w_out = g_out.reshape(D_MODEL, D_MODEL)
    w_gate_t = g_gate.reshape(D_FF, D_MODEL)
    w_up_t = g_up.reshape(D_FF, D_MODEL)
    (y1, x1, h2), _ = _fused_rows(
        "out_proj_mid_fwd", 512,
        [(a, w_out, "nn", slice(0, A_WIDTH)), (o_bf, w_out, "nn", slice(A_WIDTH, D_MODEL))],
        [x], [g2, g3], _mid_fwd_rows, [F32, F32, BF16], [])
    gate, up, g_down = _mm_pair_nt(h2, w_gate_t, w_up_t, shards["w_down"], tm=1024, tn=1408, out_dtype=BF16,
                                   name="mm_gate_up")
    w_down = with_own(g_down, shards["w_down"]).reshape(D_FF, D_MODEL)
    act = _convgate_fwd(gate, up, conv_w, conv_b, batch)
    (dx2, dy2, dg4, loss), _ = _fused_rows(
        "down_proj_loss_head", 512, [(act, w_down, "nn", None)], [x1, tgt], [g4], _loss_head_rows,
        [F32, BF16], [(1, D_MODEL), (1, 128)])

    dact = _mm(dy2, w_down, dims="nt", tm=1024, tn=1408, tk=1024, out_dtype=BF16, name="mm_dact")
    dw_down = _mm(act, dy2, dims="tn", tm=1408, tn=1024, tk=1024, out_dtype=F32, name="mm_dw_down")
    dgate, dup, dconv_w, dconv_b = _convgate_bwd(gate, up, dact, conv_w, conv_b, batch)
    (dx1, dy1, dg2, dg3), _ = _fused_rows(
        "dh2_mid_bwd", 256, [(dgate, w_gate_t, "nn", None), (dup, w_up_t, "nn", None)],
        [x1, y1, dx2], [g2, g3], _mid_bwd_rows, [F32, BF16], [(1, D_MODEL), (1, D_MODEL)])
    dw_gate_t = _mm(dgate, h2, dims="tn", tm=1408, tn=1024, tk=1024, out_dtype=F32, name="mm_dw_gate")
    dw_up_t = _mm(dup, h2, dims="tn", tm=1408, tn=1024, tk=1024, out_dtype=F32, name="mm_dw_up")
    dmix = _mm(dy1, w_out, dims="nt", tn=1024, tk=1024, name="mm_dmix", **big)
    dw_out_a = _mm(a, dy1, dims="tn", tm=A_WIDTH, tn=1024, tk=1024, out_dtype=F32, name="mm_dw_out_a")
    dw_out_b = _mm(o_bf, dy1, dims="tn", tm=B_WIDTH, tn=1024, tk=1024, out_dtype=F32, name="mm_dw_out_b")

    dw_out = jnp.concatenate([dw_out_a, dw_out_b], axis=0)
    done = [shard_major(g) for g in (dw_down, dw_gate_t, dw_up_t, dw_out)]
    (dproj, dln_g, dln_b, dw_s, dbz), recv_a = _gate_bwd(uv, dmix, ln_g, ln_b, w_s, w_st, bz, done)
    parts = [_add_halves(g, r, c_idx) for g, r in zip(done, recv_a)]
    early = dict(loss=loss, norm_mix_post=dg2, norm_ffn_pre=dg3, norm_ffn_post=dg4, ln_v_gain=dln_g,
                 ln_v_bias=dln_b, spatial_w=dw_s, spatial_b=dbz, conv_w=dconv_w, conv_b=dconv_b)
    dproj, ds, recv, early_sums = _attn_bwd(qkv, dmix, o_bf, lse, bias_t, dproj, batch, parts, list(early.values()))
    fulls = [_add_chips(p, r, s_idx, c_idx) for p, r in zip(parts, recv)]
    dw_in_t = _mm(dproj, h1, dims="tn", tm=1408, tn=1024, tk=1024, out_dtype=F32, name="mm_dw_in")
    last = [shard_major(dw_in_t)]
    drel, recv_in_a = _rel_bias_grad(ds, np.ascontiguousarray(np.swapaxes(buckets, 1, 2)), last)
    part_in = [_add_halves(g, r, c_idx) for g, r in zip(last, recv_in_a)]
    (dx0, dg1), recv_in = _fused_rows(
        "dh1_in_bwd", 512, [(dproj, w_in_t, "nn", None)], [x, dx1], [g1], _in_bwd_rows,
        [F32], [(1, D_MODEL)], exchange=part_in)
    fulls += [_add_chips(p, r, s_idx, c_idx) for p, r in zip(part_in, recv_in)]
    half_reduced = dict(zip(["w_down", "w_gate", "w_up", "w_out", "w_in"], fulls))

    return dx0, dict(zip(early, early_sums)), dict(norm_mix_pre=dg1, rel_bias=drel), half_reduced


def _adamw_update(w, g, m, v):
    nm = ADAM_B1 * m + (1.0 - ADAM_B1) * g
    nv = ADAM_B2 * v + (1.0 - ADAM_B2) * (g * g)
    m_hat = nm / (1.0 - ADAM_B1 ** ADAM_STEP)
    v_hat = nv / (1.0 - ADAM_B2 ** ADAM_STEP)
    return -ADAM_LR * (m_hat / (jnp.sqrt(v_hat) + ADAM_EPS) + ADAM_WD * w), nm, nv


def _adamw(w, g, m, v, name):
    rows, cols = w.shape
    tr = next(cand for cand in (352, 256, 128) if rows % cand == 0)

    def body(w_ref, g_ref, m_ref, v_ref, go_ref, d_ref, nm_ref, nv_ref):
        gv = g_ref[...]
        go_ref[...] = gv
        d_ref[...], nm_ref[...], nv_ref[...] = _adamw_update(w_ref[...], gv, m_ref[...], v_ref[...])

    spec = pl.BlockSpec((tr, cols), lambda i: (i, 0))
    sds = jax.ShapeDtypeStruct((rows, cols), F32)
    return pl.pallas_call(
        body, grid=(rows // tr,), in_specs=[spec] * 4, out_specs=[spec] * 4, out_shape=[sds] * 4,
        compiler_params=_params(("parallel",)), name=name,
    )(w, g, m, v)


def _adamw_small(ws, gs, ms, vs):
    n = len(ws)

    def body(*refs):
        w_refs, g_refs, m_refs, v_refs = refs[:n], refs[n:2 * n], refs[2 * n:3 * n], refs[3 * n:4 * n]
        d_refs, nm_refs, nv_refs = refs[4 * n:5 * n], refs[5 * n:6 * n], refs[6 * n:7 * n]
        for t in range(n):
            d_refs[t][...], nm_refs[t][...], nv_refs[t][...] = _adamw_update(
                w_refs[t][...], g_refs[t][...], m_refs[t][...], v_refs[t][...])

    vm = pl.BlockSpec(memory_space=pltpu.VMEM)
    sds = [jax.ShapeDtypeStruct(w.shape, F32) for w in ws]
    res = pl.pallas_call(
        body, in_specs=[vm] * (4 * n), out_specs=[vm] * (3 * n), out_shape=sds * 3,
        compiler_params=_params(), name="adamw_small",
    )(*ws, *gs, *ms, *vs)
    return res[:n], res[n:2 * n], res[2 * n:]


SMALL = ["norm_mix_pre", "norm_mix_post", "norm_ffn_pre", "norm_ffn_post", "ln_v_gain", "ln_v_bias",
         "spatial_w", "spatial_b", "rel_bias", "conv_b"]
LARGE = ["w_in", "w_gate", "w_up", "w_down", "w_out"]
TRANSPOSED = ("w_in", "w_gate", "w_up")
ORDER = ["norm_mix_pre", "norm_mix_post", "norm_ffn_pre", "norm_ffn_post", "w_in", "ln_v_gain", "ln_v_bias",
         "spatial_w", "spatial_b", "rel_bias", "w_out", "w_gate", "w_up", "conv_w", "conv_b", "w_down"]


def kernel(x, norm_mix_pre, norm_mix_post, norm_ffn_pre, norm_ffn_post, w_in, ln_v_gain, ln_v_bias, spatial_w, spatial_b, rel_bias, w_out, w_gate, w_up, conv_w, conv_b, w_down, loss_target, m_norm_mix_pre, m_norm_mix_post, m_norm_ffn_pre, m_norm_ffn_post, m_w_in, m_ln_v_gain, m_ln_v_bias, m_spatial_w, m_spatial_b, m_rel_bias, m_w_out, m_w_gate, m_w_up, m_conv_w, m_conv_b, m_w_down, v_norm_mix_pre, v_norm_mix_post, v_norm_ffn_pre, v_norm_ffn_post, v_w_in, v_ln_v_gain, v_ln_v_bias, v_spatial_w, v_spatial_b, v_rel_bias, v_w_out, v_w_gate, v_w_up, v_conv_w, v_conv_b, v_w_down):
    params = dict(norm_mix_pre=norm_mix_pre, norm_mix_post=norm_mix_post, norm_ffn_pre=norm_ffn_pre,
                  norm_ffn_post=norm_ffn_post, w_in=w_in, ln_v_gain=ln_v_gain, ln_v_bias=ln_v_bias,
                  spatial_w=spatial_w, spatial_b=spatial_b, rel_bias=rel_bias, w_out=w_out, w_gate=w_gate,
                  w_up=w_up, conv_w=conv_w, conv_b=conv_b, w_down=w_down)
    mom = dict(norm_mix_pre=m_norm_mix_pre, norm_mix_post=m_norm_mix_post, norm_ffn_pre=m_norm_ffn_pre,
               norm_ffn_post=m_norm_ffn_post, w_in=m_w_in, ln_v_gain=m_ln_v_gain, ln_v_bias=m_ln_v_bias,
               spatial_w=m_spatial_w, spatial_b=m_spatial_b, rel_bias=m_rel_bias, w_out=m_w_out, w_gate=m_w_gate,
               w_up=m_w_up, conv_w=m_conv_w, conv_b=m_conv_b, w_down=m_w_down)
    var = dict(norm_mix_pre=v_norm_mix_pre, norm_mix_post=v_norm_mix_post, norm_ffn_pre=v_norm_ffn_pre,
               norm_ffn_post=v_norm_ffn_post, w_in=v_w_in, ln_v_gain=v_ln_v_gain, ln_v_bias=v_ln_v_bias,
               spatial_w=v_spatial_w, spatial_b=v_spatial_b, rel_bias=v_rel_bias, w_out=v_w_out, w_gate=v_w_gate,
               w_up=v_w_up, conv_w=v_conv_w, conv_b=v_conv_b, w_down=v_w_down)

    batch = x.shape[0]
    xi, yi, ci = lax.axis_index("x"), lax.axis_index("y"), lax.axis_index("c")
    s_idx = (2 * xi + yi).astype(jnp.int32).reshape(1)
    c_idx = ci.astype(jnp.int32).reshape(1)

    def local(a, n):
        return jnp.swapaxes(a[0], 0, 1) if n in TRANSPOSED else a[0]

    shards = {n: local(params[n], n).astype(BF16) for n in LARGE}
    dx0, total, partial, half_reduced = _train_step(
        x.reshape(batch * SEQ, D_MODEL), loss_target.reshape(batch * SEQ, D_MODEL),
        norm_mix_pre, norm_mix_post, norm_ffn_pre, norm_ffn_post, shards,
        ln_v_gain.reshape(1, A_WIDTH), ln_v_bias.reshape(1, A_WIDTH), spatial_w[0], spatial_b[0], rel_bias,
        conv_w[0], conv_b, batch, s_idx, c_idx)
    grad_x = dx0.reshape(batch, SEQ, D_MODEL)

    names = list(partial)
    fulls, sums = _finish_reductions([half_reduced[n] for n in LARGE], [partial[n] for n in names])
    reduced = dict(zip(LARGE, fulls))
    total.update(zip(names, sums))
    loss = total["loss"][0, 0]
    total["spatial_b"] = total["spatial_b"][:, ::HEAD_DIM].T
    total["rel_bias"] = total["rel_bias"].reshape(B_HEADS, NUM_BUCKETS).T
    total["conv_w"] = lax.dynamic_slice_in_dim(total["conv_w"], s_idx[0] * SHARD_FF, SHARD_FF, axis=1)
    small_names = SMALL + ["conv_w"]
    for n in small_names:
        reduced[n] = total[n].reshape(params[n].shape)

    out_g, out_d, out_m, out_v = {}, {}, {}, {}
    for n in LARGE:
        res = _adamw(local(params[n], n), reduced[n], local(mom[n], n), local(var[n], n), name=f"adamw_{n}")
        if n in TRANSPOSED:
            res = [jnp.swapaxes(r, 0, 1) for r in res]
        out_g[n], out_d[n], out_m[n], out_v[n] = [r[None] for r in res]
    d, nm, nv = _adamw_small([params[n] for n in small_names], [reduced[n] for n in small_names],
                             [mom[n] for n in small_names], [var[n] for n in small_names])
    for n, dd, mm, vv in zip(small_names, d, nm, nv):
        out_g[n], out_d[n], out_m[n], out_v[n] = reduced[n], dd, mm, vv

    return (loss, grad_x, *[out_g[n] for n in ORDER], *[out_d[n] for n in ORDER],
            *[out_m[n] for n in ORDER], *[out_v[n] for n in ORDER])
```

```python
import functools
import math

import numpy as np
import jax
import jax.numpy as jnp
from jax import lax
from jax.experimental import pallas as pl
from jax.experimental.pallas import tpu as pltpu

F32 = jnp.float32
BF16 = jnp.bfloat16
MESH = pl.DeviceIdType.MESH

D_MODEL = 1024
SEQ = 2048
HEAD_DIM = 64
A_GROUPS = 4
A_WIDTH = 256
B_HEADS = 12
B_WIDTH = 768
CHUNK = 128
DILATED = ((128, 1), (512, 4), (2048, 16))
NUM_BUCKETS = 32
MAX_DISTANCE = 2048
D_FF = 2816
IN_COLS = 2816
NORM_EPS = 1e-6
NEG_INF = -1e30
N_SHARD = 4
SHARD_FF = D_FF // N_SHARD
LANE_BLOCK = 256
VMEM_LIMIT = 56 * 1024 * 1024

ADAM_LR = 0.001
ADAM_B1 = 0.9
ADAM_B2 = 0.999
ADAM_EPS = 1e-08
ADAM_WD = 0.01
ADAM_STEP = 10

GELU_C = math.sqrt(2.0 / math.pi)
GELU_A = 0.044715

ANY = pl.BlockSpec(memory_space=pl.ANY)


def _params(sem=None):
    return pltpu.CompilerParams(dimension_semantics=sem, vmem_limit_bytes=VMEM_LIMIT)


def _dot(a, b, precision=None):
    return jnp.dot(a, b, preferred_element_type=F32, precision=precision)


def _dot_nt(a, b, precision=None):
    return lax.dot_general(a, b, (((1,), (1,)), ((), ())), preferred_element_type=F32, precision=precision)


def _dot_tn(a, b):
    return lax.dot_general(a, b, (((0,), (0,)), ((), ())), preferred_element_type=F32)


def _gelu(x):
    t = jnp.tanh(x * (GELU_C + (GELU_C * GELU_A) * (x * x)))
    return (0.5 * x) * (1.0 + t)


def _gelu_and_grad(x):
    x2 = x * x
    u = 1.0 + jnp.tanh(x * (GELU_C + (GELU_C * GELU_A) * x2))
    hx = 0.5 * x
    dg = u * (0.5 + hx * (2.0 - u) * (GELU_C + (3.0 * GELU_C * GELU_A) * x2))
    return hx * u, dg


def _mesh_pos():
    x, y, c = lax.axis_index("x"), lax.axis_index("y"), lax.axis_index("c")
    chips = [(1 - x, y), (x, 1 - y), (1 - x, 1 - y)]
    return x, y, c, chips


class _GatherPlan:
    def __init__(self, shapes, shard_refs, out_refs, send_sems, recv_sems):
        self.shapes, self.shard_refs, self.out_refs = shapes, shard_refs, out_refs
        self.send_sems, self.recv_sems = send_sems, recv_sems
        self.x, self.y, self.c, self.chips = _mesh_pos()
        self.sib = (self.x, self.y, 1 - self.c)

    def _half(self, t, chip, which):
        rows = self.shapes[t][0] // 2
        return self.out_refs[t].at[2 * chip[0] + chip[1], pl.ds(which * rows, rows), :]

    def _copy(self, k, src, dst, to):
        return pltpu.make_async_remote_copy(src_ref=src, dst_ref=dst, send_sem=self.send_sems.at[k],
                                            recv_sem=self.recv_sems.at[k], device_id=to, device_id_type=MESH)

    def _sends(self, t):
        rows = self.shapes[t][0] // 2
        src = self.shard_refs[t].at[pl.ds(self.c * rows, rows), :]
        return [self._copy(6 * t + j, src, self._half(t, (self.x, self.y), self.c), (*chip, self.c))
                for j, chip in enumerate(self.chips)]

    def _forwards(self, t):
        return [self._copy(6 * t + 3 + j, self._half(t, chip, self.c), self._half(t, chip, self.c), self.sib)
                for j, chip in enumerate(self.chips)]

    def start(self, ts):
        for t in ts:
            for cp in self._sends(t):
                cp.start()

    def forward(self, ts):
        for t in ts:
            for j, chip in enumerate(self.chips):
                landed = self._half(t, chip, self.c)
                self._copy(6 * t + j, landed, landed, (*chip, self.c)).wait_recv()
            for cp in self._forwards(t):
                cp.start()

    def finish(self, ts):
        for t in ts:
            for j, chip in enumerate(self.chips):
                other = self._half(t, chip, 1 - self.c)
                self._copy(6 * t + 3 + j, other, other, self.sib).wait_recv()
        for t in ts:
            for cp in self._sends(t) + self._forwards(t):
                cp.wait_send()


class _SiblingExchangePlan:
    def __init__(self, shapes, grad_refs, out_refs, send_sems, recv_sems):
        self.shapes, self.grad_refs, self.out_refs = shapes, grad_refs, out_refs
        self.send_sems, self.recv_sems = send_sems, recv_sems
        self.x, self.y, self.c, _ = _mesh_pos()

    def _copies(self):
        out = []
        for t, (g, o) in enumerate(zip(self.grad_refs, self.out_refs)):
            rows = self.shapes[t][1] // 2
            out.append(pltpu.make_async_remote_copy(
                src_ref=g.at[:, pl.ds((1 - self.c) * rows, rows), :], dst_ref=o, send_sem=self.send_sems.at[t],
                recv_sem=self.recv_sems.at[t], device_id=(self.x, self.y, 1 - self.c), device_id_type=MESH))
        return out

    def start(self):
        for cp in self._copies():
            cp.start()

    def finish(self):
        for cp in self._copies():
            cp.wait()


class _ChipExchangePlan:
    def __init__(self, part_refs, out_refs, send_sems, recv_sems):
        self.part_refs, self.out_refs, self.send_sems, self.recv_sems = part_refs, out_refs, send_sems, recv_sems
        _, _, self.c, self.chips = _mesh_pos()

    def _copies(self):
        return [pltpu.make_async_remote_copy(
            src_ref=p.at[2 * chip[0] + chip[1]], dst_ref=o.at[j], send_sem=self.send_sems.at[3 * t + j],
            recv_sem=self.recv_sems.at[3 * t + j], device_id=(*chip, self.c), device_id_type=MESH)
            for t, (p, o) in enumerate(zip(self.part_refs, self.out_refs)) for j, chip in enumerate(self.chips)]

    def start(self):
        for cp in self._copies():
            cp.start()

    def finish(self):
        for cp in self._copies():
            cp.wait()


class _SmallAllReducePlan:
    def __init__(self, in_refs, out_refs, sib_refs, chip_refs, send_sems, recv_sems):
        self.in_refs, self.out_refs, self.sib_refs, self.chip_refs = in_refs, out_refs, sib_refs, chip_refs
        self.send_sems, self.recv_sems = send_sems, recv_sems
        self.n = len(in_refs)
        self.x, self.y, self.c, self.chips = _mesh_pos()

    def _copy(self, k, src, dst, to):
        return pltpu.make_async_remote_copy(src_ref=src, dst_ref=dst, send_sem=self.send_sems.at[k],
                                            recv_sem=self.recv_sems.at[k], device_id=to, device_id_type=MESH)

    def _first(self):
        return [self._copy(t, self.in_refs[t], self.sib_refs[t], (self.x, self.y, 1 - self.c)) for t in range(self.n)]

    def _second(self):
        return [self._copy(self.n + 3 * t + j, self.out_refs[t], self.chip_refs[t].at[j], (*chip, self.c))
                for t in range(self.n) for j, chip in enumerate(self.chips)]

    def start_sibling(self):
        for cp in self._first():
            cp.start()

    def sum_sibling_and_start_chips(self):
        for cp in self._first():
            cp.wait()
        for t in range(self.n):
            self.out_refs[t][...] = self.in_refs[t][...] + self.sib_refs[t][...]
        for cp in self._second():
            cp.start()

    def finish(self):
        for cp in self._second():
            cp.wait()
        for t in range(self.n):
            self.out_refs[t][...] = ((self.out_refs[t][...] + self.chip_refs[t][0])
                                     + (self.chip_refs[t][1] + self.chip_refs[t][2]))

    @staticmethod
    def scratch(arrays):
        return ([pltpu.VMEM(a.shape, F32) for a in arrays] + [pltpu.VMEM((3,) + a.shape, F32) for a in arrays]
                + _sem_pair(4 * len(arrays)))


def _sem_pair(n):
    return [pltpu.SemaphoreType.DMA((n,)), pltpu.SemaphoreType.DMA((n,))]


def _mm(a, b, *, dims, tm, tn, tk, out_dtype, name):
    if dims == "nn":
        m, k = a.shape
        n = b.shape[1]
        a_spec = pl.BlockSpec((tm, tk), lambda i, j, kk: (i, kk))
        b_spec = pl.BlockSpec((tk, tn), lambda i, j, kk: (kk, j))
        dot = _dot
    elif dims == "nt":
        m, k = a.shape
        n = b.shape[0]
        a_spec = pl.BlockSpec((tm, tk), lambda i, j, kk: (i, kk))
        b_spec = pl.BlockSpec((tn, tk), lambda i, j, kk: (j, kk))
        dot = _dot_nt
    else:
        k, m = a.shape
        n = b.shape[1]
        a_spec = pl.BlockSpec((tk, tm), lambda i, j, kk: (kk, i))
        b_spec = pl.BlockSpec((tk, tn), lambda i, j, kk: (kk, j))
        dot = _dot_tn
    assert m % tm == 0 and n % tn == 0 and k % tk == 0, (name, m, n, k)
    grid = (m // tm, n // tn, k // tk)
    nk = grid[2]
    assert nk == 1 or out_dtype == F32, name

    def body(a_ref, b_ref, o_ref):
        prod = dot(a_ref[...].astype(BF16), b_ref[...].astype(BF16))
        if nk == 1:
            o_ref[...] = prod.astype(out_dtype)
        else:
            kk = pl.program_id(2)

            @pl.when(kk == 0)
            def _():
                o_ref[...] = prod

            @pl.when(kk > 0)
            def _():
                o_ref[...] += prod

    return pl.pallas_call(
        body, grid=grid, in_specs=[a_spec, b_spec],
        out_specs=pl.BlockSpec((tm, tn), lambda i, j, kk: (i, j)),
        out_shape=jax.ShapeDtypeStruct((m, n), out_dtype),
        compiler_params=_params(("parallel", "parallel", "arbitrary")), name=name,
    )(a, b)


def _mm_pair_nt(a, w1_t, w2_t, shard, *, tm, tn, out_dtype, name):
    m, k = a.shape
    n = w1_t.shape[0]
    assert m % tm == 0 and n % tn == 0 and w2_t.shape == w1_t.shape, name
    grid = (m // tm, n // tn)
    n_steps = grid[0] * grid[1]

    def body(a_ref, w1_ref, w2_ref, shard_ref, o1_ref, o2_ref, gat_ref, send_sems, recv_sems):
        step = pl.program_id(0) * grid[1] + pl.program_id(1)
        gather = _GatherPlan([shard.shape], [shard_ref], [gat_ref], send_sems, recv_sems)

        @pl.when(step == 0)
        def _():
            gather.start([0])

        @pl.when(step == (2 * n_steps) // 3)
        def _():
            gather.forward([0])

        av = a_ref[...]
        o1_ref[...] = _dot_nt(av, w1_ref[...]).astype(out_dtype)
        o2_ref[...] = _dot_nt(av, w2_ref[...]).astype(out_dtype)

        @pl.when(step == n_steps - 1)
        def _():
            gather.finish([0])

    w_spec = pl.BlockSpec((tn, k), lambda i, j: (j, 0))
    o_spec = pl.BlockSpec((tm, tn), lambda i, j: (i, j))
    return pl.pallas_call(
        body, grid=grid,
        in_specs=[pl.BlockSpec((tm, k), lambda i, j: (i, 0)), w_spec, w_spec, ANY],
        out_specs=[o_spec, o_spec, ANY],
        out_shape=[jax.ShapeDtypeStruct((m, n), out_dtype)] * 2
        + [jax.ShapeDtypeStruct((N_SHARD,) + shard.shape, shard.dtype)],
        scratch_shapes=_sem_pair(6),
        compiler_params=_params(("arbitrary", "arbitrary")), name=name,
    )(a, w1_t, w2_t, shard)


def _fused_rows(name, tm, mats, rows, vecs, fn, row_outs, acc_outs, exchange=()):
    m = mats[0][0].shape[0]
    nm, nr, nv, nro, nao, nx = len(mats), len(rows), len(vecs), len(row_outs), len(acc_outs), len(exchange)
    n_steps = m // tm

    def body(*refs):
        a_refs, w_refs = refs[:nm], refs[nm:2 * nm]
        pos = 2 * nm
        row_refs, vec_refs, part_refs = refs[pos:pos + nr], refs[pos + nr:pos + nr + nv], refs[pos + nr + nv:pos + nr + nv + nx]
        pos += nr + nv + nx
        out_refs, acc_refs, recv_refs = refs[pos:pos + nro], refs[pos + nro:pos + nro + nao], refs[pos + nro + nao:pos + nro + nao + nx]
        sems = refs[pos + nro + nao + nx:]
        i = pl.program_id(0)
        if nx:
            plan = _ChipExchangePlan(part_refs, recv_refs, *sems)

            @pl.when(i == 0)
            def _():
                plan.start()

        @pl.when(i == 0)
        def _():
            for r in acc_refs:
                r[...] = jnp.zeros_like(r)

        y = None
        for a_ref, w_ref, (_, _, dims, sl) in zip(a_refs, w_refs, mats):
            w = w_ref[...] if sl is None else w_ref[sl, :]
            part = (_dot if dims == "nn" else _dot_nt)(a_ref[...], w)
            y = part if y is None else y + part
        res = fn(y, *[r[...] for r in row_refs], *[v[...] for v in vec_refs])
        for r, val in zip(out_refs, res[:nro]):
            r[...] = val.astype(r.dtype)
        for r, val in zip(acc_refs, res[nro:]):
            r[...] += val

        if nx:
            @pl.when(i == n_steps - 1)
            def _():
                plan.finish()

    tile = lambda width: pl.BlockSpec((tm, width), lambda i: (i, 0))
    res = pl.pallas_call(
        body, grid=(n_steps,),
        in_specs=[tile(a.shape[1]) for a, _, _, _ in mats] + [_full_spec(w.shape) for _, w, _, _ in mats]
        + [tile(D_MODEL)] * nr + [_full_spec((1, D_MODEL))] * nv + [ANY] * nx,
        out_specs=[tile(D_MODEL)] * nro + [_full_spec(s) for s in acc_outs] + [ANY] * nx,
        out_shape=[jax.ShapeDtypeStruct((m, D_MODEL), dt) for dt in row_outs]
        + [jax.ShapeDtypeStruct(s, F32) for s in acc_outs]
        + [jax.ShapeDtypeStruct((3,) + p.shape[1:], p.dtype) for p in exchange],
        scratch_shapes=_sem_pair(3 * nx) if nx else [],
        compiler_params=_params(("arbitrary",)), name=name,
    )(*[a for a, _, _, _ in mats], *[w for _, w, _, _ in mats], *rows, *vecs, *exchange)
    return list(res[:nro + nao]), list(res[nro + nao:])


ROW_TILE = 512


def _vec_spec(width=D_MODEL):
    return pl.BlockSpec((1, width), lambda i: (0, 0))


def _rstd(v):
    return lax.rsqrt(jnp.mean(v * v, axis=-1, keepdims=True) + NORM_EPS)


def _mid_fwd_rows(y1, x0, g2, g3):
    x1 = x0 + y1 * _rstd(y1) * g2
    return y1, x1, x1 * _rstd(x1) * g3


def _rms_bwd_rows(dout, v, g):
    r = _rstd(v)
    n = v * r
    dn = dout * g
    dv = r * (dn - n * jnp.mean(dn * n, axis=-1, keepdims=True))
    dg = jnp.sum(dout * n, axis=0, keepdims=True)
    return dv, dg


def _loss_head_rows(y2, x1, tgt, g4):
    x2 = x1 + y2 * _rstd(y2) * g4
    err = x2 - tgt
    loss = 0.5 * jnp.sum(jnp.mean(err * err, axis=-1, keepdims=True), axis=0, keepdims=True)
    dx2 = err * (1.0 / D_MODEL)
    dy2, dg4 = _rms_bwd_rows(dx2, y2, g4)
    return dx2, dy2, dg4, loss


def _mid_bwd_rows(dh2, x1, y1, dx2, g2, g3):
    d3, dg3 = _rms_bwd_rows(dh2, x1, g3)
    dx1 = dx2 + d3
    dy1, dg2 = _rms_bwd_rows(dx1, y1, g2)
    return dx1, dy1, dg2, dg3


def _in_bwd_rows(dh1, x0, dx1, g1):
    d1, dg1 = _rms_bwd_rows(dh1, x0, g1)
    return dx1 + d1, dg1


GATE_ROWS = 512


def _group_mean_matrix():
    p = np.zeros((A_WIDTH, A_WIDTH), np.float32)
    for g in range(A_GROUPS):
        p[g * HEAD_DIM:(g + 1) * HEAD_DIM, g * HEAD_DIM:(g + 1) * HEAD_DIM] = 1.0 / HEAD_DIM
    return jnp.asarray(p)


def _group_masks(width=A_WIDTH):
    lane = lax.broadcasted_iota(jnp.int32, (1, width), 1)
    return [(lane >= g * HEAD_DIM) & (lane < (g + 1) * HEAD_DIM) for g in range(width // HEAD_DIM)]


GROUP_SUM_PRECISION = lax.Precision.HIGH


def _layernorm_groups(vg, pavg):
    hi = GROUP_SUM_PRECISION
    mu = _dot(vg, pavg, hi)
    xc = vg - mu
    var = _dot(xc * xc, pavg, hi)
    rstd = lax.rsqrt(var + NORM_EPS)
    return xc * rstd, rstd


def _spatial_mix(w_bf, vn_chunk_bf, masks, bz):
    z = bz
    for g in range(A_GROUPS):
        z = z + jnp.where(masks[g], _dot(w_bf[g], vn_chunk_bf), 0.0)
    return z


def _full_spec(shape):
    return pl.BlockSpec(shape, lambda i: tuple(0 for _ in shape))


def _gate_fwd(uv, ln_g, ln_b, w_s, bz):
    m = uv.shape[0]
    pavg = _group_mean_matrix()

    def body(u_ref, v_ref, lg_ref, lb_ref, w_ref, bz_ref, p_ref, a_ref):
        masks = _group_masks()
        row = lax.broadcasted_iota(jnp.int32, (CHUNK, CHUNK), 0)
        col = lax.broadcasted_iota(jnp.int32, (CHUNK, CHUNK), 1)
        w_bf = [jnp.where(row >= col, w_ref[g], 0.0).astype(BF16) for g in range(A_GROUPS)]
        ug = _gelu(u_ref[...])
        vhat, _ = _layernorm_groups(_gelu(v_ref[...]), p_ref[...])
        vn = vhat * lg_ref[...] + lb_ref[...]
        bz = bz_ref[...]
        for c in range(GATE_ROWS // CHUNK):
            sl = slice(c * CHUNK, (c + 1) * CHUNK)
            z = _spatial_mix(w_bf, vn[sl].astype(BF16), masks, bz)
            a_ref[sl, :] = (ug[sl] * z).astype(BF16)

    return pl.pallas_call(
        body, grid=(m // GATE_ROWS,),
        in_specs=[pl.BlockSpec((GATE_ROWS, A_WIDTH), lambda i: (i, 0)),
                  pl.BlockSpec((GATE_ROWS, A_WIDTH), lambda i: (i, 1)),
                  _full_spec((1, A_WIDTH)), _full_spec((1, A_WIDTH)), _full_spec((A_GROUPS, CHUNK, CHUNK)),
                  _full_spec((CHUNK, A_WIDTH)), _full_spec((A_WIDTH, A_WIDTH))],
        out_specs=pl.BlockSpec((GATE_ROWS, A_WIDTH), lambda i: (i, 0)),
        out_shape=jax.ShapeDtypeStruct((m, A_WIDTH), BF16),
        compiler_params=_params(("parallel",)), name="gate_fwd",
    )(uv, uv, ln_g, ln_b, w_s, bz, pavg)


def _gate_bwd(uv, dmix, ln_g, ln_b, w_s, w_st, bz, grads):
    m = uv.shape[0]
    pavg = _group_mean_matrix()
    nsteps = m // GATE_ROWS
    nx = len(grads)
    shapes = [g.shape for g in grads]

    def body(u_ref, v_ref, da_ref, lg_ref, lb_ref, w_ref, wt_ref, bz_ref, p_ref, *rest):
        grad_refs = rest[:nx]
        duv_ref, dlg_ref, dlb_ref, dw_ref, dbz_ref = rest[nx:nx + 5]
        recv_refs = rest[nx + 5:2 * nx + 5]
        exchange = _SiblingExchangePlan(shapes, grad_refs, recv_refs, *rest[2 * nx + 5:])
        i = pl.program_id(0)

        @pl.when(i == 0)
        def _():
            exchange.start()
            dlg_ref[...] = jnp.zeros_like(dlg_ref)
            dlb_ref[...] = jnp.zeros_like(dlb_ref)
            dw_ref[...] = jnp.zeros_like(dw_ref)
            dbz_ref[...] = jnp.zeros_like(dbz_ref)

        hi = GROUP_SUM_PRECISION
        masks = _group_masks()
        row = lax.broadcasted_iota(jnp.int32, (CHUNK, CHUNK), 0)
        col = lax.broadcasted_iota(jnp.int32, (CHUNK, CHUNK), 1)
        tril = row >= col
        w_bf = [jnp.where(tril, w_ref[g], 0.0).astype(BF16) for g in range(A_GROUPS)]
        wt_bf = [jnp.where(col >= row, wt_ref[g], 0.0).astype(BF16) for g in range(A_GROUPS)]
        pavg_v = p_ref[...]
        lg = lg_ref[...]
        ug, dug = _gelu_and_grad(u_ref[...])
        vg, dvg_dx = _gelu_and_grad(v_ref[...])
        vhat, rstd = _layernorm_groups(vg, pavg_v)
        vn = vhat * lg + lb_ref[...]
        da = da_ref[...]
        bz = bz_ref[...]
        for c in range(GATE_ROWS // CHUNK):
            sl = slice(c * CHUNK, (c + 1) * CHUNK)
            vn_bf = vn[sl].astype(BF16)
            z = _spatial_mix(w_bf, vn_bf, masks, bz)
            dz = da[sl] * ug[sl]
            duv_ref[sl, 0:A_WIDTH] = (da[sl] * z * dug[sl]).astype(BF16)
            dbz_ref[...] += dz
            dz_bf = dz.astype(BF16)
            dvn = jnp.zeros((CHUNK, A_WIDTH), F32)
            for g in range(A_GROUPS):
                dz_g = jnp.where(masks[g], dz, 0.0).astype(BF16)
                dw_ref[g] += jnp.where(tril, _dot_nt(dz_g, vn_bf), 0.0)
                dvn = dvn + jnp.where(masks[g], _dot(wt_bf[g], dz_bf), 0.0)
            vh = vhat[sl]
            dlb_ref[...] += jnp.sum(dvn, axis=0, keepdims=True)
            dlg_ref[...] += jnp.sum(dvn * vh, axis=0, keepdims=True)
            dvh = dvn * lg
            m1 = _dot(dvh, pavg_v, hi)
            m2 = _dot(dvh * vh, pavg_v, hi)
            duv_ref[sl, A_WIDTH:2 * A_WIDTH] = (rstd[sl] * (dvh - m1 - vh * m2) * dvg_dx[sl]).astype(BF16)

        @pl.when(i == nsteps - 1)
        def _():
            dbz_ref[...] = _dot(dbz_ref[...], pavg_v * float(HEAD_DIM), hi)
            exchange.finish()

    res = pl.pallas_call(
        body, grid=(nsteps,),
        in_specs=[pl.BlockSpec((GATE_ROWS, A_WIDTH), lambda i: (i, 0)),
                  pl.BlockSpec((GATE_ROWS, A_WIDTH), lambda i: (i, 1)),
                  pl.BlockSpec((GATE_ROWS, A_WIDTH), lambda i: (i, 0)),
                  _full_spec((1, A_WIDTH)), _full_spec((1, A_WIDTH)), _full_spec((A_GROUPS, CHUNK, CHUNK)),
                  _full_spec((A_GROUPS, CHUNK, CHUNK)), _full_spec((CHUNK, A_WIDTH)),
                  _full_spec((A_WIDTH, A_WIDTH))] + [ANY] * nx,
        out_specs=[pl.BlockSpec((GATE_ROWS, 2 * A_WIDTH), lambda i: (i, 0)),
                   _full_spec((1, A_WIDTH)), _full_spec((1, A_WIDTH)), _full_spec((A_GROUPS, CHUNK, CHUNK)),
                   _full_spec((CHUNK, A_WIDTH))] + [ANY] * nx,
        out_shape=[jax.ShapeDtypeStruct((m, IN_COLS), BF16),
                   jax.ShapeDtypeStruct((1, A_WIDTH), F32), jax.ShapeDtypeStruct((1, A_WIDTH), F32),
                   jax.ShapeDtypeStruct((A_GROUPS, CHUNK, CHUNK), F32),
                   jax.ShapeDtypeStruct((CHUNK, A_WIDTH), F32)]
        + [jax.ShapeDtypeStruct((N_SHARD, s[1] // 2, s[2]), F32) for s in shapes],
        scratch_shapes=_sem_pair(nx),
        compiler_params=_params(("arbitrary",)), name="gate_bwd",
    )(uv, uv, dmix, ln_g, ln_b, w_s, w_st, bz, pavg, *grads)
    return res[:5], list(res[5:])


Q_BLOCK = 128
PAIR = 2 * HEAD_DIM
N_PAIR = B_HEADS // 2
N_CFG = len(DILATED)
BLOCKS_PER_CFG = SEQ // Q_BLOCK
QKV_SLABS = 3 * N_PAIR
FWD_BLOCKS_PER_TRIP = 8
BWD_BLOCKS_PER_TRIP = 4


def _t5_bucket_np(dist, dtype):
    max_exact = NUM_BUCKETS // 2
    d = np.maximum(dist, 1).astype(dtype)
    large = max_exact + (np.log(d / dtype(max_exact)) / dtype(math.log(MAX_DISTANCE / max_exact))
                         * dtype(NUM_BUCKETS - max_exact))
    large = np.minimum(large.astype(np.int32), NUM_BUCKETS - 1)
    return np.where(dist < max_exact, dist, large)


def _bucket_tables():
    i = np.arange(Q_BLOCK)[:, None]
    j = np.arange(Q_BLOCK)[None, :]
    tables = []
    for _, dil in DILATED:
        rel_prev = Q_BLOCK + i - j
        rel_cur = i - j
        rel = np.concatenate([rel_prev, rel_cur], axis=1)
        valid = np.concatenate([rel_prev <= Q_BLOCK, rel_cur >= 0], axis=1)
        dist = np.maximum(rel, 0) * dil
        b32 = _t5_bucket_np(dist, np.float32)
        b64 = _t5_bucket_np(dist, np.float64)
        assert np.array_equal(b32, b64)
        tables.append(np.where(valid, b32, -1).astype(np.int32))
    return np.stack(tables)


def _present_buckets(buckets_np):
    return [sorted(set(int(v) for v in np.unique(buckets_np[c]) if v >= 0)) for c in range(N_CFG)]


def _bias_tables_body(buckets_np):
    present = _present_buckets(buckets_np)

    def tables(rb_ref, bk_ref, o_ref, ot_ref):
        for c in range(N_CFG):
            bk = bk_ref[c]
            for h in range(B_HEADS):
                acc = jnp.full((Q_BLOCK, 2 * Q_BLOCK), NEG_INF, F32)
                for b in present[c]:
                    acc = jnp.where(bk == b, rb_ref[h, b], acc)
                o_ref[c, h] = acc
                ot_ref[c, h] = acc.T

    return tables


def _proj_fwd(x, g1, w_in_t):
    m = x.shape[0]
    tm = ROW_TILE

    def body(x_ref, g_ref, w_ref, h_ref, uv_ref, qkv_ref):
        xv = x_ref[...]
        h = (xv * _rstd(xv) * g_ref[...]).astype(BF16)
        h_ref[...] = h
        acc = _dot_nt(h, w_ref[...])
        uv_ref[...] = acc[:, :2 * A_WIDTH]
        for s in range(QKV_SLABS):
            qkv_ref[s] = acc[:, 2 * A_WIDTH + s * PAIR:2 * A_WIDTH + (s + 1) * PAIR]

    return pl.pallas_call(
        body, grid=(m // tm,),
        in_specs=[pl.BlockSpec((tm, D_MODEL), lambda i: (i, 0)), _vec_spec(),
                  pl.BlockSpec((IN_COLS, D_MODEL), lambda i: (0, 0))],
        out_specs=[pl.BlockSpec((tm, D_MODEL), lambda i: (i, 0)),
                   pl.BlockSpec((tm, 2 * A_WIDTH), lambda i: (i, 0)),
                   pl.BlockSpec((QKV_SLABS, tm, PAIR), lambda i: (0, i, 0))],
        out_shape=[jax.ShapeDtypeStruct((m, D_MODEL), BF16), jax.ShapeDtypeStruct((m, 2 * A_WIDTH), F32),
                   jax.ShapeDtypeStruct((QKV_SLABS, m, PAIR), F32)],
        compiler_params=_params(("parallel",)), name="proj_fwd",
    )(x, g1, w_in_t)


def _pair_masks():
    lane = lax.broadcasted_iota(jnp.int32, (1, PAIR), 1)
    return [lane < HEAD_DIM, lane >= HEAD_DIM]


def _block_rows(idx, dil):
    static = isinstance(idx, int)
    r, n = idx % dil, idx // dil

    def rows_of(block):
        start = r + (dil * Q_BLOCK) * block
        if dil == 1:
            return pl.ds(start if static else pl.multiple_of(start, Q_BLOCK), Q_BLOCK)
        return pl.ds(start, Q_BLOCK, stride=dil)

    prev = rows_of(n - 1) if not static or n > 0 else None
    return rows_of(n), prev


def _attn_fwd(qkv, bias, batch, shards):
    m = qkv.shape[1]
    comb_rows = 256
    nt = len(shards)
    shapes = [sh.shape for sh in shards]
    n_steps = batch * N_PAIR
    early, late = list(range(nt // 2)), list(range(nt // 2, nt))

    def body(q_ref, k_ref, v_ref, b_ref, *rest):
        shard_refs = rest[:nt]
        o_ref, l_ref = rest[nt:nt + 2]
        gat_refs = rest[nt + 2:2 * nt + 2]
        scratch = rest[2 * nt + 2:]
        oc_refs, lc_refs = scratch[:N_CFG], scratch[N_CFG:2 * N_CFG]
        step = pl.program_id(0) * N_PAIR + pl.program_id(1)
        gather = _GatherPlan(shapes, shard_refs, gat_refs, *scratch[2 * N_CFG:])

        @pl.when(step == 0)
        def _():
            gather.start(early + late)

        @pl.when(step == n_steps // 2)
        def _():
            gather.forward(early)

        @pl.when(step == n_steps - 2)
        def _():
            gather.forward(late)

        masks = _pair_masks()
        for ci, (_, dil) in enumerate(DILATED):
            nb = SEQ // dil // Q_BLOCK

            def block(trip, ci=ci, dil=dil, nb=nb):
                work = []
                for u in range(FWD_BLOCKS_PER_TRIP):
                    rows, prow = _block_rows(trip * FWD_BLOCKS_PER_TRIP + u, dil)
                    has_prev = nb > 1 and prow is not None
                    q = q_ref[rows, :] * 0.125
                    kc = k_ref[rows, :].astype(BF16)
                    vc = v_ref[rows, :]
                    kp = k_ref[prow, :].astype(BF16) if has_prev else None
                    vp = v_ref[prow, :] if has_prev else None
                    tiles = []
                    for h in range(2):
                        qh = jnp.where(masks[h], q, 0.0).astype(BF16)
                        sc = _dot_nt(qh, kc) + b_ref[ci, h, :, Q_BLOCK:]
                        sp = _dot_nt(qh, kp) + b_ref[ci, h, :, :Q_BLOCK] if has_prev else None
                        tiles.append((sc, sp))
                    work.append((rows, vc, vp, tiles))
                probs = []
                for _, _, _, tiles in work:
                    ps = []
                    for sc, sp in tiles:
                        mx = jnp.max(sc if sp is None else jnp.maximum(sc, sp), axis=1, keepdims=True)
                        pc = jnp.exp(sc - mx).astype(BF16)
                        pp = None if sp is None else jnp.exp(sp - mx).astype(BF16)
                        ps.append((mx, pc, pp))
                    probs.append(ps)
                for (rows, vc, vp, _), ps in zip(work, probs):
                    res = []
                    for h, (_, pc, pp) in enumerate(ps):
                        r = _dot(pc, jnp.where(masks[h], vc, 1.0).astype(BF16))
                        if pp is not None:
                            r = r + _dot(pp, jnp.where(masks[h], vp, 1.0).astype(BF16))
                        res.append(r)
                    num = jnp.where(masks[0], res[0], res[1])
                    den = pltpu.roll(jnp.where(masks[0], res[1], res[0]), HEAD_DIM, 1)
                    oc_refs[ci][rows, :] = num / den
                    lc_refs[ci][rows, :] = jnp.where(masks[0], ps[0][0], ps[1][0]) + jnp.log(den)

            for trip in range(BLOCKS_PER_CFG // FWD_BLOCKS_PER_TRIP):
                block(trip)

        def combine(i, carry):
            rr = pl.ds(pl.multiple_of(i * comb_rows, comb_rows), comb_rows)
            ls = [lc_refs[c][rr, :] for c in range(N_CFG)]
            mx = functools.reduce(jnp.maximum, ls)
            ws = [jnp.exp(l - mx) for l in ls]
            tot = functools.reduce(lambda a, b: a + b, ws)
            o = functools.reduce(lambda a, b: a + b, [ws[c] * oc_refs[c][rr, :] for c in range(N_CFG)]) / tot
            o_ref[rr, :] = o.astype(BF16)
            l_ref[rr, :] = mx + jnp.log(tot)
            return carry

        lax.fori_loop(0, SEQ // comb_rows, combine, 0)

        @pl.when(step == n_steps - 1)
        def _():
            gather.finish(early + late)

    def slab(first):
        return pl.BlockSpec((None, SEQ, PAIR), lambda b, p: (first + p, b, 0))

    nat = pl.BlockSpec((SEQ, PAIR), lambda b, p: (b, p))
    res = pl.pallas_call(
        body, grid=(batch, N_PAIR),
        in_specs=[slab(0), slab(N_PAIR), slab(2 * N_PAIR),
                  pl.BlockSpec((N_CFG, 2, Q_BLOCK, 2 * Q_BLOCK), lambda b, p: (0, p, 0, 0))] + [ANY] * nt,
        out_specs=[nat, nat] + [ANY] * nt,
        out_shape=[jax.ShapeDtypeStruct((m, B_WIDTH), BF16), jax.ShapeDtypeStruct((m, B_WIDTH), F32)]
        + [jax.ShapeDtypeStruct((N_SHARD,) + sh.shape, sh.dtype) for sh in shards],
        scratch_shapes=[pltpu.VMEM((SEQ, PAIR), F32)] * (2 * N_CFG) + _sem_pair(6 * nt),
        compiler_params=_params(("arbitrary", "arbitrary")), name="attn_fwd",
    )(qkv, qkv, qkv, bias, *shards)
    return res[0], res[1], list(res[2:])


def _attn_bwd(qkv, dmix, o, lse, bias_t, dproj, batch, parts, smalls):
    m = qkv.shape[1]
    nt, ns = len(parts), len(smalls)
    n_steps = N_PAIR * batch

    def body(q_ref, k_ref, v_ref, do_ref, o_ref, l_ref, b_ref, *rest):
        part_refs = rest[1:nt + 1]
        small_refs = rest[nt + 1:nt + 1 + ns]
        pos = nt + 1 + ns
        dproj_ref, ds_ref = rest[pos:pos + 2]
        recv_refs = rest[pos + 2:pos + 2 + nt]
        sum_refs = rest[pos + 2 + nt:pos + 2 + nt + ns]
        pos += 2 + nt + ns
        dq_acc, dk_acc, dv_acc, d_scr, stage, stage_sems, send_sems, recv_sems = rest[pos:pos + 8]
        allreduce = _SmallAllReducePlan(small_refs, sum_refs, rest[pos + 8:pos + 8 + ns],
                                        rest[pos + 8 + ns:pos + 8 + 2 * ns], *rest[pos + 8 + 2 * ns:])
        pair, seq = pl.program_id(0), pl.program_id(1)
        step = pair * batch + seq
        exchange = _ChipExchangePlan(part_refs, recv_refs, send_sems, recv_sems)

        @pl.when(step == 0)
        def _():
            allreduce.start_sibling()

        @pl.when(step == n_steps // 2)
        def _():
            allreduce.sum_sibling_and_start_chips()

        def stage_copies():
            rows = pl.ds(pl.multiple_of(seq * SEQ, SEQ), SEQ)
            return [pltpu.make_async_copy(
                stage.at[k],
                dproj_ref.at[rows, pl.ds(pl.multiple_of(2 * A_WIDTH + k * B_WIDTH + pair * PAIR, PAIR), PAIR)],
                stage_sems.at[k]) for k in range(3)]

        @pl.when(step == 0)
        def _():
            exchange.start()

        @pl.when(pl.program_id(1) == 0)
        def _():
            ds_ref[...] = jnp.zeros_like(ds_ref)

        dq_acc[...] = jnp.zeros_like(dq_acc)
        dk_acc[...] = jnp.zeros_like(dk_acc)
        dv_acc[...] = jnp.zeros_like(dv_acc)
        d_scr[...] = do_ref[...] * o_ref[...].astype(F32)
        masks = _pair_masks()

        def stack_heads(t):
            return jnp.concatenate([jnp.where(masks[0], t, 0.0), jnp.where(masks[1], t, 0.0)], axis=0).astype(BF16)

        for ci, (_, dil) in enumerate(DILATED):
            nb = SEQ // dil // Q_BLOCK

            def block(trip, carry, ci=ci, dil=dil, nb=nb):
                first = []
                for u in range(BWD_BLOCKS_PER_TRIP):
                    rows, prow = _block_rows(trip * BWD_BLOCKS_PER_TRIP + u, dil)
                    has_prev = nb > 1 and prow is not None
                    if has_prev:
                        kcat = jnp.concatenate([k_ref[prow, :], k_ref[rows, :]], axis=0).astype(BF16)
                        vcat = jnp.concatenate([v_ref[prow, :], v_ref[rows, :]], axis=0).astype(BF16)
                    else:
                        kcat = k_ref[rows, :].astype(BF16)
                        vcat = v_ref[rows, :].astype(BF16)
                    qst = stack_heads(q_ref[rows, :] * 0.125)
                    dost = stack_heads(do_ref[rows, :])
                    lt = l_ref[rows, :].T
                    dt = d_scr[rows, :].T
                    lrow = jnp.concatenate([lt[0:1], lt[HEAD_DIM:HEAD_DIM + 1]], axis=1)
                    drow = jnp.concatenate([jnp.sum(dt[:HEAD_DIM], axis=0, keepdims=True),
                                            jnp.sum(dt[HEAD_DIM:], axis=0, keepdims=True)], axis=1)
                    first.append((has_prev, rows, prow, kcat, qst, dost, lrow, drow,
                                  _dot_nt(kcat, qst), _dot_nt(vcat, dost)))
                second = []
                for has_prev, rows, prow, kcat, qst, dost, lrow, drow, st, dpt in first:
                    keys = slice(0, 2 * Q_BLOCK) if has_prev else slice(Q_BLOCK, 2 * Q_BLOCK)
                    bt = jnp.concatenate([b_ref[ci, 0, keys, :], b_ref[ci, 1, keys, :]], axis=1)
                    pt = jnp.exp(st + bt - lrow)
                    dst = pt * (dpt - drow)
                    ds_ref[ci, 0, keys, :] += dst[:, :Q_BLOCK]
                    ds_ref[ci, 1, keys, :] += dst[:, Q_BLOCK:]
                    second.append((has_prev, rows, prow, kcat, qst, dost, pt.astype(BF16), dst.astype(BF16)))
                for has_prev, rows, prow, kcat, qst, dost, pt_bf, dst_bf in second:
                    dk = _dot(dst_bf, qst)
                    dv = _dot(pt_bf, dost)
                    dq2 = _dot_tn(dst_bf, kcat)
                    dq_acc[rows, :] += jnp.where(masks[0], dq2[:Q_BLOCK], dq2[Q_BLOCK:]) * 0.125
                    if has_prev:
                        dk_acc[prow, :] += dk[:Q_BLOCK]
                        dv_acc[prow, :] += dv[:Q_BLOCK]
                        dk_acc[rows, :] += dk[Q_BLOCK:]
                        dv_acc[rows, :] += dv[Q_BLOCK:]
                    else:
                        dk_acc[rows, :] += dk
                        dv_acc[rows, :] += dv
                return carry

            for trip in range(BLOCKS_PER_CFG // BWD_BLOCKS_PER_TRIP):
                block(trip, 0)

        @pl.when(step > 0)
        def _():
            for cp in stage_copies():
                cp.wait()

        stage[0] = dq_acc[...].astype(BF16)
        stage[1] = dk_acc[...].astype(BF16)
        stage[2] = dv_acc[...].astype(BF16)
        for cp in stage_copies():
            cp.start()

        @pl.when(step == n_steps - 1)
        def _():
            for cp in stage_copies():
                cp.wait()
            exchange.finish()
            allreduce.finish()

    def slab(first):
        return pl.BlockSpec((None, SEQ, PAIR), lambda p, b: (first + p, b, 0))

    nat = pl.BlockSpec((SEQ, PAIR), lambda p, b: (b, p))
    tbl = pl.BlockSpec((N_CFG, 2, 2 * Q_BLOCK, Q_BLOCK), lambda p, b: (0, p, 0, 0))
    acc = pltpu.VMEM((SEQ, PAIR), F32)
    vm = pl.BlockSpec(memory_space=pltpu.VMEM)
    res = pl.pallas_call(
        body, grid=(N_PAIR, batch),
        in_specs=[slab(0), slab(N_PAIR), slab(2 * N_PAIR),
                  pl.BlockSpec((SEQ, PAIR), lambda p, b: (b, A_WIDTH // PAIR + p)), nat, nat, tbl]
        + [ANY] * (nt + 1) + [vm] * ns,
        out_specs=[ANY, tbl] + [ANY] * nt + [vm] * ns,
        out_shape=[jax.ShapeDtypeStruct(dproj.shape, dproj.dtype),
                   jax.ShapeDtypeStruct((N_CFG, B_HEADS, 2 * Q_BLOCK, Q_BLOCK), F32)]
        + [jax.ShapeDtypeStruct((3,) + p.shape[1:], p.dtype) for p in parts]
        + [jax.ShapeDtypeStruct(a.shape, F32) for a in smalls],
        input_output_aliases={7: 0},
        scratch_shapes=[acc, acc, acc, acc, pltpu.VMEM((3, SEQ, PAIR), BF16), pltpu.SemaphoreType.DMA((3,))]
        + _sem_pair(3 * nt) + _SmallAllReducePlan.scratch(smalls),
        compiler_params=_params(("arbitrary", "arbitrary")), name="attn_bwd",
    )(qkv, qkv, qkv, dmix, o, lse, bias_t, dproj, *parts, *smalls)
    return res[0], res[1], list(res[2:2 + nt]), list(res[2 + nt:])


def _rel_bias_grad(ds, buckets_np, grads):
    present = _present_buckets(buckets_np)
    nx = len(grads)
    shapes = [g.shape for g in grads]

    def body(bk_ref, ds_ref, *rest):
        o_ref = rest[nx]
        acc_ref = rest[2 * nx + 1]
        exchange = _SiblingExchangePlan(shapes, rest[:nx], rest[nx + 1:2 * nx + 1], *rest[2 * nx + 2:])
        exchange.start()
        acc_ref[...] = jnp.zeros_like(acc_ref)
        for c in range(N_CFG):
            bk = bk_ref[c]
            for h in range(B_HEADS):
                dsv = ds_ref[c, h]
                for b in present[c]:
                    part = jnp.sum(jnp.where(bk == b, dsv, 0.0), axis=0, keepdims=True)
                    acc_ref[pl.ds(h * NUM_BUCKETS + b, 1), :] += part
        o_ref[...] = jnp.sum(acc_ref[...], axis=1, keepdims=True)
        exchange.finish()

    vm = pl.BlockSpec(memory_space=pltpu.VMEM)
    res = pl.pallas_call(
        body, in_specs=[vm, vm] + [ANY] * nx, out_specs=[vm] + [ANY] * nx,
        out_shape=[jax.ShapeDtypeStruct((B_HEADS * NUM_BUCKETS, 1), F32)]
        + [jax.ShapeDtypeStruct((N_SHARD, s[1] // 2, s[2]), F32) for s in shapes],
        scratch_shapes=[pltpu.VMEM((B_HEADS * NUM_BUCKETS, buckets_np.shape[-1]), F32)] + _sem_pair(nx),
        compiler_params=_params(), name="rel_bias_grad",
    )(jnp.asarray(buckets_np), ds, *grads)
    return res[0], list(res[1:])


def _row_index():
    return lax.broadcasted_iota(jnp.int32, (SEQ, LANE_BLOCK), 0)


def _shift_down(x, k, row):
    return jnp.where(row >= k, pltpu.roll(x, k, 0), 0.0)


def _shift_up(x, k, row):
    return jnp.where(row < SEQ - k, pltpu.roll(x, SEQ - k, 0), 0.0)


def _convgate_fwd(gate, up, conv_w, conv_b, batch):
    m = gate.shape[0]

    def body(g_ref, u_ref, w_ref, b_ref, a_ref):
        g = g_ref[...].astype(F32)
        w = w_ref[...]
        row = _row_index()
        c = b_ref[...] + w[0:1] * _shift_down(g, 2, row) + w[1:2] * _shift_down(g, 1, row) + w[2:3] * g
        a_ref[...] = (_gelu(c) * u_ref[...].astype(F32)).astype(BF16)

    blk = pl.BlockSpec((SEQ, LANE_BLOCK), lambda b, j: (b, j))
    return pl.pallas_call(
        body, grid=(batch, D_FF // LANE_BLOCK),
        in_specs=[blk, blk, pl.BlockSpec((3, LANE_BLOCK), lambda b, j: (0, j)),
                  pl.BlockSpec((1, LANE_BLOCK), lambda b, j: (0, j))],
        out_specs=blk,
        out_shape=jax.ShapeDtypeStruct((m, D_FF), BF16),
        compiler_params=_params(("parallel", "parallel")), name="convgate_fwd",
    )(gate, up, conv_w, conv_b)


def _convgate_bwd(gate, up, dact, conv_w, conv_b, batch):
    m = gate.shape[0]

    def body(g_ref, u_ref, da_ref, w_ref, b_ref, dg_ref, du_ref, dw_ref, db_ref):
        @pl.when(pl.program_id(1) == 0)
        def _():
            dw_ref[...] = jnp.zeros_like(dw_ref)
            db_ref[...] = jnp.zeros_like(db_ref)

        g = g_ref[...].astype(F32)
        w = w_ref[...]
        row = _row_index()
        g1 = _shift_down(g, 1, row)
        g2 = _shift_down(g, 2, row)
        c = b_ref[...] + w[0:1] * g2 + w[1:2] * g1 + w[2:3] * g
        gg, dgg = _gelu_and_grad(c)
        da = da_ref[...].astype(F32)
        du_ref[...] = (da * gg).astype(BF16)
        dc = da * u_ref[...].astype(F32) * dgg
        db_ref[...] += jnp.sum(dc, axis=0, keepdims=True)
        dw_ref[0:1, :] += jnp.sum(dc * g2, axis=0, keepdims=True)
        dw_ref[1:2, :] += jnp.sum(dc * g1, axis=0, keepdims=True)
        dw_ref[2:3, :] += jnp.sum(dc * g, axis=0, keepdims=True)
        dg_ref[...] = (w[2:3] * dc + w[1:2] * _shift_up(dc, 1, row) + w[0:1] * _shift_up(dc, 2, row)).astype(BF16)

    blk = pl.BlockSpec((SEQ, LANE_BLOCK), lambda j, b: (b, j))
    wspec = pl.BlockSpec((3, LANE_BLOCK), lambda j, b: (0, j))
    bspec = pl.BlockSpec((1, LANE_BLOCK), lambda j, b: (0, j))
    return pl.pallas_call(
        body, grid=(D_FF // LANE_BLOCK, batch),
        in_specs=[blk, blk, blk, wspec, bspec],
        out_specs=[blk, blk, wspec, bspec],
        out_shape=[jax.ShapeDtypeStruct((m, D_FF), BF16), jax.ShapeDtypeStruct((m, D_FF), BF16),
                   jax.ShapeDtypeStruct((3, D_FF), F32), jax.ShapeDtypeStruct((1, D_FF), F32)],
        compiler_params=_params(("parallel", "arbitrary")), name="convgate_bwd",
    )(gate, up, dact, conv_w, conv_b)


def _gather_weights(shards, conv_w_shard, rel_bias, buckets_np):
    nt = len(shards)
    shapes = [sh.shape for sh in shards]
    ts = list(range(nt))
    tables = _bias_tables_body(buckets_np)

    def body(*refs):
        shard_refs = refs[:nt]
        cw_ref, rb_ref, bk_ref = refs[nt:nt + 3]
        out_refs = refs[nt + 3:2 * nt + 3]
        cw_out, bias_ref, bias_t_ref = refs[2 * nt + 3:2 * nt + 6]
        send_sems, recv_sems, cw_send, cw_recv = refs[2 * nt + 6:]
        plan = _GatherPlan(shapes, shard_refs, out_refs, send_sems, recv_sems)
        x, y, c, chips = _mesh_pos()

        def cw_copy(j, src, dst, chip):
            return pltpu.make_async_remote_copy(src_ref=src, dst_ref=dst, send_sem=cw_send.at[j],
                                                recv_sem=cw_recv.at[j], device_id=(*chip, c), device_id_type=MESH)

        plan.start(ts)
        cw_sends = [cw_copy(j, cw_ref, cw_out.at[2 * x + y], chip) for j, chip in enumerate(chips)]
        for cp in cw_sends:
            cp.start()
        tables(rb_ref, bk_ref, bias_ref, bias_t_ref)
        plan.forward(ts)
        for j, chip in enumerate(chips):
            dst = cw_out.at[2 * chip[0] + chip[1]]
            cw_copy(j, dst, dst, chip).wait_recv()
        plan.finish(ts)
        for cp in cw_sends:
            cp.wait_send()

    out_shape = [jax.ShapeDtypeStruct((N_SHARD,) + sh.shape, sh.dtype) for sh in shards]
    out_shape.append(jax.ShapeDtypeStruct((N_SHARD,) + conv_w_shard.shape, conv_w_shard.dtype))
    out_shape += [jax.ShapeDtypeStruct((N_CFG, B_HEADS, Q_BLOCK, 2 * Q_BLOCK), F32),
                  jax.ShapeDtypeStruct((N_CFG, B_HEADS, 2 * Q_BLOCK, Q_BLOCK), F32)]
    vm = pl.BlockSpec(memory_space=pltpu.VMEM)
    res = pl.pallas_call(
        body, in_specs=[ANY] * (nt + 1) + [pl.BlockSpec(memory_space=pltpu.SMEM), vm],
        out_specs=[ANY] * (nt + 1) + [vm, vm], out_shape=out_shape,
        scratch_shapes=_sem_pair(6 * nt) + _sem_pair(3),
        compiler_params=pltpu.CompilerParams(has_side_effects=True, vmem_limit_bytes=VMEM_LIMIT),
        name="gather_weights",
    )(*shards, conv_w_shard, rel_bias.T, jnp.asarray(buckets_np))
    return list(res[:nt + 1]), res[nt + 1], res[nt + 2]


def _add_halves(g, recv, c_idx):
    _, rows2, cols = g.shape
    rows = rows2 // 2
    tr = rows
    nblk = rows // tr

    def body(c_ref, g_ref, r_ref, o_ref):
        o_ref[...] = (g_ref[...] + r_ref[...]).astype(BF16)

    return pl.pallas_call(
        body,
        grid_spec=pltpu.PrefetchScalarGridSpec(
            num_scalar_prefetch=1, grid=(N_SHARD, nblk),
            in_specs=[pl.BlockSpec((None, tr, cols), lambda s, i, c: (s, c[0] * nblk + i, 0)),
                      pl.BlockSpec((None, tr, cols), lambda s, i, c: (s, i, 0))],
            out_specs=pl.BlockSpec((None, tr, cols), lambda s, i, c: (s, i, 0))),
        out_shape=jax.ShapeDtypeStruct((N_SHARD, rows, cols), BF16),
        compiler_params=_params(("parallel", "parallel")), name="rs_add_halves",
    )(c_idx, g, recv)


def _add_chips(part, recv, s_idx, c_idx):
    _, rows, cols = part.shape
    tr = rows
    nblk = rows // tr

    def body(idx_ref, p_ref, r_ref, o_ref):
        acc = p_ref[...].astype(F32)
        for j in range(3):
            acc = acc + r_ref[j].astype(F32)
        o_ref[...] = acc

    return pl.pallas_call(
        body,
        grid_spec=pltpu.PrefetchScalarGridSpec(
            num_scalar_prefetch=1, grid=(nblk,),
            in_specs=[pl.BlockSpec((None, tr, cols), lambda i, idx: (idx[0], i, 0)),
                      pl.BlockSpec((3, tr, cols), lambda i, idx: (0, i, 0))],
            out_specs=pl.BlockSpec((tr, cols), lambda i, idx: (idx[1] * nblk + i, 0))),
        out_shape=jax.ShapeDtypeStruct((2 * rows, cols), F32),
        compiler_params=_params(("parallel",)), name="rs_add_chips",
    )(jnp.concatenate([s_idx, c_idx]), part, recv)


def _finish_reductions(fulls, arrays):
    nt, n = len(fulls), len(arrays)

    def body(*refs):
        in_refs = refs[nt:nt + n]
        full_refs, out_refs = refs[nt + n:2 * nt + n], refs[2 * nt + n:2 * nt + 2 * n]
        pos = 2 * nt + 2 * n
        share_send, share_recv = refs[pos + 2 * n:pos + 2 * n + 2]
        allreduce = _SmallAllReducePlan(in_refs, out_refs, refs[pos:pos + n], refs[pos + n:pos + 2 * n],
                                        *refs[pos + 2 * n + 2:])
        x, y, c, _ = _mesh_pos()

        def half(t, which):
            rows = fulls[t].shape[0] // 2
            return full_refs[t].at[pl.ds(which * rows, rows), :]

        def share(t, which):
            return pltpu.make_async_remote_copy(
                src_ref=half(t, which), dst_ref=half(t, which), send_sem=share_send.at[t],
                recv_sem=share_recv.at[t], device_id=(x, y, 1 - c), device_id_type=MESH)

        for t in range(nt):
            share(t, c).start()
        allreduce.start_sibling()
        allreduce.sum_sibling_and_start_chips()
        allreduce.finish()
        for t in range(nt):
            share(t, 1 - c).wait_recv()
        for t in range(nt):
            share(t, c).wait_send()

    vm = pl.BlockSpec(memory_space=pltpu.VMEM)
    res = pl.pallas_call(
        body, in_specs=[ANY] * nt + [vm] * n, out_specs=[ANY] * nt + [vm] * n,
        out_shape=[jax.ShapeDtypeStruct(f.shape, f.dtype) for f in fulls]
        + [jax.ShapeDtypeStruct(a.shape, F32) for a in arrays],
        input_output_aliases={t: t for t in range(nt)},
        scratch_shapes=[pltpu.VMEM(a.shape, F32) for a in arrays] + [pltpu.VMEM((3,) + a.shape, F32) for a in arrays]
        + _sem_pair(nt) + _sem_pair(4 * n),
        compiler_params=pltpu.CompilerParams(has_side_effects=True),
        name="finish_reductions",
    )(*fulls, *arrays)
    return list(res[:nt]), list(res[nt:])


def _from_col_shards(g):
    n, rows, cols = g.shape
    return g.transpose(1, 0, 2).reshape(rows, n * cols)


def _train_step(x, tgt, g1, g2, g3, g4, shards, ln_g, ln_b, w_s, b_s, rel_bias, conv_w_shard, conv_b, batch,
                s_idx, c_idx):
    big = dict(tm=1024, out_dtype=F32)
    buckets = _bucket_tables()
    bz = jnp.repeat(b_s.T, HEAD_DIM, axis=1)
    w_st = jnp.swapaxes(w_s, 1, 2)

    def with_own(gathered, own):
        return lax.dynamic_update_index_in_dim(gathered, own, s_idx[0], 0)

    def shard_major(g):
        return g.reshape(N_SHARD, g.shape[0] // N_SHARD, D_MODEL)

    (g_in, g_convw), bias, bias_t = _gather_weights([shards["w_in"]], conv_w_shard, rel_bias, buckets)
    w_in_t = with_own(g_in, shards["w_in"]).reshape(IN_COLS, D_MODEL)
    conv_w = _from_col_shards(with_own(g_convw, conv_w_shard))

    h1, uv, qkv = _proj_fwd(x, g1, w_in_t)
    a = _gate_fwd(uv, ln_g, ln_b, w_s, bz)
    later = ["w_out", "w_gate", "w_up"]
    o_bf, lse, gathered = _attn_fwd(qkv, bias, batch, [shards[n] for n in later])
    g_out, g_gate, g_up = [with_own(g, shards[n]) for g, n in zip(gathered, later)]
    w_out = g_out.reshape(D_MODEL, D_MODEL)
    w_gate_t = g_gate.reshape(D_FF, D_MODEL)
    w_up_t = g_up.reshape(D_FF, D_MODEL)
    (y1, x1, h2), _ = _fused_rows(
        "out_proj_mid_fwd", 512,
        [(a, w_out, "nn", slice(0, A_WIDTH)), (o_bf, w_out, "nn", slice(A_WIDTH, D_MODEL))],
        [x], [g2, g3], _mid_fwd_rows, [F32, F32, BF16], [])
    gate, up, g_down = _mm_pair_nt(h2, w_gate_t, w_up_t, shards["w_down"], tm=1024, tn=1408, out_dtype=BF16,
                                   name="mm_gate_up")
    w_down = with_own(g_down, shards["w_down"]).reshape(D_FF, D_MODEL)
    act = _convgate_fwd(gate, up, conv_w, conv_b, batch)
    (dx2, dy2, dg4, loss), _ = _fused_rows(
        "down_proj_loss_head", 512, [(act, w_down, "nn", None)], [x1, tgt], [g4], _loss_head_rows,
        [F32, BF16], [(1, D_MODEL), (1, 128)])

    dact = _mm(dy2, w_down, dims="nt", tm=1024, tn=1408, tk=1024, out_dtype=BF16, name="mm_dact")
    dw_down = _mm(act, dy2, dims="tn", tm=1408, tn=1024, tk=1024, out_dtype=F32, name="mm_dw_down")
    dgate, dup, dconv_w, dconv_b = _convgate_bwd(gate, up, dact, conv_w, conv_b, batch)
    (dx1, dy1, dg2, dg3), _ = _fused_rows(
        "dh2_mid_bwd", 256, [(dgate, w_gate_t, "nn", None), (dup, w_up_t, "nn", None)],
        [x1, y1, dx2], [g2, g3], _mid_bwd_rows, [F32, BF16], [(1, D_MODEL), (1, D_MODEL)])
    dw_gate_t = _mm(dgate, h2, dims="tn", tm=1408, tn=1024, tk=1024, out_dtype=F32, name="mm_dw_gate")
    dw_up_t = _mm(dup, h2, dims="tn", tm=1408, tn=1024, tk=1024, out_dtype=F32, name="mm_dw_up")
    dmix = _mm(dy1, w_out, dims="nt", tn=1024, tk=1024, name="mm_dmix", **big)
    dw_out_a = _mm(a, dy1, dims="tn", tm=A_WIDTH, tn=1024, tk=1024, out_dtype=F32, name="mm_dw_out_a")
    dw_out_b = _mm(o_bf, dy1, dims="tn", tm=B_WIDTH, tn=1024, tk=1024, out_dtype=F32, name="mm_dw_out_b")

    dw_out = jnp.concatenate([dw_out_a, dw_out_b], axis=0)
    done = [shard_major(g) for g in (dw_down, dw_gate_t, dw_up_t, dw_out)]
    (dproj, dln_g, dln_b, dw_s, dbz), recv_a = _gate_bwd(uv, dmix, ln_g, ln_b, w_s, w_st, bz, done)
    parts = [_add_halves(g, r, c_idx) for g, r in zip(done, recv_a)]
    early = dict(loss=loss, norm_mix_post=dg2, norm_ffn_pre=dg3, norm_ffn_post=dg4, ln_v_gain=dln_g,
                 ln_v_bias=dln_b, spatial_w=dw_s, spatial_b=dbz, conv_w=dconv_w, conv_b=dconv_b)
    dproj, ds, recv, early_sums = _attn_bwd(qkv, dmix, o_bf, lse, bias_t, dproj, batch, parts, list(early.values()))
    fulls = [_add_chips(p, r, s_idx, c_idx) for p, r in zip(parts, recv)]
    dw_in_t = _mm(dproj, h1, dims="tn", tm=1408, tn=1024, tk=1024, out_dtype=F32, name="mm_dw_in")
    last = [shard_major(dw_in_t)]
    drel, recv_in_a = _rel_bias_grad(ds, np.ascontiguousarray(np.swapaxes(buckets, 1, 2)), last)
    part_in = [_add_halves(g, r, c_idx) for g, r in zip(last, recv_in_a)]
    (dx0, dg1), recv_in = _fused_rows(
        "dh1_in_bwd", 512, [(dproj, w_in_t, "nn", None)], [x, dx1], [g1], _in_bwd_rows,
        [F32], [(1, D_MODEL)], exchange=part_in)
    fulls += [_add_chips(p, r, s_idx, c_idx) for p, r in zip(part_in, recv_in)]
    half_reduced = dict(zip(["w_down", "w_gate", "w_up", "w_out", "w_in"], fulls))

    return dx0, dict(zip(early, early_sums)), dict(norm_mix_pre=dg1, rel_bias=drel), half_reduced


def _adamw_update(w, g, m, v):
    nm = ADAM_B1 * m + (1.0 - ADAM_B1) * g
    nv = ADAM_B2 * v + (1.0 - ADAM_B2) * (g * g)
    m_hat = nm / (1.0 - ADAM_B1 ** ADAM_STEP)
    v_hat = nv / (1.0 - ADAM_B2 ** ADAM_STEP)
    return -ADAM_LR * (m_hat / (jnp.sqrt(v_hat) + ADAM_EPS) + ADAM_WD * w), nm, nv


def _adamw(w, g, m, v, name):
    rows, cols = w.shape
    tr = next(cand for cand in (352, 256, 128) if rows % cand == 0)

    def body(w_ref, g_ref, m_ref, v_ref, go_ref, d_ref, nm_ref, nv_ref):
        gv = g_ref[...]
        go_ref[...] = gv
        d_ref[...], nm_ref[...], nv_ref[...] = _adamw_update(w_ref[...], gv, m_ref[...], v_ref[...])

    spec = pl.BlockSpec((tr, cols), lambda i: (i, 0))
    sds = jax.ShapeDtypeStruct((rows, cols), F32)
    return pl.pallas_call(
        body, grid=(rows // tr,), in_specs=[spec] * 4, out_specs=[spec] * 4, out_shape=[sds] * 4,
        compiler_params=_params(("parallel",)), name=name,
    )(w, g, m, v)


def _adamw_small(ws, gs, ms, vs):
    n = len(ws)

    def body(*refs):
        w_refs, g_refs, m_refs, v_refs = refs[:n], refs[n:2 * n], refs[2 * n:3 * n], refs[3 * n:4 * n]
        d_refs, nm_refs, nv_refs = refs[4 * n:5 * n], refs[5 * n:6 * n], refs[6 * n:7 * n]
        for t in range(n):
            d_refs[t][...], nm_refs[t][...], nv_refs[t][...] = _adamw_update(
                w_refs[t][...], g_refs[t][...], m_refs[t][...], v_refs[t][...])

    vm = pl.BlockSpec(memory_space=pltpu.VMEM)
    sds = [jax.ShapeDtypeStruct(w.shape, F32) for w in ws]
    res = pl.pallas_call(
        body, in_specs=[vm] * (4 * n), out_specs=[vm] * (3 * n), out_shape=sds * 3,
        compiler_params=_params(), name="adamw_small",
    )(*ws, *gs, *ms, *vs)
    return res[:n], res[n:2 * n], res[2 * n:]


SMALL = ["norm_mix_pre", "norm_mix_post", "norm_ffn_pre", "norm_ffn_post", "ln_v_gain", "ln_v_bias",
         "spatial_w", "spatial_b", "rel_bias", "conv_b"]
LARGE = ["w_in", "w_gate", "w_up", "w_down", "w_out"]
TRANSPOSED = ("w_in", "w_gate", "w_up")
ORDER = ["norm_mix_pre", "norm_mix_post", "norm_ffn_pre", "norm_ffn_post", "w_in", "ln_v_gain", "ln_v_bias",
         "spatial_w", "spatial_b", "rel_bias", "w_out", "w_gate", "w_up", "conv_w", "conv_b", "w_down"]


def kernel(x, norm_mix_pre, norm_mix_post, norm_ffn_pre, norm_ffn_post, w_in, ln_v_gain, ln_v_bias, spatial_w, spatial_b, rel_bias, w_out, w_gate, w_up, conv_w, conv_b, w_down, loss_target, m_norm_mix_pre, m_norm_mix_post, m_norm_ffn_pre, m_norm_ffn_post, m_w_in, m_ln_v_gain, m_ln_v_bias, m_spatial_w, m_spatial_b, m_rel_bias, m_w_out, m_w_gate, m_w_up, m_conv_w, m_conv_b, m_w_down, v_norm_mix_pre, v_norm_mix_post, v_norm_ffn_pre, v_norm_ffn_post, v_w_in, v_ln_v_gain, v_ln_v_bias, v_spatial_w, v_spatial_b, v_rel_bias, v_w_out, v_w_gate, v_w_up, v_conv_w, v_conv_b, v_w_down):
    params = dict(norm_mix_pre=norm_mix_pre, norm_mix_post=norm_mix_post, norm_ffn_pre=norm_ffn_pre,
                  norm_ffn_post=norm_ffn_post, w_in=w_in, ln_v_gain=ln_v_gain, ln_v_bias=ln_v_bias,
                  spatial_w=spatial_w, spatial_b=spatial_b, rel_bias=rel_bias, w_out=w_out, w_gate=w_gate,
                  w_up=w_up, conv_w=conv_w, conv_b=conv_b, w_down=w_down)
    mom = dict(norm_mix_pre=m_norm_mix_pre, norm_mix_post=m_norm_mix_post, norm_ffn_pre=m_norm_ffn_pre,
               norm_ffn_post=m_norm_ffn_post, w_in=m_w_in, ln_v_gain=m_ln_v_gain, ln_v_bias=m_ln_v_bias,
               spatial_w=m_spatial_w, spatial_b=m_spatial_b, rel_bias=m_rel_bias, w_out=m_w_out, w_gate=m_w_gate,
               w_up=m_w_up, conv_w=m_conv_w, conv_b=m_conv_b, w_down=m_w_down)
    var = dict(norm_mix_pre=v_norm_mix_pre, norm_mix_post=v_norm_mix_post, norm_ffn_pre=v_norm_ffn_pre,
               norm_ffn_post=v_norm_ffn_post, w_in=v_w_in, ln_v_gain=v_ln_v_gain, ln_v_bias=v_ln_v_bias,
               spatial_w=v_spatial_w, spatial_b=v_spatial_b, rel_bias=v_rel_bias, w_out=v_w_out, w_gate=v_w_gate,
               w_up=v_w_up, conv_w=v_conv_w, conv_b=v_conv_b, w_down=v_w_down)

    batch = x.shape[0]
    xi, yi, ci = lax.axis_index("x"), lax.axis_index("y"), lax.axis_index("c")
    s_idx = (2 * xi + yi).astype(jnp.int32).reshape(1)
    c_idx = ci.astype(jnp.int32).reshape(1)

    def local(a, n):
        return jnp.swapaxes(a[0], 0, 1) if n in TRANSPOSED else a[0]

    shards = {n: local(params[n], n).astype(BF16) for n in LARGE}
    dx0, total, partial, half_reduced = _train_step(
        x.reshape(batch * SEQ, D_MODEL), loss_target.reshape(batch * SEQ, D_MODEL),
        norm_mix_pre, norm_mix_post, norm_ffn_pre, norm_ffn_post, shards,
        ln_v_gain.reshape(1, A_WIDTH), ln_v_bias.reshape(1, A_WIDTH), spatial_w[0], spatial_b[0], rel_bias,
        conv_w[0], conv_b, batch, s_idx, c_idx)
    grad_x = dx0.reshape(batch, SEQ, D_MODEL)

    names = list(partial)
    fulls, sums = _finish_reductions([half_reduced[n] for n in LARGE], [partial[n] for n in names])
    reduced = dict(zip(LARGE, fulls))
    total.update(zip(names, sums))
    loss = total["loss"][0, 0]
    total["spatial_b"] = total["spatial_b"][:, ::HEAD_DIM].T
    total["rel_bias"] = total["rel_bias"].reshape(B_HEADS, NUM_BUCKETS)
    total["conv_w"] = lax.dynamic_slice_in_dim(total["conv_w"], s_idx[0] * SHARD_FF, SHARD_FF, axis=1)
    small_names = SMALL + ["conv_w"]

    def small(a, n):
        return a.T if n == "rel_bias" else a

    for n in small_names:
        reduced[n] = total[n].reshape(small(params[n], n).shape)

    out_g, out_d, out_m, out_v = {}, {}, {}, {}
    for n in LARGE:
        res = _adamw(local(params[n], n), reduced[n], local(mom[n], n), local(var[n], n), name=f"adamw_{n}")
        if n in TRANSPOSED:
            res = [jnp.swapaxes(r, 0, 1) for r in res]
        out_g[n], out_d[n], out_m[n], out_v[n] = [r[None] for r in res]
    d, nm, nv = _adamw_small([small(params[n], n) for n in small_names], [reduced[n] for n in small_names],
                             [small(mom[n], n) for n in small_names], [small(var[n], n) for n in small_names])
    for n, dd, mm, vv in zip(small_names, d, nm, nv):
        out_g[n], out_d[n], out_m[n], out_v[n] = [small(r, n) for r in (reduced[n], dd, mm, vv)]

    return (loss, grad_x, *[out_g[n] for n in ORDER], *[out_d[n] for n in ORDER],
            *[out_m[n] for n in ORDER], *[out_v[n] for n in ORDER])
```

```python
import functools
import math

import numpy as np
import jax
import jax.numpy as jnp
from jax import lax
from jax.experimental import pallas as pl
from jax.experimental.pallas import tpu as pltpu

F32 = jnp.float32
BF16 = jnp.bfloat16
MESH = pl.DeviceIdType.MESH

D_MODEL = 1024
SEQ = 2048
HEAD_DIM = 64
A_GROUPS = 4
A_WIDTH = 256
B_HEADS = 12
B_WIDTH = 768
CHUNK = 128
DILATED = ((128, 1), (512, 4), (2048, 16))
NUM_BUCKETS = 32
MAX_DISTANCE = 2048
D_FF = 2816
IN_COLS = 2816
NORM_EPS = 1e-6
NEG_INF = -1e30
N_SHARD = 4
SHARD_FF = D_FF // N_SHARD
LANE_BLOCK = 256
VMEM_LIMIT = 56 * 1024 * 1024

ADAM_LR = 0.001
ADAM_B1 = 0.9
ADAM_B2 = 0.999
ADAM_EPS = 1e-08
ADAM_WD = 0.01
ADAM_STEP = 10

GELU_C = math.sqrt(2.0 / math.pi)
GELU_A = 0.044715

ANY = pl.BlockSpec(memory_space=pl.ANY)


def _params(sem=None):
    return pltpu.CompilerParams(dimension_semantics=sem, vmem_limit_bytes=VMEM_LIMIT)


def _dot(a, b, precision=None):
    return jnp.dot(a, b, preferred_element_type=F32, precision=precision)


def _dot_nt(a, b, precision=None):
    return lax.dot_general(a, b, (((1,), (1,)), ((), ())), preferred_element_type=F32, precision=precision)


def _dot_tn(a, b):
    return lax.dot_general(a, b, (((0,), (0,)), ((), ())), preferred_element_type=F32)


def _gelu(x):
    t = jnp.tanh(x * (GELU_C + (GELU_C * GELU_A) * (x * x)))
    return (0.5 * x) * (1.0 + t)


def _gelu_and_grad(x):
    x2 = x * x
    u = 1.0 + jnp.tanh(x * (GELU_C + (GELU_C * GELU_A) * x2))
    hx = 0.5 * x
    dg = u * (0.5 + hx * (2.0 - u) * (GELU_C + (3.0 * GELU_C * GELU_A) * x2))
    return hx * u, dg


def _mesh_pos():
    x, y, c = lax.axis_index("x"), lax.axis_index("y"), lax.axis_index("c")
    chips = [(1 - x, y), (x, 1 - y), (1 - x, 1 - y)]
    return x, y, c, chips


class _GatherPlan:
    def __init__(self, shapes, shard_refs, out_refs, send_sems, recv_sems):
        self.shapes, self.shard_refs, self.out_refs = shapes, shard_refs, out_refs
        self.send_sems, self.recv_sems = send_sems, recv_sems
        self.x, self.y, self.c, self.chips = _mesh_pos()
        self.sib = (self.x, self.y, 1 - self.c)

    def _half(self, t, chip, which):
        rows = self.shapes[t][0] // 2
        return self.out_refs[t].at[2 * chip[0] + chip[1], pl.ds(which * rows, rows), :]

    def _copy(self, k, src, dst, to):
        return pltpu.make_async_remote_copy(src_ref=src, dst_ref=dst, send_sem=self.send_sems.at[k],
                                            recv_sem=self.recv_sems.at[k], device_id=to, device_id_type=MESH)

    def _sends(self, t):
        rows = self.shapes[t][0] // 2
        src = self.shard_refs[t].at[pl.ds(self.c * rows, rows), :]
        return [self._copy(6 * t + j, src, self._half(t, (self.x, self.y), self.c), (*chip, self.c))
                for j, chip in enumerate(self.chips)]

    def _forwards(self, t):
        return [self._copy(6 * t + 3 + j, self._half(t, chip, self.c), self._half(t, chip, self.c), self.sib)
                for j, chip in enumerate(self.chips)]

    def start(self, ts):
        for t in ts:
            for cp in self._sends(t):
                cp.start()

    def forward(self, ts):
        for t in ts:
            for j, chip in enumerate(self.chips):
                landed = self._half(t, chip, self.c)
                self._copy(6 * t + j, landed, landed, (*chip, self.c)).wait_recv()
            for cp in self._forwards(t):
                cp.start()

    def finish(self, ts):
        for t in ts:
            for j, chip in enumerate(self.chips):
                other = self._half(t, chip, 1 - self.c)
                self._copy(6 * t + 3 + j, other, other, self.sib).wait_recv()
        for t in ts:
            for cp in self._sends(t) + self._forwards(t):
                cp.wait_send()


class _SiblingExchangePlan:
    def __init__(self, shapes, grad_refs, out_refs, send_sems, recv_sems):
        self.shapes, self.grad_refs, self.out_refs = shapes, grad_refs, out_refs
        self.send_sems, self.recv_sems = send_sems, recv_sems
        self.x, self.y, self.c, _ = _mesh_pos()

    def _copies(self):
        out = []
        for t, (g, o) in enumerate(zip(self.grad_refs, self.out_refs)):
            rows = self.shapes[t][1] // 2
            out.append(pltpu.make_async_remote_copy(
                src_ref=g.at[:, pl.ds((1 - self.c) * rows, rows), :], dst_ref=o, send_sem=self.send_sems.at[t],
                recv_sem=self.recv_sems.at[t], device_id=(self.x, self.y, 1 - self.c), device_id_type=MESH))
        return out

    def start(self):
        for cp in self._copies():
            cp.start()

    def finish(self):
        for cp in self._copies():
            cp.wait()


class _ChipExchangePlan:
    def __init__(self, part_refs, out_refs, send_sems, recv_sems):
        self.part_refs, self.out_refs, self.send_sems, self.recv_sems = part_refs, out_refs, send_sems, recv_sems
        _, _, self.c, self.chips = _mesh_pos()

    def _copies(self):
        return [pltpu.make_async_remote_copy(
            src_ref=p.at[2 * chip[0] + chip[1]], dst_ref=o.at[j], send_sem=self.send_sems.at[3 * t + j],
            recv_sem=self.recv_sems.at[3 * t + j], device_id=(*chip, self.c), device_id_type=MESH)
            for t, (p, o) in enumerate(zip(self.part_refs, self.out_refs)) for j, chip in enumerate(self.chips)]

    def start(self):
        for cp in self._copies():
            cp.start()

    def finish(self):
        for cp in self._copies():
            cp.wait()


class _SmallAllReducePlan:
    def __init__(self, in_refs, out_refs, sib_refs, chip_refs, send_sems, recv_sems):
        self.in_refs, self.out_refs, self.sib_refs, self.chip_refs = in_refs, out_refs, sib_refs, chip_refs
        self.send_sems, self.recv_sems = send_sems, recv_sems
        self.n = len(in_refs)
        self.x, self.y, self.c, self.chips = _mesh_pos()

    def _copy(self, k, src, dst, to):
        return pltpu.make_async_remote_copy(src_ref=src, dst_ref=dst, send_sem=self.send_sems.at[k],
                                            recv_sem=self.recv_sems.at[k], device_id=to, device_id_type=MESH)

    def _first(self):
        return [self._copy(t, self.in_refs[t], self.sib_refs[t], (self.x, self.y, 1 - self.c)) for t in range(self.n)]

    def _second(self):
        return [self._copy(self.n + 3 * t + j, self.out_refs[t], self.chip_refs[t].at[j], (*chip, self.c))
                for t in range(self.n) for j, chip in enumerate(self.chips)]

    def start_sibling(self):
        for cp in self._first():
            cp.start()

    def sum_sibling_and_start_chips(self):
        for cp in self._first():
            cp.wait()
        for t in range(self.n):
            self.out_refs[t][...] = self.in_refs[t][...] + self.sib_refs[t][...]
        for cp in self._second():
            cp.start()

    def finish(self):
        for cp in self._second():
            cp.wait()
        for t in range(self.n):
            self.out_refs[t][...] = ((self.out_refs[t][...] + self.chip_refs[t][0])
                                     + (self.chip_refs[t][1] + self.chip_refs[t][2]))

    @staticmethod
    def scratch(arrays):
        return ([pltpu.VMEM(a.shape, F32) for a in arrays] + [pltpu.VMEM((3,) + a.shape, F32) for a in arrays]
                + _sem_pair(4 * len(arrays)))


def _sem_pair(n):
    return [pltpu.SemaphoreType.DMA((n,)), pltpu.SemaphoreType.DMA((n,))]


def _mm(a, b, *, dims, tm, tn, tk, out_dtype, name):
    if dims == "nn":
        m, k = a.shape
        n = b.shape[1]
        a_spec = pl.BlockSpec((tm, tk), lambda i, j, kk: (i, kk))
        b_spec = pl.BlockSpec((tk, tn), lambda i, j, kk: (kk, j))
        dot = _dot
    elif dims == "nt":
        m, k = a.shape
        n = b.shape[0]
        a_spec = pl.BlockSpec((tm, tk), lambda i, j, kk: (i, kk))
        b_spec = pl.BlockSpec((tn, tk), lambda i, j, kk: (j, kk))
        dot = _dot_nt
    else:
        k, m = a.shape
        n = b.shape[1]
        a_spec = pl.BlockSpec((tk, tm), lambda i, j, kk: (kk, i))
        b_spec = pl.BlockSpec((tk, tn), lambda i, j, kk: (kk, j))
        dot = _dot_tn
    assert m % tm == 0 and n % tn == 0 and k % tk == 0, (name, m, n, k)
    grid = (m // tm, n // tn, k // tk)
    nk = grid[2]
    assert nk == 1 or out_dtype == F32, name

    def body(a_ref, b_ref, o_ref):
        prod = dot(a_ref[...].astype(BF16), b_ref[...].astype(BF16))
        if nk == 1:
            o_ref[...] = prod.astype(out_dtype)
        else:
            kk = pl.program_id(2)

            @pl.when(kk == 0)
            def _():
                o_ref[...] = prod

            @pl.when(kk > 0)
            def _():
                o_ref[...] += prod

    return pl.pallas_call(
        body, grid=grid, in_specs=[a_spec, b_spec],
        out_specs=pl.BlockSpec((tm, tn), lambda i, j, kk: (i, j)),
        out_shape=jax.ShapeDtypeStruct((m, n), out_dtype),
        compiler_params=_params(("parallel", "parallel", "arbitrary")), name=name,
    )(a, b)


def _mm_pair_nt(a, w1_t, w2_t, shard, *, tm, tn, out_dtype, name):
    m, k = a.shape
    n = w1_t.shape[0]
    assert m % tm == 0 and n % tn == 0 and w2_t.shape == w1_t.shape, name
    grid = (m // tm, n // tn)
    n_steps = grid[0] * grid[1]

    def body(a_ref, w1_ref, w2_ref, shard_ref, o1_ref, o2_ref, gat_ref, send_sems, recv_sems):
        step = pl.program_id(0) * grid[1] + pl.program_id(1)
        gather = _GatherPlan([shard.shape], [shard_ref], [gat_ref], send_sems, recv_sems)

        @pl.when(step == 0)
        def _():
            gather.start([0])

        @pl.when(step == (2 * n_steps) // 3)
        def _():
            gather.forward([0])

        av = a_ref[...]
        o1_ref[...] = _dot_nt(av, w1_ref[...]).astype(out_dtype)
        o2_ref[...] = _dot_nt(av, w2_ref[...]).astype(out_dtype)

        @pl.when(step == n_steps - 1)
        def _():
            gather.finish([0])

    w_spec = pl.BlockSpec((tn, k), lambda i, j: (j, 0))
    o_spec = pl.BlockSpec((tm, tn), lambda i, j: (i, j))
    return pl.pallas_call(
        body, grid=grid,
        in_specs=[pl.BlockSpec((tm, k), lambda i, j: (i, 0)), w_spec, w_spec, ANY],
        out_specs=[o_spec, o_spec, ANY],
        out_shape=[jax.ShapeDtypeStruct((m, n), out_dtype)] * 2
        + [jax.ShapeDtypeStruct((N_SHARD,) + shard.shape, shard.dtype)],
        scratch_shapes=_sem_pair(6),
        compiler_params=_params(("arbitrary", "arbitrary")), name=name,
    )(a, w1_t, w2_t, shard)


def _out_proj_bwd(a, o, dy1, w_out):
    m = dy1.shape[0]
    tm = 1024

    def body(a_ref, o_ref, dy_ref, w_ref, dmix_ref, dw_ref):
        dy = dy_ref[...]
        dmix_ref[...] = _dot_nt(dy, w_ref[...])
        top = _dot_tn(a_ref[...], dy)
        bottom = _dot_tn(o_ref[...], dy)

        @pl.when(pl.program_id(0) == 0)
        def _():
            dw_ref[:A_WIDTH, :] = top
            dw_ref[A_WIDTH:, :] = bottom

        @pl.when(pl.program_id(0) > 0)
        def _():
            dw_ref[:A_WIDTH, :] += top
            dw_ref[A_WIDTH:, :] += bottom

    tile = lambda width: pl.BlockSpec((tm, width), lambda i: (i, 0))
    return pl.pallas_call(
        body, grid=(m // tm,),
        in_specs=[tile(A_WIDTH), tile(B_WIDTH), tile(D_MODEL), _full_spec((D_MODEL, D_MODEL))],
        out_specs=[tile(D_MODEL), _full_spec((D_MODEL, D_MODEL))],
        out_shape=[jax.ShapeDtypeStruct((m, D_MODEL), F32), jax.ShapeDtypeStruct((D_MODEL, D_MODEL), F32)],
        compiler_params=_params(("arbitrary",)), name="out_proj_bwd",
    )(a, o, dy1, w_out)


def _fused_rows(name, tm, mats, rows, vecs, fn, row_outs, acc_outs, exchange=()):
    m = mats[0][0].shape[0]
    nm, nr, nv, nro, nao, nx = len(mats), len(rows), len(vecs), len(row_outs), len(acc_outs), len(exchange)
    n_steps = m // tm

    def body(*refs):
        a_refs, w_refs = refs[:nm], refs[nm:2 * nm]
        pos = 2 * nm
        row_refs, vec_refs, part_refs = refs[pos:pos + nr], refs[pos + nr:pos + nr + nv], refs[pos + nr + nv:pos + nr + nv + nx]
        pos += nr + nv + nx
        out_refs, acc_refs, recv_refs = refs[pos:pos + nro], refs[pos + nro:pos + nro + nao], refs[pos + nro + nao:pos + nro + nao + nx]
        sems = refs[pos + nro + nao + nx:]
        i = pl.program_id(0)
        if nx:
            plan = _ChipExchangePlan(part_refs, recv_refs, *sems)

            @pl.when(i == 0)
            def _():
                plan.start()

        @pl.when(i == 0)
        def _():
            for r in acc_refs:
                r[...] = jnp.zeros_like(r)

        y = None
        for a_ref, w_ref, (_, _, dims, sl) in zip(a_refs, w_refs, mats):
            w = w_ref[...] if sl is None else w_ref[sl, :]
            part = (_dot if dims == "nn" else _dot_nt)(a_ref[...], w)
            y = part if y is None else y + part
        res = fn(y, *[r[...] for r in row_refs], *[v[...] for v in vec_refs])
        for r, val in zip(out_refs, res[:nro]):
            r[...] = val.astype(r.dtype)
        for r, val in zip(acc_refs, res[nro:]):
            r[...] += val

        if nx:
            @pl.when(i == n_steps - 1)
            def _():
                plan.finish()

    tile = lambda width: pl.BlockSpec((tm, width), lambda i: (i, 0))
    res = pl.pallas_call(
        body, grid=(n_steps,),
        in_specs=[tile(a.shape[1]) for a, _, _, _ in mats] + [_full_spec(w.shape) for _, w, _, _ in mats]
        + [tile(D_MODEL)] * nr + [_full_spec((1, D_MODEL))] * nv + [ANY] * nx,
        out_specs=[tile(D_MODEL)] * nro + [_full_spec(s) for s in acc_outs] + [ANY] * nx,
        out_shape=[jax.ShapeDtypeStruct((m, D_MODEL), dt) for dt in row_outs]
        + [jax.ShapeDtypeStruct(s, F32) for s in acc_outs]
        + [jax.ShapeDtypeStruct((3,) + p.shape[1:], p.dtype) for p in exchange],
        scratch_shapes=_sem_pair(3 * nx) if nx else [],
        compiler_params=_params(("arbitrary",)), name=name,
    )(*[a for a, _, _, _ in mats], *[w for _, w, _, _ in mats], *rows, *vecs, *exchange)
    return list(res[:nro + nao]), list(res[nro + nao:])


ROW_TILE = 512


def _vec_spec(width=D_MODEL):
    return pl.BlockSpec((1, width), lambda i: (0, 0))


def _rstd(v):
    return lax.rsqrt(jnp.mean(v * v, axis=-1, keepdims=True) + NORM_EPS)


def _mid_fwd_rows(y1, x0, g2, g3):
    x1 = x0 + y1 * _rstd(y1) * g2
    return y1, x1, x1 * _rstd(x1) * g3


def _rms_bwd_rows(dout, v, g):
    r = _rstd(v)
    n = v * r
    dn = dout * g
    dv = r * (dn - n * jnp.mean(dn * n, axis=-1, keepdims=True))
    dg = jnp.sum(dout * n, axis=0, keepdims=True)
    return dv, dg


def _loss_head_rows(y2, x1, tgt, g4):
    x2 = x1 + y2 * _rstd(y2) * g4
    err = x2 - tgt
    loss = 0.5 * jnp.sum(jnp.mean(err * err, axis=-1, keepdims=True), axis=0, keepdims=True)
    dx2 = err * (1.0 / D_MODEL)
    dy2, dg4 = _rms_bwd_rows(dx2, y2, g4)
    return dx2, dy2, dg4, loss


def _mid_bwd_rows(dh2, x1, y1, dx2, g2, g3):
    d3, dg3 = _rms_bwd_rows(dh2, x1, g3)
    dx1 = dx2 + d3
    dy1, dg2 = _rms_bwd_rows(dx1, y1, g2)
    return dx1, dy1, dg2, dg3


def _in_bwd_rows(dh1, x0, dx1, g1):
    d1, dg1 = _rms_bwd_rows(dh1, x0, g1)
    return dx1 + d1, dg1


GATE_ROWS = 512


def _group_mean_matrix():
    p = np.zeros((A_WIDTH, A_WIDTH), np.float32)
    for g in range(A_GROUPS):
        p[g * HEAD_DIM:(g + 1) * HEAD_DIM, g * HEAD_DIM:(g + 1) * HEAD_DIM] = 1.0 / HEAD_DIM
    return jnp.asarray(p)


def _group_masks(width=A_WIDTH):
    lane = lax.broadcasted_iota(jnp.int32, (1, width), 1)
    return [(lane >= g * HEAD_DIM) & (lane < (g + 1) * HEAD_DIM) for g in range(width // HEAD_DIM)]


GROUP_SUM_PRECISION = lax.Precision.HIGH


def _layernorm_groups(vg, pavg):
    hi = GROUP_SUM_PRECISION
    mu = _dot(vg, pavg, hi)
    xc = vg - mu
    var = _dot(xc * xc, pavg, hi)
    rstd = lax.rsqrt(var + NORM_EPS)
    return xc * rstd, rstd


def _spatial_mix(w_bf, vn_chunk_bf, masks, bz):
    z = bz
    for g in range(A_GROUPS):
        z = z + jnp.where(masks[g], _dot(w_bf[g], vn_chunk_bf), 0.0)
    return z


def _full_spec(shape):
    return pl.BlockSpec(shape, lambda i: tuple(0 for _ in shape))


def _gate_fwd(uv, ln_g, ln_b, w_s, bz):
    m = uv.shape[0]
    pavg = _group_mean_matrix()

    def body(u_ref, v_ref, lg_ref, lb_ref, w_ref, bz_ref, p_ref, a_ref):
        masks = _group_masks()
        row = lax.broadcasted_iota(jnp.int32, (CHUNK, CHUNK), 0)
        col = lax.broadcasted_iota(jnp.int32, (CHUNK, CHUNK), 1)
        w_bf = [jnp.where(row >= col, w_ref[g], 0.0).astype(BF16) for g in range(A_GROUPS)]
        ug = _gelu(u_ref[...])
        vhat, _ = _layernorm_groups(_gelu(v_ref[...]), p_ref[...])
        vn = vhat * lg_ref[...] + lb_ref[...]
        bz = bz_ref[...]
        for c in range(GATE_ROWS // CHUNK):
            sl = slice(c * CHUNK, (c + 1) * CHUNK)
            z = _spatial_mix(w_bf, vn[sl].astype(BF16), masks, bz)
            a_ref[sl, :] = (ug[sl] * z).astype(BF16)

    return pl.pallas_call(
        body, grid=(m // GATE_ROWS,),
        in_specs=[pl.BlockSpec((GATE_ROWS, A_WIDTH), lambda i: (i, 0)),
                  pl.BlockSpec((GATE_ROWS, A_WIDTH), lambda i: (i, 1)),
                  _full_spec((1, A_WIDTH)), _full_spec((1, A_WIDTH)), _full_spec((A_GROUPS, CHUNK, CHUNK)),
                  _full_spec((CHUNK, A_WIDTH)), _full_spec((A_WIDTH, A_WIDTH))],
        out_specs=pl.BlockSpec((GATE_ROWS, A_WIDTH), lambda i: (i, 0)),
        out_shape=jax.ShapeDtypeStruct((m, A_WIDTH), BF16),
        compiler_params=_params(("parallel",)), name="gate_fwd",
    )(uv, uv, ln_g, ln_b, w_s, bz, pavg)


def _gate_bwd(uv, dmix, ln_g, ln_b, w_s, w_st, bz, grads):
    m = uv.shape[0]
    pavg = _group_mean_matrix()
    nsteps = m // GATE_ROWS
    nx = len(grads)
    shapes = [g.shape for g in grads]

    def body(u_ref, v_ref, da_ref, lg_ref, lb_ref, w_ref, wt_ref, bz_ref, p_ref, *rest):
        grad_refs = rest[:nx]
        duv_ref, dlg_ref, dlb_ref, dw_ref, dbz_ref = rest[nx:nx + 5]
        recv_refs = rest[nx + 5:2 * nx + 5]
        exchange = _SiblingExchangePlan(shapes, grad_refs, recv_refs, *rest[2 * nx + 5:])
        i = pl.program_id(0)

        @pl.when(i == 0)
        def _():
            exchange.start()
            dlg_ref[...] = jnp.zeros_like(dlg_ref)
            dlb_ref[...] = jnp.zeros_like(dlb_ref)
            dw_ref[...] = jnp.zeros_like(dw_ref)
            dbz_ref[...] = jnp.zeros_like(dbz_ref)

        hi = GROUP_SUM_PRECISION
        masks = _group_masks()
        row = lax.broadcasted_iota(jnp.int32, (CHUNK, CHUNK), 0)
        col = lax.broadcasted_iota(jnp.int32, (CHUNK, CHUNK), 1)
        tril = row >= col
        w_bf = [jnp.where(tril, w_ref[g], 0.0).astype(BF16) for g in range(A_GROUPS)]
        wt_bf = [jnp.where(col >= row, wt_ref[g], 0.0).astype(BF16) for g in range(A_GROUPS)]
        pavg_v = p_ref[...]
        lg = lg_ref[...]
        ug, dug = _gelu_and_grad(u_ref[...])
        vg, dvg_dx = _gelu_and_grad(v_ref[...])
        vhat, rstd = _layernorm_groups(vg, pavg_v)
        vn = vhat * lg + lb_ref[...]
        da = da_ref[...]
        bz = bz_ref[...]
        for c in range(GATE_ROWS // CHUNK):
            sl = slice(c * CHUNK, (c + 1) * CHUNK)
            vn_bf = vn[sl].astype(BF16)
            z = _spatial_mix(w_bf, vn_bf, masks, bz)
            dz = da[sl] * ug[sl]
            duv_ref[sl, 0:A_WIDTH] = (da[sl] * z * dug[sl]).astype(BF16)
            dbz_ref[...] += dz
            dz_bf = dz.astype(BF16)
            dvn = jnp.zeros((CHUNK, A_WIDTH), F32)
            for g in range(A_GROUPS):
                dz_g = jnp.where(masks[g], dz, 0.0).astype(BF16)
                dw_ref[g] += jnp.where(tril, _dot_nt(dz_g, vn_bf), 0.0)
                dvn = dvn + jnp.where(masks[g], _dot(wt_bf[g], dz_bf), 0.0)
            vh = vhat[sl]
            dlb_ref[...] += jnp.sum(dvn, axis=0, keepdims=True)
            dlg_ref[...] += jnp.sum(dvn * vh, axis=0, keepdims=True)
            dvh = dvn * lg
            m1 = _dot(dvh, pavg_v, hi)
            m2 = _dot(dvh * vh, pavg_v, hi)
            duv_ref[sl, A_WIDTH:2 * A_WIDTH] = (rstd[sl] * (dvh - m1 - vh * m2) * dvg_dx[sl]).astype(BF16)

        @pl.when(i == nsteps - 1)
        def _():
            dbz_ref[...] = _dot(dbz_ref[...], pavg_v * float(HEAD_DIM), hi)
            exchange.finish()

    res = pl.pallas_call(
        body, grid=(nsteps,),
        in_specs=[pl.BlockSpec((GATE_ROWS, A_WIDTH), lambda i: (i, 0)),
                  pl.BlockSpec((GATE_ROWS, A_WIDTH), lambda i: (i, 1)),
                  pl.BlockSpec((GATE_ROWS, A_WIDTH), lambda i: (i, 0)),
                  _full_spec((1, A_WIDTH)), _full_spec((1, A_WIDTH)), _full_spec((A_GROUPS, CHUNK, CHUNK)),
                  _full_spec((A_GROUPS, CHUNK, CHUNK)), _full_spec((CHUNK, A_WIDTH)),
                  _full_spec((A_WIDTH, A_WIDTH))] + [ANY] * nx,
        out_specs=[pl.BlockSpec((GATE_ROWS, 2 * A_WIDTH), lambda i: (i, 0)),
                   _full_spec((1, A_WIDTH)), _full_spec((1, A_WIDTH)), _full_spec((A_GROUPS, CHUNK, CHUNK)),
                   _full_spec((CHUNK, A_WIDTH))] + [ANY] * nx,
        out_shape=[jax.ShapeDtypeStruct((m, IN_COLS), BF16),
                   jax.ShapeDtypeStruct((1, A_WIDTH), F32), jax.ShapeDtypeStruct((1, A_WIDTH), F32),
                   jax.ShapeDtypeStruct((A_GROUPS, CHUNK, CHUNK), F32),
                   jax.ShapeDtypeStruct((CHUNK, A_WIDTH), F32)]
        + [jax.ShapeDtypeStruct((N_SHARD, s[1] // 2, s[2]), F32) for s in shapes],
        scratch_shapes=_sem_pair(nx),
        compiler_params=_params(("arbitrary",)), name="gate_bwd",
    )(uv, uv, dmix, ln_g, ln_b, w_s, w_st, bz, pavg, *grads)
    return res[:5], list(res[5:])


Q_BLOCK = 128
PAIR = 2 * HEAD_DIM
N_PAIR = B_HEADS // 2
N_CFG = len(DILATED)
BLOCKS_PER_CFG = SEQ // Q_BLOCK
QKV_SLABS = 3 * N_PAIR
FWD_BLOCKS_PER_TRIP = 8
BWD_BLOCKS_PER_TRIP = 4


def _t5_bucket_np(dist, dtype):
    max_exact = NUM_BUCKETS // 2
    d = np.maximum(dist, 1).astype(dtype)
    large = max_exact + (np.log(d / dtype(max_exact)) / dtype(math.log(MAX_DISTANCE / max_exact))
                         * dtype(NUM_BUCKETS - max_exact))
    large = np.minimum(large.astype(np.int32), NUM_BUCKETS - 1)
    return np.where(dist < max_exact, dist, large)


def _bucket_tables():
    i = np.arange(Q_BLOCK)[:, None]
    j = np.arange(Q_BLOCK)[None, :]
    tables = []
    for _, dil in DILATED:
        rel_prev = Q_BLOCK + i - j
        rel_cur = i - j
        rel = np.concatenate([rel_prev, rel_cur], axis=1)
        valid = np.concatenate([rel_prev <= Q_BLOCK, rel_cur >= 0], axis=1)
        dist = np.maximum(rel, 0) * dil
        b32 = _t5_bucket_np(dist, np.float32)
        b64 = _t5_bucket_np(dist, np.float64)
        assert np.array_equal(b32, b64)
        tables.append(np.where(valid, b32, -1).astype(np.int32))
    return np.stack(tables)


def _present_buckets(buckets_np):
    return [sorted(set(int(v) for v in np.unique(buckets_np[c]) if v >= 0)) for c in range(N_CFG)]


def _bias_tables_body(buckets_np):
    present = _present_buckets(buckets_np)

    def tables(rb_ref, bk_ref, o_ref, ot_ref):
        for c in range(N_CFG):
            bk = bk_ref[c]
            for h in range(B_HEADS):
                acc = jnp.full((Q_BLOCK, 2 * Q_BLOCK), NEG_INF, F32)
                for b in present[c]:
                    acc = jnp.where(bk == b, rb_ref[h, b], acc)
                o_ref[c, h] = acc
                ot_ref[c, h] = acc.T

    return tables


def _proj_fwd(x, g1, w_in_t):
    m = x.shape[0]
    tm = ROW_TILE

    def body(x_ref, g_ref, w_ref, h_ref, uv_ref, qkv_ref):
        xv = x_ref[...]
        h = (xv * _rstd(xv) * g_ref[...]).astype(BF16)
        h_ref[...] = h
        acc = _dot_nt(h, w_ref[...])
        uv_ref[...] = acc[:, :2 * A_WIDTH]
        for s in range(QKV_SLABS):
            qkv_ref[s] = acc[:, 2 * A_WIDTH + s * PAIR:2 * A_WIDTH + (s + 1) * PAIR]

    return pl.pallas_call(
        body, grid=(m // tm,),
        in_specs=[pl.BlockSpec((tm, D_MODEL), lambda i: (i, 0)), _vec_spec(),
                  pl.BlockSpec((IN_COLS, D_MODEL), lambda i: (0, 0))],
        out_specs=[pl.BlockSpec((tm, D_MODEL), lambda i: (i, 0)),
                   pl.BlockSpec((tm, 2 * A_WIDTH), lambda i: (i, 0)),
                   pl.BlockSpec((QKV_SLABS, tm, PAIR), lambda i: (0, i, 0))],
        out_shape=[jax.ShapeDtypeStruct((m, D_MODEL), BF16), jax.ShapeDtypeStruct((m, 2 * A_WIDTH), F32),
                   jax.ShapeDtypeStruct((QKV_SLABS, m, PAIR), F32)],
        compiler_params=_params(("parallel",)), name="proj_fwd",
    )(x, g1, w_in_t)


def _pair_masks():
    lane = lax.broadcasted_iota(jnp.int32, (1, PAIR), 1)
    return [lane < HEAD_DIM, lane >= HEAD_DIM]


def _block_rows(idx, dil):
    static = isinstance(idx, int)
    r, n = idx % dil, idx // dil

    def rows_of(block):
        start = r + (dil * Q_BLOCK) * block
        if dil == 1:
            return pl.ds(start if static else pl.multiple_of(start, Q_BLOCK), Q_BLOCK)
        return pl.ds(start, Q_BLOCK, stride=dil)

    prev = rows_of(n - 1) if not static or n > 0 else None
    return rows_of(n), prev


def _attn_fwd(qkv, bias, batch, shards):
    m = qkv.shape[1]
    comb_rows = 256
    nt = len(shards)
    shapes = [sh.shape for sh in shards]
    n_steps = batch * N_PAIR
    early, late = list(range(nt // 2)), list(range(nt // 2, nt))

    def body(q_ref, k_ref, v_ref, b_ref, *rest):
        shard_refs = rest[:nt]
        o_ref, l_ref = rest[nt:nt + 2]
        gat_refs = rest[nt + 2:2 * nt + 2]
        scratch = rest[2 * nt + 2:]
        oc_refs, lc_refs = scratch[:N_CFG], scratch[N_CFG:2 * N_CFG]
        step = pl.program_id(0) * N_PAIR + pl.program_id(1)
        gather = _GatherPlan(shapes, shard_refs, gat_refs, *scratch[2 * N_CFG:])

        @pl.when(step == 0)
        def _():
            gather.start(early + late)

        @pl.when(step == n_steps // 2)
        def _():
            gather.forward(early)

        @pl.when(step == n_steps - 2)
        def _():
            gather.forward(late)

        masks = _pair_masks()
        for ci, (_, dil) in enumerate(DILATED):
            nb = SEQ // dil // Q_BLOCK

            def block(trip, ci=ci, dil=dil, nb=nb):
                work = []
                for u in range(FWD_BLOCKS_PER_TRIP):
                    rows, prow = _block_rows(trip * FWD_BLOCKS_PER_TRIP + u, dil)
                    has_prev = nb > 1 and prow is not None
                    q = q_ref[rows, :] * 0.125
                    kc = k_ref[rows, :].astype(BF16)
                    vc = v_ref[rows, :]
                    kp = k_ref[prow, :].astype(BF16) if has_prev else None
                    vp = v_ref[prow, :] if has_prev else None
                    tiles = []
                    for h in range(2):
                        qh = jnp.where(masks[h], q, 0.0).astype(BF16)
                        sc = _dot_nt(qh, kc) + b_ref[ci, h, :, Q_BLOCK:]
                        sp = _dot_nt(qh, kp) + b_ref[ci, h, :, :Q_BLOCK] if has_prev else None
                        tiles.append((sc, sp))
                    work.append((rows, vc, vp, tiles))
                probs = []
                for _, _, _, tiles in work:
                    ps = []
                    for sc, sp in tiles:
                        mx = jnp.max(sc if sp is None else jnp.maximum(sc, sp), axis=1, keepdims=True)
                        pc = jnp.exp(sc - mx).astype(BF16)
                        pp = None if sp is None else jnp.exp(sp - mx).astype(BF16)
                        ps.append((mx, pc, pp))
                    probs.append(ps)
                for (rows, vc, vp, _), ps in zip(work, probs):
                    res = []
                    for h, (_, pc, pp) in enumerate(ps):
                        r = _dot(pc, jnp.where(masks[h], vc, 1.0).astype(BF16))
                        if pp is not None:
                            r = r + _dot(pp, jnp.where(masks[h], vp, 1.0).astype(BF16))
                        res.append(r)
                    num = jnp.where(masks[0], res[0], res[1])
                    den = pltpu.roll(jnp.where(masks[0], res[1], res[0]), HEAD_DIM, 1)
                    oc_refs[ci][rows, :] = num / den
                    lc_refs[ci][rows, :] = jnp.where(masks[0], ps[0][0], ps[1][0]) + jnp.log(den)

            for trip in range(BLOCKS_PER_CFG // FWD_BLOCKS_PER_TRIP):
                block(trip)

        def combine(i, carry):
            rr = pl.ds(pl.multiple_of(i * comb_rows, comb_rows), comb_rows)
            ls = [lc_refs[c][rr, :] for c in range(N_CFG)]
            mx = functools.reduce(jnp.maximum, ls)
            ws = [jnp.exp(l - mx) for l in ls]
            tot = functools.reduce(lambda a, b: a + b, ws)
            o = functools.reduce(lambda a, b: a + b, [ws[c] * oc_refs[c][rr, :] for c in range(N_CFG)]) / tot
            o_ref[rr, :] = o.astype(BF16)
            l_ref[rr, :] = mx + jnp.log(tot)
            return carry

        lax.fori_loop(0, SEQ // comb_rows, combine, 0)

        @pl.when(step == n_steps - 1)
        def _():
            gather.finish(early + late)

    def slab(first):
        return pl.BlockSpec((None, SEQ, PAIR), lambda b, p: (first + p, b, 0))

    nat = pl.BlockSpec((SEQ, PAIR), lambda b, p: (b, p))
    res = pl.pallas_call(
        body, grid=(batch, N_PAIR),
        in_specs=[slab(0), slab(N_PAIR), slab(2 * N_PAIR),
                  pl.BlockSpec((N_CFG, 2, Q_BLOCK, 2 * Q_BLOCK), lambda b, p: (0, p, 0, 0))] + [ANY] * nt,
        out_specs=[nat, nat] + [ANY] * nt,
        out_shape=[jax.ShapeDtypeStruct((m, B_WIDTH), BF16), jax.ShapeDtypeStruct((m, B_WIDTH), F32)]
        + [jax.ShapeDtypeStruct((N_SHARD,) + sh.shape, sh.dtype) for sh in shards],
        scratch_shapes=[pltpu.VMEM((SEQ, PAIR), F32)] * (2 * N_CFG) + _sem_pair(6 * nt),
        compiler_params=_params(("arbitrary", "arbitrary")), name="attn_fwd",
    )(qkv, qkv, qkv, bias, *shards)
    return res[0], res[1], list(res[2:])


def _attn_bwd(qkv, dmix, o, lse, bias_t, dproj, batch, parts, smalls):
    m = qkv.shape[1]
    nt, ns = len(parts), len(smalls)
    n_steps = N_PAIR * batch

    def body(q_ref, k_ref, v_ref, do_ref, o_ref, l_ref, b_ref, *rest):
        part_refs = rest[1:nt + 1]
        small_refs = rest[nt + 1:nt + 1 + ns]
        pos = nt + 1 + ns
        dproj_ref, ds_ref = rest[pos:pos + 2]
        recv_refs = rest[pos + 2:pos + 2 + nt]
        sum_refs = rest[pos + 2 + nt:pos + 2 + nt + ns]
        pos += 2 + nt + ns
        dq_acc, dk_acc, dv_acc, d_scr, stage, stage_sems, send_sems, recv_sems = rest[pos:pos + 8]
        allreduce = _SmallAllReducePlan(small_refs, sum_refs, rest[pos + 8:pos + 8 + ns],
                                        rest[pos + 8 + ns:pos + 8 + 2 * ns], *rest[pos + 8 + 2 * ns:])
        pair, seq = pl.program_id(0), pl.program_id(1)
        step = pair * batch + seq
        exchange = _ChipExchangePlan(part_refs, recv_refs, send_sems, recv_sems)

        @pl.when(step == 0)
        def _():
            allreduce.start_sibling()

        @pl.when(step == n_steps // 2)
        def _():
            allreduce.sum_sibling_and_start_chips()

        def stage_copies():
            rows = pl.ds(pl.multiple_of(seq * SEQ, SEQ), SEQ)
            return [pltpu.make_async_copy(
                stage.at[k],
                dproj_ref.at[rows, pl.ds(pl.multiple_of(2 * A_WIDTH + k * B_WIDTH + pair * PAIR, PAIR), PAIR)],
                stage_sems.at[k]) for k in range(3)]

        @pl.when(step == 0)
        def _():
            exchange.start()

        @pl.when(pl.program_id(1) == 0)
        def _():
            ds_ref[...] = jnp.zeros_like(ds_ref)

        dq_acc[...] = jnp.zeros_like(dq_acc)
        dk_acc[...] = jnp.zeros_like(dk_acc)
        dv_acc[...] = jnp.zeros_like(dv_acc)
        d_scr[...] = do_ref[...] * o_ref[...].astype(F32)
        masks = _pair_masks()

        def stack_heads(t):
            return jnp.concatenate([jnp.where(masks[0], t, 0.0), jnp.where(masks[1], t, 0.0)], axis=0).astype(BF16)

        for ci, (_, dil) in enumerate(DILATED):
            nb = SEQ // dil // Q_BLOCK

            def block(trip, carry, ci=ci, dil=dil, nb=nb):
                first = []
                for u in range(BWD_BLOCKS_PER_TRIP):
                    rows, prow = _block_rows(trip * BWD_BLOCKS_PER_TRIP + u, dil)
                    has_prev = nb > 1 and prow is not None
                    if has_prev:
                        kcat = jnp.concatenate([k_ref[prow, :], k_ref[rows, :]], axis=0).astype(BF16)
                        vcat = jnp.concatenate([v_ref[prow, :], v_ref[rows, :]], axis=0).astype(BF16)
                    else:
                        kcat = k_ref[rows, :].astype(BF16)
                        vcat = v_ref[rows, :].astype(BF16)
                    qst = stack_heads(q_ref[rows, :] * 0.125)
                    dost = stack_heads(do_ref[rows, :])
                    lt = l_ref[rows, :].T
                    dt = d_scr[rows, :].T
                    lrow = jnp.concatenate([lt[0:1], lt[HEAD_DIM:HEAD_DIM + 1]], axis=1)
                    drow = jnp.concatenate([jnp.sum(dt[:HEAD_DIM], axis=0, keepdims=True),
                                            jnp.sum(dt[HEAD_DIM:], axis=0, keepdims=True)], axis=1)
                    first.append((has_prev, rows, prow, kcat, qst, dost, lrow, drow,
                                  _dot_nt(kcat, qst), _dot_nt(vcat, dost)))
                second = []
                for has_prev, rows, prow, kcat, qst, dost, lrow, drow, st, dpt in first:
                    keys = slice(0, 2 * Q_BLOCK) if has_prev else slice(Q_BLOCK, 2 * Q_BLOCK)
                    bt = jnp.concatenate([b_ref[ci, 0, keys, :], b_ref[ci, 1, keys, :]], axis=1)
                    pt = jnp.exp(st + bt - lrow)
                    dst = pt * (dpt - drow)
                    ds_ref[ci, 0, keys, :] += dst[:, :Q_BLOCK]
                    ds_ref[ci, 1, keys, :] += dst[:, Q_BLOCK:]
                    second.append((has_prev, rows, prow, kcat, qst, dost, pt.astype(BF16), dst.astype(BF16)))
                for has_prev, rows, prow, kcat, qst, dost, pt_bf, dst_bf in second:
                    dk = _dot(dst_bf, qst)
                    dv = _dot(pt_bf, dost)
                    dq2 = _dot_tn(dst_bf, kcat)
                    dq_acc[rows, :] += jnp.where(masks[0], dq2[:Q_BLOCK], dq2[Q_BLOCK:]) * 0.125
                    if has_prev:
                        dk_acc[prow, :] += dk[:Q_BLOCK]
                        dv_acc[prow, :] += dv[:Q_BLOCK]
                        dk_acc[rows, :] += dk[Q_BLOCK:]
                        dv_acc[rows, :] += dv[Q_BLOCK:]
                    else:
                        dk_acc[rows, :] += dk
                        dv_acc[rows, :] += dv
                return carry

            for trip in range(BLOCKS_PER_CFG // BWD_BLOCKS_PER_TRIP):
                block(trip, 0)

        @pl.when(step > 0)
        def _():
            for cp in stage_copies():
                cp.wait()

        stage[0] = dq_acc[...].astype(BF16)
        stage[1] = dk_acc[...].astype(BF16)
        stage[2] = dv_acc[...].astype(BF16)
        for cp in stage_copies():
            cp.start()

        @pl.when(step == n_steps - 1)
        def _():
            for cp in stage_copies():
                cp.wait()
            exchange.finish()
            allreduce.finish()

    def slab(first):
        return pl.BlockSpec((None, SEQ, PAIR), lambda p, b: (first + p, b, 0))

    nat = pl.BlockSpec((SEQ, PAIR), lambda p, b: (b, p))
    tbl = pl.BlockSpec((N_CFG, 2, 2 * Q_BLOCK, Q_BLOCK), lambda p, b: (0, p, 0, 0))
    acc = pltpu.VMEM((SEQ, PAIR), F32)
    vm = pl.BlockSpec(memory_space=pltpu.VMEM)
    res = pl.pallas_call(
        body, grid=(N_PAIR, batch),
        in_specs=[slab(0), slab(N_PAIR), slab(2 * N_PAIR),
                  pl.BlockSpec((SEQ, PAIR), lambda p, b: (b, A_WIDTH // PAIR + p)), nat, nat, tbl]
        + [ANY] * (nt + 1) + [vm] * ns,
        out_specs=[ANY, tbl] + [ANY] * nt + [vm] * ns,
        out_shape=[jax.ShapeDtypeStruct(dproj.shape, dproj.dtype),
                   jax.ShapeDtypeStruct((N_CFG, B_HEADS, 2 * Q_BLOCK, Q_BLOCK), F32)]
        + [jax.ShapeDtypeStruct((3,) + p.shape[1:], p.dtype) for p in parts]
        + [jax.ShapeDtypeStruct(a.shape, F32) for a in smalls],
        input_output_aliases={7: 0},
        scratch_shapes=[acc, acc, acc, acc, pltpu.VMEM((3, SEQ, PAIR), BF16), pltpu.SemaphoreType.DMA((3,))]
        + _sem_pair(3 * nt) + _SmallAllReducePlan.scratch(smalls),
        compiler_params=_params(("arbitrary", "arbitrary")), name="attn_bwd",
    )(qkv, qkv, qkv, dmix, o, lse, bias_t, dproj, *parts, *smalls)
    return res[0], res[1], list(res[2:2 + nt]), list(res[2 + nt:])


def _rel_bias_grad(ds, buckets_np, grads):
    present = _present_buckets(buckets_np)
    nx = len(grads)
    shapes = [g.shape for g in grads]

    def body(bk_ref, ds_ref, *rest):
        o_ref = rest[nx]
        acc_ref = rest[2 * nx + 1]
        exchange = _SiblingExchangePlan(shapes, rest[:nx], rest[nx + 1:2 * nx + 1], *rest[2 * nx + 2:])
        exchange.start()
        acc_ref[...] = jnp.zeros_like(acc_ref)
        for c in range(N_CFG):
            bk = bk_ref[c]
            for h in range(B_HEADS):
                dsv = ds_ref[c, h]
                for b in present[c]:
                    part = jnp.sum(jnp.where(bk == b, dsv, 0.0), axis=0, keepdims=True)
                    acc_ref[pl.ds(h * NUM_BUCKETS + b, 1), :] += part
        o_ref[...] = jnp.sum(acc_ref[...], axis=1, keepdims=True)
        exchange.finish()

    vm = pl.BlockSpec(memory_space=pltpu.VMEM)
    res = pl.pallas_call(
        body, in_specs=[vm, vm] + [ANY] * nx, out_specs=[vm] + [ANY] * nx,
        out_shape=[jax.ShapeDtypeStruct((B_HEADS * NUM_BUCKETS, 1), F32)]
        + [jax.ShapeDtypeStruct((N_SHARD, s[1] // 2, s[2]), F32) for s in shapes],
        scratch_shapes=[pltpu.VMEM((B_HEADS * NUM_BUCKETS, buckets_np.shape[-1]), F32)] + _sem_pair(nx),
        compiler_params=_params(), name="rel_bias_grad",
    )(jnp.asarray(buckets_np), ds, *grads)
    return res[0], list(res[1:])


def _row_index():
    return lax.broadcasted_iota(jnp.int32, (SEQ, LANE_BLOCK), 0)


def _shift_down(x, k, row):
    return jnp.where(row >= k, pltpu.roll(x, k, 0), 0.0)


def _shift_up(x, k, row):
    return jnp.where(row < SEQ - k, pltpu.roll(x, SEQ - k, 0), 0.0)


def _convgate_fwd(gate, up, conv_w, conv_b, batch):
    m = gate.shape[0]

    def body(g_ref, u_ref, w_ref, b_ref, a_ref):
        g = g_ref[...].astype(F32)
        w = w_ref[...]
        row = _row_index()
        c = b_ref[...] + w[0:1] * _shift_down(g, 2, row) + w[1:2] * _shift_down(g, 1, row) + w[2:3] * g
        a_ref[...] = (_gelu(c) * u_ref[...].astype(F32)).astype(BF16)

    blk = pl.BlockSpec((SEQ, LANE_BLOCK), lambda b, j: (b, j))
    return pl.pallas_call(
        body, grid=(batch, D_FF // LANE_BLOCK),
        in_specs=[blk, blk, pl.BlockSpec((3, LANE_BLOCK), lambda b, j: (0, j)),
                  pl.BlockSpec((1, LANE_BLOCK), lambda b, j: (0, j))],
        out_specs=blk,
        out_shape=jax.ShapeDtypeStruct((m, D_FF), BF16),
        compiler_params=_params(("parallel", "parallel")), name="convgate_fwd",
    )(gate, up, conv_w, conv_b)


def _convgate_bwd(gate, up, dact, conv_w, conv_b, batch):
    m = gate.shape[0]

    def body(g_ref, u_ref, da_ref, w_ref, b_ref, dg_ref, du_ref, dw_ref, db_ref):
        @pl.when(pl.program_id(1) == 0)
        def _():
            dw_ref[...] = jnp.zeros_like(dw_ref)
            db_ref[...] = jnp.zeros_like(db_ref)

        g = g_ref[...].astype(F32)
        w = w_ref[...]
        row = _row_index()
        g1 = _shift_down(g, 1, row)
        g2 = _shift_down(g, 2, row)
        c = b_ref[...] + w[0:1] * g2 + w[1:2] * g1 + w[2:3] * g
        gg, dgg = _gelu_and_grad(c)
        da = da_ref[...].astype(F32)
        du_ref[...] = (da * gg).astype(BF16)
        dc = da * u_ref[...].astype(F32) * dgg
        db_ref[...] += jnp.sum(dc, axis=0, keepdims=True)
        dw_ref[0:1, :] += jnp.sum(dc * g2, axis=0, keepdims=True)
        dw_ref[1:2, :] += jnp.sum(dc * g1, axis=0, keepdims=True)
        dw_ref[2:3, :] += jnp.sum(dc * g, axis=0, keepdims=True)
        dg_ref[...] = (w[2:3] * dc + w[1:2] * _shift_up(dc, 1, row) + w[0:1] * _shift_up(dc, 2, row)).astype(BF16)

    blk = pl.BlockSpec((SEQ, LANE_BLOCK), lambda j, b: (b, j))
    wspec = pl.BlockSpec((3, LANE_BLOCK), lambda j, b: (0, j))
    bspec = pl.BlockSpec((1, LANE_BLOCK), lambda j, b: (0, j))
    return pl.pallas_call(
        body, grid=(D_FF // LANE_BLOCK, batch),
        in_specs=[blk, blk, blk, wspec, bspec],
        out_specs=[blk, blk, wspec, bspec],
        out_shape=[jax.ShapeDtypeStruct((m, D_FF), BF16), jax.ShapeDtypeStruct((m, D_FF), BF16),
                   jax.ShapeDtypeStruct((3, D_FF), F32), jax.ShapeDtypeStruct((1, D_FF), F32)],
        compiler_params=_params(("parallel", "arbitrary")), name="convgate_bwd",
    )(gate, up, dact, conv_w, conv_b)


def _gather_weights(shards, conv_w_shard, rel_bias, buckets_np):
    nt = len(shards)
    shapes = [sh.shape for sh in shards]
    ts = list(range(nt))
    tables = _bias_tables_body(buckets_np)

    def body(*refs):
        shard_refs = refs[:nt]
        cw_ref, rb_ref, bk_ref = refs[nt:nt + 3]
        out_refs = refs[nt + 3:2 * nt + 3]
        cw_out, bias_ref, bias_t_ref = refs[2 * nt + 3:2 * nt + 6]
        send_sems, recv_sems, cw_send, cw_recv = refs[2 * nt + 6:]
        plan = _GatherPlan(shapes, shard_refs, out_refs, send_sems, recv_sems)
        x, y, c, chips = _mesh_pos()

        def cw_copy(j, src, dst, chip):
            return pltpu.make_async_remote_copy(src_ref=src, dst_ref=dst, send_sem=cw_send.at[j],
                                                recv_sem=cw_recv.at[j], device_id=(*chip, c), device_id_type=MESH)

        plan.start(ts)
        cw_sends = [cw_copy(j, cw_ref, cw_out.at[2 * x + y], chip) for j, chip in enumerate(chips)]
        for cp in cw_sends:
            cp.start()
        tables(rb_ref, bk_ref, bias_ref, bias_t_ref)
        plan.forward(ts)
        for j, chip in enumerate(chips):
            dst = cw_out.at[2 * chip[0] + chip[1]]
            cw_copy(j, dst, dst, chip).wait_recv()
        plan.finish(ts)
        for cp in cw_sends:
            cp.wait_send()

    out_shape = [jax.ShapeDtypeStruct((N_SHARD,) + sh.shape, sh.dtype) for sh in shards]
    out_shape.append(jax.ShapeDtypeStruct((N_SHARD,) + conv_w_shard.shape, conv_w_shard.dtype))
    out_shape += [jax.ShapeDtypeStruct((N_CFG, B_HEADS, Q_BLOCK, 2 * Q_BLOCK), F32),
                  jax.ShapeDtypeStruct((N_CFG, B_HEADS, 2 * Q_BLOCK, Q_BLOCK), F32)]
    vm = pl.BlockSpec(memory_space=pltpu.VMEM)
    res = pl.pallas_call(
        body, in_specs=[ANY] * (nt + 1) + [pl.BlockSpec(memory_space=pltpu.SMEM), vm],
        out_specs=[ANY] * (nt + 1) + [vm, vm], out_shape=out_shape,
        scratch_shapes=_sem_pair(6 * nt) + _sem_pair(3),
        compiler_params=pltpu.CompilerParams(has_side_effects=True, vmem_limit_bytes=VMEM_LIMIT),
        name="gather_weights",
    )(*shards, conv_w_shard, rel_bias.T, jnp.asarray(buckets_np))
    return list(res[:nt + 1]), res[nt + 1], res[nt + 2]


def _add_halves(g, recv, c_idx):
    _, rows2, cols = g.shape
    rows = rows2 // 2
    tr = rows
    nblk = rows // tr

    def body(c_ref, g_ref, r_ref, o_ref):
        o_ref[...] = (g_ref[...] + r_ref[...]).astype(BF16)

    return pl.pallas_call(
        body,
        grid_spec=pltpu.PrefetchScalarGridSpec(
            num_scalar_prefetch=1, grid=(N_SHARD, nblk),
            in_specs=[pl.BlockSpec((None, tr, cols), lambda s, i, c: (s, c[0] * nblk + i, 0)),
                      pl.BlockSpec((None, tr, cols), lambda s, i, c: (s, i, 0))],
            out_specs=pl.BlockSpec((None, tr, cols), lambda s, i, c: (s, i, 0))),
        out_shape=jax.ShapeDtypeStruct((N_SHARD, rows, cols), BF16),
        compiler_params=_params(("parallel", "parallel")), name="rs_add_halves",
    )(c_idx, g, recv)


def _add_chips(part, recv, s_idx, c_idx):
    _, rows, cols = part.shape
    tr = rows
    nblk = rows // tr

    def body(idx_ref, p_ref, r_ref, o_ref):
        acc = p_ref[...].astype(F32)
        for j in range(3):
            acc = acc + r_ref[j].astype(F32)
        o_ref[...] = acc

    return pl.pallas_call(
        body,
        grid_spec=pltpu.PrefetchScalarGridSpec(
            num_scalar_prefetch=1, grid=(nblk,),
            in_specs=[pl.BlockSpec((None, tr, cols), lambda i, idx: (idx[0], i, 0)),
                      pl.BlockSpec((3, tr, cols), lambda i, idx: (0, i, 0))],
            out_specs=pl.BlockSpec((tr, cols), lambda i, idx: (idx[1] * nblk + i, 0))),
        out_shape=jax.ShapeDtypeStruct((2 * rows, cols), F32),
        compiler_params=_params(("parallel",)), name="rs_add_chips",
    )(jnp.concatenate([s_idx, c_idx]), part, recv)


def _finish_reductions(fulls, arrays):
    nt, n = len(fulls), len(arrays)

    def body(*refs):
        in_refs = refs[nt:nt + n]
        full_refs, out_refs = refs[nt + n:2 * nt + n], refs[2 * nt + n:2 * nt + 2 * n]
        pos = 2 * nt + 2 * n
        share_send, share_recv = refs[pos + 2 * n:pos + 2 * n + 2]
        allreduce = _SmallAllReducePlan(in_refs, out_refs, refs[pos:pos + n], refs[pos + n:pos + 2 * n],
                                        *refs[pos + 2 * n + 2:])
        x, y, c, _ = _mesh_pos()

        def half(t, which):
            rows = fulls[t].shape[0] // 2
            return full_refs[t].at[pl.ds(which * rows, rows), :]

        def share(t, which):
            return pltpu.make_async_remote_copy(
                src_ref=half(t, which), dst_ref=half(t, which), send_sem=share_send.at[t],
                recv_sem=share_recv.at[t], device_id=(x, y, 1 - c), device_id_type=MESH)

        for t in range(nt):
            share(t, c).start()
        allreduce.start_sibling()
        allreduce.sum_sibling_and_start_chips()
        allreduce.finish()
        for t in range(nt):
            share(t, 1 - c).wait_recv()
        for t in range(nt):
            share(t, c).wait_send()

    vm = pl.BlockSpec(memory_space=pltpu.VMEM)
    res = pl.pallas_call(
        body, in_specs=[ANY] * nt + [vm] * n, out_specs=[ANY] * nt + [vm] * n,
        out_shape=[jax.ShapeDtypeStruct(f.shape, f.dtype) for f in fulls]
        + [jax.ShapeDtypeStruct(a.shape, F32) for a in arrays],
        input_output_aliases={t: t for t in range(nt)},
        scratch_shapes=[pltpu.VMEM(a.shape, F32) for a in arrays] + [pltpu.VMEM((3,) + a.shape, F32) for a in arrays]
        + _sem_pair(nt) + _sem_pair(4 * n),
        compiler_params=pltpu.CompilerParams(has_side_effects=True),
        name="finish_reductions",
    )(*fulls, *arrays)
    return list(res[:nt]), list(res[nt:])


def _from_col_shards(g):
    n, rows, cols = g.shape
    return g.transpose(1, 0, 2).reshape(rows, n * cols)


def _train_step(x, tgt, g1, g2, g3, g4, shards, ln_g, ln_b, w_s, b_s, rel_bias, conv_w_shard, conv_b, batch,
                s_idx, c_idx):
    buckets = _bucket_tables()
    bz = jnp.repeat(b_s.T, HEAD_DIM, axis=1)
    w_st = jnp.swapaxes(w_s, 1, 2)

    def with_own(gathered, own):
        return lax.dynamic_update_index_in_dim(gathered, own, s_idx[0], 0)

    def shard_major(g):
        return g.reshape(N_SHARD, g.shape[0] // N_SHARD, D_MODEL)

    (g_in, g_convw), bias, bias_t = _gather_weights([shards["w_in"]], conv_w_shard, rel_bias, buckets)
    w_in_t = with_own(g_in, shards["w_in"]).reshape(IN_COLS, D_MODEL)
    conv_w = _from_col_shards(with_own(g_convw, conv_w_shard))

    h1, uv, qkv = _proj_fwd(x, g1, w_in_t)
    a = _gate_fwd(uv, ln_g, ln_b, w_s, bz)
    later = ["w_out", "w_gate", "w_up"]
    o_bf, lse, gathered = _attn_fwd(qkv, bias, batch, [shards[n] for n in later])
    g_out, g_gate, g_up = [with_own(g, shards[n]) for g, n in zip(gathered, later)]
    w_out = g_out.reshape(D_MODEL, D_MODEL)
    w_gate_t = g_gate.reshape(D_FF, D_MODEL)
    w_up_t = g_up.reshape(D_FF, D_MODEL)
    (y1, x1, h2), _ = _fused_rows(
        "out_proj_mid_fwd", 512,
        [(a, w_out, "nn", slice(0, A_WIDTH)), (o_bf, w_out, "nn", slice(A_WIDTH, D_MODEL))],
        [x], [g2, g3], _mid_fwd_rows, [F32, F32, BF16], [])
    gate, up, g_down = _mm_pair_nt(h2, w_gate_t, w_up_t, shards["w_down"], tm=1024, tn=1408, out_dtype=BF16,
                                   name="mm_gate_up")
    w_down = with_own(g_down, shards["w_down"]).reshape(D_FF, D_MODEL)
    act = _convgate_fwd(gate, up, conv_w, conv_b, batch)
    (dx2, dy2, dg4, loss), _ = _fused_rows(
        "down_proj_loss_head", 512, [(act, w_down, "nn", None)], [x1, tgt], [g4], _loss_head_rows,
        [F32, BF16], [(1, D_MODEL), (1, 128)])

    dact = _mm(dy2, w_down, dims="nt", tm=1024, tn=1408, tk=1024, out_dtype=BF16, name="mm_dact")
    dw_down = _mm(act, dy2, dims="tn", tm=1408, tn=1024, tk=1024, out_dtype=F32, name="mm_dw_down")
    dgate, dup, dconv_w, dconv_b = _convgate_bwd(gate, up, dact, conv_w, conv_b, batch)
    (dx1, dy1, dg2, dg3), _ = _fused_rows(
        "dh2_mid_bwd", 256, [(dgate, w_gate_t, "nn", None), (dup, w_up_t, "nn", None)],
        [x1, y1, dx2], [g2, g3], _mid_bwd_rows, [F32, BF16], [(1, D_MODEL), (1, D_MODEL)])
    dw_gate_t = _mm(dgate, h2, dims="tn", tm=1408, tn=1024, tk=1024, out_dtype=F32, name="mm_dw_gate")
    dw_up_t = _mm(dup, h2, dims="tn", tm=1408, tn=1024, tk=1024, out_dtype=F32, name="mm_dw_up")
    dmix, dw_out = _out_proj_bwd(a, o_bf, dy1, w_out)

    done = [shard_major(g) for g in (dw_down, dw_gate_t, dw_up_t, dw_out)]
    (dproj, dln_g, dln_b, dw_s, dbz), recv_a = _gate_bwd(uv, dmix, ln_g, ln_b, w_s, w_st, bz, done)
    parts = [_add_halves(g, r, c_idx) for g, r in zip(done, recv_a)]
    early = dict(loss=loss, norm_mix_post=dg2, norm_ffn_pre=dg3, norm_ffn_post=dg4, ln_v_gain=dln_g,
                 ln_v_bias=dln_b, spatial_w=dw_s, spatial_b=dbz, conv_w=dconv_w, conv_b=dconv_b)
    dproj, ds, recv, early_sums = _attn_bwd(qkv, dmix, o_bf, lse, bias_t, dproj, batch, parts, list(early.values()))
    fulls = [_add_chips(p, r, s_idx, c_idx) for p, r in zip(parts, recv)]
    dw_in_t = _mm(dproj, h1, dims="tn", tm=1408, tn=1024, tk=1024, out_dtype=F32, name="mm_dw_in")
    last = [shard_major(dw_in_t)]
    drel, recv_in_a = _rel_bias_grad(ds, np.ascontiguousarray(np.swapaxes(buckets, 1, 2)), last)
    part_in = [_add_halves(g, r, c_idx) for g, r in zip(last, recv_in_a)]
    (dx0, dg1), recv_in = _fused_rows(
        "dh1_in_bwd", 512, [(dproj, w_in_t, "nn", None)], [x, dx1], [g1], _in_bwd_rows,
        [F32], [(1, D_MODEL)], exchange=part_in)
    fulls += [_add_chips(p, r, s_idx, c_idx) for p, r in zip(part_in, recv_in)]
    half_reduced = dict(zip(["w_down", "w_gate", "w_up", "w_out", "w_in"], fulls))

    return dx0, dict(zip(early, early_sums)), dict(norm_mix_pre=dg1, rel_bias=drel), half_reduced


def _adamw_update(w, g, m, v):
    nm = ADAM_B1 * m + (1.0 - ADAM_B1) * g
    nv = ADAM_B2 * v + (1.0 - ADAM_B2) * (g * g)
    m_hat = nm / (1.0 - ADAM_B1 ** ADAM_STEP)
    v_hat = nv / (1.0 - ADAM_B2 ** ADAM_STEP)
    return -ADAM_LR * (m_hat / (jnp.sqrt(v_hat) + ADAM_EPS) + ADAM_WD * w), nm, nv


def _adamw(w, g, m, v, name):
    rows, cols = w.shape
    tr = next(cand for cand in (352, 256, 128) if rows % cand == 0)

    def body(w_ref, g_ref, m_ref, v_ref, go_ref, d_ref, nm_ref, nv_ref):
        gv = g_ref[...]
        go_ref[...] = gv
        d_ref[...], nm_ref[...], nv_ref[...] = _adamw_update(w_ref[...], gv, m_ref[...], v_ref[...])

    spec = pl.BlockSpec((tr, cols), lambda i: (i, 0))
    sds = jax.ShapeDtypeStruct((rows, cols), F32)
    return pl.pallas_call(
        body, grid=(rows // tr,), in_specs=[spec] * 4, out_specs=[spec] * 4, out_shape=[sds] * 4,
        compiler_params=_params(("parallel",)), name=name,
    )(w, g, m, v)


def _adamw_small(ws, gs, ms, vs):
    n = len(ws)

    def body(*refs):
        w_refs, g_refs, m_refs, v_refs = refs[:n], refs[n:2 * n], refs[2 * n:3 * n], refs[3 * n:4 * n]
        d_refs, nm_refs, nv_refs = refs[4 * n:5 * n], refs[5 * n:6 * n], refs[6 * n:7 * n]
        for t in range(n):
            d_refs[t][...], nm_refs[t][...], nv_refs[t][...] = _adamw_update(
                w_refs[t][...], g_refs[t][...], m_refs[t][...], v_refs[t][...])

    vm = pl.BlockSpec(memory_space=pltpu.VMEM)
    sds = [jax.ShapeDtypeStruct(w.shape, F32) for w in ws]
    res = pl.pallas_call(
        body, in_specs=[vm] * (4 * n), out_specs=[vm] * (3 * n), out_shape=sds * 3,
        compiler_params=_params(), name="adamw_small",
    )(*ws, *gs, *ms, *vs)
    return res[:n], res[n:2 * n], res[2 * n:]


SMALL = ["norm_mix_pre", "norm_mix_post", "norm_ffn_pre", "norm_ffn_post", "ln_v_gain", "ln_v_bias",
         "spatial_w", "spatial_b", "rel_bias", "conv_b"]
LARGE = ["w_in", "w_gate", "w_up", "w_down", "w_out"]
TRANSPOSED = ("w_in", "w_gate", "w_up")
ORDER = ["norm_mix_pre", "norm_mix_post", "norm_ffn_pre", "norm_ffn_post", "w_in", "ln_v_gain", "ln_v_bias",
         "spatial_w", "spatial_b", "rel_bias", "w_out", "w_gate", "w_up", "conv_w", "conv_b", "w_down"]


def kernel(x, norm_mix_pre, norm_mix_post, norm_ffn_pre, norm_ffn_post, w_in, ln_v_gain, ln_v_bias, spatial_w, spatial_b, rel_bias, w_out, w_gate, w_up, conv_w, conv_b, w_down, loss_target, m_norm_mix_pre, m_norm_mix_post, m_norm_ffn_pre, m_norm_ffn_post, m_w_in, m_ln_v_gain, m_ln_v_bias, m_spatial_w, m_spatial_b, m_rel_bias, m_w_out, m_w_gate, m_w_up, m_conv_w, m_conv_b, m_w_down, v_norm_mix_pre, v_norm_mix_post, v_norm_ffn_pre, v_norm_ffn_post, v_w_in, v_ln_v_gain, v_ln_v_bias, v_spatial_w, v_spatial_b, v_rel_bias, v_w_out, v_w_gate, v_w_up, v_conv_w, v_conv_b, v_w_down):
    params = dict(norm_mix_pre=norm_mix_pre, norm_mix_post=norm_mix_post, norm_ffn_pre=norm_ffn_pre,
                  norm_ffn_post=norm_ffn_post, w_in=w_in, ln_v_gain=ln_v_gain, ln_v_bias=ln_v_bias,
                  spatial_w=spatial_w, spatial_b=spatial_b, rel_bias=rel_bias, w_out=w_out, w_gate=w_gate,
                  w_up=w_up, conv_w=conv_w, conv_b=conv_b, w_down=w_down)
    mom = dict(norm_mix_pre=m_norm_mix_pre, norm_mix_post=m_norm_mix_post, norm_ffn_pre=m_norm_ffn_pre,
               norm_ffn_post=m_norm_ffn_post, w_in=m_w_in, ln_v_gain=m_ln_v_gain, ln_v_bias=m_ln_v_bias,
               spatial_w=m_spatial_w, spatial_b=m_spatial_b, rel_bias=m_rel_bias, w_out=m_w_out, w_gate=m_w_gate,
               w_up=m_w_up, conv_w=m_conv_w, conv_b=m_conv_b, w_down=m_w_down)
    var = dict(norm_mix_pre=v_norm_mix_pre, norm_mix_post=v_norm_mix_post, norm_ffn_pre=v_norm_ffn_pre,
               norm_ffn_post=v_norm_ffn_post, w_in=v_w_in, ln_v_gain=v_ln_v_gain, ln_v_bias=v_ln_v_bias,
               spatial_w=v_spatial_w, spatial_b=v_spatial_b, rel_bias=v_rel_bias, w_out=v_w_out, w_gate=v_w_gate,
               w_up=v_w_up, conv_w=v_conv_w, conv_b=v_conv_b, w_down=v_w_down)

    batch = x.shape[0]
    xi, yi, ci = lax.axis_index("x"), lax.axis_index("y"), lax.axis_index("c")
    s_idx = (2 * xi + yi).astype(jnp.int32).reshape(1)
    c_idx = ci.astype(jnp.int32).reshape(1)

    def local(a, n):
        return jnp.swapaxes(a[0], 0, 1) if n in TRANSPOSED else a[0]

    shards = {n: local(params[n], n).astype(BF16) for n in LARGE}
    dx0, total, partial, half_reduced = _train_step(
        x.reshape(batch * SEQ, D_MODEL), loss_target.reshape(batch * SEQ, D_MODEL),
        norm_mix_pre, norm_mix_post, norm_ffn_pre, norm_ffn_post, shards,
        ln_v_gain.reshape(1, A_WIDTH), ln_v_bias.reshape(1, A_WIDTH), spatial_w[0], spatial_b[0], rel_bias,
        conv_w[0], conv_b, batch, s_idx, c_idx)
    grad_x = dx0.reshape(batch, SEQ, D_MODEL)

    names = list(partial)
    fulls, sums = _finish_reductions([half_reduced[n] for n in LARGE], [partial[n] for n in names])
    reduced = dict(zip(LARGE, fulls))
    total.update(zip(names, sums))
    loss = total["loss"][0, 0]
    total["spatial_b"] = total["spatial_b"][:, ::HEAD_DIM].T
    total["rel_bias"] = total["rel_bias"].reshape(B_HEADS, NUM_BUCKETS)
    total["conv_w"] = lax.dynamic_slice_in_dim(total["conv_w"], s_idx[0] * SHARD_FF, SHARD_FF, axis=1)
    small_names = SMALL + ["conv_w"]

    def small(a, n):
        return a.T if n == "rel_bias" else a

    for n in small_names:
        reduced[n] = total[n].reshape(small(params[n], n).shape)

    out_g, out_d, out_m, out_v = {}, {}, {}, {}
    for n in LARGE:
        res = _adamw(local(params[n], n), reduced[n], local(mom[n], n), local(var[n], n), name=f"adamw_{n}")
        if n in TRANSPOSED:
            res = [jnp.swapaxes(r, 0, 1) for r in res]
        out_g[n], out_d[n], out_m[n], out_v[n] = [r[None] for r in res]
    d, nm, nv = _adamw_small([small(params[n], n) for n in small_names], [reduced[n] for n in small_names],
                             [small(mom[n], n) for n in small_names], [small(var[n], n) for n in small_names])
    for n, dd, mm, vv in zip(small_names, d, nm, nv):
        out_g[n], out_d[n], out_m[n], out_v[n] = [small(r, n) for r in (reduced[n], dd, mm, vv)]

    return (loss, grad_x, *[out_g[n] for n in ORDER], *[out_d[n] for n in ORDER],
            *[out_m[n] for n in ORDER], *[out_v[n] for n in ORDER])
```

```python
import functools
import math

import numpy as np
import jax
import jax.numpy as jnp
from jax import lax
from jax.experimental import pallas as pl
from jax.experimental.pallas import tpu as pltpu

F32 = jnp.float32
BF16 = jnp.bfloat16
MESH = pl.DeviceIdType.MESH

D_MODEL = 1024
SEQ = 2048
HEAD_DIM = 64
A_GROUPS = 4
A_WIDTH = 256
B_HEADS = 12
B_WIDTH = 768
CHUNK = 128
DILATED = ((128, 1), (512, 4), (2048, 16))
NUM_BUCKETS = 32
MAX_DISTANCE = 2048
D_FF = 2816
IN_COLS = 2816
NORM_EPS = 1e-6
NEG_INF = -1e30
N_SHARD = 4
SHARD_FF = D_FF // N_SHARD
LANE_BLOCK = 256
VMEM_LIMIT = 56 * 1024 * 1024

ADAM_LR = 0.001
ADAM_B1 = 0.9
ADAM_B2 = 0.999
ADAM_EPS = 1e-08
ADAM_WD = 0.01
ADAM_STEP = 10

GELU_C = math.sqrt(2.0 / math.pi)
GELU_A = 0.044715

ANY = pl.BlockSpec(memory_space=pl.ANY)


def _params(sem=None):
    return pltpu.CompilerParams(dimension_semantics=sem, vmem_limit_bytes=VMEM_LIMIT)


def _dot(a, b, precision=None):
    return jnp.dot(a, b, preferred_element_type=F32, precision=precision)


def _dot_nt(a, b, precision=None):
    return lax.dot_general(a, b, (((1,), (1,)), ((), ())), preferred_element_type=F32, precision=precision)


def _dot_tn(a, b):
    return lax.dot_general(a, b, (((0,), (0,)), ((), ())), preferred_element_type=F32)


def _gelu(x):
    t = jnp.tanh(x * (GELU_C + (GELU_C * GELU_A) * (x * x)))
    return (0.5 * x) * (1.0 + t)


def _gelu_and_grad(x):
    x2 = x * x
    u = 1.0 + jnp.tanh(x * (GELU_C + (GELU_C * GELU_A) * x2))
    hx = 0.5 * x
    dg = u * (0.5 + hx * (2.0 - u) * (GELU_C + (3.0 * GELU_C * GELU_A) * x2))
    return hx * u, dg


def _mesh_pos():
    x, y, c = lax.axis_index("x"), lax.axis_index("y"), lax.axis_index("c")
    chips = [(1 - x, y), (x, 1 - y), (1 - x, 1 - y)]
    return x, y, c, chips


class _GatherPlan:
    def __init__(self, shapes, shard_refs, out_refs, send_sems, recv_sems):
        self.shapes, self.shard_refs, self.out_refs = shapes, shard_refs, out_refs
        self.send_sems, self.recv_sems = send_sems, recv_sems
        self.x, self.y, self.c, self.chips = _mesh_pos()
        self.sib = (self.x, self.y, 1 - self.c)

    def _half(self, t, chip, which):
        rows = self.shapes[t][0] // 2
        return self.out_refs[t].at[2 * chip[0] + chip[1], pl.ds(which * rows, rows), :]

    def _copy(self, k, src, dst, to):
        return pltpu.make_async_remote_copy(src_ref=src, dst_ref=dst, send_sem=self.send_sems.at[k],
                                            recv_sem=self.recv_sems.at[k], device_id=to, device_id_type=MESH)

    def _sends(self, t):
        rows = self.shapes[t][0] // 2
        src = self.shard_refs[t].at[pl.ds(self.c * rows, rows), :]
        return [self._copy(6 * t + j, src, self._half(t, (self.x, self.y), self.c), (*chip, self.c))
                for j, chip in enumerate(self.chips)]

    def _forwards(self, t):
        return [self._copy(6 * t + 3 + j, self._half(t, chip, self.c), self._half(t, chip, self.c), self.sib)
                for j, chip in enumerate(self.chips)]

    def start(self, ts):
        for t in ts:
            for cp in self._sends(t):
                cp.start()

    def forward(self, ts):
        for t in ts:
            for j, chip in enumerate(self.chips):
                landed = self._half(t, chip, self.c)
                self._copy(6 * t + j, landed, landed, (*chip, self.c)).wait_recv()
            for cp in self._forwards(t):
                cp.start()

    def finish(self, ts):
        for t in ts:
            for j, chip in enumerate(self.chips):
                other = self._half(t, chip, 1 - self.c)
                self._copy(6 * t + 3 + j, other, other, self.sib).wait_recv()
        for t in ts:
            for cp in self._sends(t) + self._forwards(t):
                cp.wait_send()


class _SiblingExchangePlan:
    def __init__(self, shapes, grad_refs, out_refs, send_sems, recv_sems):
        self.shapes, self.grad_refs, self.out_refs = shapes, grad_refs, out_refs
        self.send_sems, self.recv_sems = send_sems, recv_sems
        self.x, self.y, self.c, _ = _mesh_pos()

    def _copies(self):
        out = []
        for t, (g, o) in enumerate(zip(self.grad_refs, self.out_refs)):
            rows = self.shapes[t][1] // 2
            out.append(pltpu.make_async_remote_copy(
                src_ref=g.at[:, pl.ds((1 - self.c) * rows, rows), :], dst_ref=o, send_sem=self.send_sems.at[t],
                recv_sem=self.recv_sems.at[t], device_id=(self.x, self.y, 1 - self.c), device_id_type=MESH))
        return out

    def start(self):
        for cp in self._copies():
            cp.start()

    def finish(self):
        for cp in self._copies():
            cp.wait()


class _ChipExchangePlan:
    def __init__(self, part_refs, out_refs, send_sems, recv_sems):
        self.part_refs, self.out_refs, self.send_sems, self.recv_sems = part_refs, out_refs, send_sems, recv_sems
        _, _, self.c, self.chips = _mesh_pos()

    def _copies(self):
        return [pltpu.make_async_remote_copy(
            src_ref=p.at[2 * chip[0] + chip[1]], dst_ref=o.at[j], send_sem=self.send_sems.at[3 * t + j],
            recv_sem=self.recv_sems.at[3 * t + j], device_id=(*chip, self.c), device_id_type=MESH)
            for t, (p, o) in enumerate(zip(self.part_refs, self.out_refs)) for j, chip in enumerate(self.chips)]

    def start(self):
        for cp in self._copies():
            cp.start()

    def finish(self):
        for cp in self._copies():
            cp.wait()


class _SmallAllReducePlan:
    def __init__(self, in_refs, out_refs, sib_refs, chip_refs, send_sems, recv_sems):
        self.in_refs, self.out_refs, self.sib_refs, self.chip_refs = in_refs, out_refs, sib_refs, chip_refs
        self.send_sems, self.recv_sems = send_sems, recv_sems
        self.n = len(in_refs)
        self.x, self.y, self.c, self.chips = _mesh_pos()

    def _copy(self, k, src, dst, to):
        return pltpu.make_async_remote_copy(src_ref=src, dst_ref=dst, send_sem=self.send_sems.at[k],
                                            recv_sem=self.recv_sems.at[k], device_id=to, device_id_type=MESH)

    def _first(self):
        return [self._copy(t, self.in_refs[t], self.sib_refs[t], (self.x, self.y, 1 - self.c)) for t in range(self.n)]

    def _second(self):
        return [self._copy(self.n + 3 * t + j, self.out_refs[t], self.chip_refs[t].at[j], (*chip, self.c))
                for t in range(self.n) for j, chip in enumerate(self.chips)]

    def start_sibling(self):
        for cp in self._first():
            cp.start()

    def sum_sibling_and_start_chips(self):
        for cp in self._first():
            cp.wait()
        for t in range(self.n):
            self.out_refs[t][...] = self.in_refs[t][...] + self.sib_refs[t][...]
        for cp in self._second():
            cp.start()

    def finish(self):
        for cp in self._second():
            cp.wait()
        for t in range(self.n):
            self.out_refs[t][...] = ((self.out_refs[t][...] + self.chip_refs[t][0])
                                     + (self.chip_refs[t][1] + self.chip_refs[t][2]))

    @staticmethod
    def scratch(arrays):
        return ([pltpu.VMEM(a.shape, F32) for a in arrays] + [pltpu.VMEM((3,) + a.shape, F32) for a in arrays]
                + _sem_pair(4 * len(arrays)))


def _sem_pair(n):
    return [pltpu.SemaphoreType.DMA((n,)), pltpu.SemaphoreType.DMA((n,))]


def _mm(a, b, *, dims, tm, tn, tk, out_dtype, name):
    if dims == "nn":
        m, k = a.shape
        n = b.shape[1]
        a_spec = pl.BlockSpec((tm, tk), lambda i, j, kk: (i, kk))
        b_spec = pl.BlockSpec((tk, tn), lambda i, j, kk: (kk, j))
        dot = _dot
    elif dims == "nt":
        m, k = a.shape
        n = b.shape[0]
        a_spec = pl.BlockSpec((tm, tk), lambda i, j, kk: (i, kk))
        b_spec = pl.BlockSpec((tn, tk), lambda i, j, kk: (j, kk))
        dot = _dot_nt
    else:
        k, m = a.shape
        n = b.shape[1]
        a_spec = pl.BlockSpec((tk, tm), lambda i, j, kk: (kk, i))
        b_spec = pl.BlockSpec((tk, tn), lambda i, j, kk: (kk, j))
        dot = _dot_tn
    assert m % tm == 0 and n % tn == 0 and k % tk == 0, (name, m, n, k)
    grid = (m // tm, n // tn, k // tk)
    nk = grid[2]
    assert nk == 1 or out_dtype == F32, name

    def body(a_ref, b_ref, o_ref):
        prod = dot(a_ref[...].astype(BF16), b_ref[...].astype(BF16))
        if nk == 1:
            o_ref[...] = prod.astype(out_dtype)
        else:
            kk = pl.program_id(2)

            @pl.when(kk == 0)
            def _():
                o_ref[...] = prod

            @pl.when(kk > 0)
            def _():
                o_ref[...] += prod

    return pl.pallas_call(
        body, grid=grid, in_specs=[a_spec, b_spec],
        out_specs=pl.BlockSpec((tm, tn), lambda i, j, kk: (i, j)),
        out_shape=jax.ShapeDtypeStruct((m, n), out_dtype),
        compiler_params=_params(("parallel", "parallel", "arbitrary")), name=name,
    )(a, b)


def _mm_pair_nt(a, w1_t, w2_t, shard, *, tm, tn, out_dtype, name):
    m, k = a.shape
    n = w1_t.shape[0]
    assert m % tm == 0 and n % tn == 0 and w2_t.shape == w1_t.shape, name
    grid = (m // tm, n // tn)
    n_steps = grid[0] * grid[1]

    def body(a_ref, w1_ref, w2_ref, shard_ref, o1_ref, o2_ref, gat_ref, send_sems, recv_sems):
        step = pl.program_id(0) * grid[1] + pl.program_id(1)
        gather = _GatherPlan([shard.shape], [shard_ref], [gat_ref], send_sems, recv_sems)

        @pl.when(step == 0)
        def _():
            gather.start([0])

        @pl.when(step == (2 * n_steps) // 3)
        def _():
            gather.forward([0])

        av = a_ref[...]
        o1_ref[...] = _dot_nt(av, w1_ref[...]).astype(out_dtype)
        o2_ref[...] = _dot_nt(av, w2_ref[...]).astype(out_dtype)

        @pl.when(step == n_steps - 1)
        def _():
            gather.finish([0])

    w_spec = pl.BlockSpec((tn, k), lambda i, j: (j, 0))
    o_spec = pl.BlockSpec((tm, tn), lambda i, j: (i, j))
    return pl.pallas_call(
        body, grid=grid,
        in_specs=[pl.BlockSpec((tm, k), lambda i, j: (i, 0)), w_spec, w_spec, ANY],
        out_specs=[o_spec, o_spec, ANY],
        out_shape=[jax.ShapeDtypeStruct((m, n), out_dtype)] * 2
        + [jax.ShapeDtypeStruct((N_SHARD,) + shard.shape, shard.dtype)],
        scratch_shapes=_sem_pair(6),
        compiler_params=_params(("arbitrary", "arbitrary")), name=name,
    )(a, w1_t, w2_t, shard)


def _out_proj_bwd(a, o, dy1, w_out):
    m = dy1.shape[0]
    tm = 1024

    def body(a_ref, o_ref, dy_ref, w_ref, dmix_ref, dw_ref):
        dy = dy_ref[...]
        dmix_ref[...] = _dot_nt(dy, w_ref[...])
        top = _dot_tn(a_ref[...], dy)
        bottom = _dot_tn(o_ref[...], dy)

        @pl.when(pl.program_id(0) == 0)
        def _():
            dw_ref[:A_WIDTH, :] = top
            dw_ref[A_WIDTH:, :] = bottom

        @pl.when(pl.program_id(0) > 0)
        def _():
            dw_ref[:A_WIDTH, :] += top
            dw_ref[A_WIDTH:, :] += bottom

    tile = lambda width: pl.BlockSpec((tm, width), lambda i: (i, 0))
    return pl.pallas_call(
        body, grid=(m // tm,),
        in_specs=[tile(A_WIDTH), tile(B_WIDTH), tile(D_MODEL), _full_spec((D_MODEL, D_MODEL))],
        out_specs=[tile(D_MODEL), _full_spec((D_MODEL, D_MODEL))],
        out_shape=[jax.ShapeDtypeStruct((m, D_MODEL), F32), jax.ShapeDtypeStruct((D_MODEL, D_MODEL), F32)],
        compiler_params=_params(("arbitrary",)), name="out_proj_bwd",
    )(a, o, dy1, w_out)


def _fused_rows(name, tm, mats, rows, vecs, fn, row_outs, acc_outs, exchange=()):
    m = mats[0][0].shape[0]
    nm, nr, nv, nro, nao, nx = len(mats), len(rows), len(vecs), len(row_outs), len(acc_outs), len(exchange)
    n_steps = m // tm

    def body(*refs):
        a_refs, w_refs = refs[:nm], refs[nm:2 * nm]
        pos = 2 * nm
        row_refs, vec_refs, part_refs = refs[pos:pos + nr], refs[pos + nr:pos + nr + nv], refs[pos + nr + nv:pos + nr + nv + nx]
        pos += nr + nv + nx
        out_refs, acc_refs, recv_refs = refs[pos:pos + nro], refs[pos + nro:pos + nro + nao], refs[pos + nro + nao:pos + nro + nao + nx]
        sems = refs[pos + nro + nao + nx:]
        i = pl.program_id(0)
        if nx:
            plan = _ChipExchangePlan(part_refs, recv_refs, *sems)

            @pl.when(i == 0)
            def _():
                plan.start()

        @pl.when(i == 0)
        def _():
            for r in acc_refs:
                r[...] = jnp.zeros_like(r)

        y = None
        for a_ref, w_ref, (_, _, dims, sl) in zip(a_refs, w_refs, mats):
            w = w_ref[...] if sl is None else w_ref[sl, :]
            part = (_dot if dims == "nn" else _dot_nt)(a_ref[...], w)
            y = part if y is None else y + part
        res = fn(y, *[r[...] for r in row_refs], *[v[...] for v in vec_refs])
        for r, val in zip(out_refs, res[:nro]):
            r[...] = val.astype(r.dtype)
        for r, val in zip(acc_refs, res[nro:]):
            r[...] += val

        if nx:
            @pl.when(i == n_steps - 1)
            def _():
                plan.finish()

    tile = lambda width: pl.BlockSpec((tm, width), lambda i: (i, 0))
    res = pl.pallas_call(
        body, grid=(n_steps,),
        in_specs=[tile(a.shape[1]) for a, _, _, _ in mats] + [_full_spec(w.shape) for _, w, _, _ in mats]
        + [tile(D_MODEL)] * nr + [_full_spec((1, D_MODEL))] * nv + [ANY] * nx,
        out_specs=[tile(D_MODEL)] * nro + [_full_spec(s) for s in acc_outs] + [ANY] * nx,
        out_shape=[jax.ShapeDtypeStruct((m, D_MODEL), dt) for dt in row_outs]
        + [jax.ShapeDtypeStruct(s, F32) for s in acc_outs]
        + [jax.ShapeDtypeStruct((3,) + p.shape[1:], p.dtype) for p in exchange],
        scratch_shapes=_sem_pair(3 * nx) if nx else [],
        compiler_params=_params(("arbitrary",)), name=name,
    )(*[a for a, _, _, _ in mats], *[w for _, w, _, _ in mats], *rows, *vecs, *exchange)
    return list(res[:nro + nao]), list(res[nro + nao:])


def _vec_spec(width=D_MODEL):
    return pl.BlockSpec((1, width), lambda i: (0, 0))


def _rstd(v):
    return lax.rsqrt(jnp.mean(v * v, axis=-1, keepdims=True) + NORM_EPS)


def _mid_fwd_rows(y1, x0, g2, g3):
    x1 = x0 + y1 * _rstd(y1) * g2
    return y1, x1, x1 * _rstd(x1) * g3


def _rms_bwd_rows(dout, v, g):
    r = _rstd(v)
    n = v * r
    dn = dout * g
    dv = r * (dn - n * jnp.mean(dn * n, axis=-1, keepdims=True))
    dg = jnp.sum(dout * n, axis=0, keepdims=True)
    return dv, dg


def _loss_head_rows(y2, x1, tgt, g4):
    x2 = x1 + y2 * _rstd(y2) * g4
    err = x2 - tgt
    loss = 0.5 * jnp.sum(jnp.mean(err * err, axis=-1, keepdims=True), axis=0, keepdims=True)
    dx2 = err * (1.0 / D_MODEL)
    dy2, dg4 = _rms_bwd_rows(dx2, y2, g4)
    return dx2, dy2, dg4, loss


def _mid_bwd_rows(dh2, x1, y1, dx2, g2, g3):
    d3, dg3 = _rms_bwd_rows(dh2, x1, g3)
    dx1 = dx2 + d3
    dy1, dg2 = _rms_bwd_rows(dx1, y1, g2)
    return dx1, dy1, dg2, dg3


def _in_bwd_rows(dh1, x0, dx1, g1):
    d1, dg1 = _rms_bwd_rows(dh1, x0, g1)
    return dx1 + d1, dg1


GATE_ROWS = 512


def _group_mean_matrix():
    p = np.zeros((A_WIDTH, A_WIDTH), np.float32)
    for g in range(A_GROUPS):
        p[g * HEAD_DIM:(g + 1) * HEAD_DIM, g * HEAD_DIM:(g + 1) * HEAD_DIM] = 1.0 / HEAD_DIM
    return jnp.asarray(p)


def _group_masks(width=A_WIDTH):
    lane = lax.broadcasted_iota(jnp.int32, (1, width), 1)
    return [(lane >= g * HEAD_DIM) & (lane < (g + 1) * HEAD_DIM) for g in range(width // HEAD_DIM)]


GROUP_SUM_PRECISION = lax.Precision.HIGH


def _layernorm_groups(vg, pavg):
    hi = GROUP_SUM_PRECISION
    mu = _dot(vg, pavg, hi)
    xc = vg - mu
    var = _dot(xc * xc, pavg, hi)
    rstd = lax.rsqrt(var + NORM_EPS)
    return xc * rstd, rstd


def _spatial_mix(w_bf, vn_chunk_bf, masks, bz):
    z = bz
    for g in range(A_GROUPS):
        z = z + jnp.where(masks[g], _dot(w_bf[g], vn_chunk_bf), 0.0)
    return z


def _full_spec(shape):
    return pl.BlockSpec(shape, lambda i: tuple(0 for _ in shape))


def _gate_fwd_rows(u, v, lg, lb, w_ref, bz, pavg, a_ref):
    masks = _group_masks()
    row = lax.broadcasted_iota(jnp.int32, (CHUNK, CHUNK), 0)
    col = lax.broadcasted_iota(jnp.int32, (CHUNK, CHUNK), 1)
    w_bf = [jnp.where(row >= col, w_ref[g], 0.0).astype(BF16) for g in range(A_GROUPS)]
    ug = _gelu(u)
    vhat, _ = _layernorm_groups(_gelu(v), pavg)
    vn = vhat * lg + lb
    for c in range(GATE_ROWS // CHUNK):
        sl = slice(c * CHUNK, (c + 1) * CHUNK)
        z = _spatial_mix(w_bf, vn[sl].astype(BF16), masks, bz)
        a_ref[sl, :] = (ug[sl] * z).astype(BF16)


def _gate_bwd(uv, dmix, ln_g, ln_b, w_s, w_st, bz, grads):
    m = uv.shape[0]
    pavg = _group_mean_matrix()
    nsteps = m // GATE_ROWS
    nx = len(grads)
    shapes = [g.shape for g in grads]

    def body(u_ref, v_ref, da_ref, lg_ref, lb_ref, w_ref, wt_ref, bz_ref, p_ref, *rest):
        grad_refs = rest[:nx]
        duv_ref, dlg_ref, dlb_ref, dw_ref, dbz_ref = rest[nx:nx + 5]
        recv_refs = rest[nx + 5:2 * nx + 5]
        exchange = _SiblingExchangePlan(shapes, grad_refs, recv_refs, *rest[2 * nx + 5:])
        i = pl.program_id(0)

        @pl.when(i == 0)
        def _():
            exchange.start()
            dlg_ref[...] = jnp.zeros_like(dlg_ref)
            dlb_ref[...] = jnp.zeros_like(dlb_ref)
            dw_ref[...] = jnp.zeros_like(dw_ref)
            dbz_ref[...] = jnp.zeros_like(dbz_ref)

        hi = GROUP_SUM_PRECISION
        masks = _group_masks()
        row = lax.broadcasted_iota(jnp.int32, (CHUNK, CHUNK), 0)
        col = lax.broadcasted_iota(jnp.int32, (CHUNK, CHUNK), 1)
        tril = row >= col
        w_bf = [jnp.where(tril, w_ref[g], 0.0).astype(BF16) for g in range(A_GROUPS)]
        wt_bf = [jnp.where(col >= row, wt_ref[g], 0.0).astype(BF16) for g in range(A_GROUPS)]
        pavg_v = p_ref[...]
        lg = lg_ref[...]
        ug, dug = _gelu_and_grad(u_ref[...])
        vg, dvg_dx = _gelu_and_grad(v_ref[...])
        vhat, rstd = _layernorm_groups(vg, pavg_v)
        vn = vhat * lg + lb_ref[...]
        da = da_ref[...]
        bz = bz_ref[...]
        for c in range(GATE_ROWS // CHUNK):
            sl = slice(c * CHUNK, (c + 1) * CHUNK)
            vn_bf = vn[sl].astype(BF16)
            z = _spatial_mix(w_bf, vn_bf, masks, bz)
            dz = da[sl] * ug[sl]
            duv_ref[sl, 0:A_WIDTH] = (da[sl] * z * dug[sl]).astype(BF16)
            dbz_ref[...] += dz
            dz_bf = dz.astype(BF16)
            dvn = jnp.zeros((CHUNK, A_WIDTH), F32)
            for g in range(A_GROUPS):
                dz_g = jnp.where(masks[g], dz, 0.0).astype(BF16)
                dw_ref[g] += jnp.where(tril, _dot_nt(dz_g, vn_bf), 0.0)
                dvn = dvn + jnp.where(masks[g], _dot(wt_bf[g], dz_bf), 0.0)
            vh = vhat[sl]
            dlb_ref[...] += jnp.sum(dvn, axis=0, keepdims=True)
            dlg_ref[...] += jnp.sum(dvn * vh, axis=0, keepdims=True)
            dvh = dvn * lg
            m1 = _dot(dvh, pavg_v, hi)
            m2 = _dot(dvh * vh, pavg_v, hi)
            duv_ref[sl, A_WIDTH:2 * A_WIDTH] = (rstd[sl] * (dvh - m1 - vh * m2) * dvg_dx[sl]).astype(BF16)

        @pl.when(i == nsteps - 1)
        def _():
            dbz_ref[...] = _dot(dbz_ref[...], pavg_v * float(HEAD_DIM), hi)
            exchange.finish()

    res = pl.pallas_call(
        body, grid=(nsteps,),
        in_specs=[pl.BlockSpec((GATE_ROWS, A_WIDTH), lambda i: (i, 0)),
                  pl.BlockSpec((GATE_ROWS, A_WIDTH), lambda i: (i, 1)),
                  pl.BlockSpec((GATE_ROWS, A_WIDTH), lambda i: (i, 0)),
                  _full_spec((1, A_WIDTH)), _full_spec((1, A_WIDTH)), _full_spec((A_GROUPS, CHUNK, CHUNK)),
                  _full_spec((A_GROUPS, CHUNK, CHUNK)), _full_spec((CHUNK, A_WIDTH)),
                  _full_spec((A_WIDTH, A_WIDTH))] + [ANY] * nx,
        out_specs=[pl.BlockSpec((GATE_ROWS, 2 * A_WIDTH), lambda i: (i, 0)),
                   _full_spec((1, A_WIDTH)), _full_spec((1, A_WIDTH)), _full_spec((A_GROUPS, CHUNK, CHUNK)),
                   _full_spec((CHUNK, A_WIDTH))] + [ANY] * nx,
        out_shape=[jax.ShapeDtypeStruct((m, IN_COLS), BF16),
                   jax.ShapeDtypeStruct((1, A_WIDTH), F32), jax.ShapeDtypeStruct((1, A_WIDTH), F32),
                   jax.ShapeDtypeStruct((A_GROUPS, CHUNK, CHUNK), F32),
                   jax.ShapeDtypeStruct((CHUNK, A_WIDTH), F32)]
        + [jax.ShapeDtypeStruct((N_SHARD, s[1] // 2, s[2]), F32) for s in shapes],
        scratch_shapes=_sem_pair(nx),
        compiler_params=_params(("arbitrary",)), name="gate_bwd",
    )(uv, uv, dmix, ln_g, ln_b, w_s, w_st, bz, pavg, *grads)
    return res[:5], list(res[5:])


Q_BLOCK = 128
PAIR = 2 * HEAD_DIM
N_PAIR = B_HEADS // 2
N_CFG = len(DILATED)
BLOCKS_PER_CFG = SEQ // Q_BLOCK
QKV_SLABS = 3 * N_PAIR
FWD_BLOCKS_PER_TRIP = 8
BWD_BLOCKS_PER_TRIP = 4


def _t5_bucket_np(dist, dtype):
    max_exact = NUM_BUCKETS // 2
    d = np.maximum(dist, 1).astype(dtype)
    large = max_exact + (np.log(d / dtype(max_exact)) / dtype(math.log(MAX_DISTANCE / max_exact))
                         * dtype(NUM_BUCKETS - max_exact))
    large = np.minimum(large.astype(np.int32), NUM_BUCKETS - 1)
    return np.where(dist < max_exact, dist, large)


def _bucket_tables():
    i = np.arange(Q_BLOCK)[:, None]
    j = np.arange(Q_BLOCK)[None, :]
    tables = []
    for _, dil in DILATED:
        rel_prev = Q_BLOCK + i - j
        rel_cur = i - j
        rel = np.concatenate([rel_prev, rel_cur], axis=1)
        valid = np.concatenate([rel_prev <= Q_BLOCK, rel_cur >= 0], axis=1)
        dist = np.maximum(rel, 0) * dil
        b32 = _t5_bucket_np(dist, np.float32)
        b64 = _t5_bucket_np(dist, np.float64)
        assert np.array_equal(b32, b64)
        tables.append(np.where(valid, b32, -1).astype(np.int32))
    return np.stack(tables)


def _present_buckets(buckets_np):
    return [sorted(set(int(v) for v in np.unique(buckets_np[c]) if v >= 0)) for c in range(N_CFG)]


def _bias_tables_body(buckets_np):
    present = _present_buckets(buckets_np)

    def tables(rb_ref, bk_ref, o_ref, ot_ref):
        for c in range(N_CFG):
            bk = bk_ref[c]
            for h in range(B_HEADS):
                acc = jnp.full((Q_BLOCK, 2 * Q_BLOCK), NEG_INF, F32)
                for b in present[c]:
                    acc = jnp.where(bk == b, rb_ref[h, b], acc)
                o_ref[c, h] = acc
                ot_ref[c, h] = acc.T

    return tables


def _proj_fwd(x, g1, w_in_t, ln_g, ln_b, w_s, bz):
    m = x.shape[0]
    tm = GATE_ROWS
    pavg = _group_mean_matrix()

    def body(x_ref, g_ref, w_ref, lg_ref, lb_ref, ws_ref, bz_ref, p_ref, h_ref, uv_ref, qkv_ref, a_ref):
        xv = x_ref[...]
        h = (xv * _rstd(xv) * g_ref[...]).astype(BF16)
        h_ref[...] = h
        acc = _dot_nt(h, w_ref[...])
        uv_ref[...] = acc[:, :2 * A_WIDTH]
        for s in range(QKV_SLABS):
            qkv_ref[s] = acc[:, 2 * A_WIDTH + s * PAIR:2 * A_WIDTH + (s + 1) * PAIR]
        _gate_fwd_rows(acc[:, :A_WIDTH], acc[:, A_WIDTH:2 * A_WIDTH], lg_ref[...], lb_ref[...], ws_ref,
                       bz_ref[...], p_ref[...], a_ref)

    return pl.pallas_call(
        body, grid=(m // tm,),
        in_specs=[pl.BlockSpec((tm, D_MODEL), lambda i: (i, 0)), _vec_spec(),
                  pl.BlockSpec((IN_COLS, D_MODEL), lambda i: (0, 0)),
                  _full_spec((1, A_WIDTH)), _full_spec((1, A_WIDTH)), _full_spec((A_GROUPS, CHUNK, CHUNK)),
                  _full_spec((CHUNK, A_WIDTH)), _full_spec((A_WIDTH, A_WIDTH))],
        out_specs=[pl.BlockSpec((tm, D_MODEL), lambda i: (i, 0)),
                   pl.BlockSpec((tm, 2 * A_WIDTH), lambda i: (i, 0)),
                   pl.BlockSpec((QKV_SLABS, tm, PAIR), lambda i: (0, i, 0)),
                   pl.BlockSpec((tm, A_WIDTH), lambda i: (i, 0))],
        out_shape=[jax.ShapeDtypeStruct((m, D_MODEL), BF16), jax.ShapeDtypeStruct((m, 2 * A_WIDTH), F32),
                   jax.ShapeDtypeStruct((QKV_SLABS, m, PAIR), F32), jax.ShapeDtypeStruct((m, A_WIDTH), BF16)],
        compiler_params=_params(("parallel",)), name="proj_fwd",
    )(x, g1, w_in_t, ln_g, ln_b, w_s, bz, pavg)


def _pair_masks():
    lane = lax.broadcasted_iota(jnp.int32, (1, PAIR), 1)
    return [lane < HEAD_DIM, lane >= HEAD_DIM]


def _block_rows(idx, dil):
    static = isinstance(idx, int)
    r, n = idx % dil, idx // dil

    def rows_of(block):
        start = r + (dil * Q_BLOCK) * block
        if dil == 1:
            return pl.ds(start if static else pl.multiple_of(start, Q_BLOCK), Q_BLOCK)
        return pl.ds(start, Q_BLOCK, stride=dil)

    prev = rows_of(n - 1) if not static or n > 0 else None
    return rows_of(n), prev


def _attn_fwd(qkv, bias, batch, shards):
    m = qkv.shape[1]
    comb_rows = 256
    nt = len(shards)
    shapes = [sh.shape for sh in shards]
    n_steps = batch * N_PAIR
    early, late = list(range(nt // 2)), list(range(nt // 2, nt))

    def body(q_ref, k_ref, v_ref, b_ref, *rest):
        shard_refs = rest[:nt]
        o_ref, l_ref = rest[nt:nt + 2]
        gat_refs = rest[nt + 2:2 * nt + 2]
        scratch = rest[2 * nt + 2:]
        oc_refs, lc_refs = scratch[:N_CFG], scratch[N_CFG:2 * N_CFG]
        step = pl.program_id(0) * N_PAIR + pl.program_id(1)
        gather = _GatherPlan(shapes, shard_refs, gat_refs, *scratch[2 * N_CFG:])

        @pl.when(step == 0)
        def _():
            gather.start(early + late)

        @pl.when(step == n_steps // 2)
        def _():
            gather.forward(early)

        @pl.when(step == n_steps - 2)
        def _():
            gather.forward(late)

        masks = _pair_masks()
        for ci, (_, dil) in enumerate(DILATED):
            nb = SEQ // dil // Q_BLOCK

            def block(trip, ci=ci, dil=dil, nb=nb):
                work = []
                for u in range(FWD_BLOCKS_PER_TRIP):
                    rows, prow = _block_rows(trip * FWD_BLOCKS_PER_TRIP + u, dil)
                    has_prev = nb > 1 and prow is not None
                    q = q_ref[rows, :] * 0.125
                    kc = k_ref[rows, :].astype(BF16)
                    vc = v_ref[rows, :]
                    kp = k_ref[prow, :].astype(BF16) if has_prev else None
                    vp = v_ref[prow, :] if has_prev else None
                    tiles = []
                    for h in range(2):
                        qh = jnp.where(masks[h], q, 0.0).astype(BF16)
                        sc = _dot_nt(qh, kc) + b_ref[ci, h, :, Q_BLOCK:]
                        sp = _dot_nt(qh, kp) + b_ref[ci, h, :, :Q_BLOCK] if has_prev else None
                        tiles.append((sc, sp))
                    work.append((rows, vc, vp, tiles))
                probs = []
                for _, _, _, tiles in work:
                    ps = []
                    for sc, sp in tiles:
                        mx = jnp.max(sc if sp is None else jnp.maximum(sc, sp), axis=1, keepdims=True)
                        pc = jnp.exp(sc - mx).astype(BF16)
                        pp = None if sp is None else jnp.exp(sp - mx).astype(BF16)
                        ps.append((mx, pc, pp))
                    probs.append(ps)
                for (rows, vc, vp, _), ps in zip(work, probs):
                    res = []
                    for h, (_, pc, pp) in enumerate(ps):
                        r = _dot(pc, jnp.where(masks[h], vc, 1.0).astype(BF16))
                        if pp is not None:
                            r = r + _dot(pp, jnp.where(masks[h], vp, 1.0).astype(BF16))
                        res.append(r)
                    num = jnp.where(masks[0], res[0], res[1])
                    den = pltpu.roll(jnp.where(masks[0], res[1], res[0]), HEAD_DIM, 1)
                    oc_refs[ci][rows, :] = num / den
                    lc_refs[ci][rows, :] = jnp.where(masks[0], ps[0][0], ps[1][0]) + jnp.log(den)

            for trip in range(BLOCKS_PER_CFG // FWD_BLOCKS_PER_TRIP):
                block(trip)

        def combine(i, carry):
            rr = pl.ds(pl.multiple_of(i * comb_rows, comb_rows), comb_rows)
            ls = [lc_refs[c][rr, :] for c in range(N_CFG)]
            mx = functools.reduce(jnp.maximum, ls)
            ws = [jnp.exp(l - mx) for l in ls]
            tot = functools.reduce(lambda a, b: a + b, ws)
            o = functools.reduce(lambda a, b: a + b, [ws[c] * oc_refs[c][rr, :] for c in range(N_CFG)]) / tot
            o_ref[rr, :] = o.astype(BF16)
            l_ref[rr, :] = mx + jnp.log(tot)
            return carry

        lax.fori_loop(0, SEQ // comb_rows, combine, 0)

        @pl.when(step == n_steps - 1)
        def _():
            gather.finish(early + late)

    def slab(first):
        return pl.BlockSpec((None, SEQ, PAIR), lambda b, p: (first + p, b, 0))

    nat = pl.BlockSpec((SEQ, PAIR), lambda b, p: (b, p))
    res = pl.pallas_call(
        body, grid=(batch, N_PAIR),
        in_specs=[slab(0), slab(N_PAIR), slab(2 * N_PAIR),
                  pl.BlockSpec((N_CFG, 2, Q_BLOCK, 2 * Q_BLOCK), lambda b, p: (0, p, 0, 0))] + [ANY] * nt,
        out_specs=[nat, nat] + [ANY] * nt,
        out_shape=[jax.ShapeDtypeStruct((m, B_WIDTH), BF16), jax.ShapeDtypeStruct((m, B_WIDTH), F32)]
        + [jax.ShapeDtypeStruct((N_SHARD,) + sh.shape, sh.dtype) for sh in shards],
        scratch_shapes=[pltpu.VMEM((SEQ, PAIR), F32)] * (2 * N_CFG) + _sem_pair(6 * nt),
        compiler_params=_params(("arbitrary", "arbitrary")), name="attn_fwd",
    )(qkv, qkv, qkv, bias, *shards)
    return res[0], res[1], list(res[2:])


def _attn_bwd(qkv, dmix, o, lse, bias_t, dproj, batch, parts, smalls):
    m = qkv.shape[1]
    nt, ns = len(parts), len(smalls)
    n_steps = N_PAIR * batch

    def body(q_ref, k_ref, v_ref, do_ref, o_ref, l_ref, b_ref, *rest):
        part_refs = rest[1:nt + 1]
        small_refs = rest[nt + 1:nt + 1 + ns]
        pos = nt + 1 + ns
        dproj_ref, ds_ref = rest[pos:pos + 2]
        recv_refs = rest[pos + 2:pos + 2 + nt]
        sum_refs = rest[pos + 2 + nt:pos + 2 + nt + ns]
        pos += 2 + nt + ns
        dq_acc, dk_acc, dv_acc, d_scr, stage, stage_sems, send_sems, recv_sems = rest[pos:pos + 8]
        allreduce = _SmallAllReducePlan(small_refs, sum_refs, rest[pos + 8:pos + 8 + ns],
                                        rest[pos + 8 + ns:pos + 8 + 2 * ns], *rest[pos + 8 + 2 * ns:])
        pair, seq = pl.program_id(0), pl.program_id(1)
        step = pair * batch + seq
        exchange = _ChipExchangePlan(part_refs, recv_refs, send_sems, recv_sems)

        @pl.when(step == 0)
        def _():
            allreduce.start_sibling()

        @pl.when(step == n_steps // 2)
        def _():
            allreduce.sum_sibling_and_start_chips()

        def stage_copies():
            rows = pl.ds(pl.multiple_of(seq * SEQ, SEQ), SEQ)
            return [pltpu.make_async_copy(
                stage.at[k],
                dproj_ref.at[rows, pl.ds(pl.multiple_of(2 * A_WIDTH + k * B_WIDTH + pair * PAIR, PAIR), PAIR)],
                stage_sems.at[k]) for k in range(3)]

        @pl.when(step == 0)
        def _():
            exchange.start()

        @pl.when(pl.program_id(1) == 0)
        def _():
            ds_ref[...] = jnp.zeros_like(ds_ref)

        dq_acc[...] = jnp.zeros_like(dq_acc)
        dk_acc[...] = jnp.zeros_like(dk_acc)
        dv_acc[...] = jnp.zeros_like(dv_acc)
        d_scr[...] = do_ref[...] * o_ref[...].astype(F32)
        masks = _pair_masks()

        def stack_heads(t):
            return jnp.concatenate([jnp.where(masks[0], t, 0.0), jnp.where(masks[1], t, 0.0)], axis=0).astype(BF16)

        for ci, (_, dil) in enumerate(DILATED):
            nb = SEQ // dil // Q_BLOCK

            def block(trip, carry, ci=ci, dil=dil, nb=nb):
                first = []
                for u in range(BWD_BLOCKS_PER_TRIP):
                    rows, prow = _block_rows(trip * BWD_BLOCKS_PER_TRIP + u, dil)
                    has_prev = nb > 1 and prow is not None
                    if has_prev:
                        kcat = jnp.concatenate([k_ref[prow, :], k_ref[rows, :]], axis=0).astype(BF16)
                        vcat = jnp.concatenate([v_ref[prow, :], v_ref[rows, :]], axis=0).astype(BF16)
                    else:
                        kcat = k_ref[rows, :].astype(BF16)
                        vcat = v_ref[rows, :].astype(BF16)
                    qst = stack_heads(q_ref[rows, :] * 0.125)
                    dost = stack_heads(do_ref[rows, :])
                    lt = l_ref[rows, :].T
                    dt = d_scr[rows, :].T
                    lrow = jnp.concatenate([lt[0:1], lt[HEAD_DIM:HEAD_DIM + 1]], axis=1)
                    drow = jnp.concatenate([jnp.sum(dt[:HEAD_DIM], axis=0, keepdims=True),
                                            jnp.sum(dt[HEAD_DIM:], axis=0, keepdims=True)], axis=1)
                    first.append((has_prev, rows, prow, kcat, qst, dost, lrow, drow,
                                  _dot_nt(kcat, qst), _dot_nt(vcat, dost)))
                second = []
                for has_prev, rows, prow, kcat, qst, dost, lrow, drow, st, dpt in first:
                    keys = slice(0, 2 * Q_BLOCK) if has_prev else slice(Q_BLOCK, 2 * Q_BLOCK)
                    bt = jnp.concatenate([b_ref[ci, 0, keys, :], b_ref[ci, 1, keys, :]], axis=1)
                    pt = jnp.exp(st + bt - lrow)
                    dst = pt * (dpt - drow)
                    ds_ref[ci, 0, keys, :] += dst[:, :Q_BLOCK]
                    ds_ref[ci, 1, keys, :] += dst[:, Q_BLOCK:]
                    second.append((has_prev, rows, prow, kcat, qst, dost, pt.astype(BF16), dst.astype(BF16)))
                for has_prev, rows, prow, kcat, qst, dost, pt_bf, dst_bf in second:
                    dk = _dot(dst_bf, qst)
                    dv = _dot(pt_bf, dost)
                    dq2 = _dot_tn(dst_bf, kcat)
                    dq_acc[rows, :] += jnp.where(masks[0], dq2[:Q_BLOCK], dq2[Q_BLOCK:]) * 0.125
                    if has_prev:
                        dk_acc[prow, :] += dk[:Q_BLOCK]
                        dv_acc[prow, :] += dv[:Q_BLOCK]
                        dk_acc[rows, :] += dk[Q_BLOCK:]
                        dv_acc[rows, :] += dv[Q_BLOCK:]
                    else:
                        dk_acc[rows, :] += dk
                        dv_acc[rows, :] += dv
                return carry

            for trip in range(BLOCKS_PER_CFG // BWD_BLOCKS_PER_TRIP):
                block(trip, 0)

        @pl.when(step > 0)
        def _():
            for cp in stage_copies():
                cp.wait()

        stage[0] = dq_acc[...].astype(BF16)
        stage[1] = dk_acc[...].astype(BF16)
        stage[2] = dv_acc[...].astype(BF16)
        for cp in stage_copies():
            cp.start()

        @pl.when(step == n_steps - 1)
        def _():
            for cp in stage_copies():
                cp.wait()
            exchange.finish()
            allreduce.finish()

    def slab(first):
        return pl.BlockSpec((None, SEQ, PAIR), lambda p, b: (first + p, b, 0))

    nat = pl.BlockSpec((SEQ, PAIR), lambda p, b: (b, p))
    tbl = pl.BlockSpec((N_CFG, 2, 2 * Q_BLOCK, Q_BLOCK), lambda p, b: (0, p, 0, 0))
    acc = pltpu.VMEM((SEQ, PAIR), F32)
    vm = pl.BlockSpec(memory_space=pltpu.VMEM)
    res = pl.pallas_call(
        body, grid=(N_PAIR, batch),
        in_specs=[slab(0), slab(N_PAIR), slab(2 * N_PAIR),
                  pl.BlockSpec((SEQ, PAIR), lambda p, b: (b, A_WIDTH // PAIR + p)), nat, nat, tbl]
        + [ANY] * (nt + 1) + [vm] * ns,
        out_specs=[ANY, tbl] + [ANY] * nt + [vm] * ns,
        out_shape=[jax.ShapeDtypeStruct(dproj.shape, dproj.dtype),
                   jax.ShapeDtypeStruct((N_CFG, B_HEADS, 2 * Q_BLOCK, Q_BLOCK), F32)]
        + [jax.ShapeDtypeStruct((3,) + p.shape[1:], p.dtype) for p in parts]
        + [jax.ShapeDtypeStruct(a.shape, F32) for a in smalls],
        input_output_aliases={7: 0},
        scratch_shapes=[acc, acc, acc, acc, pltpu.VMEM((3, SEQ, PAIR), BF16), pltpu.SemaphoreType.DMA((3,))]
        + _sem_pair(3 * nt) + _SmallAllReducePlan.scratch(smalls),
        compiler_params=_params(("arbitrary", "arbitrary")), name="attn_bwd",
    )(qkv, qkv, qkv, dmix, o, lse, bias_t, dproj, *parts, *smalls)
    return res[0], res[1], list(res[2:2 + nt]), list(res[2 + nt:])


def _rel_bias_grad(ds, buckets_np, grads):
    present = _present_buckets(buckets_np)
    nx = len(grads)
    shapes = [g.shape for g in grads]

    def body(bk_ref, ds_ref, *rest):
        o_ref = rest[nx]
        acc_ref = rest[2 * nx + 1]
        exchange = _SiblingExchangePlan(shapes, rest[:nx], rest[nx + 1:2 * nx + 1], *rest[2 * nx + 2:])
        exchange.start()
        acc_ref[...] = jnp.zeros_like(acc_ref)
        for c in range(N_CFG):
            bk = bk_ref[c]
            for h in range(B_HEADS):
                dsv = ds_ref[c, h]
                for b in present[c]:
                    part = jnp.sum(jnp.where(bk == b, dsv, 0.0), axis=0, keepdims=True)
                    acc_ref[pl.ds(h * NUM_BUCKETS + b, 1), :] += part
        o_ref[...] = jnp.sum(acc_ref[...], axis=1, keepdims=True)
        exchange.finish()

    vm = pl.BlockSpec(memory_space=pltpu.VMEM)
    res = pl.pallas_call(
        body, in_specs=[vm, vm] + [ANY] * nx, out_specs=[vm] + [ANY] * nx,
        out_shape=[jax.ShapeDtypeStruct((B_HEADS * NUM_BUCKETS, 1), F32)]
        + [jax.ShapeDtypeStruct((N_SHARD, s[1] // 2, s[2]), F32) for s in shapes],
        scratch_shapes=[pltpu.VMEM((B_HEADS * NUM_BUCKETS, buckets_np.shape[-1]), F32)] + _sem_pair(nx),
        compiler_params=_params(), name="rel_bias_grad",
    )(jnp.asarray(buckets_np), ds, *grads)
    return res[0], list(res[1:])


def _row_index():
    return lax.broadcasted_iota(jnp.int32, (SEQ, LANE_BLOCK), 0)


def _shift_down(x, k, row):
    return jnp.where(row >= k, pltpu.roll(x, k, 0), 0.0)


def _shift_up(x, k, row):
    return jnp.where(row < SEQ - k, pltpu.roll(x, SEQ - k, 0), 0.0)


def _convgate_fwd(gate, up, conv_w, conv_b, batch):
    m = gate.shape[0]

    def body(g_ref, u_ref, w_ref, b_ref, a_ref):
        g = g_ref[...].astype(F32)
        w = w_ref[...]
        row = _row_index()
        c = b_ref[...] + w[0:1] * _shift_down(g, 2, row) + w[1:2] * _shift_down(g, 1, row) + w[2:3] * g
        a_ref[...] = (_gelu(c) * u_ref[...].astype(F32)).astype(BF16)

    blk = pl.BlockSpec((SEQ, LANE_BLOCK), lambda b, j: (b, j))
    return pl.pallas_call(
        body, grid=(batch, D_FF // LANE_BLOCK),
        in_specs=[blk, blk, pl.BlockSpec((3, LANE_BLOCK), lambda b, j: (0, j)),
                  pl.BlockSpec((1, LANE_BLOCK), lambda b, j: (0, j))],
        out_specs=blk,
        out_shape=jax.ShapeDtypeStruct((m, D_FF), BF16),
        compiler_params=_params(("parallel", "parallel")), name="convgate_fwd",
    )(gate, up, conv_w, conv_b)


def _convgate_bwd(gate, up, dact, conv_w, conv_b, batch):
    m = gate.shape[0]

    def body(g_ref, u_ref, da_ref, w_ref, b_ref, dg_ref, du_ref, dw_ref, db_ref):
        @pl.when(pl.program_id(1) == 0)
        def _():
            dw_ref[...] = jnp.zeros_like(dw_ref)
            db_ref[...] = jnp.zeros_like(db_ref)

        g = g_ref[...].astype(F32)
        w = w_ref[...]
        row = _row_index()
        g1 = _shift_down(g, 1, row)
        g2 = _shift_down(g, 2, row)
        c = b_ref[...] + w[0:1] * g2 + w[1:2] * g1 + w[2:3] * g
        gg, dgg = _gelu_and_grad(c)
        da = da_ref[...].astype(F32)
        du_ref[...] = (da * gg).astype(BF16)
        dc = da * u_ref[...].astype(F32) * dgg
        db_ref[...] += jnp.sum(dc, axis=0, keepdims=True)
        dw_ref[0:1, :] += jnp.sum(dc * g2, axis=0, keepdims=True)
        dw_ref[1:2, :] += jnp.sum(dc * g1, axis=0, keepdims=True)
        dw_ref[2:3, :] += jnp.sum(dc * g, axis=0, keepdims=True)
        dg_ref[...] = (w[2:3] * dc + w[1:2] * _shift_up(dc, 1, row) + w[0:1] * _shift_up(dc, 2, row)).astype(BF16)

    blk = pl.BlockSpec((SEQ, LANE_BLOCK), lambda j, b: (b, j))
    wspec = pl.BlockSpec((3, LANE_BLOCK), lambda j, b: (0, j))
    bspec = pl.BlockSpec((1, LANE_BLOCK), lambda j, b: (0, j))
    return pl.pallas_call(
        body, grid=(D_FF // LANE_BLOCK, batch),
        in_specs=[blk, blk, blk, wspec, bspec],
        out_specs=[blk, blk, wspec, bspec],
        out_shape=[jax.ShapeDtypeStruct((m, D_FF), BF16), jax.ShapeDtypeStruct((m, D_FF), BF16),
                   jax.ShapeDtypeStruct((3, D_FF), F32), jax.ShapeDtypeStruct((1, D_FF), F32)],
        compiler_params=_params(("parallel", "arbitrary")), name="convgate_bwd",
    )(gate, up, dact, conv_w, conv_b)


def _gather_weights(shards, conv_w_shard, rel_bias, buckets_np):
    nt = len(shards)
    shapes = [sh.shape for sh in shards]
    ts = list(range(nt))
    tables = _bias_tables_body(buckets_np)

    def body(*refs):
        shard_refs = refs[:nt]
        cw_ref, rb_ref, bk_ref = refs[nt:nt + 3]
        out_refs = refs[nt + 3:2 * nt + 3]
        cw_out, bias_ref, bias_t_ref = refs[2 * nt + 3:2 * nt + 6]
        send_sems, recv_sems, cw_send, cw_recv = refs[2 * nt + 6:]
        plan = _GatherPlan(shapes, shard_refs, out_refs, send_sems, recv_sems)
        x, y, c, chips = _mesh_pos()

        def cw_copy(j, src, dst, chip):
            return pltpu.make_async_remote_copy(src_ref=src, dst_ref=dst, send_sem=cw_send.at[j],
                                                recv_sem=cw_recv.at[j], device_id=(*chip, c), device_id_type=MESH)

        plan.start(ts)
        cw_sends = [cw_copy(j, cw_ref, cw_out.at[2 * x + y], chip) for j, chip in enumerate(chips)]
        for cp in cw_sends:
            cp.start()
        tables(rb_ref, bk_ref, bias_ref, bias_t_ref)
        plan.forward(ts)
        for j, chip in enumerate(chips):
            dst = cw_out.at[2 * chip[0] + chip[1]]
            cw_copy(j, dst, dst, chip).wait_recv()
        plan.finish(ts)
        for cp in cw_sends:
            cp.wait_send()

    out_shape = [jax.ShapeDtypeStruct((N_SHARD,) + sh.shape, sh.dtype) for sh in shards]
    out_shape.append(jax.ShapeDtypeStruct((N_SHARD,) + conv_w_shard.shape, conv_w_shard.dtype))
    out_shape += [jax.ShapeDtypeStruct((N_CFG, B_HEADS, Q_BLOCK, 2 * Q_BLOCK), F32),
                  jax.ShapeDtypeStruct((N_CFG, B_HEADS, 2 * Q_BLOCK, Q_BLOCK), F32)]
    vm = pl.BlockSpec(memory_space=pltpu.VMEM)
    res = pl.pallas_call(
        body, in_specs=[ANY] * (nt + 1) + [pl.BlockSpec(memory_space=pltpu.SMEM), vm],
        out_specs=[ANY] * (nt + 1) + [vm, vm], out_shape=out_shape,
        scratch_shapes=_sem_pair(6 * nt) + _sem_pair(3),
        compiler_params=pltpu.CompilerParams(has_side_effects=True, vmem_limit_bytes=VMEM_LIMIT),
        name="gather_weights",
    )(*shards, conv_w_shard, rel_bias.T, jnp.asarray(buckets_np))
    return list(res[:nt + 1]), res[nt + 1], res[nt + 2]


def _add_halves(g, recv, c_idx):
    _, rows2, cols = g.shape
    rows = rows2 // 2
    tr = rows
    nblk = rows // tr

    def body(c_ref, g_ref, r_ref, o_ref):
        o_ref[...] = (g_ref[...] + r_ref[...]).astype(BF16)

    return pl.pallas_call(
        body,
        grid_spec=pltpu.PrefetchScalarGridSpec(
            num_scalar_prefetch=1, grid=(N_SHARD, nblk),
            in_specs=[pl.BlockSpec((None, tr, cols), lambda s, i, c: (s, c[0] * nblk + i, 0)),
                      pl.BlockSpec((None, tr, cols), lambda s, i, c: (s, i, 0))],
            out_specs=pl.BlockSpec((None, tr, cols), lambda s, i, c: (s, i, 0))),
        out_shape=jax.ShapeDtypeStruct((N_SHARD, rows, cols), BF16),
        compiler_params=_params(("parallel", "parallel")), name="rs_add_halves",
    )(c_idx, g, recv)


def _add_chips(part, recv, s_idx, c_idx):
    _, rows, cols = part.shape
    tr = rows
    nblk = rows // tr

    def body(idx_ref, p_ref, r_ref, o_ref):
        acc = p_ref[...].astype(F32)
        for j in range(3):
            acc = acc + r_ref[j].astype(F32)
        o_ref[...] = acc

    return pl.pallas_call(
        body,
        grid_spec=pltpu.PrefetchScalarGridSpec(
            num_scalar_prefetch=1, grid=(nblk,),
            in_specs=[pl.BlockSpec((None, tr, cols), lambda i, idx: (idx[0], i, 0)),
                      pl.BlockSpec((3, tr, cols), lambda i, idx: (0, i, 0))],
            out_specs=pl.BlockSpec((tr, cols), lambda i, idx: (idx[1] * nblk + i, 0))),
        out_shape=jax.ShapeDtypeStruct((2 * rows, cols), F32),
        compiler_params=_params(("parallel",)), name="rs_add_chips",
    )(jnp.concatenate([s_idx, c_idx]), part, recv)


def _finish_reductions(fulls, arrays):
    nt, n = len(fulls), len(arrays)

    def body(*refs):
        in_refs = refs[nt:nt + n]
        full_refs, out_refs = refs[nt + n:2 * nt + n], refs[2 * nt + n:2 * nt + 2 * n]
        pos = 2 * nt + 2 * n
        share_send, share_recv = refs[pos + 2 * n:pos + 2 * n + 2]
        allreduce = _SmallAllReducePlan(in_refs, out_refs, refs[pos:pos + n], refs[pos + n:pos + 2 * n],
                                        *refs[pos + 2 * n + 2:])
        x, y, c, _ = _mesh_pos()

        def half(t, which):
            rows = fulls[t].shape[0] // 2
            return full_refs[t].at[pl.ds(which * rows, rows), :]

        def share(t, which):
            return pltpu.make_async_remote_copy(
                src_ref=half(t, which), dst_ref=half(t, which), send_sem=share_send.at[t],
                recv_sem=share_recv.at[t], device_id=(x, y, 1 - c), device_id_type=MESH)

        for t in range(nt):
            share(t, c).start()
        allreduce.start_sibling()
        allreduce.sum_sibling_and_start_chips()
        allreduce.finish()
        for t in range(nt):
            share(t, 1 - c).wait_recv()
        for t in range(nt):
            share(t, c).wait_send()

    vm = pl.BlockSpec(memory_space=pltpu.VMEM)
    res = pl.pallas_call(
        body, in_specs=[ANY] * nt + [vm] * n, out_specs=[ANY] * nt + [vm] * n,
        out_shape=[jax.ShapeDtypeStruct(f.shape, f.dtype) for f in fulls]
        + [jax.ShapeDtypeStruct(a.shape, F32) for a in arrays],
        input_output_aliases={t: t for t in range(nt)},
        scratch_shapes=[pltpu.VMEM(a.shape, F32) for a in arrays] + [pltpu.VMEM((3,) + a.shape, F32) for a in arrays]
        + _sem_pair(nt) + _sem_pair(4 * n),
        compiler_params=pltpu.CompilerParams(has_side_effects=True),
        name="finish_reductions",
    )(*fulls, *arrays)
    return list(res[:nt]), list(res[nt:])


def _from_col_shards(g):
    n, rows, cols = g.shape
    return g.transpose(1, 0, 2).reshape(rows, n * cols)


def _train_step(x, tgt, g1, g2, g3, g4, shards, ln_g, ln_b, w_s, b_s, rel_bias, conv_w_shard, conv_b, batch,
                s_idx, c_idx):
    buckets = _bucket_tables()
    bz = jnp.repeat(b_s.T, HEAD_DIM, axis=1)
    w_st = jnp.swapaxes(w_s, 1, 2)

    def with_own(gathered, own):
        return lax.dynamic_update_index_in_dim(gathered, own, s_idx[0], 0)

    def shard_major(g):
        return g.reshape(N_SHARD, g.shape[0] // N_SHARD, D_MODEL)

    (g_in, g_convw), bias, bias_t = _gather_weights([shards["w_in"]], conv_w_shard, rel_bias, buckets)
    w_in_t = with_own(g_in, shards["w_in"]).reshape(IN_COLS, D_MODEL)
    conv_w = _from_col_shards(with_own(g_convw, conv_w_shard))

    h1, uv, qkv, a = _proj_fwd(x, g1, w_in_t, ln_g, ln_b, w_s, bz)
    later = ["w_out", "w_gate", "w_up"]
    o_bf, lse, gathered = _attn_fwd(qkv, bias, batch, [shards[n] for n in later])
    g_out, g_gate, g_up = [with_own(g, shards[n]) for g, n in zip(gathered, later)]
    w_out = g_out.reshape(D_MODEL, D_MODEL)
    w_gate_t = g_gate.reshape(D_FF, D_MODEL)
    w_up_t = g_up.reshape(D_FF, D_MODEL)
    (y1, x1, h2), _ = _fused_rows(
        "out_proj_mid_fwd", 512,
        [(a, w_out, "nn", slice(0, A_WIDTH)), (o_bf, w_out, "nn", slice(A_WIDTH, D_MODEL))],
        [x], [g2, g3], _mid_fwd_rows, [F32, F32, BF16], [])
    gate, up, g_down = _mm_pair_nt(h2, w_gate_t, w_up_t, shards["w_down"], tm=1024, tn=1408, out_dtype=BF16,
                                   name="mm_gate_up")
    w_down = with_own(g_down, shards["w_down"]).reshape(D_FF, D_MODEL)
    act = _convgate_fwd(gate, up, conv_w, conv_b, batch)
    (dx2, dy2, dg4, loss), _ = _fused_rows(
        "down_proj_loss_head", 512, [(act, w_down, "nn", None)], [x1, tgt], [g4], _loss_head_rows,
        [F32, BF16], [(1, D_MODEL), (1, 128)])

    dact = _mm(dy2, w_down, dims="nt", tm=1024, tn=1408, tk=1024, out_dtype=BF16, name="mm_dact")
    dw_down = _mm(act, dy2, dims="tn", tm=1408, tn=1024, tk=1024, out_dtype=F32, name="mm_dw_down")
    dgate, dup, dconv_w, dconv_b = _convgate_bwd(gate, up, dact, conv_w, conv_b, batch)
    (dx1, dy1, dg2, dg3), _ = _fused_rows(
        "dh2_mid_bwd", 256, [(dgate, w_gate_t, "nn", None), (dup, w_up_t, "nn", None)],
        [x1, y1, dx2], [g2, g3], _mid_bwd_rows, [F32, BF16], [(1, D_MODEL), (1, D_MODEL)])
    dw_gate_t = _mm(dgate, h2, dims="tn", tm=1408, tn=1024, tk=1024, out_dtype=F32, name="mm_dw_gate")
    dw_up_t = _mm(dup, h2, dims="tn", tm=1408, tn=1024, tk=1024, out_dtype=F32, name="mm_dw_up")
    dmix, dw_out = _out_proj_bwd(a, o_bf, dy1, w_out)

    done = [shard_major(g) for g in (dw_down, dw_gate_t, dw_up_t, dw_out)]
    (dproj, dln_g, dln_b, dw_s, dbz), recv_a = _gate_bwd(uv, dmix, ln_g, ln_b, w_s, w_st, bz, done)
    parts = [_add_halves(g, r, c_idx) for g, r in zip(done, recv_a)]
    early = dict(loss=loss, norm_mix_post=dg2, norm_ffn_pre=dg3, norm_ffn_post=dg4, ln_v_gain=dln_g,
                 ln_v_bias=dln_b, spatial_w=dw_s, spatial_b=dbz, conv_w=dconv_w, conv_b=dconv_b)
    dproj, ds, recv, early_sums = _attn_bwd(qkv, dmix, o_bf, lse, bias_t, dproj, batch, parts, list(early.values()))
    fulls = [_add_chips(p, r, s_idx, c_idx) for p, r in zip(parts, recv)]
    dw_in_t = _mm(dproj, h1, dims="tn", tm=1408, tn=1024, tk=1024, out_dtype=F32, name="mm_dw_in")
    last = [shard_major(dw_in_t)]
    drel, recv_in_a = _rel_bias_grad(ds, np.ascontiguousarray(np.swapaxes(buckets, 1, 2)), last)
    part_in = [_add_halves(g, r, c_idx) for g, r in zip(last, recv_in_a)]
    (dx0, dg1), recv_in = _fused_rows(
        "dh1_in_bwd", 512, [(dproj, w_in_t, "nn", None)], [x, dx1], [g1], _in_bwd_rows,
        [F32], [(1, D_MODEL)], exchange=part_in)
    fulls += [_add_chips(p, r, s_idx, c_idx) for p, r in zip(part_in, recv_in)]
    half_reduced = dict(zip(["w_down", "w_gate", "w_up", "w_out", "w_in"], fulls))

    return dx0, dict(zip(early, early_sums)), dict(norm_mix_pre=dg1, rel_bias=drel), half_reduced


def _adamw_update(w, g, m, v):
    nm = ADAM_B1 * m + (1.0 - ADAM_B1) * g
    nv = ADAM_B2 * v + (1.0 - ADAM_B2) * (g * g)
    m_hat = nm / (1.0 - ADAM_B1 ** ADAM_STEP)
    v_hat = nv / (1.0 - ADAM_B2 ** ADAM_STEP)
    return -ADAM_LR * (m_hat / (jnp.sqrt(v_hat) + ADAM_EPS) + ADAM_WD * w), nm, nv


def _adamw(w, g, m, v, name):
    rows, cols = w.shape
    tr = next(cand for cand in (352, 256, 128) if rows % cand == 0)

    def body(w_ref, g_ref, m_ref, v_ref, go_ref, d_ref, nm_ref, nv_ref):
        gv = g_ref[...]
        go_ref[...] = gv
        d_ref[...], nm_ref[...], nv_ref[...] = _adamw_update(w_ref[...], gv, m_ref[...], v_ref[...])

    spec = pl.BlockSpec((tr, cols), lambda i: (i, 0))
    sds = jax.ShapeDtypeStruct((rows, cols), F32)
    return pl.pallas_call(
        body, grid=(rows // tr,), in_specs=[spec] * 4, out_specs=[spec] * 4, out_shape=[sds] * 4,
        compiler_params=_params(("parallel",)), name=name,
    )(w, g, m, v)


def _adamw_small(ws, gs, ms, vs):
    n = len(ws)

    def body(*refs):
        w_refs, g_refs, m_refs, v_refs = refs[:n], refs[n:2 * n], refs[2 * n:3 * n], refs[3 * n:4 * n]
        d_refs, nm_refs, nv_refs = refs[4 * n:5 * n], refs[5 * n:6 * n], refs[6 * n:7 * n]
        for t in range(n):
            d_refs[t][...], nm_refs[t][...], nv_refs[t][...] = _adamw_update(
                w_refs[t][...], g_refs[t][...], m_refs[t][...], v_refs[t][...])

    vm = pl.BlockSpec(memory_space=pltpu.VMEM)
    sds = [jax.ShapeDtypeStruct(w.shape, F32) for w in ws]
    res = pl.pallas_call(
        body, in_specs=[vm] * (4 * n), out_specs=[vm] * (3 * n), out_shape=sds * 3,
        compiler_params=_params(), name="adamw_small",
    )(*ws, *gs, *ms, *vs)
    return res[:n], res[n:2 * n], res[2 * n:]


SMALL = ["norm_mix_pre", "norm_mix_post", "norm_ffn_pre", "norm_ffn_post", "ln_v_gain", "ln_v_bias",
         "spatial_w", "spatial_b", "rel_bias", "conv_b"]
LARGE = ["w_in", "w_gate", "w_up", "w_down", "w_out"]
TRANSPOSED = ("w_in", "w_gate", "w_up")
ORDER = ["norm_mix_pre", "norm_mix_post", "norm_ffn_pre", "norm_ffn_post", "w_in", "ln_v_gain", "ln_v_bias",
         "spatial_w", "spatial_b", "rel_bias", "w_out", "w_gate", "w_up", "conv_w", "conv_b", "w_down"]


def kernel(x, norm_mix_pre, norm_mix_post, norm_ffn_pre, norm_ffn_post, w_in, ln_v_gain, ln_v_bias, spatial_w, spatial_b, rel_bias, w_out, w_gate, w_up, conv_w, conv_b, w_down, loss_target, m_norm_mix_pre, m_norm_mix_post, m_norm_ffn_pre, m_norm_ffn_post, m_w_in, m_ln_v_gain, m_ln_v_bias, m_spatial_w, m_spatial_b, m_rel_bias, m_w_out, m_w_gate, m_w_up, m_conv_w, m_conv_b, m_w_down, v_norm_mix_pre, v_norm_mix_post, v_norm_ffn_pre, v_norm_ffn_post, v_w_in, v_ln_v_gain, v_ln_v_bias, v_spatial_w, v_spatial_b, v_rel_bias, v_w_out, v_w_gate, v_w_up, v_conv_w, v_conv_b, v_w_down):
    params = dict(norm_mix_pre=norm_mix_pre, norm_mix_post=norm_mix_post, norm_ffn_pre=norm_ffn_pre,
                  norm_ffn_post=norm_ffn_post, w_in=w_in, ln_v_gain=ln_v_gain, ln_v_bias=ln_v_bias,
                  spatial_w=spatial_w, spatial_b=spatial_b, rel_bias=rel_bias, w_out=w_out, w_gate=w_gate,
                  w_up=w_up, conv_w=conv_w, conv_b=conv_b, w_down=w_down)
    mom = dict(norm_mix_pre=m_norm_mix_pre, norm_mix_post=m_norm_mix_post, norm_ffn_pre=m_norm_ffn_pre,
               norm_ffn_post=m_norm_ffn_post, w_in=m_w_in, ln_v_gain=m_ln_v_gain, ln_v_bias=m_ln_v_bias,
               spatial_w=m_spatial_w, spatial_b=m_spatial_b, rel_bias=m_rel_bias, w_out=m_w_out, w_gate=m_w_gate,
               w_up=m_w_up, conv_w=m_conv_w, conv_b=m_conv_b, w_down=m_w_down)
    var = dict(norm_mix_pre=v_norm_mix_pre, norm_mix_post=v_norm_mix_post, norm_ffn_pre=v_norm_ffn_pre,
               norm_ffn_post=v_norm_ffn_post, w_in=v_w_in, ln_v_gain=v_ln_v_gain, ln_v_bias=v_ln_v_bias,
               spatial_w=v_spatial_w, spatial_b=v_spatial_b, rel_bias=v_rel_bias, w_out=v_w_out, w_gate=v_w_gate,
               w_up=v_w_up, conv_w=v_conv_w, conv_b=v_conv_b, w_down=v_w_down)

    batch = x.shape[0]
    xi, yi, ci = lax.axis_index("x"), lax.axis_index("y"), lax.axis_index("c")
    s_idx = (2 * xi + yi).astype(jnp.int32).reshape(1)
    c_idx = ci.astype(jnp.int32).reshape(1)

    def local(a, n):
        return jnp.swapaxes(a[0], 0, 1) if n in TRANSPOSED else a[0]

    shards = {n: local(params[n], n).astype(BF16) for n in LARGE}
    dx0, total, partial, half_reduced = _train_step(
        x.reshape(batch * SEQ, D_MODEL), loss_target.reshape(batch * SEQ, D_MODEL),
        norm_mix_pre, norm_mix_post, norm_ffn_pre, norm_ffn_post, shards,
        ln_v_gain.reshape(1, A_WIDTH), ln_v_bias.reshape(1, A_WIDTH), spatial_w[0], spatial_b[0], rel_bias,
        conv_w[0], conv_b, batch, s_idx, c_idx)
    grad_x = dx0.reshape(batch, SEQ, D_MODEL)

    names = list(partial)
    fulls, sums = _finish_reductions([half_reduced[n] for n in LARGE], [partial[n] for n in names])
    reduced = dict(zip(LARGE, fulls))
    total.update(zip(names, sums))
    loss = total["loss"][0, 0]
    total["spatial_b"] = total["spatial_b"][:, ::HEAD_DIM].T
    total["rel_bias"] = total["rel_bias"].reshape(B_HEADS, NUM_BUCKETS)
    total["conv_w"] = lax.dynamic_slice_in_dim(total["conv_w"], s_idx[0] * SHARD_FF, SHARD_FF, axis=1)
    small_names = SMALL + ["conv_w"]

    def small(a, n):
        return a.T if n == "rel_bias" else a

    for n in small_names:
        reduced[n] = total[n].reshape(small(params[n], n).shape)

    out_g, out_d, out_m, out_v = {}, {}, {}, {}
    for n in LARGE:
        res = _adamw(local(params[n], n), reduced[n], local(mom[n], n), local(var[n], n), name=f"adamw_{n}")
        if n in TRANSPOSED:
            res = [jnp.swapaxes(r, 0, 1) for r in res]
        out_g[n], out_d[n], out_m[n], out_v[n] = [r[None] for r in res]
    d, nm, nv = _adamw_small([small(params[n], n) for n in small_names], [reduced[n] for n in small_names],
                             [small(mom[n], n) for n in small_names], [small(var[n], n) for n in small_names])
    for n, dd, mm, vv in zip(small_names, d, nm, nv):
        out_g[n], out_d[n], out_m[n], out_v[n] = [small(r, n) for r in (reduced[n], dd, mm, vv)]

    return (loss, grad_x, *[out_g[n] for n in ORDER], *[out_d[n] for n in ORDER],
            *[out_m[n] for n in ORDER], *[out_v[n] for n in ORDER])
```

```python
import functools
import math

import numpy as np
import jax
import jax.numpy as jnp
from jax import lax
from jax.experimental import pallas as pl
from jax.experimental.pallas import tpu as pltpu

F32 = jnp.float32
BF16 = jnp.bfloat16
MESH = pl.DeviceIdType.MESH

D_MODEL = 1024
SEQ = 2048
HEAD_DIM = 64
A_GROUPS = 4
A_WIDTH = 256
B_HEADS = 12
B_WIDTH = 768
CHUNK = 128
DILATED = ((128, 1), (512, 4), (2048, 16))
NUM_BUCKETS = 32
MAX_DISTANCE = 2048
D_FF = 2816
IN_COLS = 2816
NORM_EPS = 1e-6
NEG_INF = -1e30
N_SHARD = 4
SHARD_FF = D_FF // N_SHARD
LANE_BLOCK = 256
VMEM_LIMIT = 56 * 1024 * 1024

ADAM_LR = 0.001
ADAM_B1 = 0.9
ADAM_B2 = 0.999
ADAM_EPS = 1e-08
ADAM_WD = 0.01
ADAM_STEP = 10

GELU_C = math.sqrt(2.0 / math.pi)
GELU_A = 0.044715

ANY = pl.BlockSpec(memory_space=pl.ANY)


def _params(sem=None):
    return pltpu.CompilerParams(dimension_semantics=sem, vmem_limit_bytes=VMEM_LIMIT)


def _dot(a, b, precision=None):
    return jnp.dot(a, b, preferred_element_type=F32, precision=precision)


def _dot_nt(a, b, precision=None):
    return lax.dot_general(a, b, (((1,), (1,)), ((), ())), preferred_element_type=F32, precision=precision)


def _dot_tn(a, b):
    return lax.dot_general(a, b, (((0,), (0,)), ((), ())), preferred_element_type=F32)


def _gelu(x):
    t = jnp.tanh(x * (GELU_C + (GELU_C * GELU_A) * (x * x)))
    return (0.5 * x) * (1.0 + t)


def _gelu_and_grad(x):
    x2 = x * x
    u = 1.0 + jnp.tanh(x * (GELU_C + (GELU_C * GELU_A) * x2))
    hx = 0.5 * x
    dg = u * (0.5 + hx * (2.0 - u) * (GELU_C + (3.0 * GELU_C * GELU_A) * x2))
    return hx * u, dg


def _mesh_pos():
    x, y, c = lax.axis_index("x"), lax.axis_index("y"), lax.axis_index("c")
    chips = [(1 - x, y), (x, 1 - y), (1 - x, 1 - y)]
    return x, y, c, chips


class _GatherPlan:
    def __init__(self, shapes, shard_refs, out_refs, send_sems, recv_sems):
        self.shapes, self.shard_refs, self.out_refs = shapes, shard_refs, out_refs
        self.send_sems, self.recv_sems = send_sems, recv_sems
        self.x, self.y, self.c, self.chips = _mesh_pos()
        self.sib = (self.x, self.y, 1 - self.c)

    def _half(self, t, chip, which):
        rows = self.shapes[t][0] // 2
        return self.out_refs[t].at[2 * chip[0] + chip[1], pl.ds(which * rows, rows), :]

    def _copy(self, k, src, dst, to):
        return pltpu.make_async_remote_copy(src_ref=src, dst_ref=dst, send_sem=self.send_sems.at[k],
                                            recv_sem=self.recv_sems.at[k], device_id=to, device_id_type=MESH)

    def _sends(self, t):
        rows = self.shapes[t][0] // 2
        src = self.shard_refs[t].at[pl.ds(self.c * rows, rows), :]
        return [self._copy(6 * t + j, src, self._half(t, (self.x, self.y), self.c), (*chip, self.c))
                for j, chip in enumerate(self.chips)]

    def _forwards(self, t):
        return [self._copy(6 * t + 3 + j, self._half(t, chip, self.c), self._half(t, chip, self.c), self.sib)
                for j, chip in enumerate(self.chips)]

    def start(self, ts):
        for t in ts:
            for cp in self._sends(t):
                cp.start()

    def forward(self, ts):
        for t in ts:
            for j, chip in enumerate(self.chips):
                landed = self._half(t, chip, self.c)
                self._copy(6 * t + j, landed, landed, (*chip, self.c)).wait_recv()
            for cp in self._forwards(t):
                cp.start()

    def finish(self, ts):
        for t in ts:
            for j, chip in enumerate(self.chips):
                other = self._half(t, chip, 1 - self.c)
                self._copy(6 * t + 3 + j, other, other, self.sib).wait_recv()
        for t in ts:
            for cp in self._sends(t) + self._forwards(t):
                cp.wait_send()


class _SiblingExchangePlan:
    def __init__(self, shapes, grad_refs, out_refs, send_sems, recv_sems):
        self.shapes, self.grad_refs, self.out_refs = shapes, grad_refs, out_refs
        self.send_sems, self.recv_sems = send_sems, recv_sems
        self.x, self.y, self.c, _ = _mesh_pos()

    def _copies(self):
        out = []
        for t, (g, o) in enumerate(zip(self.grad_refs, self.out_refs)):
            rows = self.shapes[t][1] // 2
            out.append(pltpu.make_async_remote_copy(
                src_ref=g.at[:, pl.ds((1 - self.c) * rows, rows), :], dst_ref=o, send_sem=self.send_sems.at[t],
                recv_sem=self.recv_sems.at[t], device_id=(self.x, self.y, 1 - self.c), device_id_type=MESH))
        return out

    def start(self):
        for cp in self._copies():
            cp.start()

    def finish(self):
        for cp in self._copies():
            cp.wait()


class _ChipExchangePlan:
    def __init__(self, part_refs, out_refs, send_sems, recv_sems):
        self.part_refs, self.out_refs, self.send_sems, self.recv_sems = part_refs, out_refs, send_sems, recv_sems
        _, _, self.c, self.chips = _mesh_pos()

    def _copies(self):
        return [pltpu.make_async_remote_copy(
            src_ref=p.at[2 * chip[0] + chip[1]], dst_ref=o.at[j], send_sem=self.send_sems.at[3 * t + j],
            recv_sem=self.recv_sems.at[3 * t + j], device_id=(*chip, self.c), device_id_type=MESH)
            for t, (p, o) in enumerate(zip(self.part_refs, self.out_refs)) for j, chip in enumerate(self.chips)]

    def start(self):
        for cp in self._copies():
            cp.start()

    def finish(self):
        for cp in self._copies():
            cp.wait()


class _SmallAllReducePlan:
    def __init__(self, in_refs, out_refs, sib_refs, chip_refs, send_sems, recv_sems):
        self.in_refs, self.out_refs, self.sib_refs, self.chip_refs = in_refs, out_refs, sib_refs, chip_refs
        self.send_sems, self.recv_sems = send_sems, recv_sems
        self.n = len(in_refs)
        self.x, self.y, self.c, self.chips = _mesh_pos()

    def _copy(self, k, src, dst, to):
        return pltpu.make_async_remote_copy(src_ref=src, dst_ref=dst, send_sem=self.send_sems.at[k],
                                            recv_sem=self.recv_sems.at[k], device_id=to, device_id_type=MESH)

    def _first(self):
        return [self._copy(t, self.in_refs[t], self.sib_refs[t], (self.x, self.y, 1 - self.c)) for t in range(self.n)]

    def _second(self):
        return [self._copy(self.n + 3 * t + j, self.out_refs[t], self.chip_refs[t].at[j], (*chip, self.c))
                for t in range(self.n) for j, chip in enumerate(self.chips)]

    def start_sibling(self):
        for cp in self._first():
            cp.start()

    def sum_sibling_and_start_chips(self):
        for cp in self._first():
            cp.wait()
        for t in range(self.n):
            self.out_refs[t][...] = self.in_refs[t][...] + self.sib_refs[t][...]
        for cp in self._second():
            cp.start()

    def finish(self):
        for cp in self._second():
            cp.wait()
        for t in range(self.n):
            self.out_refs[t][...] = ((self.out_refs[t][...] + self.chip_refs[t][0])
                                     + (self.chip_refs[t][1] + self.chip_refs[t][2]))

    @staticmethod
    def scratch(arrays):
        return ([pltpu.VMEM(a.shape, F32) for a in arrays] + [pltpu.VMEM((3,) + a.shape, F32) for a in arrays]
                + _sem_pair(4 * len(arrays)))


def _sem_pair(n):
    return [pltpu.SemaphoreType.DMA((n,)), pltpu.SemaphoreType.DMA((n,))]


def _mm(a, b, *, dims, tm, tn, tk, out_dtype, name):
    if dims == "nn":
        m, k = a.shape
        n = b.shape[1]
        a_spec = pl.BlockSpec((tm, tk), lambda i, j, kk: (i, kk))
        b_spec = pl.BlockSpec((tk, tn), lambda i, j, kk: (kk, j))
        dot = _dot
    elif dims == "nt":
        m, k = a.shape
        n = b.shape[0]
        a_spec = pl.BlockSpec((tm, tk), lambda i, j, kk: (i, kk))
        b_spec = pl.BlockSpec((tn, tk), lambda i, j, kk: (j, kk))
        dot = _dot_nt
    else:
        k, m = a.shape
        n = b.shape[1]
        a_spec = pl.BlockSpec((tk, tm), lambda i, j, kk: (kk, i))
        b_spec = pl.BlockSpec((tk, tn), lambda i, j, kk: (kk, j))
        dot = _dot_tn
    assert m % tm == 0 and n % tn == 0 and k % tk == 0, (name, m, n, k)
    grid = (m // tm, n // tn, k // tk)
    nk = grid[2]
    assert nk == 1 or out_dtype == F32, name

    def body(a_ref, b_ref, o_ref):
        prod = dot(a_ref[...].astype(BF16), b_ref[...].astype(BF16))
        if nk == 1:
            o_ref[...] = prod.astype(out_dtype)
        else:
            kk = pl.program_id(2)

            @pl.when(kk == 0)
            def _():
                o_ref[...] = prod

            @pl.when(kk > 0)
            def _():
                o_ref[...] += prod

    return pl.pallas_call(
        body, grid=grid, in_specs=[a_spec, b_spec],
        out_specs=pl.BlockSpec((tm, tn), lambda i, j, kk: (i, j)),
        out_shape=jax.ShapeDtypeStruct((m, n), out_dtype),
        compiler_params=_params(("parallel", "parallel", "arbitrary")), name=name,
    )(a, b)


def _mm_pair_tn(a1, a2, b, *, tm, tk, name):
    k, m = a1.shape
    n = b.shape[1]
    assert m % tm == 0 and k % tk == 0 and a2.shape == a1.shape, name

    def body(a1_ref, a2_ref, b_ref, o1_ref, o2_ref):
        bv = b_ref[...]
        p1 = _dot_tn(a1_ref[...], bv)
        p2 = _dot_tn(a2_ref[...], bv)
        kk = pl.program_id(1)

        @pl.when(kk == 0)
        def _():
            o1_ref[...] = p1
            o2_ref[...] = p2

        @pl.when(kk > 0)
        def _():
            o1_ref[...] += p1
            o2_ref[...] += p2

    a_spec = pl.BlockSpec((tk, tm), lambda i, kk: (kk, i))
    o_spec = pl.BlockSpec((tm, n), lambda i, kk: (i, 0))
    return pl.pallas_call(
        body, grid=(m // tm, k // tk),
        in_specs=[a_spec, a_spec, pl.BlockSpec((tk, n), lambda i, kk: (kk, 0))],
        out_specs=[o_spec, o_spec],
        out_shape=[jax.ShapeDtypeStruct((m, n), F32)] * 2,
        compiler_params=_params(("parallel", "arbitrary")), name=name,
    )(a1, a2, b)


def _mm_pair_nt(a, w1_t, w2_t, shard, *, tm, tn, out_dtype, name):
    m, k = a.shape
    n = w1_t.shape[0]
    assert m % tm == 0 and n % tn == 0 and w2_t.shape == w1_t.shape, name
    grid = (m // tm, n // tn)
    n_steps = grid[0] * grid[1]

    def body(a_ref, w1_ref, w2_ref, shard_ref, o1_ref, o2_ref, gat_ref, send_sems, recv_sems):
        step = pl.program_id(0) * grid[1] + pl.program_id(1)
        gather = _GatherPlan([shard.shape], [shard_ref], [gat_ref], send_sems, recv_sems)

        @pl.when(step == 0)
        def _():
            gather.start([0])

        @pl.when(step == (2 * n_steps) // 3)
        def _():
            gather.forward([0])

        av = a_ref[...]
        o1_ref[...] = _dot_nt(av, w1_ref[...]).astype(out_dtype)
        o2_ref[...] = _dot_nt(av, w2_ref[...]).astype(out_dtype)

        @pl.when(step == n_steps - 1)
        def _():
            gather.finish([0])

    w_spec = pl.BlockSpec((tn, k), lambda i, j: (j, 0))
    o_spec = pl.BlockSpec((tm, tn), lambda i, j: (i, j))
    return pl.pallas_call(
        body, grid=grid,
        in_specs=[pl.BlockSpec((tm, k), lambda i, j: (i, 0)), w_spec, w_spec, ANY],
        out_specs=[o_spec, o_spec, ANY],
        out_shape=[jax.ShapeDtypeStruct((m, n), out_dtype)] * 2
        + [jax.ShapeDtypeStruct((N_SHARD,) + shard.shape, shard.dtype)],
        scratch_shapes=_sem_pair(6),
        compiler_params=_params(("arbitrary", "arbitrary")), name=name,
    )(a, w1_t, w2_t, shard)


def _out_proj_bwd(a, o, dy1, w_out):
    m = dy1.shape[0]
    tm = 1024

    def body(a_ref, o_ref, dy_ref, w_ref, dmix_ref, dw_ref):
        dy = dy_ref[...]
        dmix_ref[...] = _dot_nt(dy, w_ref[...])
        top = _dot_tn(a_ref[...], dy)
        bottom = _dot_tn(o_ref[...], dy)

        @pl.when(pl.program_id(0) == 0)
        def _():
            dw_ref[:A_WIDTH, :] = top
            dw_ref[A_WIDTH:, :] = bottom

        @pl.when(pl.program_id(0) > 0)
        def _():
            dw_ref[:A_WIDTH, :] += top
            dw_ref[A_WIDTH:, :] += bottom

    tile = lambda width: pl.BlockSpec((tm, width), lambda i: (i, 0))
    return pl.pallas_call(
        body, grid=(m // tm,),
        in_specs=[tile(A_WIDTH), tile(B_WIDTH), tile(D_MODEL), _full_spec((D_MODEL, D_MODEL))],
        out_specs=[tile(D_MODEL), _full_spec((D_MODEL, D_MODEL))],
        out_shape=[jax.ShapeDtypeStruct((m, D_MODEL), F32), jax.ShapeDtypeStruct((D_MODEL, D_MODEL), F32)],
        compiler_params=_params(("arbitrary",)), name="out_proj_bwd",
    )(a, o, dy1, w_out)


def _fused_rows(name, tm, mats, rows, vecs, fn, row_outs, acc_outs, exchange=()):
    m = mats[0][0].shape[0]
    nm, nr, nv, nro, nao, nx = len(mats), len(rows), len(vecs), len(row_outs), len(acc_outs), len(exchange)
    n_steps = m // tm

    def body(*refs):
        a_refs, w_refs = refs[:nm], refs[nm:2 * nm]
        pos = 2 * nm
        row_refs, vec_refs, part_refs = refs[pos:pos + nr], refs[pos + nr:pos + nr + nv], refs[pos + nr + nv:pos + nr + nv + nx]
        pos += nr + nv + nx
        out_refs, acc_refs, recv_refs = refs[pos:pos + nro], refs[pos + nro:pos + nro + nao], refs[pos + nro + nao:pos + nro + nao + nx]
        sems = refs[pos + nro + nao + nx:]
        i = pl.program_id(0)
        if nx:
            plan = _ChipExchangePlan(part_refs, recv_refs, *sems)

            @pl.when(i == 0)
            def _():
                plan.start()

        @pl.when(i == 0)
        def _():
            for r in acc_refs:
                r[...] = jnp.zeros_like(r)

        y = None
        for a_ref, w_ref, (_, _, dims, sl) in zip(a_refs, w_refs, mats):
            w = w_ref[...] if sl is None else w_ref[sl, :]
            part = (_dot if dims == "nn" else _dot_nt)(a_ref[...], w)
            y = part if y is None else y + part
        res = fn(y, *[r[...] for r in row_refs], *[v[...] for v in vec_refs])
        for r, val in zip(out_refs, res[:nro]):
            r[...] = val.astype(r.dtype)
        for r, val in zip(acc_refs, res[nro:]):
            r[...] += val

        if nx:
            @pl.when(i == n_steps - 1)
            def _():
                plan.finish()

    tile = lambda width: pl.BlockSpec((tm, width), lambda i: (i, 0))
    res = pl.pallas_call(
        body, grid=(n_steps,),
        in_specs=[tile(a.shape[1]) for a, _, _, _ in mats] + [_full_spec(w.shape) for _, w, _, _ in mats]
        + [tile(D_MODEL)] * nr + [_full_spec((1, D_MODEL))] * nv + [ANY] * nx,
        out_specs=[tile(D_MODEL)] * nro + [_full_spec(s) for s in acc_outs] + [ANY] * nx,
        out_shape=[jax.ShapeDtypeStruct((m, D_MODEL), dt) for dt in row_outs]
        + [jax.ShapeDtypeStruct(s, F32) for s in acc_outs]
        + [jax.ShapeDtypeStruct((3,) + p.shape[1:], p.dtype) for p in exchange],
        scratch_shapes=_sem_pair(3 * nx) if nx else [],
        compiler_params=_params(("arbitrary",)), name=name,
    )(*[a for a, _, _, _ in mats], *[w for _, w, _, _ in mats], *rows, *vecs, *exchange)
    return list(res[:nro + nao]), list(res[nro + nao:])


def _vec_spec(width=D_MODEL):
    return pl.BlockSpec((1, width), lambda i: (0, 0))


def _rstd(v):
    return lax.rsqrt(jnp.mean(v * v, axis=-1, keepdims=True) + NORM_EPS)


def _mid_fwd_rows(y1, x0, g2, g3):
    x1 = x0 + y1 * _rstd(y1) * g2
    return y1, x1, x1 * _rstd(x1) * g3


def _rms_bwd_rows(dout, v, g):
    r = _rstd(v)
    n = v * r
    dn = dout * g
    dv = r * (dn - n * jnp.mean(dn * n, axis=-1, keepdims=True))
    dg = jnp.sum(dout * n, axis=0, keepdims=True)
    return dv, dg


def _loss_head_rows(y2, x1, tgt, g4):
    x2 = x1 + y2 * _rstd(y2) * g4
    err = x2 - tgt
    loss = 0.5 * jnp.sum(jnp.mean(err * err, axis=-1, keepdims=True), axis=0, keepdims=True)
    dx2 = err * (1.0 / D_MODEL)
    dy2, dg4 = _rms_bwd_rows(dx2, y2, g4)
    return dx2, dy2, dg4, loss


def _mid_bwd_rows(dh2, x1, y1, dx2, g2, g3):
    d3, dg3 = _rms_bwd_rows(dh2, x1, g3)
    dx1 = dx2 + d3
    dy1, dg2 = _rms_bwd_rows(dx1, y1, g2)
    return dx1, dy1, dg2, dg3


def _in_bwd_rows(dh1, x0, dx1, g1):
    d1, dg1 = _rms_bwd_rows(dh1, x0, g1)
    return dx1 + d1, dg1


GATE_ROWS = 512


def _group_mean_matrix():
    p = np.zeros((A_WIDTH, A_WIDTH), np.float32)
    for g in range(A_GROUPS):
        p[g * HEAD_DIM:(g + 1) * HEAD_DIM, g * HEAD_DIM:(g + 1) * HEAD_DIM] = 1.0 / HEAD_DIM
    return jnp.asarray(p)


def _group_masks(width=A_WIDTH):
    lane = lax.broadcasted_iota(jnp.int32, (1, width), 1)
    return [(lane >= g * HEAD_DIM) & (lane < (g + 1) * HEAD_DIM) for g in range(width // HEAD_DIM)]


GROUP_SUM_PRECISION = lax.Precision.HIGH


def _layernorm_groups(vg, pavg):
    hi = GROUP_SUM_PRECISION
    mu = _dot(vg, pavg, hi)
    xc = vg - mu
    var = _dot(xc * xc, pavg, hi)
    rstd = lax.rsqrt(var + NORM_EPS)
    return xc * rstd, rstd


def _spatial_mix(w_bf, vn_chunk_bf, masks, bz):
    z = bz
    for g in range(A_GROUPS):
        z = z + jnp.where(masks[g], _dot(w_bf[g], vn_chunk_bf), 0.0)
    return z


def _full_spec(shape):
    return pl.BlockSpec(shape, lambda i: tuple(0 for _ in shape))


def _gate_fwd_rows(u, v, lg, lb, w_ref, bz, pavg, a_ref):
    masks = _group_masks()
    row = lax.broadcasted_iota(jnp.int32, (CHUNK, CHUNK), 0)
    col = lax.broadcasted_iota(jnp.int32, (CHUNK, CHUNK), 1)
    w_bf = [jnp.where(row >= col, w_ref[g], 0.0).astype(BF16) for g in range(A_GROUPS)]
    ug = _gelu(u)
    vhat, _ = _layernorm_groups(_gelu(v), pavg)
    vn = vhat * lg + lb
    for c in range(GATE_ROWS // CHUNK):
        sl = slice(c * CHUNK, (c + 1) * CHUNK)
        z = _spatial_mix(w_bf, vn[sl].astype(BF16), masks, bz)
        a_ref[sl, :] = (ug[sl] * z).astype(BF16)


def _gate_bwd(uv, dmix, ln_g, ln_b, w_s, w_st, bz, grads):
    m = uv.shape[0]
    pavg = _group_mean_matrix()
    nsteps = m // GATE_ROWS
    nx = len(grads)
    shapes = [g.shape for g in grads]

    def body(u_ref, v_ref, da_ref, lg_ref, lb_ref, w_ref, wt_ref, bz_ref, p_ref, *rest):
        grad_refs = rest[:nx]
        duv_ref, dlg_ref, dlb_ref, dw_ref, dbz_ref = rest[nx:nx + 5]
        recv_refs = rest[nx + 5:2 * nx + 5]
        exchange = _SiblingExchangePlan(shapes, grad_refs, recv_refs, *rest[2 * nx + 5:])
        i = pl.program_id(0)

        @pl.when(i == 0)
        def _():
            exchange.start()
            dlg_ref[...] = jnp.zeros_like(dlg_ref)
            dlb_ref[...] = jnp.zeros_like(dlb_ref)
            dw_ref[...] = jnp.zeros_like(dw_ref)
            dbz_ref[...] = jnp.zeros_like(dbz_ref)

        hi = GROUP_SUM_PRECISION
        masks = _group_masks()
        row = lax.broadcasted_iota(jnp.int32, (CHUNK, CHUNK), 0)
        col = lax.broadcasted_iota(jnp.int32, (CHUNK, CHUNK), 1)
        tril = row >= col
        w_bf = [jnp.where(tril, w_ref[g], 0.0).astype(BF16) for g in range(A_GROUPS)]
        wt_bf = [jnp.where(col >= row, wt_ref[g], 0.0).astype(BF16) for g in range(A_GROUPS)]
        pavg_v = p_ref[...]
        lg = lg_ref[...]
        ug, dug = _gelu_and_grad(u_ref[...])
        vg, dvg_dx = _gelu_and_grad(v_ref[...])
        vhat, rstd = _layernorm_groups(vg, pavg_v)
        vn = vhat * lg + lb_ref[...]
        da = da_ref[...]
        bz = bz_ref[...]
        for c in range(GATE_ROWS // CHUNK):
            sl = slice(c * CHUNK, (c + 1) * CHUNK)
            vn_bf = vn[sl].astype(BF16)
            z = _spatial_mix(w_bf, vn_bf, masks, bz)
            dz = da[sl] * ug[sl]
            duv_ref[sl, 0:A_WIDTH] = (da[sl] * z * dug[sl]).astype(BF16)
            dbz_ref[...] += dz
            dz_bf = dz.astype(BF16)
            dvn = jnp.zeros((CHUNK, A_WIDTH), F32)
            for g in range(A_GROUPS):
                dz_g = jnp.where(masks[g], dz, 0.0).astype(BF16)
                dw_ref[g] += jnp.where(tril, _dot_nt(dz_g, vn_bf), 0.0)
                dvn = dvn + jnp.where(masks[g], _dot(wt_bf[g], dz_bf), 0.0)
            vh = vhat[sl]
            dlb_ref[...] += jnp.sum(dvn, axis=0, keepdims=True)
            dlg_ref[...] += jnp.sum(dvn * vh, axis=0, keepdims=True)
            dvh = dvn * lg
            m1 = _dot(dvh, pavg_v, hi)
            m2 = _dot(dvh * vh, pavg_v, hi)
            duv_ref[sl, A_WIDTH:2 * A_WIDTH] = (rstd[sl] * (dvh - m1 - vh * m2) * dvg_dx[sl]).astype(BF16)

        @pl.when(i == nsteps - 1)
        def _():
            dbz_ref[...] = _dot(dbz_ref[...], pavg_v * float(HEAD_DIM), hi)
            exchange.finish()

    res = pl.pallas_call(
        body, grid=(nsteps,),
        in_specs=[pl.BlockSpec((GATE_ROWS, A_WIDTH), lambda i: (i, 0)),
                  pl.BlockSpec((GATE_ROWS, A_WIDTH), lambda i: (i, 1)),
                  pl.BlockSpec((GATE_ROWS, A_WIDTH), lambda i: (i, 0)),
                  _full_spec((1, A_WIDTH)), _full_spec((1, A_WIDTH)), _full_spec((A_GROUPS, CHUNK, CHUNK)),
                  _full_spec((A_GROUPS, CHUNK, CHUNK)), _full_spec((CHUNK, A_WIDTH)),
                  _full_spec((A_WIDTH, A_WIDTH))] + [ANY] * nx,
        out_specs=[pl.BlockSpec((GATE_ROWS, 2 * A_WIDTH), lambda i: (i, 0)),
                   _full_spec((1, A_WIDTH)), _full_spec((1, A_WIDTH)), _full_spec((A_GROUPS, CHUNK, CHUNK)),
                   _full_spec((CHUNK, A_WIDTH))] + [ANY] * nx,
        out_shape=[jax.ShapeDtypeStruct((m, IN_COLS), BF16),
                   jax.ShapeDtypeStruct((1, A_WIDTH), F32), jax.ShapeDtypeStruct((1, A_WIDTH), F32),
                   jax.ShapeDtypeStruct((A_GROUPS, CHUNK, CHUNK), F32),
                   jax.ShapeDtypeStruct((CHUNK, A_WIDTH), F32)]
        + [jax.ShapeDtypeStruct((N_SHARD, s[1] // 2, s[2]), F32) for s in shapes],
        scratch_shapes=_sem_pair(nx),
        compiler_params=_params(("arbitrary",)), name="gate_bwd",
    )(uv, uv, dmix, ln_g, ln_b, w_s, w_st, bz, pavg, *grads)
    return res[:5], list(res[5:])


Q_BLOCK = 128
PAIR = 2 * HEAD_DIM
N_PAIR = B_HEADS // 2
N_CFG = len(DILATED)
BLOCKS_PER_CFG = SEQ // Q_BLOCK
QKV_SLABS = 3 * N_PAIR
FWD_BLOCKS_PER_TRIP = 8
BWD_BLOCKS_PER_TRIP = 4


def _t5_bucket_np(dist, dtype):
    max_exact = NUM_BUCKETS // 2
    d = np.maximum(dist, 1).astype(dtype)
    large = max_exact + (np.log(d / dtype(max_exact)) / dtype(math.log(MAX_DISTANCE / max_exact))
                         * dtype(NUM_BUCKETS - max_exact))
    large = np.minimum(large.astype(np.int32), NUM_BUCKETS - 1)
    return np.where(dist < max_exact, dist, large)


def _bucket_tables():
    i = np.arange(Q_BLOCK)[:, None]
    j = np.arange(Q_BLOCK)[None, :]
    tables = []
    for _, dil in DILATED:
        rel_prev = Q_BLOCK + i - j
        rel_cur = i - j
        rel = np.concatenate([rel_prev, rel_cur], axis=1)
        valid = np.concatenate([rel_prev <= Q_BLOCK, rel_cur >= 0], axis=1)
        dist = np.maximum(rel, 0) * dil
        b32 = _t5_bucket_np(dist, np.float32)
        b64 = _t5_bucket_np(dist, np.float64)
        assert np.array_equal(b32, b64)
        tables.append(np.where(valid, b32, -1).astype(np.int32))
    return np.stack(tables)


def _present_buckets(buckets_np):
    return [sorted(set(int(v) for v in np.unique(buckets_np[c]) if v >= 0)) for c in range(N_CFG)]


def _bias_tables_body(buckets_np):
    present = _present_buckets(buckets_np)

    def tables(rb_ref, bk_ref, o_ref, ot_ref):
        for c in range(N_CFG):
            bk = bk_ref[c]
            for h in range(B_HEADS):
                acc = jnp.full((Q_BLOCK, 2 * Q_BLOCK), NEG_INF, F32)
                for b in present[c]:
                    acc = jnp.where(bk == b, rb_ref[h, b], acc)
                o_ref[c, h] = acc
                ot_ref[c, h] = acc.T

    return tables


def _proj_fwd(x, g1, w_in_t, ln_g, ln_b, w_s, bz):
    m = x.shape[0]
    tm = GATE_ROWS
    pavg = _group_mean_matrix()

    def body(x_ref, g_ref, w_ref, lg_ref, lb_ref, ws_ref, bz_ref, p_ref, h_ref, uv_ref, qkv_ref, a_ref):
        xv = x_ref[...]
        h = (xv * _rstd(xv) * g_ref[...]).astype(BF16)
        h_ref[...] = h
        acc = _dot_nt(h, w_ref[...])
        uv_ref[...] = acc[:, :2 * A_WIDTH]
        for s in range(QKV_SLABS):
            qkv_ref[s] = acc[:, 2 * A_WIDTH + s * PAIR:2 * A_WIDTH + (s + 1) * PAIR]
        _gate_fwd_rows(acc[:, :A_WIDTH], acc[:, A_WIDTH:2 * A_WIDTH], lg_ref[...], lb_ref[...], ws_ref,
                       bz_ref[...], p_ref[...], a_ref)

    return pl.pallas_call(
        body, grid=(m // tm,),
        in_specs=[pl.BlockSpec((tm, D_MODEL), lambda i: (i, 0)), _vec_spec(),
                  pl.BlockSpec((IN_COLS, D_MODEL), lambda i: (0, 0)),
                  _full_spec((1, A_WIDTH)), _full_spec((1, A_WIDTH)), _full_spec((A_GROUPS, CHUNK, CHUNK)),
                  _full_spec((CHUNK, A_WIDTH)), _full_spec((A_WIDTH, A_WIDTH))],
        out_specs=[pl.BlockSpec((tm, D_MODEL), lambda i: (i, 0)),
                   pl.BlockSpec((tm, 2 * A_WIDTH), lambda i: (i, 0)),
                   pl.BlockSpec((QKV_SLABS, tm, PAIR), lambda i: (0, i, 0)),
                   pl.BlockSpec((tm, A_WIDTH), lambda i: (i, 0))],
        out_shape=[jax.ShapeDtypeStruct((m, D_MODEL), BF16), jax.ShapeDtypeStruct((m, 2 * A_WIDTH), F32),
                   jax.ShapeDtypeStruct((QKV_SLABS, m, PAIR), F32), jax.ShapeDtypeStruct((m, A_WIDTH), BF16)],
        compiler_params=_params(("parallel",)), name="proj_fwd",
    )(x, g1, w_in_t, ln_g, ln_b, w_s, bz, pavg)


def _pair_masks():
    lane = lax.broadcasted_iota(jnp.int32, (1, PAIR), 1)
    return [lane < HEAD_DIM, lane >= HEAD_DIM]


def _block_rows(idx, dil):
    static = isinstance(idx, int)
    r, n = idx % dil, idx // dil

    def rows_of(block):
        start = r + (dil * Q_BLOCK) * block
        if dil == 1:
            return pl.ds(start if static else pl.multiple_of(start, Q_BLOCK), Q_BLOCK)
        return pl.ds(start, Q_BLOCK, stride=dil)

    prev = rows_of(n - 1) if not static or n > 0 else None
    return rows_of(n), prev


def _attn_fwd(qkv, bias, batch, shards):
    m = qkv.shape[1]
    comb_rows = 256
    nt = len(shards)
    shapes = [sh.shape for sh in shards]
    n_steps = batch * N_PAIR
    early, late = list(range(nt // 2)), list(range(nt // 2, nt))

    def body(q_ref, k_ref, v_ref, b_ref, *rest):
        shard_refs = rest[:nt]
        o_ref, l_ref = rest[nt:nt + 2]
        gat_refs = rest[nt + 2:2 * nt + 2]
        scratch = rest[2 * nt + 2:]
        oc_refs, lc_refs = scratch[:N_CFG], scratch[N_CFG:2 * N_CFG]
        step = pl.program_id(0) * N_PAIR + pl.program_id(1)
        gather = _GatherPlan(shapes, shard_refs, gat_refs, *scratch[2 * N_CFG:])

        @pl.when(step == 0)
        def _():
            gather.start(early + late)

        @pl.when(step == n_steps // 2)
        def _():
            gather.forward(early)

        @pl.when(step == n_steps - 2)
        def _():
            gather.forward(late)

        masks = _pair_masks()
        for ci, (_, dil) in enumerate(DILATED):
            nb = SEQ // dil // Q_BLOCK

            def block(trip, ci=ci, dil=dil, nb=nb):
                work = []
                for u in range(FWD_BLOCKS_PER_TRIP):
                    rows, prow = _block_rows(trip * FWD_BLOCKS_PER_TRIP + u, dil)
                    has_prev = nb > 1 and prow is not None
                    q = q_ref[rows, :] * 0.125
                    kc = k_ref[rows, :].astype(BF16)
                    vc = v_ref[rows, :]
                    kp = k_ref[prow, :].astype(BF16) if has_prev else None
                    vp = v_ref[prow, :] if has_prev else None
                    tiles = []
                    for h in range(2):
                        qh = jnp.where(masks[h], q, 0.0).astype(BF16)
                        sc = _dot_nt(qh, kc) + b_ref[ci, h, :, Q_BLOCK:]
                        sp = _dot_nt(qh, kp) + b_ref[ci, h, :, :Q_BLOCK] if has_prev else None
                        tiles.append((sc, sp))
                    work.append((rows, vc, vp, tiles))
                probs = []
                for _, _, _, tiles in work:
                    ps = []
                    for sc, sp in tiles:
                        mx = jnp.max(sc if sp is None else jnp.maximum(sc, sp), axis=1, keepdims=True)
                        pc = jnp.exp(sc - mx).astype(BF16)
                        pp = None if sp is None else jnp.exp(sp - mx).astype(BF16)
                        ps.append((mx, pc, pp))
                    probs.append(ps)
                for (rows, vc, vp, _), ps in zip(work, probs):
                    res = []
                    for h, (_, pc, pp) in enumerate(ps):
                        r = _dot(pc, jnp.where(masks[h], vc, 1.0).astype(BF16))
                        if pp is not None:
                            r = r + _dot(pp, jnp.where(masks[h], vp, 1.0).astype(BF16))
                        res.append(r)
                    num = jnp.where(masks[0], res[0], res[1])
                    den = pltpu.roll(jnp.where(masks[0], res[1], res[0]), HEAD_DIM, 1)
                    oc_refs[ci][rows, :] = num / den
                    lc_refs[ci][rows, :] = jnp.where(masks[0], ps[0][0], ps[1][0]) + jnp.log(den)

            for trip in range(BLOCKS_PER_CFG // FWD_BLOCKS_PER_TRIP):
                block(trip)

        def combine(i, carry):
            rr = pl.ds(pl.multiple_of(i * comb_rows, comb_rows), comb_rows)
            ls = [lc_refs[c][rr, :] for c in range(N_CFG)]
            mx = functools.reduce(jnp.maximum, ls)
            ws = [jnp.exp(l - mx) for l in ls]
            tot = functools.reduce(lambda a, b: a + b, ws)
            o = functools.reduce(lambda a, b: a + b, [ws[c] * oc_refs[c][rr, :] for c in range(N_CFG)]) / tot
            o_ref[rr, :] = o.astype(BF16)
            l_ref[rr, :] = mx + jnp.log(tot)
            return carry

        lax.fori_loop(0, SEQ // comb_rows, combine, 0)

        @pl.when(step == n_steps - 1)
        def _():
            gather.finish(early + late)

    def slab(first):
        return pl.BlockSpec((None, SEQ, PAIR), lambda b, p: (first + p, b, 0))

    nat = pl.BlockSpec((SEQ, PAIR), lambda b, p: (b, p))
    res = pl.pallas_call(
        body, grid=(batch, N_PAIR),
        in_specs=[slab(0), slab(N_PAIR), slab(2 * N_PAIR),
                  pl.BlockSpec((N_CFG, 2, Q_BLOCK, 2 * Q_BLOCK), lambda b, p: (0, p, 0, 0))] + [ANY] * nt,
        out_specs=[nat, nat] + [ANY] * nt,
        out_shape=[jax.ShapeDtypeStruct((m, B_WIDTH), BF16), jax.ShapeDtypeStruct((m, B_WIDTH), F32)]
        + [jax.ShapeDtypeStruct((N_SHARD,) + sh.shape, sh.dtype) for sh in shards],
        scratch_shapes=[pltpu.VMEM((SEQ, PAIR), F32)] * (2 * N_CFG) + _sem_pair(6 * nt),
        compiler_params=_params(("arbitrary", "arbitrary")), name="attn_fwd",
    )(qkv, qkv, qkv, bias, *shards)
    return res[0], res[1], list(res[2:])


def _attn_bwd(qkv, dmix, o, lse, bias_t, dproj, batch, parts, smalls):
    m = qkv.shape[1]
    nt, ns = len(parts), len(smalls)
    n_steps = N_PAIR * batch

    def body(q_ref, k_ref, v_ref, do_ref, o_ref, l_ref, b_ref, *rest):
        part_refs = rest[1:nt + 1]
        small_refs = rest[nt + 1:nt + 1 + ns]
        pos = nt + 1 + ns
        dproj_ref, ds_ref = rest[pos:pos + 2]
        recv_refs = rest[pos + 2:pos + 2 + nt]
        sum_refs = rest[pos + 2 + nt:pos + 2 + nt + ns]
        pos += 2 + nt + ns
        dq_acc, dk_acc, dv_acc, d_scr, stage, stage_sems, send_sems, recv_sems = rest[pos:pos + 8]
        allreduce = _SmallAllReducePlan(small_refs, sum_refs, rest[pos + 8:pos + 8 + ns],
                                        rest[pos + 8 + ns:pos + 8 + 2 * ns], *rest[pos + 8 + 2 * ns:])
        pair, seq = pl.program_id(0), pl.program_id(1)
        step = pair * batch + seq
        exchange = _ChipExchangePlan(part_refs, recv_refs, send_sems, recv_sems)

        @pl.when(step == 0)
        def _():
            allreduce.start_sibling()

        @pl.when(step == n_steps // 2)
        def _():
            allreduce.sum_sibling_and_start_chips()

        def stage_copies():
            rows = pl.ds(pl.multiple_of(seq * SEQ, SEQ), SEQ)
            return [pltpu.make_async_copy(
                stage.at[k],
                dproj_ref.at[rows, pl.ds(pl.multiple_of(2 * A_WIDTH + k * B_WIDTH + pair * PAIR, PAIR), PAIR)],
                stage_sems.at[k]) for k in range(3)]

        @pl.when(step == 0)
        def _():
            exchange.start()

        @pl.when(pl.program_id(1) == 0)
        def _():
            ds_ref[...] = jnp.zeros_like(ds_ref)

        dq_acc[...] = jnp.zeros_like(dq_acc)
        dk_acc[...] = jnp.zeros_like(dk_acc)
        dv_acc[...] = jnp.zeros_like(dv_acc)
        d_scr[...] = do_ref[...] * o_ref[...].astype(F32)
        masks = _pair_masks()

        def stack_heads(t):
            return jnp.concatenate([jnp.where(masks[0], t, 0.0), jnp.where(masks[1], t, 0.0)], axis=0).astype(BF16)

        for ci, (_, dil) in enumerate(DILATED):
            nb = SEQ // dil // Q_BLOCK

            def block(trip, carry, ci=ci, dil=dil, nb=nb):
                first = []
                for u in range(BWD_BLOCKS_PER_TRIP):
                    rows, prow = _block_rows(trip * BWD_BLOCKS_PER_TRIP + u, dil)
                    has_prev = nb > 1 and prow is not None
                    if has_prev:
                        kcat = jnp.concatenate([k_ref[prow, :], k_ref[rows, :]], axis=0).astype(BF16)
                        vcat = jnp.concatenate([v_ref[prow, :], v_ref[rows, :]], axis=0).astype(BF16)
                    else:
                        kcat = k_ref[rows, :].astype(BF16)
                        vcat = v_ref[rows, :].astype(BF16)
                    qst = stack_heads(q_ref[rows, :] * 0.125)
                    dost = stack_heads(do_ref[rows, :])
                    lt = l_ref[rows, :].T
                    dt = d_scr[rows, :].T
                    lrow = jnp.concatenate([lt[0:1], lt[HEAD_DIM:HEAD_DIM + 1]], axis=1)
                    drow = jnp.concatenate([jnp.sum(dt[:HEAD_DIM], axis=0, keepdims=True),
                                            jnp.sum(dt[HEAD_DIM:], axis=0, keepdims=True)], axis=1)
                    first.append((has_prev, rows, prow, kcat, qst, dost, lrow, drow,
                                  _dot_nt(kcat, qst), _dot_nt(vcat, dost)))
                second = []
                for has_prev, rows, prow, kcat, qst, dost, lrow, drow, st, dpt in first:
                    keys = slice(0, 2 * Q_BLOCK) if has_prev else slice(Q_BLOCK, 2 * Q_BLOCK)
                    bt = jnp.concatenate([b_ref[ci, 0, keys, :], b_ref[ci, 1, keys, :]], axis=1)
                    pt = jnp.exp(st + bt - lrow)
                    dst = pt * (dpt - drow)
                    ds_ref[ci, 0, keys, :] += dst[:, :Q_BLOCK]
                    ds_ref[ci, 1, keys, :] += dst[:, Q_BLOCK:]
                    second.append((has_prev, rows, prow, kcat, qst, dost, pt.astype(BF16), dst.astype(BF16)))
                for has_prev, rows, prow, kcat, qst, dost, pt_bf, dst_bf in second:
                    dk = _dot(dst_bf, qst)
                    dv = _dot(pt_bf, dost)
                    dq2 = _dot_tn(dst_bf, kcat)
                    dq_acc[rows, :] += jnp.where(masks[0], dq2[:Q_BLOCK], dq2[Q_BLOCK:]) * 0.125
                    if has_prev:
                        dk_acc[prow, :] += dk[:Q_BLOCK]
                        dv_acc[prow, :] += dv[:Q_BLOCK]
                        dk_acc[rows, :] += dk[Q_BLOCK:]
                        dv_acc[rows, :] += dv[Q_BLOCK:]
                    else:
                        dk_acc[rows, :] += dk
                        dv_acc[rows, :] += dv
                return carry

            for trip in range(BLOCKS_PER_CFG // BWD_BLOCKS_PER_TRIP):
                block(trip, 0)

        @pl.when(step > 0)
        def _():
            for cp in stage_copies():
                cp.wait()

        stage[0] = dq_acc[...].astype(BF16)
        stage[1] = dk_acc[...].astype(BF16)
        stage[2] = dv_acc[...].astype(BF16)
        for cp in stage_copies():
            cp.start()

        @pl.when(step == n_steps - 1)
        def _():
            for cp in stage_copies():
                cp.wait()
            exchange.finish()
            allreduce.finish()

    def slab(first):
        return pl.BlockSpec((None, SEQ, PAIR), lambda p, b: (first + p, b, 0))

    nat = pl.BlockSpec((SEQ, PAIR), lambda p, b: (b, p))
    tbl = pl.BlockSpec((N_CFG, 2, 2 * Q_BLOCK, Q_BLOCK), lambda p, b: (0, p, 0, 0))
    acc = pltpu.VMEM((SEQ, PAIR), F32)
    vm = pl.BlockSpec(memory_space=pltpu.VMEM)
    res = pl.pallas_call(
        body, grid=(N_PAIR, batch),
        in_specs=[slab(0), slab(N_PAIR), slab(2 * N_PAIR),
                  pl.BlockSpec((SEQ, PAIR), lambda p, b: (b, A_WIDTH // PAIR + p)), nat, nat, tbl]
        + [ANY] * (nt + 1) + [vm] * ns,
        out_specs=[ANY, tbl] + [ANY] * nt + [vm] * ns,
        out_shape=[jax.ShapeDtypeStruct(dproj.shape, dproj.dtype),
                   jax.ShapeDtypeStruct((N_CFG, B_HEADS, 2 * Q_BLOCK, Q_BLOCK), F32)]
        + [jax.ShapeDtypeStruct((3,) + p.shape[1:], p.dtype) for p in parts]
        + [jax.ShapeDtypeStruct(a.shape, F32) for a in smalls],
        input_output_aliases={7: 0},
        scratch_shapes=[acc, acc, acc, acc, pltpu.VMEM((3, SEQ, PAIR), BF16), pltpu.SemaphoreType.DMA((3,))]
        + _sem_pair(3 * nt) + _SmallAllReducePlan.scratch(smalls),
        compiler_params=_params(("arbitrary", "arbitrary")), name="attn_bwd",
    )(qkv, qkv, qkv, dmix, o, lse, bias_t, dproj, *parts, *smalls)
    return res[0], res[1], list(res[2:2 + nt]), list(res[2 + nt:])


def _rel_bias_grad(ds, buckets_np, grads):
    present = _present_buckets(buckets_np)
    nx = len(grads)
    shapes = [g.shape for g in grads]

    def body(bk_ref, ds_ref, *rest):
        o_ref = rest[nx]
        acc_ref = rest[2 * nx + 1]
        exchange = _SiblingExchangePlan(shapes, rest[:nx], rest[nx + 1:2 * nx + 1], *rest[2 * nx + 2:])
        exchange.start()
        acc_ref[...] = jnp.zeros_like(acc_ref)
        for c in range(N_CFG):
            bk = bk_ref[c]
            for h in range(B_HEADS):
                dsv = ds_ref[c, h]
                for b in present[c]:
                    part = jnp.sum(jnp.where(bk == b, dsv, 0.0), axis=0, keepdims=True)
                    acc_ref[pl.ds(h * NUM_BUCKETS + b, 1), :] += part
        o_ref[...] = jnp.sum(acc_ref[...], axis=1, keepdims=True)
        exchange.finish()

    vm = pl.BlockSpec(memory_space=pltpu.VMEM)
    res = pl.pallas_call(
        body, in_specs=[vm, vm] + [ANY] * nx, out_specs=[vm] + [ANY] * nx,
        out_shape=[jax.ShapeDtypeStruct((B_HEADS * NUM_BUCKETS, 1), F32)]
        + [jax.ShapeDtypeStruct((N_SHARD, s[1] // 2, s[2]), F32) for s in shapes],
        scratch_shapes=[pltpu.VMEM((B_HEADS * NUM_BUCKETS, buckets_np.shape[-1]), F32)] + _sem_pair(nx),
        compiler_params=_params(), name="rel_bias_grad",
    )(jnp.asarray(buckets_np), ds, *grads)
    return res[0], list(res[1:])


def _row_index():
    return lax.broadcasted_iota(jnp.int32, (SEQ, LANE_BLOCK), 0)


def _shift_down(x, k, row):
    return jnp.where(row >= k, pltpu.roll(x, k, 0), 0.0)


def _shift_up(x, k, row):
    return jnp.where(row < SEQ - k, pltpu.roll(x, SEQ - k, 0), 0.0)


def _convgate_fwd(gate, up, conv_w, conv_b, batch):
    m = gate.shape[0]

    def body(g_ref, u_ref, w_ref, b_ref, a_ref):
        g = g_ref[...].astype(F32)
        w = w_ref[...]
        row = _row_index()
        c = b_ref[...] + w[0:1] * _shift_down(g, 2, row) + w[1:2] * _shift_down(g, 1, row) + w[2:3] * g
        a_ref[...] = (_gelu(c) * u_ref[...].astype(F32)).astype(BF16)

    blk = pl.BlockSpec((SEQ, LANE_BLOCK), lambda b, j: (b, j))
    return pl.pallas_call(
        body, grid=(batch, D_FF // LANE_BLOCK),
        in_specs=[blk, blk, pl.BlockSpec((3, LANE_BLOCK), lambda b, j: (0, j)),
                  pl.BlockSpec((1, LANE_BLOCK), lambda b, j: (0, j))],
        out_specs=blk,
        out_shape=jax.ShapeDtypeStruct((m, D_FF), BF16),
        compiler_params=_params(("parallel", "parallel")), name="convgate_fwd",
    )(gate, up, conv_w, conv_b)


def _convgate_bwd(gate, up, dact, conv_w, conv_b, batch):
    m = gate.shape[0]

    def body(g_ref, u_ref, da_ref, w_ref, b_ref, dg_ref, du_ref, dw_ref, db_ref):
        @pl.when(pl.program_id(1) == 0)
        def _():
            dw_ref[...] = jnp.zeros_like(dw_ref)
            db_ref[...] = jnp.zeros_like(db_ref)

        g = g_ref[...].astype(F32)
        w = w_ref[...]
        row = _row_index()
        g1 = _shift_down(g, 1, row)
        g2 = _shift_down(g, 2, row)
        c = b_ref[...] + w[0:1] * g2 + w[1:2] * g1 + w[2:3] * g
        gg, dgg = _gelu_and_grad(c)
        da = da_ref[...].astype(F32)
        du_ref[...] = (da * gg).astype(BF16)
        dc = da * u_ref[...].astype(F32) * dgg
        db_ref[...] += jnp.sum(dc, axis=0, keepdims=True)
        dw_ref[0:1, :] += jnp.sum(dc * g2, axis=0, keepdims=True)
        dw_ref[1:2, :] += jnp.sum(dc * g1, axis=0, keepdims=True)
        dw_ref[2:3, :] += jnp.sum(dc * g, axis=0, keepdims=True)
        dg_ref[...] = (w[2:3] * dc + w[1:2] * _shift_up(dc, 1, row) + w[0:1] * _shift_up(dc, 2, row)).astype(BF16)

    blk = pl.BlockSpec((SEQ, LANE_BLOCK), lambda j, b: (b, j))
    wspec = pl.BlockSpec((3, LANE_BLOCK), lambda j, b: (0, j))
    bspec = pl.BlockSpec((1, LANE_BLOCK), lambda j, b: (0, j))
    return pl.pallas_call(
        body, grid=(D_FF // LANE_BLOCK, batch),
        in_specs=[blk, blk, blk, wspec, bspec],
        out_specs=[blk, blk, wspec, bspec],
        out_shape=[jax.ShapeDtypeStruct((m, D_FF), BF16), jax.ShapeDtypeStruct((m, D_FF), BF16),
                   jax.ShapeDtypeStruct((3, D_FF), F32), jax.ShapeDtypeStruct((1, D_FF), F32)],
        compiler_params=_params(("parallel", "arbitrary")), name="convgate_bwd",
    )(gate, up, dact, conv_w, conv_b)


def _gather_weights(shards, conv_w_shard, rel_bias, buckets_np):
    nt = len(shards)
    shapes = [sh.shape for sh in shards]
    ts = list(range(nt))
    tables = _bias_tables_body(buckets_np)

    def body(*refs):
        shard_refs = refs[:nt]
        cw_ref, rb_ref, bk_ref = refs[nt:nt + 3]
        out_refs = refs[nt + 3:2 * nt + 3]
        cw_out, bias_ref, bias_t_ref = refs[2 * nt + 3:2 * nt + 6]
        send_sems, recv_sems, cw_send, cw_recv = refs[2 * nt + 6:]
        plan = _GatherPlan(shapes, shard_refs, out_refs, send_sems, recv_sems)
        x, y, c, chips = _mesh_pos()

        def cw_copy(j, src, dst, chip):
            return pltpu.make_async_remote_copy(src_ref=src, dst_ref=dst, send_sem=cw_send.at[j],
                                                recv_sem=cw_recv.at[j], device_id=(*chip, c), device_id_type=MESH)

        plan.start(ts)
        cw_sends = [cw_copy(j, cw_ref, cw_out.at[2 * x + y], chip) for j, chip in enumerate(chips)]
        for cp in cw_sends:
            cp.start()
        tables(rb_ref, bk_ref, bias_ref, bias_t_ref)
        plan.forward(ts)
        for j, chip in enumerate(chips):
            dst = cw_out.at[2 * chip[0] + chip[1]]
            cw_copy(j, dst, dst, chip).wait_recv()
        plan.finish(ts)
        for cp in cw_sends:
            cp.wait_send()

    out_shape = [jax.ShapeDtypeStruct((N_SHARD,) + sh.shape, sh.dtype) for sh in shards]
    out_shape.append(jax.ShapeDtypeStruct((N_SHARD,) + conv_w_shard.shape, conv_w_shard.dtype))
    out_shape += [jax.ShapeDtypeStruct((N_CFG, B_HEADS, Q_BLOCK, 2 * Q_BLOCK), F32),
                  jax.ShapeDtypeStruct((N_CFG, B_HEADS, 2 * Q_BLOCK, Q_BLOCK), F32)]
    vm = pl.BlockSpec(memory_space=pltpu.VMEM)
    res = pl.pallas_call(
        body, in_specs=[ANY] * (nt + 1) + [pl.BlockSpec(memory_space=pltpu.SMEM), vm],
        out_specs=[ANY] * (nt + 1) + [vm, vm], out_shape=out_shape,
        scratch_shapes=_sem_pair(6 * nt) + _sem_pair(3),
        compiler_params=pltpu.CompilerParams(has_side_effects=True, vmem_limit_bytes=VMEM_LIMIT),
        name="gather_weights",
    )(*shards, conv_w_shard, rel_bias.T, jnp.asarray(buckets_np))
    return list(res[:nt + 1]), res[nt + 1], res[nt + 2]


def _turn(t, u, s, last):
    return jnp.where(t == u, s, jnp.where(t > u, last, 0))


def _add_halves(gs, recvs, c_idx):
    n = len(gs)
    _, rows2, cols = gs[0].shape
    rows = rows2 // 2
    assert all(g.shape == gs[0].shape for g in gs)

    def body(c_ref, *refs):
        t = pl.program_id(0)
        for u in range(n):
            @pl.when(t == u)
            def _(u=u):
                refs[2 * n + u][...] = (refs[u][...] + refs[n + u][...]).astype(BF16)

    def own(u):
        return pl.BlockSpec((None, None, rows, cols), lambda t, s, c: (_turn(t, u, s, N_SHARD - 1), c[0], 0, 0))

    def plain(u):
        return pl.BlockSpec((None, rows, cols), lambda t, s, c: (_turn(t, u, s, N_SHARD - 1), 0, 0))

    return pl.pallas_call(
        body,
        grid_spec=pltpu.PrefetchScalarGridSpec(
            num_scalar_prefetch=1, grid=(n, N_SHARD),
            in_specs=[own(u) for u in range(n)] + [plain(u) for u in range(n)],
            out_specs=[plain(u) for u in range(n)]),
        out_shape=[jax.ShapeDtypeStruct((N_SHARD, rows, cols), BF16)] * n,
        compiler_params=_params(("arbitrary", "arbitrary")), name="rs_add_halves",
    )(c_idx, *[g.reshape(N_SHARD, 2, rows, cols) for g in gs], *recvs)


def _add_chips(parts, recvs, s_idx, c_idx):
    n = len(parts)
    _, rows, cols = parts[0].shape
    assert all(p.shape == parts[0].shape for p in parts)

    def body(idx_ref, *refs):
        t = pl.program_id(0)
        for u in range(n):
            @pl.when(t == u)
            def _(u=u):
                acc = refs[u][...].astype(F32)
                for j in range(3):
                    acc = acc + refs[n + u][j].astype(F32)
                refs[2 * n + u][...] = acc

    res = pl.pallas_call(
        body,
        grid_spec=pltpu.PrefetchScalarGridSpec(
            num_scalar_prefetch=1, grid=(n,),
            in_specs=[pl.BlockSpec((None, rows, cols), lambda t, idx: (idx[0], 0, 0))] * n
            + [pl.BlockSpec((3, rows, cols), lambda t, idx: (0, 0, 0))] * n,
            out_specs=[pl.BlockSpec((None, rows, cols), lambda t, idx: (idx[1], 0, 0))] * n),
        out_shape=[jax.ShapeDtypeStruct((2, rows, cols), F32)] * n,
        compiler_params=_params(("arbitrary",)), name="rs_add_chips",
    )(jnp.concatenate([s_idx, c_idx]), *parts, *recvs)
    return [r.reshape(2 * rows, cols) for r in res]


def _finish_reductions(fulls, arrays):
    nt, n = len(fulls), len(arrays)

    def body(*refs):
        in_refs = refs[nt:nt + n]
        full_refs, out_refs = refs[nt + n:2 * nt + n], refs[2 * nt + n:2 * nt + 2 * n]
        pos = 2 * nt + 2 * n
        share_send, share_recv = refs[pos + 2 * n:pos + 2 * n + 2]
        allreduce = _SmallAllReducePlan(in_refs, out_refs, refs[pos:pos + n], refs[pos + n:pos + 2 * n],
                                        *refs[pos + 2 * n + 2:])
        x, y, c, _ = _mesh_pos()

        def half(t, which):
            rows = fulls[t].shape[0] // 2
            return full_refs[t].at[pl.ds(which * rows, rows), :]

        def share(t, which):
            return pltpu.make_async_remote_copy(
                src_ref=half(t, which), dst_ref=half(t, which), send_sem=share_send.at[t],
                recv_sem=share_recv.at[t], device_id=(x, y, 1 - c), device_id_type=MESH)

        for t in range(nt):
            share(t, c).start()
        allreduce.start_sibling()
        allreduce.sum_sibling_and_start_chips()
        allreduce.finish()
        for t in range(nt):
            share(t, 1 - c).wait_recv()
        for t in range(nt):
            share(t, c).wait_send()

    vm = pl.BlockSpec(memory_space=pltpu.VMEM)
    res = pl.pallas_call(
        body, in_specs=[ANY] * nt + [vm] * n, out_specs=[ANY] * nt + [vm] * n,
        out_shape=[jax.ShapeDtypeStruct(f.shape, f.dtype) for f in fulls]
        + [jax.ShapeDtypeStruct(a.shape, F32) for a in arrays],
        input_output_aliases={t: t for t in range(nt)},
        scratch_shapes=[pltpu.VMEM(a.shape, F32) for a in arrays] + [pltpu.VMEM((3,) + a.shape, F32) for a in arrays]
        + _sem_pair(nt) + _sem_pair(4 * n),
        compiler_params=pltpu.CompilerParams(has_side_effects=True),
        name="finish_reductions",
    )(*fulls, *arrays)
    return list(res[:nt]), list(res[nt:])


def _from_col_shards(g):
    n, rows, cols = g.shape
    return g.transpose(1, 0, 2).reshape(rows, n * cols)


def _train_step(x, tgt, g1, g2, g3, g4, shards, ln_g, ln_b, w_s, b_s, rel_bias, conv_w_shard, conv_b, batch,
                s_idx, c_idx):
    buckets = _bucket_tables()
    bz = jnp.repeat(b_s.T, HEAD_DIM, axis=1)
    w_st = jnp.swapaxes(w_s, 1, 2)

    def with_own(gathered, own):
        return lax.dynamic_update_index_in_dim(gathered, own, s_idx[0], 0)

    def shard_major(g):
        return g.reshape(N_SHARD, g.shape[0] // N_SHARD, D_MODEL)

    (g_in, g_convw), bias, bias_t = _gather_weights([shards["w_in"]], conv_w_shard, rel_bias, buckets)
    w_in_t = with_own(g_in, shards["w_in"]).reshape(IN_COLS, D_MODEL)
    conv_w = _from_col_shards(with_own(g_convw, conv_w_shard))

    h1, uv, qkv, a = _proj_fwd(x, g1, w_in_t, ln_g, ln_b, w_s, bz)
    later = ["w_out", "w_gate", "w_up"]
    o_bf, lse, gathered = _attn_fwd(qkv, bias, batch, [shards[n] for n in later])
    g_out, g_gate, g_up = [with_own(g, shards[n]) for g, n in zip(gathered, later)]
    w_out = g_out.reshape(D_MODEL, D_MODEL)
    w_gate_t = g_gate.reshape(D_FF, D_MODEL)
    w_up_t = g_up.reshape(D_FF, D_MODEL)
    (y1, x1, h2), _ = _fused_rows(
        "out_proj_mid_fwd", 512,
        [(a, w_out, "nn", slice(0, A_WIDTH)), (o_bf, w_out, "nn", slice(A_WIDTH, D_MODEL))],
        [x], [g2, g3], _mid_fwd_rows, [F32, F32, BF16], [])
    gate, up, g_down = _mm_pair_nt(h2, w_gate_t, w_up_t, shards["w_down"], tm=1024, tn=1408, out_dtype=BF16,
                                   name="mm_gate_up")
    w_down = with_own(g_down, shards["w_down"]).reshape(D_FF, D_MODEL)
    act = _convgate_fwd(gate, up, conv_w, conv_b, batch)
    (dx2, dy2, dg4, loss), _ = _fused_rows(
        "down_proj_loss_head", 512, [(act, w_down, "nn", None)], [x1, tgt], [g4], _loss_head_rows,
        [F32, BF16], [(1, D_MODEL), (1, 128)])

    dact = _mm(dy2, w_down, dims="nt", tm=1024, tn=1408, tk=1024, out_dtype=BF16, name="mm_dact")
    dw_down = _mm(act, dy2, dims="tn", tm=1408, tn=1024, tk=1024, out_dtype=F32, name="mm_dw_down")
    dgate, dup, dconv_w, dconv_b = _convgate_bwd(gate, up, dact, conv_w, conv_b, batch)
    (dx1, dy1, dg2, dg3), _ = _fused_rows(
        "dh2_mid_bwd", 256, [(dgate, w_gate_t, "nn", None), (dup, w_up_t, "nn", None)],
        [x1, y1, dx2], [g2, g3], _mid_bwd_rows, [F32, BF16], [(1, D_MODEL), (1, D_MODEL)])
    dw_gate_t, dw_up_t = _mm_pair_tn(dgate, dup, h2, tm=1408, tk=1024, name="mm_dw_gate_up")
    dmix, dw_out = _out_proj_bwd(a, o_bf, dy1, w_out)

    done = [shard_major(g) for g in (dw_down, dw_gate_t, dw_up_t, dw_out)]
    (dproj, dln_g, dln_b, dw_s, dbz), recv_a = _gate_bwd(uv, dmix, ln_g, ln_b, w_s, w_st, bz, done)
    parts = _add_halves(done[:3], recv_a[:3], c_idx) + _add_halves(done[3:], recv_a[3:], c_idx)
    early = dict(loss=loss, norm_mix_post=dg2, norm_ffn_pre=dg3, norm_ffn_post=dg4, ln_v_gain=dln_g,
                 ln_v_bias=dln_b, spatial_w=dw_s, spatial_b=dbz, conv_w=dconv_w, conv_b=dconv_b)
    dproj, ds, recv, early_sums = _attn_bwd(qkv, dmix, o_bf, lse, bias_t, dproj, batch, parts, list(early.values()))
    fulls = _add_chips(parts[:3], recv[:3], s_idx, c_idx) + _add_chips(parts[3:], recv[3:], s_idx, c_idx)
    dw_in_t = _mm(dproj, h1, dims="tn", tm=1408, tn=1024, tk=1024, out_dtype=F32, name="mm_dw_in")
    last = [shard_major(dw_in_t)]
    drel, recv_in_a = _rel_bias_grad(ds, np.ascontiguousarray(np.swapaxes(buckets, 1, 2)), last)
    part_in = _add_halves(last, recv_in_a, c_idx)
    (dx0, dg1), recv_in = _fused_rows(
        "dh1_in_bwd", 512, [(dproj, w_in_t, "nn", None)], [x, dx1], [g1], _in_bwd_rows,
        [F32], [(1, D_MODEL)], exchange=part_in)
    fulls += _add_chips(part_in, recv_in, s_idx, c_idx)
    half_reduced = dict(zip(["w_down", "w_gate", "w_up", "w_out", "w_in"], fulls))

    return dx0, dict(zip(early, early_sums)), dict(norm_mix_pre=dg1, rel_bias=drel), half_reduced


def _adamw_update(w, g, m, v):
    nm = ADAM_B1 * m + (1.0 - ADAM_B1) * g
    nv = ADAM_B2 * v + (1.0 - ADAM_B2) * (g * g)
    m_hat = nm / (1.0 - ADAM_B1 ** ADAM_STEP)
    v_hat = nv / (1.0 - ADAM_B2 ** ADAM_STEP)
    return -ADAM_LR * (m_hat / (jnp.sqrt(v_hat) + ADAM_EPS) + ADAM_WD * w), nm, nv


def _adamw(w, g, m, v, name):
    rows, cols = w.shape
    tr = next(cand for cand in (352, 256, 128) if rows % cand == 0)

    def body(w_ref, g_ref, m_ref, v_ref, go_ref, d_ref, nm_ref, nv_ref):
        gv = g_ref[...]
        go_ref[...] = gv
        d_ref[...], nm_ref[...], nv_ref[...] = _adamw_update(w_ref[...], gv, m_ref[...], v_ref[...])

    spec = pl.BlockSpec((tr, cols), lambda i: (i, 0))
    sds = jax.ShapeDtypeStruct((rows, cols), F32)
    return pl.pallas_call(
        body, grid=(rows // tr,), in_specs=[spec] * 4, out_specs=[spec] * 4, out_shape=[sds] * 4,
        compiler_params=_params(("parallel",)), name=name,
    )(w, g, m, v)


def _adamw_small(ws, gs, ms, vs):
    n = len(ws)

    def body(*refs):
        w_refs, g_refs, m_refs, v_refs = refs[:n], refs[n:2 * n], refs[2 * n:3 * n], refs[3 * n:4 * n]
        d_refs, nm_refs, nv_refs = refs[4 * n:5 * n], refs[5 * n:6 * n], refs[6 * n:7 * n]
        for t in range(n):
            d_refs[t][...], nm_refs[t][...], nv_refs[t][...] = _adamw_update(
                w_refs[t][...], g_refs[t][...], m_refs[t][...], v_refs[t][...])

    vm = pl.BlockSpec(memory_space=pltpu.VMEM)
    sds = [jax.ShapeDtypeStruct(w.shape, F32) for w in ws]
    res = pl.pallas_call(
        body, in_specs=[vm] * (4 * n), out_specs=[vm] * (3 * n), out_shape=sds * 3,
        compiler_params=_params(), name="adamw_small",
    )(*ws, *gs, *ms, *vs)
    return res[:n], res[n:2 * n], res[2 * n:]


SMALL = ["norm_mix_pre", "norm_mix_post", "norm_ffn_pre", "norm_ffn_post", "ln_v_gain", "ln_v_bias",
         "spatial_w", "spatial_b", "rel_bias", "conv_b"]
LARGE = ["w_in", "w_gate", "w_up", "w_down", "w_out"]
TRANSPOSED = ("w_in", "w_gate", "w_up")
ORDER = ["norm_mix_pre", "norm_mix_post", "norm_ffn_pre", "norm_ffn_post", "w_in", "ln_v_gain", "ln_v_bias",
         "spatial_w", "spatial_b", "rel_bias", "w_out", "w_gate", "w_up", "conv_w", "conv_b", "w_down"]


def kernel(x, norm_mix_pre, norm_mix_post, norm_ffn_pre, norm_ffn_post, w_in, ln_v_gain, ln_v_bias, spatial_w, spatial_b, rel_bias, w_out, w_gate, w_up, conv_w, conv_b, w_down, loss_target, m_norm_mix_pre, m_norm_mix_post, m_norm_ffn_pre, m_norm_ffn_post, m_w_in, m_ln_v_gain, m_ln_v_bias, m_spatial_w, m_spatial_b, m_rel_bias, m_w_out, m_w_gate, m_w_up, m_conv_w, m_conv_b, m_w_down, v_norm_mix_pre, v_norm_mix_post, v_norm_ffn_pre, v_norm_ffn_post, v_w_in, v_ln_v_gain, v_ln_v_bias, v_spatial_w, v_spatial_b, v_rel_bias, v_w_out, v_w_gate, v_w_up, v_conv_w, v_conv_b, v_w_down):
    params = dict(norm_mix_pre=norm_mix_pre, norm_mix_post=norm_mix_post, norm_ffn_pre=norm_ffn_pre,
                  norm_ffn_post=norm_ffn_post, w_in=w_in, ln_v_gain=ln_v_gain, ln_v_bias=ln_v_bias,
                  spatial_w=spatial_w, spatial_b=spatial_b, rel_bias=rel_bias, w_out=w_out, w_gate=w_gate,
                  w_up=w_up, conv_w=conv_w, conv_b=conv_b, w_down=w_down)
    mom = dict(norm_mix_pre=m_norm_mix_pre, norm_mix_post=m_norm_mix_post, norm_ffn_pre=m_norm_ffn_pre,
               norm_ffn_post=m_norm_ffn_post, w_in=m_w_in, ln_v_gain=m_ln_v_gain, ln_v_bias=m_ln_v_bias,
               spatial_w=m_spatial_w, spatial_b=m_spatial_b, rel_bias=m_rel_bias, w_out=m_w_out, w_gate=m_w_gate,
               w_up=m_w_up, conv_w=m_conv_w, conv_b=m_conv_b, w_down=m_w_down)
    var = dict(norm_mix_pre=v_norm_mix_pre, norm_mix_post=v_norm_mix_post, norm_ffn_pre=v_norm_ffn_pre,
               norm_ffn_post=v_norm_ffn_post, w_in=v_w_in, ln_v_gain=v_ln_v_gain, ln_v_bias=v_ln_v_bias,
               spatial_w=v_spatial_w, spatial_b=v_spatial_b, rel_bias=v_rel_bias, w_out=v_w_out, w_gate=v_w_gate,
               w_up=v_w_up, conv_w=v_conv_w, conv_b=v_conv_b, w_down=v_w_down)

    batch = x.shape[0]
    xi, yi, ci = lax.axis_index("x"), lax.axis_index("y"), lax.axis_index("c")
    s_idx = (2 * xi + yi).astype(jnp.int32).reshape(1)
    c_idx = ci.astype(jnp.int32).reshape(1)

    def local(a, n):
        return jnp.swapaxes(a[0], 0, 1) if n in TRANSPOSED else a[0]

    shards = {n: local(params[n], n).astype(BF16) for n in LARGE}
    dx0, total, partial, half_reduced = _train_step(
        x.reshape(batch * SEQ, D_MODEL), loss_target.reshape(batch * SEQ, D_MODEL),
        norm_mix_pre, norm_mix_post, norm_ffn_pre, norm_ffn_post, shards,
        ln_v_gain.reshape(1, A_WIDTH), ln_v_bias.reshape(1, A_WIDTH), spatial_w[0], spatial_b[0], rel_bias,
        conv_w[0], conv_b, batch, s_idx, c_idx)
    grad_x = dx0.reshape(batch, SEQ, D_MODEL)

    names = list(partial)
    fulls, sums = _finish_reductions([half_reduced[n] for n in LARGE], [partial[n] for n in names])
    reduced = dict(zip(LARGE, fulls))
    total.update(zip(names, sums))
    loss = total["loss"][0, 0]
    total["spatial_b"] = total["spatial_b"][:, ::HEAD_DIM].T
    total["rel_bias"] = total["rel_bias"].reshape(B_HEADS, NUM_BUCKETS)
    total["conv_w"] = lax.dynamic_slice_in_dim(total["conv_w"], s_idx[0] * SHARD_FF, SHARD_FF, axis=1)
    small_names = SMALL + ["conv_w"]

    def small(a, n):
        return a.T if n == "rel_bias" else a

    for n in small_names:
        reduced[n] = total[n].reshape(small(params[n], n).shape)

    out_g, out_d, out_m, out_v = {}, {}, {}, {}
    for n in LARGE:
        res = _adamw(local(params[n], n), reduced[n], local(mom[n], n), local(var[n], n), name=f"adamw_{n}")
        if n in TRANSPOSED:
            res = [jnp.swapaxes(r, 0, 1) for r in res]
        out_g[n], out_d[n], out_m[n], out_v[n] = [r[None] for r in res]
    d, nm, nv = _adamw_small([small(params[n], n) for n in small_names], [reduced[n] for n in small_names],
                             [small(mom[n], n) for n in small_names], [small(var[n], n) for n in small_names])
    for n, dd, mm, vv in zip(small_names, d, nm, nv):
        out_g[n], out_d[n], out_m[n], out_v[n] = [small(r, n) for r in (reduced[n], dd, mm, vv)]

    return (loss, grad_x, *[out_g[n] for n in ORDER], *[out_d[n] for n in ORDER],
            *[out_m[n] for n in ORDER], *[out_v[n] for n in ORDER])
```

```python
import functools
import math

import numpy as np
import jax
import jax.numpy as jnp
from jax import lax
from jax.experimental import pallas as pl
from jax.experimental.pallas import tpu as pltpu

F32 = jnp.float32
BF16 = jnp.bfloat16
MESH = pl.DeviceIdType.MESH

D_MODEL = 1024
SEQ = 2048
HEAD_DIM = 64
A_GROUPS = 4
A_WIDTH = 256
B_HEADS = 12
B_WIDTH = 768
CHUNK = 128
DILATED = ((128, 1), (512, 4), (2048, 16))
NUM_BUCKETS = 32
MAX_DISTANCE = 2048
D_FF = 2816
IN_COLS = 2816
NORM_EPS = 1e-6
NEG_INF = -1e30
N_SHARD = 4
SHARD_FF = D_FF // N_SHARD
LANE_BLOCK = 256
VMEM_LIMIT = 56 * 1024 * 1024

ADAM_LR = 0.001
ADAM_B1 = 0.9
ADAM_B2 = 0.999
ADAM_EPS = 1e-08
ADAM_WD = 0.01
ADAM_STEP = 10

GELU_C = math.sqrt(2.0 / math.pi)
GELU_A = 0.044715

ANY = pl.BlockSpec(memory_space=pl.ANY)


def _params(sem=None):
    return pltpu.CompilerParams(dimension_semantics=sem, vmem_limit_bytes=VMEM_LIMIT)


def _dot(a, b, precision=None):
    return jnp.dot(a, b, preferred_element_type=F32, precision=precision)


def _dot_nt(a, b, precision=None):
    return lax.dot_general(a, b, (((1,), (1,)), ((), ())), preferred_element_type=F32, precision=precision)


def _dot_tn(a, b):
    return lax.dot_general(a, b, (((0,), (0,)), ((), ())), preferred_element_type=F32)


def _gelu(x):
    t = jnp.tanh(x * (GELU_C + (GELU_C * GELU_A) * (x * x)))
    return (0.5 * x) * (1.0 + t)


def _gelu_and_grad(x):
    x2 = x * x
    u = 1.0 + jnp.tanh(x * (GELU_C + (GELU_C * GELU_A) * x2))
    hx = 0.5 * x
    dg = u * (0.5 + hx * (2.0 - u) * (GELU_C + (3.0 * GELU_C * GELU_A) * x2))
    return hx * u, dg


def _mesh_pos():
    x, y, c = lax.axis_index("x"), lax.axis_index("y"), lax.axis_index("c")
    chips = [(1 - x, y), (x, 1 - y), (1 - x, 1 - y)]
    return x, y, c, chips


class _GatherPlan:
    def __init__(self, shapes, shard_refs, out_refs, send_sems, recv_sems):
        self.shapes, self.shard_refs, self.out_refs = shapes, shard_refs, out_refs
        self.send_sems, self.recv_sems = send_sems, recv_sems
        self.x, self.y, self.c, self.chips = _mesh_pos()
        self.sib = (self.x, self.y, 1 - self.c)

    def _half(self, t, chip, which):
        rows = self.shapes[t][0] // 2
        return self.out_refs[t].at[2 * chip[0] + chip[1], pl.ds(which * rows, rows), :]

    def _copy(self, k, src, dst, to):
        return pltpu.make_async_remote_copy(src_ref=src, dst_ref=dst, send_sem=self.send_sems.at[k],
                                            recv_sem=self.recv_sems.at[k], device_id=to, device_id_type=MESH)

    def _sends(self, t):
        rows = self.shapes[t][0] // 2
        src = self.shard_refs[t].at[pl.ds(self.c * rows, rows), :]
        return [self._copy(6 * t + j, src, self._half(t, (self.x, self.y), self.c), (*chip, self.c))
                for j, chip in enumerate(self.chips)]

    def _forwards(self, t):
        return [self._copy(6 * t + 3 + j, self._half(t, chip, self.c), self._half(t, chip, self.c), self.sib)
                for j, chip in enumerate(self.chips)]

    def start(self, ts):
        for t in ts:
            for cp in self._sends(t):
                cp.start()

    def forward(self, ts):
        for t in ts:
            for j, chip in enumerate(self.chips):
                landed = self._half(t, chip, self.c)
                self._copy(6 * t + j, landed, landed, (*chip, self.c)).wait_recv()
            for cp in self._forwards(t):
                cp.start()

    def finish(self, ts):
        for t in ts:
            for j, chip in enumerate(self.chips):
                other = self._half(t, chip, 1 - self.c)
                self._copy(6 * t + 3 + j, other, other, self.sib).wait_recv()
        for t in ts:
            for cp in self._sends(t) + self._forwards(t):
                cp.wait_send()


class _RelayGatherPlan:
    def __init__(self, shapes, shard_refs, out_refs, send_sems, recv_sems):
        self.shapes, self.shard_refs, self.out_refs = shapes, shard_refs, out_refs
        self.send_sems, self.recv_sems = send_sems, recv_sems
        x, y, c, self.chips = _mesh_pos()
        self.me, self.c, self.sib = (x, y), c, (x, y, 1 - c)
        self.first = (x + c - 2 * x * c, y + (1 - c) - 2 * y * (1 - c))
        self.second = (x + (1 - c) - 2 * x * (1 - c), y + c - 2 * y * c)
        self.diag = (1 - x, 1 - y)

    def _half(self, t, chip, which):
        rows = self.shapes[t][0] // 2
        return self.out_refs[t].at[2 * chip[0] + chip[1], pl.ds(which * rows, rows), :]

    def _copy(self, k, src, dst, to):
        return pltpu.make_async_remote_copy(src_ref=src, dst_ref=dst, send_sem=self.send_sems.at[k],
                                            recv_sem=self.recv_sems.at[k], device_id=to, device_id_type=MESH)

    def _own(self, t):
        rows = self.shapes[t][0] // 2
        return self.shard_refs[t].at[pl.ds(self.c * rows, rows), :]

    def _step1(self, t):
        return self._copy(6 * t, self._own(t), self._half(t, self.me, self.c), (*self.first, self.c))

    def _step2(self, t):
        landed = self._half(t, self.first, self.c)
        return [self._copy(6 * t + 1, self._own(t), self._half(t, self.me, self.c), (*self.second, self.c)),
                self._copy(6 * t + 2, landed, landed, (*self.second, self.c))]

    def _forwards(self, t):
        return [self._copy(6 * t + 3 + j, self._half(t, chip, self.c), self._half(t, chip, self.c), self.sib)
                for j, chip in enumerate(self.chips)]

    def start(self, ts):
        for t in ts:
            self._step1(t).start()

    def relay(self, ts):
        for t in ts:
            landed = self._half(t, self.first, self.c)
            self._copy(6 * t, landed, landed, self.sib).wait_recv()
            for cp in self._step2(t):
                cp.start()

    def forward(self, ts):
        for t in ts:
            for k, chip in ((1, self.second), (2, self.diag)):
                landed = self._half(t, chip, self.c)
                self._copy(6 * t + k, landed, landed, self.sib).wait_recv()
            for cp in self._forwards(t):
                cp.start()

    def finish(self, ts):
        for t in ts:
            for j, chip in enumerate(self.chips):
                other = self._half(t, chip, 1 - self.c)
                self._copy(6 * t + 3 + j, other, other, self.sib).wait_recv()
        for t in ts:
            for cp in [self._step1(t)] + self._step2(t) + self._forwards(t):
                cp.wait_send()


class _SiblingExchangePlan:
    def __init__(self, shapes, grad_refs, out_refs, send_sems, recv_sems):
        self.shapes, self.grad_refs, self.out_refs = shapes, grad_refs, out_refs
        self.send_sems, self.recv_sems = send_sems, recv_sems
        self.x, self.y, self.c, _ = _mesh_pos()

    def _copies(self):
        out = []
        for t, (g, o) in enumerate(zip(self.grad_refs, self.out_refs)):
            rows = self.shapes[t][1] // 2
            out.append(pltpu.make_async_remote_copy(
                src_ref=g.at[:, pl.ds((1 - self.c) * rows, rows), :], dst_ref=o, send_sem=self.send_sems.at[t],
                recv_sem=self.recv_sems.at[t], device_id=(self.x, self.y, 1 - self.c), device_id_type=MESH))
        return out

    def start(self):
        for cp in self._copies():
            cp.start()

    def finish(self):
        for cp in self._copies():
            cp.wait()


class _ChipExchangePlan:
    def __init__(self, part_refs, out_refs, send_sems, recv_sems):
        self.part_refs, self.out_refs, self.send_sems, self.recv_sems = part_refs, out_refs, send_sems, recv_sems
        _, _, self.c, self.chips = _mesh_pos()

    def _copies(self):
        return [pltpu.make_async_remote_copy(
            src_ref=p.at[2 * chip[0] + chip[1]], dst_ref=o.at[j], send_sem=self.send_sems.at[3 * t + j],
            recv_sem=self.recv_sems.at[3 * t + j], device_id=(*chip, self.c), device_id_type=MESH)
            for t, (p, o) in enumerate(zip(self.part_refs, self.out_refs)) for j, chip in enumerate(self.chips)]

    def start(self):
        for cp in self._copies():
            cp.start()

    def finish(self):
        for cp in self._copies():
            cp.wait()


class _SmallAllReducePlan:
    def __init__(self, in_refs, out_refs, sib_refs, chip_refs, send_sems, recv_sems):
        self.in_refs, self.out_refs, self.sib_refs, self.chip_refs = in_refs, out_refs, sib_refs, chip_refs
        self.send_sems, self.recv_sems = send_sems, recv_sems
        self.n = len(in_refs)
        self.x, self.y, self.c, self.chips = _mesh_pos()

    def _copy(self, k, src, dst, to):
        return pltpu.make_async_remote_copy(src_ref=src, dst_ref=dst, send_sem=self.send_sems.at[k],
                                            recv_sem=self.recv_sems.at[k], device_id=to, device_id_type=MESH)

    def _first(self):
        return [self._copy(t, self.in_refs[t], self.sib_refs[t], (self.x, self.y, 1 - self.c)) for t in range(self.n)]

    def _second(self):
        return [self._copy(self.n + 3 * t + j, self.out_refs[t], self.chip_refs[t].at[j], (*chip, self.c))
                for t in range(self.n) for j, chip in enumerate(self.chips)]

    def start_sibling(self):
        for cp in self._first():
            cp.start()

    def sum_sibling_and_start_chips(self):
        for cp in self._first():
            cp.wait()
        for t in range(self.n):
            self.out_refs[t][...] = self.in_refs[t][...] + self.sib_refs[t][...]
        for cp in self._second():
            cp.start()

    def finish(self):
        for cp in self._second():
            cp.wait()
        for t in range(self.n):
            self.out_refs[t][...] = ((self.out_refs[t][...] + self.chip_refs[t][0])
                                     + (self.chip_refs[t][1] + self.chip_refs[t][2]))

    @staticmethod
    def scratch(arrays):
        return ([pltpu.VMEM(a.shape, F32) for a in arrays] + [pltpu.VMEM((3,) + a.shape, F32) for a in arrays]
                + _sem_pair(4 * len(arrays)))


def _sem_pair(n):
    return [pltpu.SemaphoreType.DMA((n,)), pltpu.SemaphoreType.DMA((n,))]


def _mm(a, b, *, dims, tm, tn, tk, out_dtype, name):
    if dims == "nn":
        m, k = a.shape
        n = b.shape[1]
        a_spec = pl.BlockSpec((tm, tk), lambda i, j, kk: (i, kk))
        b_spec = pl.BlockSpec((tk, tn), lambda i, j, kk: (kk, j))
        dot = _dot
    elif dims == "nt":
        m, k = a.shape
        n = b.shape[0]
        a_spec = pl.BlockSpec((tm, tk), lambda i, j, kk: (i, kk))
        b_spec = pl.BlockSpec((tn, tk), lambda i, j, kk: (j, kk))
        dot = _dot_nt
    else:
        k, m = a.shape
        n = b.shape[1]
        a_spec = pl.BlockSpec((tk, tm), lambda i, j, kk: (kk, i))
        b_spec = pl.BlockSpec((tk, tn), lambda i, j, kk: (kk, j))
        dot = _dot_tn
    assert m % tm == 0 and n % tn == 0 and k % tk == 0, (name, m, n, k)
    grid = (m // tm, n // tn, k // tk)
    nk = grid[2]
    assert nk == 1 or out_dtype == F32, name

    def body(a_ref, b_ref, o_ref):
        prod = dot(a_ref[...].astype(BF16), b_ref[...].astype(BF16))
        if nk == 1:
            o_ref[...] = prod.astype(out_dtype)
        else:
            kk = pl.program_id(2)

            @pl.when(kk == 0)
            def _():
                o_ref[...] = prod

            @pl.when(kk > 0)
            def _():
                o_ref[...] += prod

    return pl.pallas_call(
        body, grid=grid, in_specs=[a_spec, b_spec],
        out_specs=pl.BlockSpec((tm, tn), lambda i, j, kk: (i, j)),
        out_shape=jax.ShapeDtypeStruct((m, n), out_dtype),
        compiler_params=_params(("parallel", "parallel", "arbitrary")), name=name,
    )(a, b)


def _mm_pair_tn(a1, a2, b, *, tm, tk, name):
    k, m = a1.shape
    n = b.shape[1]
    assert m % tm == 0 and k % tk == 0 and a2.shape == a1.shape, name

    def body(a1_ref, a2_ref, b_ref, o1_ref, o2_ref):
        bv = b_ref[...]
        p1 = _dot_tn(a1_ref[...], bv)
        p2 = _dot_tn(a2_ref[...], bv)
        kk = pl.program_id(1)

        @pl.when(kk == 0)
        def _():
            o1_ref[...] = p1
            o2_ref[...] = p2

        @pl.when(kk > 0)
        def _():
            o1_ref[...] += p1
            o2_ref[...] += p2

    a_spec = pl.BlockSpec((tk, tm), lambda i, kk: (kk, i))
    o_spec = pl.BlockSpec((tm, n), lambda i, kk: (i, 0))
    return pl.pallas_call(
        body, grid=(m // tm, k // tk),
        in_specs=[a_spec, a_spec, pl.BlockSpec((tk, n), lambda i, kk: (kk, 0))],
        out_specs=[o_spec, o_spec],
        out_shape=[jax.ShapeDtypeStruct((m, n), F32)] * 2,
        compiler_params=_params(("parallel", "arbitrary")), name=name,
    )(a1, a2, b)


def _mm_pair_nt(a, w1_t, w2_t, shard, *, tm, tn, out_dtype, name):
    m, k = a.shape
    n = w1_t.shape[0]
    assert m % tm == 0 and n % tn == 0 and w2_t.shape == w1_t.shape, name
    grid = (m // tm, n // tn)
    n_steps = grid[0] * grid[1]

    def body(a_ref, w1_ref, w2_ref, shard_ref, o1_ref, o2_ref, gat_ref, send_sems, recv_sems):
        step = pl.program_id(0) * grid[1] + pl.program_id(1)
        gather = _GatherPlan([shard.shape], [shard_ref], [gat_ref], send_sems, recv_sems)

        @pl.when(step == 0)
        def _():
            gather.start([0])

        @pl.when(step == (2 * n_steps) // 3)
        def _():
            gather.forward([0])

        av = a_ref[...]
        o1_ref[...] = _dot_nt(av, w1_ref[...]).astype(out_dtype)
        o2_ref[...] = _dot_nt(av, w2_ref[...]).astype(out_dtype)

        @pl.when(step == n_steps - 1)
        def _():
            gather.finish([0])

    w_spec = pl.BlockSpec((tn, k), lambda i, j: (j, 0))
    o_spec = pl.BlockSpec((tm, tn), lambda i, j: (i, j))
    return pl.pallas_call(
        body, grid=grid,
        in_specs=[pl.BlockSpec((tm, k), lambda i, j: (i, 0)), w_spec, w_spec, ANY],
        out_specs=[o_spec, o_spec, ANY],
        out_shape=[jax.ShapeDtypeStruct((m, n), out_dtype)] * 2
        + [jax.ShapeDtypeStruct((N_SHARD,) + shard.shape, shard.dtype)],
        scratch_shapes=_sem_pair(6),
        compiler_params=_params(("arbitrary", "arbitrary")), name=name,
    )(a, w1_t, w2_t, shard)


def _out_proj_bwd(a, o, dy1, w_out):
    m = dy1.shape[0]
    tm = 1024

    def body(a_ref, o_ref, dy_ref, w_ref, dmix_ref, dw_ref):
        dy = dy_ref[...]
        dmix_ref[...] = _dot_nt(dy, w_ref[...])
        top = _dot_tn(a_ref[...], dy)
        bottom = _dot_tn(o_ref[...], dy)

        @pl.when(pl.program_id(0) == 0)
        def _():
            dw_ref[:A_WIDTH, :] = top
            dw_ref[A_WIDTH:, :] = bottom

        @pl.when(pl.program_id(0) > 0)
        def _():
            dw_ref[:A_WIDTH, :] += top
            dw_ref[A_WIDTH:, :] += bottom

    tile = lambda width: pl.BlockSpec((tm, width), lambda i: (i, 0))
    return pl.pallas_call(
        body, grid=(m // tm,),
        in_specs=[tile(A_WIDTH), tile(B_WIDTH), tile(D_MODEL), _full_spec((D_MODEL, D_MODEL))],
        out_specs=[tile(D_MODEL), _full_spec((D_MODEL, D_MODEL))],
        out_shape=[jax.ShapeDtypeStruct((m, D_MODEL), F32), jax.ShapeDtypeStruct((D_MODEL, D_MODEL), F32)],
        compiler_params=_params(("arbitrary",)), name="out_proj_bwd",
    )(a, o, dy1, w_out)


def _fused_rows(name, tm, mats, rows, vecs, fn, row_outs, acc_outs, exchange=()):
    m = mats[0][0].shape[0]
    nm, nr, nv, nro, nao, nx = len(mats), len(rows), len(vecs), len(row_outs), len(acc_outs), len(exchange)
    n_steps = m // tm

    def body(*refs):
        a_refs, w_refs = refs[:nm], refs[nm:2 * nm]
        pos = 2 * nm
        row_refs, vec_refs, part_refs = refs[pos:pos + nr], refs[pos + nr:pos + nr + nv], refs[pos + nr + nv:pos + nr + nv + nx]
        pos += nr + nv + nx
        out_refs, acc_refs, recv_refs = refs[pos:pos + nro], refs[pos + nro:pos + nro + nao], refs[pos + nro + nao:pos + nro + nao + nx]
        sems = refs[pos + nro + nao + nx:]
        i = pl.program_id(0)
        if nx:
            plan = _ChipExchangePlan(part_refs, recv_refs, *sems)

            @pl.when(i == 0)
            def _():
                plan.start()

        @pl.when(i == 0)
        def _():
            for r in acc_refs:
                r[...] = jnp.zeros_like(r)

        y = None
        for a_ref, w_ref, (_, _, dims, sl) in zip(a_refs, w_refs, mats):
            w = w_ref[...] if sl is None else w_ref[sl, :]
            part = (_dot if dims == "nn" else _dot_nt)(a_ref[...], w)
            y = part if y is None else y + part
        res = fn(y, *[r[...] for r in row_refs], *[v[...] for v in vec_refs])
        for r, val in zip(out_refs, res[:nro]):
            r[...] = val.astype(r.dtype)
        for r, val in zip(acc_refs, res[nro:]):
            r[...] += val

        if nx:
            @pl.when(i == n_steps - 1)
            def _():
                plan.finish()

    tile = lambda width: pl.BlockSpec((tm, width), lambda i: (i, 0))
    res = pl.pallas_call(
        body, grid=(n_steps,),
        in_specs=[tile(a.shape[1]) for a, _, _, _ in mats] + [_full_spec(w.shape) for _, w, _, _ in mats]
        + [tile(D_MODEL)] * nr + [_full_spec((1, D_MODEL))] * nv + [ANY] * nx,
        out_specs=[tile(D_MODEL)] * nro + [_full_spec(s) for s in acc_outs] + [ANY] * nx,
        out_shape=[jax.ShapeDtypeStruct((m, D_MODEL), dt) for dt in row_outs]
        + [jax.ShapeDtypeStruct(s, F32) for s in acc_outs]
        + [jax.ShapeDtypeStruct((3,) + p.shape[1:], p.dtype) for p in exchange],
        scratch_shapes=_sem_pair(3 * nx) if nx else [],
        compiler_params=_params(("arbitrary",)), name=name,
    )(*[a for a, _, _, _ in mats], *[w for _, w, _, _ in mats], *rows, *vecs, *exchange)
    return list(res[:nro + nao]), list(res[nro + nao:])


def _vec_spec(width=D_MODEL):
    return pl.BlockSpec((1, width), lambda i: (0, 0))


def _rstd(v):
    return lax.rsqrt(jnp.mean(v * v, axis=-1, keepdims=True) + NORM_EPS)


def _mid_fwd_rows(y1, x0, g2, g3):
    x1 = x0 + y1 * _rstd(y1) * g2
    return y1, x1, x1 * _rstd(x1) * g3


def _rms_bwd_rows(dout, v, g):
    r = _rstd(v)
    n = v * r
    dn = dout * g
    dv = r * (dn - n * jnp.mean(dn * n, axis=-1, keepdims=True))
    dg = jnp.sum(dout * n, axis=0, keepdims=True)
    return dv, dg


def _loss_head_rows(y2, x1, tgt, g4):
    x2 = x1 + y2 * _rstd(y2) * g4
    err = x2 - tgt
    loss = 0.5 * jnp.sum(jnp.mean(err * err, axis=-1, keepdims=True), axis=0, keepdims=True)
    dx2 = err * (1.0 / D_MODEL)
    dy2, dg4 = _rms_bwd_rows(dx2, y2, g4)
    return dx2, dy2, dg4, loss


def _mid_bwd_rows(dh2, x1, y1, dx2, g2, g3):
    d3, dg3 = _rms_bwd_rows(dh2, x1, g3)
    dx1 = dx2 + d3
    dy1, dg2 = _rms_bwd_rows(dx1, y1, g2)
    return dx1, dy1, dg2, dg3


def _in_bwd_rows(dh1, x0, dx1, g1):
    d1, dg1 = _rms_bwd_rows(dh1, x0, g1)
    return dx1 + d1, dg1


GATE_ROWS = 512


def _group_mean_matrix():
    p = np.zeros((A_WIDTH, A_WIDTH), np.float32)
    for g in range(A_GROUPS):
        p[g * HEAD_DIM:(g + 1) * HEAD_DIM, g * HEAD_DIM:(g + 1) * HEAD_DIM] = 1.0 / HEAD_DIM
    return jnp.asarray(p)


def _group_masks(width=A_WIDTH):
    lane = lax.broadcasted_iota(jnp.int32, (1, width), 1)
    return [(lane >= g * HEAD_DIM) & (lane < (g + 1) * HEAD_DIM) for g in range(width // HEAD_DIM)]


GROUP_SUM_PRECISION = lax.Precision.HIGH


def _layernorm_groups(vg, pavg):
    hi = GROUP_SUM_PRECISION
    mu = _dot(vg, pavg, hi)
    xc = vg - mu
    var = _dot(xc * xc, pavg, hi)
    rstd = lax.rsqrt(var + NORM_EPS)
    return xc * rstd, rstd


def _spatial_mix(w_bf, vn_chunk_bf, masks, bz):
    z = bz
    for g in range(A_GROUPS):
        z = z + jnp.where(masks[g], _dot(w_bf[g], vn_chunk_bf), 0.0)
    return z


def _full_spec(shape):
    return pl.BlockSpec(shape, lambda i: tuple(0 for _ in shape))


def _gate_fwd_rows(u, v, lg, lb, w_ref, bz, pavg, a_ref):
    masks = _group_masks()
    row = lax.broadcasted_iota(jnp.int32, (CHUNK, CHUNK), 0)
    col = lax.broadcasted_iota(jnp.int32, (CHUNK, CHUNK), 1)
    w_bf = [jnp.where(row >= col, w_ref[g], 0.0).astype(BF16) for g in range(A_GROUPS)]
    ug = _gelu(u)
    vhat, _ = _layernorm_groups(_gelu(v), pavg)
    vn = vhat * lg + lb
    for c in range(GATE_ROWS // CHUNK):
        sl = slice(c * CHUNK, (c + 1) * CHUNK)
        z = _spatial_mix(w_bf, vn[sl].astype(BF16), masks, bz)
        a_ref[sl, :] = (ug[sl] * z).astype(BF16)


def _gate_bwd(uv, dmix, ln_g, ln_b, w_s, w_st, bz, grads):
    m = uv.shape[0]
    pavg = _group_mean_matrix()
    nsteps = m // GATE_ROWS
    nx = len(grads)
    shapes = [g.shape for g in grads]

    def body(u_ref, v_ref, da_ref, lg_ref, lb_ref, w_ref, wt_ref, bz_ref, p_ref, *rest):
        grad_refs = rest[:nx]
        duv_ref, dlg_ref, dlb_ref, dw_ref, dbz_ref = rest[nx:nx + 5]
        recv_refs = rest[nx + 5:2 * nx + 5]
        exchange = _SiblingExchangePlan(shapes, grad_refs, recv_refs, *rest[2 * nx + 5:])
        i = pl.program_id(0)

        @pl.when(i == 0)
        def _():
            exchange.start()
            dlg_ref[...] = jnp.zeros_like(dlg_ref)
            dlb_ref[...] = jnp.zeros_like(dlb_ref)
            dw_ref[...] = jnp.zeros_like(dw_ref)
            dbz_ref[...] = jnp.zeros_like(dbz_ref)

        hi = GROUP_SUM_PRECISION
        masks = _group_masks()
        row = lax.broadcasted_iota(jnp.int32, (CHUNK, CHUNK), 0)
        col = lax.broadcasted_iota(jnp.int32, (CHUNK, CHUNK), 1)
        tril = row >= col
        w_bf = [jnp.where(tril, w_ref[g], 0.0).astype(BF16) for g in range(A_GROUPS)]
        wt_bf = [jnp.where(col >= row, wt_ref[g], 0.0).astype(BF16) for g in range(A_GROUPS)]
        pavg_v = p_ref[...]
        lg = lg_ref[...]
        ug, dug = _gelu_and_grad(u_ref[...])
        vg, dvg_dx = _gelu_and_grad(v_ref[...])
        vhat, rstd = _layernorm_groups(vg, pavg_v)
        vn = vhat * lg + lb_ref[...]
        da = da_ref[...]
        bz = bz_ref[...]
        for c in range(GATE_ROWS // CHUNK):
            sl = slice(c * CHUNK, (c + 1) * CHUNK)
            vn_bf = vn[sl].astype(BF16)
            z = _spatial_mix(w_bf, vn_bf, masks, bz)
            dz = da[sl] * ug[sl]
            duv_ref[sl, 0:A_WIDTH] = (da[sl] * z * dug[sl]).astype(BF16)
            dbz_ref[...] += dz
            dz_bf = dz.astype(BF16)
            dvn = jnp.zeros((CHUNK, A_WIDTH), F32)
            for g in range(A_GROUPS):
                dz_g = jnp.where(masks[g], dz, 0.0).astype(BF16)
                dw_ref[g] += jnp.where(tril, _dot_nt(dz_g, vn_bf), 0.0)
                dvn = dvn + jnp.where(masks[g], _dot(wt_bf[g], dz_bf), 0.0)
            vh = vhat[sl]
            dlb_ref[...] += jnp.sum(dvn, axis=0, keepdims=True)
            dlg_ref[...] += jnp.sum(dvn * vh, axis=0, keepdims=True)
            dvh = dvn * lg
            m1 = _dot(dvh, pavg_v, hi)
            m2 = _dot(dvh * vh, pavg_v, hi)
            duv_ref[sl, A_WIDTH:2 * A_WIDTH] = (rstd[sl] * (dvh - m1 - vh * m2) * dvg_dx[sl]).astype(BF16)

        @pl.when(i == nsteps - 1)
        def _():
            dbz_ref[...] = _dot(dbz_ref[...], pavg_v * float(HEAD_DIM), hi)
            exchange.finish()

    res = pl.pallas_call(
        body, grid=(nsteps,),
        in_specs=[pl.BlockSpec((GATE_ROWS, A_WIDTH), lambda i: (i, 0)),
                  pl.BlockSpec((GATE_ROWS, A_WIDTH), lambda i: (i, 1)),
                  pl.BlockSpec((GATE_ROWS, A_WIDTH), lambda i: (i, 0)),
                  _full_spec((1, A_WIDTH)), _full_spec((1, A_WIDTH)), _full_spec((A_GROUPS, CHUNK, CHUNK)),
                  _full_spec((A_GROUPS, CHUNK, CHUNK)), _full_spec((CHUNK, A_WIDTH)),
                  _full_spec((A_WIDTH, A_WIDTH))] + [ANY] * nx,
        out_specs=[pl.BlockSpec((GATE_ROWS, 2 * A_WIDTH), lambda i: (i, 0)),
                   _full_spec((1, A_WIDTH)), _full_spec((1, A_WIDTH)), _full_spec((A_GROUPS, CHUNK, CHUNK)),
                   _full_spec((CHUNK, A_WIDTH))] + [ANY] * nx,
        out_shape=[jax.ShapeDtypeStruct((m, IN_COLS), BF16),
                   jax.ShapeDtypeStruct((1, A_WIDTH), F32), jax.ShapeDtypeStruct((1, A_WIDTH), F32),
                   jax.ShapeDtypeStruct((A_GROUPS, CHUNK, CHUNK), F32),
                   jax.ShapeDtypeStruct((CHUNK, A_WIDTH), F32)]
        + [jax.ShapeDtypeStruct((N_SHARD, s[1] // 2, s[2]), F32) for s in shapes],
        scratch_shapes=_sem_pair(nx),
        compiler_params=_params(("arbitrary",)), name="gate_bwd",
    )(uv, uv, dmix, ln_g, ln_b, w_s, w_st, bz, pavg, *grads)
    return res[:5], list(res[5:])


Q_BLOCK = 128
PAIR = 2 * HEAD_DIM
N_PAIR = B_HEADS // 2
N_CFG = len(DILATED)
BLOCKS_PER_CFG = SEQ // Q_BLOCK
QKV_SLABS = 3 * N_PAIR
FWD_BLOCKS_PER_TRIP = 8
BWD_BLOCKS_PER_TRIP = 4


def _t5_bucket_np(dist, dtype):
    max_exact = NUM_BUCKETS // 2
    d = np.maximum(dist, 1).astype(dtype)
    large = max_exact + (np.log(d / dtype(max_exact)) / dtype(math.log(MAX_DISTANCE / max_exact))
                         * dtype(NUM_BUCKETS - max_exact))
    large = np.minimum(large.astype(np.int32), NUM_BUCKETS - 1)
    return np.where(dist < max_exact, dist, large)


def _bucket_tables():
    i = np.arange(Q_BLOCK)[:, None]
    j = np.arange(Q_BLOCK)[None, :]
    tables = []
    for _, dil in DILATED:
        rel_prev = Q_BLOCK + i - j
        rel_cur = i - j
        rel = np.concatenate([rel_prev, rel_cur], axis=1)
        valid = np.concatenate([rel_prev <= Q_BLOCK, rel_cur >= 0], axis=1)
        dist = np.maximum(rel, 0) * dil
        b32 = _t5_bucket_np(dist, np.float32)
        b64 = _t5_bucket_np(dist, np.float64)
        assert np.array_equal(b32, b64)
        tables.append(np.where(valid, b32, -1).astype(np.int32))
    return np.stack(tables)


def _present_buckets(buckets_np):
    return [sorted(set(int(v) for v in np.unique(buckets_np[c]) if v >= 0)) for c in range(N_CFG)]


def _bias_tables_body(buckets_np):
    present = _present_buckets(buckets_np)

    def tables(rb_ref, bk_ref, o_ref, ot_ref):
        for c in range(N_CFG):
            bk = bk_ref[c]
            for h in range(B_HEADS):
                acc = jnp.full((Q_BLOCK, 2 * Q_BLOCK), NEG_INF, F32)
                for b in present[c]:
                    acc = jnp.where(bk == b, rb_ref[h, b], acc)
                o_ref[c, h] = acc
                ot_ref[c, h] = acc.T

    return tables


def _proj_fwd(x, g1, w_in_t, ln_g, ln_b, w_s, bz):
    m = x.shape[0]
    tm = GATE_ROWS
    pavg = _group_mean_matrix()

    def body(x_ref, g_ref, w_ref, lg_ref, lb_ref, ws_ref, bz_ref, p_ref, h_ref, uv_ref, qkv_ref, a_ref):
        xv = x_ref[...]
        h = (xv * _rstd(xv) * g_ref[...]).astype(BF16)
        h_ref[...] = h
        acc = _dot_nt(h, w_ref[...])
        uv_ref[...] = acc[:, :2 * A_WIDTH]
        for s in range(QKV_SLABS):
            qkv_ref[s] = acc[:, 2 * A_WIDTH + s * PAIR:2 * A_WIDTH + (s + 1) * PAIR]
        _gate_fwd_rows(acc[:, :A_WIDTH], acc[:, A_WIDTH:2 * A_WIDTH], lg_ref[...], lb_ref[...], ws_ref,
                       bz_ref[...], p_ref[...], a_ref)

    return pl.pallas_call(
        body, grid=(m // tm,),
        in_specs=[pl.BlockSpec((tm, D_MODEL), lambda i: (i, 0)), _vec_spec(),
                  pl.BlockSpec((IN_COLS, D_MODEL), lambda i: (0, 0)),
                  _full_spec((1, A_WIDTH)), _full_spec((1, A_WIDTH)), _full_spec((A_GROUPS, CHUNK, CHUNK)),
                  _full_spec((CHUNK, A_WIDTH)), _full_spec((A_WIDTH, A_WIDTH))],
        out_specs=[pl.BlockSpec((tm, D_MODEL), lambda i: (i, 0)),
                   pl.BlockSpec((tm, 2 * A_WIDTH), lambda i: (i, 0)),
                   pl.BlockSpec((QKV_SLABS, tm, PAIR), lambda i: (0, i, 0)),
                   pl.BlockSpec((tm, A_WIDTH), lambda i: (i, 0))],
        out_shape=[jax.ShapeDtypeStruct((m, D_MODEL), BF16), jax.ShapeDtypeStruct((m, 2 * A_WIDTH), F32),
                   jax.ShapeDtypeStruct((QKV_SLABS, m, PAIR), F32), jax.ShapeDtypeStruct((m, A_WIDTH), BF16)],
        compiler_params=_params(("parallel",)), name="proj_fwd",
    )(x, g1, w_in_t, ln_g, ln_b, w_s, bz, pavg)


def _pair_masks():
    lane = lax.broadcasted_iota(jnp.int32, (1, PAIR), 1)
    return [lane < HEAD_DIM, lane >= HEAD_DIM]


def _block_rows(idx, dil):
    static = isinstance(idx, int)
    r, n = idx % dil, idx // dil

    def rows_of(block):
        start = r + (dil * Q_BLOCK) * block
        if dil == 1:
            return pl.ds(start if static else pl.multiple_of(start, Q_BLOCK), Q_BLOCK)
        return pl.ds(start, Q_BLOCK, stride=dil)

    prev = rows_of(n - 1) if not static or n > 0 else None
    return rows_of(n), prev


def _attn_fwd(qkv, bias, batch, shards):
    m = qkv.shape[1]
    comb_rows = 256
    nt = len(shards)
    shapes = [sh.shape for sh in shards]
    n_steps = batch * N_PAIR
    early, late = list(range(nt // 2)), list(range(nt // 2, nt))

    def body(q_ref, k_ref, v_ref, b_ref, *rest):
        shard_refs = rest[:nt]
        o_ref, l_ref = rest[nt:nt + 2]
        gat_refs = rest[nt + 2:2 * nt + 2]
        scratch = rest[2 * nt + 2:]
        oc_refs, lc_refs = scratch[:N_CFG], scratch[N_CFG:2 * N_CFG]
        step = pl.program_id(0) * N_PAIR + pl.program_id(1)
        gather = _GatherPlan(shapes, shard_refs, gat_refs, *scratch[2 * N_CFG:])

        @pl.when(step == 0)
        def _():
            gather.start(early + late)

        @pl.when(step == n_steps // 2)
        def _():
            gather.forward(early)

        @pl.when(step == n_steps - 2)
        def _():
            gather.forward(late)

        masks = _pair_masks()
        for ci, (_, dil) in enumerate(DILATED):
            nb = SEQ // dil // Q_BLOCK

            def block(trip, ci=ci, dil=dil, nb=nb):
                work = []
                for u in range(FWD_BLOCKS_PER_TRIP):
                    rows, prow = _block_rows(trip * FWD_BLOCKS_PER_TRIP + u, dil)
                    has_prev = nb > 1 and prow is not None
                    q = q_ref[rows, :] * 0.125
                    kc = k_ref[rows, :].astype(BF16)
                    vc = v_ref[rows, :]
                    kp = k_ref[prow, :].astype(BF16) if has_prev else None
                    vp = v_ref[prow, :] if has_prev else None
                    tiles = []
                    for h in range(2):
                        qh = jnp.where(masks[h], q, 0.0).astype(BF16)
                        sc = _dot_nt(qh, kc) + b_ref[ci, h, :, Q_BLOCK:]
                        sp = _dot_nt(qh, kp) + b_ref[ci, h, :, :Q_BLOCK] if has_prev else None
                        tiles.append((sc, sp))
                    work.append((rows, vc, vp, tiles))
                probs = []
                for _, _, _, tiles in work:
                    ps = []
                    for sc, sp in tiles:
                        mx = jnp.max(sc if sp is None else jnp.maximum(sc, sp), axis=1, keepdims=True)
                        pc = jnp.exp(sc - mx).astype(BF16)
                        pp = None if sp is None else jnp.exp(sp - mx).astype(BF16)
                        ps.append((mx, pc, pp))
                    probs.append(ps)
                for (rows, vc, vp, _), ps in zip(work, probs):
                    res = []
                    for h, (_, pc, pp) in enumerate(ps):
                        r = _dot(pc, jnp.where(masks[h], vc, 1.0).astype(BF16))
                        if pp is not None:
                            r = r + _dot(pp, jnp.where(masks[h], vp, 1.0).astype(BF16))
                        res.append(r)
                    num = jnp.where(masks[0], res[0], res[1])
                    den = pltpu.roll(jnp.where(masks[0], res[1], res[0]), HEAD_DIM, 1)
                    oc_refs[ci][rows, :] = num / den
                    lc_refs[ci][rows, :] = jnp.where(masks[0], ps[0][0], ps[1][0]) + jnp.log(den)

            for trip in range(BLOCKS_PER_CFG // FWD_BLOCKS_PER_TRIP):
                block(trip)

        def combine(i, carry):
            rr = pl.ds(pl.multiple_of(i * comb_rows, comb_rows), comb_rows)
            ls = [lc_refs[c][rr, :] for c in range(N_CFG)]
            mx = functools.reduce(jnp.maximum, ls)
            ws = [jnp.exp(l - mx) for l in ls]
            tot = functools.reduce(lambda a, b: a + b, ws)
            o = functools.reduce(lambda a, b: a + b, [ws[c] * oc_refs[c][rr, :] for c in range(N_CFG)]) / tot
            o_ref[rr, :] = o.astype(BF16)
            l_ref[rr, :] = mx + jnp.log(tot)
            return carry

        lax.fori_loop(0, SEQ // comb_rows, combine, 0)

        @pl.when(step == n_steps - 1)
        def _():
            gather.finish(early + late)

    def slab(first):
        return pl.BlockSpec((None, SEQ, PAIR), lambda b, p: (first + p, b, 0))

    nat = pl.BlockSpec((SEQ, PAIR), lambda b, p: (b, p))
    res = pl.pallas_call(
        body, grid=(batch, N_PAIR),
        in_specs=[slab(0), slab(N_PAIR), slab(2 * N_PAIR),
                  pl.BlockSpec((N_CFG, 2, Q_BLOCK, 2 * Q_BLOCK), lambda b, p: (0, p, 0, 0))] + [ANY] * nt,
        out_specs=[nat, nat] + [ANY] * nt,
        out_shape=[jax.ShapeDtypeStruct((m, B_WIDTH), BF16), jax.ShapeDtypeStruct((m, B_WIDTH), F32)]
        + [jax.ShapeDtypeStruct((N_SHARD,) + sh.shape, sh.dtype) for sh in shards],
        scratch_shapes=[pltpu.VMEM((SEQ, PAIR), F32)] * (2 * N_CFG) + _sem_pair(6 * nt),
        compiler_params=_params(("arbitrary", "arbitrary")), name="attn_fwd",
    )(qkv, qkv, qkv, bias, *shards)
    return res[0], res[1], list(res[2:])


def _attn_bwd(qkv, dmix, o, lse, bias_t, dproj, batch, parts, smalls):
    m = qkv.shape[1]
    nt, ns = len(parts), len(smalls)
    n_steps = N_PAIR * batch

    def body(q_ref, k_ref, v_ref, do_ref, o_ref, l_ref, b_ref, *rest):
        part_refs = rest[1:nt + 1]
        small_refs = rest[nt + 1:nt + 1 + ns]
        pos = nt + 1 + ns
        dproj_ref, ds_ref = rest[pos:pos + 2]
        recv_refs = rest[pos + 2:pos + 2 + nt]
        sum_refs = rest[pos + 2 + nt:pos + 2 + nt + ns]
        pos += 2 + nt + ns
        dq_acc, dk_acc, dv_acc, d_scr, stage, stage_sems, send_sems, recv_sems = rest[pos:pos + 8]
        allreduce = _SmallAllReducePlan(small_refs, sum_refs, rest[pos + 8:pos + 8 + ns],
                                        rest[pos + 8 + ns:pos + 8 + 2 * ns], *rest[pos + 8 + 2 * ns:])
        pair, seq = pl.program_id(0), pl.program_id(1)
        step = pair * batch + seq
        exchange = _ChipExchangePlan(part_refs, recv_refs, send_sems, recv_sems)

        @pl.when(step == 0)
        def _():
            allreduce.start_sibling()

        @pl.when(step == n_steps // 2)
        def _():
            allreduce.sum_sibling_and_start_chips()

        def stage_copies():
            rows = pl.ds(pl.multiple_of(seq * SEQ, SEQ), SEQ)
            return [pltpu.make_async_copy(
                stage.at[k],
                dproj_ref.at[rows, pl.ds(pl.multiple_of(2 * A_WIDTH + k * B_WIDTH + pair * PAIR, PAIR), PAIR)],
                stage_sems.at[k]) for k in range(3)]

        @pl.when(step == 0)
        def _():
            exchange.start()

        @pl.when(pl.program_id(1) == 0)
        def _():
            ds_ref[...] = jnp.zeros_like(ds_ref)

        dq_acc[...] = jnp.zeros_like(dq_acc)
        dk_acc[...] = jnp.zeros_like(dk_acc)
        dv_acc[...] = jnp.zeros_like(dv_acc)
        d_scr[...] = do_ref[...] * o_ref[...].astype(F32)
        masks = _pair_masks()

        def stack_heads(t):
            return jnp.concatenate([jnp.where(masks[0], t, 0.0), jnp.where(masks[1], t, 0.0)], axis=0).astype(BF16)

        for ci, (_, dil) in enumerate(DILATED):
            nb = SEQ // dil // Q_BLOCK

            def block(trip, carry, ci=ci, dil=dil, nb=nb):
                first = []
                for u in range(BWD_BLOCKS_PER_TRIP):
                    rows, prow = _block_rows(trip * BWD_BLOCKS_PER_TRIP + u, dil)
                    has_prev = nb > 1 and prow is not None
                    if has_prev:
                        kcat = jnp.concatenate([k_ref[prow, :], k_ref[rows, :]], axis=0).astype(BF16)
                        vcat = jnp.concatenate([v_ref[prow, :], v_ref[rows, :]], axis=0).astype(BF16)
                    else:
                        kcat = k_ref[rows, :].astype(BF16)
                        vcat = v_ref[rows, :].astype(BF16)
                    qst = stack_heads(q_ref[rows, :] * 0.125)
                    dost = stack_heads(do_ref[rows, :])
                    lt = l_ref[rows, :].T
                    dt = d_scr[rows, :].T
                    lrow = jnp.concatenate([lt[0:1], lt[HEAD_DIM:HEAD_DIM + 1]], axis=1)
                    drow = jnp.concatenate([jnp.sum(dt[:HEAD_DIM], axis=0, keepdims=True),
                                            jnp.sum(dt[HEAD_DIM:], axis=0, keepdims=True)], axis=1)
                    first.append((has_prev, rows, prow, kcat, qst, dost, lrow, drow,
                                  _dot_nt(kcat, qst), _dot_nt(vcat, dost)))
                second = []
                for has_prev, rows, prow, kcat, qst, dost, lrow, drow, st, dpt in first:
                    keys = slice(0, 2 * Q_BLOCK) if has_prev else slice(Q_BLOCK, 2 * Q_BLOCK)
                    bt = jnp.concatenate([b_ref[ci, 0, keys, :], b_ref[ci, 1, keys, :]], axis=1)
                    pt = jnp.exp(st + bt - lrow)
                    dst = pt * (dpt - drow)
                    ds_ref[ci, 0, keys, :] += dst[:, :Q_BLOCK]
                    ds_ref[ci, 1, keys, :] += dst[:, Q_BLOCK:]
                    second.append((has_prev, rows, prow, kcat, qst, dost, pt.astype(BF16), dst.astype(BF16)))
                for has_prev, rows, prow, kcat, qst, dost, pt_bf, dst_bf in second:
                    dk = _dot(dst_bf, qst)
                    dv = _dot(pt_bf, dost)
                    dq2 = _dot_tn(dst_bf, kcat)
                    dq_acc[rows, :] += jnp.where(masks[0], dq2[:Q_BLOCK], dq2[Q_BLOCK:]) * 0.125
                    if has_prev:
                        dk_acc[prow, :] += dk[:Q_BLOCK]
                        dv_acc[prow, :] += dv[:Q_BLOCK]
                        dk_acc[rows, :] += dk[Q_BLOCK:]
                        dv_acc[rows, :] += dv[Q_BLOCK:]
                    else:
                        dk_acc[rows, :] += dk
                        dv_acc[rows, :] += dv
                return carry

            for trip in range(BLOCKS_PER_CFG // BWD_BLOCKS_PER_TRIP):
                block(trip, 0)

        @pl.when(step > 0)
        def _():
            for cp in stage_copies():
                cp.wait()

        stage[0] = dq_acc[...].astype(BF16)
        stage[1] = dk_acc[...].astype(BF16)
        stage[2] = dv_acc[...].astype(BF16)
        for cp in stage_copies():
            cp.start()

        @pl.when(step == n_steps - 1)
        def _():
            for cp in stage_copies():
                cp.wait()
            exchange.finish()
            allreduce.finish()

    def slab(first):
        return pl.BlockSpec((None, SEQ, PAIR), lambda p, b: (first + p, b, 0))

    nat = pl.BlockSpec((SEQ, PAIR), lambda p, b: (b, p))
    tbl = pl.BlockSpec((N_CFG, 2, 2 * Q_BLOCK, Q_BLOCK), lambda p, b: (0, p, 0, 0))
    acc = pltpu.VMEM((SEQ, PAIR), F32)
    vm = pl.BlockSpec(memory_space=pltpu.VMEM)
    res = pl.pallas_call(
        body, grid=(N_PAIR, batch),
        in_specs=[slab(0), slab(N_PAIR), slab(2 * N_PAIR),
                  pl.BlockSpec((SEQ, PAIR), lambda p, b: (b, A_WIDTH // PAIR + p)), nat, nat, tbl]
        + [ANY] * (nt + 1) + [vm] * ns,
        out_specs=[ANY, tbl] + [ANY] * nt + [vm] * ns,
        out_shape=[jax.ShapeDtypeStruct(dproj.shape, dproj.dtype),
                   jax.ShapeDtypeStruct((N_CFG, B_HEADS, 2 * Q_BLOCK, Q_BLOCK), F32)]
        + [jax.ShapeDtypeStruct((3,) + p.shape[1:], p.dtype) for p in parts]
        + [jax.ShapeDtypeStruct(a.shape, F32) for a in smalls],
        input_output_aliases={7: 0},
        scratch_shapes=[acc, acc, acc, acc, pltpu.VMEM((3, SEQ, PAIR), BF16), pltpu.SemaphoreType.DMA((3,))]
        + _sem_pair(3 * nt) + _SmallAllReducePlan.scratch(smalls),
        compiler_params=_params(("arbitrary", "arbitrary")), name="attn_bwd",
    )(qkv, qkv, qkv, dmix, o, lse, bias_t, dproj, *parts, *smalls)
    return res[0], res[1], list(res[2:2 + nt]), list(res[2 + nt:])


def _rel_bias_grad(ds, buckets_np, grads):
    present = _present_buckets(buckets_np)
    nx = len(grads)
    shapes = [g.shape for g in grads]

    def body(bk_ref, ds_ref, *rest):
        o_ref = rest[nx]
        acc_ref = rest[2 * nx + 1]
        exchange = _SiblingExchangePlan(shapes, rest[:nx], rest[nx + 1:2 * nx + 1], *rest[2 * nx + 2:])
        exchange.start()
        acc_ref[...] = jnp.zeros_like(acc_ref)
        for c in range(N_CFG):
            bk = bk_ref[c]
            for h in range(B_HEADS):
                dsv = ds_ref[c, h]
                for b in present[c]:
                    part = jnp.sum(jnp.where(bk == b, dsv, 0.0), axis=0, keepdims=True)
                    acc_ref[pl.ds(h * NUM_BUCKETS + b, 1), :] += part
        o_ref[...] = jnp.sum(acc_ref[...], axis=1, keepdims=True)
        exchange.finish()

    vm = pl.BlockSpec(memory_space=pltpu.VMEM)
    res = pl.pallas_call(
        body, in_specs=[vm, vm] + [ANY] * nx, out_specs=[vm] + [ANY] * nx,
        out_shape=[jax.ShapeDtypeStruct((B_HEADS * NUM_BUCKETS, 1), F32)]
        + [jax.ShapeDtypeStruct((N_SHARD, s[1] // 2, s[2]), F32) for s in shapes],
        scratch_shapes=[pltpu.VMEM((B_HEADS * NUM_BUCKETS, buckets_np.shape[-1]), F32)] + _sem_pair(nx),
        compiler_params=_params(), name="rel_bias_grad",
    )(jnp.asarray(buckets_np), ds, *grads)
    return res[0], list(res[1:])


def _row_index():
    return lax.broadcasted_iota(jnp.int32, (SEQ, LANE_BLOCK), 0)


def _shift_down(x, k, row):
    return jnp.where(row >= k, pltpu.roll(x, k, 0), 0.0)


def _shift_up(x, k, row):
    return jnp.where(row < SEQ - k, pltpu.roll(x, SEQ - k, 0), 0.0)


def _convgate_fwd(gate, up, conv_w, conv_b, batch):
    m = gate.shape[0]

    def body(g_ref, u_ref, w_ref, b_ref, a_ref):
        g = g_ref[...].astype(F32)
        w = w_ref[...]
        row = _row_index()
        c = b_ref[...] + w[0:1] * _shift_down(g, 2, row) + w[1:2] * _shift_down(g, 1, row) + w[2:3] * g
        a_ref[...] = (_gelu(c) * u_ref[...].astype(F32)).astype(BF16)

    blk = pl.BlockSpec((SEQ, LANE_BLOCK), lambda b, j: (b, j))
    return pl.pallas_call(
        body, grid=(batch, D_FF // LANE_BLOCK),
        in_specs=[blk, blk, pl.BlockSpec((3, LANE_BLOCK), lambda b, j: (0, j)),
                  pl.BlockSpec((1, LANE_BLOCK), lambda b, j: (0, j))],
        out_specs=blk,
        out_shape=jax.ShapeDtypeStruct((m, D_FF), BF16),
        compiler_params=_params(("parallel", "parallel")), name="convgate_fwd",
    )(gate, up, conv_w, conv_b)


def _convgate_bwd(gate, up, dact, conv_w, conv_b, batch):
    m = gate.shape[0]

    def body(g_ref, u_ref, da_ref, w_ref, b_ref, dg_ref, du_ref, dw_ref, db_ref):
        @pl.when(pl.program_id(1) == 0)
        def _():
            dw_ref[...] = jnp.zeros_like(dw_ref)
            db_ref[...] = jnp.zeros_like(db_ref)

        g = g_ref[...].astype(F32)
        w = w_ref[...]
        row = _row_index()
        g1 = _shift_down(g, 1, row)
        g2 = _shift_down(g, 2, row)
        c = b_ref[...] + w[0:1] * g2 + w[1:2] * g1 + w[2:3] * g
        gg, dgg = _gelu_and_grad(c)
        da = da_ref[...].astype(F32)
        du_ref[...] = (da * gg).astype(BF16)
        dc = da * u_ref[...].astype(F32) * dgg
        db_ref[...] += jnp.sum(dc, axis=0, keepdims=True)
        dw_ref[0:1, :] += jnp.sum(dc * g2, axis=0, keepdims=True)
        dw_ref[1:2, :] += jnp.sum(dc * g1, axis=0, keepdims=True)
        dw_ref[2:3, :] += jnp.sum(dc * g, axis=0, keepdims=True)
        dg_ref[...] = (w[2:3] * dc + w[1:2] * _shift_up(dc, 1, row) + w[0:1] * _shift_up(dc, 2, row)).astype(BF16)

    blk = pl.BlockSpec((SEQ, LANE_BLOCK), lambda j, b: (b, j))
    wspec = pl.BlockSpec((3, LANE_BLOCK), lambda j, b: (0, j))
    bspec = pl.BlockSpec((1, LANE_BLOCK), lambda j, b: (0, j))
    return pl.pallas_call(
        body, grid=(D_FF // LANE_BLOCK, batch),
        in_specs=[blk, blk, blk, wspec, bspec],
        out_specs=[blk, blk, wspec, bspec],
        out_shape=[jax.ShapeDtypeStruct((m, D_FF), BF16), jax.ShapeDtypeStruct((m, D_FF), BF16),
                   jax.ShapeDtypeStruct((3, D_FF), F32), jax.ShapeDtypeStruct((1, D_FF), F32)],
        compiler_params=_params(("parallel", "arbitrary")), name="convgate_bwd",
    )(gate, up, dact, conv_w, conv_b)


def _gather_weights(shards, conv_w_shard, rel_bias, buckets_np):
    nt = len(shards)
    shapes = [sh.shape for sh in shards]
    ts = list(range(nt))
    tables = _bias_tables_body(buckets_np)

    def body(*refs):
        shard_refs = refs[:nt]
        cw_ref, rb_ref, bk_ref = refs[nt:nt + 3]
        out_refs = refs[nt + 3:2 * nt + 3]
        cw_out, bias_ref, bias_t_ref = refs[2 * nt + 3:2 * nt + 6]
        send_sems, recv_sems, cw_send, cw_recv = refs[2 * nt + 6:]
        plan = _RelayGatherPlan(shapes, shard_refs, out_refs, send_sems, recv_sems)
        x, y, c, chips = _mesh_pos()

        def cw_copy(j, src, dst, chip):
            return pltpu.make_async_remote_copy(src_ref=src, dst_ref=dst, send_sem=cw_send.at[j],
                                                recv_sem=cw_recv.at[j], device_id=(*chip, c), device_id_type=MESH)

        plan.start(ts)
        cw_sends = [cw_copy(j, cw_ref, cw_out.at[2 * x + y], chip) for j, chip in enumerate(chips)]
        for cp in cw_sends:
            cp.start()
        plan.relay(ts)
        tables(rb_ref, bk_ref, bias_ref, bias_t_ref)
        plan.forward(ts)
        for j, chip in enumerate(chips):
            dst = cw_out.at[2 * chip[0] + chip[1]]
            cw_copy(j, dst, dst, chip).wait_recv()
        plan.finish(ts)
        for cp in cw_sends:
            cp.wait_send()

    out_shape = [jax.ShapeDtypeStruct((N_SHARD,) + sh.shape, sh.dtype) for sh in shards]
    out_shape.append(jax.ShapeDtypeStruct((N_SHARD,) + conv_w_shard.shape, conv_w_shard.dtype))
    out_shape += [jax.ShapeDtypeStruct((N_CFG, B_HEADS, Q_BLOCK, 2 * Q_BLOCK), F32),
                  jax.ShapeDtypeStruct((N_CFG, B_HEADS, 2 * Q_BLOCK, Q_BLOCK), F32)]
    vm = pl.BlockSpec(memory_space=pltpu.VMEM)
    res = pl.pallas_call(
        body, in_specs=[ANY] * (nt + 1) + [pl.BlockSpec(memory_space=pltpu.SMEM), vm],
        out_specs=[ANY] * (nt + 1) + [vm, vm], out_shape=out_shape,
        scratch_shapes=_sem_pair(6 * nt) + _sem_pair(3),
        compiler_params=pltpu.CompilerParams(has_side_effects=True, vmem_limit_bytes=VMEM_LIMIT),
        name="gather_weights",
    )(*shards, conv_w_shard, rel_bias.T, jnp.asarray(buckets_np))
    return list(res[:nt + 1]), res[nt + 1], res[nt + 2]


def _turn(t, u, s, last):
    return jnp.where(t == u, s, jnp.where(t > u, last, 0))


def _add_halves(gs, recvs, c_idx):
    n = len(gs)
    _, rows2, cols = gs[0].shape
    rows = rows2 // 2
    assert all(g.shape == gs[0].shape for g in gs)

    def body(c_ref, *refs):
        t = pl.program_id(0)
        for u in range(n):
            @pl.when(t == u)
            def _(u=u):
                refs[2 * n + u][...] = (refs[u][...] + refs[n + u][...]).astype(BF16)

    def own(u):
        return pl.BlockSpec((None, None, rows, cols), lambda t, s, c: (_turn(t, u, s, N_SHARD - 1), c[0], 0, 0))

    def plain(u):
        return pl.BlockSpec((None, rows, cols), lambda t, s, c: (_turn(t, u, s, N_SHARD - 1), 0, 0))

    return pl.pallas_call(
        body,
        grid_spec=pltpu.PrefetchScalarGridSpec(
            num_scalar_prefetch=1, grid=(n, N_SHARD),
            in_specs=[own(u) for u in range(n)] + [plain(u) for u in range(n)],
            out_specs=[plain(u) for u in range(n)]),
        out_shape=[jax.ShapeDtypeStruct((N_SHARD, rows, cols), BF16)] * n,
        compiler_params=_params(("arbitrary", "arbitrary")), name="rs_add_halves",
    )(c_idx, *[g.reshape(N_SHARD, 2, rows, cols) for g in gs], *recvs)


def _add_chips(parts, recvs, s_idx, c_idx):
    n = len(parts)
    _, rows, cols = parts[0].shape
    assert all(p.shape == parts[0].shape for p in parts)

    def body(idx_ref, *refs):
        t = pl.program_id(0)
        for u in range(n):
            @pl.when(t == u)
            def _(u=u):
                acc = refs[u][...].astype(F32)
                for j in range(3):
                    acc = acc + refs[n + u][j].astype(F32)
                refs[2 * n + u][...] = acc

    res = pl.pallas_call(
        body,
        grid_spec=pltpu.PrefetchScalarGridSpec(
            num_scalar_prefetch=1, grid=(n,),
            in_specs=[pl.BlockSpec((None, rows, cols), lambda t, idx: (idx[0], 0, 0))] * n
            + [pl.BlockSpec((3, rows, cols), lambda t, idx: (0, 0, 0))] * n,
            out_specs=[pl.BlockSpec((None, rows, cols), lambda t, idx: (idx[1], 0, 0))] * n),
        out_shape=[jax.ShapeDtypeStruct((2, rows, cols), F32)] * n,
        compiler_params=_params(("arbitrary",)), name="rs_add_chips",
    )(jnp.concatenate([s_idx, c_idx]), *parts, *recvs)
    return [r.reshape(2 * rows, cols) for r in res]


def _finish_reductions(fulls, arrays):
    nt, n = len(fulls), len(arrays)

    def body(*refs):
        in_refs = refs[nt:nt + n]
        full_refs, out_refs = refs[nt + n:2 * nt + n], refs[2 * nt + n:2 * nt + 2 * n]
        pos = 2 * nt + 2 * n
        share_send, share_recv = refs[pos + 2 * n:pos + 2 * n + 2]
        allreduce = _SmallAllReducePlan(in_refs, out_refs, refs[pos:pos + n], refs[pos + n:pos + 2 * n],
                                        *refs[pos + 2 * n + 2:])
        x, y, c, _ = _mesh_pos()

        def half(t, which):
            rows = fulls[t].shape[0] // 2
            return full_refs[t].at[pl.ds(which * rows, rows), :]

        def share(t, which):
            return pltpu.make_async_remote_copy(
                src_ref=half(t, which), dst_ref=half(t, which), send_sem=share_send.at[t],
                recv_sem=share_recv.at[t], device_id=(x, y, 1 - c), device_id_type=MESH)

        for t in range(nt):
            share(t, c).start()
        allreduce.start_sibling()
        allreduce.sum_sibling_and_start_chips()
        allreduce.finish()
        for t in range(nt):
            share(t, 1 - c).wait_recv()
        for t in range(nt):
            share(t, c).wait_send()

    vm = pl.BlockSpec(memory_space=pltpu.VMEM)
    res = pl.pallas_call(
        body, in_specs=[ANY] * nt + [vm] * n, out_specs=[ANY] * nt + [vm] * n,
        out_shape=[jax.ShapeDtypeStruct(f.shape, f.dtype) for f in fulls]
        + [jax.ShapeDtypeStruct(a.shape, F32) for a in arrays],
        input_output_aliases={t: t for t in range(nt)},
        scratch_shapes=[pltpu.VMEM(a.shape, F32) for a in arrays] + [pltpu.VMEM((3,) + a.shape, F32) for a in arrays]
        + _sem_pair(nt) + _sem_pair(4 * n),
        compiler_params=pltpu.CompilerParams(has_side_effects=True),
        name="finish_reductions",
    )(*fulls, *arrays)
    return list(res[:nt]), list(res[nt:])


def _from_col_shards(g):
    n, rows, cols = g.shape
    return g.transpose(1, 0, 2).reshape(rows, n * cols)


def _train_step(x, tgt, g1, g2, g3, g4, shards, ln_g, ln_b, w_s, b_s, rel_bias, conv_w_shard, conv_b, batch,
                s_idx, c_idx):
    buckets = _bucket_tables()
    bz = jnp.repeat(b_s.T, HEAD_DIM, axis=1)
    w_st = jnp.swapaxes(w_s, 1, 2)

    def with_own(gathered, own):
        return lax.dynamic_update_index_in_dim(gathered, own, s_idx[0], 0)

    def shard_major(g):
        return g.reshape(N_SHARD, g.shape[0] // N_SHARD, D_MODEL)

    (g_in, g_convw), bias, bias_t = _gather_weights([shards["w_in"]], conv_w_shard, rel_bias, buckets)
    w_in_t = with_own(g_in, shards["w_in"]).reshape(IN_COLS, D_MODEL)
    conv_w = _from_col_shards(with_own(g_convw, conv_w_shard))

    h1, uv, qkv, a = _proj_fwd(x, g1, w_in_t, ln_g, ln_b, w_s, bz)
    later = ["w_out", "w_gate", "w_up"]
    o_bf, lse, gathered = _attn_fwd(qkv, bias, batch, [shards[n] for n in later])
    g_out, g_gate, g_up = [with_own(g, shards[n]) for g, n in zip(gathered, later)]
    w_out = g_out.reshape(D_MODEL, D_MODEL)
    w_gate_t = g_gate.reshape(D_FF, D_MODEL)
    w_up_t = g_up.reshape(D_FF, D_MODEL)
    (y1, x1, h2), _ = _fused_rows(
        "out_proj_mid_fwd", 512,
        [(a, w_out, "nn", slice(0, A_WIDTH)), (o_bf, w_out, "nn", slice(A_WIDTH, D_MODEL))],
        [x], [g2, g3], _mid_fwd_rows, [F32, F32, BF16], [])
    gate, up, g_down = _mm_pair_nt(h2, w_gate_t, w_up_t, shards["w_down"], tm=1024, tn=1408, out_dtype=BF16,
                                   name="mm_gate_up")
    w_down = with_own(g_down, shards["w_down"]).reshape(D_FF, D_MODEL)
    act = _convgate_fwd(gate, up, conv_w, conv_b, batch)
    (dx2, dy2, dg4, loss), _ = _fused_rows(
        "down_proj_loss_head", 512, [(act, w_down, "nn", None)], [x1, tgt], [g4], _loss_head_rows,
        [F32, BF16], [(1, D_MODEL), (1, 128)])

    dact = _mm(dy2, w_down, dims="nt", tm=1024, tn=1408, tk=1024, out_dtype=BF16, name="mm_dact")
    dw_down = _mm(act, dy2, dims="tn", tm=1408, tn=1024, tk=1024, out_dtype=F32, name="mm_dw_down")
    dgate, dup, dconv_w, dconv_b = _convgate_bwd(gate, up, dact, conv_w, conv_b, batch)
    (dx1, dy1, dg2, dg3), _ = _fused_rows(
        "dh2_mid_bwd", 256, [(dgate, w_gate_t, "nn", None), (dup, w_up_t, "nn", None)],
        [x1, y1, dx2], [g2, g3], _mid_bwd_rows, [F32, BF16], [(1, D_MODEL), (1, D_MODEL)])
    dw_gate_t, dw_up_t = _mm_pair_tn(dgate, dup, h2, tm=1408, tk=1024, name="mm_dw_gate_up")
    dmix, dw_out = _out_proj_bwd(a, o_bf, dy1, w_out)

    done = [shard_major(g) for g in (dw_down, dw_gate_t, dw_up_t, dw_out)]
    (dproj, dln_g, dln_b, dw_s, dbz), recv_a = _gate_bwd(uv, dmix, ln_g, ln_b, w_s, w_st, bz, done)
    parts = _add_halves(done[:3], recv_a[:3], c_idx) + _add_halves(done[3:], recv_a[3:], c_idx)
    early = dict(loss=loss, norm_mix_post=dg2, norm_ffn_pre=dg3, norm_ffn_post=dg4, ln_v_gain=dln_g,
                 ln_v_bias=dln_b, spatial_w=dw_s, spatial_b=dbz, conv_w=dconv_w, conv_b=dconv_b)
    dproj, ds, recv, early_sums = _attn_bwd(qkv, dmix, o_bf, lse, bias_t, dproj, batch, parts, list(early.values()))
    fulls = _add_chips(parts[:3], recv[:3], s_idx, c_idx) + _add_chips(parts[3:], recv[3:], s_idx, c_idx)
    dw_in_t = _mm(dproj, h1, dims="tn", tm=1408, tn=1024, tk=1024, out_dtype=F32, name="mm_dw_in")
    last = [shard_major(dw_in_t)]
    drel, recv_in_a = _rel_bias_grad(ds, np.ascontiguousarray(np.swapaxes(buckets, 1, 2)), last)
    part_in = _add_halves(last, recv_in_a, c_idx)
    (dx0, dg1), recv_in = _fused_rows(
        "dh1_in_bwd", 512, [(dproj, w_in_t, "nn", None)], [x, dx1], [g1], _in_bwd_rows,
        [F32], [(1, D_MODEL)], exchange=part_in)
    fulls += _add_chips(part_in, recv_in, s_idx, c_idx)
    half_reduced = dict(zip(["w_down", "w_gate", "w_up", "w_out", "w_in"], fulls))

    return dx0, dict(zip(early, early_sums)), dict(norm_mix_pre=dg1, rel_bias=drel), half_reduced


def _adamw_update(w, g, m, v):
    nm = ADAM_B1 * m + (1.0 - ADAM_B1) * g
    nv = ADAM_B2 * v + (1.0 - ADAM_B2) * (g * g)
    m_hat = nm / (1.0 - ADAM_B1 ** ADAM_STEP)
    v_hat = nv / (1.0 - ADAM_B2 ** ADAM_STEP)
    return -ADAM_LR * (m_hat / (jnp.sqrt(v_hat) + ADAM_EPS) + ADAM_WD * w), nm, nv


def _adamw(w, g, m, v, name):
    rows, cols = w.shape
    tr = next(cand for cand in (352, 256, 128) if rows % cand == 0)

    def body(w_ref, g_ref, m_ref, v_ref, go_ref, d_ref, nm_ref, nv_ref):
        gv = g_ref[...]
        go_ref[...] = gv
        d_ref[...], nm_ref[...], nv_ref[...] = _adamw_update(w_ref[...], gv, m_ref[...], v_ref[...])

    spec = pl.BlockSpec((tr, cols), lambda i: (i, 0))
    sds = jax.ShapeDtypeStruct((rows, cols), F32)
    return pl.pallas_call(
        body, grid=(rows // tr,), in_specs=[spec] * 4, out_specs=[spec] * 4, out_shape=[sds] * 4,
        compiler_params=_params(("parallel",)), name=name,
    )(w, g, m, v)


def _adamw_small(ws, gs, ms, vs):
    n = len(ws)

    def body(*refs):
        w_refs, g_refs, m_refs, v_refs = refs[:n], refs[n:2 * n], refs[2 * n:3 * n], refs[3 * n:4 * n]
        d_refs, nm_refs, nv_refs = refs[4 * n:5 * n], refs[5 * n:6 * n], refs[6 * n:7 * n]
        for t in range(n):
            d_refs[t][...], nm_refs[t][...], nv_refs[t][...] = _adamw_update(
                w_refs[t][...], g_refs[t][...], m_refs[t][...], v_refs[t][...])

    vm = pl.BlockSpec(memory_space=pltpu.VMEM)
    sds = [jax.ShapeDtypeStruct(w.shape, F32) for w in ws]
    res = pl.pallas_call(
        body, in_specs=[vm] * (4 * n), out_specs=[vm] * (3 * n), out_shape=sds * 3,
        compiler_params=_params(), name="adamw_small",
    )(*ws, *gs, *ms, *vs)
    return res[:n], res[n:2 * n], res[2 * n:]


SMALL = ["norm_mix_pre", "norm_mix_post", "norm_ffn_pre", "norm_ffn_post", "ln_v_gain", "ln_v_bias",
         "spatial_w", "spatial_b", "rel_bias", "conv_b"]
LARGE = ["w_in", "w_gate", "w_up", "w_down", "w_out"]
TRANSPOSED = ("w_in", "w_gate", "w_up")
ORDER = ["norm_mix_pre", "norm_mix_post", "norm_ffn_pre", "norm_ffn_post", "w_in", "ln_v_gain", "ln_v_bias",
         "spatial_w", "spatial_b", "rel_bias", "w_out", "w_gate", "w_up", "conv_w", "conv_b", "w_down"]


def kernel(x, norm_mix_pre, norm_mix_post, norm_ffn_pre, norm_ffn_post, w_in, ln_v_gain, ln_v_bias, spatial_w, spatial_b, rel_bias, w_out, w_gate, w_up, conv_w, conv_b, w_down, loss_target, m_norm_mix_pre, m_norm_mix_post, m_norm_ffn_pre, m_norm_ffn_post, m_w_in, m_ln_v_gain, m_ln_v_bias, m_spatial_w, m_spatial_b, m_rel_bias, m_w_out, m_w_gate, m_w_up, m_conv_w, m_conv_b, m_w_down, v_norm_mix_pre, v_norm_mix_post, v_norm_ffn_pre, v_norm_ffn_post, v_w_in, v_ln_v_gain, v_ln_v_bias, v_spatial_w, v_spatial_b, v_rel_bias, v_w_out, v_w_gate, v_w_up, v_conv_w, v_conv_b, v_w_down):
    params = dict(norm_mix_pre=norm_mix_pre, norm_mix_post=norm_mix_post, norm_ffn_pre=norm_ffn_pre,
                  norm_ffn_post=norm_ffn_post, w_in=w_in, ln_v_gain=ln_v_gain, ln_v_bias=ln_v_bias,
                  spatial_w=spatial_w, spatial_b=spatial_b, rel_bias=rel_bias, w_out=w_out, w_gate=w_gate,
                  w_up=w_up, conv_w=conv_w, conv_b=conv_b, w_down=w_down)
    mom = dict(norm_mix_pre=m_norm_mix_pre, norm_mix_post=m_norm_mix_post, norm_ffn_pre=m_norm_ffn_pre,
               norm_ffn_post=m_norm_ffn_post, w_in=m_w_in, ln_v_gain=m_ln_v_gain, ln_v_bias=m_ln_v_bias,
               spatial_w=m_spatial_w, spatial_b=m_spatial_b, rel_bias=m_rel_bias, w_out=m_w_out, w_gate=m_w_gate,
               w_up=m_w_up, conv_w=m_conv_w, conv_b=m_conv_b, w_down=m_w_down)
    var = dict(norm_mix_pre=v_norm_mix_pre, norm_mix_post=v_norm_mix_post, norm_ffn_pre=v_norm_ffn_pre,
               norm_ffn_post=v_norm_ffn_post, w_in=v_w_in, ln_v_gain=v_ln_v_gain, ln_v_bias=v_ln_v_bias,
               spatial_w=v_spatial_w, spatial_b=v_spatial_b, rel_bias=v_rel_bias, w_out=v_w_out, w_gate=v_w_gate,
               w_up=v_w_up, conv_w=v_conv_w, conv_b=v_conv_b, w_down=v_w_down)

    batch = x.shape[0]
    xi, yi, ci = lax.axis_index("x"), lax.axis_index("y"), lax.axis_index("c")
    s_idx = (2 * xi + yi).astype(jnp.int32).reshape(1)
    c_idx = ci.astype(jnp.int32).reshape(1)

    def local(a, n):
        return jnp.swapaxes(a[0], 0, 1) if n in TRANSPOSED else a[0]

    shards = {n: local(params[n], n).astype(BF16) for n in LARGE}
    dx0, total, partial, half_reduced = _train_step(
        x.reshape(batch * SEQ, D_MODEL), loss_target.reshape(batch * SEQ, D_MODEL),
        norm_mix_pre, norm_mix_post, norm_ffn_pre, norm_ffn_post, shards,
        ln_v_gain.reshape(1, A_WIDTH), ln_v_bias.reshape(1, A_WIDTH), spatial_w[0], spatial_b[0], rel_bias,
        conv_w[0], conv_b, batch, s_idx, c_idx)
    grad_x = dx0.reshape(batch, SEQ, D_MODEL)

    names = list(partial)
    fulls, sums = _finish_reductions([half_reduced[n] for n in LARGE], [partial[n] for n in names])
    reduced = dict(zip(LARGE, fulls))
    total.update(zip(names, sums))
    loss = total["loss"][0, 0]
    total["spatial_b"] = total["spatial_b"][:, ::HEAD_DIM].T
    total["rel_bias"] = total["rel_bias"].reshape(B_HEADS, NUM_BUCKETS)
    total["conv_w"] = lax.dynamic_slice_in_dim(total["conv_w"], s_idx[0] * SHARD_FF, SHARD_FF, axis=1)
    small_names = SMALL + ["conv_w"]

    def small(a, n):
        return a.T if n == "rel_bias" else a

    for n in small_names:
        reduced[n] = total[n].reshape(small(params[n], n).shape)

    out_g, out_d, out_m, out_v = {}, {}, {}, {}
    for n in LARGE:
        res = _adamw(local(params[n], n), reduced[n], local(mom[n], n), local(var[n], n), name=f"adamw_{n}")
        if n in TRANSPOSED:
            res = [jnp.swapaxes(r, 0, 1) for r in res]
        out_g[n], out_d[n], out_m[n], out_v[n] = [r[None] for r in res]
    d, nm, nv = _adamw_small([small(params[n], n) for n in small_names], [reduced[n] for n in small_names],
                             [small(mom[n], n) for n in small_names], [small(var[n], n) for n in small_names])
    for n, dd, mm, vv in zip(small_names, d, nm, nv):
        out_g[n], out_d[n], out_m[n], out_v[n] = [small(r, n) for r in (reduced[n], dd, mm, vv)]

    return (loss, grad_x, *[out_g[n] for n in ORDER], *[out_d[n] for n in ORDER],
            *[out_m[n] for n in ORDER], *[out_v[n] for n in ORDER])
```

```python
import functools
import math

import numpy as np
import jax
import jax.numpy as jnp
from jax import lax
from jax.experimental import pallas as pl
from jax.experimental.pallas import tpu as pltpu

F32 = jnp.float32
BF16 = jnp.bfloat16
MESH = pl.DeviceIdType.MESH

D_MODEL = 1024
SEQ = 2048
HEAD_DIM = 64
A_GROUPS = 4
A_WIDTH = 256
B_HEADS = 12
B_WIDTH = 768
CHUNK = 128
DILATED = ((128, 1), (512, 4), (2048, 16))
NUM_BUCKETS = 32
MAX_DISTANCE = 2048
D_FF = 2816
IN_COLS = 2816
NORM_EPS = 1e-6
NEG_INF = -1e30
N_SHARD = 4
SHARD_FF = D_FF // N_SHARD
LANE_BLOCK = 256
VMEM_LIMIT = 56 * 1024 * 1024

ADAM_LR = 0.001
ADAM_B1 = 0.9
ADAM_B2 = 0.999
ADAM_EPS = 1e-08
ADAM_WD = 0.01
ADAM_STEP = 10

GELU_C = math.sqrt(2.0 / math.pi)
GELU_A = 0.044715

ANY = pl.BlockSpec(memory_space=pl.ANY)


def _params(sem=None):
    return pltpu.CompilerParams(dimension_semantics=sem, vmem_limit_bytes=VMEM_LIMIT)


def _dot(a, b, precision=None):
    return jnp.dot(a, b, preferred_element_type=F32, precision=precision)


def _dot_nt(a, b, precision=None):
    return lax.dot_general(a, b, (((1,), (1,)), ((), ())), preferred_element_type=F32, precision=precision)


def _dot_tn(a, b):
    return lax.dot_general(a, b, (((0,), (0,)), ((), ())), preferred_element_type=F32)


def _gelu(x):
    t = jnp.tanh(x * (GELU_C + (GELU_C * GELU_A) * (x * x)))
    return (0.5 * x) * (1.0 + t)


def _gelu_and_grad(x):
    x2 = x * x
    u = 1.0 + jnp.tanh(x * (GELU_C + (GELU_C * GELU_A) * x2))
    hx = 0.5 * x
    dg = u * (0.5 + hx * (2.0 - u) * (GELU_C + (3.0 * GELU_C * GELU_A) * x2))
    return hx * u, dg


def _mesh_pos():
    x, y, c = lax.axis_index("x"), lax.axis_index("y"), lax.axis_index("c")
    chips = [(1 - x, y), (x, 1 - y), (1 - x, 1 - y)]
    return x, y, c, chips


class _GatherPlan:
    def __init__(self, shapes, out_refs, send_sems, recv_sems):
        self.shapes, self.out_refs = shapes, out_refs
        self.send_sems, self.recv_sems = send_sems, recv_sems
        self.x, self.y, self.c, self.chips = _mesh_pos()
        self.sib = (self.x, self.y, 1 - self.c)

    def _half(self, t, chip, which):
        rows = self.shapes[t][0] // 2
        return self.out_refs[t].at[2 * chip[0] + chip[1], pl.ds(which * rows, rows), :]

    def _copy(self, k, src, dst, to):
        return pltpu.make_async_remote_copy(src_ref=src, dst_ref=dst, send_sem=self.send_sems.at[k],
                                            recv_sem=self.recv_sems.at[k], device_id=to, device_id_type=MESH)

    def _sends(self, t):
        own = self._half(t, (self.x, self.y), self.c)
        return [self._copy(6 * t + j, own, own, (*chip, self.c)) for j, chip in enumerate(self.chips)]

    def _forwards(self, t):
        return [self._copy(6 * t + 3 + j, self._half(t, chip, self.c), self._half(t, chip, self.c), self.sib)
                for j, chip in enumerate(self.chips)]

    def start(self, ts):
        for t in ts:
            for cp in self._sends(t):
                cp.start()

    def forward(self, ts):
        for t in ts:
            for j, chip in enumerate(self.chips):
                landed = self._half(t, chip, self.c)
                self._copy(6 * t + j, landed, landed, (*chip, self.c)).wait_recv()
            for cp in self._forwards(t):
                cp.start()

    def finish(self, ts):
        for t in ts:
            for j, chip in enumerate(self.chips):
                other = self._half(t, chip, 1 - self.c)
                self._copy(6 * t + 3 + j, other, other, self.sib).wait_recv()
        for t in ts:
            for cp in self._sends(t) + self._forwards(t):
                cp.wait_send()


class _RelayGatherPlan:
    def __init__(self, shapes, shard_refs, out_refs, send_sems, recv_sems):
        self.shapes, self.shard_refs, self.out_refs = shapes, shard_refs, out_refs
        self.send_sems, self.recv_sems = send_sems, recv_sems
        x, y, c, self.chips = _mesh_pos()
        self.me, self.c, self.sib = (x, y), c, (x, y, 1 - c)
        self.first = (x + c - 2 * x * c, y + (1 - c) - 2 * y * (1 - c))
        self.second = (x + (1 - c) - 2 * x * (1 - c), y + c - 2 * y * c)
        self.diag = (1 - x, 1 - y)

    def _half(self, t, chip, which):
        rows = self.shapes[t][0] // 2
        return self.out_refs[t].at[2 * chip[0] + chip[1], pl.ds(which * rows, rows), :]

    def _copy(self, k, src, dst, to):
        return pltpu.make_async_remote_copy(src_ref=src, dst_ref=dst, send_sem=self.send_sems.at[k],
                                            recv_sem=self.recv_sems.at[k], device_id=to, device_id_type=MESH)

    def _own(self, t):
        rows = self.shapes[t][0] // 2
        return self.shard_refs[t].at[pl.ds(self.c * rows, rows), :]

    def _step1(self, t):
        return self._copy(6 * t, self._own(t), self._half(t, self.me, self.c), (*self.first, self.c))

    def _step2(self, t):
        landed = self._half(t, self.first, self.c)
        return [self._copy(6 * t + 1, self._own(t), self._half(t, self.me, self.c), (*self.second, self.c)),
                self._copy(6 * t + 2, landed, landed, (*self.second, self.c))]

    def _forwards(self, t):
        return [self._copy(6 * t + 3 + j, self._half(t, chip, self.c), self._half(t, chip, self.c), self.sib)
                for j, chip in enumerate(self.chips)]

    def start(self, ts):
        for t in ts:
            self._step1(t).start()
            self._step2(t)[0].start()

    def relay(self, ts):
        for t in ts:
            landed = self._half(t, self.first, self.c)
            self._copy(6 * t, landed, landed, self.sib).wait_recv()
            self._step2(t)[1].start()

    def forward(self, ts):
        for t in ts:
            for k, chip in ((1, self.second), (2, self.diag)):
                landed = self._half(t, chip, self.c)
                self._copy(6 * t + k, landed, landed, self.sib).wait_recv()
            for cp in self._forwards(t):
                cp.start()

    def finish(self, ts):
        for t in ts:
            for j, chip in enumerate(self.chips):
                other = self._half(t, chip, 1 - self.c)
                self._copy(6 * t + 3 + j, other, other, self.sib).wait_recv()
        for t in ts:
            for cp in [self._step1(t)] + self._step2(t) + self._forwards(t):
                cp.wait_send()


class _SiblingExchangePlan:
    def __init__(self, shapes, grad_refs, out_refs, send_sems, recv_sems):
        self.shapes, self.grad_refs, self.out_refs = shapes, grad_refs, out_refs
        self.send_sems, self.recv_sems = send_sems, recv_sems
        self.x, self.y, self.c, _ = _mesh_pos()

    def _copies(self):
        out = []
        for t, (g, o) in enumerate(zip(self.grad_refs, self.out_refs)):
            rows = self.shapes[t][1] // 2
            out.append(pltpu.make_async_remote_copy(
                src_ref=g.at[:, pl.ds((1 - self.c) * rows, rows), :], dst_ref=o, send_sem=self.send_sems.at[t],
                recv_sem=self.recv_sems.at[t], device_id=(self.x, self.y, 1 - self.c), device_id_type=MESH))
        return out

    def start(self):
        for cp in self._copies():
            cp.start()

    def finish(self):
        for cp in self._copies():
            cp.wait()


class _ChipExchangePlan:
    def __init__(self, part_refs, out_refs, send_sems, recv_sems):
        self.part_refs, self.out_refs, self.send_sems, self.recv_sems = part_refs, out_refs, send_sems, recv_sems
        _, _, self.c, self.chips = _mesh_pos()

    def _copies(self):
        return [pltpu.make_async_remote_copy(
            src_ref=p.at[2 * chip[0] + chip[1]], dst_ref=o.at[j], send_sem=self.send_sems.at[3 * t + j],
            recv_sem=self.recv_sems.at[3 * t + j], device_id=(*chip, self.c), device_id_type=MESH)
            for t, (p, o) in enumerate(zip(self.part_refs, self.out_refs)) for j, chip in enumerate(self.chips)]

    def start(self):
        for cp in self._copies():
            cp.start()

    def finish(self):
        for cp in self._copies():
            cp.wait()


class _SmallAllReducePlan:
    def __init__(self, in_refs, out_refs, sib_refs, chip_refs, send_sems, recv_sems):
        self.in_refs, self.out_refs, self.sib_refs, self.chip_refs = in_refs, out_refs, sib_refs, chip_refs
        self.send_sems, self.recv_sems = send_sems, recv_sems
        self.n = len(in_refs)
        self.x, self.y, self.c, self.chips = _mesh_pos()

    def _copy(self, k, src, dst, to):
        return pltpu.make_async_remote_copy(src_ref=src, dst_ref=dst, send_sem=self.send_sems.at[k],
                                            recv_sem=self.recv_sems.at[k], device_id=to, device_id_type=MESH)

    def _first(self):
        return [self._copy(t, self.in_refs[t], self.sib_refs[t], (self.x, self.y, 1 - self.c)) for t in range(self.n)]

    def _second(self):
        return [self._copy(self.n + 3 * t + j, self.out_refs[t], self.chip_refs[t].at[j], (*chip, self.c))
                for t in range(self.n) for j, chip in enumerate(self.chips)]

    def start_sibling(self):
        for cp in self._first():
            cp.start()

    def sum_sibling_and_start_chips(self):
        for cp in self._first():
            cp.wait()
        for t in range(self.n):
            self.out_refs[t][...] = self.in_refs[t][...] + self.sib_refs[t][...]
        for cp in self._second():
            cp.start()

    def finish(self):
        for cp in self._second():
            cp.wait()
        for t in range(self.n):
            self.out_refs[t][...] = ((self.out_refs[t][...] + self.chip_refs[t][0])
                                     + (self.chip_refs[t][1] + self.chip_refs[t][2]))

    @staticmethod
    def scratch(arrays):
        return ([pltpu.VMEM(a.shape, F32) for a in arrays] + [pltpu.VMEM((3,) + a.shape, F32) for a in arrays]
                + _sem_pair(4 * len(arrays)))


def _sem_pair(n):
    return [pltpu.SemaphoreType.DMA((n,)), pltpu.SemaphoreType.DMA((n,))]


def _mm(a, b, *, dims, tm, tn, tk, out_dtype, name):
    if dims == "nn":
        m, k = a.shape
        n = b.shape[1]
        a_spec = pl.BlockSpec((tm, tk), lambda i, j, kk: (i, kk))
        b_spec = pl.BlockSpec((tk, tn), lambda i, j, kk: (kk, j))
        dot = _dot
    elif dims == "nt":
        m, k = a.shape
        n = b.shape[0]
        a_spec = pl.BlockSpec((tm, tk), lambda i, j, kk: (i, kk))
        b_spec = pl.BlockSpec((tn, tk), lambda i, j, kk: (j, kk))
        dot = _dot_nt
    else:
        k, m = a.shape
        n = b.shape[1]
        a_spec = pl.BlockSpec((tk, tm), lambda i, j, kk: (kk, i))
        b_spec = pl.BlockSpec((tk, tn), lambda i, j, kk: (kk, j))
        dot = _dot_tn
    assert m % tm == 0 and n % tn == 0 and k % tk == 0, (name, m, n, k)
    grid = (m // tm, n // tn, k // tk)
    nk = grid[2]
    assert nk == 1 or out_dtype == F32, name

    def body(a_ref, b_ref, o_ref):
        prod = dot(a_ref[...].astype(BF16), b_ref[...].astype(BF16))
        if nk == 1:
            o_ref[...] = prod.astype(out_dtype)
        else:
            kk = pl.program_id(2)

            @pl.when(kk == 0)
            def _():
                o_ref[...] = prod

            @pl.when(kk > 0)
            def _():
                o_ref[...] += prod

    return pl.pallas_call(
        body, grid=grid, in_specs=[a_spec, b_spec],
        out_specs=pl.BlockSpec((tm, tn), lambda i, j, kk: (i, j)),
        out_shape=jax.ShapeDtypeStruct((m, n), out_dtype),
        compiler_params=_params(("parallel", "parallel", "arbitrary")), name=name,
    )(a, b)


def _mm_pair_tn(a1, a2, b, *, tm, tk, name):
    k, m = a1.shape
    n = b.shape[1]
    assert m % tm == 0 and k % tk == 0 and a2.shape == a1.shape, name

    def body(a1_ref, a2_ref, b_ref, o1_ref, o2_ref):
        bv = b_ref[...]
        p1 = _dot_tn(a1_ref[...], bv)
        p2 = _dot_tn(a2_ref[...], bv)
        kk = pl.program_id(1)

        @pl.when(kk == 0)
        def _():
            o1_ref[...] = p1
            o2_ref[...] = p2

        @pl.when(kk > 0)
        def _():
            o1_ref[...] += p1
            o2_ref[...] += p2

    a_spec = pl.BlockSpec((tk, tm), lambda i, kk: (kk, i))
    o_spec = pl.BlockSpec((tm, n), lambda i, kk: (i, 0))
    return pl.pallas_call(
        body, grid=(m // tm, k // tk),
        in_specs=[a_spec, a_spec, pl.BlockSpec((tk, n), lambda i, kk: (kk, 0))],
        out_specs=[o_spec, o_spec],
        out_shape=[jax.ShapeDtypeStruct((m, n), F32)] * 2,
        compiler_params=_params(("parallel", "arbitrary")), name=name,
    )(a1, a2, b)


def _mm_pair_nt(a, w1_t, w2_t, own, *, tm, tn, out_dtype, name):
    m, k = a.shape
    n = w1_t.shape[0]
    assert m % tm == 0 and n % tn == 0 and w2_t.shape == w1_t.shape, name
    grid = (m // tm, n // tn)
    n_steps = grid[0] * grid[1]

    def body(a_ref, w1_ref, w2_ref, own_ref, o1_ref, o2_ref, gat_ref, send_sems, recv_sems):
        del own_ref
        step = pl.program_id(0) * grid[1] + pl.program_id(1)
        gather = _GatherPlan([own.shape[1:]], [gat_ref], send_sems, recv_sems)

        @pl.when(step == 0)
        def _():
            gather.start([0])

        @pl.when(step == (2 * n_steps) // 3)
        def _():
            gather.forward([0])

        av = a_ref[...]
        o1_ref[...] = _dot_nt(av, w1_ref[...]).astype(out_dtype)
        o2_ref[...] = _dot_nt(av, w2_ref[...]).astype(out_dtype)

        @pl.when(step == n_steps - 1)
        def _():
            gather.finish([0])

    w_spec = pl.BlockSpec((tn, k), lambda i, j: (j, 0))
    o_spec = pl.BlockSpec((tm, tn), lambda i, j: (i, j))
    return pl.pallas_call(
        body, grid=grid,
        in_specs=[pl.BlockSpec((tm, k), lambda i, j: (i, 0)), w_spec, w_spec, ANY],
        out_specs=[o_spec, o_spec, ANY],
        out_shape=[jax.ShapeDtypeStruct((m, n), out_dtype)] * 2 + [jax.ShapeDtypeStruct(own.shape, own.dtype)],
        scratch_shapes=_sem_pair(6), input_output_aliases={3: 2},
        compiler_params=_params(("arbitrary", "arbitrary")), name=name,
    )(a, w1_t, w2_t, own)


def _out_proj_bwd(a, o, dy1, w_out):
    m = dy1.shape[0]
    tm = 1024

    def body(a_ref, o_ref, dy_ref, w_ref, dmix_ref, dw_ref):
        dy = dy_ref[...]
        dmix_ref[...] = _dot_nt(dy, w_ref[...])
        top = _dot_tn(a_ref[...], dy)
        bottom = _dot_tn(o_ref[...], dy)

        @pl.when(pl.program_id(0) == 0)
        def _():
            dw_ref[:A_WIDTH, :] = top
            dw_ref[A_WIDTH:, :] = bottom

        @pl.when(pl.program_id(0) > 0)
        def _():
            dw_ref[:A_WIDTH, :] += top
            dw_ref[A_WIDTH:, :] += bottom

    tile = lambda width: pl.BlockSpec((tm, width), lambda i: (i, 0))
    return pl.pallas_call(
        body, grid=(m // tm,),
        in_specs=[tile(A_WIDTH), tile(B_WIDTH), tile(D_MODEL), _full_spec((D_MODEL, D_MODEL))],
        out_specs=[tile(D_MODEL), _full_spec((D_MODEL, D_MODEL))],
        out_shape=[jax.ShapeDtypeStruct((m, D_MODEL), F32), jax.ShapeDtypeStruct((D_MODEL, D_MODEL), F32)],
        compiler_params=_params(("arbitrary",)), name="out_proj_bwd",
    )(a, o, dy1, w_out)


def _fused_rows(name, tm, mats, rows, vecs, fn, row_outs, acc_outs, exchange=()):
    m = mats[0][0].shape[0]
    nm, nr, nv, nro, nao, nx = len(mats), len(rows), len(vecs), len(row_outs), len(acc_outs), len(exchange)
    n_steps = m // tm

    def body(*refs):
        a_refs, w_refs = refs[:nm], refs[nm:2 * nm]
        pos = 2 * nm
        row_refs, vec_refs, part_refs = refs[pos:pos + nr], refs[pos + nr:pos + nr + nv], refs[pos + nr + nv:pos + nr + nv + nx]
        pos += nr + nv + nx
        out_refs, acc_refs, recv_refs = refs[pos:pos + nro], refs[pos + nro:pos + nro + nao], refs[pos + nro + nao:pos + nro + nao + nx]
        sems = refs[pos + nro + nao + nx:]
        i = pl.program_id(0)
        if nx:
            plan = _ChipExchangePlan(part_refs, recv_refs, *sems)

            @pl.when(i == 0)
            def _():
                plan.start()

        @pl.when(i == 0)
        def _():
            for r in acc_refs:
                r[...] = jnp.zeros_like(r)

        y = None
        for a_ref, w_ref, (_, _, dims, sl) in zip(a_refs, w_refs, mats):
            w = w_ref[...] if sl is None else w_ref[sl, :]
            part = (_dot if dims == "nn" else _dot_nt)(a_ref[...], w)
            y = part if y is None else y + part
        res = fn(y, *[r[...] for r in row_refs], *[v[...] for v in vec_refs])
        for r, val in zip(out_refs, res[:nro]):
            r[...] = val.astype(r.dtype)
        for r, val in zip(acc_refs, res[nro:]):
            r[...] += val

        if nx:
            @pl.when(i == n_steps - 1)
            def _():
                plan.finish()

    tile = lambda width: pl.BlockSpec((tm, width), lambda i: (i, 0))
    res = pl.pallas_call(
        body, grid=(n_steps,),
        in_specs=[tile(a.shape[1]) for a, _, _, _ in mats] + [_full_spec(w.shape) for _, w, _, _ in mats]
        + [tile(D_MODEL)] * nr + [_full_spec((1, D_MODEL))] * nv + [ANY] * nx,
        out_specs=[tile(D_MODEL)] * nro + [_full_spec(s) for s in acc_outs] + [ANY] * nx,
        out_shape=[jax.ShapeDtypeStruct((m, D_MODEL), dt) for dt in row_outs]
        + [jax.ShapeDtypeStruct(s, F32) for s in acc_outs]
        + [jax.ShapeDtypeStruct((3,) + p.shape[1:], p.dtype) for p in exchange],
        scratch_shapes=_sem_pair(3 * nx) if nx else [],
        compiler_params=_params(("arbitrary",)), name=name,
    )(*[a for a, _, _, _ in mats], *[w for _, w, _, _ in mats], *rows, *vecs, *exchange)
    return list(res[:nro + nao]), list(res[nro + nao:])


def _vec_spec(width=D_MODEL):
    return pl.BlockSpec((1, width), lambda i: (0, 0))


def _rstd(v):
    return lax.rsqrt(jnp.mean(v * v, axis=-1, keepdims=True) + NORM_EPS)


def _mid_fwd_rows(y1, x0, g2, g3):
    x1 = x0 + y1 * _rstd(y1) * g2
    return y1, x1, x1 * _rstd(x1) * g3


def _rms_bwd_rows(dout, v, g):
    r = _rstd(v)
    n = v * r
    dn = dout * g
    dv = r * (dn - n * jnp.mean(dn * n, axis=-1, keepdims=True))
    dg = jnp.sum(dout * n, axis=0, keepdims=True)
    return dv, dg


def _loss_head_rows(y2, x1, tgt, g4):
    x2 = x1 + y2 * _rstd(y2) * g4
    err = x2 - tgt
    loss = 0.5 * jnp.sum(jnp.mean(err * err, axis=-1, keepdims=True), axis=0, keepdims=True)
    dx2 = err * (1.0 / D_MODEL)
    dy2, dg4 = _rms_bwd_rows(dx2, y2, g4)
    return dx2, dy2, dg4, loss


def _mid_bwd_rows(dh2, x1, y1, dx2, g2, g3):
    d3, dg3 = _rms_bwd_rows(dh2, x1, g3)
    dx1 = dx2 + d3
    dy1, dg2 = _rms_bwd_rows(dx1, y1, g2)
    return dx1, dy1, dg2, dg3


def _in_bwd_rows(dh1, x0, dx1, g1):
    d1, dg1 = _rms_bwd_rows(dh1, x0, g1)
    return dx1 + d1, dg1


GATE_ROWS = 512


def _group_mean_matrix():
    p = np.zeros((A_WIDTH, A_WIDTH), np.float32)
    for g in range(A_GROUPS):
        p[g * HEAD_DIM:(g + 1) * HEAD_DIM, g * HEAD_DIM:(g + 1) * HEAD_DIM] = 1.0 / HEAD_DIM
    return jnp.asarray(p)


def _group_masks(width=A_WIDTH):
    lane = lax.broadcasted_iota(jnp.int32, (1, width), 1)
    return [(lane >= g * HEAD_DIM) & (lane < (g + 1) * HEAD_DIM) for g in range(width // HEAD_DIM)]


GROUP_SUM_PRECISION = lax.Precision.HIGH


def _layernorm_groups(vg, pavg):
    hi = GROUP_SUM_PRECISION
    mu = _dot(vg, pavg, hi)
    xc = vg - mu
    var = _dot(xc * xc, pavg, hi)
    rstd = lax.rsqrt(var + NORM_EPS)
    return xc * rstd, rstd


def _spatial_mix(w_bf, vn_chunk_bf, masks, bz):
    z = bz
    for g in range(A_GROUPS):
        z = z + jnp.where(masks[g], _dot(w_bf[g], vn_chunk_bf), 0.0)
    return z


def _full_spec(shape):
    return pl.BlockSpec(shape, lambda i: tuple(0 for _ in shape))


def _gate_fwd_rows(u, v, lg, lb, w_ref, bz, pavg, a_ref):
    masks = _group_masks()
    row = lax.broadcasted_iota(jnp.int32, (CHUNK, CHUNK), 0)
    col = lax.broadcasted_iota(jnp.int32, (CHUNK, CHUNK), 1)
    w_bf = [jnp.where(row >= col, w_ref[g], 0.0).astype(BF16) for g in range(A_GROUPS)]
    ug = _gelu(u)
    vhat, _ = _layernorm_groups(_gelu(v), pavg)
    vn = vhat * lg + lb
    for c in range(GATE_ROWS // CHUNK):
        sl = slice(c * CHUNK, (c + 1) * CHUNK)
        z = _spatial_mix(w_bf, vn[sl].astype(BF16), masks, bz)
        a_ref[sl, :] = (ug[sl] * z).astype(BF16)


def _gate_bwd(uv, dmix, ln_g, ln_b, w_s, w_st, bz, grads):
    m = uv.shape[0]
    pavg = _group_mean_matrix()
    nsteps = m // GATE_ROWS
    nx = len(grads)
    shapes = [g.shape for g in grads]

    def body(u_ref, v_ref, da_ref, lg_ref, lb_ref, w_ref, wt_ref, bz_ref, p_ref, *rest):
        grad_refs = rest[:nx]
        duv_ref, dlg_ref, dlb_ref, dw_ref, dbz_ref = rest[nx:nx + 5]
        recv_refs = rest[nx + 5:2 * nx + 5]
        exchange = _SiblingExchangePlan(shapes, grad_refs, recv_refs, *rest[2 * nx + 5:])
        i = pl.program_id(0)

        @pl.when(i == 0)
        def _():
            exchange.start()
            dlg_ref[...] = jnp.zeros_like(dlg_ref)
            dlb_ref[...] = jnp.zeros_like(dlb_ref)
            dw_ref[...] = jnp.zeros_like(dw_ref)
            dbz_ref[...] = jnp.zeros_like(dbz_ref)

        hi = GROUP_SUM_PRECISION
        masks = _group_masks()
        row = lax.broadcasted_iota(jnp.int32, (CHUNK, CHUNK), 0)
        col = lax.broadcasted_iota(jnp.int32, (CHUNK, CHUNK), 1)
        tril = row >= col
        w_bf = [jnp.where(tril, w_ref[g], 0.0).astype(BF16) for g in range(A_GROUPS)]
        wt_bf = [jnp.where(col >= row, wt_ref[g], 0.0).astype(BF16) for g in range(A_GROUPS)]
        pavg_v = p_ref[...]
        lg = lg_ref[...]
        ug, dug = _gelu_and_grad(u_ref[...])
        vg, dvg_dx = _gelu_and_grad(v_ref[...])
        vhat, rstd = _layernorm_groups(vg, pavg_v)
        vn = vhat * lg + lb_ref[...]
        da = da_ref[...]
        bz = bz_ref[...]
        for c in range(GATE_ROWS // CHUNK):
            sl = slice(c * CHUNK, (c + 1) * CHUNK)
            vn_bf = vn[sl].astype(BF16)
            z = _spatial_mix(w_bf, vn_bf, masks, bz)
            dz = da[sl] * ug[sl]
            duv_ref[sl, 0:A_WIDTH] = (da[sl] * z * dug[sl]).astype(BF16)
            dbz_ref[...] += dz
            dz_bf = dz.astype(BF16)
            dvn = jnp.zeros((CHUNK, A_WIDTH), F32)
            for g in range(A_GROUPS):
                dz_g = jnp.where(masks[g], dz, 0.0).astype(BF16)
                dw_ref[g] += jnp.where(tril, _dot_nt(dz_g, vn_bf), 0.0)
                dvn = dvn + jnp.where(masks[g], _dot(wt_bf[g], dz_bf), 0.0)
            vh = vhat[sl]
            dlb_ref[...] += jnp.sum(dvn, axis=0, keepdims=True)
            dlg_ref[...] += jnp.sum(dvn * vh, axis=0, keepdims=True)
            dvh = dvn * lg
            m1 = _dot(dvh, pavg_v, hi)
            m2 = _dot(dvh * vh, pavg_v, hi)
            duv_ref[sl, A_WIDTH:2 * A_WIDTH] = (rstd[sl] * (dvh - m1 - vh * m2) * dvg_dx[sl]).astype(BF16)

        @pl.when(i == nsteps - 1)
        def _():
            dbz_ref[...] = _dot(dbz_ref[...], pavg_v * float(HEAD_DIM), hi)
            exchange.finish()

    res = pl.pallas_call(
        body, grid=(nsteps,),
        in_specs=[pl.BlockSpec((GATE_ROWS, A_WIDTH), lambda i: (i, 0)),
                  pl.BlockSpec((GATE_ROWS, A_WIDTH), lambda i: (i, 1)),
                  pl.BlockSpec((GATE_ROWS, A_WIDTH), lambda i: (i, 0)),
                  _full_spec((1, A_WIDTH)), _full_spec((1, A_WIDTH)), _full_spec((A_GROUPS, CHUNK, CHUNK)),
                  _full_spec((A_GROUPS, CHUNK, CHUNK)), _full_spec((CHUNK, A_WIDTH)),
                  _full_spec((A_WIDTH, A_WIDTH))] + [ANY] * nx,
        out_specs=[pl.BlockSpec((GATE_ROWS, 2 * A_WIDTH), lambda i: (i, 0)),
                   _full_spec((1, A_WIDTH)), _full_spec((1, A_WIDTH)), _full_spec((A_GROUPS, CHUNK, CHUNK)),
                   _full_spec((CHUNK, A_WIDTH))] + [ANY] * nx,
        out_shape=[jax.ShapeDtypeStruct((m, IN_COLS), BF16),
                   jax.ShapeDtypeStruct((1, A_WIDTH), F32), jax.ShapeDtypeStruct((1, A_WIDTH), F32),
                   jax.ShapeDtypeStruct((A_GROUPS, CHUNK, CHUNK), F32),
                   jax.ShapeDtypeStruct((CHUNK, A_WIDTH), F32)]
        + [jax.ShapeDtypeStruct((N_SHARD, s[1] // 2, s[2]), F32) for s in shapes],
        scratch_shapes=_sem_pair(nx),
        compiler_params=_params(("arbitrary",)), name="gate_bwd",
    )(uv, uv, dmix, ln_g, ln_b, w_s, w_st, bz, pavg, *grads)
    return res[:5], list(res[5:])


Q_BLOCK = 128
PAIR = 2 * HEAD_DIM
N_PAIR = B_HEADS // 2
N_CFG = len(DILATED)
BLOCKS_PER_CFG = SEQ // Q_BLOCK
QKV_SLABS = 3 * N_PAIR
FWD_BLOCKS_PER_TRIP = 8
BWD_BLOCKS_PER_TRIP = 4


def _t5_bucket_np(dist, dtype):
    max_exact = NUM_BUCKETS // 2
    d = np.maximum(dist, 1).astype(dtype)
    large = max_exact + (np.log(d / dtype(max_exact)) / dtype(math.log(MAX_DISTANCE / max_exact))
                         * dtype(NUM_BUCKETS - max_exact))
    large = np.minimum(large.astype(np.int32), NUM_BUCKETS - 1)
    return np.where(dist < max_exact, dist, large)


def _bucket_tables():
    i = np.arange(Q_BLOCK)[:, None]
    j = np.arange(Q_BLOCK)[None, :]
    tables = []
    for _, dil in DILATED:
        rel_prev = Q_BLOCK + i - j
        rel_cur = i - j
        rel = np.concatenate([rel_prev, rel_cur], axis=1)
        valid = np.concatenate([rel_prev <= Q_BLOCK, rel_cur >= 0], axis=1)
        dist = np.maximum(rel, 0) * dil
        b32 = _t5_bucket_np(dist, np.float32)
        b64 = _t5_bucket_np(dist, np.float64)
        assert np.array_equal(b32, b64)
        tables.append(np.where(valid, b32, -1).astype(np.int32))
    return np.stack(tables)


def _present_buckets(buckets_np):
    return [sorted(set(int(v) for v in np.unique(buckets_np[c]) if v >= 0)) for c in range(N_CFG)]


def _bias_tables_body(buckets_np):
    present = _present_buckets(buckets_np)

    def tables(rb_ref, bk_ref, o_ref, ot_ref):
        for c in range(N_CFG):
            bk = bk_ref[c]
            for h in range(B_HEADS):
                acc = jnp.full((Q_BLOCK, 2 * Q_BLOCK), NEG_INF, F32)
                for b in present[c]:
                    acc = jnp.where(bk == b, rb_ref[h, b], acc)
                o_ref[c, h] = acc
                ot_ref[c, h] = acc.T

    return tables


def _proj_fwd(x, g1, w_in_t, ln_g, ln_b, w_s, bz):
    m = x.shape[0]
    tm = GATE_ROWS
    pavg = _group_mean_matrix()

    def body(x_ref, g_ref, w_ref, lg_ref, lb_ref, ws_ref, bz_ref, p_ref, h_ref, uv_ref, qkv_ref, a_ref):
        xv = x_ref[...]
        h = (xv * _rstd(xv) * g_ref[...]).astype(BF16)
        h_ref[...] = h
        acc = _dot_nt(h, w_ref[...])
        uv_ref[...] = acc[:, :2 * A_WIDTH]
        for s in range(QKV_SLABS):
            qkv_ref[s] = acc[:, 2 * A_WIDTH + s * PAIR:2 * A_WIDTH + (s + 1) * PAIR]
        _gate_fwd_rows(acc[:, :A_WIDTH], acc[:, A_WIDTH:2 * A_WIDTH], lg_ref[...], lb_ref[...], ws_ref,
                       bz_ref[...], p_ref[...], a_ref)

    return pl.pallas_call(
        body, grid=(m // tm,),
        in_specs=[pl.BlockSpec((tm, D_MODEL), lambda i: (i, 0)), _vec_spec(),
                  pl.BlockSpec((IN_COLS, D_MODEL), lambda i: (0, 0)),
                  _full_spec((1, A_WIDTH)), _full_spec((1, A_WIDTH)), _full_spec((A_GROUPS, CHUNK, CHUNK)),
                  _full_spec((CHUNK, A_WIDTH)), _full_spec((A_WIDTH, A_WIDTH))],
        out_specs=[pl.BlockSpec((tm, D_MODEL), lambda i: (i, 0)),
                   pl.BlockSpec((tm, 2 * A_WIDTH), lambda i: (i, 0)),
                   pl.BlockSpec((QKV_SLABS, tm, PAIR), lambda i: (0, i, 0)),
                   pl.BlockSpec((tm, A_WIDTH), lambda i: (i, 0))],
        out_shape=[jax.ShapeDtypeStruct((m, D_MODEL), BF16), jax.ShapeDtypeStruct((m, 2 * A_WIDTH), F32),
                   jax.ShapeDtypeStruct((QKV_SLABS, m, PAIR), F32), jax.ShapeDtypeStruct((m, A_WIDTH), BF16)],
        compiler_params=_params(("parallel",)), name="proj_fwd",
    )(x, g1, w_in_t, ln_g, ln_b, w_s, bz, pavg)


def _pair_masks():
    lane = lax.broadcasted_iota(jnp.int32, (1, PAIR), 1)
    return [lane < HEAD_DIM, lane >= HEAD_DIM]


def _block_rows(idx, dil):
    static = isinstance(idx, int)
    r, n = idx % dil, idx // dil

    def rows_of(block):
        start = r + (dil * Q_BLOCK) * block
        if dil == 1:
            return pl.ds(start if static else pl.multiple_of(start, Q_BLOCK), Q_BLOCK)
        return pl.ds(start, Q_BLOCK, stride=dil)

    prev = rows_of(n - 1) if not static or n > 0 else None
    return rows_of(n), prev


def _attn_fwd(qkv, bias, batch, owns):
    m = qkv.shape[1]
    comb_rows = 256
    nt = len(owns)
    shapes = [g.shape[1:] for g in owns]
    n_steps = batch * N_PAIR
    early, late = list(range(nt // 2)), list(range(nt // 2, nt))

    def body(q_ref, k_ref, v_ref, b_ref, *rest):
        o_ref, l_ref = rest[nt:nt + 2]
        gat_refs = rest[nt + 2:2 * nt + 2]
        scratch = rest[2 * nt + 2:]
        oc_refs, lc_refs = scratch[:N_CFG], scratch[N_CFG:2 * N_CFG]
        step = pl.program_id(0) * N_PAIR + pl.program_id(1)
        gather = _GatherPlan(shapes, gat_refs, *scratch[2 * N_CFG:])

        @pl.when(step == 0)
        def _():
            gather.start(early + late)

        @pl.when(step == n_steps // 2)
        def _():
            gather.forward(early)

        @pl.when(step == n_steps - 2)
        def _():
            gather.forward(late)

        masks = _pair_masks()
        for ci, (_, dil) in enumerate(DILATED):
            nb = SEQ // dil // Q_BLOCK

            def block(trip, ci=ci, dil=dil, nb=nb):
                work = []
                for u in range(FWD_BLOCKS_PER_TRIP):
                    rows, prow = _block_rows(trip * FWD_BLOCKS_PER_TRIP + u, dil)
                    has_prev = nb > 1 and prow is not None
                    q = q_ref[rows, :] * 0.125
                    kc = k_ref[rows, :].astype(BF16)
                    vc = v_ref[rows, :]
                    kp = k_ref[prow, :].astype(BF16) if has_prev else None
                    vp = v_ref[prow, :] if has_prev else None
                    tiles = []
                    for h in range(2):
                        qh = jnp.where(masks[h], q, 0.0).astype(BF16)
                        sc = _dot_nt(qh, kc) + b_ref[ci, h, :, Q_BLOCK:]
                        sp = _dot_nt(qh, kp) + b_ref[ci, h, :, :Q_BLOCK] if has_prev else None
                        tiles.append((sc, sp))
                    work.append((rows, vc, vp, tiles))
                probs = []
                for _, _, _, tiles in work:
                    ps = []
                    for sc, sp in tiles:
                        mx = jnp.max(sc if sp is None else jnp.maximum(sc, sp), axis=1, keepdims=True)
                        pc = jnp.exp(sc - mx).astype(BF16)
                        pp = None if sp is None else jnp.exp(sp - mx).astype(BF16)
                        ps.append((mx, pc, pp))
                    probs.append(ps)
                for (rows, vc, vp, _), ps in zip(work, probs):
                    res = []
                    for h, (_, pc, pp) in enumerate(ps):
                        r = _dot(pc, jnp.where(masks[h], vc, 1.0).astype(BF16))
                        if pp is not None:
                            r = r + _dot(pp, jnp.where(masks[h], vp, 1.0).astype(BF16))
                        res.append(r)
                    num = jnp.where(masks[0], res[0], res[1])
                    den = pltpu.roll(jnp.where(masks[0], res[1], res[0]), HEAD_DIM, 1)
                    oc_refs[ci][rows, :] = num / den
                    lc_refs[ci][rows, :] = jnp.where(masks[0], ps[0][0], ps[1][0]) + jnp.log(den)

            for trip in range(BLOCKS_PER_CFG // FWD_BLOCKS_PER_TRIP):
                block(trip)

        def combine(i, carry):
            rr = pl.ds(pl.multiple_of(i * comb_rows, comb_rows), comb_rows)
            ls = [lc_refs[c][rr, :] for c in range(N_CFG)]
            mx = functools.reduce(jnp.maximum, ls)
            ws = [jnp.exp(l - mx) for l in ls]
            tot = functools.reduce(lambda a, b: a + b, ws)
            o = functools.reduce(lambda a, b: a + b, [ws[c] * oc_refs[c][rr, :] for c in range(N_CFG)]) / tot
            o_ref[rr, :] = o.astype(BF16)
            l_ref[rr, :] = mx + jnp.log(tot)
            return carry

        lax.fori_loop(0, SEQ // comb_rows, combine, 0)

        @pl.when(step == n_steps - 1)
        def _():
            gather.finish(early + late)

    def slab(first):
        return pl.BlockSpec((None, SEQ, PAIR), lambda b, p: (first + p, b, 0))

    nat = pl.BlockSpec((SEQ, PAIR), lambda b, p: (b, p))
    res = pl.pallas_call(
        body, grid=(batch, N_PAIR),
        in_specs=[slab(0), slab(N_PAIR), slab(2 * N_PAIR),
                  pl.BlockSpec((N_CFG, 2, Q_BLOCK, 2 * Q_BLOCK), lambda b, p: (0, p, 0, 0))] + [ANY] * nt,
        out_specs=[nat, nat] + [ANY] * nt,
        out_shape=[jax.ShapeDtypeStruct((m, B_WIDTH), BF16), jax.ShapeDtypeStruct((m, B_WIDTH), F32)]
        + [jax.ShapeDtypeStruct(g.shape, g.dtype) for g in owns],
        scratch_shapes=[pltpu.VMEM((SEQ, PAIR), F32)] * (2 * N_CFG) + _sem_pair(6 * nt),
        input_output_aliases={4 + t: 2 + t for t in range(nt)},
        compiler_params=_params(("arbitrary", "arbitrary")), name="attn_fwd",
    )(qkv, qkv, qkv, bias, *owns)
    return res[0], res[1], list(res[2:])


def _attn_bwd(qkv, dmix, o, lse, bias_t, dproj, batch, parts, smalls):
    m = qkv.shape[1]
    nt, ns = len(parts), len(smalls)
    n_steps = N_PAIR * batch

    def body(q_ref, k_ref, v_ref, do_ref, o_ref, l_ref, b_ref, *rest):
        part_refs = rest[1:nt + 1]
        small_refs = rest[nt + 1:nt + 1 + ns]
        pos = nt + 1 + ns
        dproj_ref, ds_ref = rest[pos:pos + 2]
        recv_refs = rest[pos + 2:pos + 2 + nt]
        sum_refs = rest[pos + 2 + nt:pos + 2 + nt + ns]
        pos += 2 + nt + ns
        dq_acc, dk_acc, dv_acc, d_scr, stage, stage_sems, send_sems, recv_sems = rest[pos:pos + 8]
        allreduce = _SmallAllReducePlan(small_refs, sum_refs, rest[pos + 8:pos + 8 + ns],
                                        rest[pos + 8 + ns:pos + 8 + 2 * ns], *rest[pos + 8 + 2 * ns:])
        pair, seq = pl.program_id(0), pl.program_id(1)
        step = pair * batch + seq
        exchange = _ChipExchangePlan(part_refs, recv_refs, send_sems, recv_sems)

        @pl.when(step == 0)
        def _():
            allreduce.start_sibling()

        @pl.when(step == n_steps // 2)
        def _():
            allreduce.sum_sibling_and_start_chips()

        def stage_copies():
            rows = pl.ds(pl.multiple_of(seq * SEQ, SEQ), SEQ)
            return [pltpu.make_async_copy(
                stage.at[k],
                dproj_ref.at[rows, pl.ds(pl.multiple_of(2 * A_WIDTH + k * B_WIDTH + pair * PAIR, PAIR), PAIR)],
                stage_sems.at[k]) for k in range(3)]

        @pl.when(step == 0)
        def _():
            exchange.start()

        @pl.when(pl.program_id(1) == 0)
        def _():
            ds_ref[...] = jnp.zeros_like(ds_ref)

        dq_acc[...] = jnp.zeros_like(dq_acc)
        dk_acc[...] = jnp.zeros_like(dk_acc)
        dv_acc[...] = jnp.zeros_like(dv_acc)
        d_scr[...] = do_ref[...] * o_ref[...].astype(F32)
        masks = _pair_masks()

        def stack_heads(t):
            return jnp.concatenate([jnp.where(masks[0], t, 0.0), jnp.where(masks[1], t, 0.0)], axis=0).astype(BF16)

        for ci, (_, dil) in enumerate(DILATED):
            nb = SEQ // dil // Q_BLOCK

            def block(trip, carry, ci=ci, dil=dil, nb=nb):
                first = []
                for u in range(BWD_BLOCKS_PER_TRIP):
                    rows, prow = _block_rows(trip * BWD_BLOCKS_PER_TRIP + u, dil)
                    has_prev = nb > 1 and prow is not None
                    if has_prev:
                        kcat = jnp.concatenate([k_ref[prow, :], k_ref[rows, :]], axis=0).astype(BF16)
                        vcat = jnp.concatenate([v_ref[prow, :], v_ref[rows, :]], axis=0).astype(BF16)
                    else:
                        kcat = k_ref[rows, :].astype(BF16)
                        vcat = v_ref[rows, :].astype(BF16)
                    qst = stack_heads(q_ref[rows, :] * 0.125)
                    dost = stack_heads(do_ref[rows, :])
                    lt = l_ref[rows, :].T
                    dt = d_scr[rows, :].T
                    lrow = jnp.concatenate([lt[0:1], lt[HEAD_DIM:HEAD_DIM + 1]], axis=1)
                    drow = jnp.concatenate([jnp.sum(dt[:HEAD_DIM], axis=0, keepdims=True),
                                            jnp.sum(dt[HEAD_DIM:], axis=0, keepdims=True)], axis=1)
                    first.append((has_prev, rows, prow, kcat, qst, dost, lrow, drow,
                                  _dot_nt(kcat, qst), _dot_nt(vcat, dost)))
                second = []
                for has_prev, rows, prow, kcat, qst, dost, lrow, drow, st, dpt in first:
                    keys = slice(0, 2 * Q_BLOCK) if has_prev else slice(Q_BLOCK, 2 * Q_BLOCK)
                    bt = jnp.concatenate([b_ref[ci, 0, keys, :], b_ref[ci, 1, keys, :]], axis=1)
                    pt = jnp.exp(st + bt - lrow)
                    dst = pt * (dpt - drow)
                    ds_ref[ci, 0, keys, :] += dst[:, :Q_BLOCK]
                    ds_ref[ci, 1, keys, :] += dst[:, Q_BLOCK:]
                    second.append((has_prev, rows, prow, kcat, qst, dost, pt.astype(BF16), dst.astype(BF16)))
                for has_prev, rows, prow, kcat, qst, dost, pt_bf, dst_bf in second:
                    dk = _dot(dst_bf, qst)
                    dv = _dot(pt_bf, dost)
                    dq2 = _dot_tn(dst_bf, kcat)
                    dq_acc[rows, :] += jnp.where(masks[0], dq2[:Q_BLOCK], dq2[Q_BLOCK:]) * 0.125
                    if has_prev:
                        dk_acc[prow, :] += dk[:Q_BLOCK]
                        dv_acc[prow, :] += dv[:Q_BLOCK]
                        dk_acc[rows, :] += dk[Q_BLOCK:]
                        dv_acc[rows, :] += dv[Q_BLOCK:]
                    else:
                        dk_acc[rows, :] += dk
                        dv_acc[rows, :] += dv
                return carry

            for trip in range(BLOCKS_PER_CFG // BWD_BLOCKS_PER_TRIP):
                block(trip, 0)

        @pl.when(step > 0)
        def _():
            for cp in stage_copies():
                cp.wait()

        stage[0] = dq_acc[...].astype(BF16)
        stage[1] = dk_acc[...].astype(BF16)
        stage[2] = dv_acc[...].astype(BF16)
        for cp in stage_copies():
            cp.start()

        @pl.when(step == n_steps - 1)
        def _():
            for cp in stage_copies():
                cp.wait()
            exchange.finish()
            allreduce.finish()

    def slab(first):
        return pl.BlockSpec((None, SEQ, PAIR), lambda p, b: (first + p, b, 0))

    nat = pl.BlockSpec((SEQ, PAIR), lambda p, b: (b, p))
    tbl = pl.BlockSpec((N_CFG, 2, 2 * Q_BLOCK, Q_BLOCK), lambda p, b: (0, p, 0, 0))
    acc = pltpu.VMEM((SEQ, PAIR), F32)
    vm = pl.BlockSpec(memory_space=pltpu.VMEM)
    res = pl.pallas_call(
        body, grid=(N_PAIR, batch),
        in_specs=[slab(0), slab(N_PAIR), slab(2 * N_PAIR),
                  pl.BlockSpec((SEQ, PAIR), lambda p, b: (b, A_WIDTH // PAIR + p)), nat, nat, tbl]
        + [ANY] * (nt + 1) + [vm] * ns,
        out_specs=[ANY, tbl] + [ANY] * nt + [vm] * ns,
        out_shape=[jax.ShapeDtypeStruct(dproj.shape, dproj.dtype),
                   jax.ShapeDtypeStruct((N_CFG, B_HEADS, 2 * Q_BLOCK, Q_BLOCK), F32)]
        + [jax.ShapeDtypeStruct((3,) + p.shape[1:], p.dtype) for p in parts]
        + [jax.ShapeDtypeStruct(a.shape, F32) for a in smalls],
        input_output_aliases={7: 0},
        scratch_shapes=[acc, acc, acc, acc, pltpu.VMEM((3, SEQ, PAIR), BF16), pltpu.SemaphoreType.DMA((3,))]
        + _sem_pair(3 * nt) + _SmallAllReducePlan.scratch(smalls),
        compiler_params=_params(("arbitrary", "arbitrary")), name="attn_bwd",
    )(qkv, qkv, qkv, dmix, o, lse, bias_t, dproj, *parts, *smalls)
    return res[0], res[1], list(res[2:2 + nt]), list(res[2 + nt:])


def _rel_bias_grad(ds, buckets_np, grads):
    present = _present_buckets(buckets_np)
    nx = len(grads)
    shapes = [g.shape for g in grads]

    def body(bk_ref, ds_ref, *rest):
        o_ref = rest[nx]
        acc_ref = rest[2 * nx + 1]
        exchange = _SiblingExchangePlan(shapes, rest[:nx], rest[nx + 1:2 * nx + 1], *rest[2 * nx + 2:])
        exchange.start()
        acc_ref[...] = jnp.zeros_like(acc_ref)
        for c in range(N_CFG):
            bk = bk_ref[c]
            for h in range(B_HEADS):
                dsv = ds_ref[c, h]
                for b in present[c]:
                    part = jnp.sum(jnp.where(bk == b, dsv, 0.0), axis=0, keepdims=True)
                    acc_ref[pl.ds(h * NUM_BUCKETS + b, 1), :] += part
        o_ref[...] = jnp.sum(acc_ref[...], axis=1, keepdims=True)
        exchange.finish()

    vm = pl.BlockSpec(memory_space=pltpu.VMEM)
    res = pl.pallas_call(
        body, in_specs=[vm, vm] + [ANY] * nx, out_specs=[vm] + [ANY] * nx,
        out_shape=[jax.ShapeDtypeStruct((B_HEADS * NUM_BUCKETS, 1), F32)]
        + [jax.ShapeDtypeStruct((N_SHARD, s[1] // 2, s[2]), F32) for s in shapes],
        scratch_shapes=[pltpu.VMEM((B_HEADS * NUM_BUCKETS, buckets_np.shape[-1]), F32)] + _sem_pair(nx),
        compiler_params=_params(), name="rel_bias_grad",
    )(jnp.asarray(buckets_np), ds, *grads)
    return res[0], list(res[1:])


def _row_index():
    return lax.broadcasted_iota(jnp.int32, (SEQ, LANE_BLOCK), 0)


def _shift_down(x, k, row):
    return jnp.where(row >= k, pltpu.roll(x, k, 0), 0.0)


def _shift_up(x, k, row):
    return jnp.where(row < SEQ - k, pltpu.roll(x, SEQ - k, 0), 0.0)


def _convgate_fwd(gate, up, conv_w, conv_b, batch):
    m = gate.shape[0]

    def body(g_ref, u_ref, w_ref, b_ref, a_ref):
        g = g_ref[...].astype(F32)
        w = w_ref[...]
        row = _row_index()
        c = b_ref[...] + w[0:1] * _shift_down(g, 2, row) + w[1:2] * _shift_down(g, 1, row) + w[2:3] * g
        a_ref[...] = (_gelu(c) * u_ref[...].astype(F32)).astype(BF16)

    blk = pl.BlockSpec((SEQ, LANE_BLOCK), lambda b, j: (b, j))
    return pl.pallas_call(
        body, grid=(batch, D_FF // LANE_BLOCK),
        in_specs=[blk, blk, pl.BlockSpec((3, LANE_BLOCK), lambda b, j: (0, j)),
                  pl.BlockSpec((1, LANE_BLOCK), lambda b, j: (0, j))],
        out_specs=blk,
        out_shape=jax.ShapeDtypeStruct((m, D_FF), BF16),
        compiler_params=_params(("parallel", "parallel")), name="convgate_fwd",
    )(gate, up, conv_w, conv_b)


def _convgate_bwd(gate, up, dact, conv_w, conv_b, batch):
    m = gate.shape[0]

    def body(g_ref, u_ref, da_ref, w_ref, b_ref, dg_ref, du_ref, dw_ref, db_ref):
        @pl.when(pl.program_id(1) == 0)
        def _():
            dw_ref[...] = jnp.zeros_like(dw_ref)
            db_ref[...] = jnp.zeros_like(db_ref)

        g = g_ref[...].astype(F32)
        w = w_ref[...]
        row = _row_index()
        g1 = _shift_down(g, 1, row)
        g2 = _shift_down(g, 2, row)
        c = b_ref[...] + w[0:1] * g2 + w[1:2] * g1 + w[2:3] * g
        gg, dgg = _gelu_and_grad(c)
        da = da_ref[...].astype(F32)
        du_ref[...] = (da * gg).astype(BF16)
        dc = da * u_ref[...].astype(F32) * dgg
        db_ref[...] += jnp.sum(dc, axis=0, keepdims=True)
        dw_ref[0:1, :] += jnp.sum(dc * g2, axis=0, keepdims=True)
        dw_ref[1:2, :] += jnp.sum(dc * g1, axis=0, keepdims=True)
        dw_ref[2:3, :] += jnp.sum(dc * g, axis=0, keepdims=True)
        dg_ref[...] = (w[2:3] * dc + w[1:2] * _shift_up(dc, 1, row) + w[0:1] * _shift_up(dc, 2, row)).astype(BF16)

    blk = pl.BlockSpec((SEQ, LANE_BLOCK), lambda j, b: (b, j))
    wspec = pl.BlockSpec((3, LANE_BLOCK), lambda j, b: (0, j))
    bspec = pl.BlockSpec((1, LANE_BLOCK), lambda j, b: (0, j))
    return pl.pallas_call(
        body, grid=(D_FF // LANE_BLOCK, batch),
        in_specs=[blk, blk, blk, wspec, bspec],
        out_specs=[blk, blk, wspec, bspec],
        out_shape=[jax.ShapeDtypeStruct((m, D_FF), BF16), jax.ShapeDtypeStruct((m, D_FF), BF16),
                   jax.ShapeDtypeStruct((3, D_FF), F32), jax.ShapeDtypeStruct((1, D_FF), F32)],
        compiler_params=_params(("parallel", "arbitrary")), name="convgate_bwd",
    )(gate, up, dact, conv_w, conv_b)


def _gather_weights(shards, conv_w_shard, rel_bias, buckets_np):
    nt = len(shards)
    shapes = [sh.shape for sh in shards]
    ts = list(range(nt))
    tables = _bias_tables_body(buckets_np)

    def body(*refs):
        shard_refs = refs[:nt]
        cw_ref, rb_ref, bk_ref = refs[nt:nt + 3]
        out_refs = refs[nt + 3:2 * nt + 3]
        cw_out, bias_ref, bias_t_ref = refs[2 * nt + 3:2 * nt + 6]
        scratch = refs[2 * nt + 6:]
        f32_refs, bf16_refs = scratch[:nt], scratch[nt:2 * nt]
        load_sems, store_sems, send_sems, recv_sems, cw_send, cw_recv = scratch[2 * nt:]
        plan = _RelayGatherPlan(shapes[:1], bf16_refs[:1], out_refs[:1], send_sems, recv_sems)
        x, y, c, chips = _mesh_pos()
        loads = [pltpu.make_async_copy(shard_refs[t], f32_refs[t], load_sems.at[t]) for t in ts]
        stores = [pltpu.make_async_copy(bf16_refs[t], out_refs[t].at[2 * x + y], store_sems.at[t]) for t in ts]
        stores.append(pltpu.make_async_copy(cw_ref, cw_out.at[2 * x + y], store_sems.at[nt]))

        def cw_copy(j, src, dst, chip):
            return pltpu.make_async_remote_copy(src_ref=src, dst_ref=dst, send_sem=cw_send.at[j],
                                                recv_sem=cw_recv.at[j], device_id=(*chip, c), device_id_type=MESH)

        def to_bf16(t):
            loads[t].wait()
            bf16_refs[t][...] = f32_refs[t][...].astype(BF16)
            stores[t].start()

        for cp in loads:
            cp.start()
        to_bf16(0)
        plan.start([0])
        cw_sends = [cw_copy(j, cw_ref, cw_out.at[2 * x + y], chip) for j, chip in enumerate(chips)]
        for cp in cw_sends + stores[nt:]:
            cp.start()
        for t in ts[1:]:
            to_bf16(t)
        plan.relay([0])
        tables(rb_ref, bk_ref, bias_ref, bias_t_ref)
        plan.forward([0])
        for j, chip in enumerate(chips):
            dst = cw_out.at[2 * chip[0] + chip[1]]
            cw_copy(j, dst, dst, chip).wait_recv()
        plan.finish([0])
        for cp in cw_sends:
            cp.wait_send()
        for cp in stores:
            cp.wait()

    out_shape = [jax.ShapeDtypeStruct((N_SHARD,) + sh.shape, BF16) for sh in shards]
    out_shape.append(jax.ShapeDtypeStruct((N_SHARD,) + conv_w_shard.shape, conv_w_shard.dtype))
    out_shape += [jax.ShapeDtypeStruct((N_CFG, B_HEADS, Q_BLOCK, 2 * Q_BLOCK), F32),
                  jax.ShapeDtypeStruct((N_CFG, B_HEADS, 2 * Q_BLOCK, Q_BLOCK), F32)]
    vm = pl.BlockSpec(memory_space=pltpu.VMEM)
    res = pl.pallas_call(
        body, in_specs=[ANY] * (nt + 1) + [pl.BlockSpec(memory_space=pltpu.SMEM), vm],
        out_specs=[ANY] * (nt + 1) + [vm, vm], out_shape=out_shape,
        scratch_shapes=[pltpu.VMEM(sh.shape, F32) for sh in shards] + [pltpu.VMEM(sh.shape, BF16) for sh in shards]
        + [pltpu.SemaphoreType.DMA((nt,)), pltpu.SemaphoreType.DMA((nt + 1,))] + _sem_pair(6) + _sem_pair(3),
        compiler_params=pltpu.CompilerParams(has_side_effects=True, vmem_limit_bytes=VMEM_LIMIT),
        name="gather_weights",
    )(*shards, conv_w_shard, rel_bias.T, jnp.asarray(buckets_np))
    return list(res[:nt + 1]), res[nt + 1], res[nt + 2]


def _turn(t, u, s, last):
    return jnp.where(t == u, s, jnp.where(t > u, last, 0))


def _add_halves(gs, recvs, c_idx):
    n = len(gs)
    _, rows2, cols = gs[0].shape
    rows = rows2 // 2
    assert all(g.shape == gs[0].shape for g in gs)

    def body(c_ref, *refs):
        t = pl.program_id(0)
        for u in range(n):
            @pl.when(t == u)
            def _(u=u):
                refs[2 * n + u][...] = (refs[u][...] + refs[n + u][...]).astype(BF16)

    def own(u):
        return pl.BlockSpec((None, None, rows, cols), lambda t, s, c: (_turn(t, u, s, N_SHARD - 1), c[0], 0, 0))

    def plain(u):
        return pl.BlockSpec((None, rows, cols), lambda t, s, c: (_turn(t, u, s, N_SHARD - 1), 0, 0))

    return pl.pallas_call(
        body,
        grid_spec=pltpu.PrefetchScalarGridSpec(
            num_scalar_prefetch=1, grid=(n, N_SHARD),
            in_specs=[own(u) for u in range(n)] + [plain(u) for u in range(n)],
            out_specs=[plain(u) for u in range(n)]),
        out_shape=[jax.ShapeDtypeStruct((N_SHARD, rows, cols), BF16)] * n,
        compiler_params=_params(("arbitrary", "arbitrary")), name="rs_add_halves",
    )(c_idx, *[g.reshape(N_SHARD, 2, rows, cols) for g in gs], *recvs)


def _add_chips(parts, recvs, s_idx, c_idx):
    n = len(parts)
    _, rows, cols = parts[0].shape
    assert all(p.shape == parts[0].shape for p in parts)

    def body(idx_ref, *refs):
        t = pl.program_id(0)
        for u in range(n):
            @pl.when(t == u)
            def _(u=u):
                acc = refs[u][...].astype(F32)
                for j in range(3):
                    acc = acc + refs[n + u][j].astype(F32)
                refs[2 * n + u][...] = acc

    res = pl.pallas_call(
        body,
        grid_spec=pltpu.PrefetchScalarGridSpec(
            num_scalar_prefetch=1, grid=(n,),
            in_specs=[pl.BlockSpec((None, rows, cols), lambda t, idx: (idx[0], 0, 0))] * n
            + [pl.BlockSpec((3, rows, cols), lambda t, idx: (0, 0, 0))] * n,
            out_specs=[pl.BlockSpec((None, rows, cols), lambda t, idx: (idx[1], 0, 0))] * n),
        out_shape=[jax.ShapeDtypeStruct((2, rows, cols), F32)] * n,
        compiler_params=_params(("arbitrary",)), name="rs_add_chips",
    )(jnp.concatenate([s_idx, c_idx]), *parts, *recvs)
    return [r.reshape(2 * rows, cols) for r in res]


def _finish_reductions(fulls, arrays):
    nt, n = len(fulls), len(arrays)

    def body(*refs):
        in_refs = refs[nt:nt + n]
        full_refs, out_refs = refs[nt + n:2 * nt + n], refs[2 * nt + n:2 * nt + 2 * n]
        pos = 2 * nt + 2 * n
        share_send, share_recv = refs[pos + 2 * n:pos + 2 * n + 2]
        allreduce = _SmallAllReducePlan(in_refs, out_refs, refs[pos:pos + n], refs[pos + n:pos + 2 * n],
                                        *refs[pos + 2 * n + 2:])
        x, y, c, _ = _mesh_pos()

        def half(t, which):
            rows = fulls[t].shape[0] // 2
            return full_refs[t].at[pl.ds(which * rows, rows), :]

        def share(t, which):
            return pltpu.make_async_remote_copy(
                src_ref=half(t, which), dst_ref=half(t, which), send_sem=share_send.at[t],
                recv_sem=share_recv.at[t], device_id=(x, y, 1 - c), device_id_type=MESH)

        for t in range(nt):
            share(t, c).start()
        allreduce.start_sibling()
        allreduce.sum_sibling_and_start_chips()
        allreduce.finish()
        for t in range(nt):
            share(t, 1 - c).wait_recv()
        for t in range(nt):
            share(t, c).wait_send()

    vm = pl.BlockSpec(memory_space=pltpu.VMEM)
    res = pl.pallas_call(
        body, in_specs=[ANY] * nt + [vm] * n, out_specs=[ANY] * nt + [vm] * n,
        out_shape=[jax.ShapeDtypeStruct(f.shape, f.dtype) for f in fulls]
        + [jax.ShapeDtypeStruct(a.shape, F32) for a in arrays],
        input_output_aliases={t: t for t in range(nt)},
        scratch_shapes=[pltpu.VMEM(a.shape, F32) for a in arrays] + [pltpu.VMEM((3,) + a.shape, F32) for a in arrays]
        + _sem_pair(nt) + _sem_pair(4 * n),
        compiler_params=pltpu.CompilerParams(has_side_effects=True),
        name="finish_reductions",
    )(*fulls, *arrays)
    return list(res[:nt]), list(res[nt:])


def _from_col_shards(g):
    n, rows, cols = g.shape
    return g.transpose(1, 0, 2).reshape(rows, n * cols)


def _train_step(x, tgt, g1, g2, g3, g4, shards, ln_g, ln_b, w_s, b_s, rel_bias, conv_w_shard, conv_b, batch,
                s_idx, c_idx):
    buckets = _bucket_tables()
    bz = jnp.repeat(b_s.T, HEAD_DIM, axis=1)
    w_st = jnp.swapaxes(w_s, 1, 2)

    def shard_major(g):
        return g.reshape(N_SHARD, g.shape[0] // N_SHARD, D_MODEL)

    names = ["w_in", "w_out", "w_gate", "w_up", "w_down"]
    (g_in, g_out, g_gate, g_up, g_down, g_convw), bias, bias_t = _gather_weights(
        [shards[n] for n in names], conv_w_shard, rel_bias, buckets)
    w_in_t = g_in.reshape(IN_COLS, D_MODEL)
    conv_w = _from_col_shards(g_convw)

    h1, uv, qkv, a = _proj_fwd(x, g1, w_in_t, ln_g, ln_b, w_s, bz)
    o_bf, lse, (g_out, g_gate, g_up) = _attn_fwd(qkv, bias, batch, [g_out, g_gate, g_up])
    w_out = g_out.reshape(D_MODEL, D_MODEL)
    w_gate_t = g_gate.reshape(D_FF, D_MODEL)
    w_up_t = g_up.reshape(D_FF, D_MODEL)
    (y1, x1, h2), _ = _fused_rows(
        "out_proj_mid_fwd", 512,
        [(a, w_out, "nn", slice(0, A_WIDTH)), (o_bf, w_out, "nn", slice(A_WIDTH, D_MODEL))],
        [x], [g2, g3], _mid_fwd_rows, [F32, F32, BF16], [])
    gate, up, g_down = _mm_pair_nt(h2, w_gate_t, w_up_t, g_down, tm=1024, tn=1408, out_dtype=BF16,
                                   name="mm_gate_up")
    w_down = g_down.reshape(D_FF, D_MODEL)
    act = _convgate_fwd(gate, up, conv_w, conv_b, batch)
    (dx2, dy2, dg4, loss), _ = _fused_rows(
        "down_proj_loss_head", 512, [(act, w_down, "nn", None)], [x1, tgt], [g4], _loss_head_rows,
        [F32, BF16], [(1, D_MODEL), (1, 128)])

    dact = _mm(dy2, w_down, dims="nt", tm=1024, tn=1408, tk=1024, out_dtype=BF16, name="mm_dact")
    dw_down = _mm(act, dy2, dims="tn", tm=1408, tn=1024, tk=1024, out_dtype=F32, name="mm_dw_down")
    dgate, dup, dconv_w, dconv_b = _convgate_bwd(gate, up, dact, conv_w, conv_b, batch)
    (dx1, dy1, dg2, dg3), _ = _fused_rows(
        "dh2_mid_bwd", 256, [(dgate, w_gate_t, "nn", None), (dup, w_up_t, "nn", None)],
        [x1, y1, dx2], [g2, g3], _mid_bwd_rows, [F32, BF16], [(1, D_MODEL), (1, D_MODEL)])
    dw_gate_t, dw_up_t = _mm_pair_tn(dgate, dup, h2, tm=1408, tk=1024, name="mm_dw_gate_up")
    dmix, dw_out = _out_proj_bwd(a, o_bf, dy1, w_out)

    done = [shard_major(g) for g in (dw_down, dw_gate_t, dw_up_t, dw_out)]
    (dproj, dln_g, dln_b, dw_s, dbz), recv_a = _gate_bwd(uv, dmix, ln_g, ln_b, w_s, w_st, bz, done)
    parts = _add_halves(done[:3], recv_a[:3], c_idx) + _add_halves(done[3:], recv_a[3:], c_idx)
    early = dict(loss=loss, norm_mix_post=dg2, norm_ffn_pre=dg3, norm_ffn_post=dg4, ln_v_gain=dln_g,
                 ln_v_bias=dln_b, spatial_w=dw_s, spatial_b=dbz, conv_w=dconv_w, conv_b=dconv_b)
    dproj, ds, recv, early_sums = _attn_bwd(qkv, dmix, o_bf, lse, bias_t, dproj, batch, parts, list(early.values()))
    fulls = _add_chips(parts[:3], recv[:3], s_idx, c_idx) + _add_chips(parts[3:], recv[3:], s_idx, c_idx)
    dw_in_t = _mm(dproj, h1, dims="tn", tm=1408, tn=1024, tk=1024, out_dtype=F32, name="mm_dw_in")
    last = [shard_major(dw_in_t)]
    drel, recv_in_a = _rel_bias_grad(ds, np.ascontiguousarray(np.swapaxes(buckets, 1, 2)), last)
    part_in = _add_halves(last, recv_in_a, c_idx)
    (dx0, dg1), recv_in = _fused_rows(
        "dh1_in_bwd", 512, [(dproj, w_in_t, "nn", None)], [x, dx1], [g1], _in_bwd_rows,
        [F32], [(1, D_MODEL)], exchange=part_in)
    fulls += _add_chips(part_in, recv_in, s_idx, c_idx)
    half_reduced = dict(zip(["w_down", "w_gate", "w_up", "w_out", "w_in"], fulls))

    return dx0, dict(zip(early, early_sums)), dict(norm_mix_pre=dg1, rel_bias=drel), half_reduced


def _adamw_update(w, g, m, v):
    nm = ADAM_B1 * m + (1.0 - ADAM_B1) * g
    nv = ADAM_B2 * v + (1.0 - ADAM_B2) * (g * g)
    m_hat = nm / (1.0 - ADAM_B1 ** ADAM_STEP)
    v_hat = nv / (1.0 - ADAM_B2 ** ADAM_STEP)
    return -ADAM_LR * (m_hat / (jnp.sqrt(v_hat) + ADAM_EPS) + ADAM_WD * w), nm, nv


def _adamw(w, g, m, v, name):
    rows, cols = w.shape
    tr = next(cand for cand in (352, 256, 128) if rows % cand == 0)

    def body(w_ref, g_ref, m_ref, v_ref, go_ref, d_ref, nm_ref, nv_ref):
        gv = g_ref[...]
        go_ref[...] = gv
        d_ref[...], nm_ref[...], nv_ref[...] = _adamw_update(w_ref[...], gv, m_ref[...], v_ref[...])

    spec = pl.BlockSpec((tr, cols), lambda i: (i, 0))
    sds = jax.ShapeDtypeStruct((rows, cols), F32)
    return pl.pallas_call(
        body, grid=(rows // tr,), in_specs=[spec] * 4, out_specs=[spec] * 4, out_shape=[sds] * 4,
        compiler_params=_params(("parallel",)), name=name,
    )(w, g, m, v)


def _adamw_small(ws, gs, ms, vs):
    n = len(ws)

    def body(*refs):
        w_refs, g_refs, m_refs, v_refs = refs[:n], refs[n:2 * n], refs[2 * n:3 * n], refs[3 * n:4 * n]
        d_refs, nm_refs, nv_refs = refs[4 * n:5 * n], refs[5 * n:6 * n], refs[6 * n:7 * n]
        for t in range(n):
            d_refs[t][...], nm_refs[t][...], nv_refs[t][...] = _adamw_update(
                w_refs[t][...], g_refs[t][...], m_refs[t][...], v_refs[t][...])

    vm = pl.BlockSpec(memory_space=pltpu.VMEM)
    sds = [jax.ShapeDtypeStruct(w.shape, F32) for w in ws]
    res = pl.pallas_call(
        body, in_specs=[vm] * (4 * n), out_specs=[vm] * (3 * n), out_shape=sds * 3,
        compiler_params=_params(), name="adamw_small",
    )(*ws, *gs, *ms, *vs)
    return res[:n], res[n:2 * n], res[2 * n:]


SMALL = ["norm_mix_pre", "norm_mix_post", "norm_ffn_pre", "norm_ffn_post", "ln_v_gain", "ln_v_bias",
         "spatial_w", "spatial_b", "rel_bias", "conv_b"]
LARGE = ["w_in", "w_gate", "w_up", "w_down", "w_out"]
TRANSPOSED = ("w_in", "w_gate", "w_up")
ORDER = ["norm_mix_pre", "norm_mix_post", "norm_ffn_pre", "norm_ffn_post", "w_in", "ln_v_gain", "ln_v_bias",
         "spatial_w", "spatial_b", "rel_bias", "w_out", "w_gate", "w_up", "conv_w", "conv_b", "w_down"]


def kernel(x, norm_mix_pre, norm_mix_post, norm_ffn_pre, norm_ffn_post, w_in, ln_v_gain, ln_v_bias, spatial_w, spatial_b, rel_bias, w_out, w_gate, w_up, conv_w, conv_b, w_down, loss_target, m_norm_mix_pre, m_norm_mix_post, m_norm_ffn_pre, m_norm_ffn_post, m_w_in, m_ln_v_gain, m_ln_v_bias, m_spatial_w, m_spatial_b, m_rel_bias, m_w_out, m_w_gate, m_w_up, m_conv_w, m_conv_b, m_w_down, v_norm_mix_pre, v_norm_mix_post, v_norm_ffn_pre, v_norm_ffn_post, v_w_in, v_ln_v_gain, v_ln_v_bias, v_spatial_w, v_spatial_b, v_rel_bias, v_w_out, v_w_gate, v_w_up, v_conv_w, v_conv_b, v_w_down):
    params = dict(norm_mix_pre=norm_mix_pre, norm_mix_post=norm_mix_post, norm_ffn_pre=norm_ffn_pre,
                  norm_ffn_post=norm_ffn_post, w_in=w_in, ln_v_gain=ln_v_gain, ln_v_bias=ln_v_bias,
                  spatial_w=spatial_w, spatial_b=spatial_b, rel_bias=rel_bias, w_out=w_out, w_gate=w_gate,
                  w_up=w_up, conv_w=conv_w, conv_b=conv_b, w_down=w_down)
    mom = dict(norm_mix_pre=m_norm_mix_pre, norm_mix_post=m_norm_mix_post, norm_ffn_pre=m_norm_ffn_pre,
               norm_ffn_post=m_norm_ffn_post, w_in=m_w_in, ln_v_gain=m_ln_v_gain, ln_v_bias=m_ln_v_bias,
               spatial_w=m_spatial_w, spatial_b=m_spatial_b, rel_bias=m_rel_bias, w_out=m_w_out, w_gate=m_w_gate,
               w_up=m_w_up, conv_w=m_conv_w, conv_b=m_conv_b, w_down=m_w_down)
    var = dict(norm_mix_pre=v_norm_mix_pre, norm_mix_post=v_norm_mix_post, norm_ffn_pre=v_norm_ffn_pre,
               norm_ffn_post=v_norm_ffn_post, w_in=v_w_in, ln_v_gain=v_ln_v_gain, ln_v_bias=v_ln_v_bias,
               spatial_w=v_spatial_w, spatial_b=v_spatial_b, rel_bias=v_rel_bias, w_out=v_w_out, w_gate=v_w_gate,
               w_up=v_w_up, conv_w=v_conv_w, conv_b=v_conv_b, w_down=v_w_down)

    batch = x.shape[0]
    xi, yi, ci = lax.axis_index("x"), lax.axis_index("y"), lax.axis_index("c")
    s_idx = (2 * xi + yi).astype(jnp.int32).reshape(1)
    c_idx = ci.astype(jnp.int32).reshape(1)

    def local(a, n):
        return jnp.swapaxes(a[0], 0, 1) if n in TRANSPOSED else a[0]

    shards = {n: local(params[n], n) for n in LARGE}
    dx0, total, partial, half_reduced = _train_step(
        x.reshape(batch * SEQ, D_MODEL), loss_target.reshape(batch * SEQ, D_MODEL),
        norm_mix_pre, norm_mix_post, norm_ffn_pre, norm_ffn_post, shards,
        ln_v_gain.reshape(1, A_WIDTH), ln_v_bias.reshape(1, A_WIDTH), spatial_w[0], spatial_b[0], rel_bias,
        conv_w[0], conv_b, batch, s_idx, c_idx)
    grad_x = dx0.reshape(batch, SEQ, D_MODEL)

    names = list(partial)
    fulls, sums = _finish_reductions([half_reduced[n] for n in LARGE], [partial[n] for n in names])
    reduced = dict(zip(LARGE, fulls))
    total.update(zip(names, sums))
    loss = total["loss"][0, 0]
    total["spatial_b"] = total["spatial_b"][:, ::HEAD_DIM].T
    total["rel_bias"] = total["rel_bias"].reshape(B_HEADS, NUM_BUCKETS)
    total["conv_w"] = lax.dynamic_slice_in_dim(total["conv_w"], s_idx[0] * SHARD_FF, SHARD_FF, axis=1)
    small_names = SMALL + ["conv_w"]

    def small(a, n):
        return a.T if n == "rel_bias" else a

    for n in small_names:
        reduced[n] = total[n].reshape(small(params[n], n).shape)

    out_g, out_d, out_m, out_v = {}, {}, {}, {}
    for n in LARGE:
        res = _adamw(local(params[n], n), reduced[n], local(mom[n], n), local(var[n], n), name=f"adamw_{n}")
        if n in TRANSPOSED:
            res = [jnp.swapaxes(r, 0, 1) for r in res]
        out_g[n], out_d[n], out_m[n], out_v[n] = [r[None] for r in res]
    d, nm, nv = _adamw_small([small(params[n], n) for n in small_names], [reduced[n] for n in small_names],
                             [small(mom[n], n) for n in small_names], [small(var[n], n) for n in small_names])
    for n, dd, mm, vv in zip(small_names, d, nm, nv):
        out_g[n], out_d[n], out_m[n], out_v[n] = [small(r, n) for r in (reduced[n], dd, mm, vv)]

    return (loss, grad_x, *[out_g[n] for n in ORDER], *[out_d[n] for n in ORDER],
            *[out_m[n] for n in ORDER], *[out_v[n] for n in ORDER])
```

```python
import functools
import math

import numpy as np
import jax
import jax.numpy as jnp
from jax import lax
from jax.experimental import pallas as pl
from jax.experimental.pallas import tpu as pltpu

F32 = jnp.float32
BF16 = jnp.bfloat16
MESH = pl.DeviceIdType.MESH

D_MODEL = 1024
SEQ = 2048
HEAD_DIM = 64
A_GROUPS = 4
A_WIDTH = 256
B_HEADS = 12
B_WIDTH = 768
CHUNK = 128
DILATED = ((128, 1), (512, 4), (2048, 16))
NUM_BUCKETS = 32
MAX_DISTANCE = 2048
D_FF = 2816
IN_COLS = 2816
NORM_EPS = 1e-6
NEG_INF = -1e30
N_SHARD = 4
SHARD_FF = D_FF // N_SHARD
LANE_BLOCK = 256
VMEM_LIMIT = 56 * 1024 * 1024

ADAM_LR = 0.001
ADAM_B1 = 0.9
ADAM_B2 = 0.999
ADAM_EPS = 1e-08
ADAM_WD = 0.01
ADAM_STEP = 10

GELU_C = math.sqrt(2.0 / math.pi)
GELU_A = 0.044715

ANY = pl.BlockSpec(memory_space=pl.ANY)


def _params(sem=None):
    return pltpu.CompilerParams(dimension_semantics=sem, vmem_limit_bytes=VMEM_LIMIT)


def _dot(a, b, precision=None):
    return jnp.dot(a, b, preferred_element_type=F32, precision=precision)


def _dot_nt(a, b, precision=None):
    return lax.dot_general(a, b, (((1,), (1,)), ((), ())), preferred_element_type=F32, precision=precision)


def _dot_tn(a, b):
    return lax.dot_general(a, b, (((0,), (0,)), ((), ())), preferred_element_type=F32)


def _gelu(x):
    t = jnp.tanh(x * (GELU_C + (GELU_C * GELU_A) * (x * x)))
    return (0.5 * x) * (1.0 + t)


def _gelu_and_grad(x):
    x2 = x * x
    u = 1.0 + jnp.tanh(x * (GELU_C + (GELU_C * GELU_A) * x2))
    hx = 0.5 * x
    dg = u * (0.5 + hx * (2.0 - u) * (GELU_C + (3.0 * GELU_C * GELU_A) * x2))
    return hx * u, dg


def _mesh_pos():
    x, y, c = lax.axis_index("x"), lax.axis_index("y"), lax.axis_index("c")
    chips = [(1 - x, y), (x, 1 - y), (1 - x, 1 - y)]
    return x, y, c, chips


class _GatherPlan:
    def __init__(self, shapes, out_refs, send_sems, recv_sems):
        self.shapes, self.out_refs = shapes, out_refs
        self.send_sems, self.recv_sems = send_sems, recv_sems
        self.x, self.y, self.c, self.chips = _mesh_pos()
        self.sib = (self.x, self.y, 1 - self.c)

    def _half(self, t, chip, which):
        rows = self.shapes[t][0] // 2
        return self.out_refs[t].at[2 * chip[0] + chip[1], pl.ds(which * rows, rows), :]

    def _copy(self, k, src, dst, to):
        return pltpu.make_async_remote_copy(src_ref=src, dst_ref=dst, send_sem=self.send_sems.at[k],
                                            recv_sem=self.recv_sems.at[k], device_id=to, device_id_type=MESH)

    def _sends(self, t):
        own = self._half(t, (self.x, self.y), self.c)
        return [self._copy(6 * t + j, own, own, (*chip, self.c)) for j, chip in enumerate(self.chips)]

    def _forwards(self, t):
        return [self._copy(6 * t + 3 + j, self._half(t, chip, self.c), self._half(t, chip, self.c), self.sib)
                for j, chip in enumerate(self.chips)]

    def start(self, ts):
        for t in ts:
            for cp in self._sends(t):
                cp.start()

    def forward(self, ts):
        for t in ts:
            for j, chip in enumerate(self.chips):
                landed = self._half(t, chip, self.c)
                self._copy(6 * t + j, landed, landed, (*chip, self.c)).wait_recv()
            for cp in self._forwards(t):
                cp.start()

    def finish(self, ts):
        for t in ts:
            for j, chip in enumerate(self.chips):
                other = self._half(t, chip, 1 - self.c)
                self._copy(6 * t + 3 + j, other, other, self.sib).wait_recv()
        for t in ts:
            for cp in self._sends(t) + self._forwards(t):
                cp.wait_send()


class _RelayGatherPlan:
    def __init__(self, shapes, shard_refs, out_refs, send_sems, recv_sems):
        self.shapes, self.shard_refs, self.out_refs = shapes, shard_refs, out_refs
        self.send_sems, self.recv_sems = send_sems, recv_sems
        x, y, c, self.chips = _mesh_pos()
        self.me, self.c, self.sib = (x, y), c, (x, y, 1 - c)
        self.first = (x + c - 2 * x * c, y + (1 - c) - 2 * y * (1 - c))
        self.second = (x + (1 - c) - 2 * x * (1 - c), y + c - 2 * y * c)
        self.diag = (1 - x, 1 - y)

    def _half(self, t, chip, which):
        rows = self.shapes[t][0] // 2
        return self.out_refs[t].at[2 * chip[0] + chip[1], pl.ds(which * rows, rows), :]

    def _copy(self, k, src, dst, to):
        return pltpu.make_async_remote_copy(src_ref=src, dst_ref=dst, send_sem=self.send_sems.at[k],
                                            recv_sem=self.recv_sems.at[k], device_id=to, device_id_type=MESH)

    def _own(self, t):
        rows = self.shapes[t][0] // 2
        return self.shard_refs[t].at[pl.ds(self.c * rows, rows), :]

    def _step1(self, t):
        return self._copy(6 * t, self._own(t), self._half(t, self.me, self.c), (*self.first, self.c))

    def _step2(self, t):
        landed = self._half(t, self.first, self.c)
        return [self._copy(6 * t + 1, self._own(t), self._half(t, self.me, self.c), (*self.second, self.c)),
                self._copy(6 * t + 2, landed, landed, (*self.second, self.c))]

    def _forwards(self, t):
        return [self._copy(6 * t + 3 + j, self._half(t, chip, self.c), self._half(t, chip, self.c), self.sib)
                for j, chip in enumerate(self.chips)]

    def start(self, ts):
        for t in ts:
            self._step1(t).start()
            self._step2(t)[0].start()

    def relay(self, ts):
        for t in ts:
            landed = self._half(t, self.first, self.c)
            self._copy(6 * t, landed, landed, self.sib).wait_recv()
            self._step2(t)[1].start()

    def forward(self, ts):
        for t in ts:
            for k, chip in ((1, self.second), (2, self.diag)):
                landed = self._half(t, chip, self.c)
                self._copy(6 * t + k, landed, landed, self.sib).wait_recv()
            for cp in self._forwards(t):
                cp.start()

    def finish(self, ts):
        for t in ts:
            for j, chip in enumerate(self.chips):
                other = self._half(t, chip, 1 - self.c)
                self._copy(6 * t + 3 + j, other, other, self.sib).wait_recv()
        for t in ts:
            for cp in [self._step1(t)] + self._step2(t) + self._forwards(t):
                cp.wait_send()


class _SiblingExchangePlan:
    def __init__(self, shapes, grad_refs, out_refs, send_sems, recv_sems):
        self.shapes, self.grad_refs, self.out_refs = shapes, grad_refs, out_refs
        self.send_sems, self.recv_sems = send_sems, recv_sems
        self.x, self.y, self.c, _ = _mesh_pos()

    def _copies(self):
        out = []
        for t, (g, o) in enumerate(zip(self.grad_refs, self.out_refs)):
            rows = self.shapes[t][1] // 2
            out.append(pltpu.make_async_remote_copy(
                src_ref=g.at[:, pl.ds((1 - self.c) * rows, rows), :], dst_ref=o, send_sem=self.send_sems.at[t],
                recv_sem=self.recv_sems.at[t], device_id=(self.x, self.y, 1 - self.c), device_id_type=MESH))
        return out

    def start(self):
        for cp in self._copies():
            cp.start()

    def finish(self):
        for cp in self._copies():
            cp.wait()


class _ChipExchangePlan:
    def __init__(self, part_refs, out_refs, send_sems, recv_sems):
        self.part_refs, self.out_refs, self.send_sems, self.recv_sems = part_refs, out_refs, send_sems, recv_sems
        _, _, self.c, self.chips = _mesh_pos()

    def _copies(self):
        return [pltpu.make_async_remote_copy(
            src_ref=p.at[2 * chip[0] + chip[1]], dst_ref=o.at[j], send_sem=self.send_sems.at[3 * t + j],
            recv_sem=self.recv_sems.at[3 * t + j], device_id=(*chip, self.c), device_id_type=MESH)
            for t, (p, o) in enumerate(zip(self.part_refs, self.out_refs)) for j, chip in enumerate(self.chips)]

    def start(self):
        for cp in self._copies():
            cp.start()

    def finish(self):
        for cp in self._copies():
            cp.wait()


class _SmallAllReducePlan:
    def __init__(self, in_refs, out_refs, sib_refs, chip_refs, send_sems, recv_sems):
        self.in_refs, self.out_refs, self.sib_refs, self.chip_refs = in_refs, out_refs, sib_refs, chip_refs
        self.send_sems, self.recv_sems = send_sems, recv_sems
        self.n = len(in_refs)
        self.x, self.y, self.c, self.chips = _mesh_pos()

    def _copy(self, k, src, dst, to):
        return pltpu.make_async_remote_copy(src_ref=src, dst_ref=dst, send_sem=self.send_sems.at[k],
                                            recv_sem=self.recv_sems.at[k], device_id=to, device_id_type=MESH)

    def _first(self):
        return [self._copy(t, self.in_refs[t], self.sib_refs[t], (self.x, self.y, 1 - self.c)) for t in range(self.n)]

    def _second(self):
        return [self._copy(self.n + 3 * t + j, self.out_refs[t], self.chip_refs[t].at[j], (*chip, self.c))
                for t in range(self.n) for j, chip in enumerate(self.chips)]

    def start_sibling(self):
        for cp in self._first():
            cp.start()

    def sum_sibling_and_start_chips(self):
        for cp in self._first():
            cp.wait()
        for t in range(self.n):
            self.out_refs[t][...] = self.in_refs[t][...] + self.sib_refs[t][...]
        for cp in self._second():
            cp.start()

    def finish(self):
        for cp in self._second():
            cp.wait()
        for t in range(self.n):
            self.out_refs[t][...] = ((self.out_refs[t][...] + self.chip_refs[t][0])
                                     + (self.chip_refs[t][1] + self.chip_refs[t][2]))

    @staticmethod
    def scratch(arrays):
        return ([pltpu.VMEM(a.shape, F32) for a in arrays] + [pltpu.VMEM((3,) + a.shape, F32) for a in arrays]
                + _sem_pair(4 * len(arrays)))


def _sem_pair(n):
    return [pltpu.SemaphoreType.DMA((n,)), pltpu.SemaphoreType.DMA((n,))]


def _mm(a, b, *, dims, tm, tn, tk, out_dtype, name):
    if dims == "nn":
        m, k = a.shape
        n = b.shape[1]
        a_spec = pl.BlockSpec((tm, tk), lambda i, j, kk: (i, kk))
        b_spec = pl.BlockSpec((tk, tn), lambda i, j, kk: (kk, j))
        dot = _dot
    elif dims == "nt":
        m, k = a.shape
        n = b.shape[0]
        a_spec = pl.BlockSpec((tm, tk), lambda i, j, kk: (i, kk))
        b_spec = pl.BlockSpec((tn, tk), lambda i, j, kk: (j, kk))
        dot = _dot_nt
    else:
        k, m = a.shape
        n = b.shape[1]
        a_spec = pl.BlockSpec((tk, tm), lambda i, j, kk: (kk, i))
        b_spec = pl.BlockSpec((tk, tn), lambda i, j, kk: (kk, j))
        dot = _dot_tn
    assert m % tm == 0 and n % tn == 0 and k % tk == 0, (name, m, n, k)
    grid = (m // tm, n // tn, k // tk)
    nk = grid[2]
    assert nk == 1 or out_dtype == F32, name

    def body(a_ref, b_ref, o_ref):
        prod = dot(a_ref[...].astype(BF16), b_ref[...].astype(BF16))
        if nk == 1:
            o_ref[...] = prod.astype(out_dtype)
        else:
            kk = pl.program_id(2)

            @pl.when(kk == 0)
            def _():
                o_ref[...] = prod

            @pl.when(kk > 0)
            def _():
                o_ref[...] += prod

    return pl.pallas_call(
        body, grid=grid, in_specs=[a_spec, b_spec],
        out_specs=pl.BlockSpec((tm, tn), lambda i, j, kk: (i, j)),
        out_shape=jax.ShapeDtypeStruct((m, n), out_dtype),
        compiler_params=_params(("parallel", "parallel", "arbitrary")), name=name,
    )(a, b)


def _mm_pair_tn(a1, a2, b, *, tm, tk, name):
    k, m = a1.shape
    n = b.shape[1]
    assert m % tm == 0 and k % tk == 0 and a2.shape == a1.shape, name

    def body(a1_ref, a2_ref, b_ref, o1_ref, o2_ref):
        bv = b_ref[...]
        p1 = _dot_tn(a1_ref[...], bv)
        p2 = _dot_tn(a2_ref[...], bv)
        kk = pl.program_id(1)

        @pl.when(kk == 0)
        def _():
            o1_ref[...] = p1
            o2_ref[...] = p2

        @pl.when(kk > 0)
        def _():
            o1_ref[...] += p1
            o2_ref[...] += p2

    a_spec = pl.BlockSpec((tk, tm), lambda i, kk: (kk, i))
    o_spec = pl.BlockSpec((tm, n), lambda i, kk: (i, 0))
    return pl.pallas_call(
        body, grid=(m // tm, k // tk),
        in_specs=[a_spec, a_spec, pl.BlockSpec((tk, n), lambda i, kk: (kk, 0))],
        out_specs=[o_spec, o_spec],
        out_shape=[jax.ShapeDtypeStruct((m, n), F32)] * 2,
        compiler_params=_params(("parallel", "arbitrary")), name=name,
    )(a1, a2, b)


def _mm_pair_nt(a, w1_t, w2_t, own, *, tm, tn, out_dtype, name):
    m, k = a.shape
    n = w1_t.shape[0]
    assert m % tm == 0 and n % tn == 0 and w2_t.shape == w1_t.shape, name
    grid = (m // tm, n // tn)
    n_steps = grid[0] * grid[1]

    def body(a_ref, w1_ref, w2_ref, own_ref, o1_ref, o2_ref, gat_ref, send_sems, recv_sems):
        del own_ref
        step = pl.program_id(0) * grid[1] + pl.program_id(1)
        gather = _GatherPlan([own.shape[1:]], [gat_ref], send_sems, recv_sems)

        @pl.when(step == 0)
        def _():
            gather.start([0])

        @pl.when(step == (2 * n_steps) // 3)
        def _():
            gather.forward([0])

        av = a_ref[...]
        o1_ref[...] = _dot_nt(av, w1_ref[...]).astype(out_dtype)
        o2_ref[...] = _dot_nt(av, w2_ref[...]).astype(out_dtype)

        @pl.when(step == n_steps - 1)
        def _():
            gather.finish([0])

    w_spec = pl.BlockSpec((tn, k), lambda i, j: (j, 0))
    o_spec = pl.BlockSpec((tm, tn), lambda i, j: (i, j))
    return pl.pallas_call(
        body, grid=grid,
        in_specs=[pl.BlockSpec((tm, k), lambda i, j: (i, 0)), w_spec, w_spec, ANY],
        out_specs=[o_spec, o_spec, ANY],
        out_shape=[jax.ShapeDtypeStruct((m, n), out_dtype)] * 2 + [jax.ShapeDtypeStruct(own.shape, own.dtype)],
        scratch_shapes=_sem_pair(6), input_output_aliases={3: 2},
        compiler_params=_params(("arbitrary", "arbitrary")), name=name,
    )(a, w1_t, w2_t, own)


def _out_proj_bwd(a, o, dy1, w_out, grads):
    m = dy1.shape[0]
    tm = 1024
    nx = len(grads)
    shapes = [g.shape for g in grads]
    n_steps = m // tm

    def body(a_ref, o_ref, dy_ref, w_ref, *rest):
        grad_refs = rest[:nx]
        dmix_ref, dw_ref = rest[nx:nx + 2]
        exchange = _SiblingExchangePlan(shapes, grad_refs, rest[nx + 2:2 * nx + 2], *rest[2 * nx + 2:])

        @pl.when(pl.program_id(0) == 0)
        def _():
            exchange.start()

        dy = dy_ref[...]
        dmix_ref[...] = _dot_nt(dy, w_ref[...])
        top = _dot_tn(a_ref[...], dy)
        bottom = _dot_tn(o_ref[...], dy)

        @pl.when(pl.program_id(0) == 0)
        def _():
            dw_ref[:A_WIDTH, :] = top
            dw_ref[A_WIDTH:, :] = bottom

        @pl.when(pl.program_id(0) > 0)
        def _():
            dw_ref[:A_WIDTH, :] += top
            dw_ref[A_WIDTH:, :] += bottom

        @pl.when(pl.program_id(0) == n_steps - 1)
        def _():
            exchange.finish()

    tile = lambda width: pl.BlockSpec((tm, width), lambda i: (i, 0))
    res = pl.pallas_call(
        body, grid=(n_steps,),
        in_specs=[tile(A_WIDTH), tile(B_WIDTH), tile(D_MODEL), _full_spec((D_MODEL, D_MODEL))] + [ANY] * nx,
        out_specs=[tile(D_MODEL), _full_spec((D_MODEL, D_MODEL))] + [ANY] * nx,
        out_shape=[jax.ShapeDtypeStruct((m, D_MODEL), F32), jax.ShapeDtypeStruct((D_MODEL, D_MODEL), F32)]
        + [jax.ShapeDtypeStruct((N_SHARD, s[1] // 2, s[2]), F32) for s in shapes],
        scratch_shapes=_sem_pair(nx),
        compiler_params=_params(("arbitrary",)), name="out_proj_bwd",
    )(a, o, dy1, w_out, *grads)
    return res[:2], list(res[2:])


def _fused_rows(name, tm, mats, rows, vecs, fn, row_outs, acc_outs, exchange=()):
    m = mats[0][0].shape[0]
    nm, nr, nv, nro, nao, nx = len(mats), len(rows), len(vecs), len(row_outs), len(acc_outs), len(exchange)
    n_steps = m // tm

    def body(*refs):
        a_refs, w_refs = refs[:nm], refs[nm:2 * nm]
        pos = 2 * nm
        row_refs, vec_refs, part_refs = refs[pos:pos + nr], refs[pos + nr:pos + nr + nv], refs[pos + nr + nv:pos + nr + nv + nx]
        pos += nr + nv + nx
        out_refs, acc_refs, recv_refs = refs[pos:pos + nro], refs[pos + nro:pos + nro + nao], refs[pos + nro + nao:pos + nro + nao + nx]
        sems = refs[pos + nro + nao + nx:]
        i = pl.program_id(0)
        if nx:
            plan = _ChipExchangePlan(part_refs, recv_refs, *sems)

            @pl.when(i == 0)
            def _():
                plan.start()

        @pl.when(i == 0)
        def _():
            for r in acc_refs:
                r[...] = jnp.zeros_like(r)

        y = None
        for a_ref, w_ref, (_, _, dims, sl) in zip(a_refs, w_refs, mats):
            w = w_ref[...] if sl is None else w_ref[sl, :]
            part = (_dot if dims == "nn" else _dot_nt)(a_ref[...], w)
            y = part if y is None else y + part
        res = fn(y, *[r[...] for r in row_refs], *[v[...] for v in vec_refs])
        for r, val in zip(out_refs, res[:nro]):
            r[...] = val.astype(r.dtype)
        for r, val in zip(acc_refs, res[nro:]):
            r[...] += val

        if nx:
            @pl.when(i == n_steps - 1)
            def _():
                plan.finish()

    tile = lambda width: pl.BlockSpec((tm, width), lambda i: (i, 0))
    res = pl.pallas_call(
        body, grid=(n_steps,),
        in_specs=[tile(a.shape[1]) for a, _, _, _ in mats] + [_full_spec(w.shape) for _, w, _, _ in mats]
        + [tile(D_MODEL)] * nr + [_full_spec((1, D_MODEL))] * nv + [ANY] * nx,
        out_specs=[tile(D_MODEL)] * nro + [_full_spec(s) for s in acc_outs] + [ANY] * nx,
        out_shape=[jax.ShapeDtypeStruct((m, D_MODEL), dt) for dt in row_outs]
        + [jax.ShapeDtypeStruct(s, F32) for s in acc_outs]
        + [jax.ShapeDtypeStruct((3,) + p.shape[1:], p.dtype) for p in exchange],
        scratch_shapes=_sem_pair(3 * nx) if nx else [],
        compiler_params=_params(("arbitrary",)), name=name,
    )(*[a for a, _, _, _ in mats], *[w for _, w, _, _ in mats], *rows, *vecs, *exchange)
    return list(res[:nro + nao]), list(res[nro + nao:])


def _vec_spec(width=D_MODEL):
    return pl.BlockSpec((1, width), lambda i: (0, 0))


def _rstd(v):
    return lax.rsqrt(jnp.mean(v * v, axis=-1, keepdims=True) + NORM_EPS)


def _mid_fwd_rows(y1, x0, g2, g3):
    x1 = x0 + y1 * _rstd(y1) * g2
    return y1, x1, x1 * _rstd(x1) * g3


def _rms_bwd_rows(dout, v, g):
    r = _rstd(v)
    n = v * r
    dn = dout * g
    dv = r * (dn - n * jnp.mean(dn * n, axis=-1, keepdims=True))
    dg = jnp.sum(dout * n, axis=0, keepdims=True)
    return dv, dg


def _loss_head_rows(y2, x1, tgt, g4):
    x2 = x1 + y2 * _rstd(y2) * g4
    err = x2 - tgt
    loss = 0.5 * jnp.sum(jnp.mean(err * err, axis=-1, keepdims=True), axis=0, keepdims=True)
    dx2 = err * (1.0 / D_MODEL)
    dy2, dg4 = _rms_bwd_rows(dx2, y2, g4)
    return dx2, dy2, dg4, loss


def _mid_bwd_rows(dh2, x1, y1, dx2, g2, g3):
    d3, dg3 = _rms_bwd_rows(dh2, x1, g3)
    dx1 = dx2 + d3
    dy1, dg2 = _rms_bwd_rows(dx1, y1, g2)
    return dx1, dy1, dg2, dg3


def _in_bwd_rows(dh1, x0, dx1, g1):
    d1, dg1 = _rms_bwd_rows(dh1, x0, g1)
    return dx1 + d1, dg1


GATE_ROWS = 512


def _group_mean_matrix():
    p = np.zeros((A_WIDTH, A_WIDTH), np.float32)
    for g in range(A_GROUPS):
        p[g * HEAD_DIM:(g + 1) * HEAD_DIM, g * HEAD_DIM:(g + 1) * HEAD_DIM] = 1.0 / HEAD_DIM
    return jnp.asarray(p)


def _group_masks(width=A_WIDTH):
    lane = lax.broadcasted_iota(jnp.int32, (1, width), 1)
    return [(lane >= g * HEAD_DIM) & (lane < (g + 1) * HEAD_DIM) for g in range(width // HEAD_DIM)]


GROUP_SUM_PRECISION = lax.Precision.HIGH


def _layernorm_groups(vg, pavg):
    hi = GROUP_SUM_PRECISION
    mu = _dot(vg, pavg, hi)
    xc = vg - mu
    var = _dot(xc * xc, pavg, hi)
    rstd = lax.rsqrt(var + NORM_EPS)
    return xc * rstd, rstd


def _spatial_mix(w_bf, vn_chunk_bf, masks, bz):
    z = bz
    for g in range(A_GROUPS):
        z = z + jnp.where(masks[g], _dot(w_bf[g], vn_chunk_bf), 0.0)
    return z


def _full_spec(shape):
    return pl.BlockSpec(shape, lambda i: tuple(0 for _ in shape))


def _gate_fwd_rows(u, v, lg, lb, w_ref, bz, pavg, a_ref):
    masks = _group_masks()
    row = lax.broadcasted_iota(jnp.int32, (CHUNK, CHUNK), 0)
    col = lax.broadcasted_iota(jnp.int32, (CHUNK, CHUNK), 1)
    w_bf = [jnp.where(row >= col, w_ref[g], 0.0).astype(BF16) for g in range(A_GROUPS)]
    ug = _gelu(u)
    vhat, _ = _layernorm_groups(_gelu(v), pavg)
    vn = vhat * lg + lb
    for c in range(GATE_ROWS // CHUNK):
        sl = slice(c * CHUNK, (c + 1) * CHUNK)
        z = _spatial_mix(w_bf, vn[sl].astype(BF16), masks, bz)
        a_ref[sl, :] = (ug[sl] * z).astype(BF16)


def _gate_bwd(uv, dmix, ln_g, ln_b, w_s, w_st, bz, grads):
    m = uv.shape[0]
    pavg = _group_mean_matrix()
    nsteps = m // GATE_ROWS
    nx = len(grads)
    shapes = [g.shape for g in grads]

    def body(u_ref, v_ref, da_ref, lg_ref, lb_ref, w_ref, wt_ref, bz_ref, p_ref, *rest):
        grad_refs = rest[:nx]
        duv_ref, dlg_ref, dlb_ref, dw_ref, dbz_ref = rest[nx:nx + 5]
        recv_refs = rest[nx + 5:2 * nx + 5]
        exchange = _SiblingExchangePlan(shapes, grad_refs, recv_refs, *rest[2 * nx + 5:])
        i = pl.program_id(0)

        @pl.when(i == 0)
        def _():
            exchange.start()
            dlg_ref[...] = jnp.zeros_like(dlg_ref)
            dlb_ref[...] = jnp.zeros_like(dlb_ref)
            dw_ref[...] = jnp.zeros_like(dw_ref)
            dbz_ref[...] = jnp.zeros_like(dbz_ref)

        hi = GROUP_SUM_PRECISION
        masks = _group_masks()
        row = lax.broadcasted_iota(jnp.int32, (CHUNK, CHUNK), 0)
        col = lax.broadcasted_iota(jnp.int32, (CHUNK, CHUNK), 1)
        tril = row >= col
        w_bf = [jnp.where(tril, w_ref[g], 0.0).astype(BF16) for g in range(A_GROUPS)]
        wt_bf = [jnp.where(col >= row, wt_ref[g], 0.0).astype(BF16) for g in range(A_GROUPS)]
        pavg_v = p_ref[...]
        lg = lg_ref[...]
        ug, dug = _gelu_and_grad(u_ref[...])
        vg, dvg_dx = _gelu_and_grad(v_ref[...])
        vhat, rstd = _layernorm_groups(vg, pavg_v)
        vn = vhat * lg + lb_ref[...]
        da = da_ref[...]
        bz = bz_ref[...]
        for c in range(GATE_ROWS // CHUNK):
            sl = slice(c * CHUNK, (c + 1) * CHUNK)
            vn_bf = vn[sl].astype(BF16)
            z = _spatial_mix(w_bf, vn_bf, masks, bz)
            dz = da[sl] * ug[sl]
            duv_ref[sl, 0:A_WIDTH] = (da[sl] * z * dug[sl]).astype(BF16)
            dbz_ref[...] += dz
            dz_bf = dz.astype(BF16)
            dvn = jnp.zeros((CHUNK, A_WIDTH), F32)
            for g in range(A_GROUPS):
                dz_g = jnp.where(masks[g], dz, 0.0).astype(BF16)
                dw_ref[g] += jnp.where(tril, _dot_nt(dz_g, vn_bf), 0.0)
                dvn = dvn + jnp.where(masks[g], _dot(wt_bf[g], dz_bf), 0.0)
            vh = vhat[sl]
            dlb_ref[...] += jnp.sum(dvn, axis=0, keepdims=True)
            dlg_ref[...] += jnp.sum(dvn * vh, axis=0, keepdims=True)
            dvh = dvn * lg
            m1 = _dot(dvh, pavg_v, hi)
            m2 = _dot(dvh * vh, pavg_v, hi)
            duv_ref[sl, A_WIDTH:2 * A_WIDTH] = (rstd[sl] * (dvh - m1 - vh * m2) * dvg_dx[sl]).astype(BF16)

        @pl.when(i == nsteps - 1)
        def _():
            dbz_ref[...] = _dot(dbz_ref[...], pavg_v * float(HEAD_DIM), hi)
            exchange.finish()

    res = pl.pallas_call(
        body, grid=(nsteps,),
        in_specs=[pl.BlockSpec((GATE_ROWS, A_WIDTH), lambda i: (i, 0)),
                  pl.BlockSpec((GATE_ROWS, A_WIDTH), lambda i: (i, 1)),
                  pl.BlockSpec((GATE_ROWS, A_WIDTH), lambda i: (i, 0)),
                  _full_spec((1, A_WIDTH)), _full_spec((1, A_WIDTH)), _full_spec((A_GROUPS, CHUNK, CHUNK)),
                  _full_spec((A_GROUPS, CHUNK, CHUNK)), _full_spec((CHUNK, A_WIDTH)),
                  _full_spec((A_WIDTH, A_WIDTH))] + [ANY] * nx,
        out_specs=[pl.BlockSpec((GATE_ROWS, 2 * A_WIDTH), lambda i: (i, 0)),
                   _full_spec((1, A_WIDTH)), _full_spec((1, A_WIDTH)), _full_spec((A_GROUPS, CHUNK, CHUNK)),
                   _full_spec((CHUNK, A_WIDTH))] + [ANY] * nx,
        out_shape=[jax.ShapeDtypeStruct((m, IN_COLS), BF16),
                   jax.ShapeDtypeStruct((1, A_WIDTH), F32), jax.ShapeDtypeStruct((1, A_WIDTH), F32),
                   jax.ShapeDtypeStruct((A_GROUPS, CHUNK, CHUNK), F32),
                   jax.ShapeDtypeStruct((CHUNK, A_WIDTH), F32)]
        + [jax.ShapeDtypeStruct((N_SHARD, s[1] // 2, s[2]), F32) for s in shapes],
        scratch_shapes=_sem_pair(nx),
        compiler_params=_params(("arbitrary",)), name="gate_bwd",
    )(uv, uv, dmix, ln_g, ln_b, w_s, w_st, bz, pavg, *grads)
    return res[:5], list(res[5:])


Q_BLOCK = 128
PAIR = 2 * HEAD_DIM
N_PAIR = B_HEADS // 2
N_CFG = len(DILATED)
BLOCKS_PER_CFG = SEQ // Q_BLOCK
QKV_SLABS = 3 * N_PAIR
FWD_BLOCKS_PER_TRIP = 8
BWD_BLOCKS_PER_TRIP = 4


def _t5_bucket_np(dist, dtype):
    max_exact = NUM_BUCKETS // 2
    d = np.maximum(dist, 1).astype(dtype)
    large = max_exact + (np.log(d / dtype(max_exact)) / dtype(math.log(MAX_DISTANCE / max_exact))
                         * dtype(NUM_BUCKETS - max_exact))
    large = np.minimum(large.astype(np.int32), NUM_BUCKETS - 1)
    return np.where(dist < max_exact, dist, large)


def _bucket_tables():
    i = np.arange(Q_BLOCK)[:, None]
    j = np.arange(Q_BLOCK)[None, :]
    tables = []
    for _, dil in DILATED:
        rel_prev = Q_BLOCK + i - j
        rel_cur = i - j
        rel = np.concatenate([rel_prev, rel_cur], axis=1)
        valid = np.concatenate([rel_prev <= Q_BLOCK, rel_cur >= 0], axis=1)
        dist = np.maximum(rel, 0) * dil
        b32 = _t5_bucket_np(dist, np.float32)
        b64 = _t5_bucket_np(dist, np.float64)
        assert np.array_equal(b32, b64)
        tables.append(np.where(valid, b32, -1).astype(np.int32))
    return np.stack(tables)


def _present_buckets(buckets_np):
    return [sorted(set(int(v) for v in np.unique(buckets_np[c]) if v >= 0)) for c in range(N_CFG)]


def _bias_tables_body(buckets_np):
    present = _present_buckets(buckets_np)

    def tables(rb_ref, bk_ref, o_ref, ot_ref):
        for c in range(N_CFG):
            bk = bk_ref[c]
            for h in range(B_HEADS):
                acc = jnp.full((Q_BLOCK, 2 * Q_BLOCK), NEG_INF, F32)
                for b in present[c]:
                    acc = jnp.where(bk == b, rb_ref[h, b], acc)
                o_ref[c, h] = acc
                ot_ref[c, h] = acc.T

    return tables


def _proj_fwd(x, g1, w_in_t, ln_g, ln_b, w_s, bz):
    m = x.shape[0]
    tm = GATE_ROWS
    pavg = _group_mean_matrix()

    def body(x_ref, g_ref, w_ref, lg_ref, lb_ref, ws_ref, bz_ref, p_ref, h_ref, uv_ref, qkv_ref, a_ref):
        xv = x_ref[...]
        h = (xv * _rstd(xv) * g_ref[...]).astype(BF16)
        h_ref[...] = h
        acc = _dot_nt(h, w_ref[...])
        uv_ref[...] = acc[:, :2 * A_WIDTH]
        for s in range(QKV_SLABS):
            qkv_ref[s] = acc[:, 2 * A_WIDTH + s * PAIR:2 * A_WIDTH + (s + 1) * PAIR]
        _gate_fwd_rows(acc[:, :A_WIDTH], acc[:, A_WIDTH:2 * A_WIDTH], lg_ref[...], lb_ref[...], ws_ref,
                       bz_ref[...], p_ref[...], a_ref)

    return pl.pallas_call(
        body, grid=(m // tm,),
        in_specs=[pl.BlockSpec((tm, D_MODEL), lambda i: (i, 0)), _vec_spec(),
                  pl.BlockSpec((IN_COLS, D_MODEL), lambda i: (0, 0)),
                  _full_spec((1, A_WIDTH)), _full_spec((1, A_WIDTH)), _full_spec((A_GROUPS, CHUNK, CHUNK)),
                  _full_spec((CHUNK, A_WIDTH)), _full_spec((A_WIDTH, A_WIDTH))],
        out_specs=[pl.BlockSpec((tm, D_MODEL), lambda i: (i, 0)),
                   pl.BlockSpec((tm, 2 * A_WIDTH), lambda i: (i, 0)),
                   pl.BlockSpec((QKV_SLABS, tm, PAIR), lambda i: (0, i, 0)),
                   pl.BlockSpec((tm, A_WIDTH), lambda i: (i, 0))],
        out_shape=[jax.ShapeDtypeStruct((m, D_MODEL), BF16), jax.ShapeDtypeStruct((m, 2 * A_WIDTH), F32),
                   jax.ShapeDtypeStruct((QKV_SLABS, m, PAIR), F32), jax.ShapeDtypeStruct((m, A_WIDTH), BF16)],
        compiler_params=_params(("parallel",)), name="proj_fwd",
    )(x, g1, w_in_t, ln_g, ln_b, w_s, bz, pavg)


def _pair_masks():
    lane = lax.broadcasted_iota(jnp.int32, (1, PAIR), 1)
    return [lane < HEAD_DIM, lane >= HEAD_DIM]


def _block_rows(idx, dil):
    static = isinstance(idx, int)
    r, n = idx % dil, idx // dil

    def rows_of(block):
        start = r + (dil * Q_BLOCK) * block
        if dil == 1:
            return pl.ds(start if static else pl.multiple_of(start, Q_BLOCK), Q_BLOCK)
        return pl.ds(start, Q_BLOCK, stride=dil)

    prev = rows_of(n - 1) if not static or n > 0 else None
    return rows_of(n), prev


def _attn_fwd(qkv, bias, batch, owns):
    m = qkv.shape[1]
    comb_rows = 256
    nt = len(owns)
    shapes = [g.shape[1:] for g in owns]
    n_steps = batch * N_PAIR
    early, late = list(range(nt // 2)), list(range(nt // 2, nt))

    def body(q_ref, k_ref, v_ref, b_ref, *rest):
        o_ref, l_ref = rest[nt:nt + 2]
        gat_refs = rest[nt + 2:2 * nt + 2]
        scratch = rest[2 * nt + 2:]
        oc_refs, lc_refs = scratch[:N_CFG], scratch[N_CFG:2 * N_CFG]
        step = pl.program_id(0) * N_PAIR + pl.program_id(1)
        gather = _GatherPlan(shapes, gat_refs, *scratch[2 * N_CFG:])

        @pl.when(step == 0)
        def _():
            gather.start(early + late)

        @pl.when(step == n_steps // 2)
        def _():
            gather.forward(early)

        @pl.when(step == n_steps - 2)
        def _():
            gather.forward(late)

        masks = _pair_masks()
        for ci, (_, dil) in enumerate(DILATED):
            nb = SEQ // dil // Q_BLOCK

            def block(trip, ci=ci, dil=dil, nb=nb):
                work = []
                for u in range(FWD_BLOCKS_PER_TRIP):
                    rows, prow = _block_rows(trip * FWD_BLOCKS_PER_TRIP + u, dil)
                    has_prev = nb > 1 and prow is not None
                    q = q_ref[rows, :] * 0.125
                    kc = k_ref[rows, :].astype(BF16)
                    vc = v_ref[rows, :]
                    kp = k_ref[prow, :].astype(BF16) if has_prev else None
                    vp = v_ref[prow, :] if has_prev else None
                    tiles = []
                    for h in range(2):
                        qh = jnp.where(masks[h], q, 0.0).astype(BF16)
                        sc = _dot_nt(qh, kc) + b_ref[ci, h, :, Q_BLOCK:]
                        sp = _dot_nt(qh, kp) + b_ref[ci, h, :, :Q_BLOCK] if has_prev else None
                        tiles.append((sc, sp))
                    work.append((rows, vc, vp, tiles))
                probs = []
                for _, _, _, tiles in work:
                    ps = []
                    for sc, sp in tiles:
                        mx = jnp.max(sc if sp is None else jnp.maximum(sc, sp), axis=1, keepdims=True)
                        pc = jnp.exp(sc - mx).astype(BF16)
                        pp = None if sp is None else jnp.exp(sp - mx).astype(BF16)
                        ps.append((mx, pc, pp))
                    probs.append(ps)
                for (rows, vc, vp, _), ps in zip(work, probs):
                    res = []
                    for h, (_, pc, pp) in enumerate(ps):
                        r = _dot(pc, jnp.where(masks[h], vc, 1.0).astype(BF16))
                        if pp is not None:
                            r = r + _dot(pp, jnp.where(masks[h], vp, 1.0).astype(BF16))
                        res.append(r)
                    num = jnp.where(masks[0], res[0], res[1])
                    den = pltpu.roll(jnp.where(masks[0], res[1], res[0]), HEAD_DIM, 1)
                    oc_refs[ci][rows, :] = num / den
                    lc_refs[ci][rows, :] = jnp.where(masks[0], ps[0][0], ps[1][0]) + jnp.log(den)

            for trip in range(BLOCKS_PER_CFG // FWD_BLOCKS_PER_TRIP):
                block(trip)

        def combine(i, carry):
            rr = pl.ds(pl.multiple_of(i * comb_rows, comb_rows), comb_rows)
            ls = [lc_refs[c][rr, :] for c in range(N_CFG)]
            mx = functools.reduce(jnp.maximum, ls)
            ws = [jnp.exp(l - mx) for l in ls]
            tot = functools.reduce(lambda a, b: a + b, ws)
            o = functools.reduce(lambda a, b: a + b, [ws[c] * oc_refs[c][rr, :] for c in range(N_CFG)]) / tot
            o_ref[rr, :] = o.astype(BF16)
            l_ref[rr, :] = mx + jnp.log(tot)
            return carry

        lax.fori_loop(0, SEQ // comb_rows, combine, 0)

        @pl.when(step == n_steps - 1)
        def _():
            gather.finish(early + late)

    def slab(first):
        return pl.BlockSpec((None, SEQ, PAIR), lambda b, p: (first + p, b, 0))

    nat = pl.BlockSpec((SEQ, PAIR), lambda b, p: (b, p))
    res = pl.pallas_call(
        body, grid=(batch, N_PAIR),
        in_specs=[slab(0), slab(N_PAIR), slab(2 * N_PAIR),
                  pl.BlockSpec((N_CFG, 2, Q_BLOCK, 2 * Q_BLOCK), lambda b, p: (0, p, 0, 0))] + [ANY] * nt,
        out_specs=[nat, nat] + [ANY] * nt,
        out_shape=[jax.ShapeDtypeStruct((m, B_WIDTH), BF16), jax.ShapeDtypeStruct((m, B_WIDTH), F32)]
        + [jax.ShapeDtypeStruct(g.shape, g.dtype) for g in owns],
        scratch_shapes=[pltpu.VMEM((SEQ, PAIR), F32)] * (2 * N_CFG) + _sem_pair(6 * nt),
        input_output_aliases={4 + t: 2 + t for t in range(nt)},
        compiler_params=_params(("arbitrary", "arbitrary")), name="attn_fwd",
    )(qkv, qkv, qkv, bias, *owns)
    return res[0], res[1], list(res[2:])


def _attn_bwd(qkv, dmix, o, lse, bias_t, dproj, batch, parts, smalls):
    m = qkv.shape[1]
    nt, ns = len(parts), len(smalls)
    n_steps = N_PAIR * batch

    def body(q_ref, k_ref, v_ref, do_ref, o_ref, l_ref, b_ref, *rest):
        part_refs = rest[1:nt + 1]
        small_refs = rest[nt + 1:nt + 1 + ns]
        pos = nt + 1 + ns
        dproj_ref, ds_ref = rest[pos:pos + 2]
        recv_refs = rest[pos + 2:pos + 2 + nt]
        sum_refs = rest[pos + 2 + nt:pos + 2 + nt + ns]
        pos += 2 + nt + ns
        dq_acc, dk_acc, dv_acc, d_scr, stage, stage_sems, send_sems, recv_sems = rest[pos:pos + 8]
        allreduce = _SmallAllReducePlan(small_refs, sum_refs, rest[pos + 8:pos + 8 + ns],
                                        rest[pos + 8 + ns:pos + 8 + 2 * ns], *rest[pos + 8 + 2 * ns:])
        pair, seq = pl.program_id(0), pl.program_id(1)
        step = pair * batch + seq
        exchange = _ChipExchangePlan(part_refs, recv_refs, send_sems, recv_sems)

        @pl.when(step == 0)
        def _():
            allreduce.start_sibling()

        @pl.when(step == n_steps // 2)
        def _():
            allreduce.sum_sibling_and_start_chips()

        def stage_copies():
            rows = pl.ds(pl.multiple_of(seq * SEQ, SEQ), SEQ)
            return [pltpu.make_async_copy(
                stage.at[k],
                dproj_ref.at[rows, pl.ds(pl.multiple_of(2 * A_WIDTH + k * B_WIDTH + pair * PAIR, PAIR), PAIR)],
                stage_sems.at[k]) for k in range(3)]

        @pl.when(step == 0)
        def _():
            exchange.start()

        @pl.when(pl.program_id(1) == 0)
        def _():
            ds_ref[...] = jnp.zeros_like(ds_ref)

        dq_acc[...] = jnp.zeros_like(dq_acc)
        dk_acc[...] = jnp.zeros_like(dk_acc)
        dv_acc[...] = jnp.zeros_like(dv_acc)
        d_scr[...] = do_ref[...] * o_ref[...].astype(F32)
        masks = _pair_masks()

        def stack_heads(t):
            return jnp.concatenate([jnp.where(masks[0], t, 0.0), jnp.where(masks[1], t, 0.0)], axis=0).astype(BF16)

        for ci, (_, dil) in enumerate(DILATED):
            nb = SEQ // dil // Q_BLOCK

            def block(trip, carry, ci=ci, dil=dil, nb=nb):
                first = []
                for u in range(BWD_BLOCKS_PER_TRIP):
                    rows, prow = _block_rows(trip * BWD_BLOCKS_PER_TRIP + u, dil)
                    has_prev = nb > 1 and prow is not None
                    if has_prev:
                        kcat = jnp.concatenate([k_ref[prow, :], k_ref[rows, :]], axis=0).astype(BF16)
                        vcat = jnp.concatenate([v_ref[prow, :], v_ref[rows, :]], axis=0).astype(BF16)
                    else:
                        kcat = k_ref[rows, :].astype(BF16)
                        vcat = v_ref[rows, :].astype(BF16)
                    qst = stack_heads(q_ref[rows, :] * 0.125)
                    dost = stack_heads(do_ref[rows, :])
                    lt = l_ref[rows, :].T
                    dt = d_scr[rows, :].T
                    lrow = jnp.concatenate([lt[0:1], lt[HEAD_DIM:HEAD_DIM + 1]], axis=1)
                    drow = jnp.concatenate([jnp.sum(dt[:HEAD_DIM], axis=0, keepdims=True),
                                            jnp.sum(dt[HEAD_DIM:], axis=0, keepdims=True)], axis=1)
                    first.append((has_prev, rows, prow, kcat, qst, dost, lrow, drow,
                                  _dot_nt(kcat, qst), _dot_nt(vcat, dost)))
                second = []
                for has_prev, rows, prow, kcat, qst, dost, lrow, drow, st, dpt in first:
                    keys = slice(0, 2 * Q_BLOCK) if has_prev else slice(Q_BLOCK, 2 * Q_BLOCK)
                    bt = jnp.concatenate([b_ref[ci, 0, keys, :], b_ref[ci, 1, keys, :]], axis=1)
                    pt = jnp.exp(st + bt - lrow)
                    dst = pt * (dpt - drow)
                    ds_ref[ci, 0, keys, :] += dst[:, :Q_BLOCK]
                    ds_ref[ci, 1, keys, :] += dst[:, Q_BLOCK:]
                    second.append((has_prev, rows, prow, kcat, qst, dost, pt.astype(BF16), dst.astype(BF16)))
                for has_prev, rows, prow, kcat, qst, dost, pt_bf, dst_bf in second:
                    dk = _dot(dst_bf, qst)
                    dv = _dot(pt_bf, dost)
                    dq2 = _dot_tn(dst_bf, kcat)
                    dq_acc[rows, :] += jnp.where(masks[0], dq2[:Q_BLOCK], dq2[Q_BLOCK:]) * 0.125
                    if has_prev:
                        dk_acc[prow, :] += dk[:Q_BLOCK]
                        dv_acc[prow, :] += dv[:Q_BLOCK]
                        dk_acc[rows, :] += dk[Q_BLOCK:]
                        dv_acc[rows, :] += dv[Q_BLOCK:]
                    else:
                        dk_acc[rows, :] += dk
                        dv_acc[rows, :] += dv
                return carry

            for trip in range(BLOCKS_PER_CFG // BWD_BLOCKS_PER_TRIP):
                block(trip, 0)

        @pl.when(step > 0)
        def _():
            for cp in stage_copies():
                cp.wait()

        stage[0] = dq_acc[...].astype(BF16)
        stage[1] = dk_acc[...].astype(BF16)
        stage[2] = dv_acc[...].astype(BF16)
        for cp in stage_copies():
            cp.start()

        @pl.when(step == n_steps - 1)
        def _():
            for cp in stage_copies():
                cp.wait()
            exchange.finish()
            allreduce.finish()

    def slab(first):
        return pl.BlockSpec((None, SEQ, PAIR), lambda p, b: (first + p, b, 0))

    nat = pl.BlockSpec((SEQ, PAIR), lambda p, b: (b, p))
    tbl = pl.BlockSpec((N_CFG, 2, 2 * Q_BLOCK, Q_BLOCK), lambda p, b: (0, p, 0, 0))
    acc = pltpu.VMEM((SEQ, PAIR), F32)
    vm = pl.BlockSpec(memory_space=pltpu.VMEM)
    res = pl.pallas_call(
        body, grid=(N_PAIR, batch),
        in_specs=[slab(0), slab(N_PAIR), slab(2 * N_PAIR),
                  pl.BlockSpec((SEQ, PAIR), lambda p, b: (b, A_WIDTH // PAIR + p)), nat, nat, tbl]
        + [ANY] * (nt + 1) + [vm] * ns,
        out_specs=[ANY, tbl] + [ANY] * nt + [vm] * ns,
        out_shape=[jax.ShapeDtypeStruct(dproj.shape, dproj.dtype),
                   jax.ShapeDtypeStruct((N_CFG, B_HEADS, 2 * Q_BLOCK, Q_BLOCK), F32)]
        + [jax.ShapeDtypeStruct((3,) + p.shape[1:], p.dtype) for p in parts]
        + [jax.ShapeDtypeStruct(a.shape, F32) for a in smalls],
        input_output_aliases={7: 0},
        scratch_shapes=[acc, acc, acc, acc, pltpu.VMEM((3, SEQ, PAIR), BF16), pltpu.SemaphoreType.DMA((3,))]
        + _sem_pair(3 * nt) + _SmallAllReducePlan.scratch(smalls),
        compiler_params=_params(("arbitrary", "arbitrary")), name="attn_bwd",
    )(qkv, qkv, qkv, dmix, o, lse, bias_t, dproj, *parts, *smalls)
    return res[0], res[1], list(res[2:2 + nt]), list(res[2 + nt:])


def _rel_bias_grad(ds, buckets_np, grads):
    present = _present_buckets(buckets_np)
    nx = len(grads)
    shapes = [g.shape for g in grads]

    def body(bk_ref, ds_ref, *rest):
        o_ref = rest[nx]
        acc_ref = rest[2 * nx + 1]
        exchange = _SiblingExchangePlan(shapes, rest[:nx], rest[nx + 1:2 * nx + 1], *rest[2 * nx + 2:])
        exchange.start()
        acc_ref[...] = jnp.zeros_like(acc_ref)
        for c in range(N_CFG):
            bk = bk_ref[c]
            for h in range(B_HEADS):
                dsv = ds_ref[c, h]
                for b in present[c]:
                    part = jnp.sum(jnp.where(bk == b, dsv, 0.0), axis=0, keepdims=True)
                    acc_ref[pl.ds(h * NUM_BUCKETS + b, 1), :] += part
        o_ref[...] = jnp.sum(acc_ref[...], axis=1, keepdims=True)
        exchange.finish()

    vm = pl.BlockSpec(memory_space=pltpu.VMEM)
    res = pl.pallas_call(
        body, in_specs=[vm, vm] + [ANY] * nx, out_specs=[vm] + [ANY] * nx,
        out_shape=[jax.ShapeDtypeStruct((B_HEADS * NUM_BUCKETS, 1), F32)]
        + [jax.ShapeDtypeStruct((N_SHARD, s[1] // 2, s[2]), F32) for s in shapes],
        scratch_shapes=[pltpu.VMEM((B_HEADS * NUM_BUCKETS, buckets_np.shape[-1]), F32)] + _sem_pair(nx),
        compiler_params=_params(), name="rel_bias_grad",
    )(jnp.asarray(buckets_np), ds, *grads)
    return res[0], list(res[1:])


def _row_index():
    return lax.broadcasted_iota(jnp.int32, (SEQ, LANE_BLOCK), 0)


def _shift_down(x, k, row):
    return jnp.where(row >= k, pltpu.roll(x, k, 0), 0.0)


def _shift_up(x, k, row):
    return jnp.where(row < SEQ - k, pltpu.roll(x, SEQ - k, 0), 0.0)


def _convgate_fwd(gate, up, conv_w, conv_b, batch):
    m = gate.shape[0]

    def body(g_ref, u_ref, w_ref, b_ref, a_ref):
        g = g_ref[...].astype(F32)
        w = w_ref[...]
        row = _row_index()
        c = b_ref[...] + w[0:1] * _shift_down(g, 2, row) + w[1:2] * _shift_down(g, 1, row) + w[2:3] * g
        a_ref[...] = (_gelu(c) * u_ref[...].astype(F32)).astype(BF16)

    blk = pl.BlockSpec((SEQ, LANE_BLOCK), lambda b, j: (b, j))
    return pl.pallas_call(
        body, grid=(batch, D_FF // LANE_BLOCK),
        in_specs=[blk, blk, pl.BlockSpec((3, LANE_BLOCK), lambda b, j: (0, j)),
                  pl.BlockSpec((1, LANE_BLOCK), lambda b, j: (0, j))],
        out_specs=blk,
        out_shape=jax.ShapeDtypeStruct((m, D_FF), BF16),
        compiler_params=_params(("parallel", "parallel")), name="convgate_fwd",
    )(gate, up, conv_w, conv_b)


def _convgate_bwd(gate, up, dact, conv_w, conv_b, batch):
    m = gate.shape[0]

    def body(g_ref, u_ref, da_ref, w_ref, b_ref, dg_ref, du_ref, dw_ref, db_ref):
        @pl.when(pl.program_id(1) == 0)
        def _():
            dw_ref[...] = jnp.zeros_like(dw_ref)
            db_ref[...] = jnp.zeros_like(db_ref)

        g = g_ref[...].astype(F32)
        w = w_ref[...]
        row = _row_index()
        g1 = _shift_down(g, 1, row)
        g2 = _shift_down(g, 2, row)
        c = b_ref[...] + w[0:1] * g2 + w[1:2] * g1 + w[2:3] * g
        gg, dgg = _gelu_and_grad(c)
        da = da_ref[...].astype(F32)
        du_ref[...] = (da * gg).astype(BF16)
        dc = da * u_ref[...].astype(F32) * dgg
        db_ref[...] += jnp.sum(dc, axis=0, keepdims=True)
        dw_ref[0:1, :] += jnp.sum(dc * g2, axis=0, keepdims=True)
        dw_ref[1:2, :] += jnp.sum(dc * g1, axis=0, keepdims=True)
        dw_ref[2:3, :] += jnp.sum(dc * g, axis=0, keepdims=True)
        dg_ref[...] = (w[2:3] * dc + w[1:2] * _shift_up(dc, 1, row) + w[0:1] * _shift_up(dc, 2, row)).astype(BF16)

    blk = pl.BlockSpec((SEQ, LANE_BLOCK), lambda j, b: (b, j))
    wspec = pl.BlockSpec((3, LANE_BLOCK), lambda j, b: (0, j))
    bspec = pl.BlockSpec((1, LANE_BLOCK), lambda j, b: (0, j))
    return pl.pallas_call(
        body, grid=(D_FF // LANE_BLOCK, batch),
        in_specs=[blk, blk, blk, wspec, bspec],
        out_specs=[blk, blk, wspec, bspec],
        out_shape=[jax.ShapeDtypeStruct((m, D_FF), BF16), jax.ShapeDtypeStruct((m, D_FF), BF16),
                   jax.ShapeDtypeStruct((3, D_FF), F32), jax.ShapeDtypeStruct((1, D_FF), F32)],
        compiler_params=_params(("parallel", "arbitrary")), name="convgate_bwd",
    )(gate, up, dact, conv_w, conv_b)


def _gather_weights(shards, conv_w_shard, rel_bias, buckets_np):
    nt = len(shards)
    shapes = [sh.shape for sh in shards]
    ts = list(range(nt))
    tables = _bias_tables_body(buckets_np)

    def body(*refs):
        shard_refs = refs[:nt]
        cw_ref, rb_ref, bk_ref = refs[nt:nt + 3]
        out_refs = refs[nt + 3:2 * nt + 3]
        cw_out, bias_ref, bias_t_ref = refs[2 * nt + 3:2 * nt + 6]
        scratch = refs[2 * nt + 6:]
        f32_refs, bf16_refs = scratch[:nt], scratch[nt:2 * nt]
        load_sems, store_sems, send_sems, recv_sems, cw_send, cw_recv = scratch[2 * nt:]
        plan = _RelayGatherPlan(shapes[:1], bf16_refs[:1], out_refs[:1], send_sems, recv_sems)
        x, y, c, chips = _mesh_pos()
        loads = [pltpu.make_async_copy(shard_refs[t], f32_refs[t], load_sems.at[t]) for t in ts]
        stores = [pltpu.make_async_copy(bf16_refs[t], out_refs[t].at[2 * x + y], store_sems.at[t]) for t in ts]
        stores.append(pltpu.make_async_copy(cw_ref, cw_out.at[2 * x + y], store_sems.at[nt]))

        def cw_copy(j, src, dst, chip):
            return pltpu.make_async_remote_copy(src_ref=src, dst_ref=dst, send_sem=cw_send.at[j],
                                                recv_sem=cw_recv.at[j], device_id=(*chip, c), device_id_type=MESH)

        def to_bf16(t):
            loads[t].wait()
            bf16_refs[t][...] = f32_refs[t][...].astype(BF16)
            stores[t].start()

        for cp in loads:
            cp.start()
        to_bf16(0)
        plan.start([0])
        cw_sends = [cw_copy(j, cw_ref, cw_out.at[2 * x + y], chip) for j, chip in enumerate(chips)]
        for cp in cw_sends + stores[nt:]:
            cp.start()
        for t in ts[1:]:
            to_bf16(t)
        plan.relay([0])
        tables(rb_ref, bk_ref, bias_ref, bias_t_ref)
        plan.forward([0])
        for j, chip in enumerate(chips):
            dst = cw_out.at[2 * chip[0] + chip[1]]
            cw_copy(j, dst, dst, chip).wait_recv()
        plan.finish([0])
        for cp in cw_sends:
            cp.wait_send()
        for cp in stores:
            cp.wait()

    out_shape = [jax.ShapeDtypeStruct((N_SHARD,) + sh.shape, BF16) for sh in shards]
    out_shape.append(jax.ShapeDtypeStruct((N_SHARD,) + conv_w_shard.shape, conv_w_shard.dtype))
    out_shape += [jax.ShapeDtypeStruct((N_CFG, B_HEADS, Q_BLOCK, 2 * Q_BLOCK), F32),
                  jax.ShapeDtypeStruct((N_CFG, B_HEADS, 2 * Q_BLOCK, Q_BLOCK), F32)]
    vm = pl.BlockSpec(memory_space=pltpu.VMEM)
    res = pl.pallas_call(
        body, in_specs=[ANY] * (nt + 1) + [pl.BlockSpec(memory_space=pltpu.SMEM), vm],
        out_specs=[ANY] * (nt + 1) + [vm, vm], out_shape=out_shape,
        scratch_shapes=[pltpu.VMEM(sh.shape, F32) for sh in shards] + [pltpu.VMEM(sh.shape, BF16) for sh in shards]
        + [pltpu.SemaphoreType.DMA((nt,)), pltpu.SemaphoreType.DMA((nt + 1,))] + _sem_pair(6) + _sem_pair(3),
        compiler_params=pltpu.CompilerParams(has_side_effects=True, vmem_limit_bytes=VMEM_LIMIT),
        name="gather_weights",
    )(*shards, conv_w_shard, rel_bias.T, jnp.asarray(buckets_np))
    return list(res[:nt + 1]), res[nt + 1], res[nt + 2]


def _turn(t, u, s, last):
    return jnp.where(t == u, s, jnp.where(t > u, last, 0))


def _add_halves(gs, recvs, c_idx):
    n = len(gs)
    _, rows2, cols = gs[0].shape
    rows = rows2 // 2
    assert all(g.shape == gs[0].shape for g in gs)

    def body(c_ref, *refs):
        t = pl.program_id(0)
        for u in range(n):
            @pl.when(t == u)
            def _(u=u):
                refs[2 * n + u][...] = (refs[u][...] + refs[n + u][...]).astype(BF16)

    def own(u):
        return pl.BlockSpec((None, None, rows, cols), lambda t, s, c: (_turn(t, u, s, N_SHARD - 1), c[0], 0, 0))

    def plain(u):
        return pl.BlockSpec((None, rows, cols), lambda t, s, c: (_turn(t, u, s, N_SHARD - 1), 0, 0))

    return pl.pallas_call(
        body,
        grid_spec=pltpu.PrefetchScalarGridSpec(
            num_scalar_prefetch=1, grid=(n, N_SHARD),
            in_specs=[own(u) for u in range(n)] + [plain(u) for u in range(n)],
            out_specs=[plain(u) for u in range(n)]),
        out_shape=[jax.ShapeDtypeStruct((N_SHARD, rows, cols), BF16)] * n,
        compiler_params=_params(("arbitrary", "arbitrary")), name="rs_add_halves",
    )(c_idx, *[g.reshape(N_SHARD, 2, rows, cols) for g in gs], *recvs)


def _add_chips(parts, recvs, s_idx, c_idx):
    n = len(parts)
    _, rows, cols = parts[0].shape
    assert all(p.shape == parts[0].shape for p in parts)

    def body(idx_ref, *refs):
        t = pl.program_id(0)
        for u in range(n):
            @pl.when(t == u)
            def _(u=u):
                acc = refs[u][...].astype(F32)
                for j in range(3):
                    acc = acc + refs[n + u][j].astype(F32)
                refs[2 * n + u][...] = acc

    res = pl.pallas_call(
        body,
        grid_spec=pltpu.PrefetchScalarGridSpec(
            num_scalar_prefetch=1, grid=(n,),
            in_specs=[pl.BlockSpec((None, rows, cols), lambda t, idx: (idx[0], 0, 0))] * n
            + [pl.BlockSpec((3, rows, cols), lambda t, idx: (0, 0, 0))] * n,
            out_specs=[pl.BlockSpec((None, rows, cols), lambda t, idx: (idx[1], 0, 0))] * n),
        out_shape=[jax.ShapeDtypeStruct((2, rows, cols), F32)] * n,
        compiler_params=_params(("arbitrary",)), name="rs_add_chips",
    )(jnp.concatenate([s_idx, c_idx]), *parts, *recvs)
    return [r.reshape(2 * rows, cols) for r in res]


def _finish_reductions(fulls, arrays):
    nt, n = len(fulls), len(arrays)

    def body(*refs):
        in_refs = refs[nt:nt + n]
        full_refs, out_refs = refs[nt + n:2 * nt + n], refs[2 * nt + n:2 * nt + 2 * n]
        pos = 2 * nt + 2 * n
        share_send, share_recv = refs[pos + 2 * n:pos + 2 * n + 2]
        allreduce = _SmallAllReducePlan(in_refs, out_refs, refs[pos:pos + n], refs[pos + n:pos + 2 * n],
                                        *refs[pos + 2 * n + 2:])
        x, y, c, _ = _mesh_pos()

        def half(t, which):
            rows = fulls[t].shape[0] // 2
            return full_refs[t].at[pl.ds(which * rows, rows), :]

        def share(t, which):
            return pltpu.make_async_remote_copy(
                src_ref=half(t, which), dst_ref=half(t, which), send_sem=share_send.at[t],
                recv_sem=share_recv.at[t], device_id=(x, y, 1 - c), device_id_type=MESH)

        for t in range(nt):
            share(t, c).start()
        allreduce.start_sibling()
        allreduce.sum_sibling_and_start_chips()
        allreduce.finish()
        for t in range(nt):
            share(t, 1 - c).wait_recv()
        for t in range(nt):
            share(t, c).wait_send()

    vm = pl.BlockSpec(memory_space=pltpu.VMEM)
    res = pl.pallas_call(
        body, in_specs=[ANY] * nt + [vm] * n, out_specs=[ANY] * nt + [vm] * n,
        out_shape=[jax.ShapeDtypeStruct(f.shape, f.dtype) for f in fulls]
        + [jax.ShapeDtypeStruct(a.shape, F32) for a in arrays],
        input_output_aliases={t: t for t in range(nt)},
        scratch_shapes=[pltpu.VMEM(a.shape, F32) for a in arrays] + [pltpu.VMEM((3,) + a.shape, F32) for a in arrays]
        + _sem_pair(nt) + _sem_pair(4 * n),
        compiler_params=pltpu.CompilerParams(has_side_effects=True),
        name="finish_reductions",
    )(*fulls, *arrays)
    return list(res[:nt]), list(res[nt:])


def _from_col_shards(g):
    n, rows, cols = g.shape
    return g.transpose(1, 0, 2).reshape(rows, n * cols)


def _train_step(x, tgt, g1, g2, g3, g4, shards, ln_g, ln_b, w_s, b_s, rel_bias, conv_w_shard, conv_b, batch,
                s_idx, c_idx):
    buckets = _bucket_tables()
    bz = jnp.repeat(b_s.T, HEAD_DIM, axis=1)
    w_st = jnp.swapaxes(w_s, 1, 2)

    def shard_major(g):
        return g.reshape(N_SHARD, g.shape[0] // N_SHARD, D_MODEL)

    names = ["w_in", "w_out", "w_gate", "w_up", "w_down"]
    (g_in, g_out, g_gate, g_up, g_down, g_convw), bias, bias_t = _gather_weights(
        [shards[n] for n in names], conv_w_shard, rel_bias, buckets)
    w_in_t = g_in.reshape(IN_COLS, D_MODEL)
    conv_w = _from_col_shards(g_convw.reshape(N_SHARD, 3, SHARD_FF))

    h1, uv, qkv, a = _proj_fwd(x, g1, w_in_t, ln_g, ln_b, w_s, bz)
    o_bf, lse, (g_out, g_gate, g_up) = _attn_fwd(qkv, bias, batch, [g_out, g_gate, g_up])
    w_out = g_out.reshape(D_MODEL, D_MODEL)
    w_gate_t = g_gate.reshape(D_FF, D_MODEL)
    w_up_t = g_up.reshape(D_FF, D_MODEL)
    (y1, x1, h2), _ = _fused_rows(
        "out_proj_mid_fwd", 512,
        [(a, w_out, "nn", slice(0, A_WIDTH)), (o_bf, w_out, "nn", slice(A_WIDTH, D_MODEL))],
        [x], [g2, g3], _mid_fwd_rows, [F32, F32, BF16], [])
    gate, up, g_down = _mm_pair_nt(h2, w_gate_t, w_up_t, g_down, tm=1024, tn=1408, out_dtype=BF16,
                                   name="mm_gate_up")
    w_down = g_down.reshape(D_FF, D_MODEL)
    act = _convgate_fwd(gate, up, conv_w, conv_b, batch)
    (dx2, dy2, dg4, loss), _ = _fused_rows(
        "down_proj_loss_head", 512, [(act, w_down, "nn", None)], [x1, tgt], [g4], _loss_head_rows,
        [F32, BF16], [(1, D_MODEL), (1, 128)])

    dact = _mm(dy2, w_down, dims="nt", tm=1024, tn=1408, tk=1024, out_dtype=BF16, name="mm_dact")
    dw_down = _mm(act, dy2, dims="tn", tm=1408, tn=1024, tk=1024, out_dtype=F32, name="mm_dw_down")
    dgate, dup, dconv_w, dconv_b = _convgate_bwd(gate, up, dact, conv_w, conv_b, batch)
    (dx1, dy1, dg2, dg3), _ = _fused_rows(
        "dh2_mid_bwd", 256, [(dgate, w_gate_t, "nn", None), (dup, w_up_t, "nn", None)],
        [x1, y1, dx2], [g2, g3], _mid_bwd_rows, [F32, BF16], [(1, D_MODEL), (1, D_MODEL)])
    dw_gate_t, dw_up_t = _mm_pair_tn(dgate, dup, h2, tm=1408, tk=1024, name="mm_dw_gate_up")
    done = [shard_major(g) for g in (dw_down, dw_gate_t, dw_up_t)]
    (dmix, dw_out), recv_a = _out_proj_bwd(a, o_bf, dy1, w_out, done[:2])
    done.append(shard_major(dw_out))
    (dproj, dln_g, dln_b, dw_s, dbz), recv_b = _gate_bwd(uv, dmix, ln_g, ln_b, w_s, w_st, bz, done[2:])
    recv_a += recv_b
    parts = _add_halves(done[:3], recv_a[:3], c_idx) + _add_halves(done[3:], recv_a[3:], c_idx)
    early = dict(loss=loss, norm_mix_post=dg2, norm_ffn_pre=dg3, norm_ffn_post=dg4, ln_v_gain=dln_g,
                 ln_v_bias=dln_b, spatial_w=dw_s, spatial_b=dbz, conv_w=dconv_w, conv_b=dconv_b)
    dproj, ds, recv, early_sums = _attn_bwd(qkv, dmix, o_bf, lse, bias_t, dproj, batch, parts, list(early.values()))
    fulls = _add_chips(parts[:3], recv[:3], s_idx, c_idx) + _add_chips(parts[3:], recv[3:], s_idx, c_idx)
    dw_in_t = _mm(dproj, h1, dims="tn", tm=1408, tn=1024, tk=1024, out_dtype=F32, name="mm_dw_in")
    last = [shard_major(dw_in_t)]
    drel, recv_in_a = _rel_bias_grad(ds, np.ascontiguousarray(np.swapaxes(buckets, 1, 2)), last)
    part_in = _add_halves(last, recv_in_a, c_idx)
    (dx0, dg1), recv_in = _fused_rows(
        "dh1_in_bwd", 512, [(dproj, w_in_t, "nn", None)], [x, dx1], [g1], _in_bwd_rows,
        [F32], [(1, D_MODEL)], exchange=part_in)
    fulls += _add_chips(part_in, recv_in, s_idx, c_idx)
    half_reduced = dict(zip(["w_down", "w_gate", "w_up", "w_out", "w_in"], fulls))

    return dx0, dict(zip(early, early_sums)), dict(norm_mix_pre=dg1, rel_bias=drel), half_reduced


def _adamw_update(w, g, m, v):
    nm = ADAM_B1 * m + (1.0 - ADAM_B1) * g
    nv = ADAM_B2 * v + (1.0 - ADAM_B2) * (g * g)
    m_hat = nm / (1.0 - ADAM_B1 ** ADAM_STEP)
    v_hat = nv / (1.0 - ADAM_B2 ** ADAM_STEP)
    return -ADAM_LR * (m_hat / (jnp.sqrt(v_hat) + ADAM_EPS) + ADAM_WD * w), nm, nv


def _adamw(w, g, m, v, name):
    rows, cols = w.shape
    tr = next(cand for cand in (352, 256, 128) if rows % cand == 0)

    def body(w_ref, g_ref, m_ref, v_ref, go_ref, d_ref, nm_ref, nv_ref):
        gv = g_ref[...]
        go_ref[...] = gv
        d_ref[...], nm_ref[...], nv_ref[...] = _adamw_update(w_ref[...], gv, m_ref[...], v_ref[...])

    spec = pl.BlockSpec((tr, cols), lambda i: (i, 0))
    sds = jax.ShapeDtypeStruct((rows, cols), F32)
    return pl.pallas_call(
        body, grid=(rows // tr,), in_specs=[spec] * 4, out_specs=[spec] * 4, out_shape=[sds] * 4,
        compiler_params=_params(("parallel",)), name=name,
    )(w, g, m, v)


def _adamw_small(ws, gs, ms, vs):
    n = len(ws)

    def body(*refs):
        w_refs, g_refs, m_refs, v_refs = refs[:n], refs[n:2 * n], refs[2 * n:3 * n], refs[3 * n:4 * n]
        d_refs, nm_refs, nv_refs = refs[4 * n:5 * n], refs[5 * n:6 * n], refs[6 * n:7 * n]
        for t in range(n):
            d_refs[t][...], nm_refs[t][...], nv_refs[t][...] = _adamw_update(
                w_refs[t][...], g_refs[t][...], m_refs[t][...], v_refs[t][...])

    vm = pl.BlockSpec(memory_space=pltpu.VMEM)
    sds = [jax.ShapeDtypeStruct(w.shape, F32) for w in ws]
    res = pl.pallas_call(
        body, in_specs=[vm] * (4 * n), out_specs=[vm] * (3 * n), out_shape=sds * 3,
        compiler_params=_params(), name="adamw_small",
    )(*ws, *gs, *ms, *vs)
    return res[:n], res[n:2 * n], res[2 * n:]


SMALL = ["norm_mix_pre", "norm_mix_post", "norm_ffn_pre", "norm_ffn_post", "ln_v_gain", "ln_v_bias",
         "spatial_w", "spatial_b", "rel_bias", "conv_b"]
LARGE = ["w_in", "w_gate", "w_up", "w_down", "w_out"]
TRANSPOSED = ("w_in", "w_gate", "w_up")
ORDER = ["norm_mix_pre", "norm_mix_post", "norm_ffn_pre", "norm_ffn_post", "w_in", "ln_v_gain", "ln_v_bias",
         "spatial_w", "spatial_b", "rel_bias", "w_out", "w_gate", "w_up", "conv_w", "conv_b", "w_down"]


def kernel(x, norm_mix_pre, norm_mix_post, norm_ffn_pre, norm_ffn_post, w_in, ln_v_gain, ln_v_bias, spatial_w, spatial_b, rel_bias, w_out, w_gate, w_up, conv_w, conv_b, w_down, loss_target, m_norm_mix_pre, m_norm_mix_post, m_norm_ffn_pre, m_norm_ffn_post, m_w_in, m_ln_v_gain, m_ln_v_bias, m_spatial_w, m_spatial_b, m_rel_bias, m_w_out, m_w_gate, m_w_up, m_conv_w, m_conv_b, m_w_down, v_norm_mix_pre, v_norm_mix_post, v_norm_ffn_pre, v_norm_ffn_post, v_w_in, v_ln_v_gain, v_ln_v_bias, v_spatial_w, v_spatial_b, v_rel_bias, v_w_out, v_w_gate, v_w_up, v_conv_w, v_conv_b, v_w_down):
    params = dict(norm_mix_pre=norm_mix_pre, norm_mix_post=norm_mix_post, norm_ffn_pre=norm_ffn_pre,
                  norm_ffn_post=norm_ffn_post, w_in=w_in, ln_v_gain=ln_v_gain, ln_v_bias=ln_v_bias,
                  spatial_w=spatial_w, spatial_b=spatial_b, rel_bias=rel_bias, w_out=w_out, w_gate=w_gate,
                  w_up=w_up, conv_w=conv_w, conv_b=conv_b, w_down=w_down)
    mom = dict(norm_mix_pre=m_norm_mix_pre, norm_mix_post=m_norm_mix_post, norm_ffn_pre=m_norm_ffn_pre,
               norm_ffn_post=m_norm_ffn_post, w_in=m_w_in, ln_v_gain=m_ln_v_gain, ln_v_bias=m_ln_v_bias,
               spatial_w=m_spatial_w, spatial_b=m_spatial_b, rel_bias=m_rel_bias, w_out=m_w_out, w_gate=m_w_gate,
               w_up=m_w_up, conv_w=m_conv_w, conv_b=m_conv_b, w_down=m_w_down)
    var = dict(norm_mix_pre=v_norm_mix_pre, norm_mix_post=v_norm_mix_post, norm_ffn_pre=v_norm_ffn_pre,
               norm_ffn_post=v_norm_ffn_post, w_in=v_w_in, ln_v_gain=v_ln_v_gain, ln_v_bias=v_ln_v_bias,
               spatial_w=v_spatial_w, spatial_b=v_spatial_b, rel_bias=v_rel_bias, w_out=v_w_out, w_gate=v_w_gate,
               w_up=v_w_up, conv_w=v_conv_w, conv_b=v_conv_b, w_down=v_w_down)

    batch = x.shape[0]
    xi, yi, ci = lax.axis_index("x"), lax.axis_index("y"), lax.axis_index("c")
    s_idx = (2 * xi + yi).astype(jnp.int32).reshape(1)
    c_idx = ci.astype(jnp.int32).reshape(1)

    def local(a, n):
        return jnp.swapaxes(a[0], 0, 1) if n in TRANSPOSED else a[0]

    shards = {n: local(params[n], n) for n in LARGE}
    dx0, total, partial, half_reduced = _train_step(
        x.reshape(batch * SEQ, D_MODEL), loss_target.reshape(batch * SEQ, D_MODEL),
        norm_mix_pre, norm_mix_post, norm_ffn_pre, norm_ffn_post, shards,
        ln_v_gain.reshape(1, A_WIDTH), ln_v_bias.reshape(1, A_WIDTH), spatial_w[0], spatial_b[0], rel_bias,
        jnp.swapaxes(conv_w, 0, 1), conv_b, batch, s_idx, c_idx)
    grad_x = dx0.reshape(batch, SEQ, D_MODEL)

    names = list(partial)
    fulls, sums = _finish_reductions([half_reduced[n] for n in LARGE], [partial[n] for n in names])
    reduced = dict(zip(LARGE, fulls))
    total.update(zip(names, sums))
    loss = total["loss"][0, 0]
    total["spatial_b"] = total["spatial_b"][:, ::HEAD_DIM].T
    total["rel_bias"] = total["rel_bias"].reshape(B_HEADS, NUM_BUCKETS)
    total["conv_w"] = lax.dynamic_slice_in_dim(total["conv_w"], s_idx[0] * SHARD_FF, SHARD_FF, axis=1)
    small_names = SMALL + ["conv_w"]

    def small(a, n):
        return jnp.swapaxes(a, 0, 1) if n in ("rel_bias", "conv_w") else a

    for n in small_names:
        reduced[n] = total[n].reshape(small(params[n], n).shape)

    out_g, out_d, out_m, out_v = {}, {}, {}, {}
    for n in LARGE:
        res = _adamw(local(params[n], n), reduced[n], local(mom[n], n), local(var[n], n), name=f"adamw_{n}")
        if n in TRANSPOSED:
            res = [jnp.swapaxes(r, 0, 1) for r in res]
        out_g[n], out_d[n], out_m[n], out_v[n] = [r[None] for r in res]
    d, nm, nv = _adamw_small([small(params[n], n) for n in small_names], [reduced[n] for n in small_names],
                             [small(mom[n], n) for n in small_names], [small(var[n], n) for n in small_names])
    for n, dd, mm, vv in zip(small_names, d, nm, nv):
        out_g[n], out_d[n], out_m[n], out_v[n] = [small(r, n) for r in (reduced[n], dd, mm, vv)]

    return (loss, grad_x, *[out_g[n] for n in ORDER], *[out_d[n] for n in ORDER],
            *[out_m[n] for n in ORDER], *[out_v[n] for n in ORDER])
```

```python
import functools
import math

import numpy as np
import jax
import jax.numpy as jnp
from jax import lax
from jax.experimental import pallas as pl
from jax.experimental.pallas import tpu as pltpu

F32 = jnp.float32
BF16 = jnp.bfloat16
MESH = pl.DeviceIdType.MESH

D_MODEL = 1024
SEQ = 2048
HEAD_DIM = 64
A_GROUPS = 4
A_WIDTH = 256
B_HEADS = 12
B_WIDTH = 768
CHUNK = 128
DILATED = ((128, 1), (512, 4), (2048, 16))
NUM_BUCKETS = 32
MAX_DISTANCE = 2048
D_FF = 2816
IN_COLS = 2816
NORM_EPS = 1e-6
NEG_INF = -1e30
N_SHARD = 4
SHARD_FF = D_FF // N_SHARD
LANE_BLOCK = 256
VMEM_LIMIT = 56 * 1024 * 1024

ADAM_LR = 0.001
ADAM_B1 = 0.9
ADAM_B2 = 0.999
ADAM_EPS = 1e-08
ADAM_WD = 0.01
ADAM_STEP = 10

GELU_C = math.sqrt(2.0 / math.pi)
GELU_A = 0.044715

ANY = pl.BlockSpec(memory_space=pl.ANY)


def _params(sem=None):
    return pltpu.CompilerParams(dimension_semantics=sem, vmem_limit_bytes=VMEM_LIMIT)


def _dot(a, b, precision=None):
    return jnp.dot(a, b, preferred_element_type=F32, precision=precision)


def _dot_nt(a, b, precision=None):
    return lax.dot_general(a, b, (((1,), (1,)), ((), ())), preferred_element_type=F32, precision=precision)


def _dot_tn(a, b):
    return lax.dot_general(a, b, (((0,), (0,)), ((), ())), preferred_element_type=F32)


def _gelu(x):
    t = jnp.tanh(x * (GELU_C + (GELU_C * GELU_A) * (x * x)))
    return (0.5 * x) * (1.0 + t)


def _gelu_and_grad(x):
    x2 = x * x
    u = 1.0 + jnp.tanh(x * (GELU_C + (GELU_C * GELU_A) * x2))
    hx = 0.5 * x
    dg = u * (0.5 + hx * (2.0 - u) * (GELU_C + (3.0 * GELU_C * GELU_A) * x2))
    return hx * u, dg


def _mesh_pos():
    x, y, c = lax.axis_index("x"), lax.axis_index("y"), lax.axis_index("c")
    chips = [(1 - x, y), (x, 1 - y), (1 - x, 1 - y)]
    return x, y, c, chips


class _GatherPlan:
    def __init__(self, shapes, out_refs, send_sems, recv_sems):
        self.shapes, self.out_refs = shapes, out_refs
        self.send_sems, self.recv_sems = send_sems, recv_sems
        self.x, self.y, self.c, self.chips = _mesh_pos()
        self.sib = (self.x, self.y, 1 - self.c)

    def _half(self, t, chip, which):
        rows = self.shapes[t][0] // 2
        return self.out_refs[t].at[2 * chip[0] + chip[1], pl.ds(which * rows, rows), :]

    def _copy(self, k, src, dst, to):
        return pltpu.make_async_remote_copy(src_ref=src, dst_ref=dst, send_sem=self.send_sems.at[k],
                                            recv_sem=self.recv_sems.at[k], device_id=to, device_id_type=MESH)

    def _sends(self, t):
        own = self._half(t, (self.x, self.y), self.c)
        return [self._copy(6 * t + j, own, own, (*chip, self.c)) for j, chip in enumerate(self.chips)]

    def _forwards(self, t):
        return [self._copy(6 * t + 3 + j, self._half(t, chip, self.c), self._half(t, chip, self.c), self.sib)
                for j, chip in enumerate(self.chips)]

    def start(self, ts):
        for t in ts:
            for cp in self._sends(t):
                cp.start()

    def forward(self, ts):
        for t in ts:
            for j, chip in enumerate(self.chips):
                landed = self._half(t, chip, self.c)
                self._copy(6 * t + j, landed, landed, (*chip, self.c)).wait_recv()
            for cp in self._forwards(t):
                cp.start()

    def finish(self, ts):
        for t in ts:
            for j, chip in enumerate(self.chips):
                other = self._half(t, chip, 1 - self.c)
                self._copy(6 * t + 3 + j, other, other, self.sib).wait_recv()
        for t in ts:
            for cp in self._sends(t) + self._forwards(t):
                cp.wait_send()


class _RelayGatherPlan:
    def __init__(self, shapes, shard_refs, out_refs, send_sems, recv_sems):
        self.shapes, self.shard_refs, self.out_refs = shapes, shard_refs, out_refs
        self.send_sems, self.recv_sems = send_sems, recv_sems
        x, y, c, self.chips = _mesh_pos()
        self.me, self.c, self.sib = (x, y), c, (x, y, 1 - c)
        self.first = (x + c - 2 * x * c, y + (1 - c) - 2 * y * (1 - c))
        self.second = (x + (1 - c) - 2 * x * (1 - c), y + c - 2 * y * c)
        self.diag = (1 - x, 1 - y)

    def _half(self, t, chip, which):
        rows = self.shapes[t][0] // 2
        return self.out_refs[t].at[2 * chip[0] + chip[1], pl.ds(which * rows, rows), :]

    def _copy(self, k, src, dst, to):
        return pltpu.make_async_remote_copy(src_ref=src, dst_ref=dst, send_sem=self.send_sems.at[k],
                                            recv_sem=self.recv_sems.at[k], device_id=to, device_id_type=MESH)

    def _own(self, t):
        if self.shard_refs is None:
            return self._half(t, self.me, self.c)
        rows = self.shapes[t][0] // 2
        return self.shard_refs[t].at[pl.ds(self.c * rows, rows), :]

    def _step1(self, t):
        return self._copy(6 * t, self._own(t), self._half(t, self.me, self.c), (*self.first, self.c))

    def _step2(self, t):
        landed = self._half(t, self.first, self.c)
        return [self._copy(6 * t + 1, self._own(t), self._half(t, self.me, self.c), (*self.second, self.c)),
                self._copy(6 * t + 2, landed, landed, (*self.second, self.c))]

    def _forwards(self, t):
        return [self._copy(6 * t + 3 + j, self._half(t, chip, self.c), self._half(t, chip, self.c), self.sib)
                for j, chip in enumerate(self.chips)]

    def start(self, ts):
        for t in ts:
            self._step1(t).start()
            self._step2(t)[0].start()

    def relay(self, ts):
        for t in ts:
            landed = self._half(t, self.first, self.c)
            self._copy(6 * t, landed, landed, self.sib).wait_recv()
            self._step2(t)[1].start()

    def forward(self, ts):
        for t in ts:
            for k, chip in ((1, self.second), (2, self.diag)):
                landed = self._half(t, chip, self.c)
                self._copy(6 * t + k, landed, landed, self.sib).wait_recv()
            for cp in self._forwards(t):
                cp.start()

    def finish(self, ts):
        for t in ts:
            for j, chip in enumerate(self.chips):
                other = self._half(t, chip, 1 - self.c)
                self._copy(6 * t + 3 + j, other, other, self.sib).wait_recv()
        for t in ts:
            for cp in [self._step1(t)] + self._step2(t) + self._forwards(t):
                cp.wait_send()


class _SiblingExchangePlan:
    def __init__(self, shapes, grad_refs, out_refs, send_sems, recv_sems):
        self.shapes, self.grad_refs, self.out_refs = shapes, grad_refs, out_refs
        self.send_sems, self.recv_sems = send_sems, recv_sems
        self.x, self.y, self.c, _ = _mesh_pos()

    def _copies(self):
        out = []
        for t, (g, o) in enumerate(zip(self.grad_refs, self.out_refs)):
            rows = self.shapes[t][1] // 2
            out.append(pltpu.make_async_remote_copy(
                src_ref=g.at[:, pl.ds((1 - self.c) * rows, rows), :], dst_ref=o, send_sem=self.send_sems.at[t],
                recv_sem=self.recv_sems.at[t], device_id=(self.x, self.y, 1 - self.c), device_id_type=MESH))
        return out

    def start(self):
        for cp in self._copies():
            cp.start()

    def finish(self):
        for cp in self._copies():
            cp.wait()


class _ChipExchangePlan:
    def __init__(self, part_refs, out_refs, send_sems, recv_sems):
        self.part_refs, self.out_refs, self.send_sems, self.recv_sems = part_refs, out_refs, send_sems, recv_sems
        _, _, self.c, self.chips = _mesh_pos()

    def _copies(self):
        return [pltpu.make_async_remote_copy(
            src_ref=p.at[2 * chip[0] + chip[1]], dst_ref=o.at[j], send_sem=self.send_sems.at[3 * t + j],
            recv_sem=self.recv_sems.at[3 * t + j], device_id=(*chip, self.c), device_id_type=MESH)
            for t, (p, o) in enumerate(zip(self.part_refs, self.out_refs)) for j, chip in enumerate(self.chips)]

    def start(self):
        for cp in self._copies():
            cp.start()

    def finish(self):
        for cp in self._copies():
            cp.wait()


class _SmallAllReducePlan:
    def __init__(self, in_refs, out_refs, sib_refs, chip_refs, send_sems, recv_sems):
        self.in_refs, self.out_refs, self.sib_refs, self.chip_refs = in_refs, out_refs, sib_refs, chip_refs
        self.send_sems, self.recv_sems = send_sems, recv_sems
        self.n = len(in_refs)
        self.x, self.y, self.c, self.chips = _mesh_pos()

    def _copy(self, k, src, dst, to):
        return pltpu.make_async_remote_copy(src_ref=src, dst_ref=dst, send_sem=self.send_sems.at[k],
                                            recv_sem=self.recv_sems.at[k], device_id=to, device_id_type=MESH)

    def _first(self):
        return [self._copy(t, self.in_refs[t], self.sib_refs[t], (self.x, self.y, 1 - self.c)) for t in range(self.n)]

    def _second(self):
        return [self._copy(self.n + 3 * t + j, self.out_refs[t], self.chip_refs[t].at[j], (*chip, self.c))
                for t in range(self.n) for j, chip in enumerate(self.chips)]

    def start_sibling(self):
        for cp in self._first():
            cp.start()

    def sum_sibling_and_start_chips(self):
        for cp in self._first():
            cp.wait()
        for t in range(self.n):
            self.out_refs[t][...] = self.in_refs[t][...] + self.sib_refs[t][...]
        for cp in self._second():
            cp.start()

    def finish(self):
        for cp in self._second():
            cp.wait()
        for t in range(self.n):
            self.out_refs[t][...] = ((self.out_refs[t][...] + self.chip_refs[t][0])
                                     + (self.chip_refs[t][1] + self.chip_refs[t][2]))

    @staticmethod
    def scratch(arrays):
        return ([pltpu.VMEM(a.shape, F32) for a in arrays] + [pltpu.VMEM((3,) + a.shape, F32) for a in arrays]
                + _sem_pair(4 * len(arrays)))


def _sem_pair(n):
    return [pltpu.SemaphoreType.DMA((n,)), pltpu.SemaphoreType.DMA((n,))]


def _mm(a, b, *, dims, tm, tn, tk, out_dtype, name):
    if dims == "nn":
        m, k = a.shape
        n = b.shape[1]
        a_spec = pl.BlockSpec((tm, tk), lambda i, j, kk: (i, kk))
        b_spec = pl.BlockSpec((tk, tn), lambda i, j, kk: (kk, j))
        dot = _dot
    elif dims == "nt":
        m, k = a.shape
        n = b.shape[0]
        a_spec = pl.BlockSpec((tm, tk), lambda i, j, kk: (i, kk))
        b_spec = pl.BlockSpec((tn, tk), lambda i, j, kk: (j, kk))
        dot = _dot_nt
    else:
        k, m = a.shape
        n = b.shape[1]
        a_spec = pl.BlockSpec((tk, tm), lambda i, j, kk: (kk, i))
        b_spec = pl.BlockSpec((tk, tn), lambda i, j, kk: (kk, j))
        dot = _dot_tn
    assert m % tm == 0 and n % tn == 0 and k % tk == 0, (name, m, n, k)
    grid = (m // tm, n // tn, k // tk)
    nk = grid[2]
    assert nk == 1 or out_dtype == F32, name

    def body(a_ref, b_ref, o_ref):
        prod = dot(a_ref[...].astype(BF16), b_ref[...].astype(BF16))
        if nk == 1:
            o_ref[...] = prod.astype(out_dtype)
        else:
            kk = pl.program_id(2)

            @pl.when(kk == 0)
            def _():
                o_ref[...] = prod

            @pl.when(kk > 0)
            def _():
                o_ref[...] += prod

    return pl.pallas_call(
        body, grid=grid, in_specs=[a_spec, b_spec],
        out_specs=pl.BlockSpec((tm, tn), lambda i, j, kk: (i, j)),
        out_shape=jax.ShapeDtypeStruct((m, n), out_dtype),
        compiler_params=_params(("parallel", "parallel", "arbitrary")), name=name,
    )(a, b)


def _mm_pair_tn(a1, a2, b, *, tm, tk, name):
    k, m = a1.shape
    n = b.shape[1]
    assert m % tm == 0 and k % tk == 0 and a2.shape == a1.shape, name

    def body(a1_ref, a2_ref, b_ref, o1_ref, o2_ref):
        bv = b_ref[...]
        p1 = _dot_tn(a1_ref[...], bv)
        p2 = _dot_tn(a2_ref[...], bv)
        kk = pl.program_id(1)

        @pl.when(kk == 0)
        def _():
            o1_ref[...] = p1
            o2_ref[...] = p2

        @pl.when(kk > 0)
        def _():
            o1_ref[...] += p1
            o2_ref[...] += p2

    a_spec = pl.BlockSpec((tk, tm), lambda i, kk: (kk, i))
    o_spec = pl.BlockSpec((tm, n), lambda i, kk: (i, 0))
    return pl.pallas_call(
        body, grid=(m // tm, k // tk),
        in_specs=[a_spec, a_spec, pl.BlockSpec((tk, n), lambda i, kk: (kk, 0))],
        out_specs=[o_spec, o_spec],
        out_shape=[jax.ShapeDtypeStruct((m, n), F32)] * 2,
        compiler_params=_params(("parallel", "arbitrary")), name=name,
    )(a1, a2, b)


def _mm_pair_nt(a, w1_t, w2_t, own, *, tm, tn, out_dtype, name):
    m, k = a.shape
    n = w1_t.shape[0]
    assert m % tm == 0 and n % tn == 0 and w2_t.shape == w1_t.shape, name
    grid = (m // tm, n // tn)
    n_steps = grid[0] * grid[1]

    def body(a_ref, w1_ref, w2_ref, own_ref, o1_ref, o2_ref, gat_ref, send_sems, recv_sems):
        del own_ref
        step = pl.program_id(0) * grid[1] + pl.program_id(1)
        gather = _GatherPlan([own.shape[1:]], [gat_ref], send_sems, recv_sems)

        @pl.when(step == 0)
        def _():
            gather.start([0])

        @pl.when(step == (2 * n_steps) // 3)
        def _():
            gather.forward([0])

        av = a_ref[...]
        o1_ref[...] = _dot_nt(av, w1_ref[...]).astype(out_dtype)
        o2_ref[...] = _dot_nt(av, w2_ref[...]).astype(out_dtype)

        @pl.when(step == n_steps - 1)
        def _():
            gather.finish([0])

    w_spec = pl.BlockSpec((tn, k), lambda i, j: (j, 0))
    o_spec = pl.BlockSpec((tm, tn), lambda i, j: (i, j))
    return pl.pallas_call(
        body, grid=grid,
        in_specs=[pl.BlockSpec((tm, k), lambda i, j: (i, 0)), w_spec, w_spec, ANY],
        out_specs=[o_spec, o_spec, ANY],
        out_shape=[jax.ShapeDtypeStruct((m, n), out_dtype)] * 2 + [jax.ShapeDtypeStruct(own.shape, own.dtype)],
        scratch_shapes=_sem_pair(6), input_output_aliases={3: 2},
        compiler_params=_params(("arbitrary", "arbitrary")), name=name,
    )(a, w1_t, w2_t, own)


def _out_proj_bwd(a, o, dy1, w_out, grads):
    m = dy1.shape[0]
    tm = 1024
    nx = len(grads)
    shapes = [g.shape for g in grads]
    n_steps = m // tm

    def body(a_ref, o_ref, dy_ref, w_ref, *rest):
        grad_refs = rest[:nx]
        dmix_ref, dw_ref = rest[nx:nx + 2]
        exchange = _SiblingExchangePlan(shapes, grad_refs, rest[nx + 2:2 * nx + 2], *rest[2 * nx + 2:])

        @pl.when(pl.program_id(0) == 0)
        def _():
            exchange.start()

        dy = dy_ref[...]
        dmix_ref[...] = _dot_nt(dy, w_ref[...])
        top = _dot_tn(a_ref[...], dy)
        bottom = _dot_tn(o_ref[...], dy)

        @pl.when(pl.program_id(0) == 0)
        def _():
            dw_ref[:A_WIDTH, :] = top
            dw_ref[A_WIDTH:, :] = bottom

        @pl.when(pl.program_id(0) > 0)
        def _():
            dw_ref[:A_WIDTH, :] += top
            dw_ref[A_WIDTH:, :] += bottom

        @pl.when(pl.program_id(0) == n_steps - 1)
        def _():
            exchange.finish()

    tile = lambda width: pl.BlockSpec((tm, width), lambda i: (i, 0))
    res = pl.pallas_call(
        body, grid=(n_steps,),
        in_specs=[tile(A_WIDTH), tile(B_WIDTH), tile(D_MODEL), _full_spec((D_MODEL, D_MODEL))] + [ANY] * nx,
        out_specs=[tile(D_MODEL), _full_spec((D_MODEL, D_MODEL))] + [ANY] * nx,
        out_shape=[jax.ShapeDtypeStruct((m, D_MODEL), F32), jax.ShapeDtypeStruct((D_MODEL, D_MODEL), F32)]
        + [jax.ShapeDtypeStruct((N_SHARD, s[1] // 2, s[2]), F32) for s in shapes],
        scratch_shapes=_sem_pair(nx),
        compiler_params=_params(("arbitrary",)), name="out_proj_bwd",
    )(a, o, dy1, w_out, *grads)
    return res[:2], list(res[2:])


def _fused_rows(name, tm, mats, rows, vecs, fn, row_outs, acc_outs, exchange=()):
    m = mats[0][0].shape[0]
    nm, nr, nv, nro, nao, nx = len(mats), len(rows), len(vecs), len(row_outs), len(acc_outs), len(exchange)
    n_steps = m // tm

    def body(*refs):
        a_refs, w_refs = refs[:nm], refs[nm:2 * nm]
        pos = 2 * nm
        row_refs, vec_refs, part_refs = refs[pos:pos + nr], refs[pos + nr:pos + nr + nv], refs[pos + nr + nv:pos + nr + nv + nx]
        pos += nr + nv + nx
        out_refs, acc_refs, recv_refs = refs[pos:pos + nro], refs[pos + nro:pos + nro + nao], refs[pos + nro + nao:pos + nro + nao + nx]
        sems = refs[pos + nro + nao + nx:]
        i = pl.program_id(0)
        if nx:
            plan = _ChipExchangePlan(part_refs, recv_refs, *sems)

            @pl.when(i == 0)
            def _():
                plan.start()

        @pl.when(i == 0)
        def _():
            for r in acc_refs:
                r[...] = jnp.zeros_like(r)

        y = None
        for a_ref, w_ref, (_, _, dims, sl) in zip(a_refs, w_refs, mats):
            w = w_ref[...] if sl is None else w_ref[sl, :]
            part = (_dot if dims == "nn" else _dot_nt)(a_ref[...], w)
            y = part if y is None else y + part
        res = fn(y, *[r[...] for r in row_refs], *[v[...] for v in vec_refs])
        for r, val in zip(out_refs, res[:nro]):
            r[...] = val.astype(r.dtype)
        for r, val in zip(acc_refs, res[nro:]):
            r[...] += val

        if nx:
            @pl.when(i == n_steps - 1)
            def _():
                plan.finish()

    tile = lambda width: pl.BlockSpec((tm, width), lambda i: (i, 0))
    res = pl.pallas_call(
        body, grid=(n_steps,),
        in_specs=[tile(a.shape[1]) for a, _, _, _ in mats] + [_full_spec(w.shape) for _, w, _, _ in mats]
        + [tile(D_MODEL)] * nr + [_full_spec((1, D_MODEL))] * nv + [ANY] * nx,
        out_specs=[tile(D_MODEL)] * nro + [_full_spec(s) for s in acc_outs] + [ANY] * nx,
        out_shape=[jax.ShapeDtypeStruct((m, D_MODEL), dt) for dt in row_outs]
        + [jax.ShapeDtypeStruct(s, F32) for s in acc_outs]
        + [jax.ShapeDtypeStruct((3,) + p.shape[1:], p.dtype) for p in exchange],
        scratch_shapes=_sem_pair(3 * nx) if nx else [],
        compiler_params=_params(("arbitrary",)), name=name,
    )(*[a for a, _, _, _ in mats], *[w for _, w, _, _ in mats], *rows, *vecs, *exchange)
    return list(res[:nro + nao]), list(res[nro + nao:])


def _vec_spec(width=D_MODEL):
    return pl.BlockSpec((1, width), lambda i: (0, 0))


def _rstd(v):
    return lax.rsqrt(jnp.mean(v * v, axis=-1, keepdims=True) + NORM_EPS)


def _mid_fwd_rows(y1, x0, g2, g3):
    x1 = x0 + y1 * _rstd(y1) * g2
    return y1, x1, x1 * _rstd(x1) * g3


def _rms_bwd_rows(dout, v, g):
    r = _rstd(v)
    n = v * r
    dn = dout * g
    dv = r * (dn - n * jnp.mean(dn * n, axis=-1, keepdims=True))
    dg = jnp.sum(dout * n, axis=0, keepdims=True)
    return dv, dg


def _loss_head_rows(y2, x1, tgt, g4):
    x2 = x1 + y2 * _rstd(y2) * g4
    err = x2 - tgt
    loss = 0.5 * jnp.sum(jnp.mean(err * err, axis=-1, keepdims=True), axis=0, keepdims=True)
    dx2 = err * (1.0 / D_MODEL)
    dy2, dg4 = _rms_bwd_rows(dx2, y2, g4)
    return dx2, dy2, dg4, loss


def _mid_bwd_rows(dh2, x1, y1, dx2, g2, g3):
    d3, dg3 = _rms_bwd_rows(dh2, x1, g3)
    dx1 = dx2 + d3
    dy1, dg2 = _rms_bwd_rows(dx1, y1, g2)
    return dx1, dy1, dg2, dg3


def _in_bwd_rows(dh1, x0, dx1, g1):
    d1, dg1 = _rms_bwd_rows(dh1, x0, g1)
    return dx1 + d1, dg1


GATE_ROWS = 512


def _group_mean_matrix():
    p = np.zeros((A_WIDTH, A_WIDTH), np.float32)
    for g in range(A_GROUPS):
        p[g * HEAD_DIM:(g + 1) * HEAD_DIM, g * HEAD_DIM:(g + 1) * HEAD_DIM] = 1.0 / HEAD_DIM
    return jnp.asarray(p)


def _group_masks(width=A_WIDTH):
    lane = lax.broadcasted_iota(jnp.int32, (1, width), 1)
    return [(lane >= g * HEAD_DIM) & (lane < (g + 1) * HEAD_DIM) for g in range(width // HEAD_DIM)]


GROUP_SUM_PRECISION = lax.Precision.HIGH


def _layernorm_groups(vg, pavg):
    hi = GROUP_SUM_PRECISION
    mu = _dot(vg, pavg, hi)
    xc = vg - mu
    var = _dot(xc * xc, pavg, hi)
    rstd = lax.rsqrt(var + NORM_EPS)
    return xc * rstd, rstd


def _spatial_mix(w_bf, vn_chunk_bf, masks, bz):
    z = bz
    for g in range(A_GROUPS):
        z = z + jnp.where(masks[g], _dot(w_bf[g], vn_chunk_bf), 0.0)
    return z


def _full_spec(shape):
    return pl.BlockSpec(shape, lambda i: tuple(0 for _ in shape))


def _gate_fwd_rows(u, v, lg, lb, w_ref, bz, pavg, a_ref):
    masks = _group_masks()
    row = lax.broadcasted_iota(jnp.int32, (CHUNK, CHUNK), 0)
    col = lax.broadcasted_iota(jnp.int32, (CHUNK, CHUNK), 1)
    w_bf = [jnp.where(row >= col, w_ref[g], 0.0).astype(BF16) for g in range(A_GROUPS)]
    ug = _gelu(u)
    vhat, _ = _layernorm_groups(_gelu(v), pavg)
    vn = vhat * lg + lb
    for c in range(GATE_ROWS // CHUNK):
        sl = slice(c * CHUNK, (c + 1) * CHUNK)
        z = _spatial_mix(w_bf, vn[sl].astype(BF16), masks, bz)
        a_ref[sl, :] = (ug[sl] * z).astype(BF16)


def _gate_bwd(uv, dmix, ln_g, ln_b, w_s, w_st, bz, grads):
    m = uv.shape[0]
    pavg = _group_mean_matrix()
    nsteps = m // GATE_ROWS
    nx = len(grads)
    shapes = [g.shape for g in grads]

    def body(u_ref, v_ref, da_ref, lg_ref, lb_ref, w_ref, wt_ref, bz_ref, p_ref, *rest):
        grad_refs = rest[:nx]
        duv_ref, dlg_ref, dlb_ref, dw_ref, dbz_ref = rest[nx:nx + 5]
        recv_refs = rest[nx + 5:2 * nx + 5]
        exchange = _SiblingExchangePlan(shapes, grad_refs, recv_refs, *rest[2 * nx + 5:])
        i = pl.program_id(0)

        @pl.when(i == 0)
        def _():
            exchange.start()
            dlg_ref[...] = jnp.zeros_like(dlg_ref)
            dlb_ref[...] = jnp.zeros_like(dlb_ref)
            dw_ref[...] = jnp.zeros_like(dw_ref)
            dbz_ref[...] = jnp.zeros_like(dbz_ref)

        hi = GROUP_SUM_PRECISION
        masks = _group_masks()
        row = lax.broadcasted_iota(jnp.int32, (CHUNK, CHUNK), 0)
        col = lax.broadcasted_iota(jnp.int32, (CHUNK, CHUNK), 1)
        tril = row >= col
        w_bf = [jnp.where(tril, w_ref[g], 0.0).astype(BF16) for g in range(A_GROUPS)]
        wt_bf = [jnp.where(col >= row, wt_ref[g], 0.0).astype(BF16) for g in range(A_GROUPS)]
        pavg_v = p_ref[...]
        lg = lg_ref[...]
        ug, dug = _gelu_and_grad(u_ref[...])
        vg, dvg_dx = _gelu_and_grad(v_ref[...])
        vhat, rstd = _layernorm_groups(vg, pavg_v)
        vn = vhat * lg + lb_ref[...]
        da = da_ref[...]
        bz = bz_ref[...]
        for c in range(GATE_ROWS // CHUNK):
            sl = slice(c * CHUNK, (c + 1) * CHUNK)
            vn_bf = vn[sl].astype(BF16)
            z = _spatial_mix(w_bf, vn_bf, masks, bz)
            dz = da[sl] * ug[sl]
            duv_ref[sl, 0:A_WIDTH] = (da[sl] * z * dug[sl]).astype(BF16)
            dbz_ref[...] += dz
            dz_bf = dz.astype(BF16)
            dvn = jnp.zeros((CHUNK, A_WIDTH), F32)
            for g in range(A_GROUPS):
                dz_g = jnp.where(masks[g], dz, 0.0).astype(BF16)
                dw_ref[g] += jnp.where(tril, _dot_nt(dz_g, vn_bf), 0.0)
                dvn = dvn + jnp.where(masks[g], _dot(wt_bf[g], dz_bf), 0.0)
            vh = vhat[sl]
            dlb_ref[...] += jnp.sum(dvn, axis=0, keepdims=True)
            dlg_ref[...] += jnp.sum(dvn * vh, axis=0, keepdims=True)
            dvh = dvn * lg
            m1 = _dot(dvh, pavg_v, hi)
            m2 = _dot(dvh * vh, pavg_v, hi)
            duv_ref[sl, A_WIDTH:2 * A_WIDTH] = (rstd[sl] * (dvh - m1 - vh * m2) * dvg_dx[sl]).astype(BF16)

        @pl.when(i == nsteps - 1)
        def _():
            dbz_ref[...] = _dot(dbz_ref[...], pavg_v * float(HEAD_DIM), hi)
            exchange.finish()

    res = pl.pallas_call(
        body, grid=(nsteps,),
        in_specs=[pl.BlockSpec((GATE_ROWS, A_WIDTH), lambda i: (i, 0)),
                  pl.BlockSpec((GATE_ROWS, A_WIDTH), lambda i: (i, 1)),
                  pl.BlockSpec((GATE_ROWS, A_WIDTH), lambda i: (i, 0)),
                  _full_spec((1, A_WIDTH)), _full_spec((1, A_WIDTH)), _full_spec((A_GROUPS, CHUNK, CHUNK)),
                  _full_spec((A_GROUPS, CHUNK, CHUNK)), _full_spec((CHUNK, A_WIDTH)),
                  _full_spec((A_WIDTH, A_WIDTH))] + [ANY] * nx,
        out_specs=[pl.BlockSpec((GATE_ROWS, 2 * A_WIDTH), lambda i: (i, 0)),
                   _full_spec((1, A_WIDTH)), _full_spec((1, A_WIDTH)), _full_spec((A_GROUPS, CHUNK, CHUNK)),
                   _full_spec((CHUNK, A_WIDTH))] + [ANY] * nx,
        out_shape=[jax.ShapeDtypeStruct((m, IN_COLS), BF16),
                   jax.ShapeDtypeStruct((1, A_WIDTH), F32), jax.ShapeDtypeStruct((1, A_WIDTH), F32),
                   jax.ShapeDtypeStruct((A_GROUPS, CHUNK, CHUNK), F32),
                   jax.ShapeDtypeStruct((CHUNK, A_WIDTH), F32)]
        + [jax.ShapeDtypeStruct((N_SHARD, s[1] // 2, s[2]), F32) for s in shapes],
        scratch_shapes=_sem_pair(nx),
        compiler_params=_params(("arbitrary",)), name="gate_bwd",
    )(uv, uv, dmix, ln_g, ln_b, w_s, w_st, bz, pavg, *grads)
    return res[:5], list(res[5:])


Q_BLOCK = 128
PAIR = 2 * HEAD_DIM
N_PAIR = B_HEADS // 2
N_CFG = len(DILATED)
BLOCKS_PER_CFG = SEQ // Q_BLOCK
QKV_SLABS = 3 * N_PAIR
FWD_BLOCKS_PER_TRIP = 8
BWD_BLOCKS_PER_TRIP = 4


def _t5_bucket_np(dist, dtype):
    max_exact = NUM_BUCKETS // 2
    d = np.maximum(dist, 1).astype(dtype)
    large = max_exact + (np.log(d / dtype(max_exact)) / dtype(math.log(MAX_DISTANCE / max_exact))
                         * dtype(NUM_BUCKETS - max_exact))
    large = np.minimum(large.astype(np.int32), NUM_BUCKETS - 1)
    return np.where(dist < max_exact, dist, large)


def _bucket_tables():
    i = np.arange(Q_BLOCK)[:, None]
    j = np.arange(Q_BLOCK)[None, :]
    tables = []
    for _, dil in DILATED:
        rel_prev = Q_BLOCK + i - j
        rel_cur = i - j
        rel = np.concatenate([rel_prev, rel_cur], axis=1)
        valid = np.concatenate([rel_prev <= Q_BLOCK, rel_cur >= 0], axis=1)
        dist = np.maximum(rel, 0) * dil
        b32 = _t5_bucket_np(dist, np.float32)
        b64 = _t5_bucket_np(dist, np.float64)
        assert np.array_equal(b32, b64)
        tables.append(np.where(valid, b32, -1).astype(np.int32))
    return np.stack(tables)


def _present_buckets(buckets_np):
    return [sorted(set(int(v) for v in np.unique(buckets_np[c]) if v >= 0)) for c in range(N_CFG)]


def _bias_tables_body(buckets_np):
    present = _present_buckets(buckets_np)

    def tables(rb_ref, bk_ref, o_ref, ot_ref):
        for c in range(N_CFG):
            bk = bk_ref[c]
            for h in range(B_HEADS):
                acc = jnp.full((Q_BLOCK, 2 * Q_BLOCK), NEG_INF, F32)
                for b in present[c]:
                    acc = jnp.where(bk == b, rb_ref[h, b], acc)
                o_ref[c, h] = acc
                ot_ref[c, h] = acc.T

    return tables


def _proj_fwd(x, g1, w_in_t, ln_g, ln_b, w_s, bz):
    m = x.shape[0]
    tm = GATE_ROWS
    pavg = _group_mean_matrix()

    def body(x_ref, g_ref, w_ref, lg_ref, lb_ref, ws_ref, bz_ref, p_ref, h_ref, uv_ref, qkv_ref, a_ref):
        xv = x_ref[...]
        h = (xv * _rstd(xv) * g_ref[...]).astype(BF16)
        h_ref[...] = h
        acc = _dot_nt(h, w_ref[...])
        uv_ref[...] = acc[:, :2 * A_WIDTH]
        for s in range(QKV_SLABS):
            qkv_ref[s] = acc[:, 2 * A_WIDTH + s * PAIR:2 * A_WIDTH + (s + 1) * PAIR]
        _gate_fwd_rows(acc[:, :A_WIDTH], acc[:, A_WIDTH:2 * A_WIDTH], lg_ref[...], lb_ref[...], ws_ref,
                       bz_ref[...], p_ref[...], a_ref)

    return pl.pallas_call(
        body, grid=(m // tm,),
        in_specs=[pl.BlockSpec((tm, D_MODEL), lambda i: (i, 0)), _vec_spec(),
                  pl.BlockSpec((IN_COLS, D_MODEL), lambda i: (0, 0)),
                  _full_spec((1, A_WIDTH)), _full_spec((1, A_WIDTH)), _full_spec((A_GROUPS, CHUNK, CHUNK)),
                  _full_spec((CHUNK, A_WIDTH)), _full_spec((A_WIDTH, A_WIDTH))],
        out_specs=[pl.BlockSpec((tm, D_MODEL), lambda i: (i, 0)),
                   pl.BlockSpec((tm, 2 * A_WIDTH), lambda i: (i, 0)),
                   pl.BlockSpec((QKV_SLABS, tm, PAIR), lambda i: (0, i, 0)),
                   pl.BlockSpec((tm, A_WIDTH), lambda i: (i, 0))],
        out_shape=[jax.ShapeDtypeStruct((m, D_MODEL), BF16), jax.ShapeDtypeStruct((m, 2 * A_WIDTH), F32),
                   jax.ShapeDtypeStruct((QKV_SLABS, m, PAIR), F32), jax.ShapeDtypeStruct((m, A_WIDTH), BF16)],
        compiler_params=_params(("parallel",)), name="proj_fwd",
    )(x, g1, w_in_t, ln_g, ln_b, w_s, bz, pavg)


def _pair_masks():
    lane = lax.broadcasted_iota(jnp.int32, (1, PAIR), 1)
    return [lane < HEAD_DIM, lane >= HEAD_DIM]


def _block_rows(idx, dil):
    static = isinstance(idx, int)
    r, n = idx % dil, idx // dil

    def rows_of(block):
        start = r + (dil * Q_BLOCK) * block
        if dil == 1:
            return pl.ds(start if static else pl.multiple_of(start, Q_BLOCK), Q_BLOCK)
        return pl.ds(start, Q_BLOCK, stride=dil)

    prev = rows_of(n - 1) if not static or n > 0 else None
    return rows_of(n), prev


def _attn_fwd(qkv, bias, batch, owns):
    m = qkv.shape[1]
    comb_rows = 256
    nt = len(owns)
    shapes = [g.shape[1:] for g in owns]
    n_steps = batch * N_PAIR
    ts = list(range(nt))

    def body(q_ref, k_ref, v_ref, b_ref, *rest):
        o_ref, l_ref = rest[nt:nt + 2]
        gat_refs = rest[nt + 2:2 * nt + 2]
        scratch = rest[2 * nt + 2:]
        oc_refs, lc_refs = scratch[:N_CFG], scratch[N_CFG:2 * N_CFG]
        step = pl.program_id(0) * N_PAIR + pl.program_id(1)
        gather = _RelayGatherPlan(shapes, None, gat_refs, *scratch[2 * N_CFG:])

        @pl.when(step == 0)
        def _():
            gather.start(ts)

        @pl.when(step == n_steps // 2)
        def _():
            gather.relay(ts)

        @pl.when(step == n_steps - 2)
        def _():
            gather.forward(ts)

        masks = _pair_masks()
        for ci, (_, dil) in enumerate(DILATED):
            nb = SEQ // dil // Q_BLOCK

            def block(trip, ci=ci, dil=dil, nb=nb):
                work = []
                for u in range(FWD_BLOCKS_PER_TRIP):
                    rows, prow = _block_rows(trip * FWD_BLOCKS_PER_TRIP + u, dil)
                    has_prev = nb > 1 and prow is not None
                    q = q_ref[rows, :] * 0.125
                    kc = k_ref[rows, :].astype(BF16)
                    vc = v_ref[rows, :]
                    kp = k_ref[prow, :].astype(BF16) if has_prev else None
                    vp = v_ref[prow, :] if has_prev else None
                    tiles = []
                    for h in range(2):
                        qh = jnp.where(masks[h], q, 0.0).astype(BF16)
                        sc = _dot_nt(qh, kc) + b_ref[ci, h, :, Q_BLOCK:]
                        sp = _dot_nt(qh, kp) + b_ref[ci, h, :, :Q_BLOCK] if has_prev else None
                        tiles.append((sc, sp))
                    work.append((rows, vc, vp, tiles))
                probs = []
                for _, _, _, tiles in work:
                    ps = []
                    for sc, sp in tiles:
                        mx = jnp.max(sc if sp is None else jnp.maximum(sc, sp), axis=1, keepdims=True)
                        pc = jnp.exp(sc - mx).astype(BF16)
                        pp = None if sp is None else jnp.exp(sp - mx).astype(BF16)
                        ps.append((mx, pc, pp))
                    probs.append(ps)
                for (rows, vc, vp, _), ps in zip(work, probs):
                    res = []
                    for h, (_, pc, pp) in enumerate(ps):
                        r = _dot(pc, jnp.where(masks[h], vc, 1.0).astype(BF16))
                        if pp is not None:
                            r = r + _dot(pp, jnp.where(masks[h], vp, 1.0).astype(BF16))
                        res.append(r)
                    num = jnp.where(masks[0], res[0], res[1])
                    den = pltpu.roll(jnp.where(masks[0], res[1], res[0]), HEAD_DIM, 1)
                    oc_refs[ci][rows, :] = num / den
                    lc_refs[ci][rows, :] = jnp.where(masks[0], ps[0][0], ps[1][0]) + jnp.log(den)

            for trip in range(BLOCKS_PER_CFG // FWD_BLOCKS_PER_TRIP):
                block(trip)

        def combine(i, carry):
            rr = pl.ds(pl.multiple_of(i * comb_rows, comb_rows), comb_rows)
            ls = [lc_refs[c][rr, :] for c in range(N_CFG)]
            mx = functools.reduce(jnp.maximum, ls)
            ws = [jnp.exp(l - mx) for l in ls]
            tot = functools.reduce(lambda a, b: a + b, ws)
            o = functools.reduce(lambda a, b: a + b, [ws[c] * oc_refs[c][rr, :] for c in range(N_CFG)]) / tot
            o_ref[rr, :] = o.astype(BF16)
            l_ref[rr, :] = mx + jnp.log(tot)
            return carry

        lax.fori_loop(0, SEQ // comb_rows, combine, 0)

        @pl.when(step == n_steps - 1)
        def _():
            gather.finish(ts)

    def slab(first):
        return pl.BlockSpec((None, SEQ, PAIR), lambda b, p: (first + p, b, 0))

    nat = pl.BlockSpec((SEQ, PAIR), lambda b, p: (b, p))
    res = pl.pallas_call(
        body, grid=(batch, N_PAIR),
        in_specs=[slab(0), slab(N_PAIR), slab(2 * N_PAIR),
                  pl.BlockSpec((N_CFG, 2, Q_BLOCK, 2 * Q_BLOCK), lambda b, p: (0, p, 0, 0))] + [ANY] * nt,
        out_specs=[nat, nat] + [ANY] * nt,
        out_shape=[jax.ShapeDtypeStruct((m, B_WIDTH), BF16), jax.ShapeDtypeStruct((m, B_WIDTH), F32)]
        + [jax.ShapeDtypeStruct(g.shape, g.dtype) for g in owns],
        scratch_shapes=[pltpu.VMEM((SEQ, PAIR), F32)] * (2 * N_CFG) + _sem_pair(6 * nt),
        input_output_aliases={4 + t: 2 + t for t in range(nt)},
        compiler_params=_params(("arbitrary", "arbitrary")), name="attn_fwd",
    )(qkv, qkv, qkv, bias, *owns)
    return res[0], res[1], list(res[2:])


def _attn_bwd(qkv, dmix, o, lse, bias_t, dproj, batch, parts, smalls):
    m = qkv.shape[1]
    nt, ns = len(parts), len(smalls)
    n_steps = N_PAIR * batch

    def body(q_ref, k_ref, v_ref, do_ref, o_ref, l_ref, b_ref, *rest):
        part_refs = rest[1:nt + 1]
        small_refs = rest[nt + 1:nt + 1 + ns]
        pos = nt + 1 + ns
        dproj_ref, ds_ref = rest[pos:pos + 2]
        recv_refs = rest[pos + 2:pos + 2 + nt]
        sum_refs = rest[pos + 2 + nt:pos + 2 + nt + ns]
        pos += 2 + nt + ns
        dq_acc, dk_acc, dv_acc, d_scr, stage, stage_sems, send_sems, recv_sems = rest[pos:pos + 8]
        allreduce = _SmallAllReducePlan(small_refs, sum_refs, rest[pos + 8:pos + 8 + ns],
                                        rest[pos + 8 + ns:pos + 8 + 2 * ns], *rest[pos + 8 + 2 * ns:])
        pair, seq = pl.program_id(0), pl.program_id(1)
        step = pair * batch + seq
        exchange = _ChipExchangePlan(part_refs, recv_refs, send_sems, recv_sems)

        @pl.when(step == 0)
        def _():
            allreduce.start_sibling()

        @pl.when(step == n_steps // 2)
        def _():
            allreduce.sum_sibling_and_start_chips()

        def stage_copies():
            rows = pl.ds(pl.multiple_of(seq * SEQ, SEQ), SEQ)
            return [pltpu.make_async_copy(
                stage.at[k],
                dproj_ref.at[rows, pl.ds(pl.multiple_of(2 * A_WIDTH + k * B_WIDTH + pair * PAIR, PAIR), PAIR)],
                stage_sems.at[k]) for k in range(3)]

        @pl.when(step == 0)
        def _():
            exchange.start()

        @pl.when(pl.program_id(1) == 0)
        def _():
            ds_ref[...] = jnp.zeros_like(ds_ref)

        dq_acc[...] = jnp.zeros_like(dq_acc)
        dk_acc[...] = jnp.zeros_like(dk_acc)
        dv_acc[...] = jnp.zeros_like(dv_acc)
        d_scr[...] = do_ref[...] * o_ref[...].astype(F32)
        masks = _pair_masks()

        def stack_heads(t):
            return jnp.concatenate([jnp.where(masks[0], t, 0.0), jnp.where(masks[1], t, 0.0)], axis=0).astype(BF16)

        for ci, (_, dil) in enumerate(DILATED):
            nb = SEQ // dil // Q_BLOCK

            def block(trip, carry, ci=ci, dil=dil, nb=nb):
                first = []
                for u in range(BWD_BLOCKS_PER_TRIP):
                    rows, prow = _block_rows(trip * BWD_BLOCKS_PER_TRIP + u, dil)
                    has_prev = nb > 1 and prow is not None
                    if has_prev:
                        kcat = jnp.concatenate([k_ref[prow, :], k_ref[rows, :]], axis=0).astype(BF16)
                        vcat = jnp.concatenate([v_ref[prow, :], v_ref[rows, :]], axis=0).astype(BF16)
                    else:
                        kcat = k_ref[rows, :].astype(BF16)
                        vcat = v_ref[rows, :].astype(BF16)
                    qst = stack_heads(q_ref[rows, :] * 0.125)
                    dost = stack_heads(do_ref[rows, :])
                    lt = l_ref[rows, :].T
                    dt = d_scr[rows, :].T
                    lrow = jnp.concatenate([lt[0:1], lt[HEAD_DIM:HEAD_DIM + 1]], axis=1)
                    drow = jnp.concatenate([jnp.sum(dt[:HEAD_DIM], axis=0, keepdims=True),
                                            jnp.sum(dt[HEAD_DIM:], axis=0, keepdims=True)], axis=1)
                    first.append((has_prev, rows, prow, kcat, qst, dost, lrow, drow,
                                  _dot_nt(kcat, qst), _dot_nt(vcat, dost)))
                second = []
                for has_prev, rows, prow, kcat, qst, dost, lrow, drow, st, dpt in first:
                    keys = slice(0, 2 * Q_BLOCK) if has_prev else slice(Q_BLOCK, 2 * Q_BLOCK)
                    bt = jnp.concatenate([b_ref[ci, 0, keys, :], b_ref[ci, 1, keys, :]], axis=1)
                    pt = jnp.exp(st + bt - lrow)
                    dst = pt * (dpt - drow)
                    ds_ref[ci, 0, keys, :] += dst[:, :Q_BLOCK]
                    ds_ref[ci, 1, keys, :] += dst[:, Q_BLOCK:]
                    second.append((has_prev, rows, prow, kcat, qst, dost, pt.astype(BF16), dst.astype(BF16)))
                for has_prev, rows, prow, kcat, qst, dost, pt_bf, dst_bf in second:
                    dk = _dot(dst_bf, qst)
                    dv = _dot(pt_bf, dost)
                    dq2 = _dot_tn(dst_bf, kcat)
                    dq_acc[rows, :] += jnp.where(masks[0], dq2[:Q_BLOCK], dq2[Q_BLOCK:]) * 0.125
                    if has_prev:
                        dk_acc[prow, :] += dk[:Q_BLOCK]
                        dv_acc[prow, :] += dv[:Q_BLOCK]
                        dk_acc[rows, :] += dk[Q_BLOCK:]
                        dv_acc[rows, :] += dv[Q_BLOCK:]
                    else:
                        dk_acc[rows, :] += dk
                        dv_acc[rows, :] += dv
                return carry

            for trip in range(BLOCKS_PER_CFG // BWD_BLOCKS_PER_TRIP):
                block(trip, 0)

        @pl.when(step > 0)
        def _():
            for cp in stage_copies():
                cp.wait()

        stage[0] = dq_acc[...].astype(BF16)
        stage[1] = dk_acc[...].astype(BF16)
        stage[2] = dv_acc[...].astype(BF16)
        for cp in stage_copies():
            cp.start()

        @pl.when(step == n_steps - 1)
        def _():
            for cp in stage_copies():
                cp.wait()
            exchange.finish()
            allreduce.finish()

    def slab(first):
        return pl.BlockSpec((None, SEQ, PAIR), lambda p, b: (first + p, b, 0))

    nat = pl.BlockSpec((SEQ, PAIR), lambda p, b: (b, p))
    tbl = pl.BlockSpec((N_CFG, 2, 2 * Q_BLOCK, Q_BLOCK), lambda p, b: (0, p, 0, 0))
    acc = pltpu.VMEM((SEQ, PAIR), F32)
    vm = pl.BlockSpec(memory_space=pltpu.VMEM)
    res = pl.pallas_call(
        body, grid=(N_PAIR, batch),
        in_specs=[slab(0), slab(N_PAIR), slab(2 * N_PAIR),
                  pl.BlockSpec((SEQ, PAIR), lambda p, b: (b, A_WIDTH // PAIR + p)), nat, nat, tbl]
        + [ANY] * (nt + 1) + [vm] * ns,
        out_specs=[ANY, tbl] + [ANY] * nt + [vm] * ns,
        out_shape=[jax.ShapeDtypeStruct(dproj.shape, dproj.dtype),
                   jax.ShapeDtypeStruct((N_CFG, B_HEADS, 2 * Q_BLOCK, Q_BLOCK), F32)]
        + [jax.ShapeDtypeStruct((3,) + p.shape[1:], p.dtype) for p in parts]
        + [jax.ShapeDtypeStruct(a.shape, F32) for a in smalls],
        input_output_aliases={7: 0},
        scratch_shapes=[acc, acc, acc, acc, pltpu.VMEM((3, SEQ, PAIR), BF16), pltpu.SemaphoreType.DMA((3,))]
        + _sem_pair(3 * nt) + _SmallAllReducePlan.scratch(smalls),
        compiler_params=_params(("arbitrary", "arbitrary")), name="attn_bwd",
    )(qkv, qkv, qkv, dmix, o, lse, bias_t, dproj, *parts, *smalls)
    return res[0], res[1], list(res[2:2 + nt]), list(res[2 + nt:])


def _rel_bias_grad(ds, buckets_np, grads):
    present = _present_buckets(buckets_np)
    nx = len(grads)
    shapes = [g.shape for g in grads]

    def body(bk_ref, ds_ref, *rest):
        o_ref = rest[nx]
        acc_ref = rest[2 * nx + 1]
        exchange = _SiblingExchangePlan(shapes, rest[:nx], rest[nx + 1:2 * nx + 1], *rest[2 * nx + 2:])
        exchange.start()
        acc_ref[...] = jnp.zeros_like(acc_ref)
        for c in range(N_CFG):
            bk = bk_ref[c]
            for h in range(B_HEADS):
                dsv = ds_ref[c, h]
                for b in present[c]:
                    part = jnp.sum(jnp.where(bk == b, dsv, 0.0), axis=0, keepdims=True)
                    acc_ref[pl.ds(h * NUM_BUCKETS + b, 1), :] += part
        o_ref[...] = jnp.sum(acc_ref[...], axis=1, keepdims=True)
        exchange.finish()

    vm = pl.BlockSpec(memory_space=pltpu.VMEM)
    res = pl.pallas_call(
        body, in_specs=[vm, vm] + [ANY] * nx, out_specs=[vm] + [ANY] * nx,
        out_shape=[jax.ShapeDtypeStruct((B_HEADS * NUM_BUCKETS, 1), F32)]
        + [jax.ShapeDtypeStruct((N_SHARD, s[1] // 2, s[2]), F32) for s in shapes],
        scratch_shapes=[pltpu.VMEM((B_HEADS * NUM_BUCKETS, buckets_np.shape[-1]), F32)] + _sem_pair(nx),
        compiler_params=_params(), name="rel_bias_grad",
    )(jnp.asarray(buckets_np), ds, *grads)
    return res[0], list(res[1:])


def _row_index():
    return lax.broadcasted_iota(jnp.int32, (SEQ, LANE_BLOCK), 0)


def _shift_down(x, k, row):
    return jnp.where(row >= k, pltpu.roll(x, k, 0), 0.0)


def _shift_up(x, k, row):
    return jnp.where(row < SEQ - k, pltpu.roll(x, SEQ - k, 0), 0.0)


def _convgate_fwd(gate, up, conv_w, conv_b, batch):
    m = gate.shape[0]

    def body(g_ref, u_ref, w_ref, b_ref, a_ref):
        g = g_ref[...].astype(F32)
        w = w_ref[...]
        row = _row_index()
        c = b_ref[...] + w[0:1] * _shift_down(g, 2, row) + w[1:2] * _shift_down(g, 1, row) + w[2:3] * g
        a_ref[...] = (_gelu(c) * u_ref[...].astype(F32)).astype(BF16)

    blk = pl.BlockSpec((SEQ, LANE_BLOCK), lambda b, j: (b, j))
    return pl.pallas_call(
        body, grid=(batch, D_FF // LANE_BLOCK),
        in_specs=[blk, blk, pl.BlockSpec((3, LANE_BLOCK), lambda b, j: (0, j)),
                  pl.BlockSpec((1, LANE_BLOCK), lambda b, j: (0, j))],
        out_specs=blk,
        out_shape=jax.ShapeDtypeStruct((m, D_FF), BF16),
        compiler_params=_params(("parallel", "parallel")), name="convgate_fwd",
    )(gate, up, conv_w, conv_b)


def _convgate_bwd(gate, up, dact, conv_w, conv_b, batch):
    m = gate.shape[0]

    def body(g_ref, u_ref, da_ref, w_ref, b_ref, dg_ref, du_ref, dw_ref, db_ref):
        @pl.when(pl.program_id(1) == 0)
        def _():
            dw_ref[...] = jnp.zeros_like(dw_ref)
            db_ref[...] = jnp.zeros_like(db_ref)

        g = g_ref[...].astype(F32)
        w = w_ref[...]
        row = _row_index()
        g1 = _shift_down(g, 1, row)
        g2 = _shift_down(g, 2, row)
        c = b_ref[...] + w[0:1] * g2 + w[1:2] * g1 + w[2:3] * g
        gg, dgg = _gelu_and_grad(c)
        da = da_ref[...].astype(F32)
        du_ref[...] = (da * gg).astype(BF16)
        dc = da * u_ref[...].astype(F32) * dgg
        db_ref[...] += jnp.sum(dc, axis=0, keepdims=True)
        dw_ref[0:1, :] += jnp.sum(dc * g2, axis=0, keepdims=True)
        dw_ref[1:2, :] += jnp.sum(dc * g1, axis=0, keepdims=True)
        dw_ref[2:3, :] += jnp.sum(dc * g, axis=0, keepdims=True)
        dg_ref[...] = (w[2:3] * dc + w[1:2] * _shift_up(dc, 1, row) + w[0:1] * _shift_up(dc, 2, row)).astype(BF16)

    blk = pl.BlockSpec((SEQ, LANE_BLOCK), lambda j, b: (b, j))
    wspec = pl.BlockSpec((3, LANE_BLOCK), lambda j, b: (0, j))
    bspec = pl.BlockSpec((1, LANE_BLOCK), lambda j, b: (0, j))
    return pl.pallas_call(
        body, grid=(D_FF // LANE_BLOCK, batch),
        in_specs=[blk, blk, blk, wspec, bspec],
        out_specs=[blk, blk, wspec, bspec],
        out_shape=[jax.ShapeDtypeStruct((m, D_FF), BF16), jax.ShapeDtypeStruct((m, D_FF), BF16),
                   jax.ShapeDtypeStruct((3, D_FF), F32), jax.ShapeDtypeStruct((1, D_FF), F32)],
        compiler_params=_params(("parallel", "arbitrary")), name="convgate_bwd",
    )(gate, up, dact, conv_w, conv_b)


def _gather_weights(shards, conv_w_shard, rel_bias, buckets_np):
    nt = len(shards)
    shapes = [sh.shape for sh in shards]
    ts = list(range(nt))
    tables = _bias_tables_body(buckets_np)

    def body(*refs):
        shard_refs = refs[:nt]
        cw_ref, rb_ref, bk_ref = refs[nt:nt + 3]
        out_refs = refs[nt + 3:2 * nt + 3]
        cw_out, bias_ref, bias_t_ref = refs[2 * nt + 3:2 * nt + 6]
        scratch = refs[2 * nt + 6:]
        f32_refs, bf16_refs = scratch[:nt], scratch[nt:2 * nt]
        load_sems, store_sems, send_sems, recv_sems, cw_send, cw_recv = scratch[2 * nt:]
        plan = _RelayGatherPlan(shapes[:1], bf16_refs[:1], out_refs[:1], send_sems, recv_sems)
        x, y, c, chips = _mesh_pos()
        loads = [pltpu.make_async_copy(shard_refs[t], f32_refs[t], load_sems.at[t]) for t in ts]
        stores = [pltpu.make_async_copy(bf16_refs[t], out_refs[t].at[2 * x + y], store_sems.at[t]) for t in ts]
        stores.append(pltpu.make_async_copy(cw_ref, cw_out.at[2 * x + y], store_sems.at[nt]))

        def cw_copy(j, src, dst, chip):
            return pltpu.make_async_remote_copy(src_ref=src, dst_ref=dst, send_sem=cw_send.at[j],
                                                recv_sem=cw_recv.at[j], device_id=(*chip, c), device_id_type=MESH)

        def to_bf16(t):
            loads[t].wait()
            bf16_refs[t][...] = f32_refs[t][...].astype(BF16)
            stores[t].start()

        for cp in loads:
            cp.start()
        to_bf16(0)
        plan.start([0])
        cw_sends = [cw_copy(j, cw_ref, cw_out.at[2 * x + y], chip) for j, chip in enumerate(chips)]
        for cp in cw_sends + stores[nt:]:
            cp.start()
        for t in ts[1:]:
            to_bf16(t)
        plan.relay([0])
        tables(rb_ref, bk_ref, bias_ref, bias_t_ref)
        plan.forward([0])
        for j, chip in enumerate(chips):
            dst = cw_out.at[2 * chip[0] + chip[1]]
            cw_copy(j, dst, dst, chip).wait_recv()
        plan.finish([0])
        for cp in cw_sends:
            cp.wait_send()
        for cp in stores:
            cp.wait()

    out_shape = [jax.ShapeDtypeStruct((N_SHARD,) + sh.shape, BF16) for sh in shards]
    out_shape.append(jax.ShapeDtypeStruct((N_SHARD,) + conv_w_shard.shape, conv_w_shard.dtype))
    out_shape += [jax.ShapeDtypeStruct((N_CFG, B_HEADS, Q_BLOCK, 2 * Q_BLOCK), F32),
                  jax.ShapeDtypeStruct((N_CFG, B_HEADS, 2 * Q_BLOCK, Q_BLOCK), F32)]
    vm = pl.BlockSpec(memory_space=pltpu.VMEM)
    res = pl.pallas_call(
        body, in_specs=[ANY] * (nt + 1) + [pl.BlockSpec(memory_space=pltpu.SMEM), vm],
        out_specs=[ANY] * (nt + 1) + [vm, vm], out_shape=out_shape,
        scratch_shapes=[pltpu.VMEM(sh.shape, F32) for sh in shards] + [pltpu.VMEM(sh.shape, BF16) for sh in shards]
        + [pltpu.SemaphoreType.DMA((nt,)), pltpu.SemaphoreType.DMA((nt + 1,))] + _sem_pair(6) + _sem_pair(3),
        compiler_params=pltpu.CompilerParams(has_side_effects=True, vmem_limit_bytes=VMEM_LIMIT),
        name="gather_weights",
    )(*shards, conv_w_shard, rel_bias.T, jnp.asarray(buckets_np))
    return list(res[:nt + 1]), res[nt + 1], res[nt + 2]


def _turn(t, u, s, last):
    return jnp.where(t == u, s, jnp.where(t > u, last, 0))


def _add_halves(gs, recvs, c_idx):
    n = len(gs)
    _, rows2, cols = gs[0].shape
    rows = rows2 // 2
    assert all(g.shape == gs[0].shape for g in gs)

    def body(c_ref, *refs):
        t = pl.program_id(0)
        for u in range(n):
            @pl.when(t == u)
            def _(u=u):
                refs[2 * n + u][...] = (refs[u][...] + refs[n + u][...]).astype(BF16)

    def own(u):
        return pl.BlockSpec((None, None, rows, cols), lambda t, s, c: (_turn(t, u, s, N_SHARD - 1), c[0], 0, 0))

    def plain(u):
        return pl.BlockSpec((None, rows, cols), lambda t, s, c: (_turn(t, u, s, N_SHARD - 1), 0, 0))

    return pl.pallas_call(
        body,
        grid_spec=pltpu.PrefetchScalarGridSpec(
            num_scalar_prefetch=1, grid=(n, N_SHARD),
            in_specs=[own(u) for u in range(n)] + [plain(u) for u in range(n)],
            out_specs=[plain(u) for u in range(n)]),
        out_shape=[jax.ShapeDtypeStruct((N_SHARD, rows, cols), BF16)] * n,
        compiler_params=_params(("arbitrary", "arbitrary")), name="rs_add_halves",
    )(c_idx, *[g.reshape(N_SHARD, 2, rows, cols) for g in gs], *recvs)


def _add_chips(parts, recvs, s_idx, c_idx):
    n = len(parts)
    _, rows, cols = parts[0].shape
    assert all(p.shape == parts[0].shape for p in parts)

    def body(idx_ref, *refs):
        t = pl.program_id(0)
        for u in range(n):
            @pl.when(t == u)
            def _(u=u):
                acc = refs[u][...].astype(F32)
                for j in range(3):
                    acc = acc + refs[n + u][j].astype(F32)
                refs[2 * n + u][...] = acc

    res = pl.pallas_call(
        body,
        grid_spec=pltpu.PrefetchScalarGridSpec(
            num_scalar_prefetch=1, grid=(n,),
            in_specs=[pl.BlockSpec((None, rows, cols), lambda t, idx: (idx[0], 0, 0))] * n
            + [pl.BlockSpec((3, rows, cols), lambda t, idx: (0, 0, 0))] * n,
            out_specs=[pl.BlockSpec((None, rows, cols), lambda t, idx: (idx[1], 0, 0))] * n),
        out_shape=[jax.ShapeDtypeStruct((2, rows, cols), F32)] * n,
        compiler_params=_params(("arbitrary",)), name="rs_add_chips",
    )(jnp.concatenate([s_idx, c_idx]), *parts, *recvs)
    return [r.reshape(2 * rows, cols) for r in res]


def _finish_reductions(fulls, arrays):
    nt, n = len(fulls), len(arrays)

    def body(*refs):
        in_refs = refs[nt:nt + n]
        full_refs, out_refs = refs[nt + n:2 * nt + n], refs[2 * nt + n:2 * nt + 2 * n]
        pos = 2 * nt + 2 * n
        share_send, share_recv = refs[pos + 2 * n:pos + 2 * n + 2]
        allreduce = _SmallAllReducePlan(in_refs, out_refs, refs[pos:pos + n], refs[pos + n:pos + 2 * n],
                                        *refs[pos + 2 * n + 2:])
        x, y, c, _ = _mesh_pos()

        def half(t, which):
            rows = fulls[t].shape[0] // 2
            return full_refs[t].at[pl.ds(which * rows, rows), :]

        def share(t, which):
            return pltpu.make_async_remote_copy(
                src_ref=half(t, which), dst_ref=half(t, which), send_sem=share_send.at[t],
                recv_sem=share_recv.at[t], device_id=(x, y, 1 - c), device_id_type=MESH)

        for t in range(nt):
            share(t, c).start()
        allreduce.start_sibling()
        allreduce.sum_sibling_and_start_chips()
        allreduce.finish()
        for t in range(nt):
            share(t, 1 - c).wait_recv()
        for t in range(nt):
            share(t, c).wait_send()

    vm = pl.BlockSpec(memory_space=pltpu.VMEM)
    res = pl.pallas_call(
        body, in_specs=[ANY] * nt + [vm] * n, out_specs=[ANY] * nt + [vm] * n,
        out_shape=[jax.ShapeDtypeStruct(f.shape, f.dtype) for f in fulls]
        + [jax.ShapeDtypeStruct(a.shape, F32) for a in arrays],
        input_output_aliases={t: t for t in range(nt)},
        scratch_shapes=[pltpu.VMEM(a.shape, F32) for a in arrays] + [pltpu.VMEM((3,) + a.shape, F32) for a in arrays]
        + _sem_pair(nt) + _sem_pair(4 * n),
        compiler_params=pltpu.CompilerParams(has_side_effects=True),
        name="finish_reductions",
    )(*fulls, *arrays)
    return list(res[:nt]), list(res[nt:])


def _from_col_shards(g):
    n, rows, cols = g.shape
    return g.transpose(1, 0, 2).reshape(rows, n * cols)


def _train_step(x, tgt, g1, g2, g3, g4, shards, ln_g, ln_b, w_s, b_s, rel_bias, conv_w_shard, conv_b, batch,
                s_idx, c_idx):
    buckets = _bucket_tables()
    bz = jnp.repeat(b_s.T, HEAD_DIM, axis=1)
    w_st = jnp.swapaxes(w_s, 1, 2)

    def shard_major(g):
        return g.reshape(N_SHARD, g.shape[0] // N_SHARD, D_MODEL)

    names = ["w_in", "w_out", "w_gate", "w_up", "w_down"]
    (g_in, g_out, g_gate, g_up, g_down, g_convw), bias, bias_t = _gather_weights(
        [shards[n] for n in names], conv_w_shard, rel_bias, buckets)
    w_in_t = g_in.reshape(IN_COLS, D_MODEL)
    conv_w = _from_col_shards(g_convw.reshape(N_SHARD, 3, SHARD_FF))

    h1, uv, qkv, a = _proj_fwd(x, g1, w_in_t, ln_g, ln_b, w_s, bz)
    o_bf, lse, (g_out, g_gate, g_up) = _attn_fwd(qkv, bias, batch, [g_out, g_gate, g_up])
    w_out = g_out.reshape(D_MODEL, D_MODEL)
    w_gate_t = g_gate.reshape(D_FF, D_MODEL)
    w_up_t = g_up.reshape(D_FF, D_MODEL)
    (y1, x1, h2), _ = _fused_rows(
        "out_proj_mid_fwd", 512,
        [(a, w_out, "nn", slice(0, A_WIDTH)), (o_bf, w_out, "nn", slice(A_WIDTH, D_MODEL))],
        [x], [g2, g3], _mid_fwd_rows, [F32, F32, BF16], [])
    gate, up, g_down = _mm_pair_nt(h2, w_gate_t, w_up_t, g_down, tm=1024, tn=1408, out_dtype=BF16,
                                   name="mm_gate_up")
    w_down = g_down.reshape(D_FF, D_MODEL)
    act = _convgate_fwd(gate, up, conv_w, conv_b, batch)
    (dx2, dy2, dg4, loss), _ = _fused_rows(
        "down_proj_loss_head", 512, [(act, w_down, "nn", None)], [x1, tgt], [g4], _loss_head_rows,
        [F32, BF16], [(1, D_MODEL), (1, 128)])

    dact = _mm(dy2, w_down, dims="nt", tm=1024, tn=1408, tk=1024, out_dtype=BF16, name="mm_dact")
    dw_down = _mm(act, dy2, dims="tn", tm=1408, tn=1024, tk=1024, out_dtype=F32, name="mm_dw_down")
    dgate, dup, dconv_w, dconv_b = _convgate_bwd(gate, up, dact, conv_w, conv_b, batch)
    (dx1, dy1, dg2, dg3), _ = _fused_rows(
        "dh2_mid_bwd", 256, [(dgate, w_gate_t, "nn", None), (dup, w_up_t, "nn", None)],
        [x1, y1, dx2], [g2, g3], _mid_bwd_rows, [F32, BF16], [(1, D_MODEL), (1, D_MODEL)])
    dw_gate_t, dw_up_t = _mm_pair_tn(dgate, dup, h2, tm=1408, tk=1024, name="mm_dw_gate_up")
    done = [shard_major(g) for g in (dw_down, dw_gate_t, dw_up_t)]
    (dmix, dw_out), recv_a = _out_proj_bwd(a, o_bf, dy1, w_out, done[:2])
    done.append(shard_major(dw_out))
    (dproj, dln_g, dln_b, dw_s, dbz), recv_b = _gate_bwd(uv, dmix, ln_g, ln_b, w_s, w_st, bz, done[2:])
    recv_a += recv_b
    parts = _add_halves(done[:3], recv_a[:3], c_idx) + _add_halves(done[3:], recv_a[3:], c_idx)
    early = dict(loss=loss, norm_mix_post=dg2, norm_ffn_pre=dg3, norm_ffn_post=dg4, ln_v_gain=dln_g,
                 ln_v_bias=dln_b, spatial_w=dw_s, spatial_b=dbz, conv_w=dconv_w, conv_b=dconv_b)
    dproj, ds, recv, early_sums = _attn_bwd(qkv, dmix, o_bf, lse, bias_t, dproj, batch, parts, list(early.values()))
    fulls = _add_chips(parts[:3], recv[:3], s_idx, c_idx) + _add_chips(parts[3:], recv[3:], s_idx, c_idx)
    dw_in_t = _mm(dproj, h1, dims="tn", tm=1408, tn=1024, tk=1024, out_dtype=F32, name="mm_dw_in")
    last = [shard_major(dw_in_t)]
    drel, recv_in_a = _rel_bias_grad(ds, np.ascontiguousarray(np.swapaxes(buckets, 1, 2)), last)
    part_in = _add_halves(last, recv_in_a, c_idx)
    (dx0, dg1), recv_in = _fused_rows(
        "dh1_in_bwd", 512, [(dproj, w_in_t, "nn", None)], [x, dx1], [g1], _in_bwd_rows,
        [F32], [(1, D_MODEL)], exchange=part_in)
    fulls += _add_chips(part_in, recv_in, s_idx, c_idx)
    half_reduced = dict(zip(["w_down", "w_gate", "w_up", "w_out", "w_in"], fulls))

    return dx0, dict(zip(early, early_sums)), dict(norm_mix_pre=dg1, rel_bias=drel), half_reduced


def _adamw_update(w, g, m, v):
    nm = ADAM_B1 * m + (1.0 - ADAM_B1) * g
    nv = ADAM_B2 * v + (1.0 - ADAM_B2) * (g * g)
    m_hat = nm / (1.0 - ADAM_B1 ** ADAM_STEP)
    v_hat = nv / (1.0 - ADAM_B2 ** ADAM_STEP)
    return -ADAM_LR * (m_hat / (jnp.sqrt(v_hat) + ADAM_EPS) + ADAM_WD * w), nm, nv


def _adamw(w, g, m, v, name):
    rows, cols = w.shape
    tr = next(cand for cand in (352, 256, 128) if rows % cand == 0)

    def body(w_ref, g_ref, m_ref, v_ref, go_ref, d_ref, nm_ref, nv_ref):
        gv = g_ref[...]
        go_ref[...] = gv
        d_ref[...], nm_ref[...], nv_ref[...] = _adamw_update(w_ref[...], gv, m_ref[...], v_ref[...])

    spec = pl.BlockSpec((tr, cols), lambda i: (i, 0))
    sds = jax.ShapeDtypeStruct((rows, cols), F32)
    return pl.pallas_call(
        body, grid=(rows // tr,), in_specs=[spec] * 4, out_specs=[spec] * 4, out_shape=[sds] * 4,
        compiler_params=_params(("parallel",)), name=name,
    )(w, g, m, v)


def _adamw_small(ws, gs, ms, vs):
    n = len(ws)

    def body(*refs):
        w_refs, g_refs, m_refs, v_refs = refs[:n], refs[n:2 * n], refs[2 * n:3 * n], refs[3 * n:4 * n]
        d_refs, nm_refs, nv_refs = refs[4 * n:5 * n], refs[5 * n:6 * n], refs[6 * n:7 * n]
        for t in range(n):
            d_refs[t][...], nm_refs[t][...], nv_refs[t][...] = _adamw_update(
                w_refs[t][...], g_refs[t][...], m_refs[t][...], v_refs[t][...])

    vm = pl.BlockSpec(memory_space=pltpu.VMEM)
    sds = [jax.ShapeDtypeStruct(w.shape, F32) for w in ws]
    res = pl.pallas_call(
        body, in_specs=[vm] * (4 * n), out_specs=[vm] * (3 * n), out_shape=sds * 3,
        compiler_params=_params(), name="adamw_small",
    )(*ws, *gs, *ms, *vs)
    return res[:n], res[n:2 * n], res[2 * n:]


SMALL = ["norm_mix_pre", "norm_mix_post", "norm_ffn_pre", "norm_ffn_post", "ln_v_gain", "ln_v_bias",
         "spatial_w", "spatial_b", "rel_bias", "conv_b"]
LARGE = ["w_in", "w_gate", "w_up", "w_down", "w_out"]
TRANSPOSED = ("w_in", "w_gate", "w_up")
ORDER = ["norm_mix_pre", "norm_mix_post", "norm_ffn_pre", "norm_ffn_post", "w_in", "ln_v_gain", "ln_v_bias",
         "spatial_w", "spatial_b", "rel_bias", "w_out", "w_gate", "w_up", "conv_w", "conv_b", "w_down"]


def kernel(x, norm_mix_pre, norm_mix_post, norm_ffn_pre, norm_ffn_post, w_in, ln_v_gain, ln_v_bias, spatial_w, spatial_b, rel_bias, w_out, w_gate, w_up, conv_w, conv_b, w_down, loss_target, m_norm_mix_pre, m_norm_mix_post, m_norm_ffn_pre, m_norm_ffn_post, m_w_in, m_ln_v_gain, m_ln_v_bias, m_spatial_w, m_spatial_b, m_rel_bias, m_w_out, m_w_gate, m_w_up, m_conv_w, m_conv_b, m_w_down, v_norm_mix_pre, v_norm_mix_post, v_norm_ffn_pre, v_norm_ffn_post, v_w_in, v_ln_v_gain, v_ln_v_bias, v_spatial_w, v_spatial_b, v_rel_bias, v_w_out, v_w_gate, v_w_up, v_conv_w, v_conv_b, v_w_down):
    params = dict(norm_mix_pre=norm_mix_pre, norm_mix_post=norm_mix_post, norm_ffn_pre=norm_ffn_pre,
                  norm_ffn_post=norm_ffn_post, w_in=w_in, ln_v_gain=ln_v_gain, ln_v_bias=ln_v_bias,
                  spatial_w=spatial_w, spatial_b=spatial_b, rel_bias=rel_bias, w_out=w_out, w_gate=w_gate,
                  w_up=w_up, conv_w=conv_w, conv_b=conv_b, w_down=w_down)
    mom = dict(norm_mix_pre=m_norm_mix_pre, norm_mix_post=m_norm_mix_post, norm_ffn_pre=m_norm_ffn_pre,
               norm_ffn_post=m_norm_ffn_post, w_in=m_w_in, ln_v_gain=m_ln_v_gain, ln_v_bias=m_ln_v_bias,
               spatial_w=m_spatial_w, spatial_b=m_spatial_b, rel_bias=m_rel_bias, w_out=m_w_out, w_gate=m_w_gate,
               w_up=m_w_up, conv_w=m_conv_w, conv_b=m_conv_b, w_down=m_w_down)
    var = dict(norm_mix_pre=v_norm_mix_pre, norm_mix_post=v_norm_mix_post, norm_ffn_pre=v_norm_ffn_pre,
               norm_ffn_post=v_norm_ffn_post, w_in=v_w_in, ln_v_gain=v_ln_v_gain, ln_v_bias=v_ln_v_bias,
               spatial_w=v_spatial_w, spatial_b=v_spatial_b, rel_bias=v_rel_bias, w_out=v_w_out, w_gate=v_w_gate,
               w_up=v_w_up, conv_w=v_conv_w, conv_b=v_conv_b, w_down=v_w_down)

    batch = x.shape[0]
    xi, yi, ci = lax.axis_index("x"), lax.axis_index("y"), lax.axis_index("c")
    s_idx = (2 * xi + yi).astype(jnp.int32).reshape(1)
    c_idx = ci.astype(jnp.int32).reshape(1)

    def local(a, n):
        return jnp.swapaxes(a[0], 0, 1) if n in TRANSPOSED else a[0]

    shards = {n: local(params[n], n) for n in LARGE}
    dx0, total, partial, half_reduced = _train_step(
        x.reshape(batch * SEQ, D_MODEL), loss_target.reshape(batch * SEQ, D_MODEL),
        norm_mix_pre, norm_mix_post, norm_ffn_pre, norm_ffn_post, shards,
        ln_v_gain.reshape(1, A_WIDTH), ln_v_bias.reshape(1, A_WIDTH), spatial_w[0], spatial_b[0], rel_bias,
        jnp.swapaxes(conv_w, 0, 1), conv_b, batch, s_idx, c_idx)
    grad_x = dx0.reshape(batch, SEQ, D_MODEL)

    names = list(partial)
    fulls, sums = _finish_reductions([half_reduced[n] for n in LARGE], [partial[n] for n in names])
    reduced = dict(zip(LARGE, fulls))
    total.update(zip(names, sums))
    loss = total["loss"][0, 0]
    total["spatial_b"] = total["spatial_b"][:, ::HEAD_DIM].T
    total["rel_bias"] = total["rel_bias"].reshape(B_HEADS, NUM_BUCKETS)
    total["conv_w"] = lax.dynamic_slice_in_dim(total["conv_w"], s_idx[0] * SHARD_FF, SHARD_FF, axis=1)
    small_names = SMALL + ["conv_w"]

    def small(a, n):
        return jnp.swapaxes(a, 0, 1) if n in ("rel_bias", "conv_w") else a

    for n in small_names:
        reduced[n] = total[n].reshape(small(params[n], n).shape)

    out_g, out_d, out_m, out_v = {}, {}, {}, {}
    for n in LARGE:
        res = _adamw(local(params[n], n), reduced[n], local(mom[n], n), local(var[n], n), name=f"adamw_{n}")
        if n in TRANSPOSED:
            res = [jnp.swapaxes(r, 0, 1) for r in res]
        out_g[n], out_d[n], out_m[n], out_v[n] = [r[None] for r in res]
    d, nm, nv = _adamw_small([small(params[n], n) for n in small_names], [reduced[n] for n in small_names],
                             [small(mom[n], n) for n in small_names], [small(var[n], n) for n in small_names])
    for n, dd, mm, vv in zip(small_names, d, nm, nv):
        out_g[n], out_d[n], out_m[n], out_v[n] = [small(r, n) for r in (reduced[n], dd, mm, vv)]

    return (loss, grad_x, *[out_g[n] for n in ORDER], *[out_d[n] for n in ORDER],
            *[out_m[n] for n in ORDER], *[out_v[n] for n in ORDER])
```

```python
import functools
import math

import numpy as np
import jax
import jax.numpy as jnp
from jax import lax
from jax.experimental import pallas as pl
from jax.experimental.pallas import tpu as pltpu

F32 = jnp.float32
BF16 = jnp.bfloat16
MESH = pl.DeviceIdType.MESH

D_MODEL = 1024
SEQ = 2048
HEAD_DIM = 64
A_GROUPS = 4
A_WIDTH = 256
B_HEADS = 12
B_WIDTH = 768
CHUNK = 128
DILATED = ((128, 1), (512, 4), (2048, 16))
NUM_BUCKETS = 32
MAX_DISTANCE = 2048
D_FF = 2816
IN_COLS = 2816
NORM_EPS = 1e-6
NEG_INF = -1e30
N_SHARD = 4
SHARD_FF = D_FF // N_SHARD
LANE_BLOCK = 256
VMEM_LIMIT = 56 * 1024 * 1024

ADAM_LR = 0.001
ADAM_B1 = 0.9
ADAM_B2 = 0.999
ADAM_EPS = 1e-08
ADAM_WD = 0.01
ADAM_STEP = 10

GELU_C = math.sqrt(2.0 / math.pi)
GELU_A = 0.044715

ANY = pl.BlockSpec(memory_space=pl.ANY)


def _params(sem=None):
    return pltpu.CompilerParams(dimension_semantics=sem, vmem_limit_bytes=VMEM_LIMIT)


def _dot(a, b, precision=None):
    return jnp.dot(a, b, preferred_element_type=F32, precision=precision)


def _dot_nt(a, b, precision=None):
    return lax.dot_general(a, b, (((1,), (1,)), ((), ())), preferred_element_type=F32, precision=precision)


def _dot_tn(a, b):
    return lax.dot_general(a, b, (((0,), (0,)), ((), ())), preferred_element_type=F32)


def _gelu(x):
    t = jnp.tanh(x * (GELU_C + (GELU_C * GELU_A) * (x * x)))
    return (0.5 * x) * (1.0 + t)


def _gelu_and_grad(x):
    x2 = x * x
    u = 1.0 + jnp.tanh(x * (GELU_C + (GELU_C * GELU_A) * x2))
    hx = 0.5 * x
    dg = u * (0.5 + hx * (2.0 - u) * (GELU_C + (3.0 * GELU_C * GELU_A) * x2))
    return hx * u, dg


def _mesh_pos():
    x, y, c = lax.axis_index("x"), lax.axis_index("y"), lax.axis_index("c")
    chips = [(1 - x, y), (x, 1 - y), (1 - x, 1 - y)]
    return x, y, c, chips


class _RelayGatherPlan:
    def __init__(self, shapes, shard_refs, out_refs, send_sems, recv_sems):
        self.shapes, self.shard_refs, self.out_refs = shapes, shard_refs, out_refs
        self.send_sems, self.recv_sems = send_sems, recv_sems
        x, y, c, self.chips = _mesh_pos()
        self.me, self.c, self.sib = (x, y), c, (x, y, 1 - c)
        self.first = (x + c - 2 * x * c, y + (1 - c) - 2 * y * (1 - c))
        self.second = (x + (1 - c) - 2 * x * (1 - c), y + c - 2 * y * c)
        self.diag = (1 - x, 1 - y)

    def _half(self, t, chip, which):
        rows = self.shapes[t][0] // 2
        return self.out_refs[t].at[2 * chip[0] + chip[1], pl.ds(which * rows, rows), :]

    def _copy(self, k, src, dst, to):
        return pltpu.make_async_remote_copy(src_ref=src, dst_ref=dst, send_sem=self.send_sems.at[k],
                                            recv_sem=self.recv_sems.at[k], device_id=to, device_id_type=MESH)

    def _own(self, t):
        if self.shard_refs is None:
            return self._half(t, self.me, self.c)
        rows = self.shapes[t][0] // 2
        return self.shard_refs[t].at[pl.ds(self.c * rows, rows), :]

    def _step1(self, t):
        return self._copy(6 * t, self._own(t), self._half(t, self.me, self.c), (*self.first, self.c))

    def _step2(self, t):
        landed = self._half(t, self.first, self.c)
        return [self._copy(6 * t + 1, self._own(t), self._half(t, self.me, self.c), (*self.second, self.c)),
                self._copy(6 * t + 2, landed, landed, (*self.second, self.c))]

    def _forwards(self, t):
        return [self._copy(6 * t + 3 + j, self._half(t, chip, self.c), self._half(t, chip, self.c), self.sib)
                for j, chip in enumerate(self.chips)]

    def start(self, ts):
        for t in ts:
            self._step1(t).start()
            self._step2(t)[0].start()

    def relay(self, ts):
        for t in ts:
            landed = self._half(t, self.first, self.c)
            self._copy(6 * t, landed, landed, self.sib).wait_recv()
            self._step2(t)[1].start()

    def forward(self, ts):
        for t in ts:
            for k, chip in ((1, self.second), (2, self.diag)):
                landed = self._half(t, chip, self.c)
                self._copy(6 * t + k, landed, landed, self.sib).wait_recv()
            for cp in self._forwards(t):
                cp.start()

    def finish(self, ts):
        for t in ts:
            for j, chip in enumerate(self.chips):
                other = self._half(t, chip, 1 - self.c)
                self._copy(6 * t + 3 + j, other, other, self.sib).wait_recv()
        for t in ts:
            for cp in [self._step1(t)] + self._step2(t) + self._forwards(t):
                cp.wait_send()


class _SiblingExchangePlan:
    def __init__(self, shapes, grad_refs, out_refs, send_sems, recv_sems):
        self.shapes, self.grad_refs, self.out_refs = shapes, grad_refs, out_refs
        self.send_sems, self.recv_sems = send_sems, recv_sems
        self.x, self.y, self.c, _ = _mesh_pos()

    def _copies(self):
        out = []
        for t, (g, o) in enumerate(zip(self.grad_refs, self.out_refs)):
            rows = self.shapes[t][1] // 2
            out.append(pltpu.make_async_remote_copy(
                src_ref=g.at[:, pl.ds((1 - self.c) * rows, rows), :], dst_ref=o, send_sem=self.send_sems.at[t],
                recv_sem=self.recv_sems.at[t], device_id=(self.x, self.y, 1 - self.c), device_id_type=MESH))
        return out

    def start(self):
        for cp in self._copies():
            cp.start()

    def finish(self):
        for cp in self._copies():
            cp.wait()


class _ChipExchangePlan:
    def __init__(self, part_refs, out_refs, send_sems, recv_sems):
        self.part_refs, self.out_refs, self.send_sems, self.recv_sems = part_refs, out_refs, send_sems, recv_sems
        _, _, self.c, self.chips = _mesh_pos()

    def _copies(self):
        return [pltpu.make_async_remote_copy(
            src_ref=p.at[2 * chip[0] + chip[1]], dst_ref=o.at[j], send_sem=self.send_sems.at[3 * t + j],
            recv_sem=self.recv_sems.at[3 * t + j], device_id=(*chip, self.c), device_id_type=MESH)
            for t, (p, o) in enumerate(zip(self.part_refs, self.out_refs)) for j, chip in enumerate(self.chips)]

    def start(self):
        for cp in self._copies():
            cp.start()

    def finish(self):
        for cp in self._copies():
            cp.wait()


class _SmallAllReducePlan:
    def __init__(self, in_refs, out_refs, sib_refs, chip_refs, send_sems, recv_sems):
        self.in_refs, self.out_refs, self.sib_refs, self.chip_refs = in_refs, out_refs, sib_refs, chip_refs
        self.send_sems, self.recv_sems = send_sems, recv_sems
        self.n = len(in_refs)
        self.x, self.y, self.c, self.chips = _mesh_pos()

    def _copy(self, k, src, dst, to):
        return pltpu.make_async_remote_copy(src_ref=src, dst_ref=dst, send_sem=self.send_sems.at[k],
                                            recv_sem=self.recv_sems.at[k], device_id=to, device_id_type=MESH)

    def _first(self):
        return [self._copy(t, self.in_refs[t], self.sib_refs[t], (self.x, self.y, 1 - self.c)) for t in range(self.n)]

    def _second(self):
        return [self._copy(self.n + 3 * t + j, self.out_refs[t], self.chip_refs[t].at[j], (*chip, self.c))
                for t in range(self.n) for j, chip in enumerate(self.chips)]

    def start_sibling(self):
        for cp in self._first():
            cp.start()

    def sum_sibling_and_start_chips(self):
        for cp in self._first():
            cp.wait()
        for t in range(self.n):
            self.out_refs[t][...] = self.in_refs[t][...] + self.sib_refs[t][...]
        for cp in self._second():
            cp.start()

    def finish(self):
        for cp in self._second():
            cp.wait()
        for t in range(self.n):
            self.out_refs[t][...] = ((self.out_refs[t][...] + self.chip_refs[t][0])
                                     + (self.chip_refs[t][1] + self.chip_refs[t][2]))

    @staticmethod
    def scratch(arrays):
        return ([pltpu.VMEM(a.shape, F32) for a in arrays] + [pltpu.VMEM((3,) + a.shape, F32) for a in arrays]
                + _sem_pair(4 * len(arrays)))


def _sem_pair(n):
    return [pltpu.SemaphoreType.DMA((n,)), pltpu.SemaphoreType.DMA((n,))]


def _mm(a, b, *, dims, tm, tn, tk, out_dtype, name):
    if dims == "nn":
        m, k = a.shape
        n = b.shape[1]
        a_spec = pl.BlockSpec((tm, tk), lambda i, j, kk: (i, kk))
        b_spec = pl.BlockSpec((tk, tn), lambda i, j, kk: (kk, j))
        dot = _dot
    elif dims == "nt":
        m, k = a.shape
        n = b.shape[0]
        a_spec = pl.BlockSpec((tm, tk), lambda i, j, kk: (i, kk))
        b_spec = pl.BlockSpec((tn, tk), lambda i, j, kk: (j, kk))
        dot = _dot_nt
    else:
        k, m = a.shape
        n = b.shape[1]
        a_spec = pl.BlockSpec((tk, tm), lambda i, j, kk: (kk, i))
        b_spec = pl.BlockSpec((tk, tn), lambda i, j, kk: (kk, j))
        dot = _dot_tn
    assert m % tm == 0 and n % tn == 0 and k % tk == 0, (name, m, n, k)
    grid = (m // tm, n // tn, k // tk)
    nk = grid[2]
    assert nk == 1 or out_dtype == F32, name

    def body(a_ref, b_ref, o_ref):
        prod = dot(a_ref[...].astype(BF16), b_ref[...].astype(BF16))
        if nk == 1:
            o_ref[...] = prod.astype(out_dtype)
        else:
            kk = pl.program_id(2)

            @pl.when(kk == 0)
            def _():
                o_ref[...] = prod

            @pl.when(kk > 0)
            def _():
                o_ref[...] += prod

    return pl.pallas_call(
        body, grid=grid, in_specs=[a_spec, b_spec],
        out_specs=pl.BlockSpec((tm, tn), lambda i, j, kk: (i, j)),
        out_shape=jax.ShapeDtypeStruct((m, n), out_dtype),
        compiler_params=_params(("parallel", "parallel", "arbitrary")), name=name,
    )(a, b)


def _mm_pair_tn(a1, a2, b, *, tm, tk, name):
    k, m = a1.shape
    n = b.shape[1]
    assert m % tm == 0 and k % tk == 0 and a2.shape == a1.shape, name

    def body(a1_ref, a2_ref, b_ref, o1_ref, o2_ref):
        bv = b_ref[...]
        p1 = _dot_tn(a1_ref[...], bv)
        p2 = _dot_tn(a2_ref[...], bv)
        kk = pl.program_id(1)

        @pl.when(kk == 0)
        def _():
            o1_ref[...] = p1
            o2_ref[...] = p2

        @pl.when(kk > 0)
        def _():
            o1_ref[...] += p1
            o2_ref[...] += p2

    a_spec = pl.BlockSpec((tk, tm), lambda i, kk: (kk, i))
    o_spec = pl.BlockSpec((tm, n), lambda i, kk: (i, 0))
    return pl.pallas_call(
        body, grid=(m // tm, k // tk),
        in_specs=[a_spec, a_spec, pl.BlockSpec((tk, n), lambda i, kk: (kk, 0))],
        out_specs=[o_spec, o_spec],
        out_shape=[jax.ShapeDtypeStruct((m, n), F32)] * 2,
        compiler_params=_params(("parallel", "arbitrary")), name=name,
    )(a1, a2, b)


def _mm_pair_nt(a, w1_t, w2_t, *, tm, tn, out_dtype, name):
    m, k = a.shape
    n = w1_t.shape[0]
    assert m % tm == 0 and n % tn == 0 and w2_t.shape == w1_t.shape, name

    def body(a_ref, w1_ref, w2_ref, o1_ref, o2_ref):
        av = a_ref[...]
        o1_ref[...] = _dot_nt(av, w1_ref[...]).astype(out_dtype)
        o2_ref[...] = _dot_nt(av, w2_ref[...]).astype(out_dtype)

    w_spec = pl.BlockSpec((tn, k), lambda i, j: (j, 0))
    o_spec = pl.BlockSpec((tm, tn), lambda i, j: (i, j))
    return pl.pallas_call(
        body, grid=(m // tm, n // tn),
        in_specs=[pl.BlockSpec((tm, k), lambda i, j: (i, 0)), w_spec, w_spec],
        out_specs=[o_spec, o_spec],
        out_shape=[jax.ShapeDtypeStruct((m, n), out_dtype)] * 2,
        compiler_params=_params(("parallel", "parallel")), name=name,
    )(a, w1_t, w2_t)


def _out_proj_bwd(a, o, dy1, w_out, grads):
    m = dy1.shape[0]
    tm = 1024
    nx = len(grads)
    shapes = [g.shape for g in grads]
    n_steps = m // tm

    def body(a_ref, o_ref, dy_ref, w_ref, *rest):
        grad_refs = rest[:nx]
        dmix_ref, dw_ref = rest[nx:nx + 2]
        exchange = _SiblingExchangePlan(shapes, grad_refs, rest[nx + 2:2 * nx + 2], *rest[2 * nx + 2:])

        @pl.when(pl.program_id(0) == 0)
        def _():
            exchange.start()

        dy = dy_ref[...]
        dmix_ref[...] = _dot_nt(dy, w_ref[...])
        top = _dot_tn(a_ref[...], dy)
        bottom = _dot_tn(o_ref[...], dy)

        @pl.when(pl.program_id(0) == 0)
        def _():
            dw_ref[:A_WIDTH, :] = top
            dw_ref[A_WIDTH:, :] = bottom

        @pl.when(pl.program_id(0) > 0)
        def _():
            dw_ref[:A_WIDTH, :] += top
            dw_ref[A_WIDTH:, :] += bottom

        @pl.when(pl.program_id(0) == n_steps - 1)
        def _():
            exchange.finish()

    tile = lambda width: pl.BlockSpec((tm, width), lambda i: (i, 0))
    res = pl.pallas_call(
        body, grid=(n_steps,),
        in_specs=[tile(A_WIDTH), tile(B_WIDTH), tile(D_MODEL), _full_spec((D_MODEL, D_MODEL))] + [ANY] * nx,
        out_specs=[tile(D_MODEL), _full_spec((D_MODEL, D_MODEL))] + [ANY] * nx,
        out_shape=[jax.ShapeDtypeStruct((m, D_MODEL), F32), jax.ShapeDtypeStruct((D_MODEL, D_MODEL), F32)]
        + [jax.ShapeDtypeStruct((N_SHARD, s[1] // 2, s[2]), F32) for s in shapes],
        scratch_shapes=_sem_pair(nx),
        compiler_params=_params(("arbitrary",)), name="out_proj_bwd",
    )(a, o, dy1, w_out, *grads)
    return res[:2], list(res[2:])


def _fused_rows(name, tm, mats, rows, vecs, fn, row_outs, acc_outs, exchange=()):
    m = mats[0][0].shape[0]
    nm, nr, nv, nro, nao, nx = len(mats), len(rows), len(vecs), len(row_outs), len(acc_outs), len(exchange)
    n_steps = m // tm

    def body(*refs):
        a_refs, w_refs = refs[:nm], refs[nm:2 * nm]
        pos = 2 * nm
        row_refs, vec_refs, part_refs = refs[pos:pos + nr], refs[pos + nr:pos + nr + nv], refs[pos + nr + nv:pos + nr + nv + nx]
        pos += nr + nv + nx
        out_refs, acc_refs, recv_refs = refs[pos:pos + nro], refs[pos + nro:pos + nro + nao], refs[pos + nro + nao:pos + nro + nao + nx]
        sems = refs[pos + nro + nao + nx:]
        i = pl.program_id(0)
        if nx:
            plan = _ChipExchangePlan(part_refs, recv_refs, *sems)

            @pl.when(i == 0)
            def _():
                plan.start()

        @pl.when(i == 0)
        def _():
            for r in acc_refs:
                r[...] = jnp.zeros_like(r)

        y = None
        for a_ref, w_ref, (_, _, dims, sl) in zip(a_refs, w_refs, mats):
            w = w_ref[...] if sl is None else w_ref[sl, :]
            part = (_dot if dims == "nn" else _dot_nt)(a_ref[...], w)
            y = part if y is None else y + part
        res = fn(y, *[r[...] for r in row_refs], *[v[...] for v in vec_refs])
        for r, val in zip(out_refs, res[:nro]):
            r[...] = val.astype(r.dtype)
        for r, val in zip(acc_refs, res[nro:]):
            r[...] += val

        if nx:
            @pl.when(i == n_steps - 1)
            def _():
                plan.finish()

    tile = lambda width: pl.BlockSpec((tm, width), lambda i: (i, 0))
    res = pl.pallas_call(
        body, grid=(n_steps,),
        in_specs=[tile(a.shape[1]) for a, _, _, _ in mats] + [_full_spec(w.shape) for _, w, _, _ in mats]
        + [tile(D_MODEL)] * nr + [_full_spec((1, D_MODEL))] * nv + [ANY] * nx,
        out_specs=[tile(D_MODEL)] * nro + [_full_spec(s) for s in acc_outs] + [ANY] * nx,
        out_shape=[jax.ShapeDtypeStruct((m, D_MODEL), dt) for dt in row_outs]
        + [jax.ShapeDtypeStruct(s, F32) for s in acc_outs]
        + [jax.ShapeDtypeStruct((3,) + p.shape[1:], p.dtype) for p in exchange],
        scratch_shapes=_sem_pair(3 * nx) if nx else [],
        compiler_params=_params(("arbitrary",)), name=name,
    )(*[a for a, _, _, _ in mats], *[w for _, w, _, _ in mats], *rows, *vecs, *exchange)
    return list(res[:nro + nao]), list(res[nro + nao:])


def _vec_spec(width=D_MODEL):
    return pl.BlockSpec((1, width), lambda i: (0, 0))


def _rstd(v):
    return lax.rsqrt(jnp.mean(v * v, axis=-1, keepdims=True) + NORM_EPS)


def _mid_fwd_rows(y1, x0, g2, g3):
    x1 = x0 + y1 * _rstd(y1) * g2
    return y1, x1, x1 * _rstd(x1) * g3


def _rms_bwd_rows(dout, v, g):
    r = _rstd(v)
    n = v * r
    dn = dout * g
    dv = r * (dn - n * jnp.mean(dn * n, axis=-1, keepdims=True))
    dg = jnp.sum(dout * n, axis=0, keepdims=True)
    return dv, dg


def _loss_head_rows(y2, x1, tgt, g4):
    x2 = x1 + y2 * _rstd(y2) * g4
    err = x2 - tgt
    loss = 0.5 * jnp.sum(jnp.mean(err * err, axis=-1, keepdims=True), axis=0, keepdims=True)
    dx2 = err * (1.0 / D_MODEL)
    dy2, dg4 = _rms_bwd_rows(dx2, y2, g4)
    return dx2, dy2, dg4, loss


def _mid_bwd_rows(dh2, x1, y1, dx2, g2, g3):
    d3, dg3 = _rms_bwd_rows(dh2, x1, g3)
    dx1 = dx2 + d3
    dy1, dg2 = _rms_bwd_rows(dx1, y1, g2)
    return dx1, dy1, dg2, dg3


def _in_bwd_rows(dh1, x0, dx1, g1):
    d1, dg1 = _rms_bwd_rows(dh1, x0, g1)
    return dx1 + d1, dg1


GATE_ROWS = 512


def _group_mean_matrix():
    p = np.zeros((A_WIDTH, A_WIDTH), np.float32)
    for g in range(A_GROUPS):
        p[g * HEAD_DIM:(g + 1) * HEAD_DIM, g * HEAD_DIM:(g + 1) * HEAD_DIM] = 1.0 / HEAD_DIM
    return jnp.asarray(p)


def _group_masks(width=A_WIDTH):
    lane = lax.broadcasted_iota(jnp.int32, (1, width), 1)
    return [(lane >= g * HEAD_DIM) & (lane < (g + 1) * HEAD_DIM) for g in range(width // HEAD_DIM)]


GROUP_SUM_PRECISION = lax.Precision.HIGH


def _layernorm_groups(vg, pavg):
    hi = GROUP_SUM_PRECISION
    mu = _dot(vg, pavg, hi)
    xc = vg - mu
    var = _dot(xc * xc, pavg, hi)
    rstd = lax.rsqrt(var + NORM_EPS)
    return xc * rstd, rstd


def _spatial_mix(w_bf, vn_chunk_bf, masks, bz):
    z = bz
    for g in range(A_GROUPS):
        z = z + jnp.where(masks[g], _dot(w_bf[g], vn_chunk_bf), 0.0)
    return z


def _full_spec(shape):
    return pl.BlockSpec(shape, lambda i: tuple(0 for _ in shape))


def _gate_fwd_rows(u, v, lg, lb, w_ref, bz, pavg, a_ref):
    masks = _group_masks()
    row = lax.broadcasted_iota(jnp.int32, (CHUNK, CHUNK), 0)
    col = lax.broadcasted_iota(jnp.int32, (CHUNK, CHUNK), 1)
    w_bf = [jnp.where(row >= col, w_ref[g], 0.0).astype(BF16) for g in range(A_GROUPS)]
    ug = _gelu(u)
    vhat, _ = _layernorm_groups(_gelu(v), pavg)
    vn = vhat * lg + lb
    for c in range(GATE_ROWS // CHUNK):
        sl = slice(c * CHUNK, (c + 1) * CHUNK)
        z = _spatial_mix(w_bf, vn[sl].astype(BF16), masks, bz)
        a_ref[sl, :] = (ug[sl] * z).astype(BF16)


def _gate_bwd(uv, dmix, ln_g, ln_b, w_s, w_st, bz, grads):
    m = uv.shape[0]
    pavg = _group_mean_matrix()
    nsteps = m // GATE_ROWS
    nx = len(grads)
    shapes = [g.shape for g in grads]

    def body(u_ref, v_ref, da_ref, lg_ref, lb_ref, w_ref, wt_ref, bz_ref, p_ref, *rest):
        grad_refs = rest[:nx]
        duv_ref, dlg_ref, dlb_ref, dw_ref, dbz_ref = rest[nx:nx + 5]
        recv_refs = rest[nx + 5:2 * nx + 5]
        exchange = _SiblingExchangePlan(shapes, grad_refs, recv_refs, *rest[2 * nx + 5:])
        i = pl.program_id(0)

        @pl.when(i == 0)
        def _():
            exchange.start()
            dlg_ref[...] = jnp.zeros_like(dlg_ref)
            dlb_ref[...] = jnp.zeros_like(dlb_ref)
            dw_ref[...] = jnp.zeros_like(dw_ref)
            dbz_ref[...] = jnp.zeros_like(dbz_ref)

        hi = GROUP_SUM_PRECISION
        masks = _group_masks()
        row = lax.broadcasted_iota(jnp.int32, (CHUNK, CHUNK), 0)
        col = lax.broadcasted_iota(jnp.int32, (CHUNK, CHUNK), 1)
        tril = row >= col
        w_bf = [jnp.where(tril, w_ref[g], 0.0).astype(BF16) for g in range(A_GROUPS)]
        wt_bf = [jnp.where(col >= row, wt_ref[g], 0.0).astype(BF16) for g in range(A_GROUPS)]
        pavg_v = p_ref[...]
        lg = lg_ref[...]
        ug, dug = _gelu_and_grad(u_ref[...])
        vg, dvg_dx = _gelu_and_grad(v_ref[...])
        vhat, rstd = _layernorm_groups(vg, pavg_v)
        vn = vhat * lg + lb_ref[...]
        da = da_ref[...]
        bz = bz_ref[...]
        for c in range(GATE_ROWS // CHUNK):
            sl = slice(c * CHUNK, (c + 1) * CHUNK)
            vn_bf = vn[sl].astype(BF16)
            z = _spatial_mix(w_bf, vn_bf, masks, bz)
            dz = da[sl] * ug[sl]
            duv_ref[sl, 0:A_WIDTH] = (da[sl] * z * dug[sl]).astype(BF16)
            dbz_ref[...] += dz
            dz_bf = dz.astype(BF16)
            dvn = jnp.zeros((CHUNK, A_WIDTH), F32)
            for g in range(A_GROUPS):
                dz_g = jnp.where(masks[g], dz, 0.0).astype(BF16)
                dw_ref[g] += jnp.where(tril, _dot_nt(dz_g, vn_bf), 0.0)
                dvn = dvn + jnp.where(masks[g], _dot(wt_bf[g], dz_bf), 0.0)
            vh = vhat[sl]
            dlb_ref[...] += jnp.sum(dvn, axis=0, keepdims=True)
            dlg_ref[...] += jnp.sum(dvn * vh, axis=0, keepdims=True)
            dvh = dvn * lg
            m1 = _dot(dvh, pavg_v, hi)
            m2 = _dot(dvh * vh, pavg_v, hi)
            duv_ref[sl, A_WIDTH:2 * A_WIDTH] = (rstd[sl] * (dvh - m1 - vh * m2) * dvg_dx[sl]).astype(BF16)

        @pl.when(i == nsteps - 1)
        def _():
            dbz_ref[...] = _dot(dbz_ref[...], pavg_v * float(HEAD_DIM), hi)
            exchange.finish()

    res = pl.pallas_call(
        body, grid=(nsteps,),
        in_specs=[pl.BlockSpec((GATE_ROWS, A_WIDTH), lambda i: (i, 0)),
                  pl.BlockSpec((GATE_ROWS, A_WIDTH), lambda i: (i, 1)),
                  pl.BlockSpec((GATE_ROWS, A_WIDTH), lambda i: (i, 0)),
                  _full_spec((1, A_WIDTH)), _full_spec((1, A_WIDTH)), _full_spec((A_GROUPS, CHUNK, CHUNK)),
                  _full_spec((A_GROUPS, CHUNK, CHUNK)), _full_spec((CHUNK, A_WIDTH)),
                  _full_spec((A_WIDTH, A_WIDTH))] + [ANY] * nx,
        out_specs=[pl.BlockSpec((GATE_ROWS, 2 * A_WIDTH), lambda i: (i, 0)),
                   _full_spec((1, A_WIDTH)), _full_spec((1, A_WIDTH)), _full_spec((A_GROUPS, CHUNK, CHUNK)),
                   _full_spec((CHUNK, A_WIDTH))] + [ANY] * nx,
        out_shape=[jax.ShapeDtypeStruct((m, IN_COLS), BF16),
                   jax.ShapeDtypeStruct((1, A_WIDTH), F32), jax.ShapeDtypeStruct((1, A_WIDTH), F32),
                   jax.ShapeDtypeStruct((A_GROUPS, CHUNK, CHUNK), F32),
                   jax.ShapeDtypeStruct((CHUNK, A_WIDTH), F32)]
        + [jax.ShapeDtypeStruct((N_SHARD, s[1] // 2, s[2]), F32) for s in shapes],
        scratch_shapes=_sem_pair(nx),
        compiler_params=_params(("arbitrary",)), name="gate_bwd",
    )(uv, uv, dmix, ln_g, ln_b, w_s, w_st, bz, pavg, *grads)
    return res[:5], list(res[5:])


Q_BLOCK = 128
PAIR = 2 * HEAD_DIM
N_PAIR = B_HEADS // 2
N_CFG = len(DILATED)
BLOCKS_PER_CFG = SEQ // Q_BLOCK
QKV_SLABS = 3 * N_PAIR
FWD_BLOCKS_PER_TRIP = 8
BWD_BLOCKS_PER_TRIP = 4


def _t5_bucket_np(dist, dtype):
    max_exact = NUM_BUCKETS // 2
    d = np.maximum(dist, 1).astype(dtype)
    large = max_exact + (np.log(d / dtype(max_exact)) / dtype(math.log(MAX_DISTANCE / max_exact))
                         * dtype(NUM_BUCKETS - max_exact))
    large = np.minimum(large.astype(np.int32), NUM_BUCKETS - 1)
    return np.where(dist < max_exact, dist, large)


def _bucket_tables():
    i = np.arange(Q_BLOCK)[:, None]
    j = np.arange(Q_BLOCK)[None, :]
    tables = []
    for _, dil in DILATED:
        rel_prev = Q_BLOCK + i - j
        rel_cur = i - j
        rel = np.concatenate([rel_prev, rel_cur], axis=1)
        valid = np.concatenate([rel_prev <= Q_BLOCK, rel_cur >= 0], axis=1)
        dist = np.maximum(rel, 0) * dil
        b32 = _t5_bucket_np(dist, np.float32)
        b64 = _t5_bucket_np(dist, np.float64)
        assert np.array_equal(b32, b64)
        tables.append(np.where(valid, b32, -1).astype(np.int32))
    return np.stack(tables)


def _present_buckets(buckets_np):
    return [sorted(set(int(v) for v in np.unique(buckets_np[c]) if v >= 0)) for c in range(N_CFG)]


def _bias_tables_body(buckets_np):
    present = _present_buckets(buckets_np)

    def tables(rb_ref, bk_ref, o_ref, ot_ref):
        for c in range(N_CFG):
            bk = bk_ref[c]
            for h in range(B_HEADS):
                acc = jnp.full((Q_BLOCK, 2 * Q_BLOCK), NEG_INF, F32)
                for b in present[c]:
                    acc = jnp.where(bk == b, rb_ref[h, b], acc)
                o_ref[c, h] = acc
                ot_ref[c, h] = acc.T

    return tables


def _proj_fwd(x, g1, w_in_t, ln_g, ln_b, w_s, bz):
    m = x.shape[0]
    tm = GATE_ROWS
    pavg = _group_mean_matrix()

    def body(x_ref, g_ref, w_ref, lg_ref, lb_ref, ws_ref, bz_ref, p_ref, h_ref, uv_ref, qkv_ref, a_ref):
        xv = x_ref[...]
        h = (xv * _rstd(xv) * g_ref[...]).astype(BF16)
        h_ref[...] = h
        acc = _dot_nt(h, w_ref[...])
        uv_ref[...] = acc[:, :2 * A_WIDTH]
        for s in range(QKV_SLABS):
            qkv_ref[s] = acc[:, 2 * A_WIDTH + s * PAIR:2 * A_WIDTH + (s + 1) * PAIR]
        _gate_fwd_rows(acc[:, :A_WIDTH], acc[:, A_WIDTH:2 * A_WIDTH], lg_ref[...], lb_ref[...], ws_ref,
                       bz_ref[...], p_ref[...], a_ref)

    return pl.pallas_call(
        body, grid=(m // tm,),
        in_specs=[pl.BlockSpec((tm, D_MODEL), lambda i: (i, 0)), _vec_spec(),
                  pl.BlockSpec((IN_COLS, D_MODEL), lambda i: (0, 0)),
                  _full_spec((1, A_WIDTH)), _full_spec((1, A_WIDTH)), _full_spec((A_GROUPS, CHUNK, CHUNK)),
                  _full_spec((CHUNK, A_WIDTH)), _full_spec((A_WIDTH, A_WIDTH))],
        out_specs=[pl.BlockSpec((tm, D_MODEL), lambda i: (i, 0)),
                   pl.BlockSpec((tm, 2 * A_WIDTH), lambda i: (i, 0)),
                   pl.BlockSpec((QKV_SLABS, tm, PAIR), lambda i: (0, i, 0)),
                   pl.BlockSpec((tm, A_WIDTH), lambda i: (i, 0))],
        out_shape=[jax.ShapeDtypeStruct((m, D_MODEL), BF16), jax.ShapeDtypeStruct((m, 2 * A_WIDTH), F32),
                   jax.ShapeDtypeStruct((QKV_SLABS, m, PAIR), F32), jax.ShapeDtypeStruct((m, A_WIDTH), BF16)],
        compiler_params=_params(("parallel",)), name="proj_fwd",
    )(x, g1, w_in_t, ln_g, ln_b, w_s, bz, pavg)


def _pair_masks():
    lane = lax.broadcasted_iota(jnp.int32, (1, PAIR), 1)
    return [lane < HEAD_DIM, lane >= HEAD_DIM]


def _block_rows(idx, dil):
    static = isinstance(idx, int)
    r, n = idx % dil, idx // dil

    def rows_of(block):
        start = r + (dil * Q_BLOCK) * block
        if dil == 1:
            return pl.ds(start if static else pl.multiple_of(start, Q_BLOCK), Q_BLOCK)
        return pl.ds(start, Q_BLOCK, stride=dil)

    prev = rows_of(n - 1) if not static or n > 0 else None
    return rows_of(n), prev


def _attn_fwd(qkv, bias, batch, owns):
    m = qkv.shape[1]
    comb_rows = 256
    nt = len(owns)
    shapes = [g.shape[1:] for g in owns]
    n_steps = batch * N_PAIR
    ts = list(range(nt))

    def body(q_ref, k_ref, v_ref, b_ref, *rest):
        o_ref, l_ref = rest[nt:nt + 2]
        gat_refs = rest[nt + 2:2 * nt + 2]
        scratch = rest[2 * nt + 2:]
        oc_refs, lc_refs = scratch[:N_CFG], scratch[N_CFG:2 * N_CFG]
        step = pl.program_id(0) * N_PAIR + pl.program_id(1)
        gather = _RelayGatherPlan(shapes, None, gat_refs, *scratch[2 * N_CFG:])

        @pl.when(step == 0)
        def _():
            gather.start(ts)

        @pl.when(step == n_steps // 2)
        def _():
            gather.relay(ts)

        @pl.when(step == n_steps - 2)
        def _():
            gather.forward(ts)

        masks = _pair_masks()
        for ci, (_, dil) in enumerate(DILATED):
            nb = SEQ // dil // Q_BLOCK

            def block(trip, ci=ci, dil=dil, nb=nb):
                work = []
                for u in range(FWD_BLOCKS_PER_TRIP):
                    rows, prow = _block_rows(trip * FWD_BLOCKS_PER_TRIP + u, dil)
                    has_prev = nb > 1 and prow is not None
                    q = q_ref[rows, :] * 0.125
                    kc = k_ref[rows, :].astype(BF16)
                    vc = v_ref[rows, :]
                    kp = k_ref[prow, :].astype(BF16) if has_prev else None
                    vp = v_ref[prow, :] if has_prev else None
                    tiles = []
                    for h in range(2):
                        qh = jnp.where(masks[h], q, 0.0).astype(BF16)
                        sc = _dot_nt(qh, kc) + b_ref[ci, h, :, Q_BLOCK:]
                        sp = _dot_nt(qh, kp) + b_ref[ci, h, :, :Q_BLOCK] if has_prev else None
                        tiles.append((sc, sp))
                    work.append((rows, vc, vp, tiles))
                probs = []
                for _, _, _, tiles in work:
                    ps = []
                    for sc, sp in tiles:
                        mx = jnp.max(sc if sp is None else jnp.maximum(sc, sp), axis=1, keepdims=True)
                        pc = jnp.exp(sc - mx).astype(BF16)
                        pp = None if sp is None else jnp.exp(sp - mx).astype(BF16)
                        ps.append((mx, pc, pp))
                    probs.append(ps)
                for (rows, vc, vp, _), ps in zip(work, probs):
                    res = []
                    for h, (_, pc, pp) in enumerate(ps):
                        r = _dot(pc, jnp.where(masks[h], vc, 1.0).astype(BF16))
                        if pp is not None:
                            r = r + _dot(pp, jnp.where(masks[h], vp, 1.0).astype(BF16))
                        res.append(r)
                    num = jnp.where(masks[0], res[0], res[1])
                    den = pltpu.roll(jnp.where(masks[0], res[1], res[0]), HEAD_DIM, 1)
                    oc_refs[ci][rows, :] = num / den
                    lc_refs[ci][rows, :] = jnp.where(masks[0], ps[0][0], ps[1][0]) + jnp.log(den)

            for trip in range(BLOCKS_PER_CFG // FWD_BLOCKS_PER_TRIP):
                block(trip)

        def combine(i, carry):
            rr = pl.ds(pl.multiple_of(i * comb_rows, comb_rows), comb_rows)
            ls = [lc_refs[c][rr, :] for c in range(N_CFG)]
            mx = functools.reduce(jnp.maximum, ls)
            ws = [jnp.exp(l - mx) for l in ls]
            tot = functools.reduce(lambda a, b: a + b, ws)
            o = functools.reduce(lambda a, b: a + b, [ws[c] * oc_refs[c][rr, :] for c in range(N_CFG)]) / tot
            o_ref[rr, :] = o.astype(BF16)
            l_ref[rr, :] = mx + jnp.log(tot)
            return carry

        lax.fori_loop(0, SEQ // comb_rows, combine, 0)

        @pl.when(step == n_steps - 1)
        def _():
            gather.finish(ts)

    def slab(first):
        return pl.BlockSpec((None, SEQ, PAIR), lambda b, p: (first + p, b, 0))

    nat = pl.BlockSpec((SEQ, PAIR), lambda b, p: (b, p))
    res = pl.pallas_call(
        body, grid=(batch, N_PAIR),
        in_specs=[slab(0), slab(N_PAIR), slab(2 * N_PAIR),
                  pl.BlockSpec((N_CFG, 2, Q_BLOCK, 2 * Q_BLOCK), lambda b, p: (0, p, 0, 0))] + [ANY] * nt,
        out_specs=[nat, nat] + [ANY] * nt,
        out_shape=[jax.ShapeDtypeStruct((m, B_WIDTH), BF16), jax.ShapeDtypeStruct((m, B_WIDTH), F32)]
        + [jax.ShapeDtypeStruct(g.shape, g.dtype) for g in owns],
        scratch_shapes=[pltpu.VMEM((SEQ, PAIR), F32)] * (2 * N_CFG) + _sem_pair(6 * nt),
        input_output_aliases={4 + t: 2 + t for t in range(nt)},
        compiler_params=_params(("arbitrary", "arbitrary")), name="attn_fwd",
    )(qkv, qkv, qkv, bias, *owns)
    return res[0], res[1], list(res[2:])


def _attn_bwd(qkv, dmix, o, lse, bias_t, dproj, batch, parts, smalls):
    m = qkv.shape[1]
    nt, ns = len(parts), len(smalls)
    n_steps = N_PAIR * batch

    def body(q_ref, k_ref, v_ref, do_ref, o_ref, l_ref, b_ref, *rest):
        part_refs = rest[1:nt + 1]
        small_refs = rest[nt + 1:nt + 1 + ns]
        pos = nt + 1 + ns
        dproj_ref, ds_ref = rest[pos:pos + 2]
        recv_refs = rest[pos + 2:pos + 2 + nt]
        sum_refs = rest[pos + 2 + nt:pos + 2 + nt + ns]
        pos += 2 + nt + ns
        dq_acc, dk_acc, dv_acc, d_scr, stage, stage_sems, send_sems, recv_sems = rest[pos:pos + 8]
        allreduce = _SmallAllReducePlan(small_refs, sum_refs, rest[pos + 8:pos + 8 + ns],
                                        rest[pos + 8 + ns:pos + 8 + 2 * ns], *rest[pos + 8 + 2 * ns:])
        pair, seq = pl.program_id(0), pl.program_id(1)
        step = pair * batch + seq
        exchange = _ChipExchangePlan(part_refs, recv_refs, send_sems, recv_sems)

        @pl.when(step == 0)
        def _():
            allreduce.start_sibling()

        @pl.when(step == n_steps // 2)
        def _():
            allreduce.sum_sibling_and_start_chips()

        def stage_copies():
            rows = pl.ds(pl.multiple_of(seq * SEQ, SEQ), SEQ)
            return [pltpu.make_async_copy(
                stage.at[k],
                dproj_ref.at[rows, pl.ds(pl.multiple_of(2 * A_WIDTH + k * B_WIDTH + pair * PAIR, PAIR), PAIR)],
                stage_sems.at[k]) for k in range(3)]

        @pl.when(step == 0)
        def _():
            exchange.start()

        @pl.when(pl.program_id(1) == 0)
        def _():
            ds_ref[...] = jnp.zeros_like(ds_ref)

        dq_acc[...] = jnp.zeros_like(dq_acc)
        dk_acc[...] = jnp.zeros_like(dk_acc)
        dv_acc[...] = jnp.zeros_like(dv_acc)
        d_scr[...] = do_ref[...] * o_ref[...].astype(F32)
        masks = _pair_masks()

        def stack_heads(t):
            return jnp.concatenate([jnp.where(masks[0], t, 0.0), jnp.where(masks[1], t, 0.0)], axis=0).astype(BF16)

        for ci, (_, dil) in enumerate(DILATED):
            nb = SEQ // dil // Q_BLOCK

            def block(trip, carry, ci=ci, dil=dil, nb=nb):
                first = []
                for u in range(BWD_BLOCKS_PER_TRIP):
                    rows, prow = _block_rows(trip * BWD_BLOCKS_PER_TRIP + u, dil)
                    has_prev = nb > 1 and prow is not None
                    if has_prev:
                        kcat = jnp.concatenate([k_ref[prow, :], k_ref[rows, :]], axis=0).astype(BF16)
                        vcat = jnp.concatenate([v_ref[prow, :], v_ref[rows, :]], axis=0).astype(BF16)
                    else:
                        kcat = k_ref[rows, :].astype(BF16)
                        vcat = v_ref[rows, :].astype(BF16)
                    qst = stack_heads(q_ref[rows, :] * 0.125)
                    dost = stack_heads(do_ref[rows, :])
                    lt = l_ref[rows, :].T
                    dt = d_scr[rows, :].T
                    lrow = jnp.concatenate([lt[0:1], lt[HEAD_DIM:HEAD_DIM + 1]], axis=1)
                    drow = jnp.concatenate([jnp.sum(dt[:HEAD_DIM], axis=0, keepdims=True),
                                            jnp.sum(dt[HEAD_DIM:], axis=0, keepdims=True)], axis=1)
                    first.append((has_prev, rows, prow, kcat, qst, dost, lrow, drow,
                                  _dot_nt(kcat, qst), _dot_nt(vcat, dost)))
                second = []
                for has_prev, rows, prow, kcat, qst, dost, lrow, drow, st, dpt in first:
                    keys = slice(0, 2 * Q_BLOCK) if has_prev else slice(Q_BLOCK, 2 * Q_BLOCK)
                    bt = jnp.concatenate([b_ref[ci, 0, keys, :], b_ref[ci, 1, keys, :]], axis=1)
                    pt = jnp.exp(st + bt - lrow)
                    dst = pt * (dpt - drow)
                    ds_ref[ci, 0, keys, :] += dst[:, :Q_BLOCK]
                    ds_ref[ci, 1, keys, :] += dst[:, Q_BLOCK:]
                    second.append((has_prev, rows, prow, kcat, qst, dost, pt.astype(BF16), dst.astype(BF16)))
                for has_prev, rows, prow, kcat, qst, dost, pt_bf, dst_bf in second:
                    dk = _dot(dst_bf, qst)
                    dv = _dot(pt_bf, dost)
                    dq2 = _dot_tn(dst_bf, kcat)
                    dq_acc[rows, :] += jnp.where(masks[0], dq2[:Q_BLOCK], dq2[Q_BLOCK:]) * 0.125
                    if has_prev:
                        dk_acc[prow, :] += dk[:Q_BLOCK]
                        dv_acc[prow, :] += dv[:Q_BLOCK]
                        dk_acc[rows, :] += dk[Q_BLOCK:]
                        dv_acc[rows, :] += dv[Q_BLOCK:]
                    else:
                        dk_acc[rows, :] += dk
                        dv_acc[rows, :] += dv
                return carry

            for trip in range(BLOCKS_PER_CFG // BWD_BLOCKS_PER_TRIP):
                block(trip, 0)

        @pl.when(step > 0)
        def _():
            for cp in stage_copies():
                cp.wait()

        stage[0] = dq_acc[...].astype(BF16)
        stage[1] = dk_acc[...].astype(BF16)
        stage[2] = dv_acc[...].astype(BF16)
        for cp in stage_copies():
            cp.start()

        @pl.when(step == n_steps - 1)
        def _():
            for cp in stage_copies():
                cp.wait()
            exchange.finish()
            allreduce.finish()

    def slab(first):
        return pl.BlockSpec((None, SEQ, PAIR), lambda p, b: (first + p, b, 0))

    nat = pl.BlockSpec((SEQ, PAIR), lambda p, b: (b, p))
    tbl = pl.BlockSpec((N_CFG, 2, 2 * Q_BLOCK, Q_BLOCK), lambda p, b: (0, p, 0, 0))
    acc = pltpu.VMEM((SEQ, PAIR), F32)
    vm = pl.BlockSpec(memory_space=pltpu.VMEM)
    res = pl.pallas_call(
        body, grid=(N_PAIR, batch),
        in_specs=[slab(0), slab(N_PAIR), slab(2 * N_PAIR),
                  pl.BlockSpec((SEQ, PAIR), lambda p, b: (b, A_WIDTH // PAIR + p)), nat, nat, tbl]
        + [ANY] * (nt + 1) + [vm] * ns,
        out_specs=[ANY, tbl] + [ANY] * nt + [vm] * ns,
        out_shape=[jax.ShapeDtypeStruct(dproj.shape, dproj.dtype),
                   jax.ShapeDtypeStruct((N_CFG, B_HEADS, 2 * Q_BLOCK, Q_BLOCK), F32)]
        + [jax.ShapeDtypeStruct((3,) + p.shape[1:], p.dtype) for p in parts]
        + [jax.ShapeDtypeStruct(a.shape, F32) for a in smalls],
        input_output_aliases={7: 0},
        scratch_shapes=[acc, acc, acc, acc, pltpu.VMEM((3, SEQ, PAIR), BF16), pltpu.SemaphoreType.DMA((3,))]
        + _sem_pair(3 * nt) + _SmallAllReducePlan.scratch(smalls),
        compiler_params=_params(("arbitrary", "arbitrary")), name="attn_bwd",
    )(qkv, qkv, qkv, dmix, o, lse, bias_t, dproj, *parts, *smalls)
    return res[0], res[1], list(res[2:2 + nt]), list(res[2 + nt:])


def _rel_bias_grad(ds, buckets_np, grads):
    present = _present_buckets(buckets_np)
    nx = len(grads)
    shapes = [g.shape for g in grads]

    def body(bk_ref, ds_ref, *rest):
        o_ref = rest[nx]
        acc_ref = rest[2 * nx + 1]
        exchange = _SiblingExchangePlan(shapes, rest[:nx], rest[nx + 1:2 * nx + 1], *rest[2 * nx + 2:])
        exchange.start()
        acc_ref[...] = jnp.zeros_like(acc_ref)
        for c in range(N_CFG):
            bk = bk_ref[c]
            for h in range(B_HEADS):
                dsv = ds_ref[c, h]
                for b in present[c]:
                    part = jnp.sum(jnp.where(bk == b, dsv, 0.0), axis=0, keepdims=True)
                    acc_ref[pl.ds(h * NUM_BUCKETS + b, 1), :] += part
        o_ref[...] = jnp.sum(acc_ref[...], axis=1, keepdims=True)
        exchange.finish()

    vm = pl.BlockSpec(memory_space=pltpu.VMEM)
    res = pl.pallas_call(
        body, in_specs=[vm, vm] + [ANY] * nx, out_specs=[vm] + [ANY] * nx,
        out_shape=[jax.ShapeDtypeStruct((B_HEADS * NUM_BUCKETS, 1), F32)]
        + [jax.ShapeDtypeStruct((N_SHARD, s[1] // 2, s[2]), F32) for s in shapes],
        scratch_shapes=[pltpu.VMEM((B_HEADS * NUM_BUCKETS, buckets_np.shape[-1]), F32)] + _sem_pair(nx),
        compiler_params=_params(), name="rel_bias_grad",
    )(jnp.asarray(buckets_np), ds, *grads)
    return res[0], list(res[1:])


def _row_index():
    return lax.broadcasted_iota(jnp.int32, (SEQ, LANE_BLOCK), 0)


def _shift_down(x, k, row):
    return jnp.where(row >= k, pltpu.roll(x, k, 0), 0.0)


def _shift_up(x, k, row):
    return jnp.where(row < SEQ - k, pltpu.roll(x, SEQ - k, 0), 0.0)


def _convgate_fwd(gate, up, conv_w, conv_b, batch):
    m = gate.shape[0]

    def body(g_ref, u_ref, w_ref, b_ref, a_ref):
        g = g_ref[...].astype(F32)
        w = w_ref[...]
        row = _row_index()
        c = b_ref[...] + w[0:1] * _shift_down(g, 2, row) + w[1:2] * _shift_down(g, 1, row) + w[2:3] * g
        a_ref[...] = (_gelu(c) * u_ref[...].astype(F32)).astype(BF16)

    blk = pl.BlockSpec((SEQ, LANE_BLOCK), lambda b, j: (b, j))
    return pl.pallas_call(
        body, grid=(batch, D_FF // LANE_BLOCK),
        in_specs=[blk, blk, pl.BlockSpec((3, LANE_BLOCK), lambda b, j: (0, j)),
                  pl.BlockSpec((1, LANE_BLOCK), lambda b, j: (0, j))],
        out_specs=blk,
        out_shape=jax.ShapeDtypeStruct((m, D_FF), BF16),
        compiler_params=_params(("parallel", "parallel")), name="convgate_fwd",
    )(gate, up, conv_w, conv_b)


def _convgate_bwd(gate, up, dact, conv_w, conv_b, batch):
    m = gate.shape[0]

    def body(g_ref, u_ref, da_ref, w_ref, b_ref, dg_ref, du_ref, dw_ref, db_ref):
        @pl.when(pl.program_id(1) == 0)
        def _():
            dw_ref[...] = jnp.zeros_like(dw_ref)
            db_ref[...] = jnp.zeros_like(db_ref)

        g = g_ref[...].astype(F32)
        w = w_ref[...]
        row = _row_index()
        g1 = _shift_down(g, 1, row)
        g2 = _shift_down(g, 2, row)
        c = b_ref[...] + w[0:1] * g2 + w[1:2] * g1 + w[2:3] * g
        gg, dgg = _gelu_and_grad(c)
        da = da_ref[...].astype(F32)
        du_ref[...] = (da * gg).astype(BF16)
        dc = da * u_ref[...].astype(F32) * dgg
        db_ref[...] += jnp.sum(dc, axis=0, keepdims=True)
        dw_ref[0:1, :] += jnp.sum(dc * g2, axis=0, keepdims=True)
        dw_ref[1:2, :] += jnp.sum(dc * g1, axis=0, keepdims=True)
        dw_ref[2:3, :] += jnp.sum(dc * g, axis=0, keepdims=True)
        dg_ref[...] = (w[2:3] * dc + w[1:2] * _shift_up(dc, 1, row) + w[0:1] * _shift_up(dc, 2, row)).astype(BF16)

    blk = pl.BlockSpec((SEQ, LANE_BLOCK), lambda j, b: (b, j))
    wspec = pl.BlockSpec((3, LANE_BLOCK), lambda j, b: (0, j))
    bspec = pl.BlockSpec((1, LANE_BLOCK), lambda j, b: (0, j))
    return pl.pallas_call(
        body, grid=(D_FF // LANE_BLOCK, batch),
        in_specs=[blk, blk, blk, wspec, bspec],
        out_specs=[blk, blk, wspec, bspec],
        out_shape=[jax.ShapeDtypeStruct((m, D_FF), BF16), jax.ShapeDtypeStruct((m, D_FF), BF16),
                   jax.ShapeDtypeStruct((3, D_FF), F32), jax.ShapeDtypeStruct((1, D_FF), F32)],
        compiler_params=_params(("parallel", "arbitrary")), name="convgate_bwd",
    )(gate, up, dact, conv_w, conv_b)


def _gather_weights(shards, conv_w_shard, rel_bias, buckets_np):
    nt = len(shards)
    shapes = [sh.shape for sh in shards]
    ts = list(range(nt))
    tables = _bias_tables_body(buckets_np)

    def body(*refs):
        shard_refs = refs[:nt]
        cw_ref, rb_ref, bk_ref = refs[nt:nt + 3]
        out_refs = refs[nt + 3:2 * nt + 3]
        cw_out, bias_ref, bias_t_ref = refs[2 * nt + 3:2 * nt + 6]
        scratch = refs[2 * nt + 6:]
        f32_refs, bf16_refs = scratch[:nt], scratch[nt:2 * nt]
        load_sems, store_sems, send_sems, recv_sems, cw_send, cw_recv = scratch[2 * nt:]
        plan = _RelayGatherPlan(shapes[:1], bf16_refs[:1], out_refs[:1], send_sems, recv_sems)
        x, y, c, chips = _mesh_pos()
        loads = [pltpu.make_async_copy(shard_refs[t], f32_refs[t], load_sems.at[t]) for t in ts]
        stores = [pltpu.make_async_copy(bf16_refs[t], out_refs[t].at[2 * x + y], store_sems.at[t]) for t in ts]
        stores.append(pltpu.make_async_copy(cw_ref, cw_out.at[2 * x + y], store_sems.at[nt]))

        def cw_copy(j, src, dst, chip):
            return pltpu.make_async_remote_copy(src_ref=src, dst_ref=dst, send_sem=cw_send.at[j],
                                                recv_sem=cw_recv.at[j], device_id=(*chip, c), device_id_type=MESH)

        def to_bf16(t):
            loads[t].wait()
            bf16_refs[t][...] = f32_refs[t][...].astype(BF16)
            stores[t].start()

        for cp in loads:
            cp.start()
        to_bf16(0)
        plan.start([0])
        cw_sends = [cw_copy(j, cw_ref, cw_out.at[2 * x + y], chip) for j, chip in enumerate(chips)]
        for cp in cw_sends + stores[nt:]:
            cp.start()
        for t in ts[1:]:
            to_bf16(t)
        plan.relay([0])
        tables(rb_ref, bk_ref, bias_ref, bias_t_ref)
        plan.forward([0])
        for j, chip in enumerate(chips):
            dst = cw_out.at[2 * chip[0] + chip[1]]
            cw_copy(j, dst, dst, chip).wait_recv()
        plan.finish([0])
        for cp in cw_sends:
            cp.wait_send()
        for cp in stores:
            cp.wait()

    out_shape = [jax.ShapeDtypeStruct((N_SHARD,) + sh.shape, BF16) for sh in shards]
    out_shape.append(jax.ShapeDtypeStruct((N_SHARD,) + conv_w_shard.shape, conv_w_shard.dtype))
    out_shape += [jax.ShapeDtypeStruct((N_CFG, B_HEADS, Q_BLOCK, 2 * Q_BLOCK), F32),
                  jax.ShapeDtypeStruct((N_CFG, B_HEADS, 2 * Q_BLOCK, Q_BLOCK), F32)]
    vm = pl.BlockSpec(memory_space=pltpu.VMEM)
    res = pl.pallas_call(
        body, in_specs=[ANY] * (nt + 1) + [pl.BlockSpec(memory_space=pltpu.SMEM), vm],
        out_specs=[ANY] * (nt + 1) + [vm, vm], out_shape=out_shape,
        scratch_shapes=[pltpu.VMEM(sh.shape, F32) for sh in shards] + [pltpu.VMEM(sh.shape, BF16) for sh in shards]
        + [pltpu.SemaphoreType.DMA((nt,)), pltpu.SemaphoreType.DMA((nt + 1,))] + _sem_pair(6) + _sem_pair(3),
        compiler_params=pltpu.CompilerParams(has_side_effects=True, vmem_limit_bytes=VMEM_LIMIT),
        name="gather_weights",
    )(*shards, conv_w_shard, rel_bias.T, jnp.asarray(buckets_np))
    return list(res[:nt + 1]), res[nt + 1], res[nt + 2]


def _turn(t, u, s, last):
    return jnp.where(t == u, s, jnp.where(t > u, last, 0))


def _add_halves(gs, recvs, c_idx):
    n = len(gs)
    _, rows2, cols = gs[0].shape
    rows = rows2 // 2
    assert all(g.shape == gs[0].shape for g in gs)

    def body(c_ref, *refs):
        t = pl.program_id(0)
        for u in range(n):
            @pl.when(t == u)
            def _(u=u):
                refs[2 * n + u][...] = (refs[u][...] + refs[n + u][...]).astype(BF16)

    def own(u):
        return pl.BlockSpec((None, None, rows, cols), lambda t, s, c: (_turn(t, u, s, N_SHARD - 1), c[0], 0, 0))

    def plain(u):
        return pl.BlockSpec((None, rows, cols), lambda t, s, c: (_turn(t, u, s, N_SHARD - 1), 0, 0))

    return pl.pallas_call(
        body,
        grid_spec=pltpu.PrefetchScalarGridSpec(
            num_scalar_prefetch=1, grid=(n, N_SHARD),
            in_specs=[own(u) for u in range(n)] + [plain(u) for u in range(n)],
            out_specs=[plain(u) for u in range(n)]),
        out_shape=[jax.ShapeDtypeStruct((N_SHARD, rows, cols), BF16)] * n,
        compiler_params=_params(("arbitrary", "arbitrary")), name="rs_add_halves",
    )(c_idx, *[g.reshape(N_SHARD, 2, rows, cols) for g in gs], *recvs)


def _add_chips(parts, recvs, s_idx, c_idx):
    n = len(parts)
    _, rows, cols = parts[0].shape
    assert all(p.shape == parts[0].shape for p in parts)

    def body(idx_ref, *refs):
        t = pl.program_id(0)
        for u in range(n):
            @pl.when(t == u)
            def _(u=u):
                acc = refs[u][...].astype(F32)
                for j in range(3):
                    acc = acc + refs[n + u][j].astype(F32)
                refs[2 * n + u][...] = acc

    res = pl.pallas_call(
        body,
        grid_spec=pltpu.PrefetchScalarGridSpec(
            num_scalar_prefetch=1, grid=(n,),
            in_specs=[pl.BlockSpec((None, rows, cols), lambda t, idx: (idx[0], 0, 0))] * n
            + [pl.BlockSpec((3, rows, cols), lambda t, idx: (0, 0, 0))] * n,
            out_specs=[pl.BlockSpec((None, rows, cols), lambda t, idx: (idx[1], 0, 0))] * n),
        out_shape=[jax.ShapeDtypeStruct((2, rows, cols), F32)] * n,
        compiler_params=_params(("arbitrary",)), name="rs_add_chips",
    )(jnp.concatenate([s_idx, c_idx]), *parts, *recvs)
    return [r.reshape(2 * rows, cols) for r in res]


def _finish_reductions(fulls, arrays):
    nt, n = len(fulls), len(arrays)

    def body(*refs):
        in_refs = refs[nt:nt + n]
        full_refs, out_refs = refs[nt + n:2 * nt + n], refs[2 * nt + n:2 * nt + 2 * n]
        pos = 2 * nt + 2 * n
        share_send, share_recv = refs[pos + 2 * n:pos + 2 * n + 2]
        allreduce = _SmallAllReducePlan(in_refs, out_refs, refs[pos:pos + n], refs[pos + n:pos + 2 * n],
                                        *refs[pos + 2 * n + 2:])
        x, y, c, _ = _mesh_pos()

        def half(t, which):
            rows = fulls[t].shape[0] // 2
            return full_refs[t].at[pl.ds(which * rows, rows), :]

        def share(t, which):
            return pltpu.make_async_remote_copy(
                src_ref=half(t, which), dst_ref=half(t, which), send_sem=share_send.at[t],
                recv_sem=share_recv.at[t], device_id=(x, y, 1 - c), device_id_type=MESH)

        for t in range(nt):
            share(t, c).start()
        allreduce.start_sibling()
        allreduce.sum_sibling_and_start_chips()
        allreduce.finish()
        for t in range(nt):
            share(t, 1 - c).wait_recv()
        for t in range(nt):
            share(t, c).wait_send()

    vm = pl.BlockSpec(memory_space=pltpu.VMEM)
    res = pl.pallas_call(
        body, in_specs=[ANY] * nt + [vm] * n, out_specs=[ANY] * nt + [vm] * n,
        out_shape=[jax.ShapeDtypeStruct(f.shape, f.dtype) for f in fulls]
        + [jax.ShapeDtypeStruct(a.shape, F32) for a in arrays],
        input_output_aliases={t: t for t in range(nt)},
        scratch_shapes=[pltpu.VMEM(a.shape, F32) for a in arrays] + [pltpu.VMEM((3,) + a.shape, F32) for a in arrays]
        + _sem_pair(nt) + _sem_pair(4 * n),
        compiler_params=pltpu.CompilerParams(has_side_effects=True),
        name="finish_reductions",
    )(*fulls, *arrays)
    return list(res[:nt]), list(res[nt:])


def _from_col_shards(g):
    n, rows, cols = g.shape
    return g.transpose(1, 0, 2).reshape(rows, n * cols)


def _train_step(x, tgt, g1, g2, g3, g4, shards, ln_g, ln_b, w_s, b_s, rel_bias, conv_w_shard, conv_b, batch,
                s_idx, c_idx):
    buckets = _bucket_tables()
    bz = jnp.repeat(b_s.T, HEAD_DIM, axis=1)
    w_st = jnp.swapaxes(w_s, 1, 2)

    def shard_major(g):
        return g.reshape(N_SHARD, g.shape[0] // N_SHARD, D_MODEL)

    names = ["w_in", "w_out", "w_gate", "w_up", "w_down"]
    (g_in, g_out, g_gate, g_up, g_down, g_convw), bias, bias_t = _gather_weights(
        [shards[n] for n in names], conv_w_shard, rel_bias, buckets)
    w_in_t = g_in.reshape(IN_COLS, D_MODEL)
    conv_w = _from_col_shards(g_convw.reshape(N_SHARD, 3, SHARD_FF))

    h1, uv, qkv, a = _proj_fwd(x, g1, w_in_t, ln_g, ln_b, w_s, bz)
    o_bf, lse, (g_out, g_gate, g_up, g_down) = _attn_fwd(qkv, bias, batch, [g_out, g_gate, g_up, g_down])
    w_out = g_out.reshape(D_MODEL, D_MODEL)
    w_gate_t = g_gate.reshape(D_FF, D_MODEL)
    w_up_t = g_up.reshape(D_FF, D_MODEL)
    (y1, x1, h2), _ = _fused_rows(
        "out_proj_mid_fwd", 512,
        [(a, w_out, "nn", slice(0, A_WIDTH)), (o_bf, w_out, "nn", slice(A_WIDTH, D_MODEL))],
        [x], [g2, g3], _mid_fwd_rows, [F32, F32, BF16], [])
    gate, up = _mm_pair_nt(h2, w_gate_t, w_up_t, tm=1024, tn=1408, out_dtype=BF16, name="mm_gate_up")
    w_down = g_down.reshape(D_FF, D_MODEL)
    act = _convgate_fwd(gate, up, conv_w, conv_b, batch)
    (dx2, dy2, dg4, loss), _ = _fused_rows(
        "down_proj_loss_head", 512, [(act, w_down, "nn", None)], [x1, tgt], [g4], _loss_head_rows,
        [F32, BF16], [(1, D_MODEL), (1, 128)])

    dact = _mm(dy2, w_down, dims="nt", tm=1024, tn=1408, tk=1024, out_dtype=BF16, name="mm_dact")
    dw_down = _mm(act, dy2, dims="tn", tm=1408, tn=1024, tk=1024, out_dtype=F32, name="mm_dw_down")
    dgate, dup, dconv_w, dconv_b = _convgate_bwd(gate, up, dact, conv_w, conv_b, batch)
    (dx1, dy1, dg2, dg3), _ = _fused_rows(
        "dh2_mid_bwd", 256, [(dgate, w_gate_t, "nn", None), (dup, w_up_t, "nn", None)],
        [x1, y1, dx2], [g2, g3], _mid_bwd_rows, [F32, BF16], [(1, D_MODEL), (1, D_MODEL)])
    dw_gate_t, dw_up_t = _mm_pair_tn(dgate, dup, h2, tm=1408, tk=1024, name="mm_dw_gate_up")
    done = [shard_major(g) for g in (dw_down, dw_gate_t, dw_up_t)]
    (dmix, dw_out), recv_a = _out_proj_bwd(a, o_bf, dy1, w_out, done[:2])
    done.append(shard_major(dw_out))
    (dproj, dln_g, dln_b, dw_s, dbz), recv_b = _gate_bwd(uv, dmix, ln_g, ln_b, w_s, w_st, bz, done[2:])
    recv_a += recv_b
    parts = _add_halves(done[:3], recv_a[:3], c_idx) + _add_halves(done[3:], recv_a[3:], c_idx)
    early = dict(loss=loss, norm_mix_post=dg2, norm_ffn_pre=dg3, norm_ffn_post=dg4, ln_v_gain=dln_g,
                 ln_v_bias=dln_b, spatial_w=dw_s, spatial_b=dbz, conv_w=dconv_w, conv_b=dconv_b)
    dproj, ds, recv, early_sums = _attn_bwd(qkv, dmix, o_bf, lse, bias_t, dproj, batch, parts, list(early.values()))
    fulls = _add_chips(parts[:3], recv[:3], s_idx, c_idx) + _add_chips(parts[3:], recv[3:], s_idx, c_idx)
    dw_in_t = _mm(dproj, h1, dims="tn", tm=1408, tn=1024, tk=1024, out_dtype=F32, name="mm_dw_in")
    last = [shard_major(dw_in_t)]
    drel, recv_in_a = _rel_bias_grad(ds, np.ascontiguousarray(np.swapaxes(buckets, 1, 2)), last)
    part_in = _add_halves(last, recv_in_a, c_idx)
    (dx0, dg1), recv_in = _fused_rows(
        "dh1_in_bwd", 512, [(dproj, w_in_t, "nn", None)], [x, dx1], [g1], _in_bwd_rows,
        [F32], [(1, D_MODEL)], exchange=part_in)
    fulls += _add_chips(part_in, recv_in, s_idx, c_idx)
    half_reduced = dict(zip(["w_down", "w_gate", "w_up", "w_out", "w_in"], fulls))

    return dx0, dict(zip(early, early_sums)), dict(norm_mix_pre=dg1, rel_bias=drel), half_reduced


def _adamw_update(w, g, m, v):
    nm = ADAM_B1 * m + (1.0 - ADAM_B1) * g
    nv = ADAM_B2 * v + (1.0 - ADAM_B2) * (g * g)
    m_hat = nm / (1.0 - ADAM_B1 ** ADAM_STEP)
    v_hat = nv / (1.0 - ADAM_B2 ** ADAM_STEP)
    return -ADAM_LR * (m_hat / (jnp.sqrt(v_hat) + ADAM_EPS) + ADAM_WD * w), nm, nv


def _adamw(w, g, m, v, name):
    rows, cols = w.shape
    tr = next(cand for cand in (352, 256, 128) if rows % cand == 0)

    def body(w_ref, g_ref, m_ref, v_ref, go_ref, d_ref, nm_ref, nv_ref):
        gv = g_ref[...]
        go_ref[...] = gv
        d_ref[...], nm_ref[...], nv_ref[...] = _adamw_update(w_ref[...], gv, m_ref[...], v_ref[...])

    spec = pl.BlockSpec((tr, cols), lambda i: (i, 0))
    sds = jax.ShapeDtypeStruct((rows, cols), F32)
    return pl.pallas_call(
        body, grid=(rows // tr,), in_specs=[spec] * 4, out_specs=[spec] * 4, out_shape=[sds] * 4,
        compiler_params=_params(("parallel",)), name=name,
    )(w, g, m, v)


def _adamw_small(ws, gs, ms, vs):
    n = len(ws)

    def body(*refs):
        w_refs, g_refs, m_refs, v_refs = refs[:n], refs[n:2 * n], refs[2 * n:3 * n], refs[3 * n:4 * n]
        d_refs, nm_refs, nv_refs = refs[4 * n:5 * n], refs[5 * n:6 * n], refs[6 * n:7 * n]
        for t in range(n):
            d_refs[t][...], nm_refs[t][...], nv_refs[t][...] = _adamw_update(
                w_refs[t][...], g_refs[t][...], m_refs[t][...], v_refs[t][...])

    vm = pl.BlockSpec(memory_space=pltpu.VMEM)
    sds = [jax.ShapeDtypeStruct(w.shape, F32) for w in ws]
    res = pl.pallas_call(
        body, in_specs=[vm] * (4 * n), out_specs=[vm] * (3 * n), out_shape=sds * 3,
        compiler_params=_params(), name="adamw_small",
    )(*ws, *gs, *ms, *vs)
    return res[:n], res[n:2 * n], res[2 * n:]


SMALL = ["norm_mix_pre", "norm_mix_post", "norm_ffn_pre", "norm_ffn_post", "ln_v_gain", "ln_v_bias",
         "spatial_w", "spatial_b", "rel_bias", "conv_b"]
LARGE = ["w_in", "w_gate", "w_up", "w_down", "w_out"]
TRANSPOSED = ("w_in", "w_gate", "w_up")
ORDER = ["norm_mix_pre", "norm_mix_post", "norm_ffn_pre", "norm_ffn_post", "w_in", "ln_v_gain", "ln_v_bias",
         "spatial_w", "spatial_b", "rel_bias", "w_out", "w_gate", "w_up", "conv_w", "conv_b", "w_down"]


def kernel(x, norm_mix_pre, norm_mix_post, norm_ffn_pre, norm_ffn_post, w_in, ln_v_gain, ln_v_bias, spatial_w, spatial_b, rel_bias, w_out, w_gate, w_up, conv_w, conv_b, w_down, loss_target, m_norm_mix_pre, m_norm_mix_post, m_norm_ffn_pre, m_norm_ffn_post, m_w_in, m_ln_v_gain, m_ln_v_bias, m_spatial_w, m_spatial_b, m_rel_bias, m_w_out, m_w_gate, m_w_up, m_conv_w, m_conv_b, m_w_down, v_norm_mix_pre, v_norm_mix_post, v_norm_ffn_pre, v_norm_ffn_post, v_w_in, v_ln_v_gain, v_ln_v_bias, v_spatial_w, v_spatial_b, v_rel_bias, v_w_out, v_w_gate, v_w_up, v_conv_w, v_conv_b, v_w_down):
    params = dict(norm_mix_pre=norm_mix_pre, norm_mix_post=norm_mix_post, norm_ffn_pre=norm_ffn_pre,
                  norm_ffn_post=norm_ffn_post, w_in=w_in, ln_v_gain=ln_v_gain, ln_v_bias=ln_v_bias,
                  spatial_w=spatial_w, spatial_b=spatial_b, rel_bias=rel_bias, w_out=w_out, w_gate=w_gate,
                  w_up=w_up, conv_w=conv_w, conv_b=conv_b, w_down=w_down)
    mom = dict(norm_mix_pre=m_norm_mix_pre, norm_mix_post=m_norm_mix_post, norm_ffn_pre=m_norm_ffn_pre,
               norm_ffn_post=m_norm_ffn_post, w_in=m_w_in, ln_v_gain=m_ln_v_gain, ln_v_bias=m_ln_v_bias,
               spatial_w=m_spatial_w, spatial_b=m_spatial_b, rel_bias=m_rel_bias, w_out=m_w_out, w_gate=m_w_gate,
               w_up=m_w_up, conv_w=m_conv_w, conv_b=m_conv_b, w_down=m_w_down)
    var = dict(norm_mix_pre=v_norm_mix_pre, norm_mix_post=v_norm_mix_post, norm_ffn_pre=v_norm_ffn_pre,
               norm_ffn_post=v_norm_ffn_post, w_in=v_w_in, ln_v_gain=v_ln_v_gain, ln_v_bias=v_ln_v_bias,
               spatial_w=v_spatial_w, spatial_b=v_spatial_b, rel_bias=v_rel_bias, w_out=v_w_out, w_gate=v_w_gate,
               w_up=v_w_up, conv_w=v_conv_w, conv_b=v_conv_b, w_down=v_w_down)

    batch = x.shape[0]
    xi, yi, ci = lax.axis_index("x"), lax.axis_index("y"), lax.axis_index("c")
    s_idx = (2 * xi + yi).astype(jnp.int32).reshape(1)
    c_idx = ci.astype(jnp.int32).reshape(1)

    def local(a, n):
        return jnp.swapaxes(a[0], 0, 1) if n in TRANSPOSED else a[0]

    shards = {n: local(params[n], n) for n in LARGE}
    dx0, total, partial, half_reduced = _train_step(
        x.reshape(batch * SEQ, D_MODEL), loss_target.reshape(batch * SEQ, D_MODEL),
        norm_mix_pre, norm_mix_post, norm_ffn_pre, norm_ffn_post, shards,
        ln_v_gain.reshape(1, A_WIDTH), ln_v_bias.reshape(1, A_WIDTH), spatial_w[0], spatial_b[0], rel_bias,
        jnp.swapaxes(conv_w, 0, 1), conv_b, batch, s_idx, c_idx)
    grad_x = dx0.reshape(batch, SEQ, D_MODEL)

    names = list(partial)
    fulls, sums = _finish_reductions([half_reduced[n] for n in LARGE], [partial[n] for n in names])
    reduced = dict(zip(LARGE, fulls))
    total.update(zip(names, sums))
    loss = total["loss"][0, 0]
    total["spatial_b"] = total["spatial_b"][:, ::HEAD_DIM].T
    total["rel_bias"] = total["rel_bias"].reshape(B_HEADS, NUM_BUCKETS)
    total["conv_w"] = lax.dynamic_slice_in_dim(total["conv_w"], s_idx[0] * SHARD_FF, SHARD_FF, axis=1)
    small_names = SMALL + ["conv_w"]

    def small(a, n):
        return jnp.swapaxes(a, 0, 1) if n in ("rel_bias", "conv_w") else a

    for n in small_names:
        reduced[n] = total[n].reshape(small(params[n], n).shape)

    out_g, out_d, out_m, out_v = {}, {}, {}, {}
    for n in LARGE:
        res = _adamw(local(params[n], n), reduced[n], local(mom[n], n), local(var[n], n), name=f"adamw_{n}")
        if n in TRANSPOSED:
            res = [jnp.swapaxes(r, 0, 1) for r in res]
        out_g[n], out_d[n], out_m[n], out_v[n] = [r[None] for r in res]
    d, nm, nv = _adamw_small([small(params[n], n) for n in small_names], [reduced[n] for n in small_names],
                             [small(mom[n], n) for n in small_names], [small(var[n], n) for n in small_names])
    for n, dd, mm, vv in zip(small_names, d, nm, nv):
        out_g[n], out_d[n], out_m[n], out_v[n] = [small(r, n) for r in (reduced[n], dd, mm, vv)]

    return (loss, grad_x, *[out_g[n] for n in ORDER], *[out_d[n] for n in ORDER],
            *[out_m[n] for n in ORDER], *[out_v[n] for n in ORDER])
```

```python
import functools
import math

import numpy as np
import jax
import jax.numpy as jnp
from jax import lax
from jax.experimental import pallas as pl
from jax.experimental.pallas import tpu as pltpu

F32 = jnp.float32
BF16 = jnp.bfloat16
MESH = pl.DeviceIdType.MESH

D_MODEL = 1024
SEQ = 2048
HEAD_DIM = 64
A_GROUPS = 4
A_WIDTH = 256
B_HEADS = 12
B_WIDTH = 768
CHUNK = 128
DILATED = ((128, 1), (512, 4), (2048, 16))
NUM_BUCKETS = 32
MAX_DISTANCE = 2048
D_FF = 2816
IN_COLS = 2816
NORM_EPS = 1e-6
NEG_INF = -1e30
N_SHARD = 4
SHARD_FF = D_FF // N_SHARD
LANE_BLOCK = 256
VMEM_LIMIT = 56 * 1024 * 1024

ADAM_LR = 0.001
ADAM_B1 = 0.9
ADAM_B2 = 0.999
ADAM_EPS = 1e-08
ADAM_WD = 0.01
ADAM_STEP = 10

GELU_C = math.sqrt(2.0 / math.pi)
GELU_A = 0.044715

ANY = pl.BlockSpec(memory_space=pl.ANY)


def _params(sem=None):
    return pltpu.CompilerParams(dimension_semantics=sem, vmem_limit_bytes=VMEM_LIMIT)


def _dot(a, b, precision=None):
    return jnp.dot(a, b, preferred_element_type=F32, precision=precision)


def _dot_nt(a, b, precision=None):
    return lax.dot_general(a, b, (((1,), (1,)), ((), ())), preferred_element_type=F32, precision=precision)


def _dot_tn(a, b):
    return lax.dot_general(a, b, (((0,), (0,)), ((), ())), preferred_element_type=F32)


def _gelu(x):
    t = jnp.tanh(x * (GELU_C + (GELU_C * GELU_A) * (x * x)))
    return (0.5 * x) * (1.0 + t)


def _gelu_and_grad(x):
    x2 = x * x
    u = 1.0 + jnp.tanh(x * (GELU_C + (GELU_C * GELU_A) * x2))
    hx = 0.5 * x
    dg = u * (0.5 + hx * (2.0 - u) * (GELU_C + (3.0 * GELU_C * GELU_A) * x2))
    return hx * u, dg


def _mesh_pos():
    x, y, c = lax.axis_index("x"), lax.axis_index("y"), lax.axis_index("c")
    chips = [(1 - x, y), (x, 1 - y), (1 - x, 1 - y)]
    return x, y, c, chips


class _GatherPlan:
    def __init__(self, shapes, out_refs, send_sems, recv_sems):
        self.shapes, self.out_refs = shapes, out_refs
        self.send_sems, self.recv_sems = send_sems, recv_sems
        self.x, self.y, self.c, self.chips = _mesh_pos()
        self.sib = (self.x, self.y, 1 - self.c)

    def _half(self, t, chip, which):
        rows = self.shapes[t][0] // 2
        return self.out_refs[t].at[2 * chip[0] + chip[1], pl.ds(which * rows, rows), :]

    def _copy(self, k, src, dst, to):
        return pltpu.make_async_remote_copy(src_ref=src, dst_ref=dst, send_sem=self.send_sems.at[k],
                                            recv_sem=self.recv_sems.at[k], device_id=to, device_id_type=MESH)

    def _sends(self, t):
        own = self._half(t, (self.x, self.y), self.c)
        return [self._copy(6 * t + j, own, own, (*chip, self.c)) for j, chip in enumerate(self.chips)]

    def _forwards(self, t):
        return [self._copy(6 * t + 3 + j, self._half(t, chip, self.c), self._half(t, chip, self.c), self.sib)
                for j, chip in enumerate(self.chips)]

    def start(self, ts):
        for t in ts:
            for cp in self._sends(t):
                cp.start()

    def forward(self, ts):
        for t in ts:
            for j, chip in enumerate(self.chips):
                landed = self._half(t, chip, self.c)
                self._copy(6 * t + j, landed, landed, (*chip, self.c)).wait_recv()
            for cp in self._forwards(t):
                cp.start()

    def finish(self, ts):
        for t in ts:
            for j, chip in enumerate(self.chips):
                other = self._half(t, chip, 1 - self.c)
                self._copy(6 * t + 3 + j, other, other, self.sib).wait_recv()
        for t in ts:
            for cp in self._sends(t) + self._forwards(t):
                cp.wait_send()


class _RelayGatherPlan:
    def __init__(self, shapes, shard_refs, out_refs, send_sems, recv_sems):
        self.shapes, self.shard_refs, self.out_refs = shapes, shard_refs, out_refs
        self.send_sems, self.recv_sems = send_sems, recv_sems
        x, y, c, self.chips = _mesh_pos()
        self.me, self.c, self.sib = (x, y), c, (x, y, 1 - c)
        self.first = (x + c - 2 * x * c, y + (1 - c) - 2 * y * (1 - c))
        self.second = (x + (1 - c) - 2 * x * (1 - c), y + c - 2 * y * c)
        self.diag = (1 - x, 1 - y)

    def _half(self, t, chip, which):
        rows = self.shapes[t][0] // 2
        return self.out_refs[t].at[2 * chip[0] + chip[1], pl.ds(which * rows, rows), :]

    def _copy(self, k, src, dst, to):
        return pltpu.make_async_remote_copy(src_ref=src, dst_ref=dst, send_sem=self.send_sems.at[k],
                                            recv_sem=self.recv_sems.at[k], device_id=to, device_id_type=MESH)

    def _own(self, t):
        if self.shard_refs is None:
            return self._half(t, self.me, self.c)
        rows = self.shapes[t][0] // 2
        return self.shard_refs[t].at[pl.ds(self.c * rows, rows), :]

    def _step1(self, t):
        return self._copy(6 * t, self._own(t), self._half(t, self.me, self.c), (*self.first, self.c))

    def _step2(self, t):
        landed = self._half(t, self.first, self.c)
        return [self._copy(6 * t + 1, self._own(t), self._half(t, self.me, self.c), (*self.second, self.c)),
                self._copy(6 * t + 2, landed, landed, (*self.second, self.c))]

    def _forwards(self, t):
        return [self._copy(6 * t + 3 + j, self._half(t, chip, self.c), self._half(t, chip, self.c), self.sib)
                for j, chip in enumerate(self.chips)]

    def start(self, ts):
        for t in ts:
            self._step1(t).start()
            self._step2(t)[0].start()

    def relay(self, ts):
        for t in ts:
            landed = self._half(t, self.first, self.c)
            self._copy(6 * t, landed, landed, self.sib).wait_recv()
            self._step2(t)[1].start()

    def forward(self, ts):
        for t in ts:
            for k, chip in ((1, self.second), (2, self.diag)):
                landed = self._half(t, chip, self.c)
                self._copy(6 * t + k, landed, landed, self.sib).wait_recv()
            for cp in self._forwards(t):
                cp.start()

    def finish(self, ts):
        for t in ts:
            for j, chip in enumerate(self.chips):
                other = self._half(t, chip, 1 - self.c)
                self._copy(6 * t + 3 + j, other, other, self.sib).wait_recv()
        for t in ts:
            for cp in [self._step1(t)] + self._step2(t) + self._forwards(t):
                cp.wait_send()


class _SiblingExchangePlan:
    def __init__(self, shapes, grad_refs, out_refs, send_sems, recv_sems):
        self.shapes, self.grad_refs, self.out_refs = shapes, grad_refs, out_refs
        self.send_sems, self.recv_sems = send_sems, recv_sems
        self.x, self.y, self.c, _ = _mesh_pos()

    def _copies(self):
        out = []
        for t, (g, o) in enumerate(zip(self.grad_refs, self.out_refs)):
            rows = self.shapes[t][1] // 2
            out.append(pltpu.make_async_remote_copy(
                src_ref=g.at[:, pl.ds((1 - self.c) * rows, rows), :], dst_ref=o, send_sem=self.send_sems.at[t],
                recv_sem=self.recv_sems.at[t], device_id=(self.x, self.y, 1 - self.c), device_id_type=MESH))
        return out

    def start(self):
        for cp in self._copies():
            cp.start()

    def finish(self):
        for cp in self._copies():
            cp.wait()


class _ChipExchangePlan:
    def __init__(self, part_refs, out_refs, send_sems, recv_sems):
        self.part_refs, self.out_refs, self.send_sems, self.recv_sems = part_refs, out_refs, send_sems, recv_sems
        _, _, self.c, self.chips = _mesh_pos()

    def _copies(self):
        return [pltpu.make_async_remote_copy(
            src_ref=p.at[2 * chip[0] + chip[1]], dst_ref=o.at[j], send_sem=self.send_sems.at[3 * t + j],
            recv_sem=self.recv_sems.at[3 * t + j], device_id=(*chip, self.c), device_id_type=MESH)
            for t, (p, o) in enumerate(zip(self.part_refs, self.out_refs)) for j, chip in enumerate(self.chips)]

    def start(self):
        for cp in self._copies():
            cp.start()

    def finish(self):
        for cp in self._copies():
            cp.wait()


class _SmallAllReducePlan:
    def __init__(self, in_refs, out_refs, sib_refs, chip_refs, send_sems, recv_sems):
        self.in_refs, self.out_refs, self.sib_refs, self.chip_refs = in_refs, out_refs, sib_refs, chip_refs
        self.send_sems, self.recv_sems = send_sems, recv_sems
        self.n = len(in_refs)
        self.x, self.y, self.c, self.chips = _mesh_pos()

    def _copy(self, k, src, dst, to):
        return pltpu.make_async_remote_copy(src_ref=src, dst_ref=dst, send_sem=self.send_sems.at[k],
                                            recv_sem=self.recv_sems.at[k], device_id=to, device_id_type=MESH)

    def _first(self):
        return [self._copy(t, self.in_refs[t], self.sib_refs[t], (self.x, self.y, 1 - self.c)) for t in range(self.n)]

    def _second(self):
        return [self._copy(self.n + 3 * t + j, self.out_refs[t], self.chip_refs[t].at[j], (*chip, self.c))
                for t in range(self.n) for j, chip in enumerate(self.chips)]

    def start_sibling(self):
        for cp in self._first():
            cp.start()

    def sum_sibling_and_start_chips(self):
        for cp in self._first():
            cp.wait()
        for t in range(self.n):
            self.out_refs[t][...] = self.in_refs[t][...] + self.sib_refs[t][...]
        for cp in self._second():
            cp.start()

    def finish(self):
        for cp in self._second():
            cp.wait()
        for t in range(self.n):
            self.out_refs[t][...] = ((self.out_refs[t][...] + self.chip_refs[t][0])
                                     + (self.chip_refs[t][1] + self.chip_refs[t][2]))

    @staticmethod
    def scratch(arrays):
        return ([pltpu.VMEM(a.shape, F32) for a in arrays] + [pltpu.VMEM((3,) + a.shape, F32) for a in arrays]
                + _sem_pair(4 * len(arrays)))


def _sem_pair(n):
    return [pltpu.SemaphoreType.DMA((n,)), pltpu.SemaphoreType.DMA((n,))]


def _mm(a, b, *, dims, tm, tn, tk, out_dtype, name):
    if dims == "nn":
        m, k = a.shape
        n = b.shape[1]
        a_spec = pl.BlockSpec((tm, tk), lambda i, j, kk: (i, kk))
        b_spec = pl.BlockSpec((tk, tn), lambda i, j, kk: (kk, j))
        dot = _dot
    elif dims == "nt":
        m, k = a.shape
        n = b.shape[0]
        a_spec = pl.BlockSpec((tm, tk), lambda i, j, kk: (i, kk))
        b_spec = pl.BlockSpec((tn, tk), lambda i, j, kk: (j, kk))
        dot = _dot_nt
    else:
        k, m = a.shape
        n = b.shape[1]
        a_spec = pl.BlockSpec((tk, tm), lambda i, j, kk: (kk, i))
        b_spec = pl.BlockSpec((tk, tn), lambda i, j, kk: (kk, j))
        dot = _dot_tn
    assert m % tm == 0 and n % tn == 0 and k % tk == 0, (name, m, n, k)
    grid = (m // tm, n // tn, k // tk)
    nk = grid[2]
    own_acc = nk > 1 and out_dtype != F32

    def body(a_ref, b_ref, o_ref, *scratch):
        prod = dot(a_ref[...].astype(BF16), b_ref[...].astype(BF16))
        if nk == 1:
            o_ref[...] = prod.astype(out_dtype)
        else:
            acc_ref = scratch[0] if own_acc else o_ref
            kk = pl.program_id(2)

            @pl.when(kk == 0)
            def _():
                acc_ref[...] = prod

            @pl.when(kk > 0)
            def _():
                acc_ref[...] += prod

            if own_acc:
                @pl.when(kk == nk - 1)
                def _():
                    o_ref[...] = acc_ref[...].astype(out_dtype)

    return pl.pallas_call(
        body, grid=grid, in_specs=[a_spec, b_spec],
        out_specs=pl.BlockSpec((tm, tn), lambda i, j, kk: (i, j)),
        out_shape=jax.ShapeDtypeStruct((m, n), out_dtype),
        scratch_shapes=[pltpu.VMEM((tm, tn), F32)] if own_acc else [],
        compiler_params=_params(("parallel", "parallel", "arbitrary")), name=name,
    )(a, b)


def _mm_pair_tn(a1, a2, b, *, tm, tk, name):
    k, m = a1.shape
    n = b.shape[1]
    assert m % tm == 0 and k % tk == 0 and a2.shape == a1.shape, name
    nk = k // tk

    def body(a1_ref, a2_ref, b_ref, o1_ref, o2_ref, acc1_ref, acc2_ref):
        bv = b_ref[...]
        p1 = _dot_tn(a1_ref[...], bv)
        p2 = _dot_tn(a2_ref[...], bv)
        kk = pl.program_id(1)

        @pl.when(kk == 0)
        def _():
            acc1_ref[...] = p1
            acc2_ref[...] = p2

        @pl.when(kk > 0)
        def _():
            acc1_ref[...] += p1
            acc2_ref[...] += p2

        @pl.when(kk == nk - 1)
        def _():
            o1_ref[...] = acc1_ref[...].astype(BF16)
            o2_ref[...] = acc2_ref[...].astype(BF16)

    a_spec = pl.BlockSpec((tk, tm), lambda i, kk: (kk, i))
    o_spec = pl.BlockSpec((tm, n), lambda i, kk: (i, 0))
    return pl.pallas_call(
        body, grid=(m // tm, nk),
        in_specs=[a_spec, a_spec, pl.BlockSpec((tk, n), lambda i, kk: (kk, 0))],
        out_specs=[o_spec, o_spec],
        out_shape=[jax.ShapeDtypeStruct((m, n), BF16)] * 2,
        scratch_shapes=[pltpu.VMEM((tm, n), F32)] * 2,
        compiler_params=_params(("parallel", "arbitrary")), name=name,
    )(a1, a2, b)


def _mm_pair_nt(a, w1_t, w2_t, own, *, tm, tn, out_dtype, name):
    m, k = a.shape
    n = w1_t.shape[0]
    assert m % tm == 0 and n % tn == 0 and w2_t.shape == w1_t.shape, name
    grid = (m // tm, n // tn)
    n_steps = grid[0] * grid[1]

    def body(a_ref, w1_ref, w2_ref, own_ref, o1_ref, o2_ref, gat_ref, send_sems, recv_sems):
        del own_ref
        step = pl.program_id(0) * grid[1] + pl.program_id(1)
        gather = _GatherPlan([own.shape[1:]], [gat_ref], send_sems, recv_sems)

        @pl.when(step == 0)
        def _():
            gather.start([0])

        @pl.when(step == (2 * n_steps) // 3)
        def _():
            gather.forward([0])

        av = a_ref[...]
        o1_ref[...] = _dot_nt(av, w1_ref[...]).astype(out_dtype)
        o2_ref[...] = _dot_nt(av, w2_ref[...]).astype(out_dtype)

        @pl.when(step == n_steps - 1)
        def _():
            gather.finish([0])

    w_spec = pl.BlockSpec((tn, k), lambda i, j: (j, 0))
    o_spec = pl.BlockSpec((tm, tn), lambda i, j: (i, j))
    return pl.pallas_call(
        body, grid=grid,
        in_specs=[pl.BlockSpec((tm, k), lambda i, j: (i, 0)), w_spec, w_spec, ANY],
        out_specs=[o_spec, o_spec, ANY],
        out_shape=[jax.ShapeDtypeStruct((m, n), out_dtype)] * 2 + [jax.ShapeDtypeStruct(own.shape, own.dtype)],
        scratch_shapes=_sem_pair(6), input_output_aliases={3: 2},
        compiler_params=_params(("arbitrary", "arbitrary")), name=name,
    )(a, w1_t, w2_t, own)


def _out_proj_bwd(a, o, dy1, w_out, grads):
    m = dy1.shape[0]
    tm = 1024
    nx = len(grads)
    shapes = [g.shape for g in grads]
    n_steps = m // tm

    def body(a_ref, o_ref, dy_ref, w_ref, *rest):
        grad_refs = rest[:nx]
        dmix_ref, dw_ref = rest[nx:nx + 2]
        acc_ref = rest[2 * nx + 2]
        exchange = _SiblingExchangePlan(shapes, grad_refs, rest[nx + 2:2 * nx + 2], *rest[2 * nx + 3:])

        @pl.when(pl.program_id(0) == 0)
        def _():
            exchange.start()

        dy = dy_ref[...]
        dmix_ref[...] = _dot_nt(dy, w_ref[...])
        top = _dot_tn(a_ref[...], dy)
        bottom = _dot_tn(o_ref[...], dy)

        @pl.when(pl.program_id(0) == 0)
        def _():
            acc_ref[:A_WIDTH, :] = top
            acc_ref[A_WIDTH:, :] = bottom

        @pl.when(pl.program_id(0) > 0)
        def _():
            acc_ref[:A_WIDTH, :] += top
            acc_ref[A_WIDTH:, :] += bottom

        @pl.when(pl.program_id(0) == n_steps - 1)
        def _():
            dw_ref[...] = acc_ref[...].astype(BF16)
            exchange.finish()

    tile = lambda width: pl.BlockSpec((tm, width), lambda i: (i, 0))
    res = pl.pallas_call(
        body, grid=(n_steps,),
        in_specs=[tile(A_WIDTH), tile(B_WIDTH), tile(D_MODEL), _full_spec((D_MODEL, D_MODEL))] + [ANY] * nx,
        out_specs=[tile(D_MODEL), _full_spec((D_MODEL, D_MODEL))] + [ANY] * nx,
        out_shape=[jax.ShapeDtypeStruct((m, D_MODEL), F32), jax.ShapeDtypeStruct((D_MODEL, D_MODEL), BF16)]
        + [jax.ShapeDtypeStruct((N_SHARD, s[1] // 2, s[2]), g.dtype) for s, g in zip(shapes, grads)],
        scratch_shapes=[pltpu.VMEM((D_MODEL, D_MODEL), F32)] + _sem_pair(nx),
        compiler_params=_params(("arbitrary",)), name="out_proj_bwd",
    )(a, o, dy1, w_out, *grads)
    return res[:2], list(res[2:])


def _fused_rows(name, tm, mats, rows, vecs, fn, row_outs, acc_outs, exchange=()):
    m = mats[0][0].shape[0]
    nm, nr, nv, nro, nao, nx = len(mats), len(rows), len(vecs), len(row_outs), len(acc_outs), len(exchange)
    n_steps = m // tm

    def body(*refs):
        a_refs, w_refs = refs[:nm], refs[nm:2 * nm]
        pos = 2 * nm
        row_refs, vec_refs, part_refs = refs[pos:pos + nr], refs[pos + nr:pos + nr + nv], refs[pos + nr + nv:pos + nr + nv + nx]
        pos += nr + nv + nx
        out_refs, acc_refs, recv_refs = refs[pos:pos + nro], refs[pos + nro:pos + nro + nao], refs[pos + nro + nao:pos + nro + nao + nx]
        sems = refs[pos + nro + nao + nx:]
        i = pl.program_id(0)
        if nx:
            plan = _ChipExchangePlan(part_refs, recv_refs, *sems)

            @pl.when(i == 0)
            def _():
                plan.start()

        @pl.when(i == 0)
        def _():
            for r in acc_refs:
                r[...] = jnp.zeros_like(r)

        y = None
        for a_ref, w_ref, (_, _, dims, sl) in zip(a_refs, w_refs, mats):
            w = w_ref[...] if sl is None else w_ref[sl, :]
            part = (_dot if dims == "nn" else _dot_nt)(a_ref[...], w)
            y = part if y is None else y + part
        res = fn(y, *[r[...] for r in row_refs], *[v[...] for v in vec_refs])
        for r, val in zip(out_refs, res[:nro]):
            r[...] = val.astype(r.dtype)
        for r, val in zip(acc_refs, res[nro:]):
            r[...] += val

        if nx:
            @pl.when(i == n_steps - 1)
            def _():
                plan.finish()

    tile = lambda width: pl.BlockSpec((tm, width), lambda i: (i, 0))
    res = pl.pallas_call(
        body, grid=(n_steps,),
        in_specs=[tile(a.shape[1]) for a, _, _, _ in mats] + [_full_spec(w.shape) for _, w, _, _ in mats]
        + [tile(D_MODEL)] * nr + [_full_spec((1, D_MODEL))] * nv + [ANY] * nx,
        out_specs=[tile(D_MODEL)] * nro + [_full_spec(s) for s in acc_outs] + [ANY] * nx,
        out_shape=[jax.ShapeDtypeStruct((m, D_MODEL), dt) for dt in row_outs]
        + [jax.ShapeDtypeStruct(s, F32) for s in acc_outs]
        + [jax.ShapeDtypeStruct((3,) + p.shape[1:], p.dtype) for p in exchange],
        scratch_shapes=_sem_pair(3 * nx) if nx else [],
        compiler_params=_params(("arbitrary",)), name=name,
    )(*[a for a, _, _, _ in mats], *[w for _, w, _, _ in mats], *rows, *vecs, *exchange)
    return list(res[:nro + nao]), list(res[nro + nao:])


def _vec_spec(width=D_MODEL):
    return pl.BlockSpec((1, width), lambda i: (0, 0))


def _rstd(v):
    return lax.rsqrt(jnp.mean(v * v, axis=-1, keepdims=True) + NORM_EPS)


def _mid_fwd_rows(y1, x0, g2, g3):
    x1 = x0 + y1 * _rstd(y1) * g2
    return y1, x1, x1 * _rstd(x1) * g3


def _rms_bwd_rows(dout, v, g):
    r = _rstd(v)
    n = v * r
    dn = dout * g
    dv = r * (dn - n * jnp.mean(dn * n, axis=-1, keepdims=True))
    dg = jnp.sum(dout * n, axis=0, keepdims=True)
    return dv, dg


def _loss_head_rows(y2, x1, tgt, g4):
    x2 = x1 + y2 * _rstd(y2) * g4
    err = x2 - tgt
    loss = 0.5 * jnp.sum(jnp.mean(err * err, axis=-1, keepdims=True), axis=0, keepdims=True)
    dx2 = err * (1.0 / D_MODEL)
    dy2, dg4 = _rms_bwd_rows(dx2, y2, g4)
    return dx2, dy2, dg4, loss


def _mid_bwd_rows(dh2, x1, y1, dx2, g2, g3):
    d3, dg3 = _rms_bwd_rows(dh2, x1, g3)
    dx1 = dx2 + d3
    dy1, dg2 = _rms_bwd_rows(dx1, y1, g2)
    return dx1, dy1, dg2, dg3


def _in_bwd_rows(dh1, x0, dx1, g1):
    d1, dg1 = _rms_bwd_rows(dh1, x0, g1)
    return dx1 + d1, dg1


GATE_ROWS = 512


def _group_mean_matrix():
    p = np.zeros((A_WIDTH, A_WIDTH), np.float32)
    for g in range(A_GROUPS):
        p[g * HEAD_DIM:(g + 1) * HEAD_DIM, g * HEAD_DIM:(g + 1) * HEAD_DIM] = 1.0 / HEAD_DIM
    return jnp.asarray(p)


def _group_masks(width=A_WIDTH):
    lane = lax.broadcasted_iota(jnp.int32, (1, width), 1)
    return [(lane >= g * HEAD_DIM) & (lane < (g + 1) * HEAD_DIM) for g in range(width // HEAD_DIM)]


GROUP_SUM_PRECISION = lax.Precision.HIGH


def _layernorm_groups(vg, pavg):
    hi = GROUP_SUM_PRECISION
    mu = _dot(vg, pavg, hi)
    xc = vg - mu
    var = _dot(xc * xc, pavg, hi)
    rstd = lax.rsqrt(var + NORM_EPS)
    return xc * rstd, rstd


def _spatial_mix(w_bf, vn_chunk_bf, masks, bz):
    z = bz
    for g in range(A_GROUPS):
        z = z + jnp.where(masks[g], _dot(w_bf[g], vn_chunk_bf), 0.0)
    return z


def _full_spec(shape):
    return pl.BlockSpec(shape, lambda i: tuple(0 for _ in shape))


def _gate_fwd_rows(u, v, lg, lb, w_ref, bz, pavg, a_ref):
    masks = _group_masks()
    row = lax.broadcasted_iota(jnp.int32, (CHUNK, CHUNK), 0)
    col = lax.broadcasted_iota(jnp.int32, (CHUNK, CHUNK), 1)
    w_bf = [jnp.where(row >= col, w_ref[g], 0.0).astype(BF16) for g in range(A_GROUPS)]
    ug = _gelu(u)
    vhat, _ = _layernorm_groups(_gelu(v), pavg)
    vn = vhat * lg + lb
    for c in range(GATE_ROWS // CHUNK):
        sl = slice(c * CHUNK, (c + 1) * CHUNK)
        z = _spatial_mix(w_bf, vn[sl].astype(BF16), masks, bz)
        a_ref[sl, :] = (ug[sl] * z).astype(BF16)


def _gate_bwd(uv, dmix, ln_g, ln_b, w_s, w_st, bz, grads):
    m = uv.shape[0]
    pavg = _group_mean_matrix()
    nsteps = m // GATE_ROWS
    nx = len(grads)
    shapes = [g.shape for g in grads]

    def body(u_ref, v_ref, da_ref, lg_ref, lb_ref, w_ref, wt_ref, bz_ref, p_ref, *rest):
        grad_refs = rest[:nx]
        duv_ref, dlg_ref, dlb_ref, dw_ref, dbz_ref = rest[nx:nx + 5]
        recv_refs = rest[nx + 5:2 * nx + 5]
        exchange = _SiblingExchangePlan(shapes, grad_refs, recv_refs, *rest[2 * nx + 5:])
        i = pl.program_id(0)

        @pl.when(i == 0)
        def _():
            exchange.start()
            dlg_ref[...] = jnp.zeros_like(dlg_ref)
            dlb_ref[...] = jnp.zeros_like(dlb_ref)
            dw_ref[...] = jnp.zeros_like(dw_ref)
            dbz_ref[...] = jnp.zeros_like(dbz_ref)

        hi = GROUP_SUM_PRECISION
        masks = _group_masks()
        row = lax.broadcasted_iota(jnp.int32, (CHUNK, CHUNK), 0)
        col = lax.broadcasted_iota(jnp.int32, (CHUNK, CHUNK), 1)
        tril = row >= col
        w_bf = [jnp.where(tril, w_ref[g], 0.0).astype(BF16) for g in range(A_GROUPS)]
        wt_bf = [jnp.where(col >= row, wt_ref[g], 0.0).astype(BF16) for g in range(A_GROUPS)]
        pavg_v = p_ref[...]
        lg = lg_ref[...]
        ug, dug = _gelu_and_grad(u_ref[...])
        vg, dvg_dx = _gelu_and_grad(v_ref[...])
        vhat, rstd = _layernorm_groups(vg, pavg_v)
        vn = vhat * lg + lb_ref[...]
        da = da_ref[...]
        bz = bz_ref[...]
        for c in range(GATE_ROWS // CHUNK):
            sl = slice(c * CHUNK, (c + 1) * CHUNK)
            vn_bf = vn[sl].astype(BF16)
            z = _spatial_mix(w_bf, vn_bf, masks, bz)
            dz = da[sl] * ug[sl]
            duv_ref[sl, 0:A_WIDTH] = (da[sl] * z * dug[sl]).astype(BF16)
            dbz_ref[...] += dz
            dz_bf = dz.astype(BF16)
            dvn = jnp.zeros((CHUNK, A_WIDTH), F32)
            for g in range(A_GROUPS):
                dz_g = jnp.where(masks[g], dz, 0.0).astype(BF16)
                dw_ref[g] += jnp.where(tril, _dot_nt(dz_g, vn_bf), 0.0)
                dvn = dvn + jnp.where(masks[g], _dot(wt_bf[g], dz_bf), 0.0)
            vh = vhat[sl]
            dlb_ref[...] += jnp.sum(dvn, axis=0, keepdims=True)
            dlg_ref[...] += jnp.sum(dvn * vh, axis=0, keepdims=True)
            dvh = dvn * lg
            m1 = _dot(dvh, pavg_v, hi)
            m2 = _dot(dvh * vh, pavg_v, hi)
            duv_ref[sl, A_WIDTH:2 * A_WIDTH] = (rstd[sl] * (dvh - m1 - vh * m2) * dvg_dx[sl]).astype(BF16)

        @pl.when(i == nsteps - 1)
        def _():
            dbz_ref[...] = _dot(dbz_ref[...], pavg_v * float(HEAD_DIM), hi)
            exchange.finish()

    res = pl.pallas_call(
        body, grid=(nsteps,),
        in_specs=[pl.BlockSpec((GATE_ROWS, A_WIDTH), lambda i: (i, 0)),
                  pl.BlockSpec((GATE_ROWS, A_WIDTH), lambda i: (i, 1)),
                  pl.BlockSpec((GATE_ROWS, A_WIDTH), lambda i: (i, 0)),
                  _full_spec((1, A_WIDTH)), _full_spec((1, A_WIDTH)), _full_spec((A_GROUPS, CHUNK, CHUNK)),
                  _full_spec((A_GROUPS, CHUNK, CHUNK)), _full_spec((CHUNK, A_WIDTH)),
                  _full_spec((A_WIDTH, A_WIDTH))] + [ANY] * nx,
        out_specs=[pl.BlockSpec((GATE_ROWS, 2 * A_WIDTH), lambda i: (i, 0)),
                   _full_spec((1, A_WIDTH)), _full_spec((1, A_WIDTH)), _full_spec((A_GROUPS, CHUNK, CHUNK)),
                   _full_spec((CHUNK, A_WIDTH))] + [ANY] * nx,
        out_shape=[jax.ShapeDtypeStruct((m, IN_COLS), BF16),
                   jax.ShapeDtypeStruct((1, A_WIDTH), F32), jax.ShapeDtypeStruct((1, A_WIDTH), F32),
                   jax.ShapeDtypeStruct((A_GROUPS, CHUNK, CHUNK), F32),
                   jax.ShapeDtypeStruct((CHUNK, A_WIDTH), F32)]
        + [jax.ShapeDtypeStruct((N_SHARD, s[1] // 2, s[2]), g.dtype) for s, g in zip(shapes, grads)],
        scratch_shapes=_sem_pair(nx),
        compiler_params=_params(("arbitrary",)), name="gate_bwd",
    )(uv, uv, dmix, ln_g, ln_b, w_s, w_st, bz, pavg, *grads)
    return res[:5], list(res[5:])


Q_BLOCK = 128
PAIR = 2 * HEAD_DIM
N_PAIR = B_HEADS // 2
N_CFG = len(DILATED)
BLOCKS_PER_CFG = SEQ // Q_BLOCK
QKV_SLABS = 3 * N_PAIR
FWD_BLOCKS_PER_TRIP = 8
BWD_BLOCKS_PER_TRIP = 4


def _t5_bucket_np(dist, dtype):
    max_exact = NUM_BUCKETS // 2
    d = np.maximum(dist, 1).astype(dtype)
    large = max_exact + (np.log(d / dtype(max_exact)) / dtype(math.log(MAX_DISTANCE / max_exact))
                         * dtype(NUM_BUCKETS - max_exact))
    large = np.minimum(large.astype(np.int32), NUM_BUCKETS - 1)
    return np.where(dist < max_exact, dist, large)


def _bucket_tables():
    i = np.arange(Q_BLOCK)[:, None]
    j = np.arange(Q_BLOCK)[None, :]
    tables = []
    for _, dil in DILATED:
        rel_prev = Q_BLOCK + i - j
        rel_cur = i - j
        rel = np.concatenate([rel_prev, rel_cur], axis=1)
        valid = np.concatenate([rel_prev <= Q_BLOCK, rel_cur >= 0], axis=1)
        dist = np.maximum(rel, 0) * dil
        b32 = _t5_bucket_np(dist, np.float32)
        b64 = _t5_bucket_np(dist, np.float64)
        assert np.array_equal(b32, b64)
        tables.append(np.where(valid, b32, -1).astype(np.int32))
    return np.stack(tables)


def _present_buckets(buckets_np):
    return [sorted(set(int(v) for v in np.unique(buckets_np[c]) if v >= 0)) for c in range(N_CFG)]


def _bias_tables_body(buckets_np):
    present = _present_buckets(buckets_np)

    def tables(rb_ref, bk_ref, o_ref, ot_ref):
        for c in range(N_CFG):
            bk = bk_ref[c]
            for h in range(B_HEADS):
                acc = jnp.full((Q_BLOCK, 2 * Q_BLOCK), NEG_INF, F32)
                for b in present[c]:
                    acc = jnp.where(bk == b, rb_ref[h, b], acc)
                o_ref[c, h] = acc
                ot_ref[c, h] = acc.T

    return tables


def _proj_fwd(x, g1, w_in_t, ln_g, ln_b, w_s, bz):
    m = x.shape[0]
    tm = GATE_ROWS
    pavg = _group_mean_matrix()

    def body(x_ref, g_ref, w_ref, lg_ref, lb_ref, ws_ref, bz_ref, p_ref, h_ref, uv_ref, qkv_ref, a_ref):
        xv = x_ref[...]
        h = (xv * _rstd(xv) * g_ref[...]).astype(BF16)
        h_ref[...] = h
        acc = _dot_nt(h, w_ref[...])
        uv_ref[...] = acc[:, :2 * A_WIDTH]
        for s in range(QKV_SLABS):
            qkv_ref[s] = acc[:, 2 * A_WIDTH + s * PAIR:2 * A_WIDTH + (s + 1) * PAIR]
        _gate_fwd_rows(acc[:, :A_WIDTH], acc[:, A_WIDTH:2 * A_WIDTH], lg_ref[...], lb_ref[...], ws_ref,
                       bz_ref[...], p_ref[...], a_ref)

    return pl.pallas_call(
        body, grid=(m // tm,),
        in_specs=[pl.BlockSpec((tm, D_MODEL), lambda i: (i, 0)), _vec_spec(),
                  pl.BlockSpec((IN_COLS, D_MODEL), lambda i: (0, 0)),
                  _full_spec((1, A_WIDTH)), _full_spec((1, A_WIDTH)), _full_spec((A_GROUPS, CHUNK, CHUNK)),
                  _full_spec((CHUNK, A_WIDTH)), _full_spec((A_WIDTH, A_WIDTH))],
        out_specs=[pl.BlockSpec((tm, D_MODEL), lambda i: (i, 0)),
                   pl.BlockSpec((tm, 2 * A_WIDTH), lambda i: (i, 0)),
                   pl.BlockSpec((QKV_SLABS, tm, PAIR), lambda i: (0, i, 0)),
                   pl.BlockSpec((tm, A_WIDTH), lambda i: (i, 0))],
        out_shape=[jax.ShapeDtypeStruct((m, D_MODEL), BF16), jax.ShapeDtypeStruct((m, 2 * A_WIDTH), F32),
                   jax.ShapeDtypeStruct((QKV_SLABS, m, PAIR), F32), jax.ShapeDtypeStruct((m, A_WIDTH), BF16)],
        compiler_params=_params(("parallel",)), name="proj_fwd",
    )(x, g1, w_in_t, ln_g, ln_b, w_s, bz, pavg)


def _pair_masks():
    lane = lax.broadcasted_iota(jnp.int32, (1, PAIR), 1)
    return [lane < HEAD_DIM, lane >= HEAD_DIM]


def _block_rows(idx, dil):
    static = isinstance(idx, int)
    r, n = idx % dil, idx // dil

    def rows_of(block):
        start = r + (dil * Q_BLOCK) * block
        if dil == 1:
            return pl.ds(start if static else pl.multiple_of(start, Q_BLOCK), Q_BLOCK)
        return pl.ds(start, Q_BLOCK, stride=dil)

    prev = rows_of(n - 1) if not static or n > 0 else None
    return rows_of(n), prev


def _attn_fwd(qkv, bias, batch, owns):
    m = qkv.shape[1]
    comb_rows = 256
    nt = len(owns)
    shapes = [g.shape[1:] for g in owns]
    n_steps = batch * N_PAIR
    ts = list(range(nt))

    def body(q_ref, k_ref, v_ref, b_ref, *rest):
        o_ref, l_ref = rest[nt:nt + 2]
        gat_refs = rest[nt + 2:2 * nt + 2]
        scratch = rest[2 * nt + 2:]
        oc_refs, lc_refs = scratch[:N_CFG], scratch[N_CFG:2 * N_CFG]
        step = pl.program_id(0) * N_PAIR + pl.program_id(1)
        gather = _RelayGatherPlan(shapes, None, gat_refs, *scratch[2 * N_CFG:])

        @pl.when(step == 0)
        def _():
            gather.start(ts)

        @pl.when(step == n_steps // 2)
        def _():
            gather.relay(ts)

        @pl.when(step == n_steps - 2)
        def _():
            gather.forward(ts)

        masks = _pair_masks()
        for ci, (_, dil) in enumerate(DILATED):
            nb = SEQ // dil // Q_BLOCK

            def block(trip, ci=ci, dil=dil, nb=nb):
                work = []
                for u in range(FWD_BLOCKS_PER_TRIP):
                    rows, prow = _block_rows(trip * FWD_BLOCKS_PER_TRIP + u, dil)
                    has_prev = nb > 1 and prow is not None
                    q = q_ref[rows, :] * 0.125
                    kc = k_ref[rows, :].astype(BF16)
                    vc = v_ref[rows, :]
                    kp = k_ref[prow, :].astype(BF16) if has_prev else None
                    vp = v_ref[prow, :] if has_prev else None
                    tiles = []
                    for h in range(2):
                        qh = jnp.where(masks[h], q, 0.0).astype(BF16)
                        sc = _dot_nt(qh, kc) + b_ref[ci, h, :, Q_BLOCK:]
                        sp = _dot_nt(qh, kp) + b_ref[ci, h, :, :Q_BLOCK] if has_prev else None
                        tiles.append((sc, sp))
                    work.append((rows, vc, vp, tiles))
                probs = []
                for _, _, _, tiles in work:
                    ps = []
                    for sc, sp in tiles:
                        mx = jnp.max(sc if sp is None else jnp.maximum(sc, sp), axis=1, keepdims=True)
                        pc = jnp.exp(sc - mx).astype(BF16)
                        pp = None if sp is None else jnp.exp(sp - mx).astype(BF16)
                        ps.append((mx, pc, pp))
                    probs.append(ps)
                for (rows, vc, vp, _), ps in zip(work, probs):
                    res = []
                    for h, (_, pc, pp) in enumerate(ps):
                        r = _dot(pc, jnp.where(masks[h], vc, 1.0).astype(BF16))
                        if pp is not None:
                            r = r + _dot(pp, jnp.where(masks[h], vp, 1.0).astype(BF16))
                        res.append(r)
                    num = jnp.where(masks[0], res[0], res[1])
                    den = pltpu.roll(jnp.where(masks[0], res[1], res[0]), HEAD_DIM, 1)
                    oc_refs[ci][rows, :] = num / den
                    lc_refs[ci][rows, :] = jnp.where(masks[0], ps[0][0], ps[1][0]) + jnp.log(den)

            for trip in range(BLOCKS_PER_CFG // FWD_BLOCKS_PER_TRIP):
                block(trip)

        def combine(i, carry):
            rr = pl.ds(pl.multiple_of(i * comb_rows, comb_rows), comb_rows)
            ls = [lc_refs[c][rr, :] for c in range(N_CFG)]
            mx = functools.reduce(jnp.maximum, ls)
            ws = [jnp.exp(l - mx) for l in ls]
            tot = functools.reduce(lambda a, b: a + b, ws)
            o = functools.reduce(lambda a, b: a + b, [ws[c] * oc_refs[c][rr, :] for c in range(N_CFG)]) / tot
            o_ref[rr, :] = o.astype(BF16)
            l_ref[rr, :] = mx + jnp.log(tot)
            return carry

        lax.fori_loop(0, SEQ // comb_rows, combine, 0)

        @pl.when(step == n_steps - 1)
        def _():
            gather.finish(ts)

    def slab(first):
        return pl.BlockSpec((None, SEQ, PAIR), lambda b, p: (first + p, b, 0))

    nat = pl.BlockSpec((SEQ, PAIR), lambda b, p: (b, p))
    res = pl.pallas_call(
        body, grid=(batch, N_PAIR),
        in_specs=[slab(0), slab(N_PAIR), slab(2 * N_PAIR),
                  pl.BlockSpec((N_CFG, 2, Q_BLOCK, 2 * Q_BLOCK), lambda b, p: (0, p, 0, 0))] + [ANY] * nt,
        out_specs=[nat, nat] + [ANY] * nt,
        out_shape=[jax.ShapeDtypeStruct((m, B_WIDTH), BF16), jax.ShapeDtypeStruct((m, B_WIDTH), F32)]
        + [jax.ShapeDtypeStruct(g.shape, g.dtype) for g in owns],
        scratch_shapes=[pltpu.VMEM((SEQ, PAIR), F32)] * (2 * N_CFG) + _sem_pair(6 * nt),
        input_output_aliases={4 + t: 2 + t for t in range(nt)},
        compiler_params=_params(("arbitrary", "arbitrary")), name="attn_fwd",
    )(qkv, qkv, qkv, bias, *owns)
    return res[0], res[1], list(res[2:])


def _attn_bwd(qkv, dmix, o, lse, bias_t, dproj, batch, parts, smalls):
    m = qkv.shape[1]
    nt, ns = len(parts), len(smalls)
    n_steps = N_PAIR * batch

    def body(q_ref, k_ref, v_ref, do_ref, o_ref, l_ref, b_ref, *rest):
        part_refs = rest[1:nt + 1]
        small_refs = rest[nt + 1:nt + 1 + ns]
        pos = nt + 1 + ns
        dproj_ref, ds_ref = rest[pos:pos + 2]
        recv_refs = rest[pos + 2:pos + 2 + nt]
        sum_refs = rest[pos + 2 + nt:pos + 2 + nt + ns]
        pos += 2 + nt + ns
        dq_acc, dk_acc, dv_acc, d_scr, stage, stage_sems, send_sems, recv_sems = rest[pos:pos + 8]
        allreduce = _SmallAllReducePlan(small_refs, sum_refs, rest[pos + 8:pos + 8 + ns],
                                        rest[pos + 8 + ns:pos + 8 + 2 * ns], *rest[pos + 8 + 2 * ns:])
        pair, seq = pl.program_id(0), pl.program_id(1)
        step = pair * batch + seq
        exchange = _ChipExchangePlan(part_refs, recv_refs, send_sems, recv_sems)

        @pl.when(step == 0)
        def _():
            allreduce.start_sibling()

        @pl.when(step == n_steps // 2)
        def _():
            allreduce.sum_sibling_and_start_chips()

        def stage_copies():
            rows = pl.ds(pl.multiple_of(seq * SEQ, SEQ), SEQ)
            return [pltpu.make_async_copy(
                stage.at[k],
                dproj_ref.at[rows, pl.ds(pl.multiple_of(2 * A_WIDTH + k * B_WIDTH + pair * PAIR, PAIR), PAIR)],
                stage_sems.at[k]) for k in range(3)]

        @pl.when(step == 0)
        def _():
            exchange.start()

        @pl.when(pl.program_id(1) == 0)
        def _():
            ds_ref[...] = jnp.zeros_like(ds_ref)

        dq_acc[...] = jnp.zeros_like(dq_acc)
        dk_acc[...] = jnp.zeros_like(dk_acc)
        dv_acc[...] = jnp.zeros_like(dv_acc)
        d_scr[...] = do_ref[...] * o_ref[...].astype(F32)
        masks = _pair_masks()

        def stack_heads(t):
            return jnp.concatenate([jnp.where(masks[0], t, 0.0), jnp.where(masks[1], t, 0.0)], axis=0).astype(BF16)

        for ci, (_, dil) in enumerate(DILATED):
            nb = SEQ // dil // Q_BLOCK

            def block(trip, carry, ci=ci, dil=dil, nb=nb):
                first = []
                for u in range(BWD_BLOCKS_PER_TRIP):
                    rows, prow = _block_rows(trip * BWD_BLOCKS_PER_TRIP + u, dil)
                    has_prev = nb > 1 and prow is not None
                    if has_prev:
                        kcat = jnp.concatenate([k_ref[prow, :], k_ref[rows, :]], axis=0).astype(BF16)
                        vcat = jnp.concatenate([v_ref[prow, :], v_ref[rows, :]], axis=0).astype(BF16)
                    else:
                        kcat = k_ref[rows, :].astype(BF16)
                        vcat = v_ref[rows, :].astype(BF16)
                    qst = stack_heads(q_ref[rows, :] * 0.125)
                    dost = stack_heads(do_ref[rows, :])
                    lt = l_ref[rows, :].T
                    dt = d_scr[rows, :].T
                    lrow = jnp.concatenate([lt[0:1], lt[HEAD_DIM:HEAD_DIM + 1]], axis=1)
                    drow = jnp.concatenate([jnp.sum(dt[:HEAD_DIM], axis=0, keepdims=True),
                                            jnp.sum(dt[HEAD_DIM:], axis=0, keepdims=True)], axis=1)
                    first.append((has_prev, rows, prow, kcat, qst, dost, lrow, drow,
                                  _dot_nt(kcat, qst), _dot_nt(vcat, dost)))
                second = []
                for has_prev, rows, prow, kcat, qst, dost, lrow, drow, st, dpt in first:
                    keys = slice(0, 2 * Q_BLOCK) if has_prev else slice(Q_BLOCK, 2 * Q_BLOCK)
                    bt = jnp.concatenate([b_ref[ci, 0, keys, :], b_ref[ci, 1, keys, :]], axis=1)
                    pt = jnp.exp(st + bt - lrow)
                    dst = pt * (dpt - drow)
                    ds_ref[ci, 0, keys, :] += dst[:, :Q_BLOCK]
                    ds_ref[ci, 1, keys, :] += dst[:, Q_BLOCK:]
                    second.append((has_prev, rows, prow, kcat, qst, dost, pt.astype(BF16), dst.astype(BF16)))
                for has_prev, rows, prow, kcat, qst, dost, pt_bf, dst_bf in second:
                    dk = _dot(dst_bf, qst)
                    dv = _dot(pt_bf, dost)
                    dq2 = _dot_tn(dst_bf, kcat)
                    dq_acc[rows, :] += jnp.where(masks[0], dq2[:Q_BLOCK], dq2[Q_BLOCK:]) * 0.125
                    if has_prev:
                        dk_acc[prow, :] += dk[:Q_BLOCK]
                        dv_acc[prow, :] += dv[:Q_BLOCK]
                        dk_acc[rows, :] += dk[Q_BLOCK:]
                        dv_acc[rows, :] += dv[Q_BLOCK:]
                    else:
                        dk_acc[rows, :] += dk
                        dv_acc[rows, :] += dv
                return carry

            for trip in range(BLOCKS_PER_CFG // BWD_BLOCKS_PER_TRIP):
                block(trip, 0)

        @pl.when(step > 0)
        def _():
            for cp in stage_copies():
                cp.wait()

        stage[0] = dq_acc[...].astype(BF16)
        stage[1] = dk_acc[...].astype(BF16)
        stage[2] = dv_acc[...].astype(BF16)
        for cp in stage_copies():
            cp.start()

        @pl.when(step == n_steps - 1)
        def _():
            for cp in stage_copies():
                cp.wait()
            exchange.finish()
            allreduce.finish()

    def slab(first):
        return pl.BlockSpec((None, SEQ, PAIR), lambda p, b: (first + p, b, 0))

    nat = pl.BlockSpec((SEQ, PAIR), lambda p, b: (b, p))
    tbl = pl.BlockSpec((N_CFG, 2, 2 * Q_BLOCK, Q_BLOCK), lambda p, b: (0, p, 0, 0))
    acc = pltpu.VMEM((SEQ, PAIR), F32)
    vm = pl.BlockSpec(memory_space=pltpu.VMEM)
    res = pl.pallas_call(
        body, grid=(N_PAIR, batch),
        in_specs=[slab(0), slab(N_PAIR), slab(2 * N_PAIR),
                  pl.BlockSpec((SEQ, PAIR), lambda p, b: (b, A_WIDTH // PAIR + p)), nat, nat, tbl]
        + [ANY] * (nt + 1) + [vm] * ns,
        out_specs=[ANY, tbl] + [ANY] * nt + [vm] * ns,
        out_shape=[jax.ShapeDtypeStruct(dproj.shape, dproj.dtype),
                   jax.ShapeDtypeStruct((N_CFG, B_HEADS, 2 * Q_BLOCK, Q_BLOCK), F32)]
        + [jax.ShapeDtypeStruct((3,) + p.shape[1:], p.dtype) for p in parts]
        + [jax.ShapeDtypeStruct(a.shape, F32) for a in smalls],
        input_output_aliases={7: 0},
        scratch_shapes=[acc, acc, acc, acc, pltpu.VMEM((3, SEQ, PAIR), BF16), pltpu.SemaphoreType.DMA((3,))]
        + _sem_pair(3 * nt) + _SmallAllReducePlan.scratch(smalls),
        compiler_params=_params(("arbitrary", "arbitrary")), name="attn_bwd",
    )(qkv, qkv, qkv, dmix, o, lse, bias_t, dproj, *parts, *smalls)
    return res[0], res[1], list(res[2:2 + nt]), list(res[2 + nt:])


def _rel_bias_grad(ds, buckets_np, grads):
    present = _present_buckets(buckets_np)
    nx = len(grads)
    shapes = [g.shape for g in grads]

    def body(bk_ref, ds_ref, *rest):
        o_ref = rest[nx]
        acc_ref = rest[2 * nx + 1]
        exchange = _SiblingExchangePlan(shapes, rest[:nx], rest[nx + 1:2 * nx + 1], *rest[2 * nx + 2:])
        exchange.start()
        acc_ref[...] = jnp.zeros_like(acc_ref)
        for c in range(N_CFG):
            bk = bk_ref[c]
            for h in range(B_HEADS):
                dsv = ds_ref[c, h]
                for b in present[c]:
                    part = jnp.sum(jnp.where(bk == b, dsv, 0.0), axis=0, keepdims=True)
                    acc_ref[pl.ds(h * NUM_BUCKETS + b, 1), :] += part
        o_ref[...] = jnp.sum(acc_ref[...], axis=1, keepdims=True)
        exchange.finish()

    vm = pl.BlockSpec(memory_space=pltpu.VMEM)
    res = pl.pallas_call(
        body, in_specs=[vm, vm] + [ANY] * nx, out_specs=[vm] + [ANY] * nx,
        out_shape=[jax.ShapeDtypeStruct((B_HEADS * NUM_BUCKETS, 1), F32)]
        + [jax.ShapeDtypeStruct((N_SHARD, s[1] // 2, s[2]), g.dtype) for s, g in zip(shapes, grads)],
        scratch_shapes=[pltpu.VMEM((B_HEADS * NUM_BUCKETS, buckets_np.shape[-1]), F32)] + _sem_pair(nx),
        compiler_params=_params(), name="rel_bias_grad",
    )(jnp.asarray(buckets_np), ds, *grads)
    return res[0], list(res[1:])


def _row_index():
    return lax.broadcasted_iota(jnp.int32, (SEQ, LANE_BLOCK), 0)


def _shift_down(x, k, row):
    return jnp.where(row >= k, pltpu.roll(x, k, 0), 0.0)


def _shift_up(x, k, row):
    return jnp.where(row < SEQ - k, pltpu.roll(x, SEQ - k, 0), 0.0)


def _convgate_fwd(gate, up, conv_w, conv_b, batch):
    m = gate.shape[0]

    def body(g_ref, u_ref, w_ref, b_ref, a_ref):
        g = g_ref[...].astype(F32)
        w = w_ref[...]
        row = _row_index()
        c = b_ref[...] + w[0:1] * _shift_down(g, 2, row) + w[1:2] * _shift_down(g, 1, row) + w[2:3] * g
        a_ref[...] = (_gelu(c) * u_ref[...].astype(F32)).astype(BF16)

    blk = pl.BlockSpec((SEQ, LANE_BLOCK), lambda b, j: (b, j))
    return pl.pallas_call(
        body, grid=(batch, D_FF // LANE_BLOCK),
        in_specs=[blk, blk, pl.BlockSpec((3, LANE_BLOCK), lambda b, j: (0, j)),
                  pl.BlockSpec((1, LANE_BLOCK), lambda b, j: (0, j))],
        out_specs=blk,
        out_shape=jax.ShapeDtypeStruct((m, D_FF), BF16),
        compiler_params=_params(("parallel", "parallel")), name="convgate_fwd",
    )(gate, up, conv_w, conv_b)


def _convgate_bwd(gate, up, dact, conv_w, conv_b, batch):
    m = gate.shape[0]

    def body(g_ref, u_ref, da_ref, w_ref, b_ref, dg_ref, du_ref, dw_ref, db_ref):
        @pl.when(pl.program_id(1) == 0)
        def _():
            dw_ref[...] = jnp.zeros_like(dw_ref)
            db_ref[...] = jnp.zeros_like(db_ref)

        g = g_ref[...].astype(F32)
        w = w_ref[...]
        row = _row_index()
        g1 = _shift_down(g, 1, row)
        g2 = _shift_down(g, 2, row)
        c = b_ref[...] + w[0:1] * g2 + w[1:2] * g1 + w[2:3] * g
        gg, dgg = _gelu_and_grad(c)
        da = da_ref[...].astype(F32)
        du_ref[...] = (da * gg).astype(BF16)
        dc = da * u_ref[...].astype(F32) * dgg
        db_ref[...] += jnp.sum(dc, axis=0, keepdims=True)
        dw_ref[0:1, :] += jnp.sum(dc * g2, axis=0, keepdims=True)
        dw_ref[1:2, :] += jnp.sum(dc * g1, axis=0, keepdims=True)
        dw_ref[2:3, :] += jnp.sum(dc * g, axis=0, keepdims=True)
        dg_ref[...] = (w[2:3] * dc + w[1:2] * _shift_up(dc, 1, row) + w[0:1] * _shift_up(dc, 2, row)).astype(BF16)

    blk = pl.BlockSpec((SEQ, LANE_BLOCK), lambda j, b: (b, j))
    wspec = pl.BlockSpec((3, LANE_BLOCK), lambda j, b: (0, j))
    bspec = pl.BlockSpec((1, LANE_BLOCK), lambda j, b: (0, j))
    return pl.pallas_call(
        body, grid=(D_FF // LANE_BLOCK, batch),
        in_specs=[blk, blk, blk, wspec, bspec],
        out_specs=[blk, blk, wspec, bspec],
        out_shape=[jax.ShapeDtypeStruct((m, D_FF), BF16), jax.ShapeDtypeStruct((m, D_FF), BF16),
                   jax.ShapeDtypeStruct((3, D_FF), F32), jax.ShapeDtypeStruct((1, D_FF), F32)],
        compiler_params=_params(("parallel", "arbitrary")), name="convgate_bwd",
    )(gate, up, dact, conv_w, conv_b)


def _gather_weights(shards, conv_w_shard, rel_bias, buckets_np):
    nt = len(shards)
    shapes = [sh.shape for sh in shards]
    ts = list(range(nt))
    tables = _bias_tables_body(buckets_np)

    def body(*refs):
        shard_refs = refs[:nt]
        cw_ref, rb_ref, bk_ref = refs[nt:nt + 3]
        out_refs = refs[nt + 3:2 * nt + 3]
        cw_out, bias_ref, bias_t_ref = refs[2 * nt + 3:2 * nt + 6]
        scratch = refs[2 * nt + 6:]
        f32_refs, bf16_refs = scratch[:nt], scratch[nt:2 * nt]
        load_sems, store_sems, send_sems, recv_sems, cw_send, cw_recv = scratch[2 * nt:]
        plan = _RelayGatherPlan(shapes[:1], bf16_refs[:1], out_refs[:1], send_sems, recv_sems)
        x, y, c, chips = _mesh_pos()
        loads = [pltpu.make_async_copy(shard_refs[t], f32_refs[t], load_sems.at[t]) for t in ts]
        stores = [pltpu.make_async_copy(bf16_refs[t], out_refs[t].at[2 * x + y], store_sems.at[t]) for t in ts]
        stores.append(pltpu.make_async_copy(cw_ref, cw_out.at[2 * x + y], store_sems.at[nt]))

        def cw_copy(j, src, dst, chip):
            return pltpu.make_async_remote_copy(src_ref=src, dst_ref=dst, send_sem=cw_send.at[j],
                                                recv_sem=cw_recv.at[j], device_id=(*chip, c), device_id_type=MESH)

        def to_bf16(t):
            loads[t].wait()
            bf16_refs[t][...] = f32_refs[t][...].astype(BF16)
            stores[t].start()

        for cp in loads:
            cp.start()
        to_bf16(0)
        plan.start([0])
        cw_sends = [cw_copy(j, cw_ref, cw_out.at[2 * x + y], chip) for j, chip in enumerate(chips)]
        for cp in cw_sends + stores[nt:]:
            cp.start()
        for t in ts[1:]:
            to_bf16(t)
        plan.relay([0])
        tables(rb_ref, bk_ref, bias_ref, bias_t_ref)
        plan.forward([0])
        for j, chip in enumerate(chips):
            dst = cw_out.at[2 * chip[0] + chip[1]]
            cw_copy(j, dst, dst, chip).wait_recv()
        plan.finish([0])
        for cp in cw_sends:
            cp.wait_send()
        for cp in stores:
            cp.wait()

    out_shape = [jax.ShapeDtypeStruct((N_SHARD,) + sh.shape, BF16) for sh in shards]
    out_shape.append(jax.ShapeDtypeStruct((N_SHARD,) + conv_w_shard.shape, conv_w_shard.dtype))
    out_shape += [jax.ShapeDtypeStruct((N_CFG, B_HEADS, Q_BLOCK, 2 * Q_BLOCK), F32),
                  jax.ShapeDtypeStruct((N_CFG, B_HEADS, 2 * Q_BLOCK, Q_BLOCK), F32)]
    vm = pl.BlockSpec(memory_space=pltpu.VMEM)
    res = pl.pallas_call(
        body, in_specs=[ANY] * (nt + 1) + [pl.BlockSpec(memory_space=pltpu.SMEM), vm],
        out_specs=[ANY] * (nt + 1) + [vm, vm], out_shape=out_shape,
        scratch_shapes=[pltpu.VMEM(sh.shape, F32) for sh in shards] + [pltpu.VMEM(sh.shape, BF16) for sh in shards]
        + [pltpu.SemaphoreType.DMA((nt,)), pltpu.SemaphoreType.DMA((nt + 1,))] + _sem_pair(6) + _sem_pair(3),
        compiler_params=pltpu.CompilerParams(has_side_effects=True, vmem_limit_bytes=VMEM_LIMIT),
        name="gather_weights",
    )(*shards, conv_w_shard, rel_bias.T, jnp.asarray(buckets_np))
    return list(res[:nt + 1]), res[nt + 1], res[nt + 2]


def _turn(t, u, s, last):
    return jnp.where(t == u, s, jnp.where(t > u, last, 0))


def _add_halves(gs, recvs, c_idx):
    n = len(gs)
    _, rows2, cols = gs[0].shape
    rows = rows2 // 2
    assert all(g.shape == gs[0].shape for g in gs)

    def body(c_ref, *refs):
        t = pl.program_id(0)
        for u in range(n):
            @pl.when(t == u)
            def _(u=u):
                refs[2 * n + u][...] = (refs[u][...].astype(F32) + refs[n + u][...].astype(F32)).astype(BF16)

    def own(u):
        return pl.BlockSpec((None, None, rows, cols), lambda t, s, c: (_turn(t, u, s, N_SHARD - 1), c[0], 0, 0))

    def plain(u):
        return pl.BlockSpec((None, rows, cols), lambda t, s, c: (_turn(t, u, s, N_SHARD - 1), 0, 0))

    return pl.pallas_call(
        body,
        grid_spec=pltpu.PrefetchScalarGridSpec(
            num_scalar_prefetch=1, grid=(n, N_SHARD),
            in_specs=[own(u) for u in range(n)] + [plain(u) for u in range(n)],
            out_specs=[plain(u) for u in range(n)]),
        out_shape=[jax.ShapeDtypeStruct((N_SHARD, rows, cols), BF16)] * n,
        compiler_params=_params(("arbitrary", "arbitrary")), name="rs_add_halves",
    )(c_idx, *[g.reshape(N_SHARD, 2, rows, cols) for g in gs], *recvs)


def _add_chips(parts, recvs, s_idx, c_idx):
    n = len(parts)
    _, rows, cols = parts[0].shape
    assert all(p.shape == parts[0].shape for p in parts)

    def body(idx_ref, *refs):
        t = pl.program_id(0)
        for u in range(n):
            @pl.when(t == u)
            def _(u=u):
                acc = refs[u][...].astype(F32)
                for j in range(3):
                    acc = acc + refs[n + u][j].astype(F32)
                refs[2 * n + u][...] = acc

    res = pl.pallas_call(
        body,
        grid_spec=pltpu.PrefetchScalarGridSpec(
            num_scalar_prefetch=1, grid=(n,),
            in_specs=[pl.BlockSpec((None, rows, cols), lambda t, idx: (idx[0], 0, 0))] * n
            + [pl.BlockSpec((3, rows, cols), lambda t, idx: (0, 0, 0))] * n,
            out_specs=[pl.BlockSpec((None, rows, cols), lambda t, idx: (idx[1], 0, 0))] * n),
        out_shape=[jax.ShapeDtypeStruct((2, rows, cols), F32)] * n,
        compiler_params=_params(("arbitrary",)), name="rs_add_chips",
    )(jnp.concatenate([s_idx, c_idx]), *parts, *recvs)
    return [r.reshape(2 * rows, cols) for r in res]


def _finish_reductions(fulls, arrays):
    nt, n = len(fulls), len(arrays)

    def body(*refs):
        in_refs = refs[nt:nt + n]
        full_refs, out_refs = refs[nt + n:2 * nt + n], refs[2 * nt + n:2 * nt + 2 * n]
        pos = 2 * nt + 2 * n
        share_send, share_recv = refs[pos + 2 * n:pos + 2 * n + 2]
        allreduce = _SmallAllReducePlan(in_refs, out_refs, refs[pos:pos + n], refs[pos + n:pos + 2 * n],
                                        *refs[pos + 2 * n + 2:])
        x, y, c, _ = _mesh_pos()

        def half(t, which):
            rows = fulls[t].shape[0] // 2
            return full_refs[t].at[pl.ds(which * rows, rows), :]

        def share(t, which):
            return pltpu.make_async_remote_copy(
                src_ref=half(t, which), dst_ref=half(t, which), send_sem=share_send.at[t],
                recv_sem=share_recv.at[t], device_id=(x, y, 1 - c), device_id_type=MESH)

        for t in range(nt):
            share(t, c).start()
        allreduce.start_sibling()
        allreduce.sum_sibling_and_start_chips()
        allreduce.finish()
        for t in range(nt):
            share(t, 1 - c).wait_recv()
        for t in range(nt):
            share(t, c).wait_send()

    vm = pl.BlockSpec(memory_space=pltpu.VMEM)
    res = pl.pallas_call(
        body, in_specs=[ANY] * nt + [vm] * n, out_specs=[ANY] * nt + [vm] * n,
        out_shape=[jax.ShapeDtypeStruct(f.shape, f.dtype) for f in fulls]
        + [jax.ShapeDtypeStruct(a.shape, F32) for a in arrays],
        input_output_aliases={t: t for t in range(nt)},
        scratch_shapes=[pltpu.VMEM(a.shape, F32) for a in arrays] + [pltpu.VMEM((3,) + a.shape, F32) for a in arrays]
        + _sem_pair(nt) + _sem_pair(4 * n),
        compiler_params=pltpu.CompilerParams(has_side_effects=True),
        name="finish_reductions",
    )(*fulls, *arrays)
    return list(res[:nt]), list(res[nt:])


def _from_col_shards(g):
    n, rows, cols = g.shape
    return g.transpose(1, 0, 2).reshape(rows, n * cols)


def _train_step(x, tgt, g1, g2, g3, g4, shards, ln_g, ln_b, w_s, b_s, rel_bias, conv_w_shard, conv_b, batch,
                s_idx, c_idx):
    buckets = _bucket_tables()
    bz = jnp.repeat(b_s.T, HEAD_DIM, axis=1)
    w_st = jnp.swapaxes(w_s, 1, 2)

    def shard_major(g):
        return g.reshape(N_SHARD, g.shape[0] // N_SHARD, D_MODEL)

    names = ["w_in", "w_out", "w_gate", "w_up", "w_down"]
    (g_in, g_out, g_gate, g_up, g_down, g_convw), bias, bias_t = _gather_weights(
        [shards[n] for n in names], conv_w_shard, rel_bias, buckets)
    w_in_t = g_in.reshape(IN_COLS, D_MODEL)
    conv_w = _from_col_shards(g_convw.reshape(N_SHARD, 3, SHARD_FF))

    h1, uv, qkv, a = _proj_fwd(x, g1, w_in_t, ln_g, ln_b, w_s, bz)
    o_bf, lse, (g_out, g_gate, g_up) = _attn_fwd(qkv, bias, batch, [g_out, g_gate, g_up])
    w_out = g_out.reshape(D_MODEL, D_MODEL)
    w_gate_t = g_gate.reshape(D_FF, D_MODEL)
    w_up_t = g_up.reshape(D_FF, D_MODEL)
    (y1, x1, h2), _ = _fused_rows(
        "out_proj_mid_fwd", 512,
        [(a, w_out, "nn", slice(0, A_WIDTH)), (o_bf, w_out, "nn", slice(A_WIDTH, D_MODEL))],
        [x], [g2, g3], _mid_fwd_rows, [F32, F32, BF16], [])
    gate, up, g_down = _mm_pair_nt(h2, w_gate_t, w_up_t, g_down, tm=1024, tn=1408, out_dtype=BF16,
                                   name="mm_gate_up")
    w_down = g_down.reshape(D_FF, D_MODEL)
    act = _convgate_fwd(gate, up, conv_w, conv_b, batch)
    (dx2, dy2, dg4, loss), _ = _fused_rows(
        "down_proj_loss_head", 512, [(act, w_down, "nn", None)], [x1, tgt], [g4], _loss_head_rows,
        [F32, BF16], [(1, D_MODEL), (1, 128)])

    dact = _mm(dy2, w_down, dims="nt", tm=1024, tn=1408, tk=1024, out_dtype=BF16, name="mm_dact")
    dw_down = _mm(act, dy2, dims="tn", tm=1408, tn=1024, tk=1024, out_dtype=BF16, name="mm_dw_down")
    dgate, dup, dconv_w, dconv_b = _convgate_bwd(gate, up, dact, conv_w, conv_b, batch)
    (dx1, dy1, dg2, dg3), _ = _fused_rows(
        "dh2_mid_bwd", 256, [(dgate, w_gate_t, "nn", None), (dup, w_up_t, "nn", None)],
        [x1, y1, dx2], [g2, g3], _mid_bwd_rows, [F32, BF16], [(1, D_MODEL), (1, D_MODEL)])
    dw_gate_t, dw_up_t = _mm_pair_tn(dgate, dup, h2, tm=1408, tk=1024, name="mm_dw_gate_up")
    done = [shard_major(g) for g in (dw_down, dw_gate_t, dw_up_t)]
    (dmix, dw_out), recv_a = _out_proj_bwd(a, o_bf, dy1, w_out, done[:2])
    done.append(shard_major(dw_out))
    (dproj, dln_g, dln_b, dw_s, dbz), recv_b = _gate_bwd(uv, dmix, ln_g, ln_b, w_s, w_st, bz, done[2:])
    recv_a += recv_b
    parts = _add_halves(done[:3], recv_a[:3], c_idx) + _add_halves(done[3:], recv_a[3:], c_idx)
    early = dict(loss=loss, norm_mix_post=dg2, norm_ffn_pre=dg3, norm_ffn_post=dg4, ln_v_gain=dln_g,
                 ln_v_bias=dln_b, spatial_w=dw_s, spatial_b=dbz, conv_w=dconv_w, conv_b=dconv_b)
    dproj, ds, recv, early_sums = _attn_bwd(qkv, dmix, o_bf, lse, bias_t, dproj, batch, parts, list(early.values()))
    fulls = _add_chips(parts[:3], recv[:3], s_idx, c_idx) + _add_chips(parts[3:], recv[3:], s_idx, c_idx)
    dw_in_t = _mm(dproj, h1, dims="tn", tm=1408, tn=1024, tk=1024, out_dtype=BF16, name="mm_dw_in")
    last = [shard_major(dw_in_t)]
    drel, recv_in_a = _rel_bias_grad(ds, np.ascontiguousarray(np.swapaxes(buckets, 1, 2)), last)
    part_in = _add_halves(last, recv_in_a, c_idx)
    (dx0, dg1), recv_in = _fused_rows(
        "dh1_in_bwd", 512, [(dproj, w_in_t, "nn", None)], [x, dx1], [g1], _in_bwd_rows,
        [F32], [(1, D_MODEL)], exchange=part_in)
    fulls += _add_chips(part_in, recv_in, s_idx, c_idx)
    half_reduced = dict(zip(["w_down", "w_gate", "w_up", "w_out", "w_in"], fulls))

    return dx0, dict(zip(early, early_sums)), dict(norm_mix_pre=dg1, rel_bias=drel), half_reduced


def _adamw_update(w, g, m, v):
    nm = ADAM_B1 * m + (1.0 - ADAM_B1) * g
    nv = ADAM_B2 * v + (1.0 - ADAM_B2) * (g * g)
    m_hat = nm / (1.0 - ADAM_B1 ** ADAM_STEP)
    v_hat = nv / (1.0 - ADAM_B2 ** ADAM_STEP)
    return -ADAM_LR * (m_hat / (jnp.sqrt(v_hat) + ADAM_EPS) + ADAM_WD * w), nm, nv


def _adamw(w, g, m, v, name):
    rows, cols = w.shape
    tr = next(cand for cand in (352, 256, 128) if rows % cand == 0)

    def body(w_ref, g_ref, m_ref, v_ref, go_ref, d_ref, nm_ref, nv_ref):
        gv = g_ref[...]
        go_ref[...] = gv
        d_ref[...], nm_ref[...], nv_ref[...] = _adamw_update(w_ref[...], gv, m_ref[...], v_ref[...])

    spec = pl.BlockSpec((tr, cols), lambda i: (i, 0))
    sds = jax.ShapeDtypeStruct((rows, cols), F32)
    return pl.pallas_call(
        body, grid=(rows // tr,), in_specs=[spec] * 4, out_specs=[spec] * 4, out_shape=[sds] * 4,
        compiler_params=_params(("parallel",)), name=name,
    )(w, g, m, v)


def _adamw_small(ws, gs, ms, vs):
    n = len(ws)

    def body(*refs):
        w_refs, g_refs, m_refs, v_refs = refs[:n], refs[n:2 * n], refs[2 * n:3 * n], refs[3 * n:4 * n]
        d_refs, nm_refs, nv_refs = refs[4 * n:5 * n], refs[5 * n:6 * n], refs[6 * n:7 * n]
        for t in range(n):
            d_refs[t][...], nm_refs[t][...], nv_refs[t][...] = _adamw_update(
                w_refs[t][...], g_refs[t][...], m_refs[t][...], v_refs[t][...])

    vm = pl.BlockSpec(memory_space=pltpu.VMEM)
    sds = [jax.ShapeDtypeStruct(w.shape, F32) for w in ws]
    res = pl.pallas_call(
        body, in_specs=[vm] * (4 * n), out_specs=[vm] * (3 * n), out_shape=sds * 3,
        compiler_params=_params(), name="adamw_small",
    )(*ws, *gs, *ms, *vs)
    return res[:n], res[n:2 * n], res[2 * n:]


SMALL = ["norm_mix_pre", "norm_mix_post", "norm_ffn_pre", "norm_ffn_post", "ln_v_gain", "ln_v_bias",
         "spatial_w", "spatial_b", "rel_bias", "conv_b"]
LARGE = ["w_in", "w_gate", "w_up", "w_down", "w_out"]
TRANSPOSED = ("w_in", "w_gate", "w_up")
ORDER = ["norm_mix_pre", "norm_mix_post", "norm_ffn_pre", "norm_ffn_post", "w_in", "ln_v_gain", "ln_v_bias",
         "spatial_w", "spatial_b", "rel_bias", "w_out", "w_gate", "w_up", "conv_w", "conv_b", "w_down"]


def kernel(x, norm_mix_pre, norm_mix_post, norm_ffn_pre, norm_ffn_post, w_in, ln_v_gain, ln_v_bias, spatial_w, spatial_b, rel_bias, w_out, w_gate, w_up, conv_w, conv_b, w_down, loss_target, m_norm_mix_pre, m_norm_mix_post, m_norm_ffn_pre, m_norm_ffn_post, m_w_in, m_ln_v_gain, m_ln_v_bias, m_spatial_w, m_spatial_b, m_rel_bias, m_w_out, m_w_gate, m_w_up, m_conv_w, m_conv_b, m_w_down, v_norm_mix_pre, v_norm_mix_post, v_norm_ffn_pre, v_norm_ffn_post, v_w_in, v_ln_v_gain, v_ln_v_bias, v_spatial_w, v_spatial_b, v_rel_bias, v_w_out, v_w_gate, v_w_up, v_conv_w, v_conv_b, v_w_down):
    params = dict(norm_mix_pre=norm_mix_pre, norm_mix_post=norm_mix_post, norm_ffn_pre=norm_ffn_pre,
                  norm_ffn_post=norm_ffn_post, w_in=w_in, ln_v_gain=ln_v_gain, ln_v_bias=ln_v_bias,
                  spatial_w=spatial_w, spatial_b=spatial_b, rel_bias=rel_bias, w_out=w_out, w_gate=w_gate,
                  w_up=w_up, conv_w=conv_w, conv_b=conv_b, w_down=w_down)
    mom = dict(norm_mix_pre=m_norm_mix_pre, norm_mix_post=m_norm_mix_post, norm_ffn_pre=m_norm_ffn_pre,
               norm_ffn_post=m_norm_ffn_post, w_in=m_w_in, ln_v_gain=m_ln_v_gain, ln_v_bias=m_ln_v_bias,
               spatial_w=m_spatial_w, spatial_b=m_spatial_b, rel_bias=m_rel_bias, w_out=m_w_out, w_gate=m_w_gate,
               w_up=m_w_up, conv_w=m_conv_w, conv_b=m_conv_b, w_down=m_w_down)
    var = dict(norm_mix_pre=v_norm_mix_pre, norm_mix_post=v_norm_mix_post, norm_ffn_pre=v_norm_ffn_pre,
               norm_ffn_post=v_norm_ffn_post, w_in=v_w_in, ln_v_gain=v_ln_v_gain, ln_v_bias=v_ln_v_bias,
               spatial_w=v_spatial_w, spatial_b=v_spatial_b, rel_bias=v_rel_bias, w_out=v_w_out, w_gate=v_w_gate,
               w_up=v_w_up, conv_w=v_conv_w, conv_b=v_conv_b, w_down=v_w_down)

    batch = x.shape[0]
    xi, yi, ci = lax.axis_index("x"), lax.axis_index("y"), lax.axis_index("c")
    s_idx = (2 * xi + yi).astype(jnp.int32).reshape(1)
    c_idx = ci.astype(jnp.int32).reshape(1)

    def local(a, n):
        return jnp.swapaxes(a[0], 0, 1) if n in TRANSPOSED else a[0]

    shards = {n: local(params[n], n) for n in LARGE}
    dx0, total, partial, half_reduced = _train_step(
        x.reshape(batch * SEQ, D_MODEL), loss_target.reshape(batch * SEQ, D_MODEL),
        norm_mix_pre, norm_mix_post, norm_ffn_pre, norm_ffn_post, shards,
        ln_v_gain.reshape(1, A_WIDTH), ln_v_bias.reshape(1, A_WIDTH), spatial_w[0], spatial_b[0], rel_bias,
        jnp.swapaxes(conv_w, 0, 1), conv_b, batch, s_idx, c_idx)
    grad_x = dx0.reshape(batch, SEQ, D_MODEL)

    names = list(partial)
    fulls, sums = _finish_reductions([half_reduced[n] for n in LARGE], [partial[n] for n in names])
    reduced = dict(zip(LARGE, fulls))
    total.update(zip(names, sums))
    loss = total["loss"][0, 0]
    total["spatial_b"] = total["spatial_b"][:, ::HEAD_DIM].T
    total["rel_bias"] = total["rel_bias"].reshape(B_HEADS, NUM_BUCKETS)
    total["conv_w"] = lax.dynamic_slice_in_dim(total["conv_w"], s_idx[0] * SHARD_FF, SHARD_FF, axis=1)
    small_names = SMALL + ["conv_w"]

    def small(a, n):
        return jnp.swapaxes(a, 0, 1) if n in ("rel_bias", "conv_w") else a

    for n in small_names:
        reduced[n] = total[n].reshape(small(params[n], n).shape)

    out_g, out_d, out_m, out_v = {}, {}, {}, {}
    for n in LARGE:
        res = _adamw(local(params[n], n), reduced[n], local(mom[n], n), local(var[n], n), name=f"adamw_{n}")
        if n in TRANSPOSED:
            res = [jnp.swapaxes(r, 0, 1) for r in res]
        out_g[n], out_d[n], out_m[n], out_v[n] = [r[None] for r in res]
    d, nm, nv = _adamw_small([small(params[n], n) for n in small_names], [reduced[n] for n in small_names],
                             [small(mom[n], n) for n in small_names], [small(var[n], n) for n in small_names])
    for n, dd, mm, vv in zip(small_names, d, nm, nv):
        out_g[n], out_d[n], out_m[n], out_v[n] = [small(r, n) for r in (reduced[n], dd, mm, vv)]

    return (loss, grad_x, *[out_g[n] for n in ORDER], *[out_d[n] for n in ORDER],
            *[out_m[n] for n in ORDER], *[out_v[n] for n in ORDER])
```

```python
import functools
import math

import numpy as np
import jax
import jax.numpy as jnp
from jax import lax
from jax.experimental import pallas as pl
from jax.experimental.pallas import tpu as pltpu

F32 = jnp.float32
BF16 = jnp.bfloat16
MESH = pl.DeviceIdType.MESH

D_MODEL = 1024
SEQ = 2048
HEAD_DIM = 64
A_GROUPS = 4
A_WIDTH = 256
B_HEADS = 12
B_WIDTH = 768
CHUNK = 128
DILATED = ((128, 1), (512, 4), (2048, 16))
NUM_BUCKETS = 32
MAX_DISTANCE = 2048
D_FF = 2816
IN_COLS = 2816
NORM_EPS = 1e-6
NEG_INF = -1e30
N_SHARD = 4
SHARD_FF = D_FF // N_SHARD
LANE_BLOCK = 256
VMEM_LIMIT = 56 * 1024 * 1024

ADAM_LR = 0.001
ADAM_B1 = 0.9
ADAM_B2 = 0.999
ADAM_EPS = 1e-08
ADAM_WD = 0.01
ADAM_STEP = 10

GELU_C = math.sqrt(2.0 / math.pi)
GELU_A = 0.044715

ANY = pl.BlockSpec(memory_space=pl.ANY)


def _params(sem=None):
    return pltpu.CompilerParams(dimension_semantics=sem, vmem_limit_bytes=VMEM_LIMIT)


def _dot(a, b, precision=None):
    return jnp.dot(a, b, preferred_element_type=F32, precision=precision)


def _dot_nt(a, b, precision=None):
    return lax.dot_general(a, b, (((1,), (1,)), ((), ())), preferred_element_type=F32, precision=precision)


def _dot_tn(a, b):
    return lax.dot_general(a, b, (((0,), (0,)), ((), ())), preferred_element_type=F32)


def _gelu(x):
    t = jnp.tanh(x * (GELU_C + (GELU_C * GELU_A) * (x * x)))
    return (0.5 * x) * (1.0 + t)


def _gelu_and_grad(x):
    x2 = x * x
    u = 1.0 + jnp.tanh(x * (GELU_C + (GELU_C * GELU_A) * x2))
    hx = 0.5 * x
    dg = u * (0.5 + hx * (2.0 - u) * (GELU_C + (3.0 * GELU_C * GELU_A) * x2))
    return hx * u, dg


def _mesh_pos():
    x, y, c = lax.axis_index("x"), lax.axis_index("y"), lax.axis_index("c")
    chips = [(1 - x, y), (x, 1 - y), (1 - x, 1 - y)]
    return x, y, c, chips


class _GatherPlan:
    def __init__(self, shapes, out_refs, send_sems, recv_sems):
        self.shapes, self.out_refs = shapes, out_refs
        self.send_sems, self.recv_sems = send_sems, recv_sems
        self.x, self.y, self.c, self.chips = _mesh_pos()
        self.sib = (self.x, self.y, 1 - self.c)

    def _half(self, t, chip, which):
        rows = self.shapes[t][0] // 2
        return self.out_refs[t].at[2 * chip[0] + chip[1], pl.ds(which * rows, rows), :]

    def _copy(self, k, src, dst, to):
        return pltpu.make_async_remote_copy(src_ref=src, dst_ref=dst, send_sem=self.send_sems.at[k],
                                            recv_sem=self.recv_sems.at[k], device_id=to, device_id_type=MESH)

    def _sends(self, t):
        own = self._half(t, (self.x, self.y), self.c)
        return [self._copy(6 * t + j, own, own, (*chip, self.c)) for j, chip in enumerate(self.chips)]

    def _forwards(self, t):
        return [self._copy(6 * t + 3 + j, self._half(t, chip, self.c), self._half(t, chip, self.c), self.sib)
                for j, chip in enumerate(self.chips)]

    def start(self, ts):
        for t in ts:
            for cp in self._sends(t):
                cp.start()

    def forward(self, ts):
        for t in ts:
            for j, chip in enumerate(self.chips):
                landed = self._half(t, chip, self.c)
                self._copy(6 * t + j, landed, landed, (*chip, self.c)).wait_recv()
            for cp in self._forwards(t):
                cp.start()

    def finish(self, ts):
        for t in ts:
            for j, chip in enumerate(self.chips):
                other = self._half(t, chip, 1 - self.c)
                self._copy(6 * t + 3 + j, other, other, self.sib).wait_recv()
        for t in ts:
            for cp in self._sends(t) + self._forwards(t):
                cp.wait_send()


class _RelayGatherPlan:
    def __init__(self, shapes, shard_refs, out_refs, send_sems, recv_sems):
        self.shapes, self.shard_refs, self.out_refs = shapes, shard_refs, out_refs
        self.send_sems, self.recv_sems = send_sems, recv_sems
        x, y, c, self.chips = _mesh_pos()
        self.me, self.c, self.sib = (x, y), c, (x, y, 1 - c)
        self.first = (x + c - 2 * x * c, y + (1 - c) - 2 * y * (1 - c))
        self.second = (x + (1 - c) - 2 * x * (1 - c), y + c - 2 * y * c)
        self.diag = (1 - x, 1 - y)

    def _half(self, t, chip, which):
        rows = self.shapes[t][0] // 2
        return self.out_refs[t].at[2 * chip[0] + chip[1], pl.ds(which * rows, rows), :]

    def _copy(self, k, src, dst, to):
        return pltpu.make_async_remote_copy(src_ref=src, dst_ref=dst, send_sem=self.send_sems.at[k],
                                            recv_sem=self.recv_sems.at[k], device_id=to, device_id_type=MESH)

    def _own(self, t):
        if self.shard_refs is None:
            return self._half(t, self.me, self.c)
        rows = self.shapes[t][0] // 2
        return self.shard_refs[t].at[pl.ds(self.c * rows, rows), :]

    def _step1(self, t):
        return self._copy(6 * t, self._own(t), self._half(t, self.me, self.c), (*self.first, self.c))

    def _step2(self, t):
        landed = self._half(t, self.first, self.c)
        return [self._copy(6 * t + 1, self._own(t), self._half(t, self.me, self.c), (*self.second, self.c)),
                self._copy(6 * t + 2, landed, landed, (*self.second, self.c))]

    def _forwards(self, t):
        return [self._copy(6 * t + 3 + j, self._half(t, chip, self.c), self._half(t, chip, self.c), self.sib)
                for j, chip in enumerate(self.chips)]

    def start(self, ts):
        for t in ts:
            self._step1(t).start()
            self._step2(t)[0].start()

    def relay(self, ts):
        for t in ts:
            landed = self._half(t, self.first, self.c)
            self._copy(6 * t, landed, landed, self.sib).wait_recv()
            self._step2(t)[1].start()

    def forward(self, ts):
        for t in ts:
            for k, chip in ((1, self.second), (2, self.diag)):
                landed = self._half(t, chip, self.c)
                self._copy(6 * t + k, landed, landed, self.sib).wait_recv()
            for cp in self._forwards(t):
                cp.start()

    def finish(self, ts):
        for t in ts:
            for j, chip in enumerate(self.chips):
                other = self._half(t, chip, 1 - self.c)
                self._copy(6 * t + 3 + j, other, other, self.sib).wait_recv()
        for t in ts:
            for cp in [self._step1(t)] + self._step2(t) + self._forwards(t):
                cp.wait_send()


class _SiblingExchangePlan:
    def __init__(self, shapes, grad_refs, out_refs, send_sems, recv_sems):
        self.shapes, self.grad_refs, self.out_refs = shapes, grad_refs, out_refs
        self.send_sems, self.recv_sems = send_sems, recv_sems
        self.x, self.y, self.c, _ = _mesh_pos()

    def _copies(self):
        out = []
        for t, (g, o) in enumerate(zip(self.grad_refs, self.out_refs)):
            rows = self.shapes[t][1] // 2
            out.append(pltpu.make_async_remote_copy(
                src_ref=g.at[:, pl.ds((1 - self.c) * rows, rows), :], dst_ref=o, send_sem=self.send_sems.at[t],
                recv_sem=self.recv_sems.at[t], device_id=(self.x, self.y, 1 - self.c), device_id_type=MESH))
        return out

    def start(self):
        for cp in self._copies():
            cp.start()

    def finish(self):
        for cp in self._copies():
            cp.wait()


class _ChipExchangePlan:
    def __init__(self, part_refs, out_refs, send_sems, recv_sems):
        self.part_refs, self.out_refs, self.send_sems, self.recv_sems = part_refs, out_refs, send_sems, recv_sems
        _, _, self.c, self.chips = _mesh_pos()

    def _copies(self):
        return [pltpu.make_async_remote_copy(
            src_ref=p.at[2 * chip[0] + chip[1]], dst_ref=o.at[j], send_sem=self.send_sems.at[3 * t + j],
            recv_sem=self.recv_sems.at[3 * t + j], device_id=(*chip, self.c), device_id_type=MESH)
            for t, (p, o) in enumerate(zip(self.part_refs, self.out_refs)) for j, chip in enumerate(self.chips)]

    def start(self):
        for cp in self._copies():
            cp.start()

    def finish(self):
        for cp in self._copies():
            cp.wait()


class _SmallAllReducePlan:
    def __init__(self, in_refs, out_refs, sib_refs, chip_refs, send_sems, recv_sems):
        self.in_refs, self.out_refs, self.sib_refs, self.chip_refs = in_refs, out_refs, sib_refs, chip_refs
        self.send_sems, self.recv_sems = send_sems, recv_sems
        self.n = len(in_refs)
        self.x, self.y, self.c, self.chips = _mesh_pos()

    def _copy(self, k, src, dst, to):
        return pltpu.make_async_remote_copy(src_ref=src, dst_ref=dst, send_sem=self.send_sems.at[k],
                                            recv_sem=self.recv_sems.at[k], device_id=to, device_id_type=MESH)

    def _first(self):
        return [self._copy(t, self.in_refs[t], self.sib_refs[t], (self.x, self.y, 1 - self.c)) for t in range(self.n)]

    def _second(self):
        return [self._copy(self.n + 3 * t + j, self.out_refs[t], self.chip_refs[t].at[j], (*chip, self.c))
                for t in range(self.n) for j, chip in enumerate(self.chips)]

    def start_sibling(self):
        for cp in self._first():
            cp.start()

    def sum_sibling_and_start_chips(self):
        for cp in self._first():
            cp.wait()
        for t in range(self.n):
            self.out_refs[t][...] = self.in_refs[t][...] + self.sib_refs[t][...]
        for cp in self._second():
            cp.start()

    def finish(self):
        for cp in self._second():
            cp.wait()
        for t in range(self.n):
            self.out_refs[t][...] = ((self.out_refs[t][...] + self.chip_refs[t][0])
                                     + (self.chip_refs[t][1] + self.chip_refs[t][2]))

    @staticmethod
    def scratch(arrays):
        return ([pltpu.VMEM(a.shape, F32) for a in arrays] + [pltpu.VMEM((3,) + a.shape, F32) for a in arrays]
                + _sem_pair(4 * len(arrays)))


def _sem_pair(n):
    return [pltpu.SemaphoreType.DMA((n,)), pltpu.SemaphoreType.DMA((n,))]


def _mm(a, b, *, dims, tm, tn, tk, out_dtype, name):
    if dims == "nn":
        m, k = a.shape
        n = b.shape[1]
        a_spec = pl.BlockSpec((tm, tk), lambda i, j, kk: (i, kk))
        b_spec = pl.BlockSpec((tk, tn), lambda i, j, kk: (kk, j))
        dot = _dot
    elif dims == "nt":
        m, k = a.shape
        n = b.shape[0]
        a_spec = pl.BlockSpec((tm, tk), lambda i, j, kk: (i, kk))
        b_spec = pl.BlockSpec((tn, tk), lambda i, j, kk: (j, kk))
        dot = _dot_nt
    else:
        k, m = a.shape
        n = b.shape[1]
        a_spec = pl.BlockSpec((tk, tm), lambda i, j, kk: (kk, i))
        b_spec = pl.BlockSpec((tk, tn), lambda i, j, kk: (kk, j))
        dot = _dot_tn
    assert m % tm == 0 and n % tn == 0 and k % tk == 0, (name, m, n, k)
    grid = (m // tm, n // tn, k // tk)
    nk = grid[2]
    own_acc = nk > 1 and out_dtype != F32

    def body(a_ref, b_ref, o_ref, *scratch):
        prod = dot(a_ref[...].astype(BF16), b_ref[...].astype(BF16))
        if nk == 1:
            o_ref[...] = prod.astype(out_dtype)
        else:
            acc_ref = scratch[0] if own_acc else o_ref
            kk = pl.program_id(2)

            @pl.when(kk == 0)
            def _():
                acc_ref[...] = prod

            @pl.when(kk > 0)
            def _():
                acc_ref[...] += prod

            if own_acc:
                @pl.when(kk == nk - 1)
                def _():
                    o_ref[...] = acc_ref[...].astype(out_dtype)

    return pl.pallas_call(
        body, grid=grid, in_specs=[a_spec, b_spec],
        out_specs=pl.BlockSpec((tm, tn), lambda i, j, kk: (i, j)),
        out_shape=jax.ShapeDtypeStruct((m, n), out_dtype),
        scratch_shapes=[pltpu.VMEM((tm, tn), F32)] if own_acc else [],
        compiler_params=_params(("parallel", "parallel", "arbitrary")), name=name,
    )(a, b)


def _mm_pair_tn(a1, a2, b, *, tm, tk, name):
    k, m = a1.shape
    n = b.shape[1]
    assert m % tm == 0 and k % tk == 0 and a2.shape == a1.shape, name
    nk = k // tk

    def body(a1_ref, a2_ref, b_ref, o1_ref, o2_ref, acc1_ref, acc2_ref):
        bv = b_ref[...]
        p1 = _dot_tn(a1_ref[...], bv)
        p2 = _dot_tn(a2_ref[...], bv)
        kk = pl.program_id(1)

        @pl.when(kk == 0)
        def _():
            acc1_ref[...] = p1
            acc2_ref[...] = p2

        @pl.when(kk > 0)
        def _():
            acc1_ref[...] += p1
            acc2_ref[...] += p2

        @pl.when(kk == nk - 1)
        def _():
            o1_ref[...] = acc1_ref[...].astype(BF16)
            o2_ref[...] = acc2_ref[...].astype(BF16)

    a_spec = pl.BlockSpec((tk, tm), lambda i, kk: (kk, i))
    o_spec = pl.BlockSpec((tm, n), lambda i, kk: (i, 0))
    return pl.pallas_call(
        body, grid=(m // tm, nk),
        in_specs=[a_spec, a_spec, pl.BlockSpec((tk, n), lambda i, kk: (kk, 0))],
        out_specs=[o_spec, o_spec],
        out_shape=[jax.ShapeDtypeStruct((m, n), BF16)] * 2,
        scratch_shapes=[pltpu.VMEM((tm, n), F32)] * 2,
        compiler_params=_params(("parallel", "arbitrary")), name=name,
    )(a1, a2, b)


def _mm_pair_nt(a, w1_t, w2_t, own, *, tm, tn, out_dtype, name):
    m, k = a.shape
    n = w1_t.shape[0]
    assert m % tm == 0 and n % tn == 0 and w2_t.shape == w1_t.shape, name
    grid = (m // tm, n // tn)
    n_steps = grid[0] * grid[1]

    def body(a_ref, w1_ref, w2_ref, own_ref, o1_ref, o2_ref, gat_ref, send_sems, recv_sems):
        del own_ref
        step = pl.program_id(0) * grid[1] + pl.program_id(1)
        gather = _GatherPlan([own.shape[1:]], [gat_ref], send_sems, recv_sems)

        @pl.when(step == 0)
        def _():
            gather.start([0])

        @pl.when(step == (2 * n_steps) // 3)
        def _():
            gather.forward([0])

        av = a_ref[...]
        o1_ref[...] = _dot_nt(av, w1_ref[...]).astype(out_dtype)
        o2_ref[...] = _dot_nt(av, w2_ref[...]).astype(out_dtype)

        @pl.when(step == n_steps - 1)
        def _():
            gather.finish([0])

    w_spec = pl.BlockSpec((tn, k), lambda i, j: (j, 0))
    o_spec = pl.BlockSpec((tm, tn), lambda i, j: (i, j))
    return pl.pallas_call(
        body, grid=grid,
        in_specs=[pl.BlockSpec((tm, k), lambda i, j: (i, 0)), w_spec, w_spec, ANY],
        out_specs=[o_spec, o_spec, ANY],
        out_shape=[jax.ShapeDtypeStruct((m, n), out_dtype)] * 2 + [jax.ShapeDtypeStruct(own.shape, own.dtype)],
        scratch_shapes=_sem_pair(6), input_output_aliases={3: 2},
        compiler_params=_params(("arbitrary", "arbitrary")), name=name,
    )(a, w1_t, w2_t, own)


def _out_proj_bwd(a, o, dy1, w_out, grads):
    m = dy1.shape[0]
    tm = 1024
    nx = len(grads)
    shapes = [g.shape for g in grads]
    n_steps = m // tm

    def body(a_ref, o_ref, dy_ref, w_ref, *rest):
        grad_refs = rest[:nx]
        dmix_ref, dw_ref = rest[nx:nx + 2]
        acc_ref = rest[2 * nx + 2]
        exchange = _SiblingExchangePlan(shapes, grad_refs, rest[nx + 2:2 * nx + 2], *rest[2 * nx + 3:])

        @pl.when(pl.program_id(0) == 0)
        def _():
            exchange.start()

        dy = dy_ref[...]
        dmix_ref[...] = _dot_nt(dy, w_ref[...])
        top = _dot_tn(a_ref[...], dy)
        bottom = _dot_tn(o_ref[...], dy)

        @pl.when(pl.program_id(0) == 0)
        def _():
            acc_ref[:A_WIDTH, :] = top
            acc_ref[A_WIDTH:, :] = bottom

        @pl.when(pl.program_id(0) > 0)
        def _():
            acc_ref[:A_WIDTH, :] += top
            acc_ref[A_WIDTH:, :] += bottom

        @pl.when(pl.program_id(0) == n_steps - 1)
        def _():
            dw_ref[...] = acc_ref[...].astype(BF16)
            exchange.finish()

    tile = lambda width: pl.BlockSpec((tm, width), lambda i: (i, 0))
    res = pl.pallas_call(
        body, grid=(n_steps,),
        in_specs=[tile(A_WIDTH), tile(B_WIDTH), tile(D_MODEL), _full_spec((D_MODEL, D_MODEL))] + [ANY] * nx,
        out_specs=[tile(D_MODEL), _full_spec((D_MODEL, D_MODEL))] + [ANY] * nx,
        out_shape=[jax.ShapeDtypeStruct((m, D_MODEL), F32), jax.ShapeDtypeStruct((D_MODEL, D_MODEL), BF16)]
        + [jax.ShapeDtypeStruct((N_SHARD, s[1] // 2, s[2]), g.dtype) for s, g in zip(shapes, grads)],
        scratch_shapes=[pltpu.VMEM((D_MODEL, D_MODEL), F32)] + _sem_pair(nx),
        compiler_params=_params(("arbitrary",)), name="out_proj_bwd",
    )(a, o, dy1, w_out, *grads)
    return res[:2], list(res[2:])


def _fused_rows(name, tm, mats, rows, vecs, fn, row_outs, acc_outs, exchange=()):
    m = mats[0][0].shape[0]
    nm, nr, nv, nro, nao, nx = len(mats), len(rows), len(vecs), len(row_outs), len(acc_outs), len(exchange)
    n_steps = m // tm

    def body(*refs):
        a_refs, w_refs = refs[:nm], refs[nm:2 * nm]
        pos = 2 * nm
        row_refs, vec_refs, part_refs = refs[pos:pos + nr], refs[pos + nr:pos + nr + nv], refs[pos + nr + nv:pos + nr + nv + nx]
        pos += nr + nv + nx
        out_refs, acc_refs, recv_refs = refs[pos:pos + nro], refs[pos + nro:pos + nro + nao], refs[pos + nro + nao:pos + nro + nao + nx]
        sems = refs[pos + nro + nao + nx:]
        i = pl.program_id(0)
        if nx:
            plan = _ChipExchangePlan(part_refs, recv_refs, *sems)

            @pl.when(i == 0)
            def _():
                plan.start()

        @pl.when(i == 0)
        def _():
            for r in acc_refs:
                r[...] = jnp.zeros_like(r)

        y = None
        for a_ref, w_ref, (_, _, dims, sl) in zip(a_refs, w_refs, mats):
            w = w_ref[...] if sl is None else w_ref[sl, :]
            part = (_dot if dims == "nn" else _dot_nt)(a_ref[...], w)
            y = part if y is None else y + part
        res = fn(y, *[r[...] for r in row_refs], *[v[...] for v in vec_refs])
        for r, val in zip(out_refs, res[:nro]):
            r[...] = val.astype(r.dtype)
        for r, val in zip(acc_refs, res[nro:]):
            r[...] += val

        if nx:
            @pl.when(i == n_steps - 1)
            def _():
                plan.finish()

    tile = lambda width: pl.BlockSpec((tm, width), lambda i: (i, 0))
    res = pl.pallas_call(
        body, grid=(n_steps,),
        in_specs=[tile(a.shape[1]) for a, _, _, _ in mats] + [_full_spec(w.shape) for _, w, _, _ in mats]
        + [tile(D_MODEL)] * nr + [_full_spec((1, D_MODEL))] * nv + [ANY] * nx,
        out_specs=[tile(D_MODEL)] * nro + [_full_spec(s) for s in acc_outs] + [ANY] * nx,
        out_shape=[jax.ShapeDtypeStruct((m, D_MODEL), dt) for dt in row_outs]
        + [jax.ShapeDtypeStruct(s, F32) for s in acc_outs]
        + [jax.ShapeDtypeStruct((3,) + p.shape[1:], p.dtype) for p in exchange],
        scratch_shapes=_sem_pair(3 * nx) if nx else [],
        compiler_params=_params(("arbitrary",)), name=name,
    )(*[a for a, _, _, _ in mats], *[w for _, w, _, _ in mats], *rows, *vecs, *exchange)
    return list(res[:nro + nao]), list(res[nro + nao:])


def _vec_spec(width=D_MODEL):
    return pl.BlockSpec((1, width), lambda i: (0, 0))


def _rstd(v):
    return lax.rsqrt(jnp.mean(v * v, axis=-1, keepdims=True) + NORM_EPS)


def _mid_fwd_rows(y1, x0, g2, g3):
    x1 = x0 + y1 * _rstd(y1) * g2
    return y1, x1, x1 * _rstd(x1) * g3


def _rms_bwd_rows(dout, v, g):
    r = _rstd(v)
    n = v * r
    dn = dout * g
    dv = r * (dn - n * jnp.mean(dn * n, axis=-1, keepdims=True))
    dg = jnp.sum(dout * n, axis=0, keepdims=True)
    return dv, dg


def _loss_head_rows(y2, x1, tgt, g4):
    x2 = x1 + y2 * _rstd(y2) * g4
    err = x2 - tgt
    loss = 0.5 * jnp.sum(jnp.mean(err * err, axis=-1, keepdims=True), axis=0, keepdims=True)
    dx2 = err * (1.0 / D_MODEL)
    dy2, dg4 = _rms_bwd_rows(dx2, y2, g4)
    return dx2, dy2, dg4, loss


def _mid_bwd_rows(dh2, x1, y1, dx2, g2, g3):
    d3, dg3 = _rms_bwd_rows(dh2, x1, g3)
    dx1 = dx2 + d3
    dy1, dg2 = _rms_bwd_rows(dx1, y1, g2)
    return dx1, dy1, dg2, dg3


def _in_bwd_rows(dh1, x0, dx1, g1):
    d1, dg1 = _rms_bwd_rows(dh1, x0, g1)
    return dx1 + d1, dg1


GATE_ROWS = 512


def _group_mean_matrix():
    p = np.zeros((A_WIDTH, A_WIDTH), np.float32)
    for g in range(A_GROUPS):
        p[g * HEAD_DIM:(g + 1) * HEAD_DIM, g * HEAD_DIM:(g + 1) * HEAD_DIM] = 1.0 / HEAD_DIM
    return jnp.asarray(p)


def _group_masks(width=A_WIDTH):
    lane = lax.broadcasted_iota(jnp.int32, (1, width), 1)
    return [(lane >= g * HEAD_DIM) & (lane < (g + 1) * HEAD_DIM) for g in range(width // HEAD_DIM)]


GROUP_SUM_PRECISION = lax.Precision.HIGH


def _layernorm_groups(vg, pavg):
    hi = GROUP_SUM_PRECISION
    mu = _dot(vg, pavg, hi)
    xc = vg - mu
    var = _dot(xc * xc, pavg, hi)
    rstd = lax.rsqrt(var + NORM_EPS)
    return xc * rstd, rstd


def _spatial_mix(w_bf, vn_chunk_bf, masks, bz):
    z = bz
    for g in range(A_GROUPS):
        z = z + jnp.where(masks[g], _dot(w_bf[g], vn_chunk_bf), 0.0)
    return z


def _full_spec(shape):
    return pl.BlockSpec(shape, lambda i: tuple(0 for _ in shape))


def _gate_fwd_rows(u, v, lg, lb, w_ref, bz, pavg, a_ref):
    masks = _group_masks()
    row = lax.broadcasted_iota(jnp.int32, (CHUNK, CHUNK), 0)
    col = lax.broadcasted_iota(jnp.int32, (CHUNK, CHUNK), 1)
    w_bf = [jnp.where(row >= col, w_ref[g], 0.0).astype(BF16) for g in range(A_GROUPS)]
    ug = _gelu(u)
    vhat, _ = _layernorm_groups(_gelu(v), pavg)
    vn = vhat * lg + lb
    for c in range(GATE_ROWS // CHUNK):
        sl = slice(c * CHUNK, (c + 1) * CHUNK)
        z = _spatial_mix(w_bf, vn[sl].astype(BF16), masks, bz)
        a_ref[sl, :] = (ug[sl] * z).astype(BF16)


def _gate_bwd(uv, dmix, ln_g, ln_b, w_s, w_st, bz, grads):
    m = uv.shape[0]
    pavg = _group_mean_matrix()
    nsteps = m // GATE_ROWS
    nx = len(grads)
    shapes = [g.shape for g in grads]

    def body(u_ref, v_ref, da_ref, lg_ref, lb_ref, w_ref, wt_ref, bz_ref, p_ref, *rest):
        grad_refs = rest[:nx]
        duv_ref, dlg_ref, dlb_ref, dw_ref, dbz_ref = rest[nx:nx + 5]
        recv_refs = rest[nx + 5:2 * nx + 5]
        exchange = _SiblingExchangePlan(shapes, grad_refs, recv_refs, *rest[2 * nx + 5:])
        i = pl.program_id(0)

        @pl.when(i == 0)
        def _():
            exchange.start()
            dlg_ref[...] = jnp.zeros_like(dlg_ref)
            dlb_ref[...] = jnp.zeros_like(dlb_ref)
            dw_ref[...] = jnp.zeros_like(dw_ref)
            dbz_ref[...] = jnp.zeros_like(dbz_ref)

        hi = GROUP_SUM_PRECISION
        masks = _group_masks()
        row = lax.broadcasted_iota(jnp.int32, (CHUNK, CHUNK), 0)
        col = lax.broadcasted_iota(jnp.int32, (CHUNK, CHUNK), 1)
        tril = row >= col
        w_bf = [jnp.where(tril, w_ref[g], 0.0).astype(BF16) for g in range(A_GROUPS)]
        wt_bf = [jnp.where(col >= row, wt_ref[g], 0.0).astype(BF16) for g in range(A_GROUPS)]
        pavg_v = p_ref[...]
        lg = lg_ref[...]
        ug, dug = _gelu_and_grad(u_ref[...])
        vg, dvg_dx = _gelu_and_grad(v_ref[...])
        vhat, rstd = _layernorm_groups(vg, pavg_v)
        vn = vhat * lg + lb_ref[...]
        da = da_ref[...]
        bz = bz_ref[...]
        for c in range(GATE_ROWS // CHUNK):
            sl = slice(c * CHUNK, (c + 1) * CHUNK)
            vn_bf = vn[sl].astype(BF16)
            z = _spatial_mix(w_bf, vn_bf, masks, bz)
            dz = da[sl] * ug[sl]
            duv_ref[sl, 0:A_WIDTH] = (da[sl] * z * dug[sl]).astype(BF16)
            dbz_ref[...] += dz
            dz_bf = dz.astype(BF16)
            dvn = jnp.zeros((CHUNK, A_WIDTH), F32)
            for g in range(A_GROUPS):
                dz_g = jnp.where(masks[g], dz, 0.0).astype(BF16)
                dw_ref[g] += jnp.where(tril, _dot_nt(dz_g, vn_bf), 0.0)
                dvn = dvn + jnp.where(masks[g], _dot(wt_bf[g], dz_bf), 0.0)
            vh = vhat[sl]
            dlb_ref[...] += jnp.sum(dvn, axis=0, keepdims=True)
            dlg_ref[...] += jnp.sum(dvn * vh, axis=0, keepdims=True)
            dvh = dvn * lg
            m1 = _dot(dvh, pavg_v, hi)
            m2 = _dot(dvh * vh, pavg_v, hi)
            duv_ref[sl, A_WIDTH:2 * A_WIDTH] = (rstd[sl] * (dvh - m1 - vh * m2) * dvg_dx[sl]).astype(BF16)

        @pl.when(i == nsteps - 1)
        def _():
            dbz_ref[...] = _dot(dbz_ref[...], pavg_v * float(HEAD_DIM), hi)
            exchange.finish()

    res = pl.pallas_call(
        body, grid=(nsteps,),
        in_specs=[pl.BlockSpec((GATE_ROWS, A_WIDTH), lambda i: (i, 0)),
                  pl.BlockSpec((GATE_ROWS, A_WIDTH), lambda i: (i, 1)),
                  pl.BlockSpec((GATE_ROWS, A_WIDTH), lambda i: (i, 0)),
                  _full_spec((1, A_WIDTH)), _full_spec((1, A_WIDTH)), _full_spec((A_GROUPS, CHUNK, CHUNK)),
                  _full_spec((A_GROUPS, CHUNK, CHUNK)), _full_spec((CHUNK, A_WIDTH)),
                  _full_spec((A_WIDTH, A_WIDTH))] + [ANY] * nx,
        out_specs=[pl.BlockSpec((GATE_ROWS, 2 * A_WIDTH), lambda i: (i, 0)),
                   _full_spec((1, A_WIDTH)), _full_spec((1, A_WIDTH)), _full_spec((A_GROUPS, CHUNK, CHUNK)),
                   _full_spec((CHUNK, A_WIDTH))] + [ANY] * nx,
        out_shape=[jax.ShapeDtypeStruct((m, IN_COLS), BF16),
                   jax.ShapeDtypeStruct((1, A_WIDTH), F32), jax.ShapeDtypeStruct((1, A_WIDTH), F32),
                   jax.ShapeDtypeStruct((A_GROUPS, CHUNK, CHUNK), F32),
                   jax.ShapeDtypeStruct((CHUNK, A_WIDTH), F32)]
        + [jax.ShapeDtypeStruct((N_SHARD, s[1] // 2, s[2]), g.dtype) for s, g in zip(shapes, grads)],
        scratch_shapes=_sem_pair(nx),
        compiler_params=_params(("arbitrary",)), name="gate_bwd",
    )(uv, uv, dmix, ln_g, ln_b, w_s, w_st, bz, pavg, *grads)
    return res[:5], list(res[5:])


Q_BLOCK = 128
PAIR = 2 * HEAD_DIM
N_PAIR = B_HEADS // 2
N_CFG = len(DILATED)
BLOCKS_PER_CFG = SEQ // Q_BLOCK
QKV_SLABS = 3 * N_PAIR
FWD_BLOCKS_PER_TRIP = 8
BWD_BLOCKS_PER_TRIP = 4


def _t5_bucket_np(dist, dtype):
    max_exact = NUM_BUCKETS // 2
    d = np.maximum(dist, 1).astype(dtype)
    large = max_exact + (np.log(d / dtype(max_exact)) / dtype(math.log(MAX_DISTANCE / max_exact))
                         * dtype(NUM_BUCKETS - max_exact))
    large = np.minimum(large.astype(np.int32), NUM_BUCKETS - 1)
    return np.where(dist < max_exact, dist, large)


def _bucket_tables():
    i = np.arange(Q_BLOCK)[:, None]
    j = np.arange(Q_BLOCK)[None, :]
    tables = []
    for _, dil in DILATED:
        rel_prev = Q_BLOCK + i - j
        rel_cur = i - j
        rel = np.concatenate([rel_prev, rel_cur], axis=1)
        valid = np.concatenate([rel_prev <= Q_BLOCK, rel_cur >= 0], axis=1)
        dist = np.maximum(rel, 0) * dil
        b32 = _t5_bucket_np(dist, np.float32)
        b64 = _t5_bucket_np(dist, np.float64)
        assert np.array_equal(b32, b64)
        tables.append(np.where(valid, b32, -1).astype(np.int32))
    return np.stack(tables)


def _present_buckets(buckets_np):
    return [sorted(set(int(v) for v in np.unique(buckets_np[c]) if v >= 0)) for c in range(N_CFG)]


def _bias_tables_body(buckets_np):
    present = _present_buckets(buckets_np)

    def tables(rb_ref, bk_ref, o_ref, ot_ref):
        for c in range(N_CFG):
            bk = bk_ref[c]
            for h in range(B_HEADS):
                acc = jnp.full((Q_BLOCK, 2 * Q_BLOCK), NEG_INF, F32)
                for b in present[c]:
                    acc = jnp.where(bk == b, rb_ref[h, b], acc)
                o_ref[c, h] = acc
                ot_ref[c, h] = acc.T

    return tables


def _proj_fwd(x, g1, w_in_t, ln_g, ln_b, w_s, bz):
    m = x.shape[0]
    tm = GATE_ROWS
    pavg = _group_mean_matrix()

    def body(x_ref, g_ref, w_ref, lg_ref, lb_ref, ws_ref, bz_ref, p_ref, h_ref, uv_ref, qkv_ref, a_ref):
        xv = x_ref[...]
        h = (xv * _rstd(xv) * g_ref[...]).astype(BF16)
        h_ref[...] = h
        acc = _dot_nt(h, w_ref[...])
        uv_ref[...] = acc[:, :2 * A_WIDTH]
        for s in range(QKV_SLABS):
            qkv_ref[s] = acc[:, 2 * A_WIDTH + s * PAIR:2 * A_WIDTH + (s + 1) * PAIR]
        _gate_fwd_rows(acc[:, :A_WIDTH], acc[:, A_WIDTH:2 * A_WIDTH], lg_ref[...], lb_ref[...], ws_ref,
                       bz_ref[...], p_ref[...], a_ref)

    return pl.pallas_call(
        body, grid=(m // tm,),
        in_specs=[pl.BlockSpec((tm, D_MODEL), lambda i: (i, 0)), _vec_spec(),
                  pl.BlockSpec((IN_COLS, D_MODEL), lambda i: (0, 0)),
                  _full_spec((1, A_WIDTH)), _full_spec((1, A_WIDTH)), _full_spec((A_GROUPS, CHUNK, CHUNK)),
                  _full_spec((CHUNK, A_WIDTH)), _full_spec((A_WIDTH, A_WIDTH))],
        out_specs=[pl.BlockSpec((tm, D_MODEL), lambda i: (i, 0)),
                   pl.BlockSpec((tm, 2 * A_WIDTH), lambda i: (i, 0)),
                   pl.BlockSpec((QKV_SLABS, tm, PAIR), lambda i: (0, i, 0)),
                   pl.BlockSpec((tm, A_WIDTH), lambda i: (i, 0))],
        out_shape=[jax.ShapeDtypeStruct((m, D_MODEL), BF16), jax.ShapeDtypeStruct((m, 2 * A_WIDTH), F32),
                   jax.ShapeDtypeStruct((QKV_SLABS, m, PAIR), F32), jax.ShapeDtypeStruct((m, A_WIDTH), BF16)],
        compiler_params=_params(("parallel",)), name="proj_fwd",
    )(x, g1, w_in_t, ln_g, ln_b, w_s, bz, pavg)


def _pair_masks():
    lane = lax.broadcasted_iota(jnp.int32, (1, PAIR), 1)
    return [lane < HEAD_DIM, lane >= HEAD_DIM]


def _block_rows(idx, dil):
    static = isinstance(idx, int)
    r, n = idx % dil, idx // dil

    def rows_of(block):
        start = r + (dil * Q_BLOCK) * block
        if dil == 1:
            return pl.ds(start if static else pl.multiple_of(start, Q_BLOCK), Q_BLOCK)
        return pl.ds(start, Q_BLOCK, stride=dil)

    prev = rows_of(n - 1) if not static or n > 0 else None
    return rows_of(n), prev


def _attn_fwd(qkv, bias, batch, owns):
    m = qkv.shape[1]
    comb_rows = 256
    nt = len(owns)
    shapes = [g.shape[1:] for g in owns]
    n_steps = batch * N_PAIR
    ts = list(range(nt))

    def body(q_ref, k_ref, v_ref, b_ref, *rest):
        o_ref, l_ref = rest[nt:nt + 2]
        gat_refs = rest[nt + 2:2 * nt + 2]
        scratch = rest[2 * nt + 2:]
        oc_refs, lc_refs = scratch[:N_CFG], scratch[N_CFG:2 * N_CFG]
        step = pl.program_id(0) * N_PAIR + pl.program_id(1)
        gather = _RelayGatherPlan(shapes, None, gat_refs, *scratch[2 * N_CFG:])

        @pl.when(step == 0)
        def _():
            gather.start(ts)

        @pl.when(step == n_steps // 2)
        def _():
            gather.relay(ts)

        @pl.when(step == n_steps - 2)
        def _():
            gather.forward(ts)

        masks = _pair_masks()
        for ci, (_, dil) in enumerate(DILATED):
            nb = SEQ // dil // Q_BLOCK

            def block(trip, ci=ci, dil=dil, nb=nb):
                work = []
                for u in range(FWD_BLOCKS_PER_TRIP):
                    rows, prow = _block_rows(trip * FWD_BLOCKS_PER_TRIP + u, dil)
                    has_prev = nb > 1 and prow is not None
                    q = q_ref[rows, :] * 0.125
                    kc = k_ref[rows, :].astype(BF16)
                    vc = v_ref[rows, :]
                    kp = k_ref[prow, :].astype(BF16) if has_prev else None
                    vp = v_ref[prow, :] if has_prev else None
                    tiles = []
                    for h in range(2):
                        qh = jnp.where(masks[h], q, 0.0).astype(BF16)
                        sc = _dot_nt(qh, kc) + b_ref[ci, h, :, Q_BLOCK:]
                        sp = _dot_nt(qh, kp) + b_ref[ci, h, :, :Q_BLOCK] if has_prev else None
                        tiles.append((sc, sp))
                    work.append((rows, vc, vp, tiles))
                probs = []
                for _, _, _, tiles in work:
                    ps = []
                    for sc, sp in tiles:
                        mx = jnp.max(sc if sp is None else jnp.maximum(sc, sp), axis=1, keepdims=True)
                        pc = jnp.exp(sc - mx).astype(BF16)
                        pp = None if sp is None else jnp.exp(sp - mx).astype(BF16)
                        ps.append((mx, pc, pp))
                    probs.append(ps)
                for (rows, vc, vp, _), ps in zip(work, probs):
                    res = []
                    for h, (_, pc, pp) in enumerate(ps):
                        r = _dot(pc, jnp.where(masks[h], vc, 1.0).astype(BF16))
                        if pp is not None:
                            r = r + _dot(pp, jnp.where(masks[h], vp, 1.0).astype(BF16))
                        res.append(r)
                    num = jnp.where(masks[0], res[0], res[1])
                    den = pltpu.roll(jnp.where(masks[0], res[1], res[0]), HEAD_DIM, 1)
                    oc_refs[ci][rows, :] = num / den
                    lc_refs[ci][rows, :] = jnp.where(masks[0], ps[0][0], ps[1][0]) + jnp.log(den)

            for trip in range(BLOCKS_PER_CFG // FWD_BLOCKS_PER_TRIP):
                block(trip)

        def combine(i, carry):
            rr = pl.ds(pl.multiple_of(i * comb_rows, comb_rows), comb_rows)
            ls = [lc_refs[c][rr, :] for c in range(N_CFG)]
            mx = functools.reduce(jnp.maximum, ls)
            ws = [jnp.exp(l - mx) for l in ls]
            tot = functools.reduce(lambda a, b: a + b, ws)
            o = functools.reduce(lambda a, b: a + b, [ws[c] * oc_refs[c][rr, :] for c in range(N_CFG)]) / tot
            o_ref[rr, :] = o.astype(BF16)
            l_ref[rr, :] = mx + jnp.log(tot)
            return carry

        lax.fori_loop(0, SEQ // comb_rows, combine, 0)

        @pl.when(step == n_steps - 1)
        def _():
            gather.finish(ts)

    def slab(first):
        return pl.BlockSpec((None, SEQ, PAIR), lambda b, p: (first + p, b, 0))

    nat = pl.BlockSpec((SEQ, PAIR), lambda b, p: (b, p))
    res = pl.pallas_call(
        body, grid=(batch, N_PAIR),
        in_specs=[slab(0), slab(N_PAIR), slab(2 * N_PAIR),
                  pl.BlockSpec((N_CFG, 2, Q_BLOCK, 2 * Q_BLOCK), lambda b, p: (0, p, 0, 0))] + [ANY] * nt,
        out_specs=[nat, nat] + [ANY] * nt,
        out_shape=[jax.ShapeDtypeStruct((m, B_WIDTH), BF16), jax.ShapeDtypeStruct((m, B_WIDTH), F32)]
        + [jax.ShapeDtypeStruct(g.shape, g.dtype) for g in owns],
        scratch_shapes=[pltpu.VMEM((SEQ, PAIR), F32)] * (2 * N_CFG) + _sem_pair(6 * nt),
        input_output_aliases={4 + t: 2 + t for t in range(nt)},
        compiler_params=_params(("arbitrary", "arbitrary")), name="attn_fwd",
    )(qkv, qkv, qkv, bias, *owns)
    return res[0], res[1], list(res[2:])


def _attn_bwd(qkv, dmix, o, lse, bias_t, dproj, batch, parts, smalls):
    m = qkv.shape[1]
    nt, ns = len(parts), len(smalls)
    n_steps = N_PAIR * batch

    def body(q_ref, k_ref, v_ref, do_ref, o_ref, l_ref, b_ref, *rest):
        part_refs = rest[1:nt + 1]
        small_refs = rest[nt + 1:nt + 1 + ns]
        pos = nt + 1 + ns
        dproj_ref, ds_ref = rest[pos:pos + 2]
        recv_refs = rest[pos + 2:pos + 2 + nt]
        sum_refs = rest[pos + 2 + nt:pos + 2 + nt + ns]
        pos += 2 + nt + ns
        dq_acc, dk_acc, dv_acc, d_scr, stage, stage_sems, send_sems, recv_sems = rest[pos:pos + 8]
        allreduce = _SmallAllReducePlan(small_refs, sum_refs, rest[pos + 8:pos + 8 + ns],
                                        rest[pos + 8 + ns:pos + 8 + 2 * ns], *rest[pos + 8 + 2 * ns:])
        pair, seq = pl.program_id(0), pl.program_id(1)
        step = pair * batch + seq
        exchange = _ChipExchangePlan(part_refs, recv_refs, send_sems, recv_sems)

        @pl.when(step == 0)
        def _():
            allreduce.start_sibling()

        @pl.when(step == n_steps // 2)
        def _():
            allreduce.sum_sibling_and_start_chips()

        def stage_copies():
            rows = pl.ds(pl.multiple_of(seq * SEQ, SEQ), SEQ)
            return [pltpu.make_async_copy(
                stage.at[k],
                dproj_ref.at[rows, pl.ds(pl.multiple_of(2 * A_WIDTH + k * B_WIDTH + pair * PAIR, PAIR), PAIR)],
                stage_sems.at[k]) for k in range(3)]

        @pl.when(step == 0)
        def _():
            exchange.start()

        @pl.when(pl.program_id(1) == 0)
        def _():
            ds_ref[...] = jnp.zeros_like(ds_ref)

        dq_acc[...] = jnp.zeros_like(dq_acc)
        dk_acc[...] = jnp.zeros_like(dk_acc)
        dv_acc[...] = jnp.zeros_like(dv_acc)
        d_scr[...] = do_ref[...] * o_ref[...].astype(F32)
        masks = _pair_masks()

        def stack_heads(t):
            return jnp.concatenate([jnp.where(masks[0], t, 0.0), jnp.where(masks[1], t, 0.0)], axis=0).astype(BF16)

        for ci, (_, dil) in enumerate(DILATED):
            nb = SEQ // dil // Q_BLOCK

            def block(trip, carry, ci=ci, dil=dil, nb=nb):
                first = []
                for u in range(BWD_BLOCKS_PER_TRIP):
                    rows, prow = _block_rows(trip * BWD_BLOCKS_PER_TRIP + u, dil)
                    has_prev = nb > 1 and prow is not None
                    if has_prev:
                        kcat = jnp.concatenate([k_ref[prow, :], k_ref[rows, :]], axis=0).astype(BF16)
                        vcat = jnp.concatenate([v_ref[prow, :], v_ref[rows, :]], axis=0).astype(BF16)
                    else:
                        kcat = k_ref[rows, :].astype(BF16)
                        vcat = v_ref[rows, :].astype(BF16)
                    qst = stack_heads(q_ref[rows, :] * 0.125)
                    dost = stack_heads(do_ref[rows, :])
                    lt = l_ref[rows, :].T
                    dt = d_scr[rows, :].T
                    lrow = jnp.concatenate([lt[0:1], lt[HEAD_DIM:HEAD_DIM + 1]], axis=1)
                    drow = jnp.concatenate([jnp.sum(dt[:HEAD_DIM], axis=0, keepdims=True),
                                            jnp.sum(dt[HEAD_DIM:], axis=0, keepdims=True)], axis=1)
                    first.append((has_prev, rows, prow, kcat, qst, dost, lrow, drow,
                                  _dot_nt(kcat, qst), _dot_nt(vcat, dost)))
                second = []
                for has_prev, rows, prow, kcat, qst, dost, lrow, drow, st, dpt in first:
                    keys = slice(0, 2 * Q_BLOCK) if has_prev else slice(Q_BLOCK, 2 * Q_BLOCK)
                    bt = jnp.concatenate([b_ref[ci, 0, keys, :], b_ref[ci, 1, keys, :]], axis=1)
                    pt = jnp.exp(st + bt - lrow)
                    dst = pt * (dpt - drow)
                    ds_ref[ci, 0, keys, :] += dst[:, :Q_BLOCK]
                    ds_ref[ci, 1, keys, :] += dst[:, Q_BLOCK:]
                    second.append((has_prev, rows, prow, kcat, qst, dost, pt.astype(BF16), dst.astype(BF16)))
                for has_prev, rows, prow, kcat, qst, dost, pt_bf, dst_bf in second:
                    dk = _dot(dst_bf, qst)
                    dv = _dot(pt_bf, dost)
                    dq2 = _dot_tn(dst_bf, kcat)
                    dq_acc[rows, :] += jnp.where(masks[0], dq2[:Q_BLOCK], dq2[Q_BLOCK:]) * 0.125
                    if has_prev:
                        dk_acc[prow, :] += dk[:Q_BLOCK]
                        dv_acc[prow, :] += dv[:Q_BLOCK]
                        dk_acc[rows, :] += dk[Q_BLOCK:]
                        dv_acc[rows, :] += dv[Q_BLOCK:]
                    else:
                        dk_acc[rows, :] += dk
                        dv_acc[rows, :] += dv
                return carry

            for trip in range(BLOCKS_PER_CFG // BWD_BLOCKS_PER_TRIP):
                block(trip, 0)

        @pl.when(step > 0)
        def _():
            for cp in stage_copies():
                cp.wait()

        stage[0] = dq_acc[...].astype(BF16)
        stage[1] = dk_acc[...].astype(BF16)
        stage[2] = dv_acc[...].astype(BF16)
        for cp in stage_copies():
            cp.start()

        @pl.when(step == n_steps - 1)
        def _():
            for cp in stage_copies():
                cp.wait()
            exchange.finish()
            allreduce.finish()

    def slab(first):
        return pl.BlockSpec((None, SEQ, PAIR), lambda p, b: (first + p, b, 0))

    nat = pl.BlockSpec((SEQ, PAIR), lambda p, b: (b, p))
    tbl = pl.BlockSpec((N_CFG, 2, 2 * Q_BLOCK, Q_BLOCK), lambda p, b: (0, p, 0, 0))
    acc = pltpu.VMEM((SEQ, PAIR), F32)
    vm = pl.BlockSpec(memory_space=pltpu.VMEM)
    res = pl.pallas_call(
        body, grid=(N_PAIR, batch),
        in_specs=[slab(0), slab(N_PAIR), slab(2 * N_PAIR),
                  pl.BlockSpec((SEQ, PAIR), lambda p, b: (b, A_WIDTH // PAIR + p)), nat, nat, tbl]
        + [ANY] * (nt + 1) + [vm] * ns,
        out_specs=[ANY, tbl] + [ANY] * nt + [vm] * ns,
        out_shape=[jax.ShapeDtypeStruct(dproj.shape, dproj.dtype),
                   jax.ShapeDtypeStruct((N_CFG, B_HEADS, 2 * Q_BLOCK, Q_BLOCK), F32)]
        + [jax.ShapeDtypeStruct((3,) + p.shape[1:], p.dtype) for p in parts]
        + [jax.ShapeDtypeStruct(a.shape, F32) for a in smalls],
        input_output_aliases={7: 0},
        scratch_shapes=[acc, acc, acc, acc, pltpu.VMEM((3, SEQ, PAIR), BF16), pltpu.SemaphoreType.DMA((3,))]
        + _sem_pair(3 * nt) + _SmallAllReducePlan.scratch(smalls),
        compiler_params=_params(("arbitrary", "arbitrary")), name="attn_bwd",
    )(qkv, qkv, qkv, dmix, o, lse, bias_t, dproj, *parts, *smalls)
    return res[0], res[1], list(res[2:2 + nt]), list(res[2 + nt:])


def _rel_bias_grad(ds, buckets_np, grads):
    present = _present_buckets(buckets_np)
    nx = len(grads)
    shapes = [g.shape for g in grads]

    def body(bk_ref, ds_ref, *rest):
        o_ref = rest[nx]
        acc_ref = rest[2 * nx + 1]
        exchange = _SiblingExchangePlan(shapes, rest[:nx], rest[nx + 1:2 * nx + 1], *rest[2 * nx + 2:])
        exchange.start()
        acc_ref[...] = jnp.zeros_like(acc_ref)
        for c in range(N_CFG):
            bk = bk_ref[c]
            for h in range(B_HEADS):
                dsv = ds_ref[c, h]
                for b in present[c]:
                    part = jnp.sum(jnp.where(bk == b, dsv, 0.0), axis=0, keepdims=True)
                    acc_ref[pl.ds(h * NUM_BUCKETS + b, 1), :] += part
        o_ref[...] = jnp.sum(acc_ref[...], axis=1, keepdims=True)
        exchange.finish()

    vm = pl.BlockSpec(memory_space=pltpu.VMEM)
    res = pl.pallas_call(
        body, in_specs=[vm, vm] + [ANY] * nx, out_specs=[vm] + [ANY] * nx,
        out_shape=[jax.ShapeDtypeStruct((B_HEADS * NUM_BUCKETS, 1), F32)]
        + [jax.ShapeDtypeStruct((N_SHARD, s[1] // 2, s[2]), g.dtype) for s, g in zip(shapes, grads)],
        scratch_shapes=[pltpu.VMEM((B_HEADS * NUM_BUCKETS, buckets_np.shape[-1]), F32)] + _sem_pair(nx),
        compiler_params=_params(), name="rel_bias_grad",
    )(jnp.asarray(buckets_np), ds, *grads)
    return res[0], list(res[1:])


def _row_index():
    return lax.broadcasted_iota(jnp.int32, (SEQ, LANE_BLOCK), 0)


def _shift_down(x, k, row):
    return jnp.where(row >= k, pltpu.roll(x, k, 0), 0.0)


def _shift_up(x, k, row):
    return jnp.where(row < SEQ - k, pltpu.roll(x, SEQ - k, 0), 0.0)


def _convgate_fwd(gate, up, conv_w, conv_b, batch):
    m = gate.shape[0]

    def body(g_ref, u_ref, w_ref, b_ref, a_ref):
        g = g_ref[...].astype(F32)
        w = w_ref[...]
        row = _row_index()
        c = b_ref[...] + w[0:1] * _shift_down(g, 2, row) + w[1:2] * _shift_down(g, 1, row) + w[2:3] * g
        a_ref[...] = (_gelu(c) * u_ref[...].astype(F32)).astype(BF16)

    blk = pl.BlockSpec((SEQ, LANE_BLOCK), lambda b, j: (b, j))
    return pl.pallas_call(
        body, grid=(batch, D_FF // LANE_BLOCK),
        in_specs=[blk, blk, pl.BlockSpec((3, LANE_BLOCK), lambda b, j: (0, j)),
                  pl.BlockSpec((1, LANE_BLOCK), lambda b, j: (0, j))],
        out_specs=blk,
        out_shape=jax.ShapeDtypeStruct((m, D_FF), BF16),
        compiler_params=_params(("parallel", "parallel")), name="convgate_fwd",
    )(gate, up, conv_w, conv_b)


def _convgate_bwd(gate, up, dact, conv_w, conv_b, batch):
    m = gate.shape[0]

    def body(g_ref, u_ref, da_ref, w_ref, b_ref, dg_ref, du_ref, dw_ref, db_ref):
        @pl.when(pl.program_id(1) == 0)
        def _():
            dw_ref[...] = jnp.zeros_like(dw_ref)
            db_ref[...] = jnp.zeros_like(db_ref)

        g = g_ref[...].astype(F32)
        w = w_ref[...]
        row = _row_index()
        g1 = _shift_down(g, 1, row)
        g2 = _shift_down(g, 2, row)
        c = b_ref[...] + w[0:1] * g2 + w[1:2] * g1 + w[2:3] * g
        gg, dgg = _gelu_and_grad(c)
        da = da_ref[...].astype(F32)
        du_ref[...] = (da * gg).astype(BF16)
        dc = da * u_ref[...].astype(F32) * dgg
        db_ref[...] += jnp.sum(dc, axis=0, keepdims=True)
        dw_ref[0:1, :] += jnp.sum(dc * g2, axis=0, keepdims=True)
        dw_ref[1:2, :] += jnp.sum(dc * g1, axis=0, keepdims=True)
        dw_ref[2:3, :] += jnp.sum(dc * g, axis=0, keepdims=True)
        dg_ref[...] = (w[2:3] * dc + w[1:2] * _shift_up(dc, 1, row) + w[0:1] * _shift_up(dc, 2, row)).astype(BF16)

    blk = pl.BlockSpec((SEQ, LANE_BLOCK), lambda j, b: (b, j))
    wspec = pl.BlockSpec((3, LANE_BLOCK), lambda j, b: (0, j))
    bspec = pl.BlockSpec((1, LANE_BLOCK), lambda j, b: (0, j))
    return pl.pallas_call(
        body, grid=(D_FF // LANE_BLOCK, batch),
        in_specs=[blk, blk, blk, wspec, bspec],
        out_specs=[blk, blk, wspec, bspec],
        out_shape=[jax.ShapeDtypeStruct((m, D_FF), BF16), jax.ShapeDtypeStruct((m, D_FF), BF16),
                   jax.ShapeDtypeStruct((3, D_FF), F32), jax.ShapeDtypeStruct((1, D_FF), F32)],
        compiler_params=_params(("parallel", "arbitrary")), name="convgate_bwd",
    )(gate, up, dact, conv_w, conv_b)


def _gather_weights(shards, conv_w_shard, rel_bias, buckets_np):
    nt = len(shards)
    shapes = [sh.shape for sh in shards]
    ts = list(range(nt))
    tables = _bias_tables_body(buckets_np)

    def body(*refs):
        shard_refs = refs[:nt]
        cw_ref, rb_ref, bk_ref = refs[nt:nt + 3]
        out_refs = refs[nt + 3:2 * nt + 3]
        cw_out, bias_ref, bias_t_ref = refs[2 * nt + 3:2 * nt + 6]
        scratch = refs[2 * nt + 6:]
        f32_refs, bf16_refs = scratch[:nt], scratch[nt:2 * nt]
        load_sems, store_sems, send_sems, recv_sems, cw_send, cw_recv = scratch[2 * nt:]
        plan = _RelayGatherPlan(shapes[:1], bf16_refs[:1], out_refs[:1], send_sems, recv_sems)
        x, y, c, chips = _mesh_pos()
        loads = [pltpu.make_async_copy(shard_refs[t], f32_refs[t], load_sems.at[t]) for t in ts]
        stores = [pltpu.make_async_copy(bf16_refs[t], out_refs[t].at[2 * x + y], store_sems.at[t]) for t in ts]
        stores.append(pltpu.make_async_copy(cw_ref, cw_out.at[2 * x + y], store_sems.at[nt]))

        def cw_copy(j, src, dst, chip):
            return pltpu.make_async_remote_copy(src_ref=src, dst_ref=dst, send_sem=cw_send.at[j],
                                                recv_sem=cw_recv.at[j], device_id=(*chip, c), device_id_type=MESH)

        def to_bf16(t):
            loads[t].wait()
            bf16_refs[t][...] = f32_refs[t][...].astype(BF16)
            stores[t].start()

        for cp in loads:
            cp.start()
        to_bf16(0)
        plan.start([0])
        cw_sends = [cw_copy(j, cw_ref, cw_out.at[2 * x + y], chip) for j, chip in enumerate(chips)]
        for cp in cw_sends + stores[nt:]:
            cp.start()
        for t in ts[1:]:
            to_bf16(t)
        plan.relay([0])
        tables(rb_ref, bk_ref, bias_ref, bias_t_ref)
        plan.forward([0])
        for j, chip in enumerate(chips):
            dst = cw_out.at[2 * chip[0] + chip[1]]
            cw_copy(j, dst, dst, chip).wait_recv()
        plan.finish([0])
        for cp in cw_sends:
            cp.wait_send()
        for cp in stores:
            cp.wait()

    out_shape = [jax.ShapeDtypeStruct((N_SHARD,) + sh.shape, BF16) for sh in shards]
    out_shape.append(jax.ShapeDtypeStruct((N_SHARD,) + conv_w_shard.shape, conv_w_shard.dtype))
    out_shape += [jax.ShapeDtypeStruct((N_CFG, B_HEADS, Q_BLOCK, 2 * Q_BLOCK), F32),
                  jax.ShapeDtypeStruct((N_CFG, B_HEADS, 2 * Q_BLOCK, Q_BLOCK), F32)]
    vm = pl.BlockSpec(memory_space=pltpu.VMEM)
    res = pl.pallas_call(
        body, in_specs=[ANY] * (nt + 1) + [pl.BlockSpec(memory_space=pltpu.SMEM), vm],
        out_specs=[ANY] * (nt + 1) + [vm, vm], out_shape=out_shape,
        scratch_shapes=[pltpu.VMEM(sh.shape, F32) for sh in shards] + [pltpu.VMEM(sh.shape, BF16) for sh in shards]
        + [pltpu.SemaphoreType.DMA((nt,)), pltpu.SemaphoreType.DMA((nt + 1,))] + _sem_pair(6) + _sem_pair(3),
        compiler_params=pltpu.CompilerParams(has_side_effects=True, vmem_limit_bytes=VMEM_LIMIT),
        name="gather_weights",
    )(*shards, conv_w_shard, rel_bias.T, jnp.asarray(buckets_np))
    return list(res[:nt + 1]), res[nt + 1], res[nt + 2]


def _turn(t, u, s, last):
    return jnp.where(t == u, s, jnp.where(t > u, last, 0))


def _add_halves(gs, recvs, c_idx):
    n = len(gs)
    dims = [(g.shape[1] // 2, g.shape[2]) for g in gs]

    def body(c_ref, *refs):
        t = pl.program_id(0)
        for u in range(n):
            @pl.when(t == u)
            def _(u=u):
                refs[2 * n + u][...] = (refs[u][...].astype(F32) + refs[n + u][...].astype(F32)).astype(BF16)

    def own(u):
        return pl.BlockSpec((None, None) + dims[u], lambda t, s, c: (_turn(t, u, s, N_SHARD - 1), c[0], 0, 0))

    def plain(u):
        return pl.BlockSpec((None,) + dims[u], lambda t, s, c: (_turn(t, u, s, N_SHARD - 1), 0, 0))

    return pl.pallas_call(
        body,
        grid_spec=pltpu.PrefetchScalarGridSpec(
            num_scalar_prefetch=1, grid=(n, N_SHARD),
            in_specs=[own(u) for u in range(n)] + [plain(u) for u in range(n)],
            out_specs=[plain(u) for u in range(n)]),
        out_shape=[jax.ShapeDtypeStruct((N_SHARD,) + d, BF16) for d in dims],
        compiler_params=_params(("arbitrary", "arbitrary")), name="rs_add_halves",
    )(c_idx, *[g.reshape((N_SHARD, 2) + d) for g, d in zip(gs, dims)], *recvs)


def _add_chips(parts, recvs, s_idx, c_idx):
    n = len(parts)
    dims = [p.shape[1:] for p in parts]

    def body(idx_ref, *refs):
        t = pl.program_id(0)
        for u in range(n):
            @pl.when(t == u)
            def _(u=u):
                acc = refs[u][...].astype(F32)
                for j in range(3):
                    acc = acc + refs[n + u][j].astype(F32)
                refs[2 * n + u][...] = acc

    res = pl.pallas_call(
        body,
        grid_spec=pltpu.PrefetchScalarGridSpec(
            num_scalar_prefetch=1, grid=(n,),
            in_specs=[pl.BlockSpec((None,) + d, lambda t, idx: (idx[0], 0, 0)) for d in dims]
            + [pl.BlockSpec((3,) + d, lambda t, idx: (0, 0, 0)) for d in dims],
            out_specs=[pl.BlockSpec((None,) + d, lambda t, idx: (idx[1], 0, 0)) for d in dims]),
        out_shape=[jax.ShapeDtypeStruct((2,) + d, F32) for d in dims],
        compiler_params=_params(("arbitrary",)), name="rs_add_chips",
    )(jnp.concatenate([s_idx, c_idx]), *parts, *recvs)
    return [r.reshape(2 * d[0], d[1]) for r, d in zip(res, dims)]


def _finish_reductions(fulls, arrays):
    nt, n = len(fulls), len(arrays)

    def body(*refs):
        in_refs = refs[nt:nt + n]
        full_refs, out_refs = refs[nt + n:2 * nt + n], refs[2 * nt + n:2 * nt + 2 * n]
        pos = 2 * nt + 2 * n
        share_send, share_recv = refs[pos + 2 * n:pos + 2 * n + 2]
        allreduce = _SmallAllReducePlan(in_refs, out_refs, refs[pos:pos + n], refs[pos + n:pos + 2 * n],
                                        *refs[pos + 2 * n + 2:])
        x, y, c, _ = _mesh_pos()

        def half(t, which):
            rows = fulls[t].shape[0] // 2
            return full_refs[t].at[pl.ds(which * rows, rows), :]

        def share(t, which):
            return pltpu.make_async_remote_copy(
                src_ref=half(t, which), dst_ref=half(t, which), send_sem=share_send.at[t],
                recv_sem=share_recv.at[t], device_id=(x, y, 1 - c), device_id_type=MESH)

        for t in range(nt):
            share(t, c).start()
        allreduce.start_sibling()
        allreduce.sum_sibling_and_start_chips()
        allreduce.finish()
        for t in range(nt):
            share(t, 1 - c).wait_recv()
        for t in range(nt):
            share(t, c).wait_send()

    vm = pl.BlockSpec(memory_space=pltpu.VMEM)
    res = pl.pallas_call(
        body, in_specs=[ANY] * nt + [vm] * n, out_specs=[ANY] * nt + [vm] * n,
        out_shape=[jax.ShapeDtypeStruct(f.shape, f.dtype) for f in fulls]
        + [jax.ShapeDtypeStruct(a.shape, F32) for a in arrays],
        input_output_aliases={t: t for t in range(nt)},
        scratch_shapes=[pltpu.VMEM(a.shape, F32) for a in arrays] + [pltpu.VMEM((3,) + a.shape, F32) for a in arrays]
        + _sem_pair(nt) + _sem_pair(4 * n),
        compiler_params=pltpu.CompilerParams(has_side_effects=True),
        name="finish_reductions",
    )(*fulls, *arrays)
    return list(res[:nt]), list(res[nt:])


def _from_col_shards(g):
    n, rows, cols = g.shape
    return g.transpose(1, 0, 2).reshape(rows, n * cols)


def _train_step(x, tgt, g1, g2, g3, g4, shards, ln_g, ln_b, w_s, b_s, rel_bias, conv_w_shard, conv_b, batch,
                s_idx, c_idx):
    buckets = _bucket_tables()
    bz = jnp.repeat(b_s.T, HEAD_DIM, axis=1)
    w_st = jnp.swapaxes(w_s, 1, 2)

    def shard_major(g):
        return g.reshape(N_SHARD, g.shape[0] // N_SHARD, D_MODEL)

    names = ["w_in", "w_out", "w_gate", "w_up", "w_down"]
    (g_in, g_out, g_gate, g_up, g_down, g_convw), bias, bias_t = _gather_weights(
        [shards[n] for n in names], conv_w_shard, rel_bias, buckets)
    w_in_t = g_in.reshape(IN_COLS, D_MODEL)
    conv_w = _from_col_shards(g_convw.reshape(N_SHARD, 3, SHARD_FF))

    h1, uv, qkv, a = _proj_fwd(x, g1, w_in_t, ln_g, ln_b, w_s, bz)
    o_bf, lse, (g_out, g_gate, g_up) = _attn_fwd(qkv, bias, batch, [g_out, g_gate, g_up])
    w_out = g_out.reshape(D_MODEL, D_MODEL)
    w_gate_t = g_gate.reshape(D_FF, D_MODEL)
    w_up_t = g_up.reshape(D_FF, D_MODEL)
    (y1, x1, h2), _ = _fused_rows(
        "out_proj_mid_fwd", 512,
        [(a, w_out, "nn", slice(0, A_WIDTH)), (o_bf, w_out, "nn", slice(A_WIDTH, D_MODEL))],
        [x], [g2, g3], _mid_fwd_rows, [F32, F32, BF16], [])
    gate, up, g_down = _mm_pair_nt(h2, w_gate_t, w_up_t, g_down, tm=1024, tn=1408, out_dtype=BF16,
                                   name="mm_gate_up")
    w_down = g_down.reshape(D_FF, D_MODEL)
    act = _convgate_fwd(gate, up, conv_w, conv_b, batch)
    (dx2, dy2, dg4, loss), _ = _fused_rows(
        "down_proj_loss_head", 512, [(act, w_down, "nn", None)], [x1, tgt], [g4], _loss_head_rows,
        [F32, BF16], [(1, D_MODEL), (1, 128)])

    dact = _mm(dy2, w_down, dims="nt", tm=1024, tn=1408, tk=1024, out_dtype=BF16, name="mm_dact")
    dw_down = _mm(act, dy2, dims="tn", tm=1408, tn=1024, tk=1024, out_dtype=BF16, name="mm_dw_down")
    dgate, dup, dconv_w, dconv_b = _convgate_bwd(gate, up, dact, conv_w, conv_b, batch)
    (dx1, dy1, dg2, dg3), _ = _fused_rows(
        "dh2_mid_bwd", 256, [(dgate, w_gate_t, "nn", None), (dup, w_up_t, "nn", None)],
        [x1, y1, dx2], [g2, g3], _mid_bwd_rows, [F32, BF16], [(1, D_MODEL), (1, D_MODEL)])
    dw_gate_t, dw_up_t = _mm_pair_tn(dgate, dup, h2, tm=1408, tk=1024, name="mm_dw_gate_up")
    done = [shard_major(g) for g in (dw_down, dw_gate_t, dw_up_t)]
    (dmix, dw_out), recv_a = _out_proj_bwd(a, o_bf, dy1, w_out, done[:2])
    done.append(shard_major(dw_out))
    (dproj, dln_g, dln_b, dw_s, dbz), recv_b = _gate_bwd(uv, dmix, ln_g, ln_b, w_s, w_st, bz, done[2:])
    recv_a += recv_b
    parts = _add_halves(done, recv_a, c_idx)
    early = dict(loss=loss, norm_mix_post=dg2, norm_ffn_pre=dg3, norm_ffn_post=dg4, ln_v_gain=dln_g,
                 ln_v_bias=dln_b, spatial_w=dw_s, spatial_b=dbz, conv_w=dconv_w, conv_b=dconv_b)
    dproj, ds, recv, early_sums = _attn_bwd(qkv, dmix, o_bf, lse, bias_t, dproj, batch, parts, list(early.values()))
    fulls = _add_chips(parts, recv, s_idx, c_idx)
    dw_in_t = _mm(dproj, h1, dims="tn", tm=1408, tn=1024, tk=1024, out_dtype=BF16, name="mm_dw_in")
    last = [shard_major(dw_in_t)]
    drel, recv_in_a = _rel_bias_grad(ds, np.ascontiguousarray(np.swapaxes(buckets, 1, 2)), last)
    part_in = _add_halves(last, recv_in_a, c_idx)
    (dx0, dg1), recv_in = _fused_rows(
        "dh1_in_bwd", 512, [(dproj, w_in_t, "nn", None)], [x, dx1], [g1], _in_bwd_rows,
        [F32], [(1, D_MODEL)], exchange=part_in)
    fulls += _add_chips(part_in, recv_in, s_idx, c_idx)
    half_reduced = dict(zip(["w_down", "w_gate", "w_up", "w_out", "w_in"], fulls))

    return dx0, dict(zip(early, early_sums)), dict(norm_mix_pre=dg1, rel_bias=drel), half_reduced


def _adamw_update(w, g, m, v):
    nm = ADAM_B1 * m + (1.0 - ADAM_B1) * g
    nv = ADAM_B2 * v + (1.0 - ADAM_B2) * (g * g)
    m_hat = nm / (1.0 - ADAM_B1 ** ADAM_STEP)
    v_hat = nv / (1.0 - ADAM_B2 ** ADAM_STEP)
    return -ADAM_LR * (m_hat / (jnp.sqrt(v_hat) + ADAM_EPS) + ADAM_WD * w), nm, nv


def _adamw(w, g, m, v, name):
    rows, cols = w.shape
    tr = next(cand for cand in (352, 256, 128) if rows % cand == 0)

    def body(w_ref, g_ref, m_ref, v_ref, go_ref, d_ref, nm_ref, nv_ref):
        gv = g_ref[...]
        go_ref[...] = gv
        d_ref[...], nm_ref[...], nv_ref[...] = _adamw_update(w_ref[...], gv, m_ref[...], v_ref[...])

    spec = pl.BlockSpec((tr, cols), lambda i: (i, 0))
    sds = jax.ShapeDtypeStruct((rows, cols), F32)
    return pl.pallas_call(
        body, grid=(rows // tr,), in_specs=[spec] * 4, out_specs=[spec] * 4, out_shape=[sds] * 4,
        compiler_params=_params(("parallel",)), name=name,
    )(w, g, m, v)


def _adamw_small(ws, gs, ms, vs):
    n = len(ws)

    def body(*refs):
        w_refs, g_refs, m_refs, v_refs = refs[:n], refs[n:2 * n], refs[2 * n:3 * n], refs[3 * n:4 * n]
        d_refs, nm_refs, nv_refs = refs[4 * n:5 * n], refs[5 * n:6 * n], refs[6 * n:7 * n]
        for t in range(n):
            d_refs[t][...], nm_refs[t][...], nv_refs[t][...] = _adamw_update(
                w_refs[t][...], g_refs[t][...], m_refs[t][...], v_refs[t][...])

    vm = pl.BlockSpec(memory_space=pltpu.VMEM)
    sds = [jax.ShapeDtypeStruct(w.shape, F32) for w in ws]
    res = pl.pallas_call(
        body, in_specs=[vm] * (4 * n), out_specs=[vm] * (3 * n), out_shape=sds * 3,
        compiler_params=_params(), name="adamw_small",
    )(*ws, *gs, *ms, *vs)
    return res[:n], res[n:2 * n], res[2 * n:]


SMALL = ["norm_mix_pre", "norm_mix_post", "norm_ffn_pre", "norm_ffn_post", "ln_v_gain", "ln_v_bias",
         "spatial_w", "spatial_b", "rel_bias", "conv_b"]
LARGE = ["w_in", "w_gate", "w_up", "w_down", "w_out"]
TRANSPOSED = ("w_in", "w_gate", "w_up")
ORDER = ["norm_mix_pre", "norm_mix_post", "norm_ffn_pre", "norm_ffn_post", "w_in", "ln_v_gain", "ln_v_bias",
         "spatial_w", "spatial_b", "rel_bias", "w_out", "w_gate", "w_up", "conv_w", "conv_b", "w_down"]


def kernel(x, norm_mix_pre, norm_mix_post, norm_ffn_pre, norm_ffn_post, w_in, ln_v_gain, ln_v_bias, spatial_w, spatial_b, rel_bias, w_out, w_gate, w_up, conv_w, conv_b, w_down, loss_target, m_norm_mix_pre, m_norm_mix_post, m_norm_ffn_pre, m_norm_ffn_post, m_w_in, m_ln_v_gain, m_ln_v_bias, m_spatial_w, m_spatial_b, m_rel_bias, m_w_out, m_w_gate, m_w_up, m_conv_w, m_conv_b, m_w_down, v_norm_mix_pre, v_norm_mix_post, v_norm_ffn_pre, v_norm_ffn_post, v_w_in, v_ln_v_gain, v_ln_v_bias, v_spatial_w, v_spatial_b, v_rel_bias, v_w_out, v_w_gate, v_w_up, v_conv_w, v_conv_b, v_w_down):
    params = dict(norm_mix_pre=norm_mix_pre, norm_mix_post=norm_mix_post, norm_ffn_pre=norm_ffn_pre,
                  norm_ffn_post=norm_ffn_post, w_in=w_in, ln_v_gain=ln_v_gain, ln_v_bias=ln_v_bias,
                  spatial_w=spatial_w, spatial_b=spatial_b, rel_bias=rel_bias, w_out=w_out, w_gate=w_gate,
                  w_up=w_up, conv_w=conv_w, conv_b=conv_b, w_down=w_down)
    mom = dict(norm_mix_pre=m_norm_mix_pre, norm_mix_post=m_norm_mix_post, norm_ffn_pre=m_norm_ffn_pre,
               norm_ffn_post=m_norm_ffn_post, w_in=m_w_in, ln_v_gain=m_ln_v_gain, ln_v_bias=m_ln_v_bias,
               spatial_w=m_spatial_w, spatial_b=m_spatial_b, rel_bias=m_rel_bias, w_out=m_w_out, w_gate=m_w_gate,
               w_up=m_w_up, conv_w=m_conv_w, conv_b=m_conv_b, w_down=m_w_down)
    var = dict(norm_mix_pre=v_norm_mix_pre, norm_mix_post=v_norm_mix_post, norm_ffn_pre=v_norm_ffn_pre,
               norm_ffn_post=v_norm_ffn_post, w_in=v_w_in, ln_v_gain=v_ln_v_gain, ln_v_bias=v_ln_v_bias,
               spatial_w=v_spatial_w, spatial_b=v_spatial_b, rel_bias=v_rel_bias, w_out=v_w_out, w_gate=v_w_gate,
               w_up=v_w_up, conv_w=v_conv_w, conv_b=v_conv_b, w_down=v_w_down)

    batch = x.shape[0]
    xi, yi, ci = lax.axis_index("x"), lax.axis_index("y"), lax.axis_index("c")
    s_idx = (2 * xi + yi).astype(jnp.int32).reshape(1)
    c_idx = ci.astype(jnp.int32).reshape(1)

    def local(a, n):
        return jnp.swapaxes(a[0], 0, 1) if n in TRANSPOSED else a[0]

    shards = {n: local(params[n], n) for n in LARGE}
    dx0, total, partial, half_reduced = _train_step(
        x.reshape(batch * SEQ, D_MODEL), loss_target.reshape(batch * SEQ, D_MODEL),
        norm_mix_pre, norm_mix_post, norm_ffn_pre, norm_ffn_post, shards,
        ln_v_gain.reshape(1, A_WIDTH), ln_v_bias.reshape(1, A_WIDTH), spatial_w[0], spatial_b[0], rel_bias,
        jnp.swapaxes(conv_w, 0, 1), conv_b, batch, s_idx, c_idx)
    grad_x = dx0.reshape(batch, SEQ, D_MODEL)

    names = list(partial)
    fulls, sums = _finish_reductions([half_reduced[n] for n in LARGE], [partial[n] for n in names])
    reduced = dict(zip(LARGE, fulls))
    total.update(zip(names, sums))
    loss = total["loss"][0, 0]
    total["spatial_b"] = total["spatial_b"][:, ::HEAD_DIM].T
    total["rel_bias"] = total["rel_bias"].reshape(B_HEADS, NUM_BUCKETS)
    total["conv_w"] = lax.dynamic_slice_in_dim(total["conv_w"], s_idx[0] * SHARD_FF, SHARD_FF, axis=1)
    small_names = SMALL + ["conv_w"]

    def small(a, n):
        return jnp.swapaxes(a, 0, 1) if n in ("rel_bias", "conv_w") else a

    for n in small_names:
        reduced[n] = total[n].reshape(small(params[n], n).shape)

    out_g, out_d, out_m, out_v = {}, {}, {}, {}
    for n in LARGE:
        res = _adamw(local(params[n], n), reduced[n], local(mom[n], n), local(var[n], n), name=f"adamw_{n}")
        if n in TRANSPOSED:
            res = [jnp.swapaxes(r, 0, 1) for r in res]
        out_g[n], out_d[n], out_m[n], out_v[n] = [r[None] for r in res]
    d, nm, nv = _adamw_small([small(params[n], n) for n in small_names], [reduced[n] for n in small_names],
                             [small(mom[n], n) for n in small_names], [small(var[n], n) for n in small_names])
    for n, dd, mm, vv in zip(small_names, d, nm, nv):
        out_g[n], out_d[n], out_m[n], out_v[n] = [small(r, n) for r in (reduced[n], dd, mm, vv)]

    return (loss, grad_x, *[out_g[n] for n in ORDER], *[out_d[n] for n in ORDER],
            *[out_m[n] for n in ORDER], *[out_v[n] for n in ORDER])
```

```python
import functools
import math

import numpy as np
import jax
import jax.numpy as jnp
from jax import lax
from jax.experimental import pallas as pl
from jax.experimental.pallas import tpu as pltpu

F32 = jnp.float32
BF16 = jnp.bfloat16
MESH = pl.DeviceIdType.MESH

D_MODEL = 1024
SEQ = 2048
HEAD_DIM = 64
A_GROUPS = 4
A_WIDTH = 256
B_HEADS = 12
B_WIDTH = 768
CHUNK = 128
DILATED = ((128, 1), (512, 4), (2048, 16))
NUM_BUCKETS = 32
MAX_DISTANCE = 2048
D_FF = 2816
IN_COLS = 2816
NORM_EPS = 1e-6
NEG_INF = -1e30
N_SHARD = 4
SHARD_FF = D_FF // N_SHARD
LANE_BLOCK = 256
VMEM_LIMIT = 56 * 1024 * 1024

ADAM_LR = 0.001
ADAM_B1 = 0.9
ADAM_B2 = 0.999
ADAM_EPS = 1e-08
ADAM_WD = 0.01
ADAM_STEP = 10

GELU_C = math.sqrt(2.0 / math.pi)
GELU_A = 0.044715

ANY = pl.BlockSpec(memory_space=pl.ANY)


def _params(sem=None):
    return pltpu.CompilerParams(dimension_semantics=sem, vmem_limit_bytes=VMEM_LIMIT)


def _dot(a, b, precision=None):
    return jnp.dot(a, b, preferred_element_type=F32, precision=precision)


def _dot_nt(a, b, precision=None):
    return lax.dot_general(a, b, (((1,), (1,)), ((), ())), preferred_element_type=F32, precision=precision)


def _dot_tn(a, b):
    return lax.dot_general(a, b, (((0,), (0,)), ((), ())), preferred_element_type=F32)


def _gelu(x):
    t = jnp.tanh(x * (GELU_C + (GELU_C * GELU_A) * (x * x)))
    return (0.5 * x) * (1.0 + t)


def _gelu_and_grad(x):
    x2 = x * x
    u = 1.0 + jnp.tanh(x * (GELU_C + (GELU_C * GELU_A) * x2))
    hx = 0.5 * x
    dg = u * (0.5 + hx * (2.0 - u) * (GELU_C + (3.0 * GELU_C * GELU_A) * x2))
    return hx * u, dg


def _mesh_pos():
    x, y, c = lax.axis_index("x"), lax.axis_index("y"), lax.axis_index("c")
    chips = [(1 - x, y), (x, 1 - y), (1 - x, 1 - y)]
    return x, y, c, chips


class _GatherPlan:
    def __init__(self, shapes, out_refs, send_sems, recv_sems):
        self.shapes, self.out_refs = shapes, out_refs
        self.send_sems, self.recv_sems = send_sems, recv_sems
        self.x, self.y, self.c, self.chips = _mesh_pos()
        self.sib = (self.x, self.y, 1 - self.c)

    def _half(self, t, chip, which):
        rows = self.shapes[t][0] // 2
        return self.out_refs[t].at[2 * chip[0] + chip[1], pl.ds(which * rows, rows), :]

    def _copy(self, k, src, dst, to):
        return pltpu.make_async_remote_copy(src_ref=src, dst_ref=dst, send_sem=self.send_sems.at[k],
                                            recv_sem=self.recv_sems.at[k], device_id=to, device_id_type=MESH)

    def _sends(self, t):
        own = self._half(t, (self.x, self.y), self.c)
        return [self._copy(6 * t + j, own, own, (*chip, self.c)) for j, chip in enumerate(self.chips)]

    def _forwards(self, t):
        return [self._copy(6 * t + 3 + j, self._half(t, chip, self.c), self._half(t, chip, self.c), self.sib)
                for j, chip in enumerate(self.chips)]

    def start(self, ts):
        for t in ts:
            for cp in self._sends(t):
                cp.start()

    def forward(self, ts):
        for t in ts:
            for j, chip in enumerate(self.chips):
                landed = self._half(t, chip, self.c)
                self._copy(6 * t + j, landed, landed, (*chip, self.c)).wait_recv()
            for cp in self._forwards(t):
                cp.start()

    def finish(self, ts):
        for t in ts:
            for j, chip in enumerate(self.chips):
                other = self._half(t, chip, 1 - self.c)
                self._copy(6 * t + 3 + j, other, other, self.sib).wait_recv()
        for t in ts:
            for cp in self._sends(t) + self._forwards(t):
                cp.wait_send()


class _RelayGatherPlan:
    def __init__(self, shapes, shard_refs, out_refs, send_sems, recv_sems):
        self.shapes, self.shard_refs, self.out_refs = shapes, shard_refs, out_refs
        self.send_sems, self.recv_sems = send_sems, recv_sems
        x, y, c, self.chips = _mesh_pos()
        self.me, self.c, self.sib = (x, y), c, (x, y, 1 - c)
        self.first = (x + c - 2 * x * c, y + (1 - c) - 2 * y * (1 - c))
        self.second = (x + (1 - c) - 2 * x * (1 - c), y + c - 2 * y * c)
        self.diag = (1 - x, 1 - y)

    def _half(self, t, chip, which):
        rows = self.shapes[t][0] // 2
        return self.out_refs[t].at[2 * chip[0] + chip[1], pl.ds(which * rows, rows), :]

    def _copy(self, k, src, dst, to):
        return pltpu.make_async_remote_copy(src_ref=src, dst_ref=dst, send_sem=self.send_sems.at[k],
                                            recv_sem=self.recv_sems.at[k], device_id=to, device_id_type=MESH)

    def _own(self, t):
        if self.shard_refs is None:
            return self._half(t, self.me, self.c)
        rows = self.shapes[t][0] // 2
        return self.shard_refs[t].at[pl.ds(self.c * rows, rows), :]

    def _step1(self, t):
        return self._copy(6 * t, self._own(t), self._half(t, self.me, self.c), (*self.first, self.c))

    def _step2(self, t):
        landed = self._half(t, self.first, self.c)
        return [self._copy(6 * t + 1, self._own(t), self._half(t, self.me, self.c), (*self.second, self.c)),
                self._copy(6 * t + 2, landed, landed, (*self.second, self.c))]

    def _forwards(self, t):
        return [self._copy(6 * t + 3 + j, self._half(t, chip, self.c), self._half(t, chip, self.c), self.sib)
                for j, chip in enumerate(self.chips)]

    def start(self, ts):
        for t in ts:
            self._step1(t).start()
            self._step2(t)[0].start()

    def relay(self, ts):
        for t in ts:
            landed = self._half(t, self.first, self.c)
            self._copy(6 * t, landed, landed, self.sib).wait_recv()
            self._step2(t)[1].start()

    def forward(self, ts):
        for t in ts:
            for k, chip in ((1, self.second), (2, self.diag)):
                landed = self._half(t, chip, self.c)
                self._copy(6 * t + k, landed, landed, self.sib).wait_recv()
            for cp in self._forwards(t):
                cp.start()

    def finish(self, ts):
        for t in ts:
            for j, chip in enumerate(self.chips):
                other = self._half(t, chip, 1 - self.c)
                self._copy(6 * t + 3 + j, other, other, self.sib).wait_recv()
        for t in ts:
            for cp in [self._step1(t)] + self._step2(t) + self._forwards(t):
                cp.wait_send()


class _SiblingExchangePlan:
    def __init__(self, shapes, grad_refs, out_refs, send_sems, recv_sems):
        self.shapes, self.grad_refs, self.out_refs = shapes, grad_refs, out_refs
        self.send_sems, self.recv_sems = send_sems, recv_sems
        self.x, self.y, self.c, _ = _mesh_pos()

    def _copies(self):
        out = []
        for t, (g, o) in enumerate(zip(self.grad_refs, self.out_refs)):
            rows = self.shapes[t][1] // 2
            out.append(pltpu.make_async_remote_copy(
                src_ref=g.at[:, pl.ds((1 - self.c) * rows, rows), :], dst_ref=o, send_sem=self.send_sems.at[t],
                recv_sem=self.recv_sems.at[t], device_id=(self.x, self.y, 1 - self.c), device_id_type=MESH))
        return out

    def start(self):
        for cp in self._copies():
            cp.start()

    def finish(self):
        for cp in self._copies():
            cp.wait()


class _ChipExchangePlan:
    def __init__(self, part_refs, out_refs, send_sems, recv_sems):
        self.part_refs, self.out_refs, self.send_sems, self.recv_sems = part_refs, out_refs, send_sems, recv_sems
        _, _, self.c, self.chips = _mesh_pos()

    def _copies(self):
        return [pltpu.make_async_remote_copy(
            src_ref=p.at[2 * chip[0] + chip[1]], dst_ref=o.at[j], send_sem=self.send_sems.at[3 * t + j],
            recv_sem=self.recv_sems.at[3 * t + j], device_id=(*chip, self.c), device_id_type=MESH)
            for t, (p, o) in enumerate(zip(self.part_refs, self.out_refs)) for j, chip in enumerate(self.chips)]

    def start(self):
        for cp in self._copies():
            cp.start()

    def finish(self):
        for cp in self._copies():
            cp.wait()


class _SmallAllReducePlan:
    def __init__(self, in_refs, out_refs, sib_refs, chip_refs, send_sems, recv_sems):
        self.in_refs, self.out_refs, self.sib_refs, self.chip_refs = in_refs, out_refs, sib_refs, chip_refs
        self.send_sems, self.recv_sems = send_sems, recv_sems
        self.n = len(in_refs)
        self.x, self.y, self.c, self.chips = _mesh_pos()

    def _copy(self, k, src, dst, to):
        return pltpu.make_async_remote_copy(src_ref=src, dst_ref=dst, send_sem=self.send_sems.at[k],
                                            recv_sem=self.recv_sems.at[k], device_id=to, device_id_type=MESH)

    def _first(self):
        return [self._copy(t, self.in_refs[t], self.sib_refs[t], (self.x, self.y, 1 - self.c)) for t in range(self.n)]

    def _second(self):
        return [self._copy(self.n + 3 * t + j, self.out_refs[t], self.chip_refs[t].at[j], (*chip, self.c))
                for t in range(self.n) for j, chip in enumerate(self.chips)]

    def start_sibling(self):
        for cp in self._first():
            cp.start()

    def sum_sibling_and_start_chips(self):
        for cp in self._first():
            cp.wait()
        for t in range(self.n):
            self.out_refs[t][...] = self.in_refs[t][...] + self.sib_refs[t][...]
        for cp in self._second():
            cp.start()

    def finish(self):
        for cp in self._second():
            cp.wait()
        for t in range(self.n):
            self.out_refs[t][...] = ((self.out_refs[t][...] + self.chip_refs[t][0])
                                     + (self.chip_refs[t][1] + self.chip_refs[t][2]))

    @staticmethod
    def scratch(arrays):
        return ([pltpu.VMEM(a.shape, F32) for a in arrays] + [pltpu.VMEM((3,) + a.shape, F32) for a in arrays]
                + _sem_pair(4 * len(arrays)))


def _sem_pair(n):
    return [pltpu.SemaphoreType.DMA((n,)), pltpu.SemaphoreType.DMA((n,))]


def _mm(a, b, *, dims, tm, tn, tk, out_dtype, name):
    if dims == "nn":
        m, k = a.shape
        n = b.shape[1]
        a_spec = pl.BlockSpec((tm, tk), lambda i, j, kk: (i, kk))
        b_spec = pl.BlockSpec((tk, tn), lambda i, j, kk: (kk, j))
        dot = _dot
    elif dims == "nt":
        m, k = a.shape
        n = b.shape[0]
        a_spec = pl.BlockSpec((tm, tk), lambda i, j, kk: (i, kk))
        b_spec = pl.BlockSpec((tn, tk), lambda i, j, kk: (j, kk))
        dot = _dot_nt
    else:
        k, m = a.shape
        n = b.shape[1]
        a_spec = pl.BlockSpec((tk, tm), lambda i, j, kk: (kk, i))
        b_spec = pl.BlockSpec((tk, tn), lambda i, j, kk: (kk, j))
        dot = _dot_tn
    assert m % tm == 0 and n % tn == 0 and k % tk == 0, (name, m, n, k)
    grid = (m // tm, n // tn, k // tk)
    nk = grid[2]
    own_acc = nk > 1 and out_dtype != F32

    def body(a_ref, b_ref, o_ref, *scratch):
        prod = dot(a_ref[...].astype(BF16), b_ref[...].astype(BF16))
        if nk == 1:
            o_ref[...] = prod.astype(out_dtype)
        else:
            acc_ref = scratch[0] if own_acc else o_ref
            kk = pl.program_id(2)

            @pl.when(kk == 0)
            def _():
                acc_ref[...] = prod

            @pl.when(kk > 0)
            def _():
                acc_ref[...] += prod

            if own_acc:
                @pl.when(kk == nk - 1)
                def _():
                    o_ref[...] = acc_ref[...].astype(out_dtype)

    return pl.pallas_call(
        body, grid=grid, in_specs=[a_spec, b_spec],
        out_specs=pl.BlockSpec((tm, tn), lambda i, j, kk: (i, j)),
        out_shape=jax.ShapeDtypeStruct((m, n), out_dtype),
        scratch_shapes=[pltpu.VMEM((tm, tn), F32)] if own_acc else [],
        compiler_params=_params(("parallel", "parallel", "arbitrary")), name=name,
    )(a, b)


def _mm_pair_tn(a1, a2, b, *, tm, tk, name):
    k, m = a1.shape
    n = b.shape[1]
    assert m % tm == 0 and k % tk == 0 and a2.shape == a1.shape, name
    nk = k // tk

    def body(a1_ref, a2_ref, b_ref, o1_ref, o2_ref, acc1_ref, acc2_ref):
        bv = b_ref[...]
        p1 = _dot_tn(a1_ref[...], bv)
        p2 = _dot_tn(a2_ref[...], bv)
        kk = pl.program_id(1)

        @pl.when(kk == 0)
        def _():
            acc1_ref[...] = p1
            acc2_ref[...] = p2

        @pl.when(kk > 0)
        def _():
            acc1_ref[...] += p1
            acc2_ref[...] += p2

        @pl.when(kk == nk - 1)
        def _():
            o1_ref[...] = acc1_ref[...].astype(BF16)
            o2_ref[...] = acc2_ref[...].astype(BF16)

    a_spec = pl.BlockSpec((tk, tm), lambda i, kk: (kk, i))
    o_spec = pl.BlockSpec((tm, n), lambda i, kk: (i, 0))
    return pl.pallas_call(
        body, grid=(m // tm, nk),
        in_specs=[a_spec, a_spec, pl.BlockSpec((tk, n), lambda i, kk: (kk, 0))],
        out_specs=[o_spec, o_spec],
        out_shape=[jax.ShapeDtypeStruct((m, n), BF16)] * 2,
        scratch_shapes=[pltpu.VMEM((tm, n), F32)] * 2,
        compiler_params=_params(("parallel", "arbitrary")), name=name,
    )(a1, a2, b)


def _mm_pair_nt(a, w1_t, w2_t, own, *, tm, tn, out_dtype, name):
    m, k = a.shape
    n = w1_t.shape[0]
    assert m % tm == 0 and n % tn == 0 and w2_t.shape == w1_t.shape, name
    grid = (m // tm, n // tn)
    n_steps = grid[0] * grid[1]

    def body(a_ref, w1_ref, w2_ref, own_ref, o1_ref, o2_ref, gat_ref, send_sems, recv_sems):
        del own_ref
        step = pl.program_id(0) * grid[1] + pl.program_id(1)
        gather = _GatherPlan([own.shape[1:]], [gat_ref], send_sems, recv_sems)

        @pl.when(step == 0)
        def _():
            gather.start([0])

        @pl.when(step == (2 * n_steps) // 3)
        def _():
            gather.forward([0])

        av = a_ref[...]
        o1_ref[...] = _dot_nt(av, w1_ref[...]).astype(out_dtype)
        o2_ref[...] = _dot_nt(av, w2_ref[...]).astype(out_dtype)

        @pl.when(step == n_steps - 1)
        def _():
            gather.finish([0])

    w_spec = pl.BlockSpec((tn, k), lambda i, j: (j, 0))
    o_spec = pl.BlockSpec((tm, tn), lambda i, j: (i, j))
    return pl.pallas_call(
        body, grid=grid,
        in_specs=[pl.BlockSpec((tm, k), lambda i, j: (i, 0)), w_spec, w_spec, ANY],
        out_specs=[o_spec, o_spec, ANY],
        out_shape=[jax.ShapeDtypeStruct((m, n), out_dtype)] * 2 + [jax.ShapeDtypeStruct(own.shape, own.dtype)],
        scratch_shapes=_sem_pair(6), input_output_aliases={3: 2},
        compiler_params=_params(("arbitrary", "arbitrary")), name=name,
    )(a, w1_t, w2_t, own)


def _out_proj_bwd(a, o, dy1, w_out, grads):
    m = dy1.shape[0]
    tm = 1024
    nx = len(grads)
    shapes = [g.shape for g in grads]
    n_steps = m // tm

    def body(a_ref, o_ref, dy_ref, w_ref, *rest):
        grad_refs = rest[:nx]
        dmix_ref, dw_ref = rest[nx:nx + 2]
        acc_ref = rest[2 * nx + 2]
        exchange = _SiblingExchangePlan(shapes, grad_refs, rest[nx + 2:2 * nx + 2], *rest[2 * nx + 3:])

        @pl.when(pl.program_id(0) == 0)
        def _():
            exchange.start()

        dy = dy_ref[...]
        dmix_ref[...] = _dot_nt(dy, w_ref[...])
        top = _dot_tn(a_ref[...], dy)
        bottom = _dot_tn(o_ref[...], dy)

        @pl.when(pl.program_id(0) == 0)
        def _():
            acc_ref[:A_WIDTH, :] = top
            acc_ref[A_WIDTH:, :] = bottom

        @pl.when(pl.program_id(0) > 0)
        def _():
            acc_ref[:A_WIDTH, :] += top
            acc_ref[A_WIDTH:, :] += bottom

        @pl.when(pl.program_id(0) == n_steps - 1)
        def _():
            dw_ref[...] = acc_ref[...].astype(BF16)
            exchange.finish()

    tile = lambda width: pl.BlockSpec((tm, width), lambda i: (i, 0))
    res = pl.pallas_call(
        body, grid=(n_steps,),
        in_specs=[tile(A_WIDTH), tile(B_WIDTH), tile(D_MODEL), _full_spec((D_MODEL, D_MODEL))] + [ANY] * nx,
        out_specs=[tile(D_MODEL), _full_spec((D_MODEL, D_MODEL))] + [ANY] * nx,
        out_shape=[jax.ShapeDtypeStruct((m, D_MODEL), F32), jax.ShapeDtypeStruct((D_MODEL, D_MODEL), BF16)]
        + [jax.ShapeDtypeStruct((N_SHARD, s[1] // 2, s[2]), g.dtype) for s, g in zip(shapes, grads)],
        scratch_shapes=[pltpu.VMEM((D_MODEL, D_MODEL), F32)] + _sem_pair(nx),
        compiler_params=_params(("arbitrary",)), name="out_proj_bwd",
    )(a, o, dy1, w_out, *grads)
    return res[:2], list(res[2:])


def _fused_rows(name, tm, mats, rows, vecs, fn, row_outs, acc_outs, exchange=()):
    m = mats[0][0].shape[0]
    nm, nr, nv, nro, nao, nx = len(mats), len(rows), len(vecs), len(row_outs), len(acc_outs), len(exchange)
    n_steps = m // tm

    def body(*refs):
        a_refs, w_refs = refs[:nm], refs[nm:2 * nm]
        pos = 2 * nm
        row_refs, vec_refs, part_refs = refs[pos:pos + nr], refs[pos + nr:pos + nr + nv], refs[pos + nr + nv:pos + nr + nv + nx]
        pos += nr + nv + nx
        out_refs, acc_refs, recv_refs = refs[pos:pos + nro], refs[pos + nro:pos + nro + nao], refs[pos + nro + nao:pos + nro + nao + nx]
        sems = refs[pos + nro + nao + nx:]
        i = pl.program_id(0)
        if nx:
            plan = _ChipExchangePlan(part_refs, recv_refs, *sems)

            @pl.when(i == 0)
            def _():
                plan.start()

        @pl.when(i == 0)
        def _():
            for r in acc_refs:
                r[...] = jnp.zeros_like(r)

        y = None
        for a_ref, w_ref, (_, _, dims, sl) in zip(a_refs, w_refs, mats):
            w = w_ref[...] if sl is None else w_ref[sl, :]
            part = (_dot if dims == "nn" else _dot_nt)(a_ref[...], w)
            y = part if y is None else y + part
        res = fn(y, *[r[...] for r in row_refs], *[v[...] for v in vec_refs])
        for r, val in zip(out_refs, res[:nro]):
            r[...] = val.astype(r.dtype)
        for r, val in zip(acc_refs, res[nro:]):
            r[...] += val

        if nx:
            @pl.when(i == n_steps - 1)
            def _():
                plan.finish()

    tile = lambda width: pl.BlockSpec((tm, width), lambda i: (i, 0))
    res = pl.pallas_call(
        body, grid=(n_steps,),
        in_specs=[tile(a.shape[1]) for a, _, _, _ in mats] + [_full_spec(w.shape, single=True) for _, w, _, _ in mats]
        + [tile(D_MODEL)] * nr + [_full_spec((1, D_MODEL))] * nv + [ANY] * nx,
        out_specs=[tile(D_MODEL)] * nro + [_full_spec(s) for s in acc_outs] + [ANY] * nx,
        out_shape=[jax.ShapeDtypeStruct((m, D_MODEL), dt) for dt in row_outs]
        + [jax.ShapeDtypeStruct(s, F32) for s in acc_outs]
        + [jax.ShapeDtypeStruct((3,) + p.shape[1:], p.dtype) for p in exchange],
        scratch_shapes=_sem_pair(3 * nx) if nx else [],
        compiler_params=_params(("arbitrary",)), name=name,
    )(*[a for a, _, _, _ in mats], *[w for _, w, _, _ in mats], *rows, *vecs, *exchange)
    return list(res[:nro + nao]), list(res[nro + nao:])


def _vec_spec(width=D_MODEL):
    return pl.BlockSpec((1, width), lambda i: (0, 0))


def _rstd(v):
    return lax.rsqrt(jnp.mean(v * v, axis=-1, keepdims=True) + NORM_EPS)


def _mid_fwd_rows(y1, x0, g2, g3):
    x1 = x0 + y1 * _rstd(y1) * g2
    return y1, x1, x1 * _rstd(x1) * g3


def _rms_bwd_rows(dout, v, g):
    r = _rstd(v)
    n = v * r
    dn = dout * g
    dv = r * (dn - n * jnp.mean(dn * n, axis=-1, keepdims=True))
    dg = jnp.sum(dout * n, axis=0, keepdims=True)
    return dv, dg


def _loss_head_rows(y2, x1, tgt, g4):
    x2 = x1 + y2 * _rstd(y2) * g4
    err = x2 - tgt
    loss = 0.5 * jnp.sum(jnp.mean(err * err, axis=-1, keepdims=True), axis=0, keepdims=True)
    dx2 = err * (1.0 / D_MODEL)
    dy2, dg4 = _rms_bwd_rows(dx2, y2, g4)
    return dx2, dy2, dg4, loss


def _mid_bwd_rows(dh2, x1, y1, dx2, g2, g3):
    d3, dg3 = _rms_bwd_rows(dh2, x1, g3)
    dx1 = dx2 + d3
    dy1, dg2 = _rms_bwd_rows(dx1, y1, g2)
    return dx1, dy1, dg2, dg3


def _in_bwd_rows(dh1, x0, dx1, g1):
    d1, dg1 = _rms_bwd_rows(dh1, x0, g1)
    return dx1 + d1, dg1


GATE_ROWS = 512


def _group_mean_matrix():
    p = np.zeros((A_WIDTH, A_WIDTH), np.float32)
    for g in range(A_GROUPS):
        p[g * HEAD_DIM:(g + 1) * HEAD_DIM, g * HEAD_DIM:(g + 1) * HEAD_DIM] = 1.0 / HEAD_DIM
    return jnp.asarray(p)


def _group_masks(width=A_WIDTH):
    lane = lax.broadcasted_iota(jnp.int32, (1, width), 1)
    return [(lane >= g * HEAD_DIM) & (lane < (g + 1) * HEAD_DIM) for g in range(width // HEAD_DIM)]


GROUP_SUM_PRECISION = lax.Precision.HIGH


def _layernorm_groups(vg, pavg):
    hi = GROUP_SUM_PRECISION
    mu = _dot(vg, pavg, hi)
    xc = vg - mu
    var = _dot(xc * xc, pavg, hi)
    rstd = lax.rsqrt(var + NORM_EPS)
    return xc * rstd, rstd


def _spatial_mix(w_bf, vn_chunk_bf, masks, bz):
    z = bz
    for g in range(A_GROUPS):
        z = z + jnp.where(masks[g], _dot(w_bf[g], vn_chunk_bf), 0.0)
    return z


def _full_spec(shape, single=False):
    return pl.BlockSpec(shape, lambda i: tuple(0 for _ in shape), pipeline_mode=pl.Buffered(1) if single else None)


def _gate_fwd_rows(u, v, lg, lb, w_ref, bz, pavg, a_ref):
    masks = _group_masks()
    row = lax.broadcasted_iota(jnp.int32, (CHUNK, CHUNK), 0)
    col = lax.broadcasted_iota(jnp.int32, (CHUNK, CHUNK), 1)
    w_bf = [jnp.where(row >= col, w_ref[g], 0.0).astype(BF16) for g in range(A_GROUPS)]
    ug = _gelu(u)
    vhat, _ = _layernorm_groups(_gelu(v), pavg)
    vn = vhat * lg + lb
    for c in range(GATE_ROWS // CHUNK):
        sl = slice(c * CHUNK, (c + 1) * CHUNK)
        z = _spatial_mix(w_bf, vn[sl].astype(BF16), masks, bz)
        a_ref[sl, :] = (ug[sl] * z).astype(BF16)


def _gate_bwd(uv, dmix, ln_g, ln_b, w_s, w_st, bz, grads):
    m = uv.shape[0]
    pavg = _group_mean_matrix()
    nsteps = m // GATE_ROWS
    nx = len(grads)
    shapes = [g.shape for g in grads]

    def body(u_ref, v_ref, da_ref, lg_ref, lb_ref, w_ref, wt_ref, bz_ref, p_ref, *rest):
        grad_refs = rest[:nx]
        duv_ref, dlg_ref, dlb_ref, dw_ref, dbz_ref = rest[nx:nx + 5]
        recv_refs = rest[nx + 5:2 * nx + 5]
        exchange = _SiblingExchangePlan(shapes, grad_refs, recv_refs, *rest[2 * nx + 5:])
        i = pl.program_id(0)

        @pl.when(i == 0)
        def _():
            exchange.start()
            dlg_ref[...] = jnp.zeros_like(dlg_ref)
            dlb_ref[...] = jnp.zeros_like(dlb_ref)
            dw_ref[...] = jnp.zeros_like(dw_ref)
            dbz_ref[...] = jnp.zeros_like(dbz_ref)

        hi = GROUP_SUM_PRECISION
        masks = _group_masks()
        row = lax.broadcasted_iota(jnp.int32, (CHUNK, CHUNK), 0)
        col = lax.broadcasted_iota(jnp.int32, (CHUNK, CHUNK), 1)
        tril = row >= col
        w_bf = [jnp.where(tril, w_ref[g], 0.0).astype(BF16) for g in range(A_GROUPS)]
        wt_bf = [jnp.where(col >= row, wt_ref[g], 0.0).astype(BF16) for g in range(A_GROUPS)]
        pavg_v = p_ref[...]
        lg = lg_ref[...]
        ug, dug = _gelu_and_grad(u_ref[...])
        vg, dvg_dx = _gelu_and_grad(v_ref[...])
        vhat, rstd = _layernorm_groups(vg, pavg_v)
        vn = vhat * lg + lb_ref[...]
        da = da_ref[...]
        bz = bz_ref[...]
        for c in range(GATE_ROWS // CHUNK):
            sl = slice(c * CHUNK, (c + 1) * CHUNK)
            vn_bf = vn[sl].astype(BF16)
            z = _spatial_mix(w_bf, vn_bf, masks, bz)
            dz = da[sl] * ug[sl]
            duv_ref[sl, 0:A_WIDTH] = (da[sl] * z * dug[sl]).astype(BF16)
            dbz_ref[...] += dz
            dz_bf = dz.astype(BF16)
            dvn = jnp.zeros((CHUNK, A_WIDTH), F32)
            for g in range(A_GROUPS):
                dz_g = jnp.where(masks[g], dz, 0.0).astype(BF16)
                dw_ref[g] += jnp.where(tril, _dot_nt(dz_g, vn_bf), 0.0)
                dvn = dvn + jnp.where(masks[g], _dot(wt_bf[g], dz_bf), 0.0)
            vh = vhat[sl]
            dlb_ref[...] += jnp.sum(dvn, axis=0, keepdims=True)
            dlg_ref[...] += jnp.sum(dvn * vh, axis=0, keepdims=True)
            dvh = dvn * lg
            m1 = _dot(dvh, pavg_v, hi)
            m2 = _dot(dvh * vh, pavg_v, hi)
            duv_ref[sl, A_WIDTH:2 * A_WIDTH] = (rstd[sl] * (dvh - m1 - vh * m2) * dvg_dx[sl]).astype(BF16)

        @pl.when(i == nsteps - 1)
        def _():
            dbz_ref[...] = _dot(dbz_ref[...], pavg_v * float(HEAD_DIM), hi)
            exchange.finish()

    res = pl.pallas_call(
        body, grid=(nsteps,),
        in_specs=[pl.BlockSpec((GATE_ROWS, A_WIDTH), lambda i: (i, 0)),
                  pl.BlockSpec((GATE_ROWS, A_WIDTH), lambda i: (i, 1)),
                  pl.BlockSpec((GATE_ROWS, A_WIDTH), lambda i: (i, 0)),
                  _full_spec((1, A_WIDTH)), _full_spec((1, A_WIDTH)), _full_spec((A_GROUPS, CHUNK, CHUNK)),
                  _full_spec((A_GROUPS, CHUNK, CHUNK)), _full_spec((CHUNK, A_WIDTH)),
                  _full_spec((A_WIDTH, A_WIDTH))] + [ANY] * nx,
        out_specs=[pl.BlockSpec((GATE_ROWS, 2 * A_WIDTH), lambda i: (i, 0)),
                   _full_spec((1, A_WIDTH)), _full_spec((1, A_WIDTH)), _full_spec((A_GROUPS, CHUNK, CHUNK)),
                   _full_spec((CHUNK, A_WIDTH))] + [ANY] * nx,
        out_shape=[jax.ShapeDtypeStruct((m, IN_COLS), BF16),
                   jax.ShapeDtypeStruct((1, A_WIDTH), F32), jax.ShapeDtypeStruct((1, A_WIDTH), F32),
                   jax.ShapeDtypeStruct((A_GROUPS, CHUNK, CHUNK), F32),
                   jax.ShapeDtypeStruct((CHUNK, A_WIDTH), F32)]
        + [jax.ShapeDtypeStruct((N_SHARD, s[1] // 2, s[2]), g.dtype) for s, g in zip(shapes, grads)],
        scratch_shapes=_sem_pair(nx),
        compiler_params=_params(("arbitrary",)), name="gate_bwd",
    )(uv, uv, dmix, ln_g, ln_b, w_s, w_st, bz, pavg, *grads)
    return res[:5], list(res[5:])


Q_BLOCK = 128
PAIR = 2 * HEAD_DIM
N_PAIR = B_HEADS // 2
N_CFG = len(DILATED)
BLOCKS_PER_CFG = SEQ // Q_BLOCK
QKV_SLABS = 3 * N_PAIR
FWD_BLOCKS_PER_TRIP = 8
BWD_BLOCKS_PER_TRIP = 4


def _t5_bucket_np(dist, dtype):
    max_exact = NUM_BUCKETS // 2
    d = np.maximum(dist, 1).astype(dtype)
    large = max_exact + (np.log(d / dtype(max_exact)) / dtype(math.log(MAX_DISTANCE / max_exact))
                         * dtype(NUM_BUCKETS - max_exact))
    large = np.minimum(large.astype(np.int32), NUM_BUCKETS - 1)
    return np.where(dist < max_exact, dist, large)


def _bucket_tables():
    i = np.arange(Q_BLOCK)[:, None]
    j = np.arange(Q_BLOCK)[None, :]
    tables = []
    for _, dil in DILATED:
        rel_prev = Q_BLOCK + i - j
        rel_cur = i - j
        rel = np.concatenate([rel_prev, rel_cur], axis=1)
        valid = np.concatenate([rel_prev <= Q_BLOCK, rel_cur >= 0], axis=1)
        dist = np.maximum(rel, 0) * dil
        b32 = _t5_bucket_np(dist, np.float32)
        b64 = _t5_bucket_np(dist, np.float64)
        assert np.array_equal(b32, b64)
        tables.append(np.where(valid, b32, -1).astype(np.int32))
    return np.stack(tables)


def _present_buckets(buckets_np):
    return [sorted(set(int(v) for v in np.unique(buckets_np[c]) if v >= 0)) for c in range(N_CFG)]


def _bias_tables_body(buckets_np):
    present = _present_buckets(buckets_np)

    def tables(rb_ref, bk_ref, o_ref, ot_ref):
        for c in range(N_CFG):
            bk = bk_ref[c]
            for h in range(B_HEADS):
                acc = jnp.full((Q_BLOCK, 2 * Q_BLOCK), NEG_INF, F32)
                for b in present[c]:
                    acc = jnp.where(bk == b, rb_ref[h, b], acc)
                o_ref[c, h] = acc
                ot_ref[c, h] = acc.T

    return tables


def _proj_fwd(x, g1, w_in_t, ln_g, ln_b, w_s, bz):
    m = x.shape[0]
    tm = GATE_ROWS
    pavg = _group_mean_matrix()

    def body(x_ref, g_ref, w_ref, lg_ref, lb_ref, ws_ref, bz_ref, p_ref, h_ref, uv_ref, qkv_ref, a_ref):
        xv = x_ref[...]
        h = (xv * _rstd(xv) * g_ref[...]).astype(BF16)
        h_ref[...] = h
        acc = _dot_nt(h, w_ref[...])
        uv_ref[...] = acc[:, :2 * A_WIDTH]
        for s in range(QKV_SLABS):
            qkv_ref[s] = acc[:, 2 * A_WIDTH + s * PAIR:2 * A_WIDTH + (s + 1) * PAIR]
        _gate_fwd_rows(acc[:, :A_WIDTH], acc[:, A_WIDTH:2 * A_WIDTH], lg_ref[...], lb_ref[...], ws_ref,
                       bz_ref[...], p_ref[...], a_ref)

    return pl.pallas_call(
        body, grid=(m // tm,),
        in_specs=[pl.BlockSpec((tm, D_MODEL), lambda i: (i, 0)), _vec_spec(),
                  pl.BlockSpec((IN_COLS, D_MODEL), lambda i: (0, 0)),
                  _full_spec((1, A_WIDTH)), _full_spec((1, A_WIDTH)), _full_spec((A_GROUPS, CHUNK, CHUNK)),
                  _full_spec((CHUNK, A_WIDTH)), _full_spec((A_WIDTH, A_WIDTH))],
        out_specs=[pl.BlockSpec((tm, D_MODEL), lambda i: (i, 0)),
                   pl.BlockSpec((tm, 2 * A_WIDTH), lambda i: (i, 0)),
                   pl.BlockSpec((QKV_SLABS, tm, PAIR), lambda i: (0, i, 0)),
                   pl.BlockSpec((tm, A_WIDTH), lambda i: (i, 0))],
        out_shape=[jax.ShapeDtypeStruct((m, D_MODEL), BF16), jax.ShapeDtypeStruct((m, 2 * A_WIDTH), F32),
                   jax.ShapeDtypeStruct((QKV_SLABS, m, PAIR), F32), jax.ShapeDtypeStruct((m, A_WIDTH), BF16)],
        compiler_params=_params(("parallel",)), name="proj_fwd",
    )(x, g1, w_in_t, ln_g, ln_b, w_s, bz, pavg)


def _pair_masks():
    lane = lax.broadcasted_iota(jnp.int32, (1, PAIR), 1)
    return [lane < HEAD_DIM, lane >= HEAD_DIM]


def _block_rows(idx, dil):
    static = isinstance(idx, int)
    r, n = idx % dil, idx // dil

    def rows_of(block):
        start = r + (dil * Q_BLOCK) * block
        if dil == 1:
            return pl.ds(start if static else pl.multiple_of(start, Q_BLOCK), Q_BLOCK)
        return pl.ds(start, Q_BLOCK, stride=dil)

    prev = rows_of(n - 1) if not static or n > 0 else None
    return rows_of(n), prev


def _attn_fwd(qkv, bias, batch, owns):
    m = qkv.shape[1]
    comb_rows = 256
    nt = len(owns)
    shapes = [g.shape[1:] for g in owns]
    n_steps = batch * N_PAIR
    ts = list(range(nt))

    def body(q_ref, k_ref, v_ref, b_ref, *rest):
        o_ref, l_ref = rest[nt:nt + 2]
        gat_refs = rest[nt + 2:2 * nt + 2]
        scratch = rest[2 * nt + 2:]
        oc_refs, lc_refs = scratch[:N_CFG], scratch[N_CFG:2 * N_CFG]
        step = pl.program_id(0) * N_PAIR + pl.program_id(1)
        gather = _RelayGatherPlan(shapes, None, gat_refs, *scratch[2 * N_CFG:])

        @pl.when(step == 0)
        def _():
            gather.start(ts)

        @pl.when(step == n_steps // 2)
        def _():
            gather.relay(ts)

        @pl.when(step == n_steps - 2)
        def _():
            gather.forward(ts)

        masks = _pair_masks()
        for ci, (_, dil) in enumerate(DILATED):
            nb = SEQ // dil // Q_BLOCK

            def block(trip, ci=ci, dil=dil, nb=nb):
                work = []
                for u in range(FWD_BLOCKS_PER_TRIP):
                    rows, prow = _block_rows(trip * FWD_BLOCKS_PER_TRIP + u, dil)
                    has_prev = nb > 1 and prow is not None
                    q = q_ref[rows, :] * 0.125
                    kc = k_ref[rows, :].astype(BF16)
                    vc = v_ref[rows, :]
                    kp = k_ref[prow, :].astype(BF16) if has_prev else None
                    vp = v_ref[prow, :] if has_prev else None
                    tiles = []
                    for h in range(2):
                        qh = jnp.where(masks[h], q, 0.0).astype(BF16)
                        sc = _dot_nt(qh, kc) + b_ref[ci, h, :, Q_BLOCK:]
                        sp = _dot_nt(qh, kp) + b_ref[ci, h, :, :Q_BLOCK] if has_prev else None
                        tiles.append((sc, sp))
                    work.append((rows, vc, vp, tiles))
                probs = []
                for _, _, _, tiles in work:
                    ps = []
                    for sc, sp in tiles:
                        mx = jnp.max(sc if sp is None else jnp.maximum(sc, sp), axis=1, keepdims=True)
                        pc = jnp.exp(sc - mx).astype(BF16)
                        pp = None if sp is None else jnp.exp(sp - mx).astype(BF16)
                        ps.append((mx, pc, pp))
                    probs.append(ps)
                for (rows, vc, vp, _), ps in zip(work, probs):
                    res = []
                    for h, (_, pc, pp) in enumerate(ps):
                        r = _dot(pc, jnp.where(masks[h], vc, 1.0).astype(BF16))
                        if pp is not None:
                            r = r + _dot(pp, jnp.where(masks[h], vp, 1.0).astype(BF16))
                        res.append(r)
                    num = jnp.where(masks[0], res[0], res[1])
                    den = pltpu.roll(jnp.where(masks[0], res[1], res[0]), HEAD_DIM, 1)
                    oc_refs[ci][rows, :] = num / den
                    lc_refs[ci][rows, :] = jnp.where(masks[0], ps[0][0], ps[1][0]) + jnp.log(den)

            for trip in range(BLOCKS_PER_CFG // FWD_BLOCKS_PER_TRIP):
                block(trip)

        def combine(i, carry):
            rr = pl.ds(pl.multiple_of(i * comb_rows, comb_rows), comb_rows)
            ls = [lc_refs[c][rr, :] for c in range(N_CFG)]
            mx = functools.reduce(jnp.maximum, ls)
            ws = [jnp.exp(l - mx) for l in ls]
            tot = functools.reduce(lambda a, b: a + b, ws)
            o = functools.reduce(lambda a, b: a + b, [ws[c] * oc_refs[c][rr, :] for c in range(N_CFG)]) / tot
            o_ref[rr, :] = o.astype(BF16)
            l_ref[rr, :] = mx + jnp.log(tot)
            return carry

        lax.fori_loop(0, SEQ // comb_rows, combine, 0)

        @pl.when(step == n_steps - 1)
        def _():
            gather.finish(ts)

    def slab(first):
        return pl.BlockSpec((None, SEQ, PAIR), lambda b, p: (first + p, b, 0))

    nat = pl.BlockSpec((SEQ, PAIR), lambda b, p: (b, p))
    res = pl.pallas_call(
        body, grid=(batch, N_PAIR),
        in_specs=[slab(0), slab(N_PAIR), slab(2 * N_PAIR),
                  pl.BlockSpec((N_CFG, 2, Q_BLOCK, 2 * Q_BLOCK), lambda b, p: (0, p, 0, 0))] + [ANY] * nt,
        out_specs=[nat, nat] + [ANY] * nt,
        out_shape=[jax.ShapeDtypeStruct((m, B_WIDTH), BF16), jax.ShapeDtypeStruct((m, B_WIDTH), F32)]
        + [jax.ShapeDtypeStruct(g.shape, g.dtype) for g in owns],
        scratch_shapes=[pltpu.VMEM((SEQ, PAIR), F32)] * (2 * N_CFG) + _sem_pair(6 * nt),
        input_output_aliases={4 + t: 2 + t for t in range(nt)},
        compiler_params=_params(("arbitrary", "arbitrary")), name="attn_fwd",
    )(qkv, qkv, qkv, bias, *owns)
    return res[0], res[1], list(res[2:])


def _attn_bwd(qkv, dmix, o, lse, bias_t, dproj, batch, parts, smalls):
    m = qkv.shape[1]
    nt, ns = len(parts), len(smalls)
    n_steps = N_PAIR * batch

    def body(q_ref, k_ref, v_ref, do_ref, o_ref, l_ref, b_ref, *rest):
        part_refs = rest[1:nt + 1]
        small_refs = rest[nt + 1:nt + 1 + ns]
        pos = nt + 1 + ns
        dproj_ref, ds_ref = rest[pos:pos + 2]
        recv_refs = rest[pos + 2:pos + 2 + nt]
        sum_refs = rest[pos + 2 + nt:pos + 2 + nt + ns]
        pos += 2 + nt + ns
        dq_acc, dk_acc, dv_acc, d_scr, stage, stage_sems, send_sems, recv_sems = rest[pos:pos + 8]
        allreduce = _SmallAllReducePlan(small_refs, sum_refs, rest[pos + 8:pos + 8 + ns],
                                        rest[pos + 8 + ns:pos + 8 + 2 * ns], *rest[pos + 8 + 2 * ns:])
        pair, seq = pl.program_id(0), pl.program_id(1)
        step = pair * batch + seq
        exchange = _ChipExchangePlan(part_refs, recv_refs, send_sems, recv_sems)

        @pl.when(step == 0)
        def _():
            allreduce.start_sibling()

        @pl.when(step == n_steps // 2)
        def _():
            allreduce.sum_sibling_and_start_chips()

        def stage_copies():
            rows = pl.ds(pl.multiple_of(seq * SEQ, SEQ), SEQ)
            return [pltpu.make_async_copy(
                stage.at[k],
                dproj_ref.at[rows, pl.ds(pl.multiple_of(2 * A_WIDTH + k * B_WIDTH + pair * PAIR, PAIR), PAIR)],
                stage_sems.at[k]) for k in range(3)]

        @pl.when(step == 0)
        def _():
            exchange.start()

        @pl.when(pl.program_id(1) == 0)
        def _():
            ds_ref[...] = jnp.zeros_like(ds_ref)

        dq_acc[...] = jnp.zeros_like(dq_acc)
        dk_acc[...] = jnp.zeros_like(dk_acc)
        dv_acc[...] = jnp.zeros_like(dv_acc)
        d_scr[...] = do_ref[...] * o_ref[...].astype(F32)
        masks = _pair_masks()

        def stack_heads(t):
            return jnp.concatenate([jnp.where(masks[0], t, 0.0), jnp.where(masks[1], t, 0.0)], axis=0).astype(BF16)

        for ci, (_, dil) in enumerate(DILATED):
            nb = SEQ // dil // Q_BLOCK

            def block(trip, carry, ci=ci, dil=dil, nb=nb):
                first = []
                for u in range(BWD_BLOCKS_PER_TRIP):
                    rows, prow = _block_rows(trip * BWD_BLOCKS_PER_TRIP + u, dil)
                    has_prev = nb > 1 and prow is not None
                    if has_prev:
                        kcat = jnp.concatenate([k_ref[prow, :], k_ref[rows, :]], axis=0).astype(BF16)
                        vcat = jnp.concatenate([v_ref[prow, :], v_ref[rows, :]], axis=0).astype(BF16)
                    else:
                        kcat = k_ref[rows, :].astype(BF16)
                        vcat = v_ref[rows, :].astype(BF16)
                    qst = stack_heads(q_ref[rows, :] * 0.125)
                    dost = stack_heads(do_ref[rows, :])
                    lt = l_ref[rows, :].T
                    dt = d_scr[rows, :].T
                    lrow = jnp.concatenate([lt[0:1], lt[HEAD_DIM:HEAD_DIM + 1]], axis=1)
                    drow = jnp.concatenate([jnp.sum(dt[:HEAD_DIM], axis=0, keepdims=True),
                                            jnp.sum(dt[HEAD_DIM:], axis=0, keepdims=True)], axis=1)
                    first.append((has_prev, rows, prow, kcat, qst, dost, lrow, drow,
                                  _dot_nt(kcat, qst), _dot_nt(vcat, dost)))
                second = []
                for has_prev, rows, prow, kcat, qst, dost, lrow, drow, st, dpt in first:
                    keys = slice(0, 2 * Q_BLOCK) if has_prev else slice(Q_BLOCK, 2 * Q_BLOCK)
                    bt = jnp.concatenate([b_ref[ci, 0, keys, :], b_ref[ci, 1, keys, :]], axis=1)
                    pt = jnp.exp(st + bt - lrow)
                    dst = pt * (dpt - drow)
                    ds_ref[ci, 0, keys, :] += dst[:, :Q_BLOCK]
                    ds_ref[ci, 1, keys, :] += dst[:, Q_BLOCK:]
                    second.append((has_prev, rows, prow, kcat, qst, dost, pt.astype(BF16), dst.astype(BF16)))
                for has_prev, rows, prow, kcat, qst, dost, pt_bf, dst_bf in second:
                    dk = _dot(dst_bf, qst)
                    dv = _dot(pt_bf, dost)
                    dq2 = _dot_tn(dst_bf, kcat)
                    dq_acc[rows, :] += jnp.where(masks[0], dq2[:Q_BLOCK], dq2[Q_BLOCK:]) * 0.125
                    if has_prev:
                        dk_acc[prow, :] += dk[:Q_BLOCK]
                        dv_acc[prow, :] += dv[:Q_BLOCK]
                        dk_acc[rows, :] += dk[Q_BLOCK:]
                        dv_acc[rows, :] += dv[Q_BLOCK:]
                    else:
                        dk_acc[rows, :] += dk
                        dv_acc[rows, :] += dv
                return carry

            for trip in range(BLOCKS_PER_CFG // BWD_BLOCKS_PER_TRIP):
                block(trip, 0)

        @pl.when(step > 0)
        def _():
            for cp in stage_copies():
                cp.wait()

        stage[0] = dq_acc[...].astype(BF16)
        stage[1] = dk_acc[...].astype(BF16)
        stage[2] = dv_acc[...].astype(BF16)
        for cp in stage_copies():
            cp.start()

        @pl.when(step == n_steps - 1)
        def _():
            for cp in stage_copies():
                cp.wait()
            exchange.finish()
            allreduce.finish()

    def slab(first):
        return pl.BlockSpec((None, SEQ, PAIR), lambda p, b: (first + p, b, 0))

    nat = pl.BlockSpec((SEQ, PAIR), lambda p, b: (b, p))
    tbl = pl.BlockSpec((N_CFG, 2, 2 * Q_BLOCK, Q_BLOCK), lambda p, b: (0, p, 0, 0))
    acc = pltpu.VMEM((SEQ, PAIR), F32)
    vm = pl.BlockSpec(memory_space=pltpu.VMEM)
    res = pl.pallas_call(
        body, grid=(N_PAIR, batch),
        in_specs=[slab(0), slab(N_PAIR), slab(2 * N_PAIR),
                  pl.BlockSpec((SEQ, PAIR), lambda p, b: (b, A_WIDTH // PAIR + p)), nat, nat, tbl]
        + [ANY] * (nt + 1) + [vm] * ns,
        out_specs=[ANY, tbl] + [ANY] * nt + [vm] * ns,
        out_shape=[jax.ShapeDtypeStruct(dproj.shape, dproj.dtype),
                   jax.ShapeDtypeStruct((N_CFG, B_HEADS, 2 * Q_BLOCK, Q_BLOCK), F32)]
        + [jax.ShapeDtypeStruct((3,) + p.shape[1:], p.dtype) for p in parts]
        + [jax.ShapeDtypeStruct(a.shape, F32) for a in smalls],
        input_output_aliases={7: 0},
        scratch_shapes=[acc, acc, acc, acc, pltpu.VMEM((3, SEQ, PAIR), BF16), pltpu.SemaphoreType.DMA((3,))]
        + _sem_pair(3 * nt) + _SmallAllReducePlan.scratch(smalls),
        compiler_params=_params(("arbitrary", "arbitrary")), name="attn_bwd",
    )(qkv, qkv, qkv, dmix, o, lse, bias_t, dproj, *parts, *smalls)
    return res[0], res[1], list(res[2:2 + nt]), list(res[2 + nt:])


def _rel_bias_grad(ds, buckets_np, grads):
    present = _present_buckets(buckets_np)
    nx = len(grads)
    shapes = [g.shape for g in grads]

    def body(bk_ref, ds_ref, *rest):
        o_ref = rest[nx]
        acc_ref = rest[2 * nx + 1]
        exchange = _SiblingExchangePlan(shapes, rest[:nx], rest[nx + 1:2 * nx + 1], *rest[2 * nx + 2:])
        exchange.start()
        acc_ref[...] = jnp.zeros_like(acc_ref)
        for c in range(N_CFG):
            bk = bk_ref[c]
            for h in range(B_HEADS):
                dsv = ds_ref[c, h]
                for b in present[c]:
                    part = jnp.sum(jnp.where(bk == b, dsv, 0.0), axis=0, keepdims=True)
                    acc_ref[pl.ds(h * NUM_BUCKETS + b, 1), :] += part
        o_ref[...] = jnp.sum(acc_ref[...], axis=1, keepdims=True)
        exchange.finish()

    vm = pl.BlockSpec(memory_space=pltpu.VMEM)
    res = pl.pallas_call(
        body, in_specs=[vm, vm] + [ANY] * nx, out_specs=[vm] + [ANY] * nx,
        out_shape=[jax.ShapeDtypeStruct((B_HEADS * NUM_BUCKETS, 1), F32)]
        + [jax.ShapeDtypeStruct((N_SHARD, s[1] // 2, s[2]), g.dtype) for s, g in zip(shapes, grads)],
        scratch_shapes=[pltpu.VMEM((B_HEADS * NUM_BUCKETS, buckets_np.shape[-1]), F32)] + _sem_pair(nx),
        compiler_params=_params(), name="rel_bias_grad",
    )(jnp.asarray(buckets_np), ds, *grads)
    return res[0], list(res[1:])


def _row_index():
    return lax.broadcasted_iota(jnp.int32, (SEQ, LANE_BLOCK), 0)


def _shift_down(x, k, row):
    return jnp.where(row >= k, pltpu.roll(x, k, 0), 0.0)


def _shift_up(x, k, row):
    return jnp.where(row < SEQ - k, pltpu.roll(x, SEQ - k, 0), 0.0)


def _convgate_fwd(gate, up, conv_w, conv_b, batch):
    m = gate.shape[0]

    def body(g_ref, u_ref, w_ref, b_ref, a_ref):
        g = g_ref[...].astype(F32)
        w = w_ref[...]
        row = _row_index()
        c = b_ref[...] + w[0:1] * _shift_down(g, 2, row) + w[1:2] * _shift_down(g, 1, row) + w[2:3] * g
        a_ref[...] = (_gelu(c) * u_ref[...].astype(F32)).astype(BF16)

    blk = pl.BlockSpec((SEQ, LANE_BLOCK), lambda b, j: (b, j))
    return pl.pallas_call(
        body, grid=(batch, D_FF // LANE_BLOCK),
        in_specs=[blk, blk, pl.BlockSpec((3, LANE_BLOCK), lambda b, j: (0, j)),
                  pl.BlockSpec((1, LANE_BLOCK), lambda b, j: (0, j))],
        out_specs=blk,
        out_shape=jax.ShapeDtypeStruct((m, D_FF), BF16),
        compiler_params=_params(("parallel", "parallel")), name="convgate_fwd",
    )(gate, up, conv_w, conv_b)


def _convgate_bwd(gate, up, dact, conv_w, conv_b, batch):
    m = gate.shape[0]

    def body(g_ref, u_ref, da_ref, w_ref, b_ref, dg_ref, du_ref, dw_ref, db_ref):
        @pl.when(pl.program_id(1) == 0)
        def _():
            dw_ref[...] = jnp.zeros_like(dw_ref)
            db_ref[...] = jnp.zeros_like(db_ref)

        g = g_ref[...].astype(F32)
        w = w_ref[...]
        row = _row_index()
        g1 = _shift_down(g, 1, row)
        g2 = _shift_down(g, 2, row)
        c = b_ref[...] + w[0:1] * g2 + w[1:2] * g1 + w[2:3] * g
        gg, dgg = _gelu_and_grad(c)
        da = da_ref[...].astype(F32)
        du_ref[...] = (da * gg).astype(BF16)
        dc = da * u_ref[...].astype(F32) * dgg
        db_ref[...] += jnp.sum(dc, axis=0, keepdims=True)
        dw_ref[0:1, :] += jnp.sum(dc * g2, axis=0, keepdims=True)
        dw_ref[1:2, :] += jnp.sum(dc * g1, axis=0, keepdims=True)
        dw_ref[2:3, :] += jnp.sum(dc * g, axis=0, keepdims=True)
        dg_ref[...] = (w[2:3] * dc + w[1:2] * _shift_up(dc, 1, row) + w[0:1] * _shift_up(dc, 2, row)).astype(BF16)

    blk = pl.BlockSpec((SEQ, LANE_BLOCK), lambda j, b: (b, j))
    wspec = pl.BlockSpec((3, LANE_BLOCK), lambda j, b: (0, j))
    bspec = pl.BlockSpec((1, LANE_BLOCK), lambda j, b: (0, j))
    return pl.pallas_call(
        body, grid=(D_FF // LANE_BLOCK, batch),
        in_specs=[blk, blk, blk, wspec, bspec],
        out_specs=[blk, blk, wspec, bspec],
        out_shape=[jax.ShapeDtypeStruct((m, D_FF), BF16), jax.ShapeDtypeStruct((m, D_FF), BF16),
                   jax.ShapeDtypeStruct((3, D_FF), F32), jax.ShapeDtypeStruct((1, D_FF), F32)],
        compiler_params=_params(("parallel", "arbitrary")), name="convgate_bwd",
    )(gate, up, dact, conv_w, conv_b)


def _gather_weights(shards, conv_w_shard, rel_bias, buckets_np):
    nt = len(shards)
    shapes = [sh.shape for sh in shards]
    ts = list(range(nt))
    tables = _bias_tables_body(buckets_np)

    def body(*refs):
        shard_refs = refs[:nt]
        cw_ref, rb_ref, bk_ref = refs[nt:nt + 3]
        out_refs = refs[nt + 3:2 * nt + 3]
        cw_out, bias_ref, bias_t_ref = refs[2 * nt + 3:2 * nt + 6]
        scratch = refs[2 * nt + 6:]
        f32_refs, bf16_refs = scratch[:nt], scratch[nt:2 * nt]
        load_sems, store_sems, send_sems, recv_sems, cw_send, cw_recv = scratch[2 * nt:]
        plan = _RelayGatherPlan(shapes[:1], bf16_refs[:1], out_refs[:1], send_sems, recv_sems)
        x, y, c, chips = _mesh_pos()
        loads = [pltpu.make_async_copy(shard_refs[t], f32_refs[t], load_sems.at[t]) for t in ts]
        stores = [pltpu.make_async_copy(bf16_refs[t], out_refs[t].at[2 * x + y], store_sems.at[t]) for t in ts]
        stores.append(pltpu.make_async_copy(cw_ref, cw_out.at[2 * x + y], store_sems.at[nt]))

        def cw_copy(j, src, dst, chip):
            return pltpu.make_async_remote_copy(src_ref=src, dst_ref=dst, send_sem=cw_send.at[j],
                                                recv_sem=cw_recv.at[j], device_id=(*chip, c), device_id_type=MESH)

        def to_bf16(t):
            loads[t].wait()
            bf16_refs[t][...] = f32_refs[t][...].astype(BF16)
            stores[t].start()

        for cp in loads:
            cp.start()
        to_bf16(0)
        plan.start([0])
        cw_sends = [cw_copy(j, cw_ref, cw_out.at[2 * x + y], chip) for j, chip in enumerate(chips)]
        for cp in cw_sends + stores[nt:]:
            cp.start()
        for t in ts[1:]:
            to_bf16(t)
        plan.relay([0])
        tables(rb_ref, bk_ref, bias_ref, bias_t_ref)
        plan.forward([0])
        for j, chip in enumerate(chips):
            dst = cw_out.at[2 * chip[0] + chip[1]]
            cw_copy(j, dst, dst, chip).wait_recv()
        plan.finish([0])
        for cp in cw_sends:
            cp.wait_send()
        for cp in stores:
            cp.wait()

    out_shape = [jax.ShapeDtypeStruct((N_SHARD,) + sh.shape, BF16) for sh in shards]
    out_shape.append(jax.ShapeDtypeStruct((N_SHARD,) + conv_w_shard.shape, conv_w_shard.dtype))
    out_shape += [jax.ShapeDtypeStruct((N_CFG, B_HEADS, Q_BLOCK, 2 * Q_BLOCK), F32),
                  jax.ShapeDtypeStruct((N_CFG, B_HEADS, 2 * Q_BLOCK, Q_BLOCK), F32)]
    vm = pl.BlockSpec(memory_space=pltpu.VMEM)
    res = pl.pallas_call(
        body, in_specs=[ANY] * (nt + 1) + [pl.BlockSpec(memory_space=pltpu.SMEM), vm],
        out_specs=[ANY] * (nt + 1) + [vm, vm], out_shape=out_shape,
        scratch_shapes=[pltpu.VMEM(sh.shape, F32) for sh in shards] + [pltpu.VMEM(sh.shape, BF16) for sh in shards]
        + [pltpu.SemaphoreType.DMA((nt,)), pltpu.SemaphoreType.DMA((nt + 1,))] + _sem_pair(6) + _sem_pair(3),
        compiler_params=pltpu.CompilerParams(has_side_effects=True, vmem_limit_bytes=VMEM_LIMIT),
        name="gather_weights",
    )(*shards, conv_w_shard, rel_bias.T, jnp.asarray(buckets_np))
    return list(res[:nt + 1]), res[nt + 1], res[nt + 2]


def _turn(t, u, s, last):
    return jnp.where(t == u, s, jnp.where(t > u, last, 0))


def _add_halves(gs, recvs, c_idx):
    n = len(gs)
    dims = [(g.shape[1] // 2, g.shape[2]) for g in gs]

    def body(c_ref, *refs):
        t = pl.program_id(0)
        for u in range(n):
            @pl.when(t == u)
            def _(u=u):
                refs[2 * n + u][...] = (refs[u][...].astype(F32) + refs[n + u][...].astype(F32)).astype(BF16)

    def own(u):
        return pl.BlockSpec((None, None) + dims[u], lambda t, s, c: (_turn(t, u, s, N_SHARD - 1), c[0], 0, 0))

    def plain(u):
        return pl.BlockSpec((None,) + dims[u], lambda t, s, c: (_turn(t, u, s, N_SHARD - 1), 0, 0))

    return pl.pallas_call(
        body,
        grid_spec=pltpu.PrefetchScalarGridSpec(
            num_scalar_prefetch=1, grid=(n, N_SHARD),
            in_specs=[own(u) for u in range(n)] + [plain(u) for u in range(n)],
            out_specs=[plain(u) for u in range(n)]),
        out_shape=[jax.ShapeDtypeStruct((N_SHARD,) + d, BF16) for d in dims],
        compiler_params=_params(("arbitrary", "arbitrary")), name="rs_add_halves",
    )(c_idx, *[g.reshape((N_SHARD, 2) + d) for g, d in zip(gs, dims)], *recvs)


def _add_chips(parts, recvs, s_idx, c_idx):
    n = len(parts)
    dims = [p.shape[1:] for p in parts]

    def body(idx_ref, *refs):
        t = pl.program_id(0)
        for u in range(n):
            @pl.when(t == u)
            def _(u=u):
                acc = refs[u][...].astype(F32)
                for j in range(3):
                    acc = acc + refs[n + u][j].astype(F32)
                refs[2 * n + u][...] = acc

    res = pl.pallas_call(
        body,
        grid_spec=pltpu.PrefetchScalarGridSpec(
            num_scalar_prefetch=1, grid=(n,),
            in_specs=[pl.BlockSpec((None,) + d, lambda t, idx: (idx[0], 0, 0)) for d in dims]
            + [pl.BlockSpec((3,) + d, lambda t, idx: (0, 0, 0)) for d in dims],
            out_specs=[pl.BlockSpec((None,) + d, lambda t, idx: (idx[1], 0, 0)) for d in dims]),
        out_shape=[jax.ShapeDtypeStruct((2,) + d, F32) for d in dims],
        compiler_params=_params(("arbitrary",)), name="rs_add_chips",
    )(jnp.concatenate([s_idx, c_idx]), *parts, *recvs)
    return [r.reshape(2 * d[0], d[1]) for r, d in zip(res, dims)]


def _finish_reductions(fulls, arrays):
    nt, n = len(fulls), len(arrays)

    def body(*refs):
        in_refs = refs[nt:nt + n]
        full_refs, out_refs = refs[nt + n:2 * nt + n], refs[2 * nt + n:2 * nt + 2 * n]
        pos = 2 * nt + 2 * n
        share_send, share_recv = refs[pos + 2 * n:pos + 2 * n + 2]
        allreduce = _SmallAllReducePlan(in_refs, out_refs, refs[pos:pos + n], refs[pos + n:pos + 2 * n],
                                        *refs[pos + 2 * n + 2:])
        x, y, c, _ = _mesh_pos()

        def half(t, which):
            rows = fulls[t].shape[0] // 2
            return full_refs[t].at[pl.ds(which * rows, rows), :]

        def share(t, which):
            return pltpu.make_async_remote_copy(
                src_ref=half(t, which), dst_ref=half(t, which), send_sem=share_send.at[t],
                recv_sem=share_recv.at[t], device_id=(x, y, 1 - c), device_id_type=MESH)

        for t in range(nt):
            share(t, c).start()
        allreduce.start_sibling()
        allreduce.sum_sibling_and_start_chips()
        allreduce.finish()
        for t in range(nt):
            share(t, 1 - c).wait_recv()
        for t in range(nt):
            share(t, c).wait_send()

    vm = pl.BlockSpec(memory_space=pltpu.VMEM)
    res = pl.pallas_call(
        body, in_specs=[ANY] * nt + [vm] * n, out_specs=[ANY] * nt + [vm] * n,
        out_shape=[jax.ShapeDtypeStruct(f.shape, f.dtype) for f in fulls]
        + [jax.ShapeDtypeStruct(a.shape, F32) for a in arrays],
        input_output_aliases={t: t for t in range(nt)},
        scratch_shapes=[pltpu.VMEM(a.shape, F32) for a in arrays] + [pltpu.VMEM((3,) + a.shape, F32) for a in arrays]
        + _sem_pair(nt) + _sem_pair(4 * n),
        compiler_params=pltpu.CompilerParams(has_side_effects=True),
        name="finish_reductions",
    )(*fulls, *arrays)
    return list(res[:nt]), list(res[nt:])


def _from_col_shards(g):
    n, rows, cols = g.shape
    return g.transpose(1, 0, 2).reshape(rows, n * cols)


def _train_step(x, tgt, g1, g2, g3, g4, shards, ln_g, ln_b, w_s, b_s, rel_bias, conv_w_shard, conv_b, batch,
                s_idx, c_idx):
    buckets = _bucket_tables()
    bz = jnp.repeat(b_s.T, HEAD_DIM, axis=1)
    w_st = jnp.swapaxes(w_s, 1, 2)

    def shard_major(g):
        return g.reshape(N_SHARD, g.shape[0] // N_SHARD, D_MODEL)

    names = ["w_in", "w_out", "w_gate", "w_up", "w_down"]
    (g_in, g_out, g_gate, g_up, g_down, g_convw), bias, bias_t = _gather_weights(
        [shards[n] for n in names], conv_w_shard, rel_bias, buckets)
    w_in_t = g_in.reshape(IN_COLS, D_MODEL)
    conv_w = _from_col_shards(g_convw.reshape(N_SHARD, 3, SHARD_FF))

    h1, uv, qkv, a = _proj_fwd(x, g1, w_in_t, ln_g, ln_b, w_s, bz)
    o_bf, lse, (g_out, g_gate, g_up) = _attn_fwd(qkv, bias, batch, [g_out, g_gate, g_up])
    w_out = g_out.reshape(D_MODEL, D_MODEL)
    w_gate_t = g_gate.reshape(D_FF, D_MODEL)
    w_up_t = g_up.reshape(D_FF, D_MODEL)
    (y1, x1, h2), _ = _fused_rows(
        "out_proj_mid_fwd", 512,
        [(a, w_out, "nn", slice(0, A_WIDTH)), (o_bf, w_out, "nn", slice(A_WIDTH, D_MODEL))],
        [x], [g2, g3], _mid_fwd_rows, [F32, F32, BF16], [])
    gate, up, g_down = _mm_pair_nt(h2, w_gate_t, w_up_t, g_down, tm=1024, tn=1408, out_dtype=BF16,
                                   name="mm_gate_up")
    w_down = g_down.reshape(D_FF, D_MODEL)
    act = _convgate_fwd(gate, up, conv_w, conv_b, batch)
    (dx2, dy2, dg4, loss), _ = _fused_rows(
        "down_proj_loss_head", 512, [(act, w_down, "nn", None)], [x1, tgt], [g4], _loss_head_rows,
        [F32, BF16], [(1, D_MODEL), (1, 128)])

    dact = _mm(dy2, w_down, dims="nt", tm=1024, tn=1408, tk=1024, out_dtype=BF16, name="mm_dact")
    dw_down = _mm(act, dy2, dims="tn", tm=1408, tn=1024, tk=1024, out_dtype=BF16, name="mm_dw_down")
    dgate, dup, dconv_w, dconv_b = _convgate_bwd(gate, up, dact, conv_w, conv_b, batch)
    (dx1, dy1, dg2, dg3), _ = _fused_rows(
        "dh2_mid_bwd", 512, [(dgate, w_gate_t, "nn", None), (dup, w_up_t, "nn", None)],
        [x1, y1, dx2], [g2, g3], _mid_bwd_rows, [F32, BF16], [(1, D_MODEL), (1, D_MODEL)])
    dw_gate_t, dw_up_t = _mm_pair_tn(dgate, dup, h2, tm=1408, tk=1024, name="mm_dw_gate_up")
    done = [shard_major(g) for g in (dw_down, dw_gate_t, dw_up_t)]
    (dmix, dw_out), recv_a = _out_proj_bwd(a, o_bf, dy1, w_out, done[:2])
    done.append(shard_major(dw_out))
    (dproj, dln_g, dln_b, dw_s, dbz), recv_b = _gate_bwd(uv, dmix, ln_g, ln_b, w_s, w_st, bz, done[2:])
    recv_a += recv_b
    parts = _add_halves(done, recv_a, c_idx)
    early = dict(loss=loss, norm_mix_post=dg2, norm_ffn_pre=dg3, norm_ffn_post=dg4, ln_v_gain=dln_g,
                 ln_v_bias=dln_b, spatial_w=dw_s, spatial_b=dbz, conv_w=dconv_w, conv_b=dconv_b)
    dproj, ds, recv, early_sums = _attn_bwd(qkv, dmix, o_bf, lse, bias_t, dproj, batch, parts, list(early.values()))
    fulls = _add_chips(parts, recv, s_idx, c_idx)
    dw_in_t = _mm(dproj, h1, dims="tn", tm=1408, tn=1024, tk=1024, out_dtype=BF16, name="mm_dw_in")
    last = [shard_major(dw_in_t)]
    drel, recv_in_a = _rel_bias_grad(ds, np.ascontiguousarray(np.swapaxes(buckets, 1, 2)), last)
    part_in = _add_halves(last, recv_in_a, c_idx)
    (dx0, dg1), recv_in = _fused_rows(
        "dh1_in_bwd", 512, [(dproj, w_in_t, "nn", None)], [x, dx1], [g1], _in_bwd_rows,
        [F32], [(1, D_MODEL)], exchange=part_in)
    fulls += _add_chips(part_in, recv_in, s_idx, c_idx)
    half_reduced = dict(zip(["w_down", "w_gate", "w_up", "w_out", "w_in"], fulls))

    return dx0, dict(zip(early, early_sums)), dict(norm_mix_pre=dg1, rel_bias=drel), half_reduced


def _adamw_update(w, g, m, v):
    nm = ADAM_B1 * m + (1.0 - ADAM_B1) * g
    nv = ADAM_B2 * v + (1.0 - ADAM_B2) * (g * g)
    m_hat = nm / (1.0 - ADAM_B1 ** ADAM_STEP)
    v_hat = nv / (1.0 - ADAM_B2 ** ADAM_STEP)
    return -ADAM_LR * (m_hat / (jnp.sqrt(v_hat) + ADAM_EPS) + ADAM_WD * w), nm, nv


def _adamw(w, g, m, v, name):
    rows, cols = w.shape
    tr = next(cand for cand in (352, 256, 128) if rows % cand == 0)

    def body(w_ref, g_ref, m_ref, v_ref, go_ref, d_ref, nm_ref, nv_ref):
        gv = g_ref[...]
        go_ref[...] = gv
        d_ref[...], nm_ref[...], nv_ref[...] = _adamw_update(w_ref[...], gv, m_ref[...], v_ref[...])

    spec = pl.BlockSpec((tr, cols), lambda i: (i, 0))
    sds = jax.ShapeDtypeStruct((rows, cols), F32)
    return pl.pallas_call(
        body, grid=(rows // tr,), in_specs=[spec] * 4, out_specs=[spec] * 4, out_shape=[sds] * 4,
        compiler_params=_params(("parallel",)), name=name,
    )(w, g, m, v)


def _adamw_small(ws, gs, ms, vs):
    n = len(ws)

    def body(*refs):
        w_refs, g_refs, m_refs, v_refs = refs[:n], refs[n:2 * n], refs[2 * n:3 * n], refs[3 * n:4 * n]
        d_refs, nm_refs, nv_refs = refs[4 * n:5 * n], refs[5 * n:6 * n], refs[6 * n:7 * n]
        for t in range(n):
            d_refs[t][...], nm_refs[t][...], nv_refs[t][...] = _adamw_update(
                w_refs[t][...], g_refs[t][...], m_refs[t][...], v_refs[t][...])

    vm = pl.BlockSpec(memory_space=pltpu.VMEM)
    sds = [jax.ShapeDtypeStruct(w.shape, F32) for w in ws]
    res = pl.pallas_call(
        body, in_specs=[vm] * (4 * n), out_specs=[vm] * (3 * n), out_shape=sds * 3,
        compiler_params=_params(), name="adamw_small",
    )(*ws, *gs, *ms, *vs)
    return res[:n], res[n:2 * n], res[2 * n:]


SMALL = ["norm_mix_pre", "norm_mix_post", "norm_ffn_pre", "norm_ffn_post", "ln_v_gain", "ln_v_bias",
         "spatial_w", "spatial_b", "rel_bias", "conv_b"]
LARGE = ["w_in", "w_gate", "w_up", "w_down", "w_out"]
TRANSPOSED = ("w_in", "w_gate", "w_up")
ORDER = ["norm_mix_pre", "norm_mix_post", "norm_ffn_pre", "norm_ffn_post", "w_in", "ln_v_gain", "ln_v_bias",
         "spatial_w", "spatial_b", "rel_bias", "w_out", "w_gate", "w_up", "conv_w", "conv_b", "w_down"]


def kernel(x, norm_mix_pre, norm_mix_post, norm_ffn_pre, norm_ffn_post, w_in, ln_v_gain, ln_v_bias, spatial_w, spatial_b, rel_bias, w_out, w_gate, w_up, conv_w, conv_b, w_down, loss_target, m_norm_mix_pre, m_norm_mix_post, m_norm_ffn_pre, m_norm_ffn_post, m_w_in, m_ln_v_gain, m_ln_v_bias, m_spatial_w, m_spatial_b, m_rel_bias, m_w_out, m_w_gate, m_w_up, m_conv_w, m_conv_b, m_w_down, v_norm_mix_pre, v_norm_mix_post, v_norm_ffn_pre, v_norm_ffn_post, v_w_in, v_ln_v_gain, v_ln_v_bias, v_spatial_w, v_spatial_b, v_rel_bias, v_w_out, v_w_gate, v_w_up, v_conv_w, v_conv_b, v_w_down):
    params = dict(norm_mix_pre=norm_mix_pre, norm_mix_post=norm_mix_post, norm_ffn_pre=norm_ffn_pre,
                  norm_ffn_post=norm_ffn_post, w_in=w_in, ln_v_gain=ln_v_gain, ln_v_bias=ln_v_bias,
                  spatial_w=spatial_w, spatial_b=spatial_b, rel_bias=rel_bias, w_out=w_out, w_gate=w_gate,
                  w_up=w_up, conv_w=conv_w, conv_b=conv_b, w_down=w_down)
    mom = dict(norm_mix_pre=m_norm_mix_pre, norm_mix_post=m_norm_mix_post, norm_ffn_pre=m_norm_ffn_pre,
               norm_ffn_post=m_norm_ffn_post, w_in=m_w_in, ln_v_gain=m_ln_v_gain, ln_v_bias=m_ln_v_bias,
               spatial_w=m_spatial_w, spatial_b=m_spatial_b, rel_bias=m_rel_bias, w_out=m_w_out, w_gate=m_w_gate,
               w_up=m_w_up, conv_w=m_conv_w, conv_b=m_conv_b, w_down=m_w_down)
    var = dict(norm_mix_pre=v_norm_mix_pre, norm_mix_post=v_norm_mix_post, norm_ffn_pre=v_norm_ffn_pre,
               norm_ffn_post=v_norm_ffn_post, w_in=v_w_in, ln_v_gain=v_ln_v_gain, ln_v_bias=v_ln_v_bias,
               spatial_w=v_spatial_w, spatial_b=v_spatial_b, rel_bias=v_rel_bias, w_out=v_w_out, w_gate=v_w_gate,
               w_up=v_w_up, conv_w=v_conv_w, conv_b=v_conv_b, w_down=v_w_down)

    batch = x.shape[0]
    xi, yi, ci = lax.axis_index("x"), lax.axis_index("y"), lax.axis_index("c")
    s_idx = (2 * xi + yi).astype(jnp.int32).reshape(1)
    c_idx = ci.astype(jnp.int32).reshape(1)

    def local(a, n):
        return jnp.swapaxes(a[0], 0, 1) if n in TRANSPOSED else a[0]

    shards = {n: local(params[n], n) for n in LARGE}
    dx0, total, partial, half_reduced = _train_step(
        x.reshape(batch * SEQ, D_MODEL), loss_target.reshape(batch * SEQ, D_MODEL),
        norm_mix_pre, norm_mix_post, norm_ffn_pre, norm_ffn_post, shards,
        ln_v_gain.reshape(1, A_WIDTH), ln_v_bias.reshape(1, A_WIDTH), spatial_w[0], spatial_b[0], rel_bias,
        jnp.swapaxes(conv_w, 0, 1), conv_b, batch, s_idx, c_idx)
    grad_x = dx0.reshape(batch, SEQ, D_MODEL)

    names = list(partial)
    fulls, sums = _finish_reductions([half_reduced[n] for n in LARGE], [partial[n] for n in names])
    reduced = dict(zip(LARGE, fulls))
    total.update(zip(names, sums))
    loss = total["loss"][0, 0]
    total["spatial_b"] = total["spatial_b"][:, ::HEAD_DIM].T
    total["rel_bias"] = total["rel_bias"].reshape(B_HEADS, NUM_BUCKETS)
    total["conv_w"] = lax.dynamic_slice_in_dim(total["conv_w"], s_idx[0] * SHARD_FF, SHARD_FF, axis=1)
    small_names = SMALL + ["conv_w"]

    def small(a, n):
        return jnp.swapaxes(a, 0, 1) if n in ("rel_bias", "conv_w") else a

    for n in small_names:
        reduced[n] = total[n].reshape(small(params[n], n).shape)

    out_g, out_d, out_m, out_v = {}, {}, {}, {}
    for n in LARGE:
        res = _adamw(local(params[n], n), reduced[n], local(mom[n], n), local(var[n], n), name=f"adamw_{n}")
        if n in TRANSPOSED:
            res = [jnp.swapaxes(r, 0, 1) for r in res]
        out_g[n], out_d[n], out_m[n], out_v[n] = [r[None] for r in res]
    d, nm, nv = _adamw_small([small(params[n], n) for n in small_names], [reduced[n] for n in small_names],
                             [small(mom[n], n) for n in small_names], [small(var[n], n) for n in small_names])
    for n, dd, mm, vv in zip(small_names, d, nm, nv):
        out_g[n], out_d[n], out_m[n], out_v[n] = [small(r, n) for r in (reduced[n], dd, mm, vv)]

    return (loss, grad_x, *[out_g[n] for n in ORDER], *[out_d[n] for n in ORDER],
            *[out_m[n] for n in ORDER], *[out_v[n] for n in ORDER])
```

```python
import functools
import math

import numpy as np
import jax
import jax.numpy as jnp
from jax import lax
from jax.experimental import pallas as pl
from jax.experimental.pallas import tpu as pltpu

F32 = jnp.float32
BF16 = jnp.bfloat16
MESH = pl.DeviceIdType.MESH

D_MODEL = 1024
SEQ = 2048
HEAD_DIM = 64
A_GROUPS = 4
A_WIDTH = 256
B_HEADS = 12
B_WIDTH = 768
CHUNK = 128
DILATED = ((128, 1), (512, 4), (2048, 16))
NUM_BUCKETS = 32
MAX_DISTANCE = 2048
D_FF = 2816
IN_COLS = 2816
NORM_EPS = 1e-6
NEG_INF = -1e30
N_SHARD = 4
SHARD_FF = D_FF // N_SHARD
LANE_BLOCK = 256
VMEM_LIMIT = 56 * 1024 * 1024

ADAM_LR = 0.001
ADAM_B1 = 0.9
ADAM_B2 = 0.999
ADAM_EPS = 1e-08
ADAM_WD = 0.01
ADAM_STEP = 10

GELU_C = math.sqrt(2.0 / math.pi)
GELU_A = 0.044715

ANY = pl.BlockSpec(memory_space=pl.ANY)


def _params(sem=None):
    return pltpu.CompilerParams(dimension_semantics=sem, vmem_limit_bytes=VMEM_LIMIT)


def _dot(a, b, precision=None):
    return jnp.dot(a, b, preferred_element_type=F32, precision=precision)


def _dot_nt(a, b, precision=None):
    return lax.dot_general(a, b, (((1,), (1,)), ((), ())), preferred_element_type=F32, precision=precision)


def _dot_tn(a, b):
    return lax.dot_general(a, b, (((0,), (0,)), ((), ())), preferred_element_type=F32)


def _gelu(x):
    t = jnp.tanh(x * (GELU_C + (GELU_C * GELU_A) * (x * x)))
    return (0.5 * x) * (1.0 + t)


def _gelu_and_grad(x):
    x2 = x * x
    u = 1.0 + jnp.tanh(x * (GELU_C + (GELU_C * GELU_A) * x2))
    hx = 0.5 * x
    dg = u * (0.5 + hx * (2.0 - u) * (GELU_C + (3.0 * GELU_C * GELU_A) * x2))
    return hx * u, dg


def _mesh_pos():
    x, y, c = lax.axis_index("x"), lax.axis_index("y"), lax.axis_index("c")
    chips = [(1 - x, y), (x, 1 - y), (1 - x, 1 - y)]
    return x, y, c, chips


class _GatherPlan:
    def __init__(self, shapes, out_refs, send_sems, recv_sems):
        self.shapes, self.out_refs = shapes, out_refs
        self.send_sems, self.recv_sems = send_sems, recv_sems
        self.x, self.y, self.c, self.chips = _mesh_pos()
        self.sib = (self.x, self.y, 1 - self.c)

    def _half(self, t, chip, which):
        rows = self.shapes[t][0] // 2
        return self.out_refs[t].at[2 * chip[0] + chip[1], pl.ds(which * rows, rows), :]

    def _copy(self, k, src, dst, to):
        return pltpu.make_async_remote_copy(src_ref=src, dst_ref=dst, send_sem=self.send_sems.at[k],
                                            recv_sem=self.recv_sems.at[k], device_id=to, device_id_type=MESH)

    def _sends(self, t):
        own = self._half(t, (self.x, self.y), self.c)
        return [self._copy(6 * t + j, own, own, (*chip, self.c)) for j, chip in enumerate(self.chips)]

    def _forwards(self, t):
        return [self._copy(6 * t + 3 + j, self._half(t, chip, self.c), self._half(t, chip, self.c), self.sib)
                for j, chip in enumerate(self.chips)]

    def start(self, ts):
        for t in ts:
            for cp in self._sends(t):
                cp.start()

    def forward(self, ts):
        for t in ts:
            for j, chip in enumerate(self.chips):
                landed = self._half(t, chip, self.c)
                self._copy(6 * t + j, landed, landed, (*chip, self.c)).wait_recv()
            for cp in self._forwards(t):
                cp.start()

    def finish(self, ts):
        for t in ts:
            for j, chip in enumerate(self.chips):
                other = self._half(t, chip, 1 - self.c)
                self._copy(6 * t + 3 + j, other, other, self.sib).wait_recv()
        for t in ts:
            for cp in self._sends(t) + self._forwards(t):
                cp.wait_send()


class _RelayGatherPlan:
    def __init__(self, shapes, shard_refs, out_refs, send_sems, recv_sems):
        self.shapes, self.shard_refs, self.out_refs = shapes, shard_refs, out_refs
        self.send_sems, self.recv_sems = send_sems, recv_sems
        x, y, c, self.chips = _mesh_pos()
        self.me, self.c, self.sib = (x, y), c, (x, y, 1 - c)
        self.first = (x + c - 2 * x * c, y + (1 - c) - 2 * y * (1 - c))
        self.second = (x + (1 - c) - 2 * x * (1 - c), y + c - 2 * y * c)
        self.diag = (1 - x, 1 - y)

    def _half(self, t, chip, which):
        rows = self.shapes[t][0] // 2
        return self.out_refs[t].at[2 * chip[0] + chip[1], pl.ds(which * rows, rows), :]

    def _copy(self, k, src, dst, to):
        return pltpu.make_async_remote_copy(src_ref=src, dst_ref=dst, send_sem=self.send_sems.at[k],
                                            recv_sem=self.recv_sems.at[k], device_id=to, device_id_type=MESH)

    def _own(self, t):
        if self.shard_refs is None:
            return self._half(t, self.me, self.c)
        rows = self.shapes[t][0] // 2
        return self.shard_refs[t].at[pl.ds(self.c * rows, rows), :]

    def _step1(self, t):
        return self._copy(6 * t, self._own(t), self._half(t, self.me, self.c), (*self.first, self.c))

    def _step2(self, t):
        landed = self._half(t, self.first, self.c)
        return [self._copy(6 * t + 1, self._own(t), self._half(t, self.me, self.c), (*self.second, self.c)),
                self._copy(6 * t + 2, landed, landed, (*self.second, self.c))]

    def _forwards(self, t):
        return [self._copy(6 * t + 3 + j, self._half(t, chip, self.c), self._half(t, chip, self.c), self.sib)
                for j, chip in enumerate(self.chips)]

    def start(self, ts):
        for t in ts:
            self._step1(t).start()
            self._step2(t)[0].start()

    def relay(self, ts):
        for t in ts:
            landed = self._half(t, self.first, self.c)
            self._copy(6 * t, landed, landed, self.sib).wait_recv()
            self._step2(t)[1].start()

    def forward(self, ts):
        for t in ts:
            for k, chip in ((1, self.second), (2, self.diag)):
                landed = self._half(t, chip, self.c)
                self._copy(6 * t + k, landed, landed, self.sib).wait_recv()
            for cp in self._forwards(t):
                cp.start()

    def finish(self, ts):
        for t in ts:
            for j, chip in enumerate(self.chips):
                other = self._half(t, chip, 1 - self.c)
                self._copy(6 * t + 3 + j, other, other, self.sib).wait_recv()
        for t in ts:
            for cp in [self._step1(t)] + self._step2(t) + self._forwards(t):
                cp.wait_send()


class _SiblingExchangePlan:
    def __init__(self, shapes, grad_refs, out_refs, send_sems, recv_sems):
        self.shapes, self.grad_refs, self.out_refs = shapes, grad_refs, out_refs
        self.send_sems, self.recv_sems = send_sems, recv_sems
        self.x, self.y, self.c, _ = _mesh_pos()

    def _copies(self):
        out = []
        for t, (g, o) in enumerate(zip(self.grad_refs, self.out_refs)):
            rows = self.shapes[t][1] // 2
            out.append(pltpu.make_async_remote_copy(
                src_ref=g.at[:, pl.ds((1 - self.c) * rows, rows), :], dst_ref=o, send_sem=self.send_sems.at[t],
                recv_sem=self.recv_sems.at[t], device_id=(self.x, self.y, 1 - self.c), device_id_type=MESH))
        return out

    def start(self):
        for cp in self._copies():
            cp.start()

    def finish(self):
        for cp in self._copies():
            cp.wait()


class _ChipExchangePlan:
    def __init__(self, part_refs, out_refs, send_sems, recv_sems):
        self.part_refs, self.out_refs, self.send_sems, self.recv_sems = part_refs, out_refs, send_sems, recv_sems
        _, _, self.c, self.chips = _mesh_pos()

    def _copies(self):
        return [pltpu.make_async_remote_copy(
            src_ref=p.at[2 * chip[0] + chip[1]], dst_ref=o.at[j], send_sem=self.send_sems.at[3 * t + j],
            recv_sem=self.recv_sems.at[3 * t + j], device_id=(*chip, self.c), device_id_type=MESH)
            for t, (p, o) in enumerate(zip(self.part_refs, self.out_refs)) for j, chip in enumerate(self.chips)]

    def start(self):
        for cp in self._copies():
            cp.start()

    def finish(self):
        for cp in self._copies():
            cp.wait()


class _SmallAllReducePlan:
    def __init__(self, in_refs, out_refs, sib_refs, chip_refs, send_sems, recv_sems):
        self.in_refs, self.out_refs, self.sib_refs, self.chip_refs = in_refs, out_refs, sib_refs, chip_refs
        self.send_sems, self.recv_sems = send_sems, recv_sems
        self.n = len(in_refs)
        self.x, self.y, self.c, self.chips = _mesh_pos()

    def _copy(self, k, src, dst, to):
        return pltpu.make_async_remote_copy(src_ref=src, dst_ref=dst, send_sem=self.send_sems.at[k],
                                            recv_sem=self.recv_sems.at[k], device_id=to, device_id_type=MESH)

    def _first(self):
        return [self._copy(t, self.in_refs[t], self.sib_refs[t], (self.x, self.y, 1 - self.c)) for t in range(self.n)]

    def _second(self):
        return [self._copy(self.n + 3 * t + j, self.out_refs[t], self.chip_refs[t].at[j], (*chip, self.c))
                for t in range(self.n) for j, chip in enumerate(self.chips)]

    def start_sibling(self):
        for cp in self._first():
            cp.start()

    def sum_sibling_and_start_chips(self):
        for cp in self._first():
            cp.wait()
        for t in range(self.n):
            self.out_refs[t][...] = self.in_refs[t][...] + self.sib_refs[t][...]
        for cp in self._second():
            cp.start()

    def finish(self):
        for cp in self._second():
            cp.wait()
        for t in range(self.n):
            self.out_refs[t][...] = ((self.out_refs[t][...] + self.chip_refs[t][0])
                                     + (self.chip_refs[t][1] + self.chip_refs[t][2]))

    @staticmethod
    def scratch(arrays):
        return ([pltpu.VMEM(a.shape, F32) for a in arrays] + [pltpu.VMEM((3,) + a.shape, F32) for a in arrays]
                + _sem_pair(4 * len(arrays)))


def _sem_pair(n):
    return [pltpu.SemaphoreType.DMA((n,)), pltpu.SemaphoreType.DMA((n,))]


def _mm(a, b, *, dims, tm, tn, tk, out_dtype, name):
    if dims == "nn":
        m, k = a.shape
        n = b.shape[1]
        a_spec = pl.BlockSpec((tm, tk), lambda i, j, kk: (i, kk))
        b_spec = pl.BlockSpec((tk, tn), lambda i, j, kk: (kk, j))
        dot = _dot
    elif dims == "nt":
        m, k = a.shape
        n = b.shape[0]
        a_spec = pl.BlockSpec((tm, tk), lambda i, j, kk: (i, kk))
        b_spec = pl.BlockSpec((tn, tk), lambda i, j, kk: (j, kk))
        dot = _dot_nt
    else:
        k, m = a.shape
        n = b.shape[1]
        a_spec = pl.BlockSpec((tk, tm), lambda i, j, kk: (kk, i))
        b_spec = pl.BlockSpec((tk, tn), lambda i, j, kk: (kk, j))
        dot = _dot_tn
    assert m % tm == 0 and n % tn == 0 and k % tk == 0, (name, m, n, k)
    grid = (m // tm, n // tn, k // tk)
    nk = grid[2]
    own_acc = nk > 1 and out_dtype != F32

    def body(a_ref, b_ref, o_ref, *scratch):
        prod = dot(a_ref[...].astype(BF16), b_ref[...].astype(BF16))
        if nk == 1:
            o_ref[...] = prod.astype(out_dtype)
        else:
            acc_ref = scratch[0] if own_acc else o_ref
            kk = pl.program_id(2)

            @pl.when(kk == 0)
            def _():
                acc_ref[...] = prod

            @pl.when(kk > 0)
            def _():
                acc_ref[...] += prod

            if own_acc:
                @pl.when(kk == nk - 1)
                def _():
                    o_ref[...] = acc_ref[...].astype(out_dtype)

    return pl.pallas_call(
        body, grid=grid, in_specs=[a_spec, b_spec],
        out_specs=pl.BlockSpec((tm, tn), lambda i, j, kk: (i, j)),
        out_shape=jax.ShapeDtypeStruct((m, n), out_dtype),
        scratch_shapes=[pltpu.VMEM((tm, tn), F32)] if own_acc else [],
        compiler_params=_params(("parallel", "parallel", "arbitrary")), name=name,
    )(a, b)


def _mm_pair_tn(a1, a2, b, *, tm, tk, name):
    k, m = a1.shape
    n = b.shape[1]
    assert m % tm == 0 and k % tk == 0 and a2.shape == a1.shape, name
    nk = k // tk

    def body(a1_ref, a2_ref, b_ref, o1_ref, o2_ref, acc1_ref, acc2_ref):
        bv = b_ref[...]
        p1 = _dot_tn(a1_ref[...], bv)
        p2 = _dot_tn(a2_ref[...], bv)
        kk = pl.program_id(1)

        @pl.when(kk == 0)
        def _():
            acc1_ref[...] = p1
            acc2_ref[...] = p2

        @pl.when(kk > 0)
        def _():
            acc1_ref[...] += p1
            acc2_ref[...] += p2

        @pl.when(kk == nk - 1)
        def _():
            o1_ref[...] = acc1_ref[...].astype(BF16)
            o2_ref[...] = acc2_ref[...].astype(BF16)

    a_spec = pl.BlockSpec((tk, tm), lambda i, kk: (kk, i))
    o_spec = pl.BlockSpec((tm, n), lambda i, kk: (i, 0))
    return pl.pallas_call(
        body, grid=(m // tm, nk),
        in_specs=[a_spec, a_spec, pl.BlockSpec((tk, n), lambda i, kk: (kk, 0))],
        out_specs=[o_spec, o_spec],
        out_shape=[jax.ShapeDtypeStruct((m, n), BF16)] * 2,
        scratch_shapes=[pltpu.VMEM((tm, n), F32)] * 2,
        compiler_params=_params(("parallel", "arbitrary")), name=name,
    )(a1, a2, b)


def _mm_pair_nt(a, w1_t, w2_t, own, *, tm, tn, out_dtype, name):
    m, k = a.shape
    n = w1_t.shape[0]
    assert m % tm == 0 and n % tn == 0 and w2_t.shape == w1_t.shape, name
    grid = (m // tm, n // tn)
    n_steps = grid[0] * grid[1]

    def body(a_ref, w1_ref, w2_ref, own_ref, o1_ref, o2_ref, gat_ref, send_sems, recv_sems):
        del own_ref
        step = pl.program_id(0) * grid[1] + pl.program_id(1)
        gather = _GatherPlan([own.shape[1:]], [gat_ref], send_sems, recv_sems)

        @pl.when(step == 0)
        def _():
            gather.start([0])

        @pl.when(step == (2 * n_steps) // 3)
        def _():
            gather.forward([0])

        av = a_ref[...]
        o1_ref[...] = _dot_nt(av, w1_ref[...]).astype(out_dtype)
        o2_ref[...] = _dot_nt(av, w2_ref[...]).astype(out_dtype)

        @pl.when(step == n_steps - 1)
        def _():
            gather.finish([0])

    w_spec = pl.BlockSpec((tn, k), lambda i, j: (j, 0))
    o_spec = pl.BlockSpec((tm, tn), lambda i, j: (i, j))
    return pl.pallas_call(
        body, grid=grid,
        in_specs=[pl.BlockSpec((tm, k), lambda i, j: (i, 0)), w_spec, w_spec, ANY],
        out_specs=[o_spec, o_spec, ANY],
        out_shape=[jax.ShapeDtypeStruct((m, n), out_dtype)] * 2 + [jax.ShapeDtypeStruct(own.shape, own.dtype)],
        scratch_shapes=_sem_pair(6), input_output_aliases={3: 2},
        compiler_params=_params(("arbitrary", "arbitrary")), name=name,
    )(a, w1_t, w2_t, own)


def _out_proj_bwd(a, o, dy1, w_out, grads):
    m = dy1.shape[0]
    tm = 1024
    nx = len(grads)
    shapes = [g.shape for g in grads]
    n_steps = m // tm

    def body(a_ref, o_ref, dy_ref, w_ref, *rest):
        grad_refs = rest[:nx]
        dmix_ref, dw_ref = rest[nx:nx + 2]
        acc_ref = rest[2 * nx + 2]
        exchange = _SiblingExchangePlan(shapes, grad_refs, rest[nx + 2:2 * nx + 2], *rest[2 * nx + 3:])

        @pl.when(pl.program_id(0) == 0)
        def _():
            exchange.start()

        dy = dy_ref[...]
        dmix_ref[...] = _dot_nt(dy, w_ref[...])
        top = _dot_tn(a_ref[...], dy)
        bottom = _dot_tn(o_ref[...], dy)

        @pl.when(pl.program_id(0) == 0)
        def _():
            acc_ref[:A_WIDTH, :] = top
            acc_ref[A_WIDTH:, :] = bottom

        @pl.when(pl.program_id(0) > 0)
        def _():
            acc_ref[:A_WIDTH, :] += top
            acc_ref[A_WIDTH:, :] += bottom

        @pl.when(pl.program_id(0) == n_steps - 1)
        def _():
            dw_ref[...] = acc_ref[...].astype(BF16)
            exchange.finish()

    tile = lambda width: pl.BlockSpec((tm, width), lambda i: (i, 0))
    res = pl.pallas_call(
        body, grid=(n_steps,),
        in_specs=[tile(A_WIDTH), tile(B_WIDTH), tile(D_MODEL), _full_spec((D_MODEL, D_MODEL))] + [ANY] * nx,
        out_specs=[tile(D_MODEL), _full_spec((D_MODEL, D_MODEL))] + [ANY] * nx,
        out_shape=[jax.ShapeDtypeStruct((m, D_MODEL), F32), jax.ShapeDtypeStruct((D_MODEL, D_MODEL), BF16)]
        + [jax.ShapeDtypeStruct((N_SHARD, s[1] // 2, s[2]), g.dtype) for s, g in zip(shapes, grads)],
        scratch_shapes=[pltpu.VMEM((D_MODEL, D_MODEL), F32)] + _sem_pair(nx),
        compiler_params=_params(("arbitrary",)), name="out_proj_bwd",
    )(a, o, dy1, w_out, *grads)
    return res[:2], list(res[2:])


def _fused_rows(name, tm, mats, rows, vecs, fn, row_outs, acc_outs, exchange=()):
    m = mats[0][0].shape[0]
    nm, nr, nv, nro, nao, nx = len(mats), len(rows), len(vecs), len(row_outs), len(acc_outs), len(exchange)
    n_steps = m // tm

    def body(*refs):
        a_refs, w_refs = refs[:nm], refs[nm:2 * nm]
        pos = 2 * nm
        row_refs, vec_refs, part_refs = refs[pos:pos + nr], refs[pos + nr:pos + nr + nv], refs[pos + nr + nv:pos + nr + nv + nx]
        pos += nr + nv + nx
        out_refs, acc_refs, recv_refs = refs[pos:pos + nro], refs[pos + nro:pos + nro + nao], refs[pos + nro + nao:pos + nro + nao + nx]
        sems = refs[pos + nro + nao + nx:]
        i = pl.program_id(0)
        if nx:
            plan = _ChipExchangePlan(part_refs, recv_refs, *sems)

            @pl.when(i == 0)
            def _():
                plan.start()

        @pl.when(i == 0)
        def _():
            for r in acc_refs:
                r[...] = jnp.zeros_like(r)

        y = None
        for a_ref, w_ref, (_, _, dims, sl) in zip(a_refs, w_refs, mats):
            w = w_ref[...] if sl is None else w_ref[sl, :]
            part = (_dot if dims == "nn" else _dot_nt)(a_ref[...], w)
            y = part if y is None else y + part
        res = fn(y, *[r[...] for r in row_refs], *[v[...] for v in vec_refs])
        for r, val in zip(out_refs, res[:nro]):
            r[...] = val.astype(r.dtype)
        for r, val in zip(acc_refs, res[nro:]):
            r[...] += val

        if nx:
            @pl.when(i == n_steps - 1)
            def _():
                plan.finish()

    tile = lambda width: pl.BlockSpec((tm, width), lambda i: (i, 0))
    res = pl.pallas_call(
        body, grid=(n_steps,),
        in_specs=[tile(a.shape[1]) for a, _, _, _ in mats] + [_full_spec(w.shape) for _, w, _, _ in mats]
        + [tile(D_MODEL)] * nr + [_full_spec((1, D_MODEL))] * nv + [ANY] * nx,
        out_specs=[tile(D_MODEL)] * nro + [_full_spec(s) for s in acc_outs] + [ANY] * nx,
        out_shape=[jax.ShapeDtypeStruct((m, D_MODEL), dt) for dt in row_outs]
        + [jax.ShapeDtypeStruct(s, F32) for s in acc_outs]
        + [jax.ShapeDtypeStruct((3,) + p.shape[1:], p.dtype) for p in exchange],
        scratch_shapes=_sem_pair(3 * nx) if nx else [],
        compiler_params=_params(("arbitrary",)), name=name,
    )(*[a for a, _, _, _ in mats], *[w for _, w, _, _ in mats], *rows, *vecs, *exchange)
    return list(res[:nro + nao]), list(res[nro + nao:])


def _vec_spec(width=D_MODEL):
    return pl.BlockSpec((1, width), lambda i: (0, 0))


def _rstd(v):
    return lax.rsqrt(jnp.mean(v * v, axis=-1, keepdims=True) + NORM_EPS)


def _mid_fwd_rows(y1, x0, g2, g3):
    x1 = x0 + y1 * _rstd(y1) * g2
    return y1, x1, x1 * _rstd(x1) * g3


def _rms_bwd_rows(dout, v, g):
    r = _rstd(v)
    n = v * r
    dn = dout * g
    dv = r * (dn - n * jnp.mean(dn * n, axis=-1, keepdims=True))
    dg = jnp.sum(dout * n, axis=0, keepdims=True)
    return dv, dg


def _loss_head_rows(y2, x1, tgt, g4):
    x2 = x1 + y2 * _rstd(y2) * g4
    err = x2 - tgt
    loss = 0.5 * jnp.sum(jnp.mean(err * err, axis=-1, keepdims=True), axis=0, keepdims=True)
    dx2 = err * (1.0 / D_MODEL)
    dy2, dg4 = _rms_bwd_rows(dx2, y2, g4)
    return dx2, dy2, dg4, loss


def _mid_bwd_rows(dh2, x1, y1, dx2, g2, g3):
    d3, dg3 = _rms_bwd_rows(dh2, x1, g3)
    dx1 = dx2 + d3
    dy1, dg2 = _rms_bwd_rows(dx1, y1, g2)
    return dx1, dy1, dg2, dg3


def _in_bwd_rows(dh1, x0, dx1, g1):
    d1, dg1 = _rms_bwd_rows(dh1, x0, g1)
    return dx1 + d1, dg1


GATE_ROWS = 512


def _group_mean_matrix():
    p = np.zeros((A_WIDTH, A_WIDTH), np.float32)
    for g in range(A_GROUPS):
        p[g * HEAD_DIM:(g + 1) * HEAD_DIM, g * HEAD_DIM:(g + 1) * HEAD_DIM] = 1.0 / HEAD_DIM
    return jnp.asarray(p)


def _group_masks(width=A_WIDTH):
    lane = lax.broadcasted_iota(jnp.int32, (1, width), 1)
    return [(lane >= g * HEAD_DIM) & (lane < (g + 1) * HEAD_DIM) for g in range(width // HEAD_DIM)]


GROUP_SUM_PRECISION = lax.Precision.HIGH


def _layernorm_groups(vg, pavg):
    hi = GROUP_SUM_PRECISION
    mu = _dot(vg, pavg, hi)
    xc = vg - mu
    var = _dot(xc * xc, pavg, hi)
    rstd = lax.rsqrt(var + NORM_EPS)
    return xc * rstd, rstd


def _spatial_mix(w_bf, vn_chunk_bf, masks, bz):
    z = bz
    for g in range(A_GROUPS):
        z = z + jnp.where(masks[g], _dot(w_bf[g], vn_chunk_bf), 0.0)
    return z


def _full_spec(shape):
    return pl.BlockSpec(shape, lambda i: tuple(0 for _ in shape))


def _gate_fwd_rows(u, v, lg, lb, w_ref, bz, pavg, a_ref):
    masks = _group_masks()
    row = lax.broadcasted_iota(jnp.int32, (CHUNK, CHUNK), 0)
    col = lax.broadcasted_iota(jnp.int32, (CHUNK, CHUNK), 1)
    w_bf = [jnp.where(row >= col, w_ref[g], 0.0).astype(BF16) for g in range(A_GROUPS)]
    ug = _gelu(u)
    vhat, _ = _layernorm_groups(_gelu(v), pavg)
    vn = vhat * lg + lb
    for c in range(GATE_ROWS // CHUNK):
        sl = slice(c * CHUNK, (c + 1) * CHUNK)
        z = _spatial_mix(w_bf, vn[sl].astype(BF16), masks, bz)
        a_ref[sl, :] = (ug[sl] * z).astype(BF16)


def _gate_bwd(uv, dmix, ln_g, ln_b, w_s, w_st, bz, grads):
    m = uv.shape[0]
    pavg = _group_mean_matrix()
    nsteps = m // GATE_ROWS
    nx = len(grads)
    shapes = [g.shape for g in grads]

    def body(u_ref, v_ref, da_ref, lg_ref, lb_ref, w_ref, wt_ref, bz_ref, p_ref, *rest):
        grad_refs = rest[:nx]
        duv_ref, dlg_ref, dlb_ref, dw_ref, dbz_ref = rest[nx:nx + 5]
        recv_refs = rest[nx + 5:2 * nx + 5]
        exchange = _SiblingExchangePlan(shapes, grad_refs, recv_refs, *rest[2 * nx + 5:])
        i = pl.program_id(0)

        @pl.when(i == 0)
        def _():
            exchange.start()
            dlg_ref[...] = jnp.zeros_like(dlg_ref)
            dlb_ref[...] = jnp.zeros_like(dlb_ref)
            dw_ref[...] = jnp.zeros_like(dw_ref)
            dbz_ref[...] = jnp.zeros_like(dbz_ref)

        hi = GROUP_SUM_PRECISION
        masks = _group_masks()
        row = lax.broadcasted_iota(jnp.int32, (CHUNK, CHUNK), 0)
        col = lax.broadcasted_iota(jnp.int32, (CHUNK, CHUNK), 1)
        tril = row >= col
        w_bf = [jnp.where(tril, w_ref[g], 0.0).astype(BF16) for g in range(A_GROUPS)]
        wt_bf = [jnp.where(col >= row, wt_ref[g], 0.0).astype(BF16) for g in range(A_GROUPS)]
        pavg_v = p_ref[...]
        lg = lg_ref[...]
        ug, dug = _gelu_and_grad(u_ref[...])
        vg, dvg_dx = _gelu_and_grad(v_ref[...])
        vhat, rstd = _layernorm_groups(vg, pavg_v)
        vn = vhat * lg + lb_ref[...]
        da = da_ref[...]
        bz = bz_ref[...]
        for c in range(GATE_ROWS // CHUNK):
            sl = slice(c * CHUNK, (c + 1) * CHUNK)
            vn_bf = vn[sl].astype(BF16)
            z = _spatial_mix(w_bf, vn_bf, masks, bz)
            dz = da[sl] * ug[sl]
            duv_ref[sl, 0:A_WIDTH] = (da[sl] * z * dug[sl]).astype(BF16)
            dbz_ref[...] += dz
            dz_bf = dz.astype(BF16)
            dvn = jnp.zeros((CHUNK, A_WIDTH), F32)
            for g in range(A_GROUPS):
                dz_g = jnp.where(masks[g], dz, 0.0).astype(BF16)
                dw_ref[g] += jnp.where(tril, _dot_nt(dz_g, vn_bf), 0.0)
                dvn = dvn + jnp.where(masks[g], _dot(wt_bf[g], dz_bf), 0.0)
            vh = vhat[sl]
            dlb_ref[...] += jnp.sum(dvn, axis=0, keepdims=True)
            dlg_ref[...] += jnp.sum(dvn * vh, axis=0, keepdims=True)
            dvh = dvn * lg
            m1 = _dot(dvh, pavg_v, hi)
            m2 = _dot(dvh * vh, pavg_v, hi)
            duv_ref[sl, A_WIDTH:2 * A_WIDTH] = (rstd[sl] * (dvh - m1 - vh * m2) * dvg_dx[sl]).astype(BF16)

        @pl.when(i == nsteps - 1)
        def _():
            dbz_ref[...] = _dot(dbz_ref[...], pavg_v * float(HEAD_DIM), hi)
            exchange.finish()

    res = pl.pallas_call(
        body, grid=(nsteps,),
        in_specs=[pl.BlockSpec((GATE_ROWS, A_WIDTH), lambda i: (i, 0)),
                  pl.BlockSpec((GATE_ROWS, A_WIDTH), lambda i: (i, 1)),
                  pl.BlockSpec((GATE_ROWS, A_WIDTH), lambda i: (i, 0)),
                  _full_spec((1, A_WIDTH)), _full_spec((1, A_WIDTH)), _full_spec((A_GROUPS, CHUNK, CHUNK)),
                  _full_spec((A_GROUPS, CHUNK, CHUNK)), _full_spec((CHUNK, A_WIDTH)),
                  _full_spec((A_WIDTH, A_WIDTH))] + [ANY] * nx,
        out_specs=[pl.BlockSpec((GATE_ROWS, 2 * A_WIDTH), lambda i: (i, 0)),
                   _full_spec((1, A_WIDTH)), _full_spec((1, A_WIDTH)), _full_spec((A_GROUPS, CHUNK, CHUNK)),
                   _full_spec((CHUNK, A_WIDTH))] + [ANY] * nx,
        out_shape=[jax.ShapeDtypeStruct((m, IN_COLS), BF16),
                   jax.ShapeDtypeStruct((1, A_WIDTH), F32), jax.ShapeDtypeStruct((1, A_WIDTH), F32),
                   jax.ShapeDtypeStruct((A_GROUPS, CHUNK, CHUNK), F32),
                   jax.ShapeDtypeStruct((CHUNK, A_WIDTH), F32)]
        + [jax.ShapeDtypeStruct((N_SHARD, s[1] // 2, s[2]), g.dtype) for s, g in zip(shapes, grads)],
        scratch_shapes=_sem_pair(nx),
        compiler_params=_params(("arbitrary",)), name="gate_bwd",
    )(uv, uv, dmix, ln_g, ln_b, w_s, w_st, bz, pavg, *grads)
    return res[:5], list(res[5:])


Q_BLOCK = 128
PAIR = 2 * HEAD_DIM
N_PAIR = B_HEADS // 2
N_CFG = len(DILATED)
BLOCKS_PER_CFG = SEQ // Q_BLOCK
QKV_SLABS = 3 * N_PAIR
FWD_BLOCKS_PER_TRIP = 8
BWD_BLOCKS_PER_TRIP = 4


def _t5_bucket_np(dist, dtype):
    max_exact = NUM_BUCKETS // 2
    d = np.maximum(dist, 1).astype(dtype)
    large = max_exact + (np.log(d / dtype(max_exact)) / dtype(math.log(MAX_DISTANCE / max_exact))
                         * dtype(NUM_BUCKETS - max_exact))
    large = np.minimum(large.astype(np.int32), NUM_BUCKETS - 1)
    return np.where(dist < max_exact, dist, large)


def _bucket_tables():
    i = np.arange(Q_BLOCK)[:, None]
    j = np.arange(Q_BLOCK)[None, :]
    tables = []
    for _, dil in DILATED:
        rel_prev = Q_BLOCK + i - j
        rel_cur = i - j
        rel = np.concatenate([rel_prev, rel_cur], axis=1)
        valid = np.concatenate([rel_prev <= Q_BLOCK, rel_cur >= 0], axis=1)
        dist = np.maximum(rel, 0) * dil
        b32 = _t5_bucket_np(dist, np.float32)
        b64 = _t5_bucket_np(dist, np.float64)
        assert np.array_equal(b32, b64)
        tables.append(np.where(valid, b32, -1).astype(np.int32))
    return np.stack(tables)


def _present_buckets(buckets_np):
    return [sorted(set(int(v) for v in np.unique(buckets_np[c]) if v >= 0)) for c in range(N_CFG)]


def _bias_tables_body(buckets_np):
    present = _present_buckets(buckets_np)

    def tables(rb_ref, bk_ref, o_ref, ot_ref):
        for c in range(N_CFG):
            bk = bk_ref[c]
            for h in range(B_HEADS):
                acc = jnp.full((Q_BLOCK, 2 * Q_BLOCK), NEG_INF, F32)
                for b in present[c]:
                    acc = jnp.where(bk == b, rb_ref[h, b], acc)
                o_ref[c, h] = acc
                ot_ref[c, h] = acc.T

    return tables


def _proj_fwd(x, g1, w_in_t, ln_g, ln_b, w_s, bz):
    m = x.shape[0]
    tm = GATE_ROWS
    pavg = _group_mean_matrix()

    def body(x_ref, g_ref, w_ref, lg_ref, lb_ref, ws_ref, bz_ref, p_ref, h_ref, uv_ref, qkv_ref, a_ref):
        xv = x_ref[...]
        h = (xv * _rstd(xv) * g_ref[...]).astype(BF16)
        h_ref[...] = h
        acc = _dot_nt(h, w_ref[...])
        uv_ref[...] = acc[:, :2 * A_WIDTH]
        for s in range(QKV_SLABS):
            qkv_ref[s] = acc[:, 2 * A_WIDTH + s * PAIR:2 * A_WIDTH + (s + 1) * PAIR]
        _gate_fwd_rows(acc[:, :A_WIDTH], acc[:, A_WIDTH:2 * A_WIDTH], lg_ref[...], lb_ref[...], ws_ref,
                       bz_ref[...], p_ref[...], a_ref)

    return pl.pallas_call(
        body, grid=(m // tm,),
        in_specs=[pl.BlockSpec((tm, D_MODEL), lambda i: (i, 0)), _vec_spec(),
                  pl.BlockSpec((IN_COLS, D_MODEL), lambda i: (0, 0)),
                  _full_spec((1, A_WIDTH)), _full_spec((1, A_WIDTH)), _full_spec((A_GROUPS, CHUNK, CHUNK)),
                  _full_spec((CHUNK, A_WIDTH)), _full_spec((A_WIDTH, A_WIDTH))],
        out_specs=[pl.BlockSpec((tm, D_MODEL), lambda i: (i, 0)),
                   pl.BlockSpec((tm, 2 * A_WIDTH), lambda i: (i, 0)),
                   pl.BlockSpec((QKV_SLABS, tm, PAIR), lambda i: (0, i, 0)),
                   pl.BlockSpec((tm, A_WIDTH), lambda i: (i, 0))],
        out_shape=[jax.ShapeDtypeStruct((m, D_MODEL), BF16), jax.ShapeDtypeStruct((m, 2 * A_WIDTH), F32),
                   jax.ShapeDtypeStruct((QKV_SLABS, m, PAIR), F32), jax.ShapeDtypeStruct((m, A_WIDTH), BF16)],
        compiler_params=_params(("parallel",)), name="proj_fwd",
    )(x, g1, w_in_t, ln_g, ln_b, w_s, bz, pavg)


def _pair_masks():
    lane = lax.broadcasted_iota(jnp.int32, (1, PAIR), 1)
    return [lane < HEAD_DIM, lane >= HEAD_DIM]


def _block_rows(idx, dil):
    static = isinstance(idx, int)
    r, n = idx % dil, idx // dil

    def rows_of(block):
        start = r + (dil * Q_BLOCK) * block
        if dil == 1:
            return pl.ds(start if static else pl.multiple_of(start, Q_BLOCK), Q_BLOCK)
        return pl.ds(start, Q_BLOCK, stride=dil)

    prev = rows_of(n - 1) if not static or n > 0 else None
    return rows_of(n), prev


def _attn_fwd(qkv, bias, batch, owns):
    m = qkv.shape[1]
    comb_rows = 256
    nt = len(owns)
    shapes = [g.shape[1:] for g in owns]
    n_steps = batch * N_PAIR
    ts = list(range(nt))

    def body(q_ref, k_ref, v_ref, b_ref, *rest):
        o_ref, l_ref = rest[nt:nt + 2]
        gat_refs = rest[nt + 2:2 * nt + 2]
        scratch = rest[2 * nt + 2:]
        oc_refs, lc_refs = scratch[:N_CFG], scratch[N_CFG:2 * N_CFG]
        step = pl.program_id(0) * N_PAIR + pl.program_id(1)
        gather = _RelayGatherPlan(shapes, None, gat_refs, *scratch[2 * N_CFG:])

        @pl.when(step == 0)
        def _():
            gather.start(ts)

        @pl.when(step == n_steps // 2)
        def _():
            gather.relay(ts)

        @pl.when(step == n_steps - 2)
        def _():
            gather.forward(ts)

        masks = _pair_masks()
        for ci, (_, dil) in enumerate(DILATED):
            nb = SEQ // dil // Q_BLOCK

            def block(trip, ci=ci, dil=dil, nb=nb):
                work = []
                for u in range(FWD_BLOCKS_PER_TRIP):
                    rows, prow = _block_rows(trip * FWD_BLOCKS_PER_TRIP + u, dil)
                    has_prev = nb > 1 and prow is not None
                    q = q_ref[rows, :] * 0.125
                    kc = k_ref[rows, :].astype(BF16)
                    vc = v_ref[rows, :]
                    kp = k_ref[prow, :].astype(BF16) if has_prev else None
                    vp = v_ref[prow, :] if has_prev else None
                    tiles = []
                    for h in range(2):
                        qh = jnp.where(masks[h], q, 0.0).astype(BF16)
                        sc = _dot_nt(qh, kc) + b_ref[ci, h, :, Q_BLOCK:]
                        sp = _dot_nt(qh, kp) + b_ref[ci, h, :, :Q_BLOCK] if has_prev else None
                        tiles.append((sc, sp))
                    work.append((rows, vc, vp, tiles))
                probs = []
                for _, _, _, tiles in work:
                    ps = []
                    for sc, sp in tiles:
                        mx = jnp.max(sc if sp is None else jnp.maximum(sc, sp), axis=1, keepdims=True)
                        pc = jnp.exp(sc - mx).astype(BF16)
                        pp = None if sp is None else jnp.exp(sp - mx).astype(BF16)
                        ps.append((mx, pc, pp))
                    probs.append(ps)
                for (rows, vc, vp, _), ps in zip(work, probs):
                    res = []
                    for h, (_, pc, pp) in enumerate(ps):
                        r = _dot(pc, jnp.where(masks[h], vc, 1.0).astype(BF16))
                        if pp is not None:
                            r = r + _dot(pp, jnp.where(masks[h], vp, 1.0).astype(BF16))
                        res.append(r)
                    num = jnp.where(masks[0], res[0], res[1])
                    den = pltpu.roll(jnp.where(masks[0], res[1], res[0]), HEAD_DIM, 1)
                    oc_refs[ci][rows, :] = num / den
                    lc_refs[ci][rows, :] = jnp.where(masks[0], ps[0][0], ps[1][0]) + jnp.log(den)

            for trip in range(BLOCKS_PER_CFG // FWD_BLOCKS_PER_TRIP):
                block(trip)

        def combine(i, carry):
            rr = pl.ds(pl.multiple_of(i * comb_rows, comb_rows), comb_rows)
            ls = [lc_refs[c][rr, :] for c in range(N_CFG)]
            mx = functools.reduce(jnp.maximum, ls)
            ws = [jnp.exp(l - mx) for l in ls]
            tot = functools.reduce(lambda a, b: a + b, ws)
            o = functools.reduce(lambda a, b: a + b, [ws[c] * oc_refs[c][rr, :] for c in range(N_CFG)]) / tot
            o_ref[rr, :] = o.astype(BF16)
            l_ref[rr, :] = mx + jnp.log(tot)
            return carry

        lax.fori_loop(0, SEQ // comb_rows, combine, 0)

        @pl.when(step == n_steps - 1)
        def _():
            gather.finish(ts)

    def slab(first):
        return pl.BlockSpec((None, SEQ, PAIR), lambda b, p: (first + p, b, 0))

    nat = pl.BlockSpec((SEQ, PAIR), lambda b, p: (b, p))
    res = pl.pallas_call(
        body, grid=(batch, N_PAIR),
        in_specs=[slab(0), slab(N_PAIR), slab(2 * N_PAIR),
                  pl.BlockSpec((N_CFG, 2, Q_BLOCK, 2 * Q_BLOCK), lambda b, p: (0, p, 0, 0))] + [ANY] * nt,
        out_specs=[nat, nat] + [ANY] * nt,
        out_shape=[jax.ShapeDtypeStruct((m, B_WIDTH), BF16), jax.ShapeDtypeStruct((m, B_WIDTH), F32)]
        + [jax.ShapeDtypeStruct(g.shape, g.dtype) for g in owns],
        scratch_shapes=[pltpu.VMEM((SEQ, PAIR), F32)] * (2 * N_CFG) + _sem_pair(6 * nt),
        input_output_aliases={4 + t: 2 + t for t in range(nt)},
        compiler_params=_params(("arbitrary", "arbitrary")), name="attn_fwd",
    )(qkv, qkv, qkv, bias, *owns)
    return res[0], res[1], list(res[2:])


def _attn_bwd(qkv, dmix, o, lse, bias_t, dproj, batch, parts, smalls):
    m = qkv.shape[1]
    nt, ns = len(parts), len(smalls)
    n_steps = N_PAIR * batch

    def body(q_ref, k_ref, v_ref, do_ref, o_ref, l_ref, b_ref, *rest):
        part_refs = rest[1:nt + 1]
        small_refs = rest[nt + 1:nt + 1 + ns]
        pos = nt + 1 + ns
        dproj_ref, ds_ref = rest[pos:pos + 2]
        recv_refs = rest[pos + 2:pos + 2 + nt]
        sum_refs = rest[pos + 2 + nt:pos + 2 + nt + ns]
        pos += 2 + nt + ns
        dq_acc, dk_acc, dv_acc, d_scr, stage, stage_sems, send_sems, recv_sems = rest[pos:pos + 8]
        allreduce = _SmallAllReducePlan(small_refs, sum_refs, rest[pos + 8:pos + 8 + ns],
                                        rest[pos + 8 + ns:pos + 8 + 2 * ns], *rest[pos + 8 + 2 * ns:])
        pair, seq = pl.program_id(0), pl.program_id(1)
        step = pair * batch + seq
        exchange = _ChipExchangePlan(part_refs, recv_refs, send_sems, recv_sems)

        @pl.when(step == 0)
        def _():
            allreduce.start_sibling()

        @pl.when(step == n_steps // 2)
        def _():
            allreduce.sum_sibling_and_start_chips()

        def stage_copies():
            rows = pl.ds(pl.multiple_of(seq * SEQ, SEQ), SEQ)
            return [pltpu.make_async_copy(
                stage.at[k],
                dproj_ref.at[rows, pl.ds(pl.multiple_of(2 * A_WIDTH + k * B_WIDTH + pair * PAIR, PAIR), PAIR)],
                stage_sems.at[k]) for k in range(3)]

        @pl.when(step == 0)
        def _():
            exchange.start()

        @pl.when(pl.program_id(1) == 0)
        def _():
            ds_ref[...] = jnp.zeros_like(ds_ref)

        dq_acc[...] = jnp.zeros_like(dq_acc)
        dk_acc[...] = jnp.zeros_like(dk_acc)
        dv_acc[...] = jnp.zeros_like(dv_acc)
        d_scr[...] = do_ref[...] * o_ref[...].astype(F32)
        masks = _pair_masks()

        def stack_heads(t):
            return jnp.concatenate([jnp.where(masks[0], t, 0.0), jnp.where(masks[1], t, 0.0)], axis=0).astype(BF16)

        for ci, (_, dil) in enumerate(DILATED):
            nb = SEQ // dil // Q_BLOCK

            def block(trip, carry, ci=ci, dil=dil, nb=nb):
                first = []
                for u in range(BWD_BLOCKS_PER_TRIP):
                    rows, prow = _block_rows(trip * BWD_BLOCKS_PER_TRIP + u, dil)
                    has_prev = nb > 1 and prow is not None
                    if has_prev:
                        kcat = jnp.concatenate([k_ref[prow, :], k_ref[rows, :]], axis=0).astype(BF16)
                        vcat = jnp.concatenate([v_ref[prow, :], v_ref[rows, :]], axis=0).astype(BF16)
                    else:
                        kcat = k_ref[rows, :].astype(BF16)
                        vcat = v_ref[rows, :].astype(BF16)
                    qst = stack_heads(q_ref[rows, :] * 0.125)
                    dost = stack_heads(do_ref[rows, :])
                    lt = l_ref[rows, :].T
                    dt = d_scr[rows, :].T
                    lrow = jnp.concatenate([lt[0:1], lt[HEAD_DIM:HEAD_DIM + 1]], axis=1)
                    drow = jnp.concatenate([jnp.sum(dt[:HEAD_DIM], axis=0, keepdims=True),
                                            jnp.sum(dt[HEAD_DIM:], axis=0, keepdims=True)], axis=1)
                    first.append((has_prev, rows, prow, kcat, qst, dost, lrow, drow,
                                  _dot_nt(kcat, qst), _dot_nt(vcat, dost)))
                second = []
                for has_prev, rows, prow, kcat, qst, dost, lrow, drow, st, dpt in first:
                    keys = slice(0, 2 * Q_BLOCK) if has_prev else slice(Q_BLOCK, 2 * Q_BLOCK)
                    bt = jnp.concatenate([b_ref[ci, 0, keys, :], b_ref[ci, 1, keys, :]], axis=1)
                    pt = jnp.exp(st + bt - lrow)
                    dst = pt * (dpt - drow)
                    ds_ref[ci, 0, keys, :] += dst[:, :Q_BLOCK]
                    ds_ref[ci, 1, keys, :] += dst[:, Q_BLOCK:]
                    second.append((has_prev, rows, prow, kcat, qst, dost, pt.astype(BF16), dst.astype(BF16)))
                for has_prev, rows, prow, kcat, qst, dost, pt_bf, dst_bf in second:
                    dk = _dot(dst_bf, qst)
                    dv = _dot(pt_bf, dost)
                    dq2 = _dot_tn(dst_bf, kcat)
                    dq_acc[rows, :] += jnp.where(masks[0], dq2[:Q_BLOCK], dq2[Q_BLOCK:]) * 0.125
                    if has_prev:
                        dk_acc[prow, :] += dk[:Q_BLOCK]
                        dv_acc[prow, :] += dv[:Q_BLOCK]
                        dk_acc[rows, :] += dk[Q_BLOCK:]
                        dv_acc[rows, :] += dv[Q_BLOCK:]
                    else:
                        dk_acc[rows, :] += dk
                        dv_acc[rows, :] += dv
                return carry

            for trip in range(BLOCKS_PER_CFG // BWD_BLOCKS_PER_TRIP):
                block(trip, 0)

        @pl.when(step > 0)
        def _():
            for cp in stage_copies():
                cp.wait()

        stage[0] = dq_acc[...].astype(BF16)
        stage[1] = dk_acc[...].astype(BF16)
        stage[2] = dv_acc[...].astype(BF16)
        for cp in stage_copies():
            cp.start()

        @pl.when(step == n_steps - 1)
        def _():
            for cp in stage_copies():
                cp.wait()
            exchange.finish()
            allreduce.finish()

    def slab(first):
        return pl.BlockSpec((None, SEQ, PAIR), lambda p, b: (first + p, b, 0))

    nat = pl.BlockSpec((SEQ, PAIR), lambda p, b: (b, p))
    tbl = pl.BlockSpec((N_CFG, 2, 2 * Q_BLOCK, Q_BLOCK), lambda p, b: (0, p, 0, 0))
    acc = pltpu.VMEM((SEQ, PAIR), F32)
    vm = pl.BlockSpec(memory_space=pltpu.VMEM)
    res = pl.pallas_call(
        body, grid=(N_PAIR, batch),
        in_specs=[slab(0), slab(N_PAIR), slab(2 * N_PAIR),
                  pl.BlockSpec((SEQ, PAIR), lambda p, b: (b, A_WIDTH // PAIR + p)), nat, nat, tbl]
        + [ANY] * (nt + 1) + [vm] * ns,
        out_specs=[ANY, tbl] + [ANY] * nt + [vm] * ns,
        out_shape=[jax.ShapeDtypeStruct(dproj.shape, dproj.dtype),
                   jax.ShapeDtypeStruct((N_CFG, B_HEADS, 2 * Q_BLOCK, Q_BLOCK), F32)]
        + [jax.ShapeDtypeStruct((3,) + p.shape[1:], p.dtype) for p in parts]
        + [jax.ShapeDtypeStruct(a.shape, F32) for a in smalls],
        input_output_aliases={7: 0},
        scratch_shapes=[acc, acc, acc, acc, pltpu.VMEM((3, SEQ, PAIR), BF16), pltpu.SemaphoreType.DMA((3,))]
        + _sem_pair(3 * nt) + _SmallAllReducePlan.scratch(smalls),
        compiler_params=_params(("arbitrary", "arbitrary")), name="attn_bwd",
    )(qkv, qkv, qkv, dmix, o, lse, bias_t, dproj, *parts, *smalls)
    return res[0], res[1], list(res[2:2 + nt]), list(res[2 + nt:])


def _rel_bias_grad(ds, buckets_np, grads):
    present = _present_buckets(buckets_np)
    nx = len(grads)
    shapes = [g.shape for g in grads]

    def body(bk_ref, ds_ref, *rest):
        o_ref = rest[nx]
        acc_ref = rest[2 * nx + 1]
        exchange = _SiblingExchangePlan(shapes, rest[:nx], rest[nx + 1:2 * nx + 1], *rest[2 * nx + 2:])
        exchange.start()
        acc_ref[...] = jnp.zeros_like(acc_ref)
        for c in range(N_CFG):
            bk = bk_ref[c]
            for h in range(B_HEADS):
                dsv = ds_ref[c, h]
                for b in present[c]:
                    part = jnp.sum(jnp.where(bk == b, dsv, 0.0), axis=0, keepdims=True)
                    acc_ref[pl.ds(h * NUM_BUCKETS + b, 1), :] += part
        o_ref[...] = jnp.sum(acc_ref[...], axis=1, keepdims=True)
        exchange.finish()

    vm = pl.BlockSpec(memory_space=pltpu.VMEM)
    res = pl.pallas_call(
        body, in_specs=[vm, vm] + [ANY] * nx, out_specs=[vm] + [ANY] * nx,
        out_shape=[jax.ShapeDtypeStruct((B_HEADS * NUM_BUCKETS, 1), F32)]
        + [jax.ShapeDtypeStruct((N_SHARD, s[1] // 2, s[2]), g.dtype) for s, g in zip(shapes, grads)],
        scratch_shapes=[pltpu.VMEM((B_HEADS * NUM_BUCKETS, buckets_np.shape[-1]), F32)] + _sem_pair(nx),
        compiler_params=_params(), name="rel_bias_grad",
    )(jnp.asarray(buckets_np), ds, *grads)
    return res[0], list(res[1:])


def _row_index():
    return lax.broadcasted_iota(jnp.int32, (SEQ, LANE_BLOCK), 0)


def _shift_down(x, k, row):
    return jnp.where(row >= k, pltpu.roll(x, k, 0), 0.0)


def _shift_up(x, k, row):
    return jnp.where(row < SEQ - k, pltpu.roll(x, SEQ - k, 0), 0.0)


def _convgate_fwd(gate, up, conv_w, conv_b, batch):
    m = gate.shape[0]

    def body(g_ref, u_ref, w_ref, b_ref, a_ref):
        g = g_ref[...].astype(F32)
        w = w_ref[...]
        row = _row_index()
        c = b_ref[...] + w[0:1] * _shift_down(g, 2, row) + w[1:2] * _shift_down(g, 1, row) + w[2:3] * g
        a_ref[...] = (_gelu(c) * u_ref[...].astype(F32)).astype(BF16)

    blk = pl.BlockSpec((SEQ, LANE_BLOCK), lambda b, j: (b, j))
    return pl.pallas_call(
        body, grid=(batch, D_FF // LANE_BLOCK),
        in_specs=[blk, blk, pl.BlockSpec((3, LANE_BLOCK), lambda b, j: (0, j)),
                  pl.BlockSpec((1, LANE_BLOCK), lambda b, j: (0, j))],
        out_specs=blk,
        out_shape=jax.ShapeDtypeStruct((m, D_FF), BF16),
        compiler_params=_params(("parallel", "parallel")), name="convgate_fwd",
    )(gate, up, conv_w, conv_b)


def _convgate_bwd(gate, up, dact, conv_w, conv_b, batch):
    m = gate.shape[0]

    def body(g_ref, u_ref, da_ref, w_ref, b_ref, dg_ref, du_ref, dw_ref, db_ref):
        @pl.when(pl.program_id(1) == 0)
        def _():
            dw_ref[...] = jnp.zeros_like(dw_ref)
            db_ref[...] = jnp.zeros_like(db_ref)

        g = g_ref[...].astype(F32)
        w = w_ref[...]
        row = _row_index()
        g1 = _shift_down(g, 1, row)
        g2 = _shift_down(g, 2, row)
        c = b_ref[...] + w[0:1] * g2 + w[1:2] * g1 + w[2:3] * g
        gg, dgg = _gelu_and_grad(c)
        da = da_ref[...].astype(F32)
        du_ref[...] = (da * gg).astype(BF16)
        dc = da * u_ref[...].astype(F32) * dgg
        db_ref[...] += jnp.sum(dc, axis=0, keepdims=True)
        dw_ref[0:1, :] += jnp.sum(dc * g2, axis=0, keepdims=True)
        dw_ref[1:2, :] += jnp.sum(dc * g1, axis=0, keepdims=True)
        dw_ref[2:3, :] += jnp.sum(dc * g, axis=0, keepdims=True)
        dg_ref[...] = (w[2:3] * dc + w[1:2] * _shift_up(dc, 1, row) + w[0:1] * _shift_up(dc, 2, row)).astype(BF16)

    blk = pl.BlockSpec((SEQ, LANE_BLOCK), lambda j, b: (b, j))
    wspec = pl.BlockSpec((3, LANE_BLOCK), lambda j, b: (0, j))
    bspec = pl.BlockSpec((1, LANE_BLOCK), lambda j, b: (0, j))
    return pl.pallas_call(
        body, grid=(D_FF // LANE_BLOCK, batch),
        in_specs=[blk, blk, blk, wspec, bspec],
        out_specs=[blk, blk, wspec, bspec],
        out_shape=[jax.ShapeDtypeStruct((m, D_FF), BF16), jax.ShapeDtypeStruct((m, D_FF), BF16),
                   jax.ShapeDtypeStruct((3, D_FF), F32), jax.ShapeDtypeStruct((1, D_FF), F32)],
        compiler_params=_params(("parallel", "arbitrary")), name="convgate_bwd",
    )(gate, up, dact, conv_w, conv_b)


def _gather_weights(shards, conv_w_shard, rel_bias, buckets_np):
    nt = len(shards)
    shapes = [sh.shape for sh in shards]
    ts = list(range(nt))
    tables = _bias_tables_body(buckets_np)

    def body(*refs):
        shard_refs = refs[:nt]
        cw_ref, rb_ref, bk_ref = refs[nt:nt + 3]
        out_refs = refs[nt + 3:2 * nt + 3]
        cw_out, bias_ref, bias_t_ref = refs[2 * nt + 3:2 * nt + 6]
        scratch = refs[2 * nt + 6:]
        f32_refs, bf16_refs = scratch[:nt], scratch[nt:2 * nt]
        load_sems, store_sems, send_sems, recv_sems, cw_send, cw_recv = scratch[2 * nt:]
        plan = _RelayGatherPlan(shapes[:1], bf16_refs[:1], out_refs[:1], send_sems, recv_sems)
        x, y, c, chips = _mesh_pos()
        loads = [pltpu.make_async_copy(shard_refs[t], f32_refs[t], load_sems.at[t]) for t in ts]
        stores = [pltpu.make_async_copy(bf16_refs[t], out_refs[t].at[2 * x + y], store_sems.at[t]) for t in ts]
        stores.append(pltpu.make_async_copy(cw_ref, cw_out.at[2 * x + y], store_sems.at[nt]))

        def cw_copy(j, src, dst, chip):
            return pltpu.make_async_remote_copy(src_ref=src, dst_ref=dst, send_sem=cw_send.at[j],
                                                recv_sem=cw_recv.at[j], device_id=(*chip, c), device_id_type=MESH)

        def to_bf16(t):
            loads[t].wait()
            bf16_refs[t][...] = f32_refs[t][...].astype(BF16)
            stores[t].start()

        for cp in loads:
            cp.start()
        to_bf16(0)
        plan.start([0])
        cw_sends = [cw_copy(j, cw_ref, cw_out.at[2 * x + y], chip) for j, chip in enumerate(chips)]
        for cp in cw_sends + stores[nt:]:
            cp.start()
        for t in ts[1:]:
            to_bf16(t)
        plan.relay([0])
        tables(rb_ref, bk_ref, bias_ref, bias_t_ref)
        plan.forward([0])
        for j, chip in enumerate(chips):
            dst = cw_out.at[2 * chip[0] + chip[1]]
            cw_copy(j, dst, dst, chip).wait_recv()
        plan.finish([0])
        for cp in cw_sends:
            cp.wait_send()
        for cp in stores:
            cp.wait()

    out_shape = [jax.ShapeDtypeStruct((N_SHARD,) + sh.shape, BF16) for sh in shards]
    out_shape.append(jax.ShapeDtypeStruct((N_SHARD,) + conv_w_shard.shape, conv_w_shard.dtype))
    out_shape += [jax.ShapeDtypeStruct((N_CFG, B_HEADS, Q_BLOCK, 2 * Q_BLOCK), F32),
                  jax.ShapeDtypeStruct((N_CFG, B_HEADS, 2 * Q_BLOCK, Q_BLOCK), F32)]
    vm = pl.BlockSpec(memory_space=pltpu.VMEM)
    res = pl.pallas_call(
        body, in_specs=[ANY] * (nt + 1) + [pl.BlockSpec(memory_space=pltpu.SMEM), vm],
        out_specs=[ANY] * (nt + 1) + [vm, vm], out_shape=out_shape,
        scratch_shapes=[pltpu.VMEM(sh.shape, F32) for sh in shards] + [pltpu.VMEM(sh.shape, BF16) for sh in shards]
        + [pltpu.SemaphoreType.DMA((nt,)), pltpu.SemaphoreType.DMA((nt + 1,))] + _sem_pair(6) + _sem_pair(3),
        compiler_params=pltpu.CompilerParams(has_side_effects=True, vmem_limit_bytes=VMEM_LIMIT),
        name="gather_weights",
    )(*shards, conv_w_shard, rel_bias.T, jnp.asarray(buckets_np))
    return list(res[:nt + 1]), res[nt + 1], res[nt + 2]


def _turn(t, u, s, last):
    return jnp.where(t == u, s, jnp.where(t > u, last, 0))


def _add_halves(gs, recvs, c_idx):
    n = len(gs)
    dims = [(g.shape[1] // 2, g.shape[2]) for g in gs]

    def body(c_ref, *refs):
        t = pl.program_id(0)
        for u in range(n):
            @pl.when(t == u)
            def _(u=u):
                refs[2 * n + u][...] = (refs[u][...].astype(F32) + refs[n + u][...].astype(F32)).astype(BF16)

    def own(u):
        return pl.BlockSpec((None, None) + dims[u], lambda t, s, c: (_turn(t, u, s, N_SHARD - 1), c[0], 0, 0))

    def plain(u):
        return pl.BlockSpec((None,) + dims[u], lambda t, s, c: (_turn(t, u, s, N_SHARD - 1), 0, 0))

    return pl.pallas_call(
        body,
        grid_spec=pltpu.PrefetchScalarGridSpec(
            num_scalar_prefetch=1, grid=(n, N_SHARD),
            in_specs=[own(u) for u in range(n)] + [plain(u) for u in range(n)],
            out_specs=[plain(u) for u in range(n)]),
        out_shape=[jax.ShapeDtypeStruct((N_SHARD,) + d, BF16) for d in dims],
        compiler_params=_params(("arbitrary", "arbitrary")), name="rs_add_halves",
    )(c_idx, *[g.reshape((N_SHARD, 2) + d) for g, d in zip(gs, dims)], *recvs)


def _add_chips(parts, recvs, s_idx, c_idx):
    n = len(parts)
    dims = [p.shape[1:] for p in parts]

    def body(idx_ref, *refs):
        t = pl.program_id(0)
        for u in range(n):
            @pl.when(t == u)
            def _(u=u):
                acc = refs[u][...].astype(F32)
                for j in range(3):
                    acc = acc + refs[n + u][j].astype(F32)
                refs[2 * n + u][...] = acc

    res = pl.pallas_call(
        body,
        grid_spec=pltpu.PrefetchScalarGridSpec(
            num_scalar_prefetch=1, grid=(n,),
            in_specs=[pl.BlockSpec((None,) + d, lambda t, idx: (idx[0], 0, 0)) for d in dims]
            + [pl.BlockSpec((3,) + d, lambda t, idx: (0, 0, 0)) for d in dims],
            out_specs=[pl.BlockSpec((None,) + d, lambda t, idx: (idx[1], 0, 0)) for d in dims]),
        out_shape=[jax.ShapeDtypeStruct((2,) + d, F32) for d in dims],
        compiler_params=_params(("arbitrary",)), name="rs_add_chips",
    )(jnp.concatenate([s_idx, c_idx]), *parts, *recvs)
    return [r.reshape(2 * d[0], d[1]) for r, d in zip(res, dims)]


def _finish_reductions(fulls, arrays):
    nt, n = len(fulls), len(arrays)

    def body(*refs):
        in_refs = refs[nt:nt + n]
        full_refs, out_refs = refs[nt + n:2 * nt + n], refs[2 * nt + n:2 * nt + 2 * n]
        pos = 2 * nt + 2 * n
        share_send, share_recv = refs[pos + 2 * n:pos + 2 * n + 2]
        allreduce = _SmallAllReducePlan(in_refs, out_refs, refs[pos:pos + n], refs[pos + n:pos + 2 * n],
                                        *refs[pos + 2 * n + 2:])
        x, y, c, _ = _mesh_pos()

        def half(t, which):
            rows = fulls[t].shape[0] // 2
            return full_refs[t].at[pl.ds(which * rows, rows), :]

        def share(t, which):
            return pltpu.make_async_remote_copy(
                src_ref=half(t, which), dst_ref=half(t, which), send_sem=share_send.at[t],
                recv_sem=share_recv.at[t], device_id=(x, y, 1 - c), device_id_type=MESH)

        for t in range(nt):
            share(t, c).start()
        allreduce.start_sibling()
        allreduce.sum_sibling_and_start_chips()
        allreduce.finish()
        for t in range(nt):
            share(t, 1 - c).wait_recv()
        for t in range(nt):
            share(t, c).wait_send()

    vm = pl.BlockSpec(memory_space=pltpu.VMEM)
    res = pl.pallas_call(
        body, in_specs=[ANY] * nt + [vm] * n, out_specs=[ANY] * nt + [vm] * n,
        out_shape=[jax.ShapeDtypeStruct(f.shape, f.dtype) for f in fulls]
        + [jax.ShapeDtypeStruct(a.shape, F32) for a in arrays],
        input_output_aliases={t: t for t in range(nt)},
        scratch_shapes=[pltpu.VMEM(a.shape, F32) for a in arrays] + [pltpu.VMEM((3,) + a.shape, F32) for a in arrays]
        + _sem_pair(nt) + _sem_pair(4 * n),
        compiler_params=pltpu.CompilerParams(has_side_effects=True),
        name="finish_reductions",
    )(*fulls, *arrays)
    return list(res[:nt]), list(res[nt:])


def _from_col_shards(g):
    n, rows, cols = g.shape
    return g.transpose(1, 0, 2).reshape(rows, n * cols)


def _train_step(x, tgt, g1, g2, g3, g4, shards, ln_g, ln_b, w_s, b_s, rel_bias, conv_w_shard, conv_b, batch,
                s_idx, c_idx):
    buckets = _bucket_tables()
    bz = jnp.repeat(b_s.T, HEAD_DIM, axis=1)
    w_st = jnp.swapaxes(w_s, 1, 2)

    def shard_major(g):
        return g.reshape(N_SHARD, g.shape[0] // N_SHARD, D_MODEL)

    names = ["w_in", "w_out", "w_gate", "w_up", "w_down"]
    (g_in, g_out, g_gate, g_up, g_down, g_convw), bias, bias_t = _gather_weights(
        [shards[n] for n in names], conv_w_shard, rel_bias, buckets)
    w_in_t = g_in.reshape(IN_COLS, D_MODEL)
    conv_w = _from_col_shards(g_convw.reshape(N_SHARD, 3, SHARD_FF))

    h1, uv, qkv, a = _proj_fwd(x, g1, w_in_t, ln_g, ln_b, w_s, bz)
    o_bf, lse, (g_out, g_gate, g_up) = _attn_fwd(qkv, bias, batch, [g_out, g_gate, g_up])
    w_out = g_out.reshape(D_MODEL, D_MODEL)
    w_gate_t = g_gate.reshape(D_FF, D_MODEL)
    w_up_t = g_up.reshape(D_FF, D_MODEL)
    (y1, x1, h2), _ = _fused_rows(
        "out_proj_mid_fwd", 512,
        [(a, w_out, "nn", slice(0, A_WIDTH)), (o_bf, w_out, "nn", slice(A_WIDTH, D_MODEL))],
        [x], [g2, g3], _mid_fwd_rows, [F32, F32, BF16], [])
    gate, up, g_down = _mm_pair_nt(h2, w_gate_t, w_up_t, g_down, tm=1024, tn=1408, out_dtype=BF16,
                                   name="mm_gate_up")
    w_down = g_down.reshape(D_FF, D_MODEL)
    act = _convgate_fwd(gate, up, conv_w, conv_b, batch)
    (dx2, dy2, dg4, loss), _ = _fused_rows(
        "down_proj_loss_head", 512, [(act, w_down, "nn", None)], [x1, tgt], [g4], _loss_head_rows,
        [F32, BF16], [(1, D_MODEL), (1, 128)])

    dact = _mm(dy2, w_down, dims="nt", tm=1024, tn=1408, tk=1024, out_dtype=BF16, name="mm_dact")
    dw_down = _mm(act, dy2, dims="tn", tm=1408, tn=1024, tk=1024, out_dtype=BF16, name="mm_dw_down")
    dgate, dup, dconv_w, dconv_b = _convgate_bwd(gate, up, dact, conv_w, conv_b, batch)
    (dx1, dy1, dg2, dg3), _ = _fused_rows(
        "dh2_mid_bwd", 256, [(dgate, w_gate_t, "nn", None), (dup, w_up_t, "nn", None)],
        [x1, y1, dx2], [g2, g3], _mid_bwd_rows, [F32, BF16], [(1, D_MODEL), (1, D_MODEL)])
    dw_gate_t, dw_up_t = _mm_pair_tn(dgate, dup, h2, tm=1408, tk=1024, name="mm_dw_gate_up")
    done = [shard_major(g) for g in (dw_down, dw_gate_t, dw_up_t)]
    (dmix, dw_out), recv_a = _out_proj_bwd(a, o_bf, dy1, w_out, done[:2])
    done.append(shard_major(dw_out))
    (dproj, dln_g, dln_b, dw_s, dbz), recv_b = _gate_bwd(uv, dmix, ln_g, ln_b, w_s, w_st, bz, done[2:])
    recv_a += recv_b
    parts = _add_halves(done, recv_a, c_idx)
    early = dict(loss=loss, norm_mix_post=dg2, norm_ffn_pre=dg3, norm_ffn_post=dg4, ln_v_gain=dln_g,
                 ln_v_bias=dln_b, spatial_w=dw_s, spatial_b=dbz, conv_w=dconv_w, conv_b=dconv_b)
    dproj, ds, recv, early_sums = _attn_bwd(qkv, dmix, o_bf, lse, bias_t, dproj, batch, parts, list(early.values()))
    fulls = _add_chips(parts, recv, s_idx, c_idx)
    dw_in_t = _mm(dproj, h1, dims="tn", tm=1408, tn=1024, tk=1024, out_dtype=BF16, name="mm_dw_in")
    last = [shard_major(dw_in_t)]
    drel, recv_in_a = _rel_bias_grad(ds, np.ascontiguousarray(np.swapaxes(buckets, 1, 2)), last)
    part_in = _add_halves(last, recv_in_a, c_idx)
    (dx0, dg1), recv_in = _fused_rows(
        "dh1_in_bwd", 512, [(dproj, w_in_t, "nn", None)], [x, dx1], [g1], _in_bwd_rows,
        [F32], [(1, D_MODEL)], exchange=part_in)
    fulls += _add_chips(part_in, recv_in, s_idx, c_idx)
    half_reduced = dict(zip(["w_down", "w_gate", "w_up", "w_out", "w_in"], fulls))

    return dx0, dict(zip(early, early_sums)), dict(norm_mix_pre=dg1, rel_bias=drel), half_reduced


def _adamw_update(w, g, m, v):
    nm = ADAM_B1 * m + (1.0 - ADAM_B1) * g
    nv = ADAM_B2 * v + (1.0 - ADAM_B2) * (g * g)
    m_hat = nm / (1.0 - ADAM_B1 ** ADAM_STEP)
    v_hat = nv / (1.0 - ADAM_B2 ** ADAM_STEP)
    return -ADAM_LR * (m_hat / (jnp.sqrt(v_hat) + ADAM_EPS) + ADAM_WD * w), nm, nv


def _adamw(w, g, m, v, name):
    rows, cols = w.shape
    tr = next(cand for cand in (352, 256, 128) if rows % cand == 0)

    def body(w_ref, g_ref, m_ref, v_ref, go_ref, d_ref, nm_ref, nv_ref):
        gv = g_ref[...]
        go_ref[...] = gv
        d_ref[...], nm_ref[...], nv_ref[...] = _adamw_update(w_ref[...], gv, m_ref[...], v_ref[...])

    spec = pl.BlockSpec((tr, cols), lambda i: (i, 0))
    sds = jax.ShapeDtypeStruct((rows, cols), F32)
    return pl.pallas_call(
        body, grid=(rows // tr,), in_specs=[spec] * 4, out_specs=[spec] * 4, out_shape=[sds] * 4,
        compiler_params=_params(("parallel",)), name=name,
    )(w, g, m, v)


def _adamw_small(ws, gs, ms, vs):
    n = len(ws)

    def body(*refs):
        w_refs, g_refs, m_refs, v_refs = refs[:n], refs[n:2 * n], refs[2 * n:3 * n], refs[3 * n:4 * n]
        d_refs, nm_refs, nv_refs, go_refs = refs[4 * n:5 * n], refs[5 * n:6 * n], refs[6 * n:7 * n], refs[7 * n:]
        for t in range(n):
            g = g_refs[t][...]
            d_refs[t][...], nm_refs[t][...], nv_refs[t][...] = _adamw_update(
                w_refs[t][...], g, m_refs[t][...], v_refs[t][...])
            go_refs[t][...] = g

    vm = pl.BlockSpec(memory_space=pltpu.VMEM)
    sds = [jax.ShapeDtypeStruct(w.shape, F32) for w in ws]
    res = pl.pallas_call(
        body, in_specs=[vm] * (4 * n), out_specs=[vm] * (4 * n), out_shape=sds * 4,
        compiler_params=_params(), name="adamw_small",
    )(*ws, *gs, *ms, *vs)
    return res[:n], res[n:2 * n], res[2 * n:3 * n], res[3 * n:]


SMALL = ["norm_mix_pre", "norm_mix_post", "norm_ffn_pre", "norm_ffn_post", "ln_v_gain", "ln_v_bias",
         "spatial_w", "spatial_b", "rel_bias", "conv_b"]
LARGE = ["w_in", "w_gate", "w_up", "w_down", "w_out"]
TRANSPOSED = ("w_in", "w_gate", "w_up")
ORDER = ["norm_mix_pre", "norm_mix_post", "norm_ffn_pre", "norm_ffn_post", "w_in", "ln_v_gain", "ln_v_bias",
         "spatial_w", "spatial_b", "rel_bias", "w_out", "w_gate", "w_up", "conv_w", "conv_b", "w_down"]


def kernel(x, norm_mix_pre, norm_mix_post, norm_ffn_pre, norm_ffn_post, w_in, ln_v_gain, ln_v_bias, spatial_w, spatial_b, rel_bias, w_out, w_gate, w_up, conv_w, conv_b, w_down, loss_target, m_norm_mix_pre, m_norm_mix_post, m_norm_ffn_pre, m_norm_ffn_post, m_w_in, m_ln_v_gain, m_ln_v_bias, m_spatial_w, m_spatial_b, m_rel_bias, m_w_out, m_w_gate, m_w_up, m_conv_w, m_conv_b, m_w_down, v_norm_mix_pre, v_norm_mix_post, v_norm_ffn_pre, v_norm_ffn_post, v_w_in, v_ln_v_gain, v_ln_v_bias, v_spatial_w, v_spatial_b, v_rel_bias, v_w_out, v_w_gate, v_w_up, v_conv_w, v_conv_b, v_w_down):
    params = dict(norm_mix_pre=norm_mix_pre, norm_mix_post=norm_mix_post, norm_ffn_pre=norm_ffn_pre,
                  norm_ffn_post=norm_ffn_post, w_in=w_in, ln_v_gain=ln_v_gain, ln_v_bias=ln_v_bias,
                  spatial_w=spatial_w, spatial_b=spatial_b, rel_bias=rel_bias, w_out=w_out, w_gate=w_gate,
                  w_up=w_up, conv_w=conv_w, conv_b=conv_b, w_down=w_down)
    mom = dict(norm_mix_pre=m_norm_mix_pre, norm_mix_post=m_norm_mix_post, norm_ffn_pre=m_norm_ffn_pre,
               norm_ffn_post=m_norm_ffn_post, w_in=m_w_in, ln_v_gain=m_ln_v_gain, ln_v_bias=m_ln_v_bias,
               spatial_w=m_spatial_w, spatial_b=m_spatial_b, rel_bias=m_rel_bias, w_out=m_w_out, w_gate=m_w_gate,
               w_up=m_w_up, conv_w=m_conv_w, conv_b=m_conv_b, w_down=m_w_down)
    var = dict(norm_mix_pre=v_norm_mix_pre, norm_mix_post=v_norm_mix_post, norm_ffn_pre=v_norm_ffn_pre,
               norm_ffn_post=v_norm_ffn_post, w_in=v_w_in, ln_v_gain=v_ln_v_gain, ln_v_bias=v_ln_v_bias,
               spatial_w=v_spatial_w, spatial_b=v_spatial_b, rel_bias=v_rel_bias, w_out=v_w_out, w_gate=v_w_gate,
               w_up=v_w_up, conv_w=v_conv_w, conv_b=v_conv_b, w_down=v_w_down)

    batch = x.shape[0]
    xi, yi, ci = lax.axis_index("x"), lax.axis_index("y"), lax.axis_index("c")
    s_idx = (2 * xi + yi).astype(jnp.int32).reshape(1)
    c_idx = ci.astype(jnp.int32).reshape(1)

    def local(a, n):
        return jnp.swapaxes(a[0], 0, 1) if n in TRANSPOSED else a[0]

    shards = {n: local(params[n], n) for n in LARGE}
    dx0, total, partial, half_reduced = _train_step(
        x.reshape(batch * SEQ, D_MODEL), loss_target.reshape(batch * SEQ, D_MODEL),
        norm_mix_pre, norm_mix_post, norm_ffn_pre, norm_ffn_post, shards,
        ln_v_gain.reshape(1, A_WIDTH), ln_v_bias.reshape(1, A_WIDTH), spatial_w[0], spatial_b[0], rel_bias,
        jnp.swapaxes(conv_w, 0, 1), conv_b, batch, s_idx, c_idx)
    grad_x = dx0.reshape(batch, SEQ, D_MODEL)

    names = list(partial)
    fulls, sums = _finish_reductions([half_reduced[n] for n in LARGE], [partial[n] for n in names])
    reduced = dict(zip(LARGE, fulls))
    total.update(zip(names, sums))
    loss = total["loss"][0, 0]
    total["spatial_b"] = total["spatial_b"][:, ::HEAD_DIM].T
    total["rel_bias"] = total["rel_bias"].reshape(B_HEADS, NUM_BUCKETS)
    total["conv_w"] = lax.dynamic_slice_in_dim(total["conv_w"], s_idx[0] * SHARD_FF, SHARD_FF, axis=1)
    small_names = SMALL + ["conv_w"]

    def small(a, n):
        return jnp.swapaxes(a, 0, 1) if n in ("rel_bias", "conv_w") else a

    for n in small_names:
        reduced[n] = total[n].reshape(small(params[n], n).shape)

    out_g, out_d, out_m, out_v = {}, {}, {}, {}
    for n in LARGE:
        res = _adamw(local(params[n], n), reduced[n], local(mom[n], n), local(var[n], n), name=f"adamw_{n}")
        if n in TRANSPOSED:
            res = [jnp.swapaxes(r, 0, 1) for r in res]
        out_g[n], out_d[n], out_m[n], out_v[n] = [r[None] for r in res]
    d, nm, nv, gg = _adamw_small([small(params[n], n) for n in small_names], [reduced[n] for n in small_names],
                                 [small(mom[n], n) for n in small_names], [small(var[n], n) for n in small_names])
    for n, g, dd, mm, vv in zip(small_names, gg, d, nm, nv):
        out_g[n], out_d[n], out_m[n], out_v[n] = [small(r, n) for r in (g, dd, mm, vv)]

    return (loss, grad_x, *[out_g[n] for n in ORDER], *[out_d[n] for n in ORDER],
            *[out_m[n] for n in ORDER], *[out_v[n] for n in ORDER])
```

```python
import functools
import math

import numpy as np
import jax
import jax.numpy as jnp
from jax import lax
from jax.experimental import pallas as pl
from jax.experimental.pallas import tpu as pltpu

F32 = jnp.float32
BF16 = jnp.bfloat16
MESH = pl.DeviceIdType.MESH

D_MODEL = 1024
SEQ = 2048
HEAD_DIM = 64
A_GROUPS = 4
A_WIDTH = 256
B_HEADS = 12
B_WIDTH = 768
CHUNK = 128
DILATED = ((128, 1), (512, 4), (2048, 16))
NUM_BUCKETS = 32
MAX_DISTANCE = 2048
D_FF = 2816
IN_COLS = 2816
NORM_EPS = 1e-6
NEG_INF = -1e30
N_SHARD = 4
SHARD_FF = D_FF // N_SHARD
LANE_BLOCK = 256
VMEM_LIMIT = 56 * 1024 * 1024

ADAM_LR = 0.001
ADAM_B1 = 0.9
ADAM_B2 = 0.999
ADAM_EPS = 1e-08
ADAM_WD = 0.01
ADAM_STEP = 10

GELU_C = math.sqrt(2.0 / math.pi)
GELU_A = 0.044715

ANY = pl.BlockSpec(memory_space=pl.ANY)


def _params(sem=None):
    return pltpu.CompilerParams(dimension_semantics=sem, vmem_limit_bytes=VMEM_LIMIT)


def _dot(a, b, precision=None):
    return jnp.dot(a, b, preferred_element_type=F32, precision=precision)


def _dot_nt(a, b, precision=None):
    return lax.dot_general(a, b, (((1,), (1,)), ((), ())), preferred_element_type=F32, precision=precision)


def _dot_tn(a, b):
    return lax.dot_general(a, b, (((0,), (0,)), ((), ())), preferred_element_type=F32)


def _gelu(x):
    t = jnp.tanh(x * (GELU_C + (GELU_C * GELU_A) * (x * x)))
    return (0.5 * x) * (1.0 + t)


def _gelu_and_grad(x):
    x2 = x * x
    u = 1.0 + jnp.tanh(x * (GELU_C + (GELU_C * GELU_A) * x2))
    hx = 0.5 * x
    dg = u * (0.5 + hx * (2.0 - u) * (GELU_C + (3.0 * GELU_C * GELU_A) * x2))
    return hx * u, dg


def _mesh_pos():
    x, y, c = lax.axis_index("x"), lax.axis_index("y"), lax.axis_index("c")
    chips = [(1 - x, y), (x, 1 - y), (1 - x, 1 - y)]
    return x, y, c, chips


class _GatherPlan:
    def __init__(self, shapes, out_refs, send_sems, recv_sems):
        self.shapes, self.out_refs = shapes, out_refs
        self.send_sems, self.recv_sems = send_sems, recv_sems
        self.x, self.y, self.c, self.chips = _mesh_pos()
        self.sib = (self.x, self.y, 1 - self.c)

    def _half(self, t, chip, which):
        rows = self.shapes[t][0] // 2
        return self.out_refs[t].at[2 * chip[0] + chip[1], pl.ds(which * rows, rows), :]

    def _copy(self, k, src, dst, to):
        return pltpu.make_async_remote_copy(src_ref=src, dst_ref=dst, send_sem=self.send_sems.at[k],
                                            recv_sem=self.recv_sems.at[k], device_id=to, device_id_type=MESH)

    def _sends(self, t):
        own = self._half(t, (self.x, self.y), self.c)
        return [self._copy(6 * t + j, own, own, (*chip, self.c)) for j, chip in enumerate(self.chips)]

    def _forwards(self, t):
        return [self._copy(6 * t + 3 + j, self._half(t, chip, self.c), self._half(t, chip, self.c), self.sib)
                for j, chip in enumerate(self.chips)]

    def start(self, ts):
        for t in ts:
            for cp in self._sends(t):
                cp.start()

    def forward(self, ts):
        for t in ts:
            for j, chip in enumerate(self.chips):
                landed = self._half(t, chip, self.c)
                self._copy(6 * t + j, landed, landed, (*chip, self.c)).wait_recv()
            for cp in self._forwards(t):
                cp.start()

    def finish(self, ts):
        for t in ts:
            for j, chip in enumerate(self.chips):
                other = self._half(t, chip, 1 - self.c)
                self._copy(6 * t + 3 + j, other, other, self.sib).wait_recv()
        for t in ts:
            for cp in self._sends(t) + self._forwards(t):
                cp.wait_send()


class _RelayGatherPlan:
    def __init__(self, shapes, shard_refs, out_refs, send_sems, recv_sems):
        self.shapes, self.shard_refs, self.out_refs = shapes, shard_refs, out_refs
        self.send_sems, self.recv_sems = send_sems, recv_sems
        x, y, c, self.chips = _mesh_pos()
        self.me, self.c, self.sib = (x, y), c, (x, y, 1 - c)
        self.first = (x + c - 2 * x * c, y + (1 - c) - 2 * y * (1 - c))
        self.second = (x + (1 - c) - 2 * x * (1 - c), y + c - 2 * y * c)
        self.diag = (1 - x, 1 - y)

    def _half(self, t, chip, which):
        rows = self.shapes[t][0] // 2
        return self.out_refs[t].at[2 * chip[0] + chip[1], pl.ds(which * rows, rows), :]

    def _copy(self, k, src, dst, to):
        return pltpu.make_async_remote_copy(src_ref=src, dst_ref=dst, send_sem=self.send_sems.at[k],
                                            recv_sem=self.recv_sems.at[k], device_id=to, device_id_type=MESH)

    def _own(self, t):
        if self.shard_refs is None:
            return self._half(t, self.me, self.c)
        rows = self.shapes[t][0] // 2
        return self.shard_refs[t].at[pl.ds(self.c * rows, rows), :]

    def _step1(self, t):
        return self._copy(6 * t, self._own(t), self._half(t, self.me, self.c), (*self.first, self.c))

    def _step2(self, t):
        landed = self._half(t, self.first, self.c)
        return [self._copy(6 * t + 1, self._own(t), self._half(t, self.me, self.c), (*self.second, self.c)),
                self._copy(6 * t + 2, landed, landed, (*self.second, self.c))]

    def _forwards(self, t):
        return [self._copy(6 * t + 3 + j, self._half(t, chip, self.c), self._half(t, chip, self.c), self.sib)
                for j, chip in enumerate(self.chips)]

    def start(self, ts):
        for t in ts:
            self._step1(t).start()
            self._step2(t)[0].start()

    def relay(self, ts):
        for t in ts:
            landed = self._half(t, self.first, self.c)
            self._copy(6 * t, landed, landed, self.sib).wait_recv()
            self._step2(t)[1].start()

    def forward(self, ts):
        for t in ts:
            for k, chip in ((1, self.second), (2, self.diag)):
                landed = self._half(t, chip, self.c)
                self._copy(6 * t + k, landed, landed, self.sib).wait_recv()
            for cp in self._forwards(t):
                cp.start()

    def finish(self, ts):
        for t in ts:
            for j, chip in enumerate(self.chips):
                other = self._half(t, chip, 1 - self.c)
                self._copy(6 * t + 3 + j, other, other, self.sib).wait_recv()
        for t in ts:
            for cp in [self._step1(t)] + self._step2(t) + self._forwards(t):
                cp.wait_send()


class _SiblingExchangePlan:
    def __init__(self, shapes, grad_refs, out_refs, send_sems, recv_sems):
        self.shapes, self.grad_refs, self.out_refs = shapes, grad_refs, out_refs
        self.send_sems, self.recv_sems = send_sems, recv_sems
        self.x, self.y, self.c, _ = _mesh_pos()

    def _copies(self):
        out = []
        for t, (g, o) in enumerate(zip(self.grad_refs, self.out_refs)):
            rows = self.shapes[t][1] // 2
            out.append(pltpu.make_async_remote_copy(
                src_ref=g.at[:, pl.ds((1 - self.c) * rows, rows), :], dst_ref=o, send_sem=self.send_sems.at[t],
                recv_sem=self.recv_sems.at[t], device_id=(self.x, self.y, 1 - self.c), device_id_type=MESH))
        return out

    def start(self):
        for cp in self._copies():
            cp.start()

    def finish(self):
        for cp in self._copies():
            cp.wait()


class _SiblingSharePlan:
    def __init__(self, shapes, full_refs, send_sems, recv_sems):
        self.shapes, self.full_refs, self.send_sems, self.recv_sems = shapes, full_refs, send_sems, recv_sems
        self.x, self.y, self.c, _ = _mesh_pos()

    def _copy(self, t, which):
        rows = self.shapes[t][0] // 2
        half = self.full_refs[t].at[pl.ds(which * rows, rows), :]
        return pltpu.make_async_remote_copy(
            src_ref=half, dst_ref=half, send_sem=self.send_sems.at[t], recv_sem=self.recv_sems.at[t],
            device_id=(self.x, self.y, 1 - self.c), device_id_type=MESH)

    def start(self):
        for t in range(len(self.full_refs)):
            self._copy(t, self.c).start()

    def finish(self):
        for t in range(len(self.full_refs)):
            self._copy(t, 1 - self.c).wait_recv()
        for t in range(len(self.full_refs)):
            self._copy(t, self.c).wait_send()


class _ChipExchangePlan:
    def __init__(self, part_refs, out_refs, send_sems, recv_sems):
        self.part_refs, self.out_refs, self.send_sems, self.recv_sems = part_refs, out_refs, send_sems, recv_sems
        _, _, self.c, self.chips = _mesh_pos()

    def _copies(self):
        return [pltpu.make_async_remote_copy(
            src_ref=p.at[2 * chip[0] + chip[1]], dst_ref=o.at[j], send_sem=self.send_sems.at[3 * t + j],
            recv_sem=self.recv_sems.at[3 * t + j], device_id=(*chip, self.c), device_id_type=MESH)
            for t, (p, o) in enumerate(zip(self.part_refs, self.out_refs)) for j, chip in enumerate(self.chips)]

    def start(self):
        for cp in self._copies():
            cp.start()

    def finish(self):
        for cp in self._copies():
            cp.wait()


class _SmallAllReducePlan:
    def __init__(self, in_refs, out_refs, sib_refs, chip_refs, send_sems, recv_sems):
        self.in_refs, self.out_refs, self.sib_refs, self.chip_refs = in_refs, out_refs, sib_refs, chip_refs
        self.send_sems, self.recv_sems = send_sems, recv_sems
        self.n = len(in_refs)
        self.x, self.y, self.c, self.chips = _mesh_pos()

    def _copy(self, k, src, dst, to):
        return pltpu.make_async_remote_copy(src_ref=src, dst_ref=dst, send_sem=self.send_sems.at[k],
                                            recv_sem=self.recv_sems.at[k], device_id=to, device_id_type=MESH)

    def _first(self):
        return [self._copy(t, self.in_refs[t], self.sib_refs[t], (self.x, self.y, 1 - self.c)) for t in range(self.n)]

    def _second(self):
        return [self._copy(self.n + 3 * t + j, self.out_refs[t], self.chip_refs[t].at[j], (*chip, self.c))
                for t in range(self.n) for j, chip in enumerate(self.chips)]

    def start_sibling(self):
        for cp in self._first():
            cp.start()

    def sum_sibling_and_start_chips(self):
        for cp in self._first():
            cp.wait()
        for t in range(self.n):
            self.out_refs[t][...] = self.in_refs[t][...] + self.sib_refs[t][...]
        for cp in self._second():
            cp.start()

    def finish(self):
        for cp in self._second():
            cp.wait()
        for t in range(self.n):
            self.out_refs[t][...] = ((self.out_refs[t][...] + self.chip_refs[t][0])
                                     + (self.chip_refs[t][1] + self.chip_refs[t][2]))

    @staticmethod
    def scratch(arrays):
        return ([pltpu.VMEM(a.shape, F32) for a in arrays] + [pltpu.VMEM((3,) + a.shape, F32) for a in arrays]
                + _sem_pair(4 * len(arrays)))


def _sem_pair(n):
    return [pltpu.SemaphoreType.DMA((n,)), pltpu.SemaphoreType.DMA((n,))]


def _mm(a, b, *, dims, tm, tn, tk, out_dtype, name):
    if dims == "nn":
        m, k = a.shape
        n = b.shape[1]
        a_spec = pl.BlockSpec((tm, tk), lambda i, j, kk: (i, kk))
        b_spec = pl.BlockSpec((tk, tn), lambda i, j, kk: (kk, j))
        dot = _dot
    elif dims == "nt":
        m, k = a.shape
        n = b.shape[0]
        a_spec = pl.BlockSpec((tm, tk), lambda i, j, kk: (i, kk))
        b_spec = pl.BlockSpec((tn, tk), lambda i, j, kk: (j, kk))
        dot = _dot_nt
    else:
        k, m = a.shape
        n = b.shape[1]
        a_spec = pl.BlockSpec((tk, tm), lambda i, j, kk: (kk, i))
        b_spec = pl.BlockSpec((tk, tn), lambda i, j, kk: (kk, j))
        dot = _dot_tn
    assert m % tm == 0 and n % tn == 0 and k % tk == 0, (name, m, n, k)
    grid = (m // tm, n // tn, k // tk)
    nk = grid[2]
    own_acc = nk > 1 and out_dtype != F32

    def body(a_ref, b_ref, o_ref, *scratch):
        prod = dot(a_ref[...].astype(BF16), b_ref[...].astype(BF16))
        if nk == 1:
            o_ref[...] = prod.astype(out_dtype)
        else:
            acc_ref = scratch[0] if own_acc else o_ref
            kk = pl.program_id(2)

            @pl.when(kk == 0)
            def _():
                acc_ref[...] = prod

            @pl.when(kk > 0)
            def _():
                acc_ref[...] += prod

            if own_acc:
                @pl.when(kk == nk - 1)
                def _():
                    o_ref[...] = acc_ref[...].astype(out_dtype)

    return pl.pallas_call(
        body, grid=grid, in_specs=[a_spec, b_spec],
        out_specs=pl.BlockSpec((tm, tn), lambda i, j, kk: (i, j)),
        out_shape=jax.ShapeDtypeStruct((m, n), out_dtype),
        scratch_shapes=[pltpu.VMEM((tm, tn), F32)] if own_acc else [],
        compiler_params=_params(("parallel", "parallel", "arbitrary")), name=name,
    )(a, b)


def _mm_pair_tn(a1, a2, b, *, tm, tk, name):
    k, m = a1.shape
    n = b.shape[1]
    assert m % tm == 0 and k % tk == 0 and a2.shape == a1.shape, name
    nk = k // tk

    def body(a1_ref, a2_ref, b_ref, o1_ref, o2_ref, acc1_ref, acc2_ref):
        bv = b_ref[...]
        p1 = _dot_tn(a1_ref[...], bv)
        p2 = _dot_tn(a2_ref[...], bv)
        kk = pl.program_id(1)

        @pl.when(kk == 0)
        def _():
            acc1_ref[...] = p1
            acc2_ref[...] = p2

        @pl.when(kk > 0)
        def _():
            acc1_ref[...] += p1
            acc2_ref[...] += p2

        @pl.when(kk == nk - 1)
        def _():
            o1_ref[...] = acc1_ref[...].astype(BF16)
            o2_ref[...] = acc2_ref[...].astype(BF16)

    a_spec = pl.BlockSpec((tk, tm), lambda i, kk: (kk, i))
    o_spec = pl.BlockSpec((tm, n), lambda i, kk: (i, 0))
    return pl.pallas_call(
        body, grid=(m // tm, nk),
        in_specs=[a_spec, a_spec, pl.BlockSpec((tk, n), lambda i, kk: (kk, 0))],
        out_specs=[o_spec, o_spec],
        out_shape=[jax.ShapeDtypeStruct((m, n), BF16)] * 2,
        scratch_shapes=[pltpu.VMEM((tm, n), F32)] * 2,
        compiler_params=_params(("parallel", "arbitrary")), name=name,
    )(a1, a2, b)


def _mm_pair_nt(a, w1_t, w2_t, own, *, tm, tn, out_dtype, name):
    m, k = a.shape
    n = w1_t.shape[0]
    assert m % tm == 0 and n % tn == 0 and w2_t.shape == w1_t.shape, name
    grid = (m // tm, n // tn)
    n_steps = grid[0] * grid[1]

    def body(a_ref, w1_ref, w2_ref, own_ref, o1_ref, o2_ref, gat_ref, send_sems, recv_sems):
        del own_ref
        step = pl.program_id(0) * grid[1] + pl.program_id(1)
        gather = _GatherPlan([own.shape[1:]], [gat_ref], send_sems, recv_sems)

        @pl.when(step == 0)
        def _():
            gather.start([0])

        @pl.when(step == (2 * n_steps) // 3)
        def _():
            gather.forward([0])

        av = a_ref[...]
        o1_ref[...] = _dot_nt(av, w1_ref[...]).astype(out_dtype)
        o2_ref[...] = _dot_nt(av, w2_ref[...]).astype(out_dtype)

        @pl.when(step == n_steps - 1)
        def _():
            gather.finish([0])

    w_spec = pl.BlockSpec((tn, k), lambda i, j: (j, 0))
    o_spec = pl.BlockSpec((tm, tn), lambda i, j: (i, j))
    return pl.pallas_call(
        body, grid=grid,
        in_specs=[pl.BlockSpec((tm, k), lambda i, j: (i, 0)), w_spec, w_spec, ANY],
        out_specs=[o_spec, o_spec, ANY],
        out_shape=[jax.ShapeDtypeStruct((m, n), out_dtype)] * 2 + [jax.ShapeDtypeStruct(own.shape, own.dtype)],
        scratch_shapes=_sem_pair(6), input_output_aliases={3: 2},
        compiler_params=_params(("arbitrary", "arbitrary")), name=name,
    )(a, w1_t, w2_t, own)


def _out_proj_bwd(a, o, dy1, w_out, grads):
    m = dy1.shape[0]
    tm = 1024
    nx = len(grads)
    shapes = [g.shape for g in grads]
    n_steps = m // tm

    def body(a_ref, o_ref, dy_ref, w_ref, *rest):
        grad_refs = rest[:nx]
        dmix_ref, dw_ref = rest[nx:nx + 2]
        acc_ref = rest[2 * nx + 2]
        exchange = _SiblingExchangePlan(shapes, grad_refs, rest[nx + 2:2 * nx + 2], *rest[2 * nx + 3:])

        @pl.when(pl.program_id(0) == 0)
        def _():
            exchange.start()

        dy = dy_ref[...]
        dmix_ref[...] = _dot_nt(dy, w_ref[...])
        top = _dot_tn(a_ref[...], dy)
        bottom = _dot_tn(o_ref[...], dy)

        @pl.when(pl.program_id(0) == 0)
        def _():
            acc_ref[:A_WIDTH, :] = top
            acc_ref[A_WIDTH:, :] = bottom

        @pl.when(pl.program_id(0) > 0)
        def _():
            acc_ref[:A_WIDTH, :] += top
            acc_ref[A_WIDTH:, :] += bottom

        @pl.when(pl.program_id(0) == n_steps - 1)
        def _():
            dw_ref[...] = acc_ref[...].astype(BF16)
            exchange.finish()

    tile = lambda width: pl.BlockSpec((tm, width), lambda i: (i, 0))
    res = pl.pallas_call(
        body, grid=(n_steps,),
        in_specs=[tile(A_WIDTH), tile(B_WIDTH), tile(D_MODEL), _full_spec((D_MODEL, D_MODEL))] + [ANY] * nx,
        out_specs=[tile(D_MODEL), _full_spec((D_MODEL, D_MODEL))] + [ANY] * nx,
        out_shape=[jax.ShapeDtypeStruct((m, D_MODEL), F32), jax.ShapeDtypeStruct((D_MODEL, D_MODEL), BF16)]
        + [jax.ShapeDtypeStruct((N_SHARD, s[1] // 2, s[2]), g.dtype) for s, g in zip(shapes, grads)],
        scratch_shapes=[pltpu.VMEM((D_MODEL, D_MODEL), F32)] + _sem_pair(nx),
        compiler_params=_params(("arbitrary",)), name="out_proj_bwd",
    )(a, o, dy1, w_out, *grads)
    return res[:2], list(res[2:])


def _fused_rows(name, tm, mats, rows, vecs, fn, row_outs, acc_outs, exchange=()):
    m = mats[0][0].shape[0]
    nm, nr, nv, nro, nao, nx = len(mats), len(rows), len(vecs), len(row_outs), len(acc_outs), len(exchange)
    n_steps = m // tm

    def body(*refs):
        a_refs, w_refs = refs[:nm], refs[nm:2 * nm]
        pos = 2 * nm
        row_refs, vec_refs, part_refs = refs[pos:pos + nr], refs[pos + nr:pos + nr + nv], refs[pos + nr + nv:pos + nr + nv + nx]
        pos += nr + nv + nx
        out_refs, acc_refs, recv_refs = refs[pos:pos + nro], refs[pos + nro:pos + nro + nao], refs[pos + nro + nao:pos + nro + nao + nx]
        sems = refs[pos + nro + nao + nx:]
        i = pl.program_id(0)
        if nx:
            plan = _ChipExchangePlan(part_refs, recv_refs, *sems)

            @pl.when(i == 0)
            def _():
                plan.start()

        @pl.when(i == 0)
        def _():
            for r in acc_refs:
                r[...] = jnp.zeros_like(r)

        y = None
        for a_ref, w_ref, (_, _, dims, sl) in zip(a_refs, w_refs, mats):
            w = w_ref[...] if sl is None else w_ref[sl, :]
            part = (_dot if dims == "nn" else _dot_nt)(a_ref[...], w)
            y = part if y is None else y + part
        res = fn(y, *[r[...] for r in row_refs], *[v[...] for v in vec_refs])
        for r, val in zip(out_refs, res[:nro]):
            r[...] = val.astype(r.dtype)
        for r, val in zip(acc_refs, res[nro:]):
            r[...] += val

        if nx:
            @pl.when(i == n_steps - 1)
            def _():
                plan.finish()

    tile = lambda width: pl.BlockSpec((tm, width), lambda i: (i, 0))
    res = pl.pallas_call(
        body, grid=(n_steps,),
        in_specs=[tile(a.shape[1]) for a, _, _, _ in mats] + [_full_spec(w.shape) for _, w, _, _ in mats]
        + [tile(D_MODEL)] * nr + [_full_spec((1, D_MODEL))] * nv + [ANY] * nx,
        out_specs=[tile(D_MODEL)] * nro + [_full_spec(s) for s in acc_outs] + [ANY] * nx,
        out_shape=[jax.ShapeDtypeStruct((m, D_MODEL), dt) for dt in row_outs]
        + [jax.ShapeDtypeStruct(s, F32) for s in acc_outs]
        + [jax.ShapeDtypeStruct((3,) + p.shape[1:], p.dtype) for p in exchange],
        scratch_shapes=_sem_pair(3 * nx) if nx else [],
        compiler_params=_params(("arbitrary",)), name=name,
    )(*[a for a, _, _, _ in mats], *[w for _, w, _, _ in mats], *rows, *vecs, *exchange)
    return list(res[:nro + nao]), list(res[nro + nao:])


def _vec_spec(width=D_MODEL):
    return pl.BlockSpec((1, width), lambda i: (0, 0))


def _rstd(v):
    return lax.rsqrt(jnp.mean(v * v, axis=-1, keepdims=True) + NORM_EPS)


def _mid_fwd_rows(y1, x0, g2, g3):
    x1 = x0 + y1 * _rstd(y1) * g2
    return y1, x1, x1 * _rstd(x1) * g3


def _rms_bwd_rows(dout, v, g):
    r = _rstd(v)
    n = v * r
    dn = dout * g
    dv = r * (dn - n * jnp.mean(dn * n, axis=-1, keepdims=True))
    dg = jnp.sum(dout * n, axis=0, keepdims=True)
    return dv, dg


def _loss_head_rows(y2, x1, tgt, g4):
    x2 = x1 + y2 * _rstd(y2) * g4
    err = x2 - tgt
    loss = 0.5 * jnp.sum(jnp.mean(err * err, axis=-1, keepdims=True), axis=0, keepdims=True)
    dx2 = err * (1.0 / D_MODEL)
    dy2, dg4 = _rms_bwd_rows(dx2, y2, g4)
    return dx2, dy2, dg4, loss


def _mid_bwd_rows(dh2, x1, y1, dx2, g2, g3):
    d3, dg3 = _rms_bwd_rows(dh2, x1, g3)
    dx1 = dx2 + d3
    dy1, dg2 = _rms_bwd_rows(dx1, y1, g2)
    return dx1, dy1, dg2, dg3


def _in_bwd_rows(dh1, x0, dx1, g1):
    d1, dg1 = _rms_bwd_rows(dh1, x0, g1)
    return dx1 + d1, dg1


GATE_ROWS = 512


def _group_mean_matrix():
    p = np.zeros((A_WIDTH, A_WIDTH), np.float32)
    for g in range(A_GROUPS):
        p[g * HEAD_DIM:(g + 1) * HEAD_DIM, g * HEAD_DIM:(g + 1) * HEAD_DIM] = 1.0 / HEAD_DIM
    return jnp.asarray(p)


def _group_masks(width=A_WIDTH):
    lane = lax.broadcasted_iota(jnp.int32, (1, width), 1)
    return [(lane >= g * HEAD_DIM) & (lane < (g + 1) * HEAD_DIM) for g in range(width // HEAD_DIM)]


GROUP_SUM_PRECISION = lax.Precision.HIGH


def _layernorm_groups(vg, pavg):
    hi = GROUP_SUM_PRECISION
    mu = _dot(vg, pavg, hi)
    xc = vg - mu
    var = _dot(xc * xc, pavg, hi)
    rstd = lax.rsqrt(var + NORM_EPS)
    return xc * rstd, rstd


def _spatial_mix(w_bf, vn_chunk_bf, masks, bz):
    z = bz
    for g in range(A_GROUPS):
        z = z + jnp.where(masks[g], _dot(w_bf[g], vn_chunk_bf), 0.0)
    return z


def _full_spec(shape):
    return pl.BlockSpec(shape, lambda i: tuple(0 for _ in shape))


def _gate_fwd_rows(u, v, lg, lb, w_ref, bz, pavg, a_ref):
    masks = _group_masks()
    row = lax.broadcasted_iota(jnp.int32, (CHUNK, CHUNK), 0)
    col = lax.broadcasted_iota(jnp.int32, (CHUNK, CHUNK), 1)
    w_bf = [jnp.where(row >= col, w_ref[g], 0.0).astype(BF16) for g in range(A_GROUPS)]
    ug = _gelu(u)
    vhat, _ = _layernorm_groups(_gelu(v), pavg)
    vn = vhat * lg + lb
    for c in range(GATE_ROWS // CHUNK):
        sl = slice(c * CHUNK, (c + 1) * CHUNK)
        z = _spatial_mix(w_bf, vn[sl].astype(BF16), masks, bz)
        a_ref[sl, :] = (ug[sl] * z).astype(BF16)


def _gate_bwd(uv, dmix, ln_g, ln_b, w_s, w_st, bz, grads):
    m = uv.shape[0]
    pavg = _group_mean_matrix()
    nsteps = m // GATE_ROWS
    nx = len(grads)
    shapes = [g.shape for g in grads]

    def body(u_ref, v_ref, da_ref, lg_ref, lb_ref, w_ref, wt_ref, bz_ref, p_ref, *rest):
        grad_refs = rest[:nx]
        duv_ref, dlg_ref, dlb_ref, dw_ref, dbz_ref = rest[nx:nx + 5]
        recv_refs = rest[nx + 5:2 * nx + 5]
        exchange = _SiblingExchangePlan(shapes, grad_refs, recv_refs, *rest[2 * nx + 5:])
        i = pl.program_id(0)

        @pl.when(i == 0)
        def _():
            exchange.start()
            dlg_ref[...] = jnp.zeros_like(dlg_ref)
            dlb_ref[...] = jnp.zeros_like(dlb_ref)
            dw_ref[...] = jnp.zeros_like(dw_ref)
            dbz_ref[...] = jnp.zeros_like(dbz_ref)

        hi = GROUP_SUM_PRECISION
        masks = _group_masks()
        row = lax.broadcasted_iota(jnp.int32, (CHUNK, CHUNK), 0)
        col = lax.broadcasted_iota(jnp.int32, (CHUNK, CHUNK), 1)
        tril = row >= col
        w_bf = [jnp.where(tril, w_ref[g], 0.0).astype(BF16) for g in range(A_GROUPS)]
        wt_bf = [jnp.where(col >= row, wt_ref[g], 0.0).astype(BF16) for g in range(A_GROUPS)]
        pavg_v = p_ref[...]
        lg = lg_ref[...]
        ug, dug = _gelu_and_grad(u_ref[...])
        vg, dvg_dx = _gelu_and_grad(v_ref[...])
        vhat, rstd = _layernorm_groups(vg, pavg_v)
        vn = vhat * lg + lb_ref[...]
        da = da_ref[...]
        bz = bz_ref[...]
        for c in range(GATE_ROWS // CHUNK):
            sl = slice(c * CHUNK, (c + 1) * CHUNK)
            vn_bf = vn[sl].astype(BF16)
            z = _spatial_mix(w_bf, vn_bf, masks, bz)
            dz = da[sl] * ug[sl]
            duv_ref[sl, 0:A_WIDTH] = (da[sl] * z * dug[sl]).astype(BF16)
            dbz_ref[...] += dz
            dz_bf = dz.astype(BF16)
            dvn = jnp.zeros((CHUNK, A_WIDTH), F32)
            for g in range(A_GROUPS):
                dz_g = jnp.where(masks[g], dz, 0.0).astype(BF16)
                dw_ref[g] += jnp.where(tril, _dot_nt(dz_g, vn_bf), 0.0)
                dvn = dvn + jnp.where(masks[g], _dot(wt_bf[g], dz_bf), 0.0)
            vh = vhat[sl]
            dlb_ref[...] += jnp.sum(dvn, axis=0, keepdims=True)
            dlg_ref[...] += jnp.sum(dvn * vh, axis=0, keepdims=True)
            dvh = dvn * lg
            m1 = _dot(dvh, pavg_v, hi)
            m2 = _dot(dvh * vh, pavg_v, hi)
            duv_ref[sl, A_WIDTH:2 * A_WIDTH] = (rstd[sl] * (dvh - m1 - vh * m2) * dvg_dx[sl]).astype(BF16)

        @pl.when(i == nsteps - 1)
        def _():
            dbz_ref[...] = _dot(dbz_ref[...], pavg_v * float(HEAD_DIM), hi)
            exchange.finish()

    res = pl.pallas_call(
        body, grid=(nsteps,),
        in_specs=[pl.BlockSpec((GATE_ROWS, A_WIDTH), lambda i: (i, 0)),
                  pl.BlockSpec((GATE_ROWS, A_WIDTH), lambda i: (i, 1)),
                  pl.BlockSpec((GATE_ROWS, A_WIDTH), lambda i: (i, 0)),
                  _full_spec((1, A_WIDTH)), _full_spec((1, A_WIDTH)), _full_spec((A_GROUPS, CHUNK, CHUNK)),
                  _full_spec((A_GROUPS, CHUNK, CHUNK)), _full_spec((CHUNK, A_WIDTH)),
                  _full_spec((A_WIDTH, A_WIDTH))] + [ANY] * nx,
        out_specs=[pl.BlockSpec((GATE_ROWS, 2 * A_WIDTH), lambda i: (i, 0)),
                   _full_spec((1, A_WIDTH)), _full_spec((1, A_WIDTH)), _full_spec((A_GROUPS, CHUNK, CHUNK)),
                   _full_spec((CHUNK, A_WIDTH))] + [ANY] * nx,
        out_shape=[jax.ShapeDtypeStruct((m, IN_COLS), BF16),
                   jax.ShapeDtypeStruct((1, A_WIDTH), F32), jax.ShapeDtypeStruct((1, A_WIDTH), F32),
                   jax.ShapeDtypeStruct((A_GROUPS, CHUNK, CHUNK), F32),
                   jax.ShapeDtypeStruct((CHUNK, A_WIDTH), F32)]
        + [jax.ShapeDtypeStruct((N_SHARD, s[1] // 2, s[2]), g.dtype) for s, g in zip(shapes, grads)],
        scratch_shapes=_sem_pair(nx),
        compiler_params=_params(("arbitrary",)), name="gate_bwd",
    )(uv, uv, dmix, ln_g, ln_b, w_s, w_st, bz, pavg, *grads)
    return res[:5], list(res[5:])


Q_BLOCK = 128
PAIR = 2 * HEAD_DIM
N_PAIR = B_HEADS // 2
N_CFG = len(DILATED)
BLOCKS_PER_CFG = SEQ // Q_BLOCK
QKV_SLABS = 3 * N_PAIR
FWD_BLOCKS_PER_TRIP = 8
BWD_BLOCKS_PER_TRIP = 4


def _t5_bucket_np(dist, dtype):
    max_exact = NUM_BUCKETS // 2
    d = np.maximum(dist, 1).astype(dtype)
    large = max_exact + (np.log(d / dtype(max_exact)) / dtype(math.log(MAX_DISTANCE / max_exact))
                         * dtype(NUM_BUCKETS - max_exact))
    large = np.minimum(large.astype(np.int32), NUM_BUCKETS - 1)
    return np.where(dist < max_exact, dist, large)


def _bucket_tables():
    i = np.arange(Q_BLOCK)[:, None]
    j = np.arange(Q_BLOCK)[None, :]
    tables = []
    for _, dil in DILATED:
        rel_prev = Q_BLOCK + i - j
        rel_cur = i - j
        rel = np.concatenate([rel_prev, rel_cur], axis=1)
        valid = np.concatenate([rel_prev <= Q_BLOCK, rel_cur >= 0], axis=1)
        dist = np.maximum(rel, 0) * dil
        b32 = _t5_bucket_np(dist, np.float32)
        b64 = _t5_bucket_np(dist, np.float64)
        assert np.array_equal(b32, b64)
        tables.append(np.where(valid, b32, -1).astype(np.int32))
    return np.stack(tables)


def _present_buckets(buckets_np):
    return [sorted(set(int(v) for v in np.unique(buckets_np[c]) if v >= 0)) for c in range(N_CFG)]


def _bias_tables_body(buckets_np):
    present = _present_buckets(buckets_np)

    def tables(rb_ref, bk_ref, o_ref, ot_ref):
        for c in range(N_CFG):
            bk = bk_ref[c]
            for h in range(B_HEADS):
                acc = jnp.full((Q_BLOCK, 2 * Q_BLOCK), NEG_INF, F32)
                for b in present[c]:
                    acc = jnp.where(bk == b, rb_ref[h, b], acc)
                o_ref[c, h] = acc
                ot_ref[c, h] = acc.T

    return tables


def _proj_fwd(x, g1, w_in_t, ln_g, ln_b, w_s, bz):
    m = x.shape[0]
    tm = GATE_ROWS
    pavg = _group_mean_matrix()

    def body(x_ref, g_ref, w_ref, lg_ref, lb_ref, ws_ref, bz_ref, p_ref, h_ref, uv_ref, qkv_ref, a_ref):
        xv = x_ref[...]
        h = (xv * _rstd(xv) * g_ref[...]).astype(BF16)
        h_ref[...] = h
        acc = _dot_nt(h, w_ref[...])
        uv_ref[...] = acc[:, :2 * A_WIDTH]
        for s in range(QKV_SLABS):
            qkv_ref[s] = acc[:, 2 * A_WIDTH + s * PAIR:2 * A_WIDTH + (s + 1) * PAIR]
        _gate_fwd_rows(acc[:, :A_WIDTH], acc[:, A_WIDTH:2 * A_WIDTH], lg_ref[...], lb_ref[...], ws_ref,
                       bz_ref[...], p_ref[...], a_ref)

    return pl.pallas_call(
        body, grid=(m // tm,),
        in_specs=[pl.BlockSpec((tm, D_MODEL), lambda i: (i, 0)), _vec_spec(),
                  pl.BlockSpec((IN_COLS, D_MODEL), lambda i: (0, 0)),
                  _full_spec((1, A_WIDTH)), _full_spec((1, A_WIDTH)), _full_spec((A_GROUPS, CHUNK, CHUNK)),
                  _full_spec((CHUNK, A_WIDTH)), _full_spec((A_WIDTH, A_WIDTH))],
        out_specs=[pl.BlockSpec((tm, D_MODEL), lambda i: (i, 0)),
                   pl.BlockSpec((tm, 2 * A_WIDTH), lambda i: (i, 0)),
                   pl.BlockSpec((QKV_SLABS, tm, PAIR), lambda i: (0, i, 0)),
                   pl.BlockSpec((tm, A_WIDTH), lambda i: (i, 0))],
        out_shape=[jax.ShapeDtypeStruct((m, D_MODEL), BF16), jax.ShapeDtypeStruct((m, 2 * A_WIDTH), F32),
                   jax.ShapeDtypeStruct((QKV_SLABS, m, PAIR), F32), jax.ShapeDtypeStruct((m, A_WIDTH), BF16)],
        compiler_params=_params(("parallel",)), name="proj_fwd",
    )(x, g1, w_in_t, ln_g, ln_b, w_s, bz, pavg)


def _pair_masks():
    lane = lax.broadcasted_iota(jnp.int32, (1, PAIR), 1)
    return [lane < HEAD_DIM, lane >= HEAD_DIM]


def _block_rows(idx, dil):
    static = isinstance(idx, int)
    r, n = idx % dil, idx // dil

    def rows_of(block):
        start = r + (dil * Q_BLOCK) * block
        if dil == 1:
            return pl.ds(start if static else pl.multiple_of(start, Q_BLOCK), Q_BLOCK)
        return pl.ds(start, Q_BLOCK, stride=dil)

    prev = rows_of(n - 1) if not static or n > 0 else None
    return rows_of(n), prev


def _attn_fwd(qkv, bias, batch, owns):
    m = qkv.shape[1]
    comb_rows = 256
    nt = len(owns)
    shapes = [g.shape[1:] for g in owns]
    n_steps = batch * N_PAIR
    ts = list(range(nt))

    def body(q_ref, k_ref, v_ref, b_ref, *rest):
        o_ref, l_ref = rest[nt:nt + 2]
        gat_refs = rest[nt + 2:2 * nt + 2]
        scratch = rest[2 * nt + 2:]
        oc_refs, lc_refs = scratch[:N_CFG], scratch[N_CFG:2 * N_CFG]
        step = pl.program_id(0) * N_PAIR + pl.program_id(1)
        gather = _RelayGatherPlan(shapes, None, gat_refs, *scratch[2 * N_CFG:])

        @pl.when(step == 0)
        def _():
            gather.start(ts)

        @pl.when(step == n_steps // 2)
        def _():
            gather.relay(ts)

        @pl.when(step == n_steps - 2)
        def _():
            gather.forward(ts)

        masks = _pair_masks()
        for ci, (_, dil) in enumerate(DILATED):
            nb = SEQ // dil // Q_BLOCK

            def block(trip, ci=ci, dil=dil, nb=nb):
                work = []
                for u in range(FWD_BLOCKS_PER_TRIP):
                    rows, prow = _block_rows(trip * FWD_BLOCKS_PER_TRIP + u, dil)
                    has_prev = nb > 1 and prow is not None
                    q = q_ref[rows, :] * 0.125
                    kc = k_ref[rows, :].astype(BF16)
                    vc = v_ref[rows, :]
                    kp = k_ref[prow, :].astype(BF16) if has_prev else None
                    vp = v_ref[prow, :] if has_prev else None
                    tiles = []
                    for h in range(2):
                        qh = jnp.where(masks[h], q, 0.0).astype(BF16)
                        sc = _dot_nt(qh, kc) + b_ref[ci, h, :, Q_BLOCK:]
                        sp = _dot_nt(qh, kp) + b_ref[ci, h, :, :Q_BLOCK] if has_prev else None
                        tiles.append((sc, sp))
                    work.append((rows, vc, vp, tiles))
                probs = []
                for _, _, _, tiles in work:
                    ps = []
                    for sc, sp in tiles:
                        mx = jnp.max(sc if sp is None else jnp.maximum(sc, sp), axis=1, keepdims=True)
                        pc = jnp.exp(sc - mx).astype(BF16)
                        pp = None if sp is None else jnp.exp(sp - mx).astype(BF16)
                        ps.append((mx, pc, pp))
                    probs.append(ps)
                for (rows, vc, vp, _), ps in zip(work, probs):
                    res = []
                    for h, (_, pc, pp) in enumerate(ps):
                        r = _dot(pc, jnp.where(masks[h], vc, 1.0).astype(BF16))
                        if pp is not None:
                            r = r + _dot(pp, jnp.where(masks[h], vp, 1.0).astype(BF16))
                        res.append(r)
                    num = jnp.where(masks[0], res[0], res[1])
                    den = pltpu.roll(jnp.where(masks[0], res[1], res[0]), HEAD_DIM, 1)
                    oc_refs[ci][rows, :] = num / den
                    lc_refs[ci][rows, :] = jnp.where(masks[0], ps[0][0], ps[1][0]) + jnp.log(den)

            for trip in range(BLOCKS_PER_CFG // FWD_BLOCKS_PER_TRIP):
                block(trip)

        def combine(i, carry):
            rr = pl.ds(pl.multiple_of(i * comb_rows, comb_rows), comb_rows)
            ls = [lc_refs[c][rr, :] for c in range(N_CFG)]
            mx = functools.reduce(jnp.maximum, ls)
            ws = [jnp.exp(l - mx) for l in ls]
            tot = functools.reduce(lambda a, b: a + b, ws)
            o = functools.reduce(lambda a, b: a + b, [ws[c] * oc_refs[c][rr, :] for c in range(N_CFG)]) / tot
            o_ref[rr, :] = o.astype(BF16)
            l_ref[rr, :] = mx + jnp.log(tot)
            return carry

        lax.fori_loop(0, SEQ // comb_rows, combine, 0)

        @pl.when(step == n_steps - 1)
        def _():
            gather.finish(ts)

    def slab(first):
        return pl.BlockSpec((None, SEQ, PAIR), lambda b, p: (first + p, b, 0))

    nat = pl.BlockSpec((SEQ, PAIR), lambda b, p: (b, p))
    res = pl.pallas_call(
        body, grid=(batch, N_PAIR),
        in_specs=[slab(0), slab(N_PAIR), slab(2 * N_PAIR),
                  pl.BlockSpec((N_CFG, 2, Q_BLOCK, 2 * Q_BLOCK), lambda b, p: (0, p, 0, 0))] + [ANY] * nt,
        out_specs=[nat, nat] + [ANY] * nt,
        out_shape=[jax.ShapeDtypeStruct((m, B_WIDTH), BF16), jax.ShapeDtypeStruct((m, B_WIDTH), F32)]
        + [jax.ShapeDtypeStruct(g.shape, g.dtype) for g in owns],
        scratch_shapes=[pltpu.VMEM((SEQ, PAIR), F32)] * (2 * N_CFG) + _sem_pair(6 * nt),
        input_output_aliases={4 + t: 2 + t for t in range(nt)},
        compiler_params=_params(("arbitrary", "arbitrary")), name="attn_fwd",
    )(qkv, qkv, qkv, bias, *owns)
    return res[0], res[1], list(res[2:])


def _attn_bwd(qkv, dmix, o, lse, bias_t, dproj, batch, parts, smalls):
    m = qkv.shape[1]
    nt, ns = len(parts), len(smalls)
    n_steps = N_PAIR * batch

    def body(q_ref, k_ref, v_ref, do_ref, o_ref, l_ref, b_ref, *rest):
        part_refs = rest[1:nt + 1]
        small_refs = rest[nt + 1:nt + 1 + ns]
        pos = nt + 1 + ns
        dproj_ref, ds_ref = rest[pos:pos + 2]
        recv_refs = rest[pos + 2:pos + 2 + nt]
        sum_refs = rest[pos + 2 + nt:pos + 2 + nt + ns]
        pos += 2 + nt + ns
        dq_acc, dk_acc, dv_acc, d_scr, stage, stage_sems, send_sems, recv_sems = rest[pos:pos + 8]
        allreduce = _SmallAllReducePlan(small_refs, sum_refs, rest[pos + 8:pos + 8 + ns],
                                        rest[pos + 8 + ns:pos + 8 + 2 * ns], *rest[pos + 8 + 2 * ns:])
        pair, seq = pl.program_id(0), pl.program_id(1)
        step = pair * batch + seq
        exchange = _ChipExchangePlan(part_refs, recv_refs, send_sems, recv_sems)

        @pl.when(step == 0)
        def _():
            allreduce.start_sibling()

        @pl.when(step == n_steps // 2)
        def _():
            allreduce.sum_sibling_and_start_chips()

        def stage_copies():
            rows = pl.ds(pl.multiple_of(seq * SEQ, SEQ), SEQ)
            return [pltpu.make_async_copy(
                stage.at[k],
                dproj_ref.at[rows, pl.ds(pl.multiple_of(2 * A_WIDTH + k * B_WIDTH + pair * PAIR, PAIR), PAIR)],
                stage_sems.at[k]) for k in range(3)]

        @pl.when(step == 0)
        def _():
            exchange.start()

        @pl.when(pl.program_id(1) == 0)
        def _():
            ds_ref[...] = jnp.zeros_like(ds_ref)

        dq_acc[...] = jnp.zeros_like(dq_acc)
        dk_acc[...] = jnp.zeros_like(dk_acc)
        dv_acc[...] = jnp.zeros_like(dv_acc)
        d_scr[...] = do_ref[...] * o_ref[...].astype(F32)
        masks = _pair_masks()

        def stack_heads(t):
            return jnp.concatenate([jnp.where(masks[0], t, 0.0), jnp.where(masks[1], t, 0.0)], axis=0).astype(BF16)

        for ci, (_, dil) in enumerate(DILATED):
            nb = SEQ // dil // Q_BLOCK

            def block(trip, carry, ci=ci, dil=dil, nb=nb):
                first = []
                for u in range(BWD_BLOCKS_PER_TRIP):
                    rows, prow = _block_rows(trip * BWD_BLOCKS_PER_TRIP + u, dil)
                    has_prev = nb > 1 and prow is not None
                    if has_prev:
                        kcat = jnp.concatenate([k_ref[prow, :], k_ref[rows, :]], axis=0).astype(BF16)
                        vcat = jnp.concatenate([v_ref[prow, :], v_ref[rows, :]], axis=0).astype(BF16)
                    else:
                        kcat = k_ref[rows, :].astype(BF16)
                        vcat = v_ref[rows, :].astype(BF16)
                    qst = stack_heads(q_ref[rows, :] * 0.125)
                    dost = stack_heads(do_ref[rows, :])
                    lt = l_ref[rows, :].T
                    dt = d_scr[rows, :].T
                    lrow = jnp.concatenate([lt[0:1], lt[HEAD_DIM:HEAD_DIM + 1]], axis=1)
                    drow = jnp.concatenate([jnp.sum(dt[:HEAD_DIM], axis=0, keepdims=True),
                                            jnp.sum(dt[HEAD_DIM:], axis=0, keepdims=True)], axis=1)
                    first.append((has_prev, rows, prow, kcat, qst, dost, lrow, drow,
                                  _dot_nt(kcat, qst), _dot_nt(vcat, dost)))
                second = []
                for has_prev, rows, prow, kcat, qst, dost, lrow, drow, st, dpt in first:
                    keys = slice(0, 2 * Q_BLOCK) if has_prev else slice(Q_BLOCK, 2 * Q_BLOCK)
                    bt = jnp.concatenate([b_ref[ci, 0, keys, :], b_ref[ci, 1, keys, :]], axis=1)
                    pt = jnp.exp(st + bt - lrow)
                    dst = pt * (dpt - drow)
                    ds_ref[ci, 0, keys, :] += dst[:, :Q_BLOCK]
                    ds_ref[ci, 1, keys, :] += dst[:, Q_BLOCK:]
                    second.append((has_prev, rows, prow, kcat, qst, dost, pt.astype(BF16), dst.astype(BF16)))
                for has_prev, rows, prow, kcat, qst, dost, pt_bf, dst_bf in second:
                    dk = _dot(dst_bf, qst)
                    dv = _dot(pt_bf, dost)
                    dq2 = _dot_tn(dst_bf, kcat)
                    dq_acc[rows, :] += jnp.where(masks[0], dq2[:Q_BLOCK], dq2[Q_BLOCK:]) * 0.125
                    if has_prev:
                        dk_acc[prow, :] += dk[:Q_BLOCK]
                        dv_acc[prow, :] += dv[:Q_BLOCK]
                        dk_acc[rows, :] += dk[Q_BLOCK:]
                        dv_acc[rows, :] += dv[Q_BLOCK:]
                    else:
                        dk_acc[rows, :] += dk
                        dv_acc[rows, :] += dv
                return carry

            for trip in range(BLOCKS_PER_CFG // BWD_BLOCKS_PER_TRIP):
                block(trip, 0)

        @pl.when(step > 0)
        def _():
            for cp in stage_copies():
                cp.wait()

        stage[0] = dq_acc[...].astype(BF16)
        stage[1] = dk_acc[...].astype(BF16)
        stage[2] = dv_acc[...].astype(BF16)
        for cp in stage_copies():
            cp.start()

        @pl.when(step == n_steps - 1)
        def _():
            for cp in stage_copies():
                cp.wait()
            exchange.finish()
            allreduce.finish()

    def slab(first):
        return pl.BlockSpec((None, SEQ, PAIR), lambda p, b: (first + p, b, 0))

    nat = pl.BlockSpec((SEQ, PAIR), lambda p, b: (b, p))
    tbl = pl.BlockSpec((N_CFG, 2, 2 * Q_BLOCK, Q_BLOCK), lambda p, b: (0, p, 0, 0))
    acc = pltpu.VMEM((SEQ, PAIR), F32)
    vm = pl.BlockSpec(memory_space=pltpu.VMEM)
    res = pl.pallas_call(
        body, grid=(N_PAIR, batch),
        in_specs=[slab(0), slab(N_PAIR), slab(2 * N_PAIR),
                  pl.BlockSpec((SEQ, PAIR), lambda p, b: (b, A_WIDTH // PAIR + p)), nat, nat, tbl]
        + [ANY] * (nt + 1) + [vm] * ns,
        out_specs=[ANY, tbl] + [ANY] * nt + [vm] * ns,
        out_shape=[jax.ShapeDtypeStruct(dproj.shape, dproj.dtype),
                   jax.ShapeDtypeStruct((N_CFG, B_HEADS, 2 * Q_BLOCK, Q_BLOCK), F32)]
        + [jax.ShapeDtypeStruct((3,) + p.shape[1:], p.dtype) for p in parts]
        + [jax.ShapeDtypeStruct(a.shape, F32) for a in smalls],
        input_output_aliases={7: 0},
        scratch_shapes=[acc, acc, acc, acc, pltpu.VMEM((3, SEQ, PAIR), BF16), pltpu.SemaphoreType.DMA((3,))]
        + _sem_pair(3 * nt) + _SmallAllReducePlan.scratch(smalls),
        compiler_params=_params(("arbitrary", "arbitrary")), name="attn_bwd",
    )(qkv, qkv, qkv, dmix, o, lse, bias_t, dproj, *parts, *smalls)
    return res[0], res[1], list(res[2:2 + nt]), list(res[2 + nt:])


def _rel_bias_grad(ds, buckets_np, grads, fulls):
    present = _present_buckets(buckets_np)
    nx, nf = len(grads), len(fulls)
    shapes = [g.shape for g in grads]

    def body(bk_ref, ds_ref, *rest):
        o_ref = rest[nx + nf]
        recv_refs = rest[nx + nf + 1:2 * nx + nf + 1]
        full_refs = rest[2 * nx + nf + 1:2 * nx + 2 * nf + 1]
        acc_ref = rest[2 * nx + 2 * nf + 1]
        sems = rest[2 * nx + 2 * nf + 2:]
        exchange = _SiblingExchangePlan(shapes, rest[:nx], recv_refs, *sems[:2])
        share = _SiblingSharePlan([f.shape for f in fulls], full_refs, *sems[2:])
        share.start()
        exchange.start()
        acc_ref[...] = jnp.zeros_like(acc_ref)
        for c in range(N_CFG):
            bk = bk_ref[c]
            for h in range(B_HEADS):
                dsv = ds_ref[c, h]
                for b in present[c]:
                    part = jnp.sum(jnp.where(bk == b, dsv, 0.0), axis=0, keepdims=True)
                    acc_ref[pl.ds(h * NUM_BUCKETS + b, 1), :] += part
        o_ref[...] = jnp.sum(acc_ref[...], axis=1, keepdims=True)
        exchange.finish()
        share.finish()

    vm = pl.BlockSpec(memory_space=pltpu.VMEM)
    res = pl.pallas_call(
        body, in_specs=[vm, vm] + [ANY] * (nx + nf), out_specs=[vm] + [ANY] * (nx + nf),
        out_shape=[jax.ShapeDtypeStruct((B_HEADS * NUM_BUCKETS, 1), F32)]
        + [jax.ShapeDtypeStruct((N_SHARD, s[1] // 2, s[2]), g.dtype) for s, g in zip(shapes, grads)]
        + [jax.ShapeDtypeStruct(f.shape, f.dtype) for f in fulls],
        input_output_aliases={2 + nx + t: 1 + nx + t for t in range(nf)},
        scratch_shapes=[pltpu.VMEM((B_HEADS * NUM_BUCKETS, buckets_np.shape[-1]), F32)] + _sem_pair(nx) + _sem_pair(nf),
        compiler_params=_params(), name="rel_bias_grad",
    )(jnp.asarray(buckets_np), ds, *grads, *fulls)
    return res[0], list(res[1:1 + nx]), list(res[1 + nx:])


def _row_index():
    return lax.broadcasted_iota(jnp.int32, (SEQ, LANE_BLOCK), 0)


def _shift_down(x, k, row):
    return jnp.where(row >= k, pltpu.roll(x, k, 0), 0.0)


def _shift_up(x, k, row):
    return jnp.where(row < SEQ - k, pltpu.roll(x, SEQ - k, 0), 0.0)


def _convgate_fwd(gate, up, conv_w, conv_b, batch):
    m = gate.shape[0]

    def body(g_ref, u_ref, w_ref, b_ref, a_ref):
        g = g_ref[...].astype(F32)
        w = w_ref[...]
        row = _row_index()
        c = b_ref[...] + w[0:1] * _shift_down(g, 2, row) + w[1:2] * _shift_down(g, 1, row) + w[2:3] * g
        a_ref[...] = (_gelu(c) * u_ref[...].astype(F32)).astype(BF16)

    blk = pl.BlockSpec((SEQ, LANE_BLOCK), lambda b, j: (b, j))
    return pl.pallas_call(
        body, grid=(batch, D_FF // LANE_BLOCK),
        in_specs=[blk, blk, pl.BlockSpec((3, LANE_BLOCK), lambda b, j: (0, j)),
                  pl.BlockSpec((1, LANE_BLOCK), lambda b, j: (0, j))],
        out_specs=blk,
        out_shape=jax.ShapeDtypeStruct((m, D_FF), BF16),
        compiler_params=_params(("parallel", "parallel")), name="convgate_fwd",
    )(gate, up, conv_w, conv_b)


def _convgate_bwd(gate, up, dact, conv_w, conv_b, batch):
    m = gate.shape[0]

    def body(g_ref, u_ref, da_ref, w_ref, b_ref, dg_ref, du_ref, dw_ref, db_ref):
        @pl.when(pl.program_id(1) == 0)
        def _():
            dw_ref[...] = jnp.zeros_like(dw_ref)
            db_ref[...] = jnp.zeros_like(db_ref)

        g = g_ref[...].astype(F32)
        w = w_ref[...]
        row = _row_index()
        g1 = _shift_down(g, 1, row)
        g2 = _shift_down(g, 2, row)
        c = b_ref[...] + w[0:1] * g2 + w[1:2] * g1 + w[2:3] * g
        gg, dgg = _gelu_and_grad(c)
        da = da_ref[...].astype(F32)
        du_ref[...] = (da * gg).astype(BF16)
        dc = da * u_ref[...].astype(F32) * dgg
        db_ref[...] += jnp.sum(dc, axis=0, keepdims=True)
        dw_ref[0:1, :] += jnp.sum(dc * g2, axis=0, keepdims=True)
        dw_ref[1:2, :] += jnp.sum(dc * g1, axis=0, keepdims=True)
        dw_ref[2:3, :] += jnp.sum(dc * g, axis=0, keepdims=True)
        dg_ref[...] = (w[2:3] * dc + w[1:2] * _shift_up(dc, 1, row) + w[0:1] * _shift_up(dc, 2, row)).astype(BF16)

    blk = pl.BlockSpec((SEQ, LANE_BLOCK), lambda j, b: (b, j))
    wspec = pl.BlockSpec((3, LANE_BLOCK), lambda j, b: (0, j))
    bspec = pl.BlockSpec((1, LANE_BLOCK), lambda j, b: (0, j))
    return pl.pallas_call(
        body, grid=(D_FF // LANE_BLOCK, batch),
        in_specs=[blk, blk, blk, wspec, bspec],
        out_specs=[blk, blk, wspec, bspec],
        out_shape=[jax.ShapeDtypeStruct((m, D_FF), BF16), jax.ShapeDtypeStruct((m, D_FF), BF16),
                   jax.ShapeDtypeStruct((3, D_FF), F32), jax.ShapeDtypeStruct((1, D_FF), F32)],
        compiler_params=_params(("parallel", "arbitrary")), name="convgate_bwd",
    )(gate, up, dact, conv_w, conv_b)


def _gather_weights(shards, conv_w_shard, rel_bias, buckets_np):
    nt = len(shards)
    shapes = [sh.shape for sh in shards]
    ts = list(range(nt))
    tables = _bias_tables_body(buckets_np)

    def body(*refs):
        shard_refs = refs[:nt]
        cw_ref, rb_ref, bk_ref = refs[nt:nt + 3]
        out_refs = refs[nt + 3:2 * nt + 3]
        cw_out, bias_ref, bias_t_ref = refs[2 * nt + 3:2 * nt + 6]
        scratch = refs[2 * nt + 6:]
        f32_refs, bf16_refs = scratch[:nt], scratch[nt:2 * nt]
        load_sems, store_sems, send_sems, recv_sems, cw_send, cw_recv = scratch[2 * nt:]
        plan = _RelayGatherPlan(shapes[:1], bf16_refs[:1], out_refs[:1], send_sems, recv_sems)
        x, y, c, chips = _mesh_pos()
        loads = [pltpu.make_async_copy(shard_refs[t], f32_refs[t], load_sems.at[t]) for t in ts]
        stores = [pltpu.make_async_copy(bf16_refs[t], out_refs[t].at[2 * x + y], store_sems.at[t]) for t in ts]
        stores.append(pltpu.make_async_copy(cw_ref, cw_out.at[2 * x + y], store_sems.at[nt]))

        def cw_copy(j, src, dst, chip):
            return pltpu.make_async_remote_copy(src_ref=src, dst_ref=dst, send_sem=cw_send.at[j],
                                                recv_sem=cw_recv.at[j], device_id=(*chip, c), device_id_type=MESH)

        def to_bf16(t):
            loads[t].wait()
            bf16_refs[t][...] = f32_refs[t][...].astype(BF16)
            stores[t].start()

        for cp in loads:
            cp.start()
        to_bf16(0)
        plan.start([0])
        cw_sends = [cw_copy(j, cw_ref, cw_out.at[2 * x + y], chip) for j, chip in enumerate(chips)]
        for cp in cw_sends + stores[nt:]:
            cp.start()
        for t in ts[1:]:
            to_bf16(t)
        plan.relay([0])
        tables(rb_ref, bk_ref, bias_ref, bias_t_ref)
        plan.forward([0])
        for j, chip in enumerate(chips):
            dst = cw_out.at[2 * chip[0] + chip[1]]
            cw_copy(j, dst, dst, chip).wait_recv()
        plan.finish([0])
        for cp in cw_sends:
            cp.wait_send()
        for cp in stores:
            cp.wait()

    out_shape = [jax.ShapeDtypeStruct((N_SHARD,) + sh.shape, BF16) for sh in shards]
    out_shape.append(jax.ShapeDtypeStruct((N_SHARD,) + conv_w_shard.shape, conv_w_shard.dtype))
    out_shape += [jax.ShapeDtypeStruct((N_CFG, B_HEADS, Q_BLOCK, 2 * Q_BLOCK), F32),
                  jax.ShapeDtypeStruct((N_CFG, B_HEADS, 2 * Q_BLOCK, Q_BLOCK), F32)]
    vm = pl.BlockSpec(memory_space=pltpu.VMEM)
    res = pl.pallas_call(
        body, in_specs=[ANY] * (nt + 1) + [pl.BlockSpec(memory_space=pltpu.SMEM), vm],
        out_specs=[ANY] * (nt + 1) + [vm, vm], out_shape=out_shape,
        scratch_shapes=[pltpu.VMEM(sh.shape, F32) for sh in shards] + [pltpu.VMEM(sh.shape, BF16) for sh in shards]
        + [pltpu.SemaphoreType.DMA((nt,)), pltpu.SemaphoreType.DMA((nt + 1,))] + _sem_pair(6) + _sem_pair(3),
        compiler_params=pltpu.CompilerParams(has_side_effects=True, vmem_limit_bytes=VMEM_LIMIT),
        name="gather_weights",
    )(*shards, conv_w_shard, rel_bias.T, jnp.asarray(buckets_np))
    return list(res[:nt + 1]), res[nt + 1], res[nt + 2]


def _turn(t, u, s, last):
    return jnp.where(t == u, s, jnp.where(t > u, last, 0))


def _add_halves(gs, recvs, c_idx):
    n = len(gs)
    dims = [(g.shape[1] // 2, g.shape[2]) for g in gs]

    def body(c_ref, *refs):
        t = pl.program_id(0)
        for u in range(n):
            @pl.when(t == u)
            def _(u=u):
                refs[2 * n + u][...] = (refs[u][...].astype(F32) + refs[n + u][...].astype(F32)).astype(BF16)

    def own(u):
        return pl.BlockSpec((None, None) + dims[u], lambda t, s, c: (_turn(t, u, s, N_SHARD - 1), c[0], 0, 0))

    def plain(u):
        return pl.BlockSpec((None,) + dims[u], lambda t, s, c: (_turn(t, u, s, N_SHARD - 1), 0, 0))

    return pl.pallas_call(
        body,
        grid_spec=pltpu.PrefetchScalarGridSpec(
            num_scalar_prefetch=1, grid=(n, N_SHARD),
            in_specs=[own(u) for u in range(n)] + [plain(u) for u in range(n)],
            out_specs=[plain(u) for u in range(n)]),
        out_shape=[jax.ShapeDtypeStruct((N_SHARD,) + d, BF16) for d in dims],
        compiler_params=_params(("arbitrary", "arbitrary")), name="rs_add_halves",
    )(c_idx, *[g.reshape((N_SHARD, 2) + d) for g, d in zip(gs, dims)], *recvs)


def _add_chips(parts, recvs, s_idx, c_idx):
    n = len(parts)
    dims = [p.shape[1:] for p in parts]

    def body(idx_ref, *refs):
        t = pl.program_id(0)
        for u in range(n):
            @pl.when(t == u)
            def _(u=u):
                acc = refs[u][...].astype(F32)
                for j in range(3):
                    acc = acc + refs[n + u][j].astype(F32)
                refs[2 * n + u][...] = acc

    res = pl.pallas_call(
        body,
        grid_spec=pltpu.PrefetchScalarGridSpec(
            num_scalar_prefetch=1, grid=(n,),
            in_specs=[pl.BlockSpec((None,) + d, lambda t, idx: (idx[0], 0, 0)) for d in dims]
            + [pl.BlockSpec((3,) + d, lambda t, idx: (0, 0, 0)) for d in dims],
            out_specs=[pl.BlockSpec((None,) + d, lambda t, idx: (idx[1], 0, 0)) for d in dims]),
        out_shape=[jax.ShapeDtypeStruct((2,) + d, F32) for d in dims],
        compiler_params=_params(("arbitrary",)), name="rs_add_chips",
    )(jnp.concatenate([s_idx, c_idx]), *parts, *recvs)
    return [r.reshape(2 * d[0], d[1]) for r, d in zip(res, dims)]


def _finish_reductions(fulls, arrays):
    nt, n = len(fulls), len(arrays)

    def body(*refs):
        in_refs = refs[nt:nt + n]
        full_refs, out_refs = refs[nt + n:2 * nt + n], refs[2 * nt + n:2 * nt + 2 * n]
        pos = 2 * nt + 2 * n
        share_send, share_recv = refs[pos + 2 * n:pos + 2 * n + 2]
        allreduce = _SmallAllReducePlan(in_refs, out_refs, refs[pos:pos + n], refs[pos + n:pos + 2 * n],
                                        *refs[pos + 2 * n + 2:])
        share = _SiblingSharePlan([f.shape for f in fulls], full_refs, share_send, share_recv)
        share.start()
        allreduce.start_sibling()
        allreduce.sum_sibling_and_start_chips()
        allreduce.finish()
        share.finish()

    vm = pl.BlockSpec(memory_space=pltpu.VMEM)
    res = pl.pallas_call(
        body, in_specs=[ANY] * nt + [vm] * n, out_specs=[ANY] * nt + [vm] * n,
        out_shape=[jax.ShapeDtypeStruct(f.shape, f.dtype) for f in fulls]
        + [jax.ShapeDtypeStruct(a.shape, F32) for a in arrays],
        input_output_aliases={t: t for t in range(nt)},
        scratch_shapes=[pltpu.VMEM(a.shape, F32) for a in arrays] + [pltpu.VMEM((3,) + a.shape, F32) for a in arrays]
        + _sem_pair(nt) + _sem_pair(4 * n),
        compiler_params=pltpu.CompilerParams(has_side_effects=True),
        name="finish_reductions",
    )(*fulls, *arrays)
    return list(res[:nt]), list(res[nt:])


def _from_col_shards(g):
    n, rows, cols = g.shape
    return g.transpose(1, 0, 2).reshape(rows, n * cols)


def _train_step(x, tgt, g1, g2, g3, g4, shards, ln_g, ln_b, w_s, b_s, rel_bias, conv_w_shard, conv_b, batch,
                s_idx, c_idx):
    buckets = _bucket_tables()
    bz = jnp.repeat(b_s.T, HEAD_DIM, axis=1)
    w_st = jnp.swapaxes(w_s, 1, 2)

    def shard_major(g):
        return g.reshape(N_SHARD, g.shape[0] // N_SHARD, D_MODEL)

    names = ["w_in", "w_out", "w_gate", "w_up", "w_down"]
    (g_in, g_out, g_gate, g_up, g_down, g_convw), bias, bias_t = _gather_weights(
        [shards[n] for n in names], conv_w_shard, rel_bias, buckets)
    w_in_t = g_in.reshape(IN_COLS, D_MODEL)
    conv_w = _from_col_shards(g_convw.reshape(N_SHARD, 3, SHARD_FF))

    h1, uv, qkv, a = _proj_fwd(x, g1, w_in_t, ln_g, ln_b, w_s, bz)
    o_bf, lse, (g_out, g_gate, g_up) = _attn_fwd(qkv, bias, batch, [g_out, g_gate, g_up])
    w_out = g_out.reshape(D_MODEL, D_MODEL)
    w_gate_t = g_gate.reshape(D_FF, D_MODEL)
    w_up_t = g_up.reshape(D_FF, D_MODEL)
    (y1, x1, h2), _ = _fused_rows(
        "out_proj_mid_fwd", 512,
        [(a, w_out, "nn", slice(0, A_WIDTH)), (o_bf, w_out, "nn", slice(A_WIDTH, D_MODEL))],
        [x], [g2, g3], _mid_fwd_rows, [F32, F32, BF16], [])
    gate, up, g_down = _mm_pair_nt(h2, w_gate_t, w_up_t, g_down, tm=1024, tn=1408, out_dtype=BF16,
                                   name="mm_gate_up")
    w_down = g_down.reshape(D_FF, D_MODEL)
    act = _convgate_fwd(gate, up, conv_w, conv_b, batch)
    (dx2, dy2, dg4, loss), _ = _fused_rows(
        "down_proj_loss_head", 512, [(act, w_down, "nn", None)], [x1, tgt], [g4], _loss_head_rows,
        [F32, BF16], [(1, D_MODEL), (1, 128)])

    dact = _mm(dy2, w_down, dims="nt", tm=1024, tn=1408, tk=1024, out_dtype=BF16, name="mm_dact")
    dw_down = _mm(act, dy2, dims="tn", tm=1408, tn=1024, tk=1024, out_dtype=BF16, name="mm_dw_down")
    dgate, dup, dconv_w, dconv_b = _convgate_bwd(gate, up, dact, conv_w, conv_b, batch)
    (dx1, dy1, dg2, dg3), _ = _fused_rows(
        "dh2_mid_bwd", 256, [(dgate, w_gate_t, "nn", None), (dup, w_up_t, "nn", None)],
        [x1, y1, dx2], [g2, g3], _mid_bwd_rows, [F32, BF16], [(1, D_MODEL), (1, D_MODEL)])
    dw_gate_t, dw_up_t = _mm_pair_tn(dgate, dup, h2, tm=1408, tk=1024, name="mm_dw_gate_up")
    done = [shard_major(g) for g in (dw_down, dw_gate_t, dw_up_t)]
    (dmix, dw_out), recv_a = _out_proj_bwd(a, o_bf, dy1, w_out, done[:2])
    done.append(shard_major(dw_out))
    (dproj, dln_g, dln_b, dw_s, dbz), recv_b = _gate_bwd(uv, dmix, ln_g, ln_b, w_s, w_st, bz, done[2:])
    recv_a += recv_b
    parts = _add_halves(done, recv_a, c_idx)
    early = dict(loss=loss, norm_mix_post=dg2, norm_ffn_pre=dg3, norm_ffn_post=dg4, ln_v_gain=dln_g,
                 ln_v_bias=dln_b, spatial_w=dw_s, spatial_b=dbz, conv_w=dconv_w, conv_b=dconv_b)
    dproj, ds, recv, early_sums = _attn_bwd(qkv, dmix, o_bf, lse, bias_t, dproj, batch, parts, list(early.values()))
    fulls = _add_chips(parts, recv, s_idx, c_idx)
    dw_in_t = _mm(dproj, h1, dims="tn", tm=1408, tn=1024, tk=1024, out_dtype=BF16, name="mm_dw_in")
    last = [shard_major(dw_in_t)]
    drel, recv_in_a, fulls = _rel_bias_grad(ds, np.ascontiguousarray(np.swapaxes(buckets, 1, 2)), last, fulls)
    part_in = _add_halves(last, recv_in_a, c_idx)
    (dx0, dg1), recv_in = _fused_rows(
        "dh1_in_bwd", 512, [(dproj, w_in_t, "nn", None)], [x, dx1], [g1], _in_bwd_rows,
        [F32], [(1, D_MODEL)], exchange=part_in)
    fulls += _add_chips(part_in, recv_in, s_idx, c_idx)
    half_reduced = dict(zip(["w_down", "w_gate", "w_up", "w_out", "w_in"], fulls))

    return dx0, dict(zip(early, early_sums)), dict(norm_mix_pre=dg1, rel_bias=drel), half_reduced


def _adamw_update(w, g, m, v):
    nm = ADAM_B1 * m + (1.0 - ADAM_B1) * g
    nv = ADAM_B2 * v + (1.0 - ADAM_B2) * (g * g)
    m_hat = nm / (1.0 - ADAM_B1 ** ADAM_STEP)
    v_hat = nv / (1.0 - ADAM_B2 ** ADAM_STEP)
    return -ADAM_LR * (m_hat / (jnp.sqrt(v_hat) + ADAM_EPS) + ADAM_WD * w), nm, nv


def _adamw(w, g, m, v, name):
    rows, cols = w.shape
    tr = next(cand for cand in (352, 256, 128) if rows % cand == 0)

    def body(w_ref, g_ref, m_ref, v_ref, go_ref, d_ref, nm_ref, nv_ref):
        gv = g_ref[...]
        go_ref[...] = gv
        d_ref[...], nm_ref[...], nv_ref[...] = _adamw_update(w_ref[...], gv, m_ref[...], v_ref[...])

    spec = pl.BlockSpec((tr, cols), lambda i: (i, 0))
    sds = jax.ShapeDtypeStruct((rows, cols), F32)
    return pl.pallas_call(
        body, grid=(rows // tr,), in_specs=[spec] * 4, out_specs=[spec] * 4, out_shape=[sds] * 4,
        compiler_params=_params(("parallel",)), name=name,
    )(w, g, m, v)


def _adamw_small(ws, gs, ms, vs):
    n = len(ws)

    def body(*refs):
        w_refs, g_refs, m_refs, v_refs = refs[:n], refs[n:2 * n], refs[2 * n:3 * n], refs[3 * n:4 * n]
        d_refs, nm_refs, nv_refs, go_refs = refs[4 * n:5 * n], refs[5 * n:6 * n], refs[6 * n:7 * n], refs[7 * n:]
        for t in range(n):
            g = g_refs[t][...]
            d_refs[t][...], nm_refs[t][...], nv_refs[t][...] = _adamw_update(
                w_refs[t][...], g, m_refs[t][...], v_refs[t][...])
            go_refs[t][...] = g

    vm = pl.BlockSpec(memory_space=pltpu.VMEM)
    sds = [jax.ShapeDtypeStruct(w.shape, F32) for w in ws]
    res = pl.pallas_call(
        body, in_specs=[vm] * (4 * n), out_specs=[vm] * (4 * n), out_shape=sds * 4,
        compiler_params=_params(), name="adamw_small",
    )(*ws, *gs, *ms, *vs)
    return res[:n], res[n:2 * n], res[2 * n:3 * n], res[3 * n:]


SMALL = ["norm_mix_pre", "norm_mix_post", "norm_ffn_pre", "norm_ffn_post", "ln_v_gain", "ln_v_bias",
         "spatial_w", "spatial_b", "rel_bias", "conv_b"]
LARGE = ["w_in", "w_gate", "w_up", "w_down", "w_out"]
TRANSPOSED = ("w_in", "w_gate", "w_up")
ORDER = ["norm_mix_pre", "norm_mix_post", "norm_ffn_pre", "norm_ffn_post", "w_in", "ln_v_gain", "ln_v_bias",
         "spatial_w", "spatial_b", "rel_bias", "w_out", "w_gate", "w_up", "conv_w", "conv_b", "w_down"]


def kernel(x, norm_mix_pre, norm_mix_post, norm_ffn_pre, norm_ffn_post, w_in, ln_v_gain, ln_v_bias, spatial_w, spatial_b, rel_bias, w_out, w_gate, w_up, conv_w, conv_b, w_down, loss_target, m_norm_mix_pre, m_norm_mix_post, m_norm_ffn_pre, m_norm_ffn_post, m_w_in, m_ln_v_gain, m_ln_v_bias, m_spatial_w, m_spatial_b, m_rel_bias, m_w_out, m_w_gate, m_w_up, m_conv_w, m_conv_b, m_w_down, v_norm_mix_pre, v_norm_mix_post, v_norm_ffn_pre, v_norm_ffn_post, v_w_in, v_ln_v_gain, v_ln_v_bias, v_spatial_w, v_spatial_b, v_rel_bias, v_w_out, v_w_gate, v_w_up, v_conv_w, v_conv_b, v_w_down):
    params = dict(norm_mix_pre=norm_mix_pre, norm_mix_post=norm_mix_post, norm_ffn_pre=norm_ffn_pre,
                  norm_ffn_post=norm_ffn_post, w_in=w_in, ln_v_gain=ln_v_gain, ln_v_bias=ln_v_bias,
                  spatial_w=spatial_w, spatial_b=spatial_b, rel_bias=rel_bias, w_out=w_out, w_gate=w_gate,
                  w_up=w_up, conv_w=conv_w, conv_b=conv_b, w_down=w_down)
    mom = dict(norm_mix_pre=m_norm_mix_pre, norm_mix_post=m_norm_mix_post, norm_ffn_pre=m_norm_ffn_pre,
               norm_ffn_post=m_norm_ffn_post, w_in=m_w_in, ln_v_gain=m_ln_v_gain, ln_v_bias=m_ln_v_bias,
               spatial_w=m_spatial_w, spatial_b=m_spatial_b, rel_bias=m_rel_bias, w_out=m_w_out, w_gate=m_w_gate,
               w_up=m_w_up, conv_w=m_conv_w, conv_b=m_conv_b, w_down=m_w_down)
    var = dict(norm_mix_pre=v_norm_mix_pre, norm_mix_post=v_norm_mix_post, norm_ffn_pre=v_norm_ffn_pre,
               norm_ffn_post=v_norm_ffn_post, w_in=v_w_in, ln_v_gain=v_ln_v_gain, ln_v_bias=v_ln_v_bias,
               spatial_w=v_spatial_w, spatial_b=v_spatial_b, rel_bias=v_rel_bias, w_out=v_w_out, w_gate=v_w_gate,
               w_up=v_w_up, conv_w=v_conv_w, conv_b=v_conv_b, w_down=v_w_down)

    batch = x.shape[0]
    xi, yi, ci = lax.axis_index("x"), lax.axis_index("y"), lax.axis_index("c")
    s_idx = (2 * xi + yi).astype(jnp.int32).reshape(1)
    c_idx = ci.astype(jnp.int32).reshape(1)

    def local(a, n):
        return jnp.swapaxes(a[0], 0, 1) if n in TRANSPOSED else a[0]

    shards = {n: local(params[n], n) for n in LARGE}
    dx0, total, partial, half_reduced = _train_step(
        x.reshape(batch * SEQ, D_MODEL), loss_target.reshape(batch * SEQ, D_MODEL),
        norm_mix_pre, norm_mix_post, norm_ffn_pre, norm_ffn_post, shards,
        ln_v_gain.reshape(1, A_WIDTH), ln_v_bias.reshape(1, A_WIDTH), spatial_w[0], spatial_b[0], rel_bias,
        jnp.swapaxes(conv_w, 0, 1), conv_b, batch, s_idx, c_idx)
    grad_x = dx0.reshape(batch, SEQ, D_MODEL)

    names = list(partial)
    (full_in,), sums = _finish_reductions([half_reduced["w_in"]], [partial[n] for n in names])
    reduced = dict(half_reduced, w_in=full_in)
    total.update(zip(names, sums))
    loss = total["loss"][0, 0]
    total["spatial_b"] = total["spatial_b"][:, ::HEAD_DIM].T
    total["rel_bias"] = total["rel_bias"].reshape(B_HEADS, NUM_BUCKETS)
    total["conv_w"] = lax.dynamic_slice_in_dim(total["conv_w"], s_idx[0] * SHARD_FF, SHARD_FF, axis=1)
    small_names = SMALL + ["conv_w"]

    def small(a, n):
        return jnp.swapaxes(a, 0, 1) if n in ("rel_bias", "conv_w") else a

    for n in small_names:
        reduced[n] = total[n].reshape(small(params[n], n).shape)

    out_g, out_d, out_m, out_v = {}, {}, {}, {}
    for n in LARGE:
        res = _adamw(local(params[n], n), reduced[n], local(mom[n], n), local(var[n], n), name=f"adamw_{n}")
        if n in TRANSPOSED:
            res = [jnp.swapaxes(r, 0, 1) for r in res]
        out_g[n], out_d[n], out_m[n], out_v[n] = [r[None] for r in res]
    d, nm, nv, gg = _adamw_small([small(params[n], n) for n in small_names], [reduced[n] for n in small_names],
                                 [small(mom[n], n) for n in small_names], [small(var[n], n) for n in small_names])
    for n, g, dd, mm, vv in zip(small_names, gg, d, nm, nv):
        out_g[n], out_d[n], out_m[n], out_v[n] = [small(r, n) for r in (g, dd, mm, vv)]

    return (loss, grad_x, *[out_g[n] for n in ORDER], *[out_d[n] for n in ORDER],
            *[out_m[n] for n in ORDER], *[out_v[n] for n in ORDER])
```

```python
import functools
import math

import numpy as np
import jax
import jax.numpy as jnp
from jax import lax
from jax.experimental import pallas as pl
from jax.experimental.pallas import tpu as pltpu

F32 = jnp.float32
BF16 = jnp.bfloat16
MESH = pl.DeviceIdType.MESH

D_MODEL = 1024
SEQ = 2048
HEAD_DIM = 64
A_GROUPS = 4
A_WIDTH = 256
B_HEADS = 12
B_WIDTH = 768
CHUNK = 128
DILATED = ((128, 1), (512, 4), (2048, 16))
NUM_BUCKETS = 32
MAX_DISTANCE = 2048
D_FF = 2816
IN_COLS = 2816
NORM_EPS = 1e-6
NEG_INF = -1e30
N_SHARD = 4
SHARD_FF = D_FF // N_SHARD
LANE_BLOCK = 256
VMEM_LIMIT = 56 * 1024 * 1024

ADAM_LR = 0.001
ADAM_B1 = 0.9
ADAM_B2 = 0.999
ADAM_EPS = 1e-08
ADAM_WD = 0.01
ADAM_STEP = 10

GELU_C = math.sqrt(2.0 / math.pi)
GELU_A = 0.044715

ANY = pl.BlockSpec(memory_space=pl.ANY)


def _params(sem=None):
    return pltpu.CompilerParams(dimension_semantics=sem, vmem_limit_bytes=VMEM_LIMIT)


def _dot(a, b, precision=None):
    return jnp.dot(a, b, preferred_element_type=F32, precision=precision)


def _dot_nt(a, b, precision=None):
    return lax.dot_general(a, b, (((1,), (1,)), ((), ())), preferred_element_type=F32, precision=precision)


def _dot_tn(a, b):
    return lax.dot_general(a, b, (((0,), (0,)), ((), ())), preferred_element_type=F32)


def _gelu(x):
    t = jnp.tanh(x * (GELU_C + (GELU_C * GELU_A) * (x * x)))
    return (0.5 * x) * (1.0 + t)


def _gelu_and_grad(x):
    x2 = x * x
    u = 1.0 + jnp.tanh(x * (GELU_C + (GELU_C * GELU_A) * x2))
    hx = 0.5 * x
    dg = u * (0.5 + hx * (2.0 - u) * (GELU_C + (3.0 * GELU_C * GELU_A) * x2))
    return hx * u, dg


def _mesh_pos():
    x, y, c = lax.axis_index("x"), lax.axis_index("y"), lax.axis_index("c")
    chips = [(1 - x, y), (x, 1 - y), (1 - x, 1 - y)]
    return x, y, c, chips


class _GatherPlan:
    def __init__(self, shapes, out_refs, send_sems, recv_sems):
        self.shapes, self.out_refs = shapes, out_refs
        self.send_sems, self.recv_sems = send_sems, recv_sems
        self.x, self.y, self.c, self.chips = _mesh_pos()
        self.sib = (self.x, self.y, 1 - self.c)

    def _half(self, t, chip, which):
        rows = self.shapes[t][0] // 2
        return self.out_refs[t].at[2 * chip[0] + chip[1], pl.ds(which * rows, rows), :]

    def _copy(self, k, src, dst, to):
        return pltpu.make_async_remote_copy(src_ref=src, dst_ref=dst, send_sem=self.send_sems.at[k],
                                            recv_sem=self.recv_sems.at[k], device_id=to, device_id_type=MESH)

    def _sends(self, t):
        own = self._half(t, (self.x, self.y), self.c)
        return [self._copy(6 * t + j, own, own, (*chip, self.c)) for j, chip in enumerate(self.chips)]

    def _forwards(self, t):
        return [self._copy(6 * t + 3 + j, self._half(t, chip, self.c), self._half(t, chip, self.c), self.sib)
                for j, chip in enumerate(self.chips)]

    def start(self, ts):
        for t in ts:
            for cp in self._sends(t):
                cp.start()

    def forward(self, ts):
        for t in ts:
            for j, chip in enumerate(self.chips):
                landed = self._half(t, chip, self.c)
                self._copy(6 * t + j, landed, landed, (*chip, self.c)).wait_recv()
            for cp in self._forwards(t):
                cp.start()

    def finish(self, ts):
        for t in ts:
            for j, chip in enumerate(self.chips):
                other = self._half(t, chip, 1 - self.c)
                self._copy(6 * t + 3 + j, other, other, self.sib).wait_recv()
        for t in ts:
            for cp in self._sends(t) + self._forwards(t):
                cp.wait_send()


class _RelayGatherPlan:
    def __init__(self, shapes, shard_refs, out_refs, send_sems, recv_sems):
        self.shapes, self.shard_refs, self.out_refs = shapes, shard_refs, out_refs
        self.send_sems, self.recv_sems = send_sems, recv_sems
        x, y, c, self.chips = _mesh_pos()
        self.me, self.c, self.sib = (x, y), c, (x, y, 1 - c)
        self.first = (x + c - 2 * x * c, y + (1 - c) - 2 * y * (1 - c))
        self.second = (x + (1 - c) - 2 * x * (1 - c), y + c - 2 * y * c)
        self.diag = (1 - x, 1 - y)

    def _half(self, t, chip, which):
        rows = self.shapes[t][0] // 2
        return self.out_refs[t].at[2 * chip[0] + chip[1], pl.ds(which * rows, rows), :]

    def _copy(self, k, src, dst, to):
        return pltpu.make_async_remote_copy(src_ref=src, dst_ref=dst, send_sem=self.send_sems.at[k],
                                            recv_sem=self.recv_sems.at[k], device_id=to, device_id_type=MESH)

    def _own(self, t):
        if self.shard_refs is None:
            return self._half(t, self.me, self.c)
        rows = self.shapes[t][0] // 2
        return self.shard_refs[t].at[pl.ds(self.c * rows, rows), :]

    def _step1(self, t):
        return self._copy(6 * t, self._own(t), self._half(t, self.me, self.c), (*self.first, self.c))

    def _step2(self, t):
        landed = self._half(t, self.first, self.c)
        return [self._copy(6 * t + 1, self._own(t), self._half(t, self.me, self.c), (*self.second, self.c)),
                self._copy(6 * t + 2, landed, landed, (*self.second, self.c))]

    def _forwards(self, t):
        return [self._copy(6 * t + 3 + j, self._half(t, chip, self.c), self._half(t, chip, self.c), self.sib)
                for j, chip in enumerate(self.chips)]

    def start(self, ts):
        for t in ts:
            self._step1(t).start()
            self._step2(t)[0].start()

    def relay(self, ts):
        for t in ts:
            landed = self._half(t, self.first, self.c)
            self._copy(6 * t, landed, landed, self.sib).wait_recv()
            self._step2(t)[1].start()

    def forward(self, ts):
        for t in ts:
            for k, chip in ((1, self.second), (2, self.diag)):
                landed = self._half(t, chip, self.c)
                self._copy(6 * t + k, landed, landed, self.sib).wait_recv()
            for cp in self._forwards(t):
                cp.start()

    def finish(self, ts):
        for t in ts:
            for j, chip in enumerate(self.chips):
                other = self._half(t, chip, 1 - self.c)
                self._copy(6 * t + 3 + j, other, other, self.sib).wait_recv()
        for t in ts:
            for cp in [self._step1(t)] + self._step2(t) + self._forwards(t):
                cp.wait_send()


class _SiblingExchangePlan:
    def __init__(self, shapes, grad_refs, out_refs, send_sems, recv_sems):
        self.shapes, self.grad_refs, self.out_refs = shapes, grad_refs, out_refs
        self.send_sems, self.recv_sems = send_sems, recv_sems
        self.x, self.y, self.c, _ = _mesh_pos()

    def _copies(self):
        out = []
        for t, (g, o) in enumerate(zip(self.grad_refs, self.out_refs)):
            rows = self.shapes[t][1] // 2
            out.append(pltpu.make_async_remote_copy(
                src_ref=g.at[:, pl.ds((1 - self.c) * rows, rows), :], dst_ref=o, send_sem=self.send_sems.at[t],
                recv_sem=self.recv_sems.at[t], device_id=(self.x, self.y, 1 - self.c), device_id_type=MESH))
        return out

    def start(self):
        for cp in self._copies():
            cp.start()

    def finish(self):
        for cp in self._copies():
            cp.wait()


class _ChipExchangePlan:
    def __init__(self, part_refs, out_refs, send_sems, recv_sems):
        self.part_refs, self.out_refs, self.send_sems, self.recv_sems = part_refs, out_refs, send_sems, recv_sems
        _, _, self.c, self.chips = _mesh_pos()

    def _copies(self):
        return [pltpu.make_async_remote_copy(
            src_ref=p.at[2 * chip[0] + chip[1]], dst_ref=o.at[j], send_sem=self.send_sems.at[3 * t + j],
            recv_sem=self.recv_sems.at[3 * t + j], device_id=(*chip, self.c), device_id_type=MESH)
            for t, (p, o) in enumerate(zip(self.part_refs, self.out_refs)) for j, chip in enumerate(self.chips)]

    def start(self):
        for cp in self._copies():
            cp.start()

    def finish(self):
        for cp in self._copies():
            cp.wait()


class _SmallAllReducePlan:
    def __init__(self, in_refs, out_refs, sib_refs, chip_refs, send_sems, recv_sems):
        self.in_refs, self.out_refs, self.sib_refs, self.chip_refs = in_refs, out_refs, sib_refs, chip_refs
        self.send_sems, self.recv_sems = send_sems, recv_sems
        self.n = len(in_refs)
        self.x, self.y, self.c, self.chips = _mesh_pos()

    def _copy(self, k, src, dst, to):
        return pltpu.make_async_remote_copy(src_ref=src, dst_ref=dst, send_sem=self.send_sems.at[k],
                                            recv_sem=self.recv_sems.at[k], device_id=to, device_id_type=MESH)

    def _first(self):
        return [self._copy(t, self.in_refs[t], self.sib_refs[t], (self.x, self.y, 1 - self.c)) for t in range(self.n)]

    def _second(self):
        return [self._copy(self.n + 3 * t + j, self.out_refs[t], self.chip_refs[t].at[j], (*chip, self.c))
                for t in range(self.n) for j, chip in enumerate(self.chips)]

    def start_sibling(self):
        for cp in self._first():
            cp.start()

    def sum_sibling_and_start_chips(self):
        for cp in self._first():
            cp.wait()
        for t in range(self.n):
            self.out_refs[t][...] = self.in_refs[t][...] + self.sib_refs[t][...]
        for cp in self._second():
            cp.start()

    def finish(self):
        for cp in self._second():
            cp.wait()
        for t in range(self.n):
            self.out_refs[t][...] = ((self.out_refs[t][...] + self.chip_refs[t][0])
                                     + (self.chip_refs[t][1] + self.chip_refs[t][2]))

    @staticmethod
    def scratch(arrays):
        return ([pltpu.VMEM(a.shape, F32) for a in arrays] + [pltpu.VMEM((3,) + a.shape, F32) for a in arrays]
                + _sem_pair(4 * len(arrays)))


def _sem_pair(n):
    return [pltpu.SemaphoreType.DMA((n,)), pltpu.SemaphoreType.DMA((n,))]


def _mm(a, b, *, dims, tm, tn, tk, out_dtype, name):
    if dims == "nn":
        m, k = a.shape
        n = b.shape[1]
        a_spec = pl.BlockSpec((tm, tk), lambda i, j, kk: (i, kk))
        b_spec = pl.BlockSpec((tk, tn), lambda i, j, kk: (kk, j))
        dot = _dot
    elif dims == "nt":
        m, k = a.shape
        n = b.shape[0]
        a_spec = pl.BlockSpec((tm, tk), lambda i, j, kk: (i, kk))
        b_spec = pl.BlockSpec((tn, tk), lambda i, j, kk: (j, kk))
        dot = _dot_nt
    else:
        k, m = a.shape
        n = b.shape[1]
        a_spec = pl.BlockSpec((tk, tm), lambda i, j, kk: (kk, i))
        b_spec = pl.BlockSpec((tk, tn), lambda i, j, kk: (kk, j))
        dot = _dot_tn
    assert m % tm == 0 and n % tn == 0 and k % tk == 0, (name, m, n, k)
    grid = (m // tm, n // tn, k // tk)
    nk = grid[2]
    own_acc = nk > 1 and out_dtype != F32

    def body(a_ref, b_ref, o_ref, *scratch):
        prod = dot(a_ref[...].astype(BF16), b_ref[...].astype(BF16))
        if nk == 1:
            o_ref[...] = prod.astype(out_dtype)
        else:
            acc_ref = scratch[0] if own_acc else o_ref
            kk = pl.program_id(2)

            @pl.when(kk == 0)
            def _():
                acc_ref[...] = prod

            @pl.when(kk > 0)
            def _():
                acc_ref[...] += prod

            if own_acc:
                @pl.when(kk == nk - 1)
                def _():
                    o_ref[...] = acc_ref[...].astype(out_dtype)

    return pl.pallas_call(
        body, grid=grid, in_specs=[a_spec, b_spec],
        out_specs=pl.BlockSpec((tm, tn), lambda i, j, kk: (i, j)),
        out_shape=jax.ShapeDtypeStruct((m, n), out_dtype),
        scratch_shapes=[pltpu.VMEM((tm, tn), F32)] if own_acc else [],
        compiler_params=_params(("parallel", "parallel", "arbitrary")), name=name,
    )(a, b)


def _mm_pair_tn(a1, a2, b, *, tm, tk, name):
    k, m = a1.shape
    n = b.shape[1]
    assert m % tm == 0 and k % tk == 0 and a2.shape == a1.shape, name
    nk = k // tk

    def body(a1_ref, a2_ref, b_ref, o1_ref, o2_ref, acc1_ref, acc2_ref):
        bv = b_ref[...]
        p1 = _dot_tn(a1_ref[...], bv)
        p2 = _dot_tn(a2_ref[...], bv)
        kk = pl.program_id(1)

        @pl.when(kk == 0)
        def _():
            acc1_ref[...] = p1
            acc2_ref[...] = p2

        @pl.when(kk > 0)
        def _():
            acc1_ref[...] += p1
            acc2_ref[...] += p2

        @pl.when(kk == nk - 1)
        def _():
            o1_ref[...] = acc1_ref[...].astype(BF16)
            o2_ref[...] = acc2_ref[...].astype(BF16)

    a_spec = pl.BlockSpec((tk, tm), lambda i, kk: (kk, i))
    o_spec = pl.BlockSpec((tm, n), lambda i, kk: (i, 0))
    return pl.pallas_call(
        body, grid=(m // tm, nk),
        in_specs=[a_spec, a_spec, pl.BlockSpec((tk, n), lambda i, kk: (kk, 0))],
        out_specs=[o_spec, o_spec],
        out_shape=[jax.ShapeDtypeStruct((m, n), BF16)] * 2,
        scratch_shapes=[pltpu.VMEM((tm, n), F32)] * 2,
        compiler_params=_params(("parallel", "arbitrary")), name=name,
    )(a1, a2, b)


def _mm_pair_nt(a, w1_t, w2_t, own, *, tm, tn, out_dtype, name):
    m, k = a.shape
    n = w1_t.shape[0]
    assert m % tm == 0 and n % tn == 0 and w2_t.shape == w1_t.shape, name
    grid = (m // tm, n // tn)
    n_steps = grid[0] * grid[1]

    def body(a_ref, w1_ref, w2_ref, own_ref, o1_ref, o2_ref, gat_ref, send_sems, recv_sems):
        del own_ref
        step = pl.program_id(0) * grid[1] + pl.program_id(1)
        gather = _GatherPlan([own.shape[1:]], [gat_ref], send_sems, recv_sems)

        @pl.when(step == 0)
        def _():
            gather.start([0])

        @pl.when(step == (2 * n_steps) // 3)
        def _():
            gather.forward([0])

        av = a_ref[...]
        o1_ref[...] = _dot_nt(av, w1_ref[...]).astype(out_dtype)
        o2_ref[...] = _dot_nt(av, w2_ref[...]).astype(out_dtype)

        @pl.when(step == n_steps - 1)
        def _():
            gather.finish([0])

    w_spec = pl.BlockSpec((tn, k), lambda i, j: (j, 0))
    o_spec = pl.BlockSpec((tm, tn), lambda i, j: (i, j))
    return pl.pallas_call(
        body, grid=grid,
        in_specs=[pl.BlockSpec((tm, k), lambda i, j: (i, 0)), w_spec, w_spec, ANY],
        out_specs=[o_spec, o_spec, ANY],
        out_shape=[jax.ShapeDtypeStruct((m, n), out_dtype)] * 2 + [jax.ShapeDtypeStruct(own.shape, own.dtype)],
        scratch_shapes=_sem_pair(6), input_output_aliases={3: 2},
        compiler_params=_params(("arbitrary", "arbitrary")), name=name,
    )(a, w1_t, w2_t, own)


def _out_proj_bwd(a, o, dy1, w_out, grads):
    m = dy1.shape[0]
    tm = 1024
    nx = len(grads)
    shapes = [g.shape for g in grads]
    n_steps = m // tm

    def body(a_ref, o_ref, dy_ref, w_ref, *rest):
        grad_refs = rest[:nx]
        dmix_ref, dw_ref = rest[nx:nx + 2]
        acc_ref = rest[2 * nx + 2]
        exchange = _SiblingExchangePlan(shapes, grad_refs, rest[nx + 2:2 * nx + 2], *rest[2 * nx + 3:])

        @pl.when(pl.program_id(0) == 0)
        def _():
            exchange.start()

        dy = dy_ref[...]
        dmix_ref[...] = _dot_nt(dy, w_ref[...])
        top = _dot_tn(a_ref[...], dy)
        bottom = _dot_tn(o_ref[...], dy)

        @pl.when(pl.program_id(0) == 0)
        def _():
            acc_ref[:A_WIDTH, :] = top
            acc_ref[A_WIDTH:, :] = bottom

        @pl.when(pl.program_id(0) > 0)
        def _():
            acc_ref[:A_WIDTH, :] += top
            acc_ref[A_WIDTH:, :] += bottom

        @pl.when(pl.program_id(0) == n_steps - 1)
        def _():
            dw_ref[...] = acc_ref[...].astype(BF16)
            exchange.finish()

    tile = lambda width: pl.BlockSpec((tm, width), lambda i: (i, 0))
    res = pl.pallas_call(
        body, grid=(n_steps,),
        in_specs=[tile(A_WIDTH), tile(B_WIDTH), tile(D_MODEL), _full_spec((D_MODEL, D_MODEL))] + [ANY] * nx,
        out_specs=[tile(D_MODEL), _full_spec((D_MODEL, D_MODEL))] + [ANY] * nx,
        out_shape=[jax.ShapeDtypeStruct((m, D_MODEL), F32), jax.ShapeDtypeStruct((D_MODEL, D_MODEL), BF16)]
        + [jax.ShapeDtypeStruct((N_SHARD, s[1] // 2, s[2]), g.dtype) for s, g in zip(shapes, grads)],
        scratch_shapes=[pltpu.VMEM((D_MODEL, D_MODEL), F32)] + _sem_pair(nx),
        compiler_params=_params(("arbitrary",)), name="out_proj_bwd",
    )(a, o, dy1, w_out, *grads)
    return res[:2], list(res[2:])


def _fused_rows(name, tm, mats, rows, vecs, fn, row_outs, acc_outs, exchange=()):
    m = mats[0][0].shape[0]
    nm, nr, nv, nro, nao, nx = len(mats), len(rows), len(vecs), len(row_outs), len(acc_outs), len(exchange)
    n_steps = m // tm

    def body(*refs):
        a_refs, w_refs = refs[:nm], refs[nm:2 * nm]
        pos = 2 * nm
        row_refs, vec_refs, part_refs = refs[pos:pos + nr], refs[pos + nr:pos + nr + nv], refs[pos + nr + nv:pos + nr + nv + nx]
        pos += nr + nv + nx
        out_refs, acc_refs, recv_refs = refs[pos:pos + nro], refs[pos + nro:pos + nro + nao], refs[pos + nro + nao:pos + nro + nao + nx]
        sems = refs[pos + nro + nao + nx:]
        i = pl.program_id(0)
        if nx:
            plan = _ChipExchangePlan(part_refs, recv_refs, *sems)

            @pl.when(i == 0)
            def _():
                plan.start()

        @pl.when(i == 0)
        def _():
            for r in acc_refs:
                r[...] = jnp.zeros_like(r)

        y = None
        for a_ref, w_ref, (_, _, dims, sl) in zip(a_refs, w_refs, mats):
            w = w_ref[...] if sl is None else w_ref[sl, :]
            part = (_dot if dims == "nn" else _dot_nt)(a_ref[...], w)
            y = part if y is None else y + part
        res = fn(y, *[r[...] for r in row_refs], *[v[...] for v in vec_refs])
        for r, val in zip(out_refs, res[:nro]):
            r[...] = val.astype(r.dtype)
        for r, val in zip(acc_refs, res[nro:]):
            r[...] += val

        if nx:
            @pl.when(i == n_steps - 1)
            def _():
                plan.finish()

    tile = lambda width: pl.BlockSpec((tm, width), lambda i: (i, 0))
    res = pl.pallas_call(
        body, grid=(n_steps,),
        in_specs=[tile(a.shape[1]) for a, _, _, _ in mats] + [_full_spec(w.shape) for _, w, _, _ in mats]
        + [tile(D_MODEL)] * nr + [_full_spec((1, D_MODEL))] * nv + [ANY] * nx,
        out_specs=[tile(D_MODEL)] * nro + [_full_spec(s) for s in acc_outs] + [ANY] * nx,
        out_shape=[jax.ShapeDtypeStruct((m, D_MODEL), dt) for dt in row_outs]
        + [jax.ShapeDtypeStruct(s, F32) for s in acc_outs]
        + [jax.ShapeDtypeStruct((3,) + p.shape[1:], p.dtype) for p in exchange],
        scratch_shapes=_sem_pair(3 * nx) if nx else [],
        compiler_params=_params(("arbitrary",)), name=name,
    )(*[a for a, _, _, _ in mats], *[w for _, w, _, _ in mats], *rows, *vecs, *exchange)
    return list(res[:nro + nao]), list(res[nro + nao:])


def _vec_spec(width=D_MODEL):
    return pl.BlockSpec((1, width), lambda i: (0, 0))


def _rstd(v):
    return lax.rsqrt(jnp.mean(v * v, axis=-1, keepdims=True) + NORM_EPS)


def _mid_fwd_rows(y1, x0, g2, g3):
    x1 = x0 + y1 * _rstd(y1) * g2
    return y1, x1, x1 * _rstd(x1) * g3


def _rms_bwd_rows(dout, v, g):
    r = _rstd(v)
    n = v * r
    dn = dout * g
    dv = r * (dn - n * jnp.mean(dn * n, axis=-1, keepdims=True))
    dg = jnp.sum(dout * n, axis=0, keepdims=True)
    return dv, dg


def _loss_head_rows(y2, x1, tgt, g4):
    x2 = x1 + y2 * _rstd(y2) * g4
    err = x2 - tgt
    loss = 0.5 * jnp.sum(jnp.mean(err * err, axis=-1, keepdims=True), axis=0, keepdims=True)
    dx2 = err * (1.0 / D_MODEL)
    dy2, dg4 = _rms_bwd_rows(dx2, y2, g4)
    return dx2, dy2, dg4, loss


def _mid_bwd_rows(dh2, x1, y1, dx2, g2, g3):
    d3, dg3 = _rms_bwd_rows(dh2, x1, g3)
    dx1 = dx2 + d3
    dy1, dg2 = _rms_bwd_rows(dx1, y1, g2)
    return dx1, dy1, dg2, dg3


def _in_bwd_rows(dh1, x0, dx1, g1):
    d1, dg1 = _rms_bwd_rows(dh1, x0, g1)
    return dx1 + d1, dg1


GATE_ROWS = 512


def _group_mean_matrix():
    p = np.zeros((A_WIDTH, A_WIDTH), np.float32)
    for g in range(A_GROUPS):
        p[g * HEAD_DIM:(g + 1) * HEAD_DIM, g * HEAD_DIM:(g + 1) * HEAD_DIM] = 1.0 / HEAD_DIM
    return jnp.asarray(p)


def _group_masks(width=A_WIDTH):
    lane = lax.broadcasted_iota(jnp.int32, (1, width), 1)
    return [(lane >= g * HEAD_DIM) & (lane < (g + 1) * HEAD_DIM) for g in range(width // HEAD_DIM)]


GROUP_SUM_PRECISION = lax.Precision.HIGH


def _layernorm_groups(vg, pavg):
    hi = GROUP_SUM_PRECISION
    mu = _dot(vg, pavg, hi)
    xc = vg - mu
    var = _dot(xc * xc, pavg, hi)
    rstd = lax.rsqrt(var + NORM_EPS)
    return xc * rstd, rstd


def _spatial_mix(w_bf, vn_chunk_bf, masks, bz):
    z = bz
    for g in range(A_GROUPS):
        z = z + jnp.where(masks[g], _dot(w_bf[g], vn_chunk_bf), 0.0)
    return z


def _full_spec(shape):
    return pl.BlockSpec(shape, lambda i: tuple(0 for _ in shape))


def _gate_fwd_rows(u, v, lg, lb, w_ref, bz, pavg, a_ref):
    masks = _group_masks()
    row = lax.broadcasted_iota(jnp.int32, (CHUNK, CHUNK), 0)
    col = lax.broadcasted_iota(jnp.int32, (CHUNK, CHUNK), 1)
    w_bf = [jnp.where(row >= col, w_ref[g], 0.0).astype(BF16) for g in range(A_GROUPS)]
    ug = _gelu(u)
    vhat, _ = _layernorm_groups(_gelu(v), pavg)
    vn = vhat * lg + lb
    for c in range(GATE_ROWS // CHUNK):
        sl = slice(c * CHUNK, (c + 1) * CHUNK)
        z = _spatial_mix(w_bf, vn[sl].astype(BF16), masks, bz)
        a_ref[sl, :] = (ug[sl] * z).astype(BF16)


def _gate_bwd(uv, dmix, ln_g, ln_b, w_s, w_st, bz, grads):
    m = uv.shape[0]
    pavg = _group_mean_matrix()
    nsteps = m // GATE_ROWS
    nx = len(grads)
    shapes = [g.shape for g in grads]

    def body(u_ref, v_ref, da_ref, lg_ref, lb_ref, w_ref, wt_ref, bz_ref, p_ref, *rest):
        grad_refs = rest[:nx]
        duv_ref, dlg_ref, dlb_ref, dw_ref, dbz_ref = rest[nx:nx + 5]
        recv_refs = rest[nx + 5:2 * nx + 5]
        exchange = _SiblingExchangePlan(shapes, grad_refs, recv_refs, *rest[2 * nx + 5:])
        i = pl.program_id(0)

        @pl.when(i == 0)
        def _():
            exchange.start()
            dlg_ref[...] = jnp.zeros_like(dlg_ref)
            dlb_ref[...] = jnp.zeros_like(dlb_ref)
            dw_ref[...] = jnp.zeros_like(dw_ref)
            dbz_ref[...] = jnp.zeros_like(dbz_ref)

        hi = GROUP_SUM_PRECISION
        masks = _group_masks()
        row = lax.broadcasted_iota(jnp.int32, (CHUNK, CHUNK), 0)
        col = lax.broadcasted_iota(jnp.int32, (CHUNK, CHUNK), 1)
        tril = row >= col
        w_bf = [jnp.where(tril, w_ref[g], 0.0).astype(BF16) for g in range(A_GROUPS)]
        wt_bf = [jnp.where(col >= row, wt_ref[g], 0.0).astype(BF16) for g in range(A_GROUPS)]
        pavg_v = p_ref[...]
        lg = lg_ref[...]
        ug, dug = _gelu_and_grad(u_ref[...])
        vg, dvg_dx = _gelu_and_grad(v_ref[...])
        vhat, rstd = _layernorm_groups(vg, pavg_v)
        vn = vhat * lg + lb_ref[...]
        da = da_ref[...]
        bz = bz_ref[...]
        for c in range(GATE_ROWS // CHUNK):
            sl = slice(c * CHUNK, (c + 1) * CHUNK)
            vn_bf = vn[sl].astype(BF16)
            z = _spatial_mix(w_bf, vn_bf, masks, bz)
            dz = da[sl] * ug[sl]
            duv_ref[sl, 0:A_WIDTH] = (da[sl] * z * dug[sl]).astype(BF16)
            dbz_ref[...] += dz
            dz_bf = dz.astype(BF16)
            dvn = jnp.zeros((CHUNK, A_WIDTH), F32)
            for g in range(A_GROUPS):
                dz_g = jnp.where(masks[g], dz, 0.0).astype(BF16)
                dw_ref[g] += jnp.where(tril, _dot_nt(dz_g, vn_bf), 0.0)
                dvn = dvn + jnp.where(masks[g], _dot(wt_bf[g], dz_bf), 0.0)
            vh = vhat[sl]
            dlb_ref[...] += jnp.sum(dvn, axis=0, keepdims=True)
            dlg_ref[...] += jnp.sum(dvn * vh, axis=0, keepdims=True)
            dvh = dvn * lg
            m1 = _dot(dvh, pavg_v, hi)
            m2 = _dot(dvh * vh, pavg_v, hi)
            duv_ref[sl, A_WIDTH:2 * A_WIDTH] = (rstd[sl] * (dvh - m1 - vh * m2) * dvg_dx[sl]).astype(BF16)

        @pl.when(i == nsteps - 1)
        def _():
            dbz_ref[...] = _dot(dbz_ref[...], pavg_v * float(HEAD_DIM), hi)
            exchange.finish()

    res = pl.pallas_call(
        body, grid=(nsteps,),
        in_specs=[pl.BlockSpec((GATE_ROWS, A_WIDTH), lambda i: (i, 0)),
                  pl.BlockSpec((GATE_ROWS, A_WIDTH), lambda i: (i, 1)),
                  pl.BlockSpec((GATE_ROWS, A_WIDTH), lambda i: (i, 0)),
                  _full_spec((1, A_WIDTH)), _full_spec((1, A_WIDTH)), _full_spec((A_GROUPS, CHUNK, CHUNK)),
                  _full_spec((A_GROUPS, CHUNK, CHUNK)), _full_spec((CHUNK, A_WIDTH)),
                  _full_spec((A_WIDTH, A_WIDTH))] + [ANY] * nx,
        out_specs=[pl.BlockSpec((GATE_ROWS, 2 * A_WIDTH), lambda i: (i, 0)),
                   _full_spec((1, A_WIDTH)), _full_spec((1, A_WIDTH)), _full_spec((A_GROUPS, CHUNK, CHUNK)),
                   _full_spec((CHUNK, A_WIDTH))] + [ANY] * nx,
        out_shape=[jax.ShapeDtypeStruct((m, IN_COLS), BF16),
                   jax.ShapeDtypeStruct((1, A_WIDTH), F32), jax.ShapeDtypeStruct((1, A_WIDTH), F32),
                   jax.ShapeDtypeStruct((A_GROUPS, CHUNK, CHUNK), F32),
                   jax.ShapeDtypeStruct((CHUNK, A_WIDTH), F32)]
        + [jax.ShapeDtypeStruct((N_SHARD, s[1] // 2, s[2]), g.dtype) for s, g in zip(shapes, grads)],
        scratch_shapes=_sem_pair(nx),
        compiler_params=_params(("arbitrary",)), name="gate_bwd",
    )(uv, uv, dmix, ln_g, ln_b, w_s, w_st, bz, pavg, *grads)
    return res[:5], list(res[5:])


Q_BLOCK = 128
PAIR = 2 * HEAD_DIM
N_PAIR = B_HEADS // 2
N_CFG = len(DILATED)
BLOCKS_PER_CFG = SEQ // Q_BLOCK
QKV_SLABS = 3 * N_PAIR
FWD_BLOCKS_PER_TRIP = 8
BWD_BLOCKS_PER_TRIP = 4


def _t5_bucket_np(dist, dtype):
    max_exact = NUM_BUCKETS // 2
    d = np.maximum(dist, 1).astype(dtype)
    large = max_exact + (np.log(d / dtype(max_exact)) / dtype(math.log(MAX_DISTANCE / max_exact))
                         * dtype(NUM_BUCKETS - max_exact))
    large = np.minimum(large.astype(np.int32), NUM_BUCKETS - 1)
    return np.where(dist < max_exact, dist, large)


def _bucket_tables():
    i = np.arange(Q_BLOCK)[:, None]
    j = np.arange(Q_BLOCK)[None, :]
    tables = []
    for _, dil in DILATED:
        rel_prev = Q_BLOCK + i - j
        rel_cur = i - j
        rel = np.concatenate([rel_prev, rel_cur], axis=1)
        valid = np.concatenate([rel_prev <= Q_BLOCK, rel_cur >= 0], axis=1)
        dist = np.maximum(rel, 0) * dil
        b32 = _t5_bucket_np(dist, np.float32)
        b64 = _t5_bucket_np(dist, np.float64)
        assert np.array_equal(b32, b64)
        tables.append(np.where(valid, b32, -1).astype(np.int32))
    return np.stack(tables)


def _present_buckets(buckets_np):
    return [sorted(set(int(v) for v in np.unique(buckets_np[c]) if v >= 0)) for c in range(N_CFG)]


def _bias_tables_body(buckets_np):
    present = _present_buckets(buckets_np)

    def tables(rb_ref, bk_ref, o_ref, ot_ref):
        for c in range(N_CFG):
            bk = bk_ref[c]
            for h in range(B_HEADS):
                acc = jnp.full((Q_BLOCK, 2 * Q_BLOCK), NEG_INF, F32)
                for b in present[c]:
                    acc = jnp.where(bk == b, rb_ref[h, b], acc)
                o_ref[c, h] = acc
                ot_ref[c, h] = acc.T

    return tables


def _proj_fwd(x, g1, w_in_t, ln_g, ln_b, w_s, bz):
    m = x.shape[0]
    tm = GATE_ROWS
    pavg = _group_mean_matrix()

    def body(x_ref, g_ref, w_ref, lg_ref, lb_ref, ws_ref, bz_ref, p_ref, h_ref, uv_ref, qkv_ref, a_ref):
        xv = x_ref[...]
        h = (xv * _rstd(xv) * g_ref[...]).astype(BF16)
        h_ref[...] = h
        acc = _dot_nt(h, w_ref[...])
        uv_ref[...] = acc[:, :2 * A_WIDTH]
        for s in range(QKV_SLABS):
            qkv_ref[s] = acc[:, 2 * A_WIDTH + s * PAIR:2 * A_WIDTH + (s + 1) * PAIR]
        _gate_fwd_rows(acc[:, :A_WIDTH], acc[:, A_WIDTH:2 * A_WIDTH], lg_ref[...], lb_ref[...], ws_ref,
                       bz_ref[...], p_ref[...], a_ref)

    return pl.pallas_call(
        body, grid=(m // tm,),
        in_specs=[pl.BlockSpec((tm, D_MODEL), lambda i: (i, 0)), _vec_spec(),
                  pl.BlockSpec((IN_COLS, D_MODEL), lambda i: (0, 0)),
                  _full_spec((1, A_WIDTH)), _full_spec((1, A_WIDTH)), _full_spec((A_GROUPS, CHUNK, CHUNK)),
                  _full_spec((CHUNK, A_WIDTH)), _full_spec((A_WIDTH, A_WIDTH))],
        out_specs=[pl.BlockSpec((tm, D_MODEL), lambda i: (i, 0)),
                   pl.BlockSpec((tm, 2 * A_WIDTH), lambda i: (i, 0)),
                   pl.BlockSpec((QKV_SLABS, tm, PAIR), lambda i: (0, i, 0)),
                   pl.BlockSpec((tm, A_WIDTH), lambda i: (i, 0))],
        out_shape=[jax.ShapeDtypeStruct((m, D_MODEL), BF16), jax.ShapeDtypeStruct((m, 2 * A_WIDTH), F32),
                   jax.ShapeDtypeStruct((QKV_SLABS, m, PAIR), F32), jax.ShapeDtypeStruct((m, A_WIDTH), BF16)],
        compiler_params=_params(("parallel",)), name="proj_fwd",
    )(x, g1, w_in_t, ln_g, ln_b, w_s, bz, pavg)


def _pair_masks():
    lane = lax.broadcasted_iota(jnp.int32, (1, PAIR), 1)
    return [lane < HEAD_DIM, lane >= HEAD_DIM]


def _block_rows(idx, dil):
    static = isinstance(idx, int)
    r, n = idx % dil, idx // dil

    def rows_of(block):
        start = r + (dil * Q_BLOCK) * block
        if dil == 1:
            return pl.ds(start if static else pl.multiple_of(start, Q_BLOCK), Q_BLOCK)
        return pl.ds(start, Q_BLOCK, stride=dil)

    prev = rows_of(n - 1) if not static or n > 0 else None
    return rows_of(n), prev


def _attn_fwd(qkv, bias, batch, owns):
    m = qkv.shape[1]
    comb_rows = 256
    nt = len(owns)
    shapes = [g.shape[1:] for g in owns]
    n_steps = batch * N_PAIR
    ts = list(range(nt))

    def body(q_ref, k_ref, v_ref, b_ref, *rest):
        o_ref, l_ref = rest[nt:nt + 2]
        gat_refs = rest[nt + 2:2 * nt + 2]
        scratch = rest[2 * nt + 2:]
        oc_refs, lc_refs = scratch[:N_CFG], scratch[N_CFG:2 * N_CFG]
        step = pl.program_id(0) * N_PAIR + pl.program_id(1)
        gather = _RelayGatherPlan(shapes, None, gat_refs, *scratch[2 * N_CFG:])

        @pl.when(step == 0)
        def _():
            gather.start(ts)

        @pl.when(step == n_steps // 2)
        def _():
            gather.relay(ts)

        @pl.when(step == n_steps - 2)
        def _():
            gather.forward(ts)

        masks = _pair_masks()
        for ci, (_, dil) in enumerate(DILATED):
            nb = SEQ // dil // Q_BLOCK

            def block(trip, ci=ci, dil=dil, nb=nb):
                work = []
                for u in range(FWD_BLOCKS_PER_TRIP):
                    rows, prow = _block_rows(trip * FWD_BLOCKS_PER_TRIP + u, dil)
                    has_prev = nb > 1 and prow is not None
                    q = q_ref[rows, :] * 0.125
                    kc = k_ref[rows, :].astype(BF16)
                    vc = v_ref[rows, :]
                    kp = k_ref[prow, :].astype(BF16) if has_prev else None
                    vp = v_ref[prow, :] if has_prev else None
                    tiles = []
                    for h in range(2):
                        qh = jnp.where(masks[h], q, 0.0).astype(BF16)
                        sc = _dot_nt(qh, kc) + b_ref[ci, h, :, Q_BLOCK:]
                        sp = _dot_nt(qh, kp) + b_ref[ci, h, :, :Q_BLOCK] if has_prev else None
                        tiles.append((sc, sp))
                    work.append((rows, vc, vp, tiles))
                probs = []
                for _, _, _, tiles in work:
                    ps = []
                    for sc, sp in tiles:
                        mx = jnp.max(sc if sp is None else jnp.maximum(sc, sp), axis=1, keepdims=True)
                        pc = jnp.exp(sc - mx).astype(BF16)
                        pp = None if sp is None else jnp.exp(sp - mx).astype(BF16)
                        ps.append((mx, pc, pp))
                    probs.append(ps)
                for (rows, vc, vp, _), ps in zip(work, probs):
                    res = []
                    for h, (_, pc, pp) in enumerate(ps):
                        r = _dot(pc, jnp.where(masks[h], vc, 1.0).astype(BF16))
                        if pp is not None:
                            r = r + _dot(pp, jnp.where(masks[h], vp, 1.0).astype(BF16))
                        res.append(r)
                    num = jnp.where(masks[0], res[0], res[1])
                    den = pltpu.roll(jnp.where(masks[0], res[1], res[0]), HEAD_DIM, 1)
                    oc_refs[ci][rows, :] = num / den
                    lc_refs[ci][rows, :] = jnp.where(masks[0], ps[0][0], ps[1][0]) + jnp.log(den)

            for trip in range(BLOCKS_PER_CFG // FWD_BLOCKS_PER_TRIP):
                block(trip)

        def combine(i, carry):
            rr = pl.ds(pl.multiple_of(i * comb_rows, comb_rows), comb_rows)
            ls = [lc_refs[c][rr, :] for c in range(N_CFG)]
            mx = functools.reduce(jnp.maximum, ls)
            ws = [jnp.exp(l - mx) for l in ls]
            tot = functools.reduce(lambda a, b: a + b, ws)
            o = functools.reduce(lambda a, b: a + b, [ws[c] * oc_refs[c][rr, :] for c in range(N_CFG)]) / tot
            o_ref[rr, :] = o.astype(BF16)
            l_ref[rr, :] = mx + jnp.log(tot)
            return carry

        lax.fori_loop(0, SEQ // comb_rows, combine, 0)

        @pl.when(step == n_steps - 1)
        def _():
            gather.finish(ts)

    def slab(first):
        return pl.BlockSpec((None, SEQ, PAIR), lambda b, p: (first + p, b, 0))

    nat = pl.BlockSpec((SEQ, PAIR), lambda b, p: (b, p))
    res = pl.pallas_call(
        body, grid=(batch, N_PAIR),
        in_specs=[slab(0), slab(N_PAIR), slab(2 * N_PAIR),
                  pl.BlockSpec((N_CFG, 2, Q_BLOCK, 2 * Q_BLOCK), lambda b, p: (0, p, 0, 0))] + [ANY] * nt,
        out_specs=[nat, nat] + [ANY] * nt,
        out_shape=[jax.ShapeDtypeStruct((m, B_WIDTH), BF16), jax.ShapeDtypeStruct((m, B_WIDTH), F32)]
        + [jax.ShapeDtypeStruct(g.shape, g.dtype) for g in owns],
        scratch_shapes=[pltpu.VMEM((SEQ, PAIR), F32)] * (2 * N_CFG) + _sem_pair(6 * nt),
        input_output_aliases={4 + t: 2 + t for t in range(nt)},
        compiler_params=_params(("arbitrary", "arbitrary")), name="attn_fwd",
    )(qkv, qkv, qkv, bias, *owns)
    return res[0], res[1], list(res[2:])


def _attn_bwd(qkv, dmix, o, lse, bias_t, dproj, batch, parts, smalls):
    m = qkv.shape[1]
    nt, ns = len(parts), len(smalls)
    n_steps = N_PAIR * batch

    def body(q_ref, k_ref, v_ref, do_ref, o_ref, l_ref, b_ref, *rest):
        part_refs = rest[1:nt + 1]
        small_refs = rest[nt + 1:nt + 1 + ns]
        pos = nt + 1 + ns
        dproj_ref, ds_ref = rest[pos:pos + 2]
        recv_refs = rest[pos + 2:pos + 2 + nt]
        sum_refs = rest[pos + 2 + nt:pos + 2 + nt + ns]
        pos += 2 + nt + ns
        dq_acc, dk_acc, dv_acc, d_scr, stage, stage_sems, send_sems, recv_sems = rest[pos:pos + 8]
        allreduce = _SmallAllReducePlan(small_refs, sum_refs, rest[pos + 8:pos + 8 + ns],
                                        rest[pos + 8 + ns:pos + 8 + 2 * ns], *rest[pos + 8 + 2 * ns:])
        pair, seq = pl.program_id(0), pl.program_id(1)
        step = pair * batch + seq
        exchange = _ChipExchangePlan(part_refs, recv_refs, send_sems, recv_sems)

        @pl.when(step == 0)
        def _():
            allreduce.start_sibling()

        @pl.when(step == n_steps // 2)
        def _():
            allreduce.sum_sibling_and_start_chips()

        def stage_copies():
            rows = pl.ds(pl.multiple_of(seq * SEQ, SEQ), SEQ)
            return [pltpu.make_async_copy(
                stage.at[k],
                dproj_ref.at[rows, pl.ds(pl.multiple_of(2 * A_WIDTH + k * B_WIDTH + pair * PAIR, PAIR), PAIR)],
                stage_sems.at[k]) for k in range(3)]

        @pl.when(step == 0)
        def _():
            exchange.start()

        @pl.when(pl.program_id(1) == 0)
        def _():
            ds_ref[...] = jnp.zeros_like(ds_ref)

        dq_acc[...] = jnp.zeros_like(dq_acc)
        dk_acc[...] = jnp.zeros_like(dk_acc)
        dv_acc[...] = jnp.zeros_like(dv_acc)
        d_scr[...] = do_ref[...] * o_ref[...].astype(F32)
        masks = _pair_masks()

        def stack_heads(t):
            return jnp.concatenate([jnp.where(masks[0], t, 0.0), jnp.where(masks[1], t, 0.0)], axis=0).astype(BF16)

        for ci, (_, dil) in enumerate(DILATED):
            nb = SEQ // dil // Q_BLOCK

            def block(trip, carry, ci=ci, dil=dil, nb=nb):
                first = []
                for u in range(BWD_BLOCKS_PER_TRIP):
                    rows, prow = _block_rows(trip * BWD_BLOCKS_PER_TRIP + u, dil)
                    has_prev = nb > 1 and prow is not None
                    if has_prev:
                        kcat = jnp.concatenate([k_ref[prow, :], k_ref[rows, :]], axis=0).astype(BF16)
                        vcat = jnp.concatenate([v_ref[prow, :], v_ref[rows, :]], axis=0).astype(BF16)
                    else:
                        kcat = k_ref[rows, :].astype(BF16)
                        vcat = v_ref[rows, :].astype(BF16)
                    qst = stack_heads(q_ref[rows, :] * 0.125)
                    dost = stack_heads(do_ref[rows, :])
                    lt = l_ref[rows, :].T
                    dt = d_scr[rows, :].T
                    lrow = jnp.concatenate([lt[0:1], lt[HEAD_DIM:HEAD_DIM + 1]], axis=1)
                    drow = jnp.concatenate([jnp.sum(dt[:HEAD_DIM], axis=0, keepdims=True),
                                            jnp.sum(dt[HEAD_DIM:], axis=0, keepdims=True)], axis=1)
                    first.append((has_prev, rows, prow, kcat, qst, dost, lrow, drow,
                                  _dot_nt(kcat, qst), _dot_nt(vcat, dost)))
                second = []
                for has_prev, rows, prow, kcat, qst, dost, lrow, drow, st, dpt in first:
                    keys = slice(0, 2 * Q_BLOCK) if has_prev else slice(Q_BLOCK, 2 * Q_BLOCK)
                    bt = jnp.concatenate([b_ref[ci, 0, keys, :], b_ref[ci, 1, keys, :]], axis=1)
                    pt = jnp.exp(st + bt - lrow)
                    dst = pt * (dpt - drow)
                    ds_ref[ci, 0, keys, :] += dst[:, :Q_BLOCK]
                    ds_ref[ci, 1, keys, :] += dst[:, Q_BLOCK:]
                    second.append((has_prev, rows, prow, kcat, qst, dost, pt.astype(BF16), dst.astype(BF16)))
                for has_prev, rows, prow, kcat, qst, dost, pt_bf, dst_bf in second:
                    dk = _dot(dst_bf, qst)
                    dv = _dot(pt_bf, dost)
                    dq2 = _dot_tn(dst_bf, kcat)
                    dq_acc[rows, :] += jnp.where(masks[0], dq2[:Q_BLOCK], dq2[Q_BLOCK:]) * 0.125
                    if has_prev:
                        dk_acc[prow, :] += dk[:Q_BLOCK]
                        dv_acc[prow, :] += dv[:Q_BLOCK]
                        dk_acc[rows, :] += dk[Q_BLOCK:]
                        dv_acc[rows, :] += dv[Q_BLOCK:]
                    else:
                        dk_acc[rows, :] += dk
                        dv_acc[rows, :] += dv
                return carry

            for trip in range(BLOCKS_PER_CFG // BWD_BLOCKS_PER_TRIP):
                block(trip, 0)

        @pl.when(step > 0)
        def _():
            for cp in stage_copies():
                cp.wait()

        stage[0] = dq_acc[...].astype(BF16)
        stage[1] = dk_acc[...].astype(BF16)
        stage[2] = dv_acc[...].astype(BF16)
        for cp in stage_copies():
            cp.start()

        @pl.when(step == n_steps - 1)
        def _():
            for cp in stage_copies():
                cp.wait()
            exchange.finish()
            allreduce.finish()

    def slab(first):
        return pl.BlockSpec((None, SEQ, PAIR), lambda p, b: (first + p, b, 0))

    nat = pl.BlockSpec((SEQ, PAIR), lambda p, b: (b, p))
    tbl = pl.BlockSpec((N_CFG, 2, 2 * Q_BLOCK, Q_BLOCK), lambda p, b: (0, p, 0, 0))
    acc = pltpu.VMEM((SEQ, PAIR), F32)
    vm = pl.BlockSpec(memory_space=pltpu.VMEM)
    res = pl.pallas_call(
        body, grid=(N_PAIR, batch),
        in_specs=[slab(0), slab(N_PAIR), slab(2 * N_PAIR),
                  pl.BlockSpec((SEQ, PAIR), lambda p, b: (b, A_WIDTH // PAIR + p)), nat, nat, tbl]
        + [ANY] * (nt + 1) + [vm] * ns,
        out_specs=[ANY, tbl] + [ANY] * nt + [vm] * ns,
        out_shape=[jax.ShapeDtypeStruct(dproj.shape, dproj.dtype),
                   jax.ShapeDtypeStruct((N_CFG, B_HEADS, 2 * Q_BLOCK, Q_BLOCK), F32)]
        + [jax.ShapeDtypeStruct((3,) + p.shape[1:], p.dtype) for p in parts]
        + [jax.ShapeDtypeStruct(a.shape, F32) for a in smalls],
        input_output_aliases={7: 0},
        scratch_shapes=[acc, acc, acc, acc, pltpu.VMEM((3, SEQ, PAIR), BF16), pltpu.SemaphoreType.DMA((3,))]
        + _sem_pair(3 * nt) + _SmallAllReducePlan.scratch(smalls),
        compiler_params=_params(("arbitrary", "arbitrary")), name="attn_bwd",
    )(qkv, qkv, qkv, dmix, o, lse, bias_t, dproj, *parts, *smalls)
    return res[0], res[1], list(res[2:2 + nt]), list(res[2 + nt:])


def _rel_bias_grad(ds, buckets_np, grads):
    present = _present_buckets(buckets_np)
    nx = len(grads)
    shapes = [g.shape for g in grads]

    def body(bk_ref, ds_ref, *rest):
        o_ref = rest[nx]
        acc_ref = rest[2 * nx + 1]
        exchange = _SiblingExchangePlan(shapes, rest[:nx], rest[nx + 1:2 * nx + 1], *rest[2 * nx + 2:])
        exchange.start()
        acc_ref[...] = jnp.zeros_like(acc_ref)
        for c in range(N_CFG):
            bk = bk_ref[c]
            for h in range(B_HEADS):
                dsv = ds_ref[c, h]
                for b in present[c]:
                    part = jnp.sum(jnp.where(bk == b, dsv, 0.0), axis=0, keepdims=True)
                    acc_ref[pl.ds(h * NUM_BUCKETS + b, 1), :] += part
        o_ref[...] = jnp.sum(acc_ref[...], axis=1, keepdims=True)
        exchange.finish()

    vm = pl.BlockSpec(memory_space=pltpu.VMEM)
    res = pl.pallas_call(
        body, in_specs=[vm, vm] + [ANY] * nx, out_specs=[vm] + [ANY] * nx,
        out_shape=[jax.ShapeDtypeStruct((B_HEADS * NUM_BUCKETS, 1), F32)]
        + [jax.ShapeDtypeStruct((N_SHARD, s[1] // 2, s[2]), g.dtype) for s, g in zip(shapes, grads)],
        scratch_shapes=[pltpu.VMEM((B_HEADS * NUM_BUCKETS, buckets_np.shape[-1]), F32)] + _sem_pair(nx),
        compiler_params=_params(), name="rel_bias_grad",
    )(jnp.asarray(buckets_np), ds, *grads)
    return res[0], list(res[1:])


CONV_CHUNK = 64
CONV_HALO = 16


def _row_index():
    return lax.broadcasted_iota(jnp.int32, (SEQ, LANE_BLOCK), 0)


def _shift_down(x, k, row):
    return jnp.where(row >= k, pltpu.roll(x, k, 0), 0.0)


def _shift_up(x, k, row):
    return jnp.where(row < SEQ - k, pltpu.roll(x, SEQ - k, 0), 0.0)


def _convgate_fwd(gate, up, conv_w, conv_b, batch):
    m = gate.shape[0]

    def body(g_ref, u_ref, w_ref, b_ref, a_ref):
        w = w_ref[...]
        w0, w1, w2, b = w[0:1], w[1:2], w[2:3], b_ref[...]

        def finish(r0, g, g1, g2):
            c = b + w0 * g2 + w1 * g1 + w2 * g
            rows = pl.ds(r0, CONV_CHUNK)
            a_ref[rows, :] = (_gelu(c) * u_ref[rows, :].astype(F32)).astype(BF16)

        g = g_ref[0:CONV_CHUNK, :].astype(F32)
        row = lax.broadcasted_iota(jnp.int32, (CONV_CHUNK, LANE_BLOCK), 0)
        finish(0, g, jnp.where(row >= 1, pltpu.roll(g, 1, 0), 0.0), jnp.where(row >= 2, pltpu.roll(g, 2, 0), 0.0))

        for r0 in range(CONV_CHUNK, SEQ, CONV_CHUNK):
            gh = g_ref[pl.ds(r0 - CONV_HALO, CONV_CHUNK + CONV_HALO), :].astype(F32)
            finish(r0, gh[CONV_HALO:], pltpu.roll(gh, 1, 0)[CONV_HALO:], pltpu.roll(gh, 2, 0)[CONV_HALO:])

    blk = pl.BlockSpec((SEQ, LANE_BLOCK), lambda b, j: (b, j))
    return pl.pallas_call(
        body, grid=(batch, D_FF // LANE_BLOCK),
        in_specs=[blk, blk, pl.BlockSpec((3, LANE_BLOCK), lambda b, j: (0, j)),
                  pl.BlockSpec((1, LANE_BLOCK), lambda b, j: (0, j))],
        out_specs=blk,
        out_shape=jax.ShapeDtypeStruct((m, D_FF), BF16),
        compiler_params=_params(("parallel", "parallel")), name="convgate_fwd",
    )(gate, up, conv_w, conv_b)


def _convgate_bwd(gate, up, dact, conv_w, conv_b, batch):
    m = gate.shape[0]

    def body(g_ref, u_ref, da_ref, w_ref, b_ref, dg_ref, du_ref, dw_ref, db_ref):
        @pl.when(pl.program_id(1) == 0)
        def _():
            dw_ref[...] = jnp.zeros_like(dw_ref)
            db_ref[...] = jnp.zeros_like(db_ref)

        g = g_ref[...].astype(F32)
        w = w_ref[...]
        row = _row_index()
        g1 = _shift_down(g, 1, row)
        g2 = _shift_down(g, 2, row)
        c = b_ref[...] + w[0:1] * g2 + w[1:2] * g1 + w[2:3] * g
        gg, dgg = _gelu_and_grad(c)
        da = da_ref[...].astype(F32)
        du_ref[...] = (da * gg).astype(BF16)
        dc = da * u_ref[...].astype(F32) * dgg
        db_ref[...] += jnp.sum(dc, axis=0, keepdims=True)
        dw_ref[0:1, :] += jnp.sum(dc * g2, axis=0, keepdims=True)
        dw_ref[1:2, :] += jnp.sum(dc * g1, axis=0, keepdims=True)
        dw_ref[2:3, :] += jnp.sum(dc * g, axis=0, keepdims=True)
        dg_ref[...] = (w[2:3] * dc + w[1:2] * _shift_up(dc, 1, row) + w[0:1] * _shift_up(dc, 2, row)).astype(BF16)

    blk = pl.BlockSpec((SEQ, LANE_BLOCK), lambda j, b: (b, j))
    wspec = pl.BlockSpec((3, LANE_BLOCK), lambda j, b: (0, j))
    bspec = pl.BlockSpec((1, LANE_BLOCK), lambda j, b: (0, j))
    return pl.pallas_call(
        body, grid=(D_FF // LANE_BLOCK, batch),
        in_specs=[blk, blk, blk, wspec, bspec],
        out_specs=[blk, blk, wspec, bspec],
        out_shape=[jax.ShapeDtypeStruct((m, D_FF), BF16), jax.ShapeDtypeStruct((m, D_FF), BF16),
                   jax.ShapeDtypeStruct((3, D_FF), F32), jax.ShapeDtypeStruct((1, D_FF), F32)],
        compiler_params=_params(("parallel", "arbitrary")), name="convgate_bwd",
    )(gate, up, dact, conv_w, conv_b)


def _gather_weights(shards, conv_w_shard, rel_bias, buckets_np):
    nt = len(shards)
    shapes = [sh.shape for sh in shards]
    ts = list(range(nt))
    tables = _bias_tables_body(buckets_np)

    def body(*refs):
        shard_refs = refs[:nt]
        cw_ref, rb_ref, bk_ref = refs[nt:nt + 3]
        out_refs = refs[nt + 3:2 * nt + 3]
        cw_out, bias_ref, bias_t_ref = refs[2 * nt + 3:2 * nt + 6]
        scratch = refs[2 * nt + 6:]
        f32_refs, bf16_refs = scratch[:nt], scratch[nt:2 * nt]
        load_sems, store_sems, send_sems, recv_sems, cw_send, cw_recv = scratch[2 * nt:]
        plan = _RelayGatherPlan(shapes[:1], bf16_refs[:1], out_refs[:1], send_sems, recv_sems)
        x, y, c, chips = _mesh_pos()
        loads = [pltpu.make_async_copy(shard_refs[t], f32_refs[t], load_sems.at[t]) for t in ts]
        stores = [pltpu.make_async_copy(bf16_refs[t], out_refs[t].at[2 * x + y], store_sems.at[t]) for t in ts]
        stores.append(pltpu.make_async_copy(cw_ref, cw_out.at[2 * x + y], store_sems.at[nt]))

        def cw_copy(j, src, dst, chip):
            return pltpu.make_async_remote_copy(src_ref=src, dst_ref=dst, send_sem=cw_send.at[j],
                                                recv_sem=cw_recv.at[j], device_id=(*chip, c), device_id_type=MESH)

        def to_bf16(t):
            loads[t].wait()
            bf16_refs[t][...] = f32_refs[t][...].astype(BF16)
            stores[t].start()

        for cp in loads:
            cp.start()
        to_bf16(0)
        plan.start([0])
        cw_sends = [cw_copy(j, cw_ref, cw_out.at[2 * x + y], chip) for j, chip in enumerate(chips)]
        for cp in cw_sends + stores[nt:]:
            cp.start()
        for t in ts[1:]:
            to_bf16(t)
        plan.relay([0])
        tables(rb_ref, bk_ref, bias_ref, bias_t_ref)
        plan.forward([0])
        for j, chip in enumerate(chips):
            dst = cw_out.at[2 * chip[0] + chip[1]]
            cw_copy(j, dst, dst, chip).wait_recv()
        plan.finish([0])
        for cp in cw_sends:
            cp.wait_send()
        for cp in stores:
            cp.wait()

    out_shape = [jax.ShapeDtypeStruct((N_SHARD,) + sh.shape, BF16) for sh in shards]
    out_shape.append(jax.ShapeDtypeStruct((N_SHARD,) + conv_w_shard.shape, conv_w_shard.dtype))
    out_shape += [jax.ShapeDtypeStruct((N_CFG, B_HEADS, Q_BLOCK, 2 * Q_BLOCK), F32),
                  jax.ShapeDtypeStruct((N_CFG, B_HEADS, 2 * Q_BLOCK, Q_BLOCK), F32)]
    vm = pl.BlockSpec(memory_space=pltpu.VMEM)
    res = pl.pallas_call(
        body, in_specs=[ANY] * (nt + 1) + [pl.BlockSpec(memory_space=pltpu.SMEM), vm],
        out_specs=[ANY] * (nt + 1) + [vm, vm], out_shape=out_shape,
        scratch_shapes=[pltpu.VMEM(sh.shape, F32) for sh in shards] + [pltpu.VMEM(sh.shape, BF16) for sh in shards]
        + [pltpu.SemaphoreType.DMA((nt,)), pltpu.SemaphoreType.DMA((nt + 1,))] + _sem_pair(6) + _sem_pair(3),
        compiler_params=pltpu.CompilerParams(has_side_effects=True, vmem_limit_bytes=VMEM_LIMIT),
        name="gather_weights",
    )(*shards, conv_w_shard, rel_bias.T, jnp.asarray(buckets_np))
    return list(res[:nt + 1]), res[nt + 1], res[nt + 2]


def _turn(t, u, s, last):
    return jnp.where(t == u, s, jnp.where(t > u, last, 0))


def _add_halves(gs, recvs, c_idx):
    n = len(gs)
    dims = [(g.shape[1] // 2, g.shape[2]) for g in gs]

    def body(c_ref, *refs):
        t = pl.program_id(0)
        for u in range(n):
            @pl.when(t == u)
            def _(u=u):
                refs[2 * n + u][...] = (refs[u][...].astype(F32) + refs[n + u][...].astype(F32)).astype(BF16)

    def own(u):
        return pl.BlockSpec((None, None) + dims[u], lambda t, s, c: (_turn(t, u, s, N_SHARD - 1), c[0], 0, 0))

    def plain(u):
        return pl.BlockSpec((None,) + dims[u], lambda t, s, c: (_turn(t, u, s, N_SHARD - 1), 0, 0))

    return pl.pallas_call(
        body,
        grid_spec=pltpu.PrefetchScalarGridSpec(
            num_scalar_prefetch=1, grid=(n, N_SHARD),
            in_specs=[own(u) for u in range(n)] + [plain(u) for u in range(n)],
            out_specs=[plain(u) for u in range(n)]),
        out_shape=[jax.ShapeDtypeStruct((N_SHARD,) + d, BF16) for d in dims],
        compiler_params=_params(("arbitrary", "arbitrary")), name="rs_add_halves",
    )(c_idx, *[g.reshape((N_SHARD, 2) + d) for g, d in zip(gs, dims)], *recvs)


def _add_chips(parts, recvs, s_idx, c_idx):
    n = len(parts)
    dims = [p.shape[1:] for p in parts]

    def body(idx_ref, *refs):
        t = pl.program_id(0)
        for u in range(n):
            @pl.when(t == u)
            def _(u=u):
                acc = refs[u][...].astype(F32)
                for j in range(3):
                    acc = acc + refs[n + u][j].astype(F32)
                refs[2 * n + u][...] = acc

    res = pl.pallas_call(
        body,
        grid_spec=pltpu.PrefetchScalarGridSpec(
            num_scalar_prefetch=1, grid=(n,),
            in_specs=[pl.BlockSpec((None,) + d, lambda t, idx: (idx[0], 0, 0)) for d in dims]
            + [pl.BlockSpec((3,) + d, lambda t, idx: (0, 0, 0)) for d in dims],
            out_specs=[pl.BlockSpec((None,) + d, lambda t, idx: (idx[1], 0, 0)) for d in dims]),
        out_shape=[jax.ShapeDtypeStruct((2,) + d, F32) for d in dims],
        compiler_params=_params(("arbitrary",)), name="rs_add_chips",
    )(jnp.concatenate([s_idx, c_idx]), *parts, *recvs)
    return [r.reshape(2 * d[0], d[1]) for r, d in zip(res, dims)]


def _finish_reductions(fulls, arrays):
    nt, n = len(fulls), len(arrays)

    def body(*refs):
        in_refs = refs[nt:nt + n]
        full_refs, out_refs = refs[nt + n:2 * nt + n], refs[2 * nt + n:2 * nt + 2 * n]
        pos = 2 * nt + 2 * n
        share_send, share_recv = refs[pos + 2 * n:pos + 2 * n + 2]
        allreduce = _SmallAllReducePlan(in_refs, out_refs, refs[pos:pos + n], refs[pos + n:pos + 2 * n],
                                        *refs[pos + 2 * n + 2:])
        x, y, c, _ = _mesh_pos()

        def half(t, which):
            rows = fulls[t].shape[0] // 2
            return full_refs[t].at[pl.ds(which * rows, rows), :]

        def share(t, which):
            return pltpu.make_async_remote_copy(
                src_ref=half(t, which), dst_ref=half(t, which), send_sem=share_send.at[t],
                recv_sem=share_recv.at[t], device_id=(x, y, 1 - c), device_id_type=MESH)

        for t in range(nt):
            share(t, c).start()
        allreduce.start_sibling()
        allreduce.sum_sibling_and_start_chips()
        allreduce.finish()
        for t in range(nt):
            share(t, 1 - c).wait_recv()
        for t in range(nt):
            share(t, c).wait_send()

    vm = pl.BlockSpec(memory_space=pltpu.VMEM)
    res = pl.pallas_call(
        body, in_specs=[ANY] * nt + [vm] * n, out_specs=[ANY] * nt + [vm] * n,
        out_shape=[jax.ShapeDtypeStruct(f.shape, f.dtype) for f in fulls]
        + [jax.ShapeDtypeStruct(a.shape, F32) for a in arrays],
        input_output_aliases={t: t for t in range(nt)},
        scratch_shapes=[pltpu.VMEM(a.shape, F32) for a in arrays] + [pltpu.VMEM((3,) + a.shape, F32) for a in arrays]
        + _sem_pair(nt) + _sem_pair(4 * n),
        compiler_params=pltpu.CompilerParams(has_side_effects=True),
        name="finish_reductions",
    )(*fulls, *arrays)
    return list(res[:nt]), list(res[nt:])


def _from_col_shards(g):
    n, rows, cols = g.shape
    return g.transpose(1, 0, 2).reshape(rows, n * cols)


def _train_step(x, tgt, g1, g2, g3, g4, shards, ln_g, ln_b, w_s, b_s, rel_bias, conv_w_shard, conv_b, batch,
                s_idx, c_idx):
    buckets = _bucket_tables()
    bz = jnp.repeat(b_s.T, HEAD_DIM, axis=1)
    w_st = jnp.swapaxes(w_s, 1, 2)

    def shard_major(g):
        return g.reshape(N_SHARD, g.shape[0] // N_SHARD, D_MODEL)

    names = ["w_in", "w_out", "w_gate", "w_up", "w_down"]
    (g_in, g_out, g_gate, g_up, g_down, g_convw), bias, bias_t = _gather_weights(
        [shards[n] for n in names], conv_w_shard, rel_bias, buckets)
    w_in_t = g_in.reshape(IN_COLS, D_MODEL)
    conv_w = _from_col_shards(g_convw.reshape(N_SHARD, 3, SHARD_FF))

    h1, uv, qkv, a = _proj_fwd(x, g1, w_in_t, ln_g, ln_b, w_s, bz)
    o_bf, lse, (g_out, g_gate, g_up) = _attn_fwd(qkv, bias, batch, [g_out, g_gate, g_up])
    w_out = g_out.reshape(D_MODEL, D_MODEL)
    w_gate_t = g_gate.reshape(D_FF, D_MODEL)
    w_up_t = g_up.reshape(D_FF, D_MODEL)
    (y1, x1, h2), _ = _fused_rows(
        "out_proj_mid_fwd", 512,
        [(a, w_out, "nn", slice(0, A_WIDTH)), (o_bf, w_out, "nn", slice(A_WIDTH, D_MODEL))],
        [x], [g2, g3], _mid_fwd_rows, [F32, F32, BF16], [])
    gate, up, g_down = _mm_pair_nt(h2, w_gate_t, w_up_t, g_down, tm=1024, tn=1408, out_dtype=BF16,
                                   name="mm_gate_up")
    w_down = g_down.reshape(D_FF, D_MODEL)
    act = _convgate_fwd(gate, up, conv_w, conv_b, batch)
    (dx2, dy2, dg4, loss), _ = _fused_rows(
        "down_proj_loss_head", 512, [(act, w_down, "nn", None)], [x1, tgt], [g4], _loss_head_rows,
        [F32, BF16], [(1, D_MODEL), (1, 128)])

    dact = _mm(dy2, w_down, dims="nt", tm=1024, tn=1408, tk=1024, out_dtype=BF16, name="mm_dact")
    dw_down = _mm(act, dy2, dims="tn", tm=1408, tn=1024, tk=1024, out_dtype=BF16, name="mm_dw_down")
    dgate, dup, dconv_w, dconv_b = _convgate_bwd(gate, up, dact, conv_w, conv_b, batch)
    (dx1, dy1, dg2, dg3), _ = _fused_rows(
        "dh2_mid_bwd", 256, [(dgate, w_gate_t, "nn", None), (dup, w_up_t, "nn", None)],
        [x1, y1, dx2], [g2, g3], _mid_bwd_rows, [F32, BF16], [(1, D_MODEL), (1, D_MODEL)])
    dw_gate_t, dw_up_t = _mm_pair_tn(dgate, dup, h2, tm=1408, tk=1024, name="mm_dw_gate_up")
    done = [shard_major(g) for g in (dw_down, dw_gate_t, dw_up_t)]
    (dmix, dw_out), recv_a = _out_proj_bwd(a, o_bf, dy1, w_out, done[:2])
    done.append(shard_major(dw_out))
    (dproj, dln_g, dln_b, dw_s, dbz), recv_b = _gate_bwd(uv, dmix, ln_g, ln_b, w_s, w_st, bz, done[2:])
    recv_a += recv_b
    parts = _add_halves(done, recv_a, c_idx)
    early = dict(loss=loss, norm_mix_post=dg2, norm_ffn_pre=dg3, norm_ffn_post=dg4, ln_v_gain=dln_g,
                 ln_v_bias=dln_b, spatial_w=dw_s, spatial_b=dbz, conv_w=dconv_w, conv_b=dconv_b)
    dproj, ds, recv, early_sums = _attn_bwd(qkv, dmix, o_bf, lse, bias_t, dproj, batch, parts, list(early.values()))
    fulls = _add_chips(parts, recv, s_idx, c_idx)
    dw_in_t = _mm(dproj, h1, dims="tn", tm=1408, tn=1024, tk=1024, out_dtype=BF16, name="mm_dw_in")
    last = [shard_major(dw_in_t)]
    drel, recv_in_a = _rel_bias_grad(ds, np.ascontiguousarray(np.swapaxes(buckets, 1, 2)), last)
    part_in = _add_halves(last, recv_in_a, c_idx)
    (dx0, dg1), recv_in = _fused_rows(
        "dh1_in_bwd", 512, [(dproj, w_in_t, "nn", None)], [x, dx1], [g1], _in_bwd_rows,
        [F32], [(1, D_MODEL)], exchange=part_in)
    fulls += _add_chips(part_in, recv_in, s_idx, c_idx)
    half_reduced = dict(zip(["w_down", "w_gate", "w_up", "w_out", "w_in"], fulls))

    return dx0, dict(zip(early, early_sums)), dict(norm_mix_pre=dg1, rel_bias=drel), half_reduced


def _adamw_update(w, g, m, v):
    nm = ADAM_B1 * m + (1.0 - ADAM_B1) * g
    nv = ADAM_B2 * v + (1.0 - ADAM_B2) * (g * g)
    m_hat = nm / (1.0 - ADAM_B1 ** ADAM_STEP)
    v_hat = nv / (1.0 - ADAM_B2 ** ADAM_STEP)
    return -ADAM_LR * (m_hat / (jnp.sqrt(v_hat) + ADAM_EPS) + ADAM_WD * w), nm, nv


def _adamw(w, g, m, v, name):
    rows, cols = w.shape
    tr = next(cand for cand in (352, 256, 128) if rows % cand == 0)

    def body(w_ref, g_ref, m_ref, v_ref, go_ref, d_ref, nm_ref, nv_ref):
        gv = g_ref[...]
        go_ref[...] = gv
        d_ref[...], nm_ref[...], nv_ref[...] = _adamw_update(w_ref[...], gv, m_ref[...], v_ref[...])

    spec = pl.BlockSpec((tr, cols), lambda i: (i, 0))
    sds = jax.ShapeDtypeStruct((rows, cols), F32)
    return pl.pallas_call(
        body, grid=(rows // tr,), in_specs=[spec] * 4, out_specs=[spec] * 4, out_shape=[sds] * 4,
        compiler_params=_params(("parallel",)), name=name,
    )(w, g, m, v)


def _adamw_small(ws, gs, ms, vs):
    n = len(ws)

    def body(*refs):
        w_refs, g_refs, m_refs, v_refs = refs[:n], refs[n:2 * n], refs[2 * n:3 * n], refs[3 * n:4 * n]
        d_refs, nm_refs, nv_refs, go_refs = refs[4 * n:5 * n], refs[5 * n:6 * n], refs[6 * n:7 * n], refs[7 * n:]
        for t in range(n):
            g = g_refs[t][...]
            d_refs[t][...], nm_refs[t][...], nv_refs[t][...] = _adamw_update(
                w_refs[t][...], g, m_refs[t][...], v_refs[t][...])
            go_refs[t][...] = g

    vm = pl.BlockSpec(memory_space=pltpu.VMEM)
    sds = [jax.ShapeDtypeStruct(w.shape, F32) for w in ws]
    res = pl.pallas_call(
        body, in_specs=[vm] * (4 * n), out_specs=[vm] * (4 * n), out_shape=sds * 4,
        compiler_params=_params(), name="adamw_small",
    )(*ws, *gs, *ms, *vs)
    return res[:n], res[n:2 * n], res[2 * n:3 * n], res[3 * n:]


SMALL = ["norm_mix_pre", "norm_mix_post", "norm_ffn_pre", "norm_ffn_post", "ln_v_gain", "ln_v_bias",
         "spatial_w", "spatial_b", "rel_bias", "conv_b"]
LARGE = ["w_in", "w_gate", "w_up", "w_down", "w_out"]
TRANSPOSED = ("w_in", "w_gate", "w_up")
ORDER = ["norm_mix_pre", "norm_mix_post", "norm_ffn_pre", "norm_ffn_post", "w_in", "ln_v_gain", "ln_v_bias",
         "spatial_w", "spatial_b", "rel_bias", "w_out", "w_gate", "w_up", "conv_w", "conv_b", "w_down"]


def kernel(x, norm_mix_pre, norm_mix_post, norm_ffn_pre, norm_ffn_post, w_in, ln_v_gain, ln_v_bias, spatial_w, spatial_b, rel_bias, w_out, w_gate, w_up, conv_w, conv_b, w_down, loss_target, m_norm_mix_pre, m_norm_mix_post, m_norm_ffn_pre, m_norm_ffn_post, m_w_in, m_ln_v_gain, m_ln_v_bias, m_spatial_w, m_spatial_b, m_rel_bias, m_w_out, m_w_gate, m_w_up, m_conv_w, m_conv_b, m_w_down, v_norm_mix_pre, v_norm_mix_post, v_norm_ffn_pre, v_norm_ffn_post, v_w_in, v_ln_v_gain, v_ln_v_bias, v_spatial_w, v_spatial_b, v_rel_bias, v_w_out, v_w_gate, v_w_up, v_conv_w, v_conv_b, v_w_down):
    params = dict(norm_mix_pre=norm_mix_pre, norm_mix_post=norm_mix_post, norm_ffn_pre=norm_ffn_pre,
                  norm_ffn_post=norm_ffn_post, w_in=w_in, ln_v_gain=ln_v_gain, ln_v_bias=ln_v_bias,
                  spatial_w=spatial_w, spatial_b=spatial_b, rel_bias=rel_bias, w_out=w_out, w_gate=w_gate,
                  w_up=w_up, conv_w=conv_w, conv_b=conv_b, w_down=w_down)
    mom = dict(norm_mix_pre=m_norm_mix_pre, norm_mix_post=m_norm_mix_post, norm_ffn_pre=m_norm_ffn_pre,
               norm_ffn_post=m_norm_ffn_post, w_in=m_w_in, ln_v_gain=m_ln_v_gain, ln_v_bias=m_ln_v_bias,
               spatial_w=m_spatial_w, spatial_b=m_spatial_b, rel_bias=m_rel_bias, w_out=m_w_out, w_gate=m_w_gate,
               w_up=m_w_up, conv_w=m_conv_w, conv_b=m_conv_b, w_down=m_w_down)
    var = dict(norm_mix_pre=v_norm_mix_pre, norm_mix_post=v_norm_mix_post, norm_ffn_pre=v_norm_ffn_pre,
               norm_ffn_post=v_norm_ffn_post, w_in=v_w_in, ln_v_gain=v_ln_v_gain, ln_v_bias=v_ln_v_bias,
               spatial_w=v_spatial_w, spatial_b=v_spatial_b, rel_bias=v_rel_bias, w_out=v_w_out, w_gate=v_w_gate,
               w_up=v_w_up, conv_w=v_conv_w, conv_b=v_conv_b, w_down=v_w_down)

    batch = x.shape[0]
    xi, yi, ci = lax.axis_index("x"), lax.axis_index("y"), lax.axis_index("c")
    s_idx = (2 * xi + yi).astype(jnp.int32).reshape(1)
    c_idx = ci.astype(jnp.int32).reshape(1)

    def local(a, n):
        return jnp.swapaxes(a[0], 0, 1) if n in TRANSPOSED else a[0]

    shards = {n: local(params[n], n) for n in LARGE}
    dx0, total, partial, half_reduced = _train_step(
        x.reshape(batch * SEQ, D_MODEL), loss_target.reshape(batch * SEQ, D_MODEL),
        norm_mix_pre, norm_mix_post, norm_ffn_pre, norm_ffn_post, shards,
        ln_v_gain.reshape(1, A_WIDTH), ln_v_bias.reshape(1, A_WIDTH), spatial_w[0], spatial_b[0], rel_bias,
        jnp.swapaxes(conv_w, 0, 1), conv_b, batch, s_idx, c_idx)
    grad_x = dx0.reshape(batch, SEQ, D_MODEL)

    names = list(partial)
    fulls, sums = _finish_reductions([half_reduced[n] for n in LARGE], [partial[n] for n in names])
    reduced = dict(zip(LARGE, fulls))
    total.update(zip(names, sums))
    loss = total["loss"][0, 0]
    total["spatial_b"] = total["spatial_b"][:, ::HEAD_DIM].T
    total["rel_bias"] = total["rel_bias"].reshape(B_HEADS, NUM_BUCKETS)
    total["conv_w"] = lax.dynamic_slice_in_dim(total["conv_w"], s_idx[0] * SHARD_FF, SHARD_FF, axis=1)
    small_names = SMALL + ["conv_w"]

    def small(a, n):
        return jnp.swapaxes(a, 0, 1) if n in ("rel_bias", "conv_w") else a

    for n in small_names:
        reduced[n] = total[n].reshape(small(params[n], n).shape)

    out_g, out_d, out_m, out_v = {}, {}, {}, {}
    for n in LARGE:
        res = _adamw(local(params[n], n), reduced[n], local(mom[n], n), local(var[n], n), name=f"adamw_{n}")
        if n in TRANSPOSED:
            res = [jnp.swapaxes(r, 0, 1) for r in res]
        out_g[n], out_d[n], out_m[n], out_v[n] = [r[None] for r in res]
    d, nm, nv, gg = _adamw_small([small(params[n], n) for n in small_names], [reduced[n] for n in small_names],
                                 [small(mom[n], n) for n in small_names], [small(var[n], n) for n in small_names])
    for n, g, dd, mm, vv in zip(small_names, gg, d, nm, nv):
        out_g[n], out_d[n], out_m[n], out_v[n] = [small(r, n) for r in (g, dd, mm, vv)]

    return (loss, grad_x, *[out_g[n] for n in ORDER], *[out_d[n] for n in ORDER],
            *[out_m[n] for n in ORDER], *[out_v[n] for n in ORDER])
```

```python
import functools
import math

import numpy as np
import jax
import jax.numpy as jnp
from jax import lax
from jax.experimental import pallas as pl
from jax.experimental.pallas import tpu as pltpu

F32 = jnp.float32
BF16 = jnp.bfloat16
MESH = pl.DeviceIdType.MESH

D_MODEL = 1024
SEQ = 2048
HEAD_DIM = 64
A_GROUPS = 4
A_WIDTH = 256
B_HEADS = 12
B_WIDTH = 768
CHUNK = 128
DILATED = ((128, 1), (512, 4), (2048, 16))
NUM_BUCKETS = 32
MAX_DISTANCE = 2048
D_FF = 2816
IN_COLS = 2816
NORM_EPS = 1e-6
NEG_INF = -1e30
N_SHARD = 4
SHARD_FF = D_FF // N_SHARD
LANE_BLOCK = 256
VMEM_LIMIT = 56 * 1024 * 1024

ADAM_LR = 0.001
ADAM_B1 = 0.9
ADAM_B2 = 0.999
ADAM_EPS = 1e-08
ADAM_WD = 0.01
ADAM_STEP = 10

GELU_C = math.sqrt(2.0 / math.pi)
GELU_A = 0.044715

ANY = pl.BlockSpec(memory_space=pl.ANY)


def _params(sem=None):
    return pltpu.CompilerParams(dimension_semantics=sem, vmem_limit_bytes=VMEM_LIMIT)


def _dot(a, b, precision=None):
    return jnp.dot(a, b, preferred_element_type=F32, precision=precision)


def _dot_nt(a, b, precision=None):
    return lax.dot_general(a, b, (((1,), (1,)), ((), ())), preferred_element_type=F32, precision=precision)


def _dot_tn(a, b):
    return lax.dot_general(a, b, (((0,), (0,)), ((), ())), preferred_element_type=F32)


def _gelu(x):
    t = jnp.tanh(x * (GELU_C + (GELU_C * GELU_A) * (x * x)))
    return (0.5 * x) * (1.0 + t)


def _gelu_and_grad(x):
    x2 = x * x
    u = 1.0 + jnp.tanh(x * (GELU_C + (GELU_C * GELU_A) * x2))
    hx = 0.5 * x
    dg = u * (0.5 + hx * (2.0 - u) * (GELU_C + (3.0 * GELU_C * GELU_A) * x2))
    return hx * u, dg


def _mesh_pos():
    x, y, c = lax.axis_index("x"), lax.axis_index("y"), lax.axis_index("c")
    chips = [(1 - x, y), (x, 1 - y), (1 - x, 1 - y)]
    return x, y, c, chips


class _GatherPlan:
    def __init__(self, shapes, out_refs, send_sems, recv_sems):
        self.shapes, self.out_refs = shapes, out_refs
        self.send_sems, self.recv_sems = send_sems, recv_sems
        self.x, self.y, self.c, self.chips = _mesh_pos()
        self.sib = (self.x, self.y, 1 - self.c)

    def _half(self, t, chip, which):
        rows = self.shapes[t][0] // 2
        return self.out_refs[t].at[2 * chip[0] + chip[1], pl.ds(which * rows, rows), :]

    def _copy(self, k, src, dst, to):
        return pltpu.make_async_remote_copy(src_ref=src, dst_ref=dst, send_sem=self.send_sems.at[k],
                                            recv_sem=self.recv_sems.at[k], device_id=to, device_id_type=MESH)

    def _sends(self, t):
        own = self._half(t, (self.x, self.y), self.c)
        return [self._copy(6 * t + j, own, own, (*chip, self.c)) for j, chip in enumerate(self.chips)]

    def _forwards(self, t):
        return [self._copy(6 * t + 3 + j, self._half(t, chip, self.c), self._half(t, chip, self.c), self.sib)
                for j, chip in enumerate(self.chips)]

    def start(self, ts):
        for t in ts:
            for cp in self._sends(t):
                cp.start()

    def forward(self, ts):
        for t in ts:
            for j, chip in enumerate(self.chips):
                landed = self._half(t, chip, self.c)
                self._copy(6 * t + j, landed, landed, (*chip, self.c)).wait_recv()
            for cp in self._forwards(t):
                cp.start()

    def finish(self, ts):
        for t in ts:
            for j, chip in enumerate(self.chips):
                other = self._half(t, chip, 1 - self.c)
                self._copy(6 * t + 3 + j, other, other, self.sib).wait_recv()
        for t in ts:
            for cp in self._sends(t) + self._forwards(t):
                cp.wait_send()


class _RelayGatherPlan:
    def __init__(self, shapes, shard_refs, out_refs, send_sems, recv_sems):
        self.shapes, self.shard_refs, self.out_refs = shapes, shard_refs, out_refs
        self.send_sems, self.recv_sems = send_sems, recv_sems
        x, y, c, self.chips = _mesh_pos()
        self.me, self.c, self.sib = (x, y), c, (x, y, 1 - c)
        self.first = (x + c - 2 * x * c, y + (1 - c) - 2 * y * (1 - c))
        self.second = (x + (1 - c) - 2 * x * (1 - c), y + c - 2 * y * c)
        self.diag = (1 - x, 1 - y)

    def _half(self, t, chip, which):
        rows = self.shapes[t][0] // 2
        return self.out_refs[t].at[2 * chip[0] + chip[1], pl.ds(which * rows, rows), :]

    def _copy(self, k, src, dst, to):
        return pltpu.make_async_remote_copy(src_ref=src, dst_ref=dst, send_sem=self.send_sems.at[k],
                                            recv_sem=self.recv_sems.at[k], device_id=to, device_id_type=MESH)

    def _own(self, t):
        if self.shard_refs is None:
            return self._half(t, self.me, self.c)
        rows = self.shapes[t][0] // 2
        return self.shard_refs[t].at[pl.ds(self.c * rows, rows), :]

    def _step1(self, t):
        return self._copy(6 * t, self._own(t), self._half(t, self.me, self.c), (*self.first, self.c))

    def _step2(self, t):
        landed = self._half(t, self.first, self.c)
        return [self._copy(6 * t + 1, self._own(t), self._half(t, self.me, self.c), (*self.second, self.c)),
                self._copy(6 * t + 2, landed, landed, (*self.second, self.c))]

    def _forwards(self, t):
        return [self._copy(6 * t + 3 + j, self._half(t, chip, self.c), self._half(t, chip, self.c), self.sib)
                for j, chip in enumerate(self.chips)]

    def start(self, ts):
        for t in ts:
            self._step1(t).start()
            self._step2(t)[0].start()

    def relay(self, ts):
        for t in ts:
            landed = self._half(t, self.first, self.c)
            self._copy(6 * t, landed, landed, self.sib).wait_recv()
            self._step2(t)[1].start()

    def forward(self, ts):
        for t in ts:
            for k, chip in ((1, self.second), (2, self.diag)):
                landed = self._half(t, chip, self.c)
                self._copy(6 * t + k, landed, landed, self.sib).wait_recv()
            for cp in self._forwards(t):
                cp.start()

    def finish(self, ts):
        for t in ts:
            for j, chip in enumerate(self.chips):
                other = self._half(t, chip, 1 - self.c)
                self._copy(6 * t + 3 + j, other, other, self.sib).wait_recv()
        for t in ts:
            for cp in [self._step1(t)] + self._step2(t) + self._forwards(t):
                cp.wait_send()


class _SiblingExchangePlan:
    def __init__(self, shapes, grad_refs, out_refs, send_sems, recv_sems):
        self.shapes, self.grad_refs, self.out_refs = shapes, grad_refs, out_refs
        self.send_sems, self.recv_sems = send_sems, recv_sems
        self.x, self.y, self.c, _ = _mesh_pos()

    def _copies(self):
        out = []
        for t, (g, o) in enumerate(zip(self.grad_refs, self.out_refs)):
            rows = self.shapes[t][1] // 2
            out.append(pltpu.make_async_remote_copy(
                src_ref=g.at[:, pl.ds((1 - self.c) * rows, rows), :], dst_ref=o, send_sem=self.send_sems.at[t],
                recv_sem=self.recv_sems.at[t], device_id=(self.x, self.y, 1 - self.c), device_id_type=MESH))
        return out

    def start(self):
        for cp in self._copies():
            cp.start()

    def finish(self):
        for cp in self._copies():
            cp.wait()


class _ChipExchangePlan:
    def __init__(self, part_refs, out_refs, send_sems, recv_sems):
        self.part_refs, self.out_refs, self.send_sems, self.recv_sems = part_refs, out_refs, send_sems, recv_sems
        _, _, self.c, self.chips = _mesh_pos()

    def _copies(self):
        return [pltpu.make_async_remote_copy(
            src_ref=p.at[2 * chip[0] + chip[1]], dst_ref=o.at[j], send_sem=self.send_sems.at[3 * t + j],
            recv_sem=self.recv_sems.at[3 * t + j], device_id=(*chip, self.c), device_id_type=MESH)
            for t, (p, o) in enumerate(zip(self.part_refs, self.out_refs)) for j, chip in enumerate(self.chips)]

    def start(self):
        for cp in self._copies():
            cp.start()

    def finish(self):
        for cp in self._copies():
            cp.wait()


class _SmallAllReducePlan:
    def __init__(self, in_refs, out_refs, sib_refs, chip_refs, send_sems, recv_sems):
        self.in_refs, self.out_refs, self.sib_refs, self.chip_refs = in_refs, out_refs, sib_refs, chip_refs
        self.send_sems, self.recv_sems = send_sems, recv_sems
        self.n = len(in_refs)
        self.x, self.y, self.c, self.chips = _mesh_pos()

    def _copy(self, k, src, dst, to):
        return pltpu.make_async_remote_copy(src_ref=src, dst_ref=dst, send_sem=self.send_sems.at[k],
                                            recv_sem=self.recv_sems.at[k], device_id=to, device_id_type=MESH)

    def _first(self):
        return [self._copy(t, self.in_refs[t], self.sib_refs[t], (self.x, self.y, 1 - self.c)) for t in range(self.n)]

    def _second(self):
        return [self._copy(self.n + 3 * t + j, self.out_refs[t], self.chip_refs[t].at[j], (*chip, self.c))
                for t in range(self.n) for j, chip in enumerate(self.chips)]

    def start_sibling(self):
        for cp in self._first():
            cp.start()

    def sum_sibling_and_start_chips(self):
        for cp in self._first():
            cp.wait()
        for t in range(self.n):
            self.out_refs[t][...] = self.in_refs[t][...] + self.sib_refs[t][...]
        for cp in self._second():
            cp.start()

    def finish(self):
        for cp in self._second():
            cp.wait()
        for t in range(self.n):
            self.out_refs[t][...] = ((self.out_refs[t][...] + self.chip_refs[t][0])
                                     + (self.chip_refs[t][1] + self.chip_refs[t][2]))

    @staticmethod
    def scratch(arrays):
        return ([pltpu.VMEM(a.shape, F32) for a in arrays] + [pltpu.VMEM((3,) + a.shape, F32) for a in arrays]
                + _sem_pair(4 * len(arrays)))


def _sem_pair(n):
    return [pltpu.SemaphoreType.DMA((n,)), pltpu.SemaphoreType.DMA((n,))]


def _mm(a, b, *, dims, tm, tn, tk, out_dtype, name):
    if dims == "nn":
        m, k = a.shape
        n = b.shape[1]
        a_spec = pl.BlockSpec((tm, tk), lambda i, j, kk: (i, kk))
        b_spec = pl.BlockSpec((tk, tn), lambda i, j, kk: (kk, j))
        dot = _dot
    elif dims == "nt":
        m, k = a.shape
        n = b.shape[0]
        a_spec = pl.BlockSpec((tm, tk), lambda i, j, kk: (i, kk))
        b_spec = pl.BlockSpec((tn, tk), lambda i, j, kk: (j, kk))
        dot = _dot_nt
    else:
        k, m = a.shape
        n = b.shape[1]
        a_spec = pl.BlockSpec((tk, tm), lambda i, j, kk: (kk, i))
        b_spec = pl.BlockSpec((tk, tn), lambda i, j, kk: (kk, j))
        dot = _dot_tn
    assert m % tm == 0 and n % tn == 0 and k % tk == 0, (name, m, n, k)
    grid = (m // tm, n // tn, k // tk)
    nk = grid[2]
    own_acc = nk > 1 and out_dtype != F32

    def body(a_ref, b_ref, o_ref, *scratch):
        prod = dot(a_ref[...].astype(BF16), b_ref[...].astype(BF16))
        if nk == 1:
            o_ref[...] = prod.astype(out_dtype)
        else:
            acc_ref = scratch[0] if own_acc else o_ref
            kk = pl.program_id(2)

            @pl.when(kk == 0)
            def _():
                acc_ref[...] = prod

            @pl.when(kk > 0)
            def _():
                acc_ref[...] += prod

            if own_acc:
                @pl.when(kk == nk - 1)
                def _():
                    o_ref[...] = acc_ref[...].astype(out_dtype)

    return pl.pallas_call(
        body, grid=grid, in_specs=[a_spec, b_spec],
        out_specs=pl.BlockSpec((tm, tn), lambda i, j, kk: (i, j)),
        out_shape=jax.ShapeDtypeStruct((m, n), out_dtype),
        scratch_shapes=[pltpu.VMEM((tm, tn), F32)] if own_acc else [],
        compiler_params=_params(("parallel", "parallel", "arbitrary")), name=name,
    )(a, b)


def _mm_pair_tn(a1, a2, b, *, tm, tk, name):
    k, m = a1.shape
    n = b.shape[1]
    assert m % tm == 0 and k % tk == 0 and a2.shape == a1.shape, name
    nk = k // tk

    def body(a1_ref, a2_ref, b_ref, o1_ref, o2_ref, acc1_ref, acc2_ref):
        bv = b_ref[...]
        p1 = _dot_tn(a1_ref[...], bv)
        p2 = _dot_tn(a2_ref[...], bv)
        kk = pl.program_id(1)

        @pl.when(kk == 0)
        def _():
            acc1_ref[...] = p1
            acc2_ref[...] = p2

        @pl.when(kk > 0)
        def _():
            acc1_ref[...] += p1
            acc2_ref[...] += p2

        @pl.when(kk == nk - 1)
        def _():
            o1_ref[...] = acc1_ref[...].astype(BF16)
            o2_ref[...] = acc2_ref[...].astype(BF16)

    a_spec = pl.BlockSpec((tk, tm), lambda i, kk: (kk, i))
    o_spec = pl.BlockSpec((tm, n), lambda i, kk: (i, 0))
    return pl.pallas_call(
        body, grid=(m // tm, nk),
        in_specs=[a_spec, a_spec, pl.BlockSpec((tk, n), lambda i, kk: (kk, 0))],
        out_specs=[o_spec, o_spec],
        out_shape=[jax.ShapeDtypeStruct((m, n), BF16)] * 2,
        scratch_shapes=[pltpu.VMEM((tm, n), F32)] * 2,
        compiler_params=_params(("parallel", "arbitrary")), name=name,
    )(a1, a2, b)


def _mm_pair_nt(a, w1_t, w2_t, own, *, tm, tn, out_dtype, name):
    m, k = a.shape
    n = w1_t.shape[0]
    assert m % tm == 0 and n % tn == 0 and w2_t.shape == w1_t.shape, name
    grid = (m // tm, n // tn)
    n_steps = grid[0] * grid[1]

    def body(a_ref, w1_ref, w2_ref, own_ref, o1_ref, o2_ref, gat_ref, send_sems, recv_sems):
        del own_ref
        step = pl.program_id(0) * grid[1] + pl.program_id(1)
        gather = _GatherPlan([own.shape[1:]], [gat_ref], send_sems, recv_sems)

        @pl.when(step == 0)
        def _():
            gather.start([0])

        @pl.when(step == (2 * n_steps) // 3)
        def _():
            gather.forward([0])

        av = a_ref[...]
        o1_ref[...] = _dot_nt(av, w1_ref[...]).astype(out_dtype)
        o2_ref[...] = _dot_nt(av, w2_ref[...]).astype(out_dtype)

        @pl.when(step == n_steps - 1)
        def _():
            gather.finish([0])

    w_spec = pl.BlockSpec((tn, k), lambda i, j: (j, 0))
    o_spec = pl.BlockSpec((tm, tn), lambda i, j: (i, j))
    return pl.pallas_call(
        body, grid=grid,
        in_specs=[pl.BlockSpec((tm, k), lambda i, j: (i, 0)), w_spec, w_spec, ANY],
        out_specs=[o_spec, o_spec, ANY],
        out_shape=[jax.ShapeDtypeStruct((m, n), out_dtype)] * 2 + [jax.ShapeDtypeStruct(own.shape, own.dtype)],
        scratch_shapes=_sem_pair(6), input_output_aliases={3: 2},
        compiler_params=_params(("arbitrary", "arbitrary")), name=name,
    )(a, w1_t, w2_t, own)


def _out_proj_bwd(a, o, dy1, w_out, grads):
    m = dy1.shape[0]
    tm = 1024
    nx = len(grads)
    shapes = [g.shape for g in grads]
    n_steps = m // tm

    def body(a_ref, o_ref, dy_ref, w_ref, *rest):
        grad_refs = rest[:nx]
        dmix_ref, dw_ref = rest[nx:nx + 2]
        acc_ref = rest[2 * nx + 2]
        exchange = _SiblingExchangePlan(shapes, grad_refs, rest[nx + 2:2 * nx + 2], *rest[2 * nx + 3:])

        @pl.when(pl.program_id(0) == 0)
        def _():
            exchange.start()

        dy = dy_ref[...]
        dmix_ref[...] = _dot_nt(dy, w_ref[...])
        top = _dot_tn(a_ref[...], dy)
        bottom = _dot_tn(o_ref[...], dy)

        @pl.when(pl.program_id(0) == 0)
        def _():
            acc_ref[:A_WIDTH, :] = top
            acc_ref[A_WIDTH:, :] = bottom

        @pl.when(pl.program_id(0) > 0)
        def _():
            acc_ref[:A_WIDTH, :] += top
            acc_ref[A_WIDTH:, :] += bottom

        @pl.when(pl.program_id(0) == n_steps - 1)
        def _():
            dw_ref[...] = acc_ref[...].astype(BF16)
            exchange.finish()

    tile = lambda width: pl.BlockSpec((tm, width), lambda i: (i, 0))
    res = pl.pallas_call(
        body, grid=(n_steps,),
        in_specs=[tile(A_WIDTH), tile(B_WIDTH), tile(D_MODEL), _full_spec((D_MODEL, D_MODEL))] + [ANY] * nx,
        out_specs=[tile(D_MODEL), _full_spec((D_MODEL, D_MODEL))] + [ANY] * nx,
        out_shape=[jax.ShapeDtypeStruct((m, D_MODEL), F32), jax.ShapeDtypeStruct((D_MODEL, D_MODEL), BF16)]
        + [jax.ShapeDtypeStruct((N_SHARD, s[1] // 2, s[2]), g.dtype) for s, g in zip(shapes, grads)],
        scratch_shapes=[pltpu.VMEM((D_MODEL, D_MODEL), F32)] + _sem_pair(nx),
        compiler_params=_params(("arbitrary",)), name="out_proj_bwd",
    )(a, o, dy1, w_out, *grads)
    return res[:2], list(res[2:])


def _fused_rows(name, tm, mats, rows, vecs, fn, row_outs, acc_outs, exchange=()):
    m = mats[0][0].shape[0]
    nm, nr, nv, nro, nao, nx = len(mats), len(rows), len(vecs), len(row_outs), len(acc_outs), len(exchange)
    n_steps = m // tm

    def body(*refs):
        a_refs, w_refs = refs[:nm], refs[nm:2 * nm]
        pos = 2 * nm
        row_refs, vec_refs, part_refs = refs[pos:pos + nr], refs[pos + nr:pos + nr + nv], refs[pos + nr + nv:pos + nr + nv + nx]
        pos += nr + nv + nx
        out_refs, acc_refs, recv_refs = refs[pos:pos + nro], refs[pos + nro:pos + nro + nao], refs[pos + nro + nao:pos + nro + nao + nx]
        sems = refs[pos + nro + nao + nx:]
        i = pl.program_id(0)
        if nx:
            plan = _ChipExchangePlan(part_refs, recv_refs, *sems)

            @pl.when(i == 0)
            def _():
                plan.start()

        @pl.when(i == 0)
        def _():
            for r in acc_refs:
                r[...] = jnp.zeros_like(r)

        y = None
        for a_ref, w_ref, (_, _, dims, sl) in zip(a_refs, w_refs, mats):
            w = w_ref[...] if sl is None else w_ref[sl, :]
            part = (_dot if dims == "nn" else _dot_nt)(a_ref[...], w)
            y = part if y is None else y + part
        res = fn(y, *[r[...] for r in row_refs], *[v[...] for v in vec_refs])
        for r, val in zip(out_refs, res[:nro]):
            r[...] = val.astype(r.dtype)
        for r, val in zip(acc_refs, res[nro:]):
            r[...] += val

        if nx:
            @pl.when(i == n_steps - 1)
            def _():
                plan.finish()

    tile = lambda width: pl.BlockSpec((tm, width), lambda i: (i, 0))
    res = pl.pallas_call(
        body, grid=(n_steps,),
        in_specs=[tile(a.shape[1]) for a, _, _, _ in mats] + [_full_spec(w.shape) for _, w, _, _ in mats]
        + [tile(D_MODEL)] * nr + [_full_spec((1, D_MODEL))] * nv + [ANY] * nx,
        out_specs=[tile(D_MODEL)] * nro + [_full_spec(s) for s in acc_outs] + [ANY] * nx,
        out_shape=[jax.ShapeDtypeStruct((m, D_MODEL), dt) for dt in row_outs]
        + [jax.ShapeDtypeStruct(s, F32) for s in acc_outs]
        + [jax.ShapeDtypeStruct((3,) + p.shape[1:], p.dtype) for p in exchange],
        scratch_shapes=_sem_pair(3 * nx) if nx else [],
        compiler_params=_params(("arbitrary",)), name=name,
    )(*[a for a, _, _, _ in mats], *[w for _, w, _, _ in mats], *rows, *vecs, *exchange)
    return list(res[:nro + nao]), list(res[nro + nao:])


def _vec_spec(width=D_MODEL):
    return pl.BlockSpec((1, width), lambda i: (0, 0))


def _rstd(v):
    return lax.rsqrt(jnp.mean(v * v, axis=-1, keepdims=True) + NORM_EPS)


def _mid_fwd_rows(y1, x0, g2, g3):
    x1 = x0 + y1 * _rstd(y1) * g2
    return y1, x1, x1 * _rstd(x1) * g3


def _rms_bwd_rows(dout, v, g):
    r = _rstd(v)
    n = v * r
    dn = dout * g
    dv = r * (dn - n * jnp.mean(dn * n, axis=-1, keepdims=True))
    dg = jnp.sum(dout * n, axis=0, keepdims=True)
    return dv, dg


def _loss_head_rows(y2, x1, tgt, g4):
    x2 = x1 + y2 * _rstd(y2) * g4
    err = x2 - tgt
    loss = 0.5 * jnp.sum(jnp.mean(err * err, axis=-1, keepdims=True), axis=0, keepdims=True)
    dx2 = err * (1.0 / D_MODEL)
    dy2, dg4 = _rms_bwd_rows(dx2, y2, g4)
    return dx2, dy2, dg4, loss


def _mid_bwd_rows(dh2, x1, y1, dx2, g2, g3):
    d3, dg3 = _rms_bwd_rows(dh2, x1, g3)
    dx1 = dx2 + d3
    dy1, dg2 = _rms_bwd_rows(dx1, y1, g2)
    return dx1, dy1, dg2, dg3


def _in_bwd_rows(dh1, x0, dx1, g1):
    d1, dg1 = _rms_bwd_rows(dh1, x0, g1)
    return dx1 + d1, dg1


GATE_ROWS = 512


def _group_mean_matrix():
    p = np.zeros((A_WIDTH, A_WIDTH), np.float32)
    for g in range(A_GROUPS):
        p[g * HEAD_DIM:(g + 1) * HEAD_DIM, g * HEAD_DIM:(g + 1) * HEAD_DIM] = 1.0 / HEAD_DIM
    return jnp.asarray(p)


def _group_masks(width=A_WIDTH):
    lane = lax.broadcasted_iota(jnp.int32, (1, width), 1)
    return [(lane >= g * HEAD_DIM) & (lane < (g + 1) * HEAD_DIM) for g in range(width // HEAD_DIM)]


GROUP_SUM_PRECISION = lax.Precision.HIGH


def _layernorm_groups(vg, pavg):
    hi = GROUP_SUM_PRECISION
    mu = _dot(vg, pavg, hi)
    xc = vg - mu
    var = _dot(xc * xc, pavg, hi)
    rstd = lax.rsqrt(var + NORM_EPS)
    return xc * rstd, rstd


def _spatial_mix(w_bf, vn_chunk_bf, masks, bz):
    z = bz
    for g in range(A_GROUPS):
        z = z + jnp.where(masks[g], _dot(w_bf[g], vn_chunk_bf), 0.0)
    return z


def _full_spec(shape):
    return pl.BlockSpec(shape, lambda i: tuple(0 for _ in shape))


def _gate_fwd_rows(u, v, lg, lb, w_ref, bz, pavg, a_ref):
    masks = _group_masks()
    row = lax.broadcasted_iota(jnp.int32, (CHUNK, CHUNK), 0)
    col = lax.broadcasted_iota(jnp.int32, (CHUNK, CHUNK), 1)
    w_bf = [jnp.where(row >= col, w_ref[g], 0.0).astype(BF16) for g in range(A_GROUPS)]
    ug = _gelu(u)
    vhat, _ = _layernorm_groups(_gelu(v), pavg)
    vn = vhat * lg + lb
    for c in range(GATE_ROWS // CHUNK):
        sl = slice(c * CHUNK, (c + 1) * CHUNK)
        z = _spatial_mix(w_bf, vn[sl].astype(BF16), masks, bz)
        a_ref[sl, :] = (ug[sl] * z).astype(BF16)


def _gate_bwd(uv, dmix, ln_g, ln_b, w_s, w_st, bz, grads):
    m = uv.shape[0]
    pavg = _group_mean_matrix()
    nsteps = m // GATE_ROWS
    nx = len(grads)
    shapes = [g.shape for g in grads]

    def body(u_ref, v_ref, da_ref, lg_ref, lb_ref, w_ref, wt_ref, bz_ref, p_ref, *rest):
        grad_refs = rest[:nx]
        duv_ref, dlg_ref, dlb_ref, dw_ref, dbz_ref = rest[nx:nx + 5]
        recv_refs = rest[nx + 5:2 * nx + 5]
        exchange = _SiblingExchangePlan(shapes, grad_refs, recv_refs, *rest[2 * nx + 5:])
        i = pl.program_id(0)

        @pl.when(i == 0)
        def _():
            exchange.start()
            dlg_ref[...] = jnp.zeros_like(dlg_ref)
            dlb_ref[...] = jnp.zeros_like(dlb_ref)
            dw_ref[...] = jnp.zeros_like(dw_ref)
            dbz_ref[...] = jnp.zeros_like(dbz_ref)

        hi = GROUP_SUM_PRECISION
        masks = _group_masks()
        row = lax.broadcasted_iota(jnp.int32, (CHUNK, CHUNK), 0)
        col = lax.broadcasted_iota(jnp.int32, (CHUNK, CHUNK), 1)
        tril = row >= col
        w_bf = [jnp.where(tril, w_ref[g], 0.0).astype(BF16) for g in range(A_GROUPS)]
        wt_bf = [jnp.where(col >= row, wt_ref[g], 0.0).astype(BF16) for g in range(A_GROUPS)]
        pavg_v = p_ref[...]
        lg = lg_ref[...]
        ug, dug = _gelu_and_grad(u_ref[...])
        vg, dvg_dx = _gelu_and_grad(v_ref[...])
        vhat, rstd = _layernorm_groups(vg, pavg_v)
        vn = vhat * lg + lb_ref[...]
        da = da_ref[...]
        bz = bz_ref[...]
        for c in range(GATE_ROWS // CHUNK):
            sl = slice(c * CHUNK, (c + 1) * CHUNK)
            vn_bf = vn[sl].astype(BF16)
            z = _spatial_mix(w_bf, vn_bf, masks, bz)
            dz = da[sl] * ug[sl]
            duv_ref[sl, 0:A_WIDTH] = (da[sl] * z * dug[sl]).astype(BF16)
            dbz_ref[...] += dz
            dz_bf = dz.astype(BF16)
            dvn = jnp.zeros((CHUNK, A_WIDTH), F32)
            for g in range(A_GROUPS):
                dz_g = jnp.where(masks[g], dz, 0.0).astype(BF16)
                dw_ref[g] += jnp.where(tril, _dot_nt(dz_g, vn_bf), 0.0)
                dvn = dvn + jnp.where(masks[g], _dot(wt_bf[g], dz_bf), 0.0)
            vh = vhat[sl]
            dlb_ref[...] += jnp.sum(dvn, axis=0, keepdims=True)
            dlg_ref[...] += jnp.sum(dvn * vh, axis=0, keepdims=True)
            dvh = dvn * lg
            m1 = _dot(dvh, pavg_v, hi)
            m2 = _dot(dvh * vh, pavg_v, hi)
            duv_ref[sl, A_WIDTH:2 * A_WIDTH] = (rstd[sl] * (dvh - m1 - vh * m2) * dvg_dx[sl]).astype(BF16)

        @pl.when(i == nsteps - 1)
        def _():
            dbz_ref[...] = _dot(dbz_ref[...], pavg_v * float(HEAD_DIM), hi)
            exchange.finish()

    res = pl.pallas_call(
        body, grid=(nsteps,),
        in_specs=[pl.BlockSpec((GATE_ROWS, A_WIDTH), lambda i: (i, 0)),
                  pl.BlockSpec((GATE_ROWS, A_WIDTH), lambda i: (i, 1)),
                  pl.BlockSpec((GATE_ROWS, A_WIDTH), lambda i: (i, 0)),
                  _full_spec((1, A_WIDTH)), _full_spec((1, A_WIDTH)), _full_spec((A_GROUPS, CHUNK, CHUNK)),
                  _full_spec((A_GROUPS, CHUNK, CHUNK)), _full_spec((CHUNK, A_WIDTH)),
                  _full_spec((A_WIDTH, A_WIDTH))] + [ANY] * nx,
        out_specs=[pl.BlockSpec((GATE_ROWS, 2 * A_WIDTH), lambda i: (i, 0)),
                   _full_spec((1, A_WIDTH)), _full_spec((1, A_WIDTH)), _full_spec((A_GROUPS, CHUNK, CHUNK)),
                   _full_spec((CHUNK, A_WIDTH))] + [ANY] * nx,
        out_shape=[jax.ShapeDtypeStruct((m, IN_COLS), BF16),
                   jax.ShapeDtypeStruct((1, A_WIDTH), F32), jax.ShapeDtypeStruct((1, A_WIDTH), F32),
                   jax.ShapeDtypeStruct((A_GROUPS, CHUNK, CHUNK), F32),
                   jax.ShapeDtypeStruct((CHUNK, A_WIDTH), F32)]
        + [jax.ShapeDtypeStruct((N_SHARD, s[1] // 2, s[2]), g.dtype) for s, g in zip(shapes, grads)],
        scratch_shapes=_sem_pair(nx),
        compiler_params=_params(("arbitrary",)), name="gate_bwd",
    )(uv, uv, dmix, ln_g, ln_b, w_s, w_st, bz, pavg, *grads)
    return res[:5], list(res[5:])


Q_BLOCK = 128
PAIR = 2 * HEAD_DIM
N_PAIR = B_HEADS // 2
N_CFG = len(DILATED)
BLOCKS_PER_CFG = SEQ // Q_BLOCK
QKV_SLABS = 3 * N_PAIR
FWD_BLOCKS_PER_TRIP = 8
BWD_BLOCKS_PER_TRIP = 4


def _t5_bucket_np(dist, dtype):
    max_exact = NUM_BUCKETS // 2
    d = np.maximum(dist, 1).astype(dtype)
    large = max_exact + (np.log(d / dtype(max_exact)) / dtype(math.log(MAX_DISTANCE / max_exact))
                         * dtype(NUM_BUCKETS - max_exact))
    large = np.minimum(large.astype(np.int32), NUM_BUCKETS - 1)
    return np.where(dist < max_exact, dist, large)


def _bucket_tables():
    i = np.arange(Q_BLOCK)[:, None]
    j = np.arange(Q_BLOCK)[None, :]
    tables = []
    for _, dil in DILATED:
        rel_prev = Q_BLOCK + i - j
        rel_cur = i - j
        rel = np.concatenate([rel_prev, rel_cur], axis=1)
        valid = np.concatenate([rel_prev <= Q_BLOCK, rel_cur >= 0], axis=1)
        dist = np.maximum(rel, 0) * dil
        b32 = _t5_bucket_np(dist, np.float32)
        b64 = _t5_bucket_np(dist, np.float64)
        assert np.array_equal(b32, b64)
        tables.append(np.where(valid, b32, -1).astype(np.int32))
    return np.stack(tables)


def _present_buckets(buckets_np):
    return [sorted(set(int(v) for v in np.unique(buckets_np[c]) if v >= 0)) for c in range(N_CFG)]


def _bias_tables_body(buckets_np):
    present = _present_buckets(buckets_np)

    def tables(rb_ref, bk_ref, o_ref, ot_ref):
        for c in range(N_CFG):
            bk = bk_ref[c]
            for h in range(B_HEADS):
                acc = jnp.full((Q_BLOCK, 2 * Q_BLOCK), NEG_INF, F32)
                for b in present[c]:
                    acc = jnp.where(bk == b, rb_ref[h, b], acc)
                o_ref[c, h] = acc
                ot_ref[c, h] = acc.T

    return tables


def _proj_fwd(x, g1, w_in_t, ln_g, ln_b, w_s, bz):
    m = x.shape[0]
    tm = GATE_ROWS
    pavg = _group_mean_matrix()

    def body(x_ref, g_ref, w_ref, lg_ref, lb_ref, ws_ref, bz_ref, p_ref, h_ref, uv_ref, qkv_ref, a_ref):
        xv = x_ref[...]
        h = (xv * _rstd(xv) * g_ref[...]).astype(BF16)
        h_ref[...] = h
        acc = _dot_nt(h, w_ref[...])
        uv_ref[...] = acc[:, :2 * A_WIDTH]
        for s in range(QKV_SLABS):
            qkv_ref[s] = acc[:, 2 * A_WIDTH + s * PAIR:2 * A_WIDTH + (s + 1) * PAIR]
        _gate_fwd_rows(acc[:, :A_WIDTH], acc[:, A_WIDTH:2 * A_WIDTH], lg_ref[...], lb_ref[...], ws_ref,
                       bz_ref[...], p_ref[...], a_ref)

    return pl.pallas_call(
        body, grid=(m // tm,),
        in_specs=[pl.BlockSpec((tm, D_MODEL), lambda i: (i, 0)), _vec_spec(),
                  pl.BlockSpec((IN_COLS, D_MODEL), lambda i: (0, 0)),
                  _full_spec((1, A_WIDTH)), _full_spec((1, A_WIDTH)), _full_spec((A_GROUPS, CHUNK, CHUNK)),
                  _full_spec((CHUNK, A_WIDTH)), _full_spec((A_WIDTH, A_WIDTH))],
        out_specs=[pl.BlockSpec((tm, D_MODEL), lambda i: (i, 0)),
                   pl.BlockSpec((tm, 2 * A_WIDTH), lambda i: (i, 0)),
                   pl.BlockSpec((QKV_SLABS, tm, PAIR), lambda i: (0, i, 0)),
                   pl.BlockSpec((tm, A_WIDTH), lambda i: (i, 0))],
        out_shape=[jax.ShapeDtypeStruct((m, D_MODEL), BF16), jax.ShapeDtypeStruct((m, 2 * A_WIDTH), F32),
                   jax.ShapeDtypeStruct((QKV_SLABS, m, PAIR), F32), jax.ShapeDtypeStruct((m, A_WIDTH), BF16)],
        compiler_params=_params(("parallel",)), name="proj_fwd",
    )(x, g1, w_in_t, ln_g, ln_b, w_s, bz, pavg)


def _pair_masks():
    lane = lax.broadcasted_iota(jnp.int32, (1, PAIR), 1)
    return [lane < HEAD_DIM, lane >= HEAD_DIM]


def _block_rows(idx, dil):
    static = isinstance(idx, int)
    r, n = idx % dil, idx // dil

    def rows_of(block):
        start = r + (dil * Q_BLOCK) * block
        if dil == 1:
            return pl.ds(start if static else pl.multiple_of(start, Q_BLOCK), Q_BLOCK)
        return pl.ds(start, Q_BLOCK, stride=dil)

    prev = rows_of(n - 1) if not static or n > 0 else None
    return rows_of(n), prev


def _attn_fwd(qkv, bias, batch, owns):
    m = qkv.shape[1]
    comb_rows = 256
    nt = len(owns)
    shapes = [g.shape[1:] for g in owns]
    n_steps = batch * N_PAIR
    ts = list(range(nt))

    def body(q_ref, k_ref, v_ref, b_ref, *rest):
        o_ref, l_ref = rest[nt:nt + 2]
        gat_refs = rest[nt + 2:2 * nt + 2]
        scratch = rest[2 * nt + 2:]
        oc_refs, lc_refs = scratch[:N_CFG], scratch[N_CFG:2 * N_CFG]
        step = pl.program_id(0) * N_PAIR + pl.program_id(1)
        gather = _RelayGatherPlan(shapes, None, gat_refs, *scratch[2 * N_CFG:])

        @pl.when(step == 0)
        def _():
            gather.start(ts)

        @pl.when(step == n_steps // 2)
        def _():
            gather.relay(ts)

        @pl.when(step == n_steps - 2)
        def _():
            gather.forward(ts)

        masks = _pair_masks()
        for ci, (_, dil) in enumerate(DILATED):
            nb = SEQ // dil // Q_BLOCK

            def block(trip, ci=ci, dil=dil, nb=nb):
                work = []
                for u in range(FWD_BLOCKS_PER_TRIP):
                    rows, prow = _block_rows(trip * FWD_BLOCKS_PER_TRIP + u, dil)
                    has_prev = nb > 1 and prow is not None
                    q = q_ref[rows, :] * 0.125
                    kc = k_ref[rows, :].astype(BF16)
                    vc = v_ref[rows, :]
                    kp = k_ref[prow, :].astype(BF16) if has_prev else None
                    vp = v_ref[prow, :] if has_prev else None
                    tiles = []
                    for h in range(2):
                        qh = jnp.where(masks[h], q, 0.0).astype(BF16)
                        sc = _dot_nt(qh, kc) + b_ref[ci, h, :, Q_BLOCK:]
                        sp = _dot_nt(qh, kp) + b_ref[ci, h, :, :Q_BLOCK] if has_prev else None
                        tiles.append((sc, sp))
                    work.append((rows, vc, vp, tiles))
                probs = []
                for _, _, _, tiles in work:
                    ps = []
                    for sc, sp in tiles:
                        mx = jnp.max(sc if sp is None else jnp.maximum(sc, sp), axis=1, keepdims=True)
                        pc = jnp.exp(sc - mx).astype(BF16)
                        pp = None if sp is None else jnp.exp(sp - mx).astype(BF16)
                        ps.append((mx, pc, pp))
                    probs.append(ps)
                for (rows, vc, vp, _), ps in zip(work, probs):
                    res = []
                    for h, (_, pc, pp) in enumerate(ps):
                        r = _dot(pc, jnp.where(masks[h], vc, 1.0).astype(BF16))
                        if pp is not None:
                            r = r + _dot(pp, jnp.where(masks[h], vp, 1.0).astype(BF16))
                        res.append(r)
                    num = jnp.where(masks[0], res[0], res[1])
                    den = pltpu.roll(jnp.where(masks[0], res[1], res[0]), HEAD_DIM, 1)
                    oc_refs[ci][rows, :] = num / den
                    lc_refs[ci][rows, :] = jnp.where(masks[0], ps[0][0], ps[1][0]) + jnp.log(den)

            for trip in range(BLOCKS_PER_CFG // FWD_BLOCKS_PER_TRIP):
                block(trip)

        def combine(i, carry):
            rr = pl.ds(pl.multiple_of(i * comb_rows, comb_rows), comb_rows)
            ls = [lc_refs[c][rr, :] for c in range(N_CFG)]
            mx = functools.reduce(jnp.maximum, ls)
            ws = [jnp.exp(l - mx) for l in ls]
            tot = functools.reduce(lambda a, b: a + b, ws)
            o = functools.reduce(lambda a, b: a + b, [ws[c] * oc_refs[c][rr, :] for c in range(N_CFG)]) / tot
            o_ref[rr, :] = o.astype(BF16)
            l_ref[rr, :] = mx + jnp.log(tot)
            return carry

        lax.fori_loop(0, SEQ // comb_rows, combine, 0)

        @pl.when(step == n_steps - 1)
        def _():
            gather.finish(ts)

    def slab(first):
        return pl.BlockSpec((None, SEQ, PAIR), lambda b, p: (first + p, b, 0))

    nat = pl.BlockSpec((SEQ, PAIR), lambda b, p: (b, p))
    res = pl.pallas_call(
        body, grid=(batch, N_PAIR),
        in_specs=[slab(0), slab(N_PAIR), slab(2 * N_PAIR),
                  pl.BlockSpec((N_CFG, 2, Q_BLOCK, 2 * Q_BLOCK), lambda b, p: (0, p, 0, 0))] + [ANY] * nt,
        out_specs=[nat, nat] + [ANY] * nt,
        out_shape=[jax.ShapeDtypeStruct((m, B_WIDTH), BF16), jax.ShapeDtypeStruct((m, B_WIDTH), F32)]
        + [jax.ShapeDtypeStruct(g.shape, g.dtype) for g in owns],
        scratch_shapes=[pltpu.VMEM((SEQ, PAIR), F32)] * (2 * N_CFG) + _sem_pair(6 * nt),
        input_output_aliases={4 + t: 2 + t for t in range(nt)},
        compiler_params=_params(("arbitrary", "arbitrary")), name="attn_fwd",
    )(qkv, qkv, qkv, bias, *owns)
    return res[0], res[1], list(res[2:])


def _attn_bwd(qkv, dmix, o, lse, bias_t, dproj, batch, parts, smalls):
    m = qkv.shape[1]
    nt, ns = len(parts), len(smalls)
    n_steps = N_PAIR * batch

    def body(q_ref, k_ref, v_ref, do_ref, o_ref, l_ref, b_ref, *rest):
        part_refs = rest[1:nt + 1]
        small_refs = rest[nt + 1:nt + 1 + ns]
        pos = nt + 1 + ns
        dproj_ref, ds_ref = rest[pos:pos + 2]
        recv_refs = rest[pos + 2:pos + 2 + nt]
        sum_refs = rest[pos + 2 + nt:pos + 2 + nt + ns]
        pos += 2 + nt + ns
        dq_acc, dk_acc, dv_acc, d_scr, stage, stage_sems, send_sems, recv_sems = rest[pos:pos + 8]
        allreduce = _SmallAllReducePlan(small_refs, sum_refs, rest[pos + 8:pos + 8 + ns],
                                        rest[pos + 8 + ns:pos + 8 + 2 * ns], *rest[pos + 8 + 2 * ns:])
        pair, seq = pl.program_id(0), pl.program_id(1)
        step = pair * batch + seq
        exchange = _ChipExchangePlan(part_refs, recv_refs, send_sems, recv_sems)

        @pl.when(step == 0)
        def _():
            allreduce.start_sibling()

        @pl.when(step == n_steps // 2)
        def _():
            allreduce.sum_sibling_and_start_chips()

        def stage_copies():
            rows = pl.ds(pl.multiple_of(seq * SEQ, SEQ), SEQ)
            return [pltpu.make_async_copy(
                stage.at[k],
                dproj_ref.at[rows, pl.ds(pl.multiple_of(2 * A_WIDTH + k * B_WIDTH + pair * PAIR, PAIR), PAIR)],
                stage_sems.at[k]) for k in range(3)]

        @pl.when(step == 0)
        def _():
            exchange.start()

        @pl.when(pl.program_id(1) == 0)
        def _():
            ds_ref[...] = jnp.zeros_like(ds_ref)

        dq_acc[...] = jnp.zeros_like(dq_acc)
        dk_acc[...] = jnp.zeros_like(dk_acc)
        dv_acc[...] = jnp.zeros_like(dv_acc)
        d_scr[...] = do_ref[...] * o_ref[...].astype(F32)
        masks = _pair_masks()

        def stack_heads(t):
            return jnp.concatenate([jnp.where(masks[0], t, 0.0), jnp.where(masks[1], t, 0.0)], axis=0).astype(BF16)

        for ci, (_, dil) in enumerate(DILATED):
            nb = SEQ // dil // Q_BLOCK

            def block(trip, carry, ci=ci, dil=dil, nb=nb):
                first = []
                for u in range(BWD_BLOCKS_PER_TRIP):
                    rows, prow = _block_rows(trip * BWD_BLOCKS_PER_TRIP + u, dil)
                    has_prev = nb > 1 and prow is not None
                    if has_prev:
                        kcat = jnp.concatenate([k_ref[prow, :], k_ref[rows, :]], axis=0).astype(BF16)
                        vcat = jnp.concatenate([v_ref[prow, :], v_ref[rows, :]], axis=0).astype(BF16)
                    else:
                        kcat = k_ref[rows, :].astype(BF16)
                        vcat = v_ref[rows, :].astype(BF16)
                    qst = stack_heads(q_ref[rows, :] * 0.125)
                    dost = stack_heads(do_ref[rows, :])
                    lt = l_ref[rows, :].T
                    dt = d_scr[rows, :].T
                    lrow = jnp.concatenate([lt[0:1], lt[HEAD_DIM:HEAD_DIM + 1]], axis=1)
                    drow = jnp.concatenate([jnp.sum(dt[:HEAD_DIM], axis=0, keepdims=True),
                                            jnp.sum(dt[HEAD_DIM:], axis=0, keepdims=True)], axis=1)
                    first.append((has_prev, rows, prow, kcat, qst, dost, lrow, drow,
                                  _dot_nt(kcat, qst), _dot_nt(vcat, dost)))
                second = []
                for has_prev, rows, prow, kcat, qst, dost, lrow, drow, st, dpt in first:
                    keys = slice(0, 2 * Q_BLOCK) if has_prev else slice(Q_BLOCK, 2 * Q_BLOCK)
                    bt = jnp.concatenate([b_ref[ci, 0, keys, :], b_ref[ci, 1, keys, :]], axis=1)
                    pt = jnp.exp(st + bt - lrow)
                    dst = pt * (dpt - drow)
                    ds_ref[ci, 0, keys, :] += dst[:, :Q_BLOCK]
                    ds_ref[ci, 1, keys, :] += dst[:, Q_BLOCK:]
                    second.append((has_prev, rows, prow, kcat, qst, dost, pt.astype(BF16), dst.astype(BF16)))
                for has_prev, rows, prow, kcat, qst, dost, pt_bf, dst_bf in second:
                    dk = _dot(dst_bf, qst)
                    dv = _dot(pt_bf, dost)
                    dq2 = _dot_tn(dst_bf, kcat)
                    dq_acc[rows, :] += jnp.where(masks[0], dq2[:Q_BLOCK], dq2[Q_BLOCK:]) * 0.125
                    if has_prev:
                        dk_acc[prow, :] += dk[:Q_BLOCK]
                        dv_acc[prow, :] += dv[:Q_BLOCK]
                        dk_acc[rows, :] += dk[Q_BLOCK:]
                        dv_acc[rows, :] += dv[Q_BLOCK:]
                    else:
                        dk_acc[rows, :] += dk
                        dv_acc[rows, :] += dv
                return carry

            for trip in range(BLOCKS_PER_CFG // BWD_BLOCKS_PER_TRIP):
                block(trip, 0)

        @pl.when(step > 0)
        def _():
            for cp in stage_copies():
                cp.wait()

        stage[0] = dq_acc[...].astype(BF16)
        stage[1] = dk_acc[...].astype(BF16)
        stage[2] = dv_acc[...].astype(BF16)
        for cp in stage_copies():
            cp.start()

        @pl.when(step == n_steps - 1)
        def _():
            for cp in stage_copies():
                cp.wait()
            exchange.finish()
            allreduce.finish()

    def slab(first):
        return pl.BlockSpec((None, SEQ, PAIR), lambda p, b: (first + p, b, 0))

    nat = pl.BlockSpec((SEQ, PAIR), lambda p, b: (b, p))
    tbl = pl.BlockSpec((N_CFG, 2, 2 * Q_BLOCK, Q_BLOCK), lambda p, b: (0, p, 0, 0))
    acc = pltpu.VMEM((SEQ, PAIR), F32)
    vm = pl.BlockSpec(memory_space=pltpu.VMEM)
    res = pl.pallas_call(
        body, grid=(N_PAIR, batch),
        in_specs=[slab(0), slab(N_PAIR), slab(2 * N_PAIR),
                  pl.BlockSpec((SEQ, PAIR), lambda p, b: (b, A_WIDTH // PAIR + p)), nat, nat, tbl]
        + [ANY] * (nt + 1) + [vm] * ns,
        out_specs=[ANY, tbl] + [ANY] * nt + [vm] * ns,
        out_shape=[jax.ShapeDtypeStruct(dproj.shape, dproj.dtype),
                   jax.ShapeDtypeStruct((N_CFG, B_HEADS, 2 * Q_BLOCK, Q_BLOCK), F32)]
        + [jax.ShapeDtypeStruct((3,) + p.shape[1:], p.dtype) for p in parts]
        + [jax.ShapeDtypeStruct(a.shape, F32) for a in smalls],
        input_output_aliases={7: 0},
        scratch_shapes=[acc, acc, acc, acc, pltpu.VMEM((3, SEQ, PAIR), BF16), pltpu.SemaphoreType.DMA((3,))]
        + _sem_pair(3 * nt) + _SmallAllReducePlan.scratch(smalls),
        compiler_params=_params(("arbitrary", "arbitrary")), name="attn_bwd",
    )(qkv, qkv, qkv, dmix, o, lse, bias_t, dproj, *parts, *smalls)
    return res[0], res[1], list(res[2:2 + nt]), list(res[2 + nt:])


def _rel_bias_grad(ds, buckets_np, grads):
    present = _present_buckets(buckets_np)
    nx = len(grads)
    shapes = [g.shape for g in grads]

    def body(bk_ref, ds_ref, *rest):
        o_ref = rest[nx]
        acc_ref = rest[2 * nx + 1]
        exchange = _SiblingExchangePlan(shapes, rest[:nx], rest[nx + 1:2 * nx + 1], *rest[2 * nx + 2:])
        exchange.start()
        acc_ref[...] = jnp.zeros_like(acc_ref)
        for c in range(N_CFG):
            bk = bk_ref[c]
            for h in range(B_HEADS):
                dsv = ds_ref[c, h]
                for b in present[c]:
                    part = jnp.sum(jnp.where(bk == b, dsv, 0.0), axis=0, keepdims=True)
                    acc_ref[pl.ds(h * NUM_BUCKETS + b, 1), :] += part
        o_ref[...] = jnp.sum(acc_ref[...], axis=1, keepdims=True)
        exchange.finish()

    vm = pl.BlockSpec(memory_space=pltpu.VMEM)
    res = pl.pallas_call(
        body, in_specs=[vm, vm] + [ANY] * nx, out_specs=[vm] + [ANY] * nx,
        out_shape=[jax.ShapeDtypeStruct((B_HEADS * NUM_BUCKETS, 1), F32)]
        + [jax.ShapeDtypeStruct((N_SHARD, s[1] // 2, s[2]), g.dtype) for s, g in zip(shapes, grads)],
        scratch_shapes=[pltpu.VMEM((B_HEADS * NUM_BUCKETS, buckets_np.shape[-1]), F32)] + _sem_pair(nx),
        compiler_params=_params(), name="rel_bias_grad",
    )(jnp.asarray(buckets_np), ds, *grads)
    return res[0], list(res[1:])


CONV_WIDTH = D_FF // 2
CONV_LANES = 128
CONV_CHUNK = 128
CONV_HALO = 16


def _row_index():
    return lax.broadcasted_iota(jnp.int32, (SEQ, LANE_BLOCK), 0)


def _shift_down(x, k, row):
    return jnp.where(row >= k, pltpu.roll(x, k, 0), 0.0)


def _shift_up(x, k, row):
    return jnp.where(row < SEQ - k, pltpu.roll(x, SEQ - k, 0), 0.0)


def _convgate_fwd(gate, up, conv_w, conv_b, batch):
    m = gate.shape[0]

    def body(g_ref, u_ref, w_ref, b_ref, a_ref):
        row = lax.broadcasted_iota(jnp.int32, (CONV_CHUNK, CONV_LANES), 0)
        for j in range(CONV_WIDTH // CONV_LANES):
            lanes = slice(j * CONV_LANES, (j + 1) * CONV_LANES)
            w = w_ref[:, lanes]
            w0, w1, w2, b = w[0:1], w[1:2], w[2:3], b_ref[:, lanes]

            def finish(r0, g, g1, g2, lanes=lanes, w0=w0, w1=w1, w2=w2, b=b):
                c = b + w0 * g2 + w1 * g1 + w2 * g
                rows = pl.ds(r0, CONV_CHUNK)
                a_ref[rows, lanes] = (_gelu(c) * u_ref[rows, lanes].astype(F32)).astype(BF16)

            g = g_ref[0:CONV_CHUNK, lanes].astype(F32)
            finish(0, g, jnp.where(row >= 1, pltpu.roll(g, 1, 0), 0.0), jnp.where(row >= 2, pltpu.roll(g, 2, 0), 0.0))

            def chunk(i, carry, lanes=lanes, finish=finish):
                r0 = pl.multiple_of(i * CONV_CHUNK, CONV_CHUNK)
                gh = g_ref[pl.ds(r0 - CONV_HALO, CONV_CHUNK + CONV_HALO), lanes].astype(F32)
                finish(r0, gh[CONV_HALO:], pltpu.roll(gh, 1, 0)[CONV_HALO:], pltpu.roll(gh, 2, 0)[CONV_HALO:])
                return carry

            lax.fori_loop(1, SEQ // CONV_CHUNK, chunk, 0, unroll=5)

    blk = pl.BlockSpec((SEQ, CONV_WIDTH), lambda b, j: (b, j))
    return pl.pallas_call(
        body, grid=(batch, D_FF // CONV_WIDTH),
        in_specs=[blk, blk, pl.BlockSpec((3, CONV_WIDTH), lambda b, j: (0, j)),
                  pl.BlockSpec((1, CONV_WIDTH), lambda b, j: (0, j))],
        out_specs=blk,
        out_shape=jax.ShapeDtypeStruct((m, D_FF), BF16),
        compiler_params=_params(("parallel", "parallel")), name="convgate_fwd",
    )(gate, up, conv_w, conv_b)


def _convgate_bwd(gate, up, dact, conv_w, conv_b, batch):
    m = gate.shape[0]

    def body(g_ref, u_ref, da_ref, w_ref, b_ref, dg_ref, du_ref, dw_ref, db_ref):
        @pl.when(pl.program_id(1) == 0)
        def _():
            dw_ref[...] = jnp.zeros_like(dw_ref)
            db_ref[...] = jnp.zeros_like(db_ref)

        g = g_ref[...].astype(F32)
        w = w_ref[...]
        row = _row_index()
        g1 = _shift_down(g, 1, row)
        g2 = _shift_down(g, 2, row)
        c = b_ref[...] + w[0:1] * g2 + w[1:2] * g1 + w[2:3] * g
        gg, dgg = _gelu_and_grad(c)
        da = da_ref[...].astype(F32)
        du_ref[...] = (da * gg).astype(BF16)
        dc = da * u_ref[...].astype(F32) * dgg
        db_ref[...] += jnp.sum(dc, axis=0, keepdims=True)
        dw_ref[0:1, :] += jnp.sum(dc * g2, axis=0, keepdims=True)
        dw_ref[1:2, :] += jnp.sum(dc * g1, axis=0, keepdims=True)
        dw_ref[2:3, :] += jnp.sum(dc * g, axis=0, keepdims=True)
        dg_ref[...] = (w[2:3] * dc + w[1:2] * _shift_up(dc, 1, row) + w[0:1] * _shift_up(dc, 2, row)).astype(BF16)

    blk = pl.BlockSpec((SEQ, LANE_BLOCK), lambda j, b: (b, j))
    wspec = pl.BlockSpec((3, LANE_BLOCK), lambda j, b: (0, j))
    bspec = pl.BlockSpec((1, LANE_BLOCK), lambda j, b: (0, j))
    return pl.pallas_call(
        body, grid=(D_FF // LANE_BLOCK, batch),
        in_specs=[blk, blk, blk, wspec, bspec],
        out_specs=[blk, blk, wspec, bspec],
        out_shape=[jax.ShapeDtypeStruct((m, D_FF), BF16), jax.ShapeDtypeStruct((m, D_FF), BF16),
                   jax.ShapeDtypeStruct((3, D_FF), F32), jax.ShapeDtypeStruct((1, D_FF), F32)],
        compiler_params=_params(("parallel", "arbitrary")), name="convgate_bwd",
    )(gate, up, dact, conv_w, conv_b)


def _gather_weights(shards, conv_w_shard, rel_bias, buckets_np):
    nt = len(shards)
    shapes = [sh.shape for sh in shards]
    ts = list(range(nt))
    tables = _bias_tables_body(buckets_np)

    def body(*refs):
        shard_refs = refs[:nt]
        cw_ref, rb_ref, bk_ref = refs[nt:nt + 3]
        out_refs = refs[nt + 3:2 * nt + 3]
        cw_out, bias_ref, bias_t_ref = refs[2 * nt + 3:2 * nt + 6]
        scratch = refs[2 * nt + 6:]
        f32_refs, bf16_refs = scratch[:nt], scratch[nt:2 * nt]
        load_sems, store_sems, send_sems, recv_sems, cw_send, cw_recv = scratch[2 * nt:]
        plan = _RelayGatherPlan(shapes[:1], bf16_refs[:1], out_refs[:1], send_sems, recv_sems)
        x, y, c, chips = _mesh_pos()
        loads = [pltpu.make_async_copy(shard_refs[t], f32_refs[t], load_sems.at[t]) for t in ts]
        stores = [pltpu.make_async_copy(bf16_refs[t], out_refs[t].at[2 * x + y], store_sems.at[t]) for t in ts]
        stores.append(pltpu.make_async_copy(cw_ref, cw_out.at[2 * x + y], store_sems.at[nt]))

        def cw_copy(j, src, dst, chip):
            return pltpu.make_async_remote_copy(src_ref=src, dst_ref=dst, send_sem=cw_send.at[j],
                                                recv_sem=cw_recv.at[j], device_id=(*chip, c), device_id_type=MESH)

        def to_bf16(t):
            loads[t].wait()
            bf16_refs[t][...] = f32_refs[t][...].astype(BF16)
            stores[t].start()

        for cp in loads:
            cp.start()
        to_bf16(0)
        plan.start([0])
        cw_sends = [cw_copy(j, cw_ref, cw_out.at[2 * x + y], chip) for j, chip in enumerate(chips)]
        for cp in cw_sends + stores[nt:]:
            cp.start()
        for t in ts[1:]:
            to_bf16(t)
        plan.relay([0])
        tables(rb_ref, bk_ref, bias_ref, bias_t_ref)
        plan.forward([0])
        for j, chip in enumerate(chips):
            dst = cw_out.at[2 * chip[0] + chip[1]]
            cw_copy(j, dst, dst, chip).wait_recv()
        plan.finish([0])
        for cp in cw_sends:
            cp.wait_send()
        for cp in stores:
            cp.wait()

    out_shape = [jax.ShapeDtypeStruct((N_SHARD,) + sh.shape, BF16) for sh in shards]
    out_shape.append(jax.ShapeDtypeStruct((N_SHARD,) + conv_w_shard.shape, conv_w_shard.dtype))
    out_shape += [jax.ShapeDtypeStruct((N_CFG, B_HEADS, Q_BLOCK, 2 * Q_BLOCK), F32),
                  jax.ShapeDtypeStruct((N_CFG, B_HEADS, 2 * Q_BLOCK, Q_BLOCK), F32)]
    vm = pl.BlockSpec(memory_space=pltpu.VMEM)
    res = pl.pallas_call(
        body, in_specs=[ANY] * (nt + 1) + [pl.BlockSpec(memory_space=pltpu.SMEM), vm],
        out_specs=[ANY] * (nt + 1) + [vm, vm], out_shape=out_shape,
        scratch_shapes=[pltpu.VMEM(sh.shape, F32) for sh in shards] + [pltpu.VMEM(sh.shape, BF16) for sh in shards]
        + [pltpu.SemaphoreType.DMA((nt,)), pltpu.SemaphoreType.DMA((nt + 1,))] + _sem_pair(6) + _sem_pair(3),
        compiler_params=pltpu.CompilerParams(has_side_effects=True, vmem_limit_bytes=VMEM_LIMIT),
        name="gather_weights",
    )(*shards, conv_w_shard, rel_bias.T, jnp.asarray(buckets_np))
    return list(res[:nt + 1]), res[nt + 1], res[nt + 2]


def _turn(t, u, s, last):
    return jnp.where(t == u, s, jnp.where(t > u, last, 0))


def _add_halves(gs, recvs, c_idx):
    n = len(gs)
    dims = [(g.shape[1] // 2, g.shape[2]) for g in gs]

    def body(c_ref, *refs):
        t = pl.program_id(0)
        for u in range(n):
            @pl.when(t == u)
            def _(u=u):
                refs[2 * n + u][...] = (refs[u][...].astype(F32) + refs[n + u][...].astype(F32)).astype(BF16)

    def own(u):
        return pl.BlockSpec((None, None) + dims[u], lambda t, s, c: (_turn(t, u, s, N_SHARD - 1), c[0], 0, 0))

    def plain(u):
        return pl.BlockSpec((None,) + dims[u], lambda t, s, c: (_turn(t, u, s, N_SHARD - 1), 0, 0))

    return pl.pallas_call(
        body,
        grid_spec=pltpu.PrefetchScalarGridSpec(
            num_scalar_prefetch=1, grid=(n, N_SHARD),
            in_specs=[own(u) for u in range(n)] + [plain(u) for u in range(n)],
            out_specs=[plain(u) for u in range(n)]),
        out_shape=[jax.ShapeDtypeStruct((N_SHARD,) + d, BF16) for d in dims],
        compiler_params=_params(("arbitrary", "arbitrary")), name="rs_add_halves",
    )(c_idx, *[g.reshape((N_SHARD, 2) + d) for g, d in zip(gs, dims)], *recvs)


def _add_chips(parts, recvs, s_idx, c_idx):
    n = len(parts)
    dims = [p.shape[1:] for p in parts]

    def body(idx_ref, *refs):
        t = pl.program_id(0)
        for u in range(n):
            @pl.when(t == u)
            def _(u=u):
                acc = refs[u][...].astype(F32)
                for j in range(3):
                    acc = acc + refs[n + u][j].astype(F32)
                refs[2 * n + u][...] = acc

    res = pl.pallas_call(
        body,
        grid_spec=pltpu.PrefetchScalarGridSpec(
            num_scalar_prefetch=1, grid=(n,),
            in_specs=[pl.BlockSpec((None,) + d, lambda t, idx: (idx[0], 0, 0)) for d in dims]
            + [pl.BlockSpec((3,) + d, lambda t, idx: (0, 0, 0)) for d in dims],
            out_specs=[pl.BlockSpec((None,) + d, lambda t, idx: (idx[1], 0, 0)) for d in dims]),
        out_shape=[jax.ShapeDtypeStruct((2,) + d, F32) for d in dims],
        compiler_params=_params(("arbitrary",)), name="rs_add_chips",
    )(jnp.concatenate([s_idx, c_idx]), *parts, *recvs)
    return [r.reshape(2 * d[0], d[1]) for r, d in zip(res, dims)]


def _finish_reductions(fulls, arrays):
    nt, n = len(fulls), len(arrays)

    def body(*refs):
        in_refs = refs[nt:nt + n]
        full_refs, out_refs = refs[nt + n:2 * nt + n], refs[2 * nt + n:2 * nt + 2 * n]
        pos = 2 * nt + 2 * n
        share_send, share_recv = refs[pos + 2 * n:pos + 2 * n + 2]
        allreduce = _SmallAllReducePlan(in_refs, out_refs, refs[pos:pos + n], refs[pos + n:pos + 2 * n],
                                        *refs[pos + 2 * n + 2:])
        x, y, c, _ = _mesh_pos()

        def half(t, which):
            rows = fulls[t].shape[0] // 2
            return full_refs[t].at[pl.ds(which * rows, rows), :]

        def share(t, which):
            return pltpu.make_async_remote_copy(
                src_ref=half(t, which), dst_ref=half(t, which), send_sem=share_send.at[t],
                recv_sem=share_recv.at[t], device_id=(x, y, 1 - c), device_id_type=MESH)

        for t in range(nt):
            share(t, c).start()
        allreduce.start_sibling()
        allreduce.sum_sibling_and_start_chips()
        allreduce.finish()
        for t in range(nt):
            share(t, 1 - c).wait_recv()
        for t in range(nt):
            share(t, c).wait_send()

    vm = pl.BlockSpec(memory_space=pltpu.VMEM)
    res = pl.pallas_call(
        body, in_specs=[ANY] * nt + [vm] * n, out_specs=[ANY] * nt + [vm] * n,
        out_shape=[jax.ShapeDtypeStruct(f.shape, f.dtype) for f in fulls]
        + [jax.ShapeDtypeStruct(a.shape, F32) for a in arrays],
        input_output_aliases={t: t for t in range(nt)},
        scratch_shapes=[pltpu.VMEM(a.shape, F32) for a in arrays] + [pltpu.VMEM((3,) + a.shape, F32) for a in arrays]
        + _sem_pair(nt) + _sem_pair(4 * n),
        compiler_params=pltpu.CompilerParams(has_side_effects=True),
        name="finish_reductions",
    )(*fulls, *arrays)
    return list(res[:nt]), list(res[nt:])


def _from_col_shards(g):
    n, rows, cols = g.shape
    return g.transpose(1, 0, 2).reshape(rows, n * cols)


def _train_step(x, tgt, g1, g2, g3, g4, shards, ln_g, ln_b, w_s, b_s, rel_bias, conv_w_shard, conv_b, batch,
                s_idx, c_idx):
    buckets = _bucket_tables()
    bz = jnp.repeat(b_s.T, HEAD_DIM, axis=1)
    w_st = jnp.swapaxes(w_s, 1, 2)

    def shard_major(g):
        return g.reshape(N_SHARD, g.shape[0] // N_SHARD, D_MODEL)

    names = ["w_in", "w_out", "w_gate", "w_up", "w_down"]
    (g_in, g_out, g_gate, g_up, g_down, g_convw), bias, bias_t = _gather_weights(
        [shards[n] for n in names], conv_w_shard, rel_bias, buckets)
    w_in_t = g_in.reshape(IN_COLS, D_MODEL)
    conv_w = _from_col_shards(g_convw.reshape(N_SHARD, 3, SHARD_FF))

    h1, uv, qkv, a = _proj_fwd(x, g1, w_in_t, ln_g, ln_b, w_s, bz)
    o_bf, lse, (g_out, g_gate, g_up) = _attn_fwd(qkv, bias, batch, [g_out, g_gate, g_up])
    w_out = g_out.reshape(D_MODEL, D_MODEL)
    w_gate_t = g_gate.reshape(D_FF, D_MODEL)
    w_up_t = g_up.reshape(D_FF, D_MODEL)
    (y1, x1, h2), _ = _fused_rows(
        "out_proj_mid_fwd", 512,
        [(a, w_out, "nn", slice(0, A_WIDTH)), (o_bf, w_out, "nn", slice(A_WIDTH, D_MODEL))],
        [x], [g2, g3], _mid_fwd_rows, [F32, F32, BF16], [])
    gate, up, g_down = _mm_pair_nt(h2, w_gate_t, w_up_t, g_down, tm=1024, tn=1408, out_dtype=BF16,
                                   name="mm_gate_up")
    w_down = g_down.reshape(D_FF, D_MODEL)
    act = _convgate_fwd(gate, up, conv_w, conv_b, batch)
    (dx2, dy2, dg4, loss), _ = _fused_rows(
        "down_proj_loss_head", 512, [(act, w_down, "nn", None)], [x1, tgt], [g4], _loss_head_rows,
        [F32, BF16], [(1, D_MODEL), (1, 128)])

    dact = _mm(dy2, w_down, dims="nt", tm=1024, tn=1408, tk=1024, out_dtype=BF16, name="mm_dact")
    dw_down = _mm(act, dy2, dims="tn", tm=1408, tn=1024, tk=1024, out_dtype=BF16, name="mm_dw_down")
    dgate, dup, dconv_w, dconv_b = _convgate_bwd(gate, up, dact, conv_w, conv_b, batch)
    (dx1, dy1, dg2, dg3), _ = _fused_rows(
        "dh2_mid_bwd", 256, [(dgate, w_gate_t, "nn", None), (dup, w_up_t, "nn", None)],
        [x1, y1, dx2], [g2, g3], _mid_bwd_rows, [F32, BF16], [(1, D_MODEL), (1, D_MODEL)])
    dw_gate_t, dw_up_t = _mm_pair_tn(dgate, dup, h2, tm=1408, tk=1024, name="mm_dw_gate_up")
    done = [shard_major(g) for g in (dw_down, dw_gate_t, dw_up_t)]
    (dmix, dw_out), recv_a = _out_proj_bwd(a, o_bf, dy1, w_out, done[:2])
    done.append(shard_major(dw_out))
    (dproj, dln_g, dln_b, dw_s, dbz), recv_b = _gate_bwd(uv, dmix, ln_g, ln_b, w_s, w_st, bz, done[2:])
    recv_a += recv_b
    parts = _add_halves(done, recv_a, c_idx)
    early = dict(loss=loss, norm_mix_post=dg2, norm_ffn_pre=dg3, norm_ffn_post=dg4, ln_v_gain=dln_g,
                 ln_v_bias=dln_b, spatial_w=dw_s, spatial_b=dbz, conv_w=dconv_w, conv_b=dconv_b)
    dproj, ds, recv, early_sums = _attn_bwd(qkv, dmix, o_bf, lse, bias_t, dproj, batch, parts, list(early.values()))
    fulls = _add_chips(parts, recv, s_idx, c_idx)
    dw_in_t = _mm(dproj, h1, dims="tn", tm=1408, tn=1024, tk=1024, out_dtype=BF16, name="mm_dw_in")
    last = [shard_major(dw_in_t)]
    drel, recv_in_a = _rel_bias_grad(ds, np.ascontiguousarray(np.swapaxes(buckets, 1, 2)), last)
    part_in = _add_halves(last, recv_in_a, c_idx)
    (dx0, dg1), recv_in = _fused_rows(
        "dh1_in_bwd", 512, [(dproj, w_in_t, "nn", None)], [x, dx1], [g1], _in_bwd_rows,
        [F32], [(1, D_MODEL)], exchange=part_in)
    fulls += _add_chips(part_in, recv_in, s_idx, c_idx)
    half_reduced = dict(zip(["w_down", "w_gate", "w_up", "w_out", "w_in"], fulls))

    return dx0, dict(zip(early, early_sums)), dict(norm_mix_pre=dg1, rel_bias=drel), half_reduced


def _adamw_update(w, g, m, v):
    nm = ADAM_B1 * m + (1.0 - ADAM_B1) * g
    nv = ADAM_B2 * v + (1.0 - ADAM_B2) * (g * g)
    m_hat = nm / (1.0 - ADAM_B1 ** ADAM_STEP)
    v_hat = nv / (1.0 - ADAM_B2 ** ADAM_STEP)
    return -ADAM_LR * (m_hat / (jnp.sqrt(v_hat) + ADAM_EPS) + ADAM_WD * w), nm, nv


def _adamw(w, g, m, v, name):
    rows, cols = w.shape
    tr = next(cand for cand in (352, 256, 128) if rows % cand == 0)

    def body(w_ref, g_ref, m_ref, v_ref, go_ref, d_ref, nm_ref, nv_ref):
        gv = g_ref[...]
        go_ref[...] = gv
        d_ref[...], nm_ref[...], nv_ref[...] = _adamw_update(w_ref[...], gv, m_ref[...], v_ref[...])

    spec = pl.BlockSpec((tr, cols), lambda i: (i, 0))
    sds = jax.ShapeDtypeStruct((rows, cols), F32)
    return pl.pallas_call(
        body, grid=(rows // tr,), in_specs=[spec] * 4, out_specs=[spec] * 4, out_shape=[sds] * 4,
        compiler_params=_params(("parallel",)), name=name,
    )(w, g, m, v)


def _adamw_small(ws, gs, ms, vs):
    n = len(ws)

    def body(*refs):
        w_refs, g_refs, m_refs, v_refs = refs[:n], refs[n:2 * n], refs[2 * n:3 * n], refs[3 * n:4 * n]
        d_refs, nm_refs, nv_refs, go_refs = refs[4 * n:5 * n], refs[5 * n:6 * n], refs[6 * n:7 * n], refs[7 * n:]
        for t in range(n):
            g = g_refs[t][...]
            d_refs[t][...], nm_refs[t][...], nv_refs[t][...] = _adamw_update(
                w_refs[t][...], g, m_refs[t][...], v_refs[t][...])
            go_refs[t][...] = g

    vm = pl.BlockSpec(memory_space=pltpu.VMEM)
    sds = [jax.ShapeDtypeStruct(w.shape, F32) for w in ws]
    res = pl.pallas_call(
        body, in_specs=[vm] * (4 * n), out_specs=[vm] * (4 * n), out_shape=sds * 4,
        compiler_params=_params(), name="adamw_small",
    )(*ws, *gs, *ms, *vs)
    return res[:n], res[n:2 * n], res[2 * n:3 * n], res[3 * n:]


SMALL = ["norm_mix_pre", "norm_mix_post", "norm_ffn_pre", "norm_ffn_post", "ln_v_gain", "ln_v_bias",
         "spatial_w", "spatial_b", "rel_bias", "conv_b"]
LARGE = ["w_in", "w_gate", "w_up", "w_down", "w_out"]
TRANSPOSED = ("w_in", "w_gate", "w_up")
ORDER = ["norm_mix_pre", "norm_mix_post", "norm_ffn_pre", "norm_ffn_post", "w_in", "ln_v_gain", "ln_v_bias",
         "spatial_w", "spatial_b", "rel_bias", "w_out", "w_gate", "w_up", "conv_w", "conv_b", "w_down"]


def kernel(x, norm_mix_pre, norm_mix_post, norm_ffn_pre, norm_ffn_post, w_in, ln_v_gain, ln_v_bias, spatial_w, spatial_b, rel_bias, w_out, w_gate, w_up, conv_w, conv_b, w_down, loss_target, m_norm_mix_pre, m_norm_mix_post, m_norm_ffn_pre, m_norm_ffn_post, m_w_in, m_ln_v_gain, m_ln_v_bias, m_spatial_w, m_spatial_b, m_rel_bias, m_w_out, m_w_gate, m_w_up, m_conv_w, m_conv_b, m_w_down, v_norm_mix_pre, v_norm_mix_post, v_norm_ffn_pre, v_norm_ffn_post, v_w_in, v_ln_v_gain, v_ln_v_bias, v_spatial_w, v_spatial_b, v_rel_bias, v_w_out, v_w_gate, v_w_up, v_conv_w, v_conv_b, v_w_down):
    params = dict(norm_mix_pre=norm_mix_pre, norm_mix_post=norm_mix_post, norm_ffn_pre=norm_ffn_pre,
                  norm_ffn_post=norm_ffn_post, w_in=w_in, ln_v_gain=ln_v_gain, ln_v_bias=ln_v_bias,
                  spatial_w=spatial_w, spatial_b=spatial_b, rel_bias=rel_bias, w_out=w_out, w_gate=w_gate,
                  w_up=w_up, conv_w=conv_w, conv_b=conv_b, w_down=w_down)
    mom = dict(norm_mix_pre=m_norm_mix_pre, norm_mix_post=m_norm_mix_post, norm_ffn_pre=m_norm_ffn_pre,
               norm_ffn_post=m_norm_ffn_post, w_in=m_w_in, ln_v_gain=m_ln_v_gain, ln_v_bias=m_ln_v_bias,
               spatial_w=m_spatial_w, spatial_b=m_spatial_b, rel_bias=m_rel_bias, w_out=m_w_out, w_gate=m_w_gate,
               w_up=m_w_up, conv_w=m_conv_w, conv_b=m_conv_b, w_down=m_w_down)
    var = dict(norm_mix_pre=v_norm_mix_pre, norm_mix_post=v_norm_mix_post, norm_ffn_pre=v_norm_ffn_pre,
               norm_ffn_post=v_norm_ffn_post, w_in=v_w_in, ln_v_gain=v_ln_v_gain, ln_v_bias=v_ln_v_bias,
               spatial_w=v_spatial_w, spatial_b=v_spatial_b, rel_bias=v_rel_bias, w_out=v_w_out, w_gate=v_w_gate,
               w_up=v_w_up, conv_w=v_conv_w, conv_b=v_conv_b, w_down=v_w_down)

    batch = x.shape[0]
    xi, yi, ci = lax.axis_index("x"), lax.axis_index("y"), lax.axis_index("c")
    s_idx = (2 * xi + yi).astype(jnp.int32).reshape(1)
    c_idx = ci.astype(jnp.int32).reshape(1)

    def local(a, n):
        return jnp.swapaxes(a[0], 0, 1) if n in TRANSPOSED else a[0]

    shards = {n: local(params[n], n) for n in LARGE}
    dx0, total, partial, half_reduced = _train_step(
        x.reshape(batch * SEQ, D_MODEL), loss_target.reshape(batch * SEQ, D_MODEL),
        norm_mix_pre, norm_mix_post, norm_ffn_pre, norm_ffn_post, shards,
        ln_v_gain.reshape(1, A_WIDTH), ln_v_bias.reshape(1, A_WIDTH), spatial_w[0], spatial_b[0], rel_bias,
        jnp.swapaxes(conv_w, 0, 1), conv_b, batch, s_idx, c_idx)
    grad_x = dx0.reshape(batch, SEQ, D_MODEL)

    names = list(partial)
    fulls, sums = _finish_reductions([half_reduced[n] for n in LARGE], [partial[n] for n in names])
    reduced = dict(zip(LARGE, fulls))
    total.update(zip(names, sums))
    loss = total["loss"][0, 0]
    total["spatial_b"] = total["spatial_b"][:, ::HEAD_DIM].T
    total["rel_bias"] = total["rel_bias"].reshape(B_HEADS, NUM_BUCKETS)
    total["conv_w"] = lax.dynamic_slice_in_dim(total["conv_w"], s_idx[0] * SHARD_FF, SHARD_FF, axis=1)
    small_names = SMALL + ["conv_w"]

    def small(a, n):
        return jnp.swapaxes(a, 0, 1) if n in ("rel_bias", "conv_w") else a

    for n in small_names:
        reduced[n] = total[n].reshape(small(params[n], n).shape)

    out_g, out_d, out_m, out_v = {}, {}, {}, {}
    for n in LARGE:
        res = _adamw(local(params[n], n), reduced[n], local(mom[n], n), local(var[n], n), name=f"adamw_{n}")
        if n in TRANSPOSED:
            res = [jnp.swapaxes(r, 0, 1) for r in res]
        out_g[n], out_d[n], out_m[n], out_v[n] = [r[None] for r in res]
    d, nm, nv, gg = _adamw_small([small(params[n], n) for n in small_names], [reduced[n] for n in small_names],
                                 [small(mom[n], n) for n in small_names], [small(var[n], n) for n in small_names])
    for n, g, dd, mm, vv in zip(small_names, gg, d, nm, nv):
        out_g[n], out_d[n], out_m[n], out_v[n] = [small(r, n) for r in (g, dd, mm, vv)]

    return (loss, grad_x, *[out_g[n] for n in ORDER], *[out_d[n] for n in ORDER],
            *[out_m[n] for n in ORDER], *[out_v[n] for n in ORDER])
```

```python
import functools
import math

import numpy as np
import jax
import jax.numpy as jnp
from jax import lax
from jax.experimental import pallas as pl
from jax.experimental.pallas import tpu as pltpu

F32 = jnp.float32
BF16 = jnp.bfloat16
MESH = pl.DeviceIdType.MESH

D_MODEL = 1024
SEQ = 2048
HEAD_DIM = 64
A_GROUPS = 4
A_WIDTH = 256
B_HEADS = 12
B_WIDTH = 768
CHUNK = 128
DILATED = ((128, 1), (512, 4), (2048, 16))
NUM_BUCKETS = 32
MAX_DISTANCE = 2048
D_FF = 2816
IN_COLS = 2816
NORM_EPS = 1e-6
NEG_INF = -1e30
N_SHARD = 4
SHARD_FF = D_FF // N_SHARD
LANE_BLOCK = 256
VMEM_LIMIT = 56 * 1024 * 1024

ADAM_LR = 0.001
ADAM_B1 = 0.9
ADAM_B2 = 0.999
ADAM_EPS = 1e-08
ADAM_WD = 0.01
ADAM_STEP = 10

GELU_C = math.sqrt(2.0 / math.pi)
GELU_A = 0.044715

ANY = pl.BlockSpec(memory_space=pl.ANY)


def _params(sem=None):
    return pltpu.CompilerParams(dimension_semantics=sem, vmem_limit_bytes=VMEM_LIMIT)


def _dot(a, b, precision=None):
    return jnp.dot(a, b, preferred_element_type=F32, precision=precision)


def _dot_nt(a, b, precision=None):
    return lax.dot_general(a, b, (((1,), (1,)), ((), ())), preferred_element_type=F32, precision=precision)


def _dot_tn(a, b):
    return lax.dot_general(a, b, (((0,), (0,)), ((), ())), preferred_element_type=F32)


def _gelu(x):
    t = jnp.tanh(x * (GELU_C + (GELU_C * GELU_A) * (x * x)))
    return (0.5 * x) * (1.0 + t)


def _gelu_and_grad(x):
    x2 = x * x
    u = 1.0 + jnp.tanh(x * (GELU_C + (GELU_C * GELU_A) * x2))
    hx = 0.5 * x
    dg = u * (0.5 + hx * (2.0 - u) * (GELU_C + (3.0 * GELU_C * GELU_A) * x2))
    return hx * u, dg


def _mesh_pos():
    x, y, c = lax.axis_index("x"), lax.axis_index("y"), lax.axis_index("c")
    chips = [(1 - x, y), (x, 1 - y), (1 - x, 1 - y)]
    return x, y, c, chips


class _GatherPlan:
    def __init__(self, shapes, out_refs, send_sems, recv_sems):
        self.shapes, self.out_refs = shapes, out_refs
        self.send_sems, self.recv_sems = send_sems, recv_sems
        self.x, self.y, self.c, self.chips = _mesh_pos()
        self.sib = (self.x, self.y, 1 - self.c)

    def _half(self, t, chip, which):
        rows = self.shapes[t][0] // 2
        return self.out_refs[t].at[2 * chip[0] + chip[1], pl.ds(which * rows, rows), :]

    def _copy(self, k, src, dst, to):
        return pltpu.make_async_remote_copy(src_ref=src, dst_ref=dst, send_sem=self.send_sems.at[k],
                                            recv_sem=self.recv_sems.at[k], device_id=to, device_id_type=MESH)

    def _sends(self, t):
        own = self._half(t, (self.x, self.y), self.c)
        return [self._copy(6 * t + j, own, own, (*chip, self.c)) for j, chip in enumerate(self.chips)]

    def _forwards(self, t):
        return [self._copy(6 * t + 3 + j, self._half(t, chip, self.c), self._half(t, chip, self.c), self.sib)
                for j, chip in enumerate(self.chips)]

    def start(self, ts):
        for t in ts:
            for cp in self._sends(t):
                cp.start()

    def forward(self, ts):
        for t in ts:
            for j, chip in enumerate(self.chips):
                landed = self._half(t, chip, self.c)
                self._copy(6 * t + j, landed, landed, (*chip, self.c)).wait_recv()
            for cp in self._forwards(t):
                cp.start()

    def finish(self, ts):
        for t in ts:
            for j, chip in enumerate(self.chips):
                other = self._half(t, chip, 1 - self.c)
                self._copy(6 * t + 3 + j, other, other, self.sib).wait_recv()
        for t in ts:
            for cp in self._sends(t) + self._forwards(t):
                cp.wait_send()


class _RelayGatherPlan:
    def __init__(self, shapes, shard_refs, out_refs, send_sems, recv_sems):
        self.shapes, self.shard_refs, self.out_refs = shapes, shard_refs, out_refs
        self.send_sems, self.recv_sems = send_sems, recv_sems
        x, y, c, self.chips = _mesh_pos()
        self.me, self.c, self.sib = (x, y), c, (x, y, 1 - c)
        self.first = (x + c - 2 * x * c, y + (1 - c) - 2 * y * (1 - c))
        self.second = (x + (1 - c) - 2 * x * (1 - c), y + c - 2 * y * c)
        self.diag = (1 - x, 1 - y)

    def _half(self, t, chip, which):
        rows = self.shapes[t][0] // 2
        return self.out_refs[t].at[2 * chip[0] + chip[1], pl.ds(which * rows, rows), :]

    def _copy(self, k, src, dst, to):
        return pltpu.make_async_remote_copy(src_ref=src, dst_ref=dst, send_sem=self.send_sems.at[k],
                                            recv_sem=self.recv_sems.at[k], device_id=to, device_id_type=MESH)

    def _own(self, t):
        if self.shard_refs is None:
            return self._half(t, self.me, self.c)
        rows = self.shapes[t][0] // 2
        return self.shard_refs[t].at[pl.ds(self.c * rows, rows), :]

    def _step1(self, t):
        return self._copy(6 * t, self._own(t), self._half(t, self.me, self.c), (*self.first, self.c))

    def _step2(self, t):
        landed = self._half(t, self.first, self.c)
        return [self._copy(6 * t + 1, self._own(t), self._half(t, self.me, self.c), (*self.second, self.c)),
                self._copy(6 * t + 2, landed, landed, (*self.second, self.c))]

    def _forwards(self, t):
        return [self._copy(6 * t + 3 + j, self._half(t, chip, self.c), self._half(t, chip, self.c), self.sib)
                for j, chip in enumerate(self.chips)]

    def start(self, ts):
        for t in ts:
            self._step1(t).start()
            self._step2(t)[0].start()

    def relay(self, ts):
        for t in ts:
            landed = self._half(t, self.first, self.c)
            self._copy(6 * t, landed, landed, self.sib).wait_recv()
            self._step2(t)[1].start()

    def forward(self, ts):
        for t in ts:
            for k, chip in ((1, self.second), (2, self.diag)):
                landed = self._half(t, chip, self.c)
                self._copy(6 * t + k, landed, landed, self.sib).wait_recv()
            for cp in self._forwards(t):
                cp.start()

    def finish(self, ts):
        for t in ts:
            for j, chip in enumerate(self.chips):
                other = self._half(t, chip, 1 - self.c)
                self._copy(6 * t + 3 + j, other, other, self.sib).wait_recv()
        for t in ts:
            for cp in [self._step1(t)] + self._step2(t) + self._forwards(t):
                cp.wait_send()


class _SiblingExchangePlan:
    def __init__(self, shapes, grad_refs, out_refs, send_sems, recv_sems):
        self.shapes, self.grad_refs, self.out_refs = shapes, grad_refs, out_refs
        self.send_sems, self.recv_sems = send_sems, recv_sems
        self.x, self.y, self.c, _ = _mesh_pos()

    def _copies(self):
        out = []
        for t, (g, o) in enumerate(zip(self.grad_refs, self.out_refs)):
            rows = self.shapes[t][1] // 2
            out.append(pltpu.make_async_remote_copy(
                src_ref=g.at[:, pl.ds((1 - self.c) * rows, rows), :], dst_ref=o, send_sem=self.send_sems.at[t],
                recv_sem=self.recv_sems.at[t], device_id=(self.x, self.y, 1 - self.c), device_id_type=MESH))
        return out

    def start(self):
        for cp in self._copies():
            cp.start()

    def finish(self):
        for cp in self._copies():
            cp.wait()


class _ChipExchangePlan:
    def __init__(self, part_refs, out_refs, send_sems, recv_sems):
        self.part_refs, self.out_refs, self.send_sems, self.recv_sems = part_refs, out_refs, send_sems, recv_sems
        _, _, self.c, self.chips = _mesh_pos()

    def _copies(self):
        return [pltpu.make_async_remote_copy(
            src_ref=p.at[2 * chip[0] + chip[1]], dst_ref=o.at[j], send_sem=self.send_sems.at[3 * t + j],
            recv_sem=self.recv_sems.at[3 * t + j], device_id=(*chip, self.c), device_id_type=MESH)
            for t, (p, o) in enumerate(zip(self.part_refs, self.out_refs)) for j, chip in enumerate(self.chips)]

    def start(self):
        for cp in self._copies():
            cp.start()

    def finish(self):
        for cp in self._copies():
            cp.wait()


class _SmallAllReducePlan:
    def __init__(self, in_refs, out_refs, sib_refs, chip_refs, send_sems, recv_sems):
        self.in_refs, self.out_refs, self.sib_refs, self.chip_refs = in_refs, out_refs, sib_refs, chip_refs
        self.send_sems, self.recv_sems = send_sems, recv_sems
        self.n = len(in_refs)
        self.x, self.y, self.c, self.chips = _mesh_pos()

    def _copy(self, k, src, dst, to):
        return pltpu.make_async_remote_copy(src_ref=src, dst_ref=dst, send_sem=self.send_sems.at[k],
                                            recv_sem=self.recv_sems.at[k], device_id=to, device_id_type=MESH)

    def _first(self):
        return [self._copy(t, self.in_refs[t], self.sib_refs[t], (self.x, self.y, 1 - self.c)) for t in range(self.n)]

    def _second(self):
        return [self._copy(self.n + 3 * t + j, self.out_refs[t], self.chip_refs[t].at[j], (*chip, self.c))
                for t in range(self.n) for j, chip in enumerate(self.chips)]

    def start_sibling(self):
        for cp in self._first():
            cp.start()

    def sum_sibling_and_start_chips(self):
        for cp in self._first():
            cp.wait()
        for t in range(self.n):
            self.out_refs[t][...] = self.in_refs[t][...] + self.sib_refs[t][...]
        for cp in self._second():
            cp.start()

    def finish(self):
        for cp in self._second():
            cp.wait()
        for t in range(self.n):
            self.out_refs[t][...] = ((self.out_refs[t][...] + self.chip_refs[t][0])
                                     + (self.chip_refs[t][1] + self.chip_refs[t][2]))

    @staticmethod
    def scratch(arrays):
        return ([pltpu.VMEM(a.shape, F32) for a in arrays] + [pltpu.VMEM((3,) + a.shape, F32) for a in arrays]
                + _sem_pair(4 * len(arrays)))


def _sem_pair(n):
    return [pltpu.SemaphoreType.DMA((n,)), pltpu.SemaphoreType.DMA((n,))]


def _mm(a, b, *, dims, tm, tn, tk, out_dtype, name):
    if dims == "nn":
        m, k = a.shape
        n = b.shape[1]
        a_spec = pl.BlockSpec((tm, tk), lambda i, j, kk: (i, kk))
        b_spec = pl.BlockSpec((tk, tn), lambda i, j, kk: (kk, j))
        dot = _dot
    elif dims == "nt":
        m, k = a.shape
        n = b.shape[0]
        a_spec = pl.BlockSpec((tm, tk), lambda i, j, kk: (i, kk))
        b_spec = pl.BlockSpec((tn, tk), lambda i, j, kk: (j, kk))
        dot = _dot_nt
    else:
        k, m = a.shape
        n = b.shape[1]
        a_spec = pl.BlockSpec((tk, tm), lambda i, j, kk: (kk, i))
        b_spec = pl.BlockSpec((tk, tn), lambda i, j, kk: (kk, j))
        dot = _dot_tn
    assert m % tm == 0 and n % tn == 0 and k % tk == 0, (name, m, n, k)
    grid = (m // tm, n // tn, k // tk)
    nk = grid[2]
    own_acc = nk > 1 and out_dtype != F32

    def body(a_ref, b_ref, o_ref, *scratch):
        prod = dot(a_ref[...].astype(BF16), b_ref[...].astype(BF16))
        if nk == 1:
            o_ref[...] = prod.astype(out_dtype)
        else:
            acc_ref = scratch[0] if own_acc else o_ref
            kk = pl.program_id(2)

            @pl.when(kk == 0)
            def _():
                acc_ref[...] = prod

            @pl.when(kk > 0)
            def _():
                acc_ref[...] += prod

            if own_acc:
                @pl.when(kk == nk - 1)
                def _():
                    o_ref[...] = acc_ref[...].astype(out_dtype)

    return pl.pallas_call(
        body, grid=grid, in_specs=[a_spec, b_spec],
        out_specs=pl.BlockSpec((tm, tn), lambda i, j, kk: (i, j)),
        out_shape=jax.ShapeDtypeStruct((m, n), out_dtype),
        scratch_shapes=[pltpu.VMEM((tm, tn), F32)] if own_acc else [],
        compiler_params=_params(("parallel", "parallel", "arbitrary")), name=name,
    )(a, b)


def _mm_pair_tn(a1, a2, b, *, tm, tk, name):
    k, m = a1.shape
    n = b.shape[1]
    assert m % tm == 0 and k % tk == 0 and a2.shape == a1.shape, name
    nk = k // tk

    def body(a1_ref, a2_ref, b_ref, o1_ref, o2_ref, acc1_ref, acc2_ref):
        bv = b_ref[...]
        p1 = _dot_tn(a1_ref[...], bv)
        p2 = _dot_tn(a2_ref[...], bv)
        kk = pl.program_id(1)

        @pl.when(kk == 0)
        def _():
            acc1_ref[...] = p1
            acc2_ref[...] = p2

        @pl.when(kk > 0)
        def _():
            acc1_ref[...] += p1
            acc2_ref[...] += p2

        @pl.when(kk == nk - 1)
        def _():
            o1_ref[...] = acc1_ref[...].astype(BF16)
            o2_ref[...] = acc2_ref[...].astype(BF16)

    a_spec = pl.BlockSpec((tk, tm), lambda i, kk: (kk, i))
    o_spec = pl.BlockSpec((tm, n), lambda i, kk: (i, 0))
    return pl.pallas_call(
        body, grid=(m // tm, nk),
        in_specs=[a_spec, a_spec, pl.BlockSpec((tk, n), lambda i, kk: (kk, 0))],
        out_specs=[o_spec, o_spec],
        out_shape=[jax.ShapeDtypeStruct((m, n), BF16)] * 2,
        scratch_shapes=[pltpu.VMEM((tm, n), F32)] * 2,
        compiler_params=_params(("parallel", "arbitrary")), name=name,
    )(a1, a2, b)


def _mm_pair_nt(a, w1_t, w2_t, own, *, tm, tn, out_dtype, name):
    m, k = a.shape
    n = w1_t.shape[0]
    assert m % tm == 0 and n % tn == 0 and w2_t.shape == w1_t.shape, name
    grid = (m // tm, n // tn)
    n_steps = grid[0] * grid[1]

    def body(a_ref, w1_ref, w2_ref, own_ref, o1_ref, o2_ref, gat_ref, send_sems, recv_sems):
        del own_ref
        step = pl.program_id(0) * grid[1] + pl.program_id(1)
        gather = _GatherPlan([own.shape[1:]], [gat_ref], send_sems, recv_sems)

        @pl.when(step == 0)
        def _():
            gather.start([0])

        @pl.when(step == (2 * n_steps) // 3)
        def _():
            gather.forward([0])

        av = a_ref[...]
        o1_ref[...] = _dot_nt(av, w1_ref[...]).astype(out_dtype)
        o2_ref[...] = _dot_nt(av, w2_ref[...]).astype(out_dtype)

        @pl.when(step == n_steps - 1)
        def _():
            gather.finish([0])

    w_spec = pl.BlockSpec((tn, k), lambda i, j: (j, 0))
    o_spec = pl.BlockSpec((tm, tn), lambda i, j: (i, j))
    return pl.pallas_call(
        body, grid=grid,
        in_specs=[pl.BlockSpec((tm, k), lambda i, j: (i, 0)), w_spec, w_spec, ANY],
        out_specs=[o_spec, o_spec, ANY],
        out_shape=[jax.ShapeDtypeStruct((m, n), out_dtype)] * 2 + [jax.ShapeDtypeStruct(own.shape, own.dtype)],
        scratch_shapes=_sem_pair(6), input_output_aliases={3: 2},
        compiler_params=_params(("arbitrary", "arbitrary")), name=name,
    )(a, w1_t, w2_t, own)


def _out_proj_bwd(a, o, dy1, w_out, grads):
    m = dy1.shape[0]
    tm = 1024
    nx = len(grads)
    shapes = [g.shape for g in grads]
    n_steps = m // tm

    def body(a_ref, o_ref, dy_ref, w_ref, *rest):
        grad_refs = rest[:nx]
        dmix_ref, dw_ref = rest[nx:nx + 2]
        acc_ref = rest[2 * nx + 2]
        exchange = _SiblingExchangePlan(shapes, grad_refs, rest[nx + 2:2 * nx + 2], *rest[2 * nx + 3:])

        @pl.when(pl.program_id(0) == 0)
        def _():
            exchange.start()

        dy = dy_ref[...]
        dmix_ref[...] = _dot_nt(dy, w_ref[...])
        top = _dot_tn(a_ref[...], dy)
        bottom = _dot_tn(o_ref[...], dy)

        @pl.when(pl.program_id(0) == 0)
        def _():
            acc_ref[:A_WIDTH, :] = top
            acc_ref[A_WIDTH:, :] = bottom

        @pl.when(pl.program_id(0) > 0)
        def _():
            acc_ref[:A_WIDTH, :] += top
            acc_ref[A_WIDTH:, :] += bottom

        @pl.when(pl.program_id(0) == n_steps - 1)
        def _():
            dw_ref[...] = acc_ref[...].astype(BF16)
            exchange.finish()

    tile = lambda width: pl.BlockSpec((tm, width), lambda i: (i, 0))
    res = pl.pallas_call(
        body, grid=(n_steps,),
        in_specs=[tile(A_WIDTH), tile(B_WIDTH), tile(D_MODEL), _full_spec((D_MODEL, D_MODEL))] + [ANY] * nx,
        out_specs=[tile(D_MODEL), _full_spec((D_MODEL, D_MODEL))] + [ANY] * nx,
        out_shape=[jax.ShapeDtypeStruct((m, D_MODEL), F32), jax.ShapeDtypeStruct((D_MODEL, D_MODEL), BF16)]
        + [jax.ShapeDtypeStruct((N_SHARD, s[1] // 2, s[2]), g.dtype) for s, g in zip(shapes, grads)],
        scratch_shapes=[pltpu.VMEM((D_MODEL, D_MODEL), F32)] + _sem_pair(nx),
        compiler_params=_params(("arbitrary",)), name="out_proj_bwd",
    )(a, o, dy1, w_out, *grads)
    return res[:2], list(res[2:])


def _fused_rows(name, tm, mats, rows, vecs, fn, row_outs, acc_outs, exchange=()):
    m = mats[0][0].shape[0]
    nm, nr, nv, nro, nao, nx = len(mats), len(rows), len(vecs), len(row_outs), len(acc_outs), len(exchange)
    n_steps = m // tm

    def body(*refs):
        a_refs, w_refs = refs[:nm], refs[nm:2 * nm]
        pos = 2 * nm
        row_refs, vec_refs, part_refs = refs[pos:pos + nr], refs[pos + nr:pos + nr + nv], refs[pos + nr + nv:pos + nr + nv + nx]
        pos += nr + nv + nx
        out_refs, acc_refs, recv_refs = refs[pos:pos + nro], refs[pos + nro:pos + nro + nao], refs[pos + nro + nao:pos + nro + nao + nx]
        sems = refs[pos + nro + nao + nx:]
        i = pl.program_id(0)
        if nx:
            plan = _ChipExchangePlan(part_refs, recv_refs, *sems)

            @pl.when(i == 0)
            def _():
                plan.start()

        @pl.when(i == 0)
        def _():
            for r in acc_refs:
                r[...] = jnp.zeros_like(r)

        y = None
        for a_ref, w_ref, (_, _, dims, sl) in zip(a_refs, w_refs, mats):
            w = w_ref[...] if sl is None else w_ref[sl, :]
            part = (_dot if dims == "nn" else _dot_nt)(a_ref[...], w)
            y = part if y is None else y + part
        res = fn(y, *[r[...] for r in row_refs], *[v[...] for v in vec_refs])
        for r, val in zip(out_refs, res[:nro]):
            r[...] = val.astype(r.dtype)
        for r, val in zip(acc_refs, res[nro:]):
            r[...] += val

        if nx:
            @pl.when(i == n_steps - 1)
            def _():
                plan.finish()

    tile = lambda width: pl.BlockSpec((tm, width), lambda i: (i, 0))
    res = pl.pallas_call(
        body, grid=(n_steps,),
        in_specs=[tile(a.shape[1]) for a, _, _, _ in mats] + [_full_spec(w.shape) for _, w, _, _ in mats]
        + [tile(D_MODEL)] * nr + [_full_spec((1, D_MODEL))] * nv + [ANY] * nx,
        out_specs=[tile(D_MODEL)] * nro + [_full_spec(s) for s in acc_outs] + [ANY] * nx,
        out_shape=[jax.ShapeDtypeStruct((m, D_MODEL), dt) for dt in row_outs]
        + [jax.ShapeDtypeStruct(s, F32) for s in acc_outs]
        + [jax.ShapeDtypeStruct((3,) + p.shape[1:], p.dtype) for p in exchange],
        scratch_shapes=_sem_pair(3 * nx) if nx else [],
        compiler_params=_params(("arbitrary",)), name=name,
    )(*[a for a, _, _, _ in mats], *[w for _, w, _, _ in mats], *rows, *vecs, *exchange)
    return list(res[:nro + nao]), list(res[nro + nao:])


def _vec_spec(width=D_MODEL):
    return pl.BlockSpec((1, width), lambda i: (0, 0))


def _rstd(v):
    return lax.rsqrt(jnp.mean(v * v, axis=-1, keepdims=True) + NORM_EPS)


def _mid_fwd_rows(y1, x0, g2, g3):
    x1 = x0 + y1 * _rstd(y1) * g2
    return y1, x1, x1 * _rstd(x1) * g3


def _rms_bwd_rows(dout, v, g):
    r = _rstd(v)
    n = v * r
    dn = dout * g
    dv = r * (dn - n * jnp.mean(dn * n, axis=-1, keepdims=True))
    dg = jnp.sum(dout * n, axis=0, keepdims=True)
    return dv, dg


def _loss_head_rows(y2, x1, tgt, g4):
    x2 = x1 + y2 * _rstd(y2) * g4
    err = x2 - tgt
    loss = 0.5 * jnp.sum(jnp.mean(err * err, axis=-1, keepdims=True), axis=0, keepdims=True)
    dx2 = err * (1.0 / D_MODEL)
    dy2, dg4 = _rms_bwd_rows(dx2, y2, g4)
    return dx2, dy2, dg4, loss


def _mid_bwd_rows(dh2, x1, y1, dx2, g2, g3):
    d3, dg3 = _rms_bwd_rows(dh2, x1, g3)
    dx1 = dx2 + d3
    dy1, dg2 = _rms_bwd_rows(dx1, y1, g2)
    return dx1, dy1, dg2, dg3


def _in_bwd_rows(dh1, x0, dx1, g1):
    d1, dg1 = _rms_bwd_rows(dh1, x0, g1)
    return dx1 + d1, dg1


GATE_ROWS = 512


def _group_mean_matrix():
    p = np.zeros((A_WIDTH, A_WIDTH), np.float32)
    for g in range(A_GROUPS):
        p[g * HEAD_DIM:(g + 1) * HEAD_DIM, g * HEAD_DIM:(g + 1) * HEAD_DIM] = 1.0 / HEAD_DIM
    return jnp.asarray(p)


def _group_masks(width=A_WIDTH):
    lane = lax.broadcasted_iota(jnp.int32, (1, width), 1)
    return [(lane >= g * HEAD_DIM) & (lane < (g + 1) * HEAD_DIM) for g in range(width // HEAD_DIM)]


GROUP_SUM_PRECISION = lax.Precision.HIGH


def _layernorm_groups(vg, pavg):
    hi = GROUP_SUM_PRECISION
    mu = _dot(vg, pavg, hi)
    xc = vg - mu
    var = _dot(xc * xc, pavg, hi)
    rstd = lax.rsqrt(var + NORM_EPS)
    return xc * rstd, rstd


def _spatial_mix(w_bf, vn_chunk_bf, masks, bz):
    z = bz
    for g in range(A_GROUPS):
        z = z + jnp.where(masks[g], _dot(w_bf[g], vn_chunk_bf), 0.0)
    return z


def _full_spec(shape):
    return pl.BlockSpec(shape, lambda i: tuple(0 for _ in shape))


def _gate_fwd_rows(u, v, lg, lb, w_ref, bz, pavg, a_ref):
    masks = _group_masks()
    row = lax.broadcasted_iota(jnp.int32, (CHUNK, CHUNK), 0)
    col = lax.broadcasted_iota(jnp.int32, (CHUNK, CHUNK), 1)
    w_bf = [jnp.where(row >= col, w_ref[g], 0.0).astype(BF16) for g in range(A_GROUPS)]
    ug = _gelu(u)
    vhat, _ = _layernorm_groups(_gelu(v), pavg)
    vn = vhat * lg + lb
    for c in range(GATE_ROWS // CHUNK):
        sl = slice(c * CHUNK, (c + 1) * CHUNK)
        z = _spatial_mix(w_bf, vn[sl].astype(BF16), masks, bz)
        a_ref[sl, :] = (ug[sl] * z).astype(BF16)


def _gate_bwd(uv, dmix, ln_g, ln_b, w_s, w_st, bz, grads):
    m = uv.shape[0]
    pavg = _group_mean_matrix()
    nsteps = m // GATE_ROWS
    nx = len(grads)
    shapes = [g.shape for g in grads]

    def body(u_ref, v_ref, da_ref, lg_ref, lb_ref, w_ref, wt_ref, bz_ref, p_ref, *rest):
        grad_refs = rest[:nx]
        duv_ref, dlg_ref, dlb_ref, dw_ref, dbz_ref = rest[nx:nx + 5]
        recv_refs = rest[nx + 5:2 * nx + 5]
        exchange = _SiblingExchangePlan(shapes, grad_refs, recv_refs, *rest[2 * nx + 5:])
        i = pl.program_id(0)

        @pl.when(i == 0)
        def _():
            exchange.start()
            dlg_ref[...] = jnp.zeros_like(dlg_ref)
            dlb_ref[...] = jnp.zeros_like(dlb_ref)
            dw_ref[...] = jnp.zeros_like(dw_ref)
            dbz_ref[...] = jnp.zeros_like(dbz_ref)

        hi = GROUP_SUM_PRECISION
        masks = _group_masks()
        row = lax.broadcasted_iota(jnp.int32, (CHUNK, CHUNK), 0)
        col = lax.broadcasted_iota(jnp.int32, (CHUNK, CHUNK), 1)
        tril = row >= col
        w_bf = [jnp.where(tril, w_ref[g], 0.0).astype(BF16) for g in range(A_GROUPS)]
        wt_bf = [jnp.where(col >= row, wt_ref[g], 0.0).astype(BF16) for g in range(A_GROUPS)]
        pavg_v = p_ref[...]
        lg = lg_ref[...]
        ug, dug = _gelu_and_grad(u_ref[...])
        vg, dvg_dx = _gelu_and_grad(v_ref[...])
        vhat, rstd = _layernorm_groups(vg, pavg_v)
        vn = vhat * lg + lb_ref[...]
        da = da_ref[...]
        bz = bz_ref[...]
        for c in range(GATE_ROWS // CHUNK):
            sl = slice(c * CHUNK, (c + 1) * CHUNK)
            vn_bf = vn[sl].astype(BF16)
            z = _spatial_mix(w_bf, vn_bf, masks, bz)
            dz = da[sl] * ug[sl]
            duv_ref[sl, 0:A_WIDTH] = (da[sl] * z * dug[sl]).astype(BF16)
            dbz_ref[...] += dz
            dz_bf = dz.astype(BF16)
            dvn = jnp.zeros((CHUNK, A_WIDTH), F32)
            for g in range(A_GROUPS):
                dz_g = jnp.where(masks[g], dz, 0.0).astype(BF16)
                dw_ref[g] += jnp.where(tril, _dot_nt(dz_g, vn_bf), 0.0)
                dvn = dvn + jnp.where(masks[g], _dot(wt_bf[g], dz_bf), 0.0)
            vh = vhat[sl]
            dlb_ref[...] += jnp.sum(dvn, axis=0, keepdims=True)
            dlg_ref[...] += jnp.sum(dvn * vh, axis=0, keepdims=True)
            dvh = dvn * lg
            m1 = _dot(dvh, pavg_v, hi)
            m2 = _dot(dvh * vh, pavg_v, hi)
            duv_ref[sl, A_WIDTH:2 * A_WIDTH] = (rstd[sl] * (dvh - m1 - vh * m2) * dvg_dx[sl]).astype(BF16)

        @pl.when(i == nsteps - 1)
        def _():
            dbz_ref[...] = _dot(dbz_ref[...], pavg_v * float(HEAD_DIM), hi)
            exchange.finish()

    res = pl.pallas_call(
        body, grid=(nsteps,),
        in_specs=[pl.BlockSpec((GATE_ROWS, A_WIDTH), lambda i: (i, 0)),
                  pl.BlockSpec((GATE_ROWS, A_WIDTH), lambda i: (i, 1)),
                  pl.BlockSpec((GATE_ROWS, A_WIDTH), lambda i: (i, 0)),
                  _full_spec((1, A_WIDTH)), _full_spec((1, A_WIDTH)), _full_spec((A_GROUPS, CHUNK, CHUNK)),
                  _full_spec((A_GROUPS, CHUNK, CHUNK)), _full_spec((CHUNK, A_WIDTH)),
                  _full_spec((A_WIDTH, A_WIDTH))] + [ANY] * nx,
        out_specs=[pl.BlockSpec((GATE_ROWS, 2 * A_WIDTH), lambda i: (i, 0)),
                   _full_spec((1, A_WIDTH)), _full_spec((1, A_WIDTH)), _full_spec((A_GROUPS, CHUNK, CHUNK)),
                   _full_spec((CHUNK, A_WIDTH))] + [ANY] * nx,
        out_shape=[jax.ShapeDtypeStruct((m, IN_COLS), BF16),
                   jax.ShapeDtypeStruct((1, A_WIDTH), F32), jax.ShapeDtypeStruct((1, A_WIDTH), F32),
                   jax.ShapeDtypeStruct((A_GROUPS, CHUNK, CHUNK), F32),
                   jax.ShapeDtypeStruct((CHUNK, A_WIDTH), F32)]
        + [jax.ShapeDtypeStruct((N_SHARD, s[1] // 2, s[2]), g.dtype) for s, g in zip(shapes, grads)],
        scratch_shapes=_sem_pair(nx),
        compiler_params=_params(("arbitrary",)), name="gate_bwd",
    )(uv, uv, dmix, ln_g, ln_b, w_s, w_st, bz, pavg, *grads)
    return res[:5], list(res[5:])


Q_BLOCK = 128
PAIR = 2 * HEAD_DIM
N_PAIR = B_HEADS // 2
N_CFG = len(DILATED)
BLOCKS_PER_CFG = SEQ // Q_BLOCK
QKV_SLABS = 3 * N_PAIR
FWD_BLOCKS_PER_TRIP = 8
BWD_BLOCKS_PER_TRIP = 4


def _t5_bucket_np(dist, dtype):
    max_exact = NUM_BUCKETS // 2
    d = np.maximum(dist, 1).astype(dtype)
    large = max_exact + (np.log(d / dtype(max_exact)) / dtype(math.log(MAX_DISTANCE / max_exact))
                         * dtype(NUM_BUCKETS - max_exact))
    large = np.minimum(large.astype(np.int32), NUM_BUCKETS - 1)
    return np.where(dist < max_exact, dist, large)


def _bucket_tables():
    i = np.arange(Q_BLOCK)[:, None]
    j = np.arange(Q_BLOCK)[None, :]
    tables = []
    for _, dil in DILATED:
        rel_prev = Q_BLOCK + i - j
        rel_cur = i - j
        rel = np.concatenate([rel_prev, rel_cur], axis=1)
        valid = np.concatenate([rel_prev <= Q_BLOCK, rel_cur >= 0], axis=1)
        dist = np.maximum(rel, 0) * dil
        b32 = _t5_bucket_np(dist, np.float32)
        b64 = _t5_bucket_np(dist, np.float64)
        assert np.array_equal(b32, b64)
        tables.append(np.where(valid, b32, -1).astype(np.int32))
    return np.stack(tables)


def _present_buckets(buckets_np):
    return [sorted(set(int(v) for v in np.unique(buckets_np[c]) if v >= 0)) for c in range(N_CFG)]


def _bias_tables_body(buckets_np):
    present = _present_buckets(buckets_np)

    def tables(rb_ref, bk_ref, o_ref, ot_ref):
        for c in range(N_CFG):
            bk = bk_ref[c]
            for h in range(B_HEADS):
                acc = jnp.full((Q_BLOCK, 2 * Q_BLOCK), NEG_INF, F32)
                for b in present[c]:
                    acc = jnp.where(bk == b, rb_ref[h, b], acc)
                o_ref[c, h] = acc
                ot_ref[c, h] = acc.T

    return tables


def _proj_fwd(x, g1, w_in_t, ln_g, ln_b, w_s, bz):
    m = x.shape[0]
    tm = GATE_ROWS
    pavg = _group_mean_matrix()

    def body(x_ref, g_ref, w_ref, lg_ref, lb_ref, ws_ref, bz_ref, p_ref, h_ref, uv_ref, qkv_ref, a_ref):
        xv = x_ref[...]
        h = (xv * _rstd(xv) * g_ref[...]).astype(BF16)
        h_ref[...] = h
        acc = _dot_nt(h, w_ref[...])
        uv_ref[...] = acc[:, :2 * A_WIDTH]
        for s in range(QKV_SLABS):
            qkv_ref[s] = acc[:, 2 * A_WIDTH + s * PAIR:2 * A_WIDTH + (s + 1) * PAIR]
        _gate_fwd_rows(acc[:, :A_WIDTH], acc[:, A_WIDTH:2 * A_WIDTH], lg_ref[...], lb_ref[...], ws_ref,
                       bz_ref[...], p_ref[...], a_ref)

    return pl.pallas_call(
        body, grid=(m // tm,),
        in_specs=[pl.BlockSpec((tm, D_MODEL), lambda i: (i, 0)), _vec_spec(),
                  pl.BlockSpec((IN_COLS, D_MODEL), lambda i: (0, 0)),
                  _full_spec((1, A_WIDTH)), _full_spec((1, A_WIDTH)), _full_spec((A_GROUPS, CHUNK, CHUNK)),
                  _full_spec((CHUNK, A_WIDTH)), _full_spec((A_WIDTH, A_WIDTH))],
        out_specs=[pl.BlockSpec((tm, D_MODEL), lambda i: (i, 0)),
                   pl.BlockSpec((tm, 2 * A_WIDTH), lambda i: (i, 0)),
                   pl.BlockSpec((QKV_SLABS, tm, PAIR), lambda i: (0, i, 0)),
                   pl.BlockSpec((tm, A_WIDTH), lambda i: (i, 0))],
        out_shape=[jax.ShapeDtypeStruct((m, D_MODEL), BF16), jax.ShapeDtypeStruct((m, 2 * A_WIDTH), F32),
                   jax.ShapeDtypeStruct((QKV_SLABS, m, PAIR), F32), jax.ShapeDtypeStruct((m, A_WIDTH), BF16)],
        compiler_params=_params(("parallel",)), name="proj_fwd",
    )(x, g1, w_in_t, ln_g, ln_b, w_s, bz, pavg)


def _pair_masks():
    lane = lax.broadcasted_iota(jnp.int32, (1, PAIR), 1)
    return [lane < HEAD_DIM, lane >= HEAD_DIM]


def _block_rows(idx, dil):
    static = isinstance(idx, int)
    r, n = idx % dil, idx // dil

    def rows_of(block):
        start = r + (dil * Q_BLOCK) * block
        if dil == 1:
            return pl.ds(start if static else pl.multiple_of(start, Q_BLOCK), Q_BLOCK)
        return pl.ds(start, Q_BLOCK, stride=dil)

    prev = rows_of(n - 1) if not static or n > 0 else None
    return rows_of(n), prev


def _attn_fwd(qkv, bias, batch, owns):
    m = qkv.shape[1]
    comb_rows = 256
    nt = len(owns)
    shapes = [g.shape[1:] for g in owns]
    n_steps = batch * N_PAIR
    ts = list(range(nt))

    def body(q_ref, k_ref, v_ref, b_ref, *rest):
        o_ref, l_ref = rest[nt:nt + 2]
        gat_refs = rest[nt + 2:2 * nt + 2]
        scratch = rest[2 * nt + 2:]
        oc_refs, lc_refs = scratch[:N_CFG], scratch[N_CFG:2 * N_CFG]
        step = pl.program_id(0) * N_PAIR + pl.program_id(1)
        gather = _RelayGatherPlan(shapes, None, gat_refs, *scratch[2 * N_CFG:])

        @pl.when(step == 0)
        def _():
            gather.start(ts)

        @pl.when(step == n_steps // 2)
        def _():
            gather.relay(ts)

        @pl.when(step == n_steps - 2)
        def _():
            gather.forward(ts)

        masks = _pair_masks()
        for ci, (_, dil) in enumerate(DILATED):
            nb = SEQ // dil // Q_BLOCK

            def block(trip, ci=ci, dil=dil, nb=nb):
                work = []
                for u in range(FWD_BLOCKS_PER_TRIP):
                    rows, prow = _block_rows(trip * FWD_BLOCKS_PER_TRIP + u, dil)
                    has_prev = nb > 1 and prow is not None
                    q = q_ref[rows, :] * 0.125
                    kc = k_ref[rows, :].astype(BF16)
                    vc = v_ref[rows, :]
                    kp = k_ref[prow, :].astype(BF16) if has_prev else None
                    vp = v_ref[prow, :] if has_prev else None
                    tiles = []
                    for h in range(2):
                        qh = jnp.where(masks[h], q, 0.0).astype(BF16)
                        sc = _dot_nt(qh, kc) + b_ref[ci, h, :, Q_BLOCK:]
                        sp = _dot_nt(qh, kp) + b_ref[ci, h, :, :Q_BLOCK] if has_prev else None
                        tiles.append((sc, sp))
                    work.append((rows, vc, vp, tiles))
                probs = []
                for _, _, _, tiles in work:
                    ps = []
                    for sc, sp in tiles:
                        mx = jnp.max(sc if sp is None else jnp.maximum(sc, sp), axis=1, keepdims=True)
                        pc = jnp.exp(sc - mx).astype(BF16)
                        pp = None if sp is None else jnp.exp(sp - mx).astype(BF16)
                        ps.append((mx, pc, pp))
                    probs.append(ps)
                for (rows, vc, vp, _), ps in zip(work, probs):
                    res = []
                    for h, (_, pc, pp) in enumerate(ps):
                        r = _dot(pc, jnp.where(masks[h], vc, 1.0).astype(BF16))
                        if pp is not None:
                            r = r + _dot(pp, jnp.where(masks[h], vp, 1.0).astype(BF16))
                        res.append(r)
                    num = jnp.where(masks[0], res[0], res[1])
                    den = pltpu.roll(jnp.where(masks[0], res[1], res[0]), HEAD_DIM, 1)
                    oc_refs[ci][rows, :] = num / den
                    lc_refs[ci][rows, :] = jnp.where(masks[0], ps[0][0], ps[1][0]) + jnp.log(den)

            for trip in range(BLOCKS_PER_CFG // FWD_BLOCKS_PER_TRIP):
                block(trip)

        def combine(i, carry):
            rr = pl.ds(pl.multiple_of(i * comb_rows, comb_rows), comb_rows)
            ls = [lc_refs[c][rr, :] for c in range(N_CFG)]
            mx = functools.reduce(jnp.maximum, ls)
            ws = [jnp.exp(l - mx) for l in ls]
            tot = functools.reduce(lambda a, b: a + b, ws)
            o = functools.reduce(lambda a, b: a + b, [ws[c] * oc_refs[c][rr, :] for c in range(N_CFG)]) / tot
            o_ref[rr, :] = o.astype(BF16)
            l_ref[rr, :] = mx + jnp.log(tot)
            return carry

        lax.fori_loop(0, SEQ // comb_rows, combine, 0)

        @pl.when(step == n_steps - 1)
        def _():
            gather.finish(ts)

    def slab(first):
        return pl.BlockSpec((None, SEQ, PAIR), lambda b, p: (first + p, b, 0))

    nat = pl.BlockSpec((SEQ, PAIR), lambda b, p: (b, p))
    res = pl.pallas_call(
        body, grid=(batch, N_PAIR),
        in_specs=[slab(0), slab(N_PAIR), slab(2 * N_PAIR),
                  pl.BlockSpec((N_CFG, 2, Q_BLOCK, 2 * Q_BLOCK), lambda b, p: (0, p, 0, 0))] + [ANY] * nt,
        out_specs=[nat, nat] + [ANY] * nt,
        out_shape=[jax.ShapeDtypeStruct((m, B_WIDTH), BF16), jax.ShapeDtypeStruct((m, B_WIDTH), F32)]
        + [jax.ShapeDtypeStruct(g.shape, g.dtype) for g in owns],
        scratch_shapes=[pltpu.VMEM((SEQ, PAIR), F32)] * (2 * N_CFG) + _sem_pair(6 * nt),
        input_output_aliases={4 + t: 2 + t for t in range(nt)},
        compiler_params=_params(("arbitrary", "arbitrary")), name="attn_fwd",
    )(qkv, qkv, qkv, bias, *owns)
    return res[0], res[1], list(res[2:])


def _attn_bwd(qkv, dmix, o, lse, bias_t, dproj, batch, parts, smalls):
    m = qkv.shape[1]
    nt, ns = len(parts), len(smalls)
    n_steps = N_PAIR * batch

    def body(q_ref, k_ref, v_ref, do_ref, o_ref, l_ref, b_ref, *rest):
        part_refs = rest[1:nt + 1]
        small_refs = rest[nt + 1:nt + 1 + ns]
        pos = nt + 1 + ns
        dproj_ref, ds_ref = rest[pos:pos + 2]
        recv_refs = rest[pos + 2:pos + 2 + nt]
        sum_refs = rest[pos + 2 + nt:pos + 2 + nt + ns]
        pos += 2 + nt + ns
        dq_acc, dk_acc, dv_acc, d_scr, stage, stage_sems, send_sems, recv_sems = rest[pos:pos + 8]
        allreduce = _SmallAllReducePlan(small_refs, sum_refs, rest[pos + 8:pos + 8 + ns],
                                        rest[pos + 8 + ns:pos + 8 + 2 * ns], *rest[pos + 8 + 2 * ns:])
        pair, seq = pl.program_id(0), pl.program_id(1)
        step = pair * batch + seq
        exchange = _ChipExchangePlan(part_refs, recv_refs, send_sems, recv_sems)

        @pl.when(step == 0)
        def _():
            allreduce.start_sibling()

        @pl.when(step == n_steps // 2)
        def _():
            allreduce.sum_sibling_and_start_chips()

        def stage_copies():
            rows = pl.ds(pl.multiple_of(seq * SEQ, SEQ), SEQ)
            return [pltpu.make_async_copy(
                stage.at[k],
                dproj_ref.at[rows, pl.ds(pl.multiple_of(2 * A_WIDTH + k * B_WIDTH + pair * PAIR, PAIR), PAIR)],
                stage_sems.at[k]) for k in range(3)]

        @pl.when(step == 0)
        def _():
            exchange.start()

        @pl.when(pl.program_id(1) == 0)
        def _():
            ds_ref[...] = jnp.zeros_like(ds_ref)

        dq_acc[...] = jnp.zeros_like(dq_acc)
        dk_acc[...] = jnp.zeros_like(dk_acc)
        dv_acc[...] = jnp.zeros_like(dv_acc)
        d_scr[...] = do_ref[...] * o_ref[...].astype(F32)
        masks = _pair_masks()

        def stack_heads(t):
            return jnp.concatenate([jnp.where(masks[0], t, 0.0), jnp.where(masks[1], t, 0.0)], axis=0).astype(BF16)

        for ci, (_, dil) in enumerate(DILATED):
            nb = SEQ // dil // Q_BLOCK

            def block(trip, carry, ci=ci, dil=dil, nb=nb):
                first = []
                for u in range(BWD_BLOCKS_PER_TRIP):
                    rows, prow = _block_rows(trip * BWD_BLOCKS_PER_TRIP + u, dil)
                    has_prev = nb > 1 and prow is not None
                    if has_prev:
                        kcat = jnp.concatenate([k_ref[prow, :], k_ref[rows, :]], axis=0).astype(BF16)
                        vcat = jnp.concatenate([v_ref[prow, :], v_ref[rows, :]], axis=0).astype(BF16)
                    else:
                        kcat = k_ref[rows, :].astype(BF16)
                        vcat = v_ref[rows, :].astype(BF16)
                    qst = stack_heads(q_ref[rows, :] * 0.125)
                    dost = stack_heads(do_ref[rows, :])
                    lt = l_ref[rows, :].T
                    dt = d_scr[rows, :].T
                    lrow = jnp.concatenate([lt[0:1], lt[HEAD_DIM:HEAD_DIM + 1]], axis=1)
                    drow = jnp.concatenate([jnp.sum(dt[:HEAD_DIM], axis=0, keepdims=True),
                                            jnp.sum(dt[HEAD_DIM:], axis=0, keepdims=True)], axis=1)
                    first.append((has_prev, rows, prow, kcat, qst, dost, lrow, drow,
                                  _dot_nt(kcat, qst), _dot_nt(vcat, dost)))
                second = []
                for has_prev, rows, prow, kcat, qst, dost, lrow, drow, st, dpt in first:
                    keys = slice(0, 2 * Q_BLOCK) if has_prev else slice(Q_BLOCK, 2 * Q_BLOCK)
                    bt = jnp.concatenate([b_ref[ci, 0, keys, :], b_ref[ci, 1, keys, :]], axis=1)
                    pt = jnp.exp(st + bt - lrow)
                    dst = pt * (dpt - drow)
                    ds_ref[ci, 0, keys, :] += dst[:, :Q_BLOCK]
                    ds_ref[ci, 1, keys, :] += dst[:, Q_BLOCK:]
                    second.append((has_prev, rows, prow, kcat, qst, dost, pt.astype(BF16), dst.astype(BF16)))
                for has_prev, rows, prow, kcat, qst, dost, pt_bf, dst_bf in second:
                    dk = _dot(dst_bf, qst)
                    dv = _dot(pt_bf, dost)
                    dq2 = _dot_tn(dst_bf, kcat)
                    dq_acc[rows, :] += jnp.where(masks[0], dq2[:Q_BLOCK], dq2[Q_BLOCK:]) * 0.125
                    if has_prev:
                        dk_acc[prow, :] += dk[:Q_BLOCK]
                        dv_acc[prow, :] += dv[:Q_BLOCK]
                        dk_acc[rows, :] += dk[Q_BLOCK:]
                        dv_acc[rows, :] += dv[Q_BLOCK:]
                    else:
                        dk_acc[rows, :] += dk
                        dv_acc[rows, :] += dv
                return carry

            for trip in range(BLOCKS_PER_CFG // BWD_BLOCKS_PER_TRIP):
                block(trip, 0)

        @pl.when(step > 0)
        def _():
            for cp in stage_copies():
                cp.wait()

        stage[0] = dq_acc[...].astype(BF16)
        stage[1] = dk_acc[...].astype(BF16)
        stage[2] = dv_acc[...].astype(BF16)
        for cp in stage_copies():
            cp.start()

        @pl.when(step == n_steps - 1)
        def _():
            for cp in stage_copies():
                cp.wait()
            exchange.finish()
            allreduce.finish()

    def slab(first):
        return pl.BlockSpec((None, SEQ, PAIR), lambda p, b: (first + p, b, 0))

    nat = pl.BlockSpec((SEQ, PAIR), lambda p, b: (b, p))
    tbl = pl.BlockSpec((N_CFG, 2, 2 * Q_BLOCK, Q_BLOCK), lambda p, b: (0, p, 0, 0))
    acc = pltpu.VMEM((SEQ, PAIR), F32)
    vm = pl.BlockSpec(memory_space=pltpu.VMEM)
    res = pl.pallas_call(
        body, grid=(N_PAIR, batch),
        in_specs=[slab(0), slab(N_PAIR), slab(2 * N_PAIR),
                  pl.BlockSpec((SEQ, PAIR), lambda p, b: (b, A_WIDTH // PAIR + p)), nat, nat, tbl]
        + [ANY] * (nt + 1) + [vm] * ns,
        out_specs=[ANY, tbl] + [ANY] * nt + [vm] * ns,
        out_shape=[jax.ShapeDtypeStruct(dproj.shape, dproj.dtype),
                   jax.ShapeDtypeStruct((N_CFG, B_HEADS, 2 * Q_BLOCK, Q_BLOCK), F32)]
        + [jax.ShapeDtypeStruct((3,) + p.shape[1:], p.dtype) for p in parts]
        + [jax.ShapeDtypeStruct(a.shape, F32) for a in smalls],
        input_output_aliases={7: 0},
        scratch_shapes=[acc, acc, acc, acc, pltpu.VMEM((3, SEQ, PAIR), BF16), pltpu.SemaphoreType.DMA((3,))]
        + _sem_pair(3 * nt) + _SmallAllReducePlan.scratch(smalls),
        compiler_params=_params(("arbitrary", "arbitrary")), name="attn_bwd",
    )(qkv, qkv, qkv, dmix, o, lse, bias_t, dproj, *parts, *smalls)
    return res[0], res[1], list(res[2:2 + nt]), list(res[2 + nt:])


def _rel_bias_grad(ds, buckets_np, grads):
    present = _present_buckets(buckets_np)
    nx = len(grads)
    shapes = [g.shape for g in grads]

    def body(bk_ref, ds_ref, *rest):
        o_ref = rest[nx]
        acc_ref = rest[2 * nx + 1]
        exchange = _SiblingExchangePlan(shapes, rest[:nx], rest[nx + 1:2 * nx + 1], *rest[2 * nx + 2:])
        exchange.start()
        acc_ref[...] = jnp.zeros_like(acc_ref)
        for c in range(N_CFG):
            bk = bk_ref[c]
            for h in range(B_HEADS):
                dsv = ds_ref[c, h]
                for b in present[c]:
                    part = jnp.sum(jnp.where(bk == b, dsv, 0.0), axis=0, keepdims=True)
                    acc_ref[pl.ds(h * NUM_BUCKETS + b, 1), :] += part
        o_ref[...] = jnp.sum(acc_ref[...], axis=1, keepdims=True)
        exchange.finish()

    vm = pl.BlockSpec(memory_space=pltpu.VMEM)
    res = pl.pallas_call(
        body, in_specs=[vm, vm] + [ANY] * nx, out_specs=[vm] + [ANY] * nx,
        out_shape=[jax.ShapeDtypeStruct((B_HEADS * NUM_BUCKETS, 1), F32)]
        + [jax.ShapeDtypeStruct((N_SHARD, s[1] // 2, s[2]), g.dtype) for s, g in zip(shapes, grads)],
        scratch_shapes=[pltpu.VMEM((B_HEADS * NUM_BUCKETS, buckets_np.shape[-1]), F32)] + _sem_pair(nx),
        compiler_params=_params(), name="rel_bias_grad",
    )(jnp.asarray(buckets_np), ds, *grads)
    return res[0], list(res[1:])


CONV_CHUNK = 64
CONV_HALO = 16


def _row_index():
    return lax.broadcasted_iota(jnp.int32, (SEQ, LANE_BLOCK), 0)


def _shift_down(x, k, row):
    return jnp.where(row >= k, pltpu.roll(x, k, 0), 0.0)


def _shift_up(x, k, row):
    return jnp.where(row < SEQ - k, pltpu.roll(x, SEQ - k, 0), 0.0)


def _convgate_fwd(gate, up, conv_w, conv_b, batch):
    m = gate.shape[0]

    def body(g_ref, u_ref, w_ref, b_ref, a_ref):
        w = w_ref[...]
        w0, w1, w2, b = w[0:1], w[1:2], w[2:3], b_ref[...]

        def finish(r0, g, g1, g2):
            c = b + w0 * g2 + w1 * g1 + w2 * g
            rows = pl.ds(r0, CONV_CHUNK)
            a_ref[rows, :] = _gelu(c).astype(BF16) * u_ref[rows, :]

        g = g_ref[0:CONV_CHUNK, :].astype(F32)
        row = lax.broadcasted_iota(jnp.int32, (CONV_CHUNK, LANE_BLOCK), 0)
        finish(0, g, jnp.where(row >= 1, pltpu.roll(g, 1, 0), 0.0), jnp.where(row >= 2, pltpu.roll(g, 2, 0), 0.0))

        for r0 in range(CONV_CHUNK, SEQ, CONV_CHUNK):
            gh = g_ref[pl.ds(r0 - CONV_HALO, CONV_CHUNK + CONV_HALO), :].astype(F32)
            finish(r0, gh[CONV_HALO:], pltpu.roll(gh, 1, 0)[CONV_HALO:], pltpu.roll(gh, 2, 0)[CONV_HALO:])

    blk = pl.BlockSpec((SEQ, LANE_BLOCK), lambda b, j: (b, j))
    return pl.pallas_call(
        body, grid=(batch, D_FF // LANE_BLOCK),
        in_specs=[blk, blk, pl.BlockSpec((3, LANE_BLOCK), lambda b, j: (0, j)),
                  pl.BlockSpec((1, LANE_BLOCK), lambda b, j: (0, j))],
        out_specs=blk,
        out_shape=jax.ShapeDtypeStruct((m, D_FF), BF16),
        compiler_params=_params(("parallel", "parallel")), name="convgate_fwd",
    )(gate, up, conv_w, conv_b)


def _convgate_bwd(gate, up, dact, conv_w, conv_b, batch):
    m = gate.shape[0]

    def body(g_ref, u_ref, da_ref, w_ref, b_ref, dg_ref, du_ref, dw_ref, db_ref):
        @pl.when(pl.program_id(1) == 0)
        def _():
            dw_ref[...] = jnp.zeros_like(dw_ref)
            db_ref[...] = jnp.zeros_like(db_ref)

        g = g_ref[...].astype(F32)
        w = w_ref[...]
        row = _row_index()
        g1 = _shift_down(g, 1, row)
        g2 = _shift_down(g, 2, row)
        c = b_ref[...] + w[0:1] * g2 + w[1:2] * g1 + w[2:3] * g
        gg, dgg = _gelu_and_grad(c)
        da = da_ref[...].astype(F32)
        du_ref[...] = (da * gg).astype(BF16)
        dc = da * u_ref[...].astype(F32) * dgg
        db_ref[...] += jnp.sum(dc, axis=0, keepdims=True)
        dw_ref[0:1, :] += jnp.sum(dc * g2, axis=0, keepdims=True)
        dw_ref[1:2, :] += jnp.sum(dc * g1, axis=0, keepdims=True)
        dw_ref[2:3, :] += jnp.sum(dc * g, axis=0, keepdims=True)
        dg_ref[...] = (w[2:3] * dc + w[1:2] * _shift_up(dc, 1, row) + w[0:1] * _shift_up(dc, 2, row)).astype(BF16)

    blk = pl.BlockSpec((SEQ, LANE_BLOCK), lambda j, b: (b, j))
    wspec = pl.BlockSpec((3, LANE_BLOCK), lambda j, b: (0, j))
    bspec = pl.BlockSpec((1, LANE_BLOCK), lambda j, b: (0, j))
    return pl.pallas_call(
        body, grid=(D_FF // LANE_BLOCK, batch),
        in_specs=[blk, blk, blk, wspec, bspec],
        out_specs=[blk, blk, wspec, bspec],
        out_shape=[jax.ShapeDtypeStruct((m, D_FF), BF16), jax.ShapeDtypeStruct((m, D_FF), BF16),
                   jax.ShapeDtypeStruct((3, D_FF), F32), jax.ShapeDtypeStruct((1, D_FF), F32)],
        compiler_params=_params(("parallel", "arbitrary")), name="convgate_bwd",
    )(gate, up, dact, conv_w, conv_b)


def _gather_weights(shards, conv_w_shard, rel_bias, buckets_np):
    nt = len(shards)
    shapes = [sh.shape for sh in shards]
    ts = list(range(nt))
    tables = _bias_tables_body(buckets_np)

    def body(*refs):
        shard_refs = refs[:nt]
        cw_ref, rb_ref, bk_ref = refs[nt:nt + 3]
        out_refs = refs[nt + 3:2 * nt + 3]
        cw_out, bias_ref, bias_t_ref = refs[2 * nt + 3:2 * nt + 6]
        scratch = refs[2 * nt + 6:]
        f32_refs, bf16_refs = scratch[:nt], scratch[nt:2 * nt]
        load_sems, store_sems, send_sems, recv_sems, cw_send, cw_recv = scratch[2 * nt:]
        plan = _RelayGatherPlan(shapes[:1], bf16_refs[:1], out_refs[:1], send_sems, recv_sems)
        x, y, c, chips = _mesh_pos()
        loads = [pltpu.make_async_copy(shard_refs[t], f32_refs[t], load_sems.at[t]) for t in ts]
        stores = [pltpu.make_async_copy(bf16_refs[t], out_refs[t].at[2 * x + y], store_sems.at[t]) for t in ts]
        stores.append(pltpu.make_async_copy(cw_ref, cw_out.at[2 * x + y], store_sems.at[nt]))

        def cw_copy(j, src, dst, chip):
            return pltpu.make_async_remote_copy(src_ref=src, dst_ref=dst, send_sem=cw_send.at[j],
                                                recv_sem=cw_recv.at[j], device_id=(*chip, c), device_id_type=MESH)

        def to_bf16(t):
            loads[t].wait()
            bf16_refs[t][...] = f32_refs[t][...].astype(BF16)
            stores[t].start()

        for cp in loads:
            cp.start()
        to_bf16(0)
        plan.start([0])
        cw_sends = [cw_copy(j, cw_ref, cw_out.at[2 * x + y], chip) for j, chip in enumerate(chips)]
        for cp in cw_sends + stores[nt:]:
            cp.start()
        for t in ts[1:]:
            to_bf16(t)
        plan.relay([0])
        tables(rb_ref, bk_ref, bias_ref, bias_t_ref)
        plan.forward([0])
        for j, chip in enumerate(chips):
            dst = cw_out.at[2 * chip[0] + chip[1]]
            cw_copy(j, dst, dst, chip).wait_recv()
        plan.finish([0])
        for cp in cw_sends:
            cp.wait_send()
        for cp in stores:
            cp.wait()

    out_shape = [jax.ShapeDtypeStruct((N_SHARD,) + sh.shape, BF16) for sh in shards]
    out_shape.append(jax.ShapeDtypeStruct((N_SHARD,) + conv_w_shard.shape, conv_w_shard.dtype))
    out_shape += [jax.ShapeDtypeStruct((N_CFG, B_HEADS, Q_BLOCK, 2 * Q_BLOCK), F32),
                  jax.ShapeDtypeStruct((N_CFG, B_HEADS, 2 * Q_BLOCK, Q_BLOCK), F32)]
    vm = pl.BlockSpec(memory_space=pltpu.VMEM)
    res = pl.pallas_call(
        body, in_specs=[ANY] * (nt + 1) + [pl.BlockSpec(memory_space=pltpu.SMEM), vm],
        out_specs=[ANY] * (nt + 1) + [vm, vm], out_shape=out_shape,
        scratch_shapes=[pltpu.VMEM(sh.shape, F32) for sh in shards] + [pltpu.VMEM(sh.shape, BF16) for sh in shards]
        + [pltpu.SemaphoreType.DMA((nt,)), pltpu.SemaphoreType.DMA((nt + 1,))] + _sem_pair(6) + _sem_pair(3),
        compiler_params=pltpu.CompilerParams(has_side_effects=True, vmem_limit_bytes=VMEM_LIMIT),
        name="gather_weights",
    )(*shards, conv_w_shard, rel_bias.T, jnp.asarray(buckets_np))
    return list(res[:nt + 1]), res[nt + 1], res[nt + 2]


def _turn(t, u, s, last):
    return jnp.where(t == u, s, jnp.where(t > u, last, 0))


def _add_halves(gs, recvs, c_idx):
    n = len(gs)
    dims = [(g.shape[1] // 2, g.shape[2]) for g in gs]

    def body(c_ref, *refs):
        t = pl.program_id(0)
        for u in range(n):
            @pl.when(t == u)
            def _(u=u):
                refs[2 * n + u][...] = (refs[u][...].astype(F32) + refs[n + u][...].astype(F32)).astype(BF16)

    def own(u):
        return pl.BlockSpec((None, None) + dims[u], lambda t, s, c: (_turn(t, u, s, N_SHARD - 1), c[0], 0, 0))

    def plain(u):
        return pl.BlockSpec((None,) + dims[u], lambda t, s, c: (_turn(t, u, s, N_SHARD - 1), 0, 0))

    return pl.pallas_call(
        body,
        grid_spec=pltpu.PrefetchScalarGridSpec(
            num_scalar_prefetch=1, grid=(n, N_SHARD),
            in_specs=[own(u) for u in range(n)] + [plain(u) for u in range(n)],
            out_specs=[plain(u) for u in range(n)]),
        out_shape=[jax.ShapeDtypeStruct((N_SHARD,) + d, BF16) for d in dims],
        compiler_params=_params(("arbitrary", "arbitrary")), name="rs_add_halves",
    )(c_idx, *[g.reshape((N_SHARD, 2) + d) for g, d in zip(gs, dims)], *recvs)


def _add_chips(parts, recvs, s_idx, c_idx):
    n = len(parts)
    dims = [p.shape[1:] for p in parts]

    def body(idx_ref, *refs):
        t = pl.program_id(0)
        for u in range(n):
            @pl.when(t == u)
            def _(u=u):
                acc = refs[u][...].astype(F32)
                for j in range(3):
                    acc = acc + refs[n + u][j].astype(F32)
                refs[2 * n + u][...] = acc

    res = pl.pallas_call(
        body,
        grid_spec=pltpu.PrefetchScalarGridSpec(
            num_scalar_prefetch=1, grid=(n,),
            in_specs=[pl.BlockSpec((None,) + d, lambda t, idx: (idx[0], 0, 0)) for d in dims]
            + [pl.BlockSpec((3,) + d, lambda t, idx: (0, 0, 0)) for d in dims],
            out_specs=[pl.BlockSpec((None,) + d, lambda t, idx: (idx[1], 0, 0)) for d in dims]),
        out_shape=[jax.ShapeDtypeStruct((2,) + d, F32) for d in dims],
        compiler_params=_params(("arbitrary",)), name="rs_add_chips",
    )(jnp.concatenate([s_idx, c_idx]), *parts, *recvs)
    return [r.reshape(2 * d[0], d[1]) for r, d in zip(res, dims)]


def _finish_reductions(fulls, arrays):
    nt, n = len(fulls), len(arrays)

    def body(*refs):
        in_refs = refs[nt:nt + n]
        full_refs, out_refs = refs[nt + n:2 * nt + n], refs[2 * nt + n:2 * nt + 2 * n]
        pos = 2 * nt + 2 * n
        share_send, share_recv = refs[pos + 2 * n:pos + 2 * n + 2]
        allreduce = _SmallAllReducePlan(in_refs, out_refs, refs[pos:pos + n], refs[pos + n:pos + 2 * n],
                                        *refs[pos + 2 * n + 2:])
        x, y, c, _ = _mesh_pos()

        def half(t, which):
            rows = fulls[t].shape[0] // 2
            return full_refs[t].at[pl.ds(which * rows, rows), :]

        def share(t, which):
            return pltpu.make_async_remote_copy(
                src_ref=half(t, which), dst_ref=half(t, which), send_sem=share_send.at[t],
                recv_sem=share_recv.at[t], device_id=(x, y, 1 - c), device_id_type=MESH)

        for t in range(nt):
            share(t, c).start()
        allreduce.start_sibling()
        allreduce.sum_sibling_and_start_chips()
        allreduce.finish()
        for t in range(nt):
            share(t, 1 - c).wait_recv()
        for t in range(nt):
            share(t, c).wait_send()

    vm = pl.BlockSpec(memory_space=pltpu.VMEM)
    res = pl.pallas_call(
        body, in_specs=[ANY] * nt + [vm] * n, out_specs=[ANY] * nt + [vm] * n,
        out_shape=[jax.ShapeDtypeStruct(f.shape, f.dtype) for f in fulls]
        + [jax.ShapeDtypeStruct(a.shape, F32) for a in arrays],
        input_output_aliases={t: t for t in range(nt)},
        scratch_shapes=[pltpu.VMEM(a.shape, F32) for a in arrays] + [pltpu.VMEM((3,) + a.shape, F32) for a in arrays]
        + _sem_pair(nt) + _sem_pair(4 * n),
        compiler_params=pltpu.CompilerParams(has_side_effects=True),
        name="finish_reductions",
    )(*fulls, *arrays)
    return list(res[:nt]), list(res[nt:])


def _from_col_shards(g):
    n, rows, cols = g.shape
    return g.transpose(1, 0, 2).reshape(rows, n * cols)


def _train_step(x, tgt, g1, g2, g3, g4, shards, ln_g, ln_b, w_s, b_s, rel_bias, conv_w_shard, conv_b, batch,
                s_idx, c_idx):
    buckets = _bucket_tables()
    bz = jnp.repeat(b_s.T, HEAD_DIM, axis=1)
    w_st = jnp.swapaxes(w_s, 1, 2)

    def shard_major(g):
        return g.reshape(N_SHARD, g.shape[0] // N_SHARD, D_MODEL)

    names = ["w_in", "w_out", "w_gate", "w_up", "w_down"]
    (g_in, g_out, g_gate, g_up, g_down, g_convw), bias, bias_t = _gather_weights(
        [shards[n] for n in names], conv_w_shard, rel_bias, buckets)
    w_in_t = g_in.reshape(IN_COLS, D_MODEL)
    conv_w = _from_col_shards(g_convw.reshape(N_SHARD, 3, SHARD_FF))

    h1, uv, qkv, a = _proj_fwd(x, g1, w_in_t, ln_g, ln_b, w_s, bz)
    o_bf, lse, (g_out, g_gate, g_up) = _attn_fwd(qkv, bias, batch, [g_out, g_gate, g_up])
    w_out = g_out.reshape(D_MODEL, D_MODEL)
    w_gate_t = g_gate.reshape(D_FF, D_MODEL)
    w_up_t = g_up.reshape(D_FF, D_MODEL)
    (y1, x1, h2), _ = _fused_rows(
        "out_proj_mid_fwd", 512,
        [(a, w_out, "nn", slice(0, A_WIDTH)), (o_bf, w_out, "nn", slice(A_WIDTH, D_MODEL))],
        [x], [g2, g3], _mid_fwd_rows, [F32, F32, BF16], [])
    gate, up, g_down = _mm_pair_nt(h2, w_gate_t, w_up_t, g_down, tm=1024, tn=1408, out_dtype=BF16,
                                   name="mm_gate_up")
    w_down = g_down.reshape(D_FF, D_MODEL)
    act = _convgate_fwd(gate, up, conv_w, conv_b, batch)
    (dx2, dy2, dg4, loss), _ = _fused_rows(
        "down_proj_loss_head", 512, [(act, w_down, "nn", None)], [x1, tgt], [g4], _loss_head_rows,
        [F32, BF16], [(1, D_MODEL), (1, 128)])

    dact = _mm(dy2, w_down, dims="nt", tm=1024, tn=1408, tk=1024, out_dtype=BF16, name="mm_dact")
    dw_down = _mm(act, dy2, dims="tn", tm=1408, tn=1024, tk=1024, out_dtype=BF16, name="mm_dw_down")
    dgate, dup, dconv_w, dconv_b = _convgate_bwd(gate, up, dact, conv_w, conv_b, batch)
    (dx1, dy1, dg2, dg3), _ = _fused_rows(
        "dh2_mid_bwd", 256, [(dgate, w_gate_t, "nn", None), (dup, w_up_t, "nn", None)],
        [x1, y1, dx2], [g2, g3], _mid_bwd_rows, [F32, BF16], [(1, D_MODEL), (1, D_MODEL)])
    dw_gate_t, dw_up_t = _mm_pair_tn(dgate, dup, h2, tm=1408, tk=1024, name="mm_dw_gate_up")
    done = [shard_major(g) for g in (dw_down, dw_gate_t, dw_up_t)]
    (dmix, dw_out), recv_a = _out_proj_bwd(a, o_bf, dy1, w_out, done[:2])
    done.append(shard_major(dw_out))
    (dproj, dln_g, dln_b, dw_s, dbz), recv_b = _gate_bwd(uv, dmix, ln_g, ln_b, w_s, w_st, bz, done[2:])
    recv_a += recv_b
    parts = _add_halves(done, recv_a, c_idx)
    early = dict(loss=loss, norm_mix_post=dg2, norm_ffn_pre=dg3, norm_ffn_post=dg4, ln_v_gain=dln_g,
                 ln_v_bias=dln_b, spatial_w=dw_s, spatial_b=dbz, conv_w=dconv_w, conv_b=dconv_b)
    dproj, ds, recv, early_sums = _attn_bwd(qkv, dmix, o_bf, lse, bias_t, dproj, batch, parts, list(early.values()))
    fulls = _add_chips(parts, recv, s_idx, c_idx)
    dw_in_t = _mm(dproj, h1, dims="tn", tm=1408, tn=1024, tk=1024, out_dtype=BF16, name="mm_dw_in")
    last = [shard_major(dw_in_t)]
    drel, recv_in_a = _rel_bias_grad(ds, np.ascontiguousarray(np.swapaxes(buckets, 1, 2)), last)
    part_in = _add_halves(last, recv_in_a, c_idx)
    (dx0, dg1), recv_in = _fused_rows(
        "dh1_in_bwd", 512, [(dproj, w_in_t, "nn", None)], [x, dx1], [g1], _in_bwd_rows,
        [F32], [(1, D_MODEL)], exchange=part_in)
    fulls += _add_chips(part_in, recv_in, s_idx, c_idx)
    half_reduced = dict(zip(["w_down", "w_gate", "w_up", "w_out", "w_in"], fulls))

    return dx0, dict(zip(early, early_sums)), dict(norm_mix_pre=dg1, rel_bias=drel), half_reduced


def _adamw_update(w, g, m, v):
    nm = ADAM_B1 * m + (1.0 - ADAM_B1) * g
    nv = ADAM_B2 * v + (1.0 - ADAM_B2) * (g * g)
    m_hat = nm / (1.0 - ADAM_B1 ** ADAM_STEP)
    v_hat = nv / (1.0 - ADAM_B2 ** ADAM_STEP)
    return -ADAM_LR * (m_hat / (jnp.sqrt(v_hat) + ADAM_EPS) + ADAM_WD * w), nm, nv


def _adamw(w, g, m, v, name):
    rows, cols = w.shape
    tr = next(cand for cand in (352, 256, 128) if rows % cand == 0)

    def body(w_ref, g_ref, m_ref, v_ref, go_ref, d_ref, nm_ref, nv_ref):
        gv = g_ref[...]
        go_ref[...] = gv
        d_ref[...], nm_ref[...], nv_ref[...] = _adamw_update(w_ref[...], gv, m_ref[...], v_ref[...])

    spec = pl.BlockSpec((tr, cols), lambda i: (i, 0))
    sds = jax.ShapeDtypeStruct((rows, cols), F32)
    return pl.pallas_call(
        body, grid=(rows // tr,), in_specs=[spec] * 4, out_specs=[spec] * 4, out_shape=[sds] * 4,
        compiler_params=_params(("parallel",)), name=name,
    )(w, g, m, v)


def _adamw_small(ws, gs, ms, vs):
    n = len(ws)

    def body(*refs):
        w_refs, g_refs, m_refs, v_refs = refs[:n], refs[n:2 * n], refs[2 * n:3 * n], refs[3 * n:4 * n]
        d_refs, nm_refs, nv_refs, go_refs = refs[4 * n:5 * n], refs[5 * n:6 * n], refs[6 * n:7 * n], refs[7 * n:]
        for t in range(n):
            g = g_refs[t][...]
            d_refs[t][...], nm_refs[t][...], nv_refs[t][...] = _adamw_update(
                w_refs[t][...], g, m_refs[t][...], v_refs[t][...])
            go_refs[t][...] = g

    vm = pl.BlockSpec(memory_space=pltpu.VMEM)
    sds = [jax.ShapeDtypeStruct(w.shape, F32) for w in ws]
    res = pl.pallas_call(
        body, in_specs=[vm] * (4 * n), out_specs=[vm] * (4 * n), out_shape=sds * 4,
        compiler_params=_params(), name="adamw_small",
    )(*ws, *gs, *ms, *vs)
    return res[:n], res[n:2 * n], res[2 * n:3 * n], res[3 * n:]


SMALL = ["norm_mix_pre", "norm_mix_post", "norm_ffn_pre", "norm_ffn_post", "ln_v_gain", "ln_v_bias",
         "spatial_w", "spatial_b", "rel_bias", "conv_b"]
LARGE = ["w_in", "w_gate", "w_up", "w_down", "w_out"]
TRANSPOSED = ("w_in", "w_gate", "w_up")
ORDER = ["norm_mix_pre", "norm_mix_post", "norm_ffn_pre", "norm_ffn_post", "w_in", "ln_v_gain", "ln_v_bias",
         "spatial_w", "spatial_b", "rel_bias", "w_out", "w_gate", "w_up", "conv_w", "conv_b", "w_down"]


def kernel(x, norm_mix_pre, norm_mix_post, norm_ffn_pre, norm_ffn_post, w_in, ln_v_gain, ln_v_bias, spatial_w, spatial_b, rel_bias, w_out, w_gate, w_up, conv_w, conv_b, w_down, loss_target, m_norm_mix_pre, m_norm_mix_post, m_norm_ffn_pre, m_norm_ffn_post, m_w_in, m_ln_v_gain, m_ln_v_bias, m_spatial_w, m_spatial_b, m_rel_bias, m_w_out, m_w_gate, m_w_up, m_conv_w, m_conv_b, m_w_down, v_norm_mix_pre, v_norm_mix_post, v_norm_ffn_pre, v_norm_ffn_post, v_w_in, v_ln_v_gain, v_ln_v_bias, v_spatial_w, v_spatial_b, v_rel_bias, v_w_out, v_w_gate, v_w_up, v_conv_w, v_conv_b, v_w_down):
    params = dict(norm_mix_pre=norm_mix_pre, norm_mix_post=norm_mix_post, norm_ffn_pre=norm_ffn_pre,
                  norm_ffn_post=norm_ffn_post, w_in=w_in, ln_v_gain=ln_v_gain, ln_v_bias=ln_v_bias,
                  spatial_w=spatial_w, spatial_b=spatial_b, rel_bias=rel_bias, w_out=w_out, w_gate=w_gate,
                  w_up=w_up, conv_w=conv_w, conv_b=conv_b, w_down=w_down)
    mom = dict(norm_mix_pre=m_norm_mix_pre, norm_mix_post=m_norm_mix_post, norm_ffn_pre=m_norm_ffn_pre,
               norm_ffn_post=m_norm_ffn_post, w_in=m_w_in, ln_v_gain=m_ln_v_gain, ln_v_bias=m_ln_v_bias,
               spatial_w=m_spatial_w, spatial_b=m_spatial_b, rel_bias=m_rel_bias, w_out=m_w_out, w_gate=m_w_gate,
               w_up=m_w_up, conv_w=m_conv_w, conv_b=m_conv_b, w_down=m_w_down)
    var = dict(norm_mix_pre=v_norm_mix_pre, norm_mix_post=v_norm_mix_post, norm_ffn_pre=v_norm_ffn_pre,
               norm_ffn_post=v_norm_ffn_post, w_in=v_w_in, ln_v_gain=v_ln_v_gain, ln_v_bias=v_ln_v_bias,
               spatial_w=v_spatial_w, spatial_b=v_spatial_b, rel_bias=v_rel_bias, w_out=v_w_out, w_gate=v_w_gate,
               w_up=v_w_up, conv_w=v_conv_w, conv_b=v_conv_b, w_down=v_w_down)

    batch = x.shape[0]
    xi, yi, ci = lax.axis_index("x"), lax.axis_index("y"), lax.axis_index("c")
    s_idx = (2 * xi + yi).astype(jnp.int32).reshape(1)
    c_idx = ci.astype(jnp.int32).reshape(1)

    def local(a, n):
        return jnp.swapaxes(a[0], 0, 1) if n in TRANSPOSED else a[0]

    shards = {n: local(params[n], n) for n in LARGE}
    dx0, total, partial, half_reduced = _train_step(
        x.reshape(batch * SEQ, D_MODEL), loss_target.reshape(batch * SEQ, D_MODEL),
        norm_mix_pre, norm_mix_post, norm_ffn_pre, norm_ffn_post, shards,
        ln_v_gain.reshape(1, A_WIDTH), ln_v_bias.reshape(1, A_WIDTH), spatial_w[0], spatial_b[0], rel_bias,
        jnp.swapaxes(conv_w, 0, 1), conv_b, batch, s_idx, c_idx)
    grad_x = dx0.reshape(batch, SEQ, D_MODEL)

    names = list(partial)
    fulls, sums = _finish_reductions([half_reduced[n] for n in LARGE], [partial[n] for n in names])
    reduced = dict(zip(LARGE, fulls))
    total.update(zip(names, sums))
    loss = total["loss"][0, 0]
    total["spatial_b"] = total["spatial_b"][:, ::HEAD_DIM].T
    total["rel_bias"] = total["rel_bias"].reshape(B_HEADS, NUM_BUCKETS)
    total["conv_w"] = lax.dynamic_slice_in_dim(total["conv_w"], s_idx[0] * SHARD_FF, SHARD_FF, axis=1)
    small_names = SMALL + ["conv_w"]

    def small(a, n):
        return jnp.swapaxes(a, 0, 1) if n in ("rel_bias", "conv_w") else a

    for n in small_names:
        reduced[n] = total[n].reshape(small(params[n], n).shape)

    out_g, out_d, out_m, out_v = {}, {}, {}, {}
    for n in LARGE:
        res = _adamw(local(params[n], n), reduced[n], local(mom[n], n), local(var[n], n), name=f"adamw_{n}")
        if n in TRANSPOSED:
            res = [jnp.swapaxes(r, 0, 1) for r in res]
        out_g[n], out_d[n], out_m[n], out_v[n] = [r[None] for r in res]
    d, nm, nv, gg = _adamw_small([small(params[n], n) for n in small_names], [reduced[n] for n in small_names],
                                 [small(mom[n], n) for n in small_names], [small(var[n], n) for n in small_names])
    for n, g, dd, mm, vv in zip(small_names, gg, d, nm, nv):
        out_g[n], out_d[n], out_m[n], out_v[n] = [small(r, n) for r in (g, dd, mm, vv)]

    return (loss, grad_x, *[out_g[n] for n in ORDER], *[out_d[n] for n in ORDER],
            *[out_m[n] for n in ORDER], *[out_v[n] for n in ORDER])
```
